```python
import jax, jax.numpy as jnp
from jax import lax
import numpy as np

D_MODEL = 1024
BATCH = 8
SEQ = 4096
DEPTH = 1

N_MEM = 256
HEAD_DIM = 64
ATTN_WIDTH = D_MODEL // 2
CONV_WIDTH = D_MODEL // 4
XATTN_WIDTH = D_MODEL // 4
N_ATTN_HEADS = ATTN_WIDTH // HEAD_DIM
N_XATTN_HEADS = 4
XATTN_HEAD_DIM = XATTN_WIDTH // N_XATTN_HEADS
MIX_WIDTH = ATTN_WIDTH + CONV_WIDTH + XATTN_WIDTH
IN_PROJ_WIDTH = 3 * ATTN_WIDTH + 3 * CONV_WIDTH + XATTN_WIDTH
DILATED_PATTERNS = ((128, 1), (512, 4), (2048, 16))
CONV_K = 3
D_FF = 4 * D_MODEL
ROPE_THETA = 10000.0
EPS = 1e-6
NEG_INF = -1e30

kernel_name = "hybrid_dilated_attn_shortconv_memxattn_block"


def rms_norm(x, g):
    xf = x.astype(jnp.float32)
    y = xf * lax.rsqrt(jnp.mean(xf * xf, axis=-1, keepdims=True) + EPS)
    return (y * g.astype(jnp.float32)).astype(x.dtype)


def apply_rope(t, positions):
    dh = t.shape[-1]
    half = dh // 2
    inv_freq = jnp.float32(ROPE_THETA) ** (-(jnp.arange(half, dtype=jnp.float32) * 2.0 / dh))
    ang = positions.astype(jnp.float32)[..., None] * inv_freq
    cos = jnp.cos(ang)[:, :, None, :]
    sin = jnp.sin(ang)[:, :, None, :]
    tf = t.astype(jnp.float32)
    t1, t2 = tf[..., :half], tf[..., half:]
    out = jnp.concatenate([t1 * cos - t2 * sin, t1 * sin + t2 * cos], axis=-1)
    return out.astype(t.dtype)


def dilated_window_attention(q, k, v, window, dilation):
    B, S, H, Dh = q.shape
    L = S // dilation
    n_back = window // dilation
    blk = n_back
    nb = -(-L // blk)
    Lp = nb * blk

    def to_blocks(t):
        t = t.reshape(B, L, dilation, H, Dh).transpose(0, 2, 1, 3, 4)
        t = jnp.pad(t, ((0, 0), (0, 0), (0, Lp - L), (0, 0), (0, 0)))
        return t.reshape(B, dilation, nb, blk, H, Dh)

    qb, kb, vb = to_blocks(q), to_blocks(k), to_blocks(v)

    def with_prev(t):
        prev = jnp.pad(t, ((0, 0), (0, 0), (1, 0), (0, 0), (0, 0), (0, 0)))[:, :, :-1]
        return jnp.concatenate([prev, t], axis=3)

    kw, vw = with_prev(kb), with_prev(vb)
    scale = Dh ** -0.5
    s = jnp.einsum('bdnqhc,bdnkhc->bdnhqk', qb, kw,
                   preferred_element_type=jnp.float32) * scale
    qi = jnp.arange(blk)[:, None]
    kj = jnp.arange(2 * blk)[None, :]
    band = (kj >= qi) & (kj <= qi + n_back)
    valid = band[None] & ((jnp.arange(nb)[:, None, None] > 0) | (kj[None] >= blk))
    s = jnp.where(valid[None, None, :, None], s, NEG_INF)
    lse = jax.nn.logsumexp(s, axis=-1)
    p = jnp.exp(s - lse[..., None])
    o = jnp.einsum('bdnhqk,bdnkhc->bdnqhc', p.astype(v.dtype), vw,
                   preferred_element_type=jnp.float32)
    o = o.reshape(B, dilation, Lp, H, Dh)[:, :, :L]
    o = o.transpose(0, 2, 1, 3, 4).reshape(B, S, H, Dh)
    lse = lse.transpose(0, 1, 2, 4, 3).reshape(B, dilation, Lp, H)[:, :, :L]
    lse = lse.transpose(0, 2, 1, 3).reshape(B, S, H)
    return o, lse


def dilated_mixture_attention(q, k, v):
    outs, lses = [], []
    for window, dilation in DILATED_PATTERNS:
        o, lse = dilated_window_attention(q, k, v, window, dilation)
        outs.append(o)
        lses.append(lse)
    w = jax.nn.softmax(jnp.stack(lses, axis=0), axis=0)
    o = jnp.sum(w[..., None] * jnp.stack(outs, axis=0), axis=0)
    return o.astype(q.dtype)


def short_gated_conv(b_gate, c_gate, u, conv_w):
    z = c_gate * u
    S = z.shape[1]
    zp = jnp.pad(z, ((0, 0), (CONV_K - 1, 0), (0, 0)))
    y = zp[:, 0:S] * conv_w[0]
    for tap in range(1, CONV_K):
        y = y + zp[:, tap:tap + S] * conv_w[tap]
    return b_gate * y


def memory_cross_attention(qx, mem_kv):
    B, S, _ = qx.shape
    q = qx.reshape(B, S, N_XATTN_HEADS, XATTN_HEAD_DIM)
    km, vm = jnp.split(mem_kv, 2, axis=-1)
    km = km.reshape(B, -1, N_XATTN_HEADS, XATTN_HEAD_DIM)
    vm = vm.reshape(B, -1, N_XATTN_HEADS, XATTN_HEAD_DIM)
    s = jnp.einsum('bshc,bmhc->bhsm', q, km,
                   preferred_element_type=jnp.float32) * (XATTN_HEAD_DIM ** -0.5)
    p = jax.nn.softmax(s, axis=-1)
    o = jnp.einsum('bhsm,bmhc->bshc', p.astype(vm.dtype), vm)
    return o.reshape(B, S, XATTN_WIDTH)


def _fwd_setup_inputs(seed: int = 0) -> dict:
    key = jax.random.key(seed)
    ks = jax.random.split(key, 20)
    f32 = jnp.float32

    def w(k, shape, fan_in):
        return jax.random.normal(k, shape, f32) * (fan_in ** -0.5)

    def gain(k, width):
        return 1.0 + 0.05 * jax.random.normal(k, (DEPTH, width), f32)

    x = jax.random.normal(ks[0], (BATCH, SEQ, D_MODEL), f32)
    mem = jax.random.normal(ks[1], (BATCH, N_MEM, D_MODEL), f32)
    offset = jax.random.randint(ks[2], (BATCH, 1), 0, 1024, dtype=jnp.int32)
    positions = offset + jnp.arange(SEQ, dtype=jnp.int32)[None, :]
    return {
        "x": x,
        "mem": mem,
        "positions": positions,
        "g_pre_mix": gain(ks[3], D_MODEL),
        "g_mem": gain(ks[4], D_MODEL),
        "w_in": w(ks[5], (DEPTH, D_MODEL, IN_PROJ_WIDTH), D_MODEL),
        "w_mem_kv": w(ks[6], (DEPTH, D_MODEL, 2 * XATTN_WIDTH), D_MODEL),
        "conv_w": w(ks[7], (DEPTH, CONV_K, CONV_WIDTH), CONV_K),
        "g_attn_out": gain(ks[8], ATTN_WIDTH),
        "g_conv_out": gain(ks[9], CONV_WIDTH),
        "g_xattn_out": gain(ks[10], XATTN_WIDTH),
        "w_out": w(ks[11], (DEPTH, MIX_WIDTH, D_MODEL), MIX_WIDTH),
        "g_post_mix": gain(ks[12], D_MODEL),
        "g_pre_mlp": gain(ks[13], D_MODEL),
        "w_up": w(ks[14], (DEPTH, D_MODEL, D_FF), D_MODEL),
        "w_down": w(ks[15], (DEPTH, D_FF, D_MODEL), D_FF),
        "g_post_mlp": gain(ks[16], D_MODEL),
    }


def _fwd_reference(x, mem, positions, g_pre_mix, g_mem, w_in, w_mem_kv, conv_w,
              g_attn_out, g_conv_out, g_xattn_out, w_out, g_post_mix,
              g_pre_mlp, w_up, w_down, g_post_mlp):
    B, S, _ = x.shape
    a0 = ATTN_WIDTH
    c0 = 3 * ATTN_WIDTH
    x0 = 3 * ATTN_WIDTH + 3 * CONV_WIDTH
    for l in range(DEPTH):
        h = rms_norm(x, g_pre_mix[l])
        proj = jnp.einsum('bsd,de->bse', h, w_in[l])

        q = proj[..., 0:a0].reshape(B, S, N_ATTN_HEADS, HEAD_DIM)
        k = proj[..., a0:2 * a0].reshape(B, S, N_ATTN_HEADS, HEAD_DIM)
        v = proj[..., 2 * a0:3 * a0].reshape(B, S, N_ATTN_HEADS, HEAD_DIM)
        q = apply_rope(q, positions)
        k = apply_rope(k, positions)
        y_attn = dilated_mixture_attention(q, k, v).reshape(B, S, ATTN_WIDTH)

        b_gate = proj[..., c0:c0 + CONV_WIDTH]
        c_gate = proj[..., c0 + CONV_WIDTH:c0 + 2 * CONV_WIDTH]
        u = proj[..., c0 + 2 * CONV_WIDTH:c0 + 3 * CONV_WIDTH]
        y_conv = short_gated_conv(b_gate, c_gate, u, conv_w[l])

        qx = proj[..., x0:x0 + XATTN_WIDTH]
        mem_kv = jnp.einsum('bmd,de->bme', rms_norm(mem, g_mem[l]), w_mem_kv[l])
        y_x = memory_cross_attention(qx, mem_kv)

        y = jnp.concatenate([rms_norm(y_attn, g_attn_out[l]),
                             rms_norm(y_conv, g_conv_out[l]),
                             rms_norm(y_x, g_xattn_out[l])], axis=-1)
        y = jnp.einsum('bse,ed->bsd', y, w_out[l])
        x = x + rms_norm(y, g_post_mix[l])

        h2 = rms_norm(x, g_pre_mlp[l])
        f = jnp.square(jax.nn.relu(jnp.einsum('bsd,df->bsf', h2, w_up[l])))
        f = jnp.einsum('bsf,fd->bsd', f, w_down[l])
        x = x + rms_norm(f, g_post_mlp[l])
    return x


import jax as _jax
import jax.numpy as _jnp

TWIN_FORMAT = 'train_step'
FWD_PARAMS = ['x', 'mem', 'positions', 'g_pre_mix', 'g_mem', 'w_in', 'w_mem_kv', 'conv_w', 'g_attn_out', 'g_conv_out', 'g_xattn_out', 'w_out', 'g_post_mix', 'g_pre_mlp', 'w_up', 'w_down', 'g_post_mlp']
TWIN_WEIGHTS = ['g_pre_mix', 'g_mem', 'w_in', 'w_mem_kv', 'conv_w', 'g_attn_out', 'g_conv_out', 'g_xattn_out', 'w_out', 'g_post_mix', 'g_pre_mlp', 'w_up', 'w_down', 'g_post_mlp']
TWIN_DIFF_INPUT = 'x'
TWIN_INPUTS = ['x', 'mem', 'positions', 'g_pre_mix', 'g_mem', 'w_in', 'w_mem_kv', 'conv_w', 'g_attn_out', 'g_conv_out', 'g_xattn_out', 'w_out', 'g_post_mix', 'g_pre_mlp', 'w_up', 'w_down', 'g_post_mlp', 'loss_target', 'm_g_pre_mix', 'm_g_mem', 'm_w_in', 'm_w_mem_kv', 'm_conv_w', 'm_g_attn_out', 'm_g_conv_out', 'm_g_xattn_out', 'm_w_out', 'm_g_post_mix', 'm_g_pre_mlp', 'm_w_up', 'm_w_down', 'm_g_post_mlp', 'v_g_pre_mix', 'v_g_mem', 'v_w_in', 'v_w_mem_kv', 'v_conv_w', 'v_g_attn_out', 'v_g_conv_out', 'v_g_xattn_out', 'v_w_out', 'v_g_post_mix', 'v_g_pre_mlp', 'v_w_up', 'v_w_down', 'v_g_post_mlp']
TWIN_OUTPUTS = ['loss', 'grad_x', 'grad_g_pre_mix', 'grad_g_mem', 'grad_w_in', 'grad_w_mem_kv', 'grad_conv_w', 'grad_g_attn_out', 'grad_g_conv_out', 'grad_g_xattn_out', 'grad_w_out', 'grad_g_post_mix', 'grad_g_pre_mlp', 'grad_w_up', 'grad_w_down', 'grad_g_post_mlp', 'delta_g_pre_mix', 'delta_g_mem', 'delta_w_in', 'delta_w_mem_kv', 'delta_conv_w', 'delta_g_attn_out', 'delta_g_conv_out', 'delta_g_xattn_out', 'delta_w_out', 'delta_g_post_mix', 'delta_g_pre_mlp', 'delta_w_up', 'delta_w_down', 'delta_g_post_mlp', 'new_m_g_pre_mix', 'new_m_g_mem', 'new_m_w_in', 'new_m_w_mem_kv', 'new_m_conv_w', 'new_m_g_attn_out', 'new_m_g_conv_out', 'new_m_g_xattn_out', 'new_m_w_out', 'new_m_g_post_mix', 'new_m_g_pre_mlp', 'new_m_w_up', 'new_m_w_down', 'new_m_g_post_mlp', 'new_v_g_pre_mix', 'new_v_g_mem', 'new_v_w_in', 'new_v_w_mem_kv', 'new_v_conv_w', 'new_v_g_attn_out', 'new_v_g_conv_out', 'new_v_g_xattn_out', 'new_v_w_out', 'new_v_g_post_mix', 'new_v_g_pre_mlp', 'new_v_w_up', 'new_v_w_down', 'new_v_g_post_mlp']
TWIN_LEAF_KINDS = {'loss': 'loss', 'grad_x': 'grad_x', 'grad_g_pre_mix': 'grad_w', 'grad_g_mem': 'grad_w', 'grad_w_in': 'grad_w', 'grad_w_mem_kv': 'grad_w', 'grad_conv_w': 'grad_w', 'grad_g_attn_out': 'grad_w', 'grad_g_conv_out': 'grad_w', 'grad_g_xattn_out': 'grad_w', 'grad_w_out': 'grad_w', 'grad_g_post_mix': 'grad_w', 'grad_g_pre_mlp': 'grad_w', 'grad_w_up': 'grad_w', 'grad_w_down': 'grad_w', 'grad_g_post_mlp': 'grad_w', 'delta_g_pre_mix': 'delta_w', 'delta_g_mem': 'delta_w', 'delta_w_in': 'delta_w', 'delta_w_mem_kv': 'delta_w', 'delta_conv_w': 'delta_w', 'delta_g_attn_out': 'delta_w', 'delta_g_conv_out': 'delta_w', 'delta_g_xattn_out': 'delta_w', 'delta_w_out': 'delta_w', 'delta_g_post_mix': 'delta_w', 'delta_g_pre_mlp': 'delta_w', 'delta_w_up': 'delta_w', 'delta_w_down': 'delta_w', 'delta_g_post_mlp': 'delta_w', 'new_m_g_pre_mix': 'new_m', 'new_m_g_mem': 'new_m', 'new_m_w_in': 'new_m', 'new_m_w_mem_kv': 'new_m', 'new_m_conv_w': 'new_m', 'new_m_g_attn_out': 'new_m', 'new_m_g_conv_out': 'new_m', 'new_m_g_xattn_out': 'new_m', 'new_m_w_out': 'new_m', 'new_m_g_post_mix': 'new_m', 'new_m_g_pre_mlp': 'new_m', 'new_m_w_up': 'new_m', 'new_m_w_down': 'new_m', 'new_m_g_post_mlp': 'new_m', 'new_v_g_pre_mix': 'new_v', 'new_v_g_mem': 'new_v', 'new_v_w_in': 'new_v', 'new_v_w_mem_kv': 'new_v', 'new_v_conv_w': 'new_v', 'new_v_g_attn_out': 'new_v', 'new_v_g_conv_out': 'new_v', 'new_v_g_xattn_out': 'new_v', 'new_v_w_out': 'new_v', 'new_v_g_post_mix': 'new_v', 'new_v_g_pre_mlp': 'new_v', 'new_v_w_up': 'new_v', 'new_v_w_down': 'new_v', 'new_v_g_post_mlp': 'new_v'}


def _forward(args):
    return _fwd_reference(*[args[k] for k in FWD_PARAMS])


def _output_shape():
    def fwd():
        inp = _fwd_setup_inputs(0)
        return _fwd_reference(*[inp[k] for k in FWD_PARAMS])
    out = _jax.eval_shape(fwd)
    return out.shape, out.dtype

N_MICROBATCH = 1
ADAM_LR = 0.001
ADAM_B1 = 0.9
ADAM_B2 = 0.999
ADAM_EPS = 1e-08
ADAM_WD = 0.01
ADAM_STEP = 10
PER_EXAMPLE_BATCH_AXIS = {'x': 0, 'mem': 0, 'positions': 0, 'loss_target': 0}
SHARED_INPUTS = []
_WEIGHT_DTYPES = {'g_pre_mix': _jnp.float32, 'g_mem': _jnp.float32, 'w_in': _jnp.float32, 'w_mem_kv': _jnp.float32, 'conv_w': _jnp.float32, 'g_attn_out': _jnp.float32, 'g_conv_out': _jnp.float32, 'g_xattn_out': _jnp.float32, 'w_out': _jnp.float32, 'g_post_mix': _jnp.float32, 'g_pre_mlp': _jnp.float32, 'w_up': _jnp.float32, 'w_down': _jnp.float32, 'g_post_mlp': _jnp.float32}
MOMENT_SCALE = {'g_pre_mix': 9.518783e-01, 'g_mem': 1.599816e+00, 'w_in': 5.788843e-01, 'w_mem_kv': 2.182616e+00, 'conv_w': 5.367505e-01, 'g_attn_out': 1.355594e+00, 'g_conv_out': 1.033780e+00, 'g_xattn_out': 3.168966e+00, 'w_out': 1.658820e+00, 'g_post_mix': 3.219711e+01, 'g_pre_mlp': 1.048534e+00, 'w_up': 5.374061e-01, 'w_down': 1.793160e+00, 'g_post_mlp': 3.307655e+01}


def _to_microbatches(a, axis):
    t = _jnp.moveaxis(a, axis, 0)
    t = t.reshape((N_MICROBATCH, t.shape[0] // N_MICROBATCH) + t.shape[1:])
    return _jnp.moveaxis(t, 1, axis + 1)


def setup_inputs(seed: int = 0) -> dict:
    inp = _fwd_setup_inputs(seed)
    key = _jax.random.fold_in(_jax.random.key(seed), 7919)
    shape, _ = _output_shape()
    out = dict(inp)
    out["loss_target"] = _jax.random.normal(_jax.random.fold_in(key, 0), shape, _jnp.float32)
    for i, name in enumerate(TWIN_WEIGHTS):
        w = inp[name].astype(_jnp.float32)
        if MOMENT_SCALE is None:
            s = _jnp.sqrt(_jnp.mean(_jnp.square(w)) + 1e-30)
        else:
            s = MOMENT_SCALE[name]
        km, kv = _jax.random.split(_jax.random.fold_in(key, i + 1))
        out[name] = w
        out["m_" + name] = s * _jax.random.normal(km, w.shape, _jnp.float32)
        out["v_" + name] = (s * s) * _jax.random.uniform(kv, w.shape, _jnp.float32, 0.5, 1.5)
    if N_MICROBATCH > 1:
        for name, axis in PER_EXAMPLE_BATCH_AXIS.items():
            out[name] = _to_microbatches(out[name], axis)
    return {'x': out['x'], 'mem': out['mem'], 'positions': out['positions'], 'g_pre_mix': out['g_pre_mix'], 'g_mem': out['g_mem'], 'w_in': out['w_in'], 'w_mem_kv': out['w_mem_kv'], 'conv_w': out['conv_w'], 'g_attn_out': out['g_attn_out'], 'g_conv_out': out['g_conv_out'], 'g_xattn_out': out['g_xattn_out'], 'w_out': out['w_out'], 'g_post_mix': out['g_post_mix'], 'g_pre_mlp': out['g_pre_mlp'], 'w_up': out['w_up'], 'w_down': out['w_down'], 'g_post_mlp': out['g_post_mlp'], 'loss_target': out['loss_target'], 'm_g_pre_mix': out['m_g_pre_mix'], 'm_g_mem': out['m_g_mem'], 'm_w_in': out['m_w_in'], 'm_w_mem_kv': out['m_w_mem_kv'], 'm_conv_w': out['m_conv_w'], 'm_g_attn_out': out['m_g_attn_out'], 'm_g_conv_out': out['m_g_conv_out'], 'm_g_xattn_out': out['m_g_xattn_out'], 'm_w_out': out['m_w_out'], 'm_g_post_mix': out['m_g_post_mix'], 'm_g_pre_mlp': out['m_g_pre_mlp'], 'm_w_up': out['m_w_up'], 'm_w_down': out['m_w_down'], 'm_g_post_mlp': out['m_g_post_mlp'], 'v_g_pre_mix': out['v_g_pre_mix'], 'v_g_mem': out['v_g_mem'], 'v_w_in': out['v_w_in'], 'v_w_mem_kv': out['v_w_mem_kv'], 'v_conv_w': out['v_conv_w'], 'v_g_attn_out': out['v_g_attn_out'], 'v_g_conv_out': out['v_g_conv_out'], 'v_g_xattn_out': out['v_g_xattn_out'], 'v_w_out': out['v_w_out'], 'v_g_post_mix': out['v_g_post_mix'], 'v_g_pre_mlp': out['v_g_pre_mlp'], 'v_w_up': out['v_w_up'], 'v_w_down': out['v_w_down'], 'v_g_post_mlp': out['v_g_post_mlp']}


def _loss(weights, diff, rest, loss_target):
    with _jax.named_scope("forward"):
        args = {**rest, TWIN_DIFF_INPUT: diff, **{k: w.astype(_WEIGHT_DTYPES[k]) for k, w in weights.items()}}
        y = _forward(args)
    with _jax.named_scope("loss_head"):
        err = _jnp.square(y.astype(_jnp.float32) - loss_target)
        return 0.5 * _jnp.sum(_jnp.mean(err, axis=-1)) if err.ndim else 0.5 * err


def _adamw(w, g, m, v):
    m = ADAM_B1 * m + (1.0 - ADAM_B1) * g
    v = ADAM_B2 * v + (1.0 - ADAM_B2) * _jnp.square(g)
    m_hat = m / (1.0 - ADAM_B1 ** ADAM_STEP)
    v_hat = v / (1.0 - ADAM_B2 ** ADAM_STEP)
    delta = -ADAM_LR * (m_hat / (_jnp.sqrt(v_hat) + ADAM_EPS) + ADAM_WD * w)
    return delta, m, v


def reference(x, mem, positions, g_pre_mix, g_mem, w_in, w_mem_kv, conv_w, g_attn_out, g_conv_out, g_xattn_out, w_out, g_post_mix, g_pre_mlp, w_up, w_down, g_post_mlp, loss_target, m_g_pre_mix, m_g_mem, m_w_in, m_w_mem_kv, m_conv_w, m_g_attn_out, m_g_conv_out, m_g_xattn_out, m_w_out, m_g_post_mix, m_g_pre_mlp, m_w_up, m_w_down, m_g_post_mlp, v_g_pre_mix, v_g_mem, v_w_in, v_w_mem_kv, v_conv_w, v_g_attn_out, v_g_conv_out, v_g_xattn_out, v_w_out, v_g_post_mix, v_g_pre_mlp, v_w_up, v_w_down, v_g_post_mlp):
    given = dict(x=x, mem=mem, positions=positions, g_pre_mix=g_pre_mix, g_mem=g_mem, w_in=w_in, w_mem_kv=w_mem_kv, conv_w=conv_w, g_attn_out=g_attn_out, g_conv_out=g_conv_out, g_xattn_out=g_xattn_out, w_out=w_out, g_post_mix=g_post_mix, g_pre_mlp=g_pre_mlp, w_up=w_up, w_down=w_down, g_post_mlp=g_post_mlp, loss_target=loss_target, m_g_pre_mix=m_g_pre_mix, m_g_mem=m_g_mem, m_w_in=m_w_in, m_w_mem_kv=m_w_mem_kv, m_conv_w=m_conv_w, m_g_attn_out=m_g_attn_out, m_g_conv_out=m_g_conv_out, m_g_xattn_out=m_g_xattn_out, m_w_out=m_w_out, m_g_post_mix=m_g_post_mix, m_g_pre_mlp=m_g_pre_mlp, m_w_up=m_w_up, m_w_down=m_w_down, m_g_post_mlp=m_g_post_mlp, v_g_pre_mix=v_g_pre_mix, v_g_mem=v_g_mem, v_w_in=v_w_in, v_w_mem_kv=v_w_mem_kv, v_conv_w=v_conv_w, v_g_attn_out=v_g_attn_out, v_g_conv_out=v_g_conv_out, v_g_xattn_out=v_g_xattn_out, v_w_out=v_w_out, v_g_post_mix=v_g_post_mix, v_g_pre_mlp=v_g_pre_mlp, v_w_up=v_w_up, v_w_down=v_w_down, v_g_post_mlp=v_g_post_mlp)
    weights = {n: given[n] for n in TWIN_WEIGHTS}
    shared = {n: given[n] for n in SHARED_INPUTS}
    per_example = {n: given[n] for n in ['x', 'mem', 'positions']}
    grad_fn = _jax.value_and_grad(_loss, argnums=(0, 1))

    def one_microbatch(ex, loss_target):
        ex = dict(ex)
        diff = ex.pop(TWIN_DIFF_INPUT)
        return grad_fn(weights, diff, {**shared, **ex}, loss_target)

    if N_MICROBATCH == 1:
        loss, (grad_w, grad_x) = one_microbatch(per_example, given["loss_target"])
    else:
        def body(carry, xs):
            loss_sum, grad_sum = carry
            l_k, (gw_k, gx_k) = one_microbatch(xs[0], xs[1])
            with _jax.named_scope("update"):
                return (loss_sum + l_k, _jax.tree.map(_jnp.add, grad_sum, gw_k)), gx_k

        init = (_jnp.zeros((), _jnp.float32), _jax.tree.map(_jnp.zeros_like, weights))
        (loss, grad_w), grad_x = _jax.lax.scan(body, init, (per_example, given["loss_target"]))
    with _jax.named_scope("update"):
        delta_w, new_m, new_v = {}, {}, {}
        for n in TWIN_WEIGHTS:
            delta_w[n], new_m[n], new_v[n] = _adamw(weights[n], grad_w[n], given["m_" + n], given["v_" + n])
    return (loss, grad_x, *[grad_w[n] for n in TWIN_WEIGHTS], *[delta_w[n] for n in TWIN_WEIGHTS],
            *[new_m[n] for n in TWIN_WEIGHTS], *[new_v[n] for n in TWIN_WEIGHTS])
```

```python
import functools

import jax
import jax.numpy as jnp
from jax import lax
from jax.experimental import pallas as pl
from jax.experimental.pallas import tpu as pltpu

F32, BF16 = jnp.float32, jnp.bfloat16

D_MODEL = 1024
ATTN_W = 512
CONV_W = 256
XATTN_W = 256
PROJ_W = 3 * ATTN_W + 3 * CONV_W + XATTN_W
D_FF = 4096
HEAD = 64
N_BACK = 128
DILATIONS = (1, 4, 16)
ROPE_THETA = 10000.0
EPS = 1e-6
NEG_INF = -1e30
SCALE = HEAD ** -0.5
N_CHIPS = 4
SHARD_IN = PROJ_W // N_CHIPS
SHARD_FF = D_FF // N_CHIPS

ADAM_LR, ADAM_B1, ADAM_B2, ADAM_EPS, ADAM_WD, ADAM_STEP = 0.001, 0.9, 0.999, 1e-08, 0.01, 10

VMEM_LIMIT_V7X = 56 * 1024 * 1024
ROW_TILE = 256
SMALL_ROWS = 16

NT = (((1,), (1,)), ((), ()))
TN = (((0,), (0,)), ((), ()))
MESH = pl.DeviceIdType.MESH


def _params(*sem):
    return pltpu.CompilerParams(dimension_semantics=sem, vmem_limit_bytes=VMEM_LIMIT_V7X)


def _resident(shape):
    return pl.BlockSpec(shape, lambda *_: (0,) * len(shape), pipeline_mode=pl.Buffered(1))


def _rows(tm, width):
    return pl.BlockSpec((tm, width), lambda i: (i, 0))


def _rms_hat(x):
    r = lax.rsqrt(jnp.mean(x * x, axis=-1, keepdims=True) + EPS)
    return x * r, r


def _rms_bwd(xhat, r, g, dy):
    gdy = dy * g
    return r * (gdy - xhat * jnp.mean(xhat * gdy, axis=-1, keepdims=True))


def _rope128(t, cos, sin_signed, inverse):
    lane = lax.broadcasted_iota(jnp.int32, t.shape, 1)
    first_half = (lane % HEAD) < (HEAD // 2)
    rot = jnp.where(first_half, pltpu.roll(t, 128 - HEAD // 2, 1), pltpu.roll(t, HEAD // 2, 1))
    return t * cos - rot * sin_signed if inverse else t * cos + rot * sin_signed


def _in_proj_fwd(x, g, w_in, cos, sin, tm):
    S = x.shape[0]

    def body(x_ref, g_ref, w_ref, cos_ref, sin_ref, h_ref, q_ref, k_ref, v_ref, bcu_ref, qx_ref, proj):
        xhat, _ = _rms_hat(x_ref[...])
        h = (xhat * g_ref[...]).astype(BF16)
        h_ref[...] = h
        for j in range(N_CHIPS):
            proj[:, SHARD_IN * j:SHARD_IN * (j + 1)] = jnp.dot(h, w_ref[j], preferred_element_type=F32)
        c, s = cos_ref[...], sin_ref[...]
        for j in range(ATTN_W // 128):
            lo = 128 * j
            q_ref[:, lo:lo + 128] = _rope128(proj[:, lo:lo + 128], c, s, False).astype(BF16)
            k_ref[:, lo:lo + 128] = _rope128(proj[:, ATTN_W + lo:ATTN_W + lo + 128], c, s, False).astype(BF16)
        v_ref[...] = proj[:, 2 * ATTN_W:3 * ATTN_W].astype(BF16)
        bcu_ref[...] = proj[:, 3 * ATTN_W:3 * ATTN_W + 3 * CONV_W]
        qx_ref[...] = proj[:, 3 * ATTN_W + 3 * CONV_W:PROJ_W].astype(BF16)

    return pl.pallas_call(
        body, name="in_proj_fwd", grid=(S // tm,),
        in_specs=[_rows(tm, D_MODEL), _resident((1, D_MODEL)), _resident((N_CHIPS, D_MODEL, SHARD_IN)),
                  _rows(tm, 128), _rows(tm, 128)],
        out_specs=[_rows(tm, D_MODEL), _rows(tm, ATTN_W), _rows(tm, ATTN_W), _rows(tm, ATTN_W),
                   _rows(tm, 3 * CONV_W), _rows(tm, XATTN_W)],
        out_shape=[jax.ShapeDtypeStruct((S, D_MODEL), BF16), jax.ShapeDtypeStruct((S, ATTN_W), BF16),
                   jax.ShapeDtypeStruct((S, ATTN_W), BF16), jax.ShapeDtypeStruct((S, ATTN_W), BF16),
                   jax.ShapeDtypeStruct((S, 3 * CONV_W), F32), jax.ShapeDtypeStruct((S, XATTN_W), BF16)],
        scratch_shapes=[pltpu.VMEM((tm, PROJ_W), F32)],
        compiler_params=_params("parallel"),
    )(x, g, w_in, cos, sin)


def _memkv_fwd(mem, g_mem, w_kv):
    n_mem = mem.shape[0]

    def body(mem_ref, g_ref, w_ref, mn_ref, kv_ref):
        mhat, _ = _rms_hat(mem_ref[...])
        mn = (mhat * g_ref[...]).astype(BF16)
        mn_ref[...] = mn
        kv_ref[...] = jnp.dot(mn, w_ref[...], preferred_element_type=F32).astype(BF16)

    return pl.pallas_call(
        body, name="memkv_fwd",
        out_shape=[jax.ShapeDtypeStruct((n_mem, D_MODEL), BF16), jax.ShapeDtypeStruct((n_mem, 2 * XATTN_W), BF16)],
        compiler_params=pltpu.CompilerParams(vmem_limit_bytes=VMEM_LIMIT_V7X),
    )(mem, g_mem, w_kv)


def _band_mask(n):
    row = lax.broadcasted_iota(jnp.int32, (N_BACK, 2 * N_BACK), 0)
    col = lax.broadcasted_iota(jnp.int32, (N_BACK, 2 * N_BACK), 1)
    first_key = jnp.where(n > 0, 0, N_BACK)
    return (col >= row) & (col <= row + N_BACK) & (col >= first_key)


def _window(ref, n):
    own = pl.multiple_of(n * N_BACK, N_BACK)
    prev = pl.multiple_of(jnp.maximum(n - 1, 0) * N_BACK, N_BACK)
    return jnp.concatenate([ref[pl.ds(prev, N_BACK), :], ref[pl.ds(own, N_BACK), :]], axis=0), own, prev


def _attn_fwd(q, k, v, d):
    S = q.shape[0]
    L = S // d
    nb = L // N_BACK
    view = lambda t: t.reshape(L, d * ATTN_W)

    def body(q_ref, k_ref, v_ref, o_ref, lse_ref):
        def block(n, carry):
            kw, own, _ = _window(k_ref, n)
            vw, _, _ = _window(v_ref, n)
            qb = q_ref[pl.ds(own, N_BACK), :]
            valid = _band_mask(n)
            for hh in range(2):
                sl = slice(HEAD * hh, HEAD * (hh + 1))
                s = lax.dot_general(qb[:, sl], kw[:, sl], NT, preferred_element_type=F32) * SCALE
                s = jnp.where(valid, s, NEG_INF)
                m = jnp.max(s, axis=1, keepdims=True)
                p = jnp.exp(s - m)
                l = jnp.sum(p, axis=1, keepdims=True)
                o = jnp.dot(p.astype(BF16), vw[:, sl], preferred_element_type=F32) / l
                o_ref[pl.ds(own, N_BACK), sl] = o
                lse_ref[pl.ds(own, N_BACK), sl] = jnp.broadcast_to(m + jnp.log(l), (N_BACK, HEAD))
            return carry

        lax.fori_loop(0, nb, block, 0)

    col = pl.BlockSpec((L, 128), lambda j: (0, j))
    o, lse = pl.pallas_call(
        body, name=f"attn_fwd_d{d}", grid=(d * ATTN_W // 128,),
        in_specs=[col, col, col], out_specs=[col, col],
        out_shape=[jax.ShapeDtypeStruct((L, d * ATTN_W), F32)] * 2,
        compiler_params=_params("parallel"),
    )(view(q), view(k), view(v))
    return o.reshape(S, ATTN_W), lse.reshape(S, ATTN_W)


def _attn_bwd(q, k, v, dy, y, lse, d):
    S = q.shape[0]
    L = S // d
    nb = L // N_BACK
    view = lambda t: t.reshape(L, d * ATTN_W)

    def body(q_ref, k_ref, v_ref, dy_ref, y_ref, lse_ref, dq_ref, dk_ref, dv_ref):
        dk_ref[...] = jnp.zeros_like(dk_ref)
        dv_ref[...] = jnp.zeros_like(dv_ref)

        def block(n, carry):
            kw, own, prev = _window(k_ref, n)
            vw, _, _ = _window(v_ref, n)
            rows = pl.ds(own, N_BACK)
            qb, dyb, yb, lseb = q_ref[rows, :], dy_ref[rows, :], y_ref[rows, :], lse_ref[rows, :]
            valid = _band_mask(n)
            for hh in range(2):
                sl = slice(HEAD * hh, HEAD * (hh + 1))
                s = lax.dot_general(qb[:, sl], kw[:, sl], NT, preferred_element_type=F32) * SCALE
                s = jnp.where(valid, s, NEG_INF)
                p = jnp.exp(s - lseb[:, HEAD * hh:HEAD * hh + 1])
                do = dyb[:, sl]
                delta = jnp.sum(do * yb[:, sl], axis=1, keepdims=True)
                dob = do.astype(BF16)
                dp = lax.dot_general(dob, vw[:, sl], NT, preferred_element_type=F32)
                ds = (p * (dp - delta) * SCALE).astype(BF16)
                dq_ref[rows, sl] = jnp.dot(ds, kw[:, sl], preferred_element_type=F32)
                dkw = lax.dot_general(ds, qb[:, sl], TN, preferred_element_type=F32)
                dvw = lax.dot_general(p.astype(BF16), dob, TN, preferred_element_type=F32)

                @pl.when(n > 0)
                def _():
                    dk_ref[pl.ds(prev, N_BACK), sl] += dkw[:N_BACK]
                    dv_ref[pl.ds(prev, N_BACK), sl] += dvw[:N_BACK]

                dk_ref[rows, sl] += dkw[N_BACK:]
                dv_ref[rows, sl] += dvw[N_BACK:]
            return carry

        lax.fori_loop(0, nb, block, 0)

    col = pl.BlockSpec((L, 128), lambda j: (0, j))
    dq, dk, dv = pl.pallas_call(
        body, name=f"attn_bwd_d{d}", grid=(d * ATTN_W // 128,),
        in_specs=[col] * 6, out_specs=[col] * 3,
        out_shape=[jax.ShapeDtypeStruct((L, d * ATTN_W), F32)] * 3,
        compiler_params=_params("parallel"),
    )(view(q), view(k), view(v), view(dy), view(y), view(lse))
    return dq.reshape(S, ATTN_W), dk.reshape(S, ATTN_W), dv.reshape(S, ATTN_W)


def _shift_down(z, before, k):
    row = lax.broadcasted_iota(jnp.int32, z.shape, 0)
    out = pltpu.roll(z, k, 0)
    for i in range(k):
        out = jnp.where(row == i, before[8 - k + i:8 - k + i + 1, :], out)
    return out


def _shift_up(z, after, k):
    rows = z.shape[0]
    row = lax.broadcasted_iota(jnp.int32, z.shape, 0)
    out = pltpu.roll(z, rows - k, 0)
    for i in range(k):
        out = jnp.where(row == rows - k + i, after[i:i + 1, :], out)
    return out


def _conv_fwd(bcu, before, is_first, w):
    b, c, u = bcu[:, 0:CONV_W], bcu[:, CONV_W:2 * CONV_W], bcu[:, 2 * CONV_W:3 * CONV_W]
    z = c * u
    zb = jnp.where(is_first, 0.0, before[:, CONV_W:2 * CONV_W] * before[:, 2 * CONV_W:3 * CONV_W])
    z1, z2 = _shift_down(z, zb, 1), _shift_down(z, zb, 2)
    cv = w[0:1, :] * z2 + w[1:2, :] * z1 + w[2:3, :] * z
    return b, c, u, z, z1, z2, cv


def _halo_before(tm, width):
    return pl.BlockSpec((8, width), lambda i: (jnp.maximum(i * (tm // 8) - 1, 0), 0))


def _halo_after(tm, width, S):
    return pl.BlockSpec((8, width), lambda i: (jnp.minimum((i + 1) * (tm // 8), S // 8 - 1), 0))


def _mix_fwd(o3, lse3, bcu, qx, mkv, conv_w, g_a, g_c, g_x, w_out, g_post, x, tm):
    S = x.shape[0]

    def body(o1, o2, o3_, l1, l2, l3, bcu_ref, before_ref, qx_ref, mkv_ref, cw_ref, ga_ref, gc_ref, gx_ref,
             wo_ref, gp_ref, x_ref, ya_ref, lse_ref, yx_ref, ycat_ref, y2_ref, x1_ref):
        la, lb, lc = l1[...], l2[...], l3[...]
        m = jnp.maximum(jnp.maximum(la, lb), lc)
        ea, eb, ec = jnp.exp(la - m), jnp.exp(lb - m), jnp.exp(lc - m)
        den = ea + eb + ec
        ya = (ea * o1[...] + eb * o2[...] + ec * o3_[...]) / den
        ya_ref[...] = ya
        lse_ref[...] = m + jnp.log(den)

        b, _, _, _, _, _, cv = _conv_fwd(bcu_ref[...], before_ref[...], pl.program_id(0) == 0, cw_ref[...])
        yc = b * cv

        qxb, mkvb = qx_ref[...], mkv_ref[...]
        for hd in range(XATTN_W // HEAD):
            sl = slice(HEAD * hd, HEAD * (hd + 1))
            s = lax.dot_general(qxb[:, sl], mkvb[:, sl], NT, preferred_element_type=F32) * SCALE
            mx = jnp.max(s, axis=1, keepdims=True)
            p = jnp.exp(s - mx)
            l = jnp.sum(p, axis=1, keepdims=True)
            vm = mkvb[:, XATTN_W + HEAD * hd:XATTN_W + HEAD * (hd + 1)]
            yx_ref[:, sl] = jnp.dot(p.astype(BF16), vm, preferred_element_type=F32) / l
        yx = yx_ref[...]

        ycat_ref[:, 0:ATTN_W] = (_rms_hat(ya)[0] * ga_ref[...]).astype(BF16)
        ycat_ref[:, ATTN_W:ATTN_W + CONV_W] = (_rms_hat(yc)[0] * gc_ref[...]).astype(BF16)
        ycat_ref[:, ATTN_W + CONV_W:D_MODEL] = (_rms_hat(yx)[0] * gx_ref[...]).astype(BF16)
        y2 = jnp.dot(ycat_ref[...], wo_ref[...], preferred_element_type=F32)
        y2_ref[...] = y2
        x1_ref[...] = x_ref[...] + _rms_hat(y2)[0] * gp_ref[...]

    n_mem = mkv.shape[0]
    return pl.pallas_call(
        body, name="mix_fwd", grid=(S // tm,),
        in_specs=[_rows(tm, ATTN_W)] * 6 + [_rows(tm, 3 * CONV_W), _halo_before(tm, 3 * CONV_W), _rows(tm, XATTN_W),
                  _resident((n_mem, 2 * XATTN_W)), _resident((3, CONV_W)), _resident((1, ATTN_W)),
                  _resident((1, CONV_W)), _resident((1, XATTN_W)), _resident((D_MODEL, D_MODEL)),
                  _resident((1, D_MODEL)), _rows(tm, D_MODEL)],
        out_specs=[_rows(tm, ATTN_W), _rows(tm, ATTN_W), _rows(tm, XATTN_W), _rows(tm, D_MODEL),
                   _rows(tm, D_MODEL), _rows(tm, D_MODEL)],
        out_shape=[jax.ShapeDtypeStruct((S, ATTN_W), F32), jax.ShapeDtypeStruct((S, ATTN_W), F32),
                   jax.ShapeDtypeStruct((S, XATTN_W), F32), jax.ShapeDtypeStruct((S, D_MODEL), BF16),
                   jax.ShapeDtypeStruct((S, D_MODEL), F32), jax.ShapeDtypeStruct((S, D_MODEL), F32)],
        compiler_params=_params("parallel"),
    )(*o3, *lse3, bcu, bcu, qx, mkv, conv_w, g_a, g_c, g_x, w_out, g_post, x)


def _mlp_fwd_bwd(x1, target, g_pre, g_post, w_up, w_down, tm):
    S = x1.shape[0]
    n_ff = D_FF // SHARD_FF

    def body(x1_ref, t_ref, gpre_ref, gpost_ref, wup_ref, wdn_ref,
             h2_ref, f_ref, du_ref, df2_ref, dx1_ref, dgpre_ref, dgpost_ref, loss_ref, u_scr):
        @pl.when(pl.program_id(0) == 0)
        def _():
            dgpre_ref[...] = jnp.zeros_like(dgpre_ref)
            dgpost_ref[...] = jnp.zeros_like(dgpost_ref)
            loss_ref[...] = jnp.zeros_like(loss_ref)

        x1 = x1_ref[...]
        x1hat, r1 = _rms_hat(x1)
        h2 = (x1hat * gpre_ref[...]).astype(BF16)
        h2_ref[...] = h2
        f2 = jnp.zeros((tm, D_MODEL), F32)
        for j in range(n_ff):
            cols = slice(SHARD_FF * j, SHARD_FF * (j + 1))
            u = jnp.maximum(jnp.dot(h2, wup_ref[j], preferred_element_type=F32), 0.0)
            u_scr[:, cols] = u
            f = (u * u).astype(BF16)
            f_ref[:, cols] = f
            f2 = f2 + jnp.dot(f, wdn_ref[cols, :], preferred_element_type=F32)
        f2hat, r2 = _rms_hat(f2)
        err = x1 + f2hat * gpost_ref[...] - t_ref[...]
        loss_ref[...] += 0.5 * jnp.sum(jnp.mean(err * err, axis=-1, keepdims=True), axis=0, keepdims=True)
        dx2 = err * (1.0 / D_MODEL)
        dgpost_ref[...] += jnp.sum(dx2 * f2hat, axis=0, keepdims=True)
        df2 = _rms_bwd(f2hat, r2, gpost_ref[...], dx2).astype(BF16)
        df2_ref[...] = df2
        dh2 = jnp.zeros((tm, D_MODEL), F32)
        for j in range(n_ff):
            cols = slice(SHARD_FF * j, SHARD_FF * (j + 1))
            df = lax.dot_general(df2, wdn_ref[cols, :], NT, preferred_element_type=F32)
            du = (2.0 * u_scr[:, cols] * df).astype(BF16)
            du_ref[:, cols] = du
            dh2 = dh2 + lax.dot_general(du, wup_ref[j], NT, preferred_element_type=F32)
        dgpre_ref[...] += jnp.sum(dh2 * x1hat, axis=0, keepdims=True)
        dx1_ref[...] = dx2 + _rms_bwd(x1hat, r1, gpre_ref[...], dh2)

    acc = pl.BlockSpec((1, D_MODEL), lambda i: (0, 0))
    return pl.pallas_call(
        body, name="mlp_fwd_bwd", grid=(S // tm,),
        in_specs=[_rows(tm, D_MODEL), _rows(tm, D_MODEL), _resident((1, D_MODEL)), _resident((1, D_MODEL)),
                  _resident((n_ff, D_MODEL, SHARD_FF)), _resident((D_FF, D_MODEL))],
        out_specs=[_rows(tm, D_MODEL), _rows(tm, D_FF), _rows(tm, D_FF), _rows(tm, D_MODEL), _rows(tm, D_MODEL),
                   acc, acc, pl.BlockSpec((1, 1), lambda i: (0, 0))],
        out_shape=[jax.ShapeDtypeStruct((S, D_MODEL), BF16), jax.ShapeDtypeStruct((S, D_FF), BF16),
                   jax.ShapeDtypeStruct((S, D_FF), BF16), jax.ShapeDtypeStruct((S, D_MODEL), BF16),
                   jax.ShapeDtypeStruct((S, D_MODEL), F32), jax.ShapeDtypeStruct((1, D_MODEL), F32),
                   jax.ShapeDtypeStruct((1, D_MODEL), F32), jax.ShapeDtypeStruct((1, 1), F32)],
        scratch_shapes=[pltpu.VMEM((tm, D_FF), F32)],
        compiler_params=_params("arbitrary"),
    )(x1, target, g_pre, g_post, w_up, w_down)


def _weight_grad(name, a, b, tk, tn, ts, rows_sharded):
    S, K = a.shape
    N = b.shape[1]
    nk, nn = K // tk, N // tn
    if rows_sharded:
        assert nk == 2 * N_CHIPS and nn == 1
        out_index = lambda k, n, s: (k % 2, k // 2, 0, 0)
    else:
        assert nk == 2 and nn == N_CHIPS
        out_index = lambda k, n, s: (k, n, 0, 0)

    def body(a_ref, b_ref, o_ref):
        @pl.when(pl.program_id(2) == 0)
        def _():
            o_ref[...] = jnp.zeros_like(o_ref)

        o_ref[0, 0] += lax.dot_general(a_ref[...], b_ref[...], TN, preferred_element_type=F32)

    return pl.pallas_call(
        body, name=name, grid=(nk, nn, S // ts),
        in_specs=[pl.BlockSpec((ts, tk), lambda k, n, s: (s, k)), pl.BlockSpec((ts, tn), lambda k, n, s: (s, n))],
        out_specs=pl.BlockSpec((1, 1, tk, tn), out_index),
        out_shape=jax.ShapeDtypeStruct((2, N_CHIPS, tk, tn), F32),
        compiler_params=_params("parallel", "parallel", "arbitrary"),
    )(a, b)


def _mix_bwd(dx1, y2, ya, yx, bcu, conv_w, g_a, g_c, g_x, w_out, g_post, tm):
    S = dx1.shape[0]

    def body(dx1_ref, y2_ref, ya_ref, yx_ref, bcu_ref, before_ref, cw_ref, ga_ref, gc_ref, gx_ref, wo_ref, gp_ref,
             dy2_ref, dya_ref, dycx_ref, dgp_ref, dga_ref, dgc_ref, dgx_ref):
        @pl.when(pl.program_id(0) == 0)
        def _():
            for ref in (dgp_ref, dga_ref, dgc_ref, dgx_ref):
                ref[...] = jnp.zeros_like(ref)

        dx1 = dx1_ref[...]
        y2hat, r2 = _rms_hat(y2_ref[...])
        dgp_ref[...] += jnp.sum(dx1 * y2hat, axis=0, keepdims=True)
        dy2 = _rms_bwd(y2hat, r2, gp_ref[...], dx1).astype(BF16)
        dy2_ref[...] = dy2
        dycat = lax.dot_general(dy2, wo_ref[...], NT, preferred_element_type=F32)

        d_na = dycat[:, 0:ATTN_W]
        yahat, ra = _rms_hat(ya_ref[...])
        dga_ref[...] += jnp.sum(d_na * yahat, axis=0, keepdims=True)
        dya_ref[...] = _rms_bwd(yahat, ra, ga_ref[...], d_na)

        b, _, _, _, _, _, cv = _conv_fwd(bcu_ref[...], before_ref[...], pl.program_id(0) == 0, cw_ref[...])
        d_nc = dycat[:, ATTN_W:ATTN_W + CONV_W]
        ychat, rc = _rms_hat(b * cv)
        dgc_ref[...] += jnp.sum(d_nc * ychat, axis=0, keepdims=True)
        dycx_ref[:, 0:CONV_W] = _rms_bwd(ychat, rc, gc_ref[...], d_nc)

        d_nx = dycat[:, ATTN_W + CONV_W:D_MODEL]
        yxhat, rx = _rms_hat(yx_ref[...])
        dgx_ref[...] += jnp.sum(d_nx * yxhat, axis=0, keepdims=True)
        dycx_ref[:, CONV_W:CONV_W + XATTN_W] = _rms_bwd(yxhat, rx, gx_ref[...], d_nx)

    acc = lambda w: pl.BlockSpec((1, w), lambda i: (0, 0))
    return pl.pallas_call(
        body, name="mix_bwd", grid=(S // tm,),
        in_specs=[_rows(tm, D_MODEL), _rows(tm, D_MODEL), _rows(tm, ATTN_W), _rows(tm, XATTN_W),
                  _rows(tm, 3 * CONV_W), _halo_before(tm, 3 * CONV_W), _resident((3, CONV_W)),
                  _resident((1, ATTN_W)), _resident((1, CONV_W)), _resident((1, XATTN_W)),
                  _resident((D_MODEL, D_MODEL)), _resident((1, D_MODEL))],
        out_specs=[_rows(tm, D_MODEL), _rows(tm, ATTN_W), _rows(tm, CONV_W + XATTN_W),
                   acc(D_MODEL), acc(ATTN_W), acc(CONV_W), acc(XATTN_W)],
        out_shape=[jax.ShapeDtypeStruct((S, D_MODEL), BF16), jax.ShapeDtypeStruct((S, ATTN_W), F32),
                   jax.ShapeDtypeStruct((S, CONV_W + XATTN_W), F32), jax.ShapeDtypeStruct((1, D_MODEL), F32),
                   jax.ShapeDtypeStruct((1, ATTN_W), F32), jax.ShapeDtypeStruct((1, CONV_W), F32),
                   jax.ShapeDtypeStruct((1, XATTN_W), F32)],
        compiler_params=_params("arbitrary"),
    )(dx1, y2, ya, yx, bcu, bcu, conv_w, g_a, g_c, g_x, w_out, g_post)


def _conv_xattn_bwd(dycx, bcu, qx, mkv, conv_w, tm):
    S = dycx.shape[0]
    n_mem = mkv.shape[0]
    n_tiles = S // tm

    def body(d_ref, dafter_ref, bcu_ref, before_ref, after_ref, qx_ref, mkv_ref, cw_ref,
             tail_ref, dmkv_ref, dcw_ref):
        i = pl.program_id(0)

        @pl.when(i == 0)
        def _():
            dmkv_ref[...] = jnp.zeros_like(dmkv_ref)
            dcw_ref[...] = jnp.zeros_like(dcw_ref)

        w = cw_ref[...]
        b, c, u, z, z1, z2, cv = _conv_fwd(bcu_ref[...], before_ref[...], i == 0, w)
        dyc = d_ref[:, 0:CONV_W]
        dcv = dyc * b
        dcv_after = jnp.where(i == n_tiles - 1, 0.0, dafter_ref[:, 0:CONV_W] * after_ref[:, 0:CONV_W])
        dz = w[2:3, :] * dcv + w[1:2, :] * _shift_up(dcv, dcv_after, 1) + w[0:1, :] * _shift_up(dcv, dcv_after, 2)
        dcw_ref[0:1, :] += jnp.sum(dcv * z2, axis=0, keepdims=True)
        dcw_ref[1:2, :] += jnp.sum(dcv * z1, axis=0, keepdims=True)
        dcw_ref[2:3, :] += jnp.sum(dcv * z, axis=0, keepdims=True)
        tail_ref[:, 0:CONV_W] = (dyc * cv).astype(BF16)
        tail_ref[:, CONV_W:2 * CONV_W] = (dz * u).astype(BF16)
        tail_ref[:, 2 * CONV_W:3 * CONV_W] = (dz * c).astype(BF16)

        qxb, mkvb = qx_ref[...], mkv_ref[...]
        for hd in range(XATTN_W // HEAD):
            sl = slice(HEAD * hd, HEAD * (hd + 1))
            vsl = slice(XATTN_W + HEAD * hd, XATTN_W + HEAD * (hd + 1))
            s = lax.dot_general(qxb[:, sl], mkvb[:, sl], NT, preferred_element_type=F32) * SCALE
            e = jnp.exp(s - jnp.max(s, axis=1, keepdims=True))
            p = e / jnp.sum(e, axis=1, keepdims=True)
            dob = d_ref[:, CONV_W + HEAD * hd:CONV_W + HEAD * (hd + 1)].astype(BF16)
            dp = lax.dot_general(dob, mkvb[:, vsl], NT, preferred_element_type=F32)
            ds = (p * (dp - jnp.sum(p * dp, axis=1, keepdims=True)) * SCALE).astype(BF16)
            tail_ref[:, 3 * CONV_W + HEAD * hd:3 * CONV_W + HEAD * (hd + 1)] = jnp.dot(
                ds, mkvb[:, sl], preferred_element_type=F32).astype(BF16)
            dmkv_ref[:, sl] += lax.dot_general(ds, qxb[:, sl], TN, preferred_element_type=F32)
            dmkv_ref[:, vsl] += lax.dot_general(p.astype(BF16), dob, TN, preferred_element_type=F32)

    width = CONV_W + XATTN_W
    return pl.pallas_call(
        body, name="conv_xattn_bwd", grid=(n_tiles,),
        in_specs=[_rows(tm, width), _halo_after(tm, width, S), _rows(tm, 3 * CONV_W), _halo_before(tm, 3 * CONV_W),
                  _halo_after(tm, 3 * CONV_W, S), _rows(tm, XATTN_W), _resident((n_mem, 2 * XATTN_W)),
                  _resident((3, CONV_W))],
        out_specs=[_rows(tm, 3 * CONV_W + XATTN_W), pl.BlockSpec((n_mem, 2 * XATTN_W), lambda i: (0, 0)),
                   pl.BlockSpec((3, CONV_W), lambda i: (0, 0))],
        out_shape=[jax.ShapeDtypeStruct((S, 3 * CONV_W + XATTN_W), BF16),
                   jax.ShapeDtypeStruct((n_mem, 2 * XATTN_W), F32), jax.ShapeDtypeStruct((3, CONV_W), F32)],
        compiler_params=_params("arbitrary"),
    )(dycx, dycx, bcu, bcu, bcu, qx, mkv, conv_w)


def _memkv_bwd(mem, g_mem, w_kv, dmkv):
    n_mem = mem.shape[0]
    half = D_MODEL // N_CHIPS // 2

    def body(mem_ref, g_ref, w_ref, d_ref, dw_ref, dg_ref):
        mhat, _ = _rms_hat(mem_ref[...])
        mn = (mhat * g_ref[...]).astype(BF16)
        d = d_ref[...].astype(BF16)
        for k in range(2 * N_CHIPS):
            dw_ref[k % 2, k // 2] = lax.dot_general(mn[:, half * k:half * (k + 1)], d, TN, preferred_element_type=F32)
        dmn = lax.dot_general(d, w_ref[...], NT, preferred_element_type=F32)
        dg_ref[...] = jnp.sum(dmn * mhat, axis=0, keepdims=True)

    return pl.pallas_call(
        body, name="memkv_bwd",
        out_shape=[jax.ShapeDtypeStruct((2, N_CHIPS, half, 2 * XATTN_W), F32), jax.ShapeDtypeStruct((1, D_MODEL), F32)],
        compiler_params=pltpu.CompilerParams(vmem_limit_bytes=VMEM_LIMIT_V7X),
    )(mem, g_mem, w_kv, dmkv)


def _in_proj_bwd(dqkv, tail, cos, sin, w_in, x, g, dx1, tm):
    S = x.shape[0]

    def body(*refs):
        dq_refs, dk_refs, dv_refs = refs[0:3], refs[3:6], refs[6:9]
        tail_ref, cos_ref, sin_ref, w_ref, x_ref, g_ref, dx1_ref, dproj_ref, dx_ref, dg_ref = refs[9:]

        @pl.when(pl.program_id(0) == 0)
        def _():
            dg_ref[...] = jnp.zeros_like(dg_ref)

        c, s = cos_ref[...], sin_ref[...]
        for j in range(ATTN_W // 128):
            cols = slice(128 * j, 128 * (j + 1))
            dq = dq_refs[0][:, cols] + dq_refs[1][:, cols] + dq_refs[2][:, cols]
            dk = dk_refs[0][:, cols] + dk_refs[1][:, cols] + dk_refs[2][:, cols]
            dv = dv_refs[0][:, cols] + dv_refs[1][:, cols] + dv_refs[2][:, cols]
            dproj_ref[:, cols] = _rope128(dq, c, s, True).astype(BF16)
            dproj_ref[:, ATTN_W + 128 * j:ATTN_W + 128 * (j + 1)] = _rope128(dk, c, s, True).astype(BF16)
            dproj_ref[:, 2 * ATTN_W + 128 * j:2 * ATTN_W + 128 * (j + 1)] = dv.astype(BF16)
        dproj_ref[:, 3 * ATTN_W:PROJ_W] = tail_ref[...]
        dh = jnp.zeros((tm, D_MODEL), F32)
        for j in range(N_CHIPS):
            dh = dh + lax.dot_general(dproj_ref[:, SHARD_IN * j:SHARD_IN * (j + 1)], w_ref[j], NT,
                                      preferred_element_type=F32)
        xhat, r = _rms_hat(x_ref[...])
        dg_ref[...] += jnp.sum(dh * xhat, axis=0, keepdims=True)
        dx_ref[...] = dx1_ref[...] + _rms_bwd(xhat, r, g_ref[...], dh)

    return pl.pallas_call(
        body, name="in_proj_bwd", grid=(S // tm,),
        in_specs=[_rows(tm, ATTN_W)] * 9 + [_rows(tm, PROJ_W - 3 * ATTN_W), _rows(tm, 128), _rows(tm, 128),
                  _resident((N_CHIPS, D_MODEL, SHARD_IN)), _rows(tm, D_MODEL), _resident((1, D_MODEL)),
                  _rows(tm, D_MODEL)],
        out_specs=[_rows(tm, PROJ_W), _rows(tm, D_MODEL), pl.BlockSpec((1, D_MODEL), lambda i: (0, 0))],
        out_shape=[jax.ShapeDtypeStruct((S, PROJ_W), BF16), jax.ShapeDtypeStruct((S, D_MODEL), F32),
                   jax.ShapeDtypeStruct((1, D_MODEL), F32)],
        compiler_params=_params("arbitrary"),
    )(*dqkv, tail, cos, sin, w_in, x, g, dx1)


def _row_tile(rows):
    return ROW_TILE if rows % ROW_TILE == 0 else rows


def _pair_sum_bf16(name, a, b):
    n, rows, cols = a.shape
    tr = _row_tile(rows)

    def body(a_ref, b_ref, o_ref):
        o_ref[...] = (a_ref[...] + b_ref[...]).astype(BF16)

    spec = pl.BlockSpec((1, tr, cols), lambda s, i: (s, i, 0))
    return pl.pallas_call(
        body, name=name, grid=(n, rows // tr), in_specs=[spec, spec], out_specs=spec,
        out_shape=jax.ShapeDtypeStruct(a.shape, BF16), compiler_params=_params("parallel", "parallel"),
    )(a, b)


def _final_sum(name, own, sibling, others):
    rows, cols = own.shape
    tr = _row_tile(rows)

    def body(own_ref, sib_ref, o0, o1, o2, out_ref):
        acc = own_ref[...] + sib_ref[...]
        for o in (o0, o1, o2):
            acc = acc + o[0].astype(F32)
        out_ref[...] = acc

    spec = pl.BlockSpec((tr, cols), lambda i: (i, 0))
    other = lambda k: pl.BlockSpec((1, tr, cols), lambda i: (k, i, 0))
    return pl.pallas_call(
        body, name=name, grid=(rows // tr,), in_specs=[spec, spec, other(0), other(1), other(2)], out_specs=spec,
        out_shape=jax.ShapeDtypeStruct(own.shape, F32), compiler_params=_params("parallel"),
    )(own, sibling, others, others, others)


def _adamw(name, w, g, m, v):
    rows, cols = w.shape
    tr = _row_tile(rows)

    def body(w_ref, g_ref, m_ref, v_ref, d_ref, nm_ref, nv_ref):
        g = g_ref[...]
        m = ADAM_B1 * m_ref[...] + (1.0 - ADAM_B1) * g
        v = ADAM_B2 * v_ref[...] + (1.0 - ADAM_B2) * (g * g)
        m_hat = m / (1.0 - ADAM_B1 ** ADAM_STEP)
        v_hat = v / (1.0 - ADAM_B2 ** ADAM_STEP)
        d_ref[...] = -ADAM_LR * (m_hat / (jnp.sqrt(v_hat) + ADAM_EPS) + ADAM_WD * w_ref[...])
        nm_ref[...] = m
        nv_ref[...] = v

    spec = pl.BlockSpec((tr, cols), lambda i: (i, 0))
    return pl.pallas_call(
        body, name=name, grid=(rows // tr,), in_specs=[spec] * 4, out_specs=[spec] * 3,
        out_shape=[jax.ShapeDtypeStruct(w.shape, F32)] * 3, compiler_params=_params("parallel"),
    )(w, g, m, v)


def _sum_blocks(name, blocks):
    n, rows, cols = blocks.shape

    def body(b_ref, o_ref):
        acc = b_ref[0]
        for k in range(1, n):
            acc = acc + b_ref[k]
        o_ref[...] = acc

    return pl.pallas_call(body, name=name, out_shape=jax.ShapeDtypeStruct((rows, cols), F32))(blocks)


def _place():
    return lax.axis_index("x"), lax.axis_index("y"), lax.axis_index("c")


def _other_chips(x, y):
    return [(1 - x, y), (x, 1 - y), (1 - x, 1 - y)]


def _small_allgather(name, block):
    rows, cols = block.shape
    relations = [(dx, dy, dc) for dx in (0, 1) for dy in (0, 1) for dc in (0, 1) if (dx, dy, dc) != (0, 0, 0)]

    def body(x_ref, out_ref, send_sems, recv_sems, local_sem):
        x, y, c = _place()

        def peer(rel):
            return (1 - x if rel[0] else x, 1 - y if rel[1] else y, 1 - c if rel[2] else c)

        def index(p):
            return 4 * p[0] + 2 * p[1] + p[2]

        def copy(k, origin, to):
            return pltpu.make_async_remote_copy(
                src_ref=x_ref, dst_ref=out_ref.at[index(origin)], send_sem=send_sems.at[k], recv_sem=recv_sems.at[k],
                device_id=to, device_id_type=MESH)

        mine = pltpu.make_async_copy(x_ref, out_ref.at[index((x, y, c))], local_sem)
        mine.start()
        sends = [copy(k, (x, y, c), peer(rel)) for k, rel in enumerate(relations)]
        for cp in sends:
            cp.start()
        for k, rel in enumerate(relations):
            copy(k, peer(rel), (x, y, c)).wait_recv()
        for cp in sends:
            cp.wait_send()
        mine.wait()

    return pl.pallas_call(
        body, name=name, out_shape=jax.ShapeDtypeStruct((8, rows, cols), F32),
        in_specs=[pl.BlockSpec(memory_space=pltpu.VMEM)], out_specs=pl.BlockSpec(memory_space=pltpu.VMEM),
        scratch_shapes=[pltpu.SemaphoreType.DMA((7,)), pltpu.SemaphoreType.DMA((7,)), pltpu.SemaphoreType.DMA],
    )(block)


def _weights_allgather(shards):
    n = len(shards)

    def body(*refs):
        ins, outs = refs[:n], refs[n:2 * n]
        send_sems, recv_sems, local_sems = refs[2 * n:]
        x, y, c = _place()
        me, sibling = (x, y, c), (x, y, 1 - c)
        chips = _other_chips(x, y)
        chip_index = lambda chip: 2 * chip[0] + chip[1]

        def copy(a, k, chip, half, to, src=None):
            place = outs[a].at[chip_index(chip), half]
            return pltpu.make_async_remote_copy(
                src_ref=place if src is None else src, dst_ref=place, send_sem=send_sems.at[6 * a + k],
                recv_sem=recv_sems.at[6 * a + k], device_id=to, device_id_type=MESH)

        local = [pltpu.make_async_copy(ins[a], outs[a].at[chip_index((x, y))], local_sems.at[a]) for a in range(n)]
        for cp in local:
            cp.start()
        first = [copy(a, k, (x, y), c, (*chip, c), src=ins[a].at[c]) for a in range(n) for k, chip in enumerate(chips)]
        for cp in first:
            cp.start()
        passed = []
        for a in range(n):
            for k, chip in enumerate(chips):
                copy(a, k, chip, c, me).wait_recv()
                passed.append(copy(a, 3 + k, chip, c, sibling))
                passed[-1].start()
        for a in range(n):
            for k, chip in enumerate(chips):
                copy(a, 3 + k, chip, 1 - c, me).wait_recv()
        for cp in first + passed:
            cp.wait_send()
        for cp in local:
            cp.wait()

    any_spec = pl.BlockSpec(memory_space=pl.ANY)
    return pl.pallas_call(
        body, name="weights_allgather",
        out_shape=[jax.ShapeDtypeStruct((N_CHIPS,) + s.shape, s.dtype) for s in shards],
        in_specs=[any_spec] * n, out_specs=[any_spec] * n,
        scratch_shapes=[pltpu.SemaphoreType.DMA((6 * n,)), pltpu.SemaphoreType.DMA((6 * n,)),
                        pltpu.SemaphoreType.DMA((n,))],
    )(*shards)


def _to_sibling(name, arrays, pick_other_half):
    n = len(arrays)

    def body(*refs):
        ins, outs = refs[:n], refs[n:2 * n]
        send_sems, recv_sems = refs[2 * n:]
        x, y, c = _place()
        copies = [pltpu.make_async_remote_copy(
            src_ref=ins[a].at[1 - c] if pick_other_half else ins[a], dst_ref=outs[a], send_sem=send_sems.at[a],
            recv_sem=recv_sems.at[a], device_id=(x, y, 1 - c), device_id_type=MESH) for a in range(n)]
        for cp in copies:
            cp.start()
        for cp in copies:
            cp.wait()

    any_spec = pl.BlockSpec(memory_space=pl.ANY)
    return pl.pallas_call(
        body, name=name,
        out_shape=[jax.ShapeDtypeStruct(a.shape[1:] if pick_other_half else a.shape, a.dtype) for a in arrays],
        in_specs=[any_spec] * n, out_specs=[any_spec] * n,
        scratch_shapes=[pltpu.SemaphoreType.DMA((n,)), pltpu.SemaphoreType.DMA((n,))],
    )(*arrays)


def _to_other_chips(partials):
    n = len(partials)

    def body(*refs):
        ins, outs = refs[:n], refs[n:2 * n]
        send_sems, recv_sems = refs[2 * n:]
        x, y, c = _place()
        copies = [pltpu.make_async_remote_copy(
            src_ref=ins[a].at[2 * chip[0] + chip[1]], dst_ref=outs[a].at[k], send_sem=send_sems.at[3 * a + k],
            recv_sem=recv_sems.at[3 * a + k], device_id=(*chip, c), device_id_type=MESH)
            for a in range(n) for k, chip in enumerate(_other_chips(x, y))]
        for cp in copies:
            cp.start()
        for cp in copies:
            cp.wait()

    any_spec = pl.BlockSpec(memory_space=pl.ANY)
    return pl.pallas_call(
        body, name="grads_to_other_chips",
        out_shape=[jax.ShapeDtypeStruct((3,) + p.shape[1:], p.dtype) for p in partials],
        in_specs=[any_spec] * n, out_specs=[any_spec] * n,
        scratch_shapes=[pltpu.SemaphoreType.DMA((3 * n,)), pltpu.SemaphoreType.DMA((3 * n,))],
    )(*partials)


def _exchange_halves(halves):
    n = len(halves)

    def body(*refs):
        ins, outs = refs[:n], refs[n:2 * n]
        send_sems, recv_sems, local_sems = refs[2 * n:]
        x, y, c = _place()
        local = [pltpu.make_async_copy(ins[a], outs[a].at[c], local_sems.at[a]) for a in range(n)]
        remote = [pltpu.make_async_remote_copy(
            src_ref=ins[a], dst_ref=outs[a].at[c], send_sem=send_sems.at[a], recv_sem=recv_sems.at[a],
            device_id=(x, y, 1 - c), device_id_type=MESH) for a in range(n)]
        for cp in local + remote:
            cp.start()
        for a in range(n):
            pltpu.make_async_remote_copy(
                src_ref=ins[a], dst_ref=outs[a].at[1 - c], send_sem=send_sems.at[a], recv_sem=recv_sems.at[a],
                device_id=(x, y, 1 - c), device_id_type=MESH).wait_recv()
        for cp in remote:
            cp.wait_send()
        for cp in local:
            cp.wait()

    any_spec = pl.BlockSpec(memory_space=pl.ANY)
    return pl.pallas_call(
        body, name="sums_to_sibling",
        out_shape=[jax.ShapeDtypeStruct((2,) + h.shape, h.dtype) for h in halves],
        in_specs=[any_spec] * n, out_specs=[any_spec] * n,
        scratch_shapes=[pltpu.SemaphoreType.DMA((n,)), pltpu.SemaphoreType.DMA((n,)), pltpu.SemaphoreType.DMA((n,))],
    )(*halves)


def _reduce_scatter(grads):
    x, y, c = _place()
    j = 2 * x + y
    from_sibling = _to_sibling("grads_to_sibling", grads, True)
    mine = [lax.dynamic_index_in_dim(g, c, 0, keepdims=False) for g in grads]
    partials = [_pair_sum_bf16(f"chip_sum_{a}", mine[a], from_sibling[a]) for a in range(len(grads))]
    from_chips = _to_other_chips(partials)
    own = [lax.dynamic_index_in_dim(m, j, 0, keepdims=False) for m in mine]
    sib = [lax.dynamic_index_in_dim(s, j, 0, keepdims=False) for s in from_sibling]
    halves = [_final_sum(f"final_sum_{a}", own[a], sib[a], from_chips[a]) for a in range(len(grads))]
    return [t.reshape(2 * t.shape[1], t.shape[2]) for t in _exchange_halves(halves)]


def _rope_tables(positions):
    half = HEAD // 2
    inv_freq = jnp.float32(ROPE_THETA) ** (-(jnp.arange(half, dtype=F32) * 2.0 / HEAD))
    ang = positions.astype(F32)[:, None] * inv_freq
    cos, sin = jnp.cos(ang), jnp.sin(ang)
    return jnp.tile(cos, (1, 4)), jnp.tile(jnp.concatenate([-sin, sin], axis=1), (1, 2))


def _local_step(x, mem, positions, target, gains, w_in, w_kv, conv_w, w_out, w_up, w_down):
    g_pre_mix, g_mem, g_a, g_c, g_x, g_post_mix, g_pre_mlp, g_post_mlp = gains
    tm = ROW_TILE
    cos, sin = _rope_tables(positions)

    h, q, k, v, bcu, qx = _in_proj_fwd(x, g_pre_mix, w_in, cos, sin, tm)
    memn, mkv = _memkv_fwd(mem, g_mem, w_kv)
    fwd = [_attn_fwd(q, k, v, d) for d in DILATIONS]
    ya, lse, yx, ycat, y2, x1 = _mix_fwd([o for o, _ in fwd], [l for _, l in fwd], bcu, qx, mkv, conv_w,
                                         g_a, g_c, g_x, w_out, g_post_mix, x, tm)
    h2, f, du, df2, dx1, dg_pre_mlp, dg_post_mlp, loss = _mlp_fwd_bwd(x1, target, g_pre_mlp, g_post_mlp, w_up, w_down, tm)
    gw_down = _weight_grad("grad_w_down", f, df2, 512, D_MODEL, 512, True)
    gw_up = _weight_grad("grad_w_up", h2, du, 512, SHARD_FF, 512, False)

    dy2, dya, dycx, dg_post_mix, dg_a, dg_c, dg_x = _mix_bwd(dx1, y2, ya, yx, bcu, conv_w, g_a, g_c, g_x, w_out,
                                                           g_post_mix, tm)
    gw_out = _weight_grad("grad_w_out", ycat, dy2, D_MODEL // 8, D_MODEL, 512, True)
    tail, dmkv, g_conv = _conv_xattn_bwd(dycx, bcu, qx, mkv, conv_w, tm)
    gw_kv, dg_mem = _memkv_bwd(mem, g_mem, w_kv, dmkv)
    bwd = [_attn_bwd(q, k, v, dya, ya, lse, d) for d in DILATIONS]
    dqkv = [t[0] for t in bwd] + [t[1] for t in bwd] + [t[2] for t in bwd]
    dproj, grad_x, dg_pre_mix = _in_proj_bwd(dqkv, tail, cos, sin, w_in, x, g_pre_mix, dx1, tm)
    gw_in = _weight_grad("grad_w_in", h, dproj, 512, SHARD_IN, 512, False)

    gain_grads = [dg_pre_mix, dg_mem, dg_a, dg_c, dg_x, dg_post_mix, dg_pre_mlp, dg_post_mlp]
    return loss, grad_x, [gw_in, gw_kv, gw_out, gw_up, gw_down], g_conv, gain_grads


def _pack_small(gains, conv):
    rows = [jnp.pad(g, ((0, 0), (0, D_MODEL - g.shape[1]))) for g in gains]
    rows.append(jnp.pad(conv, ((0, SMALL_ROWS - 8 - conv.shape[0]), (0, D_MODEL - conv.shape[1]))))
    return jnp.concatenate(rows, axis=0)


def _unpack_small(block, gain_widths, conv_width):
    gains = [block[i:i + 1, :w] for i, w in enumerate(gain_widths)]
    return gains, block[8:11, :conv_width]


def kernel(x, mem, positions, g_pre_mix, g_mem, w_in, w_mem_kv, conv_w, g_attn_out, g_conv_out, g_xattn_out, w_out, g_post_mix, g_pre_mlp, w_up, w_down, g_post_mlp, loss_target, m_g_pre_mix, m_g_mem, m_w_in, m_w_mem_kv, m_conv_w, m_g_attn_out, m_g_conv_out, m_g_xattn_out, m_w_out, m_g_post_mix, m_g_pre_mlp, m_w_up, m_w_down, m_g_post_mlp, v_g_pre_mix, v_g_mem, v_w_in, v_w_mem_kv, v_conv_w, v_g_attn_out, v_g_conv_out, v_g_xattn_out, v_w_out, v_g_post_mix, v_g_pre_mlp, v_w_up, v_w_down, v_g_post_mlp):
    cx, cy, cc = _place()
    chip = 2 * cx + cy
    gains = [g_pre_mix, g_mem, g_attn_out, g_conv_out, g_xattn_out, g_post_mix, g_pre_mlp, g_post_mlp]
    gains_m = [m_g_pre_mix, m_g_mem, m_g_attn_out, m_g_conv_out, m_g_xattn_out, m_g_post_mix, m_g_pre_mlp, m_g_post_mlp]
    gains_v = [v_g_pre_mix, v_g_mem, v_g_attn_out, v_g_conv_out, v_g_xattn_out, v_g_post_mix, v_g_pre_mlp, v_g_post_mlp]
    gain_widths = [g.shape[1] for g in gains]
    mats = [w_in[0], w_mem_kv[0], w_out[0], w_up[0], w_down[0]]
    mats_m = [m_w_in[0], m_w_mem_kv[0], m_w_out[0], m_w_up[0], m_w_down[0]]
    mats_v = [v_w_in[0], v_w_mem_kv[0], v_w_out[0], v_w_up[0], v_w_down[0]]

    shards = [w.astype(BF16).reshape(2, w.shape[0] // 2, w.shape[1]) for w in mats]
    full = [g.reshape(N_CHIPS, 2 * g.shape[2], g.shape[3]) for g in _weights_allgather(shards)]
    w_in_f, w_kv_f, w_out_f, w_up_f, w_down_f = full
    small = _small_allgather("conv_allgather", _pack_small(gains, conv_w[0]))
    conv_full = small[::2, 8:11, :conv_w.shape[2]].transpose(1, 0, 2).reshape(3, CONV_W)

    loss, grad_x, mat_grads, g_conv, gain_grads = _local_step(
        x[0], mem[0], positions[0], loss_target[0], gains, w_in_f, w_kv_f.reshape(D_MODEL, 2 * XATTN_W), conv_full,
        w_out_f.reshape(D_MODEL, D_MODEL), w_up_f, w_down_f.reshape(D_FF, D_MODEL))

    mat_sums = _reduce_scatter(mat_grads)
    small_sum = _sum_blocks("small_sum", _small_allgather("small_grads_allgather", _pack_small(gain_grads, g_conv)))
    gain_sums, conv_sum_full = _unpack_small(small_sum, gain_widths, CONV_W)
    conv_sum = lax.dynamic_slice_in_dim(conv_sum_full, chip * conv_w.shape[2], conv_w.shape[2], axis=1)

    mat_new = [_adamw(f"adamw_{a}", mats[a], mat_sums[a], mats_m[a], mats_v[a]) for a in range(len(mats))]
    pack = lambda gs, cv: _pack_small(gs, cv)
    small_new = _adamw("adamw_small", pack(gains, conv_w[0]), pack(gain_sums, conv_sum), pack(gains_m, m_conv_w[0]),
                       pack(gains_v, v_conv_w[0]))
    small_out = [_unpack_small(t, gain_widths, conv_w.shape[2]) for t in small_new]

    total = lax.psum(loss[0, 0], ("x", "y", "c"))
    order = ["g_pre_mix", "g_mem", "w_in", "w_mem_kv", "conv_w", "g_attn_out", "g_conv_out", "g_xattn_out", "w_out",
             "g_post_mix", "g_pre_mlp", "w_up", "w_down", "g_post_mlp"]
    gain_names = ["g_pre_mix", "g_mem", "g_attn_out", "g_conv_out", "g_xattn_out", "g_post_mix", "g_pre_mlp", "g_post_mlp"]
    mat_names = ["w_in", "w_mem_kv", "w_out", "w_up", "w_down"]

    def leaf(kind, name):
        if name in gain_names:
            i = gain_names.index(name)
            return gain_sums[i] if kind == 0 else small_out[kind - 1][0][i]
        if name == "conv_w":
            return (conv_sum if kind == 0 else small_out[kind - 1][1])[None]
        a = mat_names.index(name)
        return (mat_sums[a] if kind == 0 else mat_new[a][kind - 1])[None]

    return (total, grad_x[None], *[leaf(kind, name) for kind in range(4) for name in order])
```

```python
import functools

import jax
import jax.numpy as jnp
from jax import lax
from jax.experimental import pallas as pl
from jax.experimental.pallas import tpu as pltpu

F32, BF16 = jnp.float32, jnp.bfloat16

D_MODEL = 1024
ATTN_W = 512
CONV_W = 256
XATTN_W = 256
PROJ_W = 3 * ATTN_W + 3 * CONV_W + XATTN_W
D_FF = 4096
HEAD = 64
N_BACK = 128
DILATIONS = (1, 4, 16)
ROPE_THETA = 10000.0
EPS = 1e-6
NEG_INF = -1e30
SCALE = HEAD ** -0.5
N_CHIPS = 4
SHARD_IN = PROJ_W // N_CHIPS
SHARD_FF = D_FF // N_CHIPS

ADAM_LR, ADAM_B1, ADAM_B2, ADAM_EPS, ADAM_WD, ADAM_STEP = 0.001, 0.9, 0.999, 1e-08, 0.01, 10

VMEM_LIMIT_V7X = 56 * 1024 * 1024
ROW_TILE = 256
SMALL_ROWS = 16

NT = (((1,), (1,)), ((), ()))
TN = (((0,), (0,)), ((), ()))
MESH = pl.DeviceIdType.MESH


def _params(*sem):
    return pltpu.CompilerParams(dimension_semantics=sem, vmem_limit_bytes=VMEM_LIMIT_V7X)


def _resident(shape):
    return pl.BlockSpec(shape, lambda *_: (0,) * len(shape), pipeline_mode=pl.Buffered(1))


def _rows(tm, width):
    return pl.BlockSpec((tm, width), lambda i: (i, 0))


def _rms_hat(x):
    r = lax.rsqrt(jnp.mean(x * x, axis=-1, keepdims=True) + EPS)
    return x * r, r


def _rms_bwd(xhat, r, g, dy):
    gdy = dy * g
    return r * (gdy - xhat * jnp.mean(xhat * gdy, axis=-1, keepdims=True))


def _rope128(t, cos, sin_signed, inverse):
    lane = lax.broadcasted_iota(jnp.int32, t.shape, 1)
    first_half = (lane % HEAD) < (HEAD // 2)
    rot = jnp.where(first_half, pltpu.roll(t, 128 - HEAD // 2, 1), pltpu.roll(t, HEAD // 2, 1))
    return t * cos - rot * sin_signed if inverse else t * cos + rot * sin_signed


def _in_proj_fwd(x, g, w_in, cos, sin, tm):
    S = x.shape[0]

    def body(x_ref, g_ref, w_ref, cos_ref, sin_ref, h_ref, q_ref, k_ref, v_ref, bcu_ref, qx_ref, proj):
        xhat, _ = _rms_hat(x_ref[...])
        h = (xhat * g_ref[...]).astype(BF16)
        h_ref[...] = h
        for j in range(N_CHIPS):
            proj[:, SHARD_IN * j:SHARD_IN * (j + 1)] = jnp.dot(h, w_ref[j], preferred_element_type=F32)
        c, s = cos_ref[...], sin_ref[...]
        for j in range(ATTN_W // 128):
            lo = 128 * j
            q_ref[:, lo:lo + 128] = _rope128(proj[:, lo:lo + 128], c, s, False) * SCALE
            k_ref[:, lo:lo + 128] = _rope128(proj[:, ATTN_W + lo:ATTN_W + lo + 128], c, s, False)
        v_ref[...] = proj[:, 2 * ATTN_W:3 * ATTN_W]
        bcu_ref[...] = proj[:, 3 * ATTN_W:3 * ATTN_W + 3 * CONV_W]
        qx_ref[...] = proj[:, 3 * ATTN_W + 3 * CONV_W:PROJ_W].astype(BF16)

    return pl.pallas_call(
        body, name="in_proj_fwd", grid=(S // tm,),
        in_specs=[_rows(tm, D_MODEL), _resident((1, D_MODEL)), _resident((N_CHIPS, D_MODEL, SHARD_IN)),
                  _rows(tm, 128), _rows(tm, 128)],
        out_specs=[_rows(tm, D_MODEL), _rows(tm, ATTN_W), _rows(tm, ATTN_W), _rows(tm, ATTN_W),
                   _rows(tm, 3 * CONV_W), _rows(tm, XATTN_W)],
        out_shape=[jax.ShapeDtypeStruct((S, D_MODEL), BF16), jax.ShapeDtypeStruct((S, ATTN_W), F32),
                   jax.ShapeDtypeStruct((S, ATTN_W), F32), jax.ShapeDtypeStruct((S, ATTN_W), F32),
                   jax.ShapeDtypeStruct((S, 3 * CONV_W), F32), jax.ShapeDtypeStruct((S, XATTN_W), BF16)],
        scratch_shapes=[pltpu.VMEM((tm, PROJ_W), F32)],
        compiler_params=_params("parallel"),
    )(x, g, w_in, cos, sin)


def _memkv_fwd(mem, g_mem, w_kv):
    n_mem = mem.shape[0]

    def body(mem_ref, g_ref, w_ref, mn_ref, kv_ref):
        mhat, _ = _rms_hat(mem_ref[...])
        mn = (mhat * g_ref[...]).astype(BF16)
        mn_ref[...] = mn
        kv_ref[...] = jnp.dot(mn, w_ref[...], preferred_element_type=F32).astype(BF16)

    return pl.pallas_call(
        body, name="memkv_fwd",
        out_shape=[jax.ShapeDtypeStruct((n_mem, D_MODEL), BF16), jax.ShapeDtypeStruct((n_mem, 2 * XATTN_W), BF16)],
        compiler_params=pltpu.CompilerParams(vmem_limit_bytes=VMEM_LIMIT_V7X),
    )(mem, g_mem, w_kv)


def _fill_band_bias(bias):
    row = lax.broadcasted_iota(jnp.int32, (N_BACK, 2 * N_BACK), 0)
    col = lax.broadcasted_iota(jnp.int32, (N_BACK, 2 * N_BACK), 1)
    band = (col >= row) & (col <= row + N_BACK)
    bias[1] = jnp.where(band, 0.0, NEG_INF)
    bias[0] = jnp.where(band & (col >= N_BACK), 0.0, NEG_INF)


def _strided(start, size, d):
    return pl.ds(start, size) if d == 1 else pl.ds(start, size, stride=d)


def _block_starts(t, nb, d):
    r, n = lax.shift_right_logical(t, nb.bit_length() - 1), lax.bitwise_and(t, nb - 1)
    own = r + n * (N_BACK * d)
    prev = r + jnp.maximum(n - 1, 0) * (N_BACK * d)
    if d == 1:
        own, prev = pl.multiple_of(own, N_BACK), pl.multiple_of(prev, N_BACK)
    return own, prev, n


def _by_head(a, b):
    lane = lax.broadcasted_iota(jnp.int32, (a.shape[0], 2 * HEAD), 1)
    return jnp.where(lane < HEAD, a, b)


def _head_only(t, hh):
    lane = lax.broadcasted_iota(jnp.int32, t.shape, 1)
    return jnp.where((lane < HEAD) == (hh == 0), t, jnp.zeros_like(t))


def _stack_heads(t):
    return jnp.concatenate([_head_only(t, 0), _head_only(t, 1)], axis=0)


def _head_columns(t):
    return jnp.concatenate([t[:, 0:1], t[:, HEAD:HEAD + 1]], axis=0)


def _unstack(t):
    return _by_head(t[:N_BACK], t[N_BACK:])


def _unstack_columns(t):
    return _by_head(jnp.broadcast_to(t[:N_BACK], (N_BACK, 2 * HEAD)), jnp.broadcast_to(t[N_BACK:], (N_BACK, 2 * HEAD)))


FWD_BLOCKS_PER_STEP = 4
BWD_BLOCKS_PER_STEP = 2


def _attn_fwd(q, k, v):
    S = q.shape[0]
    U = FWD_BLOCKS_PER_STEP

    def body(q_ref, k_ref, v_ref, y_ref, m_ref, l_scr, bias):
        _fill_band_bias(bias)
        for g, d in enumerate(DILATIONS):
            nb = S // d // N_BACK
            first_pattern, last_pattern = g == 0, g == len(DILATIONS) - 1

            def step(i, carry, d=d, nb=nb, first_pattern=first_pattern, last_pattern=last_pattern):
                blocks = [_block_starts(U * i + u, nb, d) for u in range(U)]
                rows = [_strided(own, N_BACK, d) for own, _, _ in blocks]
                prev_rows = [_strided(prev, N_BACK, d) for _, prev, _ in blocks]
                ss = []
                for u, (_, _, n) in enumerate(blocks):
                    kw = jnp.concatenate([k_ref[prev_rows[u], :], k_ref[rows[u], :]], 0).astype(BF16)
                    qs = _stack_heads(q_ref[rows[u], :].astype(BF16))
                    b = bias[jnp.minimum(n, 1)]
                    ss.append(lax.dot_general(qs, kw, NT, preferred_element_type=F32) + jnp.concatenate([b, b], axis=0))
                ms = [jnp.max(s, axis=1, keepdims=True) for s in ss]
                ps = [jnp.exp(s - m) for s, m in zip(ss, ms)]
                ls = [jnp.sum(p, axis=1, keepdims=True) for p in ps]
                os_ = []
                for u in range(U):
                    vw = jnp.concatenate([v_ref[prev_rows[u], :], v_ref[rows[u], :]], 0).astype(BF16)
                    os_.append(jnp.dot(ps[u].astype(BF16), vw, preferred_element_type=F32))
                for u in range(U):
                    o_g, m_g, l_g = _unstack(os_[u]), _unstack_columns(ms[u]), _unstack_columns(ls[u])
                    r = rows[u]
                    if first_pattern:
                        m_new, l_new, acc = m_g, l_g, o_g
                    else:
                        m_old = m_ref[r, :]
                        m_new = jnp.maximum(m_old, m_g)
                        alpha, beta = jnp.exp(m_old - m_new), jnp.exp(m_g - m_new)
                        l_new = l_scr[r, :] * alpha + l_g * beta
                        acc = y_ref[r, :] * alpha + o_g * beta
                    if last_pattern:
                        y_ref[r, :] = acc / l_new
                        m_ref[r, :] = m_new + jnp.log(l_new)
                    else:
                        y_ref[r, :] = acc
                        m_ref[r, :] = m_new
                        l_scr[r, :] = l_new
                return carry

            lax.fori_loop(0, d * nb // U, step, 0)

    col = pl.BlockSpec((S, 2 * HEAD), lambda j: (0, j))
    return pl.pallas_call(
        body, name="attn_fwd", grid=(q.shape[1] // (2 * HEAD),),
        in_specs=[col, col, col], out_specs=[col, col],
        out_shape=[jax.ShapeDtypeStruct(q.shape, F32)] * 2,
        scratch_shapes=[pltpu.VMEM((S, 2 * HEAD), F32), pltpu.VMEM((2, N_BACK, 2 * N_BACK), F32)],
        compiler_params=_params("parallel"),
    )(q, k, v)


def _attn_bwd(q, k, v, dy, lse, delta):
    S = q.shape[0]
    U = BWD_BLOCKS_PER_STEP

    def body(q_ref, k_ref, v_ref, dy_ref, lse_ref, delta_ref, dq_ref, dk_ref, dv_ref, bias):
        _fill_band_bias(bias)
        dk_ref[...] = jnp.zeros_like(dk_ref)
        dv_ref[...] = jnp.zeros_like(dv_ref)
        for g, d in enumerate(DILATIONS):
            nb = S // d // N_BACK

            def step(i, carry, d=d, nb=nb, g=g):
                blocks = [_block_starts(U * i + u, nb, d) for u in range(U)]
                rows = [_strided(own, N_BACK, d) for own, _, _ in blocks]
                prev_rows = [_strided(prev, N_BACK, d) for _, prev, _ in blocks]
                kws = [jnp.concatenate([k_ref[prev_rows[u], :], k_ref[rows[u], :]], 0).astype(BF16) for u in range(U)]
                vws = [jnp.concatenate([v_ref[prev_rows[u], :], v_ref[rows[u], :]], 0).astype(BF16) for u in range(U)]
                qss = [_stack_heads(q_ref[rows[u], :].astype(BF16)) for u in range(U)]
                doss = [_stack_heads(dy_ref[rows[u], :].astype(BF16)) for u in range(U)]
                ss, dps = [], []
                for u, (_, _, n) in enumerate(blocks):
                    b = bias[jnp.minimum(n, 1)]
                    ss.append(lax.dot_general(qss[u], kws[u], NT, preferred_element_type=F32) + jnp.concatenate([b, b], axis=0))
                    dps.append(lax.dot_general(doss[u], vws[u], NT, preferred_element_type=F32))
                ps = [jnp.exp(ss[u] - _head_columns(lse_ref[rows[u], :])) for u in range(U)]
                dss = [(ps[u] * (dps[u] - _head_columns(delta_ref[rows[u], :]))).astype(BF16) for u in range(U)]
                pbs = [p.astype(BF16) for p in ps]
                dqs = [jnp.dot(dss[u], kws[u], preferred_element_type=F32) for u in range(U)]
                dkws = [lax.dot_general(dss[u], qss[u], TN, preferred_element_type=F32) for u in range(U)]
                dvws = [lax.dot_general(pbs[u], doss[u], TN, preferred_element_type=F32) for u in range(U)]
                for u in range(U):
                    dq = _unstack(dqs[u])
                    if g == 0:
                        dq_ref[rows[u], :] = dq
                    else:
                        dq_ref[rows[u], :] += dq
                    dk_ref[prev_rows[u], :] += dkws[u][:N_BACK]
                    dv_ref[prev_rows[u], :] += dvws[u][:N_BACK]
                    dk_ref[rows[u], :] += dkws[u][N_BACK:]
                    dv_ref[rows[u], :] += dvws[u][N_BACK:]
                return carry

            lax.fori_loop(0, d * nb // U, step, 0)

    col = pl.BlockSpec((S, 2 * HEAD), lambda j: (0, j))
    return pl.pallas_call(
        body, name="attn_bwd", grid=(q.shape[1] // (2 * HEAD),),
        in_specs=[col] * 6, out_specs=[col] * 3,
        out_shape=[jax.ShapeDtypeStruct(q.shape, F32)] * 3,
        scratch_shapes=[pltpu.VMEM((2, N_BACK, 2 * N_BACK), F32)],
        compiler_params=_params("parallel"),
    )(q, k, v, dy, lse, delta)


def _shift_down(z, before, k):
    row = lax.broadcasted_iota(jnp.int32, z.shape, 0)
    out = pltpu.roll(z, k, 0)
    for i in range(k):
        out = jnp.where(row == i, before[8 - k + i:8 - k + i + 1, :], out)
    return out


def _shift_up(z, after, k):
    rows = z.shape[0]
    row = lax.broadcasted_iota(jnp.int32, z.shape, 0)
    out = pltpu.roll(z, rows - k, 0)
    for i in range(k):
        out = jnp.where(row == rows - k + i, after[i:i + 1, :], out)
    return out


def _conv_fwd(bcu, before, is_first, w):
    b, c, u = bcu[:, 0:CONV_W], bcu[:, CONV_W:2 * CONV_W], bcu[:, 2 * CONV_W:3 * CONV_W]
    z = c * u
    zb = jnp.where(is_first, 0.0, before[:, CONV_W:2 * CONV_W] * before[:, 2 * CONV_W:3 * CONV_W])
    z1, z2 = _shift_down(z, zb, 1), _shift_down(z, zb, 2)
    cv = w[0:1, :] * z2 + w[1:2, :] * z1 + w[2:3, :] * z
    return b, c, u, z, z1, z2, cv


def _halo_before(tm, width):
    return pl.BlockSpec((8, width), lambda i: (jnp.maximum(i * (tm // 8) - 1, 0), 0))


def _halo_after(tm, width, S):
    return pl.BlockSpec((8, width), lambda i: (jnp.minimum((i + 1) * (tm // 8), S // 8 - 1), 0))


def _mix_fwd(ya, bcu, qx, mkv, conv_w, g_a, g_c, g_x, w_out, g_post, x, tm):
    S = x.shape[0]

    def body(ya_ref, bcu_ref, before_ref, qx_ref, mkv_ref, cw_ref, ga_ref, gc_ref, gx_ref,
             wo_ref, gp_ref, x_ref, yx_ref, ycat_ref, y2_ref, x1_ref):
        ya = ya_ref[...]
        b, _, _, _, _, _, cv = _conv_fwd(bcu_ref[...], before_ref[...], pl.program_id(0) == 0, cw_ref[...])
        yc = b * cv

        qxb, mkvb = qx_ref[...], mkv_ref[...]
        for hd in range(XATTN_W // HEAD):
            sl = slice(HEAD * hd, HEAD * (hd + 1))
            s = lax.dot_general(qxb[:, sl], mkvb[:, sl], NT, preferred_element_type=F32) * SCALE
            mx = jnp.max(s, axis=1, keepdims=True)
            p = jnp.exp(s - mx)
            l = jnp.sum(p, axis=1, keepdims=True)
            vm = mkvb[:, XATTN_W + HEAD * hd:XATTN_W + HEAD * (hd + 1)]
            yx_ref[:, sl] = jnp.dot(p.astype(BF16), vm, preferred_element_type=F32) / l
        yx = yx_ref[...]

        ycat_ref[:, 0:ATTN_W] = (_rms_hat(ya)[0] * ga_ref[...]).astype(BF16)
        ycat_ref[:, ATTN_W:ATTN_W + CONV_W] = (_rms_hat(yc)[0] * gc_ref[...]).astype(BF16)
        ycat_ref[:, ATTN_W + CONV_W:D_MODEL] = (_rms_hat(yx)[0] * gx_ref[...]).astype(BF16)
        y2 = jnp.dot(ycat_ref[...], wo_ref[...], preferred_element_type=F32)
        y2_ref[...] = y2
        x1_ref[...] = x_ref[...] + _rms_hat(y2)[0] * gp_ref[...]

    n_mem = mkv.shape[0]
    return pl.pallas_call(
        body, name="mix_fwd", grid=(S // tm,),
        in_specs=[_rows(tm, ATTN_W), _rows(tm, 3 * CONV_W), _halo_before(tm, 3 * CONV_W), _rows(tm, XATTN_W),
                  _resident((n_mem, 2 * XATTN_W)), _resident((3, CONV_W)), _resident((1, ATTN_W)),
                  _resident((1, CONV_W)), _resident((1, XATTN_W)), _resident((D_MODEL, D_MODEL)),
                  _resident((1, D_MODEL)), _rows(tm, D_MODEL)],
        out_specs=[_rows(tm, XATTN_W), _rows(tm, D_MODEL), _rows(tm, D_MODEL), _rows(tm, D_MODEL)],
        out_shape=[jax.ShapeDtypeStruct((S, XATTN_W), F32), jax.ShapeDtypeStruct((S, D_MODEL), BF16),
                   jax.ShapeDtypeStruct((S, D_MODEL), F32), jax.ShapeDtypeStruct((S, D_MODEL), F32)],
        compiler_params=_params("parallel"),
    )(ya, bcu, bcu, qx, mkv, conv_w, g_a, g_c, g_x, w_out, g_post, x)


def _mlp_fwd_bwd(x1, target, g_pre, g_post, w_up, w_down, tm):
    S = x1.shape[0]
    n_ff = D_FF // SHARD_FF

    def body(x1_ref, t_ref, gpre_ref, gpost_ref, wup_ref, wdn_ref,
             h2_ref, f_ref, du_ref, df2_ref, dx1_ref, dgpre_ref, dgpost_ref, loss_ref, u_scr):
        @pl.when(pl.program_id(0) == 0)
        def _():
            dgpre_ref[...] = jnp.zeros_like(dgpre_ref)
            dgpost_ref[...] = jnp.zeros_like(dgpost_ref)
            loss_ref[...] = jnp.zeros_like(loss_ref)

        x1 = x1_ref[...]
        x1hat, r1 = _rms_hat(x1)
        h2 = (x1hat * gpre_ref[...]).astype(BF16)
        h2_ref[...] = h2
        f2 = jnp.zeros((tm, D_MODEL), F32)
        for j in range(n_ff):
            cols = slice(SHARD_FF * j, SHARD_FF * (j + 1))
            u = jnp.maximum(jnp.dot(h2, wup_ref[j], preferred_element_type=F32), 0.0)
            u_scr[:, cols] = u
            f = (u * u).astype(BF16)
            f_ref[:, cols] = f
            f2 = f2 + jnp.dot(f, wdn_ref[cols, :], preferred_element_type=F32)
        f2hat, r2 = _rms_hat(f2)
        err = x1 + f2hat * gpost_ref[...] - t_ref[...]
        loss_ref[...] += 0.5 * jnp.sum(jnp.mean(err * err, axis=-1, keepdims=True), axis=0, keepdims=True)
        dx2 = err * (1.0 / D_MODEL)
        dgpost_ref[...] += jnp.sum(dx2 * f2hat, axis=0, keepdims=True)
        df2 = _rms_bwd(f2hat, r2, gpost_ref[...], dx2).astype(BF16)
        df2_ref[...] = df2
        dh2 = jnp.zeros((tm, D_MODEL), F32)
        for j in range(n_ff):
            cols = slice(SHARD_FF * j, SHARD_FF * (j + 1))
            df = lax.dot_general(df2, wdn_ref[cols, :], NT, preferred_element_type=F32)
            du = (2.0 * u_scr[:, cols] * df).astype(BF16)
            du_ref[:, cols] = du
            dh2 = dh2 + lax.dot_general(du, wup_ref[j], NT, preferred_element_type=F32)
        dgpre_ref[...] += jnp.sum(dh2 * x1hat, axis=0, keepdims=True)
        dx1_ref[...] = dx2 + _rms_bwd(x1hat, r1, gpre_ref[...], dh2)

    acc = pl.BlockSpec((1, D_MODEL), lambda i: (0, 0))
    return pl.pallas_call(
        body, name="mlp_fwd_bwd", grid=(S // tm,),
        in_specs=[_rows(tm, D_MODEL), _rows(tm, D_MODEL), _resident((1, D_MODEL)), _resident((1, D_MODEL)),
                  _resident((n_ff, D_MODEL, SHARD_FF)), _resident((D_FF, D_MODEL))],
        out_specs=[_rows(tm, D_MODEL), _rows(tm, D_FF), _rows(tm, D_FF), _rows(tm, D_MODEL), _rows(tm, D_MODEL),
                   acc, acc, pl.BlockSpec((1, 1), lambda i: (0, 0))],
        out_shape=[jax.ShapeDtypeStruct((S, D_MODEL), BF16), jax.ShapeDtypeStruct((S, D_FF), BF16),
                   jax.ShapeDtypeStruct((S, D_FF), BF16), jax.ShapeDtypeStruct((S, D_MODEL), BF16),
                   jax.ShapeDtypeStruct((S, D_MODEL), F32), jax.ShapeDtypeStruct((1, D_MODEL), F32),
                   jax.ShapeDtypeStruct((1, D_MODEL), F32), jax.ShapeDtypeStruct((1, 1), F32)],
        scratch_shapes=[pltpu.VMEM((tm, D_FF), F32)],
        compiler_params=_params("arbitrary"),
    )(x1, target, g_pre, g_post, w_up, w_down)


def _weight_grad(name, a, b, tk, tn, ts, rows_sharded):
    S, K = a.shape
    N = b.shape[1]
    nk, nn = K // tk, N // tn
    if rows_sharded:
        assert nk == 2 * N_CHIPS and nn == 1
        out_index = lambda k, n, s: (k % 2, k // 2, 0, 0)
    else:
        assert nk == 2 and nn == N_CHIPS
        out_index = lambda k, n, s: (k, n, 0, 0)

    def body(a_ref, b_ref, o_ref):
        @pl.when(pl.program_id(2) == 0)
        def _():
            o_ref[...] = jnp.zeros_like(o_ref)

        o_ref[0, 0] += lax.dot_general(a_ref[...], b_ref[...], TN, preferred_element_type=F32)

    return pl.pallas_call(
        body, name=name, grid=(nk, nn, S // ts),
        in_specs=[pl.BlockSpec((ts, tk), lambda k, n, s: (s, k)), pl.BlockSpec((ts, tn), lambda k, n, s: (s, n))],
        out_specs=pl.BlockSpec((1, 1, tk, tn), out_index),
        out_shape=jax.ShapeDtypeStruct((2, N_CHIPS, tk, tn), F32),
        compiler_params=_params("parallel", "parallel", "arbitrary"),
    )(a, b)


def _mix_bwd(dx1, y2, ya, yx, bcu, conv_w, g_a, g_c, g_x, w_out, g_post, tm):
    S = dx1.shape[0]

    def body(dx1_ref, y2_ref, ya_ref, yx_ref, bcu_ref, before_ref, cw_ref, ga_ref, gc_ref, gx_ref, wo_ref, gp_ref,
             dy2_ref, dya_ref, delta_ref, dycx_ref, dgp_ref, dga_ref, dgc_ref, dgx_ref):
        @pl.when(pl.program_id(0) == 0)
        def _():
            for ref in (dgp_ref, dga_ref, dgc_ref, dgx_ref):
                ref[...] = jnp.zeros_like(ref)

        dx1 = dx1_ref[...]
        y2hat, r2 = _rms_hat(y2_ref[...])
        dgp_ref[...] += jnp.sum(dx1 * y2hat, axis=0, keepdims=True)
        dy2 = _rms_bwd(y2hat, r2, gp_ref[...], dx1).astype(BF16)
        dy2_ref[...] = dy2
        dycat = lax.dot_general(dy2, wo_ref[...], NT, preferred_element_type=F32)

        d_na = dycat[:, 0:ATTN_W]
        ya = ya_ref[...]
        yahat, ra = _rms_hat(ya)
        dga_ref[...] += jnp.sum(d_na * yahat, axis=0, keepdims=True)
        dya = _rms_bwd(yahat, ra, ga_ref[...], d_na)
        dya_ref[...] = dya
        prod = dya * ya
        hi = prod.astype(BF16)
        lo = (prod - hi.astype(F32)).astype(BF16)
        head_of = lambda axis: lax.shift_right_logical(lax.broadcasted_iota(jnp.int32, (ATTN_W, ATTN_W), axis),
                                                       HEAD.bit_length() - 1)
        same_head = head_of(0) == head_of(1)
        ones = jnp.where(same_head, 1.0, 0.0).astype(BF16)
        delta_ref[...] = (jnp.dot(hi, ones, preferred_element_type=F32) + jnp.dot(lo, ones, preferred_element_type=F32))

        b, _, _, _, _, _, cv = _conv_fwd(bcu_ref[...], before_ref[...], pl.program_id(0) == 0, cw_ref[...])
        d_nc = dycat[:, ATTN_W:ATTN_W + CONV_W]
        ychat, rc = _rms_hat(b * cv)
        dgc_ref[...] += jnp.sum(d_nc * ychat, axis=0, keepdims=True)
        dycx_ref[:, 0:CONV_W] = _rms_bwd(ychat, rc, gc_ref[...], d_nc)

        d_nx = dycat[:, ATTN_W + CONV_W:D_MODEL]
        yxhat, rx = _rms_hat(yx_ref[...])
        dgx_ref[...] += jnp.sum(d_nx * yxhat, axis=0, keepdims=True)
        dycx_ref[:, CONV_W:CONV_W + XATTN_W] = _rms_bwd(yxhat, rx, gx_ref[...], d_nx)

    acc = lambda w: pl.BlockSpec((1, w), lambda i: (0, 0))
    return pl.pallas_call(
        body, name="mix_bwd", grid=(S // tm,),
        in_specs=[_rows(tm, D_MODEL), _rows(tm, D_MODEL), _rows(tm, ATTN_W), _rows(tm, XATTN_W),
                  _rows(tm, 3 * CONV_W), _halo_before(tm, 3 * CONV_W), _resident((3, CONV_W)),
                  _resident((1, ATTN_W)), _resident((1, CONV_W)), _resident((1, XATTN_W)),
                  _resident((D_MODEL, D_MODEL)), _resident((1, D_MODEL))],
        out_specs=[_rows(tm, D_MODEL), _rows(tm, ATTN_W), _rows(tm, ATTN_W), _rows(tm, CONV_W + XATTN_W),
                   acc(D_MODEL), acc(ATTN_W), acc(CONV_W), acc(XATTN_W)],
        out_shape=[jax.ShapeDtypeStruct((S, D_MODEL), BF16), jax.ShapeDtypeStruct((S, ATTN_W), F32),
                   jax.ShapeDtypeStruct((S, ATTN_W), F32),
                   jax.ShapeDtypeStruct((S, CONV_W + XATTN_W), F32), jax.ShapeDtypeStruct((1, D_MODEL), F32),
                   jax.ShapeDtypeStruct((1, ATTN_W), F32), jax.ShapeDtypeStruct((1, CONV_W), F32),
                   jax.ShapeDtypeStruct((1, XATTN_W), F32)],
        compiler_params=_params("arbitrary"),
    )(dx1, y2, ya, yx, bcu, bcu, conv_w, g_a, g_c, g_x, w_out, g_post)


def _conv_xattn_bwd(dycx, bcu, qx, mkv, conv_w, tm):
    S = dycx.shape[0]
    n_mem = mkv.shape[0]
    n_tiles = S // tm

    def body(d_ref, dafter_ref, bcu_ref, before_ref, after_ref, qx_ref, mkv_ref, cw_ref,
             tail_ref, dmkv_ref, dcw_ref):
        i = pl.program_id(0)

        @pl.when(i == 0)
        def _():
            dmkv_ref[...] = jnp.zeros_like(dmkv_ref)
            dcw_ref[...] = jnp.zeros_like(dcw_ref)

        w = cw_ref[...]
        b, c, u, z, z1, z2, cv = _conv_fwd(bcu_ref[...], before_ref[...], i == 0, w)
        dyc = d_ref[:, 0:CONV_W]
        dcv = dyc * b
        dcv_after = jnp.where(i == n_tiles - 1, 0.0, dafter_ref[:, 0:CONV_W] * after_ref[:, 0:CONV_W])
        dz = w[2:3, :] * dcv + w[1:2, :] * _shift_up(dcv, dcv_after, 1) + w[0:1, :] * _shift_up(dcv, dcv_after, 2)
        dcw_ref[0:1, :] += jnp.sum(dcv * z2, axis=0, keepdims=True)
        dcw_ref[1:2, :] += jnp.sum(dcv * z1, axis=0, keepdims=True)
        dcw_ref[2:3, :] += jnp.sum(dcv * z, axis=0, keepdims=True)
        tail_ref[:, 0:CONV_W] = (dyc * cv).astype(BF16)
        tail_ref[:, CONV_W:2 * CONV_W] = (dz * u).astype(BF16)
        tail_ref[:, 2 * CONV_W:3 * CONV_W] = (dz * c).astype(BF16)

        qxb, mkvb = qx_ref[...], mkv_ref[...]
        for hd in range(XATTN_W // HEAD):
            sl = slice(HEAD * hd, HEAD * (hd + 1))
            vsl = slice(XATTN_W + HEAD * hd, XATTN_W + HEAD * (hd + 1))
            s = lax.dot_general(qxb[:, sl], mkvb[:, sl], NT, preferred_element_type=F32) * SCALE
            e = jnp.exp(s - jnp.max(s, axis=1, keepdims=True))
            p = e / jnp.sum(e, axis=1, keepdims=True)
            dob = d_ref[:, CONV_W + HEAD * hd:CONV_W + HEAD * (hd + 1)].astype(BF16)
            dp = lax.dot_general(dob, mkvb[:, vsl], NT, preferred_element_type=F32)
            ds = (p * (dp - jnp.sum(p * dp, axis=1, keepdims=True)) * SCALE).astype(BF16)
            tail_ref[:, 3 * CONV_W + HEAD * hd:3 * CONV_W + HEAD * (hd + 1)] = jnp.dot(
                ds, mkvb[:, sl], preferred_element_type=F32).astype(BF16)
            dmkv_ref[:, sl] += lax.dot_general(ds, qxb[:, sl], TN, preferred_element_type=F32)
            dmkv_ref[:, vsl] += lax.dot_general(p.astype(BF16), dob, TN, preferred_element_type=F32)

    width = CONV_W + XATTN_W
    return pl.pallas_call(
        body, name="conv_xattn_bwd", grid=(n_tiles,),
        in_specs=[_rows(tm, width), _halo_after(tm, width, S), _rows(tm, 3 * CONV_W), _halo_before(tm, 3 * CONV_W),
                  _halo_after(tm, 3 * CONV_W, S), _rows(tm, XATTN_W), _resident((n_mem, 2 * XATTN_W)),
                  _resident((3, CONV_W))],
        out_specs=[_rows(tm, 3 * CONV_W + XATTN_W), pl.BlockSpec((n_mem, 2 * XATTN_W), lambda i: (0, 0)),
                   pl.BlockSpec((3, CONV_W), lambda i: (0, 0))],
        out_shape=[jax.ShapeDtypeStruct((S, 3 * CONV_W + XATTN_W), BF16),
                   jax.ShapeDtypeStruct((n_mem, 2 * XATTN_W), F32), jax.ShapeDtypeStruct((3, CONV_W), F32)],
        compiler_params=_params("arbitrary"),
    )(dycx, dycx, bcu, bcu, bcu, qx, mkv, conv_w)


def _memkv_bwd(mem, g_mem, w_kv, dmkv):
    n_mem = mem.shape[0]
    half = D_MODEL // N_CHIPS // 2

    def body(mem_ref, g_ref, w_ref, d_ref, dw_ref, dg_ref):
        mhat, _ = _rms_hat(mem_ref[...])
        mn = (mhat * g_ref[...]).astype(BF16)
        d = d_ref[...].astype(BF16)
        for k in range(2 * N_CHIPS):
            dw_ref[k % 2, k // 2] = lax.dot_general(mn[:, half * k:half * (k + 1)], d, TN, preferred_element_type=F32)
        dmn = lax.dot_general(d, w_ref[...], NT, preferred_element_type=F32)
        dg_ref[...] = jnp.sum(dmn * mhat, axis=0, keepdims=True)

    return pl.pallas_call(
        body, name="memkv_bwd",
        out_shape=[jax.ShapeDtypeStruct((2, N_CHIPS, half, 2 * XATTN_W), F32), jax.ShapeDtypeStruct((1, D_MODEL), F32)],
        compiler_params=pltpu.CompilerParams(vmem_limit_bytes=VMEM_LIMIT_V7X),
    )(mem, g_mem, w_kv, dmkv)


def _in_proj_bwd(dqkv, tail, cos, sin, w_in, x, g, dx1, tm):
    S = x.shape[0]

    def body(dq_ref, dk_ref, dv_ref, tail_ref, cos_ref, sin_ref, w_ref, x_ref, g_ref, dx1_ref, dproj_ref, dx_ref, dg_ref):
        @pl.when(pl.program_id(0) == 0)
        def _():
            dg_ref[...] = jnp.zeros_like(dg_ref)

        c, s = cos_ref[...], sin_ref[...]
        for j in range(ATTN_W // 128):
            cols = slice(128 * j, 128 * (j + 1))
            dproj_ref[:, cols] = _rope128(dq_ref[:, cols] * SCALE, c, s, True).astype(BF16)
            dproj_ref[:, ATTN_W + 128 * j:ATTN_W + 128 * (j + 1)] = _rope128(dk_ref[:, cols], c, s, True).astype(BF16)
        dproj_ref[:, 2 * ATTN_W:3 * ATTN_W] = dv_ref[...].astype(BF16)
        dproj_ref[:, 3 * ATTN_W:PROJ_W] = tail_ref[...]
        dh = jnp.zeros((tm, D_MODEL), F32)
        for j in range(N_CHIPS):
            dh = dh + lax.dot_general(dproj_ref[:, SHARD_IN * j:SHARD_IN * (j + 1)], w_ref[j], NT,
                                      preferred_element_type=F32)
        xhat, r = _rms_hat(x_ref[...])
        dg_ref[...] += jnp.sum(dh * xhat, axis=0, keepdims=True)
        dx_ref[...] = dx1_ref[...] + _rms_bwd(xhat, r, g_ref[...], dh)

    return pl.pallas_call(
        body, name="in_proj_bwd", grid=(S // tm,),
        in_specs=[_rows(tm, ATTN_W)] * 3 + [_rows(tm, PROJ_W - 3 * ATTN_W), _rows(tm, 128), _rows(tm, 128),
                  _resident((N_CHIPS, D_MODEL, SHARD_IN)), _rows(tm, D_MODEL), _resident((1, D_MODEL)),
                  _rows(tm, D_MODEL)],
        out_specs=[_rows(tm, PROJ_W), _rows(tm, D_MODEL), pl.BlockSpec((1, D_MODEL), lambda i: (0, 0))],
        out_shape=[jax.ShapeDtypeStruct((S, PROJ_W), BF16), jax.ShapeDtypeStruct((S, D_MODEL), F32),
                   jax.ShapeDtypeStruct((1, D_MODEL), F32)],
        compiler_params=_params("arbitrary"),
    )(*dqkv, tail, cos, sin, w_in, x, g, dx1)


def _row_tile(rows):
    return ROW_TILE if rows % ROW_TILE == 0 else rows


def _pair_sum_bf16(name, a, b):
    n, rows, cols = a.shape
    tr = _row_tile(rows)

    def body(a_ref, b_ref, o_ref):
        o_ref[...] = (a_ref[...] + b_ref[...]).astype(BF16)

    spec = pl.BlockSpec((1, tr, cols), lambda s, i: (s, i, 0))
    return pl.pallas_call(
        body, name=name, grid=(n, rows // tr), in_specs=[spec, spec], out_specs=spec,
        out_shape=jax.ShapeDtypeStruct(a.shape, BF16), compiler_params=_params("parallel", "parallel"),
    )(a, b)


def _final_sum(name, own, sibling, others):
    rows, cols = own.shape
    tr = _row_tile(rows)

    def body(own_ref, sib_ref, o0, o1, o2, out_ref):
        acc = own_ref[...] + sib_ref[...]
        for o in (o0, o1, o2):
            acc = acc + o[0].astype(F32)
        out_ref[...] = acc

    spec = pl.BlockSpec((tr, cols), lambda i: (i, 0))
    other = lambda k: pl.BlockSpec((1, tr, cols), lambda i: (k, i, 0))
    return pl.pallas_call(
        body, name=name, grid=(rows // tr,), in_specs=[spec, spec, other(0), other(1), other(2)], out_specs=spec,
        out_shape=jax.ShapeDtypeStruct(own.shape, F32), compiler_params=_params("parallel"),
    )(own, sibling, others, others, others)


def _adamw(name, w, g, m, v):
    rows, cols = w.shape
    tr = _row_tile(rows)

    def body(w_ref, g_ref, m_ref, v_ref, d_ref, nm_ref, nv_ref):
        g = g_ref[...]
        m = ADAM_B1 * m_ref[...] + (1.0 - ADAM_B1) * g
        v = ADAM_B2 * v_ref[...] + (1.0 - ADAM_B2) * (g * g)
        m_hat = m / (1.0 - ADAM_B1 ** ADAM_STEP)
        v_hat = v / (1.0 - ADAM_B2 ** ADAM_STEP)
        d_ref[...] = -ADAM_LR * (m_hat / (jnp.sqrt(v_hat) + ADAM_EPS) + ADAM_WD * w_ref[...])
        nm_ref[...] = m
        nv_ref[...] = v

    spec = pl.BlockSpec((tr, cols), lambda i: (i, 0))
    return pl.pallas_call(
        body, name=name, grid=(rows // tr,), in_specs=[spec] * 4, out_specs=[spec] * 3,
        out_shape=[jax.ShapeDtypeStruct(w.shape, F32)] * 3, compiler_params=_params("parallel"),
    )(w, g, m, v)


def _sum_blocks(name, blocks):
    n, rows, cols = blocks.shape

    def body(b_ref, o_ref):
        acc = b_ref[0]
        for k in range(1, n):
            acc = acc + b_ref[k]
        o_ref[...] = acc

    return pl.pallas_call(body, name=name, out_shape=jax.ShapeDtypeStruct((rows, cols), F32))(blocks)


def _place():
    return lax.axis_index("x"), lax.axis_index("y"), lax.axis_index("c")


def _other_chips(x, y):
    return [(1 - x, y), (x, 1 - y), (1 - x, 1 - y)]


def _small_allgather(name, block):
    rows, cols = block.shape
    relations = [(dx, dy, dc) for dx in (0, 1) for dy in (0, 1) for dc in (0, 1) if (dx, dy, dc) != (0, 0, 0)]

    def body(x_ref, out_ref, send_sems, recv_sems, local_sem):
        x, y, c = _place()

        def peer(rel):
            return (1 - x if rel[0] else x, 1 - y if rel[1] else y, 1 - c if rel[2] else c)

        def index(p):
            return 4 * p[0] + 2 * p[1] + p[2]

        def copy(k, origin, to):
            return pltpu.make_async_remote_copy(
                src_ref=x_ref, dst_ref=out_ref.at[index(origin)], send_sem=send_sems.at[k], recv_sem=recv_sems.at[k],
                device_id=to, device_id_type=MESH)

        mine = pltpu.make_async_copy(x_ref, out_ref.at[index((x, y, c))], local_sem)
        mine.start()
        sends = [copy(k, (x, y, c), peer(rel)) for k, rel in enumerate(relations)]
        for cp in sends:
            cp.start()
        for k, rel in enumerate(relations):
            copy(k, peer(rel), (x, y, c)).wait_recv()
        for cp in sends:
            cp.wait_send()
        mine.wait()

    return pl.pallas_call(
        body, name=name, out_shape=jax.ShapeDtypeStruct((8, rows, cols), F32),
        in_specs=[pl.BlockSpec(memory_space=pltpu.VMEM)], out_specs=pl.BlockSpec(memory_space=pltpu.VMEM),
        scratch_shapes=[pltpu.SemaphoreType.DMA((7,)), pltpu.SemaphoreType.DMA((7,)), pltpu.SemaphoreType.DMA],
    )(block)


def _weights_allgather(shards):
    n = len(shards)

    def body(*refs):
        ins, outs, stage = refs[:n], refs[n:2 * n], refs[2 * n:3 * n]
        send_sems, recv_sems, local_sems = refs[3 * n:]
        x, y, c = _place()
        me, sibling = (x, y, c), (x, y, 1 - c)
        chips = _other_chips(x, y)
        chip_index = lambda chip: 2 * chip[0] + chip[1]

        def copy(a, k, chip, half, to, src=None):
            place = outs[a].at[chip_index(chip), half]
            return pltpu.make_async_remote_copy(
                src_ref=place if src is None else src, dst_ref=place, send_sem=send_sems.at[6 * a + k],
                recv_sem=recv_sems.at[6 * a + k], device_id=to, device_id_type=MESH)

        load = [pltpu.make_async_copy(ins[a], stage[a], local_sems.at[a]) for a in range(n)]
        local = [pltpu.make_async_copy(stage[a], outs[a].at[chip_index((x, y))], local_sems.at[a]) for a in range(n)]
        for cp in load:
            cp.start()
        first = [copy(a, k, (x, y), c, (*chip, c), src=ins[a].at[c]) for a in range(n) for k, chip in enumerate(chips)]
        for cp in first:
            cp.start()
        for a in range(n):
            load[a].wait()
            local[a].start()
        passed = []
        for a in range(n):
            for k, chip in enumerate(chips):
                copy(a, k, chip, c, me).wait_recv()
                passed.append(copy(a, 3 + k, chip, c, sibling))
                passed[-1].start()
        for a in range(n):
            for k, chip in enumerate(chips):
                copy(a, 3 + k, chip, 1 - c, me).wait_recv()
        for cp in first + passed:
            cp.wait_send()
        for cp in local:
            cp.wait()

    any_spec = pl.BlockSpec(memory_space=pl.ANY)
    return pl.pallas_call(
        body, name="weights_allgather",
        out_shape=[jax.ShapeDtypeStruct((N_CHIPS,) + s.shape, s.dtype) for s in shards],
        in_specs=[any_spec] * n, out_specs=[any_spec] * n,
        scratch_shapes=[pltpu.VMEM(s.shape, s.dtype) for s in shards]
        + [pltpu.SemaphoreType.DMA((6 * n,)), pltpu.SemaphoreType.DMA((6 * n,)), pltpu.SemaphoreType.DMA((n,))],
        compiler_params=pltpu.CompilerParams(vmem_limit_bytes=VMEM_LIMIT_V7X),
    )(*shards)


def _to_sibling(name, arrays, pick_other_half):
    n = len(arrays)

    def body(*refs):
        ins, outs = refs[:n], refs[n:2 * n]
        send_sems, recv_sems = refs[2 * n:]
        x, y, c = _place()
        copies = [pltpu.make_async_remote_copy(
            src_ref=ins[a].at[1 - c] if pick_other_half else ins[a], dst_ref=outs[a], send_sem=send_sems.at[a],
            recv_sem=recv_sems.at[a], device_id=(x, y, 1 - c), device_id_type=MESH) for a in range(n)]
        for cp in copies:
            cp.start()
        for cp in copies:
            cp.wait()

    any_spec = pl.BlockSpec(memory_space=pl.ANY)
    return pl.pallas_call(
        body, name=name,
        out_shape=[jax.ShapeDtypeStruct(a.shape[1:] if pick_other_half else a.shape, a.dtype) for a in arrays],
        in_specs=[any_spec] * n, out_specs=[any_spec] * n,
        scratch_shapes=[pltpu.SemaphoreType.DMA((n,)), pltpu.SemaphoreType.DMA((n,))],
    )(*arrays)


def _to_other_chips(partials):
    n = len(partials)

    def body(*refs):
        ins, outs = refs[:n], refs[n:2 * n]
        send_sems, recv_sems = refs[2 * n:]
        x, y, c = _place()
        copies = [pltpu.make_async_remote_copy(
            src_ref=ins[a].at[2 * chip[0] + chip[1]], dst_ref=outs[a].at[k], send_sem=send_sems.at[3 * a + k],
            recv_sem=recv_sems.at[3 * a + k], device_id=(*chip, c), device_id_type=MESH)
            for a in range(n) for k, chip in enumerate(_other_chips(x, y))]
        for cp in copies:
            cp.start()
        for cp in copies:
            cp.wait()

    any_spec = pl.BlockSpec(memory_space=pl.ANY)
    return pl.pallas_call(
        body, name="grads_to_other_chips",
        out_shape=[jax.ShapeDtypeStruct((3,) + p.shape[1:], p.dtype) for p in partials],
        in_specs=[any_spec] * n, out_specs=[any_spec] * n,
        scratch_shapes=[pltpu.SemaphoreType.DMA((3 * n,)), pltpu.SemaphoreType.DMA((3 * n,))],
    )(*partials)


def _exchange_halves(halves):
    n = len(halves)

    def body(*refs):
        ins, outs, stage = refs[:n], refs[n:2 * n], refs[2 * n:3 * n]
        send_sems, recv_sems, local_sems = refs[3 * n:]
        x, y, c = _place()
        load = [pltpu.make_async_copy(ins[a], stage[a], local_sems.at[a]) for a in range(n)]
        local = [pltpu.make_async_copy(stage[a], outs[a].at[c], local_sems.at[a]) for a in range(n)]
        remote = [pltpu.make_async_remote_copy(
            src_ref=stage[a], dst_ref=outs[a].at[c], send_sem=send_sems.at[a], recv_sem=recv_sems.at[a],
            device_id=(x, y, 1 - c), device_id_type=MESH) for a in range(n)]
        for cp in load:
            cp.start()
        for a in range(n):
            load[a].wait()
            remote[a].start()
            local[a].start()
        for a in range(n):
            pltpu.make_async_remote_copy(
                src_ref=ins[a], dst_ref=outs[a].at[1 - c], send_sem=send_sems.at[a], recv_sem=recv_sems.at[a],
                device_id=(x, y, 1 - c), device_id_type=MESH).wait_recv()
        for cp in remote:
            cp.wait_send()
        for cp in local:
            cp.wait()

    any_spec = pl.BlockSpec(memory_space=pl.ANY)
    return pl.pallas_call(
        body, name="sums_to_sibling",
        out_shape=[jax.ShapeDtypeStruct((2,) + h.shape, h.dtype) for h in halves],
        in_specs=[any_spec] * n, out_specs=[any_spec] * n,
        scratch_shapes=[pltpu.VMEM(h.shape, h.dtype) for h in halves]
        + [pltpu.SemaphoreType.DMA((n,)), pltpu.SemaphoreType.DMA((n,)), pltpu.SemaphoreType.DMA((n,))],
        compiler_params=pltpu.CompilerParams(vmem_limit_bytes=VMEM_LIMIT_V7X),
    )(*halves)


def _reduce_scatter(grads):
    x, y, c = _place()
    j = 2 * x + y
    from_sibling = _to_sibling("grads_to_sibling", grads, True)
    mine = [lax.dynamic_index_in_dim(g, c, 0, keepdims=False) for g in grads]
    partials = [_pair_sum_bf16(f"chip_sum_{a}", mine[a], from_sibling[a]) for a in range(len(grads))]
    from_chips = _to_other_chips(partials)
    own = [lax.dynamic_index_in_dim(m, j, 0, keepdims=False) for m in mine]
    sib = [lax.dynamic_index_in_dim(s, j, 0, keepdims=False) for s in from_sibling]
    halves = [_final_sum(f"final_sum_{a}", own[a], sib[a], from_chips[a]) for a in range(len(grads))]
    return [t.reshape(2 * t.shape[1], t.shape[2]) for t in _exchange_halves(halves)]


def _rope_tables(positions):
    half = HEAD // 2
    inv_freq = jnp.float32(ROPE_THETA) ** (-(jnp.arange(half, dtype=F32) * 2.0 / HEAD))
    ang = positions.astype(F32)[:, None] * inv_freq
    cos, sin = jnp.cos(ang), jnp.sin(ang)
    return jnp.tile(cos, (1, 4)), jnp.tile(jnp.concatenate([-sin, sin], axis=1), (1, 2))


def _local_step(x, mem, positions, target, gains, w_in, w_kv, conv_w, w_out, w_up, w_down):
    g_pre_mix, g_mem, g_a, g_c, g_x, g_post_mix, g_pre_mlp, g_post_mlp = gains
    tm = ROW_TILE
    cos, sin = _rope_tables(positions)

    h, q, k, v, bcu, qx = _in_proj_fwd(x, g_pre_mix, w_in, cos, sin, tm)
    memn, mkv = _memkv_fwd(mem, g_mem, w_kv)
    ya, lse = _attn_fwd(q, k, v)
    yx, ycat, y2, x1 = _mix_fwd(ya, bcu, qx, mkv, conv_w, g_a, g_c, g_x, w_out, g_post_mix, x, tm)
    h2, f, du, df2, dx1, dg_pre_mlp, dg_post_mlp, loss = _mlp_fwd_bwd(x1, target, g_pre_mlp, g_post_mlp, w_up, w_down, tm)
    gw_down = _weight_grad("grad_w_down", f, df2, 512, D_MODEL, 512, True)
    gw_up = _weight_grad("grad_w_up", h2, du, 512, SHARD_FF, 512, False)

    dy2, dya, delta, dycx, dg_post_mix, dg_a, dg_c, dg_x = _mix_bwd(dx1, y2, ya, yx, bcu, conv_w, g_a, g_c, g_x,
                                                                  w_out, g_post_mix, tm)
    gw_out = _weight_grad("grad_w_out", ycat, dy2, D_MODEL // 8, D_MODEL, 512, True)
    tail, dmkv, g_conv = _conv_xattn_bwd(dycx, bcu, qx, mkv, conv_w, tm)
    gw_kv, dg_mem = _memkv_bwd(mem, g_mem, w_kv, dmkv)
    dqkv = _attn_bwd(q, k, v, dya, lse, delta)
    dproj, grad_x, dg_pre_mix = _in_proj_bwd(dqkv, tail, cos, sin, w_in, x, g_pre_mix, dx1, tm)
    gw_in = _weight_grad("grad_w_in", h, dproj, 512, SHARD_IN, 512, False)

    gain_grads = [dg_pre_mix, dg_mem, dg_a, dg_c, dg_x, dg_post_mix, dg_pre_mlp, dg_post_mlp]
    return loss, grad_x, [gw_in, gw_kv, gw_out, gw_up, gw_down], g_conv, gain_grads


def _pack_small(gains, conv):
    rows = [jnp.pad(g, ((0, 0), (0, D_MODEL - g.shape[1]))) for g in gains]
    rows.append(jnp.pad(conv, ((0, SMALL_ROWS - 8 - conv.shape[0]), (0, D_MODEL - conv.shape[1]))))
    return jnp.concatenate(rows, axis=0)


def _unpack_small(block, gain_widths, conv_width):
    gains = [block[i:i + 1, :w] for i, w in enumerate(gain_widths)]
    return gains, block[8:11, :conv_width]


def kernel(x, mem, positions, g_pre_mix, g_mem, w_in, w_mem_kv, conv_w, g_attn_out, g_conv_out, g_xattn_out, w_out, g_post_mix, g_pre_mlp, w_up, w_down, g_post_mlp, loss_target, m_g_pre_mix, m_g_mem, m_w_in, m_w_mem_kv, m_conv_w, m_g_attn_out, m_g_conv_out, m_g_xattn_out, m_w_out, m_g_post_mix, m_g_pre_mlp, m_w_up, m_w_down, m_g_post_mlp, v_g_pre_mix, v_g_mem, v_w_in, v_w_mem_kv, v_conv_w, v_g_attn_out, v_g_conv_out, v_g_xattn_out, v_w_out, v_g_post_mix, v_g_pre_mlp, v_w_up, v_w_down, v_g_post_mlp):
    cx, cy, cc = _place()
    chip = 2 * cx + cy
    gains = [g_pre_mix, g_mem, g_attn_out, g_conv_out, g_xattn_out, g_post_mix, g_pre_mlp, g_post_mlp]
    gains_m = [m_g_pre_mix, m_g_mem, m_g_attn_out, m_g_conv_out, m_g_xattn_out, m_g_post_mix, m_g_pre_mlp, m_g_post_mlp]
    gains_v = [v_g_pre_mix, v_g_mem, v_g_attn_out, v_g_conv_out, v_g_xattn_out, v_g_post_mix, v_g_pre_mlp, v_g_post_mlp]
    gain_widths = [g.shape[1] for g in gains]
    mats = [w_in[0], w_mem_kv[0], w_out[0], w_up[0], w_down[0]]
    mats_m = [m_w_in[0], m_w_mem_kv[0], m_w_out[0], m_w_up[0], m_w_down[0]]
    mats_v = [v_w_in[0], v_w_mem_kv[0], v_w_out[0], v_w_up[0], v_w_down[0]]

    shards = [w.astype(BF16).reshape(2, w.shape[0] // 2, w.shape[1]) for w in mats]
    full = [g.reshape(N_CHIPS, 2 * g.shape[2], g.shape[3]) for g in _weights_allgather(shards)]
    w_in_f, w_kv_f, w_out_f, w_up_f, w_down_f = full
    small = _small_allgather("conv_allgather", _pack_small(gains, conv_w[0]))
    conv_full = small[::2, 8:11, :conv_w.shape[2]].transpose(1, 0, 2).reshape(3, CONV_W)

    loss, grad_x, mat_grads, g_conv, gain_grads = _local_step(
        x[0], mem[0], positions[0], loss_target[0], gains, w_in_f, w_kv_f.reshape(D_MODEL, 2 * XATTN_W), conv_full,
        w_out_f.reshape(D_MODEL, D_MODEL), w_up_f, w_down_f.reshape(D_FF, D_MODEL))

    mat_sums = _reduce_scatter(mat_grads)
    small_sum = _sum_blocks("small_sum", _small_allgather("small_grads_allgather", _pack_small(gain_grads, g_conv)))
    gain_sums, conv_sum_full = _unpack_small(small_sum, gain_widths, CONV_W)
    conv_sum = lax.dynamic_slice_in_dim(conv_sum_full, chip * conv_w.shape[2], conv_w.shape[2], axis=1)

    mat_new = [_adamw(f"adamw_{a}", mats[a], mat_sums[a], mats_m[a], mats_v[a]) for a in range(len(mats))]
    pack = lambda gs, cv: _pack_small(gs, cv)
    small_new = _adamw("adamw_small", pack(gains, conv_w[0]), pack(gain_sums, conv_sum), pack(gains_m, m_conv_w[0]),
                       pack(gains_v, v_conv_w[0]))
    small_out = [_unpack_small(t, gain_widths, conv_w.shape[2]) for t in small_new]

    total = lax.psum(loss[0, 0], ("x", "y", "c"))
    order = ["g_pre_mix", "g_mem", "w_in", "w_mem_kv", "conv_w", "g_attn_out", "g_conv_out", "g_xattn_out", "w_out",
             "g_post_mix", "g_pre_mlp", "w_up", "w_down", "g_post_mlp"]
    gain_names = ["g_pre_mix", "g_mem", "g_attn_out", "g_conv_out", "g_xattn_out", "g_post_mix", "g_pre_mlp", "g_post_mlp"]
    mat_names = ["w_in", "w_mem_kv", "w_out", "w_up", "w_down"]

    def leaf(kind, name):
        if name in gain_names:
            i = gain_names.index(name)
            return gain_sums[i] if kind == 0 else small_out[kind - 1][0][i]
        if name == "conv_w":
            return (conv_sum if kind == 0 else small_out[kind - 1][1])[None]
        a = mat_names.index(name)
        return (mat_sums[a] if kind == 0 else mat_new[a][kind - 1])[None]

    return (total, grad_x[None], *[leaf(kind, name) for kind in range(4) for name in order])
```

```python
import functools

import jax
import jax.numpy as jnp
from jax import lax
from jax.experimental import pallas as pl
from jax.experimental.pallas import tpu as pltpu

F32, BF16 = jnp.float32, jnp.bfloat16

D_MODEL = 1024
ATTN_W = 512
CONV_W = 256
XATTN_W = 256
PROJ_W = 3 * ATTN_W + 3 * CONV_W + XATTN_W
D_FF = 4096
HEAD = 64
N_BACK = 128
DILATIONS = (1, 4, 16)
ROPE_THETA = 10000.0
EPS = 1e-6
NEG_INF = -1e30
SCALE = HEAD ** -0.5
N_CHIPS = 4
SHARD_IN = PROJ_W // N_CHIPS
SHARD_FF = D_FF // N_CHIPS

ADAM_LR, ADAM_B1, ADAM_B2, ADAM_EPS, ADAM_WD, ADAM_STEP = 0.001, 0.9, 0.999, 1e-08, 0.01, 10

VMEM_LIMIT_V7X = 56 * 1024 * 1024
ROW_TILE = 256
SMALL_ROWS = 16

NT = (((1,), (1,)), ((), ()))
TN = (((0,), (0,)), ((), ()))
MESH = pl.DeviceIdType.MESH


def _params(*sem):
    return pltpu.CompilerParams(dimension_semantics=sem, vmem_limit_bytes=VMEM_LIMIT_V7X)


def _resident(shape):
    return pl.BlockSpec(shape, lambda *_: (0,) * len(shape), pipeline_mode=pl.Buffered(1))


def _rows(tm, width):
    return pl.BlockSpec((tm, width), lambda i: (i, 0))


def _rms_hat(x):
    r = lax.rsqrt(jnp.mean(x * x, axis=-1, keepdims=True) + EPS)
    return x * r, r


def _rms_bwd(xhat, r, g, dy):
    gdy = dy * g
    return r * (gdy - xhat * jnp.mean(xhat * gdy, axis=-1, keepdims=True))


def _rope128(t, cos, sin_signed, inverse):
    lane = lax.broadcasted_iota(jnp.int32, t.shape, 1)
    first_half = (lane % HEAD) < (HEAD // 2)
    rot = jnp.where(first_half, pltpu.roll(t, 128 - HEAD // 2, 1), pltpu.roll(t, HEAD // 2, 1))
    return t * cos - rot * sin_signed if inverse else t * cos + rot * sin_signed


def _in_proj_fwd(x, g, w_in, cos, sin, tm):
    S = x.shape[0]

    def body(x_ref, g_ref, w_ref, cos_ref, sin_ref, h_ref, q_ref, k_ref, v_ref, bcu_ref, qx_ref, proj):
        xhat, _ = _rms_hat(x_ref[...])
        h = (xhat * g_ref[...]).astype(BF16)
        h_ref[...] = h
        for j in range(N_CHIPS):
            proj[:, SHARD_IN * j:SHARD_IN * (j + 1)] = jnp.dot(h, w_ref[j], preferred_element_type=F32)
        c, s = cos_ref[...], sin_ref[...]
        for j in range(ATTN_W // 128):
            lo = 128 * j
            q_ref[:, lo:lo + 128] = _rope128(proj[:, lo:lo + 128], c, s, False) * SCALE
            k_ref[:, lo:lo + 128] = _rope128(proj[:, ATTN_W + lo:ATTN_W + lo + 128], c, s, False)
        v_ref[...] = proj[:, 2 * ATTN_W:3 * ATTN_W]
        bcu_ref[...] = proj[:, 3 * ATTN_W:3 * ATTN_W + 3 * CONV_W]
        qx_ref[...] = proj[:, 3 * ATTN_W + 3 * CONV_W:PROJ_W].astype(BF16)

    return pl.pallas_call(
        body, name="in_proj_fwd", grid=(S // tm,),
        in_specs=[_rows(tm, D_MODEL), _resident((1, D_MODEL)), _resident((N_CHIPS, D_MODEL, SHARD_IN)),
                  _rows(tm, 128), _rows(tm, 128)],
        out_specs=[_rows(tm, D_MODEL), _rows(tm, ATTN_W), _rows(tm, ATTN_W), _rows(tm, ATTN_W),
                   _rows(tm, 3 * CONV_W), _rows(tm, XATTN_W)],
        out_shape=[jax.ShapeDtypeStruct((S, D_MODEL), BF16), jax.ShapeDtypeStruct((S, ATTN_W), F32),
                   jax.ShapeDtypeStruct((S, ATTN_W), F32), jax.ShapeDtypeStruct((S, ATTN_W), F32),
                   jax.ShapeDtypeStruct((S, 3 * CONV_W), F32), jax.ShapeDtypeStruct((S, XATTN_W), BF16)],
        scratch_shapes=[pltpu.VMEM((tm, PROJ_W), F32)],
        compiler_params=_params("parallel"),
    )(x, g, w_in, cos, sin)


def _memkv_fwd(mem, g_mem, w_kv):
    n_mem = mem.shape[0]

    def body(mem_ref, g_ref, w_ref, mn_ref, kv_ref):
        mhat, _ = _rms_hat(mem_ref[...])
        mn = (mhat * g_ref[...]).astype(BF16)
        mn_ref[...] = mn
        kv_ref[...] = jnp.dot(mn, w_ref[...], preferred_element_type=F32).astype(BF16)

    return pl.pallas_call(
        body, name="memkv_fwd",
        out_shape=[jax.ShapeDtypeStruct((n_mem, D_MODEL), BF16), jax.ShapeDtypeStruct((n_mem, 2 * XATTN_W), BF16)],
        compiler_params=pltpu.CompilerParams(vmem_limit_bytes=VMEM_LIMIT_V7X),
    )(mem, g_mem, w_kv)


def _fill_band_bias(bias):
    row = lax.broadcasted_iota(jnp.int32, (N_BACK, 2 * N_BACK), 0)
    col = lax.broadcasted_iota(jnp.int32, (N_BACK, 2 * N_BACK), 1)
    band = (col >= row) & (col <= row + N_BACK)
    bias[1] = jnp.where(band, 0.0, NEG_INF)
    bias[0] = jnp.where(band & (col >= N_BACK), 0.0, NEG_INF)


def _strided(start, size, d):
    return pl.ds(start, size) if d == 1 else pl.ds(start, size, stride=d)


def _block_starts(t, nb, d):
    r, n = lax.shift_right_logical(t, nb.bit_length() - 1), lax.bitwise_and(t, nb - 1)
    own = r + n * (N_BACK * d)
    prev = r + jnp.maximum(n - 1, 0) * (N_BACK * d)
    if d == 1:
        own, prev = pl.multiple_of(own, N_BACK), pl.multiple_of(prev, N_BACK)
    return own, prev, n


def _by_head(a, b):
    lane = lax.broadcasted_iota(jnp.int32, (a.shape[0], 2 * HEAD), 1)
    return jnp.where(lane < HEAD, a, b)


def _head_only(t, hh):
    lane = lax.broadcasted_iota(jnp.int32, t.shape, 1)
    return jnp.where((lane < HEAD) == (hh == 0), t, jnp.zeros_like(t))


def _stack_heads(t):
    return jnp.concatenate([_head_only(t, 0), _head_only(t, 1)], axis=0)


def _head_columns(t):
    return jnp.concatenate([t[:, 0:1], t[:, HEAD:HEAD + 1]], axis=0)


def _unstack(t):
    return _by_head(t[:N_BACK], t[N_BACK:])


def _unstack_columns(t):
    return _by_head(jnp.broadcast_to(t[:N_BACK], (N_BACK, 2 * HEAD)), jnp.broadcast_to(t[N_BACK:], (N_BACK, 2 * HEAD)))


FWD_BLOCKS_PER_STEP = 4
BWD_BLOCKS_PER_STEP = 2


def _attn_fwd(q, k, v):
    S = q.shape[0]
    U = FWD_BLOCKS_PER_STEP

    def body(q_ref, k_ref, v_ref, y_ref, m_ref, l_scr, bias):
        _fill_band_bias(bias)
        for g, d in enumerate(DILATIONS):
            nb = S // d // N_BACK
            first_pattern, last_pattern = g == 0, g == len(DILATIONS) - 1

            def step(i, carry, d=d, nb=nb, first_pattern=first_pattern, last_pattern=last_pattern):
                blocks = [_block_starts(U * i + u, nb, d) for u in range(U)]
                rows = [_strided(own, N_BACK, d) for own, _, _ in blocks]
                prev_rows = [_strided(prev, N_BACK, d) for _, prev, _ in blocks]
                ss = []
                for u, (_, _, n) in enumerate(blocks):
                    kw = jnp.concatenate([k_ref[prev_rows[u], :], k_ref[rows[u], :]], 0).astype(BF16)
                    qs = _stack_heads(q_ref[rows[u], :].astype(BF16))
                    b = bias[jnp.minimum(n, 1)]
                    ss.append(lax.dot_general(qs, kw, NT, preferred_element_type=F32) + jnp.concatenate([b, b], axis=0))
                ms = [jnp.max(s, axis=1, keepdims=True) for s in ss]
                ps = [jnp.exp(s - m) for s, m in zip(ss, ms)]
                ls = [jnp.sum(p, axis=1, keepdims=True) for p in ps]
                os_ = []
                for u in range(U):
                    vw = jnp.concatenate([v_ref[prev_rows[u], :], v_ref[rows[u], :]], 0).astype(BF16)
                    os_.append(jnp.dot(ps[u].astype(BF16), vw, preferred_element_type=F32))
                for u in range(U):
                    o_g, m_g, l_g = _unstack(os_[u]), _unstack_columns(ms[u]), _unstack_columns(ls[u])
                    r = rows[u]
                    if first_pattern:
                        m_new, l_new, acc = m_g, l_g, o_g
                    else:
                        m_old = m_ref[r, :]
                        m_new = jnp.maximum(m_old, m_g)
                        alpha, beta = jnp.exp(m_old - m_new), jnp.exp(m_g - m_new)
                        l_new = l_scr[r, :] * alpha + l_g * beta
                        acc = y_ref[r, :] * alpha + o_g * beta
                    if last_pattern:
                        y_ref[r, :] = acc / l_new
                        m_ref[r, :] = m_new + jnp.log(l_new)
                    else:
                        y_ref[r, :] = acc
                        m_ref[r, :] = m_new
                        l_scr[r, :] = l_new
                return carry

            lax.fori_loop(0, d * nb // U, step, 0)

    col = pl.BlockSpec((S, 2 * HEAD), lambda j: (0, j))
    return pl.pallas_call(
        body, name="attn_fwd", grid=(q.shape[1] // (2 * HEAD),),
        in_specs=[col, col, col], out_specs=[col, col],
        out_shape=[jax.ShapeDtypeStruct(q.shape, F32)] * 2,
        scratch_shapes=[pltpu.VMEM((S, 2 * HEAD), F32), pltpu.VMEM((2, N_BACK, 2 * N_BACK), F32)],
        compiler_params=_params("parallel"),
    )(q, k, v)


def _attn_bwd(q, k, v, dy, lse, delta):
    S = q.shape[0]
    U = BWD_BLOCKS_PER_STEP

    def body(q_ref, k_ref, v_ref, dy_ref, lse_ref, delta_ref, dq_ref, dk_ref, dv_ref, bias):
        _fill_band_bias(bias)
        dk_ref[...] = jnp.zeros_like(dk_ref)
        dv_ref[...] = jnp.zeros_like(dv_ref)
        for g, d in enumerate(DILATIONS):
            nb = S // d // N_BACK

            def step(i, carry, d=d, nb=nb, g=g):
                blocks = [_block_starts(U * i + u, nb, d) for u in range(U)]
                rows = [_strided(own, N_BACK, d) for own, _, _ in blocks]
                prev_rows = [_strided(prev, N_BACK, d) for _, prev, _ in blocks]
                kws = [jnp.concatenate([k_ref[prev_rows[u], :], k_ref[rows[u], :]], 0).astype(BF16) for u in range(U)]
                vws = [jnp.concatenate([v_ref[prev_rows[u], :], v_ref[rows[u], :]], 0).astype(BF16) for u in range(U)]
                qss = [_stack_heads(q_ref[rows[u], :].astype(BF16)) for u in range(U)]
                doss = [_stack_heads(dy_ref[rows[u], :].astype(BF16)) for u in range(U)]
                ss, dps = [], []
                for u, (_, _, n) in enumerate(blocks):
                    b = bias[jnp.minimum(n, 1)]
                    ss.append(lax.dot_general(qss[u], kws[u], NT, preferred_element_type=F32) + jnp.concatenate([b, b], axis=0))
                    dps.append(lax.dot_general(doss[u], vws[u], NT, preferred_element_type=F32))
                ps = [jnp.exp(ss[u] - _head_columns(lse_ref[rows[u], :])) for u in range(U)]
                dss = [(ps[u] * (dps[u] - _head_columns(delta_ref[rows[u], :]))).astype(BF16) for u in range(U)]
                pbs = [p.astype(BF16) for p in ps]
                dqs = [jnp.dot(dss[u], kws[u], preferred_element_type=F32) for u in range(U)]
                dkws = [lax.dot_general(dss[u], qss[u], TN, preferred_element_type=F32) for u in range(U)]
                dvws = [lax.dot_general(pbs[u], doss[u], TN, preferred_element_type=F32) for u in range(U)]
                for u in range(U):
                    dq = _unstack(dqs[u])
                    if g == 0:
                        dq_ref[rows[u], :] = dq
                    else:
                        dq_ref[rows[u], :] += dq
                    dk_ref[prev_rows[u], :] += dkws[u][:N_BACK]
                    dv_ref[prev_rows[u], :] += dvws[u][:N_BACK]
                    dk_ref[rows[u], :] += dkws[u][N_BACK:]
                    dv_ref[rows[u], :] += dvws[u][N_BACK:]
                return carry

            lax.fori_loop(0, d * nb // U, step, 0)

    col = pl.BlockSpec((S, 2 * HEAD), lambda j: (0, j))
    return pl.pallas_call(
        body, name="attn_bwd", grid=(q.shape[1] // (2 * HEAD),),
        in_specs=[col] * 6, out_specs=[col] * 3,
        out_shape=[jax.ShapeDtypeStruct(q.shape, F32)] * 3,
        scratch_shapes=[pltpu.VMEM((2, N_BACK, 2 * N_BACK), F32)],
        compiler_params=_params("parallel"),
    )(q, k, v, dy, lse, delta)


def _shift_down(z, before, k):
    row = lax.broadcasted_iota(jnp.int32, z.shape, 0)
    out = pltpu.roll(z, k, 0)
    for i in range(k):
        out = jnp.where(row == i, before[8 - k + i:8 - k + i + 1, :], out)
    return out


def _shift_up(z, after, k):
    rows = z.shape[0]
    row = lax.broadcasted_iota(jnp.int32, z.shape, 0)
    out = pltpu.roll(z, rows - k, 0)
    for i in range(k):
        out = jnp.where(row == rows - k + i, after[i:i + 1, :], out)
    return out


def _conv_fwd(bcu, before, is_first, w):
    b, c, u = bcu[:, 0:CONV_W], bcu[:, CONV_W:2 * CONV_W], bcu[:, 2 * CONV_W:3 * CONV_W]
    z = c * u
    zb = jnp.where(is_first, 0.0, before[:, CONV_W:2 * CONV_W] * before[:, 2 * CONV_W:3 * CONV_W])
    z1, z2 = _shift_down(z, zb, 1), _shift_down(z, zb, 2)
    cv = w[0:1, :] * z2 + w[1:2, :] * z1 + w[2:3, :] * z
    return b, c, u, z, z1, z2, cv


def _halo_before(tm, width):
    return pl.BlockSpec((8, width), lambda i: (jnp.maximum(i * (tm // 8) - 1, 0), 0))


def _halo_after(tm, width, S):
    return pl.BlockSpec((8, width), lambda i: (jnp.minimum((i + 1) * (tm // 8), S // 8 - 1), 0))


def _mix_fwd(ya, bcu, qx, mkv, conv_w, g_a, g_c, g_x, w_out, g_post, x, tm):
    S = x.shape[0]

    def body(ya_ref, bcu_ref, before_ref, qx_ref, mkv_ref, cw_ref, ga_ref, gc_ref, gx_ref,
             wo_ref, gp_ref, x_ref, yx_ref, ycat_ref, y2_ref, x1_ref):
        ya = ya_ref[...]
        b, _, _, _, _, _, cv = _conv_fwd(bcu_ref[...], before_ref[...], pl.program_id(0) == 0, cw_ref[...])
        yc = b * cv

        qxb, mkvb = qx_ref[...], mkv_ref[...]
        for hd in range(XATTN_W // HEAD):
            sl = slice(HEAD * hd, HEAD * (hd + 1))
            s = lax.dot_general(qxb[:, sl], mkvb[:, sl], NT, preferred_element_type=F32) * SCALE
            mx = jnp.max(s, axis=1, keepdims=True)
            p = jnp.exp(s - mx)
            l = jnp.sum(p, axis=1, keepdims=True)
            vm = mkvb[:, XATTN_W + HEAD * hd:XATTN_W + HEAD * (hd + 1)]
            yx_ref[:, sl] = jnp.dot(p.astype(BF16), vm, preferred_element_type=F32) / l
        yx = yx_ref[...]

        ycat_ref[:, 0:ATTN_W] = (_rms_hat(ya)[0] * ga_ref[...]).astype(BF16)
        ycat_ref[:, ATTN_W:ATTN_W + CONV_W] = (_rms_hat(yc)[0] * gc_ref[...]).astype(BF16)
        ycat_ref[:, ATTN_W + CONV_W:D_MODEL] = (_rms_hat(yx)[0] * gx_ref[...]).astype(BF16)
        y2 = jnp.dot(ycat_ref[...], wo_ref[...], preferred_element_type=F32)
        y2_ref[...] = y2
        x1_ref[...] = x_ref[...] + _rms_hat(y2)[0] * gp_ref[...]

    n_mem = mkv.shape[0]
    return pl.pallas_call(
        body, name="mix_fwd", grid=(S // tm,),
        in_specs=[_rows(tm, ATTN_W), _rows(tm, 3 * CONV_W), _halo_before(tm, 3 * CONV_W), _rows(tm, XATTN_W),
                  _resident((n_mem, 2 * XATTN_W)), _resident((3, CONV_W)), _resident((1, ATTN_W)),
                  _resident((1, CONV_W)), _resident((1, XATTN_W)), _resident((D_MODEL, D_MODEL)),
                  _resident((1, D_MODEL)), _rows(tm, D_MODEL)],
        out_specs=[_rows(tm, XATTN_W), _rows(tm, D_MODEL), _rows(tm, D_MODEL), _rows(tm, D_MODEL)],
        out_shape=[jax.ShapeDtypeStruct((S, XATTN_W), F32), jax.ShapeDtypeStruct((S, D_MODEL), BF16),
                   jax.ShapeDtypeStruct((S, D_MODEL), F32), jax.ShapeDtypeStruct((S, D_MODEL), F32)],
        compiler_params=_params("parallel"),
    )(ya, bcu, bcu, qx, mkv, conv_w, g_a, g_c, g_x, w_out, g_post, x)


def _mlp_fwd_bwd(x1, target, g_pre, g_post, w_up, w_down, tm):
    S = x1.shape[0]
    n_ff = D_FF // SHARD_FF

    def body(x1_ref, t_ref, gpre_ref, gpost_ref, wup_ref, wdn_ref,
             h2_ref, f_ref, du_ref, df2_ref, dx1_ref, dgpre_ref, dgpost_ref, loss_ref, u_scr):
        @pl.when(pl.program_id(0) == 0)
        def _():
            dgpre_ref[...] = jnp.zeros_like(dgpre_ref)
            dgpost_ref[...] = jnp.zeros_like(dgpost_ref)
            loss_ref[...] = jnp.zeros_like(loss_ref)

        x1 = x1_ref[...]
        x1hat, r1 = _rms_hat(x1)
        h2 = (x1hat * gpre_ref[...]).astype(BF16)
        h2_ref[...] = h2
        f2 = jnp.zeros((tm, D_MODEL), F32)
        for j in range(n_ff):
            cols = slice(SHARD_FF * j, SHARD_FF * (j + 1))
            u = jnp.maximum(jnp.dot(h2, wup_ref[j], preferred_element_type=F32), 0.0)
            u_scr[:, cols] = u
            f = (u * u).astype(BF16)
            f_ref[:, cols] = f
            f2 = f2 + jnp.dot(f, wdn_ref[cols, :], preferred_element_type=F32)
        f2hat, r2 = _rms_hat(f2)
        err = x1 + f2hat * gpost_ref[...] - t_ref[...]
        loss_ref[...] += 0.5 * jnp.sum(jnp.mean(err * err, axis=-1, keepdims=True), axis=0, keepdims=True)
        dx2 = err * (1.0 / D_MODEL)
        dgpost_ref[...] += jnp.sum(dx2 * f2hat, axis=0, keepdims=True)
        df2 = _rms_bwd(f2hat, r2, gpost_ref[...], dx2).astype(BF16)
        df2_ref[...] = df2
        dh2 = jnp.zeros((tm, D_MODEL), F32)
        for j in range(n_ff):
            cols = slice(SHARD_FF * j, SHARD_FF * (j + 1))
            df = lax.dot_general(df2, wdn_ref[cols, :], NT, preferred_element_type=F32)
            du = (2.0 * u_scr[:, cols] * df).astype(BF16)
            du_ref[:, cols] = du
            dh2 = dh2 + lax.dot_general(du, wup_ref[j], NT, preferred_element_type=F32)
        dgpre_ref[...] += jnp.sum(dh2 * x1hat, axis=0, keepdims=True)
        dx1_ref[...] = dx2 + _rms_bwd(x1hat, r1, gpre_ref[...], dh2)

    acc = pl.BlockSpec((1, D_MODEL), lambda i: (0, 0))
    return pl.pallas_call(
        body, name="mlp_fwd_bwd", grid=(S // tm,),
        in_specs=[_rows(tm, D_MODEL), _rows(tm, D_MODEL), _resident((1, D_MODEL)), _resident((1, D_MODEL)),
                  _resident((n_ff, D_MODEL, SHARD_FF)), _resident((D_FF, D_MODEL))],
        out_specs=[_rows(tm, D_MODEL), _rows(tm, D_FF), _rows(tm, D_FF), _rows(tm, D_MODEL), _rows(tm, D_MODEL),
                   acc, acc, pl.BlockSpec((1, 1), lambda i: (0, 0))],
        out_shape=[jax.ShapeDtypeStruct((S, D_MODEL), BF16), jax.ShapeDtypeStruct((S, D_FF), BF16),
                   jax.ShapeDtypeStruct((S, D_FF), BF16), jax.ShapeDtypeStruct((S, D_MODEL), BF16),
                   jax.ShapeDtypeStruct((S, D_MODEL), F32), jax.ShapeDtypeStruct((1, D_MODEL), F32),
                   jax.ShapeDtypeStruct((1, D_MODEL), F32), jax.ShapeDtypeStruct((1, 1), F32)],
        scratch_shapes=[pltpu.VMEM((tm, D_FF), F32)],
        compiler_params=_params("arbitrary"),
    )(x1, target, g_pre, g_post, w_up, w_down)


def _weight_grad(name, a, b, rows_sharded):
    S, K = a.shape
    N = b.shape[1]
    if rows_sharded:
        tk, tn = K // N_CHIPS, N
        a_spec = pl.BlockSpec((S, tk), lambda j: (0, j))
        b_spec = pl.BlockSpec((S, tn), lambda j: (0, 0), pipeline_mode=pl.Buffered(1))
    else:
        tk, tn = K, N // N_CHIPS
        a_spec = pl.BlockSpec((S, tk), lambda j: (0, 0), pipeline_mode=pl.Buffered(1))
        b_spec = pl.BlockSpec((S, tn), lambda j: (0, j))
    half = tk // 2

    def body(a_ref, b_ref, o_ref):
        res = lax.dot_general(a_ref[...], b_ref[...], TN, preferred_element_type=F32)
        o_ref[0, 0] = res[:half]
        o_ref[1, 0] = res[half:]

    return pl.pallas_call(
        body, name=name, grid=(N_CHIPS,), in_specs=[a_spec, b_spec],
        out_specs=pl.BlockSpec((2, 1, half, tn), lambda j: (0, j, 0, 0)),
        out_shape=jax.ShapeDtypeStruct((2, N_CHIPS, half, tn), F32),
        compiler_params=_params("parallel"),
    )(a, b)


def _mix_bwd(dx1, y2, ya, yx, bcu, conv_w, g_a, g_c, g_x, w_out, g_post, tm):
    S = dx1.shape[0]

    def body(dx1_ref, y2_ref, ya_ref, yx_ref, bcu_ref, before_ref, cw_ref, ga_ref, gc_ref, gx_ref, wo_ref, gp_ref,
             dy2_ref, dya_ref, delta_ref, dycx_ref, dgp_ref, dga_ref, dgc_ref, dgx_ref):
        @pl.when(pl.program_id(0) == 0)
        def _():
            for ref in (dgp_ref, dga_ref, dgc_ref, dgx_ref):
                ref[...] = jnp.zeros_like(ref)

        dx1 = dx1_ref[...]
        y2hat, r2 = _rms_hat(y2_ref[...])
        dgp_ref[...] += jnp.sum(dx1 * y2hat, axis=0, keepdims=True)
        dy2 = _rms_bwd(y2hat, r2, gp_ref[...], dx1).astype(BF16)
        dy2_ref[...] = dy2
        dycat = lax.dot_general(dy2, wo_ref[...], NT, preferred_element_type=F32)

        d_na = dycat[:, 0:ATTN_W]
        ya = ya_ref[...]
        yahat, ra = _rms_hat(ya)
        dga_ref[...] += jnp.sum(d_na * yahat, axis=0, keepdims=True)
        dya = _rms_bwd(yahat, ra, ga_ref[...], d_na)
        dya_ref[...] = dya
        prod = dya * ya
        hi = prod.astype(BF16)
        lo = (prod - hi.astype(F32)).astype(BF16)
        head_of = lambda axis: lax.shift_right_logical(lax.broadcasted_iota(jnp.int32, (ATTN_W, ATTN_W), axis),
                                                       HEAD.bit_length() - 1)
        same_head = head_of(0) == head_of(1)
        ones = jnp.where(same_head, 1.0, 0.0).astype(BF16)
        delta_ref[...] = (jnp.dot(hi, ones, preferred_element_type=F32) + jnp.dot(lo, ones, preferred_element_type=F32))

        b, _, _, _, _, _, cv = _conv_fwd(bcu_ref[...], before_ref[...], pl.program_id(0) == 0, cw_ref[...])
        d_nc = dycat[:, ATTN_W:ATTN_W + CONV_W]
        ychat, rc = _rms_hat(b * cv)
        dgc_ref[...] += jnp.sum(d_nc * ychat, axis=0, keepdims=True)
        dycx_ref[:, 0:CONV_W] = _rms_bwd(ychat, rc, gc_ref[...], d_nc)

        d_nx = dycat[:, ATTN_W + CONV_W:D_MODEL]
        yxhat, rx = _rms_hat(yx_ref[...])
        dgx_ref[...] += jnp.sum(d_nx * yxhat, axis=0, keepdims=True)
        dycx_ref[:, CONV_W:CONV_W + XATTN_W] = _rms_bwd(yxhat, rx, gx_ref[...], d_nx)

    acc = lambda w: pl.BlockSpec((1, w), lambda i: (0, 0))
    return pl.pallas_call(
        body, name="mix_bwd", grid=(S // tm,),
        in_specs=[_rows(tm, D_MODEL), _rows(tm, D_MODEL), _rows(tm, ATTN_W), _rows(tm, XATTN_W),
                  _rows(tm, 3 * CONV_W), _halo_before(tm, 3 * CONV_W), _resident((3, CONV_W)),
                  _resident((1, ATTN_W)), _resident((1, CONV_W)), _resident((1, XATTN_W)),
                  _resident((D_MODEL, D_MODEL)), _resident((1, D_MODEL))],
        out_specs=[_rows(tm, D_MODEL), _rows(tm, ATTN_W), _rows(tm, ATTN_W), _rows(tm, CONV_W + XATTN_W),
                   acc(D_MODEL), acc(ATTN_W), acc(CONV_W), acc(XATTN_W)],
        out_shape=[jax.ShapeDtypeStruct((S, D_MODEL), BF16), jax.ShapeDtypeStruct((S, ATTN_W), F32),
                   jax.ShapeDtypeStruct((S, ATTN_W), F32),
                   jax.ShapeDtypeStruct((S, CONV_W + XATTN_W), F32), jax.ShapeDtypeStruct((1, D_MODEL), F32),
                   jax.ShapeDtypeStruct((1, ATTN_W), F32), jax.ShapeDtypeStruct((1, CONV_W), F32),
                   jax.ShapeDtypeStruct((1, XATTN_W), F32)],
        compiler_params=_params("arbitrary"),
    )(dx1, y2, ya, yx, bcu, bcu, conv_w, g_a, g_c, g_x, w_out, g_post)


def _conv_xattn_bwd(dycx, bcu, qx, mkv, conv_w, tm):
    S = dycx.shape[0]
    n_mem = mkv.shape[0]
    n_tiles = S // tm

    def body(d_ref, dafter_ref, bcu_ref, before_ref, after_ref, qx_ref, mkv_ref, cw_ref,
             tail_ref, dmkv_ref, dcw_ref):
        i = pl.program_id(0)

        @pl.when(i == 0)
        def _():
            dmkv_ref[...] = jnp.zeros_like(dmkv_ref)
            dcw_ref[...] = jnp.zeros_like(dcw_ref)

        w = cw_ref[...]
        b, c, u, z, z1, z2, cv = _conv_fwd(bcu_ref[...], before_ref[...], i == 0, w)
        dyc = d_ref[:, 0:CONV_W]
        dcv = dyc * b
        dcv_after = jnp.where(i == n_tiles - 1, 0.0, dafter_ref[:, 0:CONV_W] * after_ref[:, 0:CONV_W])
        dz = w[2:3, :] * dcv + w[1:2, :] * _shift_up(dcv, dcv_after, 1) + w[0:1, :] * _shift_up(dcv, dcv_after, 2)
        dcw_ref[0:1, :] += jnp.sum(dcv * z2, axis=0, keepdims=True)
        dcw_ref[1:2, :] += jnp.sum(dcv * z1, axis=0, keepdims=True)
        dcw_ref[2:3, :] += jnp.sum(dcv * z, axis=0, keepdims=True)
        tail_ref[:, 0:CONV_W] = (dyc * cv).astype(BF16)
        tail_ref[:, CONV_W:2 * CONV_W] = (dz * u).astype(BF16)
        tail_ref[:, 2 * CONV_W:3 * CONV_W] = (dz * c).astype(BF16)

        qxb, mkvb = qx_ref[...], mkv_ref[...]
        for hd in range(XATTN_W // HEAD):
            sl = slice(HEAD * hd, HEAD * (hd + 1))
            vsl = slice(XATTN_W + HEAD * hd, XATTN_W + HEAD * (hd + 1))
            s = lax.dot_general(qxb[:, sl], mkvb[:, sl], NT, preferred_element_type=F32) * SCALE
            e = jnp.exp(s - jnp.max(s, axis=1, keepdims=True))
            p = e / jnp.sum(e, axis=1, keepdims=True)
            dob = d_ref[:, CONV_W + HEAD * hd:CONV_W + HEAD * (hd + 1)].astype(BF16)
            dp = lax.dot_general(dob, mkvb[:, vsl], NT, preferred_element_type=F32)
            ds = (p * (dp - jnp.sum(p * dp, axis=1, keepdims=True)) * SCALE).astype(BF16)
            tail_ref[:, 3 * CONV_W + HEAD * hd:3 * CONV_W + HEAD * (hd + 1)] = jnp.dot(
                ds, mkvb[:, sl], preferred_element_type=F32).astype(BF16)
            dmkv_ref[:, sl] += lax.dot_general(ds, qxb[:, sl], TN, preferred_element_type=F32)
            dmkv_ref[:, vsl] += lax.dot_general(p.astype(BF16), dob, TN, preferred_element_type=F32)

    width = CONV_W + XATTN_W
    return pl.pallas_call(
        body, name="conv_xattn_bwd", grid=(n_tiles,),
        in_specs=[_rows(tm, width), _halo_after(tm, width, S), _rows(tm, 3 * CONV_W), _halo_before(tm, 3 * CONV_W),
                  _halo_after(tm, 3 * CONV_W, S), _rows(tm, XATTN_W), _resident((n_mem, 2 * XATTN_W)),
                  _resident((3, CONV_W))],
        out_specs=[_rows(tm, 3 * CONV_W + XATTN_W), pl.BlockSpec((n_mem, 2 * XATTN_W), lambda i: (0, 0)),
                   pl.BlockSpec((3, CONV_W), lambda i: (0, 0))],
        out_shape=[jax.ShapeDtypeStruct((S, 3 * CONV_W + XATTN_W), BF16),
                   jax.ShapeDtypeStruct((n_mem, 2 * XATTN_W), F32), jax.ShapeDtypeStruct((3, CONV_W), F32)],
        compiler_params=_params("arbitrary"),
    )(dycx, dycx, bcu, bcu, bcu, qx, mkv, conv_w)


def _memkv_bwd(mem, g_mem, w_kv, dmkv):
    n_mem = mem.shape[0]
    half = D_MODEL // N_CHIPS // 2

    def body(mem_ref, g_ref, w_ref, d_ref, dw_ref, dg_ref):
        mhat, _ = _rms_hat(mem_ref[...])
        mn = (mhat * g_ref[...]).astype(BF16)
        d = d_ref[...].astype(BF16)
        for k in range(2 * N_CHIPS):
            dw_ref[k % 2, k // 2] = lax.dot_general(mn[:, half * k:half * (k + 1)], d, TN, preferred_element_type=F32)
        dmn = lax.dot_general(d, w_ref[...], NT, preferred_element_type=F32)
        dg_ref[...] = jnp.sum(dmn * mhat, axis=0, keepdims=True)

    return pl.pallas_call(
        body, name="memkv_bwd",
        out_shape=[jax.ShapeDtypeStruct((2, N_CHIPS, half, 2 * XATTN_W), F32), jax.ShapeDtypeStruct((1, D_MODEL), F32)],
        compiler_params=pltpu.CompilerParams(vmem_limit_bytes=VMEM_LIMIT_V7X),
    )(mem, g_mem, w_kv, dmkv)


def _in_proj_bwd(dqkv, tail, cos, sin, w_in, x, g, dx1, tm):
    S = x.shape[0]

    def body(dq_ref, dk_ref, dv_ref, tail_ref, cos_ref, sin_ref, w_ref, x_ref, g_ref, dx1_ref, dproj_ref, dx_ref, dg_ref):
        @pl.when(pl.program_id(0) == 0)
        def _():
            dg_ref[...] = jnp.zeros_like(dg_ref)

        c, s = cos_ref[...], sin_ref[...]
        for j in range(ATTN_W // 128):
            cols = slice(128 * j, 128 * (j + 1))
            dproj_ref[:, cols] = _rope128(dq_ref[:, cols] * SCALE, c, s, True).astype(BF16)
            dproj_ref[:, ATTN_W + 128 * j:ATTN_W + 128 * (j + 1)] = _rope128(dk_ref[:, cols], c, s, True).astype(BF16)
        dproj_ref[:, 2 * ATTN_W:3 * ATTN_W] = dv_ref[...].astype(BF16)
        dproj_ref[:, 3 * ATTN_W:PROJ_W] = tail_ref[...]
        dh = jnp.zeros((tm, D_MODEL), F32)
        for j in range(N_CHIPS):
            dh = dh + lax.dot_general(dproj_ref[:, SHARD_IN * j:SHARD_IN * (j + 1)], w_ref[j], NT,
                                      preferred_element_type=F32)
        xhat, r = _rms_hat(x_ref[...])
        dg_ref[...] += jnp.sum(dh * xhat, axis=0, keepdims=True)
        dx_ref[...] = dx1_ref[...] + _rms_bwd(xhat, r, g_ref[...], dh)

    return pl.pallas_call(
        body, name="in_proj_bwd", grid=(S // tm,),
        in_specs=[_rows(tm, ATTN_W)] * 3 + [_rows(tm, PROJ_W - 3 * ATTN_W), _rows(tm, 128), _rows(tm, 128),
                  _resident((N_CHIPS, D_MODEL, SHARD_IN)), _rows(tm, D_MODEL), _resident((1, D_MODEL)),
                  _rows(tm, D_MODEL)],
        out_specs=[_rows(tm, PROJ_W), _rows(tm, D_MODEL), pl.BlockSpec((1, D_MODEL), lambda i: (0, 0))],
        out_shape=[jax.ShapeDtypeStruct((S, PROJ_W), BF16), jax.ShapeDtypeStruct((S, D_MODEL), F32),
                   jax.ShapeDtypeStruct((1, D_MODEL), F32)],
        compiler_params=_params("arbitrary"),
    )(*dqkv, tail, cos, sin, w_in, x, g, dx1)


def _row_tile(rows):
    return ROW_TILE if rows % ROW_TILE == 0 else rows


def _pair_sum_bf16(name, a, b):
    n, rows, cols = a.shape
    tr = _row_tile(rows)

    def body(a_ref, b_ref, o_ref):
        o_ref[...] = (a_ref[...] + b_ref[...]).astype(BF16)

    spec = pl.BlockSpec((1, tr, cols), lambda s, i: (s, i, 0))
    return pl.pallas_call(
        body, name=name, grid=(n, rows // tr), in_specs=[spec, spec], out_specs=spec,
        out_shape=jax.ShapeDtypeStruct(a.shape, BF16), compiler_params=_params("parallel", "parallel"),
    )(a, b)


def _final_sum(name, own, sibling, others):
    rows, cols = own.shape
    tr = _row_tile(rows)

    def body(own_ref, sib_ref, o0, o1, o2, out_ref):
        acc = own_ref[...] + sib_ref[...]
        for o in (o0, o1, o2):
            acc = acc + o[0].astype(F32)
        out_ref[...] = acc

    spec = pl.BlockSpec((tr, cols), lambda i: (i, 0))
    other = lambda k: pl.BlockSpec((1, tr, cols), lambda i: (k, i, 0))
    return pl.pallas_call(
        body, name=name, grid=(rows // tr,), in_specs=[spec, spec, other(0), other(1), other(2)], out_specs=spec,
        out_shape=jax.ShapeDtypeStruct(own.shape, F32), compiler_params=_params("parallel"),
    )(own, sibling, others, others, others)


def _adamw(name, w, g, m, v):
    rows, cols = w.shape
    tr = _row_tile(rows)

    def body(w_ref, g_ref, m_ref, v_ref, d_ref, nm_ref, nv_ref):
        g = g_ref[...]
        m = ADAM_B1 * m_ref[...] + (1.0 - ADAM_B1) * g
        v = ADAM_B2 * v_ref[...] + (1.0 - ADAM_B2) * (g * g)
        m_hat = m / (1.0 - ADAM_B1 ** ADAM_STEP)
        v_hat = v / (1.0 - ADAM_B2 ** ADAM_STEP)
        d_ref[...] = -ADAM_LR * (m_hat / (jnp.sqrt(v_hat) + ADAM_EPS) + ADAM_WD * w_ref[...])
        nm_ref[...] = m
        nv_ref[...] = v

    spec = pl.BlockSpec((tr, cols), lambda i: (i, 0))
    return pl.pallas_call(
        body, name=name, grid=(rows // tr,), in_specs=[spec] * 4, out_specs=[spec] * 3,
        out_shape=[jax.ShapeDtypeStruct(w.shape, F32)] * 3, compiler_params=_params("parallel"),
    )(w, g, m, v)


def _sum_blocks(name, blocks):
    n, rows, cols = blocks.shape

    def body(b_ref, o_ref):
        acc = b_ref[0]
        for k in range(1, n):
            acc = acc + b_ref[k]
        o_ref[...] = acc

    return pl.pallas_call(body, name=name, out_shape=jax.ShapeDtypeStruct((rows, cols), F32))(blocks)


def _place():
    return lax.axis_index("x"), lax.axis_index("y"), lax.axis_index("c")


def _other_chips(x, y):
    return [(1 - x, y), (x, 1 - y), (1 - x, 1 - y)]


def _small_allgather(name, block):
    rows, cols = block.shape
    relations = [(dx, dy, dc) for dx in (0, 1) for dy in (0, 1) for dc in (0, 1) if (dx, dy, dc) != (0, 0, 0)]

    def body(x_ref, out_ref, send_sems, recv_sems, local_sem):
        x, y, c = _place()

        def peer(rel):
            return (1 - x if rel[0] else x, 1 - y if rel[1] else y, 1 - c if rel[2] else c)

        def index(p):
            return 4 * p[0] + 2 * p[1] + p[2]

        def copy(k, origin, to):
            return pltpu.make_async_remote_copy(
                src_ref=x_ref, dst_ref=out_ref.at[index(origin)], send_sem=send_sems.at[k], recv_sem=recv_sems.at[k],
                device_id=to, device_id_type=MESH)

        mine = pltpu.make_async_copy(x_ref, out_ref.at[index((x, y, c))], local_sem)
        mine.start()
        sends = [copy(k, (x, y, c), peer(rel)) for k, rel in enumerate(relations)]
        for cp in sends:
            cp.start()
        for k, rel in enumerate(relations):
            copy(k, peer(rel), (x, y, c)).wait_recv()
        for cp in sends:
            cp.wait_send()
        mine.wait()

    return pl.pallas_call(
        body, name=name, out_shape=jax.ShapeDtypeStruct((8, rows, cols), F32),
        in_specs=[pl.BlockSpec(memory_space=pltpu.VMEM)], out_specs=pl.BlockSpec(memory_space=pltpu.VMEM),
        scratch_shapes=[pltpu.SemaphoreType.DMA((7,)), pltpu.SemaphoreType.DMA((7,)), pltpu.SemaphoreType.DMA],
    )(block)


def _weights_allgather(shards):
    n = len(shards)

    def body(*refs):
        ins, outs, stage = refs[:n], refs[n:2 * n], refs[2 * n:3 * n]
        send_sems, recv_sems, local_sems = refs[3 * n:]
        x, y, c = _place()
        me, sibling = (x, y, c), (x, y, 1 - c)
        chips = _other_chips(x, y)
        chip_index = lambda chip: 2 * chip[0] + chip[1]

        def copy(a, k, chip, half, to, src=None):
            place = outs[a].at[chip_index(chip), half]
            return pltpu.make_async_remote_copy(
                src_ref=place if src is None else src, dst_ref=place, send_sem=send_sems.at[6 * a + k],
                recv_sem=recv_sems.at[6 * a + k], device_id=to, device_id_type=MESH)

        load = [pltpu.make_async_copy(ins[a], stage[a], local_sems.at[a]) for a in range(n)]
        local = [pltpu.make_async_copy(stage[a], outs[a].at[chip_index((x, y))], local_sems.at[a]) for a in range(n)]
        for cp in load:
            cp.start()
        first = [copy(a, k, (x, y), c, (*chip, c), src=ins[a].at[c]) for a in range(n) for k, chip in enumerate(chips)]
        for cp in first:
            cp.start()
        for a in range(n):
            load[a].wait()
            local[a].start()
        passed = []
        for a in range(n):
            for k, chip in enumerate(chips):
                copy(a, k, chip, c, me).wait_recv()
                passed.append(copy(a, 3 + k, chip, c, sibling))
                passed[-1].start()
        for a in range(n):
            for k, chip in enumerate(chips):
                copy(a, 3 + k, chip, 1 - c, me).wait_recv()
        for cp in first + passed:
            cp.wait_send()
        for cp in local:
            cp.wait()

    any_spec = pl.BlockSpec(memory_space=pl.ANY)
    return pl.pallas_call(
        body, name="weights_allgather",
        out_shape=[jax.ShapeDtypeStruct((N_CHIPS,) + s.shape, s.dtype) for s in shards],
        in_specs=[any_spec] * n, out_specs=[any_spec] * n,
        scratch_shapes=[pltpu.VMEM(s.shape, s.dtype) for s in shards]
        + [pltpu.SemaphoreType.DMA((6 * n,)), pltpu.SemaphoreType.DMA((6 * n,)), pltpu.SemaphoreType.DMA((n,))],
        compiler_params=pltpu.CompilerParams(vmem_limit_bytes=VMEM_LIMIT_V7X),
    )(*shards)


def _to_sibling(name, arrays, pick_other_half):
    n = len(arrays)

    def body(*refs):
        ins, outs = refs[:n], refs[n:2 * n]
        send_sems, recv_sems = refs[2 * n:]
        x, y, c = _place()
        copies = [pltpu.make_async_remote_copy(
            src_ref=ins[a].at[1 - c] if pick_other_half else ins[a], dst_ref=outs[a], send_sem=send_sems.at[a],
            recv_sem=recv_sems.at[a], device_id=(x, y, 1 - c), device_id_type=MESH) for a in range(n)]
        for cp in copies:
            cp.start()
        for cp in copies:
            cp.wait()

    any_spec = pl.BlockSpec(memory_space=pl.ANY)
    return pl.pallas_call(
        body, name=name,
        out_shape=[jax.ShapeDtypeStruct(a.shape[1:] if pick_other_half else a.shape, a.dtype) for a in arrays],
        in_specs=[any_spec] * n, out_specs=[any_spec] * n,
        scratch_shapes=[pltpu.SemaphoreType.DMA((n,)), pltpu.SemaphoreType.DMA((n,))],
    )(*arrays)


def _to_other_chips(partials):
    n = len(partials)

    def body(*refs):
        ins, outs = refs[:n], refs[n:2 * n]
        send_sems, recv_sems = refs[2 * n:]
        x, y, c = _place()
        copies = [pltpu.make_async_remote_copy(
            src_ref=ins[a].at[2 * chip[0] + chip[1]], dst_ref=outs[a].at[k], send_sem=send_sems.at[3 * a + k],
            recv_sem=recv_sems.at[3 * a + k], device_id=(*chip, c), device_id_type=MESH)
            for a in range(n) for k, chip in enumerate(_other_chips(x, y))]
        for cp in copies:
            cp.start()
        for cp in copies:
            cp.wait()

    any_spec = pl.BlockSpec(memory_space=pl.ANY)
    return pl.pallas_call(
        body, name="grads_to_other_chips",
        out_shape=[jax.ShapeDtypeStruct((3,) + p.shape[1:], p.dtype) for p in partials],
        in_specs=[any_spec] * n, out_specs=[any_spec] * n,
        scratch_shapes=[pltpu.SemaphoreType.DMA((3 * n,)), pltpu.SemaphoreType.DMA((3 * n,))],
    )(*partials)


def _exchange_halves(halves):
    n = len(halves)

    def body(*refs):
        ins, outs, stage = refs[:n], refs[n:2 * n], refs[2 * n:3 * n]
        send_sems, recv_sems, local_sems = refs[3 * n:]
        x, y, c = _place()
        load = [pltpu.make_async_copy(ins[a], stage[a], local_sems.at[a]) for a in range(n)]
        local = [pltpu.make_async_copy(stage[a], outs[a].at[c], local_sems.at[a]) for a in range(n)]
        remote = [pltpu.make_async_remote_copy(
            src_ref=stage[a], dst_ref=outs[a].at[c], send_sem=send_sems.at[a], recv_sem=recv_sems.at[a],
            device_id=(x, y, 1 - c), device_id_type=MESH) for a in range(n)]
        for cp in load:
            cp.start()
        for a in range(n):
            load[a].wait()
            remote[a].start()
            local[a].start()
        for a in range(n):
            pltpu.make_async_remote_copy(
                src_ref=ins[a], dst_ref=outs[a].at[1 - c], send_sem=send_sems.at[a], recv_sem=recv_sems.at[a],
                device_id=(x, y, 1 - c), device_id_type=MESH).wait_recv()
        for cp in remote:
            cp.wait_send()
        for cp in local:
            cp.wait()

    any_spec = pl.BlockSpec(memory_space=pl.ANY)
    return pl.pallas_call(
        body, name="sums_to_sibling",
        out_shape=[jax.ShapeDtypeStruct((2,) + h.shape, h.dtype) for h in halves],
        in_specs=[any_spec] * n, out_specs=[any_spec] * n,
        scratch_shapes=[pltpu.VMEM(h.shape, h.dtype) for h in halves]
        + [pltpu.SemaphoreType.DMA((n,)), pltpu.SemaphoreType.DMA((n,)), pltpu.SemaphoreType.DMA((n,))],
        compiler_params=pltpu.CompilerParams(vmem_limit_bytes=VMEM_LIMIT_V7X),
    )(*halves)


def _reduce_scatter(grads):
    x, y, c = _place()
    j = 2 * x + y
    from_sibling = _to_sibling("grads_to_sibling", grads, True)
    mine = [lax.dynamic_index_in_dim(g, c, 0, keepdims=False) for g in grads]
    partials = [_pair_sum_bf16(f"chip_sum_{a}", mine[a], from_sibling[a]) for a in range(len(grads))]
    from_chips = _to_other_chips(partials)
    own = [lax.dynamic_index_in_dim(m, j, 0, keepdims=False) for m in mine]
    sib = [lax.dynamic_index_in_dim(s, j, 0, keepdims=False) for s in from_sibling]
    halves = [_final_sum(f"final_sum_{a}", own[a], sib[a], from_chips[a]) for a in range(len(grads))]
    return [t.reshape(2 * t.shape[1], t.shape[2]) for t in _exchange_halves(halves)]


def _rope_tables(positions):
    half = HEAD // 2
    inv_freq = jnp.float32(ROPE_THETA) ** (-(jnp.arange(half, dtype=F32) * 2.0 / HEAD))
    ang = positions.astype(F32)[:, None] * inv_freq
    cos, sin = jnp.cos(ang), jnp.sin(ang)
    return jnp.tile(cos, (1, 4)), jnp.tile(jnp.concatenate([-sin, sin], axis=1), (1, 2))


def _local_step(x, mem, positions, target, gains, w_in, w_kv, conv_w, w_out, w_up, w_down):
    g_pre_mix, g_mem, g_a, g_c, g_x, g_post_mix, g_pre_mlp, g_post_mlp = gains
    tm = ROW_TILE
    cos, sin = _rope_tables(positions)

    h, q, k, v, bcu, qx = _in_proj_fwd(x, g_pre_mix, w_in, cos, sin, tm)
    memn, mkv = _memkv_fwd(mem, g_mem, w_kv)
    ya, lse = _attn_fwd(q, k, v)
    yx, ycat, y2, x1 = _mix_fwd(ya, bcu, qx, mkv, conv_w, g_a, g_c, g_x, w_out, g_post_mix, x, tm)
    h2, f, du, df2, dx1, dg_pre_mlp, dg_post_mlp, loss = _mlp_fwd_bwd(x1, target, g_pre_mlp, g_post_mlp, w_up, w_down, tm)
    gw_down = _weight_grad("grad_w_down", f, df2, True)
    gw_up = _weight_grad("grad_w_up", h2, du, False)

    dy2, dya, delta, dycx, dg_post_mix, dg_a, dg_c, dg_x = _mix_bwd(dx1, y2, ya, yx, bcu, conv_w, g_a, g_c, g_x,
                                                                  w_out, g_post_mix, tm)
    gw_out = _weight_grad("grad_w_out", ycat, dy2, True)
    tail, dmkv, g_conv = _conv_xattn_bwd(dycx, bcu, qx, mkv, conv_w, tm)
    gw_kv, dg_mem = _memkv_bwd(mem, g_mem, w_kv, dmkv)
    dqkv = _attn_bwd(q, k, v, dya, lse, delta)
    dproj, grad_x, dg_pre_mix = _in_proj_bwd(dqkv, tail, cos, sin, w_in, x, g_pre_mix, dx1, tm)
    gw_in = _weight_grad("grad_w_in", h, dproj, False)

    gain_grads = [dg_pre_mix, dg_mem, dg_a, dg_c, dg_x, dg_post_mix, dg_pre_mlp, dg_post_mlp]
    return loss, grad_x, [gw_in, gw_kv, gw_out, gw_up, gw_down], g_conv, gain_grads


def _pack_small(gains, conv):
    rows = [jnp.pad(g, ((0, 0), (0, D_MODEL - g.shape[1]))) for g in gains]
    rows.append(jnp.pad(conv, ((0, SMALL_ROWS - 8 - conv.shape[0]), (0, D_MODEL - conv.shape[1]))))
    return jnp.concatenate(rows, axis=0)


def _unpack_small(block, gain_widths, conv_width):
    gains = [block[i:i + 1, :w] for i, w in enumerate(gain_widths)]
    return gains, block[8:11, :conv_width]


def kernel(x, mem, positions, g_pre_mix, g_mem, w_in, w_mem_kv, conv_w, g_attn_out, g_conv_out, g_xattn_out, w_out, g_post_mix, g_pre_mlp, w_up, w_down, g_post_mlp, loss_target, m_g_pre_mix, m_g_mem, m_w_in, m_w_mem_kv, m_conv_w, m_g_attn_out, m_g_conv_out, m_g_xattn_out, m_w_out, m_g_post_mix, m_g_pre_mlp, m_w_up, m_w_down, m_g_post_mlp, v_g_pre_mix, v_g_mem, v_w_in, v_w_mem_kv, v_conv_w, v_g_attn_out, v_g_conv_out, v_g_xattn_out, v_w_out, v_g_post_mix, v_g_pre_mlp, v_w_up, v_w_down, v_g_post_mlp):
    cx, cy, cc = _place()
    chip = 2 * cx + cy
    gains = [g_pre_mix, g_mem, g_attn_out, g_conv_out, g_xattn_out, g_post_mix, g_pre_mlp, g_post_mlp]
    gains_m = [m_g_pre_mix, m_g_mem, m_g_attn_out, m_g_conv_out, m_g_xattn_out, m_g_post_mix, m_g_pre_mlp, m_g_post_mlp]
    gains_v = [v_g_pre_mix, v_g_mem, v_g_attn_out, v_g_conv_out, v_g_xattn_out, v_g_post_mix, v_g_pre_mlp, v_g_post_mlp]
    gain_widths = [g.shape[1] for g in gains]
    mats = [w_in[0], w_mem_kv[0], w_out[0], w_up[0], w_down[0]]
    mats_m = [m_w_in[0], m_w_mem_kv[0], m_w_out[0], m_w_up[0], m_w_down[0]]
    mats_v = [v_w_in[0], v_w_mem_kv[0], v_w_out[0], v_w_up[0], v_w_down[0]]

    shards = [w.astype(BF16).reshape(2, w.shape[0] // 2, w.shape[1]) for w in mats]
    full = [g.reshape(N_CHIPS, 2 * g.shape[2], g.shape[3]) for g in _weights_allgather(shards)]
    w_in_f, w_kv_f, w_out_f, w_up_f, w_down_f = full
    small = _small_allgather("conv_allgather", _pack_small(gains, conv_w[0]))
    conv_full = small[::2, 8:11, :conv_w.shape[2]].transpose(1, 0, 2).reshape(3, CONV_W)

    loss, grad_x, mat_grads, g_conv, gain_grads = _local_step(
        x[0], mem[0], positions[0], loss_target[0], gains, w_in_f, w_kv_f.reshape(D_MODEL, 2 * XATTN_W), conv_full,
        w_out_f.reshape(D_MODEL, D_MODEL), w_up_f, w_down_f.reshape(D_FF, D_MODEL))

    mat_sums = _reduce_scatter(mat_grads)
    small_sum = _sum_blocks("small_sum", _small_allgather("small_grads_allgather", _pack_small(gain_grads, g_conv)))
    gain_sums, conv_sum_full = _unpack_small(small_sum, gain_widths, CONV_W)
    conv_sum = lax.dynamic_slice_in_dim(conv_sum_full, chip * conv_w.shape[2], conv_w.shape[2], axis=1)

    mat_new = [_adamw(f"adamw_{a}", mats[a], mat_sums[a], mats_m[a], mats_v[a]) for a in range(len(mats))]
    pack = lambda gs, cv: _pack_small(gs, cv)
    small_new = _adamw("adamw_small", pack(gains, conv_w[0]), pack(gain_sums, conv_sum), pack(gains_m, m_conv_w[0]),
                       pack(gains_v, v_conv_w[0]))
    small_out = [_unpack_small(t, gain_widths, conv_w.shape[2]) for t in small_new]

    total = lax.psum(loss[0, 0], ("x", "y", "c"))
    order = ["g_pre_mix", "g_mem", "w_in", "w_mem_kv", "conv_w", "g_attn_out", "g_conv_out", "g_xattn_out", "w_out",
             "g_post_mix", "g_pre_mlp", "w_up", "w_down", "g_post_mlp"]
    gain_names = ["g_pre_mix", "g_mem", "g_attn_out", "g_conv_out", "g_xattn_out", "g_post_mix", "g_pre_mlp", "g_post_mlp"]
    mat_names = ["w_in", "w_mem_kv", "w_out", "w_up", "w_down"]

    def leaf(kind, name):
        if name in gain_names:
            i = gain_names.index(name)
            return gain_sums[i] if kind == 0 else small_out[kind - 1][0][i]
        if name == "conv_w":
            return (conv_sum if kind == 0 else small_out[kind - 1][1])[None]
        a = mat_names.index(name)
        return (mat_sums[a] if kind == 0 else mat_new[a][kind - 1])[None]

    return (total, grad_x[None], *[leaf(kind, name) for kind in range(4) for name in order])
```

```python
import functools

import jax
import jax.numpy as jnp
from jax import lax
from jax.experimental import pallas as pl
from jax.experimental.pallas import tpu as pltpu

F32, BF16 = jnp.float32, jnp.bfloat16

D_MODEL = 1024
ATTN_W = 512
CONV_W = 256
XATTN_W = 256
PROJ_W = 3 * ATTN_W + 3 * CONV_W + XATTN_W
D_FF = 4096
HEAD = 64
N_BACK = 128
DILATIONS = (1, 4, 16)
ROPE_THETA = 10000.0
EPS = 1e-6
NEG_INF = -1e30
SCALE = HEAD ** -0.5
N_CHIPS = 4
SHARD_IN = PROJ_W // N_CHIPS
SHARD_FF = D_FF // N_CHIPS

ADAM_LR, ADAM_B1, ADAM_B2, ADAM_EPS, ADAM_WD, ADAM_STEP = 0.001, 0.9, 0.999, 1e-08, 0.01, 10

VMEM_LIMIT_V7X = 56 * 1024 * 1024
ROW_TILE = 256
SMALL_ROWS = 16

NT = (((1,), (1,)), ((), ()))
TN = (((0,), (0,)), ((), ()))
MESH = pl.DeviceIdType.MESH


def _params(*sem):
    return pltpu.CompilerParams(dimension_semantics=sem, vmem_limit_bytes=VMEM_LIMIT_V7X)


def _resident(shape):
    return pl.BlockSpec(shape, lambda *_: (0,) * len(shape), pipeline_mode=pl.Buffered(1))


def _rows(tm, width):
    return pl.BlockSpec((tm, width), lambda i: (i, 0))


def _rms_hat(x):
    r = lax.rsqrt(jnp.mean(x * x, axis=-1, keepdims=True) + EPS)
    return x * r, r


def _rms_bwd(xhat, r, g, dy):
    gdy = dy * g
    return r * (gdy - xhat * jnp.mean(xhat * gdy, axis=-1, keepdims=True))


def _rope128(t, cos, sin_signed, inverse):
    lane = lax.broadcasted_iota(jnp.int32, t.shape, 1)
    first_half = (lane % HEAD) < (HEAD // 2)
    rot = jnp.where(first_half, pltpu.roll(t, 128 - HEAD // 2, 1), pltpu.roll(t, HEAD // 2, 1))
    return t * cos - rot * sin_signed if inverse else t * cos + rot * sin_signed


def _in_proj_fwd(x, g, w_in, cos, sin, tm):
    S = x.shape[0]

    def body(x_ref, g_ref, w_ref, cos_ref, sin_ref, h_ref, q_ref, k_ref, v_ref, bcu_ref, qx_ref, proj):
        xhat, _ = _rms_hat(x_ref[...])
        h = (xhat * g_ref[...]).astype(BF16)
        h_ref[...] = h
        for j in range(N_CHIPS):
            proj[:, SHARD_IN * j:SHARD_IN * (j + 1)] = jnp.dot(h, w_ref[j], preferred_element_type=F32)
        c, s = cos_ref[...], sin_ref[...]
        for j in range(ATTN_W // 128):
            lo = 128 * j
            q_ref[:, lo:lo + 128] = _rope128(proj[:, lo:lo + 128], c, s, False) * SCALE
            k_ref[:, lo:lo + 128] = _rope128(proj[:, ATTN_W + lo:ATTN_W + lo + 128], c, s, False)
        v_ref[...] = proj[:, 2 * ATTN_W:3 * ATTN_W]
        bcu_ref[...] = proj[:, 3 * ATTN_W:3 * ATTN_W + 3 * CONV_W]
        qx_ref[...] = proj[:, 3 * ATTN_W + 3 * CONV_W:PROJ_W].astype(BF16)

    return pl.pallas_call(
        body, name="in_proj_fwd", grid=(S // tm,),
        in_specs=[_rows(tm, D_MODEL), _resident((1, D_MODEL)), _resident((N_CHIPS, D_MODEL, SHARD_IN)),
                  _rows(tm, 128), _rows(tm, 128)],
        out_specs=[_rows(tm, D_MODEL), _rows(tm, ATTN_W), _rows(tm, ATTN_W), _rows(tm, ATTN_W),
                   _rows(tm, 3 * CONV_W), _rows(tm, XATTN_W)],
        out_shape=[jax.ShapeDtypeStruct((S, D_MODEL), BF16), jax.ShapeDtypeStruct((S, ATTN_W), F32),
                   jax.ShapeDtypeStruct((S, ATTN_W), F32), jax.ShapeDtypeStruct((S, ATTN_W), F32),
                   jax.ShapeDtypeStruct((S, 3 * CONV_W), F32), jax.ShapeDtypeStruct((S, XATTN_W), BF16)],
        scratch_shapes=[pltpu.VMEM((tm, PROJ_W), F32)],
        compiler_params=_params("parallel"),
    )(x, g, w_in, cos, sin)


def _memkv_fwd(mem, g_mem, w_kv):
    n_mem = mem.shape[0]

    def body(mem_ref, g_ref, w_ref, mn_ref, kv_ref):
        mhat, _ = _rms_hat(mem_ref[...])
        mn = (mhat * g_ref[...]).astype(BF16)
        mn_ref[...] = mn
        kv_ref[...] = jnp.dot(mn, w_ref[...], preferred_element_type=F32).astype(BF16)

    return pl.pallas_call(
        body, name="memkv_fwd",
        out_shape=[jax.ShapeDtypeStruct((n_mem, D_MODEL), BF16), jax.ShapeDtypeStruct((n_mem, 2 * XATTN_W), BF16)],
        compiler_params=pltpu.CompilerParams(vmem_limit_bytes=VMEM_LIMIT_V7X),
    )(mem, g_mem, w_kv)


def _fill_band_bias(bias):
    row = lax.broadcasted_iota(jnp.int32, (N_BACK, 2 * N_BACK), 0)
    col = lax.broadcasted_iota(jnp.int32, (N_BACK, 2 * N_BACK), 1)
    band = (col >= row) & (col <= row + N_BACK)
    bias[1] = jnp.where(band, 0.0, NEG_INF)
    bias[0] = jnp.where(band & (col >= N_BACK), 0.0, NEG_INF)


def _strided(start, size, d):
    return pl.ds(start, size) if d == 1 else pl.ds(start, size, stride=d)


def _block_starts(t, nb, d):
    r, n = lax.shift_right_logical(t, nb.bit_length() - 1), lax.bitwise_and(t, nb - 1)
    own = r + n * (N_BACK * d)
    prev = r + jnp.maximum(n - 1, 0) * (N_BACK * d)
    if d == 1:
        own, prev = pl.multiple_of(own, N_BACK), pl.multiple_of(prev, N_BACK)
    return own, prev, n


def _by_head(a, b):
    lane = lax.broadcasted_iota(jnp.int32, (a.shape[0], 2 * HEAD), 1)
    return jnp.where(lane < HEAD, a, b)


def _head_only(t, hh):
    lane = lax.broadcasted_iota(jnp.int32, t.shape, 1)
    return jnp.where((lane < HEAD) == (hh == 0), t, jnp.zeros_like(t))


def _stack_heads(t):
    return jnp.concatenate([_head_only(t, 0), _head_only(t, 1)], axis=0)


def _head_columns(t):
    return jnp.concatenate([t[:, 0:1], t[:, HEAD:HEAD + 1]], axis=0)


def _unstack(t):
    return _by_head(t[:N_BACK], t[N_BACK:])


def _unstack_columns(t):
    return _by_head(jnp.broadcast_to(t[:N_BACK], (N_BACK, 2 * HEAD)), jnp.broadcast_to(t[N_BACK:], (N_BACK, 2 * HEAD)))


FWD_BLOCKS_PER_STEP = 4
BWD_BLOCKS_PER_STEP = 2


def _attn_fwd(q, k, v):
    S = q.shape[0]
    U = FWD_BLOCKS_PER_STEP

    def body(q_ref, k_ref, v_ref, y_ref, m_ref, l_scr, bias):
        _fill_band_bias(bias)
        for g, d in enumerate(DILATIONS):
            nb = S // d // N_BACK
            first_pattern, last_pattern = g == 0, g == len(DILATIONS) - 1

            def step(i, carry, d=d, nb=nb, first_pattern=first_pattern, last_pattern=last_pattern):
                blocks = [_block_starts(U * i + u, nb, d) for u in range(U)]
                rows = [_strided(own, N_BACK, d) for own, _, _ in blocks]
                prev_rows = [_strided(prev, N_BACK, d) for _, prev, _ in blocks]
                ss = []
                for u, (_, _, n) in enumerate(blocks):
                    kw = jnp.concatenate([k_ref[prev_rows[u], :], k_ref[rows[u], :]], 0).astype(BF16)
                    qs = _stack_heads(q_ref[rows[u], :].astype(BF16))
                    b = bias[jnp.minimum(n, 1)]
                    ss.append(lax.dot_general(qs, kw, NT, preferred_element_type=F32) + jnp.concatenate([b, b], axis=0))
                ms = [jnp.max(s, axis=1, keepdims=True) for s in ss]
                ps = [jnp.exp(s - m) for s, m in zip(ss, ms)]
                ls = [jnp.sum(p, axis=1, keepdims=True) for p in ps]
                os_ = []
                for u in range(U):
                    vw = jnp.concatenate([v_ref[prev_rows[u], :], v_ref[rows[u], :]], 0).astype(BF16)
                    os_.append(jnp.dot(ps[u].astype(BF16), vw, preferred_element_type=F32))
                for u in range(U):
                    o_g, m_g, l_g = _unstack(os_[u]), _unstack_columns(ms[u]), _unstack_columns(ls[u])
                    r = rows[u]
                    if first_pattern:
                        m_new, l_new, acc = m_g, l_g, o_g
                    else:
                        m_old = m_ref[r, :]
                        m_new = jnp.maximum(m_old, m_g)
                        alpha, beta = jnp.exp(m_old - m_new), jnp.exp(m_g - m_new)
                        l_new = l_scr[r, :] * alpha + l_g * beta
                        acc = y_ref[r, :] * alpha + o_g * beta
                    if last_pattern:
                        y_ref[r, :] = acc / l_new
                        m_ref[r, :] = m_new + jnp.log(l_new)
                    else:
                        y_ref[r, :] = acc
                        m_ref[r, :] = m_new
                        l_scr[r, :] = l_new
                return carry

            lax.fori_loop(0, d * nb // U, step, 0)

    col = pl.BlockSpec((S, 2 * HEAD), lambda j: (0, j))
    return pl.pallas_call(
        body, name="attn_fwd", grid=(q.shape[1] // (2 * HEAD),),
        in_specs=[col, col, col], out_specs=[col, col],
        out_shape=[jax.ShapeDtypeStruct(q.shape, F32)] * 2,
        scratch_shapes=[pltpu.VMEM((S, 2 * HEAD), F32), pltpu.VMEM((2, N_BACK, 2 * N_BACK), F32)],
        compiler_params=_params("parallel"),
    )(q, k, v)


def _attn_bwd(q, k, v, dy, lse, delta):
    S = q.shape[0]
    U = BWD_BLOCKS_PER_STEP

    def body(q_ref, k_ref, v_ref, dy_ref, lse_ref, delta_ref, dq_ref, dk_ref, dv_ref, bias):
        _fill_band_bias(bias)
        dk_ref[...] = jnp.zeros_like(dk_ref)
        dv_ref[...] = jnp.zeros_like(dv_ref)
        for g, d in enumerate(DILATIONS):
            nb = S // d // N_BACK

            def step(i, carry, d=d, nb=nb, g=g):
                blocks = [_block_starts(U * i + u, nb, d) for u in range(U)]
                rows = [_strided(own, N_BACK, d) for own, _, _ in blocks]
                prev_rows = [_strided(prev, N_BACK, d) for _, prev, _ in blocks]
                kws = [jnp.concatenate([k_ref[prev_rows[u], :], k_ref[rows[u], :]], 0).astype(BF16) for u in range(U)]
                vws = [jnp.concatenate([v_ref[prev_rows[u], :], v_ref[rows[u], :]], 0).astype(BF16) for u in range(U)]
                qss = [_stack_heads(q_ref[rows[u], :].astype(BF16)) for u in range(U)]
                doss = [_stack_heads(dy_ref[rows[u], :].astype(BF16)) for u in range(U)]
                ss, dps = [], []
                for u, (_, _, n) in enumerate(blocks):
                    b = bias[jnp.minimum(n, 1)]
                    ss.append(lax.dot_general(qss[u], kws[u], NT, preferred_element_type=F32) + jnp.concatenate([b, b], axis=0))
                    dps.append(lax.dot_general(doss[u], vws[u], NT, preferred_element_type=F32))
                ps = [jnp.exp(ss[u] - _head_columns(lse_ref[rows[u], :])) for u in range(U)]
                dss = [(ps[u] * (dps[u] - _head_columns(delta_ref[rows[u], :]))).astype(BF16) for u in range(U)]
                pbs = [p.astype(BF16) for p in ps]
                dqs = [jnp.dot(dss[u], kws[u], preferred_element_type=F32) for u in range(U)]
                dkws = [lax.dot_general(dss[u], qss[u], TN, preferred_element_type=F32) for u in range(U)]
                dvws = [lax.dot_general(pbs[u], doss[u], TN, preferred_element_type=F32) for u in range(U)]
                for u in range(U):
                    dq = _unstack(dqs[u])
                    if g == 0:
                        dq_ref[rows[u], :] = dq
                    else:
                        dq_ref[rows[u], :] += dq
                    dk_ref[prev_rows[u], :] += dkws[u][:N_BACK]
                    dv_ref[prev_rows[u], :] += dvws[u][:N_BACK]
                    dk_ref[rows[u], :] += dkws[u][N_BACK:]
                    dv_ref[rows[u], :] += dvws[u][N_BACK:]
                return carry

            lax.fori_loop(0, d * nb // U, step, 0)

    col = pl.BlockSpec((S, 2 * HEAD), lambda j: (0, j))
    return pl.pallas_call(
        body, name="attn_bwd", grid=(q.shape[1] // (2 * HEAD),),
        in_specs=[col] * 6, out_specs=[col] * 3,
        out_shape=[jax.ShapeDtypeStruct(q.shape, F32)] * 3,
        scratch_shapes=[pltpu.VMEM((2, N_BACK, 2 * N_BACK), F32)],
        compiler_params=_params("parallel"),
    )(q, k, v, dy, lse, delta)


def _shift_down(z, before, k):
    row = lax.broadcasted_iota(jnp.int32, z.shape, 0)
    out = pltpu.roll(z, k, 0)
    for i in range(k):
        out = jnp.where(row == i, before[8 - k + i:8 - k + i + 1, :], out)
    return out


def _shift_up(z, after, k):
    rows = z.shape[0]
    row = lax.broadcasted_iota(jnp.int32, z.shape, 0)
    out = pltpu.roll(z, rows - k, 0)
    for i in range(k):
        out = jnp.where(row == rows - k + i, after[i:i + 1, :], out)
    return out


def _conv_fwd(bcu, before, is_first, w):
    b, c, u = bcu[:, 0:CONV_W], bcu[:, CONV_W:2 * CONV_W], bcu[:, 2 * CONV_W:3 * CONV_W]
    z = c * u
    zb = jnp.where(is_first, 0.0, before[:, CONV_W:2 * CONV_W] * before[:, 2 * CONV_W:3 * CONV_W])
    z1, z2 = _shift_down(z, zb, 1), _shift_down(z, zb, 2)
    cv = w[0:1, :] * z2 + w[1:2, :] * z1 + w[2:3, :] * z
    return b, c, u, z, z1, z2, cv


def _halo_before(tm, width):
    return pl.BlockSpec((8, width), lambda i: (jnp.maximum(i * (tm // 8) - 1, 0), 0))


def _halo_after(tm, width, S):
    return pl.BlockSpec((8, width), lambda i: (jnp.minimum((i + 1) * (tm // 8), S // 8 - 1), 0))


def _mix_fwd(ya, bcu, qx, mkv, conv_w, g_a, g_c, g_x, w_out, g_post, x, tm):
    S = x.shape[0]

    def body(ya_ref, bcu_ref, before_ref, qx_ref, mkv_ref, cw_ref, ga_ref, gc_ref, gx_ref,
             wo_ref, gp_ref, x_ref, yx_ref, ycat_ref, y2_ref, x1_ref):
        ya = ya_ref[...]
        b, _, _, _, _, _, cv = _conv_fwd(bcu_ref[...], before_ref[...], pl.program_id(0) == 0, cw_ref[...])
        yc = b * cv

        qxb, mkvb = qx_ref[...], mkv_ref[...]
        for hd in range(XATTN_W // HEAD):
            sl = slice(HEAD * hd, HEAD * (hd + 1))
            s = lax.dot_general(qxb[:, sl], mkvb[:, sl], NT, preferred_element_type=F32) * SCALE
            mx = jnp.max(s, axis=1, keepdims=True)
            p = jnp.exp(s - mx)
            l = jnp.sum(p, axis=1, keepdims=True)
            vm = mkvb[:, XATTN_W + HEAD * hd:XATTN_W + HEAD * (hd + 1)]
            yx_ref[:, sl] = jnp.dot(p.astype(BF16), vm, preferred_element_type=F32) / l
        yx = yx_ref[...]

        ycat_ref[:, 0:ATTN_W] = (_rms_hat(ya)[0] * ga_ref[...]).astype(BF16)
        ycat_ref[:, ATTN_W:ATTN_W + CONV_W] = (_rms_hat(yc)[0] * gc_ref[...]).astype(BF16)
        ycat_ref[:, ATTN_W + CONV_W:D_MODEL] = (_rms_hat(yx)[0] * gx_ref[...]).astype(BF16)
        y2 = jnp.dot(ycat_ref[...], wo_ref[...], preferred_element_type=F32)
        y2_ref[...] = y2
        x1_ref[...] = x_ref[...] + _rms_hat(y2)[0] * gp_ref[...]

    n_mem = mkv.shape[0]
    return pl.pallas_call(
        body, name="mix_fwd", grid=(S // tm,),
        in_specs=[_rows(tm, ATTN_W), _rows(tm, 3 * CONV_W), _halo_before(tm, 3 * CONV_W), _rows(tm, XATTN_W),
                  _resident((n_mem, 2 * XATTN_W)), _resident((3, CONV_W)), _resident((1, ATTN_W)),
                  _resident((1, CONV_W)), _resident((1, XATTN_W)), _resident((D_MODEL, D_MODEL)),
                  _resident((1, D_MODEL)), _rows(tm, D_MODEL)],
        out_specs=[_rows(tm, XATTN_W), _rows(tm, D_MODEL), _rows(tm, D_MODEL), _rows(tm, D_MODEL)],
        out_shape=[jax.ShapeDtypeStruct((S, XATTN_W), F32), jax.ShapeDtypeStruct((S, D_MODEL), BF16),
                   jax.ShapeDtypeStruct((S, D_MODEL), F32), jax.ShapeDtypeStruct((S, D_MODEL), F32)],
        compiler_params=_params("parallel"),
    )(ya, bcu, bcu, qx, mkv, conv_w, g_a, g_c, g_x, w_out, g_post, x)


def _mlp_fwd_bwd(x1, target, g_pre, g_post, w_up, w_down, tm):
    S = x1.shape[0]
    n_ff = D_FF // SHARD_FF

    def body(x1_ref, t_ref, gpre_ref, gpost_ref, wup_ref, wdn_ref,
             h2_ref, f_ref, du_ref, df2_ref, dx1_ref, dgpre_ref, dgpost_ref, loss_ref, u_scr):
        @pl.when(pl.program_id(0) == 0)
        def _():
            dgpre_ref[...] = jnp.zeros_like(dgpre_ref)
            dgpost_ref[...] = jnp.zeros_like(dgpost_ref)
            loss_ref[...] = jnp.zeros_like(loss_ref)

        x1 = x1_ref[...]
        x1hat, r1 = _rms_hat(x1)
        h2 = (x1hat * gpre_ref[...]).astype(BF16)
        h2_ref[...] = h2
        f2 = jnp.zeros((tm, D_MODEL), F32)
        for j in range(n_ff):
            cols = slice(SHARD_FF * j, SHARD_FF * (j + 1))
            u = jnp.maximum(jnp.dot(h2, wup_ref[j], preferred_element_type=F32), 0.0)
            u_scr[:, cols] = u
            f = (u * u).astype(BF16)
            f_ref[:, cols] = f
            f2 = f2 + jnp.dot(f, wdn_ref[cols, :], preferred_element_type=F32)
        f2hat, r2 = _rms_hat(f2)
        err = x1 + f2hat * gpost_ref[...] - t_ref[...]
        loss_ref[...] += 0.5 * jnp.sum(jnp.mean(err * err, axis=-1, keepdims=True), axis=0, keepdims=True)
        dx2 = err * (1.0 / D_MODEL)
        dgpost_ref[...] += jnp.sum(dx2 * f2hat, axis=0, keepdims=True)
        df2 = _rms_bwd(f2hat, r2, gpost_ref[...], dx2).astype(BF16)
        df2_ref[...] = df2
        dh2 = jnp.zeros((tm, D_MODEL), F32)
        for j in range(n_ff):
            cols = slice(SHARD_FF * j, SHARD_FF * (j + 1))
            df = lax.dot_general(df2, wdn_ref[cols, :], NT, preferred_element_type=F32)
            du = (2.0 * u_scr[:, cols] * df).astype(BF16)
            du_ref[:, cols] = du
            dh2 = dh2 + lax.dot_general(du, wup_ref[j], NT, preferred_element_type=F32)
        dgpre_ref[...] += jnp.sum(dh2 * x1hat, axis=0, keepdims=True)
        dx1_ref[...] = dx2 + _rms_bwd(x1hat, r1, gpre_ref[...], dh2)

    acc = pl.BlockSpec((1, D_MODEL), lambda i: (0, 0))
    return pl.pallas_call(
        body, name="mlp_fwd_bwd", grid=(S // tm,),
        in_specs=[_rows(tm, D_MODEL), _rows(tm, D_MODEL), _resident((1, D_MODEL)), _resident((1, D_MODEL)),
                  _resident((n_ff, D_MODEL, SHARD_FF)), _resident((D_FF, D_MODEL))],
        out_specs=[_rows(tm, D_MODEL), _rows(tm, D_FF), _rows(tm, D_FF), _rows(tm, D_MODEL), _rows(tm, D_MODEL),
                   acc, acc, pl.BlockSpec((1, 1), lambda i: (0, 0))],
        out_shape=[jax.ShapeDtypeStruct((S, D_MODEL), BF16), jax.ShapeDtypeStruct((S, D_FF), BF16),
                   jax.ShapeDtypeStruct((S, D_FF), BF16), jax.ShapeDtypeStruct((S, D_MODEL), BF16),
                   jax.ShapeDtypeStruct((S, D_MODEL), F32), jax.ShapeDtypeStruct((1, D_MODEL), F32),
                   jax.ShapeDtypeStruct((1, D_MODEL), F32), jax.ShapeDtypeStruct((1, 1), F32)],
        scratch_shapes=[pltpu.VMEM((tm, D_FF), F32)],
        compiler_params=_params("arbitrary"),
    )(x1, target, g_pre, g_post, w_up, w_down)


def _weight_grad(name, a, b, rows_sharded):
    S, K = a.shape
    N = b.shape[1]
    if rows_sharded:
        tk, tn = K // N_CHIPS, N
        a_spec = pl.BlockSpec((S, tk), lambda j: (0, j))
        b_spec = pl.BlockSpec((S, tn), lambda j: (0, 0), pipeline_mode=pl.Buffered(1))
    else:
        tk, tn = K, N // N_CHIPS
        a_spec = pl.BlockSpec((S, tk), lambda j: (0, 0), pipeline_mode=pl.Buffered(1))
        b_spec = pl.BlockSpec((S, tn), lambda j: (0, j))
    half = tk // 2

    def body(a_ref, b_ref, o_ref):
        res = lax.dot_general(a_ref[...], b_ref[...], TN, preferred_element_type=F32)
        o_ref[0, 0] = res[:half]
        o_ref[1, 0] = res[half:]

    return pl.pallas_call(
        body, name=name, grid=(N_CHIPS,), in_specs=[a_spec, b_spec],
        out_specs=pl.BlockSpec((2, 1, half, tn), lambda j: (0, j, 0, 0)),
        out_shape=jax.ShapeDtypeStruct((2, N_CHIPS, half, tn), F32),
        compiler_params=_params("parallel"),
    )(a, b)


def _mix_bwd(dx1, y2, ya, yx, bcu, conv_w, g_a, g_c, g_x, w_out, g_post, tm):
    S = dx1.shape[0]

    def body(dx1_ref, y2_ref, ya_ref, yx_ref, bcu_ref, before_ref, cw_ref, ga_ref, gc_ref, gx_ref, wo_ref, gp_ref,
             dy2_ref, dya_ref, delta_ref, dycx_ref, dgp_ref, dga_ref, dgc_ref, dgx_ref):
        @pl.when(pl.program_id(0) == 0)
        def _():
            for ref in (dgp_ref, dga_ref, dgc_ref, dgx_ref):
                ref[...] = jnp.zeros_like(ref)

        dx1 = dx1_ref[...]
        y2hat, r2 = _rms_hat(y2_ref[...])
        dgp_ref[...] += jnp.sum(dx1 * y2hat, axis=0, keepdims=True)
        dy2 = _rms_bwd(y2hat, r2, gp_ref[...], dx1).astype(BF16)
        dy2_ref[...] = dy2
        dycat = lax.dot_general(dy2, wo_ref[...], NT, preferred_element_type=F32)

        d_na = dycat[:, 0:ATTN_W]
        ya = ya_ref[...]
        yahat, ra = _rms_hat(ya)
        dga_ref[...] += jnp.sum(d_na * yahat, axis=0, keepdims=True)
        dya = _rms_bwd(yahat, ra, ga_ref[...], d_na)
        dya_ref[...] = dya
        prod = dya * ya
        hi = prod.astype(BF16)
        lo = (prod - hi.astype(F32)).astype(BF16)
        head_of = lambda axis: lax.shift_right_logical(lax.broadcasted_iota(jnp.int32, (ATTN_W, ATTN_W), axis),
                                                       HEAD.bit_length() - 1)
        same_head = head_of(0) == head_of(1)
        ones = jnp.where(same_head, 1.0, 0.0).astype(BF16)
        delta_ref[...] = (jnp.dot(hi, ones, preferred_element_type=F32) + jnp.dot(lo, ones, preferred_element_type=F32))

        b, _, _, _, _, _, cv = _conv_fwd(bcu_ref[...], before_ref[...], pl.program_id(0) == 0, cw_ref[...])
        d_nc = dycat[:, ATTN_W:ATTN_W + CONV_W]
        ychat, rc = _rms_hat(b * cv)
        dgc_ref[...] += jnp.sum(d_nc * ychat, axis=0, keepdims=True)
        dycx_ref[:, 0:CONV_W] = _rms_bwd(ychat, rc, gc_ref[...], d_nc)

        d_nx = dycat[:, ATTN_W + CONV_W:D_MODEL]
        yxhat, rx = _rms_hat(yx_ref[...])
        dgx_ref[...] += jnp.sum(d_nx * yxhat, axis=0, keepdims=True)
        dycx_ref[:, CONV_W:CONV_W + XATTN_W] = _rms_bwd(yxhat, rx, gx_ref[...], d_nx)

    acc = lambda w: pl.BlockSpec((1, w), lambda i: (0, 0))
    return pl.pallas_call(
        body, name="mix_bwd", grid=(S // tm,),
        in_specs=[_rows(tm, D_MODEL), _rows(tm, D_MODEL), _rows(tm, ATTN_W), _rows(tm, XATTN_W),
                  _rows(tm, 3 * CONV_W), _halo_before(tm, 3 * CONV_W), _resident((3, CONV_W)),
                  _resident((1, ATTN_W)), _resident((1, CONV_W)), _resident((1, XATTN_W)),
                  _resident((D_MODEL, D_MODEL)), _resident((1, D_MODEL))],
        out_specs=[_rows(tm, D_MODEL), _rows(tm, ATTN_W), _rows(tm, ATTN_W), _rows(tm, CONV_W + XATTN_W),
                   acc(D_MODEL), acc(ATTN_W), acc(CONV_W), acc(XATTN_W)],
        out_shape=[jax.ShapeDtypeStruct((S, D_MODEL), BF16), jax.ShapeDtypeStruct((S, ATTN_W), F32),
                   jax.ShapeDtypeStruct((S, ATTN_W), F32),
                   jax.ShapeDtypeStruct((S, CONV_W + XATTN_W), F32), jax.ShapeDtypeStruct((1, D_MODEL), F32),
                   jax.ShapeDtypeStruct((1, ATTN_W), F32), jax.ShapeDtypeStruct((1, CONV_W), F32),
                   jax.ShapeDtypeStruct((1, XATTN_W), F32)],
        compiler_params=_params("arbitrary"),
    )(dx1, y2, ya, yx, bcu, bcu, conv_w, g_a, g_c, g_x, w_out, g_post)


def _conv_xattn_bwd(dycx, bcu, qx, mkv, conv_w, tm):
    S = dycx.shape[0]
    n_mem = mkv.shape[0]
    n_tiles = S // tm

    def body(d_ref, dafter_ref, bcu_ref, before_ref, after_ref, qx_ref, mkv_ref, cw_ref,
             tail_ref, dmkv_ref, dcw_ref):
        i = pl.program_id(0)

        @pl.when(i == 0)
        def _():
            dmkv_ref[...] = jnp.zeros_like(dmkv_ref)
            dcw_ref[...] = jnp.zeros_like(dcw_ref)

        w = cw_ref[...]
        b, c, u, z, z1, z2, cv = _conv_fwd(bcu_ref[...], before_ref[...], i == 0, w)
        dyc = d_ref[:, 0:CONV_W]
        dcv = dyc * b
        dcv_after = jnp.where(i == n_tiles - 1, 0.0, dafter_ref[:, 0:CONV_W] * after_ref[:, 0:CONV_W])
        dz = w[2:3, :] * dcv + w[1:2, :] * _shift_up(dcv, dcv_after, 1) + w[0:1, :] * _shift_up(dcv, dcv_after, 2)
        dcw_ref[0:1, :] += jnp.sum(dcv * z2, axis=0, keepdims=True)
        dcw_ref[1:2, :] += jnp.sum(dcv * z1, axis=0, keepdims=True)
        dcw_ref[2:3, :] += jnp.sum(dcv * z, axis=0, keepdims=True)
        tail_ref[:, 0:CONV_W] = (dyc * cv).astype(BF16)
        tail_ref[:, CONV_W:2 * CONV_W] = (dz * u).astype(BF16)
        tail_ref[:, 2 * CONV_W:3 * CONV_W] = (dz * c).astype(BF16)

        qxb, mkvb = qx_ref[...], mkv_ref[...]
        for hd in range(XATTN_W // HEAD):
            sl = slice(HEAD * hd, HEAD * (hd + 1))
            vsl = slice(XATTN_W + HEAD * hd, XATTN_W + HEAD * (hd + 1))
            s = lax.dot_general(qxb[:, sl], mkvb[:, sl], NT, preferred_element_type=F32) * SCALE
            e = jnp.exp(s - jnp.max(s, axis=1, keepdims=True))
            p = e / jnp.sum(e, axis=1, keepdims=True)
            dob = d_ref[:, CONV_W + HEAD * hd:CONV_W + HEAD * (hd + 1)].astype(BF16)
            dp = lax.dot_general(dob, mkvb[:, vsl], NT, preferred_element_type=F32)
            ds = (p * (dp - jnp.sum(p * dp, axis=1, keepdims=True)) * SCALE).astype(BF16)
            tail_ref[:, 3 * CONV_W + HEAD * hd:3 * CONV_W + HEAD * (hd + 1)] = jnp.dot(
                ds, mkvb[:, sl], preferred_element_type=F32).astype(BF16)
            dmkv_ref[:, sl] += lax.dot_general(ds, qxb[:, sl], TN, preferred_element_type=F32)
            dmkv_ref[:, vsl] += lax.dot_general(p.astype(BF16), dob, TN, preferred_element_type=F32)

    width = CONV_W + XATTN_W
    return pl.pallas_call(
        body, name="conv_xattn_bwd", grid=(n_tiles,),
        in_specs=[_rows(tm, width), _halo_after(tm, width, S), _rows(tm, 3 * CONV_W), _halo_before(tm, 3 * CONV_W),
                  _halo_after(tm, 3 * CONV_W, S), _rows(tm, XATTN_W), _resident((n_mem, 2 * XATTN_W)),
                  _resident((3, CONV_W))],
        out_specs=[_rows(tm, 3 * CONV_W + XATTN_W), pl.BlockSpec((n_mem, 2 * XATTN_W), lambda i: (0, 0)),
                   pl.BlockSpec((3, CONV_W), lambda i: (0, 0))],
        out_shape=[jax.ShapeDtypeStruct((S, 3 * CONV_W + XATTN_W), BF16),
                   jax.ShapeDtypeStruct((n_mem, 2 * XATTN_W), F32), jax.ShapeDtypeStruct((3, CONV_W), F32)],
        compiler_params=_params("arbitrary"),
    )(dycx, dycx, bcu, bcu, bcu, qx, mkv, conv_w)


def _memkv_bwd(mem, g_mem, w_kv, dmkv):
    n_mem = mem.shape[0]
    half = D_MODEL // N_CHIPS // 2

    def body(mem_ref, g_ref, w_ref, d_ref, dw_ref, dg_ref):
        mhat, _ = _rms_hat(mem_ref[...])
        mn = (mhat * g_ref[...]).astype(BF16)
        d = d_ref[...].astype(BF16)
        for k in range(2 * N_CHIPS):
            dw_ref[k % 2, k // 2] = lax.dot_general(mn[:, half * k:half * (k + 1)], d, TN, preferred_element_type=F32)
        dmn = lax.dot_general(d, w_ref[...], NT, preferred_element_type=F32)
        dg_ref[...] = jnp.sum(dmn * mhat, axis=0, keepdims=True)

    return pl.pallas_call(
        body, name="memkv_bwd",
        out_shape=[jax.ShapeDtypeStruct((2, N_CHIPS, half, 2 * XATTN_W), F32), jax.ShapeDtypeStruct((1, D_MODEL), F32)],
        compiler_params=pltpu.CompilerParams(vmem_limit_bytes=VMEM_LIMIT_V7X),
    )(mem, g_mem, w_kv, dmkv)


def _in_proj_bwd(dqkv, tail, cos, sin, w_in, x, g, dx1, tm):
    S = x.shape[0]

    def body(dq_ref, dk_ref, dv_ref, tail_ref, cos_ref, sin_ref, w_ref, x_ref, g_ref, dx1_ref, dproj_ref, dx_ref, dg_ref):
        @pl.when(pl.program_id(0) == 0)
        def _():
            dg_ref[...] = jnp.zeros_like(dg_ref)

        c, s = cos_ref[...], sin_ref[...]
        for j in range(ATTN_W // 128):
            cols = slice(128 * j, 128 * (j + 1))
            dproj_ref[:, cols] = _rope128(dq_ref[:, cols] * SCALE, c, s, True).astype(BF16)
            dproj_ref[:, ATTN_W + 128 * j:ATTN_W + 128 * (j + 1)] = _rope128(dk_ref[:, cols], c, s, True).astype(BF16)
        dproj_ref[:, 2 * ATTN_W:3 * ATTN_W] = dv_ref[...].astype(BF16)
        dproj_ref[:, 3 * ATTN_W:PROJ_W] = tail_ref[...]
        dh = jnp.zeros((tm, D_MODEL), F32)
        for j in range(N_CHIPS):
            dh = dh + lax.dot_general(dproj_ref[:, SHARD_IN * j:SHARD_IN * (j + 1)], w_ref[j], NT,
                                      preferred_element_type=F32)
        xhat, r = _rms_hat(x_ref[...])
        dg_ref[...] += jnp.sum(dh * xhat, axis=0, keepdims=True)
        dx_ref[...] = dx1_ref[...] + _rms_bwd(xhat, r, g_ref[...], dh)

    return pl.pallas_call(
        body, name="in_proj_bwd", grid=(S // tm,),
        in_specs=[_rows(tm, ATTN_W)] * 3 + [_rows(tm, PROJ_W - 3 * ATTN_W), _rows(tm, 128), _rows(tm, 128),
                  _resident((N_CHIPS, D_MODEL, SHARD_IN)), _rows(tm, D_MODEL), _resident((1, D_MODEL)),
                  _rows(tm, D_MODEL)],
        out_specs=[_rows(tm, PROJ_W), _rows(tm, D_MODEL), pl.BlockSpec((1, D_MODEL), lambda i: (0, 0))],
        out_shape=[jax.ShapeDtypeStruct((S, PROJ_W), BF16), jax.ShapeDtypeStruct((S, D_MODEL), F32),
                   jax.ShapeDtypeStruct((1, D_MODEL), F32)],
        compiler_params=_params("arbitrary"),
    )(*dqkv, tail, cos, sin, w_in, x, g, dx1)


def _row_tile(rows):
    return ROW_TILE if rows % ROW_TILE == 0 else rows


def _pair_sum_bf16(name, a, b):
    n, rows, cols = a.shape
    tr = _row_tile(rows)

    def body(a_ref, b_ref, o_ref):
        o_ref[...] = (a_ref[...] + b_ref[...]).astype(BF16)

    spec = pl.BlockSpec((1, tr, cols), lambda s, i: (s, i, 0))
    return pl.pallas_call(
        body, name=name, grid=(n, rows // tr), in_specs=[spec, spec], out_specs=spec,
        out_shape=jax.ShapeDtypeStruct(a.shape, BF16), compiler_params=_params("parallel", "parallel"),
    )(a, b)


def _final_sum(name, own, sibling, others):
    rows, cols = own.shape
    tr = _row_tile(rows)

    def body(own_ref, sib_ref, o0, o1, o2, out_ref):
        acc = own_ref[...] + sib_ref[...]
        for o in (o0, o1, o2):
            acc = acc + o[0].astype(F32)
        out_ref[...] = acc

    spec = pl.BlockSpec((tr, cols), lambda i: (i, 0))
    other = lambda k: pl.BlockSpec((1, tr, cols), lambda i: (k, i, 0))
    return pl.pallas_call(
        body, name=name, grid=(rows // tr,), in_specs=[spec, spec, other(0), other(1), other(2)], out_specs=spec,
        out_shape=jax.ShapeDtypeStruct(own.shape, F32), compiler_params=_params("parallel"),
    )(own, sibling, others, others, others)


def _adamw(name, w, g, m, v):
    rows, cols = w.shape
    tr = _row_tile(rows)

    def body(w_ref, g_ref, m_ref, v_ref, d_ref, nm_ref, nv_ref):
        g = g_ref[...]
        m = ADAM_B1 * m_ref[...] + (1.0 - ADAM_B1) * g
        v = ADAM_B2 * v_ref[...] + (1.0 - ADAM_B2) * (g * g)
        m_hat = m / (1.0 - ADAM_B1 ** ADAM_STEP)
        v_hat = v / (1.0 - ADAM_B2 ** ADAM_STEP)
        d_ref[...] = -ADAM_LR * (m_hat / (jnp.sqrt(v_hat) + ADAM_EPS) + ADAM_WD * w_ref[...])
        nm_ref[...] = m
        nv_ref[...] = v

    spec = pl.BlockSpec((tr, cols), lambda i: (i, 0))
    return pl.pallas_call(
        body, name=name, grid=(rows // tr,), in_specs=[spec] * 4, out_specs=[spec] * 3,
        out_shape=[jax.ShapeDtypeStruct(w.shape, F32)] * 3, compiler_params=_params("parallel"),
    )(w, g, m, v)


def _sum_blocks(name, blocks):
    n, rows, cols = blocks.shape

    def body(b_ref, o_ref):
        acc = b_ref[0]
        for k in range(1, n):
            acc = acc + b_ref[k]
        o_ref[...] = acc

    return pl.pallas_call(body, name=name, out_shape=jax.ShapeDtypeStruct((rows, cols), F32))(blocks)


def _place():
    return lax.axis_index("x"), lax.axis_index("y"), lax.axis_index("c")


def _other_chips(x, y):
    return [(1 - x, y), (x, 1 - y), (1 - x, 1 - y)]


def _small_allgather(name, block):
    rows, cols = block.shape
    relations = [(dx, dy, dc) for dx in (0, 1) for dy in (0, 1) for dc in (0, 1) if (dx, dy, dc) != (0, 0, 0)]

    def body(x_ref, out_ref, send_sems, recv_sems, local_sem):
        x, y, c = _place()

        def peer(rel):
            return (1 - x if rel[0] else x, 1 - y if rel[1] else y, 1 - c if rel[2] else c)

        def index(p):
            return 4 * p[0] + 2 * p[1] + p[2]

        def copy(k, origin, to):
            return pltpu.make_async_remote_copy(
                src_ref=x_ref, dst_ref=out_ref.at[index(origin)], send_sem=send_sems.at[k], recv_sem=recv_sems.at[k],
                device_id=to, device_id_type=MESH)

        mine = pltpu.make_async_copy(x_ref, out_ref.at[index((x, y, c))], local_sem)
        mine.start()
        sends = [copy(k, (x, y, c), peer(rel)) for k, rel in enumerate(relations)]
        for cp in sends:
            cp.start()
        for k, rel in enumerate(relations):
            copy(k, peer(rel), (x, y, c)).wait_recv()
        for cp in sends:
            cp.wait_send()
        mine.wait()

    return pl.pallas_call(
        body, name=name, out_shape=jax.ShapeDtypeStruct((8, rows, cols), F32),
        in_specs=[pl.BlockSpec(memory_space=pltpu.VMEM)], out_specs=pl.BlockSpec(memory_space=pltpu.VMEM),
        scratch_shapes=[pltpu.SemaphoreType.DMA((7,)), pltpu.SemaphoreType.DMA((7,)), pltpu.SemaphoreType.DMA],
    )(block)


def _weights_allgather(name, shards, landed=None):
    n = len(shards)
    first_hop = landed is None

    def body(*refs):
        ins, outs, stage = refs[:n], refs[-3 - 2 * n:-3 - n], refs[-3 - n:-3]
        send_sems, recv_sems, local_sems = refs[-3:]
        x, y, c = _place()
        me, sibling = (x, y, c), (x, y, 1 - c)
        chips = _other_chips(x, y)
        chip_index = lambda chip: 2 * chip[0] + chip[1]

        def copy(a, k, chip, half, to, src=None):
            place = outs[a].at[chip_index(chip), half]
            return pltpu.make_async_remote_copy(
                src_ref=place if src is None else src, dst_ref=place, send_sem=send_sems.at[6 * a + k],
                recv_sem=recv_sems.at[6 * a + k], device_id=to, device_id_type=MESH)

        load = [pltpu.make_async_copy(ins[a], stage[a], local_sems.at[a]) for a in range(n)]
        local = [pltpu.make_async_copy(stage[a], outs[a].at[chip_index((x, y))], local_sems.at[a]) for a in range(n)]
        for cp in load:
            cp.start()
        first = []
        if first_hop:
            first = [copy(a, k, (x, y), c, (*chip, c), src=ins[a].at[c]) for a in range(n) for k, chip in enumerate(chips)]
        for cp in first:
            cp.start()
        for a in range(n):
            load[a].wait()
            local[a].start()
        passed = []
        for a in range(n):
            for k, chip in enumerate(chips):
                if first_hop:
                    copy(a, k, chip, c, me).wait_recv()
                passed.append(copy(a, 3 + k, chip, c, sibling))
                passed[-1].start()
        for a in range(n):
            for k, chip in enumerate(chips):
                copy(a, 3 + k, chip, 1 - c, me).wait_recv()
        for cp in first + passed:
            cp.wait_send()
        for cp in local:
            cp.wait()

    any_spec = pl.BlockSpec(memory_space=pl.ANY)
    operands = list(shards) + ([] if first_hop else list(landed))
    return pl.pallas_call(
        body, name=name,
        out_shape=[jax.ShapeDtypeStruct((N_CHIPS,) + s.shape, s.dtype) for s in shards],
        in_specs=[any_spec] * len(operands), out_specs=[any_spec] * n,
        input_output_aliases={} if first_hop else {n + a: a for a in range(n)},
        scratch_shapes=[pltpu.VMEM(s.shape, s.dtype) for s in shards]
        + [pltpu.SemaphoreType.DMA((6 * n,)), pltpu.SemaphoreType.DMA((6 * n,)), pltpu.SemaphoreType.DMA((n,))],
        compiler_params=pltpu.CompilerParams(vmem_limit_bytes=VMEM_LIMIT_V7X),
    )(*operands)


def _plan_first_hop(x, y, c, shards, lands):
    return [(shards[a].at[c], lands[a].at[2 * x + y, c], lands[a].at[2 * chip[0] + chip[1], c], (*chip, c))
            for a in range(len(shards)) for chip in _other_chips(x, y)]


def _plan_other_half_to_sibling(x, y, c, grads, lands):
    return [(grads[a].at[1 - c], lands[a], lands[a], (x, y, 1 - c)) for a in range(len(grads))]


def _plan_to_other_chips(x, y, c, partials, lands):
    return [(partials[a].at[2 * chip[0] + chip[1]], lands[a].at[k], lands[a].at[k], (*chip, c))
            for a in range(len(partials)) for k, chip in enumerate(_other_chips(x, y))]


def _planned_copies(plan, srcs, lands, send_sems, recv_sems):
    x, y, c = _place()

    def pair(k, src, there, here, to):
        make = lambda dst: pltpu.make_async_remote_copy(
            src_ref=src, dst_ref=dst, send_sem=send_sems.at[k], recv_sem=recv_sems.at[k], device_id=to, device_id_type=MESH)
        return make(there), make(here)

    return [pair(k, *entry) for k, entry in enumerate(plan(x, y, c, srcs, lands))]


def _exchange(name, plan, n_copies, srcs, land_shapes):
    ns, nl = len(srcs), len(land_shapes)

    def body(*refs):
        pairs = _planned_copies(plan, refs[:ns], refs[ns:ns + nl], refs[ns + nl], refs[ns + nl + 1])
        for send, _ in pairs:
            send.start()
        for send, recv in pairs:
            send.wait_send()
            recv.wait_recv()

    any_spec = pl.BlockSpec(memory_space=pl.ANY)
    return pl.pallas_call(
        body, name=name, out_shape=list(land_shapes), in_specs=[any_spec] * ns, out_specs=[any_spec] * nl,
        scratch_shapes=[pltpu.SemaphoreType.DMA((n_copies,)), pltpu.SemaphoreType.DMA((n_copies,))],
    )(*srcs)


_HBM_SPEC = pl.BlockSpec(memory_space=pltpu.HBM)
_SEM_SPEC = pl.BlockSpec(memory_space=pltpu.SEMAPHORE)


def _hbm(a):
    return pltpu.with_memory_space_constraint(a, pltpu.HBM)


def _exchange_start(name, plan, n_copies, srcs, land_shapes):
    ns, nl = len(srcs), len(land_shapes)

    def body(*refs):
        for send, _ in _planned_copies(plan, refs[:ns], refs[ns:ns + nl], refs[ns + nl], refs[ns + nl + 1]):
            send.start()
        refs[-1][...] = jnp.zeros_like(refs[-1])

    out = pl.pallas_call(
        body, name=name,
        out_shape=(pltpu.SemaphoreType.DMA((n_copies,)), pltpu.SemaphoreType.DMA((n_copies,)),
                   *[pltpu.HBM(s.shape, s.dtype) for s in land_shapes], jax.ShapeDtypeStruct((8, 128), F32)),
        in_specs=[_HBM_SPEC] * (ns + nl),
        out_specs=(_SEM_SPEC, _SEM_SPEC, *[_HBM_SPEC] * nl, pl.BlockSpec(memory_space=pltpu.VMEM)),
        input_output_aliases={ns + i: 2 + i for i in range(nl)},
        compiler_params=pltpu.CompilerParams(has_side_effects=pltpu.SideEffectType.DATAFLOW_SIDE_EFFECTING),
    )(*[_hbm(s) for s in srcs], *[_hbm(lax.empty(s.shape, s.dtype)) for s in land_shapes])
    return out[0], out[1], list(out[2:2 + nl]), out[-1][0, 0]


def _exchange_wait(name, plan, srcs, started, after):
    send_sems, recv_sems, lands, _ = started
    ns, nl = len(srcs), len(lands)

    def body(*refs):
        for send, recv in _planned_copies(plan, refs[:ns], refs[ns:ns + nl], refs[ns + nl], refs[ns + nl + 1]):
            send.wait_send()
            recv.wait_recv()

    return pl.pallas_call(
        body, name=name, out_shape=[pltpu.HBM(l.shape, l.dtype) for l in lands],
        in_specs=[_HBM_SPEC] * (ns + nl) + [_SEM_SPEC, _SEM_SPEC, pl.BlockSpec(memory_space=pl.ANY)],
        out_specs=[_HBM_SPEC] * nl, input_output_aliases={ns + i: i for i in range(nl)},
        compiler_params=pltpu.CompilerParams(has_side_effects=pltpu.SideEffectType.DATAFLOW_SIDE_EFFECTING),
    )(*[_hbm(s) for s in srcs], *lands, send_sems, recv_sems, after)


def _exchange_halves(halves):
    n = len(halves)

    def body(*refs):
        ins, outs, stage = refs[:n], refs[n:2 * n], refs[2 * n:3 * n]
        send_sems, recv_sems, local_sems = refs[3 * n:]
        x, y, c = _place()
        load = [pltpu.make_async_copy(ins[a], stage[a], local_sems.at[a]) for a in range(n)]
        local = [pltpu.make_async_copy(stage[a], outs[a].at[c], local_sems.at[a]) for a in range(n)]
        remote = [pltpu.make_async_remote_copy(
            src_ref=stage[a], dst_ref=outs[a].at[c], send_sem=send_sems.at[a], recv_sem=recv_sems.at[a],
            device_id=(x, y, 1 - c), device_id_type=MESH) for a in range(n)]
        for cp in load:
            cp.start()
        for a in range(n):
            load[a].wait()
            remote[a].start()
            local[a].start()
        for a in range(n):
            pltpu.make_async_remote_copy(
                src_ref=ins[a], dst_ref=outs[a].at[1 - c], send_sem=send_sems.at[a], recv_sem=recv_sems.at[a],
                device_id=(x, y, 1 - c), device_id_type=MESH).wait_recv()
        for cp in remote:
            cp.wait_send()
        for cp in local:
            cp.wait()

    any_spec = pl.BlockSpec(memory_space=pl.ANY)
    return pl.pallas_call(
        body, name="sums_to_sibling",
        out_shape=[jax.ShapeDtypeStruct((2,) + h.shape, h.dtype) for h in halves],
        in_specs=[any_spec] * n, out_specs=[any_spec] * n,
        scratch_shapes=[pltpu.VMEM(h.shape, h.dtype) for h in halves]
        + [pltpu.SemaphoreType.DMA((n,)), pltpu.SemaphoreType.DMA((n,)), pltpu.SemaphoreType.DMA((n,))],
        compiler_params=pltpu.CompilerParams(vmem_limit_bytes=VMEM_LIMIT_V7X),
    )(*halves)


def _like(arrays, lead, dtype=None):
    return [jax.ShapeDtypeStruct(tuple(lead) + a.shape[-2:], dtype or a.dtype) for a in arrays]


class _StepExchanges:
    def __init__(self, mats, gains, conv_w):
        self.shards = [w.astype(BF16).reshape(2, w.shape[0] // 2, w.shape[1]) for w in mats]
        (w_in,) = _weights_allgather("w_in_allgather", self.shards[:1])
        self.w_in = w_in.reshape(N_CHIPS, 2 * w_in.shape[2], w_in.shape[3])
        small = _small_allgather("conv_allgather", _pack_small(gains, conv_w))
        self.conv_w = small[::2, 8:11, :conv_w.shape[1]].transpose(1, 0, 2).reshape(3, CONV_W)
        rest = self.shards[1:]
        self._rest = _exchange_start("rest_allgather_start", _plan_first_hop, 3 * len(rest), rest,
                                     _like(rest, (N_CHIPS, 2)))
        self.zero = self._rest[3]

    def rest_weights(self, after):
        rest = self.shards[1:]
        landed = _exchange_wait("rest_allgather_wait", _plan_first_hop, rest, self._rest, after)
        full = _weights_allgather("rest_allgather_finish", rest, landed=landed)
        return [g.reshape(N_CHIPS, 2 * g.shape[2], g.shape[3]) for g in full]

    def early_grads(self, grads):
        self._early = list(grads)
        self._early_sibling = _exchange_start("early_grads_to_sibling_start", _plan_other_half_to_sibling,
                                              len(grads), self._early, _like(grads, (N_CHIPS,)))
        self.zero = self._early_sibling[3]

    def early_grads_on(self, after):
        c = lax.axis_index("c")
        from_sibling = _exchange_wait("early_grads_to_sibling_wait", _plan_other_half_to_sibling, self._early,
                                      self._early_sibling, after)
        self._early_mine = [lax.dynamic_index_in_dim(g, c, 0, keepdims=False) for g in self._early]
        self._early_from_sibling = from_sibling
        self._early_partials = [_pair_sum_bf16(f"early_chip_sum_{a}", self._early_mine[a], from_sibling[a])
                                for a in range(len(self._early))]
        self._early_chips = _exchange_start("early_grads_to_chips_start", _plan_to_other_chips, 3 * len(self._early),
                                            self._early_partials, _like(self._early_partials, (3,)))
        self.zero = self._early_chips[3]

    def reduce_scatter(self, late_grads):
        x, y, c = _place()
        j = 2 * x + y
        n = len(late_grads)
        from_sibling = _exchange("late_grads_to_sibling", _plan_other_half_to_sibling, n, late_grads,
                                 _like(late_grads, (N_CHIPS,)))
        mine = [lax.dynamic_index_in_dim(g, c, 0, keepdims=False) for g in late_grads]
        partials = [_pair_sum_bf16(f"late_chip_sum_{a}", mine[a], from_sibling[a]) for a in range(n)]
        from_chips = _exchange("late_grads_to_chips", _plan_to_other_chips, 3 * n, partials, _like(partials, (3,)))
        early_from_chips = _exchange_wait("early_grads_to_chips_wait", _plan_to_other_chips, self._early_partials,
                                          self._early_chips, from_chips[0])
        mine, from_sibling = mine + self._early_mine, list(from_sibling) + list(self._early_from_sibling)
        from_chips = list(from_chips) + list(early_from_chips)
        own = [lax.dynamic_index_in_dim(m, j, 0, keepdims=False) for m in mine]
        sib = [lax.dynamic_index_in_dim(s, j, 0, keepdims=False) for s in from_sibling]
        halves = [_final_sum(f"final_sum_{a}", own[a], sib[a], from_chips[a]) for a in range(len(mine))]
        return [t.reshape(2 * t.shape[1], t.shape[2]) for t in _exchange_halves(halves)]


def _rope_tables(positions):
    half = HEAD // 2
    inv_freq = jnp.float32(ROPE_THETA) ** (-(jnp.arange(half, dtype=F32) * 2.0 / HEAD))
    ang = positions.astype(F32)[:, None] * inv_freq
    cos, sin = jnp.cos(ang), jnp.sin(ang)
    return jnp.tile(cos, (1, 4)), jnp.tile(jnp.concatenate([-sin, sin], axis=1), (1, 2))


def _local_step(x, mem, positions, target, gains, ex):
    g_pre_mix, g_mem, g_a, g_c, g_x, g_post_mix, g_pre_mlp, g_post_mlp = gains
    tm = ROW_TILE
    cos, sin = _rope_tables(positions)
    w_in, conv_w = ex.w_in, ex.conv_w

    h, q, k, v, bcu, qx = _in_proj_fwd(x, g_pre_mix + ex.zero, w_in, cos, sin, tm)
    ya, lse = _attn_fwd(q, k, v)
    w_kv, w_out, w_up, w_down = ex.rest_weights(lse)
    w_kv, w_out, w_down = (w.reshape(N_CHIPS * w.shape[1], w.shape[2]) for w in (w_kv, w_out, w_down))
    memn, mkv = _memkv_fwd(mem, g_mem, w_kv)
    yx, ycat, y2, x1 = _mix_fwd(ya, bcu, qx, mkv, conv_w, g_a, g_c, g_x, w_out, g_post_mix, x, tm)
    h2, f, du, df2, dx1, dg_pre_mlp, dg_post_mlp, loss = _mlp_fwd_bwd(x1, target, g_pre_mlp, g_post_mlp, w_up, w_down, tm)
    gw_down = _weight_grad("grad_w_down", f, df2, True)
    gw_up = _weight_grad("grad_w_up", h2, du, False)
    ex.early_grads([gw_up, gw_down])

    dy2, dya, delta, dycx, dg_post_mix, dg_a, dg_c, dg_x = _mix_bwd(dx1, y2, ya, yx, bcu, conv_w, g_a, g_c, g_x,
                                                                  w_out, g_post_mix + ex.zero, tm)
    ex.early_grads_on(dy2)
    gw_out = _weight_grad("grad_w_out", ycat, dy2, True)
    tail, dmkv, g_conv = _conv_xattn_bwd(dycx, bcu, qx, mkv, conv_w + ex.zero, tm)
    gw_kv, dg_mem = _memkv_bwd(mem, g_mem, w_kv, dmkv)
    dqkv = _attn_bwd(q, k, v, dya, lse, delta)
    dproj, grad_x, dg_pre_mix = _in_proj_bwd(dqkv, tail, cos, sin, w_in, x, g_pre_mix, dx1, tm)
    gw_in = _weight_grad("grad_w_in", h, dproj, False)
    mat_sums = ex.reduce_scatter([gw_in, gw_kv, gw_out])

    gain_grads = [dg_pre_mix, dg_mem, dg_a, dg_c, dg_x, dg_post_mix, dg_pre_mlp, dg_post_mlp]
    return loss, grad_x, mat_sums, g_conv, gain_grads


def _pack_small(gains, conv):
    rows = [jnp.pad(g, ((0, 0), (0, D_MODEL - g.shape[1]))) for g in gains]
    rows.append(jnp.pad(conv, ((0, SMALL_ROWS - 8 - conv.shape[0]), (0, D_MODEL - conv.shape[1]))))
    return jnp.concatenate(rows, axis=0)


def _unpack_small(block, gain_widths, conv_width):
    gains = [block[i:i + 1, :w] for i, w in enumerate(gain_widths)]
    return gains, block[8:11, :conv_width]


def kernel(x, mem, positions, g_pre_mix, g_mem, w_in, w_mem_kv, conv_w, g_attn_out, g_conv_out, g_xattn_out, w_out, g_post_mix, g_pre_mlp, w_up, w_down, g_post_mlp, loss_target, m_g_pre_mix, m_g_mem, m_w_in, m_w_mem_kv, m_conv_w, m_g_attn_out, m_g_conv_out, m_g_xattn_out, m_w_out, m_g_post_mix, m_g_pre_mlp, m_w_up, m_w_down, m_g_post_mlp, v_g_pre_mix, v_g_mem, v_w_in, v_w_mem_kv, v_conv_w, v_g_attn_out, v_g_conv_out, v_g_xattn_out, v_w_out, v_g_post_mix, v_g_pre_mlp, v_w_up, v_w_down, v_g_post_mlp):
    cx, cy, cc = _place()
    chip = 2 * cx + cy
    gains = [g_pre_mix, g_mem, g_attn_out, g_conv_out, g_xattn_out, g_post_mix, g_pre_mlp, g_post_mlp]
    gains_m = [m_g_pre_mix, m_g_mem, m_g_attn_out, m_g_conv_out, m_g_xattn_out, m_g_post_mix, m_g_pre_mlp, m_g_post_mlp]
    gains_v = [v_g_pre_mix, v_g_mem, v_g_attn_out, v_g_conv_out, v_g_xattn_out, v_g_post_mix, v_g_pre_mlp, v_g_post_mlp]
    gain_widths = [g.shape[1] for g in gains]
    mats = [w_in[0], w_mem_kv[0], w_out[0], w_up[0], w_down[0]]
    mats_m = [m_w_in[0], m_w_mem_kv[0], m_w_out[0], m_w_up[0], m_w_down[0]]
    mats_v = [v_w_in[0], v_w_mem_kv[0], v_w_out[0], v_w_up[0], v_w_down[0]]

    ex = _StepExchanges(mats, gains, conv_w[0])
    loss, grad_x, mat_sums, g_conv, gain_grads = _local_step(x[0], mem[0], positions[0], loss_target[0], gains, ex)

    small_sum = _sum_blocks("small_sum", _small_allgather("small_grads_allgather", _pack_small(gain_grads, g_conv)))
    gain_sums, conv_sum_full = _unpack_small(small_sum, gain_widths, CONV_W)
    conv_sum = lax.dynamic_slice_in_dim(conv_sum_full, chip * conv_w.shape[2], conv_w.shape[2], axis=1)

    mat_new = [_adamw(f"adamw_{a}", mats[a], mat_sums[a], mats_m[a], mats_v[a]) for a in range(len(mats))]
    pack = lambda gs, cv: _pack_small(gs, cv)
    small_new = _adamw("adamw_small", pack(gains, conv_w[0]), pack(gain_sums, conv_sum), pack(gains_m, m_conv_w[0]),
                       pack(gains_v, v_conv_w[0]))
    small_out = [_unpack_small(t, gain_widths, conv_w.shape[2]) for t in small_new]

    total = lax.psum(loss[0, 0], ("x", "y", "c"))
    order = ["g_pre_mix", "g_mem", "w_in", "w_mem_kv", "conv_w", "g_attn_out", "g_conv_out", "g_xattn_out", "w_out",
             "g_post_mix", "g_pre_mlp", "w_up", "w_down", "g_post_mlp"]
    gain_names = ["g_pre_mix", "g_mem", "g_attn_out", "g_conv_out", "g_xattn_out", "g_post_mix", "g_pre_mlp", "g_post_mlp"]
    mat_names = ["w_in", "w_mem_kv", "w_out", "w_up", "w_down"]

    def leaf(kind, name):
        if name in gain_names:
            i = gain_names.index(name)
            return gain_sums[i] if kind == 0 else small_out[kind - 1][0][i]
        if name == "conv_w":
            return (conv_sum if kind == 0 else small_out[kind - 1][1])[None]
        a = mat_names.index(name)
        return (mat_sums[a] if kind == 0 else mat_new[a][kind - 1])[None]

    return (total, grad_x[None], *[leaf(kind, name) for kind in range(4) for name in order])
```

```python
import functools

import jax
import jax.numpy as jnp
from jax import lax
from jax.experimental import pallas as pl
from jax.experimental.pallas import tpu as pltpu

F32, BF16 = jnp.float32, jnp.bfloat16

D_MODEL = 1024
ATTN_W = 512
CONV_W = 256
XATTN_W = 256
PROJ_W = 3 * ATTN_W + 3 * CONV_W + XATTN_W
D_FF = 4096
HEAD = 64
N_BACK = 128
DILATIONS = (1, 4, 16)
ROPE_THETA = 10000.0
EPS = 1e-6
NEG_INF = -1e30
SCALE = HEAD ** -0.5
N_CHIPS = 4
SHARD_IN = PROJ_W // N_CHIPS
SHARD_FF = D_FF // N_CHIPS

ADAM_LR, ADAM_B1, ADAM_B2, ADAM_EPS, ADAM_WD, ADAM_STEP = 0.001, 0.9, 0.999, 1e-08, 0.01, 10

VMEM_LIMIT_V7X = 56 * 1024 * 1024
ROW_TILE = 256
SMALL_ROWS = 16

NT = (((1,), (1,)), ((), ()))
TN = (((0,), (0,)), ((), ()))
MESH = pl.DeviceIdType.MESH


def _params(*sem):
    return pltpu.CompilerParams(dimension_semantics=sem, vmem_limit_bytes=VMEM_LIMIT_V7X)


def _resident(shape):
    return pl.BlockSpec(shape, lambda *_: (0,) * len(shape), pipeline_mode=pl.Buffered(1))


def _rows(tm, width):
    return pl.BlockSpec((tm, width), lambda i: (i, 0))


def _rms_hat(x):
    r = lax.rsqrt(jnp.mean(x * x, axis=-1, keepdims=True) + EPS)
    return x * r, r


def _rms_bwd(xhat, r, g, dy):
    gdy = dy * g
    return r * (gdy - xhat * jnp.mean(xhat * gdy, axis=-1, keepdims=True))


def _rope128(t, cos, sin_signed, inverse):
    lane = lax.broadcasted_iota(jnp.int32, t.shape, 1)
    first_half = (lane % HEAD) < (HEAD // 2)
    rot = jnp.where(first_half, pltpu.roll(t, 128 - HEAD // 2, 1), pltpu.roll(t, HEAD // 2, 1))
    return t * cos - rot * sin_signed if inverse else t * cos + rot * sin_signed


def _in_proj_fwd(x, g, w_in, cos, sin, tm):
    S = x.shape[0]

    def body(x_ref, g_ref, w_ref, cos_ref, sin_ref, h_ref, q_ref, k_ref, v_ref, bcu_ref, qx_ref, proj):
        xhat, _ = _rms_hat(x_ref[...])
        h = (xhat * g_ref[...]).astype(BF16)
        h_ref[...] = h
        for j in range(N_CHIPS):
            proj[:, SHARD_IN * j:SHARD_IN * (j + 1)] = jnp.dot(h, w_ref[j], preferred_element_type=F32)
        c, s = cos_ref[...], sin_ref[...]
        for j in range(ATTN_W // 128):
            lo = 128 * j
            q_ref[:, lo:lo + 128] = _rope128(proj[:, lo:lo + 128], c, s, False) * SCALE
            k_ref[:, lo:lo + 128] = _rope128(proj[:, ATTN_W + lo:ATTN_W + lo + 128], c, s, False)
        v_ref[...] = proj[:, 2 * ATTN_W:3 * ATTN_W]
        bcu_ref[...] = proj[:, 3 * ATTN_W:3 * ATTN_W + 3 * CONV_W]
        qx_ref[...] = proj[:, 3 * ATTN_W + 3 * CONV_W:PROJ_W].astype(BF16)

    return pl.pallas_call(
        body, name="in_proj_fwd", grid=(S // tm,),
        in_specs=[_rows(tm, D_MODEL), _resident((1, D_MODEL)), _resident((N_CHIPS, D_MODEL, SHARD_IN)),
                  _rows(tm, 128), _rows(tm, 128)],
        out_specs=[_rows(tm, D_MODEL), _rows(tm, ATTN_W), _rows(tm, ATTN_W), _rows(tm, ATTN_W),
                   _rows(tm, 3 * CONV_W), _rows(tm, XATTN_W)],
        out_shape=[jax.ShapeDtypeStruct((S, D_MODEL), BF16), jax.ShapeDtypeStruct((S, ATTN_W), F32),
                   jax.ShapeDtypeStruct((S, ATTN_W), F32), jax.ShapeDtypeStruct((S, ATTN_W), F32),
                   jax.ShapeDtypeStruct((S, 3 * CONV_W), F32), jax.ShapeDtypeStruct((S, XATTN_W), BF16)],
        scratch_shapes=[pltpu.VMEM((tm, PROJ_W), F32)],
        compiler_params=_params("parallel"),
    )(x, g, w_in, cos, sin)


def _memkv_fwd(mem, g_mem, w_kv):
    n_mem = mem.shape[0]

    def body(mem_ref, g_ref, w_ref, mn_ref, kv_ref):
        mhat, _ = _rms_hat(mem_ref[...])
        mn = (mhat * g_ref[...]).astype(BF16)
        mn_ref[...] = mn
        kv_ref[...] = jnp.dot(mn, w_ref[...], preferred_element_type=F32).astype(BF16)

    return pl.pallas_call(
        body, name="memkv_fwd",
        out_shape=[jax.ShapeDtypeStruct((n_mem, D_MODEL), BF16), jax.ShapeDtypeStruct((n_mem, 2 * XATTN_W), BF16)],
        compiler_params=pltpu.CompilerParams(vmem_limit_bytes=VMEM_LIMIT_V7X),
    )(mem, g_mem, w_kv)


def _fill_band_bias(bias):
    row = lax.broadcasted_iota(jnp.int32, (N_BACK, 2 * N_BACK), 0)
    col = lax.broadcasted_iota(jnp.int32, (N_BACK, 2 * N_BACK), 1)
    band = (col >= row) & (col <= row + N_BACK)
    bias[1] = jnp.where(band, 0.0, NEG_INF)
    bias[0] = jnp.where(band & (col >= N_BACK), 0.0, NEG_INF)


def _strided(start, size, d):
    return pl.ds(start, size) if d == 1 else pl.ds(start, size, stride=d)


def _block_starts(t, nb, d):
    r, n = lax.shift_right_logical(t, nb.bit_length() - 1), lax.bitwise_and(t, nb - 1)
    own = r + n * (N_BACK * d)
    prev = r + jnp.maximum(n - 1, 0) * (N_BACK * d)
    if d == 1:
        own, prev = pl.multiple_of(own, N_BACK), pl.multiple_of(prev, N_BACK)
    return own, prev, n


def _by_head(a, b):
    lane = lax.broadcasted_iota(jnp.int32, (a.shape[0], 2 * HEAD), 1)
    return jnp.where(lane < HEAD, a, b)


def _head_only(t, hh):
    lane = lax.broadcasted_iota(jnp.int32, t.shape, 1)
    return jnp.where((lane < HEAD) == (hh == 0), t, jnp.zeros_like(t))


def _stack_heads(t):
    return jnp.concatenate([_head_only(t, 0), _head_only(t, 1)], axis=0)


def _head_columns(t):
    return jnp.concatenate([t[:, 0:1], t[:, HEAD:HEAD + 1]], axis=0)


def _unstack(t):
    return _by_head(t[:N_BACK], t[N_BACK:])


def _unstack_columns(t):
    return _by_head(jnp.broadcast_to(t[:N_BACK], (N_BACK, 2 * HEAD)), jnp.broadcast_to(t[N_BACK:], (N_BACK, 2 * HEAD)))


FWD_BLOCKS_PER_STEP = 4
BWD_BLOCKS_PER_STEP = 2


def _attn_fwd(q, k, v):
    S = q.shape[0]
    U = FWD_BLOCKS_PER_STEP

    def body(q_ref, k_ref, v_ref, y_ref, m_ref, l_scr, bias):
        _fill_band_bias(bias)
        for g, d in enumerate(DILATIONS):
            nb = S // d // N_BACK
            first_pattern, last_pattern = g == 0, g == len(DILATIONS) - 1

            def step(i, carry, d=d, nb=nb, first_pattern=first_pattern, last_pattern=last_pattern):
                blocks = [_block_starts(U * i + u, nb, d) for u in range(U)]
                rows = [_strided(own, N_BACK, d) for own, _, _ in blocks]
                prev_rows = [_strided(prev, N_BACK, d) for _, prev, _ in blocks]
                ss = []
                for u, (_, _, n) in enumerate(blocks):
                    kw = jnp.concatenate([k_ref[prev_rows[u], :], k_ref[rows[u], :]], 0).astype(BF16)
                    qs = _stack_heads(q_ref[rows[u], :].astype(BF16))
                    b = bias[jnp.minimum(n, 1)]
                    ss.append(lax.dot_general(qs, kw, NT, preferred_element_type=F32) + jnp.concatenate([b, b], axis=0))
                ms = [jnp.max(s, axis=1, keepdims=True) for s in ss]
                ps = [jnp.exp(s - m) for s, m in zip(ss, ms)]
                ls = [jnp.sum(p, axis=1, keepdims=True) for p in ps]
                os_ = []
                for u in range(U):
                    vw = jnp.concatenate([v_ref[prev_rows[u], :], v_ref[rows[u], :]], 0).astype(BF16)
                    os_.append(jnp.dot(ps[u].astype(BF16), vw, preferred_element_type=F32))
                for u in range(U):
                    o_g, m_g, l_g = _unstack(os_[u]), _unstack_columns(ms[u]), _unstack_columns(ls[u])
                    r = rows[u]
                    if first_pattern:
                        m_new, l_new, acc = m_g, l_g, o_g
                    else:
                        m_old = m_ref[r, :]
                        m_new = jnp.maximum(m_old, m_g)
                        alpha, beta = jnp.exp(m_old - m_new), jnp.exp(m_g - m_new)
                        l_new = l_scr[r, :] * alpha + l_g * beta
                        acc = y_ref[r, :] * alpha + o_g * beta
                    if last_pattern:
                        y_ref[r, :] = acc / l_new
                        m_ref[r, :] = m_new + jnp.log(l_new)
                    else:
                        y_ref[r, :] = acc
                        m_ref[r, :] = m_new
                        l_scr[r, :] = l_new
                return carry

            lax.fori_loop(0, d * nb // U, step, 0)

    col = pl.BlockSpec((S, 2 * HEAD), lambda j: (0, j))
    return pl.pallas_call(
        body, name="attn_fwd", grid=(q.shape[1] // (2 * HEAD),),
        in_specs=[col, col, col], out_specs=[col, col],
        out_shape=[jax.ShapeDtypeStruct(q.shape, F32)] * 2,
        scratch_shapes=[pltpu.VMEM((S, 2 * HEAD), F32), pltpu.VMEM((2, N_BACK, 2 * N_BACK), F32)],
        compiler_params=_params("parallel"),
    )(q, k, v)


def _attn_bwd(q, k, v, dy, lse, delta, after):
    S = q.shape[0]
    U = BWD_BLOCKS_PER_STEP

    def body(q_ref, k_ref, v_ref, dy_ref, lse_ref, delta_ref, after_ref, dq_ref, dk_ref, dv_ref, bias):
        _fill_band_bias(bias)
        dk_ref[...] = jnp.zeros_like(dk_ref)
        dv_ref[...] = jnp.zeros_like(dv_ref)
        for g, d in enumerate(DILATIONS):
            nb = S // d // N_BACK

            def step(i, carry, d=d, nb=nb, g=g):
                blocks = [_block_starts(U * i + u, nb, d) for u in range(U)]
                rows = [_strided(own, N_BACK, d) for own, _, _ in blocks]
                prev_rows = [_strided(prev, N_BACK, d) for _, prev, _ in blocks]
                kws = [jnp.concatenate([k_ref[prev_rows[u], :], k_ref[rows[u], :]], 0).astype(BF16) for u in range(U)]
                vws = [jnp.concatenate([v_ref[prev_rows[u], :], v_ref[rows[u], :]], 0).astype(BF16) for u in range(U)]
                qss = [_stack_heads(q_ref[rows[u], :].astype(BF16)) for u in range(U)]
                doss = [_stack_heads(dy_ref[rows[u], :].astype(BF16)) for u in range(U)]
                ss, dps = [], []
                for u, (_, _, n) in enumerate(blocks):
                    b = bias[jnp.minimum(n, 1)]
                    ss.append(lax.dot_general(qss[u], kws[u], NT, preferred_element_type=F32) + jnp.concatenate([b, b], axis=0))
                    dps.append(lax.dot_general(doss[u], vws[u], NT, preferred_element_type=F32))
                ps = [jnp.exp(ss[u] - _head_columns(lse_ref[rows[u], :])) for u in range(U)]
                dss = [(ps[u] * (dps[u] - _head_columns(delta_ref[rows[u], :]))).astype(BF16) for u in range(U)]
                pbs = [p.astype(BF16) for p in ps]
                dqs = [jnp.dot(dss[u], kws[u], preferred_element_type=F32) for u in range(U)]
                dkws = [lax.dot_general(dss[u], qss[u], TN, preferred_element_type=F32) for u in range(U)]
                dvws = [lax.dot_general(pbs[u], doss[u], TN, preferred_element_type=F32) for u in range(U)]
                for u in range(U):
                    dq = _unstack(dqs[u])
                    if g == 0:
                        dq_ref[rows[u], :] = dq
                    else:
                        dq_ref[rows[u], :] += dq
                    dk_ref[prev_rows[u], :] += dkws[u][:N_BACK]
                    dv_ref[prev_rows[u], :] += dvws[u][:N_BACK]
                    dk_ref[rows[u], :] += dkws[u][N_BACK:]
                    dv_ref[rows[u], :] += dvws[u][N_BACK:]
                return carry

            lax.fori_loop(0, d * nb // U, step, 0)

    col = pl.BlockSpec((S, 2 * HEAD), lambda j: (0, j))
    return pl.pallas_call(
        body, name="attn_bwd", grid=(q.shape[1] // (2 * HEAD),),
        in_specs=[col] * 6 + [pl.BlockSpec(memory_space=pl.ANY)], out_specs=[col] * 3,
        out_shape=[jax.ShapeDtypeStruct(q.shape, F32)] * 3,
        scratch_shapes=[pltpu.VMEM((2, N_BACK, 2 * N_BACK), F32)],
        compiler_params=_params("parallel"),
    )(q, k, v, dy, lse, delta, after)


def _shift_down(z, before, k):
    row = lax.broadcasted_iota(jnp.int32, z.shape, 0)
    out = pltpu.roll(z, k, 0)
    for i in range(k):
        out = jnp.where(row == i, before[8 - k + i:8 - k + i + 1, :], out)
    return out


def _shift_up(z, after, k):
    rows = z.shape[0]
    row = lax.broadcasted_iota(jnp.int32, z.shape, 0)
    out = pltpu.roll(z, rows - k, 0)
    for i in range(k):
        out = jnp.where(row == rows - k + i, after[i:i + 1, :], out)
    return out


def _conv_fwd(bcu, before, is_first, w):
    b, c, u = bcu[:, 0:CONV_W], bcu[:, CONV_W:2 * CONV_W], bcu[:, 2 * CONV_W:3 * CONV_W]
    z = c * u
    zb = jnp.where(is_first, 0.0, before[:, CONV_W:2 * CONV_W] * before[:, 2 * CONV_W:3 * CONV_W])
    z1, z2 = _shift_down(z, zb, 1), _shift_down(z, zb, 2)
    cv = w[0:1, :] * z2 + w[1:2, :] * z1 + w[2:3, :] * z
    return b, c, u, z, z1, z2, cv


def _halo_before(tm, width):
    return pl.BlockSpec((8, width), lambda i: (jnp.maximum(i * (tm // 8) - 1, 0), 0))


def _halo_after(tm, width, S):
    return pl.BlockSpec((8, width), lambda i: (jnp.minimum((i + 1) * (tm // 8), S // 8 - 1), 0))


def _mix_fwd(ya, bcu, qx, mkv, conv_w, g_a, g_c, g_x, w_out, g_post, x, tm):
    S = x.shape[0]

    def body(ya_ref, bcu_ref, before_ref, qx_ref, mkv_ref, cw_ref, ga_ref, gc_ref, gx_ref,
             wo_ref, gp_ref, x_ref, yx_ref, ycat_ref, y2_ref, x1_ref):
        ya = ya_ref[...]
        b, _, _, _, _, _, cv = _conv_fwd(bcu_ref[...], before_ref[...], pl.program_id(0) == 0, cw_ref[...])
        yc = b * cv

        qxb, mkvb = qx_ref[...], mkv_ref[...]
        for hd in range(XATTN_W // HEAD):
            sl = slice(HEAD * hd, HEAD * (hd + 1))
            s = lax.dot_general(qxb[:, sl], mkvb[:, sl], NT, preferred_element_type=F32) * SCALE
            mx = jnp.max(s, axis=1, keepdims=True)
            p = jnp.exp(s - mx)
            l = jnp.sum(p, axis=1, keepdims=True)
            vm = mkvb[:, XATTN_W + HEAD * hd:XATTN_W + HEAD * (hd + 1)]
            yx_ref[:, sl] = jnp.dot(p.astype(BF16), vm, preferred_element_type=F32) / l
        yx = yx_ref[...]

        ycat_ref[:, 0:ATTN_W] = (_rms_hat(ya)[0] * ga_ref[...]).astype(BF16)
        ycat_ref[:, ATTN_W:ATTN_W + CONV_W] = (_rms_hat(yc)[0] * gc_ref[...]).astype(BF16)
        ycat_ref[:, ATTN_W + CONV_W:D_MODEL] = (_rms_hat(yx)[0] * gx_ref[...]).astype(BF16)
        y2 = jnp.dot(ycat_ref[...], wo_ref[...], preferred_element_type=F32)
        y2_ref[...] = y2
        x1_ref[...] = x_ref[...] + _rms_hat(y2)[0] * gp_ref[...]

    n_mem = mkv.shape[0]
    return pl.pallas_call(
        body, name="mix_fwd", grid=(S // tm,),
        in_specs=[_rows(tm, ATTN_W), _rows(tm, 3 * CONV_W), _halo_before(tm, 3 * CONV_W), _rows(tm, XATTN_W),
                  _resident((n_mem, 2 * XATTN_W)), _resident((3, CONV_W)), _resident((1, ATTN_W)),
                  _resident((1, CONV_W)), _resident((1, XATTN_W)), _resident((D_MODEL, D_MODEL)),
                  _resident((1, D_MODEL)), _rows(tm, D_MODEL)],
        out_specs=[_rows(tm, XATTN_W), _rows(tm, D_MODEL), _rows(tm, D_MODEL), _rows(tm, D_MODEL)],
        out_shape=[jax.ShapeDtypeStruct((S, XATTN_W), F32), jax.ShapeDtypeStruct((S, D_MODEL), BF16),
                   jax.ShapeDtypeStruct((S, D_MODEL), F32), jax.ShapeDtypeStruct((S, D_MODEL), F32)],
        compiler_params=_params("parallel"),
    )(ya, bcu, bcu, qx, mkv, conv_w, g_a, g_c, g_x, w_out, g_post, x)


def _mlp_fwd_bwd(x1, target, g_pre, g_post, w_up, w_down, tm):
    S = x1.shape[0]
    n_ff = D_FF // SHARD_FF

    def body(x1_ref, t_ref, gpre_ref, gpost_ref, wup_ref, wdn_ref,
             h2_ref, f_ref, du_ref, df2_ref, dx1_ref, dgpre_ref, dgpost_ref, loss_ref, u_scr):
        @pl.when(pl.program_id(0) == 0)
        def _():
            dgpre_ref[...] = jnp.zeros_like(dgpre_ref)
            dgpost_ref[...] = jnp.zeros_like(dgpost_ref)
            loss_ref[...] = jnp.zeros_like(loss_ref)

        x1 = x1_ref[...]
        x1hat, r1 = _rms_hat(x1)
        h2 = (x1hat * gpre_ref[...]).astype(BF16)
        h2_ref[...] = h2
        f2 = jnp.zeros((tm, D_MODEL), F32)
        for j in range(n_ff):
            cols = slice(SHARD_FF * j, SHARD_FF * (j + 1))
            u = jnp.maximum(jnp.dot(h2, wup_ref[j], preferred_element_type=F32), 0.0)
            u_scr[:, cols] = u
            f = (u * u).astype(BF16)
            f_ref[:, cols] = f
            f2 = f2 + jnp.dot(f, wdn_ref[cols, :], preferred_element_type=F32)
        f2hat, r2 = _rms_hat(f2)
        err = x1 + f2hat * gpost_ref[...] - t_ref[...]
        loss_ref[...] += 0.5 * jnp.sum(jnp.mean(err * err, axis=-1, keepdims=True), axis=0, keepdims=True)
        dx2 = err * (1.0 / D_MODEL)
        dgpost_ref[...] += jnp.sum(dx2 * f2hat, axis=0, keepdims=True)
        df2 = _rms_bwd(f2hat, r2, gpost_ref[...], dx2).astype(BF16)
        df2_ref[...] = df2
        dh2 = jnp.zeros((tm, D_MODEL), F32)
        for j in range(n_ff):
            cols = slice(SHARD_FF * j, SHARD_FF * (j + 1))
            df = lax.dot_general(df2, wdn_ref[cols, :], NT, preferred_element_type=F32)
            du = (2.0 * u_scr[:, cols] * df).astype(BF16)
            du_ref[:, cols] = du
            dh2 = dh2 + lax.dot_general(du, wup_ref[j], NT, preferred_element_type=F32)
        dgpre_ref[...] += jnp.sum(dh2 * x1hat, axis=0, keepdims=True)
        dx1_ref[...] = dx2 + _rms_bwd(x1hat, r1, gpre_ref[...], dh2)

    acc = pl.BlockSpec((1, D_MODEL), lambda i: (0, 0))
    return pl.pallas_call(
        body, name="mlp_fwd_bwd", grid=(S // tm,),
        in_specs=[_rows(tm, D_MODEL), _rows(tm, D_MODEL), _resident((1, D_MODEL)), _resident((1, D_MODEL)),
                  _resident((n_ff, D_MODEL, SHARD_FF)), _resident((D_FF, D_MODEL))],
        out_specs=[_rows(tm, D_MODEL), _rows(tm, D_FF), _rows(tm, D_FF), _rows(tm, D_MODEL), _rows(tm, D_MODEL),
                   acc, acc, pl.BlockSpec((1, 1), lambda i: (0, 0))],
        out_shape=[jax.ShapeDtypeStruct((S, D_MODEL), BF16), jax.ShapeDtypeStruct((S, D_FF), BF16),
                   jax.ShapeDtypeStruct((S, D_FF), BF16), jax.ShapeDtypeStruct((S, D_MODEL), BF16),
                   jax.ShapeDtypeStruct((S, D_MODEL), F32), jax.ShapeDtypeStruct((1, D_MODEL), F32),
                   jax.ShapeDtypeStruct((1, D_MODEL), F32), jax.ShapeDtypeStruct((1, 1), F32)],
        scratch_shapes=[pltpu.VMEM((tm, D_FF), F32)],
        compiler_params=_params("arbitrary"),
    )(x1, target, g_pre, g_post, w_up, w_down)


def _weight_grad(name, a, b, rows_sharded):
    S, K = a.shape
    N = b.shape[1]
    if rows_sharded:
        tk, tn = K // N_CHIPS, N
        a_spec = pl.BlockSpec((S, tk), lambda j: (0, j))
        b_spec = pl.BlockSpec((S, tn), lambda j: (0, 0), pipeline_mode=pl.Buffered(1))
    else:
        tk, tn = K, N // N_CHIPS
        a_spec = pl.BlockSpec((S, tk), lambda j: (0, 0), pipeline_mode=pl.Buffered(1))
        b_spec = pl.BlockSpec((S, tn), lambda j: (0, j))
    half = tk // 2

    def body(a_ref, b_ref, o_ref):
        res = lax.dot_general(a_ref[...], b_ref[...], TN, preferred_element_type=F32)
        o_ref[0, 0] = res[:half]
        o_ref[1, 0] = res[half:]

    return pl.pallas_call(
        body, name=name, grid=(N_CHIPS,), in_specs=[a_spec, b_spec],
        out_specs=pl.BlockSpec((2, 1, half, tn), lambda j: (0, j, 0, 0)),
        out_shape=jax.ShapeDtypeStruct((2, N_CHIPS, half, tn), F32),
        compiler_params=_params("parallel"),
    )(a, b)


def _mix_bwd(dx1, y2, ya, yx, bcu, conv_w, g_a, g_c, g_x, w_out, g_post, tm):
    S = dx1.shape[0]

    def body(dx1_ref, y2_ref, ya_ref, yx_ref, bcu_ref, before_ref, cw_ref, ga_ref, gc_ref, gx_ref, wo_ref, gp_ref,
             dy2_ref, dya_ref, delta_ref, dycx_ref, dgp_ref, dga_ref, dgc_ref, dgx_ref):
        @pl.when(pl.program_id(0) == 0)
        def _():
            for ref in (dgp_ref, dga_ref, dgc_ref, dgx_ref):
                ref[...] = jnp.zeros_like(ref)

        dx1 = dx1_ref[...]
        y2hat, r2 = _rms_hat(y2_ref[...])
        dgp_ref[...] += jnp.sum(dx1 * y2hat, axis=0, keepdims=True)
        dy2 = _rms_bwd(y2hat, r2, gp_ref[...], dx1).astype(BF16)
        dy2_ref[...] = dy2
        dycat = lax.dot_general(dy2, wo_ref[...], NT, preferred_element_type=F32)

        d_na = dycat[:, 0:ATTN_W]
        ya = ya_ref[...]
        yahat, ra = _rms_hat(ya)
        dga_ref[...] += jnp.sum(d_na * yahat, axis=0, keepdims=True)
        dya = _rms_bwd(yahat, ra, ga_ref[...], d_na)
        dya_ref[...] = dya
        prod = dya * ya
        hi = prod.astype(BF16)
        lo = (prod - hi.astype(F32)).astype(BF16)
        head_of = lambda axis: lax.shift_right_logical(lax.broadcasted_iota(jnp.int32, (ATTN_W, ATTN_W), axis),
                                                       HEAD.bit_length() - 1)
        same_head = head_of(0) == head_of(1)
        ones = jnp.where(same_head, 1.0, 0.0).astype(BF16)
        delta_ref[...] = (jnp.dot(hi, ones, preferred_element_type=F32) + jnp.dot(lo, ones, preferred_element_type=F32))

        b, _, _, _, _, _, cv = _conv_fwd(bcu_ref[...], before_ref[...], pl.program_id(0) == 0, cw_ref[...])
        d_nc = dycat[:, ATTN_W:ATTN_W + CONV_W]
        ychat, rc = _rms_hat(b * cv)
        dgc_ref[...] += jnp.sum(d_nc * ychat, axis=0, keepdims=True)
        dycx_ref[:, 0:CONV_W] = _rms_bwd(ychat, rc, gc_ref[...], d_nc)

        d_nx = dycat[:, ATTN_W + CONV_W:D_MODEL]
        yxhat, rx = _rms_hat(yx_ref[...])
        dgx_ref[...] += jnp.sum(d_nx * yxhat, axis=0, keepdims=True)
        dycx_ref[:, CONV_W:CONV_W + XATTN_W] = _rms_bwd(yxhat, rx, gx_ref[...], d_nx)

    acc = lambda w: pl.BlockSpec((1, w), lambda i: (0, 0))
    return pl.pallas_call(
        body, name="mix_bwd", grid=(S // tm,),
        in_specs=[_rows(tm, D_MODEL), _rows(tm, D_MODEL), _rows(tm, ATTN_W), _rows(tm, XATTN_W),
                  _rows(tm, 3 * CONV_W), _halo_before(tm, 3 * CONV_W), _resident((3, CONV_W)),
                  _resident((1, ATTN_W)), _resident((1, CONV_W)), _resident((1, XATTN_W)),
                  _resident((D_MODEL, D_MODEL)), _resident((1, D_MODEL))],
        out_specs=[_rows(tm, D_MODEL), _rows(tm, ATTN_W), _rows(tm, ATTN_W), _rows(tm, CONV_W + XATTN_W),
                   acc(D_MODEL), acc(ATTN_W), acc(CONV_W), acc(XATTN_W)],
        out_shape=[jax.ShapeDtypeStruct((S, D_MODEL), BF16), jax.ShapeDtypeStruct((S, ATTN_W), F32),
                   jax.ShapeDtypeStruct((S, ATTN_W), F32),
                   jax.ShapeDtypeStruct((S, CONV_W + XATTN_W), F32), jax.ShapeDtypeStruct((1, D_MODEL), F32),
                   jax.ShapeDtypeStruct((1, ATTN_W), F32), jax.ShapeDtypeStruct((1, CONV_W), F32),
                   jax.ShapeDtypeStruct((1, XATTN_W), F32)],
        compiler_params=_params("arbitrary"),
    )(dx1, y2, ya, yx, bcu, bcu, conv_w, g_a, g_c, g_x, w_out, g_post)


def _conv_xattn_bwd(dycx, bcu, qx, mkv, conv_w, tm):
    S = dycx.shape[0]
    n_mem = mkv.shape[0]
    n_tiles = S // tm

    def body(d_ref, dafter_ref, bcu_ref, before_ref, after_ref, qx_ref, mkv_ref, cw_ref,
             tail_ref, dmkv_ref, dcw_ref):
        i = pl.program_id(0)

        @pl.when(i == 0)
        def _():
            dmkv_ref[...] = jnp.zeros_like(dmkv_ref)
            dcw_ref[...] = jnp.zeros_like(dcw_ref)

        w = cw_ref[...]
        b, c, u, z, z1, z2, cv = _conv_fwd(bcu_ref[...], before_ref[...], i == 0, w)
        dyc = d_ref[:, 0:CONV_W]
        dcv = dyc * b
        dcv_after = jnp.where(i == n_tiles - 1, 0.0, dafter_ref[:, 0:CONV_W] * after_ref[:, 0:CONV_W])
        dz = w[2:3, :] * dcv + w[1:2, :] * _shift_up(dcv, dcv_after, 1) + w[0:1, :] * _shift_up(dcv, dcv_after, 2)
        dcw_ref[0:1, :] += jnp.sum(dcv * z2, axis=0, keepdims=True)
        dcw_ref[1:2, :] += jnp.sum(dcv * z1, axis=0, keepdims=True)
        dcw_ref[2:3, :] += jnp.sum(dcv * z, axis=0, keepdims=True)
        tail_ref[:, 0:CONV_W] = (dyc * cv).astype(BF16)
        tail_ref[:, CONV_W:2 * CONV_W] = (dz * u).astype(BF16)
        tail_ref[:, 2 * CONV_W:3 * CONV_W] = (dz * c).astype(BF16)

        qxb, mkvb = qx_ref[...], mkv_ref[...]
        for hd in range(XATTN_W // HEAD):
            sl = slice(HEAD * hd, HEAD * (hd + 1))
            vsl = slice(XATTN_W + HEAD * hd, XATTN_W + HEAD * (hd + 1))
            s = lax.dot_general(qxb[:, sl], mkvb[:, sl], NT, preferred_element_type=F32) * SCALE
            e = jnp.exp(s - jnp.max(s, axis=1, keepdims=True))
            p = e / jnp.sum(e, axis=1, keepdims=True)
            dob = d_ref[:, CONV_W + HEAD * hd:CONV_W + HEAD * (hd + 1)].astype(BF16)
            dp = lax.dot_general(dob, mkvb[:, vsl], NT, preferred_element_type=F32)
            ds = (p * (dp - jnp.sum(p * dp, axis=1, keepdims=True)) * SCALE).astype(BF16)
            tail_ref[:, 3 * CONV_W + HEAD * hd:3 * CONV_W + HEAD * (hd + 1)] = jnp.dot(
                ds, mkvb[:, sl], preferred_element_type=F32).astype(BF16)
            dmkv_ref[:, sl] += lax.dot_general(ds, qxb[:, sl], TN, preferred_element_type=F32)
            dmkv_ref[:, vsl] += lax.dot_general(p.astype(BF16), dob, TN, preferred_element_type=F32)

    width = CONV_W + XATTN_W
    return pl.pallas_call(
        body, name="conv_xattn_bwd", grid=(n_tiles,),
        in_specs=[_rows(tm, width), _halo_after(tm, width, S), _rows(tm, 3 * CONV_W), _halo_before(tm, 3 * CONV_W),
                  _halo_after(tm, 3 * CONV_W, S), _rows(tm, XATTN_W), _resident((n_mem, 2 * XATTN_W)),
                  _resident((3, CONV_W))],
        out_specs=[_rows(tm, 3 * CONV_W + XATTN_W), pl.BlockSpec((n_mem, 2 * XATTN_W), lambda i: (0, 0)),
                   pl.BlockSpec((3, CONV_W), lambda i: (0, 0))],
        out_shape=[jax.ShapeDtypeStruct((S, 3 * CONV_W + XATTN_W), BF16),
                   jax.ShapeDtypeStruct((n_mem, 2 * XATTN_W), F32), jax.ShapeDtypeStruct((3, CONV_W), F32)],
        compiler_params=_params("arbitrary"),
    )(dycx, dycx, bcu, bcu, bcu, qx, mkv, conv_w)


def _memkv_bwd(mem, g_mem, w_kv, dmkv):
    n_mem = mem.shape[0]
    half = D_MODEL // N_CHIPS // 2

    def body(mem_ref, g_ref, w_ref, d_ref, dw_ref, dg_ref):
        mhat, _ = _rms_hat(mem_ref[...])
        mn = (mhat * g_ref[...]).astype(BF16)
        d = d_ref[...].astype(BF16)
        for k in range(2 * N_CHIPS):
            dw_ref[k % 2, k // 2] = lax.dot_general(mn[:, half * k:half * (k + 1)], d, TN, preferred_element_type=F32)
        dmn = lax.dot_general(d, w_ref[...], NT, preferred_element_type=F32)
        dg_ref[...] = jnp.sum(dmn * mhat, axis=0, keepdims=True)

    return pl.pallas_call(
        body, name="memkv_bwd",
        out_shape=[jax.ShapeDtypeStruct((2, N_CHIPS, half, 2 * XATTN_W), F32), jax.ShapeDtypeStruct((1, D_MODEL), F32)],
        compiler_params=pltpu.CompilerParams(vmem_limit_bytes=VMEM_LIMIT_V7X),
    )(mem, g_mem, w_kv, dmkv)


def _in_proj_bwd(dqkv, tail, cos, sin, w_in, x, g, dx1, tm):
    S = x.shape[0]

    def body(dq_ref, dk_ref, dv_ref, tail_ref, cos_ref, sin_ref, w_ref, x_ref, g_ref, dx1_ref, dproj_ref, dx_ref, dg_ref):
        @pl.when(pl.program_id(0) == 0)
        def _():
            dg_ref[...] = jnp.zeros_like(dg_ref)

        c, s = cos_ref[...], sin_ref[...]
        for j in range(ATTN_W // 128):
            cols = slice(128 * j, 128 * (j + 1))
            dproj_ref[:, cols] = _rope128(dq_ref[:, cols] * SCALE, c, s, True).astype(BF16)
            dproj_ref[:, ATTN_W + 128 * j:ATTN_W + 128 * (j + 1)] = _rope128(dk_ref[:, cols], c, s, True).astype(BF16)
        dproj_ref[:, 2 * ATTN_W:3 * ATTN_W] = dv_ref[...].astype(BF16)
        dproj_ref[:, 3 * ATTN_W:PROJ_W] = tail_ref[...]
        dh = jnp.zeros((tm, D_MODEL), F32)
        for j in range(N_CHIPS):
            dh = dh + lax.dot_general(dproj_ref[:, SHARD_IN * j:SHARD_IN * (j + 1)], w_ref[j], NT,
                                      preferred_element_type=F32)
        xhat, r = _rms_hat(x_ref[...])
        dg_ref[...] += jnp.sum(dh * xhat, axis=0, keepdims=True)
        dx_ref[...] = dx1_ref[...] + _rms_bwd(xhat, r, g_ref[...], dh)

    return pl.pallas_call(
        body, name="in_proj_bwd", grid=(S // tm,),
        in_specs=[_rows(tm, ATTN_W)] * 3 + [_rows(tm, PROJ_W - 3 * ATTN_W), _rows(tm, 128), _rows(tm, 128),
                  _resident((N_CHIPS, D_MODEL, SHARD_IN)), _rows(tm, D_MODEL), _resident((1, D_MODEL)),
                  _rows(tm, D_MODEL)],
        out_specs=[_rows(tm, PROJ_W), _rows(tm, D_MODEL), pl.BlockSpec((1, D_MODEL), lambda i: (0, 0))],
        out_shape=[jax.ShapeDtypeStruct((S, PROJ_W), BF16), jax.ShapeDtypeStruct((S, D_MODEL), F32),
                   jax.ShapeDtypeStruct((1, D_MODEL), F32)],
        compiler_params=_params("arbitrary"),
    )(*dqkv, tail, cos, sin, w_in, x, g, dx1)


def _row_tile(rows):
    return ROW_TILE if rows % ROW_TILE == 0 else rows


def _chip_sum_bf16(name, grad, from_sibling, place):
    _, n, rows, cols = grad.shape
    tr = _row_tile(rows)

    def body(place_ref, g_ref, b_ref, o_ref):
        o_ref[...] = (g_ref[0] + b_ref[...]).astype(BF16)

    spec = pl.BlockSpec((1, tr, cols), lambda s, i, p: (s, i, 0))
    return pl.pallas_call(
        body, name=name, out_shape=jax.ShapeDtypeStruct((n, rows, cols), BF16),
        grid_spec=pltpu.PrefetchScalarGridSpec(
            num_scalar_prefetch=1, grid=(n, rows // tr),
            in_specs=[pl.BlockSpec((1, 1, tr, cols), lambda s, i, p: (p[0], s, i, 0)), spec], out_specs=spec),
        compiler_params=_params("parallel", "parallel"),
    )(place, grad, from_sibling)


def _final_sum(name, grad, from_sibling, others, place):
    _, _, rows, cols = grad.shape
    tr = _row_tile(rows)

    def body(place_ref, own_ref, sib_ref, o0, o1, o2, out_ref):
        acc = own_ref[0, 0] + sib_ref[0]
        for o in (o0, o1, o2):
            acc = acc + o[0].astype(F32)
        out_ref[...] = acc

    other = lambda k: pl.BlockSpec((1, tr, cols), lambda i, p: (k, i, 0))
    return pl.pallas_call(
        body, name=name, out_shape=jax.ShapeDtypeStruct((rows, cols), F32),
        grid_spec=pltpu.PrefetchScalarGridSpec(
            num_scalar_prefetch=1, grid=(rows // tr,),
            in_specs=[pl.BlockSpec((1, 1, tr, cols), lambda i, p: (p[0], p[1], i, 0)),
                      pl.BlockSpec((1, tr, cols), lambda i, p: (p[1], i, 0)), other(0), other(1), other(2)],
            out_specs=pl.BlockSpec((tr, cols), lambda i, p: (i, 0))),
        compiler_params=_params("parallel"),
    )(place, grad, from_sibling, others, others, others)


def _adamw(name, w, g, m, v, after):
    rows, cols = w.shape
    tr = _row_tile(rows)

    def body(w_ref, g_ref, m_ref, v_ref, after_ref, d_ref, nm_ref, nv_ref):
        g = g_ref[...]
        m = ADAM_B1 * m_ref[...] + (1.0 - ADAM_B1) * g
        v = ADAM_B2 * v_ref[...] + (1.0 - ADAM_B2) * (g * g)
        m_hat = m / (1.0 - ADAM_B1 ** ADAM_STEP)
        v_hat = v / (1.0 - ADAM_B2 ** ADAM_STEP)
        d_ref[...] = -ADAM_LR * (m_hat / (jnp.sqrt(v_hat) + ADAM_EPS) + ADAM_WD * w_ref[...])
        nm_ref[...] = m
        nv_ref[...] = v

    spec = pl.BlockSpec((tr, cols), lambda i: (i, 0))
    return pl.pallas_call(
        body, name=name, grid=(rows // tr,), in_specs=[spec] * 4 + [pl.BlockSpec(memory_space=pl.ANY)],
        out_specs=[spec] * 3, out_shape=[jax.ShapeDtypeStruct(w.shape, F32)] * 3, compiler_params=_params("parallel"),
    )(w, g, m, v, after)


def _sum_blocks(name, blocks):
    n, rows, cols = blocks.shape

    def body(b_ref, o_ref):
        acc = b_ref[0]
        for k in range(1, n):
            acc = acc + b_ref[k]
        o_ref[...] = acc

    return pl.pallas_call(body, name=name, out_shape=jax.ShapeDtypeStruct((rows, cols), F32))(blocks)


def _place():
    return lax.axis_index("x"), lax.axis_index("y"), lax.axis_index("c")


def _other_chips(x, y):
    return [(1 - x, y), (x, 1 - y), (1 - x, 1 - y)]


def _small_allgather(name, block):
    rows, cols = block.shape
    relations = [(dx, dy, dc) for dx in (0, 1) for dy in (0, 1) for dc in (0, 1) if (dx, dy, dc) != (0, 0, 0)]

    def body(x_ref, out_ref, send_sems, recv_sems, local_sem):
        x, y, c = _place()

        def peer(rel):
            return (1 - x if rel[0] else x, 1 - y if rel[1] else y, 1 - c if rel[2] else c)

        def index(p):
            return 4 * p[0] + 2 * p[1] + p[2]

        def copy(k, origin, to):
            return pltpu.make_async_remote_copy(
                src_ref=x_ref, dst_ref=out_ref.at[index(origin)], send_sem=send_sems.at[k], recv_sem=recv_sems.at[k],
                device_id=to, device_id_type=MESH)

        mine = pltpu.make_async_copy(x_ref, out_ref.at[index((x, y, c))], local_sem)
        mine.start()
        sends = [copy(k, (x, y, c), peer(rel)) for k, rel in enumerate(relations)]
        for cp in sends:
            cp.start()
        for k, rel in enumerate(relations):
            copy(k, peer(rel), (x, y, c)).wait_recv()
        for cp in sends:
            cp.wait_send()
        mine.wait()

    return pl.pallas_call(
        body, name=name, out_shape=jax.ShapeDtypeStruct((8, rows, cols), F32),
        in_specs=[pl.BlockSpec(memory_space=pltpu.VMEM)], out_specs=pl.BlockSpec(memory_space=pltpu.VMEM),
        scratch_shapes=[pltpu.SemaphoreType.DMA((7,)), pltpu.SemaphoreType.DMA((7,)), pltpu.SemaphoreType.DMA],
    )(block)


def _weights_allgather(name, shards, landed=None):
    n = len(shards)
    first_hop = landed is None

    def body(*refs):
        ins, outs, stage = refs[:n], refs[-3 - 2 * n:-3 - n], refs[-3 - n:-3]
        send_sems, recv_sems, local_sems = refs[-3:]
        x, y, c = _place()
        me, sibling = (x, y, c), (x, y, 1 - c)
        chips = _other_chips(x, y)
        chip_index = lambda chip: 2 * chip[0] + chip[1]

        def copy(a, k, chip, half, to, src=None):
            place = outs[a].at[chip_index(chip), half]
            return pltpu.make_async_remote_copy(
                src_ref=place if src is None else src, dst_ref=place, send_sem=send_sems.at[6 * a + k],
                recv_sem=recv_sems.at[6 * a + k], device_id=to, device_id_type=MESH)

        load = [pltpu.make_async_copy(ins[a], stage[a], local_sems.at[a]) for a in range(n)]
        local = [pltpu.make_async_copy(stage[a], outs[a].at[chip_index((x, y))], local_sems.at[a]) for a in range(n)]
        for cp in load:
            cp.start()
        first = []
        if first_hop:
            first = [copy(a, k, (x, y), c, (*chip, c), src=ins[a].at[c]) for a in range(n) for k, chip in enumerate(chips)]
        for cp in first:
            cp.start()
        for a in range(n):
            load[a].wait()
            local[a].start()
        passed = []
        for a in range(n):
            for k, chip in enumerate(chips):
                if first_hop:
                    copy(a, k, chip, c, me).wait_recv()
                passed.append(copy(a, 3 + k, chip, c, sibling))
                passed[-1].start()
        for a in range(n):
            for k, chip in enumerate(chips):
                copy(a, 3 + k, chip, 1 - c, me).wait_recv()
        for cp in first + passed:
            cp.wait_send()
        for cp in local:
            cp.wait()

    any_spec = pl.BlockSpec(memory_space=pl.ANY)
    operands = list(shards) + ([] if first_hop else list(landed))
    return pl.pallas_call(
        body, name=name,
        out_shape=[jax.ShapeDtypeStruct((N_CHIPS,) + s.shape, s.dtype) for s in shards],
        in_specs=[any_spec] * len(operands), out_specs=[any_spec] * n,
        input_output_aliases={} if first_hop else {n + a: a for a in range(n)},
        scratch_shapes=[pltpu.VMEM(s.shape, s.dtype) for s in shards]
        + [pltpu.SemaphoreType.DMA((6 * n,)), pltpu.SemaphoreType.DMA((6 * n,)), pltpu.SemaphoreType.DMA((n,))],
        compiler_params=pltpu.CompilerParams(vmem_limit_bytes=VMEM_LIMIT_V7X),
    )(*operands)


def _plan_first_hop(x, y, c, shards, lands):
    return [(shards[a].at[c], lands[a].at[2 * x + y, c], lands[a].at[2 * chip[0] + chip[1], c], (*chip, c))
            for a in range(len(shards)) for chip in _other_chips(x, y)]


def _plan_other_half_to_sibling(x, y, c, grads, lands):
    return [(grads[a].at[1 - c], lands[a], lands[a], (x, y, 1 - c)) for a in range(len(grads))]


def _plan_to_other_chips(x, y, c, partials, lands):
    return [(partials[a].at[2 * chip[0] + chip[1]], lands[a].at[k], lands[a].at[k], (*chip, c))
            for a in range(len(partials)) for k, chip in enumerate(_other_chips(x, y))]


def _planned_copies(plan, srcs, lands, send_sems, recv_sems):
    x, y, c = _place()

    def pair(k, src, there, here, to):
        make = lambda dst: pltpu.make_async_remote_copy(
            src_ref=src, dst_ref=dst, send_sem=send_sems.at[k], recv_sem=recv_sems.at[k], device_id=to, device_id_type=MESH)
        return make(there), make(here)

    return [pair(k, *entry) for k, entry in enumerate(plan(x, y, c, srcs, lands))]


def _exchange(name, plan, n_copies, srcs, land_shapes):
    ns, nl = len(srcs), len(land_shapes)

    def body(*refs):
        pairs = _planned_copies(plan, refs[:ns], refs[ns:ns + nl], refs[ns + nl], refs[ns + nl + 1])
        for send, _ in pairs:
            send.start()
        for send, recv in pairs:
            send.wait_send()
            recv.wait_recv()

    any_spec = pl.BlockSpec(memory_space=pl.ANY)
    return pl.pallas_call(
        body, name=name, out_shape=list(land_shapes), in_specs=[any_spec] * ns, out_specs=[any_spec] * nl,
        scratch_shapes=[pltpu.SemaphoreType.DMA((n_copies,)), pltpu.SemaphoreType.DMA((n_copies,))],
    )(*srcs)


_HBM_SPEC = pl.BlockSpec(memory_space=pltpu.HBM)
_SEM_SPEC = pl.BlockSpec(memory_space=pltpu.SEMAPHORE)


def _hbm(a):
    return pltpu.with_memory_space_constraint(a, pltpu.HBM)


def _exchange_start(name, plan, n_copies, srcs, land_shapes, after):
    ns, nl = len(srcs), len(land_shapes)
    n_in = ns + nl + 1

    def body(*refs):
        for send, _ in _planned_copies(plan, refs[:ns], refs[ns:ns + nl], refs[n_in], refs[n_in + 1]):
            send.start()
        refs[-1][...] = jnp.zeros_like(refs[-1])

    out = pl.pallas_call(
        body, name=name,
        out_shape=(pltpu.SemaphoreType.DMA((n_copies,)), pltpu.SemaphoreType.DMA((n_copies,)),
                   *[pltpu.HBM(s.shape, s.dtype) for s in land_shapes], jax.ShapeDtypeStruct((8, 128), F32)),
        in_specs=[_HBM_SPEC] * (ns + nl) + [pl.BlockSpec(memory_space=pl.ANY)],
        out_specs=(_SEM_SPEC, _SEM_SPEC, *[_HBM_SPEC] * nl, pl.BlockSpec(memory_space=pltpu.VMEM)),
        input_output_aliases={ns + i: 2 + i for i in range(nl)},
        compiler_params=pltpu.CompilerParams(has_side_effects=pltpu.SideEffectType.DATAFLOW_SIDE_EFFECTING),
    )(*[_hbm(s) for s in srcs], *[_hbm(lax.empty(s.shape, s.dtype)) for s in land_shapes], after)
    return out[0], out[1], list(out[2:2 + nl]), out[-1]


def _exchange_wait(name, plan, srcs, started, after):
    send_sems, recv_sems, lands, _ = started
    ns, nl = len(srcs), len(lands)

    def body(*refs):
        for send, recv in _planned_copies(plan, refs[:ns], refs[ns:ns + nl], refs[ns + nl], refs[ns + nl + 1]):
            send.wait_send()
            recv.wait_recv()

    return pl.pallas_call(
        body, name=name, out_shape=[pltpu.HBM(l.shape, l.dtype) for l in lands],
        in_specs=[_HBM_SPEC] * (ns + nl) + [_SEM_SPEC, _SEM_SPEC, pl.BlockSpec(memory_space=pl.ANY)],
        out_specs=[_HBM_SPEC] * nl, input_output_aliases={ns + i: i for i in range(nl)},
        compiler_params=pltpu.CompilerParams(has_side_effects=pltpu.SideEffectType.DATAFLOW_SIDE_EFFECTING),
    )(*[_hbm(s) for s in srcs], *lands, send_sems, recv_sems, after)


def _exchange_halves(name, halves, after):
    n = len(halves)

    def body(*refs):
        ins, outs, stage = refs[:n], refs[n + 1:2 * n + 1], refs[2 * n + 1:3 * n + 1]
        send_sems, recv_sems, local_sems = refs[3 * n + 1:]
        x, y, c = _place()
        load = [pltpu.make_async_copy(ins[a], stage[a], local_sems.at[a]) for a in range(n)]
        local = [pltpu.make_async_copy(stage[a], outs[a].at[c], local_sems.at[a]) for a in range(n)]
        remote = [pltpu.make_async_remote_copy(
            src_ref=stage[a], dst_ref=outs[a].at[c], send_sem=send_sems.at[a], recv_sem=recv_sems.at[a],
            device_id=(x, y, 1 - c), device_id_type=MESH) for a in range(n)]
        for cp in load:
            cp.start()
        for a in range(n):
            load[a].wait()
            remote[a].start()
            local[a].start()
        for a in range(n):
            pltpu.make_async_remote_copy(
                src_ref=ins[a], dst_ref=outs[a].at[1 - c], send_sem=send_sems.at[a], recv_sem=recv_sems.at[a],
                device_id=(x, y, 1 - c), device_id_type=MESH).wait_recv()
        for cp in remote:
            cp.wait_send()
        for cp in local:
            cp.wait()

    any_spec = pl.BlockSpec(memory_space=pl.ANY)
    return pl.pallas_call(
        body, name=name,
        out_shape=[jax.ShapeDtypeStruct((2,) + h.shape, h.dtype) for h in halves],
        in_specs=[any_spec] * (n + 1), out_specs=[any_spec] * n,
        scratch_shapes=[pltpu.VMEM(h.shape, h.dtype) for h in halves]
        + [pltpu.SemaphoreType.DMA((n,)), pltpu.SemaphoreType.DMA((n,)), pltpu.SemaphoreType.DMA((n,))],
        compiler_params=pltpu.CompilerParams(vmem_limit_bytes=VMEM_LIMIT_V7X),
    )(*halves, after)


def _like(arrays, lead, dtype=None):
    return [jax.ShapeDtypeStruct(tuple(lead) + a.shape[-2:], dtype or a.dtype) for a in arrays]


class _StepExchanges:
    def __init__(self, mats, conv_w):
        x, y, c = _place()
        self.place = jnp.stack([c, 2 * x + y]).astype(jnp.int32)
        shards = [w.astype(BF16).reshape(2, w.shape[0] // 2, w.shape[1]) for w in mats]
        (w_in,) = _weights_allgather("w_in_allgather", shards[:1])
        self.w_in = w_in.reshape(N_CHIPS, 2 * w_in.shape[2], w_in.shape[3])
        taps = jnp.pad(conv_w, ((0, 8 - conv_w.shape[0]), (0, 128 - conv_w.shape[1])))
        self._rest_shards = shards[1:] + [jnp.stack([taps, jnp.zeros_like(taps)])]
        self._rest = _exchange_start("rest_allgather_start", _plan_first_hop, 3 * len(self._rest_shards),
                                     self._rest_shards, _like(self._rest_shards, (N_CHIPS, 2)), w_in)
        self.zero = self._rest[3]
        self._taps_shape = conv_w.shape
        self._groups = {}

    def rest_weights(self, after):
        landed = _exchange_wait("rest_allgather_wait", _plan_first_hop, self._rest_shards, self._rest, after)
        *mats, taps = _weights_allgather("rest_allgather_finish", self._rest_shards, landed=landed)
        k, w = self._taps_shape
        taps = taps[:, 0, :k, :w].transpose(1, 0, 2).reshape(k, N_CHIPS * w)
        return [g.reshape(N_CHIPS, 2 * g.shape[2], g.shape[3]) for g in mats], taps

    def send_grads(self, key, grads):
        grads = list(grads)
        started = _exchange_start(f"{key}_grads_to_sibling_start", _plan_other_half_to_sibling, len(grads), grads,
                                  _like(grads, (N_CHIPS,)), grads[-1])
        self._groups[key] = dict(grads=grads, to_sibling=started)
        self.zero = started[3]

    def grads_at_sibling(self, key, after):
        group = self._groups[key]
        grads = group["grads"]
        group["from_sibling"] = _exchange_wait(f"{key}_grads_to_sibling_wait", _plan_other_half_to_sibling, grads,
                                               group["to_sibling"], after)
        group["partials"] = [_chip_sum_bf16(f"{key}_chip_sum_{a}", grads[a], group["from_sibling"][a], self.place)
                             for a in range(len(grads))]
        group["to_chips"] = _exchange_start(f"{key}_grads_to_chips_start", _plan_to_other_chips, 3 * len(grads),
                                            group["partials"], _like(group["partials"], (3,)), group["partials"][-1])
        self.zero = group["to_chips"][3]

    def grads_summed(self, key, after):
        group = self._groups[key]
        from_chips = _exchange_wait(f"{key}_grads_to_chips_wait", _plan_to_other_chips, group["partials"],
                                    group["to_chips"], after)
        return [_final_sum(f"{key}_final_sum_{a}", group["grads"][a], group["from_sibling"][a], from_chips[a], self.place)
                for a in range(len(from_chips))]


def _rope_tables(positions):
    half = HEAD // 2
    inv_freq = jnp.float32(ROPE_THETA) ** (-(jnp.arange(half, dtype=F32) * 2.0 / HEAD))
    ang = positions.astype(F32)[:, None] * inv_freq
    cos, sin = jnp.cos(ang), jnp.sin(ang)
    return jnp.tile(cos, (1, 4)), jnp.tile(jnp.concatenate([-sin, sin], axis=1), (1, 2))


def _local_step(x, mem, positions, target, gains, ex):
    g_pre_mix, g_mem, g_a, g_c, g_x, g_post_mix, g_pre_mlp, g_post_mlp = gains
    tm = ROW_TILE
    cos, sin = _rope_tables(positions)
    w_in = ex.w_in

    h, q, k, v, bcu, qx = _in_proj_fwd(x, g_pre_mix + ex.zero[:1, :1], w_in, cos, sin, tm)
    ya, lse = _attn_fwd(q, k, v)
    (w_kv, w_out, w_up, w_down), conv_w = ex.rest_weights(lse)
    w_kv, w_out, w_down = (w.reshape(N_CHIPS * w.shape[1], w.shape[2]) for w in (w_kv, w_out, w_down))
    memn, mkv = _memkv_fwd(mem, g_mem, w_kv)
    yx, ycat, y2, x1 = _mix_fwd(ya, bcu, qx, mkv, conv_w, g_a, g_c, g_x, w_out, g_post_mix, x, tm)
    h2, f, du, df2, dx1, dg_pre_mlp, dg_post_mlp, loss = _mlp_fwd_bwd(x1, target, g_pre_mlp, g_post_mlp, w_up, w_down, tm)
    gw_down = _weight_grad("grad_w_down", f, df2, True)
    gw_up = _weight_grad("grad_w_up", h2, du, False)
    ex.send_grads("early", [gw_up, gw_down])

    dy2, dya, delta, dycx, dg_post_mix, dg_a, dg_c, dg_x = _mix_bwd(dx1, y2, ya, yx, bcu, conv_w, g_a, g_c, g_x,
                                                                  w_out, g_post_mix + ex.zero[:1, :1], tm)
    ex.grads_at_sibling("early", dy2)
    gw_out = _weight_grad("grad_w_out", ycat, dy2, True)
    tail, dmkv, g_conv = _conv_xattn_bwd(dycx, bcu, qx, mkv, conv_w + ex.zero[:1, :1], tm)
    gw_kv, dg_mem = _memkv_bwd(mem, g_mem, w_kv, dmkv)
    ex.send_grads("mid", [gw_out, gw_kv])
    dqkv = _attn_bwd(q, k, v, dya, lse, delta, ex.zero)
    ex.grads_at_sibling("mid", dqkv[0])
    dproj, grad_x, dg_pre_mix = _in_proj_bwd(dqkv, tail, cos, sin, w_in, x, g_pre_mix + ex.zero[:1, :1], dx1, tm)
    gw_in = _weight_grad("grad_w_in", h, dproj, False)
    ex.send_grads("late", [gw_in])

    gain_grads = [dg_pre_mix, dg_mem, dg_a, dg_c, dg_x, dg_post_mix, dg_pre_mlp, dg_post_mlp]
    return loss, grad_x, g_conv, gain_grads


def _pack_small(gains, conv):
    rows = [jnp.pad(g, ((0, 0), (0, D_MODEL - g.shape[1]))) for g in gains]
    rows.append(jnp.pad(conv, ((0, SMALL_ROWS - 8 - conv.shape[0]), (0, D_MODEL - conv.shape[1]))))
    return jnp.concatenate(rows, axis=0)


def _unpack_small(block, gain_widths, conv_width):
    gains = [block[i:i + 1, :w] for i, w in enumerate(gain_widths)]
    return gains, block[8:11, :conv_width]


def kernel(x, mem, positions, g_pre_mix, g_mem, w_in, w_mem_kv, conv_w, g_attn_out, g_conv_out, g_xattn_out, w_out, g_post_mix, g_pre_mlp, w_up, w_down, g_post_mlp, loss_target, m_g_pre_mix, m_g_mem, m_w_in, m_w_mem_kv, m_conv_w, m_g_attn_out, m_g_conv_out, m_g_xattn_out, m_w_out, m_g_post_mix, m_g_pre_mlp, m_w_up, m_w_down, m_g_post_mlp, v_g_pre_mix, v_g_mem, v_w_in, v_w_mem_kv, v_conv_w, v_g_attn_out, v_g_conv_out, v_g_xattn_out, v_w_out, v_g_post_mix, v_g_pre_mlp, v_w_up, v_w_down, v_g_post_mlp):
    cx, cy, cc = _place()
    chip = 2 * cx + cy
    gains = [g_pre_mix, g_mem, g_attn_out, g_conv_out, g_xattn_out, g_post_mix, g_pre_mlp, g_post_mlp]
    gains_m = [m_g_pre_mix, m_g_mem, m_g_attn_out, m_g_conv_out, m_g_xattn_out, m_g_post_mix, m_g_pre_mlp, m_g_post_mlp]
    gains_v = [v_g_pre_mix, v_g_mem, v_g_attn_out, v_g_conv_out, v_g_xattn_out, v_g_post_mix, v_g_pre_mlp, v_g_post_mlp]
    gain_widths = [g.shape[1] for g in gains]
    mats = [w_in[0], w_mem_kv[0], w_out[0], w_up[0], w_down[0]]
    mats_m = [m_w_in[0], m_w_mem_kv[0], m_w_out[0], m_w_up[0], m_w_down[0]]
    mats_v = [v_w_in[0], v_w_mem_kv[0], v_w_out[0], v_w_up[0], v_w_down[0]]

    ex = _StepExchanges(mats, conv_w[0])
    loss, grad_x, g_conv, gain_grads = _local_step(x[0], mem[0], positions[0], loss_target[0], gains, ex)

    both = lambda halves: [t.reshape(2 * t.shape[1], t.shape[2]) for t in halves]
    done = ex.grads_summed("early", ex.zero) + ex.grads_summed("mid", ex.zero)
    ex.grads_at_sibling("late", done[-1])
    up_sum, down_sum, out_sum, kv_sum = both(_exchange_halves("sums_to_sibling", done, ex.zero))
    adamw = lambda a, g, after: _adamw(f"adamw_{a}", mats[a], g, mats_m[a], mats_v[a], after)
    new_up, new_down, new_out, new_kv = adamw(3, up_sum, ex.zero), adamw(4, down_sum, ex.zero), adamw(2, out_sum, ex.zero), adamw(1, kv_sum, ex.zero)

    small_sum = _sum_blocks("small_sum", _small_allgather("small_grads_allgather", _pack_small(gain_grads, g_conv)))
    gain_sums, conv_sum_full = _unpack_small(small_sum, gain_widths, CONV_W)
    conv_sum = lax.dynamic_slice_in_dim(conv_sum_full, chip * conv_w.shape[2], conv_w.shape[2], axis=1)
    pack = lambda gs, cv: _pack_small(gs, cv)
    small_new = _adamw("adamw_small", pack(gains, conv_w[0]), pack(gain_sums, conv_sum), pack(gains_m, m_conv_w[0]),
                       pack(gains_v, v_conv_w[0]), ex.zero)
    small_out = [_unpack_small(t, gain_widths, conv_w.shape[2]) for t in small_new]

    (in_half,) = ex.grads_summed("late", small_new[0])
    (in_sum,) = both(_exchange_halves("late_sum_to_sibling", [in_half], in_half))
    new_in = adamw(0, in_sum, in_sum)
    mat_sums = [in_sum, kv_sum, out_sum, up_sum, down_sum]
    mat_new = [new_in, new_kv, new_out, new_up, new_down]

    total = lax.psum(loss[0, 0], ("x", "y", "c"))
    order = ["g_pre_mix", "g_mem", "w_in", "w_mem_kv", "conv_w", "g_attn_out", "g_conv_out", "g_xattn_out", "w_out",
             "g_post_mix", "g_pre_mlp", "w_up", "w_down", "g_post_mlp"]
    gain_names = ["g_pre_mix", "g_mem", "g_attn_out", "g_conv_out", "g_xattn_out", "g_post_mix", "g_pre_mlp", "g_post_mlp"]
    mat_names = ["w_in", "w_mem_kv", "w_out", "w_up", "w_down"]

    def leaf(kind, name):
        if name in gain_names:
            i = gain_names.index(name)
            return gain_sums[i] if kind == 0 else small_out[kind - 1][0][i]
        if name == "conv_w":
            return (conv_sum if kind == 0 else small_out[kind - 1][1])[None]
        a = mat_names.index(name)
        return (mat_sums[a] if kind == 0 else mat_new[a][kind - 1])[None]

    return (total, grad_x[None], *[leaf(kind, name) for kind in range(4) for name in order])
```

```python
import jax
import jax.numpy as jnp
from jax import lax
from jax.experimental import pallas as pl
from jax.experimental.pallas import tpu as pltpu

F32, BF16 = jnp.float32, jnp.bfloat16

D_MODEL = 1024
ATTN_W = 512
CONV_W = 256
XATTN_W = 256
PROJ_W = 3 * ATTN_W + 3 * CONV_W + XATTN_W
D_FF = 4096
HEAD = 64
N_BACK = 128
DILATIONS = (1, 4, 16)
ROPE_THETA = 10000.0
EPS = 1e-6
NEG_INF = -1e30
SCALE = HEAD ** -0.5
N_CHIPS = 4
SHARD_IN = PROJ_W // N_CHIPS
SHARD_FF = D_FF // N_CHIPS

ADAM_LR, ADAM_B1, ADAM_B2, ADAM_EPS, ADAM_WD, ADAM_STEP = 0.001, 0.9, 0.999, 1e-08, 0.01, 10

VMEM_LIMIT_V7X = 56 * 1024 * 1024
ROW_TILE = 256
SMALL_ROWS = 16

NT = (((1,), (1,)), ((), ()))
TN = (((0,), (0,)), ((), ()))
MESH = pl.DeviceIdType.MESH


def _params(*sem):
    return pltpu.CompilerParams(dimension_semantics=sem, vmem_limit_bytes=VMEM_LIMIT_V7X)


def _resident(shape):
    return pl.BlockSpec(shape, lambda *_: (0,) * len(shape), pipeline_mode=pl.Buffered(1))


def _rows(tm, width):
    return pl.BlockSpec((tm, width), lambda i: (i, 0))


def _rms_hat(x):
    r = lax.rsqrt(jnp.mean(x * x, axis=-1, keepdims=True) + EPS)
    return x * r, r


def _rms_bwd(xhat, r, g, dy):
    gdy = dy * g
    return r * (gdy - xhat * jnp.mean(xhat * gdy, axis=-1, keepdims=True))


def _rope128(t, cos, sin_signed, inverse):
    lane = lax.broadcasted_iota(jnp.int32, t.shape, 1)
    first_half = (lane % HEAD) < (HEAD // 2)
    rot = jnp.where(first_half, pltpu.roll(t, 128 - HEAD // 2, 1), pltpu.roll(t, HEAD // 2, 1))
    return t * cos - rot * sin_signed if inverse else t * cos + rot * sin_signed


def _in_proj_fwd(x, g, w_in, cos, sin, tm):
    S = x.shape[0]

    def body(x_ref, g_ref, w_ref, cos_ref, sin_ref, h_ref, q_ref, k_ref, v_ref, bcu_ref, qx_ref, proj):
        xhat, _ = _rms_hat(x_ref[...])
        h = (xhat * g_ref[...]).astype(BF16)
        h_ref[...] = h
        for j in range(N_CHIPS):
            proj[:, SHARD_IN * j:SHARD_IN * (j + 1)] = jnp.dot(h, w_ref[j], preferred_element_type=F32)
        c, s = cos_ref[...], sin_ref[...]
        for j in range(ATTN_W // 128):
            lo = 128 * j
            q_ref[:, lo:lo + 128] = _rope128(proj[:, lo:lo + 128], c, s, False) * SCALE
            k_ref[:, lo:lo + 128] = _rope128(proj[:, ATTN_W + lo:ATTN_W + lo + 128], c, s, False)
        v_ref[...] = proj[:, 2 * ATTN_W:3 * ATTN_W]
        bcu_ref[...] = proj[:, 3 * ATTN_W:3 * ATTN_W + 3 * CONV_W]
        qx_ref[...] = proj[:, 3 * ATTN_W + 3 * CONV_W:PROJ_W].astype(BF16)

    return pl.pallas_call(
        body, name="in_proj_fwd", grid=(S // tm,),
        in_specs=[_rows(tm, D_MODEL), _resident((1, D_MODEL)), _resident((N_CHIPS, D_MODEL, SHARD_IN)),
                  _rows(tm, 128), _rows(tm, 128)],
        out_specs=[_rows(tm, D_MODEL), _rows(tm, ATTN_W), _rows(tm, ATTN_W), _rows(tm, ATTN_W),
                   _rows(tm, 3 * CONV_W), _rows(tm, XATTN_W)],
        out_shape=[jax.ShapeDtypeStruct((S, D_MODEL), BF16), jax.ShapeDtypeStruct((S, ATTN_W), F32),
                   jax.ShapeDtypeStruct((S, ATTN_W), F32), jax.ShapeDtypeStruct((S, ATTN_W), F32),
                   jax.ShapeDtypeStruct((S, 3 * CONV_W), F32), jax.ShapeDtypeStruct((S, XATTN_W), BF16)],
        scratch_shapes=[pltpu.VMEM((tm, PROJ_W), F32)],
        compiler_params=_params("parallel"),
    )(x, g, w_in, cos, sin)


def _memkv_fwd(mem, g_mem, w_kv):
    n_mem = mem.shape[0]

    def body(mem_ref, g_ref, w_ref, mn_ref, kv_ref):
        mhat, _ = _rms_hat(mem_ref[...])
        mn = (mhat * g_ref[...]).astype(BF16)
        mn_ref[...] = mn
        kv_ref[...] = jnp.dot(mn, w_ref[...], preferred_element_type=F32).astype(BF16)

    return pl.pallas_call(
        body, name="memkv_fwd",
        out_shape=[jax.ShapeDtypeStruct((n_mem, D_MODEL), BF16), jax.ShapeDtypeStruct((n_mem, 2 * XATTN_W), BF16)],
        compiler_params=pltpu.CompilerParams(vmem_limit_bytes=VMEM_LIMIT_V7X),
    )(mem, g_mem, w_kv)


def _fill_band_bias(bias):
    row = lax.broadcasted_iota(jnp.int32, (N_BACK, 2 * N_BACK), 0)
    col = lax.broadcasted_iota(jnp.int32, (N_BACK, 2 * N_BACK), 1)
    band = (col >= row) & (col <= row + N_BACK)
    bias[1] = jnp.where(band, 0.0, NEG_INF)
    bias[0] = jnp.where(band & (col >= N_BACK), 0.0, NEG_INF)


def _strided(start, size, d):
    return pl.ds(start, size) if d == 1 else pl.ds(start, size, stride=d)


def _block_starts(t, nb, d):
    r, n = lax.shift_right_logical(t, nb.bit_length() - 1), lax.bitwise_and(t, nb - 1)
    own = r + n * (N_BACK * d)
    prev = r + jnp.maximum(n - 1, 0) * (N_BACK * d)
    if d == 1:
        own, prev = pl.multiple_of(own, N_BACK), pl.multiple_of(prev, N_BACK)
    return own, prev, n


def _by_head(a, b):
    lane = lax.broadcasted_iota(jnp.int32, (a.shape[0], 2 * HEAD), 1)
    return jnp.where(lane < HEAD, a, b)


def _head_only(t, hh):
    lane = lax.broadcasted_iota(jnp.int32, t.shape, 1)
    return jnp.where((lane < HEAD) == (hh == 0), t, jnp.zeros_like(t))


def _stack_heads(t):
    return jnp.concatenate([_head_only(t, 0), _head_only(t, 1)], axis=0)


def _head_columns(t):
    return jnp.concatenate([t[:, 0:1], t[:, HEAD:HEAD + 1]], axis=0)


def _unstack(t):
    return _by_head(t[:N_BACK], t[N_BACK:])


def _unstack_columns(t):
    return _by_head(jnp.broadcast_to(t[:N_BACK], (N_BACK, 2 * HEAD)), jnp.broadcast_to(t[N_BACK:], (N_BACK, 2 * HEAD)))


FWD_BLOCKS_PER_STEP = 4
BWD_BLOCKS_PER_STEP = 2


def _attn_fwd(q, k, v):
    S = q.shape[0]
    U = FWD_BLOCKS_PER_STEP

    def body(q_ref, k_ref, v_ref, y_ref, m_ref, l_scr, bias):
        _fill_band_bias(bias)
        for g, d in enumerate(DILATIONS):
            nb = S // d // N_BACK
            first_pattern, last_pattern = g == 0, g == len(DILATIONS) - 1

            def step(i, carry, d=d, nb=nb, first_pattern=first_pattern, last_pattern=last_pattern):
                blocks = [_block_starts(U * i + u, nb, d) for u in range(U)]
                rows = [_strided(own, N_BACK, d) for own, _, _ in blocks]
                prev_rows = [_strided(prev, N_BACK, d) for _, prev, _ in blocks]
                ss = []
                for u, (_, _, n) in enumerate(blocks):
                    kw = jnp.concatenate([k_ref[prev_rows[u], :], k_ref[rows[u], :]], 0).astype(BF16)
                    qs = _stack_heads(q_ref[rows[u], :].astype(BF16))
                    b = bias[jnp.minimum(n, 1)]
                    ss.append(lax.dot_general(qs, kw, NT, preferred_element_type=F32) + jnp.concatenate([b, b], axis=0))
                ms = [jnp.max(s, axis=1, keepdims=True) for s in ss]
                ps = [jnp.exp(s - m) for s, m in zip(ss, ms)]
                ls = [jnp.sum(p, axis=1, keepdims=True) for p in ps]
                os_ = []
                for u in range(U):
                    vw = jnp.concatenate([v_ref[prev_rows[u], :], v_ref[rows[u], :]], 0).astype(BF16)
                    os_.append(jnp.dot(ps[u].astype(BF16), vw, preferred_element_type=F32))
                for u in range(U):
                    o_g, m_g, l_g = _unstack(os_[u]), _unstack_columns(ms[u]), _unstack_columns(ls[u])
                    r = rows[u]
                    if first_pattern:
                        m_new, l_new, acc = m_g, l_g, o_g
                    else:
                        m_old = m_ref[r, :]
                        m_new = jnp.maximum(m_old, m_g)
                        alpha, beta = jnp.exp(m_old - m_new), jnp.exp(m_g - m_new)
                        l_new = l_scr[r, :] * alpha + l_g * beta
                        acc = y_ref[r, :] * alpha + o_g * beta
                    if last_pattern:
                        y_ref[r, :] = acc / l_new
                        m_ref[r, :] = m_new + jnp.log(l_new)
                    else:
                        y_ref[r, :] = acc
                        m_ref[r, :] = m_new
                        l_scr[r, :] = l_new
                return carry

            lax.fori_loop(0, d * nb // U, step, 0)

    col = pl.BlockSpec((S, 2 * HEAD), lambda j: (0, j))
    return pl.pallas_call(
        body, name="attn_fwd", grid=(q.shape[1] // (2 * HEAD),),
        in_specs=[col, col, col], out_specs=[col, col],
        out_shape=[jax.ShapeDtypeStruct(q.shape, F32)] * 2,
        scratch_shapes=[pltpu.VMEM((S, 2 * HEAD), F32), pltpu.VMEM((2, N_BACK, 2 * N_BACK), F32)],
        compiler_params=_params("parallel"),
    )(q, k, v)


def _attn_bwd(q, k, v, dy, lse, delta, after):
    S = q.shape[0]
    U = BWD_BLOCKS_PER_STEP

    def body(q_ref, k_ref, v_ref, dy_ref, lse_ref, delta_ref, after_ref, dq_ref, dk_ref, dv_ref, bias):
        _fill_band_bias(bias)
        dk_ref[...] = jnp.zeros_like(dk_ref)
        dv_ref[...] = jnp.zeros_like(dv_ref)
        for g, d in enumerate(DILATIONS):
            nb = S // d // N_BACK

            def step(i, carry, d=d, nb=nb, g=g):
                blocks = [_block_starts(U * i + u, nb, d) for u in range(U)]
                rows = [_strided(own, N_BACK, d) for own, _, _ in blocks]
                prev_rows = [_strided(prev, N_BACK, d) for _, prev, _ in blocks]
                kws = [jnp.concatenate([k_ref[prev_rows[u], :], k_ref[rows[u], :]], 0).astype(BF16) for u in range(U)]
                vws = [jnp.concatenate([v_ref[prev_rows[u], :], v_ref[rows[u], :]], 0).astype(BF16) for u in range(U)]
                qss = [_stack_heads(q_ref[rows[u], :].astype(BF16)) for u in range(U)]
                doss = [_stack_heads(dy_ref[rows[u], :].astype(BF16)) for u in range(U)]
                ss, dps = [], []
                for u, (_, _, n) in enumerate(blocks):
                    b = bias[jnp.minimum(n, 1)]
                    ss.append(lax.dot_general(qss[u], kws[u], NT, preferred_element_type=F32) + jnp.concatenate([b, b], axis=0))
                    dps.append(lax.dot_general(doss[u], vws[u], NT, preferred_element_type=F32))
                ps = [jnp.exp(ss[u] - _head_columns(lse_ref[rows[u], :])) for u in range(U)]
                dss = [(ps[u] * (dps[u] - _head_columns(delta_ref[rows[u], :]))).astype(BF16) for u in range(U)]
                pbs = [p.astype(BF16) for p in ps]
                dqs = [jnp.dot(dss[u], kws[u], preferred_element_type=F32) for u in range(U)]
                dkws = [lax.dot_general(dss[u], qss[u], TN, preferred_element_type=F32) for u in range(U)]
                dvws = [lax.dot_general(pbs[u], doss[u], TN, preferred_element_type=F32) for u in range(U)]
                for u in range(U):
                    dq = _unstack(dqs[u])
                    if g == 0:
                        dq_ref[rows[u], :] = dq
                    else:
                        dq_ref[rows[u], :] += dq
                    dk_ref[prev_rows[u], :] += dkws[u][:N_BACK]
                    dv_ref[prev_rows[u], :] += dvws[u][:N_BACK]
                    dk_ref[rows[u], :] += dkws[u][N_BACK:]
                    dv_ref[rows[u], :] += dvws[u][N_BACK:]
                return carry

            lax.fori_loop(0, d * nb // U, step, 0)

    col = pl.BlockSpec((S, 2 * HEAD), lambda j: (0, j))
    return pl.pallas_call(
        body, name="attn_bwd", grid=(q.shape[1] // (2 * HEAD),),
        in_specs=[col] * 6 + [pl.BlockSpec(memory_space=pl.ANY)], out_specs=[col] * 3,
        out_shape=[jax.ShapeDtypeStruct(q.shape, F32)] * 3,
        scratch_shapes=[pltpu.VMEM((2, N_BACK, 2 * N_BACK), F32)],
        compiler_params=_params("parallel"),
    )(q, k, v, dy, lse, delta, after)


def _shift_down(z, before, k):
    row = lax.broadcasted_iota(jnp.int32, z.shape, 0)
    out = pltpu.roll(z, k, 0)
    for i in range(k):
        out = jnp.where(row == i, before[8 - k + i:8 - k + i + 1, :], out)
    return out


def _shift_up(z, after, k):
    rows = z.shape[0]
    row = lax.broadcasted_iota(jnp.int32, z.shape, 0)
    out = pltpu.roll(z, rows - k, 0)
    for i in range(k):
        out = jnp.where(row == rows - k + i, after[i:i + 1, :], out)
    return out


def _conv_fwd(bcu, before, is_first, w):
    b, c, u = bcu[:, 0:CONV_W], bcu[:, CONV_W:2 * CONV_W], bcu[:, 2 * CONV_W:3 * CONV_W]
    z = c * u
    zb = jnp.where(is_first, 0.0, before[:, CONV_W:2 * CONV_W] * before[:, 2 * CONV_W:3 * CONV_W])
    z1, z2 = _shift_down(z, zb, 1), _shift_down(z, zb, 2)
    cv = w[0:1, :] * z2 + w[1:2, :] * z1 + w[2:3, :] * z
    return b, c, u, z, z1, z2, cv


def _halo_before(tm, width):
    return pl.BlockSpec((8, width), lambda i: (jnp.maximum(i * (tm // 8) - 1, 0), 0))


def _halo_after(tm, width, S):
    return pl.BlockSpec((8, width), lambda i: (jnp.minimum((i + 1) * (tm // 8), S // 8 - 1), 0))


def _mix_fwd(ya, bcu, qx, mkv, conv_w, g_a, g_c, g_x, w_out, g_post, x, tm):
    S = x.shape[0]

    def body(ya_ref, bcu_ref, before_ref, qx_ref, mkv_ref, cw_ref, ga_ref, gc_ref, gx_ref,
             wo_ref, gp_ref, x_ref, yx_ref, ycat_ref, y2_ref, x1_ref):
        ya = ya_ref[...]
        b, _, _, _, _, _, cv = _conv_fwd(bcu_ref[...], before_ref[...], pl.program_id(0) == 0, cw_ref[...])
        yc = b * cv

        qxb, mkvb = qx_ref[...], mkv_ref[...]
        for hd in range(XATTN_W // HEAD):
            sl = slice(HEAD * hd, HEAD * (hd + 1))
            s = lax.dot_general(qxb[:, sl], mkvb[:, sl], NT, preferred_element_type=F32) * SCALE
            mx = jnp.max(s, axis=1, keepdims=True)
            p = jnp.exp(s - mx)
            l = jnp.sum(p, axis=1, keepdims=True)
            vm = mkvb[:, XATTN_W + HEAD * hd:XATTN_W + HEAD * (hd + 1)]
            yx_ref[:, sl] = jnp.dot(p.astype(BF16), vm, preferred_element_type=F32) / l
        yx = yx_ref[...]

        ycat_ref[:, 0:ATTN_W] = (_rms_hat(ya)[0] * ga_ref[...]).astype(BF16)
        ycat_ref[:, ATTN_W:ATTN_W + CONV_W] = (_rms_hat(yc)[0] * gc_ref[...]).astype(BF16)
        ycat_ref[:, ATTN_W + CONV_W:D_MODEL] = (_rms_hat(yx)[0] * gx_ref[...]).astype(BF16)
        y2 = jnp.dot(ycat_ref[...], wo_ref[...], preferred_element_type=F32)
        y2_ref[...] = y2
        x1_ref[...] = x_ref[...] + _rms_hat(y2)[0] * gp_ref[...]

    n_mem = mkv.shape[0]
    return pl.pallas_call(
        body, name="mix_fwd", grid=(S // tm,),
        in_specs=[_rows(tm, ATTN_W), _rows(tm, 3 * CONV_W), _halo_before(tm, 3 * CONV_W), _rows(tm, XATTN_W),
                  _resident((n_mem, 2 * XATTN_W)), _resident((3, CONV_W)), _resident((1, ATTN_W)),
                  _resident((1, CONV_W)), _resident((1, XATTN_W)), _resident((D_MODEL, D_MODEL)),
                  _resident((1, D_MODEL)), _rows(tm, D_MODEL)],
        out_specs=[_rows(tm, XATTN_W), _rows(tm, D_MODEL), _rows(tm, D_MODEL), _rows(tm, D_MODEL)],
        out_shape=[jax.ShapeDtypeStruct((S, XATTN_W), F32), jax.ShapeDtypeStruct((S, D_MODEL), BF16),
                   jax.ShapeDtypeStruct((S, D_MODEL), F32), jax.ShapeDtypeStruct((S, D_MODEL), F32)],
        compiler_params=_params("parallel"),
    )(ya, bcu, bcu, qx, mkv, conv_w, g_a, g_c, g_x, w_out, g_post, x)


def _mlp_fwd_bwd(x1, target, g_pre, g_post, w_up, w_down, tm):
    S = x1.shape[0]
    n_ff = D_FF // SHARD_FF

    def body(x1_ref, t_ref, gpre_ref, gpost_ref, wup_ref, wdn_ref,
             h2_ref, f_ref, du_ref, df2_ref, dx1_ref, dgpre_ref, dgpost_ref, loss_ref, u_scr):
        @pl.when(pl.program_id(0) == 0)
        def _():
            dgpre_ref[...] = jnp.zeros_like(dgpre_ref)
            dgpost_ref[...] = jnp.zeros_like(dgpost_ref)
            loss_ref[...] = jnp.zeros_like(loss_ref)

        x1 = x1_ref[...]
        x1hat, r1 = _rms_hat(x1)
        h2 = (x1hat * gpre_ref[...]).astype(BF16)
        h2_ref[...] = h2
        f2 = jnp.zeros((tm, D_MODEL), F32)
        for j in range(n_ff):
            cols = slice(SHARD_FF * j, SHARD_FF * (j + 1))
            u = jnp.maximum(jnp.dot(h2, wup_ref[j], preferred_element_type=F32), 0.0)
            u_scr[:, cols] = u
            f = (u * u).astype(BF16)
            f_ref[:, cols] = f
            f2 = f2 + jnp.dot(f, wdn_ref[cols, :], preferred_element_type=F32)
        f2hat, r2 = _rms_hat(f2)
        err = x1 + f2hat * gpost_ref[...] - t_ref[...]
        loss_ref[...] += 0.5 * jnp.sum(jnp.mean(err * err, axis=-1, keepdims=True), axis=0, keepdims=True)
        dx2 = err * (1.0 / D_MODEL)
        dgpost_ref[...] += jnp.sum(dx2 * f2hat, axis=0, keepdims=True)
        df2 = _rms_bwd(f2hat, r2, gpost_ref[...], dx2).astype(BF16)
        df2_ref[...] = df2
        dh2 = jnp.zeros((tm, D_MODEL), F32)
        for j in range(n_ff):
            cols = slice(SHARD_FF * j, SHARD_FF * (j + 1))
            df = lax.dot_general(df2, wdn_ref[cols, :], NT, preferred_element_type=F32)
            du = (2.0 * u_scr[:, cols] * df).astype(BF16)
            du_ref[:, cols] = du
            dh2 = dh2 + lax.dot_general(du, wup_ref[j], NT, preferred_element_type=F32)
        dgpre_ref[...] += jnp.sum(dh2 * x1hat, axis=0, keepdims=True)
        dx1_ref[...] = dx2 + _rms_bwd(x1hat, r1, gpre_ref[...], dh2)

    acc = pl.BlockSpec((1, D_MODEL), lambda i: (0, 0))
    return pl.pallas_call(
        body, name="mlp_fwd_bwd", grid=(S // tm,),
        in_specs=[_rows(tm, D_MODEL), _rows(tm, D_MODEL), _resident((1, D_MODEL)), _resident((1, D_MODEL)),
                  _resident((n_ff, D_MODEL, SHARD_FF)), _resident((D_FF, D_MODEL))],
        out_specs=[_rows(tm, D_MODEL), _rows(tm, D_FF), _rows(tm, D_FF), _rows(tm, D_MODEL), _rows(tm, D_MODEL),
                   acc, acc, pl.BlockSpec((1, 1), lambda i: (0, 0))],
        out_shape=[jax.ShapeDtypeStruct((S, D_MODEL), BF16), jax.ShapeDtypeStruct((S, D_FF), BF16),
                   jax.ShapeDtypeStruct((S, D_FF), BF16), jax.ShapeDtypeStruct((S, D_MODEL), BF16),
                   jax.ShapeDtypeStruct((S, D_MODEL), F32), jax.ShapeDtypeStruct((1, D_MODEL), F32),
                   jax.ShapeDtypeStruct((1, D_MODEL), F32), jax.ShapeDtypeStruct((1, 1), F32)],
        scratch_shapes=[pltpu.VMEM((tm, D_FF), F32)],
        compiler_params=_params("arbitrary"),
    )(x1, target, g_pre, g_post, w_up, w_down)


def _weight_grad(name, a, b, rows_sharded):
    S, K = a.shape
    N = b.shape[1]
    if rows_sharded:
        tk, tn = K // N_CHIPS, N
        a_spec = pl.BlockSpec((S, tk), lambda j: (0, j))
        b_spec = pl.BlockSpec((S, tn), lambda j: (0, 0), pipeline_mode=pl.Buffered(1))
    else:
        tk, tn = K, N // N_CHIPS
        a_spec = pl.BlockSpec((S, tk), lambda j: (0, 0), pipeline_mode=pl.Buffered(1))
        b_spec = pl.BlockSpec((S, tn), lambda j: (0, j))
    half = tk // 2

    def body(a_ref, b_ref, o_ref):
        res = lax.dot_general(a_ref[...], b_ref[...], TN, preferred_element_type=F32)
        o_ref[0, 0] = res[:half]
        o_ref[1, 0] = res[half:]

    return pl.pallas_call(
        body, name=name, grid=(N_CHIPS,), in_specs=[a_spec, b_spec],
        out_specs=pl.BlockSpec((2, 1, half, tn), lambda j: (0, j, 0, 0)),
        out_shape=jax.ShapeDtypeStruct((2, N_CHIPS, half, tn), F32),
        compiler_params=_params("parallel"),
    )(a, b)


def _mix_bwd(dx1, y2, ya, yx, bcu, conv_w, g_a, g_c, g_x, w_out, g_post, tm):
    S = dx1.shape[0]

    def body(dx1_ref, y2_ref, ya_ref, yx_ref, bcu_ref, before_ref, cw_ref, ga_ref, gc_ref, gx_ref, wo_ref, gp_ref,
             dy2_ref, dya_ref, delta_ref, dycx_ref, dgp_ref, dga_ref, dgc_ref, dgx_ref):
        @pl.when(pl.program_id(0) == 0)
        def _():
            for ref in (dgp_ref, dga_ref, dgc_ref, dgx_ref):
                ref[...] = jnp.zeros_like(ref)

        dx1 = dx1_ref[...]
        y2hat, r2 = _rms_hat(y2_ref[...])
        dgp_ref[...] += jnp.sum(dx1 * y2hat, axis=0, keepdims=True)
        dy2 = _rms_bwd(y2hat, r2, gp_ref[...], dx1).astype(BF16)
        dy2_ref[...] = dy2
        dycat = lax.dot_general(dy2, wo_ref[...], NT, preferred_element_type=F32)

        d_na = dycat[:, 0:ATTN_W]
        ya = ya_ref[...]
        yahat, ra = _rms_hat(ya)
        dga_ref[...] += jnp.sum(d_na * yahat, axis=0, keepdims=True)
        dya = _rms_bwd(yahat, ra, ga_ref[...], d_na)
        dya_ref[...] = dya
        prod = dya * ya
        hi = prod.astype(BF16)
        lo = (prod - hi.astype(F32)).astype(BF16)
        head_of = lambda axis: lax.shift_right_logical(lax.broadcasted_iota(jnp.int32, (ATTN_W, ATTN_W), axis),
                                                       HEAD.bit_length() - 1)
        same_head = head_of(0) == head_of(1)
        ones = jnp.where(same_head, 1.0, 0.0).astype(BF16)
        delta_ref[...] = (jnp.dot(hi, ones, preferred_element_type=F32) + jnp.dot(lo, ones, preferred_element_type=F32))

        b, _, _, _, _, _, cv = _conv_fwd(bcu_ref[...], before_ref[...], pl.program_id(0) == 0, cw_ref[...])
        d_nc = dycat[:, ATTN_W:ATTN_W + CONV_W]
        ychat, rc = _rms_hat(b * cv)
        dgc_ref[...] += jnp.sum(d_nc * ychat, axis=0, keepdims=True)
        dycx_ref[:, 0:CONV_W] = _rms_bwd(ychat, rc, gc_ref[...], d_nc)

        d_nx = dycat[:, ATTN_W + CONV_W:D_MODEL]
        yxhat, rx = _rms_hat(yx_ref[...])
        dgx_ref[...] += jnp.sum(d_nx * yxhat, axis=0, keepdims=True)
        dycx_ref[:, CONV_W:CONV_W + XATTN_W] = _rms_bwd(yxhat, rx, gx_ref[...], d_nx)

    acc = lambda w: pl.BlockSpec((1, w), lambda i: (0, 0))
    return pl.pallas_call(
        body, name="mix_bwd", grid=(S // tm,),
        in_specs=[_rows(tm, D_MODEL), _rows(tm, D_MODEL), _rows(tm, ATTN_W), _rows(tm, XATTN_W),
                  _rows(tm, 3 * CONV_W), _halo_before(tm, 3 * CONV_W), _resident((3, CONV_W)),
                  _resident((1, ATTN_W)), _resident((1, CONV_W)), _resident((1, XATTN_W)),
                  _resident((D_MODEL, D_MODEL)), _resident((1, D_MODEL))],
        out_specs=[_rows(tm, D_MODEL), _rows(tm, ATTN_W), _rows(tm, ATTN_W), _rows(tm, CONV_W + XATTN_W),
                   acc(D_MODEL), acc(ATTN_W), acc(CONV_W), acc(XATTN_W)],
        out_shape=[jax.ShapeDtypeStruct((S, D_MODEL), BF16), jax.ShapeDtypeStruct((S, ATTN_W), F32),
                   jax.ShapeDtypeStruct((S, ATTN_W), F32),
                   jax.ShapeDtypeStruct((S, CONV_W + XATTN_W), F32), jax.ShapeDtypeStruct((1, D_MODEL), F32),
                   jax.ShapeDtypeStruct((1, ATTN_W), F32), jax.ShapeDtypeStruct((1, CONV_W), F32),
                   jax.ShapeDtypeStruct((1, XATTN_W), F32)],
        compiler_params=_params("arbitrary"),
    )(dx1, y2, ya, yx, bcu, bcu, conv_w, g_a, g_c, g_x, w_out, g_post)


def _conv_xattn_bwd(dycx, bcu, qx, mkv, conv_w, tm):
    S = dycx.shape[0]
    n_mem = mkv.shape[0]
    n_tiles = S // tm

    def body(d_ref, dafter_ref, bcu_ref, before_ref, after_ref, qx_ref, mkv_ref, cw_ref,
             tail_ref, dmkv_ref, dcw_ref):
        i = pl.program_id(0)

        @pl.when(i == 0)
        def _():
            dmkv_ref[...] = jnp.zeros_like(dmkv_ref)
            dcw_ref[...] = jnp.zeros_like(dcw_ref)

        w = cw_ref[...]
        b, c, u, z, z1, z2, cv = _conv_fwd(bcu_ref[...], before_ref[...], i == 0, w)
        dyc = d_ref[:, 0:CONV_W]
        dcv = dyc * b
        dcv_after = jnp.where(i == n_tiles - 1, 0.0, dafter_ref[:, 0:CONV_W] * after_ref[:, 0:CONV_W])
        dz = w[2:3, :] * dcv + w[1:2, :] * _shift_up(dcv, dcv_after, 1) + w[0:1, :] * _shift_up(dcv, dcv_after, 2)
        dcw_ref[0:1, :] += jnp.sum(dcv * z2, axis=0, keepdims=True)
        dcw_ref[1:2, :] += jnp.sum(dcv * z1, axis=0, keepdims=True)
        dcw_ref[2:3, :] += jnp.sum(dcv * z, axis=0, keepdims=True)
        tail_ref[:, 0:CONV_W] = (dyc * cv).astype(BF16)
        tail_ref[:, CONV_W:2 * CONV_W] = (dz * u).astype(BF16)
        tail_ref[:, 2 * CONV_W:3 * CONV_W] = (dz * c).astype(BF16)

        qxb, mkvb = qx_ref[...], mkv_ref[...]
        for hd in range(XATTN_W // HEAD):
            sl = slice(HEAD * hd, HEAD * (hd + 1))
            vsl = slice(XATTN_W + HEAD * hd, XATTN_W + HEAD * (hd + 1))
            s = lax.dot_general(qxb[:, sl], mkvb[:, sl], NT, preferred_element_type=F32) * SCALE
            e = jnp.exp(s - jnp.max(s, axis=1, keepdims=True))
            p = e / jnp.sum(e, axis=1, keepdims=True)
            dob = d_ref[:, CONV_W + HEAD * hd:CONV_W + HEAD * (hd + 1)].astype(BF16)
            dp = lax.dot_general(dob, mkvb[:, vsl], NT, preferred_element_type=F32)
            ds = (p * (dp - jnp.sum(p * dp, axis=1, keepdims=True)) * SCALE).astype(BF16)
            tail_ref[:, 3 * CONV_W + HEAD * hd:3 * CONV_W + HEAD * (hd + 1)] = jnp.dot(
                ds, mkvb[:, sl], preferred_element_type=F32).astype(BF16)
            dmkv_ref[:, sl] += lax.dot_general(ds, qxb[:, sl], TN, preferred_element_type=F32)
            dmkv_ref[:, vsl] += lax.dot_general(p.astype(BF16), dob, TN, preferred_element_type=F32)

    width = CONV_W + XATTN_W
    return pl.pallas_call(
        body, name="conv_xattn_bwd", grid=(n_tiles,),
        in_specs=[_rows(tm, width), _halo_after(tm, width, S), _rows(tm, 3 * CONV_W), _halo_before(tm, 3 * CONV_W),
                  _halo_after(tm, 3 * CONV_W, S), _rows(tm, XATTN_W), _resident((n_mem, 2 * XATTN_W)),
                  _resident((3, CONV_W))],
        out_specs=[_rows(tm, 3 * CONV_W + XATTN_W), pl.BlockSpec((n_mem, 2 * XATTN_W), lambda i: (0, 0)),
                   pl.BlockSpec((3, CONV_W), lambda i: (0, 0))],
        out_shape=[jax.ShapeDtypeStruct((S, 3 * CONV_W + XATTN_W), BF16),
                   jax.ShapeDtypeStruct((n_mem, 2 * XATTN_W), F32), jax.ShapeDtypeStruct((3, CONV_W), F32)],
        compiler_params=_params("arbitrary"),
    )(dycx, dycx, bcu, bcu, bcu, qx, mkv, conv_w)


def _memkv_bwd(mem, g_mem, w_kv, dmkv):
    n_mem = mem.shape[0]
    half = D_MODEL // N_CHIPS // 2

    def body(mem_ref, g_ref, w_ref, d_ref, dw_ref, dg_ref):
        mhat, _ = _rms_hat(mem_ref[...])
        mn = (mhat * g_ref[...]).astype(BF16)
        d = d_ref[...].astype(BF16)
        for k in range(2 * N_CHIPS):
            dw_ref[k % 2, k // 2] = lax.dot_general(mn[:, half * k:half * (k + 1)], d, TN, preferred_element_type=F32)
        dmn = lax.dot_general(d, w_ref[...], NT, preferred_element_type=F32)
        dg_ref[...] = jnp.sum(dmn * mhat, axis=0, keepdims=True)

    return pl.pallas_call(
        body, name="memkv_bwd",
        out_shape=[jax.ShapeDtypeStruct((2, N_CHIPS, half, 2 * XATTN_W), F32), jax.ShapeDtypeStruct((1, D_MODEL), F32)],
        compiler_params=pltpu.CompilerParams(vmem_limit_bytes=VMEM_LIMIT_V7X),
    )(mem, g_mem, w_kv, dmkv)


def _in_proj_bwd(dqkv, tail, cos, sin, w_in, x, g, dx1, tm):
    S = x.shape[0]

    def body(dq_ref, dk_ref, dv_ref, tail_ref, cos_ref, sin_ref, w_ref, x_ref, g_ref, dx1_ref, dproj_ref, dx_ref, dg_ref):
        @pl.when(pl.program_id(0) == 0)
        def _():
            dg_ref[...] = jnp.zeros_like(dg_ref)

        c, s = cos_ref[...], sin_ref[...]
        for j in range(ATTN_W // 128):
            cols = slice(128 * j, 128 * (j + 1))
            dproj_ref[:, cols] = _rope128(dq_ref[:, cols] * SCALE, c, s, True).astype(BF16)
            dproj_ref[:, ATTN_W + 128 * j:ATTN_W + 128 * (j + 1)] = _rope128(dk_ref[:, cols], c, s, True).astype(BF16)
        dproj_ref[:, 2 * ATTN_W:3 * ATTN_W] = dv_ref[...].astype(BF16)
        dproj_ref[:, 3 * ATTN_W:PROJ_W] = tail_ref[...]
        dh = jnp.zeros((tm, D_MODEL), F32)
        for j in range(N_CHIPS):
            dh = dh + lax.dot_general(dproj_ref[:, SHARD_IN * j:SHARD_IN * (j + 1)], w_ref[j], NT,
                                      preferred_element_type=F32)
        xhat, r = _rms_hat(x_ref[...])
        dg_ref[...] += jnp.sum(dh * xhat, axis=0, keepdims=True)
        dx_ref[...] = dx1_ref[...] + _rms_bwd(xhat, r, g_ref[...], dh)

    return pl.pallas_call(
        body, name="in_proj_bwd", grid=(S // tm,),
        in_specs=[_rows(tm, ATTN_W)] * 3 + [_rows(tm, PROJ_W - 3 * ATTN_W), _rows(tm, 128), _rows(tm, 128),
                  _resident((N_CHIPS, D_MODEL, SHARD_IN)), _rows(tm, D_MODEL), _resident((1, D_MODEL)),
                  _rows(tm, D_MODEL)],
        out_specs=[_rows(tm, PROJ_W), _rows(tm, D_MODEL), pl.BlockSpec((1, D_MODEL), lambda i: (0, 0))],
        out_shape=[jax.ShapeDtypeStruct((S, PROJ_W), BF16), jax.ShapeDtypeStruct((S, D_MODEL), F32),
                   jax.ShapeDtypeStruct((1, D_MODEL), F32)],
        compiler_params=_params("arbitrary"),
    )(*dqkv, tail, cos, sin, w_in, x, g, dx1)


def _row_tile(rows):
    return ROW_TILE if rows % ROW_TILE == 0 else rows


def _chip_sum_bf16(name, grad, from_sibling, place):
    _, n, rows, cols = grad.shape
    tr = _row_tile(rows)

    def body(place_ref, g_ref, b_ref, o_ref):
        o_ref[...] = (g_ref[0] + b_ref[...]).astype(BF16)

    spec = pl.BlockSpec((1, tr, cols), lambda s, i, p: (s, i, 0))
    return pl.pallas_call(
        body, name=name, out_shape=jax.ShapeDtypeStruct((n, rows, cols), BF16),
        grid_spec=pltpu.PrefetchScalarGridSpec(
            num_scalar_prefetch=1, grid=(n, rows // tr),
            in_specs=[pl.BlockSpec((1, 1, tr, cols), lambda s, i, p: (p[0], s, i, 0)), spec], out_specs=spec),
        compiler_params=_params("parallel", "parallel"),
    )(place, grad, from_sibling)


def _final_sum(name, grad, from_sibling, others, place):
    _, _, rows, cols = grad.shape
    tr = _row_tile(rows)

    def body(place_ref, own_ref, sib_ref, o0, o1, o2, out_ref):
        acc = own_ref[0, 0] + sib_ref[0]
        for o in (o0, o1, o2):
            acc = acc + o[0].astype(F32)
        out_ref[...] = acc

    other = lambda k: pl.BlockSpec((1, tr, cols), lambda i, p: (k, i, 0))
    return pl.pallas_call(
        body, name=name, out_shape=jax.ShapeDtypeStruct((rows, cols), F32),
        grid_spec=pltpu.PrefetchScalarGridSpec(
            num_scalar_prefetch=1, grid=(rows // tr,),
            in_specs=[pl.BlockSpec((1, 1, tr, cols), lambda i, p: (p[0], p[1], i, 0)),
                      pl.BlockSpec((1, tr, cols), lambda i, p: (p[1], i, 0)), other(0), other(1), other(2)],
            out_specs=pl.BlockSpec((tr, cols), lambda i, p: (i, 0))),
        compiler_params=_params("parallel"),
    )(place, grad, from_sibling, others, others, others)


def _adamw(name, w, g, m, v, after):
    rows, cols = w.shape
    tr = _row_tile(rows)

    def body(w_ref, g_ref, m_ref, v_ref, after_ref, d_ref, nm_ref, nv_ref):
        g = g_ref[...]
        m = ADAM_B1 * m_ref[...] + (1.0 - ADAM_B1) * g
        v = ADAM_B2 * v_ref[...] + (1.0 - ADAM_B2) * (g * g)
        m_hat = m / (1.0 - ADAM_B1 ** ADAM_STEP)
        v_hat = v / (1.0 - ADAM_B2 ** ADAM_STEP)
        d_ref[...] = -ADAM_LR * (m_hat / (jnp.sqrt(v_hat) + ADAM_EPS) + ADAM_WD * w_ref[...])
        nm_ref[...] = m
        nv_ref[...] = v

    spec = pl.BlockSpec((tr, cols), lambda i: (i, 0))
    return pl.pallas_call(
        body, name=name, grid=(rows // tr,), in_specs=[spec] * 4 + [pl.BlockSpec(memory_space=pl.ANY)],
        out_specs=[spec] * 3, out_shape=[jax.ShapeDtypeStruct(w.shape, F32)] * 3, compiler_params=_params("parallel"),
    )(w, g, m, v, after)


def _sum_blocks(name, blocks):
    n, rows, cols = blocks.shape

    def body(b_ref, o_ref):
        acc = b_ref[0]
        for k in range(1, n):
            acc = acc + b_ref[k]
        o_ref[...] = acc

    return pl.pallas_call(body, name=name, out_shape=jax.ShapeDtypeStruct((rows, cols), F32))(blocks)


def _place():
    return lax.axis_index("x"), lax.axis_index("y"), lax.axis_index("c")


def _other_chips(x, y):
    return [(1 - x, y), (x, 1 - y), (1 - x, 1 - y)]


def _weights_allgather(name, shards, landed=None):
    n = len(shards)
    first_hop = landed is None

    def body(*refs):
        ins, outs, stage = refs[:n], refs[-3 - 2 * n:-3 - n], refs[-3 - n:-3]
        send_sems, recv_sems, local_sems = refs[-3:]
        x, y, c = _place()
        me, sibling = (x, y, c), (x, y, 1 - c)
        chips = _other_chips(x, y)
        chip_index = lambda chip: 2 * chip[0] + chip[1]

        def copy(a, k, chip, half, to, src=None):
            place = outs[a].at[chip_index(chip), half]
            return pltpu.make_async_remote_copy(
                src_ref=place if src is None else src, dst_ref=place, send_sem=send_sems.at[6 * a + k],
                recv_sem=recv_sems.at[6 * a + k], device_id=to, device_id_type=MESH)

        load = [pltpu.make_async_copy(ins[a], stage[a], local_sems.at[a]) for a in range(n)]
        local = [pltpu.make_async_copy(stage[a], outs[a].at[chip_index((x, y))], local_sems.at[a]) for a in range(n)]
        for cp in load:
            cp.start()
        first = []
        if first_hop:
            first = [copy(a, k, (x, y), c, (*chip, c), src=ins[a].at[c]) for a in range(n) for k, chip in enumerate(chips)]
        for cp in first:
            cp.start()
        for a in range(n):
            load[a].wait()
            local[a].start()
        passed = []
        for a in range(n):
            for k, chip in enumerate(chips):
                if first_hop:
                    copy(a, k, chip, c, me).wait_recv()
                passed.append(copy(a, 3 + k, chip, c, sibling))
                passed[-1].start()
        for a in range(n):
            for k, chip in enumerate(chips):
                copy(a, 3 + k, chip, 1 - c, me).wait_recv()
        for cp in first + passed:
            cp.wait_send()
        for cp in local:
            cp.wait()

    any_spec = pl.BlockSpec(memory_space=pl.ANY)
    operands = list(shards) + ([] if first_hop else list(landed))
    return pl.pallas_call(
        body, name=name,
        out_shape=[jax.ShapeDtypeStruct((N_CHIPS,) + s.shape, s.dtype) for s in shards],
        in_specs=[any_spec] * len(operands), out_specs=[any_spec] * n,
        input_output_aliases={} if first_hop else {n + a: a for a in range(n)},
        scratch_shapes=[pltpu.VMEM(s.shape, s.dtype) for s in shards]
        + [pltpu.SemaphoreType.DMA((6 * n,)), pltpu.SemaphoreType.DMA((6 * n,)), pltpu.SemaphoreType.DMA((n,))],
        compiler_params=pltpu.CompilerParams(vmem_limit_bytes=VMEM_LIMIT_V7X),
    )(*operands)


def _plan_first_hop(x, y, c, shards, lands):
    return [(shards[a].at[c], lands[a].at[2 * x + y, c], lands[a].at[2 * chip[0] + chip[1], c], (*chip, c))
            for a in range(len(shards)) for chip in _other_chips(x, y)]


def _plan_other_half_to_sibling(x, y, c, grads, lands):
    return [(grads[a].at[1 - c], lands[a], lands[a], (x, y, 1 - c)) for a in range(len(grads))]


def _plan_to_other_chips(x, y, c, partials, lands):
    return [(partials[a].at[2 * chip[0] + chip[1]], lands[a].at[k], lands[a].at[k], (*chip, c))
            for a in range(len(partials)) for k, chip in enumerate(_other_chips(x, y))]


def _plan_to_all(x, y, c, blocks, lands):
    flips = [(fx, fy, fc) for fx in (0, 1) for fy in (0, 1) for fc in (0, 1) if (fx, fy, fc) != (0, 0, 0)]
    peers = [(1 - x if fx else x, 1 - y if fy else y, 1 - c if fc else c) for fx, fy, fc in flips]
    return [(blocks[0], lands[0].at[4 * x + 2 * y + c], lands[0].at[4 * p[0] + 2 * p[1] + p[2]], p) for p in peers]


def _planned_copies(plan, srcs, lands, send_sems, recv_sems):
    x, y, c = _place()

    def pair(k, src, there, here, to):
        make = lambda dst: pltpu.make_async_remote_copy(
            src_ref=src, dst_ref=dst, send_sem=send_sems.at[k], recv_sem=recv_sems.at[k], device_id=to, device_id_type=MESH)
        return make(there), make(here)

    return [pair(k, *entry) for k, entry in enumerate(plan(x, y, c, srcs, lands))]


_HBM_SPEC = pl.BlockSpec(memory_space=pltpu.HBM)
_SEM_SPEC = pl.BlockSpec(memory_space=pltpu.SEMAPHORE)


def _hbm(a):
    return pltpu.with_memory_space_constraint(a, pltpu.HBM)


def _exchange_start(name, plan, n_copies, srcs, land_shapes, after):
    ns, nl = len(srcs), len(land_shapes)
    n_in = ns + nl + 1

    def body(*refs):
        for send, _ in _planned_copies(plan, refs[:ns], refs[ns:ns + nl], refs[n_in], refs[n_in + 1]):
            send.start()
        refs[-1][...] = jnp.zeros_like(refs[-1])

    out = pl.pallas_call(
        body, name=name,
        out_shape=(pltpu.SemaphoreType.DMA((n_copies,)), pltpu.SemaphoreType.DMA((n_copies,)),
                   *[pltpu.HBM(s.shape, s.dtype) for s in land_shapes], jax.ShapeDtypeStruct((8, 128), F32)),
        in_specs=[_HBM_SPEC] * (ns + nl) + [pl.BlockSpec(memory_space=pl.ANY)],
        out_specs=(_SEM_SPEC, _SEM_SPEC, *[_HBM_SPEC] * nl, pl.BlockSpec(memory_space=pltpu.VMEM)),
        input_output_aliases={ns + i: 2 + i for i in range(nl)},
        compiler_params=pltpu.CompilerParams(has_side_effects=pltpu.SideEffectType.DATAFLOW_SIDE_EFFECTING),
    )(*[_hbm(s) for s in srcs], *[_hbm(lax.empty(s.shape, s.dtype)) for s in land_shapes], after)
    return out[0], out[1], list(out[2:2 + nl]), out[-1]


def _exchange_wait(name, plan, srcs, started, after):
    send_sems, recv_sems, lands, _ = started
    ns, nl = len(srcs), len(lands)

    def body(*refs):
        for send, recv in _planned_copies(plan, refs[:ns], refs[ns:ns + nl], refs[ns + nl], refs[ns + nl + 1]):
            send.wait_send()
            recv.wait_recv()

    return pl.pallas_call(
        body, name=name, out_shape=[pltpu.HBM(l.shape, l.dtype) for l in lands],
        in_specs=[_HBM_SPEC] * (ns + nl) + [_SEM_SPEC, _SEM_SPEC, pl.BlockSpec(memory_space=pl.ANY)],
        out_specs=[_HBM_SPEC] * nl, input_output_aliases={ns + i: i for i in range(nl)},
        compiler_params=pltpu.CompilerParams(has_side_effects=pltpu.SideEffectType.DATAFLOW_SIDE_EFFECTING),
    )(*[_hbm(s) for s in srcs], *lands, send_sems, recv_sems, after)


def _exchange_halves(name, halves, after):
    n = len(halves)

    def body(*refs):
        ins, outs, stage = refs[:n], refs[n + 1:2 * n + 1], refs[2 * n + 1:3 * n + 1]
        send_sems, recv_sems, local_sems = refs[3 * n + 1:]
        x, y, c = _place()
        load = [pltpu.make_async_copy(ins[a], stage[a], local_sems.at[a]) for a in range(n)]
        local = [pltpu.make_async_copy(stage[a], outs[a].at[c], local_sems.at[a]) for a in range(n)]
        remote = [pltpu.make_async_remote_copy(
            src_ref=stage[a], dst_ref=outs[a].at[c], send_sem=send_sems.at[a], recv_sem=recv_sems.at[a],
            device_id=(x, y, 1 - c), device_id_type=MESH) for a in range(n)]
        for cp in load:
            cp.start()
        for a in range(n):
            load[a].wait()
            remote[a].start()
            local[a].start()
        for a in range(n):
            pltpu.make_async_remote_copy(
                src_ref=ins[a], dst_ref=outs[a].at[1 - c], send_sem=send_sems.at[a], recv_sem=recv_sems.at[a],
                device_id=(x, y, 1 - c), device_id_type=MESH).wait_recv()
        for cp in remote:
            cp.wait_send()
        for cp in local:
            cp.wait()

    any_spec = pl.BlockSpec(memory_space=pl.ANY)
    return pl.pallas_call(
        body, name=name,
        out_shape=[jax.ShapeDtypeStruct((2,) + h.shape, h.dtype) for h in halves],
        in_specs=[any_spec] * (n + 1), out_specs=[any_spec] * n,
        scratch_shapes=[pltpu.VMEM(h.shape, h.dtype) for h in halves]
        + [pltpu.SemaphoreType.DMA((n,)), pltpu.SemaphoreType.DMA((n,)), pltpu.SemaphoreType.DMA((n,))],
        compiler_params=pltpu.CompilerParams(vmem_limit_bytes=VMEM_LIMIT_V7X),
    )(*halves, after)


def _like(arrays, lead, dtype=None):
    return [jax.ShapeDtypeStruct(tuple(lead) + a.shape[-2:], dtype or a.dtype) for a in arrays]


class _StepExchanges:
    def __init__(self, mats, conv_w):
        x, y, c = _place()
        self.place = jnp.stack([c, 2 * x + y]).astype(jnp.int32)
        shards = [w.astype(BF16).reshape(2, w.shape[0] // 2, w.shape[1]) for w in mats]
        (w_in,) = _weights_allgather("w_in_allgather", shards[:1])
        self.w_in = w_in.reshape(N_CHIPS, 2 * w_in.shape[2], w_in.shape[3])
        taps = jnp.pad(conv_w, ((0, 8 - conv_w.shape[0]), (0, 128 - conv_w.shape[1])))
        self._rest_shards = shards[1:] + [jnp.stack([taps, jnp.zeros_like(taps)])]
        self._rest = _exchange_start("rest_allgather_start", _plan_first_hop, 3 * len(self._rest_shards),
                                     self._rest_shards, _like(self._rest_shards, (N_CHIPS, 2)), w_in)
        self.zero = self._rest[3]
        self._taps_shape = conv_w.shape
        self._groups = {}

    def rest_weights(self, after):
        landed = _exchange_wait("rest_allgather_wait", _plan_first_hop, self._rest_shards, self._rest, after)
        *mats, taps = _weights_allgather("rest_allgather_finish", self._rest_shards, landed=landed)
        k, w = self._taps_shape
        taps = taps[:, 0, :k, :w].transpose(1, 0, 2).reshape(k, N_CHIPS * w)
        return [g.reshape(N_CHIPS, 2 * g.shape[2], g.shape[3]) for g in mats], taps

    def send_grads(self, key, grads):
        grads = list(grads)
        started = _exchange_start(f"{key}_grads_to_sibling_start", _plan_other_half_to_sibling, len(grads), grads,
                                  _like(grads, (N_CHIPS,)), grads[-1])
        self._groups[key] = dict(grads=grads, to_sibling=started)
        self.zero = started[3]

    def grads_at_sibling(self, key, after):
        group = self._groups[key]
        grads = group["grads"]
        group["from_sibling"] = _exchange_wait(f"{key}_grads_to_sibling_wait", _plan_other_half_to_sibling, grads,
                                               group["to_sibling"], after)
        group["partials"] = [_chip_sum_bf16(f"{key}_chip_sum_{a}", grads[a], group["from_sibling"][a], self.place)
                             for a in range(len(grads))]
        group["to_chips"] = _exchange_start(f"{key}_grads_to_chips_start", _plan_to_other_chips, 3 * len(grads),
                                            group["partials"], _like(group["partials"], (3,)), group["partials"][-1])
        self.zero = group["to_chips"][3]

    def grads_summed(self, key, after):
        group = self._groups[key]
        from_chips = _exchange_wait(f"{key}_grads_to_chips_wait", _plan_to_other_chips, group["partials"],
                                    group["to_chips"], after)
        return [_final_sum(f"{key}_final_sum_{a}", group["grads"][a], group["from_sibling"][a], from_chips[a], self.place)
                for a in range(len(from_chips))]

    def send_small(self, block):
        self._small = block
        self._small_started = _exchange_start("small_grads_start", _plan_to_all, 7, [block],
                                              [jax.ShapeDtypeStruct((8,) + block.shape, block.dtype)], block)
        self.zero = self._small_started[3]

    def small_summed(self, after):
        x, y, c = _place()
        (landed,) = _exchange_wait("small_grads_wait", _plan_to_all, [self._small], self._small_started, after)
        blocks = lax.dynamic_update_index_in_dim(landed, self._small, 4 * x + 2 * y + c, 0)
        return _sum_blocks("small_sum", blocks)


def _rope_tables(positions):
    half = HEAD // 2
    inv_freq = jnp.float32(ROPE_THETA) ** (-(jnp.arange(half, dtype=F32) * 2.0 / HEAD))
    ang = positions.astype(F32)[:, None] * inv_freq
    cos, sin = jnp.cos(ang), jnp.sin(ang)
    return jnp.tile(cos, (1, 4)), jnp.tile(jnp.concatenate([-sin, sin], axis=1), (1, 2))


def _local_step(x, mem, positions, target, gains, ex):
    g_pre_mix, g_mem, g_a, g_c, g_x, g_post_mix, g_pre_mlp, g_post_mlp = gains
    tm = ROW_TILE
    cos, sin = _rope_tables(positions)
    w_in = ex.w_in

    h, q, k, v, bcu, qx = _in_proj_fwd(x, g_pre_mix + ex.zero[:1, :1], w_in, cos, sin, tm)
    ya, lse = _attn_fwd(q, k, v)
    (w_kv, w_out, w_up, w_down), conv_w = ex.rest_weights(lse)
    w_kv, w_out, w_down = (w.reshape(N_CHIPS * w.shape[1], w.shape[2]) for w in (w_kv, w_out, w_down))
    memn, mkv = _memkv_fwd(mem, g_mem, w_kv)
    yx, ycat, y2, x1 = _mix_fwd(ya, bcu, qx, mkv, conv_w, g_a, g_c, g_x, w_out, g_post_mix, x, tm)
    h2, f, du, df2, dx1, dg_pre_mlp, dg_post_mlp, loss = _mlp_fwd_bwd(x1, target, g_pre_mlp, g_post_mlp, w_up, w_down, tm)
    gw_down = _weight_grad("grad_w_down", f, df2, True)
    gw_up = _weight_grad("grad_w_up", h2, du, False)
    ex.send_grads("early", [gw_up, gw_down])

    dy2, dya, delta, dycx, dg_post_mix, dg_a, dg_c, dg_x = _mix_bwd(dx1, y2, ya, yx, bcu, conv_w, g_a, g_c, g_x,
                                                                  w_out, g_post_mix + ex.zero[:1, :1], tm)
    ex.grads_at_sibling("early", dy2)
    gw_out = _weight_grad("grad_w_out", ycat, dy2, True)
    tail, dmkv, g_conv = _conv_xattn_bwd(dycx, bcu, qx, mkv, conv_w + ex.zero[:1, :1], tm)
    gw_kv, dg_mem = _memkv_bwd(mem, g_mem, w_kv, dmkv)
    ex.send_grads("mid", [gw_out, gw_kv])
    dqkv = _attn_bwd(q, k, v, dya, lse, delta, ex.zero)
    ex.grads_at_sibling("mid", dqkv[0])
    dproj, grad_x, dg_pre_mix = _in_proj_bwd(dqkv, tail, cos, sin, w_in, x, g_pre_mix + ex.zero[:1, :1], dx1, tm)
    gain_grads = [dg_pre_mix, dg_mem, dg_a, dg_c, dg_x, dg_post_mix, dg_pre_mlp, dg_post_mlp]
    ex.send_small(_pack_small(gain_grads, g_conv, loss))
    gw_in = _weight_grad("grad_w_in", h, dproj, False)
    ex.send_grads("late", [gw_in])
    return grad_x


def _pack_small(gains, conv, scalar=None):
    rows = [jnp.pad(g, ((0, 0), (0, D_MODEL - g.shape[1]))) for g in gains]
    rows.append(jnp.pad(conv, ((0, 0), (0, D_MODEL - conv.shape[1]))))
    last = jnp.zeros((SMALL_ROWS - 8 - conv.shape[0], D_MODEL), F32)
    rows.append(last if scalar is None else last.at[0:1, 0:1].set(scalar))
    return jnp.concatenate(rows, axis=0)


def _unpack_small(block, gain_widths, conv_width):
    gains = [block[i:i + 1, :w] for i, w in enumerate(gain_widths)]
    return gains, block[8:11, :conv_width], block[11, 0]


def kernel(x, mem, positions, g_pre_mix, g_mem, w_in, w_mem_kv, conv_w, g_attn_out, g_conv_out, g_xattn_out, w_out, g_post_mix, g_pre_mlp, w_up, w_down, g_post_mlp, loss_target, m_g_pre_mix, m_g_mem, m_w_in, m_w_mem_kv, m_conv_w, m_g_attn_out, m_g_conv_out, m_g_xattn_out, m_w_out, m_g_post_mix, m_g_pre_mlp, m_w_up, m_w_down, m_g_post_mlp, v_g_pre_mix, v_g_mem, v_w_in, v_w_mem_kv, v_conv_w, v_g_attn_out, v_g_conv_out, v_g_xattn_out, v_w_out, v_g_post_mix, v_g_pre_mlp, v_w_up, v_w_down, v_g_post_mlp):
    cx, cy, cc = _place()
    chip = 2 * cx + cy
    gains = [g_pre_mix, g_mem, g_attn_out, g_conv_out, g_xattn_out, g_post_mix, g_pre_mlp, g_post_mlp]
    gains_m = [m_g_pre_mix, m_g_mem, m_g_attn_out, m_g_conv_out, m_g_xattn_out, m_g_post_mix, m_g_pre_mlp, m_g_post_mlp]
    gains_v = [v_g_pre_mix, v_g_mem, v_g_attn_out, v_g_conv_out, v_g_xattn_out, v_g_post_mix, v_g_pre_mlp, v_g_post_mlp]
    gain_widths = [g.shape[1] for g in gains]
    mats = [w_in[0], w_mem_kv[0], w_out[0], w_up[0], w_down[0]]
    mats_m = [m_w_in[0], m_w_mem_kv[0], m_w_out[0], m_w_up[0], m_w_down[0]]
    mats_v = [v_w_in[0], v_w_mem_kv[0], v_w_out[0], v_w_up[0], v_w_down[0]]

    ex = _StepExchanges(mats, conv_w[0])
    grad_x = _local_step(x[0], mem[0], positions[0], loss_target[0], gains, ex)

    both = lambda halves: [t.reshape(2 * t.shape[1], t.shape[2]) for t in halves]
    done = ex.grads_summed("early", ex.zero) + ex.grads_summed("mid", ex.zero)
    ex.grads_at_sibling("late", done[-1])
    up_sum, down_sum, out_sum, kv_sum = both(_exchange_halves("sums_to_sibling", done, ex.zero))
    adamw = lambda a, g, after: _adamw(f"adamw_{a}", mats[a], g, mats_m[a], mats_v[a], after)
    new_up, new_down, new_out, new_kv = adamw(3, up_sum, ex.zero), adamw(4, down_sum, ex.zero), adamw(2, out_sum, ex.zero), adamw(1, kv_sum, ex.zero)

    gain_sums, conv_sum_full, total = _unpack_small(ex.small_summed(new_kv[0]), gain_widths, CONV_W)
    conv_sum = lax.dynamic_slice_in_dim(conv_sum_full, chip * conv_w.shape[2], conv_w.shape[2], axis=1)
    pack = lambda gs, cv: _pack_small(gs, cv)
    small_new = _adamw("adamw_small", pack(gains, conv_w[0]), pack(gain_sums, conv_sum), pack(gains_m, m_conv_w[0]),
                       pack(gains_v, v_conv_w[0]), ex.zero)
    small_out = [_unpack_small(t, gain_widths, conv_w.shape[2])[:2] for t in small_new]

    (in_half,) = ex.grads_summed("late", small_new[0])
    (in_sum,) = both(_exchange_halves("late_sum_to_sibling", [in_half], in_half))
    new_in = adamw(0, in_sum, in_sum)
    mat_sums = [in_sum, kv_sum, out_sum, up_sum, down_sum]
    mat_new = [new_in, new_kv, new_out, new_up, new_down]

    order = ["g_pre_mix", "g_mem", "w_in", "w_mem_kv", "conv_w", "g_attn_out", "g_conv_out", "g_xattn_out", "w_out",
             "g_post_mix", "g_pre_mlp", "w_up", "w_down", "g_post_mlp"]
    gain_names = ["g_pre_mix", "g_mem", "g_attn_out", "g_conv_out", "g_xattn_out", "g_post_mix", "g_pre_mlp", "g_post_mlp"]
    mat_names = ["w_in", "w_mem_kv", "w_out", "w_up", "w_down"]

    def leaf(kind, name):
        if name in gain_names:
            i = gain_names.index(name)
            return gain_sums[i] if kind == 0 else small_out[kind - 1][0][i]
        if name == "conv_w":
            return (conv_sum if kind == 0 else small_out[kind - 1][1])[None]
        a = mat_names.index(name)
        return (mat_sums[a] if kind == 0 else mat_new[a][kind - 1])[None]

    return (total, grad_x[None], *[leaf(kind, name) for kind in range(4) for name in order])
```

```python
import jax
import jax.numpy as jnp
from jax import lax
from jax.experimental import pallas as pl
from jax.experimental.pallas import tpu as pltpu

F32, BF16 = jnp.float32, jnp.bfloat16

D_MODEL = 1024
ATTN_W = 512
CONV_W = 256
XATTN_W = 256
PROJ_W = 3 * ATTN_W + 3 * CONV_W + XATTN_W
D_FF = 4096
HEAD = 64
N_BACK = 128
DILATIONS = (1, 4, 16)
ROPE_THETA = 10000.0
EPS = 1e-6
NEG_INF = -1e30
SCALE = HEAD ** -0.5
N_CHIPS = 4
SHARD_IN = PROJ_W // N_CHIPS
SHARD_FF = D_FF // N_CHIPS

ADAM_LR, ADAM_B1, ADAM_B2, ADAM_EPS, ADAM_WD, ADAM_STEP = 0.001, 0.9, 0.999, 1e-08, 0.01, 10

VMEM_LIMIT_V7X = 56 * 1024 * 1024
ROW_TILE = 256
SMALL_ROWS = 16

NT = (((1,), (1,)), ((), ()))
TN = (((0,), (0,)), ((), ()))
MESH = pl.DeviceIdType.MESH


def _params(*sem):
    return pltpu.CompilerParams(dimension_semantics=sem, vmem_limit_bytes=VMEM_LIMIT_V7X)


def _resident(shape):
    return pl.BlockSpec(shape, lambda *_: (0,) * len(shape), pipeline_mode=pl.Buffered(1))


def _rows(tm, width):
    return pl.BlockSpec((tm, width), lambda i: (i, 0))


def _rms_hat(x):
    r = lax.rsqrt(jnp.mean(x * x, axis=-1, keepdims=True) + EPS)
    return x * r, r


def _rms_bwd(xhat, r, g, dy):
    gdy = dy * g
    return r * (gdy - xhat * jnp.mean(xhat * gdy, axis=-1, keepdims=True))


def _rope128(t, cos, sin_signed, inverse):
    lane = lax.broadcasted_iota(jnp.int32, t.shape, 1)
    first_half = (lane % HEAD) < (HEAD // 2)
    rot = jnp.where(first_half, pltpu.roll(t, 128 - HEAD // 2, 1), pltpu.roll(t, HEAD // 2, 1))
    return t * cos - rot * sin_signed if inverse else t * cos + rot * sin_signed


def _in_proj_fwd(x, g, w_in, cos, sin, tm):
    S = x.shape[0]

    def body(x_ref, g_ref, w_ref, cos_ref, sin_ref, h_ref, q_ref, k_ref, v_ref, bcu_ref, qx_ref, proj):
        xhat, _ = _rms_hat(x_ref[...])
        h = (xhat * g_ref[...]).astype(BF16)
        h_ref[...] = h
        for j in range(N_CHIPS):
            proj[:, SHARD_IN * j:SHARD_IN * (j + 1)] = jnp.dot(h, w_ref[j], preferred_element_type=F32)
        c, s = cos_ref[...], sin_ref[...]
        for j in range(ATTN_W // 128):
            lo = 128 * j
            q_ref[:, lo:lo + 128] = _rope128(proj[:, lo:lo + 128], c, s, False) * SCALE
            k_ref[:, lo:lo + 128] = _rope128(proj[:, ATTN_W + lo:ATTN_W + lo + 128], c, s, False)
        v_ref[...] = proj[:, 2 * ATTN_W:3 * ATTN_W]
        bcu_ref[...] = proj[:, 3 * ATTN_W:3 * ATTN_W + 3 * CONV_W]
        qx_ref[...] = proj[:, 3 * ATTN_W + 3 * CONV_W:PROJ_W].astype(BF16)

    return pl.pallas_call(
        body, name="in_proj_fwd", grid=(S // tm,),
        in_specs=[_rows(tm, D_MODEL), _resident((1, D_MODEL)), _resident((N_CHIPS, D_MODEL, SHARD_IN)),
                  _rows(tm, 128), _rows(tm, 128)],
        out_specs=[_rows(tm, D_MODEL), _rows(tm, ATTN_W), _rows(tm, ATTN_W), _rows(tm, ATTN_W),
                   _rows(tm, 3 * CONV_W), _rows(tm, XATTN_W)],
        out_shape=[jax.ShapeDtypeStruct((S, D_MODEL), BF16), jax.ShapeDtypeStruct((S, ATTN_W), F32),
                   jax.ShapeDtypeStruct((S, ATTN_W), F32), jax.ShapeDtypeStruct((S, ATTN_W), F32),
                   jax.ShapeDtypeStruct((S, 3 * CONV_W), F32), jax.ShapeDtypeStruct((S, XATTN_W), BF16)],
        scratch_shapes=[pltpu.VMEM((tm, PROJ_W), F32)],
        compiler_params=_params("parallel"),
    )(x, g, w_in, cos, sin)


def _memkv_fwd(mem, g_mem, w_kv):
    n_mem = mem.shape[0]

    def body(mem_ref, g_ref, w_ref, mn_ref, kv_ref):
        mhat, _ = _rms_hat(mem_ref[...])
        mn = (mhat * g_ref[...]).astype(BF16)
        mn_ref[...] = mn
        kv_ref[...] = jnp.dot(mn, w_ref[...], preferred_element_type=F32).astype(BF16)

    return pl.pallas_call(
        body, name="memkv_fwd",
        out_shape=[jax.ShapeDtypeStruct((n_mem, D_MODEL), BF16), jax.ShapeDtypeStruct((n_mem, 2 * XATTN_W), BF16)],
        compiler_params=pltpu.CompilerParams(vmem_limit_bytes=VMEM_LIMIT_V7X),
    )(mem, g_mem, w_kv)


def _fill_band_bias(bias):
    row = lax.broadcasted_iota(jnp.int32, (N_BACK, 2 * N_BACK), 0)
    col = lax.broadcasted_iota(jnp.int32, (N_BACK, 2 * N_BACK), 1)
    band = (col >= row) & (col <= row + N_BACK)
    bias[1] = jnp.where(band, 0.0, NEG_INF)
    bias[0] = jnp.where(band & (col >= N_BACK), 0.0, NEG_INF)


def _strided(start, size, d):
    return pl.ds(start, size) if d == 1 else pl.ds(start, size, stride=d)


def _block_starts(t, nb, d):
    r, n = lax.shift_right_logical(t, nb.bit_length() - 1), lax.bitwise_and(t, nb - 1)
    own = r + n * (N_BACK * d)
    prev = r + jnp.maximum(n - 1, 0) * (N_BACK * d)
    if d == 1:
        own, prev = pl.multiple_of(own, N_BACK), pl.multiple_of(prev, N_BACK)
    return own, prev, n


def _by_head(a, b):
    lane = lax.broadcasted_iota(jnp.int32, (a.shape[0], 2 * HEAD), 1)
    return jnp.where(lane < HEAD, a, b)


def _head_only(t, hh):
    lane = lax.broadcasted_iota(jnp.int32, t.shape, 1)
    return jnp.where((lane < HEAD) == (hh == 0), t, jnp.zeros_like(t))


def _stack_heads(t):
    return jnp.concatenate([_head_only(t, 0), _head_only(t, 1)], axis=0)


def _head_columns(t):
    return jnp.concatenate([t[:, 0:1], t[:, HEAD:HEAD + 1]], axis=0)


def _unstack(t):
    return _by_head(t[:N_BACK], t[N_BACK:])


def _unstack_columns(t):
    return _by_head(jnp.broadcast_to(t[:N_BACK], (N_BACK, 2 * HEAD)), jnp.broadcast_to(t[N_BACK:], (N_BACK, 2 * HEAD)))


FWD_BLOCKS_PER_STEP = 4
BWD_BLOCKS_PER_STEP = 2


def _attn_fwd(q, k, v):
    S = q.shape[0]
    U = FWD_BLOCKS_PER_STEP

    def body(q_ref, k_ref, v_ref, y_ref, m_ref, l_scr, bias):
        _fill_band_bias(bias)
        for g, d in enumerate(DILATIONS):
            nb = S // d // N_BACK
            first_pattern, last_pattern = g == 0, g == len(DILATIONS) - 1

            def step(i, carry, d=d, nb=nb, first_pattern=first_pattern, last_pattern=last_pattern):
                blocks = [_block_starts(U * i + u, nb, d) for u in range(U)]
                rows = [_strided(own, N_BACK, d) for own, _, _ in blocks]
                prev_rows = [_strided(prev, N_BACK, d) for _, prev, _ in blocks]
                ss = []
                for u, (_, _, n) in enumerate(blocks):
                    kw = jnp.concatenate([k_ref[prev_rows[u], :], k_ref[rows[u], :]], 0).astype(BF16)
                    qs = _stack_heads(q_ref[rows[u], :].astype(BF16))
                    b = bias[jnp.minimum(n, 1)]
                    ss.append(lax.dot_general(qs, kw, NT, preferred_element_type=F32) + jnp.concatenate([b, b], axis=0))
                ms = [jnp.max(s, axis=1, keepdims=True) for s in ss]
                ps = [jnp.exp(s - m) for s, m in zip(ss, ms)]
                ls = [jnp.sum(p, axis=1, keepdims=True) for p in ps]
                os_ = []
                for u in range(U):
                    vw = jnp.concatenate([v_ref[prev_rows[u], :], v_ref[rows[u], :]], 0).astype(BF16)
                    os_.append(jnp.dot(ps[u].astype(BF16), vw, preferred_element_type=F32))
                for u in range(U):
                    o_g, m_g, l_g = _unstack(os_[u]), _unstack_columns(ms[u]), _unstack_columns(ls[u])
                    r = rows[u]
                    if first_pattern:
                        m_new, l_new, acc = m_g, l_g, o_g
                    else:
                        m_old = m_ref[r, :]
                        m_new = jnp.maximum(m_old, m_g)
                        alpha, beta = jnp.exp(m_old - m_new), jnp.exp(m_g - m_new)
                        l_new = l_scr[r, :] * alpha + l_g * beta
                        acc = y_ref[r, :] * alpha + o_g * beta
                    if last_pattern:
                        y_ref[r, :] = acc / l_new
                        m_ref[r, :] = m_new + jnp.log(l_new)
                    else:
                        y_ref[r, :] = acc
                        m_ref[r, :] = m_new
                        l_scr[r, :] = l_new
                return carry

            lax.fori_loop(0, d * nb // U, step, 0)

    col = pl.BlockSpec((S, 2 * HEAD), lambda j: (0, j))
    return pl.pallas_call(
        body, name="attn_fwd", grid=(q.shape[1] // (2 * HEAD),),
        in_specs=[col, col, col], out_specs=[col, col],
        out_shape=[jax.ShapeDtypeStruct(q.shape, F32)] * 2,
        scratch_shapes=[pltpu.VMEM((S, 2 * HEAD), F32), pltpu.VMEM((2, N_BACK, 2 * N_BACK), F32)],
        compiler_params=_params("parallel"),
    )(q, k, v)


def _attn_bwd(q, k, v, dy, lse, delta, after):
    S = q.shape[0]
    U = BWD_BLOCKS_PER_STEP

    def body(q_ref, k_ref, v_ref, dy_ref, lse_ref, delta_ref, after_ref, dq_ref, dk_ref, dv_ref, bias):
        _fill_band_bias(bias)
        dk_ref[...] = jnp.zeros_like(dk_ref)
        dv_ref[...] = jnp.zeros_like(dv_ref)
        for g, d in enumerate(DILATIONS):
            nb = S // d // N_BACK

            def step(i, carry, d=d, nb=nb, g=g):
                blocks = [_block_starts(U * i + u, nb, d) for u in range(U)]
                rows = [_strided(own, N_BACK, d) for own, _, _ in blocks]
                prev_rows = [_strided(prev, N_BACK, d) for _, prev, _ in blocks]
                kws = [jnp.concatenate([k_ref[prev_rows[u], :], k_ref[rows[u], :]], 0).astype(BF16) for u in range(U)]
                vws = [jnp.concatenate([v_ref[prev_rows[u], :], v_ref[rows[u], :]], 0).astype(BF16) for u in range(U)]
                qss = [_stack_heads(q_ref[rows[u], :].astype(BF16)) for u in range(U)]
                doss = [_stack_heads(dy_ref[rows[u], :].astype(BF16)) for u in range(U)]
                ss, dps = [], []
                for u, (_, _, n) in enumerate(blocks):
                    b = bias[jnp.minimum(n, 1)]
                    ss.append(lax.dot_general(qss[u], kws[u], NT, preferred_element_type=F32) + jnp.concatenate([b, b], axis=0))
                    dps.append(lax.dot_general(doss[u], vws[u], NT, preferred_element_type=F32))
                ps = [jnp.exp(ss[u] - _head_columns(lse_ref[rows[u], :])) for u in range(U)]
                dss = [(ps[u] * (dps[u] - _head_columns(delta_ref[rows[u], :]))).astype(BF16) for u in range(U)]
                pbs = [p.astype(BF16) for p in ps]
                dqs = [jnp.dot(dss[u], kws[u], preferred_element_type=F32) for u in range(U)]
                dkws = [lax.dot_general(dss[u], qss[u], TN, preferred_element_type=F32) for u in range(U)]
                dvws = [lax.dot_general(pbs[u], doss[u], TN, preferred_element_type=F32) for u in range(U)]
                for u in range(U):
                    dq = _unstack(dqs[u])
                    if g == 0:
                        dq_ref[rows[u], :] = dq
                    else:
                        dq_ref[rows[u], :] += dq
                    dk_ref[prev_rows[u], :] += dkws[u][:N_BACK]
                    dv_ref[prev_rows[u], :] += dvws[u][:N_BACK]
                    dk_ref[rows[u], :] += dkws[u][N_BACK:]
                    dv_ref[rows[u], :] += dvws[u][N_BACK:]
                return carry

            lax.fori_loop(0, d * nb // U, step, 0)

    col = pl.BlockSpec((S, 2 * HEAD), lambda j: (0, j))
    return pl.pallas_call(
        body, name="attn_bwd", grid=(q.shape[1] // (2 * HEAD),),
        in_specs=[col] * 6 + [pl.BlockSpec(memory_space=pl.ANY)], out_specs=[col] * 3,
        out_shape=[jax.ShapeDtypeStruct(q.shape, F32)] * 3,
        scratch_shapes=[pltpu.VMEM((2, N_BACK, 2 * N_BACK), F32)],
        compiler_params=_params("parallel"),
    )(q, k, v, dy, lse, delta, after)


def _shift_down(z, before, k):
    row = lax.broadcasted_iota(jnp.int32, z.shape, 0)
    out = pltpu.roll(z, k, 0)
    for i in range(k):
        out = jnp.where(row == i, before[8 - k + i:8 - k + i + 1, :], out)
    return out


def _shift_up(z, after, k):
    rows = z.shape[0]
    row = lax.broadcasted_iota(jnp.int32, z.shape, 0)
    out = pltpu.roll(z, rows - k, 0)
    for i in range(k):
        out = jnp.where(row == rows - k + i, after[i:i + 1, :], out)
    return out


def _conv_fwd(bcu, before, is_first, w):
    b, c, u = bcu[:, 0:CONV_W], bcu[:, CONV_W:2 * CONV_W], bcu[:, 2 * CONV_W:3 * CONV_W]
    z = c * u
    zb = jnp.where(is_first, 0.0, before[:, CONV_W:2 * CONV_W] * before[:, 2 * CONV_W:3 * CONV_W])
    z1, z2 = _shift_down(z, zb, 1), _shift_down(z, zb, 2)
    cv = w[0:1, :] * z2 + w[1:2, :] * z1 + w[2:3, :] * z
    return b, c, u, z, z1, z2, cv


def _halo_before(tm, width):
    return pl.BlockSpec((8, width), lambda i: (jnp.maximum(i * (tm // 8) - 1, 0), 0))


def _halo_after(tm, width, S):
    return pl.BlockSpec((8, width), lambda i: (jnp.minimum((i + 1) * (tm // 8), S // 8 - 1), 0))


def _mix_fwd(ya, bcu, qx, mkv, conv_w, g_a, g_c, g_x, w_out, g_post, x, tm):
    S = x.shape[0]

    def body(ya_ref, bcu_ref, before_ref, qx_ref, mkv_ref, cw_ref, ga_ref, gc_ref, gx_ref,
             wo_ref, gp_ref, x_ref, yx_ref, ycat_ref, y2_ref, x1_ref):
        ya = ya_ref[...]
        b, _, _, _, _, _, cv = _conv_fwd(bcu_ref[...], before_ref[...], pl.program_id(0) == 0, cw_ref[...])
        yc = b * cv

        qxb, mkvb = qx_ref[...], mkv_ref[...]
        for hd in range(XATTN_W // HEAD):
            sl = slice(HEAD * hd, HEAD * (hd + 1))
            s = lax.dot_general(qxb[:, sl], mkvb[:, sl], NT, preferred_element_type=F32) * SCALE
            mx = jnp.max(s, axis=1, keepdims=True)
            p = jnp.exp(s - mx)
            l = jnp.sum(p, axis=1, keepdims=True)
            vm = mkvb[:, XATTN_W + HEAD * hd:XATTN_W + HEAD * (hd + 1)]
            yx_ref[:, sl] = jnp.dot(p.astype(BF16), vm, preferred_element_type=F32) / l
        yx = yx_ref[...]

        ycat_ref[:, 0:ATTN_W] = (_rms_hat(ya)[0] * ga_ref[...]).astype(BF16)
        ycat_ref[:, ATTN_W:ATTN_W + CONV_W] = (_rms_hat(yc)[0] * gc_ref[...]).astype(BF16)
        ycat_ref[:, ATTN_W + CONV_W:D_MODEL] = (_rms_hat(yx)[0] * gx_ref[...]).astype(BF16)
        y2 = jnp.dot(ycat_ref[...], wo_ref[...], preferred_element_type=F32)
        y2_ref[...] = y2
        x1_ref[...] = x_ref[...] + _rms_hat(y2)[0] * gp_ref[...]

    n_mem = mkv.shape[0]
    return pl.pallas_call(
        body, name="mix_fwd", grid=(S // tm,),
        in_specs=[_rows(tm, ATTN_W), _rows(tm, 3 * CONV_W), _halo_before(tm, 3 * CONV_W), _rows(tm, XATTN_W),
                  _resident((n_mem, 2 * XATTN_W)), _resident((3, CONV_W)), _resident((1, ATTN_W)),
                  _resident((1, CONV_W)), _resident((1, XATTN_W)), _resident((D_MODEL, D_MODEL)),
                  _resident((1, D_MODEL)), _rows(tm, D_MODEL)],
        out_specs=[_rows(tm, XATTN_W), _rows(tm, D_MODEL), _rows(tm, D_MODEL), _rows(tm, D_MODEL)],
        out_shape=[jax.ShapeDtypeStruct((S, XATTN_W), F32), jax.ShapeDtypeStruct((S, D_MODEL), BF16),
                   jax.ShapeDtypeStruct((S, D_MODEL), F32), jax.ShapeDtypeStruct((S, D_MODEL), F32)],
        compiler_params=_params("parallel"),
    )(ya, bcu, bcu, qx, mkv, conv_w, g_a, g_c, g_x, w_out, g_post, x)


def _mlp_fwd_bwd(x1, target, g_pre, g_post, w_up, w_down, tm):
    S = x1.shape[0]
    n_ff = D_FF // SHARD_FF

    def body(x1_ref, t_ref, gpre_ref, gpost_ref, wup_ref, wdn_ref,
             h2_ref, f_ref, du_ref, df2_ref, dx1_ref, dgpre_ref, dgpost_ref, loss_ref, u_scr):
        @pl.when(pl.program_id(0) == 0)
        def _():
            dgpre_ref[...] = jnp.zeros_like(dgpre_ref)
            dgpost_ref[...] = jnp.zeros_like(dgpost_ref)
            loss_ref[...] = jnp.zeros_like(loss_ref)

        x1 = x1_ref[...]
        x1hat, r1 = _rms_hat(x1)
        h2 = (x1hat * gpre_ref[...]).astype(BF16)
        h2_ref[...] = h2
        f2 = jnp.zeros((tm, D_MODEL), F32)
        for j in range(n_ff):
            cols = slice(SHARD_FF * j, SHARD_FF * (j + 1))
            u = jnp.maximum(jnp.dot(h2, wup_ref[j], preferred_element_type=F32), 0.0)
            u_scr[:, cols] = u
            f = (u * u).astype(BF16)
            f_ref[:, cols] = f
            f2 = f2 + jnp.dot(f, wdn_ref[cols, :], preferred_element_type=F32)
        f2hat, r2 = _rms_hat(f2)
        err = x1 + f2hat * gpost_ref[...] - t_ref[...]
        loss_ref[...] += 0.5 * jnp.sum(jnp.mean(err * err, axis=-1, keepdims=True), axis=0, keepdims=True)
        dx2 = err * (1.0 / D_MODEL)
        dgpost_ref[...] += jnp.sum(dx2 * f2hat, axis=0, keepdims=True)
        df2 = _rms_bwd(f2hat, r2, gpost_ref[...], dx2).astype(BF16)
        df2_ref[...] = df2
        dh2 = jnp.zeros((tm, D_MODEL), F32)
        for j in range(n_ff):
            cols = slice(SHARD_FF * j, SHARD_FF * (j + 1))
            df = lax.dot_general(df2, wdn_ref[cols, :], NT, preferred_element_type=F32)
            du = (2.0 * u_scr[:, cols] * df).astype(BF16)
            du_ref[:, cols] = du
            dh2 = dh2 + lax.dot_general(du, wup_ref[j], NT, preferred_element_type=F32)
        dgpre_ref[...] += jnp.sum(dh2 * x1hat, axis=0, keepdims=True)
        dx1_ref[...] = dx2 + _rms_bwd(x1hat, r1, gpre_ref[...], dh2)

    acc = pl.BlockSpec((1, D_MODEL), lambda i: (0, 0))
    return pl.pallas_call(
        body, name="mlp_fwd_bwd", grid=(S // tm,),
        in_specs=[_rows(tm, D_MODEL), _rows(tm, D_MODEL), _resident((1, D_MODEL)), _resident((1, D_MODEL)),
                  _resident((n_ff, D_MODEL, SHARD_FF)), _resident((D_FF, D_MODEL))],
        out_specs=[_rows(tm, D_MODEL), _rows(tm, D_FF), _rows(tm, D_FF), _rows(tm, D_MODEL), _rows(tm, D_MODEL),
                   acc, acc, pl.BlockSpec((1, 1), lambda i: (0, 0))],
        out_shape=[jax.ShapeDtypeStruct((S, D_MODEL), BF16), jax.ShapeDtypeStruct((S, D_FF), BF16),
                   jax.ShapeDtypeStruct((S, D_FF), BF16), jax.ShapeDtypeStruct((S, D_MODEL), BF16),
                   jax.ShapeDtypeStruct((S, D_MODEL), F32), jax.ShapeDtypeStruct((1, D_MODEL), F32),
                   jax.ShapeDtypeStruct((1, D_MODEL), F32), jax.ShapeDtypeStruct((1, 1), F32)],
        scratch_shapes=[pltpu.VMEM((tm, D_FF), F32)],
        compiler_params=_params("arbitrary"),
    )(x1, target, g_pre, g_post, w_up, w_down)


def _weight_grad(name, a, b, rows_sharded):
    S, K = a.shape
    N = b.shape[1]
    if rows_sharded:
        tk, tn = K // N_CHIPS, N
        a_spec = pl.BlockSpec((S, tk), lambda j: (0, j))
        b_spec = pl.BlockSpec((S, tn), lambda j: (0, 0), pipeline_mode=pl.Buffered(1))
    else:
        tk, tn = K, N // N_CHIPS
        a_spec = pl.BlockSpec((S, tk), lambda j: (0, 0), pipeline_mode=pl.Buffered(1))
        b_spec = pl.BlockSpec((S, tn), lambda j: (0, j))
    half = tk // 2

    def body(a_ref, b_ref, o_ref):
        res = lax.dot_general(a_ref[...], b_ref[...], TN, preferred_element_type=F32)
        o_ref[0, 0] = res[:half]
        o_ref[1, 0] = res[half:]

    return pl.pallas_call(
        body, name=name, grid=(N_CHIPS,), in_specs=[a_spec, b_spec],
        out_specs=pl.BlockSpec((2, 1, half, tn), lambda j: (0, j, 0, 0)),
        out_shape=jax.ShapeDtypeStruct((2, N_CHIPS, half, tn), F32),
        compiler_params=_params("parallel"),
    )(a, b)


def _mix_bwd(dx1, y2, ya, yx, bcu, conv_w, g_a, g_c, g_x, w_out, g_post, tm):
    S = dx1.shape[0]

    def body(dx1_ref, y2_ref, ya_ref, yx_ref, bcu_ref, before_ref, cw_ref, ga_ref, gc_ref, gx_ref, wo_ref, gp_ref,
             dy2_ref, dya_ref, delta_ref, dycx_ref, dgp_ref, dga_ref, dgc_ref, dgx_ref):
        @pl.when(pl.program_id(0) == 0)
        def _():
            for ref in (dgp_ref, dga_ref, dgc_ref, dgx_ref):
                ref[...] = jnp.zeros_like(ref)

        dx1 = dx1_ref[...]
        y2hat, r2 = _rms_hat(y2_ref[...])
        dgp_ref[...] += jnp.sum(dx1 * y2hat, axis=0, keepdims=True)
        dy2 = _rms_bwd(y2hat, r2, gp_ref[...], dx1).astype(BF16)
        dy2_ref[...] = dy2
        dycat = lax.dot_general(dy2, wo_ref[...], NT, preferred_element_type=F32)

        d_na = dycat[:, 0:ATTN_W]
        ya = ya_ref[...]
        yahat, ra = _rms_hat(ya)
        dga_ref[...] += jnp.sum(d_na * yahat, axis=0, keepdims=True)
        dya = _rms_bwd(yahat, ra, ga_ref[...], d_na)
        dya_ref[...] = dya
        prod = dya * ya
        hi = prod.astype(BF16)
        lo = (prod - hi.astype(F32)).astype(BF16)
        head_of = lambda axis: lax.shift_right_logical(lax.broadcasted_iota(jnp.int32, (ATTN_W, ATTN_W), axis),
                                                       HEAD.bit_length() - 1)
        same_head = head_of(0) == head_of(1)
        ones = jnp.where(same_head, 1.0, 0.0).astype(BF16)
        delta_ref[...] = (jnp.dot(hi, ones, preferred_element_type=F32) + jnp.dot(lo, ones, preferred_element_type=F32))

        b, _, _, _, _, _, cv = _conv_fwd(bcu_ref[...], before_ref[...], pl.program_id(0) == 0, cw_ref[...])
        d_nc = dycat[:, ATTN_W:ATTN_W + CONV_W]
        ychat, rc = _rms_hat(b * cv)
        dgc_ref[...] += jnp.sum(d_nc * ychat, axis=0, keepdims=True)
        dycx_ref[:, 0:CONV_W] = _rms_bwd(ychat, rc, gc_ref[...], d_nc)

        d_nx = dycat[:, ATTN_W + CONV_W:D_MODEL]
        yxhat, rx = _rms_hat(yx_ref[...])
        dgx_ref[...] += jnp.sum(d_nx * yxhat, axis=0, keepdims=True)
        dycx_ref[:, CONV_W:CONV_W + XATTN_W] = _rms_bwd(yxhat, rx, gx_ref[...], d_nx)

    acc = lambda w: pl.BlockSpec((1, w), lambda i: (0, 0))
    return pl.pallas_call(
        body, name="mix_bwd", grid=(S // tm,),
        in_specs=[_rows(tm, D_MODEL), _rows(tm, D_MODEL), _rows(tm, ATTN_W), _rows(tm, XATTN_W),
                  _rows(tm, 3 * CONV_W), _halo_before(tm, 3 * CONV_W), _resident((3, CONV_W)),
                  _resident((1, ATTN_W)), _resident((1, CONV_W)), _resident((1, XATTN_W)),
                  _resident((D_MODEL, D_MODEL)), _resident((1, D_MODEL))],
        out_specs=[_rows(tm, D_MODEL), _rows(tm, ATTN_W), _rows(tm, ATTN_W), _rows(tm, CONV_W + XATTN_W),
                   acc(D_MODEL), acc(ATTN_W), acc(CONV_W), acc(XATTN_W)],
        out_shape=[jax.ShapeDtypeStruct((S, D_MODEL), BF16), jax.ShapeDtypeStruct((S, ATTN_W), F32),
                   jax.ShapeDtypeStruct((S, ATTN_W), F32),
                   jax.ShapeDtypeStruct((S, CONV_W + XATTN_W), F32), jax.ShapeDtypeStruct((1, D_MODEL), F32),
                   jax.ShapeDtypeStruct((1, ATTN_W), F32), jax.ShapeDtypeStruct((1, CONV_W), F32),
                   jax.ShapeDtypeStruct((1, XATTN_W), F32)],
        compiler_params=_params("arbitrary"),
    )(dx1, y2, ya, yx, bcu, bcu, conv_w, g_a, g_c, g_x, w_out, g_post)


def _conv_xattn_bwd(dycx, bcu, qx, mkv, conv_w, tm):
    S = dycx.shape[0]
    n_mem = mkv.shape[0]
    n_tiles = S // tm

    def body(d_ref, dafter_ref, bcu_ref, before_ref, after_ref, qx_ref, mkv_ref, cw_ref,
             tail_ref, dmkv_ref, dcw_ref):
        i = pl.program_id(0)

        @pl.when(i == 0)
        def _():
            dmkv_ref[...] = jnp.zeros_like(dmkv_ref)
            dcw_ref[...] = jnp.zeros_like(dcw_ref)

        w = cw_ref[...]
        b, c, u, z, z1, z2, cv = _conv_fwd(bcu_ref[...], before_ref[...], i == 0, w)
        dyc = d_ref[:, 0:CONV_W]
        dcv = dyc * b
        dcv_after = jnp.where(i == n_tiles - 1, 0.0, dafter_ref[:, 0:CONV_W] * after_ref[:, 0:CONV_W])
        dz = w[2:3, :] * dcv + w[1:2, :] * _shift_up(dcv, dcv_after, 1) + w[0:1, :] * _shift_up(dcv, dcv_after, 2)
        dcw_ref[0:1, :] += jnp.sum(dcv * z2, axis=0, keepdims=True)
        dcw_ref[1:2, :] += jnp.sum(dcv * z1, axis=0, keepdims=True)
        dcw_ref[2:3, :] += jnp.sum(dcv * z, axis=0, keepdims=True)
        tail_ref[:, 0:CONV_W] = (dyc * cv).astype(BF16)
        tail_ref[:, CONV_W:2 * CONV_W] = (dz * u).astype(BF16)
        tail_ref[:, 2 * CONV_W:3 * CONV_W] = (dz * c).astype(BF16)

        qxb, mkvb = qx_ref[...], mkv_ref[...]
        for hd in range(XATTN_W // HEAD):
            sl = slice(HEAD * hd, HEAD * (hd + 1))
            vsl = slice(XATTN_W + HEAD * hd, XATTN_W + HEAD * (hd + 1))
            s = lax.dot_general(qxb[:, sl], mkvb[:, sl], NT, preferred_element_type=F32) * SCALE
            e = jnp.exp(s - jnp.max(s, axis=1, keepdims=True))
            p = e / jnp.sum(e, axis=1, keepdims=True)
            dob = d_ref[:, CONV_W + HEAD * hd:CONV_W + HEAD * (hd + 1)].astype(BF16)
            dp = lax.dot_general(dob, mkvb[:, vsl], NT, preferred_element_type=F32)
            ds = (p * (dp - jnp.sum(p * dp, axis=1, keepdims=True)) * SCALE).astype(BF16)
            tail_ref[:, 3 * CONV_W + HEAD * hd:3 * CONV_W + HEAD * (hd + 1)] = jnp.dot(
                ds, mkvb[:, sl], preferred_element_type=F32).astype(BF16)
            dmkv_ref[:, sl] += lax.dot_general(ds, qxb[:, sl], TN, preferred_element_type=F32)
            dmkv_ref[:, vsl] += lax.dot_general(p.astype(BF16), dob, TN, preferred_element_type=F32)

    width = CONV_W + XATTN_W
    return pl.pallas_call(
        body, name="conv_xattn_bwd", grid=(n_tiles,),
        in_specs=[_rows(tm, width), _halo_after(tm, width, S), _rows(tm, 3 * CONV_W), _halo_before(tm, 3 * CONV_W),
                  _halo_after(tm, 3 * CONV_W, S), _rows(tm, XATTN_W), _resident((n_mem, 2 * XATTN_W)),
                  _resident((3, CONV_W))],
        out_specs=[_rows(tm, 3 * CONV_W + XATTN_W), pl.BlockSpec((n_mem, 2 * XATTN_W), lambda i: (0, 0)),
                   pl.BlockSpec((3, CONV_W), lambda i: (0, 0))],
        out_shape=[jax.ShapeDtypeStruct((S, 3 * CONV_W + XATTN_W), BF16),
                   jax.ShapeDtypeStruct((n_mem, 2 * XATTN_W), F32), jax.ShapeDtypeStruct((3, CONV_W), F32)],
        compiler_params=_params("arbitrary"),
    )(dycx, dycx, bcu, bcu, bcu, qx, mkv, conv_w)


def _memkv_bwd(mem, g_mem, w_kv, dmkv):
    n_mem = mem.shape[0]
    half = D_MODEL // N_CHIPS // 2

    def body(mem_ref, g_ref, w_ref, d_ref, dw_ref, dg_ref):
        mhat, _ = _rms_hat(mem_ref[...])
        mn = (mhat * g_ref[...]).astype(BF16)
        d = d_ref[...].astype(BF16)
        for k in range(2 * N_CHIPS):
            dw_ref[k % 2, k // 2] = lax.dot_general(mn[:, half * k:half * (k + 1)], d, TN, preferred_element_type=F32)
        dmn = lax.dot_general(d, w_ref[...], NT, preferred_element_type=F32)
        dg_ref[...] = jnp.sum(dmn * mhat, axis=0, keepdims=True)

    return pl.pallas_call(
        body, name="memkv_bwd",
        out_shape=[jax.ShapeDtypeStruct((2, N_CHIPS, half, 2 * XATTN_W), F32), jax.ShapeDtypeStruct((1, D_MODEL), F32)],
        compiler_params=pltpu.CompilerParams(vmem_limit_bytes=VMEM_LIMIT_V7X),
    )(mem, g_mem, w_kv, dmkv)


def _in_proj_bwd(dqkv, tail, cos, sin, w_in, x, g, dx1, tm):
    S = x.shape[0]

    def body(dq_ref, dk_ref, dv_ref, tail_ref, cos_ref, sin_ref, w_ref, x_ref, g_ref, dx1_ref, dproj_ref, dx_ref, dg_ref):
        @pl.when(pl.program_id(0) == 0)
        def _():
            dg_ref[...] = jnp.zeros_like(dg_ref)

        c, s = cos_ref[...], sin_ref[...]
        for j in range(ATTN_W // 128):
            cols = slice(128 * j, 128 * (j + 1))
            dproj_ref[:, cols] = _rope128(dq_ref[:, cols] * SCALE, c, s, True).astype(BF16)
            dproj_ref[:, ATTN_W + 128 * j:ATTN_W + 128 * (j + 1)] = _rope128(dk_ref[:, cols], c, s, True).astype(BF16)
        dproj_ref[:, 2 * ATTN_W:3 * ATTN_W] = dv_ref[...].astype(BF16)
        dproj_ref[:, 3 * ATTN_W:PROJ_W] = tail_ref[...]
        dh = jnp.zeros((tm, D_MODEL), F32)
        for j in range(N_CHIPS):
            dh = dh + lax.dot_general(dproj_ref[:, SHARD_IN * j:SHARD_IN * (j + 1)], w_ref[j], NT,
                                      preferred_element_type=F32)
        xhat, r = _rms_hat(x_ref[...])
        dg_ref[...] += jnp.sum(dh * xhat, axis=0, keepdims=True)
        dx_ref[...] = dx1_ref[...] + _rms_bwd(xhat, r, g_ref[...], dh)

    return pl.pallas_call(
        body, name="in_proj_bwd", grid=(S // tm,),
        in_specs=[_rows(tm, ATTN_W)] * 3 + [_rows(tm, PROJ_W - 3 * ATTN_W), _rows(tm, 128), _rows(tm, 128),
                  _resident((N_CHIPS, D_MODEL, SHARD_IN)), _rows(tm, D_MODEL), _resident((1, D_MODEL)),
                  _rows(tm, D_MODEL)],
        out_specs=[_rows(tm, PROJ_W), _rows(tm, D_MODEL), pl.BlockSpec((1, D_MODEL), lambda i: (0, 0))],
        out_shape=[jax.ShapeDtypeStruct((S, PROJ_W), BF16), jax.ShapeDtypeStruct((S, D_MODEL), F32),
                   jax.ShapeDtypeStruct((1, D_MODEL), F32)],
        compiler_params=_params("arbitrary"),
    )(*dqkv, tail, cos, sin, w_in, x, g, dx1)


def _row_tile(rows):
    return ROW_TILE if rows % ROW_TILE == 0 else rows


def _chip_sum_bf16(name, grad, from_sibling, place):
    _, n, rows, cols = grad.shape
    tr = _row_tile(rows)

    def body(place_ref, g_ref, b_ref, o_ref):
        o_ref[...] = (g_ref[0] + b_ref[...]).astype(BF16)

    spec = pl.BlockSpec((1, tr, cols), lambda s, i, p: (s, i, 0))
    return pl.pallas_call(
        body, name=name, out_shape=jax.ShapeDtypeStruct((n, rows, cols), BF16),
        grid_spec=pltpu.PrefetchScalarGridSpec(
            num_scalar_prefetch=1, grid=(n, rows // tr),
            in_specs=[pl.BlockSpec((1, 1, tr, cols), lambda s, i, p: (p[0], s, i, 0)), spec], out_specs=spec),
        compiler_params=_params("parallel", "parallel"),
    )(place, grad, from_sibling)


def _final_sum(name, grad, from_sibling, others, place):
    _, _, rows, cols = grad.shape
    tr = _row_tile(rows)

    def body(place_ref, own_ref, sib_ref, o0, o1, o2, out_ref):
        acc = own_ref[0, 0] + sib_ref[0]
        for o in (o0, o1, o2):
            acc = acc + o[0].astype(F32)
        out_ref[...] = acc

    other = lambda k: pl.BlockSpec((1, tr, cols), lambda i, p: (k, i, 0))
    return pl.pallas_call(
        body, name=name, out_shape=jax.ShapeDtypeStruct((rows, cols), F32),
        grid_spec=pltpu.PrefetchScalarGridSpec(
            num_scalar_prefetch=1, grid=(rows // tr,),
            in_specs=[pl.BlockSpec((1, 1, tr, cols), lambda i, p: (p[0], p[1], i, 0)),
                      pl.BlockSpec((1, tr, cols), lambda i, p: (p[1], i, 0)), other(0), other(1), other(2)],
            out_specs=pl.BlockSpec((tr, cols), lambda i, p: (i, 0))),
        compiler_params=_params("parallel"),
    )(place, grad, from_sibling, others, others, others)


def _adamw_update(w, g, m, v):
    m = ADAM_B1 * m + (1.0 - ADAM_B1) * g
    v = ADAM_B2 * v + (1.0 - ADAM_B2) * (g * g)
    m_hat = m * (1.0 / (1.0 - ADAM_B1 ** ADAM_STEP))
    v_hat = v * (1.0 / (1.0 - ADAM_B2 ** ADAM_STEP))
    return -ADAM_LR * (m_hat / (jnp.sqrt(v_hat) + ADAM_EPS) + ADAM_WD * w), m, v


def _adamw(name, w, g, m, v, after):
    rows, cols = w.shape
    tr = _row_tile(rows)

    def body(w_ref, g_ref, m_ref, v_ref, after_ref, d_ref, nm_ref, nv_ref):
        d_ref[...], nm_ref[...], nv_ref[...] = _adamw_update(w_ref[...], g_ref[...], m_ref[...], v_ref[...])

    spec = pl.BlockSpec((tr, cols), lambda i: (i, 0))
    return pl.pallas_call(
        body, name=name, grid=(rows // tr,), in_specs=[spec] * 4 + [pl.BlockSpec(memory_space=pl.ANY)],
        out_specs=[spec] * 3, out_shape=[jax.ShapeDtypeStruct(w.shape, F32)] * 3, compiler_params=_params("parallel"),
    )(w, g, m, v, after)


def _small_update(summed, chip, gains, gains_m, gains_v, taps, taps_m, taps_v):
    n = len(gains)
    widths = [g.shape[1] for g in gains]
    k, w = taps.shape

    def body(*refs):
        chip_ref, sum_ref = refs[0], refs[1]
        params = [refs[2 + 3 * i:5 + 3 * i] for i in range(n + 1)]
        outs = [refs[2 + 3 * (n + 1) + 4 * i:2 + 3 * (n + 1) + 4 * (i + 1)] for i in range(n + 1)]
        loss_ref = refs[-1]
        for i in range(n):
            g = sum_ref[i:i + 1, 0:widths[i]]
            wr, mr, vr = params[i]
            outs[i][0][...] = g
            outs[i][1][...], outs[i][2][...], outs[i][3][...] = _adamw_update(wr[...], g, mr[...], vr[...])
        g = sum_ref[n:n + k, 0:w]
        for j in range(1, N_CHIPS):
            g = jnp.where(chip_ref[0] == j, sum_ref[n:n + k, w * j:w * (j + 1)], g)
        wr, mr, vr = params[n]
        outs[n][0][...] = g
        outs[n][1][...], outs[n][2][...], outs[n][3][...] = _adamw_update(wr[...], g, mr[...], vr[...])
        loss_ref[...] = sum_ref[n + k:n + k + 1, 0:1]

    vmem = pl.BlockSpec(memory_space=pltpu.VMEM)
    operands = [chip, summed]
    for p in zip(list(gains) + [taps], list(gains_m) + [taps_m], list(gains_v) + [taps_v]):
        operands += list(p)
    shapes = [jax.ShapeDtypeStruct(p.shape, F32) for p in list(gains) + [taps] for _ in range(4)]
    out = pl.pallas_call(
        body, name="small_update", out_shape=shapes + [jax.ShapeDtypeStruct((1, 1), F32)],
        in_specs=[pl.BlockSpec(memory_space=pltpu.SMEM)] + [vmem] * (len(operands) - 1),
        out_specs=[vmem] * (len(shapes) + 1),
    )(*operands)
    return [out[4 * i:4 * (i + 1)] for i in range(n + 1)], out[-1]


def _sum_blocks(name, blocks):
    n, rows, cols = blocks.shape

    def body(b_ref, o_ref):
        acc = b_ref[0]
        for k in range(1, n):
            acc = acc + b_ref[k]
        o_ref[...] = acc

    return pl.pallas_call(body, name=name, out_shape=jax.ShapeDtypeStruct((rows, cols), F32))(blocks)


def _place():
    return lax.axis_index("x"), lax.axis_index("y"), lax.axis_index("c")


def _other_chips(x, y):
    return [(1 - x, y), (x, 1 - y), (1 - x, 1 - y)]


def _weights_allgather(name, shards, landed=None):
    n = len(shards)
    first_hop = landed is None

    def body(*refs):
        ins, outs, stage = refs[:n], refs[-3 - 2 * n:-3 - n], refs[-3 - n:-3]
        send_sems, recv_sems, local_sems = refs[-3:]
        x, y, c = _place()
        me, sibling = (x, y, c), (x, y, 1 - c)
        chips = _other_chips(x, y)
        chip_index = lambda chip: 2 * chip[0] + chip[1]

        def copy(a, k, chip, half, to, src=None):
            place = outs[a].at[chip_index(chip), half]
            return pltpu.make_async_remote_copy(
                src_ref=place if src is None else src, dst_ref=place, send_sem=send_sems.at[6 * a + k],
                recv_sem=recv_sems.at[6 * a + k], device_id=to, device_id_type=MESH)

        load = [pltpu.make_async_copy(ins[a], stage[a], local_sems.at[a]) for a in range(n)]
        local = [pltpu.make_async_copy(stage[a], outs[a].at[chip_index((x, y))], local_sems.at[a]) for a in range(n)]
        for cp in load:
            cp.start()
        first = []
        if first_hop:
            first = [copy(a, k, (x, y), c, (*chip, c), src=ins[a].at[c]) for a in range(n) for k, chip in enumerate(chips)]
        for cp in first:
            cp.start()
        for a in range(n):
            load[a].wait()
            local[a].start()
        passed = []
        for a in range(n):
            for k, chip in enumerate(chips):
                if first_hop:
                    copy(a, k, chip, c, me).wait_recv()
                passed.append(copy(a, 3 + k, chip, c, sibling))
                passed[-1].start()
        for a in range(n):
            for k, chip in enumerate(chips):
                copy(a, 3 + k, chip, 1 - c, me).wait_recv()
        for cp in first + passed:
            cp.wait_send()
        for cp in local:
            cp.wait()

    any_spec = pl.BlockSpec(memory_space=pl.ANY)
    operands = list(shards) + ([] if first_hop else list(landed))
    return pl.pallas_call(
        body, name=name,
        out_shape=[jax.ShapeDtypeStruct((N_CHIPS,) + s.shape, s.dtype) for s in shards],
        in_specs=[any_spec] * len(operands), out_specs=[any_spec] * n,
        input_output_aliases={} if first_hop else {n + a: a for a in range(n)},
        scratch_shapes=[pltpu.VMEM(s.shape, s.dtype) for s in shards]
        + [pltpu.SemaphoreType.DMA((6 * n,)), pltpu.SemaphoreType.DMA((6 * n,)), pltpu.SemaphoreType.DMA((n,))],
        compiler_params=pltpu.CompilerParams(vmem_limit_bytes=VMEM_LIMIT_V7X),
    )(*operands)


def _plan_first_hop(x, y, c, shards, lands):
    return [(shards[a].at[c], lands[a].at[2 * x + y, c], lands[a].at[2 * chip[0] + chip[1], c], (*chip, c))
            for a in range(len(shards)) for chip in _other_chips(x, y)]


def _plan_other_half_to_sibling(x, y, c, grads, lands):
    return [(grads[a].at[1 - c], lands[a], lands[a], (x, y, 1 - c)) for a in range(len(grads))]


def _plan_to_other_chips(x, y, c, partials, lands):
    return [(partials[a].at[2 * chip[0] + chip[1]], lands[a].at[k], lands[a].at[k], (*chip, c))
            for a in range(len(partials)) for k, chip in enumerate(_other_chips(x, y))]


def _plan_to_all(x, y, c, blocks, lands):
    flips = [(fx, fy, fc) for fx in (0, 1) for fy in (0, 1) for fc in (0, 1) if (fx, fy, fc) != (0, 0, 0)]
    peers = [(1 - x if fx else x, 1 - y if fy else y, 1 - c if fc else c) for fx, fy, fc in flips]
    return [(blocks[0], lands[0].at[4 * x + 2 * y + c], lands[0].at[4 * p[0] + 2 * p[1] + p[2]], p) for p in peers]


def _planned_copies(plan, srcs, lands, send_sems, recv_sems):
    x, y, c = _place()

    def pair(k, src, there, here, to):
        make = lambda dst: pltpu.make_async_remote_copy(
            src_ref=src, dst_ref=dst, send_sem=send_sems.at[k], recv_sem=recv_sems.at[k], device_id=to, device_id_type=MESH)
        return make(there), make(here)

    return [pair(k, *entry) for k, entry in enumerate(plan(x, y, c, srcs, lands))]


_HBM_SPEC = pl.BlockSpec(memory_space=pltpu.HBM)
_SEM_SPEC = pl.BlockSpec(memory_space=pltpu.SEMAPHORE)


def _hbm(a):
    return pltpu.with_memory_space_constraint(a, pltpu.HBM)


def _exchange_start(name, plan, n_copies, srcs, land_shapes, after):
    ns, nl = len(srcs), len(land_shapes)
    n_in = ns + nl + 1

    def body(*refs):
        for send, _ in _planned_copies(plan, refs[:ns], refs[ns:ns + nl], refs[n_in], refs[n_in + 1]):
            send.start()
        refs[-1][...] = jnp.zeros_like(refs[-1])

    out = pl.pallas_call(
        body, name=name,
        out_shape=(pltpu.SemaphoreType.DMA((n_copies,)), pltpu.SemaphoreType.DMA((n_copies,)),
                   *[pltpu.HBM(s.shape, s.dtype) for s in land_shapes], jax.ShapeDtypeStruct((8, 128), F32)),
        in_specs=[_HBM_SPEC] * (ns + nl) + [pl.BlockSpec(memory_space=pl.ANY)],
        out_specs=(_SEM_SPEC, _SEM_SPEC, *[_HBM_SPEC] * nl, pl.BlockSpec(memory_space=pltpu.VMEM)),
        input_output_aliases={ns + i: 2 + i for i in range(nl)},
        compiler_params=pltpu.CompilerParams(has_side_effects=pltpu.SideEffectType.DATAFLOW_SIDE_EFFECTING),
    )(*[_hbm(s) for s in srcs], *[_hbm(lax.empty(s.shape, s.dtype)) for s in land_shapes], after)
    return out[0], out[1], list(out[2:2 + nl]), out[-1]


def _exchange_wait(name, plan, srcs, started, after):
    send_sems, recv_sems, lands, _ = started
    ns, nl = len(srcs), len(lands)

    def body(*refs):
        for send, recv in _planned_copies(plan, refs[:ns], refs[ns:ns + nl], refs[ns + nl], refs[ns + nl + 1]):
            send.wait_send()
            recv.wait_recv()

    return pl.pallas_call(
        body, name=name, out_shape=[pltpu.HBM(l.shape, l.dtype) for l in lands],
        in_specs=[_HBM_SPEC] * (ns + nl) + [_SEM_SPEC, _SEM_SPEC, pl.BlockSpec(memory_space=pl.ANY)],
        out_specs=[_HBM_SPEC] * nl, input_output_aliases={ns + i: i for i in range(nl)},
        compiler_params=pltpu.CompilerParams(has_side_effects=pltpu.SideEffectType.DATAFLOW_SIDE_EFFECTING),
    )(*[_hbm(s) for s in srcs], *lands, send_sems, recv_sems, after)


def _exchange_halves(name, halves, after):
    n = len(halves)

    def body(*refs):
        ins, outs, stage = refs[:n], refs[n + 1:2 * n + 1], refs[2 * n + 1:3 * n + 1]
        send_sems, recv_sems, local_sems = refs[3 * n + 1:]
        x, y, c = _place()
        load = [pltpu.make_async_copy(ins[a], stage[a], local_sems.at[a]) for a in range(n)]
        local = [pltpu.make_async_copy(stage[a], outs[a].at[c], local_sems.at[a]) for a in range(n)]
        remote = [pltpu.make_async_remote_copy(
            src_ref=stage[a], dst_ref=outs[a].at[c], send_sem=send_sems.at[a], recv_sem=recv_sems.at[a],
            device_id=(x, y, 1 - c), device_id_type=MESH) for a in range(n)]
        for cp in load:
            cp.start()
        for a in range(n):
            load[a].wait()
            remote[a].start()
            local[a].start()
        for a in range(n):
            pltpu.make_async_remote_copy(
                src_ref=ins[a], dst_ref=outs[a].at[1 - c], send_sem=send_sems.at[a], recv_sem=recv_sems.at[a],
                device_id=(x, y, 1 - c), device_id_type=MESH).wait_recv()
        for cp in remote:
            cp.wait_send()
        for cp in local:
            cp.wait()

    any_spec = pl.BlockSpec(memory_space=pl.ANY)
    return pl.pallas_call(
        body, name=name,
        out_shape=[jax.ShapeDtypeStruct((2,) + h.shape, h.dtype) for h in halves],
        in_specs=[any_spec] * (n + 1), out_specs=[any_spec] * n,
        scratch_shapes=[pltpu.VMEM(h.shape, h.dtype) for h in halves]
        + [pltpu.SemaphoreType.DMA((n,)), pltpu.SemaphoreType.DMA((n,)), pltpu.SemaphoreType.DMA((n,))],
        compiler_params=pltpu.CompilerParams(vmem_limit_bytes=VMEM_LIMIT_V7X),
    )(*halves, after)


def _like(arrays, lead, dtype=None):
    return [jax.ShapeDtypeStruct(tuple(lead) + a.shape[-2:], dtype or a.dtype) for a in arrays]


class _StepExchanges:
    def __init__(self, mats, conv_w):
        x, y, c = _place()
        self.place = jnp.stack([c, 2 * x + y]).astype(jnp.int32)
        shards = [w.astype(BF16).reshape(2, w.shape[0] // 2, w.shape[1]) for w in mats]
        (w_in,) = _weights_allgather("w_in_allgather", shards[:1])
        self.w_in = w_in.reshape(N_CHIPS, 2 * w_in.shape[2], w_in.shape[3])
        taps = jnp.pad(conv_w, ((0, 8 - conv_w.shape[0]), (0, 128 - conv_w.shape[1])))
        self._rest_shards = shards[1:] + [jnp.stack([taps, jnp.zeros_like(taps)])]
        self._rest = _exchange_start("rest_allgather_start", _plan_first_hop, 3 * len(self._rest_shards),
                                     self._rest_shards, _like(self._rest_shards, (N_CHIPS, 2)), w_in)
        self.zero = self._rest[3]
        self._taps_shape = conv_w.shape
        self._groups = {}

    def rest_weights(self, after):
        landed = _exchange_wait("rest_allgather_wait", _plan_first_hop, self._rest_shards, self._rest, after)
        *mats, taps = _weights_allgather("rest_allgather_finish", self._rest_shards, landed=landed)
        k, w = self._taps_shape
        taps = taps[:, 0, :k, :w].transpose(1, 0, 2).reshape(k, N_CHIPS * w)
        return [g.reshape(N_CHIPS, 2 * g.shape[2], g.shape[3]) for g in mats], taps

    def send_grads(self, key, grads):
        grads = list(grads)
        started = _exchange_start(f"{key}_grads_to_sibling_start", _plan_other_half_to_sibling, len(grads), grads,
                                  _like(grads, (N_CHIPS,)), self.zero)
        self._groups[key] = dict(grads=grads, to_sibling=started)
        self.zero = started[3]

    def grads_at_sibling(self, key, after):
        group = self._groups[key]
        grads = group["grads"]
        group["from_sibling"] = _exchange_wait(f"{key}_grads_to_sibling_wait", _plan_other_half_to_sibling, grads,
                                               group["to_sibling"], after)
        group["partials"] = [_chip_sum_bf16(f"{key}_chip_sum_{a}", grads[a], group["from_sibling"][a], self.place)
                             for a in range(len(grads))]
        group["to_chips"] = _exchange_start(f"{key}_grads_to_chips_start", _plan_to_other_chips, 3 * len(grads),
                                            group["partials"], _like(group["partials"], (3,)), self.zero)
        self.zero = group["to_chips"][3]

    def grads_summed(self, key, after):
        group = self._groups[key]
        from_chips = _exchange_wait(f"{key}_grads_to_chips_wait", _plan_to_other_chips, group["partials"],
                                    group["to_chips"], after)
        return [_final_sum(f"{key}_final_sum_{a}", group["grads"][a], group["from_sibling"][a], from_chips[a], self.place)
                for a in range(len(from_chips))]

    def send_small(self, block):
        self._small = block
        self._small_started = _exchange_start("small_grads_start", _plan_to_all, 7, [block],
                                              [jax.ShapeDtypeStruct((8,) + block.shape, block.dtype)], self.zero)
        self.zero = self._small_started[3]

    def small_summed(self, after):
        x, y, c = _place()
        (landed,) = _exchange_wait("small_grads_wait", _plan_to_all, [self._small], self._small_started, after)
        blocks = lax.dynamic_update_index_in_dim(landed, self._small, 4 * x + 2 * y + c, 0)
        return _sum_blocks("small_sum", blocks)


def _rope_tables(positions):
    half = HEAD // 2
    inv_freq = jnp.float32(ROPE_THETA) ** (-(jnp.arange(half, dtype=F32) * 2.0 / HEAD))
    ang = positions.astype(F32)[:, None] * inv_freq
    cos, sin = jnp.cos(ang), jnp.sin(ang)
    return jnp.tile(cos, (1, 4)), jnp.tile(jnp.concatenate([-sin, sin], axis=1), (1, 2))


def _local_step(x, mem, positions, target, gains, ex):
    g_pre_mix, g_mem, g_a, g_c, g_x, g_post_mix, g_pre_mlp, g_post_mlp = gains
    tm = ROW_TILE
    cos, sin = _rope_tables(positions)
    w_in = ex.w_in

    h, q, k, v, bcu, qx = _in_proj_fwd(x, g_pre_mix + ex.zero[:1, :1], w_in, cos, sin, tm)
    ya, lse = _attn_fwd(q, k, v)
    (w_kv, w_out, w_up, w_down), conv_w = ex.rest_weights(lse)
    w_kv, w_out, w_down = (w.reshape(N_CHIPS * w.shape[1], w.shape[2]) for w in (w_kv, w_out, w_down))
    memn, mkv = _memkv_fwd(mem, g_mem, w_kv)
    yx, ycat, y2, x1 = _mix_fwd(ya, bcu, qx, mkv, conv_w, g_a, g_c, g_x, w_out, g_post_mix, x, tm)
    h2, f, du, df2, dx1, dg_pre_mlp, dg_post_mlp, loss = _mlp_fwd_bwd(x1, target, g_pre_mlp, g_post_mlp, w_up, w_down, tm)
    gw_down = _weight_grad("grad_w_down", f, df2, True)
    gw_up = _weight_grad("grad_w_up", h2, du, False)
    ex.send_grads("early", [gw_up, gw_down])

    dy2, dya, delta, dycx, dg_post_mix, dg_a, dg_c, dg_x = _mix_bwd(dx1, y2, ya, yx, bcu, conv_w, g_a, g_c, g_x,
                                                                  w_out, g_post_mix + ex.zero[:1, :1], tm)
    ex.grads_at_sibling("early", dy2)
    gw_out = _weight_grad("grad_w_out", ycat, dy2, True)
    tail, dmkv, g_conv = _conv_xattn_bwd(dycx, bcu, qx, mkv, conv_w + ex.zero[:1, :1], tm)
    gw_kv, dg_mem = _memkv_bwd(mem, g_mem, w_kv, dmkv)
    ex.send_grads("mid", [gw_out, gw_kv])
    dqkv = _attn_bwd(q, k, v, dya, lse, delta, ex.zero)
    ex.grads_at_sibling("mid", dqkv[0])
    dproj, grad_x, dg_pre_mix = _in_proj_bwd(dqkv, tail, cos, sin, w_in, x, g_pre_mix + ex.zero[:1, :1], dx1, tm)
    gain_grads = [dg_pre_mix, dg_mem, dg_a, dg_c, dg_x, dg_post_mix, dg_pre_mlp, dg_post_mlp]
    ex.send_small(_pack_small(gain_grads, g_conv, loss))
    gw_in = _weight_grad("grad_w_in", h, dproj, False)
    ex.send_grads("late", [gw_in])
    return grad_x


def _pack_small(gains, conv, scalar=None):
    rows = [jnp.pad(g, ((0, 0), (0, D_MODEL - g.shape[1]))) for g in gains]
    rows.append(jnp.pad(conv, ((0, 0), (0, D_MODEL - conv.shape[1]))))
    last = jnp.zeros((SMALL_ROWS - 8 - conv.shape[0], D_MODEL), F32)
    rows.append(last if scalar is None else last.at[0:1, 0:1].set(scalar))
    return jnp.concatenate(rows, axis=0)


def _unpack_small(block, gain_widths, conv_width):
    gains = [block[i:i + 1, :w] for i, w in enumerate(gain_widths)]
    return gains, block[8:11, :conv_width], block[11, 0]


def kernel(x, mem, positions, g_pre_mix, g_mem, w_in, w_mem_kv, conv_w, g_attn_out, g_conv_out, g_xattn_out, w_out, g_post_mix, g_pre_mlp, w_up, w_down, g_post_mlp, loss_target, m_g_pre_mix, m_g_mem, m_w_in, m_w_mem_kv, m_conv_w, m_g_attn_out, m_g_conv_out, m_g_xattn_out, m_w_out, m_g_post_mix, m_g_pre_mlp, m_w_up, m_w_down, m_g_post_mlp, v_g_pre_mix, v_g_mem, v_w_in, v_w_mem_kv, v_conv_w, v_g_attn_out, v_g_conv_out, v_g_xattn_out, v_w_out, v_g_post_mix, v_g_pre_mlp, v_w_up, v_w_down, v_g_post_mlp):
    cx, cy, cc = _place()
    chip = 2 * cx + cy
    gains = [g_pre_mix, g_mem, g_attn_out, g_conv_out, g_xattn_out, g_post_mix, g_pre_mlp, g_post_mlp]
    gains_m = [m_g_pre_mix, m_g_mem, m_g_attn_out, m_g_conv_out, m_g_xattn_out, m_g_post_mix, m_g_pre_mlp, m_g_post_mlp]
    gains_v = [v_g_pre_mix, v_g_mem, v_g_attn_out, v_g_conv_out, v_g_xattn_out, v_g_post_mix, v_g_pre_mlp, v_g_post_mlp]
    gain_widths = [g.shape[1] for g in gains]
    mats = [w_in[0], w_mem_kv[0], w_out[0], w_up[0], w_down[0]]
    mats_m = [m_w_in[0], m_w_mem_kv[0], m_w_out[0], m_w_up[0], m_w_down[0]]
    mats_v = [v_w_in[0], v_w_mem_kv[0], v_w_out[0], v_w_up[0], v_w_down[0]]

    ex = _StepExchanges(mats, conv_w[0])
    grad_x = _local_step(x[0], mem[0], positions[0], loss_target[0], gains, ex)

    both = lambda halves: [t.reshape(2 * t.shape[1], t.shape[2]) for t in halves]
    done = ex.grads_summed("early", ex.zero) + ex.grads_summed("mid", ex.zero)
    ex.grads_at_sibling("late", sum(t[:8, :128] for t in done))
    up_sum, down_sum, out_sum, kv_sum = both(_exchange_halves("sums_to_sibling", done, ex.zero))
    adamw = lambda a, g, after: _adamw(f"adamw_{a}", mats[a], g, mats_m[a], mats_v[a], after)
    new_up, new_down, new_out, new_kv = adamw(3, up_sum, ex.zero), adamw(4, down_sum, ex.zero), adamw(2, out_sum, ex.zero), adamw(1, kv_sum, ex.zero)

    small, total = _small_update(ex.small_summed(new_kv[0]), chip.reshape(1).astype(jnp.int32), gains, gains_m,
                                 gains_v, conv_w[0], m_conv_w[0], v_conv_w[0])

    (in_half,) = ex.grads_summed("late", small[0][1])
    (in_sum,) = both(_exchange_halves("late_sum_to_sibling", [in_half], in_half))
    new_in = adamw(0, in_sum, in_sum)
    mat_sums = [in_sum, kv_sum, out_sum, up_sum, down_sum]
    mat_new = [new_in, new_kv, new_out, new_up, new_down]

    order = ["g_pre_mix", "g_mem", "w_in", "w_mem_kv", "conv_w", "g_attn_out", "g_conv_out", "g_xattn_out", "w_out",
             "g_post_mix", "g_pre_mlp", "w_up", "w_down", "g_post_mlp"]
    gain_names = ["g_pre_mix", "g_mem", "g_attn_out", "g_conv_out", "g_xattn_out", "g_post_mix", "g_pre_mlp", "g_post_mlp"]
    mat_names = ["w_in", "w_mem_kv", "w_out", "w_up", "w_down"]

    def leaf(kind, name):
        if name in gain_names:
            return small[gain_names.index(name)][kind]
        if name == "conv_w":
            return small[len(gain_names)][kind][None]
        a = mat_names.index(name)
        return (mat_sums[a] if kind == 0 else mat_new[a][kind - 1])[None]

    return (total[0, 0], grad_x[None], *[leaf(kind, name) for kind in range(4) for name in order])
```

```python
import jax
import jax.numpy as jnp
from jax import lax
from jax.experimental import pallas as pl
from jax.experimental.pallas import tpu as pltpu

F32, BF16 = jnp.float32, jnp.bfloat16

D_MODEL = 1024
ATTN_W = 512
CONV_W = 256
XATTN_W = 256
PROJ_W = 3 * ATTN_W + 3 * CONV_W + XATTN_W
D_FF = 4096
HEAD = 64
N_BACK = 128
DILATIONS = (1, 4, 16)
ROPE_THETA = 10000.0
EPS = 1e-6
NEG_INF = -1e30
SCALE = HEAD ** -0.5
N_CHIPS = 4
SHARD_IN = PROJ_W // N_CHIPS
SHARD_FF = D_FF // N_CHIPS

ADAM_LR, ADAM_B1, ADAM_B2, ADAM_EPS, ADAM_WD, ADAM_STEP = 0.001, 0.9, 0.999, 1e-08, 0.01, 10

VMEM_LIMIT_V7X = 56 * 1024 * 1024
ROW_TILE = 512
MLP_ROW_TILE = 256
SMALL_ROWS = 16

NT = (((1,), (1,)), ((), ()))
TN = (((0,), (0,)), ((), ()))
MESH = pl.DeviceIdType.MESH


def _params(*sem):
    return pltpu.CompilerParams(dimension_semantics=sem, vmem_limit_bytes=VMEM_LIMIT_V7X)


def _resident(shape):
    return pl.BlockSpec(shape, lambda *_: (0,) * len(shape), pipeline_mode=pl.Buffered(1))


def _rows(tm, width):
    return pl.BlockSpec((tm, width), lambda i: (i, 0))


def _rms_hat(x):
    r = lax.rsqrt(jnp.mean(x * x, axis=-1, keepdims=True) + EPS)
    return x * r, r


def _rms_bwd(xhat, r, g, dy):
    gdy = dy * g
    return r * (gdy - xhat * jnp.mean(xhat * gdy, axis=-1, keepdims=True))


def _rope128(t, cos, sin_signed, inverse):
    lane = lax.broadcasted_iota(jnp.int32, t.shape, 1)
    first_half = (lane % HEAD) < (HEAD // 2)
    rot = jnp.where(first_half, pltpu.roll(t, 128 - HEAD // 2, 1), pltpu.roll(t, HEAD // 2, 1))
    return t * cos - rot * sin_signed if inverse else t * cos + rot * sin_signed


def _in_proj_fwd(x, g, w_in, cos, sin, tm):
    S = x.shape[0]

    def body(x_ref, g_ref, w_ref, cos_ref, sin_ref, h_ref, q_ref, k_ref, v_ref, bcu_ref, qx_ref, proj):
        xhat, _ = _rms_hat(x_ref[...])
        h = (xhat * g_ref[...]).astype(BF16)
        h_ref[...] = h
        for j in range(N_CHIPS):
            proj[:, SHARD_IN * j:SHARD_IN * (j + 1)] = jnp.dot(h, w_ref[j], preferred_element_type=F32)
        c, s = cos_ref[...], sin_ref[...]
        for j in range(ATTN_W // 128):
            lo = 128 * j
            q_ref[:, lo:lo + 128] = _rope128(proj[:, lo:lo + 128], c, s, False) * SCALE
            k_ref[:, lo:lo + 128] = _rope128(proj[:, ATTN_W + lo:ATTN_W + lo + 128], c, s, False)
        v_ref[...] = proj[:, 2 * ATTN_W:3 * ATTN_W]
        bcu_ref[...] = proj[:, 3 * ATTN_W:3 * ATTN_W + 3 * CONV_W]
        qx_ref[...] = proj[:, 3 * ATTN_W + 3 * CONV_W:PROJ_W].astype(BF16)

    return pl.pallas_call(
        body, name="in_proj_fwd", grid=(S // tm,),
        in_specs=[_rows(tm, D_MODEL), _resident((1, D_MODEL)), _resident((N_CHIPS, D_MODEL, SHARD_IN)),
                  _rows(tm, 128), _rows(tm, 128)],
        out_specs=[_rows(tm, D_MODEL), _rows(tm, ATTN_W), _rows(tm, ATTN_W), _rows(tm, ATTN_W),
                   _rows(tm, 3 * CONV_W), _rows(tm, XATTN_W)],
        out_shape=[jax.ShapeDtypeStruct((S, D_MODEL), BF16), jax.ShapeDtypeStruct((S, ATTN_W), F32),
                   jax.ShapeDtypeStruct((S, ATTN_W), F32), jax.ShapeDtypeStruct((S, ATTN_W), F32),
                   jax.ShapeDtypeStruct((S, 3 * CONV_W), F32), jax.ShapeDtypeStruct((S, XATTN_W), BF16)],
        scratch_shapes=[pltpu.VMEM((tm, PROJ_W), F32)],
        compiler_params=_params("parallel"),
    )(x, g, w_in, cos, sin)


def _memkv_fwd(mem, g_mem, w_kv):
    n_mem = mem.shape[0]

    def body(mem_ref, g_ref, w_ref, mn_ref, kv_ref):
        mhat, _ = _rms_hat(mem_ref[...])
        mn = (mhat * g_ref[...]).astype(BF16)
        mn_ref[...] = mn
        kv_ref[...] = jnp.dot(mn, w_ref[...], preferred_element_type=F32).astype(BF16)

    return pl.pallas_call(
        body, name="memkv_fwd",
        out_shape=[jax.ShapeDtypeStruct((n_mem, D_MODEL), BF16), jax.ShapeDtypeStruct((n_mem, 2 * XATTN_W), BF16)],
        compiler_params=pltpu.CompilerParams(vmem_limit_bytes=VMEM_LIMIT_V7X),
    )(mem, g_mem, w_kv)


def _fill_band_bias(bias):
    row = lax.broadcasted_iota(jnp.int32, (N_BACK, 2 * N_BACK), 0)
    col = lax.broadcasted_iota(jnp.int32, (N_BACK, 2 * N_BACK), 1)
    band = (col >= row) & (col <= row + N_BACK)
    bias[1] = jnp.where(band, 0.0, NEG_INF)
    bias[0] = jnp.where(band & (col >= N_BACK), 0.0, NEG_INF)


def _strided(start, size, d):
    return pl.ds(start, size) if d == 1 else pl.ds(start, size, stride=d)


def _group_starts(g, G, nb, d):
    t0 = g * G
    r, n0 = lax.shift_right_logical(t0, nb.bit_length() - 1), lax.bitwise_and(t0, nb - 1)
    first = r + n0 * (N_BACK * d)
    before = r + jnp.maximum(n0 - 1, 0) * (N_BACK * d)
    starts = [before] + [first + u * (N_BACK * d) for u in range(G)]
    if d == 1:
        starts = [pl.multiple_of(st, N_BACK) for st in starts]
    return starts, n0


def _step_blocks(i, U, nb, d):
    G = min(U, nb)
    row_blocks, blocks = [], []
    for grp in range(U // G):
        starts, n0 = _group_starts(i * (U // G) + grp, G, nb, d)
        base = len(row_blocks)
        row_blocks += [_strided(st, N_BACK, d) for st in starts]
        for u in range(G):
            blocks.append((base + u, base + u + 1, jnp.minimum(n0, 1) if u == 0 else 1))
    return row_blocks, blocks


def _by_head(a, b):
    lane = lax.broadcasted_iota(jnp.int32, (a.shape[0], 2 * HEAD), 1)
    return jnp.where(lane < HEAD, a, b)


def _head_only(t, hh):
    lane = lax.broadcasted_iota(jnp.int32, t.shape, 1)
    return jnp.where((lane < HEAD) == (hh == 0), t, jnp.zeros_like(t))


def _stack_heads(t):
    return jnp.concatenate([_head_only(t, 0), _head_only(t, 1)], axis=0)


def _head_columns(t):
    return jnp.concatenate([t[:, 0:1], t[:, HEAD:HEAD + 1]], axis=0)


def _unstack(t):
    return _by_head(t[:N_BACK], t[N_BACK:])


def _unstack_columns(t):
    return _by_head(jnp.broadcast_to(t[:N_BACK], (N_BACK, 2 * HEAD)), jnp.broadcast_to(t[N_BACK:], (N_BACK, 2 * HEAD)))


FWD_BLOCKS_PER_STEP = 4
BWD_BLOCKS_PER_STEP = 2


def _attn_fwd(q, k, v):
    S = q.shape[0]
    U = FWD_BLOCKS_PER_STEP

    def body(q_ref, k_ref, v_ref, y_ref, m_ref, l_scr, bias):
        _fill_band_bias(bias)
        for g, d in enumerate(DILATIONS):
            nb = S // d // N_BACK
            first_pattern, last_pattern = g == 0, g == len(DILATIONS) - 1

            def step(i, carry, d=d, nb=nb, first_pattern=first_pattern, last_pattern=last_pattern):
                row_blocks, blocks = _step_blocks(i, U, nb, d)
                kb = [k_ref[r, :].astype(BF16) for r in row_blocks]
                ss = []
                for before, own, which in blocks:
                    kw = jnp.concatenate([kb[before], kb[own]], 0)
                    qs = _stack_heads(q_ref[row_blocks[own], :].astype(BF16))
                    b = bias[which]
                    ss.append(lax.dot_general(qs, kw, NT, preferred_element_type=F32) + jnp.concatenate([b, b], axis=0))
                ms = [jnp.max(s, axis=1, keepdims=True) for s in ss]
                ps = [jnp.exp(s - m) for s, m in zip(ss, ms)]
                ls = [jnp.sum(p, axis=1, keepdims=True) for p in ps]
                vb = [v_ref[r, :].astype(BF16) for r in row_blocks]
                os_ = [jnp.dot(ps[u].astype(BF16), jnp.concatenate([vb[before], vb[own]], 0), preferred_element_type=F32)
                       for u, (before, own, _) in enumerate(blocks)]
                for u, (_, own, _) in enumerate(blocks):
                    o_g, m_g, l_g = _unstack(os_[u]), _unstack_columns(ms[u]), _unstack_columns(ls[u])
                    r = row_blocks[own]
                    if first_pattern:
                        m_new, l_new, acc = m_g, l_g, o_g
                    else:
                        m_old = m_ref[r, :]
                        m_new = jnp.maximum(m_old, m_g)
                        alpha, beta = jnp.exp(m_old - m_new), jnp.exp(m_g - m_new)
                        l_new = l_scr[r, :] * alpha + l_g * beta
                        acc = y_ref[r, :] * alpha + o_g * beta
                    if last_pattern:
                        y_ref[r, :] = acc / l_new
                        m_ref[r, :] = m_new + jnp.log(l_new)
                    else:
                        y_ref[r, :] = acc
                        m_ref[r, :] = m_new
                        l_scr[r, :] = l_new
                return carry

            lax.fori_loop(0, d * nb // U, step, 0)

    col = pl.BlockSpec((S, 2 * HEAD), lambda j: (0, j))
    return pl.pallas_call(
        body, name="attn_fwd", grid=(q.shape[1] // (2 * HEAD),),
        in_specs=[col, col, col], out_specs=[col, col],
        out_shape=[jax.ShapeDtypeStruct(q.shape, F32)] * 2,
        scratch_shapes=[pltpu.VMEM((S, 2 * HEAD), F32), pltpu.VMEM((2, N_BACK, 2 * N_BACK), F32)],
        compiler_params=_params("parallel"),
    )(q, k, v)


def _attn_bwd(q, k, v, dy, lse, delta, after):
    S = q.shape[0]
    U = BWD_BLOCKS_PER_STEP

    def body(q_ref, k_ref, v_ref, dy_ref, lse_ref, delta_ref, after_ref, dq_ref, dk_ref, dv_ref, bias):
        _fill_band_bias(bias)
        dk_ref[...] = jnp.zeros_like(dk_ref)
        dv_ref[...] = jnp.zeros_like(dv_ref)
        for g, d in enumerate(DILATIONS):
            nb = S // d // N_BACK

            def step(i, carry, d=d, nb=nb, g=g):
                row_blocks, blocks = _step_blocks(i, U, nb, d)
                kb = [k_ref[r, :].astype(BF16) for r in row_blocks]
                vb = [v_ref[r, :].astype(BF16) for r in row_blocks]
                kws = [jnp.concatenate([kb[before], kb[own]], 0) for before, own, _ in blocks]
                vws = [jnp.concatenate([vb[before], vb[own]], 0) for before, own, _ in blocks]
                qss = [_stack_heads(q_ref[row_blocks[own], :].astype(BF16)) for _, own, _ in blocks]
                doss = [_stack_heads(dy_ref[row_blocks[own], :].astype(BF16)) for _, own, _ in blocks]
                ss, dps = [], []
                for u, (_, _, which) in enumerate(blocks):
                    b = bias[which]
                    ss.append(lax.dot_general(qss[u], kws[u], NT, preferred_element_type=F32) + jnp.concatenate([b, b], axis=0))
                    dps.append(lax.dot_general(doss[u], vws[u], NT, preferred_element_type=F32))
                ps = [jnp.exp(ss[u] - _head_columns(lse_ref[row_blocks[own], :])) for u, (_, own, _) in enumerate(blocks)]
                dss = [(ps[u] * (dps[u] - _head_columns(delta_ref[row_blocks[own], :]))).astype(BF16)
                       for u, (_, own, _) in enumerate(blocks)]
                pbs = [p.astype(BF16) for p in ps]
                dqs = [jnp.dot(dss[u], kws[u], preferred_element_type=F32) for u in range(U)]
                dkws = [lax.dot_general(dss[u], qss[u], TN, preferred_element_type=F32) for u in range(U)]
                dvws = [lax.dot_general(pbs[u], doss[u], TN, preferred_element_type=F32) for u in range(U)]
                dk_parts, dv_parts = [None] * len(row_blocks), [None] * len(row_blocks)
                for u, (before, own, _) in enumerate(blocks):
                    dq = _unstack(dqs[u])
                    if g == 0:
                        dq_ref[row_blocks[own], :] = dq
                    else:
                        dq_ref[row_blocks[own], :] += dq
                    for idx, dkp, dvp in ((before, dkws[u][:N_BACK], dvws[u][:N_BACK]),
                                          (own, dkws[u][N_BACK:], dvws[u][N_BACK:])):
                        dk_parts[idx] = dkp if dk_parts[idx] is None else dk_parts[idx] + dkp
                        dv_parts[idx] = dvp if dv_parts[idx] is None else dv_parts[idx] + dvp
                for idx, r in enumerate(row_blocks):
                    dk_ref[r, :] += dk_parts[idx]
                    dv_ref[r, :] += dv_parts[idx]
                return carry

            lax.fori_loop(0, d * nb // U, step, 0)

    col = pl.BlockSpec((S, 2 * HEAD), lambda j: (0, j))
    return pl.pallas_call(
        body, name="attn_bwd", grid=(q.shape[1] // (2 * HEAD),),
        in_specs=[col] * 6 + [pl.BlockSpec(memory_space=pl.ANY)], out_specs=[col] * 3,
        out_shape=[jax.ShapeDtypeStruct(q.shape, F32)] * 3,
        scratch_shapes=[pltpu.VMEM((2, N_BACK, 2 * N_BACK), F32)],
        compiler_params=_params("parallel"),
    )(q, k, v, dy, lse, delta, after)


def _shift_down(z, before, k):
    row = lax.broadcasted_iota(jnp.int32, z.shape, 0)
    out = pltpu.roll(z, k, 0)
    for i in range(k):
        out = jnp.where(row == i, before[8 - k + i:8 - k + i + 1, :], out)
    return out


def _shift_up(z, after, k):
    rows = z.shape[0]
    row = lax.broadcasted_iota(jnp.int32, z.shape, 0)
    out = pltpu.roll(z, rows - k, 0)
    for i in range(k):
        out = jnp.where(row == rows - k + i, after[i:i + 1, :], out)
    return out


def _conv_fwd(bcu, before, is_first, w):
    b, c, u = bcu[:, 0:CONV_W], bcu[:, CONV_W:2 * CONV_W], bcu[:, 2 * CONV_W:3 * CONV_W]
    z = c * u
    zb = jnp.where(is_first, 0.0, before[:, CONV_W:2 * CONV_W] * before[:, 2 * CONV_W:3 * CONV_W])
    z1, z2 = _shift_down(z, zb, 1), _shift_down(z, zb, 2)
    cv = w[0:1, :] * z2 + w[1:2, :] * z1 + w[2:3, :] * z
    return b, c, u, z, z1, z2, cv


def _halo_before(tm, width):
    return pl.BlockSpec((8, width), lambda i: (jnp.maximum(i * (tm // 8) - 1, 0), 0))


def _halo_after(tm, width, S):
    return pl.BlockSpec((8, width), lambda i: (jnp.minimum((i + 1) * (tm // 8), S // 8 - 1), 0))


def _mix_fwd(ya, bcu, qx, mkv, conv_w, g_a, g_c, g_x, w_out, g_post, x, tm):
    S = x.shape[0]

    def body(ya_ref, bcu_ref, before_ref, qx_ref, mkv_ref, cw_ref, ga_ref, gc_ref, gx_ref,
             wo_ref, gp_ref, x_ref, yx_ref, ycat_ref, y2_ref, x1_ref):
        ya = ya_ref[...]
        b, _, _, _, _, _, cv = _conv_fwd(bcu_ref[...], before_ref[...], pl.program_id(0) == 0, cw_ref[...])
        yc = b * cv

        qxb, mkvb = qx_ref[...], mkv_ref[...]
        for hd in range(XATTN_W // HEAD):
            sl = slice(HEAD * hd, HEAD * (hd + 1))
            s = lax.dot_general(qxb[:, sl], mkvb[:, sl], NT, preferred_element_type=F32) * SCALE
            mx = jnp.max(s, axis=1, keepdims=True)
            p = jnp.exp(s - mx)
            l = jnp.sum(p, axis=1, keepdims=True)
            vm = mkvb[:, XATTN_W + HEAD * hd:XATTN_W + HEAD * (hd + 1)]
            yx_ref[:, sl] = jnp.dot(p.astype(BF16), vm, preferred_element_type=F32) / l
        yx = yx_ref[...]

        ycat_ref[:, 0:ATTN_W] = (_rms_hat(ya)[0] * ga_ref[...]).astype(BF16)
        ycat_ref[:, ATTN_W:ATTN_W + CONV_W] = (_rms_hat(yc)[0] * gc_ref[...]).astype(BF16)
        ycat_ref[:, ATTN_W + CONV_W:D_MODEL] = (_rms_hat(yx)[0] * gx_ref[...]).astype(BF16)
        y2 = jnp.dot(ycat_ref[...], wo_ref[...], preferred_element_type=F32)
        y2_ref[...] = y2
        x1_ref[...] = x_ref[...] + _rms_hat(y2)[0] * gp_ref[...]

    n_mem = mkv.shape[0]
    return pl.pallas_call(
        body, name="mix_fwd", grid=(S // tm,),
        in_specs=[_rows(tm, ATTN_W), _rows(tm, 3 * CONV_W), _halo_before(tm, 3 * CONV_W), _rows(tm, XATTN_W),
                  _resident((n_mem, 2 * XATTN_W)), _resident((3, CONV_W)), _resident((1, ATTN_W)),
                  _resident((1, CONV_W)), _resident((1, XATTN_W)), _resident((D_MODEL, D_MODEL)),
                  _resident((1, D_MODEL)), _rows(tm, D_MODEL)],
        out_specs=[_rows(tm, XATTN_W), _rows(tm, D_MODEL), _rows(tm, D_MODEL), _rows(tm, D_MODEL)],
        out_shape=[jax.ShapeDtypeStruct((S, XATTN_W), F32), jax.ShapeDtypeStruct((S, D_MODEL), BF16),
                   jax.ShapeDtypeStruct((S, D_MODEL), F32), jax.ShapeDtypeStruct((S, D_MODEL), F32)],
        compiler_params=_params("parallel"),
    )(ya, bcu, bcu, qx, mkv, conv_w, g_a, g_c, g_x, w_out, g_post, x)


def _mlp_fwd_bwd(x1, target, g_pre, g_post, w_up, w_down, tm):
    S = x1.shape[0]
    n_ff = D_FF // SHARD_FF

    def body(x1_ref, t_ref, gpre_ref, gpost_ref, wup_ref, wdn_ref,
             h2_ref, f_ref, du_ref, df2_ref, dx1_ref, dgpre_ref, dgpost_ref, loss_ref, u_scr):
        @pl.when(pl.program_id(0) == 0)
        def _():
            dgpre_ref[...] = jnp.zeros_like(dgpre_ref)
            dgpost_ref[...] = jnp.zeros_like(dgpost_ref)
            loss_ref[...] = jnp.zeros_like(loss_ref)

        x1 = x1_ref[...]
        x1hat, r1 = _rms_hat(x1)
        h2 = (x1hat * gpre_ref[...]).astype(BF16)
        h2_ref[...] = h2
        f2 = jnp.zeros((tm, D_MODEL), F32)
        for j in range(n_ff):
            cols = slice(SHARD_FF * j, SHARD_FF * (j + 1))
            u = jnp.maximum(jnp.dot(h2, wup_ref[j], preferred_element_type=F32), 0.0)
            u_scr[:, cols] = u
            f = (u * u).astype(BF16)
            f_ref[:, cols] = f
            f2 = f2 + jnp.dot(f, wdn_ref[cols, :], preferred_element_type=F32)
        f2hat, r2 = _rms_hat(f2)
        err = x1 + f2hat * gpost_ref[...] - t_ref[...]
        loss_ref[...] += 0.5 * jnp.sum(jnp.mean(err * err, axis=-1, keepdims=True), axis=0, keepdims=True)
        dx2 = err * (1.0 / D_MODEL)
        dgpost_ref[...] += jnp.sum(dx2 * f2hat, axis=0, keepdims=True)
        df2 = _rms_bwd(f2hat, r2, gpost_ref[...], dx2).astype(BF16)
        df2_ref[...] = df2
        dh2 = jnp.zeros((tm, D_MODEL), F32)
        for j in range(n_ff):
            cols = slice(SHARD_FF * j, SHARD_FF * (j + 1))
            df = lax.dot_general(df2, wdn_ref[cols, :], NT, preferred_element_type=F32)
            du = (2.0 * u_scr[:, cols] * df).astype(BF16)
            du_ref[:, cols] = du
            dh2 = dh2 + lax.dot_general(du, wup_ref[j], NT, preferred_element_type=F32)
        dgpre_ref[...] += jnp.sum(dh2 * x1hat, axis=0, keepdims=True)
        dx1_ref[...] = dx2 + _rms_bwd(x1hat, r1, gpre_ref[...], dh2)

    acc = pl.BlockSpec((1, D_MODEL), lambda i: (0, 0))
    return pl.pallas_call(
        body, name="mlp_fwd_bwd", grid=(S // tm,),
        in_specs=[_rows(tm, D_MODEL), _rows(tm, D_MODEL), _resident((1, D_MODEL)), _resident((1, D_MODEL)),
                  _resident((n_ff, D_MODEL, SHARD_FF)), _resident((D_FF, D_MODEL))],
        out_specs=[_rows(tm, D_MODEL), _rows(tm, D_FF), _rows(tm, D_FF), _rows(tm, D_MODEL), _rows(tm, D_MODEL),
                   acc, acc, pl.BlockSpec((1, 1), lambda i: (0, 0))],
        out_shape=[jax.ShapeDtypeStruct((S, D_MODEL), BF16), jax.ShapeDtypeStruct((S, D_FF), BF16),
                   jax.ShapeDtypeStruct((S, D_FF), BF16), jax.ShapeDtypeStruct((S, D_MODEL), BF16),
                   jax.ShapeDtypeStruct((S, D_MODEL), F32), jax.ShapeDtypeStruct((1, D_MODEL), F32),
                   jax.ShapeDtypeStruct((1, D_MODEL), F32), jax.ShapeDtypeStruct((1, 1), F32)],
        scratch_shapes=[pltpu.VMEM((tm, D_FF), F32)],
        compiler_params=_params("arbitrary"),
    )(x1, target, g_pre, g_post, w_up, w_down)


def _weight_grad(name, a, b, rows_sharded):
    S, K = a.shape
    N = b.shape[1]
    if rows_sharded:
        tk, tn = K // N_CHIPS, N
        a_spec = pl.BlockSpec((S, tk), lambda j: (0, j))
        b_spec = pl.BlockSpec((S, tn), lambda j: (0, 0), pipeline_mode=pl.Buffered(1))
    else:
        tk, tn = K, N // N_CHIPS
        a_spec = pl.BlockSpec((S, tk), lambda j: (0, 0), pipeline_mode=pl.Buffered(1))
        b_spec = pl.BlockSpec((S, tn), lambda j: (0, j))
    half = tk // 2

    def body(a_ref, b_ref, o_ref):
        res = lax.dot_general(a_ref[...], b_ref[...], TN, preferred_element_type=F32)
        o_ref[0, 0] = res[:half]
        o_ref[1, 0] = res[half:]

    return pl.pallas_call(
        body, name=name, grid=(N_CHIPS,), in_specs=[a_spec, b_spec],
        out_specs=pl.BlockSpec((2, 1, half, tn), lambda j: (0, j, 0, 0)),
        out_shape=jax.ShapeDtypeStruct((2, N_CHIPS, half, tn), F32),
        compiler_params=_params("parallel"),
    )(a, b)


def _mix_bwd(dx1, y2, ya, yx, bcu, conv_w, g_a, g_c, g_x, w_out, g_post, tm):
    S = dx1.shape[0]

    def body(dx1_ref, y2_ref, ya_ref, yx_ref, bcu_ref, before_ref, cw_ref, ga_ref, gc_ref, gx_ref, wo_ref, gp_ref,
             dy2_ref, dya_ref, delta_ref, dycx_ref, dgp_ref, dga_ref, dgc_ref, dgx_ref):
        @pl.when(pl.program_id(0) == 0)
        def _():
            for ref in (dgp_ref, dga_ref, dgc_ref, dgx_ref):
                ref[...] = jnp.zeros_like(ref)

        dx1 = dx1_ref[...]
        y2hat, r2 = _rms_hat(y2_ref[...])
        dgp_ref[...] += jnp.sum(dx1 * y2hat, axis=0, keepdims=True)
        dy2 = _rms_bwd(y2hat, r2, gp_ref[...], dx1).astype(BF16)
        dy2_ref[...] = dy2
        dycat = lax.dot_general(dy2, wo_ref[...], NT, preferred_element_type=F32)

        d_na = dycat[:, 0:ATTN_W]
        ya = ya_ref[...]
        yahat, ra = _rms_hat(ya)
        dga_ref[...] += jnp.sum(d_na * yahat, axis=0, keepdims=True)
        dya = _rms_bwd(yahat, ra, ga_ref[...], d_na)
        dya_ref[...] = dya
        prod = dya * ya
        hi = prod.astype(BF16)
        lo = (prod - hi.astype(F32)).astype(BF16)
        head_of = lambda axis: lax.shift_right_logical(lax.broadcasted_iota(jnp.int32, (ATTN_W, ATTN_W), axis),
                                                       HEAD.bit_length() - 1)
        same_head = head_of(0) == head_of(1)
        ones = jnp.where(same_head, 1.0, 0.0).astype(BF16)
        delta_ref[...] = (jnp.dot(hi, ones, preferred_element_type=F32) + jnp.dot(lo, ones, preferred_element_type=F32))

        b, _, _, _, _, _, cv = _conv_fwd(bcu_ref[...], before_ref[...], pl.program_id(0) == 0, cw_ref[...])
        d_nc = dycat[:, ATTN_W:ATTN_W + CONV_W]
        ychat, rc = _rms_hat(b * cv)
        dgc_ref[...] += jnp.sum(d_nc * ychat, axis=0, keepdims=True)
        dycx_ref[:, 0:CONV_W] = _rms_bwd(ychat, rc, gc_ref[...], d_nc)

        d_nx = dycat[:, ATTN_W + CONV_W:D_MODEL]
        yxhat, rx = _rms_hat(yx_ref[...])
        dgx_ref[...] += jnp.sum(d_nx * yxhat, axis=0, keepdims=True)
        dycx_ref[:, CONV_W:CONV_W + XATTN_W] = _rms_bwd(yxhat, rx, gx_ref[...], d_nx)

    acc = lambda w: pl.BlockSpec((1, w), lambda i: (0, 0))
    return pl.pallas_call(
        body, name="mix_bwd", grid=(S // tm,),
        in_specs=[_rows(tm, D_MODEL), _rows(tm, D_MODEL), _rows(tm, ATTN_W), _rows(tm, XATTN_W),
                  _rows(tm, 3 * CONV_W), _halo_before(tm, 3 * CONV_W), _resident((3, CONV_W)),
                  _resident((1, ATTN_W)), _resident((1, CONV_W)), _resident((1, XATTN_W)),
                  _resident((D_MODEL, D_MODEL)), _resident((1, D_MODEL))],
        out_specs=[_rows(tm, D_MODEL), _rows(tm, ATTN_W), _rows(tm, ATTN_W), _rows(tm, CONV_W + XATTN_W),
                   acc(D_MODEL), acc(ATTN_W), acc(CONV_W), acc(XATTN_W)],
        out_shape=[jax.ShapeDtypeStruct((S, D_MODEL), BF16), jax.ShapeDtypeStruct((S, ATTN_W), F32),
                   jax.ShapeDtypeStruct((S, ATTN_W), F32),
                   jax.ShapeDtypeStruct((S, CONV_W + XATTN_W), F32), jax.ShapeDtypeStruct((1, D_MODEL), F32),
                   jax.ShapeDtypeStruct((1, ATTN_W), F32), jax.ShapeDtypeStruct((1, CONV_W), F32),
                   jax.ShapeDtypeStruct((1, XATTN_W), F32)],
        compiler_params=_params("arbitrary"),
    )(dx1, y2, ya, yx, bcu, bcu, conv_w, g_a, g_c, g_x, w_out, g_post)


def _conv_xattn_bwd(dycx, bcu, qx, mkv, conv_w, tm):
    S = dycx.shape[0]
    n_mem = mkv.shape[0]
    n_tiles = S // tm

    def body(d_ref, dafter_ref, bcu_ref, before_ref, after_ref, qx_ref, mkv_ref, cw_ref,
             tail_ref, dmkv_ref, dcw_ref):
        i = pl.program_id(0)

        @pl.when(i == 0)
        def _():
            dmkv_ref[...] = jnp.zeros_like(dmkv_ref)
            dcw_ref[...] = jnp.zeros_like(dcw_ref)

        w = cw_ref[...]
        b, c, u, z, z1, z2, cv = _conv_fwd(bcu_ref[...], before_ref[...], i == 0, w)
        dyc = d_ref[:, 0:CONV_W]
        dcv = dyc * b
        dcv_after = jnp.where(i == n_tiles - 1, 0.0, dafter_ref[:, 0:CONV_W] * after_ref[:, 0:CONV_W])
        dz = w[2:3, :] * dcv + w[1:2, :] * _shift_up(dcv, dcv_after, 1) + w[0:1, :] * _shift_up(dcv, dcv_after, 2)
        dcw_ref[0:1, :] += jnp.sum(dcv * z2, axis=0, keepdims=True)
        dcw_ref[1:2, :] += jnp.sum(dcv * z1, axis=0, keepdims=True)
        dcw_ref[2:3, :] += jnp.sum(dcv * z, axis=0, keepdims=True)
        tail_ref[:, 0:CONV_W] = (dyc * cv).astype(BF16)
        tail_ref[:, CONV_W:2 * CONV_W] = (dz * u).astype(BF16)
        tail_ref[:, 2 * CONV_W:3 * CONV_W] = (dz * c).astype(BF16)

        qxb, mkvb = qx_ref[...], mkv_ref[...]
        for hd in range(XATTN_W // HEAD):
            sl = slice(HEAD * hd, HEAD * (hd + 1))
            vsl = slice(XATTN_W + HEAD * hd, XATTN_W + HEAD * (hd + 1))
            s = lax.dot_general(qxb[:, sl], mkvb[:, sl], NT, preferred_element_type=F32) * SCALE
            e = jnp.exp(s - jnp.max(s, axis=1, keepdims=True))
            p = e / jnp.sum(e, axis=1, keepdims=True)
            dob = d_ref[:, CONV_W + HEAD * hd:CONV_W + HEAD * (hd + 1)].astype(BF16)
            dp = lax.dot_general(dob, mkvb[:, vsl], NT, preferred_element_type=F32)
            ds = (p * (dp - jnp.sum(p * dp, axis=1, keepdims=True)) * SCALE).astype(BF16)
            tail_ref[:, 3 * CONV_W + HEAD * hd:3 * CONV_W + HEAD * (hd + 1)] = jnp.dot(
                ds, mkvb[:, sl], preferred_element_type=F32).astype(BF16)
            dmkv_ref[:, sl] += lax.dot_general(ds, qxb[:, sl], TN, preferred_element_type=F32)
            dmkv_ref[:, vsl] += lax.dot_general(p.astype(BF16), dob, TN, preferred_element_type=F32)

    width = CONV_W + XATTN_W
    return pl.pallas_call(
        body, name="conv_xattn_bwd", grid=(n_tiles,),
        in_specs=[_rows(tm, width), _halo_after(tm, width, S), _rows(tm, 3 * CONV_W), _halo_before(tm, 3 * CONV_W),
                  _halo_after(tm, 3 * CONV_W, S), _rows(tm, XATTN_W), _resident((n_mem, 2 * XATTN_W)),
                  _resident((3, CONV_W))],
        out_specs=[_rows(tm, 3 * CONV_W + XATTN_W), pl.BlockSpec((n_mem, 2 * XATTN_W), lambda i: (0, 0)),
                   pl.BlockSpec((3, CONV_W), lambda i: (0, 0))],
        out_shape=[jax.ShapeDtypeStruct((S, 3 * CONV_W + XATTN_W), BF16),
                   jax.ShapeDtypeStruct((n_mem, 2 * XATTN_W), F32), jax.ShapeDtypeStruct((3, CONV_W), F32)],
        compiler_params=_params("arbitrary"),
    )(dycx, dycx, bcu, bcu, bcu, qx, mkv, conv_w)


def _memkv_bwd(mem, g_mem, w_kv, dmkv):
    n_mem = mem.shape[0]
    half = D_MODEL // N_CHIPS // 2

    def body(mem_ref, g_ref, w_ref, d_ref, dw_ref, dg_ref):
        mhat, _ = _rms_hat(mem_ref[...])
        mn = (mhat * g_ref[...]).astype(BF16)
        d = d_ref[...].astype(BF16)
        for k in range(2 * N_CHIPS):
            dw_ref[k % 2, k // 2] = lax.dot_general(mn[:, half * k:half * (k + 1)], d, TN, preferred_element_type=F32)
        dmn = lax.dot_general(d, w_ref[...], NT, preferred_element_type=F32)
        dg_ref[...] = jnp.sum(dmn * mhat, axis=0, keepdims=True)

    return pl.pallas_call(
        body, name="memkv_bwd",
        out_shape=[jax.ShapeDtypeStruct((2, N_CHIPS, half, 2 * XATTN_W), F32), jax.ShapeDtypeStruct((1, D_MODEL), F32)],
        compiler_params=pltpu.CompilerParams(vmem_limit_bytes=VMEM_LIMIT_V7X),
    )(mem, g_mem, w_kv, dmkv)


def _in_proj_bwd(dqkv, tail, cos, sin, w_in, x, g, dx1, tm):
    S = x.shape[0]

    def body(dq_ref, dk_ref, dv_ref, tail_ref, cos_ref, sin_ref, w_ref, x_ref, g_ref, dx1_ref, dproj_ref, dx_ref, dg_ref):
        @pl.when(pl.program_id(0) == 0)
        def _():
            dg_ref[...] = jnp.zeros_like(dg_ref)

        c, s = cos_ref[...], sin_ref[...]
        for j in range(ATTN_W // 128):
            cols = slice(128 * j, 128 * (j + 1))
            dproj_ref[:, cols] = _rope128(dq_ref[:, cols] * SCALE, c, s, True).astype(BF16)
            dproj_ref[:, ATTN_W + 128 * j:ATTN_W + 128 * (j + 1)] = _rope128(dk_ref[:, cols], c, s, True).astype(BF16)
        dproj_ref[:, 2 * ATTN_W:3 * ATTN_W] = dv_ref[...].astype(BF16)
        dproj_ref[:, 3 * ATTN_W:PROJ_W] = tail_ref[...]
        dh = jnp.zeros((tm, D_MODEL), F32)
        for j in range(N_CHIPS):
            dh = dh + lax.dot_general(dproj_ref[:, SHARD_IN * j:SHARD_IN * (j + 1)], w_ref[j], NT,
                                      preferred_element_type=F32)
        xhat, r = _rms_hat(x_ref[...])
        dg_ref[...] += jnp.sum(dh * xhat, axis=0, keepdims=True)
        dx_ref[...] = dx1_ref[...] + _rms_bwd(xhat, r, g_ref[...], dh)

    return pl.pallas_call(
        body, name="in_proj_bwd", grid=(S // tm,),
        in_specs=[_rows(tm, ATTN_W)] * 3 + [_rows(tm, PROJ_W - 3 * ATTN_W), _rows(tm, 128), _rows(tm, 128),
                  _resident((N_CHIPS, D_MODEL, SHARD_IN)), _rows(tm, D_MODEL), _resident((1, D_MODEL)),
                  _rows(tm, D_MODEL)],
        out_specs=[_rows(tm, PROJ_W), _rows(tm, D_MODEL), pl.BlockSpec((1, D_MODEL), lambda i: (0, 0))],
        out_shape=[jax.ShapeDtypeStruct((S, PROJ_W), BF16), jax.ShapeDtypeStruct((S, D_MODEL), F32),
                   jax.ShapeDtypeStruct((1, D_MODEL), F32)],
        compiler_params=_params("arbitrary"),
    )(*dqkv, tail, cos, sin, w_in, x, g, dx1)


def _row_tile(rows):
    return ROW_TILE if rows % ROW_TILE == 0 else rows


def _chip_sum_bf16(name, grad, from_sibling, place):
    _, n, rows, cols = grad.shape
    tr = _row_tile(rows)

    def body(place_ref, g_ref, b_ref, o_ref):
        o_ref[...] = (g_ref[0] + b_ref[...]).astype(BF16)

    spec = pl.BlockSpec((1, tr, cols), lambda s, i, p: (s, i, 0))
    return pl.pallas_call(
        body, name=name, out_shape=jax.ShapeDtypeStruct((n, rows, cols), BF16),
        grid_spec=pltpu.PrefetchScalarGridSpec(
            num_scalar_prefetch=1, grid=(n, rows // tr),
            in_specs=[pl.BlockSpec((1, 1, tr, cols), lambda s, i, p: (p[0], s, i, 0)), spec], out_specs=spec),
        compiler_params=_params("parallel", "parallel"),
    )(place, grad, from_sibling)


def _final_sum(name, grad, from_sibling, others, place):
    _, _, rows, cols = grad.shape
    tr = _row_tile(rows)

    def body(place_ref, own_ref, sib_ref, o0, o1, o2, out_ref):
        acc = own_ref[0, 0] + sib_ref[0]
        for o in (o0, o1, o2):
            acc = acc + o[0].astype(F32)
        out_ref[...] = acc

    other = lambda k: pl.BlockSpec((1, tr, cols), lambda i, p: (k, i, 0))
    return pl.pallas_call(
        body, name=name, out_shape=jax.ShapeDtypeStruct((rows, cols), F32),
        grid_spec=pltpu.PrefetchScalarGridSpec(
            num_scalar_prefetch=1, grid=(rows // tr,),
            in_specs=[pl.BlockSpec((1, 1, tr, cols), lambda i, p: (p[0], p[1], i, 0)),
                      pl.BlockSpec((1, tr, cols), lambda i, p: (p[1], i, 0)), other(0), other(1), other(2)],
            out_specs=pl.BlockSpec((tr, cols), lambda i, p: (i, 0))),
        compiler_params=_params("parallel"),
    )(place, grad, from_sibling, others, others, others)


def _adamw_update(w, g, m, v):
    m = ADAM_B1 * m + (1.0 - ADAM_B1) * g
    v = ADAM_B2 * v + (1.0 - ADAM_B2) * (g * g)
    m_hat = m * (1.0 / (1.0 - ADAM_B1 ** ADAM_STEP))
    v_hat = v * (1.0 / (1.0 - ADAM_B2 ** ADAM_STEP))
    return -ADAM_LR * (m_hat / (jnp.sqrt(v_hat) + ADAM_EPS) + ADAM_WD * w), m, v


def _adamw(name, w, g, m, v, after):
    rows, cols = w.shape
    tr = _row_tile(rows)

    def body(w_ref, g_ref, m_ref, v_ref, after_ref, d_ref, nm_ref, nv_ref):
        d_ref[...], nm_ref[...], nv_ref[...] = _adamw_update(w_ref[...], g_ref[...], m_ref[...], v_ref[...])

    spec = pl.BlockSpec((tr, cols), lambda i: (i, 0))
    return pl.pallas_call(
        body, name=name, grid=(rows // tr,), in_specs=[spec] * 4 + [pl.BlockSpec(memory_space=pl.ANY)],
        out_specs=[spec] * 3, out_shape=[jax.ShapeDtypeStruct(w.shape, F32)] * 3, compiler_params=_params("parallel"),
    )(w, g, m, v, after)


def _small_update(summed, chip, gains, gains_m, gains_v, taps, taps_m, taps_v):
    n = len(gains)
    widths = [g.shape[1] for g in gains]
    k, w = taps.shape

    def body(*refs):
        chip_ref, sum_ref = refs[0], refs[1]
        params = [refs[2 + 3 * i:5 + 3 * i] for i in range(n + 1)]
        outs = [refs[2 + 3 * (n + 1) + 4 * i:2 + 3 * (n + 1) + 4 * (i + 1)] for i in range(n + 1)]
        loss_ref = refs[-1]
        for i in range(n):
            g = sum_ref[i:i + 1, 0:widths[i]]
            wr, mr, vr = params[i]
            outs[i][0][...] = g
            outs[i][1][...], outs[i][2][...], outs[i][3][...] = _adamw_update(wr[...], g, mr[...], vr[...])
        g = sum_ref[n:n + k, 0:w]
        for j in range(1, N_CHIPS):
            g = jnp.where(chip_ref[0] == j, sum_ref[n:n + k, w * j:w * (j + 1)], g)
        wr, mr, vr = params[n]
        outs[n][0][...] = g
        outs[n][1][...], outs[n][2][...], outs[n][3][...] = _adamw_update(wr[...], g, mr[...], vr[...])
        loss_ref[...] = sum_ref[n + k:n + k + 1, 0:1]

    vmem = pl.BlockSpec(memory_space=pltpu.VMEM)
    operands = [chip, summed]
    for p in zip(list(gains) + [taps], list(gains_m) + [taps_m], list(gains_v) + [taps_v]):
        operands += list(p)
    shapes = [jax.ShapeDtypeStruct(p.shape, F32) for p in list(gains) + [taps] for _ in range(4)]
    out = pl.pallas_call(
        body, name="small_update", out_shape=shapes + [jax.ShapeDtypeStruct((1, 1), F32)],
        in_specs=[pl.BlockSpec(memory_space=pltpu.SMEM)] + [vmem] * (len(operands) - 1),
        out_specs=[vmem] * (len(shapes) + 1),
    )(*operands)
    return [out[4 * i:4 * (i + 1)] for i in range(n + 1)], out[-1]


def _sum_blocks(name, blocks):
    n, rows, cols = blocks.shape

    def body(b_ref, o_ref):
        acc = b_ref[0]
        for k in range(1, n):
            acc = acc + b_ref[k]
        o_ref[...] = acc

    return pl.pallas_call(body, name=name, out_shape=jax.ShapeDtypeStruct((rows, cols), F32))(blocks)


def _place():
    return lax.axis_index("x"), lax.axis_index("y"), lax.axis_index("c")


def _other_chips(x, y):
    return [(1 - x, y), (x, 1 - y), (1 - x, 1 - y)]


def _weights_allgather(name, shards, landed=None):
    n = len(shards)
    first_hop = landed is None

    def body(*refs):
        ins, outs, stage = refs[:n], refs[-3 - 2 * n:-3 - n], refs[-3 - n:-3]
        send_sems, recv_sems, local_sems = refs[-3:]
        x, y, c = _place()
        me, sibling = (x, y, c), (x, y, 1 - c)
        chips = _other_chips(x, y)
        chip_index = lambda chip: 2 * chip[0] + chip[1]

        def copy(a, k, chip, half, to, src=None):
            place = outs[a].at[chip_index(chip), half]
            return pltpu.make_async_remote_copy(
                src_ref=place if src is None else src, dst_ref=place, send_sem=send_sems.at[6 * a + k],
                recv_sem=recv_sems.at[6 * a + k], device_id=to, device_id_type=MESH)

        load = [pltpu.make_async_copy(ins[a], stage[a], local_sems.at[a]) for a in range(n)]
        local = [pltpu.make_async_copy(stage[a], outs[a].at[chip_index((x, y))], local_sems.at[a]) for a in range(n)]
        for cp in load:
            cp.start()
        first = []
        if first_hop:
            first = [copy(a, k, (x, y), c, (*chip, c), src=ins[a].at[c]) for a in range(n) for k, chip in enumerate(chips)]
        for cp in first:
            cp.start()
        for a in range(n):
            load[a].wait()
            local[a].start()
        passed = []
        for a in range(n):
            for k, chip in enumerate(chips):
                if first_hop:
                    copy(a, k, chip, c, me).wait_recv()
                passed.append(copy(a, 3 + k, chip, c, sibling))
                passed[-1].start()
        for a in range(n):
            for k, chip in enumerate(chips):
                copy(a, 3 + k, chip, 1 - c, me).wait_recv()
        for cp in first + passed:
            cp.wait_send()
        for cp in local:
            cp.wait()

    any_spec = pl.BlockSpec(memory_space=pl.ANY)
    operands = list(shards) + ([] if first_hop else list(landed))
    return pl.pallas_call(
        body, name=name,
        out_shape=[jax.ShapeDtypeStruct((N_CHIPS,) + s.shape, s.dtype) for s in shards],
        in_specs=[any_spec] * len(operands), out_specs=[any_spec] * n,
        input_output_aliases={} if first_hop else {n + a: a for a in range(n)},
        scratch_shapes=[pltpu.VMEM(s.shape, s.dtype) for s in shards]
        + [pltpu.SemaphoreType.DMA((6 * n,)), pltpu.SemaphoreType.DMA((6 * n,)), pltpu.SemaphoreType.DMA((n,))],
        compiler_params=pltpu.CompilerParams(vmem_limit_bytes=VMEM_LIMIT_V7X),
    )(*operands)


def _plan_first_hop(x, y, c, shards, lands):
    return [(shards[a].at[c], lands[a].at[2 * x + y, c], lands[a].at[2 * chip[0] + chip[1], c], (*chip, c))
            for a in range(len(shards)) for chip in _other_chips(x, y)]


def _plan_other_half_to_sibling(x, y, c, grads, lands):
    return [(grads[a].at[1 - c], lands[a], lands[a], (x, y, 1 - c)) for a in range(len(grads))]


def _plan_to_other_chips(x, y, c, partials, lands):
    return [(partials[a].at[2 * chip[0] + chip[1]], lands[a].at[k], lands[a].at[k], (*chip, c))
            for a in range(len(partials)) for k, chip in enumerate(_other_chips(x, y))]


def _plan_to_all(x, y, c, blocks, lands):
    flips = [(fx, fy, fc) for fx in (0, 1) for fy in (0, 1) for fc in (0, 1) if (fx, fy, fc) != (0, 0, 0)]
    peers = [(1 - x if fx else x, 1 - y if fy else y, 1 - c if fc else c) for fx, fy, fc in flips]
    return [(blocks[0], lands[0].at[4 * x + 2 * y + c], lands[0].at[4 * p[0] + 2 * p[1] + p[2]], p) for p in peers]


def _planned_copies(plan, srcs, lands, send_sems, recv_sems):
    x, y, c = _place()

    def pair(k, src, there, here, to):
        make = lambda dst: pltpu.make_async_remote_copy(
            src_ref=src, dst_ref=dst, send_sem=send_sems.at[k], recv_sem=recv_sems.at[k], device_id=to, device_id_type=MESH)
        return make(there), make(here)

    return [pair(k, *entry) for k, entry in enumerate(plan(x, y, c, srcs, lands))]


_HBM_SPEC = pl.BlockSpec(memory_space=pltpu.HBM)
_SEM_SPEC = pl.BlockSpec(memory_space=pltpu.SEMAPHORE)


def _hbm(a):
    return pltpu.with_memory_space_constraint(a, pltpu.HBM)


def _exchange_start(name, plan, n_copies, srcs, land_shapes, after):
    ns, nl = len(srcs), len(land_shapes)
    n_in = ns + nl + 1

    def body(*refs):
        for send, _ in _planned_copies(plan, refs[:ns], refs[ns:ns + nl], refs[n_in], refs[n_in + 1]):
            send.start()
        refs[-1][...] = jnp.zeros_like(refs[-1])

    out = pl.pallas_call(
        body, name=name,
        out_shape=(pltpu.SemaphoreType.DMA((n_copies,)), pltpu.SemaphoreType.DMA((n_copies,)),
                   *[pltpu.HBM(s.shape, s.dtype) for s in land_shapes], jax.ShapeDtypeStruct((8, 128), F32)),
        in_specs=[_HBM_SPEC] * (ns + nl) + [pl.BlockSpec(memory_space=pl.ANY)],
        out_specs=(_SEM_SPEC, _SEM_SPEC, *[_HBM_SPEC] * nl, pl.BlockSpec(memory_space=pltpu.VMEM)),
        input_output_aliases={ns + i: 2 + i for i in range(nl)},
        compiler_params=pltpu.CompilerParams(has_side_effects=pltpu.SideEffectType.DATAFLOW_SIDE_EFFECTING),
    )(*[_hbm(s) for s in srcs], *[_hbm(lax.empty(s.shape, s.dtype)) for s in land_shapes], after)
    return out[0], out[1], list(out[2:2 + nl]), out[-1]


def _exchange_wait(name, plan, srcs, started, after):
    send_sems, recv_sems, lands, _ = started
    ns, nl = len(srcs), len(lands)

    def body(*refs):
        for send, recv in _planned_copies(plan, refs[:ns], refs[ns:ns + nl], refs[ns + nl], refs[ns + nl + 1]):
            send.wait_send()
            recv.wait_recv()

    return pl.pallas_call(
        body, name=name, out_shape=[pltpu.HBM(l.shape, l.dtype) for l in lands],
        in_specs=[_HBM_SPEC] * (ns + nl) + [_SEM_SPEC, _SEM_SPEC, pl.BlockSpec(memory_space=pl.ANY)],
        out_specs=[_HBM_SPEC] * nl, input_output_aliases={ns + i: i for i in range(nl)},
        compiler_params=pltpu.CompilerParams(has_side_effects=pltpu.SideEffectType.DATAFLOW_SIDE_EFFECTING),
    )(*[_hbm(s) for s in srcs], *lands, send_sems, recv_sems, after)


def _exchange_halves(name, halves, after):
    n = len(halves)

    def body(*refs):
        ins, outs, stage = refs[:n], refs[n + 1:2 * n + 1], refs[2 * n + 1:3 * n + 1]
        send_sems, recv_sems, local_sems = refs[3 * n + 1:]
        x, y, c = _place()
        load = [pltpu.make_async_copy(ins[a], stage[a], local_sems.at[a]) for a in range(n)]
        local = [pltpu.make_async_copy(stage[a], outs[a].at[c], local_sems.at[a]) for a in range(n)]
        remote = [pltpu.make_async_remote_copy(
            src_ref=stage[a], dst_ref=outs[a].at[c], send_sem=send_sems.at[a], recv_sem=recv_sems.at[a],
            device_id=(x, y, 1 - c), device_id_type=MESH) for a in range(n)]
        for cp in load:
            cp.start()
        for a in range(n):
            load[a].wait()
            remote[a].start()
            local[a].start()
        for a in range(n):
            pltpu.make_async_remote_copy(
                src_ref=ins[a], dst_ref=outs[a].at[1 - c], send_sem=send_sems.at[a], recv_sem=recv_sems.at[a],
                device_id=(x, y, 1 - c), device_id_type=MESH).wait_recv()
        for cp in remote:
            cp.wait_send()
        for cp in local:
            cp.wait()

    any_spec = pl.BlockSpec(memory_space=pl.ANY)
    return pl.pallas_call(
        body, name=name,
        out_shape=[jax.ShapeDtypeStruct((2,) + h.shape, h.dtype) for h in halves],
        in_specs=[any_spec] * (n + 1), out_specs=[any_spec] * n,
        scratch_shapes=[pltpu.VMEM(h.shape, h.dtype) for h in halves]
        + [pltpu.SemaphoreType.DMA((n,)), pltpu.SemaphoreType.DMA((n,)), pltpu.SemaphoreType.DMA((n,))],
        compiler_params=pltpu.CompilerParams(vmem_limit_bytes=VMEM_LIMIT_V7X),
    )(*halves, after)


def _like(arrays, lead, dtype=None):
    return [jax.ShapeDtypeStruct(tuple(lead) + a.shape[-2:], dtype or a.dtype) for a in arrays]


class _StepExchanges:
    def __init__(self, mats, conv_w):
        x, y, c = _place()
        self.place = jnp.stack([c, 2 * x + y]).astype(jnp.int32)
        shards = [w.astype(BF16).reshape(2, w.shape[0] // 2, w.shape[1]) for w in mats]
        (w_in,) = _weights_allgather("w_in_allgather", shards[:1])
        self.w_in = w_in.reshape(N_CHIPS, 2 * w_in.shape[2], w_in.shape[3])
        taps = jnp.pad(conv_w, ((0, 8 - conv_w.shape[0]), (0, 128 - conv_w.shape[1])))
        self._rest_shards = shards[1:] + [jnp.stack([taps, jnp.zeros_like(taps)])]
        self._rest = _exchange_start("rest_allgather_start", _plan_first_hop, 3 * len(self._rest_shards),
                                     self._rest_shards, _like(self._rest_shards, (N_CHIPS, 2)), w_in)
        self.zero = self._rest[3]
        self._taps_shape = conv_w.shape
        self._groups = {}

    def rest_weights(self, after):
        landed = _exchange_wait("rest_allgather_wait", _plan_first_hop, self._rest_shards, self._rest, after)
        *mats, taps = _weights_allgather("rest_allgather_finish", self._rest_shards, landed=landed)
        k, w = self._taps_shape
        taps = taps[:, 0, :k, :w].transpose(1, 0, 2).reshape(k, N_CHIPS * w)
        return [g.reshape(N_CHIPS, 2 * g.shape[2], g.shape[3]) for g in mats], taps

    def send_grads(self, key, grads):
        grads = list(grads)
        started = _exchange_start(f"{key}_grads_to_sibling_start", _plan_other_half_to_sibling, len(grads), grads,
                                  _like(grads, (N_CHIPS,)), self.zero)
        self._groups[key] = dict(grads=grads, to_sibling=started)
        self.zero = started[3]

    def grads_at_sibling(self, key, after):
        group = self._groups[key]
        grads = group["grads"]
        group["from_sibling"] = _exchange_wait(f"{key}_grads_to_sibling_wait", _plan_other_half_to_sibling, grads,
                                               group["to_sibling"], after)
        group["partials"] = [_chip_sum_bf16(f"{key}_chip_sum_{a}", grads[a], group["from_sibling"][a], self.place)
                             for a in range(len(grads))]
        group["to_chips"] = _exchange_start(f"{key}_grads_to_chips_start", _plan_to_other_chips, 3 * len(grads),
                                            group["partials"], _like(group["partials"], (3,)), self.zero)
        self.zero = group["to_chips"][3]

    def grads_summed(self, key, after):
        group = self._groups[key]
        from_chips = _exchange_wait(f"{key}_grads_to_chips_wait", _plan_to_other_chips, group["partials"],
                                    group["to_chips"], after)
        return [_final_sum(f"{key}_final_sum_{a}", group["grads"][a], group["from_sibling"][a], from_chips[a], self.place)
                for a in range(len(from_chips))]

    def send_small(self, block):
        self._small = block
        self._small_started = _exchange_start("small_grads_start", _plan_to_all, 7, [block],
                                              [jax.ShapeDtypeStruct((8,) + block.shape, block.dtype)], self.zero)
        self.zero = self._small_started[3]

    def small_summed(self, after):
        x, y, c = _place()
        (landed,) = _exchange_wait("small_grads_wait", _plan_to_all, [self._small], self._small_started, after)
        blocks = lax.dynamic_update_index_in_dim(landed, self._small, 4 * x + 2 * y + c, 0)
        return _sum_blocks("small_sum", blocks)


def _rope_tables(positions):
    half = HEAD // 2
    inv_freq = jnp.float32(ROPE_THETA) ** (-(jnp.arange(half, dtype=F32) * 2.0 / HEAD))
    ang = positions.astype(F32)[:, None] * inv_freq
    cos, sin = jnp.cos(ang), jnp.sin(ang)
    return jnp.tile(cos, (1, 4)), jnp.tile(jnp.concatenate([-sin, sin], axis=1), (1, 2))


def _local_step(x, mem, positions, target, gains, ex):
    g_pre_mix, g_mem, g_a, g_c, g_x, g_post_mix, g_pre_mlp, g_post_mlp = gains
    tm = ROW_TILE
    cos, sin = _rope_tables(positions)
    w_in = ex.w_in

    h, q, k, v, bcu, qx = _in_proj_fwd(x, g_pre_mix + ex.zero[:1, :1], w_in, cos, sin, tm)
    ya, lse = _attn_fwd(q, k, v)
    (w_kv, w_out, w_up, w_down), conv_w = ex.rest_weights(lse)
    w_kv, w_out, w_down = (w.reshape(N_CHIPS * w.shape[1], w.shape[2]) for w in (w_kv, w_out, w_down))
    memn, mkv = _memkv_fwd(mem, g_mem, w_kv)
    yx, ycat, y2, x1 = _mix_fwd(ya, bcu, qx, mkv, conv_w, g_a, g_c, g_x, w_out, g_post_mix, x, tm)
    h2, f, du, df2, dx1, dg_pre_mlp, dg_post_mlp, loss = _mlp_fwd_bwd(x1, target, g_pre_mlp, g_post_mlp, w_up, w_down,
                                                                      MLP_ROW_TILE)
    gw_down = _weight_grad("grad_w_down", f, df2, True)
    gw_up = _weight_grad("grad_w_up", h2, du, False)
    ex.send_grads("early", [gw_up, gw_down])

    dy2, dya, delta, dycx, dg_post_mix, dg_a, dg_c, dg_x = _mix_bwd(dx1, y2, ya, yx, bcu, conv_w, g_a, g_c, g_x,
                                                                  w_out, g_post_mix + ex.zero[:1, :1], tm)
    ex.grads_at_sibling("early", dy2)
    gw_out = _weight_grad("grad_w_out", ycat, dy2, True)
    tail, dmkv, g_conv = _conv_xattn_bwd(dycx, bcu, qx, mkv, conv_w + ex.zero[:1, :1], tm)
    gw_kv, dg_mem = _memkv_bwd(mem, g_mem, w_kv, dmkv)
    ex.send_grads("mid", [gw_out, gw_kv])
    dqkv = _attn_bwd(q, k, v, dya, lse, delta, ex.zero)
    ex.grads_at_sibling("mid", dqkv[0])
    dproj, grad_x, dg_pre_mix = _in_proj_bwd(dqkv, tail, cos, sin, w_in, x, g_pre_mix + ex.zero[:1, :1], dx1, tm)
    gain_grads = [dg_pre_mix, dg_mem, dg_a, dg_c, dg_x, dg_post_mix, dg_pre_mlp, dg_post_mlp]
    ex.send_small(_pack_small(gain_grads, g_conv, loss))
    gw_in = _weight_grad("grad_w_in", h, dproj, False)
    ex.send_grads("late", [gw_in])
    return grad_x


def _pack_small(gains, conv, scalar=None):
    rows = [jnp.pad(g, ((0, 0), (0, D_MODEL - g.shape[1]))) for g in gains]
    rows.append(jnp.pad(conv, ((0, 0), (0, D_MODEL - conv.shape[1]))))
    last = jnp.zeros((SMALL_ROWS - 8 - conv.shape[0], D_MODEL), F32)
    rows.append(last if scalar is None else last.at[0:1, 0:1].set(scalar))
    return jnp.concatenate(rows, axis=0)


def _unpack_small(block, gain_widths, conv_width):
    gains = [block[i:i + 1, :w] for i, w in enumerate(gain_widths)]
    return gains, block[8:11, :conv_width], block[11, 0]


def kernel(x, mem, positions, g_pre_mix, g_mem, w_in, w_mem_kv, conv_w, g_attn_out, g_conv_out, g_xattn_out, w_out, g_post_mix, g_pre_mlp, w_up, w_down, g_post_mlp, loss_target, m_g_pre_mix, m_g_mem, m_w_in, m_w_mem_kv, m_conv_w, m_g_attn_out, m_g_conv_out, m_g_xattn_out, m_w_out, m_g_post_mix, m_g_pre_mlp, m_w_up, m_w_down, m_g_post_mlp, v_g_pre_mix, v_g_mem, v_w_in, v_w_mem_kv, v_conv_w, v_g_attn_out, v_g_conv_out, v_g_xattn_out, v_w_out, v_g_post_mix, v_g_pre_mlp, v_w_up, v_w_down, v_g_post_mlp):
    cx, cy, cc = _place()
    chip = 2 * cx + cy
    gains = [g_pre_mix, g_mem, g_attn_out, g_conv_out, g_xattn_out, g_post_mix, g_pre_mlp, g_post_mlp]
    gains_m = [m_g_pre_mix, m_g_mem, m_g_attn_out, m_g_conv_out, m_g_xattn_out, m_g_post_mix, m_g_pre_mlp, m_g_post_mlp]
    gains_v = [v_g_pre_mix, v_g_mem, v_g_attn_out, v_g_conv_out, v_g_xattn_out, v_g_post_mix, v_g_pre_mlp, v_g_post_mlp]
    gain_widths = [g.shape[1] for g in gains]
    mats = [w_in[0], w_mem_kv[0], w_out[0], w_up[0], w_down[0]]
    mats_m = [m_w_in[0], m_w_mem_kv[0], m_w_out[0], m_w_up[0], m_w_down[0]]
    mats_v = [v_w_in[0], v_w_mem_kv[0], v_w_out[0], v_w_up[0], v_w_down[0]]

    ex = _StepExchanges(mats, conv_w[0])
    grad_x = _local_step(x[0], mem[0], positions[0], loss_target[0], gains, ex)

    both = lambda halves: [t.reshape(2 * t.shape[1], t.shape[2]) for t in halves]
    done = ex.grads_summed("early", ex.zero) + ex.grads_summed("mid", ex.zero)
    ex.grads_at_sibling("late", sum(t[:8, :128] for t in done))
    up_sum, down_sum, out_sum, kv_sum = both(_exchange_halves("sums_to_sibling", done, ex.zero))
    adamw = lambda a, g, after: _adamw(f"adamw_{a}", mats[a], g, mats_m[a], mats_v[a], after)
    new_up, new_down, new_out, new_kv = adamw(3, up_sum, ex.zero), adamw(4, down_sum, ex.zero), adamw(2, out_sum, ex.zero), adamw(1, kv_sum, ex.zero)

    small, total = _small_update(ex.small_summed(new_kv[0]), chip.reshape(1).astype(jnp.int32), gains, gains_m,
                                 gains_v, conv_w[0], m_conv_w[0], v_conv_w[0])

    (in_half,) = ex.grads_summed("late", small[0][1])
    (in_sum,) = both(_exchange_halves("late_sum_to_sibling", [in_half], in_half))
    new_in = adamw(0, in_sum, in_sum)
    mat_sums = [in_sum, kv_sum, out_sum, up_sum, down_sum]
    mat_new = [new_in, new_kv, new_out, new_up, new_down]

    order = ["g_pre_mix", "g_mem", "w_in", "w_mem_kv", "conv_w", "g_attn_out", "g_conv_out", "g_xattn_out", "w_out",
             "g_post_mix", "g_pre_mlp", "w_up", "w_down", "g_post_mlp"]
    gain_names = ["g_pre_mix", "g_mem", "g_attn_out", "g_conv_out", "g_xattn_out", "g_post_mix", "g_pre_mlp", "g_post_mlp"]
    mat_names = ["w_in", "w_mem_kv", "w_out", "w_up", "w_down"]

    def leaf(kind, name):
        if name in gain_names:
            return small[gain_names.index(name)][kind]
        if name == "conv_w":
            return small[len(gain_names)][kind][None]
        a = mat_names.index(name)
        return (mat_sums[a] if kind == 0 else mat_new[a][kind - 1])[None]

    return (total[0, 0], grad_x[None], *[leaf(kind, name) for kind in range(4) for name in order])
```

```python
import jax
import jax.numpy as jnp
from jax import lax
from jax.experimental import pallas as pl
from jax.experimental.pallas import tpu as pltpu

F32, BF16 = jnp.float32, jnp.bfloat16

D_MODEL = 1024
ATTN_W = 512
CONV_W = 256
XATTN_W = 256
PROJ_W = 3 * ATTN_W + 3 * CONV_W + XATTN_W
D_FF = 4096
HEAD = 64
N_BACK = 128
DILATIONS = (1, 4, 16)
ROPE_THETA = 10000.0
EPS = 1e-6
NEG_INF = -1e30
SCALE = HEAD ** -0.5
N_CHIPS = 4
SHARD_IN = PROJ_W // N_CHIPS
SHARD_FF = D_FF // N_CHIPS

ADAM_LR, ADAM_B1, ADAM_B2, ADAM_EPS, ADAM_WD, ADAM_STEP = 0.001, 0.9, 0.999, 1e-08, 0.01, 10

VMEM_LIMIT_V7X = 56 * 1024 * 1024
ROW_TILE = 512
MLP_ROW_TILE = 256
SMALL_ROWS = 16

NT = (((1,), (1,)), ((), ()))
TN = (((0,), (0,)), ((), ()))
MESH = pl.DeviceIdType.MESH


def _params(*sem):
    return pltpu.CompilerParams(dimension_semantics=sem, vmem_limit_bytes=VMEM_LIMIT_V7X)


def _resident(shape):
    return pl.BlockSpec(shape, lambda *_: (0,) * len(shape), pipeline_mode=pl.Buffered(1))


def _rows(tm, width):
    return pl.BlockSpec((tm, width), lambda i: (i, 0))


def _rms_hat(x):
    r = lax.rsqrt(jnp.mean(x * x, axis=-1, keepdims=True) + EPS)
    return x * r, r


def _rms_bwd(xhat, r, g, dy):
    gdy = dy * g
    return r * (gdy - xhat * jnp.mean(xhat * gdy, axis=-1, keepdims=True))


def _rope128(t, cos, sin_signed, inverse):
    lane = lax.broadcasted_iota(jnp.int32, t.shape, 1)
    first_half = (lane % HEAD) < (HEAD // 2)
    rot = jnp.where(first_half, pltpu.roll(t, 128 - HEAD // 2, 1), pltpu.roll(t, HEAD // 2, 1))
    return t * cos - rot * sin_signed if inverse else t * cos + rot * sin_signed


def _pre_norm(x, g, tm):
    S = x.shape[0]

    def body(x_ref, g_ref, h_ref):
        h_ref[...] = (_rms_hat(x_ref[...])[0] * g_ref[...]).astype(BF16)

    return pl.pallas_call(
        body, name="pre_norm", grid=(S // tm,), in_specs=[_rows(tm, D_MODEL), _resident((1, D_MODEL))],
        out_specs=_rows(tm, D_MODEL), out_shape=jax.ShapeDtypeStruct((S, D_MODEL), BF16),
        compiler_params=_params("parallel"),
    )(x, g)


def _in_proj_fwd(h, w_in, cos, sin, after, tm):
    S = h.shape[0]

    def body(h_ref, w_ref, cos_ref, sin_ref, after_ref, q_ref, k_ref, v_ref, bcu_ref, qx_ref, proj):
        h = h_ref[...]
        for j in range(N_CHIPS):
            proj[:, SHARD_IN * j:SHARD_IN * (j + 1)] = jnp.dot(h, w_ref[j], preferred_element_type=F32)
        c, s = cos_ref[...], sin_ref[...]
        for j in range(ATTN_W // 128):
            lo = 128 * j
            q_ref[:, lo:lo + 128] = _rope128(proj[:, lo:lo + 128], c, s, False) * SCALE
            k_ref[:, lo:lo + 128] = _rope128(proj[:, ATTN_W + lo:ATTN_W + lo + 128], c, s, False)
        v_ref[...] = proj[:, 2 * ATTN_W:3 * ATTN_W]
        bcu_ref[...] = proj[:, 3 * ATTN_W:3 * ATTN_W + 3 * CONV_W]
        qx_ref[...] = proj[:, 3 * ATTN_W + 3 * CONV_W:PROJ_W].astype(BF16)

    return pl.pallas_call(
        body, name="in_proj_fwd", grid=(S // tm,),
        in_specs=[_rows(tm, D_MODEL), _resident((N_CHIPS, D_MODEL, SHARD_IN)), _rows(tm, 128), _rows(tm, 128),
                  pl.BlockSpec(memory_space=pl.ANY)],
        out_specs=[_rows(tm, ATTN_W), _rows(tm, ATTN_W), _rows(tm, ATTN_W), _rows(tm, 3 * CONV_W), _rows(tm, XATTN_W)],
        out_shape=[jax.ShapeDtypeStruct((S, ATTN_W), F32), jax.ShapeDtypeStruct((S, ATTN_W), F32),
                   jax.ShapeDtypeStruct((S, ATTN_W), F32), jax.ShapeDtypeStruct((S, 3 * CONV_W), F32),
                   jax.ShapeDtypeStruct((S, XATTN_W), BF16)],
        scratch_shapes=[pltpu.VMEM((tm, PROJ_W), F32)],
        compiler_params=_params("parallel"),
    )(h, w_in, cos, sin, after)


def _memkv_fwd(mem, g_mem, w_kv):
    n_mem = mem.shape[0]

    def body(mem_ref, g_ref, w_ref, mn_ref, kv_ref):
        mhat, _ = _rms_hat(mem_ref[...])
        mn = (mhat * g_ref[...]).astype(BF16)
        mn_ref[...] = mn
        kv_ref[...] = jnp.dot(mn, w_ref[...], preferred_element_type=F32).astype(BF16)

    return pl.pallas_call(
        body, name="memkv_fwd",
        out_shape=[jax.ShapeDtypeStruct((n_mem, D_MODEL), BF16), jax.ShapeDtypeStruct((n_mem, 2 * XATTN_W), BF16)],
        compiler_params=pltpu.CompilerParams(vmem_limit_bytes=VMEM_LIMIT_V7X),
    )(mem, g_mem, w_kv)


def _fill_band_bias(bias):
    row = lax.broadcasted_iota(jnp.int32, (N_BACK, 2 * N_BACK), 0)
    col = lax.broadcasted_iota(jnp.int32, (N_BACK, 2 * N_BACK), 1)
    band = (col >= row) & (col <= row + N_BACK)
    bias[1] = jnp.where(band, 0.0, NEG_INF)
    bias[0] = jnp.where(band & (col >= N_BACK), 0.0, NEG_INF)


def _strided(start, size, d):
    return pl.ds(start, size) if d == 1 else pl.ds(start, size, stride=d)


def _group_starts(g, G, nb, d):
    t0 = g * G
    r, n0 = lax.shift_right_logical(t0, nb.bit_length() - 1), lax.bitwise_and(t0, nb - 1)
    first = r + n0 * (N_BACK * d)
    before = r + jnp.maximum(n0 - 1, 0) * (N_BACK * d)
    starts = [before] + [first + u * (N_BACK * d) for u in range(G)]
    if d == 1:
        starts = [pl.multiple_of(st, N_BACK) for st in starts]
    return starts, n0


def _step_blocks(i, U, nb, d):
    G = min(U, nb)
    row_blocks, blocks = [], []
    for grp in range(U // G):
        starts, n0 = _group_starts(i * (U // G) + grp, G, nb, d)
        base = len(row_blocks)
        row_blocks += [_strided(st, N_BACK, d) for st in starts]
        for u in range(G):
            blocks.append((base + u, base + u + 1, jnp.minimum(n0, 1) if u == 0 else 1))
    return row_blocks, blocks


def _by_head(a, b):
    lane = lax.broadcasted_iota(jnp.int32, (a.shape[0], 2 * HEAD), 1)
    return jnp.where(lane < HEAD, a, b)


def _head_only(t, hh):
    lane = lax.broadcasted_iota(jnp.int32, t.shape, 1)
    return jnp.where((lane < HEAD) == (hh == 0), t, jnp.zeros_like(t))


def _stack_heads(t):
    return jnp.concatenate([_head_only(t, 0), _head_only(t, 1)], axis=0)


def _head_columns(t):
    return jnp.concatenate([t[:, 0:1], t[:, HEAD:HEAD + 1]], axis=0)


def _unstack(t):
    return _by_head(t[:N_BACK], t[N_BACK:])


def _unstack_columns(t):
    return _by_head(jnp.broadcast_to(t[:N_BACK], (N_BACK, 2 * HEAD)), jnp.broadcast_to(t[N_BACK:], (N_BACK, 2 * HEAD)))


FWD_BLOCKS_PER_STEP = 4
BWD_BLOCKS_PER_STEP = 2


def _attn_fwd(q, k, v):
    S = q.shape[0]
    U = FWD_BLOCKS_PER_STEP

    def body(q_ref, k_ref, v_ref, y_ref, m_ref, l_scr, bias):
        _fill_band_bias(bias)
        for g, d in enumerate(DILATIONS):
            nb = S // d // N_BACK
            first_pattern, last_pattern = g == 0, g == len(DILATIONS) - 1

            def step(i, carry, d=d, nb=nb, first_pattern=first_pattern, last_pattern=last_pattern):
                row_blocks, blocks = _step_blocks(i, U, nb, d)
                kb = [k_ref[r, :].astype(BF16) for r in row_blocks]
                ss = []
                for before, own, which in blocks:
                    kw = jnp.concatenate([kb[before], kb[own]], 0)
                    qs = _stack_heads(q_ref[row_blocks[own], :].astype(BF16))
                    b = bias[which]
                    ss.append(lax.dot_general(qs, kw, NT, preferred_element_type=F32) + jnp.concatenate([b, b], axis=0))
                ms = [jnp.max(s, axis=1, keepdims=True) for s in ss]
                ps = [jnp.exp(s - m) for s, m in zip(ss, ms)]
                ls = [jnp.sum(p, axis=1, keepdims=True) for p in ps]
                vb = [v_ref[r, :].astype(BF16) for r in row_blocks]
                os_ = [jnp.dot(ps[u].astype(BF16), jnp.concatenate([vb[before], vb[own]], 0), preferred_element_type=F32)
                       for u, (before, own, _) in enumerate(blocks)]
                for u, (_, own, _) in enumerate(blocks):
                    o_g, m_g, l_g = _unstack(os_[u]), _unstack_columns(ms[u]), _unstack_columns(ls[u])
                    r = row_blocks[own]
                    if first_pattern:
                        m_new, l_new, acc = m_g, l_g, o_g
                    else:
                        m_old = m_ref[r, :]
                        m_new = jnp.maximum(m_old, m_g)
                        alpha, beta = jnp.exp(m_old - m_new), jnp.exp(m_g - m_new)
                        l_new = l_scr[r, :] * alpha + l_g * beta
                        acc = y_ref[r, :] * alpha + o_g * beta
                    if last_pattern:
                        y_ref[r, :] = acc / l_new
                        m_ref[r, :] = m_new + jnp.log(l_new)
                    else:
                        y_ref[r, :] = acc
                        m_ref[r, :] = m_new
                        l_scr[r, :] = l_new
                return carry

            lax.fori_loop(0, d * nb // U, step, 0)

    col = pl.BlockSpec((S, 2 * HEAD), lambda j: (0, j))
    return pl.pallas_call(
        body, name="attn_fwd", grid=(q.shape[1] // (2 * HEAD),),
        in_specs=[col, col, col], out_specs=[col, col],
        out_shape=[jax.ShapeDtypeStruct(q.shape, F32)] * 2,
        scratch_shapes=[pltpu.VMEM((S, 2 * HEAD), F32), pltpu.VMEM((2, N_BACK, 2 * N_BACK), F32)],
        compiler_params=_params("parallel"),
    )(q, k, v)


def _attn_bwd(q, k, v, dy, lse, delta, after):
    S = q.shape[0]
    U = BWD_BLOCKS_PER_STEP

    def body(q_ref, k_ref, v_ref, dy_ref, lse_ref, delta_ref, after_ref, dq_ref, dk_ref, dv_ref, bias):
        _fill_band_bias(bias)
        dk_ref[...] = jnp.zeros_like(dk_ref)
        dv_ref[...] = jnp.zeros_like(dv_ref)
        for g, d in enumerate(DILATIONS):
            nb = S // d // N_BACK

            def step(i, carry, d=d, nb=nb, g=g):
                row_blocks, blocks = _step_blocks(i, U, nb, d)
                kb = [k_ref[r, :].astype(BF16) for r in row_blocks]
                vb = [v_ref[r, :].astype(BF16) for r in row_blocks]
                kws = [jnp.concatenate([kb[before], kb[own]], 0) for before, own, _ in blocks]
                vws = [jnp.concatenate([vb[before], vb[own]], 0) for before, own, _ in blocks]
                qss = [_stack_heads(q_ref[row_blocks[own], :].astype(BF16)) for _, own, _ in blocks]
                doss = [_stack_heads(dy_ref[row_blocks[own], :].astype(BF16)) for _, own, _ in blocks]
                ss, dps = [], []
                for u, (_, _, which) in enumerate(blocks):
                    b = bias[which]
                    ss.append(lax.dot_general(qss[u], kws[u], NT, preferred_element_type=F32) + jnp.concatenate([b, b], axis=0))
                    dps.append(lax.dot_general(doss[u], vws[u], NT, preferred_element_type=F32))
                ps = [jnp.exp(ss[u] - _head_columns(lse_ref[row_blocks[own], :])) for u, (_, own, _) in enumerate(blocks)]
                dss = [(ps[u] * (dps[u] - _head_columns(delta_ref[row_blocks[own], :]))).astype(BF16)
                       for u, (_, own, _) in enumerate(blocks)]
                pbs = [p.astype(BF16) for p in ps]
                dqs = [jnp.dot(dss[u], kws[u], preferred_element_type=F32) for u in range(U)]
                dkws = [lax.dot_general(dss[u], qss[u], TN, preferred_element_type=F32) for u in range(U)]
                dvws = [lax.dot_general(pbs[u], doss[u], TN, preferred_element_type=F32) for u in range(U)]
                dk_parts, dv_parts = [None] * len(row_blocks), [None] * len(row_blocks)
                for u, (before, own, _) in enumerate(blocks):
                    dq = _unstack(dqs[u])
                    if g == 0:
                        dq_ref[row_blocks[own], :] = dq
                    else:
                        dq_ref[row_blocks[own], :] += dq
                    for idx, dkp, dvp in ((before, dkws[u][:N_BACK], dvws[u][:N_BACK]),
                                          (own, dkws[u][N_BACK:], dvws[u][N_BACK:])):
                        dk_parts[idx] = dkp if dk_parts[idx] is None else dk_parts[idx] + dkp
                        dv_parts[idx] = dvp if dv_parts[idx] is None else dv_parts[idx] + dvp
                for idx, r in enumerate(row_blocks):
                    dk_ref[r, :] += dk_parts[idx]
                    dv_ref[r, :] += dv_parts[idx]
                return carry

            lax.fori_loop(0, d * nb // U, step, 0)

    col = pl.BlockSpec((S, 2 * HEAD), lambda j: (0, j))
    return pl.pallas_call(
        body, name="attn_bwd", grid=(q.shape[1] // (2 * HEAD),),
        in_specs=[col] * 6 + [pl.BlockSpec(memory_space=pl.ANY)], out_specs=[col] * 3,
        out_shape=[jax.ShapeDtypeStruct(q.shape, F32)] * 3,
        scratch_shapes=[pltpu.VMEM((2, N_BACK, 2 * N_BACK), F32)],
        compiler_params=_params("parallel"),
    )(q, k, v, dy, lse, delta, after)


def _shift_down(z, before, k):
    row = lax.broadcasted_iota(jnp.int32, z.shape, 0)
    out = pltpu.roll(z, k, 0)
    for i in range(k):
        out = jnp.where(row == i, before[8 - k + i:8 - k + i + 1, :], out)
    return out


def _shift_up(z, after, k):
    rows = z.shape[0]
    row = lax.broadcasted_iota(jnp.int32, z.shape, 0)
    out = pltpu.roll(z, rows - k, 0)
    for i in range(k):
        out = jnp.where(row == rows - k + i, after[i:i + 1, :], out)
    return out


def _conv_fwd(bcu, before, is_first, w):
    b, c, u = bcu[:, 0:CONV_W], bcu[:, CONV_W:2 * CONV_W], bcu[:, 2 * CONV_W:3 * CONV_W]
    z = c * u
    zb = jnp.where(is_first, 0.0, before[:, CONV_W:2 * CONV_W] * before[:, 2 * CONV_W:3 * CONV_W])
    z1, z2 = _shift_down(z, zb, 1), _shift_down(z, zb, 2)
    cv = w[0:1, :] * z2 + w[1:2, :] * z1 + w[2:3, :] * z
    return b, c, u, z, z1, z2, cv


def _halo_before(tm, width):
    return pl.BlockSpec((8, width), lambda i: (jnp.maximum(i * (tm // 8) - 1, 0), 0))


def _halo_after(tm, width, S):
    return pl.BlockSpec((8, width), lambda i: (jnp.minimum((i + 1) * (tm // 8), S // 8 - 1), 0))


def _mix_fwd(ya, bcu, qx, mkv, conv_w, g_a, g_c, g_x, w_out, g_post, x, tm):
    S = x.shape[0]

    def body(ya_ref, bcu_ref, before_ref, qx_ref, mkv_ref, cw_ref, ga_ref, gc_ref, gx_ref,
             wo_ref, gp_ref, x_ref, yx_ref, ycat_ref, y2_ref, x1_ref):
        ya = ya_ref[...]
        b, _, _, _, _, _, cv = _conv_fwd(bcu_ref[...], before_ref[...], pl.program_id(0) == 0, cw_ref[...])
        yc = b * cv

        qxb, mkvb = qx_ref[...], mkv_ref[...]
        for hd in range(XATTN_W // HEAD):
            sl = slice(HEAD * hd, HEAD * (hd + 1))
            s = lax.dot_general(qxb[:, sl], mkvb[:, sl], NT, preferred_element_type=F32) * SCALE
            mx = jnp.max(s, axis=1, keepdims=True)
            p = jnp.exp(s - mx)
            l = jnp.sum(p, axis=1, keepdims=True)
            vm = mkvb[:, XATTN_W + HEAD * hd:XATTN_W + HEAD * (hd + 1)]
            yx_ref[:, sl] = jnp.dot(p.astype(BF16), vm, preferred_element_type=F32) / l
        yx = yx_ref[...]

        ycat_ref[:, 0:ATTN_W] = (_rms_hat(ya)[0] * ga_ref[...]).astype(BF16)
        ycat_ref[:, ATTN_W:ATTN_W + CONV_W] = (_rms_hat(yc)[0] * gc_ref[...]).astype(BF16)
        ycat_ref[:, ATTN_W + CONV_W:D_MODEL] = (_rms_hat(yx)[0] * gx_ref[...]).astype(BF16)
        y2 = jnp.dot(ycat_ref[...], wo_ref[...], preferred_element_type=F32)
        y2_ref[...] = y2
        x1_ref[...] = x_ref[...] + _rms_hat(y2)[0] * gp_ref[...]

    n_mem = mkv.shape[0]
    return pl.pallas_call(
        body, name="mix_fwd", grid=(S // tm,),
        in_specs=[_rows(tm, ATTN_W), _rows(tm, 3 * CONV_W), _halo_before(tm, 3 * CONV_W), _rows(tm, XATTN_W),
                  _resident((n_mem, 2 * XATTN_W)), _resident((3, CONV_W)), _resident((1, ATTN_W)),
                  _resident((1, CONV_W)), _resident((1, XATTN_W)), _resident((D_MODEL, D_MODEL)),
                  _resident((1, D_MODEL)), _rows(tm, D_MODEL)],
        out_specs=[_rows(tm, XATTN_W), _rows(tm, D_MODEL), _rows(tm, D_MODEL), _rows(tm, D_MODEL)],
        out_shape=[jax.ShapeDtypeStruct((S, XATTN_W), F32), jax.ShapeDtypeStruct((S, D_MODEL), BF16),
                   jax.ShapeDtypeStruct((S, D_MODEL), F32), jax.ShapeDtypeStruct((S, D_MODEL), F32)],
        compiler_params=_params("parallel"),
    )(ya, bcu, bcu, qx, mkv, conv_w, g_a, g_c, g_x, w_out, g_post, x)


def _mlp_fwd_bwd(x1, target, g_pre, g_post, w_up, w_down, tm):
    S = x1.shape[0]
    n_ff = D_FF // SHARD_FF

    def body(x1_ref, t_ref, gpre_ref, gpost_ref, wup_ref, wdn_ref,
             h2_ref, f_ref, du_ref, df2_ref, dx1_ref, dgpre_ref, dgpost_ref, loss_ref, u_scr):
        @pl.when(pl.program_id(0) == 0)
        def _():
            dgpre_ref[...] = jnp.zeros_like(dgpre_ref)
            dgpost_ref[...] = jnp.zeros_like(dgpost_ref)
            loss_ref[...] = jnp.zeros_like(loss_ref)

        x1 = x1_ref[...]
        x1hat, r1 = _rms_hat(x1)
        h2 = (x1hat * gpre_ref[...]).astype(BF16)
        h2_ref[...] = h2
        f2 = jnp.zeros((tm, D_MODEL), F32)
        for j in range(n_ff):
            cols = slice(SHARD_FF * j, SHARD_FF * (j + 1))
            u = jnp.maximum(jnp.dot(h2, wup_ref[j], preferred_element_type=F32), 0.0)
            u_scr[:, cols] = u
            f = (u * u).astype(BF16)
            f_ref[:, cols] = f
            f2 = f2 + jnp.dot(f, wdn_ref[cols, :], preferred_element_type=F32)
        f2hat, r2 = _rms_hat(f2)
        err = x1 + f2hat * gpost_ref[...] - t_ref[...]
        loss_ref[...] += 0.5 * jnp.sum(jnp.mean(err * err, axis=-1, keepdims=True), axis=0, keepdims=True)
        dx2 = err * (1.0 / D_MODEL)
        dgpost_ref[...] += jnp.sum(dx2 * f2hat, axis=0, keepdims=True)
        df2 = _rms_bwd(f2hat, r2, gpost_ref[...], dx2).astype(BF16)
        df2_ref[...] = df2
        dh2 = jnp.zeros((tm, D_MODEL), F32)
        for j in range(n_ff):
            cols = slice(SHARD_FF * j, SHARD_FF * (j + 1))
            df = lax.dot_general(df2, wdn_ref[cols, :], NT, preferred_element_type=F32)
            du = (2.0 * u_scr[:, cols] * df).astype(BF16)
            du_ref[:, cols] = du
            dh2 = dh2 + lax.dot_general(du, wup_ref[j], NT, preferred_element_type=F32)
        dgpre_ref[...] += jnp.sum(dh2 * x1hat, axis=0, keepdims=True)
        dx1_ref[...] = dx2 + _rms_bwd(x1hat, r1, gpre_ref[...], dh2)

    acc = pl.BlockSpec((1, D_MODEL), lambda i: (0, 0))
    return pl.pallas_call(
        body, name="mlp_fwd_bwd", grid=(S // tm,),
        in_specs=[_rows(tm, D_MODEL), _rows(tm, D_MODEL), _resident((1, D_MODEL)), _resident((1, D_MODEL)),
                  _resident((n_ff, D_MODEL, SHARD_FF)), _resident((D_FF, D_MODEL))],
        out_specs=[_rows(tm, D_MODEL), _rows(tm, D_FF), _rows(tm, D_FF), _rows(tm, D_MODEL), _rows(tm, D_MODEL),
                   acc, acc, pl.BlockSpec((1, 1), lambda i: (0, 0))],
        out_shape=[jax.ShapeDtypeStruct((S, D_MODEL), BF16), jax.ShapeDtypeStruct((S, D_FF), BF16),
                   jax.ShapeDtypeStruct((S, D_FF), BF16), jax.ShapeDtypeStruct((S, D_MODEL), BF16),
                   jax.ShapeDtypeStruct((S, D_MODEL), F32), jax.ShapeDtypeStruct((1, D_MODEL), F32),
                   jax.ShapeDtypeStruct((1, D_MODEL), F32), jax.ShapeDtypeStruct((1, 1), F32)],
        scratch_shapes=[pltpu.VMEM((tm, D_FF), F32)],
        compiler_params=_params("arbitrary"),
    )(x1, target, g_pre, g_post, w_up, w_down)


def _weight_grad(name, a, b, rows_sharded):
    S, K = a.shape
    N = b.shape[1]
    if rows_sharded:
        tk, tn = K // N_CHIPS, N
        a_spec = pl.BlockSpec((S, tk), lambda j: (0, j))
        b_spec = pl.BlockSpec((S, tn), lambda j: (0, 0), pipeline_mode=pl.Buffered(1))
    else:
        tk, tn = K, N // N_CHIPS
        a_spec = pl.BlockSpec((S, tk), lambda j: (0, 0), pipeline_mode=pl.Buffered(1))
        b_spec = pl.BlockSpec((S, tn), lambda j: (0, j))
    half = tk // 2

    def body(a_ref, b_ref, o_ref):
        res = lax.dot_general(a_ref[...], b_ref[...], TN, preferred_element_type=F32)
        o_ref[0, 0] = res[:half]
        o_ref[1, 0] = res[half:]

    return pl.pallas_call(
        body, name=name, grid=(N_CHIPS,), in_specs=[a_spec, b_spec],
        out_specs=pl.BlockSpec((2, 1, half, tn), lambda j: (0, j, 0, 0)),
        out_shape=jax.ShapeDtypeStruct((2, N_CHIPS, half, tn), F32),
        compiler_params=_params("parallel"),
    )(a, b)


def _mix_bwd(dx1, y2, ya, yx, bcu, conv_w, g_a, g_c, g_x, w_out, g_post, tm):
    S = dx1.shape[0]

    def body(dx1_ref, y2_ref, ya_ref, yx_ref, bcu_ref, before_ref, cw_ref, ga_ref, gc_ref, gx_ref, wo_ref, gp_ref,
             dy2_ref, dya_ref, delta_ref, dycx_ref, dgp_ref, dga_ref, dgc_ref, dgx_ref):
        @pl.when(pl.program_id(0) == 0)
        def _():
            for ref in (dgp_ref, dga_ref, dgc_ref, dgx_ref):
                ref[...] = jnp.zeros_like(ref)

        dx1 = dx1_ref[...]
        y2hat, r2 = _rms_hat(y2_ref[...])
        dgp_ref[...] += jnp.sum(dx1 * y2hat, axis=0, keepdims=True)
        dy2 = _rms_bwd(y2hat, r2, gp_ref[...], dx1).astype(BF16)
        dy2_ref[...] = dy2
        dycat = lax.dot_general(dy2, wo_ref[...], NT, preferred_element_type=F32)

        d_na = dycat[:, 0:ATTN_W]
        ya = ya_ref[...]
        yahat, ra = _rms_hat(ya)
        dga_ref[...] += jnp.sum(d_na * yahat, axis=0, keepdims=True)
        dya = _rms_bwd(yahat, ra, ga_ref[...], d_na)
        dya_ref[...] = dya
        prod = dya * ya
        hi = prod.astype(BF16)
        lo = (prod - hi.astype(F32)).astype(BF16)
        head_of = lambda axis: lax.shift_right_logical(lax.broadcasted_iota(jnp.int32, (ATTN_W, ATTN_W), axis),
                                                       HEAD.bit_length() - 1)
        same_head = head_of(0) == head_of(1)
        ones = jnp.where(same_head, 1.0, 0.0).astype(BF16)
        delta_ref[...] = (jnp.dot(hi, ones, preferred_element_type=F32) + jnp.dot(lo, ones, preferred_element_type=F32))

        b, _, _, _, _, _, cv = _conv_fwd(bcu_ref[...], before_ref[...], pl.program_id(0) == 0, cw_ref[...])
        d_nc = dycat[:, ATTN_W:ATTN_W + CONV_W]
        ychat, rc = _rms_hat(b * cv)
        dgc_ref[...] += jnp.sum(d_nc * ychat, axis=0, keepdims=True)
        dycx_ref[:, 0:CONV_W] = _rms_bwd(ychat, rc, gc_ref[...], d_nc)

        d_nx = dycat[:, ATTN_W + CONV_W:D_MODEL]
        yxhat, rx = _rms_hat(yx_ref[...])
        dgx_ref[...] += jnp.sum(d_nx * yxhat, axis=0, keepdims=True)
        dycx_ref[:, CONV_W:CONV_W + XATTN_W] = _rms_bwd(yxhat, rx, gx_ref[...], d_nx)

    acc = lambda w: pl.BlockSpec((1, w), lambda i: (0, 0))
    return pl.pallas_call(
        body, name="mix_bwd", grid=(S // tm,),
        in_specs=[_rows(tm, D_MODEL), _rows(tm, D_MODEL), _rows(tm, ATTN_W), _rows(tm, XATTN_W),
                  _rows(tm, 3 * CONV_W), _halo_before(tm, 3 * CONV_W), _resident((3, CONV_W)),
                  _resident((1, ATTN_W)), _resident((1, CONV_W)), _resident((1, XATTN_W)),
                  _resident((D_MODEL, D_MODEL)), _resident((1, D_MODEL))],
        out_specs=[_rows(tm, D_MODEL), _rows(tm, ATTN_W), _rows(tm, ATTN_W), _rows(tm, CONV_W + XATTN_W),
                   acc(D_MODEL), acc(ATTN_W), acc(CONV_W), acc(XATTN_W)],
        out_shape=[jax.ShapeDtypeStruct((S, D_MODEL), BF16), jax.ShapeDtypeStruct((S, ATTN_W), F32),
                   jax.ShapeDtypeStruct((S, ATTN_W), F32),
                   jax.ShapeDtypeStruct((S, CONV_W + XATTN_W), F32), jax.ShapeDtypeStruct((1, D_MODEL), F32),
                   jax.ShapeDtypeStruct((1, ATTN_W), F32), jax.ShapeDtypeStruct((1, CONV_W), F32),
                   jax.ShapeDtypeStruct((1, XATTN_W), F32)],
        compiler_params=_params("arbitrary"),
    )(dx1, y2, ya, yx, bcu, bcu, conv_w, g_a, g_c, g_x, w_out, g_post)


def _conv_xattn_bwd(dycx, bcu, qx, mkv, conv_w, tm):
    S = dycx.shape[0]
    n_mem = mkv.shape[0]
    n_tiles = S // tm

    def body(d_ref, dafter_ref, bcu_ref, before_ref, after_ref, qx_ref, mkv_ref, cw_ref,
             tail_ref, dmkv_ref, dcw_ref):
        i = pl.program_id(0)

        @pl.when(i == 0)
        def _():
            dmkv_ref[...] = jnp.zeros_like(dmkv_ref)
            dcw_ref[...] = jnp.zeros_like(dcw_ref)

        w = cw_ref[...]
        b, c, u, z, z1, z2, cv = _conv_fwd(bcu_ref[...], before_ref[...], i == 0, w)
        dyc = d_ref[:, 0:CONV_W]
        dcv = dyc * b
        dcv_after = jnp.where(i == n_tiles - 1, 0.0, dafter_ref[:, 0:CONV_W] * after_ref[:, 0:CONV_W])
        dz = w[2:3, :] * dcv + w[1:2, :] * _shift_up(dcv, dcv_after, 1) + w[0:1, :] * _shift_up(dcv, dcv_after, 2)
        dcw_ref[0:1, :] += jnp.sum(dcv * z2, axis=0, keepdims=True)
        dcw_ref[1:2, :] += jnp.sum(dcv * z1, axis=0, keepdims=True)
        dcw_ref[2:3, :] += jnp.sum(dcv * z, axis=0, keepdims=True)
        tail_ref[:, 0:CONV_W] = (dyc * cv).astype(BF16)
        tail_ref[:, CONV_W:2 * CONV_W] = (dz * u).astype(BF16)
        tail_ref[:, 2 * CONV_W:3 * CONV_W] = (dz * c).astype(BF16)

        qxb, mkvb = qx_ref[...], mkv_ref[...]
        for hd in range(XATTN_W // HEAD):
            sl = slice(HEAD * hd, HEAD * (hd + 1))
            vsl = slice(XATTN_W + HEAD * hd, XATTN_W + HEAD * (hd + 1))
            s = lax.dot_general(qxb[:, sl], mkvb[:, sl], NT, preferred_element_type=F32) * SCALE
            e = jnp.exp(s - jnp.max(s, axis=1, keepdims=True))
            p = e / jnp.sum(e, axis=1, keepdims=True)
            dob = d_ref[:, CONV_W + HEAD * hd:CONV_W + HEAD * (hd + 1)].astype(BF16)
            dp = lax.dot_general(dob, mkvb[:, vsl], NT, preferred_element_type=F32)
            ds = (p * (dp - jnp.sum(p * dp, axis=1, keepdims=True)) * SCALE).astype(BF16)
            tail_ref[:, 3 * CONV_W + HEAD * hd:3 * CONV_W + HEAD * (hd + 1)] = jnp.dot(
                ds, mkvb[:, sl], preferred_element_type=F32).astype(BF16)
            dmkv_ref[:, sl] += lax.dot_general(ds, qxb[:, sl], TN, preferred_element_type=F32)
            dmkv_ref[:, vsl] += lax.dot_general(p.astype(BF16), dob, TN, preferred_element_type=F32)

    width = CONV_W + XATTN_W
    return pl.pallas_call(
        body, name="conv_xattn_bwd", grid=(n_tiles,),
        in_specs=[_rows(tm, width), _halo_after(tm, width, S), _rows(tm, 3 * CONV_W), _halo_before(tm, 3 * CONV_W),
                  _halo_after(tm, 3 * CONV_W, S), _rows(tm, XATTN_W), _resident((n_mem, 2 * XATTN_W)),
                  _resident((3, CONV_W))],
        out_specs=[_rows(tm, 3 * CONV_W + XATTN_W), pl.BlockSpec((n_mem, 2 * XATTN_W), lambda i: (0, 0)),
                   pl.BlockSpec((3, CONV_W), lambda i: (0, 0))],
        out_shape=[jax.ShapeDtypeStruct((S, 3 * CONV_W + XATTN_W), BF16),
                   jax.ShapeDtypeStruct((n_mem, 2 * XATTN_W), F32), jax.ShapeDtypeStruct((3, CONV_W), F32)],
        compiler_params=_params("arbitrary"),
    )(dycx, dycx, bcu, bcu, bcu, qx, mkv, conv_w)


def _memkv_bwd(mem, g_mem, w_kv, dmkv):
    n_mem = mem.shape[0]
    half = D_MODEL // N_CHIPS // 2

    def body(mem_ref, g_ref, w_ref, d_ref, dw_ref, dg_ref):
        mhat, _ = _rms_hat(mem_ref[...])
        mn = (mhat * g_ref[...]).astype(BF16)
        d = d_ref[...].astype(BF16)
        for k in range(2 * N_CHIPS):
            dw_ref[k % 2, k // 2] = lax.dot_general(mn[:, half * k:half * (k + 1)], d, TN, preferred_element_type=F32)
        dmn = lax.dot_general(d, w_ref[...], NT, preferred_element_type=F32)
        dg_ref[...] = jnp.sum(dmn * mhat, axis=0, keepdims=True)

    return pl.pallas_call(
        body, name="memkv_bwd",
        out_shape=[jax.ShapeDtypeStruct((2, N_CHIPS, half, 2 * XATTN_W), F32), jax.ShapeDtypeStruct((1, D_MODEL), F32)],
        compiler_params=pltpu.CompilerParams(vmem_limit_bytes=VMEM_LIMIT_V7X),
    )(mem, g_mem, w_kv, dmkv)


def _in_proj_bwd(dqkv, tail, cos, sin, w_in, x, g, dx1, tm):
    S = x.shape[0]

    def body(dq_ref, dk_ref, dv_ref, tail_ref, cos_ref, sin_ref, w_ref, x_ref, g_ref, dx1_ref, dproj_ref, dx_ref, dg_ref):
        @pl.when(pl.program_id(0) == 0)
        def _():
            dg_ref[...] = jnp.zeros_like(dg_ref)

        c, s = cos_ref[...], sin_ref[...]
        for j in range(ATTN_W // 128):
            cols = slice(128 * j, 128 * (j + 1))
            dproj_ref[:, cols] = _rope128(dq_ref[:, cols] * SCALE, c, s, True).astype(BF16)
            dproj_ref[:, ATTN_W + 128 * j:ATTN_W + 128 * (j + 1)] = _rope128(dk_ref[:, cols], c, s, True).astype(BF16)
        dproj_ref[:, 2 * ATTN_W:3 * ATTN_W] = dv_ref[...].astype(BF16)
        dproj_ref[:, 3 * ATTN_W:PROJ_W] = tail_ref[...]
        dh = jnp.zeros((tm, D_MODEL), F32)
        for j in range(N_CHIPS):
            dh = dh + lax.dot_general(dproj_ref[:, SHARD_IN * j:SHARD_IN * (j + 1)], w_ref[j], NT,
                                      preferred_element_type=F32)
        xhat, r = _rms_hat(x_ref[...])
        dg_ref[...] += jnp.sum(dh * xhat, axis=0, keepdims=True)
        dx_ref[...] = dx1_ref[...] + _rms_bwd(xhat, r, g_ref[...], dh)

    return pl.pallas_call(
        body, name="in_proj_bwd", grid=(S // tm,),
        in_specs=[_rows(tm, ATTN_W)] * 3 + [_rows(tm, PROJ_W - 3 * ATTN_W), _rows(tm, 128), _rows(tm, 128),
                  _resident((N_CHIPS, D_MODEL, SHARD_IN)), _rows(tm, D_MODEL), _resident((1, D_MODEL)),
                  _rows(tm, D_MODEL)],
        out_specs=[_rows(tm, PROJ_W), _rows(tm, D_MODEL), pl.BlockSpec((1, D_MODEL), lambda i: (0, 0))],
        out_shape=[jax.ShapeDtypeStruct((S, PROJ_W), BF16), jax.ShapeDtypeStruct((S, D_MODEL), F32),
                   jax.ShapeDtypeStruct((1, D_MODEL), F32)],
        compiler_params=_params("arbitrary"),
    )(*dqkv, tail, cos, sin, w_in, x, g, dx1)


def _row_tile(rows):
    return ROW_TILE if rows % ROW_TILE == 0 else rows


def _chip_sum_bf16(name, grad, from_sibling, place):
    _, n, rows, cols = grad.shape
    tr = _row_tile(rows)

    def body(place_ref, g_ref, b_ref, o_ref):
        o_ref[...] = (g_ref[0] + b_ref[...]).astype(BF16)

    spec = pl.BlockSpec((1, tr, cols), lambda s, i, p: (s, i, 0))
    return pl.pallas_call(
        body, name=name, out_shape=jax.ShapeDtypeStruct((n, rows, cols), BF16),
        grid_spec=pltpu.PrefetchScalarGridSpec(
            num_scalar_prefetch=1, grid=(n, rows // tr),
            in_specs=[pl.BlockSpec((1, 1, tr, cols), lambda s, i, p: (p[0], s, i, 0)), spec], out_specs=spec),
        compiler_params=_params("parallel", "parallel"),
    )(place, grad, from_sibling)


def _final_sum(name, grad, from_sibling, others, place):
    _, _, rows, cols = grad.shape
    tr = _row_tile(rows)

    def body(place_ref, own_ref, sib_ref, o0, o1, o2, out_ref):
        acc = own_ref[0, 0] + sib_ref[0]
        for o in (o0, o1, o2):
            acc = acc + o[0].astype(F32)
        out_ref[...] = acc

    other = lambda k: pl.BlockSpec((1, tr, cols), lambda i, p: (k, i, 0))
    return pl.pallas_call(
        body, name=name, out_shape=jax.ShapeDtypeStruct((rows, cols), F32),
        grid_spec=pltpu.PrefetchScalarGridSpec(
            num_scalar_prefetch=1, grid=(rows // tr,),
            in_specs=[pl.BlockSpec((1, 1, tr, cols), lambda i, p: (p[0], p[1], i, 0)),
                      pl.BlockSpec((1, tr, cols), lambda i, p: (p[1], i, 0)), other(0), other(1), other(2)],
            out_specs=pl.BlockSpec((tr, cols), lambda i, p: (i, 0))),
        compiler_params=_params("parallel"),
    )(place, grad, from_sibling, others, others, others)


def _adamw_update(w, g, m, v):
    m = ADAM_B1 * m + (1.0 - ADAM_B1) * g
    v = ADAM_B2 * v + (1.0 - ADAM_B2) * (g * g)
    m_hat = m * (1.0 / (1.0 - ADAM_B1 ** ADAM_STEP))
    v_hat = v * (1.0 / (1.0 - ADAM_B2 ** ADAM_STEP))
    return -ADAM_LR * (m_hat / (jnp.sqrt(v_hat) + ADAM_EPS) + ADAM_WD * w), m, v


def _adamw(name, w, g, m, v, after):
    rows, cols = w.shape
    tr = _row_tile(rows)

    def body(w_ref, g_ref, m_ref, v_ref, after_ref, d_ref, nm_ref, nv_ref):
        d_ref[...], nm_ref[...], nv_ref[...] = _adamw_update(w_ref[...], g_ref[...], m_ref[...], v_ref[...])

    spec = pl.BlockSpec((tr, cols), lambda i: (i, 0))
    return pl.pallas_call(
        body, name=name, grid=(rows // tr,), in_specs=[spec] * 4 + [pl.BlockSpec(memory_space=pl.ANY)],
        out_specs=[spec] * 3, out_shape=[jax.ShapeDtypeStruct(w.shape, F32)] * 3, compiler_params=_params("parallel"),
    )(w, g, m, v, after)


def _small_update(summed, chip, gains, gains_m, gains_v, taps, taps_m, taps_v):
    n = len(gains)
    widths = [g.shape[1] for g in gains]
    k, w = taps.shape

    def body(*refs):
        chip_ref, sum_ref = refs[0], refs[1]
        params = [refs[2 + 3 * i:5 + 3 * i] for i in range(n + 1)]
        outs = [refs[2 + 3 * (n + 1) + 4 * i:2 + 3 * (n + 1) + 4 * (i + 1)] for i in range(n + 1)]
        loss_ref = refs[-1]
        for i in range(n):
            g = sum_ref[i:i + 1, 0:widths[i]]
            wr, mr, vr = params[i]
            outs[i][0][...] = g
            outs[i][1][...], outs[i][2][...], outs[i][3][...] = _adamw_update(wr[...], g, mr[...], vr[...])
        g = sum_ref[n:n + k, 0:w]
        for j in range(1, N_CHIPS):
            g = jnp.where(chip_ref[0] == j, sum_ref[n:n + k, w * j:w * (j + 1)], g)
        wr, mr, vr = params[n]
        outs[n][0][...] = g
        outs[n][1][...], outs[n][2][...], outs[n][3][...] = _adamw_update(wr[...], g, mr[...], vr[...])
        loss_ref[...] = sum_ref[n + k:n + k + 1, 0:1]

    vmem = pl.BlockSpec(memory_space=pltpu.VMEM)
    operands = [chip, summed]
    for p in zip(list(gains) + [taps], list(gains_m) + [taps_m], list(gains_v) + [taps_v]):
        operands += list(p)
    shapes = [jax.ShapeDtypeStruct(p.shape, F32) for p in list(gains) + [taps] for _ in range(4)]
    out = pl.pallas_call(
        body, name="small_update", out_shape=shapes + [jax.ShapeDtypeStruct((1, 1), F32)],
        in_specs=[pl.BlockSpec(memory_space=pltpu.SMEM)] + [vmem] * (len(operands) - 1),
        out_specs=[vmem] * (len(shapes) + 1),
    )(*operands)
    return [out[4 * i:4 * (i + 1)] for i in range(n + 1)], out[-1]


def _sum_blocks(name, blocks):
    n, rows, cols = blocks.shape

    def body(b_ref, o_ref):
        acc = b_ref[0]
        for k in range(1, n):
            acc = acc + b_ref[k]
        o_ref[...] = acc

    return pl.pallas_call(body, name=name, out_shape=jax.ShapeDtypeStruct((rows, cols), F32))(blocks)


def _place():
    return lax.axis_index("x"), lax.axis_index("y"), lax.axis_index("c")


def _other_chips(x, y):
    return [(1 - x, y), (x, 1 - y), (1 - x, 1 - y)]


def _weights_allgather(name, shards, landed=None):
    n = len(shards)
    first_hop = landed is None

    def body(*refs):
        ins, outs, stage = refs[:n], refs[-3 - 2 * n:-3 - n], refs[-3 - n:-3]
        send_sems, recv_sems, local_sems = refs[-3:]
        x, y, c = _place()
        me, sibling = (x, y, c), (x, y, 1 - c)
        chips = _other_chips(x, y)
        chip_index = lambda chip: 2 * chip[0] + chip[1]

        def copy(a, k, chip, half, to, src=None):
            place = outs[a].at[chip_index(chip), half]
            return pltpu.make_async_remote_copy(
                src_ref=place if src is None else src, dst_ref=place, send_sem=send_sems.at[6 * a + k],
                recv_sem=recv_sems.at[6 * a + k], device_id=to, device_id_type=MESH)

        load = [pltpu.make_async_copy(ins[a], stage[a], local_sems.at[a]) for a in range(n)]
        local = [pltpu.make_async_copy(stage[a], outs[a].at[chip_index((x, y))], local_sems.at[a]) for a in range(n)]
        for cp in load:
            cp.start()
        first = []
        if first_hop:
            first = [copy(a, k, (x, y), c, (*chip, c), src=ins[a].at[c]) for a in range(n) for k, chip in enumerate(chips)]
        for cp in first:
            cp.start()
        for a in range(n):
            load[a].wait()
            local[a].start()
        passed = []
        for a in range(n):
            for k, chip in enumerate(chips):
                if first_hop:
                    copy(a, k, chip, c, me).wait_recv()
                passed.append(copy(a, 3 + k, chip, c, sibling))
                passed[-1].start()
        for a in range(n):
            for k, chip in enumerate(chips):
                copy(a, 3 + k, chip, 1 - c, me).wait_recv()
        for cp in first + passed:
            cp.wait_send()
        for cp in local:
            cp.wait()

    any_spec = pl.BlockSpec(memory_space=pl.ANY)
    operands = list(shards) + ([] if first_hop else list(landed))
    return pl.pallas_call(
        body, name=name,
        out_shape=[jax.ShapeDtypeStruct((N_CHIPS,) + s.shape, s.dtype) for s in shards],
        in_specs=[any_spec] * len(operands), out_specs=[any_spec] * n,
        input_output_aliases={} if first_hop else {n + a: a for a in range(n)},
        scratch_shapes=[pltpu.VMEM(s.shape, s.dtype) for s in shards]
        + [pltpu.SemaphoreType.DMA((6 * n,)), pltpu.SemaphoreType.DMA((6 * n,)), pltpu.SemaphoreType.DMA((n,))],
        compiler_params=pltpu.CompilerParams(vmem_limit_bytes=VMEM_LIMIT_V7X),
    )(*operands)


def _plan_first_hop(x, y, c, shards, lands):
    return [(shards[a].at[c], lands[a].at[2 * x + y, c], lands[a].at[2 * chip[0] + chip[1], c], (*chip, c))
            for a in range(len(shards)) for chip in _other_chips(x, y)]


def _plan_other_half_to_sibling(x, y, c, grads, lands):
    return [(grads[a].at[1 - c], lands[a], lands[a], (x, y, 1 - c)) for a in range(len(grads))]


def _plan_to_other_chips(x, y, c, partials, lands):
    return [(partials[a].at[2 * chip[0] + chip[1]], lands[a].at[k], lands[a].at[k], (*chip, c))
            for a in range(len(partials)) for k, chip in enumerate(_other_chips(x, y))]


def _plan_to_all(x, y, c, blocks, lands):
    flips = [(fx, fy, fc) for fx in (0, 1) for fy in (0, 1) for fc in (0, 1) if (fx, fy, fc) != (0, 0, 0)]
    peers = [(1 - x if fx else x, 1 - y if fy else y, 1 - c if fc else c) for fx, fy, fc in flips]
    return [(blocks[0], lands[0].at[4 * x + 2 * y + c], lands[0].at[4 * p[0] + 2 * p[1] + p[2]], p) for p in peers]


def _planned_copies(plan, srcs, lands, send_sems, recv_sems):
    x, y, c = _place()

    def pair(k, src, there, here, to):
        make = lambda dst: pltpu.make_async_remote_copy(
            src_ref=src, dst_ref=dst, send_sem=send_sems.at[k], recv_sem=recv_sems.at[k], device_id=to, device_id_type=MESH)
        return make(there), make(here)

    return [pair(k, *entry) for k, entry in enumerate(plan(x, y, c, srcs, lands))]


_HBM_SPEC = pl.BlockSpec(memory_space=pltpu.HBM)
_SEM_SPEC = pl.BlockSpec(memory_space=pltpu.SEMAPHORE)


def _hbm(a):
    return pltpu.with_memory_space_constraint(a, pltpu.HBM)


def _exchange_start(name, plan, n_copies, srcs, land_shapes, after):
    ns, nl = len(srcs), len(land_shapes)
    n_in = ns + nl + 1

    def body(*refs):
        for send, _ in _planned_copies(plan, refs[:ns], refs[ns:ns + nl], refs[n_in], refs[n_in + 1]):
            send.start()
        refs[-1][...] = jnp.zeros_like(refs[-1])

    out = pl.pallas_call(
        body, name=name,
        out_shape=(pltpu.SemaphoreType.DMA((n_copies,)), pltpu.SemaphoreType.DMA((n_copies,)),
                   *[pltpu.HBM(s.shape, s.dtype) for s in land_shapes], jax.ShapeDtypeStruct((8, 128), F32)),
        in_specs=[_HBM_SPEC] * (ns + nl) + [pl.BlockSpec(memory_space=pl.ANY)],
        out_specs=(_SEM_SPEC, _SEM_SPEC, *[_HBM_SPEC] * nl, pl.BlockSpec(memory_space=pltpu.VMEM)),
        input_output_aliases={ns + i: 2 + i for i in range(nl)},
        compiler_params=pltpu.CompilerParams(has_side_effects=pltpu.SideEffectType.DATAFLOW_SIDE_EFFECTING),
    )(*[_hbm(s) for s in srcs], *[_hbm(lax.empty(s.shape, s.dtype)) for s in land_shapes], after)
    return out[0], out[1], list(out[2:2 + nl]), out[-1]


def _exchange_wait(name, plan, srcs, started, after):
    send_sems, recv_sems, lands, _ = started
    ns, nl = len(srcs), len(lands)

    def body(*refs):
        for send, recv in _planned_copies(plan, refs[:ns], refs[ns:ns + nl], refs[ns + nl], refs[ns + nl + 1]):
            send.wait_send()
            recv.wait_recv()

    return pl.pallas_call(
        body, name=name, out_shape=[pltpu.HBM(l.shape, l.dtype) for l in lands],
        in_specs=[_HBM_SPEC] * (ns + nl) + [_SEM_SPEC, _SEM_SPEC, pl.BlockSpec(memory_space=pl.ANY)],
        out_specs=[_HBM_SPEC] * nl, input_output_aliases={ns + i: i for i in range(nl)},
        compiler_params=pltpu.CompilerParams(has_side_effects=pltpu.SideEffectType.DATAFLOW_SIDE_EFFECTING),
    )(*[_hbm(s) for s in srcs], *lands, send_sems, recv_sems, after)


def _exchange_halves(name, halves, after):
    n = len(halves)

    def body(*refs):
        ins, outs, stage = refs[:n], refs[n + 1:2 * n + 1], refs[2 * n + 1:3 * n + 1]
        send_sems, recv_sems, local_sems = refs[3 * n + 1:]
        x, y, c = _place()
        load = [pltpu.make_async_copy(ins[a], stage[a], local_sems.at[a]) for a in range(n)]
        local = [pltpu.make_async_copy(stage[a], outs[a].at[c], local_sems.at[a]) for a in range(n)]
        remote = [pltpu.make_async_remote_copy(
            src_ref=stage[a], dst_ref=outs[a].at[c], send_sem=send_sems.at[a], recv_sem=recv_sems.at[a],
            device_id=(x, y, 1 - c), device_id_type=MESH) for a in range(n)]
        for cp in load:
            cp.start()
        for a in range(n):
            load[a].wait()
            remote[a].start()
            local[a].start()
        for a in range(n):
            pltpu.make_async_remote_copy(
                src_ref=ins[a], dst_ref=outs[a].at[1 - c], send_sem=send_sems.at[a], recv_sem=recv_sems.at[a],
                device_id=(x, y, 1 - c), device_id_type=MESH).wait_recv()
        for cp in remote:
            cp.wait_send()
        for cp in local:
            cp.wait()

    any_spec = pl.BlockSpec(memory_space=pl.ANY)
    return pl.pallas_call(
        body, name=name,
        out_shape=[jax.ShapeDtypeStruct((2,) + h.shape, h.dtype) for h in halves],
        in_specs=[any_spec] * (n + 1), out_specs=[any_spec] * n,
        scratch_shapes=[pltpu.VMEM(h.shape, h.dtype) for h in halves]
        + [pltpu.SemaphoreType.DMA((n,)), pltpu.SemaphoreType.DMA((n,)), pltpu.SemaphoreType.DMA((n,))],
        compiler_params=pltpu.CompilerParams(vmem_limit_bytes=VMEM_LIMIT_V7X),
    )(*halves, after)


def _like(arrays, lead, dtype=None):
    return [jax.ShapeDtypeStruct(tuple(lead) + a.shape[-2:], dtype or a.dtype) for a in arrays]


class _StepExchanges:
    def __init__(self, mats, conv_w):
        x, y, c = _place()
        self.place = jnp.stack([c, 2 * x + y]).astype(jnp.int32)
        shards = [w.astype(BF16).reshape(2, w.shape[0] // 2, w.shape[1]) for w in mats]
        self._in_shard = shards[:1]
        self._in = _exchange_start("w_in_allgather_start", _plan_first_hop, 3, self._in_shard,
                                   _like(self._in_shard, (N_CHIPS, 2)), shards[0])
        self.zero = self._in[3]
        taps = jnp.pad(conv_w, ((0, 8 - conv_w.shape[0]), (0, 128 - conv_w.shape[1])))
        self._rest_shards = shards[1:] + [jnp.stack([taps, jnp.zeros_like(taps)])]
        self._taps_shape = conv_w.shape
        self._groups = {}

    def w_in(self, after):
        landed = _exchange_wait("w_in_allgather_wait", _plan_first_hop, self._in_shard, self._in, after)
        (w_in,) = _weights_allgather("w_in_allgather_finish", self._in_shard, landed=landed)
        self._rest = _exchange_start("rest_allgather_start", _plan_first_hop, 3 * len(self._rest_shards),
                                     self._rest_shards, _like(self._rest_shards, (N_CHIPS, 2)), w_in)
        self.zero = self._rest[3]
        return w_in.reshape(N_CHIPS, 2 * w_in.shape[2], w_in.shape[3])

    def rest_weights(self, after):
        landed = _exchange_wait("rest_allgather_wait", _plan_first_hop, self._rest_shards, self._rest, after)
        *mats, taps = _weights_allgather("rest_allgather_finish", self._rest_shards, landed=landed)
        k, w = self._taps_shape
        taps = taps[:, 0, :k, :w].transpose(1, 0, 2).reshape(k, N_CHIPS * w)
        return [g.reshape(N_CHIPS, 2 * g.shape[2], g.shape[3]) for g in mats], taps

    def send_grads(self, key, grads):
        grads = list(grads)
        started = _exchange_start(f"{key}_grads_to_sibling_start", _plan_other_half_to_sibling, len(grads), grads,
                                  _like(grads, (N_CHIPS,)), self.zero)
        self._groups[key] = dict(grads=grads, to_sibling=started)
        self.zero = started[3]

    def grads_at_sibling(self, key, after):
        group = self._groups[key]
        grads = group["grads"]
        group["from_sibling"] = _exchange_wait(f"{key}_grads_to_sibling_wait", _plan_other_half_to_sibling, grads,
                                               group["to_sibling"], after)
        group["partials"] = [_chip_sum_bf16(f"{key}_chip_sum_{a}", grads[a], group["from_sibling"][a], self.place)
                             for a in range(len(grads))]
        group["to_chips"] = _exchange_start(f"{key}_grads_to_chips_start", _plan_to_other_chips, 3 * len(grads),
                                            group["partials"], _like(group["partials"], (3,)), self.zero)
        self.zero = group["to_chips"][3]

    def grads_summed(self, key, after):
        group = self._groups[key]
        from_chips = _exchange_wait(f"{key}_grads_to_chips_wait", _plan_to_other_chips, group["partials"],
                                    group["to_chips"], after)
        return [_final_sum(f"{key}_final_sum_{a}", group["grads"][a], group["from_sibling"][a], from_chips[a], self.place)
                for a in range(len(from_chips))]

    def send_small(self, block):
        self._small = block
        self._small_started = _exchange_start("small_grads_start", _plan_to_all, 7, [block],
                                              [jax.ShapeDtypeStruct((8,) + block.shape, block.dtype)], self.zero)
        self.zero = self._small_started[3]

    def small_summed(self, after):
        x, y, c = _place()
        (landed,) = _exchange_wait("small_grads_wait", _plan_to_all, [self._small], self._small_started, after)
        blocks = lax.dynamic_update_index_in_dim(landed, self._small, 4 * x + 2 * y + c, 0)
        return _sum_blocks("small_sum", blocks)


def _rope_tables(positions):
    half = HEAD // 2
    inv_freq = jnp.float32(ROPE_THETA) ** (-(jnp.arange(half, dtype=F32) * 2.0 / HEAD))
    ang = positions.astype(F32)[:, None] * inv_freq
    cos, sin = jnp.cos(ang), jnp.sin(ang)
    return jnp.tile(cos, (1, 4)), jnp.tile(jnp.concatenate([-sin, sin], axis=1), (1, 2))


def _local_step(x, mem, positions, target, gains, ex):
    g_pre_mix, g_mem, g_a, g_c, g_x, g_post_mix, g_pre_mlp, g_post_mlp = gains
    tm = ROW_TILE
    cos, sin = _rope_tables(positions)
    h = _pre_norm(x, g_pre_mix + ex.zero[:1, :1], tm)
    w_in = ex.w_in(h)

    q, k, v, bcu, qx = _in_proj_fwd(h, w_in, cos, sin, ex.zero, tm)
    ya, lse = _attn_fwd(q, k, v)
    (w_kv, w_out, w_up, w_down), conv_w = ex.rest_weights(lse)
    w_kv, w_out, w_down = (w.reshape(N_CHIPS * w.shape[1], w.shape[2]) for w in (w_kv, w_out, w_down))
    memn, mkv = _memkv_fwd(mem, g_mem, w_kv)
    yx, ycat, y2, x1 = _mix_fwd(ya, bcu, qx, mkv, conv_w, g_a, g_c, g_x, w_out, g_post_mix, x, tm)
    h2, f, du, df2, dx1, dg_pre_mlp, dg_post_mlp, loss = _mlp_fwd_bwd(x1, target, g_pre_mlp, g_post_mlp, w_up, w_down,
                                                                      MLP_ROW_TILE)
    gw_down = _weight_grad("grad_w_down", f, df2, True)
    gw_up = _weight_grad("grad_w_up", h2, du, False)
    ex.send_grads("early", [gw_up, gw_down])

    dy2, dya, delta, dycx, dg_post_mix, dg_a, dg_c, dg_x = _mix_bwd(dx1, y2, ya, yx, bcu, conv_w, g_a, g_c, g_x,
                                                                  w_out, g_post_mix + ex.zero[:1, :1], tm)
    ex.grads_at_sibling("early", dy2)
    gw_out = _weight_grad("grad_w_out", ycat, dy2, True)
    tail, dmkv, g_conv = _conv_xattn_bwd(dycx, bcu, qx, mkv, conv_w + ex.zero[:1, :1], tm)
    gw_kv, dg_mem = _memkv_bwd(mem, g_mem, w_kv, dmkv)
    ex.send_grads("mid", [gw_out, gw_kv])
    dqkv = _attn_bwd(q, k, v, dya, lse, delta, ex.zero)
    ex.grads_at_sibling("mid", dqkv[0])
    dproj, grad_x, dg_pre_mix = _in_proj_bwd(dqkv, tail, cos, sin, w_in, x, g_pre_mix + ex.zero[:1, :1], dx1, tm)
    gain_grads = [dg_pre_mix, dg_mem, dg_a, dg_c, dg_x, dg_post_mix, dg_pre_mlp, dg_post_mlp]
    ex.send_small(_pack_small(gain_grads, g_conv, loss))
    gw_in = _weight_grad("grad_w_in", h, dproj, False)
    ex.send_grads("late", [gw_in])
    return grad_x


def _pack_small(gains, conv, scalar=None):
    rows = [jnp.pad(g, ((0, 0), (0, D_MODEL - g.shape[1]))) for g in gains]
    rows.append(jnp.pad(conv, ((0, 0), (0, D_MODEL - conv.shape[1]))))
    last = jnp.zeros((SMALL_ROWS - 8 - conv.shape[0], D_MODEL), F32)
    rows.append(last if scalar is None else last.at[0:1, 0:1].set(scalar))
    return jnp.concatenate(rows, axis=0)


def _unpack_small(block, gain_widths, conv_width):
    gains = [block[i:i + 1, :w] for i, w in enumerate(gain_widths)]
    return gains, block[8:11, :conv_width], block[11, 0]


def kernel(x, mem, positions, g_pre_mix, g_mem, w_in, w_mem_kv, conv_w, g_attn_out, g_conv_out, g_xattn_out, w_out, g_post_mix, g_pre_mlp, w_up, w_down, g_post_mlp, loss_target, m_g_pre_mix, m_g_mem, m_w_in, m_w_mem_kv, m_conv_w, m_g_attn_out, m_g_conv_out, m_g_xattn_out, m_w_out, m_g_post_mix, m_g_pre_mlp, m_w_up, m_w_down, m_g_post_mlp, v_g_pre_mix, v_g_mem, v_w_in, v_w_mem_kv, v_conv_w, v_g_attn_out, v_g_conv_out, v_g_xattn_out, v_w_out, v_g_post_mix, v_g_pre_mlp, v_w_up, v_w_down, v_g_post_mlp):
    cx, cy, cc = _place()
    chip = 2 * cx + cy
    gains = [g_pre_mix, g_mem, g_attn_out, g_conv_out, g_xattn_out, g_post_mix, g_pre_mlp, g_post_mlp]
    gains_m = [m_g_pre_mix, m_g_mem, m_g_attn_out, m_g_conv_out, m_g_xattn_out, m_g_post_mix, m_g_pre_mlp, m_g_post_mlp]
    gains_v = [v_g_pre_mix, v_g_mem, v_g_attn_out, v_g_conv_out, v_g_xattn_out, v_g_post_mix, v_g_pre_mlp, v_g_post_mlp]
    gain_widths = [g.shape[1] for g in gains]
    mats = [w_in[0], w_mem_kv[0], w_out[0], w_up[0], w_down[0]]
    mats_m = [m_w_in[0], m_w_mem_kv[0], m_w_out[0], m_w_up[0], m_w_down[0]]
    mats_v = [v_w_in[0], v_w_mem_kv[0], v_w_out[0], v_w_up[0], v_w_down[0]]

    ex = _StepExchanges(mats, conv_w[0])
    grad_x = _local_step(x[0], mem[0], positions[0], loss_target[0], gains, ex)

    both = lambda halves: [t.reshape(2 * t.shape[1], t.shape[2]) for t in halves]
    done = ex.grads_summed("early", ex.zero) + ex.grads_summed("mid", ex.zero)
    ex.grads_at_sibling("late", sum(t[:8, :128] for t in done))
    up_sum, down_sum, out_sum, kv_sum = both(_exchange_halves("sums_to_sibling", done, ex.zero))
    adamw = lambda a, g, after: _adamw(f"adamw_{a}", mats[a], g, mats_m[a], mats_v[a], after)
    new_up, new_down, new_out, new_kv = adamw(3, up_sum, ex.zero), adamw(4, down_sum, ex.zero), adamw(2, out_sum, ex.zero), adamw(1, kv_sum, ex.zero)

    small, total = _small_update(ex.small_summed(new_kv[0]), chip.reshape(1).astype(jnp.int32), gains, gains_m,
                                 gains_v, conv_w[0], m_conv_w[0], v_conv_w[0])

    (in_half,) = ex.grads_summed("late", small[0][1])
    (in_sum,) = both(_exchange_halves("late_sum_to_sibling", [in_half], in_half))
    new_in = adamw(0, in_sum, in_sum)
    mat_sums = [in_sum, kv_sum, out_sum, up_sum, down_sum]
    mat_new = [new_in, new_kv, new_out, new_up, new_down]

    order = ["g_pre_mix", "g_mem", "w_in", "w_mem_kv", "conv_w", "g_attn_out", "g_conv_out", "g_xattn_out", "w_out",
             "g_post_mix", "g_pre_mlp", "w_up", "w_down", "g_post_mlp"]
    gain_names = ["g_pre_mix", "g_mem", "g_attn_out", "g_conv_out", "g_xattn_out", "g_post_mix", "g_pre_mlp", "g_post_mlp"]
    mat_names = ["w_in", "w_mem_kv", "w_out", "w_up", "w_down"]

    def leaf(kind, name):
        if name in gain_names:
            return small[gain_names.index(name)][kind]
        if name == "conv_w":
            return small[len(gain_names)][kind][None]
        a = mat_names.index(name)
        return (mat_sums[a] if kind == 0 else mat_new[a][kind - 1])[None]

    return (total[0, 0], grad_x[None], *[leaf(kind, name) for kind in range(4) for name in order])
```

```python
import jax
import jax.numpy as jnp
from jax import lax
from jax.experimental import pallas as pl
from jax.experimental.pallas import tpu as pltpu

F32, BF16 = jnp.float32, jnp.bfloat16

D_MODEL = 1024
ATTN_W = 512
CONV_W = 256
XATTN_W = 256
PROJ_W = 3 * ATTN_W + 3 * CONV_W + XATTN_W
D_FF = 4096
HEAD = 64
N_BACK = 128
DILATIONS = (1, 4, 16)
ROPE_THETA = 10000.0
EPS = 1e-6
NEG_INF = -1e30
SCALE = HEAD ** -0.5
N_CHIPS = 4
SHARD_IN = PROJ_W // N_CHIPS
SHARD_FF = D_FF // N_CHIPS

ADAM_LR, ADAM_B1, ADAM_B2, ADAM_EPS, ADAM_WD, ADAM_STEP = 0.001, 0.9, 0.999, 1e-08, 0.01, 10

VMEM_LIMIT_V7X = 56 * 1024 * 1024
ROW_TILE = 512
MLP_ROW_TILE = 256
SMALL_ROWS = 16

NT = (((1,), (1,)), ((), ()))
TN = (((0,), (0,)), ((), ()))
MESH = pl.DeviceIdType.MESH


def _params(*sem):
    return pltpu.CompilerParams(dimension_semantics=sem, vmem_limit_bytes=VMEM_LIMIT_V7X)


def _resident(shape):
    return pl.BlockSpec(shape, lambda *_: (0,) * len(shape), pipeline_mode=pl.Buffered(1))


def _rows(tm, width):
    return pl.BlockSpec((tm, width), lambda i: (i, 0))


def _rms_hat(x):
    r = lax.rsqrt(jnp.mean(x * x, axis=-1, keepdims=True) + EPS)
    return x * r, r


def _rms_bwd(xhat, r, g, dy):
    gdy = dy * g
    return r * (gdy - xhat * jnp.mean(xhat * gdy, axis=-1, keepdims=True))


def _rope128(t, cos, sin_signed, inverse):
    lane = lax.broadcasted_iota(jnp.int32, t.shape, 1)
    first_half = (lane % HEAD) < (HEAD // 2)
    rot = jnp.where(first_half, pltpu.roll(t, 128 - HEAD // 2, 1), pltpu.roll(t, HEAD // 2, 1))
    return t * cos - rot * sin_signed if inverse else t * cos + rot * sin_signed


def _pre_norm(x, g, tm):
    S = x.shape[0]

    def body(x_ref, g_ref, h_ref):
        h_ref[...] = (_rms_hat(x_ref[...])[0] * g_ref[...]).astype(BF16)

    return pl.pallas_call(
        body, name="pre_norm", grid=(S // tm,), in_specs=[_rows(tm, D_MODEL), _resident((1, D_MODEL))],
        out_specs=_rows(tm, D_MODEL), out_shape=jax.ShapeDtypeStruct((S, D_MODEL), BF16),
        compiler_params=_params("parallel"),
    )(x, g)


def _in_proj_fwd(h, w_in, cos, sin, after, tm):
    S = h.shape[0]

    def body(h_ref, w_ref, cos_ref, sin_ref, after_ref, q_ref, k_ref, v_ref, bcu_ref, qx_ref, proj):
        h = h_ref[...]
        for j in range(N_CHIPS):
            proj[:, SHARD_IN * j:SHARD_IN * (j + 1)] = jnp.dot(h, w_ref[j], preferred_element_type=F32)
        c, s = cos_ref[...], sin_ref[...]
        for j in range(ATTN_W // 128):
            lo = 128 * j
            q_ref[:, lo:lo + 128] = _rope128(proj[:, lo:lo + 128], c, s, False) * SCALE
            k_ref[:, lo:lo + 128] = _rope128(proj[:, ATTN_W + lo:ATTN_W + lo + 128], c, s, False)
        v_ref[...] = proj[:, 2 * ATTN_W:3 * ATTN_W]
        bcu_ref[...] = proj[:, 3 * ATTN_W:3 * ATTN_W + 3 * CONV_W]
        qx_ref[...] = proj[:, 3 * ATTN_W + 3 * CONV_W:PROJ_W].astype(BF16)

    return pl.pallas_call(
        body, name="in_proj_fwd", grid=(S // tm,),
        in_specs=[_rows(tm, D_MODEL), _resident((N_CHIPS, D_MODEL, SHARD_IN)), _rows(tm, 128), _rows(tm, 128),
                  pl.BlockSpec(memory_space=pl.ANY)],
        out_specs=[_rows(tm, ATTN_W), _rows(tm, ATTN_W), _rows(tm, ATTN_W), _rows(tm, 3 * CONV_W), _rows(tm, XATTN_W)],
        out_shape=[jax.ShapeDtypeStruct((S, ATTN_W), F32), jax.ShapeDtypeStruct((S, ATTN_W), F32),
                   jax.ShapeDtypeStruct((S, ATTN_W), F32), jax.ShapeDtypeStruct((S, 3 * CONV_W), F32),
                   jax.ShapeDtypeStruct((S, XATTN_W), BF16)],
        scratch_shapes=[pltpu.VMEM((tm, PROJ_W), F32)],
        compiler_params=_params("parallel"),
    )(h, w_in, cos, sin, after)


def _memkv_fwd(mem, g_mem, w_kv):
    n_mem = mem.shape[0]

    def body(mem_ref, g_ref, w_ref, mn_ref, kv_ref):
        mhat, _ = _rms_hat(mem_ref[...])
        mn = (mhat * g_ref[...]).astype(BF16)
        mn_ref[...] = mn
        kv_ref[...] = jnp.dot(mn, w_ref[...], preferred_element_type=F32).astype(BF16)

    return pl.pallas_call(
        body, name="memkv_fwd",
        out_shape=[jax.ShapeDtypeStruct((n_mem, D_MODEL), BF16), jax.ShapeDtypeStruct((n_mem, 2 * XATTN_W), BF16)],
        compiler_params=pltpu.CompilerParams(vmem_limit_bytes=VMEM_LIMIT_V7X),
    )(mem, g_mem, w_kv)


def _fill_band_bias(bias):
    row = lax.broadcasted_iota(jnp.int32, (N_BACK, 2 * N_BACK), 0)
    col = lax.broadcasted_iota(jnp.int32, (N_BACK, 2 * N_BACK), 1)
    band = (col >= row) & (col <= row + N_BACK)
    bias[1] = jnp.where(band, 0.0, NEG_INF)
    bias[0] = jnp.where(band & (col >= N_BACK), 0.0, NEG_INF)


def _strided(start, size, d):
    return pl.ds(start, size) if d == 1 else pl.ds(start, size, stride=d)


def _group_starts(g, G, nb, d):
    t0 = g * G
    r, n0 = lax.shift_right_logical(t0, nb.bit_length() - 1), lax.bitwise_and(t0, nb - 1)
    first = r + n0 * (N_BACK * d)
    before = r + jnp.maximum(n0 - 1, 0) * (N_BACK * d)
    starts = [before] + [first + u * (N_BACK * d) for u in range(G)]
    if d == 1:
        starts = [pl.multiple_of(st, N_BACK) for st in starts]
    return starts, n0


def _step_blocks(i, U, nb, d):
    G = min(U, nb)
    row_blocks, blocks = [], []
    for grp in range(U // G):
        starts, n0 = _group_starts(i * (U // G) + grp, G, nb, d)
        base = len(row_blocks)
        row_blocks += [_strided(st, N_BACK, d) for st in starts]
        for u in range(G):
            blocks.append((base + u, base + u + 1, jnp.minimum(n0, 1) if u == 0 else 1))
    return row_blocks, blocks


def _by_head(a, b):
    lane = lax.broadcasted_iota(jnp.int32, (a.shape[0], 2 * HEAD), 1)
    return jnp.where(lane < HEAD, a, b)


def _head_only(t, hh):
    lane = lax.broadcasted_iota(jnp.int32, t.shape, 1)
    return jnp.where((lane < HEAD) == (hh == 0), t, jnp.zeros_like(t))


def _stack_heads(t):
    return jnp.concatenate([_head_only(t, 0), _head_only(t, 1)], axis=0)


def _head_columns(t):
    return jnp.concatenate([t[:, 0:1], t[:, HEAD:HEAD + 1]], axis=0)


def _unstack(t):
    return _by_head(t[:N_BACK], t[N_BACK:])


def _unstack_columns(t):
    return _by_head(jnp.broadcast_to(t[:N_BACK], (N_BACK, 2 * HEAD)), jnp.broadcast_to(t[N_BACK:], (N_BACK, 2 * HEAD)))


FWD_BLOCKS_PER_STEP = 4
BWD_BLOCKS_PER_STEP = 2


def _attn_fwd(q, k, v):
    S = q.shape[0]
    U = FWD_BLOCKS_PER_STEP

    def body(q_ref, k_ref, v_ref, y_ref, m_ref, l_scr, bias):
        _fill_band_bias(bias)
        for g, d in enumerate(DILATIONS):
            nb = S // d // N_BACK
            first_pattern, last_pattern = g == 0, g == len(DILATIONS) - 1

            def step(i, carry, d=d, nb=nb, first_pattern=first_pattern, last_pattern=last_pattern):
                row_blocks, blocks = _step_blocks(i, U, nb, d)
                kb = [k_ref[r, :].astype(BF16) for r in row_blocks]
                ss = []
                for before, own, which in blocks:
                    kw = jnp.concatenate([kb[before], kb[own]], 0)
                    qs = _stack_heads(q_ref[row_blocks[own], :].astype(BF16))
                    b = bias[which]
                    ss.append(lax.dot_general(qs, kw, NT, preferred_element_type=F32) + jnp.concatenate([b, b], axis=0))
                ms = [jnp.max(s, axis=1, keepdims=True) for s in ss]
                ps = [jnp.exp(s - m) for s, m in zip(ss, ms)]
                ls = [jnp.sum(p, axis=1, keepdims=True) for p in ps]
                vb = [v_ref[r, :].astype(BF16) for r in row_blocks]
                os_ = [jnp.dot(ps[u].astype(BF16), jnp.concatenate([vb[before], vb[own]], 0), preferred_element_type=F32)
                       for u, (before, own, _) in enumerate(blocks)]
                for u, (_, own, _) in enumerate(blocks):
                    o_g, m_g, l_g = _unstack(os_[u]), _unstack_columns(ms[u]), _unstack_columns(ls[u])
                    r = row_blocks[own]
                    if first_pattern:
                        m_new, l_new, acc = m_g, l_g, o_g
                    else:
                        m_old = m_ref[r, :]
                        m_new = jnp.maximum(m_old, m_g)
                        alpha, beta = jnp.exp(m_old - m_new), jnp.exp(m_g - m_new)
                        l_new = l_scr[r, :] * alpha + l_g * beta
                        acc = y_ref[r, :] * alpha + o_g * beta
                    if last_pattern:
                        y_ref[r, :] = acc / l_new
                        m_ref[r, :] = m_new + jnp.log(l_new)
                    else:
                        y_ref[r, :] = acc
                        m_ref[r, :] = m_new
                        l_scr[r, :] = l_new
                return carry

            lax.fori_loop(0, d * nb // U, step, 0)

    col = pl.BlockSpec((S, 2 * HEAD), lambda j: (0, j))
    return pl.pallas_call(
        body, name="attn_fwd", grid=(q.shape[1] // (2 * HEAD),),
        in_specs=[col, col, col], out_specs=[col, col],
        out_shape=[jax.ShapeDtypeStruct(q.shape, F32)] * 2,
        scratch_shapes=[pltpu.VMEM((S, 2 * HEAD), F32), pltpu.VMEM((2, N_BACK, 2 * N_BACK), F32)],
        compiler_params=_params("parallel"),
    )(q, k, v)


def _attn_bwd(q, k, v, dy, lse, delta, after):
    S = q.shape[0]
    U = BWD_BLOCKS_PER_STEP

    def body(q_ref, k_ref, v_ref, dy_ref, lse_ref, delta_ref, after_ref, dq_ref, dk_ref, dv_ref, bias):
        _fill_band_bias(bias)
        dk_ref[...] = jnp.zeros_like(dk_ref)
        dv_ref[...] = jnp.zeros_like(dv_ref)
        for g, d in enumerate(DILATIONS):
            nb = S // d // N_BACK

            def step(i, carry, d=d, nb=nb, g=g):
                row_blocks, blocks = _step_blocks(i, U, nb, d)
                kb = [k_ref[r, :].astype(BF16) for r in row_blocks]
                vb = [v_ref[r, :].astype(BF16) for r in row_blocks]
                kws = [jnp.concatenate([kb[before], kb[own]], 0) for before, own, _ in blocks]
                vws = [jnp.concatenate([vb[before], vb[own]], 0) for before, own, _ in blocks]
                qss = [_stack_heads(q_ref[row_blocks[own], :].astype(BF16)) for _, own, _ in blocks]
                doss = [_stack_heads(dy_ref[row_blocks[own], :].astype(BF16)) for _, own, _ in blocks]
                ss, dps = [], []
                for u, (_, _, which) in enumerate(blocks):
                    b = bias[which]
                    ss.append(lax.dot_general(qss[u], kws[u], NT, preferred_element_type=F32) + jnp.concatenate([b, b], axis=0))
                    dps.append(lax.dot_general(doss[u], vws[u], NT, preferred_element_type=F32))
                ps = [jnp.exp(ss[u] - _head_columns(lse_ref[row_blocks[own], :])) for u, (_, own, _) in enumerate(blocks)]
                dss = [(ps[u] * (dps[u] - _head_columns(delta_ref[row_blocks[own], :]))).astype(BF16)
                       for u, (_, own, _) in enumerate(blocks)]
                pbs = [p.astype(BF16) for p in ps]
                dqs = [jnp.dot(dss[u], kws[u], preferred_element_type=F32) for u in range(U)]
                dkws = [lax.dot_general(dss[u], qss[u], TN, preferred_element_type=F32) for u in range(U)]
                dvws = [lax.dot_general(pbs[u], doss[u], TN, preferred_element_type=F32) for u in range(U)]
                dk_parts, dv_parts = [None] * len(row_blocks), [None] * len(row_blocks)
                for u, (before, own, _) in enumerate(blocks):
                    dq = _unstack(dqs[u])
                    if g == 0:
                        dq_ref[row_blocks[own], :] = dq
                    else:
                        dq_ref[row_blocks[own], :] += dq
                    for idx, dkp, dvp in ((before, dkws[u][:N_BACK], dvws[u][:N_BACK]),
                                          (own, dkws[u][N_BACK:], dvws[u][N_BACK:])):
                        dk_parts[idx] = dkp if dk_parts[idx] is None else dk_parts[idx] + dkp
                        dv_parts[idx] = dvp if dv_parts[idx] is None else dv_parts[idx] + dvp
                for idx, r in enumerate(row_blocks):
                    dk_ref[r, :] += dk_parts[idx]
                    dv_ref[r, :] += dv_parts[idx]
                return carry

            lax.fori_loop(0, d * nb // U, step, 0)

    col = pl.BlockSpec((S, 2 * HEAD), lambda j: (0, j))
    return pl.pallas_call(
        body, name="attn_bwd", grid=(q.shape[1] // (2 * HEAD),),
        in_specs=[col] * 6 + [pl.BlockSpec(memory_space=pl.ANY)], out_specs=[col] * 3,
        out_shape=[jax.ShapeDtypeStruct(q.shape, F32)] * 3,
        scratch_shapes=[pltpu.VMEM((2, N_BACK, 2 * N_BACK), F32)],
        compiler_params=_params("parallel"),
    )(q, k, v, dy, lse, delta, after)


def _shift_down(z, before, k):
    row = lax.broadcasted_iota(jnp.int32, z.shape, 0)
    out = pltpu.roll(z, k, 0)
    for i in range(k):
        out = jnp.where(row == i, before[8 - k + i:8 - k + i + 1, :], out)
    return out


def _shift_up(z, after, k):
    rows = z.shape[0]
    row = lax.broadcasted_iota(jnp.int32, z.shape, 0)
    out = pltpu.roll(z, rows - k, 0)
    for i in range(k):
        out = jnp.where(row == rows - k + i, after[i:i + 1, :], out)
    return out


def _conv_fwd(bcu, before, is_first, w):
    b, c, u = bcu[:, 0:CONV_W], bcu[:, CONV_W:2 * CONV_W], bcu[:, 2 * CONV_W:3 * CONV_W]
    z = c * u
    zb = jnp.where(is_first, 0.0, before[:, CONV_W:2 * CONV_W] * before[:, 2 * CONV_W:3 * CONV_W])
    z1, z2 = _shift_down(z, zb, 1), _shift_down(z, zb, 2)
    cv = w[0:1, :] * z2 + w[1:2, :] * z1 + w[2:3, :] * z
    return b, c, u, z, z1, z2, cv


def _halo_before(tm, width):
    return pl.BlockSpec((8, width), lambda i: (jnp.maximum(i * (tm // 8) - 1, 0), 0))


def _halo_after(tm, width, S):
    return pl.BlockSpec((8, width), lambda i: (jnp.minimum((i + 1) * (tm // 8), S // 8 - 1), 0))


def _mix_fwd(ya, bcu, qx, mkv, conv_w, g_a, g_c, g_x, w_out, g_post, x, tm):
    S = x.shape[0]

    def body(ya_ref, bcu_ref, before_ref, qx_ref, mkv_ref, cw_ref, ga_ref, gc_ref, gx_ref,
             wo_ref, gp_ref, x_ref, yx_ref, ycat_ref, y2_ref, x1_ref):
        ya = ya_ref[...]
        b, _, _, _, _, _, cv = _conv_fwd(bcu_ref[...], before_ref[...], pl.program_id(0) == 0, cw_ref[...])
        yc = b * cv

        qxb, mkvb = qx_ref[...], mkv_ref[...]
        for hd in range(XATTN_W // HEAD):
            sl = slice(HEAD * hd, HEAD * (hd + 1))
            s = lax.dot_general(qxb[:, sl], mkvb[:, sl], NT, preferred_element_type=F32) * SCALE
            mx = jnp.max(s, axis=1, keepdims=True)
            p = jnp.exp(s - mx)
            l = jnp.sum(p, axis=1, keepdims=True)
            vm = mkvb[:, XATTN_W + HEAD * hd:XATTN_W + HEAD * (hd + 1)]
            yx_ref[:, sl] = jnp.dot(p.astype(BF16), vm, preferred_element_type=F32) / l
        yx = yx_ref[...]

        ycat_ref[:, 0:ATTN_W] = (_rms_hat(ya)[0] * ga_ref[...]).astype(BF16)
        ycat_ref[:, ATTN_W:ATTN_W + CONV_W] = (_rms_hat(yc)[0] * gc_ref[...]).astype(BF16)
        ycat_ref[:, ATTN_W + CONV_W:D_MODEL] = (_rms_hat(yx)[0] * gx_ref[...]).astype(BF16)
        y2 = jnp.dot(ycat_ref[...], wo_ref[...], preferred_element_type=F32)
        y2_ref[...] = y2
        x1_ref[...] = x_ref[...] + _rms_hat(y2)[0] * gp_ref[...]

    n_mem = mkv.shape[0]
    return pl.pallas_call(
        body, name="mix_fwd", grid=(S // tm,),
        in_specs=[_rows(tm, ATTN_W), _rows(tm, 3 * CONV_W), _halo_before(tm, 3 * CONV_W), _rows(tm, XATTN_W),
                  _resident((n_mem, 2 * XATTN_W)), _resident((3, CONV_W)), _resident((1, ATTN_W)),
                  _resident((1, CONV_W)), _resident((1, XATTN_W)), _resident((D_MODEL, D_MODEL)),
                  _resident((1, D_MODEL)), _rows(tm, D_MODEL)],
        out_specs=[_rows(tm, XATTN_W), _rows(tm, D_MODEL), _rows(tm, D_MODEL), _rows(tm, D_MODEL)],
        out_shape=[jax.ShapeDtypeStruct((S, XATTN_W), F32), jax.ShapeDtypeStruct((S, D_MODEL), BF16),
                   jax.ShapeDtypeStruct((S, D_MODEL), F32), jax.ShapeDtypeStruct((S, D_MODEL), F32)],
        compiler_params=_params("parallel"),
    )(ya, bcu, bcu, qx, mkv, conv_w, g_a, g_c, g_x, w_out, g_post, x)


def _mlp_fwd_bwd(x1, target, g_pre, g_post, w_up, w_down, tm):
    S = x1.shape[0]
    n_ff = D_FF // SHARD_FF

    def body(x1_ref, t_ref, gpre_ref, gpost_ref, wup_ref, wdn_ref,
             h2_ref, f_ref, du_ref, df2_ref, dx1_ref, dgpre_ref, dgpost_ref, loss_ref, u_scr):
        @pl.when(pl.program_id(0) == 0)
        def _():
            dgpre_ref[...] = jnp.zeros_like(dgpre_ref)
            dgpost_ref[...] = jnp.zeros_like(dgpost_ref)
            loss_ref[...] = jnp.zeros_like(loss_ref)

        x1 = x1_ref[...]
        x1hat, r1 = _rms_hat(x1)
        h2 = (x1hat * gpre_ref[...]).astype(BF16)
        h2_ref[...] = h2
        f2 = jnp.zeros((tm, D_MODEL), F32)
        for j in range(n_ff):
            cols = slice(SHARD_FF * j, SHARD_FF * (j + 1))
            u = jnp.maximum(jnp.dot(h2, wup_ref[j], preferred_element_type=F32), 0.0)
            u_scr[:, cols] = u
            f = (u * u).astype(BF16)
            f_ref[:, cols] = f
            f2 = f2 + jnp.dot(f, wdn_ref[cols, :], preferred_element_type=F32)
        f2hat, r2 = _rms_hat(f2)
        err = x1 + f2hat * gpost_ref[...] - t_ref[...]
        loss_ref[...] += 0.5 * jnp.sum(jnp.mean(err * err, axis=-1, keepdims=True), axis=0, keepdims=True)
        dx2 = err * (1.0 / D_MODEL)
        dgpost_ref[...] += jnp.sum(dx2 * f2hat, axis=0, keepdims=True)
        df2 = _rms_bwd(f2hat, r2, gpost_ref[...], dx2).astype(BF16)
        df2_ref[...] = df2
        dh2 = jnp.zeros((tm, D_MODEL), F32)
        for j in range(n_ff):
            cols = slice(SHARD_FF * j, SHARD_FF * (j + 1))
            df = lax.dot_general(df2, wdn_ref[cols, :], NT, preferred_element_type=F32)
            du = (2.0 * u_scr[:, cols] * df).astype(BF16)
            du_ref[:, cols] = du
            dh2 = dh2 + lax.dot_general(du, wup_ref[j], NT, preferred_element_type=F32)
        dgpre_ref[...] += jnp.sum(dh2 * x1hat, axis=0, keepdims=True)
        dx1_ref[...] = dx2 + _rms_bwd(x1hat, r1, gpre_ref[...], dh2)

    acc = pl.BlockSpec((1, D_MODEL), lambda i: (0, 0))
    return pl.pallas_call(
        body, name="mlp_fwd_bwd", grid=(S // tm,),
        in_specs=[_rows(tm, D_MODEL), _rows(tm, D_MODEL), _resident((1, D_MODEL)), _resident((1, D_MODEL)),
                  _resident((n_ff, D_MODEL, SHARD_FF)), _resident((D_FF, D_MODEL))],
        out_specs=[_rows(tm, D_MODEL), _rows(tm, D_FF), _rows(tm, D_FF), _rows(tm, D_MODEL), _rows(tm, D_MODEL),
                   acc, acc, pl.BlockSpec((1, 1), lambda i: (0, 0))],
        out_shape=[jax.ShapeDtypeStruct((S, D_MODEL), BF16), jax.ShapeDtypeStruct((S, D_FF), BF16),
                   jax.ShapeDtypeStruct((S, D_FF), BF16), jax.ShapeDtypeStruct((S, D_MODEL), BF16),
                   jax.ShapeDtypeStruct((S, D_MODEL), F32), jax.ShapeDtypeStruct((1, D_MODEL), F32),
                   jax.ShapeDtypeStruct((1, D_MODEL), F32), jax.ShapeDtypeStruct((1, 1), F32)],
        scratch_shapes=[pltpu.VMEM((tm, D_FF), F32)],
        compiler_params=_params("arbitrary"),
    )(x1, target, g_pre, g_post, w_up, w_down)


def _weight_grad(name, a, b, rows_sharded):
    S, K = a.shape
    N = b.shape[1]
    if rows_sharded:
        tk, tn = K // N_CHIPS, N
        a_spec = pl.BlockSpec((S, tk), lambda j: (0, j))
        b_spec = pl.BlockSpec((S, tn), lambda j: (0, 0), pipeline_mode=pl.Buffered(1))
    else:
        tk, tn = K, N // N_CHIPS
        a_spec = pl.BlockSpec((S, tk), lambda j: (0, 0), pipeline_mode=pl.Buffered(1))
        b_spec = pl.BlockSpec((S, tn), lambda j: (0, j))
    half = tk // 2

    def body(a_ref, b_ref, o_ref):
        res = lax.dot_general(a_ref[...], b_ref[...], TN, preferred_element_type=F32)
        o_ref[0, 0] = res[:half]
        o_ref[1, 0] = res[half:]

    return pl.pallas_call(
        body, name=name, grid=(N_CHIPS,), in_specs=[a_spec, b_spec],
        out_specs=pl.BlockSpec((2, 1, half, tn), lambda j: (0, j, 0, 0)),
        out_shape=jax.ShapeDtypeStruct((2, N_CHIPS, half, tn), F32),
        compiler_params=_params("parallel"),
    )(a, b)


def _mix_bwd(dx1, y2, ya, yx, bcu, conv_w, g_a, g_c, g_x, w_out, g_post, tm):
    S = dx1.shape[0]

    def body(dx1_ref, y2_ref, ya_ref, yx_ref, bcu_ref, before_ref, cw_ref, ga_ref, gc_ref, gx_ref, wo_ref, gp_ref,
             dy2_ref, dya_ref, delta_ref, dycx_ref, dgp_ref, dga_ref, dgc_ref, dgx_ref):
        @pl.when(pl.program_id(0) == 0)
        def _():
            for ref in (dgp_ref, dga_ref, dgc_ref, dgx_ref):
                ref[...] = jnp.zeros_like(ref)

        dx1 = dx1_ref[...]
        y2hat, r2 = _rms_hat(y2_ref[...])
        dgp_ref[...] += jnp.sum(dx1 * y2hat, axis=0, keepdims=True)
        dy2 = _rms_bwd(y2hat, r2, gp_ref[...], dx1).astype(BF16)
        dy2_ref[...] = dy2
        dycat = lax.dot_general(dy2, wo_ref[...], NT, preferred_element_type=F32)

        d_na = dycat[:, 0:ATTN_W]
        ya = ya_ref[...]
        yahat, ra = _rms_hat(ya)
        dga_ref[...] += jnp.sum(d_na * yahat, axis=0, keepdims=True)
        dya = _rms_bwd(yahat, ra, ga_ref[...], d_na)
        dya_ref[...] = dya
        prod = dya * ya
        hi = prod.astype(BF16)
        lo = (prod - hi.astype(F32)).astype(BF16)
        head_of = lambda axis: lax.shift_right_logical(lax.broadcasted_iota(jnp.int32, (ATTN_W, ATTN_W), axis),
                                                       HEAD.bit_length() - 1)
        same_head = head_of(0) == head_of(1)
        ones = jnp.where(same_head, 1.0, 0.0).astype(BF16)
        delta_ref[...] = (jnp.dot(hi, ones, preferred_element_type=F32) + jnp.dot(lo, ones, preferred_element_type=F32))

        b, _, _, _, _, _, cv = _conv_fwd(bcu_ref[...], before_ref[...], pl.program_id(0) == 0, cw_ref[...])
        d_nc = dycat[:, ATTN_W:ATTN_W + CONV_W]
        ychat, rc = _rms_hat(b * cv)
        dgc_ref[...] += jnp.sum(d_nc * ychat, axis=0, keepdims=True)
        dycx_ref[:, 0:CONV_W] = _rms_bwd(ychat, rc, gc_ref[...], d_nc)

        d_nx = dycat[:, ATTN_W + CONV_W:D_MODEL]
        yxhat, rx = _rms_hat(yx_ref[...])
        dgx_ref[...] += jnp.sum(d_nx * yxhat, axis=0, keepdims=True)
        dycx_ref[:, CONV_W:CONV_W + XATTN_W] = _rms_bwd(yxhat, rx, gx_ref[...], d_nx)

    acc = lambda w: pl.BlockSpec((1, w), lambda i: (0, 0))
    return pl.pallas_call(
        body, name="mix_bwd", grid=(S // tm,),
        in_specs=[_rows(tm, D_MODEL), _rows(tm, D_MODEL), _rows(tm, ATTN_W), _rows(tm, XATTN_W),
                  _rows(tm, 3 * CONV_W), _halo_before(tm, 3 * CONV_W), _resident((3, CONV_W)),
                  _resident((1, ATTN_W)), _resident((1, CONV_W)), _resident((1, XATTN_W)),
                  _resident((D_MODEL, D_MODEL)), _resident((1, D_MODEL))],
        out_specs=[_rows(tm, D_MODEL), _rows(tm, ATTN_W), _rows(tm, ATTN_W), _rows(tm, CONV_W + XATTN_W),
                   acc(D_MODEL), acc(ATTN_W), acc(CONV_W), acc(XATTN_W)],
        out_shape=[jax.ShapeDtypeStruct((S, D_MODEL), BF16), jax.ShapeDtypeStruct((S, ATTN_W), F32),
                   jax.ShapeDtypeStruct((S, ATTN_W), F32),
                   jax.ShapeDtypeStruct((S, CONV_W + XATTN_W), F32), jax.ShapeDtypeStruct((1, D_MODEL), F32),
                   jax.ShapeDtypeStruct((1, ATTN_W), F32), jax.ShapeDtypeStruct((1, CONV_W), F32),
                   jax.ShapeDtypeStruct((1, XATTN_W), F32)],
        compiler_params=_params("arbitrary"),
    )(dx1, y2, ya, yx, bcu, bcu, conv_w, g_a, g_c, g_x, w_out, g_post)


def _conv_xattn_bwd(dycx, bcu, qx, mkv, conv_w, tm):
    S = dycx.shape[0]
    n_mem = mkv.shape[0]
    n_tiles = S // tm

    def body(d_ref, dafter_ref, bcu_ref, before_ref, after_ref, qx_ref, mkv_ref, cw_ref,
             tail_ref, dmkv_ref, dcw_ref):
        i = pl.program_id(0)

        @pl.when(i == 0)
        def _():
            dmkv_ref[...] = jnp.zeros_like(dmkv_ref)
            dcw_ref[...] = jnp.zeros_like(dcw_ref)

        w = cw_ref[...]
        b, c, u, z, z1, z2, cv = _conv_fwd(bcu_ref[...], before_ref[...], i == 0, w)
        dyc = d_ref[:, 0:CONV_W]
        dcv = dyc * b
        dcv_after = jnp.where(i == n_tiles - 1, 0.0, dafter_ref[:, 0:CONV_W] * after_ref[:, 0:CONV_W])
        dz = w[2:3, :] * dcv + w[1:2, :] * _shift_up(dcv, dcv_after, 1) + w[0:1, :] * _shift_up(dcv, dcv_after, 2)
        dcw_ref[0:1, :] += jnp.sum(dcv * z2, axis=0, keepdims=True)
        dcw_ref[1:2, :] += jnp.sum(dcv * z1, axis=0, keepdims=True)
        dcw_ref[2:3, :] += jnp.sum(dcv * z, axis=0, keepdims=True)
        tail_ref[:, 0:CONV_W] = (dyc * cv).astype(BF16)
        tail_ref[:, CONV_W:2 * CONV_W] = (dz * u).astype(BF16)
        tail_ref[:, 2 * CONV_W:3 * CONV_W] = (dz * c).astype(BF16)

        qxb, mkvb = qx_ref[...], mkv_ref[...]
        for hd in range(XATTN_W // HEAD):
            sl = slice(HEAD * hd, HEAD * (hd + 1))
            vsl = slice(XATTN_W + HEAD * hd, XATTN_W + HEAD * (hd + 1))
            s = lax.dot_general(qxb[:, sl], mkvb[:, sl], NT, preferred_element_type=F32) * SCALE
            e = jnp.exp(s - jnp.max(s, axis=1, keepdims=True))
            p = e / jnp.sum(e, axis=1, keepdims=True)
            dob = d_ref[:, CONV_W + HEAD * hd:CONV_W + HEAD * (hd + 1)].astype(BF16)
            dp = lax.dot_general(dob, mkvb[:, vsl], NT, preferred_element_type=F32)
            ds = (p * (dp - jnp.sum(p * dp, axis=1, keepdims=True)) * SCALE).astype(BF16)
            tail_ref[:, 3 * CONV_W + HEAD * hd:3 * CONV_W + HEAD * (hd + 1)] = jnp.dot(
                ds, mkvb[:, sl], preferred_element_type=F32).astype(BF16)
            dmkv_ref[:, sl] += lax.dot_general(ds, qxb[:, sl], TN, preferred_element_type=F32)
            dmkv_ref[:, vsl] += lax.dot_general(p.astype(BF16), dob, TN, preferred_element_type=F32)

    width = CONV_W + XATTN_W
    return pl.pallas_call(
        body, name="conv_xattn_bwd", grid=(n_tiles,),
        in_specs=[_rows(tm, width), _halo_after(tm, width, S), _rows(tm, 3 * CONV_W), _halo_before(tm, 3 * CONV_W),
                  _halo_after(tm, 3 * CONV_W, S), _rows(tm, XATTN_W), _resident((n_mem, 2 * XATTN_W)),
                  _resident((3, CONV_W))],
        out_specs=[_rows(tm, 3 * CONV_W + XATTN_W), pl.BlockSpec((n_mem, 2 * XATTN_W), lambda i: (0, 0)),
                   pl.BlockSpec((3, CONV_W), lambda i: (0, 0))],
        out_shape=[jax.ShapeDtypeStruct((S, 3 * CONV_W + XATTN_W), BF16),
                   jax.ShapeDtypeStruct((n_mem, 2 * XATTN_W), F32), jax.ShapeDtypeStruct((3, CONV_W), F32)],
        compiler_params=_params("arbitrary"),
    )(dycx, dycx, bcu, bcu, bcu, qx, mkv, conv_w)


def _memkv_bwd(mem, g_mem, w_kv, dmkv):
    n_mem = mem.shape[0]
    half = D_MODEL // N_CHIPS // 2

    def body(mem_ref, g_ref, w_ref, d_ref, dw_ref, dg_ref):
        mhat, _ = _rms_hat(mem_ref[...])
        mn = (mhat * g_ref[...]).astype(BF16)
        d = d_ref[...].astype(BF16)
        for k in range(2 * N_CHIPS):
            dw_ref[k % 2, k // 2] = lax.dot_general(mn[:, half * k:half * (k + 1)], d, TN, preferred_element_type=F32)
        dmn = lax.dot_general(d, w_ref[...], NT, preferred_element_type=F32)
        dg_ref[...] = jnp.sum(dmn * mhat, axis=0, keepdims=True)

    return pl.pallas_call(
        body, name="memkv_bwd",
        out_shape=[jax.ShapeDtypeStruct((2, N_CHIPS, half, 2 * XATTN_W), F32), jax.ShapeDtypeStruct((1, D_MODEL), F32)],
        compiler_params=pltpu.CompilerParams(vmem_limit_bytes=VMEM_LIMIT_V7X),
    )(mem, g_mem, w_kv, dmkv)


def _in_proj_bwd(dqkv, tail, cos, sin, w_in, x, g, dx1, tm):
    S = x.shape[0]

    def body(dq_ref, dk_ref, dv_ref, tail_ref, cos_ref, sin_ref, w_ref, x_ref, g_ref, dx1_ref, dproj_ref, dx_ref, dg_ref):
        @pl.when(pl.program_id(0) == 0)
        def _():
            dg_ref[...] = jnp.zeros_like(dg_ref)

        c, s = cos_ref[...], sin_ref[...]
        for j in range(ATTN_W // 128):
            cols = slice(128 * j, 128 * (j + 1))
            dproj_ref[:, cols] = _rope128(dq_ref[:, cols] * SCALE, c, s, True).astype(BF16)
            dproj_ref[:, ATTN_W + 128 * j:ATTN_W + 128 * (j + 1)] = _rope128(dk_ref[:, cols], c, s, True).astype(BF16)
        dproj_ref[:, 2 * ATTN_W:3 * ATTN_W] = dv_ref[...].astype(BF16)
        dproj_ref[:, 3 * ATTN_W:PROJ_W] = tail_ref[...]
        dh = jnp.zeros((tm, D_MODEL), F32)
        for j in range(N_CHIPS):
            dh = dh + lax.dot_general(dproj_ref[:, SHARD_IN * j:SHARD_IN * (j + 1)], w_ref[j], NT,
                                      preferred_element_type=F32)
        xhat, r = _rms_hat(x_ref[...])
        dg_ref[...] += jnp.sum(dh * xhat, axis=0, keepdims=True)
        dx_ref[...] = dx1_ref[...] + _rms_bwd(xhat, r, g_ref[...], dh)

    return pl.pallas_call(
        body, name="in_proj_bwd", grid=(S // tm,),
        in_specs=[_rows(tm, ATTN_W)] * 3 + [_rows(tm, PROJ_W - 3 * ATTN_W), _rows(tm, 128), _rows(tm, 128),
                  _resident((N_CHIPS, D_MODEL, SHARD_IN)), _rows(tm, D_MODEL), _resident((1, D_MODEL)),
                  _rows(tm, D_MODEL)],
        out_specs=[_rows(tm, PROJ_W), _rows(tm, D_MODEL), pl.BlockSpec((1, D_MODEL), lambda i: (0, 0))],
        out_shape=[jax.ShapeDtypeStruct((S, PROJ_W), BF16), jax.ShapeDtypeStruct((S, D_MODEL), F32),
                   jax.ShapeDtypeStruct((1, D_MODEL), F32)],
        compiler_params=_params("arbitrary"),
    )(*dqkv, tail, cos, sin, w_in, x, g, dx1)


def _row_tile(rows):
    return ROW_TILE if rows % ROW_TILE == 0 else rows


def _chip_sum_bf16(name, grad, from_sibling, place):
    _, n, rows, cols = grad.shape
    tr = _row_tile(rows)

    def body(place_ref, g_ref, b_ref, o_ref):
        o_ref[...] = (g_ref[0] + b_ref[...]).astype(BF16)

    spec = pl.BlockSpec((1, tr, cols), lambda s, i, p: (s, i, 0))
    return pl.pallas_call(
        body, name=name, out_shape=jax.ShapeDtypeStruct((n, rows, cols), BF16),
        grid_spec=pltpu.PrefetchScalarGridSpec(
            num_scalar_prefetch=1, grid=(n, rows // tr),
            in_specs=[pl.BlockSpec((1, 1, tr, cols), lambda s, i, p: (p[0], s, i, 0)), spec], out_specs=spec),
        compiler_params=_params("parallel", "parallel"),
    )(place, grad, from_sibling)


def _final_sum(name, grad, from_sibling, others, place):
    _, _, rows, cols = grad.shape
    tr = _row_tile(rows)

    def body(place_ref, own_ref, sib_ref, o0, o1, o2, out_ref):
        acc = own_ref[0, 0] + sib_ref[0]
        for o in (o0, o1, o2):
            acc = acc + o[0].astype(F32)
        out_ref[...] = acc

    other = lambda k: pl.BlockSpec((1, tr, cols), lambda i, p: (k, i, 0))
    return pl.pallas_call(
        body, name=name, out_shape=jax.ShapeDtypeStruct((rows, cols), F32),
        grid_spec=pltpu.PrefetchScalarGridSpec(
            num_scalar_prefetch=1, grid=(rows // tr,),
            in_specs=[pl.BlockSpec((1, 1, tr, cols), lambda i, p: (p[0], p[1], i, 0)),
                      pl.BlockSpec((1, tr, cols), lambda i, p: (p[1], i, 0)), other(0), other(1), other(2)],
            out_specs=pl.BlockSpec((tr, cols), lambda i, p: (i, 0))),
        compiler_params=_params("parallel"),
    )(place, grad, from_sibling, others, others, others)


def _adamw_update(w, g, m, v):
    m = ADAM_B1 * m + (1.0 - ADAM_B1) * g
    v = ADAM_B2 * v + (1.0 - ADAM_B2) * (g * g)
    m_hat = m * (1.0 / (1.0 - ADAM_B1 ** ADAM_STEP))
    v_hat = v * (1.0 / (1.0 - ADAM_B2 ** ADAM_STEP))
    return -ADAM_LR * (m_hat / (jnp.sqrt(v_hat) + ADAM_EPS) + ADAM_WD * w), m, v


def _adamw(name, w, g, m, v, after):
    rows, cols = w.shape
    tr = _row_tile(rows)

    def body(w_ref, g_ref, m_ref, v_ref, after_ref, d_ref, nm_ref, nv_ref):
        d_ref[...], nm_ref[...], nv_ref[...] = _adamw_update(w_ref[...], g_ref[...], m_ref[...], v_ref[...])

    spec = pl.BlockSpec((tr, cols), lambda i: (i, 0))
    return pl.pallas_call(
        body, name=name, grid=(rows // tr,), in_specs=[spec] * 4 + [pl.BlockSpec(memory_space=pl.ANY)],
        out_specs=[spec] * 3, out_shape=[jax.ShapeDtypeStruct(w.shape, F32)] * 3, compiler_params=_params("parallel"),
    )(w, g, m, v, after)


def _small_update(summed, chip, gains, gains_m, gains_v, taps, taps_m, taps_v):
    n = len(gains)
    widths = [g.shape[1] for g in gains]
    k, w = taps.shape

    def body(*refs):
        chip_ref, sum_ref = refs[0], refs[1]
        params = [refs[2 + 3 * i:5 + 3 * i] for i in range(n + 1)]
        outs = [refs[2 + 3 * (n + 1) + 4 * i:2 + 3 * (n + 1) + 4 * (i + 1)] for i in range(n + 1)]
        loss_ref = refs[-1]
        for i in range(n):
            g = sum_ref[i:i + 1, 0:widths[i]]
            wr, mr, vr = params[i]
            outs[i][0][...] = g
            outs[i][1][...], outs[i][2][...], outs[i][3][...] = _adamw_update(wr[...], g, mr[...], vr[...])
        g = sum_ref[n:n + k, 0:w]
        for j in range(1, N_CHIPS):
            g = jnp.where(chip_ref[0] == j, sum_ref[n:n + k, w * j:w * (j + 1)], g)
        wr, mr, vr = params[n]
        outs[n][0][...] = g
        outs[n][1][...], outs[n][2][...], outs[n][3][...] = _adamw_update(wr[...], g, mr[...], vr[...])
        loss_ref[...] = sum_ref[n + k:n + k + 1, 0:1]

    vmem = pl.BlockSpec(memory_space=pltpu.VMEM)
    operands = [chip, summed]
    for p in zip(list(gains) + [taps], list(gains_m) + [taps_m], list(gains_v) + [taps_v]):
        operands += list(p)
    shapes = [jax.ShapeDtypeStruct(p.shape, F32) for p in list(gains) + [taps] for _ in range(4)]
    out = pl.pallas_call(
        body, name="small_update", out_shape=shapes + [jax.ShapeDtypeStruct((1, 1), F32)],
        in_specs=[pl.BlockSpec(memory_space=pltpu.SMEM)] + [vmem] * (len(operands) - 1),
        out_specs=[vmem] * (len(shapes) + 1),
    )(*operands)
    return [out[4 * i:4 * (i + 1)] for i in range(n + 1)], out[-1]


def _sum_blocks(name, blocks):
    n, rows, cols = blocks.shape

    def body(b_ref, o_ref):
        acc = b_ref[0]
        for k in range(1, n):
            acc = acc + b_ref[k]
        o_ref[...] = acc

    return pl.pallas_call(body, name=name, out_shape=jax.ShapeDtypeStruct((rows, cols), F32))(blocks)


def _place():
    return lax.axis_index("x"), lax.axis_index("y"), lax.axis_index("c")


def _other_chips(x, y):
    return [(1 - x, y), (x, 1 - y), (1 - x, 1 - y)]


def _weights_allgather(name, shards, landed=None):
    n = len(shards)
    first_hop = landed is None

    def body(*refs):
        ins, outs, stage = refs[:n], refs[-3 - 2 * n:-3 - n], refs[-3 - n:-3]
        send_sems, recv_sems, local_sems = refs[-3:]
        x, y, c = _place()
        me, sibling = (x, y, c), (x, y, 1 - c)
        chips = _other_chips(x, y)
        chip_index = lambda chip: 2 * chip[0] + chip[1]

        def copy(a, k, chip, half, to, src=None):
            place = outs[a].at[chip_index(chip), half]
            return pltpu.make_async_remote_copy(
                src_ref=place if src is None else src, dst_ref=place, send_sem=send_sems.at[6 * a + k],
                recv_sem=recv_sems.at[6 * a + k], device_id=to, device_id_type=MESH)

        load = [pltpu.make_async_copy(ins[a], stage[a], local_sems.at[a]) for a in range(n)]
        local = [pltpu.make_async_copy(stage[a], outs[a].at[chip_index((x, y))], local_sems.at[a]) for a in range(n)]
        for cp in load:
            cp.start()
        first = []
        if first_hop:
            first = [copy(a, k, (x, y), c, (*chip, c), src=ins[a].at[c]) for a in range(n) for k, chip in enumerate(chips)]
        for cp in first:
            cp.start()
        for a in range(n):
            load[a].wait()
            local[a].start()
        passed = []
        for a in range(n):
            for k, chip in enumerate(chips):
                if first_hop:
                    copy(a, k, chip, c, me).wait_recv()
                passed.append(copy(a, 3 + k, chip, c, sibling))
                passed[-1].start()
        for a in range(n):
            for k, chip in enumerate(chips):
                copy(a, 3 + k, chip, 1 - c, me).wait_recv()
        for cp in first + passed:
            cp.wait_send()
        for cp in local:
            cp.wait()

    any_spec = pl.BlockSpec(memory_space=pl.ANY)
    operands = list(shards) + ([] if first_hop else list(landed))
    return pl.pallas_call(
        body, name=name,
        out_shape=[jax.ShapeDtypeStruct((N_CHIPS,) + s.shape, s.dtype) for s in shards],
        in_specs=[any_spec] * len(operands), out_specs=[any_spec] * n,
        input_output_aliases={} if first_hop else {n + a: a for a in range(n)},
        scratch_shapes=[pltpu.VMEM(s.shape, s.dtype) for s in shards]
        + [pltpu.SemaphoreType.DMA((6 * n,)), pltpu.SemaphoreType.DMA((6 * n,)), pltpu.SemaphoreType.DMA((n,))],
        compiler_params=pltpu.CompilerParams(vmem_limit_bytes=VMEM_LIMIT_V7X),
    )(*operands)


def _plan_first_hop(x, y, c, shards, lands):
    return [(shards[a].at[c], lands[a].at[2 * x + y, c], lands[a].at[2 * chip[0] + chip[1], c], (*chip, c))
            for a in range(len(shards)) for chip in _other_chips(x, y)]


def _plan_other_half_to_sibling(x, y, c, grads, lands):
    return [(grads[a].at[1 - c], lands[a], lands[a], (x, y, 1 - c)) for a in range(len(grads))]


def _plan_to_other_chips(x, y, c, partials, lands):
    return [(partials[a].at[2 * chip[0] + chip[1]], lands[a].at[k], lands[a].at[k], (*chip, c))
            for a in range(len(partials)) for k, chip in enumerate(_other_chips(x, y))]


def _plan_to_all(x, y, c, blocks, lands):
    flips = [(fx, fy, fc) for fx in (0, 1) for fy in (0, 1) for fc in (0, 1) if (fx, fy, fc) != (0, 0, 0)]
    peers = [(1 - x if fx else x, 1 - y if fy else y, 1 - c if fc else c) for fx, fy, fc in flips]
    return [(blocks[0], lands[0].at[4 * x + 2 * y + c], lands[0].at[4 * p[0] + 2 * p[1] + p[2]], p) for p in peers]


def _planned_copies(plan, srcs, lands, send_sems, recv_sems):
    x, y, c = _place()

    def pair(k, src, there, here, to):
        make = lambda dst: pltpu.make_async_remote_copy(
            src_ref=src, dst_ref=dst, send_sem=send_sems.at[k], recv_sem=recv_sems.at[k], device_id=to, device_id_type=MESH)
        return make(there), make(here)

    return [pair(k, *entry) for k, entry in enumerate(plan(x, y, c, srcs, lands))]


_HBM_SPEC = pl.BlockSpec(memory_space=pltpu.HBM)
_SEM_SPEC = pl.BlockSpec(memory_space=pltpu.SEMAPHORE)


def _hbm(a):
    return pltpu.with_memory_space_constraint(a, pltpu.HBM)


def _exchange_start(name, plan, n_copies, srcs, land_shapes, after):
    ns, nl = len(srcs), len(land_shapes)
    n_in = ns + nl + 1

    def body(*refs):
        for send, _ in _planned_copies(plan, refs[:ns], refs[ns:ns + nl], refs[n_in], refs[n_in + 1]):
            send.start()
        refs[-1][...] = jnp.zeros_like(refs[-1])

    out = pl.pallas_call(
        body, name=name,
        out_shape=(pltpu.SemaphoreType.DMA((n_copies,)), pltpu.SemaphoreType.DMA((n_copies,)),
                   *[pltpu.HBM(s.shape, s.dtype) for s in land_shapes], jax.ShapeDtypeStruct((8, 128), F32)),
        in_specs=[_HBM_SPEC] * (ns + nl) + [pl.BlockSpec(memory_space=pl.ANY)],
        out_specs=(_SEM_SPEC, _SEM_SPEC, *[_HBM_SPEC] * nl, pl.BlockSpec(memory_space=pltpu.VMEM)),
        input_output_aliases={ns + i: 2 + i for i in range(nl)},
        compiler_params=pltpu.CompilerParams(has_side_effects=pltpu.SideEffectType.DATAFLOW_SIDE_EFFECTING),
    )(*[_hbm(s) for s in srcs], *[_hbm(lax.empty(s.shape, s.dtype)) for s in land_shapes], after)
    return out[0], out[1], list(out[2:2 + nl]), out[-1]


def _exchange_wait(name, plan, srcs, started, after):
    send_sems, recv_sems, lands, _ = started
    ns, nl = len(srcs), len(lands)
    after = list(after) if isinstance(after, (list, tuple)) else [after]

    def body(*refs):
        for send, recv in _planned_copies(plan, refs[:ns], refs[ns:ns + nl], refs[ns + nl], refs[ns + nl + 1]):
            send.wait_send()
            recv.wait_recv()

    return pl.pallas_call(
        body, name=name, out_shape=[pltpu.HBM(l.shape, l.dtype) for l in lands],
        in_specs=[_HBM_SPEC] * (ns + nl) + [_SEM_SPEC, _SEM_SPEC] + [pl.BlockSpec(memory_space=pl.ANY)] * len(after),
        out_specs=[_HBM_SPEC] * nl, input_output_aliases={ns + i: i for i in range(nl)},
        compiler_params=pltpu.CompilerParams(has_side_effects=pltpu.SideEffectType.DATAFLOW_SIDE_EFFECTING),
    )(*[_hbm(s) for s in srcs], *lands, send_sems, recv_sems, *after)


def _exchange_halves(name, halves, after):
    n = len(halves)

    def body(*refs):
        ins, outs, stage = refs[:n], refs[n + 1:2 * n + 1], refs[2 * n + 1:3 * n + 1]
        send_sems, recv_sems, local_sems = refs[3 * n + 1:]
        x, y, c = _place()
        load = [pltpu.make_async_copy(ins[a], stage[a], local_sems.at[a]) for a in range(n)]
        local = [pltpu.make_async_copy(stage[a], outs[a].at[c], local_sems.at[a]) for a in range(n)]
        remote = [pltpu.make_async_remote_copy(
            src_ref=stage[a], dst_ref=outs[a].at[c], send_sem=send_sems.at[a], recv_sem=recv_sems.at[a],
            device_id=(x, y, 1 - c), device_id_type=MESH) for a in range(n)]
        for cp in load:
            cp.start()
        for a in range(n):
            load[a].wait()
            remote[a].start()
            local[a].start()
        for a in range(n):
            pltpu.make_async_remote_copy(
                src_ref=ins[a], dst_ref=outs[a].at[1 - c], send_sem=send_sems.at[a], recv_sem=recv_sems.at[a],
                device_id=(x, y, 1 - c), device_id_type=MESH).wait_recv()
        for cp in remote:
            cp.wait_send()
        for cp in local:
            cp.wait()

    any_spec = pl.BlockSpec(memory_space=pl.ANY)
    return pl.pallas_call(
        body, name=name,
        out_shape=[jax.ShapeDtypeStruct((2,) + h.shape, h.dtype) for h in halves],
        in_specs=[any_spec] * (n + 1), out_specs=[any_spec] * n,
        scratch_shapes=[pltpu.VMEM(h.shape, h.dtype) for h in halves]
        + [pltpu.SemaphoreType.DMA((n,)), pltpu.SemaphoreType.DMA((n,)), pltpu.SemaphoreType.DMA((n,))],
        compiler_params=pltpu.CompilerParams(vmem_limit_bytes=VMEM_LIMIT_V7X),
    )(*halves, after)


def _like(arrays, lead, dtype=None):
    return [jax.ShapeDtypeStruct(tuple(lead) + a.shape[-2:], dtype or a.dtype) for a in arrays]


class _StepExchanges:
    def __init__(self, mats, conv_w):
        x, y, c = _place()
        self.place = jnp.stack([c, 2 * x + y]).astype(jnp.int32)
        shards = [w.astype(BF16).reshape(2, w.shape[0] // 2, w.shape[1]) for w in mats]
        self._in_shard = shards[:1]
        self._in = _exchange_start("w_in_allgather_start", _plan_first_hop, 3, self._in_shard,
                                   _like(self._in_shard, (N_CHIPS, 2)), shards[0])
        self.zero = self._in[3]
        taps = jnp.pad(conv_w, ((0, 8 - conv_w.shape[0]), (0, 128 - conv_w.shape[1])))
        self._rest_shards = shards[1:] + [jnp.stack([taps, jnp.zeros_like(taps)])]
        self._taps_shape = conv_w.shape
        self._groups = {}

    def w_in(self, after):
        landed = _exchange_wait("w_in_allgather_wait", _plan_first_hop, self._in_shard, self._in,
                                list(after) + self._rest_shards)
        (w_in,) = _weights_allgather("w_in_allgather_finish", self._in_shard, landed=landed)
        self._rest = _exchange_start("rest_allgather_start", _plan_first_hop, 3 * len(self._rest_shards),
                                     self._rest_shards, _like(self._rest_shards, (N_CHIPS, 2)), w_in)
        self.zero = self._rest[3]
        return w_in.reshape(N_CHIPS, 2 * w_in.shape[2], w_in.shape[3])

    def rest_weights(self, after):
        landed = _exchange_wait("rest_allgather_wait", _plan_first_hop, self._rest_shards, self._rest, after)
        *mats, taps = _weights_allgather("rest_allgather_finish", self._rest_shards, landed=landed)
        k, w = self._taps_shape
        taps = taps[:, 0, :k, :w].transpose(1, 0, 2).reshape(k, N_CHIPS * w)
        return [g.reshape(N_CHIPS, 2 * g.shape[2], g.shape[3]) for g in mats], taps

    def send_grads(self, key, grads):
        grads = list(grads)
        started = _exchange_start(f"{key}_grads_to_sibling_start", _plan_other_half_to_sibling, len(grads), grads,
                                  _like(grads, (N_CHIPS,)), self.zero)
        self._groups[key] = dict(grads=grads, to_sibling=started)
        self.zero = started[3]

    def grads_at_sibling(self, key, after):
        group = self._groups[key]
        grads = group["grads"]
        group["from_sibling"] = _exchange_wait(f"{key}_grads_to_sibling_wait", _plan_other_half_to_sibling, grads,
                                               group["to_sibling"], after)
        group["partials"] = [_chip_sum_bf16(f"{key}_chip_sum_{a}", grads[a], group["from_sibling"][a], self.place)
                             for a in range(len(grads))]
        group["to_chips"] = _exchange_start(f"{key}_grads_to_chips_start", _plan_to_other_chips, 3 * len(grads),
                                            group["partials"], _like(group["partials"], (3,)), self.zero)
        self.zero = group["to_chips"][3]

    def grads_summed(self, key, after):
        group = self._groups[key]
        from_chips = _exchange_wait(f"{key}_grads_to_chips_wait", _plan_to_other_chips, group["partials"],
                                    group["to_chips"], after)
        return [_final_sum(f"{key}_final_sum_{a}", group["grads"][a], group["from_sibling"][a], from_chips[a], self.place)
                for a in range(len(from_chips))]

    def send_small(self, block):
        self._small = block
        self._small_started = _exchange_start("small_grads_start", _plan_to_all, 7, [block],
                                              [jax.ShapeDtypeStruct((8,) + block.shape, block.dtype)], self.zero)
        self.zero = self._small_started[3]

    def small_summed(self, after):
        x, y, c = _place()
        (landed,) = _exchange_wait("small_grads_wait", _plan_to_all, [self._small], self._small_started, after)
        blocks = lax.dynamic_update_index_in_dim(landed, self._small, 4 * x + 2 * y + c, 0)
        return _sum_blocks("small_sum", blocks)


def _rope_tables(positions):
    half = HEAD // 2
    inv_freq = jnp.float32(ROPE_THETA) ** (-(jnp.arange(half, dtype=F32) * 2.0 / HEAD))
    ang = positions.astype(F32)[:, None] * inv_freq
    cos, sin = jnp.cos(ang), jnp.sin(ang)
    return jnp.tile(cos, (1, 4)), jnp.tile(jnp.concatenate([-sin, sin], axis=1), (1, 2))


def _local_step(x, mem, positions, target, gains, ex):
    g_pre_mix, g_mem, g_a, g_c, g_x, g_post_mix, g_pre_mlp, g_post_mlp = gains
    tm = ROW_TILE
    cos, sin = _rope_tables(positions)
    h = _pre_norm(x, g_pre_mix + ex.zero[:1, :1], tm)
    w_in = ex.w_in([h, cos, sin])

    q, k, v, bcu, qx = _in_proj_fwd(h, w_in, cos, sin, ex.zero, tm)
    ya, lse = _attn_fwd(q, k, v)
    (w_kv, w_out, w_up, w_down), conv_w = ex.rest_weights(lse)
    w_kv, w_out, w_down = (w.reshape(N_CHIPS * w.shape[1], w.shape[2]) for w in (w_kv, w_out, w_down))
    memn, mkv = _memkv_fwd(mem, g_mem, w_kv)
    yx, ycat, y2, x1 = _mix_fwd(ya, bcu, qx, mkv, conv_w, g_a, g_c, g_x, w_out, g_post_mix, x, tm)
    h2, f, du, df2, dx1, dg_pre_mlp, dg_post_mlp, loss = _mlp_fwd_bwd(x1, target, g_pre_mlp, g_post_mlp, w_up, w_down,
                                                                      MLP_ROW_TILE)
    gw_down = _weight_grad("grad_w_down", f, df2, True)
    gw_up = _weight_grad("grad_w_up", h2, du, False)
    ex.send_grads("early", [gw_up, gw_down])

    dy2, dya, delta, dycx, dg_post_mix, dg_a, dg_c, dg_x = _mix_bwd(dx1, y2, ya, yx, bcu, conv_w, g_a, g_c, g_x,
                                                                  w_out, g_post_mix + ex.zero[:1, :1], tm)
    ex.grads_at_sibling("early", dy2)
    gw_out = _weight_grad("grad_w_out", ycat, dy2, True)
    tail, dmkv, g_conv = _conv_xattn_bwd(dycx, bcu, qx, mkv, conv_w + ex.zero[:1, :1], tm)
    gw_kv, dg_mem = _memkv_bwd(mem, g_mem, w_kv, dmkv)
    ex.send_grads("mid", [gw_out, gw_kv])
    dqkv = _attn_bwd(q, k, v, dya, lse, delta, ex.zero)
    ex.grads_at_sibling("mid", dqkv[0])
    dproj, grad_x, dg_pre_mix = _in_proj_bwd(dqkv, tail, cos, sin, w_in, x, g_pre_mix + ex.zero[:1, :1], dx1, tm)
    gain_grads = [dg_pre_mix, dg_mem, dg_a, dg_c, dg_x, dg_post_mix, dg_pre_mlp, dg_post_mlp]
    ex.send_small(_pack_small(gain_grads, g_conv, loss))
    gw_in = _weight_grad("grad_w_in", h, dproj, False)
    ex.send_grads("late", [gw_in])
    return grad_x


def _pack_small(gains, conv, scalar=None):
    rows = [jnp.pad(g, ((0, 0), (0, D_MODEL - g.shape[1]))) for g in gains]
    rows.append(jnp.pad(conv, ((0, 0), (0, D_MODEL - conv.shape[1]))))
    last = jnp.zeros((SMALL_ROWS - 8 - conv.shape[0], D_MODEL), F32)
    rows.append(last if scalar is None else last.at[0:1, 0:1].set(scalar))
    return jnp.concatenate(rows, axis=0)


def _unpack_small(block, gain_widths, conv_width):
    gains = [block[i:i + 1, :w] for i, w in enumerate(gain_widths)]
    return gains, block[8:11, :conv_width], block[11, 0]


def kernel(x, mem, positions, g_pre_mix, g_mem, w_in, w_mem_kv, conv_w, g_attn_out, g_conv_out, g_xattn_out, w_out, g_post_mix, g_pre_mlp, w_up, w_down, g_post_mlp, loss_target, m_g_pre_mix, m_g_mem, m_w_in, m_w_mem_kv, m_conv_w, m_g_attn_out, m_g_conv_out, m_g_xattn_out, m_w_out, m_g_post_mix, m_g_pre_mlp, m_w_up, m_w_down, m_g_post_mlp, v_g_pre_mix, v_g_mem, v_w_in, v_w_mem_kv, v_conv_w, v_g_attn_out, v_g_conv_out, v_g_xattn_out, v_w_out, v_g_post_mix, v_g_pre_mlp, v_w_up, v_w_down, v_g_post_mlp):
    cx, cy, cc = _place()
    chip = 2 * cx + cy
    gains = [g_pre_mix, g_mem, g_attn_out, g_conv_out, g_xattn_out, g_post_mix, g_pre_mlp, g_post_mlp]
    gains_m = [m_g_pre_mix, m_g_mem, m_g_attn_out, m_g_conv_out, m_g_xattn_out, m_g_post_mix, m_g_pre_mlp, m_g_post_mlp]
    gains_v = [v_g_pre_mix, v_g_mem, v_g_attn_out, v_g_conv_out, v_g_xattn_out, v_g_post_mix, v_g_pre_mlp, v_g_post_mlp]
    gain_widths = [g.shape[1] for g in gains]
    mats = [w_in[0], w_mem_kv[0], w_out[0], w_up[0], w_down[0]]
    mats_m = [m_w_in[0], m_w_mem_kv[0], m_w_out[0], m_w_up[0], m_w_down[0]]
    mats_v = [v_w_in[0], v_w_mem_kv[0], v_w_out[0], v_w_up[0], v_w_down[0]]

    ex = _StepExchanges(mats, conv_w[0])
    grad_x = _local_step(x[0], mem[0], positions[0], loss_target[0], gains, ex)

    both = lambda halves: [t.reshape(2 * t.shape[1], t.shape[2]) for t in halves]
    done = ex.grads_summed("early", ex.zero) + ex.grads_summed("mid", ex.zero)
    ex.grads_at_sibling("late", sum(t[:8, :128] for t in done))
    up_sum, down_sum, out_sum, kv_sum = both(_exchange_halves("sums_to_sibling", done, ex.zero))
    adamw = lambda a, g, after: _adamw(f"adamw_{a}", mats[a], g, mats_m[a], mats_v[a], after)
    new_up, new_down, new_out, new_kv = adamw(3, up_sum, ex.zero), adamw(4, down_sum, ex.zero), adamw(2, out_sum, ex.zero), adamw(1, kv_sum, ex.zero)

    small, total = _small_update(ex.small_summed(new_kv[0]), chip.reshape(1).astype(jnp.int32), gains, gains_m,
                                 gains_v, conv_w[0], m_conv_w[0], v_conv_w[0])

    (in_half,) = ex.grads_summed("late", small[0][1])
    (in_sum,) = both(_exchange_halves("late_sum_to_sibling", [in_half], in_half))
    new_in = adamw(0, in_sum, in_sum)
    mat_sums = [in_sum, kv_sum, out_sum, up_sum, down_sum]
    mat_new = [new_in, new_kv, new_out, new_up, new_down]

    order = ["g_pre_mix", "g_mem", "w_in", "w_mem_kv", "conv_w", "g_attn_out", "g_conv_out", "g_xattn_out", "w_out",
             "g_post_mix", "g_pre_mlp", "w_up", "w_down", "g_post_mlp"]
    gain_names = ["g_pre_mix", "g_mem", "g_attn_out", "g_conv_out", "g_xattn_out", "g_post_mix", "g_pre_mlp", "g_post_mlp"]
    mat_names = ["w_in", "w_mem_kv", "w_out", "w_up", "w_down"]

    def leaf(kind, name):
        if name in gain_names:
            return small[gain_names.index(name)][kind]
        if name == "conv_w":
            return small[len(gain_names)][kind][None]
        a = mat_names.index(name)
        return (mat_sums[a] if kind == 0 else mat_new[a][kind - 1])[None]

    return (total[0, 0], grad_x[None], *[leaf(kind, name) for kind in range(4) for name in order])
```

```python
import jax
import jax.numpy as jnp
from jax import lax
from jax.experimental import pallas as pl
from jax.experimental.pallas import tpu as pltpu

F32, BF16 = jnp.float32, jnp.bfloat16

D_MODEL = 1024
ATTN_W = 512
CONV_W = 256
XATTN_W = 256
PROJ_W = 3 * ATTN_W + 3 * CONV_W + XATTN_W
D_FF = 4096
HEAD = 64
N_BACK = 128
DILATIONS = (1, 4, 16)
ROPE_THETA = 10000.0
EPS = 1e-6
NEG_INF = -1e30
SCALE = HEAD ** -0.5
N_CHIPS = 4
SHARD_IN = PROJ_W // N_CHIPS
SHARD_FF = D_FF // N_CHIPS

ADAM_LR, ADAM_B1, ADAM_B2, ADAM_EPS, ADAM_WD, ADAM_STEP = 0.001, 0.9, 0.999, 1e-08, 0.01, 10

VMEM_LIMIT_V7X = 56 * 1024 * 1024
ROW_TILE = 512
MLP_ROW_TILE = 256
ADAMW_ROW_TILE = 256
SMALL_ROWS = 16

NT = (((1,), (1,)), ((), ()))
TN = (((0,), (0,)), ((), ()))
MESH = pl.DeviceIdType.MESH


def _params(*sem):
    return pltpu.CompilerParams(dimension_semantics=sem, vmem_limit_bytes=VMEM_LIMIT_V7X)


def _resident(shape):
    return pl.BlockSpec(shape, lambda *_: (0,) * len(shape), pipeline_mode=pl.Buffered(1))


def _rows(tm, width):
    return pl.BlockSpec((tm, width), lambda i: (i, 0))


def _rms_hat(x):
    r = lax.rsqrt(jnp.mean(x * x, axis=-1, keepdims=True) + EPS)
    return x * r, r


def _rms_bwd(xhat, r, g, dy):
    gdy = dy * g
    return r * (gdy - xhat * jnp.mean(xhat * gdy, axis=-1, keepdims=True))


def _rope128(t, cos, sin_signed, inverse):
    lane = lax.broadcasted_iota(jnp.int32, t.shape, 1)
    first_half = (lane % HEAD) < (HEAD // 2)
    rot = jnp.where(first_half, pltpu.roll(t, 128 - HEAD // 2, 1), pltpu.roll(t, HEAD // 2, 1))
    return t * cos - rot * sin_signed if inverse else t * cos + rot * sin_signed


def _pre_norm(x, g, tm):
    S = x.shape[0]

    def body(x_ref, g_ref, h_ref):
        h_ref[...] = (_rms_hat(x_ref[...])[0] * g_ref[...]).astype(BF16)

    return pl.pallas_call(
        body, name="pre_norm", grid=(S // tm,), in_specs=[_rows(tm, D_MODEL), _resident((1, D_MODEL))],
        out_specs=_rows(tm, D_MODEL), out_shape=jax.ShapeDtypeStruct((S, D_MODEL), BF16),
        compiler_params=_params("parallel"),
    )(x, g)


def _in_proj_fwd(h, w_in, cos, sin, after, tm):
    S = h.shape[0]

    def body(h_ref, w_ref, cos_ref, sin_ref, after_ref, q_ref, k_ref, v_ref, bcu_ref, qx_ref, proj):
        h = h_ref[...]
        for j in range(N_CHIPS):
            proj[:, SHARD_IN * j:SHARD_IN * (j + 1)] = jnp.dot(h, w_ref[j], preferred_element_type=F32)
        c, s = cos_ref[...], sin_ref[...]
        for j in range(ATTN_W // 128):
            lo = 128 * j
            q_ref[:, lo:lo + 128] = _rope128(proj[:, lo:lo + 128], c, s, False) * SCALE
            k_ref[:, lo:lo + 128] = _rope128(proj[:, ATTN_W + lo:ATTN_W + lo + 128], c, s, False)
        v_ref[...] = proj[:, 2 * ATTN_W:3 * ATTN_W]
        bcu_ref[...] = proj[:, 3 * ATTN_W:3 * ATTN_W + 3 * CONV_W]
        qx_ref[...] = proj[:, 3 * ATTN_W + 3 * CONV_W:PROJ_W].astype(BF16)

    return pl.pallas_call(
        body, name="in_proj_fwd", grid=(S // tm,),
        in_specs=[_rows(tm, D_MODEL), _resident((N_CHIPS, D_MODEL, SHARD_IN)), _rows(tm, 128), _rows(tm, 128),
                  pl.BlockSpec(memory_space=pl.ANY)],
        out_specs=[_rows(tm, ATTN_W), _rows(tm, ATTN_W), _rows(tm, ATTN_W), _rows(tm, 3 * CONV_W), _rows(tm, XATTN_W)],
        out_shape=[jax.ShapeDtypeStruct((S, ATTN_W), F32), jax.ShapeDtypeStruct((S, ATTN_W), F32),
                   jax.ShapeDtypeStruct((S, ATTN_W), F32), jax.ShapeDtypeStruct((S, 3 * CONV_W), F32),
                   jax.ShapeDtypeStruct((S, XATTN_W), BF16)],
        scratch_shapes=[pltpu.VMEM((tm, PROJ_W), F32)],
        compiler_params=_params("parallel"),
    )(h, w_in, cos, sin, after)


def _memkv_fwd(mem, g_mem, w_kv):
    n_mem = mem.shape[0]

    def body(mem_ref, g_ref, w_ref, mn_ref, kv_ref):
        mhat, _ = _rms_hat(mem_ref[...])
        mn = (mhat * g_ref[...]).astype(BF16)
        mn_ref[...] = mn
        kv_ref[...] = jnp.dot(mn, w_ref[...], preferred_element_type=F32).astype(BF16)

    return pl.pallas_call(
        body, name="memkv_fwd",
        out_shape=[jax.ShapeDtypeStruct((n_mem, D_MODEL), BF16), jax.ShapeDtypeStruct((n_mem, 2 * XATTN_W), BF16)],
        compiler_params=pltpu.CompilerParams(vmem_limit_bytes=VMEM_LIMIT_V7X),
    )(mem, g_mem, w_kv)


def _fill_band_bias(bias):
    row = lax.broadcasted_iota(jnp.int32, (N_BACK, 2 * N_BACK), 0)
    col = lax.broadcasted_iota(jnp.int32, (N_BACK, 2 * N_BACK), 1)
    band = (col >= row) & (col <= row + N_BACK)
    bias[1] = jnp.where(band, 0.0, NEG_INF)
    bias[0] = jnp.where(band & (col >= N_BACK), 0.0, NEG_INF)


def _strided(start, size, d):
    return pl.ds(start, size) if d == 1 else pl.ds(start, size, stride=d)


def _group_starts(g, G, nb, d):
    t0 = g * G
    r, n0 = lax.shift_right_logical(t0, nb.bit_length() - 1), lax.bitwise_and(t0, nb - 1)
    first = r + n0 * (N_BACK * d)
    before = r + jnp.maximum(n0 - 1, 0) * (N_BACK * d)
    starts = [before] + [first + u * (N_BACK * d) for u in range(G)]
    if d == 1:
        starts = [pl.multiple_of(st, N_BACK) for st in starts]
    return starts, n0


def _step_blocks(i, U, nb, d):
    G = min(U, nb)
    row_blocks, blocks = [], []
    for grp in range(U // G):
        starts, n0 = _group_starts(i * (U // G) + grp, G, nb, d)
        base = len(row_blocks)
        row_blocks += [_strided(st, N_BACK, d) for st in starts]
        for u in range(G):
            blocks.append((base + u, base + u + 1, jnp.minimum(n0, 1) if u == 0 else 1))
    return row_blocks, blocks


def _by_head(a, b):
    lane = lax.broadcasted_iota(jnp.int32, (a.shape[0], 2 * HEAD), 1)
    return jnp.where(lane < HEAD, a, b)


def _head_only(t, hh):
    lane = lax.broadcasted_iota(jnp.int32, t.shape, 1)
    return jnp.where((lane < HEAD) == (hh == 0), t, jnp.zeros_like(t))


def _stack_heads(t):
    return jnp.concatenate([_head_only(t, 0), _head_only(t, 1)], axis=0)


def _head_columns(t):
    return jnp.concatenate([t[:, 0:1], t[:, HEAD:HEAD + 1]], axis=0)


def _unstack(t):
    return _by_head(t[:N_BACK], t[N_BACK:])


def _unstack_columns(t):
    return _by_head(jnp.broadcast_to(t[:N_BACK], (N_BACK, 2 * HEAD)), jnp.broadcast_to(t[N_BACK:], (N_BACK, 2 * HEAD)))


FWD_BLOCKS_PER_STEP = 4
BWD_BLOCKS_PER_STEP = 2


def _attn_fwd(q, k, v):
    S = q.shape[0]
    U = FWD_BLOCKS_PER_STEP

    def body(q_ref, k_ref, v_ref, y_ref, m_ref, l_scr, bias):
        _fill_band_bias(bias)
        for g, d in enumerate(DILATIONS):
            nb = S // d // N_BACK
            first_pattern, last_pattern = g == 0, g == len(DILATIONS) - 1

            def step(i, carry, d=d, nb=nb, first_pattern=first_pattern, last_pattern=last_pattern):
                row_blocks, blocks = _step_blocks(i, U, nb, d)
                kb = [k_ref[r, :].astype(BF16) for r in row_blocks]
                ss = []
                for before, own, which in blocks:
                    kw = jnp.concatenate([kb[before], kb[own]], 0)
                    qs = _stack_heads(q_ref[row_blocks[own], :].astype(BF16))
                    b = bias[which]
                    ss.append(lax.dot_general(qs, kw, NT, preferred_element_type=F32) + jnp.concatenate([b, b], axis=0))
                ms = [jnp.max(s, axis=1, keepdims=True) for s in ss]
                ps = [jnp.exp(s - m) for s, m in zip(ss, ms)]
                ls = [jnp.sum(p, axis=1, keepdims=True) for p in ps]
                vb = [v_ref[r, :].astype(BF16) for r in row_blocks]
                os_ = [jnp.dot(ps[u].astype(BF16), jnp.concatenate([vb[before], vb[own]], 0), preferred_element_type=F32)
                       for u, (before, own, _) in enumerate(blocks)]
                for u, (_, own, _) in enumerate(blocks):
                    o_g, m_g, l_g = _unstack(os_[u]), _unstack_columns(ms[u]), _unstack_columns(ls[u])
                    r = row_blocks[own]
                    if first_pattern:
                        m_new, l_new, acc = m_g, l_g, o_g
                    else:
                        m_old = m_ref[r, :]
                        m_new = jnp.maximum(m_old, m_g)
                        alpha, beta = jnp.exp(m_old - m_new), jnp.exp(m_g - m_new)
                        l_new = l_scr[r, :] * alpha + l_g * beta
                        acc = y_ref[r, :] * alpha + o_g * beta
                    if last_pattern:
                        y_ref[r, :] = acc / l_new
                        m_ref[r, :] = m_new + jnp.log(l_new)
                    else:
                        y_ref[r, :] = acc
                        m_ref[r, :] = m_new
                        l_scr[r, :] = l_new
                return carry

            lax.fori_loop(0, d * nb // U, step, 0)

    col = pl.BlockSpec((S, 2 * HEAD), lambda j: (0, j))
    return pl.pallas_call(
        body, name="attn_fwd", grid=(q.shape[1] // (2 * HEAD),),
        in_specs=[col, col, col], out_specs=[col, col],
        out_shape=[jax.ShapeDtypeStruct(q.shape, F32)] * 2,
        scratch_shapes=[pltpu.VMEM((S, 2 * HEAD), F32), pltpu.VMEM((2, N_BACK, 2 * N_BACK), F32)],
        compiler_params=_params("parallel"),
    )(q, k, v)


def _attn_bwd(q, k, v, dy, lse, delta, after):
    S = q.shape[0]
    U = BWD_BLOCKS_PER_STEP

    def body(q_ref, k_ref, v_ref, dy_ref, lse_ref, delta_ref, after_ref, dq_ref, dk_ref, dv_ref, bias):
        _fill_band_bias(bias)
        dk_ref[...] = jnp.zeros_like(dk_ref)
        dv_ref[...] = jnp.zeros_like(dv_ref)
        for g, d in enumerate(DILATIONS):
            nb = S // d // N_BACK

            def step(i, carry, d=d, nb=nb, g=g):
                row_blocks, blocks = _step_blocks(i, U, nb, d)
                kb = [k_ref[r, :].astype(BF16) for r in row_blocks]
                vb = [v_ref[r, :].astype(BF16) for r in row_blocks]
                kws = [jnp.concatenate([kb[before], kb[own]], 0) for before, own, _ in blocks]
                vws = [jnp.concatenate([vb[before], vb[own]], 0) for before, own, _ in blocks]
                qss = [_stack_heads(q_ref[row_blocks[own], :].astype(BF16)) for _, own, _ in blocks]
                doss = [_stack_heads(dy_ref[row_blocks[own], :].astype(BF16)) for _, own, _ in blocks]
                ss, dps = [], []
                for u, (_, _, which) in enumerate(blocks):
                    b = bias[which]
                    ss.append(lax.dot_general(qss[u], kws[u], NT, preferred_element_type=F32) + jnp.concatenate([b, b], axis=0))
                    dps.append(lax.dot_general(doss[u], vws[u], NT, preferred_element_type=F32))
                ps = [jnp.exp(ss[u] - _head_columns(lse_ref[row_blocks[own], :])) for u, (_, own, _) in enumerate(blocks)]
                dss = [(ps[u] * (dps[u] - _head_columns(delta_ref[row_blocks[own], :]))).astype(BF16)
                       for u, (_, own, _) in enumerate(blocks)]
                pbs = [p.astype(BF16) for p in ps]
                dqs = [jnp.dot(dss[u], kws[u], preferred_element_type=F32) for u in range(U)]
                dkws = [lax.dot_general(dss[u], qss[u], TN, preferred_element_type=F32) for u in range(U)]
                dvws = [lax.dot_general(pbs[u], doss[u], TN, preferred_element_type=F32) for u in range(U)]
                dk_parts, dv_parts = [None] * len(row_blocks), [None] * len(row_blocks)
                for u, (before, own, _) in enumerate(blocks):
                    dq = _unstack(dqs[u])
                    if g == 0:
                        dq_ref[row_blocks[own], :] = dq
                    else:
                        dq_ref[row_blocks[own], :] += dq
                    for idx, dkp, dvp in ((before, dkws[u][:N_BACK], dvws[u][:N_BACK]),
                                          (own, dkws[u][N_BACK:], dvws[u][N_BACK:])):
                        dk_parts[idx] = dkp if dk_parts[idx] is None else dk_parts[idx] + dkp
                        dv_parts[idx] = dvp if dv_parts[idx] is None else dv_parts[idx] + dvp
                for idx, r in enumerate(row_blocks):
                    dk_ref[r, :] += dk_parts[idx]
                    dv_ref[r, :] += dv_parts[idx]
                return carry

            lax.fori_loop(0, d * nb // U, step, 0)

    col = pl.BlockSpec((S, 2 * HEAD), lambda j: (0, j))
    return pl.pallas_call(
        body, name="attn_bwd", grid=(q.shape[1] // (2 * HEAD),),
        in_specs=[col] * 6 + [pl.BlockSpec(memory_space=pl.ANY)], out_specs=[col] * 3,
        out_shape=[jax.ShapeDtypeStruct(q.shape, F32)] * 3,
        scratch_shapes=[pltpu.VMEM((2, N_BACK, 2 * N_BACK), F32)],
        compiler_params=_params("parallel"),
    )(q, k, v, dy, lse, delta, after)


def _shift_down(z, before, k):
    row = lax.broadcasted_iota(jnp.int32, z.shape, 0)
    out = pltpu.roll(z, k, 0)
    for i in range(k):
        out = jnp.where(row == i, before[8 - k + i:8 - k + i + 1, :], out)
    return out


def _shift_up(z, after, k):
    rows = z.shape[0]
    row = lax.broadcasted_iota(jnp.int32, z.shape, 0)
    out = pltpu.roll(z, rows - k, 0)
    for i in range(k):
        out = jnp.where(row == rows - k + i, after[i:i + 1, :], out)
    return out


def _conv_fwd(bcu, before, is_first, w):
    b, c, u = bcu[:, 0:CONV_W], bcu[:, CONV_W:2 * CONV_W], bcu[:, 2 * CONV_W:3 * CONV_W]
    z = c * u
    zb = jnp.where(is_first, 0.0, before[:, CONV_W:2 * CONV_W] * before[:, 2 * CONV_W:3 * CONV_W])
    z1, z2 = _shift_down(z, zb, 1), _shift_down(z, zb, 2)
    cv = w[0:1, :] * z2 + w[1:2, :] * z1 + w[2:3, :] * z
    return b, c, u, z, z1, z2, cv


def _halo_before(tm, width):
    return pl.BlockSpec((8, width), lambda i: (jnp.maximum(i * (tm // 8) - 1, 0), 0))


def _halo_after(tm, width, S):
    return pl.BlockSpec((8, width), lambda i: (jnp.minimum((i + 1) * (tm // 8), S // 8 - 1), 0))


def _mix_fwd(ya, bcu, qx, mkv, conv_w, g_a, g_c, g_x, w_out, g_post, x, tm):
    S = x.shape[0]

    def body(ya_ref, bcu_ref, before_ref, qx_ref, mkv_ref, cw_ref, ga_ref, gc_ref, gx_ref,
             wo_ref, gp_ref, x_ref, yx_ref, ycat_ref, y2_ref, x1_ref):
        ya = ya_ref[...]
        b, _, _, _, _, _, cv = _conv_fwd(bcu_ref[...], before_ref[...], pl.program_id(0) == 0, cw_ref[...])
        yc = b * cv

        qxb, mkvb = qx_ref[...], mkv_ref[...]
        for hd in range(XATTN_W // HEAD):
            sl = slice(HEAD * hd, HEAD * (hd + 1))
            s = lax.dot_general(qxb[:, sl], mkvb[:, sl], NT, preferred_element_type=F32) * SCALE
            mx = jnp.max(s, axis=1, keepdims=True)
            p = jnp.exp(s - mx)
            l = jnp.sum(p, axis=1, keepdims=True)
            vm = mkvb[:, XATTN_W + HEAD * hd:XATTN_W + HEAD * (hd + 1)]
            yx_ref[:, sl] = jnp.dot(p.astype(BF16), vm, preferred_element_type=F32) / l
        yx = yx_ref[...]

        ycat_ref[:, 0:ATTN_W] = (_rms_hat(ya)[0] * ga_ref[...]).astype(BF16)
        ycat_ref[:, ATTN_W:ATTN_W + CONV_W] = (_rms_hat(yc)[0] * gc_ref[...]).astype(BF16)
        ycat_ref[:, ATTN_W + CONV_W:D_MODEL] = (_rms_hat(yx)[0] * gx_ref[...]).astype(BF16)
        y2 = jnp.dot(ycat_ref[...], wo_ref[...], preferred_element_type=F32)
        y2_ref[...] = y2
        x1_ref[...] = x_ref[...] + _rms_hat(y2)[0] * gp_ref[...]

    n_mem = mkv.shape[0]
    return pl.pallas_call(
        body, name="mix_fwd", grid=(S // tm,),
        in_specs=[_rows(tm, ATTN_W), _rows(tm, 3 * CONV_W), _halo_before(tm, 3 * CONV_W), _rows(tm, XATTN_W),
                  _resident((n_mem, 2 * XATTN_W)), _resident((3, CONV_W)), _resident((1, ATTN_W)),
                  _resident((1, CONV_W)), _resident((1, XATTN_W)), _resident((D_MODEL, D_MODEL)),
                  _resident((1, D_MODEL)), _rows(tm, D_MODEL)],
        out_specs=[_rows(tm, XATTN_W), _rows(tm, D_MODEL), _rows(tm, D_MODEL), _rows(tm, D_MODEL)],
        out_shape=[jax.ShapeDtypeStruct((S, XATTN_W), F32), jax.ShapeDtypeStruct((S, D_MODEL), BF16),
                   jax.ShapeDtypeStruct((S, D_MODEL), F32), jax.ShapeDtypeStruct((S, D_MODEL), F32)],
        compiler_params=_params("parallel"),
    )(ya, bcu, bcu, qx, mkv, conv_w, g_a, g_c, g_x, w_out, g_post, x)


def _mlp_fwd_bwd(x1, target, g_pre, g_post, w_up, w_down, tm):
    S = x1.shape[0]
    n_ff = D_FF // SHARD_FF

    def body(x1_ref, t_ref, gpre_ref, gpost_ref, wup_ref, wdn_ref,
             h2_ref, f_ref, du_ref, df2_ref, dx1_ref, dgpre_ref, dgpost_ref, loss_ref, u_scr):
        @pl.when(pl.program_id(0) == 0)
        def _():
            dgpre_ref[...] = jnp.zeros_like(dgpre_ref)
            dgpost_ref[...] = jnp.zeros_like(dgpost_ref)
            loss_ref[...] = jnp.zeros_like(loss_ref)

        x1 = x1_ref[...]
        x1hat, r1 = _rms_hat(x1)
        h2 = (x1hat * gpre_ref[...]).astype(BF16)
        h2_ref[...] = h2
        f2 = jnp.zeros((tm, D_MODEL), F32)
        for j in range(n_ff):
            cols = slice(SHARD_FF * j, SHARD_FF * (j + 1))
            u = jnp.maximum(jnp.dot(h2, wup_ref[j], preferred_element_type=F32), 0.0)
            u_scr[:, cols] = u
            f = (u * u).astype(BF16)
            f_ref[:, cols] = f
            f2 = f2 + jnp.dot(f, wdn_ref[cols, :], preferred_element_type=F32)
        f2hat, r2 = _rms_hat(f2)
        err = x1 + f2hat * gpost_ref[...] - t_ref[...]
        loss_ref[...] += 0.5 * jnp.sum(jnp.mean(err * err, axis=-1, keepdims=True), axis=0, keepdims=True)
        dx2 = err * (1.0 / D_MODEL)
        dgpost_ref[...] += jnp.sum(dx2 * f2hat, axis=0, keepdims=True)
        df2 = _rms_bwd(f2hat, r2, gpost_ref[...], dx2).astype(BF16)
        df2_ref[...] = df2
        dh2 = jnp.zeros((tm, D_MODEL), F32)
        for j in range(n_ff):
            cols = slice(SHARD_FF * j, SHARD_FF * (j + 1))
            df = lax.dot_general(df2, wdn_ref[cols, :], NT, preferred_element_type=F32)
            du = (2.0 * u_scr[:, cols] * df).astype(BF16)
            du_ref[:, cols] = du
            dh2 = dh2 + lax.dot_general(du, wup_ref[j], NT, preferred_element_type=F32)
        dgpre_ref[...] += jnp.sum(dh2 * x1hat, axis=0, keepdims=True)
        dx1_ref[...] = dx2 + _rms_bwd(x1hat, r1, gpre_ref[...], dh2)

    acc = pl.BlockSpec((1, D_MODEL), lambda i: (0, 0))
    return pl.pallas_call(
        body, name="mlp_fwd_bwd", grid=(S // tm,),
        in_specs=[_rows(tm, D_MODEL), _rows(tm, D_MODEL), _resident((1, D_MODEL)), _resident((1, D_MODEL)),
                  _resident((n_ff, D_MODEL, SHARD_FF)), _resident((D_FF, D_MODEL))],
        out_specs=[_rows(tm, D_MODEL), _rows(tm, D_FF), _rows(tm, D_FF), _rows(tm, D_MODEL), _rows(tm, D_MODEL),
                   acc, acc, pl.BlockSpec((1, 1), lambda i: (0, 0))],
        out_shape=[jax.ShapeDtypeStruct((S, D_MODEL), BF16), jax.ShapeDtypeStruct((S, D_FF), BF16),
                   jax.ShapeDtypeStruct((S, D_FF), BF16), jax.ShapeDtypeStruct((S, D_MODEL), BF16),
                   jax.ShapeDtypeStruct((S, D_MODEL), F32), jax.ShapeDtypeStruct((1, D_MODEL), F32),
                   jax.ShapeDtypeStruct((1, D_MODEL), F32), jax.ShapeDtypeStruct((1, 1), F32)],
        scratch_shapes=[pltpu.VMEM((tm, D_FF), F32)],
        compiler_params=_params("arbitrary"),
    )(x1, target, g_pre, g_post, w_up, w_down)


def _weight_grad(name, a, b, rows_sharded):
    S, K = a.shape
    N = b.shape[1]
    if rows_sharded:
        tk, tn = K // N_CHIPS, N
        a_spec = pl.BlockSpec((S, tk), lambda j: (0, j))
        b_spec = pl.BlockSpec((S, tn), lambda j: (0, 0), pipeline_mode=pl.Buffered(1))
    else:
        tk, tn = K, N // N_CHIPS
        a_spec = pl.BlockSpec((S, tk), lambda j: (0, 0), pipeline_mode=pl.Buffered(1))
        b_spec = pl.BlockSpec((S, tn), lambda j: (0, j))
    half = tk // 2

    def body(a_ref, b_ref, o_ref):
        res = lax.dot_general(a_ref[...], b_ref[...], TN, preferred_element_type=F32)
        o_ref[0, 0] = res[:half]
        o_ref[1, 0] = res[half:]

    return pl.pallas_call(
        body, name=name, grid=(N_CHIPS,), in_specs=[a_spec, b_spec],
        out_specs=pl.BlockSpec((2, 1, half, tn), lambda j: (0, j, 0, 0)),
        out_shape=jax.ShapeDtypeStruct((2, N_CHIPS, half, tn), F32),
        compiler_params=_params("parallel"),
    )(a, b)


def _mix_bwd(dx1, y2, ya, yx, bcu, conv_w, g_a, g_c, g_x, w_out, g_post, tm):
    S = dx1.shape[0]

    def body(dx1_ref, y2_ref, ya_ref, yx_ref, bcu_ref, before_ref, cw_ref, ga_ref, gc_ref, gx_ref, wo_ref, gp_ref,
             dy2_ref, dya_ref, delta_ref, dycx_ref, dgp_ref, dga_ref, dgc_ref, dgx_ref):
        @pl.when(pl.program_id(0) == 0)
        def _():
            for ref in (dgp_ref, dga_ref, dgc_ref, dgx_ref):
                ref[...] = jnp.zeros_like(ref)

        dx1 = dx1_ref[...]
        y2hat, r2 = _rms_hat(y2_ref[...])
        dgp_ref[...] += jnp.sum(dx1 * y2hat, axis=0, keepdims=True)
        dy2 = _rms_bwd(y2hat, r2, gp_ref[...], dx1).astype(BF16)
        dy2_ref[...] = dy2
        dycat = lax.dot_general(dy2, wo_ref[...], NT, preferred_element_type=F32)

        d_na = dycat[:, 0:ATTN_W]
        ya = ya_ref[...]
        yahat, ra = _rms_hat(ya)
        dga_ref[...] += jnp.sum(d_na * yahat, axis=0, keepdims=True)
        dya = _rms_bwd(yahat, ra, ga_ref[...], d_na)
        dya_ref[...] = dya
        prod = dya * ya
        hi = prod.astype(BF16)
        lo = (prod - hi.astype(F32)).astype(BF16)
        head_of = lambda axis: lax.shift_right_logical(lax.broadcasted_iota(jnp.int32, (ATTN_W, ATTN_W), axis),
                                                       HEAD.bit_length() - 1)
        same_head = head_of(0) == head_of(1)
        ones = jnp.where(same_head, 1.0, 0.0).astype(BF16)
        delta_ref[...] = (jnp.dot(hi, ones, preferred_element_type=F32) + jnp.dot(lo, ones, preferred_element_type=F32))

        b, _, _, _, _, _, cv = _conv_fwd(bcu_ref[...], before_ref[...], pl.program_id(0) == 0, cw_ref[...])
        d_nc = dycat[:, ATTN_W:ATTN_W + CONV_W]
        ychat, rc = _rms_hat(b * cv)
        dgc_ref[...] += jnp.sum(d_nc * ychat, axis=0, keepdims=True)
        dycx_ref[:, 0:CONV_W] = _rms_bwd(ychat, rc, gc_ref[...], d_nc)

        d_nx = dycat[:, ATTN_W + CONV_W:D_MODEL]
        yxhat, rx = _rms_hat(yx_ref[...])
        dgx_ref[...] += jnp.sum(d_nx * yxhat, axis=0, keepdims=True)
        dycx_ref[:, CONV_W:CONV_W + XATTN_W] = _rms_bwd(yxhat, rx, gx_ref[...], d_nx)

    acc = lambda w: pl.BlockSpec((1, w), lambda i: (0, 0))
    return pl.pallas_call(
        body, name="mix_bwd", grid=(S // tm,),
        in_specs=[_rows(tm, D_MODEL), _rows(tm, D_MODEL), _rows(tm, ATTN_W), _rows(tm, XATTN_W),
                  _rows(tm, 3 * CONV_W), _halo_before(tm, 3 * CONV_W), _resident((3, CONV_W)),
                  _resident((1, ATTN_W)), _resident((1, CONV_W)), _resident((1, XATTN_W)),
                  _resident((D_MODEL, D_MODEL)), _resident((1, D_MODEL))],
        out_specs=[_rows(tm, D_MODEL), _rows(tm, ATTN_W), _rows(tm, ATTN_W), _rows(tm, CONV_W + XATTN_W),
                   acc(D_MODEL), acc(ATTN_W), acc(CONV_W), acc(XATTN_W)],
        out_shape=[jax.ShapeDtypeStruct((S, D_MODEL), BF16), jax.ShapeDtypeStruct((S, ATTN_W), F32),
                   jax.ShapeDtypeStruct((S, ATTN_W), F32),
                   jax.ShapeDtypeStruct((S, CONV_W + XATTN_W), F32), jax.ShapeDtypeStruct((1, D_MODEL), F32),
                   jax.ShapeDtypeStruct((1, ATTN_W), F32), jax.ShapeDtypeStruct((1, CONV_W), F32),
                   jax.ShapeDtypeStruct((1, XATTN_W), F32)],
        compiler_params=_params("arbitrary"),
    )(dx1, y2, ya, yx, bcu, bcu, conv_w, g_a, g_c, g_x, w_out, g_post)


def _conv_xattn_bwd(dycx, bcu, qx, mkv, conv_w, tm):
    S = dycx.shape[0]
    n_mem = mkv.shape[0]
    n_tiles = S // tm

    def body(d_ref, dafter_ref, bcu_ref, before_ref, after_ref, qx_ref, mkv_ref, cw_ref,
             tail_ref, dmkv_ref, dcw_ref):
        i = pl.program_id(0)

        @pl.when(i == 0)
        def _():
            dmkv_ref[...] = jnp.zeros_like(dmkv_ref)
            dcw_ref[...] = jnp.zeros_like(dcw_ref)

        w = cw_ref[...]
        b, c, u, z, z1, z2, cv = _conv_fwd(bcu_ref[...], before_ref[...], i == 0, w)
        dyc = d_ref[:, 0:CONV_W]
        dcv = dyc * b
        dcv_after = jnp.where(i == n_tiles - 1, 0.0, dafter_ref[:, 0:CONV_W] * after_ref[:, 0:CONV_W])
        dz = w[2:3, :] * dcv + w[1:2, :] * _shift_up(dcv, dcv_after, 1) + w[0:1, :] * _shift_up(dcv, dcv_after, 2)
        dcw_ref[0:1, :] += jnp.sum(dcv * z2, axis=0, keepdims=True)
        dcw_ref[1:2, :] += jnp.sum(dcv * z1, axis=0, keepdims=True)
        dcw_ref[2:3, :] += jnp.sum(dcv * z, axis=0, keepdims=True)
        tail_ref[:, 0:CONV_W] = (dyc * cv).astype(BF16)
        tail_ref[:, CONV_W:2 * CONV_W] = (dz * u).astype(BF16)
        tail_ref[:, 2 * CONV_W:3 * CONV_W] = (dz * c).astype(BF16)

        qxb, mkvb = qx_ref[...], mkv_ref[...]
        for hd in range(XATTN_W // HEAD):
            sl = slice(HEAD * hd, HEAD * (hd + 1))
            vsl = slice(XATTN_W + HEAD * hd, XATTN_W + HEAD * (hd + 1))
            s = lax.dot_general(qxb[:, sl], mkvb[:, sl], NT, preferred_element_type=F32) * SCALE
            e = jnp.exp(s - jnp.max(s, axis=1, keepdims=True))
            p = e / jnp.sum(e, axis=1, keepdims=True)
            dob = d_ref[:, CONV_W + HEAD * hd:CONV_W + HEAD * (hd + 1)].astype(BF16)
            dp = lax.dot_general(dob, mkvb[:, vsl], NT, preferred_element_type=F32)
            ds = (p * (dp - jnp.sum(p * dp, axis=1, keepdims=True)) * SCALE).astype(BF16)
            tail_ref[:, 3 * CONV_W + HEAD * hd:3 * CONV_W + HEAD * (hd + 1)] = jnp.dot(
                ds, mkvb[:, sl], preferred_element_type=F32).astype(BF16)
            dmkv_ref[:, sl] += lax.dot_general(ds, qxb[:, sl], TN, preferred_element_type=F32)
            dmkv_ref[:, vsl] += lax.dot_general(p.astype(BF16), dob, TN, preferred_element_type=F32)

    width = CONV_W + XATTN_W
    return pl.pallas_call(
        body, name="conv_xattn_bwd", grid=(n_tiles,),
        in_specs=[_rows(tm, width), _halo_after(tm, width, S), _rows(tm, 3 * CONV_W), _halo_before(tm, 3 * CONV_W),
                  _halo_after(tm, 3 * CONV_W, S), _rows(tm, XATTN_W), _resident((n_mem, 2 * XATTN_W)),
                  _resident((3, CONV_W))],
        out_specs=[_rows(tm, 3 * CONV_W + XATTN_W), pl.BlockSpec((n_mem, 2 * XATTN_W), lambda i: (0, 0)),
                   pl.BlockSpec((3, CONV_W), lambda i: (0, 0))],
        out_shape=[jax.ShapeDtypeStruct((S, 3 * CONV_W + XATTN_W), BF16),
                   jax.ShapeDtypeStruct((n_mem, 2 * XATTN_W), F32), jax.ShapeDtypeStruct((3, CONV_W), F32)],
        compiler_params=_params("arbitrary"),
    )(dycx, dycx, bcu, bcu, bcu, qx, mkv, conv_w)


def _memkv_bwd(mem, g_mem, w_kv, dmkv):
    n_mem = mem.shape[0]
    half = D_MODEL // N_CHIPS // 2

    def body(mem_ref, g_ref, w_ref, d_ref, dw_ref, dg_ref):
        mhat, _ = _rms_hat(mem_ref[...])
        mn = (mhat * g_ref[...]).astype(BF16)
        d = d_ref[...].astype(BF16)
        for k in range(2 * N_CHIPS):
            dw_ref[k % 2, k // 2] = lax.dot_general(mn[:, half * k:half * (k + 1)], d, TN, preferred_element_type=F32)
        dmn = lax.dot_general(d, w_ref[...], NT, preferred_element_type=F32)
        dg_ref[...] = jnp.sum(dmn * mhat, axis=0, keepdims=True)

    return pl.pallas_call(
        body, name="memkv_bwd",
        out_shape=[jax.ShapeDtypeStruct((2, N_CHIPS, half, 2 * XATTN_W), F32), jax.ShapeDtypeStruct((1, D_MODEL), F32)],
        compiler_params=pltpu.CompilerParams(vmem_limit_bytes=VMEM_LIMIT_V7X),
    )(mem, g_mem, w_kv, dmkv)


def _in_proj_bwd(dqkv, tail, cos, sin, w_in, x, g, dx1, tm):
    S = x.shape[0]

    def body(dq_ref, dk_ref, dv_ref, tail_ref, cos_ref, sin_ref, w_ref, x_ref, g_ref, dx1_ref, dproj_ref, dx_ref, dg_ref):
        @pl.when(pl.program_id(0) == 0)
        def _():
            dg_ref[...] = jnp.zeros_like(dg_ref)

        c, s = cos_ref[...], sin_ref[...]
        for j in range(ATTN_W // 128):
            cols = slice(128 * j, 128 * (j + 1))
            dproj_ref[:, cols] = _rope128(dq_ref[:, cols] * SCALE, c, s, True).astype(BF16)
            dproj_ref[:, ATTN_W + 128 * j:ATTN_W + 128 * (j + 1)] = _rope128(dk_ref[:, cols], c, s, True).astype(BF16)
        dproj_ref[:, 2 * ATTN_W:3 * ATTN_W] = dv_ref[...].astype(BF16)
        dproj_ref[:, 3 * ATTN_W:PROJ_W] = tail_ref[...]
        dh = jnp.zeros((tm, D_MODEL), F32)
        for j in range(N_CHIPS):
            dh = dh + lax.dot_general(dproj_ref[:, SHARD_IN * j:SHARD_IN * (j + 1)], w_ref[j], NT,
                                      preferred_element_type=F32)
        xhat, r = _rms_hat(x_ref[...])
        dg_ref[...] += jnp.sum(dh * xhat, axis=0, keepdims=True)
        dx_ref[...] = dx1_ref[...] + _rms_bwd(xhat, r, g_ref[...], dh)

    return pl.pallas_call(
        body, name="in_proj_bwd", grid=(S // tm,),
        in_specs=[_rows(tm, ATTN_W)] * 3 + [_rows(tm, PROJ_W - 3 * ATTN_W), _rows(tm, 128), _rows(tm, 128),
                  _resident((N_CHIPS, D_MODEL, SHARD_IN)), _rows(tm, D_MODEL), _resident((1, D_MODEL)),
                  _rows(tm, D_MODEL)],
        out_specs=[_rows(tm, PROJ_W), _rows(tm, D_MODEL), pl.BlockSpec((1, D_MODEL), lambda i: (0, 0))],
        out_shape=[jax.ShapeDtypeStruct((S, PROJ_W), BF16), jax.ShapeDtypeStruct((S, D_MODEL), F32),
                   jax.ShapeDtypeStruct((1, D_MODEL), F32)],
        compiler_params=_params("arbitrary"),
    )(*dqkv, tail, cos, sin, w_in, x, g, dx1)


def _row_tile(rows):
    return ROW_TILE if rows % ROW_TILE == 0 else rows


def _chip_sums_bf16(name, grads, from_sibling, place):
    k = len(grads)
    _, n, rows, _ = grads[0].shape
    tr = _row_tile(rows)

    def body(place_ref, *refs):
        for g_ref, b_ref, o_ref in zip(refs[:k], refs[k:2 * k], refs[2 * k:]):
            o_ref[...] = (g_ref[0] + b_ref[...]).astype(BF16)

    mine = lambda g: pl.BlockSpec((1, 1, tr, g.shape[3]), lambda s, i, p: (p[0], s, i, 0))
    slab = lambda g: pl.BlockSpec((1, tr, g.shape[3]), lambda s, i, p: (s, i, 0))
    return pl.pallas_call(
        body, name=name, out_shape=[jax.ShapeDtypeStruct(g.shape[1:], BF16) for g in grads],
        grid_spec=pltpu.PrefetchScalarGridSpec(
            num_scalar_prefetch=1, grid=(n, rows // tr),
            in_specs=[mine(g) for g in grads] + [slab(g) for g in grads], out_specs=[slab(g) for g in grads]),
        compiler_params=_params("parallel", "parallel"),
    )(place, *grads, *from_sibling)


def _final_sums(name, grads, from_sibling, others, place):
    k = len(grads)
    rows = grads[0].shape[2]
    tr = _row_tile(rows)

    def body(place_ref, *refs):
        for a in range(k):
            own_ref, sib_ref = refs[a], refs[k + a]
            acc = own_ref[0, 0] + sib_ref[0]
            for o in refs[2 * k + 3 * a:2 * k + 3 * a + 3]:
                acc = acc + o[0].astype(F32)
            refs[5 * k + a][...] = acc

    own = lambda g: pl.BlockSpec((1, 1, tr, g.shape[3]), lambda i, p: (p[0], p[1], i, 0))
    sib = lambda g: pl.BlockSpec((1, tr, g.shape[3]), lambda i, p: (p[1], i, 0))
    other = lambda g, j: pl.BlockSpec((1, tr, g.shape[3]), lambda i, p: (j, i, 0))
    return pl.pallas_call(
        body, name=name, out_shape=[jax.ShapeDtypeStruct(g.shape[2:], F32) for g in grads],
        grid_spec=pltpu.PrefetchScalarGridSpec(
            num_scalar_prefetch=1, grid=(rows // tr,),
            in_specs=[own(g) for g in grads] + [sib(g) for g in grads] + [other(g, j) for g in grads for j in range(3)],
            out_specs=[pl.BlockSpec((tr, g.shape[3]), lambda i, p: (i, 0)) for g in grads]),
        compiler_params=_params("parallel"),
    )(place, *grads, *from_sibling, *[o for o in others for _ in range(3)])


def _adamw_update(w, g, m, v):
    m = ADAM_B1 * m + (1.0 - ADAM_B1) * g
    v = ADAM_B2 * v + (1.0 - ADAM_B2) * (g * g)
    m_hat = m * (1.0 / (1.0 - ADAM_B1 ** ADAM_STEP))
    v_hat = v * (1.0 / (1.0 - ADAM_B2 ** ADAM_STEP))
    return -ADAM_LR * (m_hat / (jnp.sqrt(v_hat) + ADAM_EPS) + ADAM_WD * w), m, v


def _adamw(name, params, after):
    k = len(params)
    rows = params[0][0].shape[0]
    tr = ADAMW_ROW_TILE if rows % ADAMW_ROW_TILE == 0 else rows

    def body(*refs):
        ins, outs = refs[:4 * k], refs[4 * k + 1:]
        for a in range(k):
            w_ref, g_ref, m_ref, v_ref = ins[4 * a:4 * a + 4]
            g = g_ref[...]
            outs[4 * a][...] = g
            outs[4 * a + 1][...], outs[4 * a + 2][...], outs[4 * a + 3][...] = _adamw_update(w_ref[...], g, m_ref[...], v_ref[...])

    spec = lambda w: pl.BlockSpec((tr, w.shape[1]), lambda i: (i, 0))
    out = pl.pallas_call(
        body, name=name, grid=(rows // tr,),
        in_specs=[spec(p[0]) for p in params for _ in range(4)] + [pl.BlockSpec(memory_space=pl.ANY)],
        out_specs=[spec(p[0]) for p in params for _ in range(4)],
        out_shape=[jax.ShapeDtypeStruct(p[0].shape, F32) for p in params for _ in range(4)],
        compiler_params=_params("parallel"),
    )(*[t for p in params for t in p], after)
    return [out[4 * a:4 * a + 4] for a in range(k)]


def _small_update(summed, chip, gains, gains_m, gains_v, taps, taps_m, taps_v):
    n = len(gains)
    widths = [g.shape[1] for g in gains]
    k, w = taps.shape

    def body(*refs):
        chip_ref, sum_ref = refs[0], refs[1]
        params = [refs[2 + 3 * i:5 + 3 * i] for i in range(n + 1)]
        outs = [refs[2 + 3 * (n + 1) + 4 * i:2 + 3 * (n + 1) + 4 * (i + 1)] for i in range(n + 1)]
        loss_ref = refs[-1]
        for i in range(n):
            g = sum_ref[i:i + 1, 0:widths[i]]
            wr, mr, vr = params[i]
            outs[i][0][...] = g
            outs[i][1][...], outs[i][2][...], outs[i][3][...] = _adamw_update(wr[...], g, mr[...], vr[...])
        g = sum_ref[n:n + k, 0:w]
        for j in range(1, N_CHIPS):
            g = jnp.where(chip_ref[0] == j, sum_ref[n:n + k, w * j:w * (j + 1)], g)
        wr, mr, vr = params[n]
        outs[n][0][...] = g
        outs[n][1][...], outs[n][2][...], outs[n][3][...] = _adamw_update(wr[...], g, mr[...], vr[...])
        loss_ref[...] = sum_ref[n + k:n + k + 1, 0:1]

    vmem = pl.BlockSpec(memory_space=pltpu.VMEM)
    operands = [chip, summed]
    for p in zip(list(gains) + [taps], list(gains_m) + [taps_m], list(gains_v) + [taps_v]):
        operands += list(p)
    shapes = [jax.ShapeDtypeStruct(p.shape, F32) for p in list(gains) + [taps] for _ in range(4)]
    out = pl.pallas_call(
        body, name="small_update", out_shape=shapes + [jax.ShapeDtypeStruct((1, 1), F32)],
        in_specs=[pl.BlockSpec(memory_space=pltpu.SMEM)] + [vmem] * (len(operands) - 1),
        out_specs=[vmem] * (len(shapes) + 1),
    )(*operands)
    return [out[4 * i:4 * (i + 1)] for i in range(n + 1)], out[-1]


def _sum_blocks(name, blocks):
    n, rows, cols = blocks.shape

    def body(b_ref, o_ref):
        acc = b_ref[0]
        for k in range(1, n):
            acc = acc + b_ref[k]
        o_ref[...] = acc

    return pl.pallas_call(body, name=name, out_shape=jax.ShapeDtypeStruct((rows, cols), F32))(blocks)


def _place():
    return lax.axis_index("x"), lax.axis_index("y"), lax.axis_index("c")


def _other_chips(x, y):
    return [(1 - x, y), (x, 1 - y), (1 - x, 1 - y)]


def _allgather_finish(name, shards, landed, pass_on):
    n = len(shards)

    def body(*refs):
        ins, outs, stage = refs[:n], refs[2 * n:3 * n], refs[3 * n:4 * n]
        send_sems, recv_sems, local_sems = refs[4 * n:]
        x, y, c = _place()
        chips = _other_chips(x, y)

        def copy(a, k, chip, half):
            place = outs[a].at[2 * chip[0] + chip[1], half]
            return pltpu.make_async_remote_copy(
                src_ref=place, dst_ref=place, send_sem=send_sems.at[3 * a + k], recv_sem=recv_sems.at[3 * a + k],
                device_id=(x, y, 1 - c), device_id_type=MESH)

        load = [pltpu.make_async_copy(ins[a], stage[a], local_sems.at[a]) for a in range(n)]
        local = [pltpu.make_async_copy(stage[a], outs[a].at[2 * x + y], local_sems.at[a]) for a in range(n)]
        for cp in load:
            cp.start()
        passed = [copy(a, k, chip, c) for a in range(n) if pass_on[a] for k, chip in enumerate(chips)]
        for cp in passed:
            cp.start()
        for a in range(n):
            load[a].wait()
            local[a].start()
        for a in range(n):
            if pass_on[a]:
                for k, chip in enumerate(chips):
                    copy(a, k, chip, 1 - c).wait_recv()
        for cp in passed:
            cp.wait_send()
        for cp in local:
            cp.wait()

    any_spec = pl.BlockSpec(memory_space=pl.ANY)
    return pl.pallas_call(
        body, name=name,
        out_shape=[jax.ShapeDtypeStruct((N_CHIPS,) + s.shape, s.dtype) for s in shards],
        in_specs=[any_spec] * (2 * n), out_specs=[any_spec] * n,
        input_output_aliases={n + a: a for a in range(n)},
        scratch_shapes=[pltpu.VMEM(s.shape, s.dtype) for s in shards]
        + [pltpu.SemaphoreType.DMA((3 * n,)), pltpu.SemaphoreType.DMA((3 * n,)), pltpu.SemaphoreType.DMA((n,))],
        compiler_params=pltpu.CompilerParams(vmem_limit_bytes=VMEM_LIMIT_V7X),
    )(*shards, *landed)


def _plan_first_hop(x, y, c, shards, lands):
    return [(shards[a].at[c], lands[a].at[2 * x + y, c], lands[a].at[2 * chip[0] + chip[1], c], (*chip, c))
            for a in range(len(shards)) for chip in _other_chips(x, y)]


def _plan_pass_on(x, y, c, nothing, lands):
    def place(a, chip, half):
        return lands[a].at[2 * chip[0] + chip[1], half]

    return [(place(a, chip, c), place(a, chip, c), place(a, chip, 1 - c), (x, y, 1 - c))
            for a in range(len(lands)) for chip in _other_chips(x, y)]


def _plan_other_half_to_sibling(x, y, c, grads, lands):
    return [(grads[a].at[1 - c], lands[a], lands[a], (x, y, 1 - c)) for a in range(len(grads))]


def _plan_to_other_chips(x, y, c, partials, lands):
    return [(partials[a].at[2 * chip[0] + chip[1]], lands[a].at[k], lands[a].at[k], (*chip, c))
            for a in range(len(partials)) for k, chip in enumerate(_other_chips(x, y))]


def _plan_to_all(x, y, c, blocks, lands):
    flips = [(fx, fy, fc) for fx in (0, 1) for fy in (0, 1) for fc in (0, 1) if (fx, fy, fc) != (0, 0, 0)]
    peers = [(1 - x if fx else x, 1 - y if fy else y, 1 - c if fc else c) for fx, fy, fc in flips]
    return [(blocks[0], lands[0].at[4 * x + 2 * y + c], lands[0].at[4 * p[0] + 2 * p[1] + p[2]], p) for p in peers]


def _planned_copies(plan, srcs, lands, send_sems, recv_sems):
    x, y, c = _place()

    def pair(k, src, there, here, to):
        make = lambda dst: pltpu.make_async_remote_copy(
            src_ref=src, dst_ref=dst, send_sem=send_sems.at[k], recv_sem=recv_sems.at[k], device_id=to, device_id_type=MESH)
        return make(there), make(here)

    return [pair(k, *entry) for k, entry in enumerate(plan(x, y, c, srcs, lands))]


_HBM_SPEC = pl.BlockSpec(memory_space=pltpu.HBM)
_SEM_SPEC = pl.BlockSpec(memory_space=pltpu.SEMAPHORE)


def _hbm(a):
    return pltpu.with_memory_space_constraint(a, pltpu.HBM)


def _exchange_start(name, plan, n_copies, srcs, land_shapes, after, lands=None):
    if lands is None:
        lands = [lax.empty(s.shape, s.dtype) for s in land_shapes]
    land_shapes = lands
    ns, nl = len(srcs), len(land_shapes)
    n_in = ns + nl + 1

    def body(*refs):
        for send, _ in _planned_copies(plan, refs[:ns], refs[ns:ns + nl], refs[n_in], refs[n_in + 1]):
            send.start()
        refs[-1][...] = jnp.zeros_like(refs[-1])

    out = pl.pallas_call(
        body, name=name,
        out_shape=(pltpu.SemaphoreType.DMA((n_copies,)), pltpu.SemaphoreType.DMA((n_copies,)),
                   *[pltpu.HBM(s.shape, s.dtype) for s in land_shapes], jax.ShapeDtypeStruct((8, 128), F32)),
        in_specs=[_HBM_SPEC] * (ns + nl) + [pl.BlockSpec(memory_space=pl.ANY)],
        out_specs=(_SEM_SPEC, _SEM_SPEC, *[_HBM_SPEC] * nl, pl.BlockSpec(memory_space=pltpu.VMEM)),
        input_output_aliases={ns + i: 2 + i for i in range(nl)},
        compiler_params=pltpu.CompilerParams(has_side_effects=pltpu.SideEffectType.DATAFLOW_SIDE_EFFECTING),
    )(*[_hbm(s) for s in srcs], *[_hbm(l) for l in lands], after)
    return out[0], out[1], list(out[2:2 + nl]), out[-1]


def _exchange_wait(name, plan, srcs, started, after):
    send_sems, recv_sems, lands, _ = started
    ns, nl = len(srcs), len(lands)
    after = list(after) if isinstance(after, (list, tuple)) else [after]

    def body(*refs):
        for send, recv in _planned_copies(plan, refs[:ns], refs[ns:ns + nl], refs[ns + nl], refs[ns + nl + 1]):
            send.wait_send()
            recv.wait_recv()

    return pl.pallas_call(
        body, name=name, out_shape=[pltpu.HBM(l.shape, l.dtype) for l in lands],
        in_specs=[_HBM_SPEC] * (ns + nl) + [_SEM_SPEC, _SEM_SPEC] + [pl.BlockSpec(memory_space=pl.ANY)] * len(after),
        out_specs=[_HBM_SPEC] * nl, input_output_aliases={ns + i: i for i in range(nl)},
        compiler_params=pltpu.CompilerParams(has_side_effects=pltpu.SideEffectType.DATAFLOW_SIDE_EFFECTING),
    )(*[_hbm(s) for s in srcs], *lands, send_sems, recv_sems, *after)


def _exchange_halves(name, halves, after):
    n = len(halves)

    def body(*refs):
        ins, outs, stage = refs[:n], refs[n + 1:2 * n + 1], refs[2 * n + 1:3 * n + 1]
        send_sems, recv_sems, local_sems = refs[3 * n + 1:]
        x, y, c = _place()
        load = [pltpu.make_async_copy(ins[a], stage[a], local_sems.at[a]) for a in range(n)]
        local = [pltpu.make_async_copy(stage[a], outs[a].at[c], local_sems.at[a]) for a in range(n)]
        remote = [pltpu.make_async_remote_copy(
            src_ref=stage[a], dst_ref=outs[a].at[c], send_sem=send_sems.at[a], recv_sem=recv_sems.at[a],
            device_id=(x, y, 1 - c), device_id_type=MESH) for a in range(n)]
        for cp in load:
            cp.start()
        for a in range(n):
            load[a].wait()
            remote[a].start()
            local[a].start()
        for a in range(n):
            pltpu.make_async_remote_copy(
                src_ref=ins[a], dst_ref=outs[a].at[1 - c], send_sem=send_sems.at[a], recv_sem=recv_sems.at[a],
                device_id=(x, y, 1 - c), device_id_type=MESH).wait_recv()
        for cp in remote:
            cp.wait_send()
        for cp in local:
            cp.wait()

    any_spec = pl.BlockSpec(memory_space=pl.ANY)
    return pl.pallas_call(
        body, name=name,
        out_shape=[jax.ShapeDtypeStruct((2,) + h.shape, h.dtype) for h in halves],
        in_specs=[any_spec] * (n + 1), out_specs=[any_spec] * n,
        scratch_shapes=[pltpu.VMEM(h.shape, h.dtype) for h in halves]
        + [pltpu.SemaphoreType.DMA((n,)), pltpu.SemaphoreType.DMA((n,)), pltpu.SemaphoreType.DMA((n,))],
        compiler_params=pltpu.CompilerParams(vmem_limit_bytes=VMEM_LIMIT_V7X),
    )(*halves, after)


def _like(arrays, lead, dtype=None):
    return [jax.ShapeDtypeStruct(tuple(lead) + a.shape[-2:], dtype or a.dtype) for a in arrays]


class _StepExchanges:
    def __init__(self, mats, conv_w):
        x, y, c = _place()
        self.place = jnp.stack([c, 2 * x + y]).astype(jnp.int32)
        shards = [w.astype(BF16).reshape(2, w.shape[0] // 2, w.shape[1]) for w in mats]
        self._in_shard = shards[:1]
        self._in = _exchange_start("w_in_allgather_start", _plan_first_hop, 3, self._in_shard,
                                   _like(self._in_shard, (N_CHIPS, 2)), shards[0])
        self.zero = self._in[3]
        taps = jnp.pad(conv_w, ((0, 8 - conv_w.shape[0]), (0, 128 - conv_w.shape[1])))
        self._rest_shards = shards[1:] + [jnp.stack([taps, jnp.zeros_like(taps)])]
        self._taps_shape = conv_w.shape
        self._groups = {}

    def w_in(self, after):
        landed = _exchange_wait("w_in_allgather_wait", _plan_first_hop, self._in_shard, self._in,
                                list(after) + self._rest_shards)
        (w_in,) = _allgather_finish("w_in_allgather_finish", self._in_shard, landed, [True])
        self._rest = _exchange_start("rest_allgather_start", _plan_first_hop, 3 * len(self._rest_shards),
                                     self._rest_shards, _like(self._rest_shards, (N_CHIPS, 2)), w_in)
        self.zero = self._rest[3]
        return w_in.reshape(N_CHIPS, 2 * w_in.shape[2], w_in.shape[3])

    def rest_weights(self, after):
        landed = _exchange_wait("rest_allgather_wait", _plan_first_hop, self._rest_shards, self._rest, after)
        kv, out, up, down, taps = _allgather_finish("rest_allgather_finish", self._rest_shards, landed,
                                                    [True, True, False, False, True])
        self._up_down = _exchange_start("up_down_pass_on_start", _plan_pass_on, 6, [], None, self.zero, lands=[up, down])
        self.zero = self._up_down[3]
        k, w = self._taps_shape
        taps = taps[:, 0, :k, :w].transpose(1, 0, 2).reshape(k, N_CHIPS * w)
        return [g.reshape(N_CHIPS, 2 * g.shape[2], g.shape[3]) for g in (kv, out)], taps

    def up_down(self, after):
        full = _exchange_wait("up_down_pass_on_wait", _plan_pass_on, [], self._up_down, after)
        return [g.reshape(N_CHIPS, 2 * g.shape[2], g.shape[3]) for g in full]

    def send_grads(self, key, grads):
        grads = list(grads)
        started = _exchange_start(f"{key}_grads_to_sibling_start", _plan_other_half_to_sibling, len(grads), grads,
                                  _like(grads, (N_CHIPS,)), self.zero)
        self._groups[key] = dict(grads=grads, to_sibling=started)
        self.zero = started[3]

    def grads_at_sibling(self, key, after):
        group = self._groups[key]
        grads = group["grads"]
        group["from_sibling"] = _exchange_wait(f"{key}_grads_to_sibling_wait", _plan_other_half_to_sibling, grads,
                                               group["to_sibling"], after)
        group["partials"] = _chip_sums_bf16(f"{key}_chip_sums", grads, group["from_sibling"], self.place)
        group["to_chips"] = _exchange_start(f"{key}_grads_to_chips_start", _plan_to_other_chips, 3 * len(grads),
                                            group["partials"], _like(group["partials"], (3,)), self.zero)
        self.zero = group["to_chips"][3]

    def grads_summed(self, key, after):
        group = self._groups[key]
        from_chips = _exchange_wait(f"{key}_grads_to_chips_wait", _plan_to_other_chips, group["partials"],
                                    group["to_chips"], after)
        return _final_sums(f"{key}_final_sums", group["grads"], group["from_sibling"], from_chips, self.place)

    def send_small(self, block):
        self._small = block
        self._small_started = _exchange_start("small_grads_start", _plan_to_all, 7, [block],
                                              [jax.ShapeDtypeStruct((8,) + block.shape, block.dtype)], self.zero)
        self.zero = self._small_started[3]

    def small_summed(self, after):
        x, y, c = _place()
        (landed,) = _exchange_wait("small_grads_wait", _plan_to_all, [self._small], self._small_started, after)
        blocks = lax.dynamic_update_index_in_dim(landed, self._small, 4 * x + 2 * y + c, 0)
        return _sum_blocks("small_sum", blocks)


def _rope_tables(positions):
    half = HEAD // 2
    inv_freq = jnp.float32(ROPE_THETA) ** (-(jnp.arange(half, dtype=F32) * 2.0 / HEAD))
    ang = positions.astype(F32)[:, None] * inv_freq
    cos, sin = jnp.cos(ang), jnp.sin(ang)
    return jnp.tile(cos, (1, 4)), jnp.tile(jnp.concatenate([-sin, sin], axis=1), (1, 2))


def _local_step(x, mem, positions, target, gains, ex):
    g_pre_mix, g_mem, g_a, g_c, g_x, g_post_mix, g_pre_mlp, g_post_mlp = gains
    tm = ROW_TILE
    cos, sin = _rope_tables(positions)
    h = _pre_norm(x, g_pre_mix + ex.zero[:1, :1], tm)
    w_in = ex.w_in([h, cos, sin])

    q, k, v, bcu, qx = _in_proj_fwd(h, w_in, cos, sin, ex.zero, tm)
    ya, lse = _attn_fwd(q, k, v)
    (w_kv, w_out), conv_w = ex.rest_weights(lse)
    w_kv, w_out = (w.reshape(N_CHIPS * w.shape[1], w.shape[2]) for w in (w_kv, w_out))
    memn, mkv = _memkv_fwd(mem, g_mem + ex.zero[:1, :1], w_kv)
    yx, ycat, y2, x1 = _mix_fwd(ya, bcu, qx, mkv, conv_w, g_a, g_c, g_x, w_out, g_post_mix, x, tm)
    w_up, w_down = ex.up_down(x1)
    w_down = w_down.reshape(N_CHIPS * w_down.shape[1], w_down.shape[2])
    h2, f, du, df2, dx1, dg_pre_mlp, dg_post_mlp, loss = _mlp_fwd_bwd(x1, target, g_pre_mlp, g_post_mlp, w_up, w_down,
                                                                      MLP_ROW_TILE)
    gw_down = _weight_grad("grad_w_down", f, df2, True)
    gw_up = _weight_grad("grad_w_up", h2, du, False)
    ex.send_grads("early", [gw_up, gw_down])

    dy2, dya, delta, dycx, dg_post_mix, dg_a, dg_c, dg_x = _mix_bwd(dx1, y2, ya, yx, bcu, conv_w, g_a, g_c, g_x,
                                                                  w_out, g_post_mix + ex.zero[:1, :1], tm)
    ex.grads_at_sibling("early", dy2)
    gw_out = _weight_grad("grad_w_out", ycat, dy2, True)
    tail, dmkv, g_conv = _conv_xattn_bwd(dycx, bcu, qx, mkv, conv_w + ex.zero[:1, :1], tm)
    gw_kv, dg_mem = _memkv_bwd(mem, g_mem, w_kv, dmkv)
    ex.send_grads("mid", [gw_out, gw_kv])
    dqkv = _attn_bwd(q, k, v, dya, lse, delta, ex.zero)
    ex.grads_at_sibling("mid", dqkv[0])
    dproj, grad_x, dg_pre_mix = _in_proj_bwd(dqkv, tail, cos, sin, w_in, x, g_pre_mix + ex.zero[:1, :1], dx1, tm)
    gain_grads = [dg_pre_mix, dg_mem, dg_a, dg_c, dg_x, dg_post_mix, dg_pre_mlp, dg_post_mlp]
    ex.send_small(_pack_small(gain_grads, g_conv, loss))
    gw_in = _weight_grad("grad_w_in", h, dproj, False)
    ex.send_grads("late", [gw_in])
    return grad_x


def _pack_small(gains, conv, scalar=None):
    rows = [jnp.pad(g, ((0, 0), (0, D_MODEL - g.shape[1]))) for g in gains]
    rows.append(jnp.pad(conv, ((0, 0), (0, D_MODEL - conv.shape[1]))))
    last = jnp.zeros((SMALL_ROWS - 8 - conv.shape[0], D_MODEL), F32)
    rows.append(last if scalar is None else last.at[0:1, 0:1].set(scalar))
    return jnp.concatenate(rows, axis=0)


def _unpack_small(block, gain_widths, conv_width):
    gains = [block[i:i + 1, :w] for i, w in enumerate(gain_widths)]
    return gains, block[8:11, :conv_width], block[11, 0]


def kernel(x, mem, positions, g_pre_mix, g_mem, w_in, w_mem_kv, conv_w, g_attn_out, g_conv_out, g_xattn_out, w_out, g_post_mix, g_pre_mlp, w_up, w_down, g_post_mlp, loss_target, m_g_pre_mix, m_g_mem, m_w_in, m_w_mem_kv, m_conv_w, m_g_attn_out, m_g_conv_out, m_g_xattn_out, m_w_out, m_g_post_mix, m_g_pre_mlp, m_w_up, m_w_down, m_g_post_mlp, v_g_pre_mix, v_g_mem, v_w_in, v_w_mem_kv, v_conv_w, v_g_attn_out, v_g_conv_out, v_g_xattn_out, v_w_out, v_g_post_mix, v_g_pre_mlp, v_w_up, v_w_down, v_g_post_mlp):
    cx, cy, cc = _place()
    chip = 2 * cx + cy
    gains = [g_pre_mix, g_mem, g_attn_out, g_conv_out, g_xattn_out, g_post_mix, g_pre_mlp, g_post_mlp]
    gains_m = [m_g_pre_mix, m_g_mem, m_g_attn_out, m_g_conv_out, m_g_xattn_out, m_g_post_mix, m_g_pre_mlp, m_g_post_mlp]
    gains_v = [v_g_pre_mix, v_g_mem, v_g_attn_out, v_g_conv_out, v_g_xattn_out, v_g_post_mix, v_g_pre_mlp, v_g_post_mlp]
    gain_widths = [g.shape[1] for g in gains]
    mats = [w_in[0], w_mem_kv[0], w_out[0], w_up[0], w_down[0]]
    mats_m = [m_w_in[0], m_w_mem_kv[0], m_w_out[0], m_w_up[0], m_w_down[0]]
    mats_v = [v_w_in[0], v_w_mem_kv[0], v_w_out[0], v_w_up[0], v_w_down[0]]

    ex = _StepExchanges(mats, conv_w[0])
    grad_x = _local_step(x[0], mem[0], positions[0], loss_target[0], gains, ex)

    both = lambda halves: [t.reshape(2 * t.shape[1], t.shape[2]) for t in halves]
    done = ex.grads_summed("early", ex.zero) + ex.grads_summed("mid", ex.zero)
    ex.grads_at_sibling("late", sum(t[:8, :128] for t in done))
    up_sum, down_sum, out_sum, kv_sum = both(_exchange_halves("sums_to_sibling", done, ex.zero))
    params = lambda a, g: (mats[a], g, mats_m[a], mats_v[a])
    new_up, new_down = _adamw("adamw_up_down", [params(3, up_sum), params(4, down_sum)], ex.zero)
    new_out, new_kv = _adamw("adamw_out_kv", [params(2, out_sum), params(1, kv_sum)], ex.zero)

    small, total = _small_update(ex.small_summed(new_kv[1]), chip.reshape(1).astype(jnp.int32), gains, gains_m,
                                 gains_v, conv_w[0], m_conv_w[0], v_conv_w[0])

    (in_half,) = ex.grads_summed("late", small[0][1])
    (in_sum,) = both(_exchange_halves("late_sum_to_sibling", [in_half], in_half))
    (new_in,) = _adamw("adamw_in", [params(0, in_sum)], in_sum)
    mat_new = [new_in, new_kv, new_out, new_up, new_down]

    order = ["g_pre_mix", "g_mem", "w_in", "w_mem_kv", "conv_w", "g_attn_out", "g_conv_out", "g_xattn_out", "w_out",
             "g_post_mix", "g_pre_mlp", "w_up", "w_down", "g_post_mlp"]
    gain_names = ["g_pre_mix", "g_mem", "g_attn_out", "g_conv_out", "g_xattn_out", "g_post_mix", "g_pre_mlp", "g_post_mlp"]
    mat_names = ["w_in", "w_mem_kv", "w_out", "w_up", "w_down"]

    def leaf(kind, name):
        if name in gain_names:
            return small[gain_names.index(name)][kind]
        if name == "conv_w":
            return small[len(gain_names)][kind][None]
        return mat_new[mat_names.index(name)][kind][None]

    return (total[0, 0], grad_x[None], *[leaf(kind, name) for kind in range(4) for name in order])
```

```python
import jax
import jax.numpy as jnp
from jax import lax
from jax.experimental import pallas as pl
from jax.experimental.pallas import tpu as pltpu

F32, BF16 = jnp.float32, jnp.bfloat16

D_MODEL = 1024
ATTN_W = 512
CONV_W = 256
XATTN_W = 256
PROJ_W = 3 * ATTN_W + 3 * CONV_W + XATTN_W
D_FF = 4096
HEAD = 64
N_BACK = 128
DILATIONS = (1, 4, 16)
ROPE_THETA = 10000.0
EPS = 1e-6
NEG_INF = -1e30
SCALE = HEAD ** -0.5
N_CHIPS = 4
SHARD_IN = PROJ_W // N_CHIPS
SHARD_FF = D_FF // N_CHIPS

ADAM_LR, ADAM_B1, ADAM_B2, ADAM_EPS, ADAM_WD, ADAM_STEP = 0.001, 0.9, 0.999, 1e-08, 0.01, 10

VMEM_LIMIT_V7X = 56 * 1024 * 1024
ROW_TILE = 512
MLP_ROW_TILE = 256
ADAMW_ROW_TILE = 256
SMALL_ROWS = 16

NT = (((1,), (1,)), ((), ()))
TN = (((0,), (0,)), ((), ()))
MESH = pl.DeviceIdType.MESH


def _params(*sem):
    return pltpu.CompilerParams(dimension_semantics=sem, vmem_limit_bytes=VMEM_LIMIT_V7X)


def _resident(shape):
    return pl.BlockSpec(shape, lambda *_: (0,) * len(shape), pipeline_mode=pl.Buffered(1))


def _rows(tm, width):
    return pl.BlockSpec((tm, width), lambda i: (i, 0))


def _rms_hat(x):
    r = lax.rsqrt(jnp.mean(x * x, axis=-1, keepdims=True) + EPS)
    return x * r, r


def _rms_bwd(xhat, r, g, dy):
    gdy = dy * g
    return r * (gdy - xhat * jnp.mean(xhat * gdy, axis=-1, keepdims=True))


def _rope128(t, cos, sin_signed, inverse):
    lane = lax.broadcasted_iota(jnp.int32, t.shape, 1)
    first_half = (lane % HEAD) < (HEAD // 2)
    rot = jnp.where(first_half, pltpu.roll(t, 128 - HEAD // 2, 1), pltpu.roll(t, HEAD // 2, 1))
    return t * cos - rot * sin_signed if inverse else t * cos + rot * sin_signed


def _pre_norm(x, g, after, tm):
    S = x.shape[0]

    def body(x_ref, g_ref, after_ref, h_ref):
        h_ref[...] = (_rms_hat(x_ref[...])[0] * g_ref[...]).astype(BF16)

    return pl.pallas_call(
        body, name="pre_norm", grid=(S // tm,),
        in_specs=[_rows(tm, D_MODEL), _resident((1, D_MODEL)), pl.BlockSpec(memory_space=pl.ANY)],
        out_specs=_rows(tm, D_MODEL), out_shape=jax.ShapeDtypeStruct((S, D_MODEL), BF16),
        compiler_params=_params("parallel"),
    )(x, g, after)


def _in_proj_fwd(h, w_in, cos, sin, after, tm):
    S = h.shape[0]

    def body(h_ref, w_ref, cos_ref, sin_ref, after_ref, q_ref, k_ref, v_ref, bcu_ref, qx_ref, proj):
        h = h_ref[...]
        for j in range(N_CHIPS):
            proj[:, SHARD_IN * j:SHARD_IN * (j + 1)] = jnp.dot(h, w_ref[j], preferred_element_type=F32)
        c, s = cos_ref[...], sin_ref[...]
        for j in range(ATTN_W // 128):
            lo = 128 * j
            q_ref[:, lo:lo + 128] = _rope128(proj[:, lo:lo + 128], c, s, False) * SCALE
            k_ref[:, lo:lo + 128] = _rope128(proj[:, ATTN_W + lo:ATTN_W + lo + 128], c, s, False)
        v_ref[...] = proj[:, 2 * ATTN_W:3 * ATTN_W]
        bcu_ref[...] = proj[:, 3 * ATTN_W:3 * ATTN_W + 3 * CONV_W]
        qx_ref[...] = proj[:, 3 * ATTN_W + 3 * CONV_W:PROJ_W].astype(BF16)

    return pl.pallas_call(
        body, name="in_proj_fwd", grid=(S // tm,),
        in_specs=[_rows(tm, D_MODEL), _resident((N_CHIPS, D_MODEL, SHARD_IN)), _rows(tm, 128), _rows(tm, 128),
                  pl.BlockSpec(memory_space=pl.ANY)],
        out_specs=[_rows(tm, ATTN_W), _rows(tm, ATTN_W), _rows(tm, ATTN_W), _rows(tm, 3 * CONV_W), _rows(tm, XATTN_W)],
        out_shape=[jax.ShapeDtypeStruct((S, ATTN_W), F32), jax.ShapeDtypeStruct((S, ATTN_W), F32),
                   jax.ShapeDtypeStruct((S, ATTN_W), F32), jax.ShapeDtypeStruct((S, 3 * CONV_W), F32),
                   jax.ShapeDtypeStruct((S, XATTN_W), BF16)],
        scratch_shapes=[pltpu.VMEM((tm, PROJ_W), F32)],
        compiler_params=_params("parallel"),
    )(h, w_in, cos, sin, after)


def _memkv_fwd(mem, g_mem, w_kv, after):
    n_mem = mem.shape[0]

    def body(mem_ref, g_ref, w_ref, after_ref, mn_ref, kv_ref):
        mhat, _ = _rms_hat(mem_ref[...])
        mn = (mhat * g_ref[...]).astype(BF16)
        mn_ref[...] = mn
        kv_ref[...] = jnp.dot(mn, w_ref[...], preferred_element_type=F32).astype(BF16)

    vmem = pl.BlockSpec(memory_space=pltpu.VMEM)
    return pl.pallas_call(
        body, name="memkv_fwd", in_specs=[vmem, vmem, vmem, pl.BlockSpec(memory_space=pl.ANY)], out_specs=[vmem, vmem],
        out_shape=[jax.ShapeDtypeStruct((n_mem, D_MODEL), BF16), jax.ShapeDtypeStruct((n_mem, 2 * XATTN_W), BF16)],
        compiler_params=pltpu.CompilerParams(vmem_limit_bytes=VMEM_LIMIT_V7X),
    )(mem, g_mem, w_kv, after)


def _fill_band_bias(bias):
    row = lax.broadcasted_iota(jnp.int32, (N_BACK, 2 * N_BACK), 0)
    col = lax.broadcasted_iota(jnp.int32, (N_BACK, 2 * N_BACK), 1)
    band = (col >= row) & (col <= row + N_BACK)
    bias[1] = jnp.where(band, 0.0, NEG_INF)
    bias[0] = jnp.where(band & (col >= N_BACK), 0.0, NEG_INF)


def _strided(start, size, d):
    return pl.ds(start, size) if d == 1 else pl.ds(start, size, stride=d)


def _group_starts(g, G, nb, d):
    t0 = g * G
    r, n0 = lax.shift_right_logical(t0, nb.bit_length() - 1), lax.bitwise_and(t0, nb - 1)
    first = r + n0 * (N_BACK * d)
    before = r + jnp.maximum(n0 - 1, 0) * (N_BACK * d)
    starts = [before] + [first + u * (N_BACK * d) for u in range(G)]
    if d == 1:
        starts = [pl.multiple_of(st, N_BACK) for st in starts]
    return starts, n0


def _step_blocks(i, U, nb, d):
    G = min(U, nb)
    row_blocks, blocks = [], []
    for grp in range(U // G):
        starts, n0 = _group_starts(i * (U // G) + grp, G, nb, d)
        base = len(row_blocks)
        row_blocks += [_strided(st, N_BACK, d) for st in starts]
        for u in range(G):
            blocks.append((base + u, base + u + 1, jnp.minimum(n0, 1) if u == 0 else 1))
    return row_blocks, blocks


def _by_head(a, b):
    lane = lax.broadcasted_iota(jnp.int32, (a.shape[0], 2 * HEAD), 1)
    return jnp.where(lane < HEAD, a, b)


def _head_only(t, hh):
    lane = lax.broadcasted_iota(jnp.int32, t.shape, 1)
    return jnp.where((lane < HEAD) == (hh == 0), t, jnp.zeros_like(t))


def _stack_heads(t):
    return jnp.concatenate([_head_only(t, 0), _head_only(t, 1)], axis=0)


def _head_columns(t):
    return jnp.concatenate([t[:, 0:1], t[:, HEAD:HEAD + 1]], axis=0)


def _unstack(t):
    return _by_head(t[:N_BACK], t[N_BACK:])


def _unstack_columns(t):
    return _by_head(jnp.broadcast_to(t[:N_BACK], (N_BACK, 2 * HEAD)), jnp.broadcast_to(t[N_BACK:], (N_BACK, 2 * HEAD)))


FWD_BLOCKS_PER_STEP = 4
BWD_BLOCKS_PER_STEP = 2


def _attn_fwd(q, k, v):
    S = q.shape[0]
    U = FWD_BLOCKS_PER_STEP

    def body(q_ref, k_ref, v_ref, y_ref, m_ref, l_scr, bias):
        _fill_band_bias(bias)
        for g, d in enumerate(DILATIONS):
            nb = S // d // N_BACK
            first_pattern, last_pattern = g == 0, g == len(DILATIONS) - 1

            def step(i, carry, d=d, nb=nb, first_pattern=first_pattern, last_pattern=last_pattern):
                row_blocks, blocks = _step_blocks(i, U, nb, d)
                kb = [k_ref[r, :].astype(BF16) for r in row_blocks]
                ss = []
                for before, own, which in blocks:
                    kw = jnp.concatenate([kb[before], kb[own]], 0)
                    qs = _stack_heads(q_ref[row_blocks[own], :].astype(BF16))
                    b = bias[which]
                    ss.append(lax.dot_general(qs, kw, NT, preferred_element_type=F32) + jnp.concatenate([b, b], axis=0))
                ms = [jnp.max(s, axis=1, keepdims=True) for s in ss]
                ps = [jnp.exp(s - m) for s, m in zip(ss, ms)]
                ls = [jnp.sum(p, axis=1, keepdims=True) for p in ps]
                vb = [v_ref[r, :].astype(BF16) for r in row_blocks]
                os_ = [jnp.dot(ps[u].astype(BF16), jnp.concatenate([vb[before], vb[own]], 0), preferred_element_type=F32)
                       for u, (before, own, _) in enumerate(blocks)]
                for u, (_, own, _) in enumerate(blocks):
                    o_g, m_g, l_g = _unstack(os_[u]), _unstack_columns(ms[u]), _unstack_columns(ls[u])
                    r = row_blocks[own]
                    if first_pattern:
                        m_new, l_new, acc = m_g, l_g, o_g
                    else:
                        m_old = m_ref[r, :]
                        m_new = jnp.maximum(m_old, m_g)
                        alpha, beta = jnp.exp(m_old - m_new), jnp.exp(m_g - m_new)
                        l_new = l_scr[r, :] * alpha + l_g * beta
                        acc = y_ref[r, :] * alpha + o_g * beta
                    if last_pattern:
                        y_ref[r, :] = acc / l_new
                        m_ref[r, :] = m_new + jnp.log(l_new)
                    else:
                        y_ref[r, :] = acc
                        m_ref[r, :] = m_new
                        l_scr[r, :] = l_new
                return carry

            lax.fori_loop(0, d * nb // U, step, 0)

    col = pl.BlockSpec((S, 2 * HEAD), lambda j: (0, j))
    return pl.pallas_call(
        body, name="attn_fwd", grid=(q.shape[1] // (2 * HEAD),),
        in_specs=[col, col, col], out_specs=[col, col],
        out_shape=[jax.ShapeDtypeStruct(q.shape, F32)] * 2,
        scratch_shapes=[pltpu.VMEM((S, 2 * HEAD), F32), pltpu.VMEM((2, N_BACK, 2 * N_BACK), F32)],
        compiler_params=_params("parallel"),
    )(q, k, v)


def _attn_bwd(q, k, v, dy, lse, delta, after):
    S = q.shape[0]
    U = BWD_BLOCKS_PER_STEP

    def body(q_ref, k_ref, v_ref, dy_ref, lse_ref, delta_ref, after_ref, dq_ref, dk_ref, dv_ref, bias):
        _fill_band_bias(bias)
        dk_ref[...] = jnp.zeros_like(dk_ref)
        dv_ref[...] = jnp.zeros_like(dv_ref)
        for g, d in enumerate(DILATIONS):
            nb = S // d // N_BACK

            def step(i, carry, d=d, nb=nb, g=g):
                row_blocks, blocks = _step_blocks(i, U, nb, d)
                kb = [k_ref[r, :].astype(BF16) for r in row_blocks]
                vb = [v_ref[r, :].astype(BF16) for r in row_blocks]
                kws = [jnp.concatenate([kb[before], kb[own]], 0) for before, own, _ in blocks]
                vws = [jnp.concatenate([vb[before], vb[own]], 0) for before, own, _ in blocks]
                qss = [_stack_heads(q_ref[row_blocks[own], :].astype(BF16)) for _, own, _ in blocks]
                doss = [_stack_heads(dy_ref[row_blocks[own], :].astype(BF16)) for _, own, _ in blocks]
                ss, dps = [], []
                for u, (_, _, which) in enumerate(blocks):
                    b = bias[which]
                    ss.append(lax.dot_general(qss[u], kws[u], NT, preferred_element_type=F32) + jnp.concatenate([b, b], axis=0))
                    dps.append(lax.dot_general(doss[u], vws[u], NT, preferred_element_type=F32))
                ps = [jnp.exp(ss[u] - _head_columns(lse_ref[row_blocks[own], :])) for u, (_, own, _) in enumerate(blocks)]
                dss = [(ps[u] * (dps[u] - _head_columns(delta_ref[row_blocks[own], :]))).astype(BF16)
                       for u, (_, own, _) in enumerate(blocks)]
                pbs = [p.astype(BF16) for p in ps]
                dqs = [jnp.dot(dss[u], kws[u], preferred_element_type=F32) for u in range(U)]
                dkws = [lax.dot_general(dss[u], qss[u], TN, preferred_element_type=F32) for u in range(U)]
                dvws = [lax.dot_general(pbs[u], doss[u], TN, preferred_element_type=F32) for u in range(U)]
                dk_parts, dv_parts = [None] * len(row_blocks), [None] * len(row_blocks)
                for u, (before, own, _) in enumerate(blocks):
                    dq = _unstack(dqs[u])
                    if g == 0:
                        dq_ref[row_blocks[own], :] = dq
                    else:
                        dq_ref[row_blocks[own], :] += dq
                    for idx, dkp, dvp in ((before, dkws[u][:N_BACK], dvws[u][:N_BACK]),
                                          (own, dkws[u][N_BACK:], dvws[u][N_BACK:])):
                        dk_parts[idx] = dkp if dk_parts[idx] is None else dk_parts[idx] + dkp
                        dv_parts[idx] = dvp if dv_parts[idx] is None else dv_parts[idx] + dvp
                for idx, r in enumerate(row_blocks):
                    dk_ref[r, :] += dk_parts[idx]
                    dv_ref[r, :] += dv_parts[idx]
                return carry

            lax.fori_loop(0, d * nb // U, step, 0)

    col = pl.BlockSpec((S, 2 * HEAD), lambda j: (0, j))
    return pl.pallas_call(
        body, name="attn_bwd", grid=(q.shape[1] // (2 * HEAD),),
        in_specs=[col] * 6 + [pl.BlockSpec(memory_space=pl.ANY)], out_specs=[col] * 3,
        out_shape=[jax.ShapeDtypeStruct(q.shape, F32)] * 3,
        scratch_shapes=[pltpu.VMEM((2, N_BACK, 2 * N_BACK), F32)],
        compiler_params=_params("parallel"),
    )(q, k, v, dy, lse, delta, after)


def _shift_down(z, before, k):
    row = lax.broadcasted_iota(jnp.int32, z.shape, 0)
    out = pltpu.roll(z, k, 0)
    for i in range(k):
        out = jnp.where(row == i, before[8 - k + i:8 - k + i + 1, :], out)
    return out


def _shift_up(z, after, k):
    rows = z.shape[0]
    row = lax.broadcasted_iota(jnp.int32, z.shape, 0)
    out = pltpu.roll(z, rows - k, 0)
    for i in range(k):
        out = jnp.where(row == rows - k + i, after[i:i + 1, :], out)
    return out


def _conv_fwd(bcu, before, is_first, w):
    b, c, u = bcu[:, 0:CONV_W], bcu[:, CONV_W:2 * CONV_W], bcu[:, 2 * CONV_W:3 * CONV_W]
    z = c * u
    zb = jnp.where(is_first, 0.0, before[:, CONV_W:2 * CONV_W] * before[:, 2 * CONV_W:3 * CONV_W])
    z1, z2 = _shift_down(z, zb, 1), _shift_down(z, zb, 2)
    cv = w[0:1, :] * z2 + w[1:2, :] * z1 + w[2:3, :] * z
    return b, c, u, z, z1, z2, cv


def _halo_before(tm, width):
    return pl.BlockSpec((8, width), lambda i: (jnp.maximum(i * (tm // 8) - 1, 0), 0))


def _halo_after(tm, width, S):
    return pl.BlockSpec((8, width), lambda i: (jnp.minimum((i + 1) * (tm // 8), S // 8 - 1), 0))


def _mix_fwd(ya, bcu, qx, mkv, conv_w, g_a, g_c, g_x, w_out, g_post, x, tm):
    S = x.shape[0]

    def body(ya_ref, bcu_ref, before_ref, qx_ref, mkv_ref, cw_ref, ga_ref, gc_ref, gx_ref,
             wo_ref, gp_ref, x_ref, yx_ref, ycat_ref, y2_ref, x1_ref):
        ya = ya_ref[...]
        b, _, _, _, _, _, cv = _conv_fwd(bcu_ref[...], before_ref[...], pl.program_id(0) == 0, cw_ref[...])
        yc = b * cv

        qxb, mkvb = qx_ref[...], mkv_ref[...]
        for hd in range(XATTN_W // HEAD):
            sl = slice(HEAD * hd, HEAD * (hd + 1))
            s = lax.dot_general(qxb[:, sl], mkvb[:, sl], NT, preferred_element_type=F32) * SCALE
            mx = jnp.max(s, axis=1, keepdims=True)
            p = jnp.exp(s - mx)
            l = jnp.sum(p, axis=1, keepdims=True)
            vm = mkvb[:, XATTN_W + HEAD * hd:XATTN_W + HEAD * (hd + 1)]
            yx_ref[:, sl] = jnp.dot(p.astype(BF16), vm, preferred_element_type=F32) / l
        yx = yx_ref[...]

        ycat_ref[:, 0:ATTN_W] = (_rms_hat(ya)[0] * ga_ref[...]).astype(BF16)
        ycat_ref[:, ATTN_W:ATTN_W + CONV_W] = (_rms_hat(yc)[0] * gc_ref[...]).astype(BF16)
        ycat_ref[:, ATTN_W + CONV_W:D_MODEL] = (_rms_hat(yx)[0] * gx_ref[...]).astype(BF16)
        y2 = jnp.dot(ycat_ref[...], wo_ref[...], preferred_element_type=F32)
        y2_ref[...] = y2
        x1_ref[...] = x_ref[...] + _rms_hat(y2)[0] * gp_ref[...]

    n_mem = mkv.shape[0]
    return pl.pallas_call(
        body, name="mix_fwd", grid=(S // tm,),
        in_specs=[_rows(tm, ATTN_W), _rows(tm, 3 * CONV_W), _halo_before(tm, 3 * CONV_W), _rows(tm, XATTN_W),
                  _resident((n_mem, 2 * XATTN_W)), _resident((3, CONV_W)), _resident((1, ATTN_W)),
                  _resident((1, CONV_W)), _resident((1, XATTN_W)), _resident((D_MODEL, D_MODEL)),
                  _resident((1, D_MODEL)), _rows(tm, D_MODEL)],
        out_specs=[_rows(tm, XATTN_W), _rows(tm, D_MODEL), _rows(tm, D_MODEL), _rows(tm, D_MODEL)],
        out_shape=[jax.ShapeDtypeStruct((S, XATTN_W), F32), jax.ShapeDtypeStruct((S, D_MODEL), BF16),
                   jax.ShapeDtypeStruct((S, D_MODEL), F32), jax.ShapeDtypeStruct((S, D_MODEL), F32)],
        compiler_params=_params("parallel"),
    )(ya, bcu, bcu, qx, mkv, conv_w, g_a, g_c, g_x, w_out, g_post, x)


def _mlp_fwd_bwd(x1, target, g_pre, g_post, w_up, w_down, tm):
    S = x1.shape[0]
    n_ff = D_FF // SHARD_FF

    def body(x1_ref, t_ref, gpre_ref, gpost_ref, wup_ref, wdn_ref,
             h2_ref, f_ref, du_ref, df2_ref, dx1_ref, dgpre_ref, dgpost_ref, loss_ref, u_scr):
        @pl.when(pl.program_id(0) == 0)
        def _():
            dgpre_ref[...] = jnp.zeros_like(dgpre_ref)
            dgpost_ref[...] = jnp.zeros_like(dgpost_ref)
            loss_ref[...] = jnp.zeros_like(loss_ref)

        x1 = x1_ref[...]
        x1hat, r1 = _rms_hat(x1)
        h2 = (x1hat * gpre_ref[...]).astype(BF16)
        h2_ref[...] = h2
        f2 = jnp.zeros((tm, D_MODEL), F32)
        for j in range(n_ff):
            cols = slice(SHARD_FF * j, SHARD_FF * (j + 1))
            u = jnp.maximum(jnp.dot(h2, wup_ref[j], preferred_element_type=F32), 0.0)
            u_scr[:, cols] = u
            f = (u * u).astype(BF16)
            f_ref[:, cols] = f
            f2 = f2 + jnp.dot(f, wdn_ref[cols, :], preferred_element_type=F32)
        f2hat, r2 = _rms_hat(f2)
        err = x1 + f2hat * gpost_ref[...] - t_ref[...]
        loss_ref[...] += 0.5 * jnp.sum(jnp.mean(err * err, axis=-1, keepdims=True), axis=0, keepdims=True)
        dx2 = err * (1.0 / D_MODEL)
        dgpost_ref[...] += jnp.sum(dx2 * f2hat, axis=0, keepdims=True)
        df2 = _rms_bwd(f2hat, r2, gpost_ref[...], dx2).astype(BF16)
        df2_ref[...] = df2
        dh2 = jnp.zeros((tm, D_MODEL), F32)
        for j in range(n_ff):
            cols = slice(SHARD_FF * j, SHARD_FF * (j + 1))
            df = lax.dot_general(df2, wdn_ref[cols, :], NT, preferred_element_type=F32)
            du = (2.0 * u_scr[:, cols] * df).astype(BF16)
            du_ref[:, cols] = du
            dh2 = dh2 + lax.dot_general(du, wup_ref[j], NT, preferred_element_type=F32)
        dgpre_ref[...] += jnp.sum(dh2 * x1hat, axis=0, keepdims=True)
        dx1_ref[...] = dx2 + _rms_bwd(x1hat, r1, gpre_ref[...], dh2)

    acc = pl.BlockSpec((1, D_MODEL), lambda i: (0, 0))
    return pl.pallas_call(
        body, name="mlp_fwd_bwd", grid=(S // tm,),
        in_specs=[_rows(tm, D_MODEL), _rows(tm, D_MODEL), _resident((1, D_MODEL)), _resident((1, D_MODEL)),
                  _resident((n_ff, D_MODEL, SHARD_FF)), _resident((D_FF, D_MODEL))],
        out_specs=[_rows(tm, D_MODEL), _rows(tm, D_FF), _rows(tm, D_FF), _rows(tm, D_MODEL), _rows(tm, D_MODEL),
                   acc, acc, pl.BlockSpec((1, 1), lambda i: (0, 0))],
        out_shape=[jax.ShapeDtypeStruct((S, D_MODEL), BF16), jax.ShapeDtypeStruct((S, D_FF), BF16),
                   jax.ShapeDtypeStruct((S, D_FF), BF16), jax.ShapeDtypeStruct((S, D_MODEL), BF16),
                   jax.ShapeDtypeStruct((S, D_MODEL), F32), jax.ShapeDtypeStruct((1, D_MODEL), F32),
                   jax.ShapeDtypeStruct((1, D_MODEL), F32), jax.ShapeDtypeStruct((1, 1), F32)],
        scratch_shapes=[pltpu.VMEM((tm, D_FF), F32)],
        compiler_params=_params("arbitrary"),
    )(x1, target, g_pre, g_post, w_up, w_down)


def _weight_grad(name, a, b, rows_sharded):
    S, K = a.shape
    N = b.shape[1]
    if rows_sharded:
        tk, tn = K // N_CHIPS, N
        a_spec = pl.BlockSpec((S, tk), lambda j: (0, j))
        b_spec = pl.BlockSpec((S, tn), lambda j: (0, 0), pipeline_mode=pl.Buffered(1))
    else:
        tk, tn = K, N // N_CHIPS
        a_spec = pl.BlockSpec((S, tk), lambda j: (0, 0), pipeline_mode=pl.Buffered(1))
        b_spec = pl.BlockSpec((S, tn), lambda j: (0, j))
    half = tk // 2

    def body(a_ref, b_ref, o_ref):
        res = lax.dot_general(a_ref[...], b_ref[...], TN, preferred_element_type=F32)
        o_ref[0, 0] = res[:half]
        o_ref[1, 0] = res[half:]

    return pl.pallas_call(
        body, name=name, grid=(N_CHIPS,), in_specs=[a_spec, b_spec],
        out_specs=pl.BlockSpec((2, 1, half, tn), lambda j: (0, j, 0, 0)),
        out_shape=jax.ShapeDtypeStruct((2, N_CHIPS, half, tn), F32),
        compiler_params=_params("parallel"),
    )(a, b)


def _mix_bwd(dx1, y2, ya, yx, bcu, conv_w, g_a, g_c, g_x, w_out, g_post, after, tm):
    S = dx1.shape[0]

    def body(dx1_ref, y2_ref, ya_ref, yx_ref, bcu_ref, before_ref, cw_ref, ga_ref, gc_ref, gx_ref, wo_ref, gp_ref,
             after_ref, dy2_ref, dya_ref, delta_ref, dycx_ref, dgp_ref, dga_ref, dgc_ref, dgx_ref):
        @pl.when(pl.program_id(0) == 0)
        def _():
            for ref in (dgp_ref, dga_ref, dgc_ref, dgx_ref):
                ref[...] = jnp.zeros_like(ref)

        dx1 = dx1_ref[...]
        y2hat, r2 = _rms_hat(y2_ref[...])
        dgp_ref[...] += jnp.sum(dx1 * y2hat, axis=0, keepdims=True)
        dy2 = _rms_bwd(y2hat, r2, gp_ref[...], dx1).astype(BF16)
        dy2_ref[...] = dy2
        dycat = lax.dot_general(dy2, wo_ref[...], NT, preferred_element_type=F32)

        d_na = dycat[:, 0:ATTN_W]
        ya = ya_ref[...]
        yahat, ra = _rms_hat(ya)
        dga_ref[...] += jnp.sum(d_na * yahat, axis=0, keepdims=True)
        dya = _rms_bwd(yahat, ra, ga_ref[...], d_na)
        dya_ref[...] = dya
        prod = dya * ya
        hi = prod.astype(BF16)
        lo = (prod - hi.astype(F32)).astype(BF16)
        head_of = lambda axis: lax.shift_right_logical(lax.broadcasted_iota(jnp.int32, (ATTN_W, ATTN_W), axis),
                                                       HEAD.bit_length() - 1)
        same_head = head_of(0) == head_of(1)
        ones = jnp.where(same_head, 1.0, 0.0).astype(BF16)
        delta_ref[...] = (jnp.dot(hi, ones, preferred_element_type=F32) + jnp.dot(lo, ones, preferred_element_type=F32))

        b, _, _, _, _, _, cv = _conv_fwd(bcu_ref[...], before_ref[...], pl.program_id(0) == 0, cw_ref[...])
        d_nc = dycat[:, ATTN_W:ATTN_W + CONV_W]
        ychat, rc = _rms_hat(b * cv)
        dgc_ref[...] += jnp.sum(d_nc * ychat, axis=0, keepdims=True)
        dycx_ref[:, 0:CONV_W] = _rms_bwd(ychat, rc, gc_ref[...], d_nc)

        d_nx = dycat[:, ATTN_W + CONV_W:D_MODEL]
        yxhat, rx = _rms_hat(yx_ref[...])
        dgx_ref[...] += jnp.sum(d_nx * yxhat, axis=0, keepdims=True)
        dycx_ref[:, CONV_W:CONV_W + XATTN_W] = _rms_bwd(yxhat, rx, gx_ref[...], d_nx)

    acc = lambda w: pl.BlockSpec((1, w), lambda i: (0, 0))
    return pl.pallas_call(
        body, name="mix_bwd", grid=(S // tm,),
        in_specs=[_rows(tm, D_MODEL), _rows(tm, D_MODEL), _rows(tm, ATTN_W), _rows(tm, XATTN_W),
                  _rows(tm, 3 * CONV_W), _halo_before(tm, 3 * CONV_W), _resident((3, CONV_W)),
                  _resident((1, ATTN_W)), _resident((1, CONV_W)), _resident((1, XATTN_W)),
                  _resident((D_MODEL, D_MODEL)), _resident((1, D_MODEL)), pl.BlockSpec(memory_space=pl.ANY)],
        out_specs=[_rows(tm, D_MODEL), _rows(tm, ATTN_W), _rows(tm, ATTN_W), _rows(tm, CONV_W + XATTN_W),
                   acc(D_MODEL), acc(ATTN_W), acc(CONV_W), acc(XATTN_W)],
        out_shape=[jax.ShapeDtypeStruct((S, D_MODEL), BF16), jax.ShapeDtypeStruct((S, ATTN_W), F32),
                   jax.ShapeDtypeStruct((S, ATTN_W), F32),
                   jax.ShapeDtypeStruct((S, CONV_W + XATTN_W), F32), jax.ShapeDtypeStruct((1, D_MODEL), F32),
                   jax.ShapeDtypeStruct((1, ATTN_W), F32), jax.ShapeDtypeStruct((1, CONV_W), F32),
                   jax.ShapeDtypeStruct((1, XATTN_W), F32)],
        compiler_params=_params("arbitrary"),
    )(dx1, y2, ya, yx, bcu, bcu, conv_w, g_a, g_c, g_x, w_out, g_post, after)


def _conv_xattn_bwd(dycx, bcu, qx, mkv, conv_w, behind, tm):
    S = dycx.shape[0]
    n_mem = mkv.shape[0]
    n_tiles = S // tm

    def body(d_ref, dafter_ref, bcu_ref, before_ref, after_ref, qx_ref, mkv_ref, cw_ref, behind_ref,
             tail_ref, dmkv_ref, dcw_ref):
        i = pl.program_id(0)

        @pl.when(i == 0)
        def _():
            dmkv_ref[...] = jnp.zeros_like(dmkv_ref)
            dcw_ref[...] = jnp.zeros_like(dcw_ref)

        w = cw_ref[...]
        b, c, u, z, z1, z2, cv = _conv_fwd(bcu_ref[...], before_ref[...], i == 0, w)
        dyc = d_ref[:, 0:CONV_W]
        dcv = dyc * b
        dcv_after = jnp.where(i == n_tiles - 1, 0.0, dafter_ref[:, 0:CONV_W] * after_ref[:, 0:CONV_W])
        dz = w[2:3, :] * dcv + w[1:2, :] * _shift_up(dcv, dcv_after, 1) + w[0:1, :] * _shift_up(dcv, dcv_after, 2)
        dcw_ref[0:1, :] += jnp.sum(dcv * z2, axis=0, keepdims=True)
        dcw_ref[1:2, :] += jnp.sum(dcv * z1, axis=0, keepdims=True)
        dcw_ref[2:3, :] += jnp.sum(dcv * z, axis=0, keepdims=True)
        tail_ref[:, 0:CONV_W] = (dyc * cv).astype(BF16)
        tail_ref[:, CONV_W:2 * CONV_W] = (dz * u).astype(BF16)
        tail_ref[:, 2 * CONV_W:3 * CONV_W] = (dz * c).astype(BF16)

        qxb, mkvb = qx_ref[...], mkv_ref[...]
        for hd in range(XATTN_W // HEAD):
            sl = slice(HEAD * hd, HEAD * (hd + 1))
            vsl = slice(XATTN_W + HEAD * hd, XATTN_W + HEAD * (hd + 1))
            s = lax.dot_general(qxb[:, sl], mkvb[:, sl], NT, preferred_element_type=F32) * SCALE
            e = jnp.exp(s - jnp.max(s, axis=1, keepdims=True))
            p = e / jnp.sum(e, axis=1, keepdims=True)
            dob = d_ref[:, CONV_W + HEAD * hd:CONV_W + HEAD * (hd + 1)].astype(BF16)
            dp = lax.dot_general(dob, mkvb[:, vsl], NT, preferred_element_type=F32)
            ds = (p * (dp - jnp.sum(p * dp, axis=1, keepdims=True)) * SCALE).astype(BF16)
            tail_ref[:, 3 * CONV_W + HEAD * hd:3 * CONV_W + HEAD * (hd + 1)] = jnp.dot(
                ds, mkvb[:, sl], preferred_element_type=F32).astype(BF16)
            dmkv_ref[:, sl] += lax.dot_general(ds, qxb[:, sl], TN, preferred_element_type=F32)
            dmkv_ref[:, vsl] += lax.dot_general(p.astype(BF16), dob, TN, preferred_element_type=F32)

    width = CONV_W + XATTN_W
    return pl.pallas_call(
        body, name="conv_xattn_bwd", grid=(n_tiles,),
        in_specs=[_rows(tm, width), _halo_after(tm, width, S), _rows(tm, 3 * CONV_W), _halo_before(tm, 3 * CONV_W),
                  _halo_after(tm, 3 * CONV_W, S), _rows(tm, XATTN_W), _resident((n_mem, 2 * XATTN_W)),
                  _resident((3, CONV_W)), pl.BlockSpec(memory_space=pl.ANY)],
        out_specs=[_rows(tm, 3 * CONV_W + XATTN_W), pl.BlockSpec((n_mem, 2 * XATTN_W), lambda i: (0, 0)),
                   pl.BlockSpec((3, CONV_W), lambda i: (0, 0))],
        out_shape=[jax.ShapeDtypeStruct((S, 3 * CONV_W + XATTN_W), BF16),
                   jax.ShapeDtypeStruct((n_mem, 2 * XATTN_W), F32), jax.ShapeDtypeStruct((3, CONV_W), F32)],
        compiler_params=_params("arbitrary"),
    )(dycx, dycx, bcu, bcu, bcu, qx, mkv, conv_w, behind)


def _memkv_bwd(mem, g_mem, w_kv, dmkv):
    n_mem = mem.shape[0]
    half = D_MODEL // N_CHIPS // 2

    def body(mem_ref, g_ref, w_ref, d_ref, dw_ref, dg_ref):
        mhat, _ = _rms_hat(mem_ref[...])
        mn = (mhat * g_ref[...]).astype(BF16)
        d = d_ref[...].astype(BF16)
        for k in range(2 * N_CHIPS):
            dw_ref[k % 2, k // 2] = lax.dot_general(mn[:, half * k:half * (k + 1)], d, TN, preferred_element_type=F32)
        dmn = lax.dot_general(d, w_ref[...], NT, preferred_element_type=F32)
        dg_ref[...] = jnp.sum(dmn * mhat, axis=0, keepdims=True)

    return pl.pallas_call(
        body, name="memkv_bwd",
        out_shape=[jax.ShapeDtypeStruct((2, N_CHIPS, half, 2 * XATTN_W), F32), jax.ShapeDtypeStruct((1, D_MODEL), F32)],
        compiler_params=pltpu.CompilerParams(vmem_limit_bytes=VMEM_LIMIT_V7X),
    )(mem, g_mem, w_kv, dmkv)


def _in_proj_bwd(dqkv, tail, cos, sin, w_in, x, g, dx1, after, tm):
    S = x.shape[0]

    def body(dq_ref, dk_ref, dv_ref, tail_ref, cos_ref, sin_ref, w_ref, x_ref, g_ref, dx1_ref, after_ref,
             dproj_ref, dx_ref, dg_ref):
        @pl.when(pl.program_id(0) == 0)
        def _():
            dg_ref[...] = jnp.zeros_like(dg_ref)

        c, s = cos_ref[...], sin_ref[...]
        for j in range(ATTN_W // 128):
            cols = slice(128 * j, 128 * (j + 1))
            dproj_ref[:, cols] = _rope128(dq_ref[:, cols] * SCALE, c, s, True).astype(BF16)
            dproj_ref[:, ATTN_W + 128 * j:ATTN_W + 128 * (j + 1)] = _rope128(dk_ref[:, cols], c, s, True).astype(BF16)
        dproj_ref[:, 2 * ATTN_W:3 * ATTN_W] = dv_ref[...].astype(BF16)
        dproj_ref[:, 3 * ATTN_W:PROJ_W] = tail_ref[...]
        dh = jnp.zeros((tm, D_MODEL), F32)
        for j in range(N_CHIPS):
            dh = dh + lax.dot_general(dproj_ref[:, SHARD_IN * j:SHARD_IN * (j + 1)], w_ref[j], NT,
                                      preferred_element_type=F32)
        xhat, r = _rms_hat(x_ref[...])
        dg_ref[...] += jnp.sum(dh * xhat, axis=0, keepdims=True)
        dx_ref[...] = dx1_ref[...] + _rms_bwd(xhat, r, g_ref[...], dh)

    return pl.pallas_call(
        body, name="in_proj_bwd", grid=(S // tm,),
        in_specs=[_rows(tm, ATTN_W)] * 3 + [_rows(tm, PROJ_W - 3 * ATTN_W), _rows(tm, 128), _rows(tm, 128),
                  _resident((N_CHIPS, D_MODEL, SHARD_IN)), _rows(tm, D_MODEL), _resident((1, D_MODEL)),
                  _rows(tm, D_MODEL), pl.BlockSpec(memory_space=pl.ANY)],
        out_specs=[_rows(tm, PROJ_W), _rows(tm, D_MODEL), pl.BlockSpec((1, D_MODEL), lambda i: (0, 0))],
        out_shape=[jax.ShapeDtypeStruct((S, PROJ_W), BF16), jax.ShapeDtypeStruct((S, D_MODEL), F32),
                   jax.ShapeDtypeStruct((1, D_MODEL), F32)],
        compiler_params=_params("arbitrary"),
    )(*dqkv, tail, cos, sin, w_in, x, g, dx1, after)


def _row_tile(rows):
    return ROW_TILE if rows % ROW_TILE == 0 else rows


def _chip_sums_bf16(name, grads, from_sibling, place):
    k = len(grads)
    _, n, rows, _ = grads[0].shape
    tr = _row_tile(rows)

    def body(place_ref, *refs):
        for g_ref, b_ref, o_ref in zip(refs[:k], refs[k:2 * k], refs[2 * k:]):
            o_ref[...] = (g_ref[0] + b_ref[...]).astype(BF16)

    mine = lambda g: pl.BlockSpec((1, 1, tr, g.shape[3]), lambda s, i, p: (p[0], s, i, 0))
    slab = lambda g: pl.BlockSpec((1, tr, g.shape[3]), lambda s, i, p: (s, i, 0))
    return pl.pallas_call(
        body, name=name, out_shape=[jax.ShapeDtypeStruct(g.shape[1:], BF16) for g in grads],
        grid_spec=pltpu.PrefetchScalarGridSpec(
            num_scalar_prefetch=1, grid=(n, rows // tr),
            in_specs=[mine(g) for g in grads] + [slab(g) for g in grads], out_specs=[slab(g) for g in grads]),
        compiler_params=_params("parallel", "parallel"),
    )(place, *grads, *from_sibling)


def _final_sums(name, grads, from_sibling, others, place):
    k = len(grads)
    rows = grads[0].shape[2]
    tr = _row_tile(rows)

    def body(place_ref, *refs):
        for a in range(k):
            own_ref, sib_ref = refs[a], refs[k + a]
            acc = own_ref[0, 0] + sib_ref[0]
            for o in refs[2 * k + 3 * a:2 * k + 3 * a + 3]:
                acc = acc + o[0].astype(F32)
            refs[5 * k + a][0] = acc

    own = lambda g: pl.BlockSpec((1, 1, tr, g.shape[3]), lambda i, p: (p[0], p[1], i, 0))
    sib = lambda g: pl.BlockSpec((1, tr, g.shape[3]), lambda i, p: (p[1], i, 0))
    other = lambda g, j: pl.BlockSpec((1, tr, g.shape[3]), lambda i, p: (j, i, 0))
    return pl.pallas_call(
        body, name=name, out_shape=[jax.ShapeDtypeStruct((2,) + g.shape[2:], F32) for g in grads],
        grid_spec=pltpu.PrefetchScalarGridSpec(
            num_scalar_prefetch=1, grid=(rows // tr,),
            in_specs=[own(g) for g in grads] + [sib(g) for g in grads] + [other(g, j) for g in grads for j in range(3)],
            out_specs=[pl.BlockSpec((1, tr, g.shape[3]), lambda i, p: (p[0], i, 0)) for g in grads]),
        compiler_params=_params("parallel"),
    )(place, *grads, *from_sibling, *[o for o in others for _ in range(3)])


def _adamw_update(w, g, m, v):
    m = ADAM_B1 * m + (1.0 - ADAM_B1) * g
    v = ADAM_B2 * v + (1.0 - ADAM_B2) * (g * g)
    m_hat = m * (1.0 / (1.0 - ADAM_B1 ** ADAM_STEP))
    v_hat = v * (1.0 / (1.0 - ADAM_B2 ** ADAM_STEP))
    return -ADAM_LR * (m_hat / (jnp.sqrt(v_hat) + ADAM_EPS) + ADAM_WD * w), m, v


def _adamw(name, params, after):
    k = len(params)
    rows = params[0][0].shape[0]
    tr = ADAMW_ROW_TILE if rows % ADAMW_ROW_TILE == 0 else rows

    def body(*refs):
        ins, outs = refs[:4 * k], refs[4 * k + 1:]
        for a in range(k):
            w_ref, g_ref, m_ref, v_ref = ins[4 * a:4 * a + 4]
            g = g_ref[...]
            outs[4 * a][...] = g
            outs[4 * a + 1][...], outs[4 * a + 2][...], outs[4 * a + 3][...] = _adamw_update(w_ref[...], g, m_ref[...], v_ref[...])

    spec = lambda w: pl.BlockSpec((tr, w.shape[1]), lambda i: (i, 0))
    out = pl.pallas_call(
        body, name=name, grid=(rows // tr,),
        in_specs=[spec(p[0]) for p in params for _ in range(4)] + [pl.BlockSpec(memory_space=pl.ANY)],
        out_specs=[spec(p[0]) for p in params for _ in range(4)],
        out_shape=[jax.ShapeDtypeStruct(p[0].shape, F32) for p in params for _ in range(4)],
        compiler_params=_params("parallel"),
    )(*[t for p in params for t in p], after)
    return [out[4 * a:4 * a + 4] for a in range(k)]


def _small_update(summed, chip, gains, gains_m, gains_v, taps, taps_m, taps_v):
    n = len(gains)
    widths = [g.shape[1] for g in gains]
    k, w = taps.shape

    def body(*refs):
        chip_ref, sum_ref = refs[0], refs[1]
        params = [refs[2 + 3 * i:5 + 3 * i] for i in range(n + 1)]
        outs = [refs[2 + 3 * (n + 1) + 4 * i:2 + 3 * (n + 1) + 4 * (i + 1)] for i in range(n + 1)]
        loss_ref = refs[-1]
        for i in range(n):
            g = sum_ref[i:i + 1, 0:widths[i]]
            wr, mr, vr = params[i]
            outs[i][0][...] = g
            outs[i][1][...], outs[i][2][...], outs[i][3][...] = _adamw_update(wr[...], g, mr[...], vr[...])
        g = sum_ref[n:n + k, 0:w]
        for j in range(1, N_CHIPS):
            g = jnp.where(chip_ref[0] == j, sum_ref[n:n + k, w * j:w * (j + 1)], g)
        wr, mr, vr = params[n]
        outs[n][0][...] = g
        outs[n][1][...], outs[n][2][...], outs[n][3][...] = _adamw_update(wr[...], g, mr[...], vr[...])
        loss_ref[...] = sum_ref[n + k:n + k + 1, 0:1]

    vmem = pl.BlockSpec(memory_space=pltpu.VMEM)
    operands = [chip, summed]
    for p in zip(list(gains) + [taps], list(gains_m) + [taps_m], list(gains_v) + [taps_v]):
        operands += list(p)
    shapes = [jax.ShapeDtypeStruct(p.shape, F32) for p in list(gains) + [taps] for _ in range(4)]
    out = pl.pallas_call(
        body, name="small_update", out_shape=shapes + [jax.ShapeDtypeStruct((1, 1), F32)],
        in_specs=[pl.BlockSpec(memory_space=pltpu.SMEM)] + [vmem] * (len(operands) - 1),
        out_specs=[vmem] * (len(shapes) + 1),
    )(*operands)
    return [out[4 * i:4 * (i + 1)] for i in range(n + 1)], out[-1]


def _sum_blocks(name, blocks):
    n, rows, cols = blocks.shape

    def body(b_ref, o_ref):
        acc = b_ref[0]
        for k in range(1, n):
            acc = acc + b_ref[k]
        o_ref[...] = acc

    return pl.pallas_call(body, name=name, out_shape=jax.ShapeDtypeStruct((rows, cols), F32))(blocks)


def _place():
    return lax.axis_index("x"), lax.axis_index("y"), lax.axis_index("c")


def _other_chips(x, y):
    return [(1 - x, y), (x, 1 - y), (1 - x, 1 - y)]


def _allgather_finish(name, shards, landed, pass_on):
    n = len(shards)

    def body(*refs):
        ins, outs, stage = refs[:n], refs[2 * n:3 * n], refs[3 * n:4 * n]
        send_sems, recv_sems, local_sems = refs[4 * n:]
        x, y, c = _place()
        chips = _other_chips(x, y)

        def copy(a, k, chip, half):
            place = outs[a].at[2 * chip[0] + chip[1], half]
            return pltpu.make_async_remote_copy(
                src_ref=place, dst_ref=place, send_sem=send_sems.at[3 * a + k], recv_sem=recv_sems.at[3 * a + k],
                device_id=(x, y, 1 - c), device_id_type=MESH)

        load = [pltpu.make_async_copy(ins[a], stage[a], local_sems.at[a]) for a in range(n)]
        local = [pltpu.make_async_copy(stage[a], outs[a].at[2 * x + y], local_sems.at[a]) for a in range(n)]
        for cp in load:
            cp.start()
        passed = [copy(a, k, chip, c) for a in range(n) if pass_on[a] for k, chip in enumerate(chips)]
        for cp in passed:
            cp.start()
        for a in range(n):
            load[a].wait()
            local[a].start()
        for a in range(n):
            if pass_on[a]:
                for k, chip in enumerate(chips):
                    copy(a, k, chip, 1 - c).wait_recv()
        for cp in passed:
            cp.wait_send()
        for cp in local:
            cp.wait()

    any_spec = pl.BlockSpec(memory_space=pl.ANY)
    return pl.pallas_call(
        body, name=name,
        out_shape=[jax.ShapeDtypeStruct((N_CHIPS,) + s.shape, s.dtype) for s in shards],
        in_specs=[any_spec] * (2 * n), out_specs=[any_spec] * n,
        input_output_aliases={n + a: a for a in range(n)},
        scratch_shapes=[pltpu.VMEM(s.shape, s.dtype) for s in shards]
        + [pltpu.SemaphoreType.DMA((3 * n,)), pltpu.SemaphoreType.DMA((3 * n,)), pltpu.SemaphoreType.DMA((n,))],
        compiler_params=pltpu.CompilerParams(vmem_limit_bytes=VMEM_LIMIT_V7X),
    )(*shards, *landed)


def _plan_first_hop(x, y, c, shards, lands):
    return [(shards[a].at[c], lands[a].at[2 * x + y, c], lands[a].at[2 * chip[0] + chip[1], c], (*chip, c))
            for a in range(len(shards)) for chip in _other_chips(x, y)]


def _plan_pass_on(x, y, c, nothing, lands):
    def place(a, chip, half):
        return lands[a].at[2 * chip[0] + chip[1], half]

    return [(place(a, chip, c), place(a, chip, c), place(a, chip, 1 - c), (x, y, 1 - c))
            for a in range(len(lands)) for chip in _other_chips(x, y)]


def _plan_own_half_to_sibling(x, y, c, nothing, lands):
    return [(lands[a].at[c], lands[a].at[c], lands[a].at[1 - c], (x, y, 1 - c)) for a in range(len(lands))]


def _plan_other_half_to_sibling(x, y, c, grads, lands):
    return [(grads[a].at[1 - c], lands[a], lands[a], (x, y, 1 - c)) for a in range(len(grads))]


def _plan_to_other_chips(x, y, c, partials, lands):
    return [(partials[a].at[2 * chip[0] + chip[1]], lands[a].at[k], lands[a].at[k], (*chip, c))
            for a in range(len(partials)) for k, chip in enumerate(_other_chips(x, y))]


def _plan_to_all(x, y, c, blocks, lands):
    flips = [(fx, fy, fc) for fx in (0, 1) for fy in (0, 1) for fc in (0, 1) if (fx, fy, fc) != (0, 0, 0)]
    peers = [(1 - x if fx else x, 1 - y if fy else y, 1 - c if fc else c) for fx, fy, fc in flips]
    return [(blocks[0], lands[0].at[4 * x + 2 * y + c], lands[0].at[4 * p[0] + 2 * p[1] + p[2]], p) for p in peers]


def _planned_copies(plan, srcs, lands, send_sems, recv_sems):
    x, y, c = _place()

    def pair(k, src, there, here, to):
        make = lambda dst: pltpu.make_async_remote_copy(
            src_ref=src, dst_ref=dst, send_sem=send_sems.at[k], recv_sem=recv_sems.at[k], device_id=to, device_id_type=MESH)
        return make(there), make(here)

    return [pair(k, *entry) for k, entry in enumerate(plan(x, y, c, srcs, lands))]


_HBM_SPEC = pl.BlockSpec(memory_space=pltpu.HBM)
_SEM_SPEC = pl.BlockSpec(memory_space=pltpu.SEMAPHORE)


def _hbm(a):
    return pltpu.with_memory_space_constraint(a, pltpu.HBM)


def _exchange_start(name, plan, n_copies, srcs, land_shapes, after, lands=None):
    if lands is None:
        lands = [lax.empty(s.shape, s.dtype) for s in land_shapes]
    land_shapes = lands
    ns, nl = len(srcs), len(land_shapes)
    n_in = ns + nl + 1

    def body(*refs):
        for send, _ in _planned_copies(plan, refs[:ns], refs[ns:ns + nl], refs[n_in], refs[n_in + 1]):
            send.start()
        refs[-1][...] = jnp.zeros_like(refs[-1])

    out = pl.pallas_call(
        body, name=name,
        out_shape=(pltpu.SemaphoreType.DMA((n_copies,)), pltpu.SemaphoreType.DMA((n_copies,)),
                   *[pltpu.HBM(s.shape, s.dtype) for s in land_shapes], jax.ShapeDtypeStruct((8, 128), F32)),
        in_specs=[_HBM_SPEC] * (ns + nl) + [pl.BlockSpec(memory_space=pl.ANY)],
        out_specs=(_SEM_SPEC, _SEM_SPEC, *[_HBM_SPEC] * nl, pl.BlockSpec(memory_space=pltpu.VMEM)),
        input_output_aliases={ns + i: 2 + i for i in range(nl)},
        compiler_params=pltpu.CompilerParams(has_side_effects=pltpu.SideEffectType.DATAFLOW_SIDE_EFFECTING),
    )(*[_hbm(s) for s in srcs], *[_hbm(l) for l in lands], after)
    return out[0], out[1], list(out[2:2 + nl]), out[-1]


def _exchange_wait(name, plan, srcs, started, after):
    send_sems, recv_sems, lands, _ = started
    ns, nl = len(srcs), len(lands)
    after = list(after) if isinstance(after, (list, tuple)) else [after]

    def body(*refs):
        for send, recv in _planned_copies(plan, refs[:ns], refs[ns:ns + nl], refs[ns + nl], refs[ns + nl + 1]):
            send.wait_send()
            recv.wait_recv()

    return pl.pallas_call(
        body, name=name, out_shape=[pltpu.HBM(l.shape, l.dtype) for l in lands],
        in_specs=[_HBM_SPEC] * (ns + nl) + [_SEM_SPEC, _SEM_SPEC] + [pl.BlockSpec(memory_space=pl.ANY)] * len(after),
        out_specs=[_HBM_SPEC] * nl, input_output_aliases={ns + i: i for i in range(nl)},
        compiler_params=pltpu.CompilerParams(has_side_effects=pltpu.SideEffectType.DATAFLOW_SIDE_EFFECTING),
    )(*[_hbm(s) for s in srcs], *lands, send_sems, recv_sems, *after)


def _like(arrays, lead, dtype=None):
    return [jax.ShapeDtypeStruct(tuple(lead) + a.shape[-2:], dtype or a.dtype) for a in arrays]


class _StepExchanges:
    def __init__(self, mats, conv_w):
        x, y, c = _place()
        self.place = jnp.stack([c, 2 * x + y]).astype(jnp.int32)
        shards = [w.astype(BF16).reshape(2, w.shape[0] // 2, w.shape[1]) for w in mats]
        self._in_shard = shards[:1]
        self._in = _exchange_start("w_in_allgather_start", _plan_first_hop, 3, self._in_shard,
                                   _like(self._in_shard, (N_CHIPS, 2)), shards[0])
        self.zero = self._in[3]
        taps = jnp.pad(conv_w, ((0, 8 - conv_w.shape[0]), (0, 128 - conv_w.shape[1])))
        self._rest_shards = shards[1:] + [jnp.stack([taps, jnp.zeros_like(taps)])]
        self._taps_shape = conv_w.shape
        self._groups = {}

    def w_in(self, after):
        landed = _exchange_wait("w_in_allgather_wait", _plan_first_hop, self._in_shard, self._in,
                                list(after) + self._rest_shards)
        (w_in,) = _allgather_finish("w_in_allgather_finish", self._in_shard, landed, [True])
        self._rest = _exchange_start("rest_allgather_start", _plan_first_hop, 3 * len(self._rest_shards),
                                     self._rest_shards, _like(self._rest_shards, (N_CHIPS, 2)), w_in)
        self.zero = self._rest[3]
        return w_in.reshape(N_CHIPS, 2 * w_in.shape[2], w_in.shape[3])

    def rest_weights(self, after):
        landed = _exchange_wait("rest_allgather_wait", _plan_first_hop, self._rest_shards, self._rest, after)
        kv, out, up, down, taps = _allgather_finish("rest_allgather_finish", self._rest_shards, landed,
                                                    [True, True, False, False, True])
        self._up_down = _exchange_start("up_down_pass_on_start", _plan_pass_on, 6, [], None, self.zero, lands=[up, down])
        self.zero = self._up_down[3]
        k, w = self._taps_shape
        taps = taps[:, 0, :k, :w].transpose(1, 0, 2).reshape(k, N_CHIPS * w)
        return [g.reshape(N_CHIPS, 2 * g.shape[2], g.shape[3]) for g in (kv, out)], taps

    def up_down(self, after):
        full = _exchange_wait("up_down_pass_on_wait", _plan_pass_on, [], self._up_down, after)
        return [g.reshape(N_CHIPS, 2 * g.shape[2], g.shape[3]) for g in full]

    def send_grads(self, key, grads):
        grads = list(grads)
        started = _exchange_start(f"{key}_grads_to_sibling_start", _plan_other_half_to_sibling, len(grads), grads,
                                  _like(grads, (N_CHIPS,)), self.zero)
        self._groups[key] = dict(grads=grads, to_sibling=started)
        self.zero = started[3]

    def grads_at_sibling(self, key, after):
        group = self._groups[key]
        grads = group["grads"]
        group["from_sibling"] = _exchange_wait(f"{key}_grads_to_sibling_wait", _plan_other_half_to_sibling, grads,
                                               group["to_sibling"], after)
        group["partials"] = _chip_sums_bf16(f"{key}_chip_sums", grads, group["from_sibling"], self.place)
        group["to_chips"] = _exchange_start(f"{key}_grads_to_chips_start", _plan_to_other_chips, 3 * len(grads),
                                            group["partials"], _like(group["partials"], (3,)), self.zero)
        self.zero = group["to_chips"][3]

    def grads_summed(self, key, after):
        group = self._groups[key]
        from_chips = _exchange_wait(f"{key}_grads_to_chips_wait", _plan_to_other_chips, group["partials"],
                                    group["to_chips"], after)
        return _final_sums(f"{key}_final_sums", group["grads"], group["from_sibling"], from_chips, self.place)

    def send_sums(self, key, sums):
        self._groups[key + "_sums"] = _exchange_start(f"{key}_sums_to_sibling_start", _plan_own_half_to_sibling,
                                                      len(sums), [], None, self.zero, lands=list(sums))
        self.zero = self._groups[key + "_sums"][3]

    def whole_sums(self, key, after):
        full = _exchange_wait(f"{key}_sums_to_sibling_wait", _plan_own_half_to_sibling, [], self._groups[key + "_sums"], after)
        return [t.reshape(2 * t.shape[1], t.shape[2]) for t in full]

    def send_small(self, block):
        self._small = block
        self._small_started = _exchange_start("small_grads_start", _plan_to_all, 7, [block],
                                              [jax.ShapeDtypeStruct((8,) + block.shape, block.dtype)], self.zero)
        self.zero = self._small_started[3]

    def small_summed(self, after):
        x, y, c = _place()
        (landed,) = _exchange_wait("small_grads_wait", _plan_to_all, [self._small], self._small_started, after)
        blocks = lax.dynamic_update_index_in_dim(landed, self._small, 4 * x + 2 * y + c, 0)
        return _sum_blocks("small_sum", blocks)


def _rope_tables(positions):
    half = HEAD // 2
    inv_freq = jnp.float32(ROPE_THETA) ** (-(jnp.arange(half, dtype=F32) * 2.0 / HEAD))
    ang = positions.astype(F32)[:, None] * inv_freq
    cos, sin = jnp.cos(ang), jnp.sin(ang)
    return jnp.tile(cos, (1, 4)), jnp.tile(jnp.concatenate([-sin, sin], axis=1), (1, 2))


def _local_step(x, mem, positions, target, gains, ex):
    g_pre_mix, g_mem, g_a, g_c, g_x, g_post_mix, g_pre_mlp, g_post_mlp = gains
    tm = ROW_TILE
    cos, sin = _rope_tables(positions)
    h = _pre_norm(x, g_pre_mix, ex.zero, tm)
    w_in = ex.w_in([h, cos, sin])

    q, k, v, bcu, qx = _in_proj_fwd(h, w_in, cos, sin, ex.zero, tm)
    ya, lse = _attn_fwd(q, k, v)
    (w_kv, w_out), conv_w = ex.rest_weights(lse)
    w_kv, w_out = (w.reshape(N_CHIPS * w.shape[1], w.shape[2]) for w in (w_kv, w_out))
    memn, mkv = _memkv_fwd(mem, g_mem, w_kv, ex.zero)
    yx, ycat, y2, x1 = _mix_fwd(ya, bcu, qx, mkv, conv_w, g_a, g_c, g_x, w_out, g_post_mix, x, tm)
    w_up, w_down = ex.up_down(x1)
    w_down = w_down.reshape(N_CHIPS * w_down.shape[1], w_down.shape[2])
    h2, f, du, df2, dx1, dg_pre_mlp, dg_post_mlp, loss = _mlp_fwd_bwd(x1, target, g_pre_mlp, g_post_mlp, w_up, w_down,
                                                                      MLP_ROW_TILE)
    gw_down = _weight_grad("grad_w_down", f, df2, True)
    gw_up = _weight_grad("grad_w_up", h2, du, False)
    ex.send_grads("early", [gw_up, gw_down])

    dy2, dya, delta, dycx, dg_post_mix, dg_a, dg_c, dg_x = _mix_bwd(dx1, y2, ya, yx, bcu, conv_w, g_a, g_c, g_x,
                                                                  w_out, g_post_mix, ex.zero, tm)
    ex.grads_at_sibling("early", dy2)
    gw_out = _weight_grad("grad_w_out", ycat, dy2, True)
    tail, dmkv, g_conv = _conv_xattn_bwd(dycx, bcu, qx, mkv, conv_w, ex.zero, tm)
    gw_kv, dg_mem = _memkv_bwd(mem, g_mem, w_kv, dmkv)
    ex.send_grads("mid", [gw_out, gw_kv])
    dqkv = _attn_bwd(q, k, v, dya, lse, delta, ex.zero)
    ex.grads_at_sibling("mid", dqkv[0])
    dproj, grad_x, dg_pre_mix = _in_proj_bwd(dqkv, tail, cos, sin, w_in, x, g_pre_mix, dx1, ex.zero, tm)
    gain_grads = [dg_pre_mix, dg_mem, dg_a, dg_c, dg_x, dg_post_mix, dg_pre_mlp, dg_post_mlp]
    ex.send_small(_pack_small(gain_grads, g_conv, loss))
    gw_in = _weight_grad("grad_w_in", h, dproj, False)
    ex.send_grads("late", [gw_in])
    return grad_x


def _pack_small(gains, conv, scalar=None):
    rows = [jnp.pad(g, ((0, 0), (0, D_MODEL - g.shape[1]))) for g in gains]
    rows.append(jnp.pad(conv, ((0, 0), (0, D_MODEL - conv.shape[1]))))
    last = jnp.zeros((SMALL_ROWS - 8 - conv.shape[0], D_MODEL), F32)
    rows.append(last if scalar is None else last.at[0:1, 0:1].set(scalar))
    return jnp.concatenate(rows, axis=0)


def _unpack_small(block, gain_widths, conv_width):
    gains = [block[i:i + 1, :w] for i, w in enumerate(gain_widths)]
    return gains, block[8:11, :conv_width], block[11, 0]


def kernel(x, mem, positions, g_pre_mix, g_mem, w_in, w_mem_kv, conv_w, g_attn_out, g_conv_out, g_xattn_out, w_out, g_post_mix, g_pre_mlp, w_up, w_down, g_post_mlp, loss_target, m_g_pre_mix, m_g_mem, m_w_in, m_w_mem_kv, m_conv_w, m_g_attn_out, m_g_conv_out, m_g_xattn_out, m_w_out, m_g_post_mix, m_g_pre_mlp, m_w_up, m_w_down, m_g_post_mlp, v_g_pre_mix, v_g_mem, v_w_in, v_w_mem_kv, v_conv_w, v_g_attn_out, v_g_conv_out, v_g_xattn_out, v_w_out, v_g_post_mix, v_g_pre_mlp, v_w_up, v_w_down, v_g_post_mlp):
    cx, cy, cc = _place()
    chip = 2 * cx + cy
    gains = [g_pre_mix, g_mem, g_attn_out, g_conv_out, g_xattn_out, g_post_mix, g_pre_mlp, g_post_mlp]
    gains_m = [m_g_pre_mix, m_g_mem, m_g_attn_out, m_g_conv_out, m_g_xattn_out, m_g_post_mix, m_g_pre_mlp, m_g_post_mlp]
    gains_v = [v_g_pre_mix, v_g_mem, v_g_attn_out, v_g_conv_out, v_g_xattn_out, v_g_post_mix, v_g_pre_mlp, v_g_post_mlp]
    gain_widths = [g.shape[1] for g in gains]
    mats = [w_in[0], w_mem_kv[0], w_out[0], w_up[0], w_down[0]]
    mats_m = [m_w_in[0], m_w_mem_kv[0], m_w_out[0], m_w_up[0], m_w_down[0]]
    mats_v = [v_w_in[0], v_w_mem_kv[0], v_w_out[0], v_w_up[0], v_w_down[0]]

    ex = _StepExchanges(mats, conv_w[0])
    grad_x = _local_step(x[0], mem[0], positions[0], loss_target[0], gains, ex)

    ex.send_sums("four", ex.grads_summed("early", ex.zero) + ex.grads_summed("mid", ex.zero))
    ex.grads_at_sibling("late", ex.zero)
    up_sum, down_sum, out_sum, kv_sum = ex.whole_sums("four", ex.zero)
    params = lambda a, g: (mats[a], g, mats_m[a], mats_v[a])
    new_up, new_down = _adamw("adamw_up_down", [params(3, up_sum), params(4, down_sum)], ex.zero)
    new_out, new_kv = _adamw("adamw_out_kv", [params(2, out_sum), params(1, kv_sum)], ex.zero)

    small, total = _small_update(ex.small_summed(new_kv[1]), chip.reshape(1).astype(jnp.int32), gains, gains_m,
                                 gains_v, conv_w[0], m_conv_w[0], v_conv_w[0])

    ex.send_sums("last", ex.grads_summed("late", small[0][1]))
    (in_sum,) = ex.whole_sums("last", ex.zero)
    (new_in,) = _adamw("adamw_in", [params(0, in_sum)], in_sum)
    mat_new = [new_in, new_kv, new_out, new_up, new_down]

    order = ["g_pre_mix", "g_mem", "w_in", "w_mem_kv", "conv_w", "g_attn_out", "g_conv_out", "g_xattn_out", "w_out",
             "g_post_mix", "g_pre_mlp", "w_up", "w_down", "g_post_mlp"]
    gain_names = ["g_pre_mix", "g_mem", "g_attn_out", "g_conv_out", "g_xattn_out", "g_post_mix", "g_pre_mlp", "g_post_mlp"]
    mat_names = ["w_in", "w_mem_kv", "w_out", "w_up", "w_down"]

    def leaf(kind, name):
        if name in gain_names:
            return small[gain_names.index(name)][kind]
        if name == "conv_w":
            return small[len(gain_names)][kind][None]
        return mat_new[mat_names.index(name)][kind][None]

    return (total[0, 0], grad_x[None], *[leaf(kind, name) for kind in range(4) for name in order])
```

```python
import jax
import jax.numpy as jnp
from jax import lax
from jax.experimental import pallas as pl
from jax.experimental.pallas import tpu as pltpu

F32, BF16 = jnp.float32, jnp.bfloat16

D_MODEL = 1024
ATTN_W = 512
CONV_W = 256
XATTN_W = 256
PROJ_W = 3 * ATTN_W + 3 * CONV_W + XATTN_W
D_FF = 4096
HEAD = 64
N_BACK = 128
DILATIONS = (1, 4, 16)
ROPE_THETA = 10000.0
EPS = 1e-6
NEG_INF = -1e30
SCALE = HEAD ** -0.5
N_CHIPS = 4
SHARD_IN = PROJ_W // N_CHIPS
SHARD_FF = D_FF // N_CHIPS

ADAM_LR, ADAM_B1, ADAM_B2, ADAM_EPS, ADAM_WD, ADAM_STEP = 0.001, 0.9, 0.999, 1e-08, 0.01, 10

VMEM_LIMIT_V7X = 56 * 1024 * 1024
ROW_TILE = 512
MLP_ROW_TILE = 256
ADAMW_ROW_TILE = 256
SMALL_ROWS = 16

NT = (((1,), (1,)), ((), ()))
TN = (((0,), (0,)), ((), ()))
MESH = pl.DeviceIdType.MESH


def _params(*sem):
    return pltpu.CompilerParams(dimension_semantics=sem, vmem_limit_bytes=VMEM_LIMIT_V7X)


def _resident(shape):
    return pl.BlockSpec(shape, lambda *_: (0,) * len(shape), pipeline_mode=pl.Buffered(1))


def _rows(tm, width):
    return pl.BlockSpec((tm, width), lambda i: (i, 0))


def _rms_hat(x):
    r = lax.rsqrt(jnp.mean(x * x, axis=-1, keepdims=True) + EPS)
    return x * r, r


def _rms_bwd(xhat, r, g, dy):
    gdy = dy * g
    return r * (gdy - xhat * jnp.mean(xhat * gdy, axis=-1, keepdims=True))


def _rope128(t, cos, sin_signed, inverse):
    lane = lax.broadcasted_iota(jnp.int32, t.shape, 1)
    first_half = (lane % HEAD) < (HEAD // 2)
    rot = jnp.where(first_half, pltpu.roll(t, 128 - HEAD // 2, 1), pltpu.roll(t, HEAD // 2, 1))
    return t * cos - rot * sin_signed if inverse else t * cos + rot * sin_signed


def _pre_norm(x, g, after, tm):
    S = x.shape[0]

    def body(x_ref, g_ref, after_ref, h_ref):
        h_ref[...] = (_rms_hat(x_ref[...])[0] * g_ref[...]).astype(BF16)

    return pl.pallas_call(
        body, name="pre_norm", grid=(S // tm,),
        in_specs=[_rows(tm, D_MODEL), _resident((1, D_MODEL)), pl.BlockSpec(memory_space=pl.ANY)],
        out_specs=_rows(tm, D_MODEL), out_shape=jax.ShapeDtypeStruct((S, D_MODEL), BF16),
        compiler_params=_params("parallel"),
    )(x, g, after)


def _in_proj_fwd(h, w_in, cos, sin, after, tm):
    S = h.shape[0]

    def body(h_ref, w_ref, cos_ref, sin_ref, after_ref, q_ref, k_ref, v_ref, bcu_ref, qx_ref, proj):
        h = h_ref[...]
        for j in range(N_CHIPS):
            proj[:, SHARD_IN * j:SHARD_IN * (j + 1)] = jnp.dot(h, w_ref[j], preferred_element_type=F32)
        c, s = cos_ref[...], sin_ref[...]
        for j in range(ATTN_W // 128):
            lo = 128 * j
            q_ref[:, lo:lo + 128] = _rope128(proj[:, lo:lo + 128], c, s, False) * SCALE
            k_ref[:, lo:lo + 128] = _rope128(proj[:, ATTN_W + lo:ATTN_W + lo + 128], c, s, False)
        v_ref[...] = proj[:, 2 * ATTN_W:3 * ATTN_W]
        bcu_ref[...] = proj[:, 3 * ATTN_W:3 * ATTN_W + 3 * CONV_W]
        qx_ref[...] = proj[:, 3 * ATTN_W + 3 * CONV_W:PROJ_W].astype(BF16)

    return pl.pallas_call(
        body, name="in_proj_fwd", grid=(S // tm,),
        in_specs=[_rows(tm, D_MODEL), _resident((N_CHIPS, D_MODEL, SHARD_IN)), _rows(tm, 128), _rows(tm, 128),
                  pl.BlockSpec(memory_space=pl.ANY)],
        out_specs=[_rows(tm, ATTN_W), _rows(tm, ATTN_W), _rows(tm, ATTN_W), _rows(tm, 3 * CONV_W), _rows(tm, XATTN_W)],
        out_shape=[jax.ShapeDtypeStruct((S, ATTN_W), F32), jax.ShapeDtypeStruct((S, ATTN_W), F32),
                   jax.ShapeDtypeStruct((S, ATTN_W), F32), jax.ShapeDtypeStruct((S, 3 * CONV_W), F32),
                   jax.ShapeDtypeStruct((S, XATTN_W), BF16)],
        scratch_shapes=[pltpu.VMEM((tm, PROJ_W), F32)],
        compiler_params=_params("parallel"),
    )(h, w_in, cos, sin, after)


def _memkv_fwd(mem, g_mem, w_kv, after):
    n_mem = mem.shape[0]

    def body(mem_ref, g_ref, w_ref, after_ref, mn_ref, kv_ref):
        mhat, _ = _rms_hat(mem_ref[...])
        mn = (mhat * g_ref[...]).astype(BF16)
        mn_ref[...] = mn
        kv_ref[...] = jnp.dot(mn, w_ref[...], preferred_element_type=F32).astype(BF16)

    vmem = pl.BlockSpec(memory_space=pltpu.VMEM)
    return pl.pallas_call(
        body, name="memkv_fwd", in_specs=[vmem, vmem, vmem, pl.BlockSpec(memory_space=pl.ANY)], out_specs=[vmem, vmem],
        out_shape=[jax.ShapeDtypeStruct((n_mem, D_MODEL), BF16), jax.ShapeDtypeStruct((n_mem, 2 * XATTN_W), BF16)],
        compiler_params=pltpu.CompilerParams(vmem_limit_bytes=VMEM_LIMIT_V7X),
    )(mem, g_mem, w_kv, after)


def _fill_band_bias(bias):
    row = lax.broadcasted_iota(jnp.int32, (N_BACK, 2 * N_BACK), 0)
    col = lax.broadcasted_iota(jnp.int32, (N_BACK, 2 * N_BACK), 1)
    band = (col >= row) & (col <= row + N_BACK)
    bias[1] = jnp.where(band, 0.0, NEG_INF)
    bias[0] = jnp.where(band & (col >= N_BACK), 0.0, NEG_INF)


def _strided(start, size, d):
    return pl.ds(start, size) if d == 1 else pl.ds(start, size, stride=d)


def _group_starts(g, G, nb, d):
    t0 = g * G
    r, n0 = lax.shift_right_logical(t0, nb.bit_length() - 1), lax.bitwise_and(t0, nb - 1)
    first = r + n0 * (N_BACK * d)
    before = r + jnp.maximum(n0 - 1, 0) * (N_BACK * d)
    starts = [before] + [first + u * (N_BACK * d) for u in range(G)]
    if d == 1:
        starts = [pl.multiple_of(st, N_BACK) for st in starts]
    return starts, n0


def _step_blocks(i, U, nb, d):
    G = min(U, nb)
    whole = G == nb
    row_blocks, blocks = [], []
    for grp in range(U // G):
        starts, n0 = _group_starts(i * (U // G) + grp, G, nb, d)
        base = len(row_blocks)
        if whole:
            row_blocks += [_strided(st, N_BACK, d) for st in starts[1:]]
            blocks += [(base + max(u - 1, 0), base + u, min(u, 1)) for u in range(G)]
        else:
            row_blocks += [_strided(st, N_BACK, d) for st in starts]
            blocks += [(base + u, base + u + 1, jnp.minimum(n0, 1) if u == 0 else 1) for u in range(G)]
    return row_blocks, blocks


def _by_head(a, b):
    lane = lax.broadcasted_iota(jnp.int32, (a.shape[0], 2 * HEAD), 1)
    return jnp.where(lane < HEAD, a, b)


def _head_only(t, hh):
    lane = lax.broadcasted_iota(jnp.int32, t.shape, 1)
    return jnp.where((lane < HEAD) == (hh == 0), t, jnp.zeros_like(t))


def _stack_heads(t):
    return jnp.concatenate([_head_only(t, 0), _head_only(t, 1)], axis=0)


def _head_columns(t):
    return jnp.concatenate([t[:, 0:1], t[:, HEAD:HEAD + 1]], axis=0)


def _unstack(t):
    return _by_head(t[:N_BACK], t[N_BACK:])


def _unstack_columns(t):
    return _by_head(jnp.broadcast_to(t[:N_BACK], (N_BACK, 2 * HEAD)), jnp.broadcast_to(t[N_BACK:], (N_BACK, 2 * HEAD)))


FWD_BLOCKS_PER_STEP = 4
BWD_BLOCKS_PER_STEP = 4


def _attn_fwd(q, k, v):
    S = q.shape[0]
    U = FWD_BLOCKS_PER_STEP

    def body(q_ref, k_ref, v_ref, y_ref, m_ref, l_scr, bias):
        _fill_band_bias(bias)
        for g, d in enumerate(DILATIONS):
            nb = S // d // N_BACK
            first_pattern, last_pattern = g == 0, g == len(DILATIONS) - 1

            def step(i, carry, d=d, nb=nb, first_pattern=first_pattern, last_pattern=last_pattern):
                row_blocks, blocks = _step_blocks(i, U, nb, d)
                kb = [k_ref[r, :].astype(BF16) for r in row_blocks]
                ss = []
                for before, own, which in blocks:
                    kw = jnp.concatenate([kb[before], kb[own]], 0)
                    qs = _stack_heads(q_ref[row_blocks[own], :].astype(BF16))
                    b = bias[which]
                    ss.append(lax.dot_general(qs, kw, NT, preferred_element_type=F32) + jnp.concatenate([b, b], axis=0))
                ms = [jnp.max(s, axis=1, keepdims=True) for s in ss]
                ps = [jnp.exp(s - m) for s, m in zip(ss, ms)]
                ls = [jnp.sum(p, axis=1, keepdims=True) for p in ps]
                vb = [v_ref[r, :].astype(BF16) for r in row_blocks]
                os_ = [jnp.dot(ps[u].astype(BF16), jnp.concatenate([vb[before], vb[own]], 0), preferred_element_type=F32)
                       for u, (before, own, _) in enumerate(blocks)]
                for u, (_, own, _) in enumerate(blocks):
                    o_g, m_g, l_g = _unstack(os_[u]), _unstack_columns(ms[u]), _unstack_columns(ls[u])
                    r = row_blocks[own]
                    if first_pattern:
                        m_new, l_new, acc = m_g, l_g, o_g
                    else:
                        m_old = m_ref[r, :]
                        m_new = jnp.maximum(m_old, m_g)
                        alpha, beta = jnp.exp(m_old - m_new), jnp.exp(m_g - m_new)
                        l_new = l_scr[r, :] * alpha + l_g * beta
                        acc = y_ref[r, :] * alpha + o_g * beta
                    if last_pattern:
                        y_ref[r, :] = acc / l_new
                        m_ref[r, :] = m_new + jnp.log(l_new)
                    else:
                        y_ref[r, :] = acc
                        m_ref[r, :] = m_new
                        l_scr[r, :] = l_new
                return carry

            lax.fori_loop(0, d * nb // U, step, 0)

    col = pl.BlockSpec((S, 2 * HEAD), lambda j: (0, j))
    return pl.pallas_call(
        body, name="attn_fwd", grid=(q.shape[1] // (2 * HEAD),),
        in_specs=[col, col, col], out_specs=[col, col],
        out_shape=[jax.ShapeDtypeStruct(q.shape, F32)] * 2,
        scratch_shapes=[pltpu.VMEM((S, 2 * HEAD), F32), pltpu.VMEM((2, N_BACK, 2 * N_BACK), F32)],
        compiler_params=_params("parallel"),
    )(q, k, v)


def _attn_bwd(q, k, v, dy, lse, delta, after):
    S = q.shape[0]
    U = BWD_BLOCKS_PER_STEP

    def body(q_ref, k_ref, v_ref, dy_ref, lse_ref, delta_ref, after_ref, dq_ref, dk_ref, dv_ref, bias):
        _fill_band_bias(bias)
        dk_ref[...] = jnp.zeros_like(dk_ref)
        dv_ref[...] = jnp.zeros_like(dv_ref)
        for g, d in enumerate(DILATIONS):
            nb = S // d // N_BACK

            def step(i, carry, d=d, nb=nb, g=g):
                row_blocks, blocks = _step_blocks(i, U, nb, d)
                kb = [k_ref[r, :].astype(BF16) for r in row_blocks]
                vb = [v_ref[r, :].astype(BF16) for r in row_blocks]
                kws = [jnp.concatenate([kb[before], kb[own]], 0) for before, own, _ in blocks]
                vws = [jnp.concatenate([vb[before], vb[own]], 0) for before, own, _ in blocks]
                qss = [_stack_heads(q_ref[row_blocks[own], :].astype(BF16)) for _, own, _ in blocks]
                doss = [_stack_heads(dy_ref[row_blocks[own], :].astype(BF16)) for _, own, _ in blocks]
                ss, dps = [], []
                for u, (_, _, which) in enumerate(blocks):
                    b = bias[which]
                    ss.append(lax.dot_general(qss[u], kws[u], NT, preferred_element_type=F32) + jnp.concatenate([b, b], axis=0))
                    dps.append(lax.dot_general(doss[u], vws[u], NT, preferred_element_type=F32))
                ps = [jnp.exp(ss[u] - _head_columns(lse_ref[row_blocks[own], :])) for u, (_, own, _) in enumerate(blocks)]
                dss = [(ps[u] * (dps[u] - _head_columns(delta_ref[row_blocks[own], :]))).astype(BF16)
                       for u, (_, own, _) in enumerate(blocks)]
                pbs = [p.astype(BF16) for p in ps]
                dqs = [jnp.dot(dss[u], kws[u], preferred_element_type=F32) for u in range(U)]
                dkws = [lax.dot_general(dss[u], qss[u], TN, preferred_element_type=F32) for u in range(U)]
                dvws = [lax.dot_general(pbs[u], doss[u], TN, preferred_element_type=F32) for u in range(U)]
                dk_parts, dv_parts = [None] * len(row_blocks), [None] * len(row_blocks)
                for u, (before, own, _) in enumerate(blocks):
                    dq = _unstack(dqs[u])
                    if g == 0:
                        dq_ref[row_blocks[own], :] = dq
                    else:
                        dq_ref[row_blocks[own], :] += dq
                    for idx, dkp, dvp in ((before, dkws[u][:N_BACK], dvws[u][:N_BACK]),
                                          (own, dkws[u][N_BACK:], dvws[u][N_BACK:])):
                        dk_parts[idx] = dkp if dk_parts[idx] is None else dk_parts[idx] + dkp
                        dv_parts[idx] = dvp if dv_parts[idx] is None else dv_parts[idx] + dvp
                for idx, r in enumerate(row_blocks):
                    dk_ref[r, :] += dk_parts[idx]
                    dv_ref[r, :] += dv_parts[idx]
                return carry

            lax.fori_loop(0, d * nb // U, step, 0)

    col = pl.BlockSpec((S, 2 * HEAD), lambda j: (0, j))
    return pl.pallas_call(
        body, name="attn_bwd", grid=(q.shape[1] // (2 * HEAD),),
        in_specs=[col] * 6 + [pl.BlockSpec(memory_space=pl.ANY)], out_specs=[col] * 3,
        out_shape=[jax.ShapeDtypeStruct(q.shape, F32)] * 3,
        scratch_shapes=[pltpu.VMEM((2, N_BACK, 2 * N_BACK), F32)],
        compiler_params=_params("parallel"),
    )(q, k, v, dy, lse, delta, after)


def _shift_down(z, before, k):
    row = lax.broadcasted_iota(jnp.int32, z.shape, 0)
    out = pltpu.roll(z, k, 0)
    for i in range(k):
        out = jnp.where(row == i, before[8 - k + i:8 - k + i + 1, :], out)
    return out


def _shift_up(z, after, k):
    rows = z.shape[0]
    row = lax.broadcasted_iota(jnp.int32, z.shape, 0)
    out = pltpu.roll(z, rows - k, 0)
    for i in range(k):
        out = jnp.where(row == rows - k + i, after[i:i + 1, :], out)
    return out


def _conv_fwd(bcu, before, is_first, w):
    b, c, u = bcu[:, 0:CONV_W], bcu[:, CONV_W:2 * CONV_W], bcu[:, 2 * CONV_W:3 * CONV_W]
    z = c * u
    zb = jnp.where(is_first, 0.0, before[:, CONV_W:2 * CONV_W] * before[:, 2 * CONV_W:3 * CONV_W])
    z1, z2 = _shift_down(z, zb, 1), _shift_down(z, zb, 2)
    cv = w[0:1, :] * z2 + w[1:2, :] * z1 + w[2:3, :] * z
    return b, c, u, z, z1, z2, cv


def _halo_before(tm, width):
    return pl.BlockSpec((8, width), lambda i: (jnp.maximum(i * (tm // 8) - 1, 0), 0))


def _halo_after(tm, width, S):
    return pl.BlockSpec((8, width), lambda i: (jnp.minimum((i + 1) * (tm // 8), S // 8 - 1), 0))


def _mix_fwd(ya, bcu, qx, mkv, conv_w, g_a, g_c, g_x, w_out, g_post, x, tm):
    S = x.shape[0]

    def body(ya_ref, bcu_ref, before_ref, qx_ref, mkv_ref, cw_ref, ga_ref, gc_ref, gx_ref,
             wo_ref, gp_ref, x_ref, yx_ref, ycat_ref, y2_ref, x1_ref):
        ya = ya_ref[...]
        b, _, _, _, _, _, cv = _conv_fwd(bcu_ref[...], before_ref[...], pl.program_id(0) == 0, cw_ref[...])
        yc = b * cv

        qxb, mkvb = qx_ref[...], mkv_ref[...]
        for hd in range(XATTN_W // HEAD):
            sl = slice(HEAD * hd, HEAD * (hd + 1))
            s = lax.dot_general(qxb[:, sl], mkvb[:, sl], NT, preferred_element_type=F32) * SCALE
            mx = jnp.max(s, axis=1, keepdims=True)
            p = jnp.exp(s - mx)
            l = jnp.sum(p, axis=1, keepdims=True)
            vm = mkvb[:, XATTN_W + HEAD * hd:XATTN_W + HEAD * (hd + 1)]
            yx_ref[:, sl] = jnp.dot(p.astype(BF16), vm, preferred_element_type=F32) / l
        yx = yx_ref[...]

        ycat_ref[:, 0:ATTN_W] = (_rms_hat(ya)[0] * ga_ref[...]).astype(BF16)
        ycat_ref[:, ATTN_W:ATTN_W + CONV_W] = (_rms_hat(yc)[0] * gc_ref[...]).astype(BF16)
        ycat_ref[:, ATTN_W + CONV_W:D_MODEL] = (_rms_hat(yx)[0] * gx_ref[...]).astype(BF16)
        y2 = jnp.dot(ycat_ref[...], wo_ref[...], preferred_element_type=F32)
        y2_ref[...] = y2
        x1_ref[...] = x_ref[...] + _rms_hat(y2)[0] * gp_ref[...]

    n_mem = mkv.shape[0]
    return pl.pallas_call(
        body, name="mix_fwd", grid=(S // tm,),
        in_specs=[_rows(tm, ATTN_W), _rows(tm, 3 * CONV_W), _halo_before(tm, 3 * CONV_W), _rows(tm, XATTN_W),
                  _resident((n_mem, 2 * XATTN_W)), _resident((3, CONV_W)), _resident((1, ATTN_W)),
                  _resident((1, CONV_W)), _resident((1, XATTN_W)), _resident((D_MODEL, D_MODEL)),
                  _resident((1, D_MODEL)), _rows(tm, D_MODEL)],
        out_specs=[_rows(tm, XATTN_W), _rows(tm, D_MODEL), _rows(tm, D_MODEL), _rows(tm, D_MODEL)],
        out_shape=[jax.ShapeDtypeStruct((S, XATTN_W), F32), jax.ShapeDtypeStruct((S, D_MODEL), BF16),
                   jax.ShapeDtypeStruct((S, D_MODEL), F32), jax.ShapeDtypeStruct((S, D_MODEL), F32)],
        compiler_params=_params("parallel"),
    )(ya, bcu, bcu, qx, mkv, conv_w, g_a, g_c, g_x, w_out, g_post, x)


def _mlp_fwd_bwd(x1, target, g_pre, g_post, w_up, w_down, tm):
    S = x1.shape[0]
    n_ff = D_FF // SHARD_FF

    def body(x1_ref, t_ref, gpre_ref, gpost_ref, wup_ref, wdn_ref,
             h2_ref, f_ref, du_ref, df2_ref, dx1_ref, dgpre_ref, dgpost_ref, loss_ref, u_scr):
        @pl.when(pl.program_id(0) == 0)
        def _():
            dgpre_ref[...] = jnp.zeros_like(dgpre_ref)
            dgpost_ref[...] = jnp.zeros_like(dgpost_ref)
            loss_ref[...] = jnp.zeros_like(loss_ref)

        x1 = x1_ref[...]
        x1hat, r1 = _rms_hat(x1)
        h2 = (x1hat * gpre_ref[...]).astype(BF16)
        h2_ref[...] = h2
        f2 = jnp.zeros((tm, D_MODEL), F32)
        for j in range(n_ff):
            cols = slice(SHARD_FF * j, SHARD_FF * (j + 1))
            u = jnp.maximum(jnp.dot(h2, wup_ref[j], preferred_element_type=F32), 0.0)
            u_scr[:, cols] = u
            f = (u * u).astype(BF16)
            f_ref[:, cols] = f
            f2 = f2 + jnp.dot(f, wdn_ref[cols, :], preferred_element_type=F32)
        f2hat, r2 = _rms_hat(f2)
        err = x1 + f2hat * gpost_ref[...] - t_ref[...]
        loss_ref[...] += 0.5 * jnp.sum(jnp.mean(err * err, axis=-1, keepdims=True), axis=0, keepdims=True)
        dx2 = err * (1.0 / D_MODEL)
        dgpost_ref[...] += jnp.sum(dx2 * f2hat, axis=0, keepdims=True)
        df2 = _rms_bwd(f2hat, r2, gpost_ref[...], dx2).astype(BF16)
        df2_ref[...] = df2
        dh2 = jnp.zeros((tm, D_MODEL), F32)
        for j in range(n_ff):
            cols = slice(SHARD_FF * j, SHARD_FF * (j + 1))
            df = lax.dot_general(df2, wdn_ref[cols, :], NT, preferred_element_type=F32)
            du = (2.0 * u_scr[:, cols] * df).astype(BF16)
            du_ref[:, cols] = du
            dh2 = dh2 + lax.dot_general(du, wup_ref[j], NT, preferred_element_type=F32)
        dgpre_ref[...] += jnp.sum(dh2 * x1hat, axis=0, keepdims=True)
        dx1_ref[...] = dx2 + _rms_bwd(x1hat, r1, gpre_ref[...], dh2)

    acc = pl.BlockSpec((1, D_MODEL), lambda i: (0, 0))
    return pl.pallas_call(
        body, name="mlp_fwd_bwd", grid=(S // tm,),
        in_specs=[_rows(tm, D_MODEL), _rows(tm, D_MODEL), _resident((1, D_MODEL)), _resident((1, D_MODEL)),
                  _resident((n_ff, D_MODEL, SHARD_FF)), _resident((D_FF, D_MODEL))],
        out_specs=[_rows(tm, D_MODEL), _rows(tm, D_FF), _rows(tm, D_FF), _rows(tm, D_MODEL), _rows(tm, D_MODEL),
                   acc, acc, pl.BlockSpec((1, 1), lambda i: (0, 0))],
        out_shape=[jax.ShapeDtypeStruct((S, D_MODEL), BF16), jax.ShapeDtypeStruct((S, D_FF), BF16),
                   jax.ShapeDtypeStruct((S, D_FF), BF16), jax.ShapeDtypeStruct((S, D_MODEL), BF16),
                   jax.ShapeDtypeStruct((S, D_MODEL), F32), jax.ShapeDtypeStruct((1, D_MODEL), F32),
                   jax.ShapeDtypeStruct((1, D_MODEL), F32), jax.ShapeDtypeStruct((1, 1), F32)],
        scratch_shapes=[pltpu.VMEM((tm, D_FF), F32)],
        compiler_params=_params("arbitrary"),
    )(x1, target, g_pre, g_post, w_up, w_down)


def _weight_grad(name, a, b, rows_sharded):
    S, K = a.shape
    N = b.shape[1]
    if rows_sharded:
        tk, tn = K // N_CHIPS, N
        a_spec = pl.BlockSpec((S, tk), lambda j: (0, j))
        b_spec = pl.BlockSpec((S, tn), lambda j: (0, 0), pipeline_mode=pl.Buffered(1))
    else:
        tk, tn = K, N // N_CHIPS
        a_spec = pl.BlockSpec((S, tk), lambda j: (0, 0), pipeline_mode=pl.Buffered(1))
        b_spec = pl.BlockSpec((S, tn), lambda j: (0, j))
    half = tk // 2

    def body(a_ref, b_ref, o_ref):
        res = lax.dot_general(a_ref[...], b_ref[...], TN, preferred_element_type=F32)
        o_ref[0, 0] = res[:half]
        o_ref[1, 0] = res[half:]

    return pl.pallas_call(
        body, name=name, grid=(N_CHIPS,), in_specs=[a_spec, b_spec],
        out_specs=pl.BlockSpec((2, 1, half, tn), lambda j: (0, j, 0, 0)),
        out_shape=jax.ShapeDtypeStruct((2, N_CHIPS, half, tn), F32),
        compiler_params=_params("parallel"),
    )(a, b)


def _mix_bwd(dx1, y2, ya, yx, bcu, conv_w, g_a, g_c, g_x, w_out, g_post, after, tm):
    S = dx1.shape[0]

    def body(dx1_ref, y2_ref, ya_ref, yx_ref, bcu_ref, before_ref, cw_ref, ga_ref, gc_ref, gx_ref, wo_ref, gp_ref,
             after_ref, dy2_ref, dya_ref, delta_ref, dycx_ref, dgp_ref, dga_ref, dgc_ref, dgx_ref):
        @pl.when(pl.program_id(0) == 0)
        def _():
            for ref in (dgp_ref, dga_ref, dgc_ref, dgx_ref):
                ref[...] = jnp.zeros_like(ref)

        dx1 = dx1_ref[...]
        y2hat, r2 = _rms_hat(y2_ref[...])
        dgp_ref[...] += jnp.sum(dx1 * y2hat, axis=0, keepdims=True)
        dy2 = _rms_bwd(y2hat, r2, gp_ref[...], dx1).astype(BF16)
        dy2_ref[...] = dy2
        dycat = lax.dot_general(dy2, wo_ref[...], NT, preferred_element_type=F32)

        d_na = dycat[:, 0:ATTN_W]
        ya = ya_ref[...]
        yahat, ra = _rms_hat(ya)
        dga_ref[...] += jnp.sum(d_na * yahat, axis=0, keepdims=True)
        dya = _rms_bwd(yahat, ra, ga_ref[...], d_na)
        dya_ref[...] = dya
        prod = dya * ya
        hi = prod.astype(BF16)
        lo = (prod - hi.astype(F32)).astype(BF16)
        head_of = lambda axis: lax.shift_right_logical(lax.broadcasted_iota(jnp.int32, (ATTN_W, ATTN_W), axis),
                                                       HEAD.bit_length() - 1)
        same_head = head_of(0) == head_of(1)
        ones = jnp.where(same_head, 1.0, 0.0).astype(BF16)
        delta_ref[...] = (jnp.dot(hi, ones, preferred_element_type=F32) + jnp.dot(lo, ones, preferred_element_type=F32))

        b, _, _, _, _, _, cv = _conv_fwd(bcu_ref[...], before_ref[...], pl.program_id(0) == 0, cw_ref[...])
        d_nc = dycat[:, ATTN_W:ATTN_W + CONV_W]
        ychat, rc = _rms_hat(b * cv)
        dgc_ref[...] += jnp.sum(d_nc * ychat, axis=0, keepdims=True)
        dycx_ref[:, 0:CONV_W] = _rms_bwd(ychat, rc, gc_ref[...], d_nc)

        d_nx = dycat[:, ATTN_W + CONV_W:D_MODEL]
        yxhat, rx = _rms_hat(yx_ref[...])
        dgx_ref[...] += jnp.sum(d_nx * yxhat, axis=0, keepdims=True)
        dycx_ref[:, CONV_W:CONV_W + XATTN_W] = _rms_bwd(yxhat, rx, gx_ref[...], d_nx)

    acc = lambda w: pl.BlockSpec((1, w), lambda i: (0, 0))
    return pl.pallas_call(
        body, name="mix_bwd", grid=(S // tm,),
        in_specs=[_rows(tm, D_MODEL), _rows(tm, D_MODEL), _rows(tm, ATTN_W), _rows(tm, XATTN_W),
                  _rows(tm, 3 * CONV_W), _halo_before(tm, 3 * CONV_W), _resident((3, CONV_W)),
                  _resident((1, ATTN_W)), _resident((1, CONV_W)), _resident((1, XATTN_W)),
                  _resident((D_MODEL, D_MODEL)), _resident((1, D_MODEL)), pl.BlockSpec(memory_space=pl.ANY)],
        out_specs=[_rows(tm, D_MODEL), _rows(tm, ATTN_W), _rows(tm, ATTN_W), _rows(tm, CONV_W + XATTN_W),
                   acc(D_MODEL), acc(ATTN_W), acc(CONV_W), acc(XATTN_W)],
        out_shape=[jax.ShapeDtypeStruct((S, D_MODEL), BF16), jax.ShapeDtypeStruct((S, ATTN_W), F32),
                   jax.ShapeDtypeStruct((S, ATTN_W), F32),
                   jax.ShapeDtypeStruct((S, CONV_W + XATTN_W), F32), jax.ShapeDtypeStruct((1, D_MODEL), F32),
                   jax.ShapeDtypeStruct((1, ATTN_W), F32), jax.ShapeDtypeStruct((1, CONV_W), F32),
                   jax.ShapeDtypeStruct((1, XATTN_W), F32)],
        compiler_params=_params("arbitrary"),
    )(dx1, y2, ya, yx, bcu, bcu, conv_w, g_a, g_c, g_x, w_out, g_post, after)


def _conv_xattn_bwd(dycx, bcu, qx, mkv, conv_w, behind, tm):
    S = dycx.shape[0]
    n_mem = mkv.shape[0]
    n_tiles = S // tm

    def body(d_ref, dafter_ref, bcu_ref, before_ref, after_ref, qx_ref, mkv_ref, cw_ref, behind_ref,
             tail_ref, dmkv_ref, dcw_ref):
        i = pl.program_id(0)

        @pl.when(i == 0)
        def _():
            dmkv_ref[...] = jnp.zeros_like(dmkv_ref)
            dcw_ref[...] = jnp.zeros_like(dcw_ref)

        w = cw_ref[...]
        b, c, u, z, z1, z2, cv = _conv_fwd(bcu_ref[...], before_ref[...], i == 0, w)
        dyc = d_ref[:, 0:CONV_W]
        dcv = dyc * b
        dcv_after = jnp.where(i == n_tiles - 1, 0.0, dafter_ref[:, 0:CONV_W] * after_ref[:, 0:CONV_W])
        dz = w[2:3, :] * dcv + w[1:2, :] * _shift_up(dcv, dcv_after, 1) + w[0:1, :] * _shift_up(dcv, dcv_after, 2)
        dcw_ref[0:1, :] += jnp.sum(dcv * z2, axis=0, keepdims=True)
        dcw_ref[1:2, :] += jnp.sum(dcv * z1, axis=0, keepdims=True)
        dcw_ref[2:3, :] += jnp.sum(dcv * z, axis=0, keepdims=True)
        tail_ref[:, 0:CONV_W] = (dyc * cv).astype(BF16)
        tail_ref[:, CONV_W:2 * CONV_W] = (dz * u).astype(BF16)
        tail_ref[:, 2 * CONV_W:3 * CONV_W] = (dz * c).astype(BF16)

        qxb, mkvb = qx_ref[...], mkv_ref[...]
        for hd in range(XATTN_W // HEAD):
            sl = slice(HEAD * hd, HEAD * (hd + 1))
            vsl = slice(XATTN_W + HEAD * hd, XATTN_W + HEAD * (hd + 1))
            s = lax.dot_general(qxb[:, sl], mkvb[:, sl], NT, preferred_element_type=F32) * SCALE
            e = jnp.exp(s - jnp.max(s, axis=1, keepdims=True))
            p = e / jnp.sum(e, axis=1, keepdims=True)
            dob = d_ref[:, CONV_W + HEAD * hd:CONV_W + HEAD * (hd + 1)].astype(BF16)
            dp = lax.dot_general(dob, mkvb[:, vsl], NT, preferred_element_type=F32)
            ds = (p * (dp - jnp.sum(p * dp, axis=1, keepdims=True)) * SCALE).astype(BF16)
            tail_ref[:, 3 * CONV_W + HEAD * hd:3 * CONV_W + HEAD * (hd + 1)] = jnp.dot(
                ds, mkvb[:, sl], preferred_element_type=F32).astype(BF16)
            dmkv_ref[:, sl] += lax.dot_general(ds, qxb[:, sl], TN, preferred_element_type=F32)
            dmkv_ref[:, vsl] += lax.dot_general(p.astype(BF16), dob, TN, preferred_element_type=F32)

    width = CONV_W + XATTN_W
    return pl.pallas_call(
        body, name="conv_xattn_bwd", grid=(n_tiles,),
        in_specs=[_rows(tm, width), _halo_after(tm, width, S), _rows(tm, 3 * CONV_W), _halo_before(tm, 3 * CONV_W),
                  _halo_after(tm, 3 * CONV_W, S), _rows(tm, XATTN_W), _resident((n_mem, 2 * XATTN_W)),
                  _resident((3, CONV_W)), pl.BlockSpec(memory_space=pl.ANY)],
        out_specs=[_rows(tm, 3 * CONV_W + XATTN_W), pl.BlockSpec((n_mem, 2 * XATTN_W), lambda i: (0, 0)),
                   pl.BlockSpec((3, CONV_W), lambda i: (0, 0))],
        out_shape=[jax.ShapeDtypeStruct((S, 3 * CONV_W + XATTN_W), BF16),
                   jax.ShapeDtypeStruct((n_mem, 2 * XATTN_W), F32), jax.ShapeDtypeStruct((3, CONV_W), F32)],
        compiler_params=_params("arbitrary"),
    )(dycx, dycx, bcu, bcu, bcu, qx, mkv, conv_w, behind)


def _memkv_bwd(mem, g_mem, w_kv, dmkv):
    n_mem = mem.shape[0]
    half = D_MODEL // N_CHIPS // 2

    def body(mem_ref, g_ref, w_ref, d_ref, dw_ref, dg_ref):
        mhat, _ = _rms_hat(mem_ref[...])
        mn = (mhat * g_ref[...]).astype(BF16)
        d = d_ref[...].astype(BF16)
        for k in range(2 * N_CHIPS):
            dw_ref[k % 2, k // 2] = lax.dot_general(mn[:, half * k:half * (k + 1)], d, TN, preferred_element_type=F32)
        dmn = lax.dot_general(d, w_ref[...], NT, preferred_element_type=F32)
        dg_ref[...] = jnp.sum(dmn * mhat, axis=0, keepdims=True)

    return pl.pallas_call(
        body, name="memkv_bwd",
        out_shape=[jax.ShapeDtypeStruct((2, N_CHIPS, half, 2 * XATTN_W), F32), jax.ShapeDtypeStruct((1, D_MODEL), F32)],
        compiler_params=pltpu.CompilerParams(vmem_limit_bytes=VMEM_LIMIT_V7X),
    )(mem, g_mem, w_kv, dmkv)


def _in_proj_bwd(dqkv, tail, cos, sin, w_in, x, g, dx1, after, tm):
    S = x.shape[0]

    def body(dq_ref, dk_ref, dv_ref, tail_ref, cos_ref, sin_ref, w_ref, x_ref, g_ref, dx1_ref, after_ref,
             dproj_ref, dx_ref, dg_ref):
        @pl.when(pl.program_id(0) == 0)
        def _():
            dg_ref[...] = jnp.zeros_like(dg_ref)

        c, s = cos_ref[...], sin_ref[...]
        for j in range(ATTN_W // 128):
            cols = slice(128 * j, 128 * (j + 1))
            dproj_ref[:, cols] = _rope128(dq_ref[:, cols] * SCALE, c, s, True).astype(BF16)
            dproj_ref[:, ATTN_W + 128 * j:ATTN_W + 128 * (j + 1)] = _rope128(dk_ref[:, cols], c, s, True).astype(BF16)
        dproj_ref[:, 2 * ATTN_W:3 * ATTN_W] = dv_ref[...].astype(BF16)
        dproj_ref[:, 3 * ATTN_W:PROJ_W] = tail_ref[...]
        dh = jnp.zeros((tm, D_MODEL), F32)
        for j in range(N_CHIPS):
            dh = dh + lax.dot_general(dproj_ref[:, SHARD_IN * j:SHARD_IN * (j + 1)], w_ref[j], NT,
                                      preferred_element_type=F32)
        xhat, r = _rms_hat(x_ref[...])
        dg_ref[...] += jnp.sum(dh * xhat, axis=0, keepdims=True)
        dx_ref[...] = dx1_ref[...] + _rms_bwd(xhat, r, g_ref[...], dh)

    return pl.pallas_call(
        body, name="in_proj_bwd", grid=(S // tm,),
        in_specs=[_rows(tm, ATTN_W)] * 3 + [_rows(tm, PROJ_W - 3 * ATTN_W), _rows(tm, 128), _rows(tm, 128),
                  _resident((N_CHIPS, D_MODEL, SHARD_IN)), _rows(tm, D_MODEL), _resident((1, D_MODEL)),
                  _rows(tm, D_MODEL), pl.BlockSpec(memory_space=pl.ANY)],
        out_specs=[_rows(tm, PROJ_W), _rows(tm, D_MODEL), pl.BlockSpec((1, D_MODEL), lambda i: (0, 0))],
        out_shape=[jax.ShapeDtypeStruct((S, PROJ_W), BF16), jax.ShapeDtypeStruct((S, D_MODEL), F32),
                   jax.ShapeDtypeStruct((1, D_MODEL), F32)],
        compiler_params=_params("arbitrary"),
    )(*dqkv, tail, cos, sin, w_in, x, g, dx1, after)


def _row_tile(rows):
    return ROW_TILE if rows % ROW_TILE == 0 else rows


def _chip_sums_bf16(name, grads, from_sibling, place):
    k = len(grads)
    _, n, rows, _ = grads[0].shape
    tr = _row_tile(rows)

    def body(place_ref, *refs):
        for g_ref, b_ref, o_ref in zip(refs[:k], refs[k:2 * k], refs[2 * k:]):
            o_ref[...] = (g_ref[0] + b_ref[...]).astype(BF16)

    mine = lambda g: pl.BlockSpec((1, 1, tr, g.shape[3]), lambda s, i, p: (p[0], s, i, 0))
    slab = lambda g: pl.BlockSpec((1, tr, g.shape[3]), lambda s, i, p: (s, i, 0))
    return pl.pallas_call(
        body, name=name, out_shape=[jax.ShapeDtypeStruct(g.shape[1:], BF16) for g in grads],
        grid_spec=pltpu.PrefetchScalarGridSpec(
            num_scalar_prefetch=1, grid=(n, rows // tr),
            in_specs=[mine(g) for g in grads] + [slab(g) for g in grads], out_specs=[slab(g) for g in grads]),
        compiler_params=_params("parallel", "parallel"),
    )(place, *grads, *from_sibling)


def _final_sums(name, grads, from_sibling, others, place):
    k = len(grads)
    rows = grads[0].shape[2]
    tr = _row_tile(rows)

    def body(place_ref, *refs):
        for a in range(k):
            own_ref, sib_ref = refs[a], refs[k + a]
            acc = own_ref[0, 0] + sib_ref[0]
            for o in refs[2 * k + 3 * a:2 * k + 3 * a + 3]:
                acc = acc + o[0].astype(F32)
            refs[5 * k + a][0] = acc

    own = lambda g: pl.BlockSpec((1, 1, tr, g.shape[3]), lambda i, p: (p[0], p[1], i, 0))
    sib = lambda g: pl.BlockSpec((1, tr, g.shape[3]), lambda i, p: (p[1], i, 0))
    other = lambda g, j: pl.BlockSpec((1, tr, g.shape[3]), lambda i, p: (j, i, 0))
    return pl.pallas_call(
        body, name=name, out_shape=[jax.ShapeDtypeStruct((2,) + g.shape[2:], F32) for g in grads],
        grid_spec=pltpu.PrefetchScalarGridSpec(
            num_scalar_prefetch=1, grid=(rows // tr,),
            in_specs=[own(g) for g in grads] + [sib(g) for g in grads] + [other(g, j) for g in grads for j in range(3)],
            out_specs=[pl.BlockSpec((1, tr, g.shape[3]), lambda i, p: (p[0], i, 0)) for g in grads]),
        compiler_params=_params("parallel"),
    )(place, *grads, *from_sibling, *[o for o in others for _ in range(3)])


def _adamw_update(w, g, m, v):
    m = ADAM_B1 * m + (1.0 - ADAM_B1) * g
    v = ADAM_B2 * v + (1.0 - ADAM_B2) * (g * g)
    m_hat = m * (1.0 / (1.0 - ADAM_B1 ** ADAM_STEP))
    v_hat = v * (1.0 / (1.0 - ADAM_B2 ** ADAM_STEP))
    return -ADAM_LR * (m_hat / (jnp.sqrt(v_hat) + ADAM_EPS) + ADAM_WD * w), m, v


def _adamw(name, params, after):
    k = len(params)
    rows = params[0][0].shape[0]
    tr = ADAMW_ROW_TILE if rows % ADAMW_ROW_TILE == 0 else rows

    def body(*refs):
        ins, outs = refs[:4 * k], refs[4 * k + 1:]
        for a in range(k):
            w_ref, g_ref, m_ref, v_ref = ins[4 * a:4 * a + 4]
            g = g_ref[...]
            outs[4 * a][...] = g
            outs[4 * a + 1][...], outs[4 * a + 2][...], outs[4 * a + 3][...] = _adamw_update(w_ref[...], g, m_ref[...], v_ref[...])

    spec = lambda w: pl.BlockSpec((tr, w.shape[1]), lambda i: (i, 0))
    out = pl.pallas_call(
        body, name=name, grid=(rows // tr,),
        in_specs=[spec(p[0]) for p in params for _ in range(4)] + [pl.BlockSpec(memory_space=pl.ANY)],
        out_specs=[spec(p[0]) for p in params for _ in range(4)],
        out_shape=[jax.ShapeDtypeStruct(p[0].shape, F32) for p in params for _ in range(4)],
        compiler_params=_params("parallel"),
    )(*[t for p in params for t in p], after)
    return [out[4 * a:4 * a + 4] for a in range(k)]


def _small_update(summed, chip, gains, gains_m, gains_v, taps, taps_m, taps_v):
    n = len(gains)
    widths = [g.shape[1] for g in gains]
    k, w = taps.shape

    def body(*refs):
        chip_ref, sum_ref = refs[0], refs[1]
        params = [refs[2 + 3 * i:5 + 3 * i] for i in range(n + 1)]
        outs = [refs[2 + 3 * (n + 1) + 4 * i:2 + 3 * (n + 1) + 4 * (i + 1)] for i in range(n + 1)]
        loss_ref = refs[-1]
        for i in range(n):
            g = sum_ref[i:i + 1, 0:widths[i]]
            wr, mr, vr = params[i]
            outs[i][0][...] = g
            outs[i][1][...], outs[i][2][...], outs[i][3][...] = _adamw_update(wr[...], g, mr[...], vr[...])
        g = sum_ref[n:n + k, 0:w]
        for j in range(1, N_CHIPS):
            g = jnp.where(chip_ref[0] == j, sum_ref[n:n + k, w * j:w * (j + 1)], g)
        wr, mr, vr = params[n]
        outs[n][0][...] = g
        outs[n][1][...], outs[n][2][...], outs[n][3][...] = _adamw_update(wr[...], g, mr[...], vr[...])
        loss_ref[...] = sum_ref[n + k:n + k + 1, 0:1]

    vmem = pl.BlockSpec(memory_space=pltpu.VMEM)
    operands = [chip, summed]
    for p in zip(list(gains) + [taps], list(gains_m) + [taps_m], list(gains_v) + [taps_v]):
        operands += list(p)
    shapes = [jax.ShapeDtypeStruct(p.shape, F32) for p in list(gains) + [taps] for _ in range(4)]
    out = pl.pallas_call(
        body, name="small_update", out_shape=shapes + [jax.ShapeDtypeStruct((1, 1), F32)],
        in_specs=[pl.BlockSpec(memory_space=pltpu.SMEM)] + [vmem] * (len(operands) - 1),
        out_specs=[vmem] * (len(shapes) + 1),
    )(*operands)
    return [out[4 * i:4 * (i + 1)] for i in range(n + 1)], out[-1]


def _sum_blocks(name, blocks):
    n, rows, cols = blocks.shape

    def body(b_ref, o_ref):
        acc = b_ref[0]
        for k in range(1, n):
            acc = acc + b_ref[k]
        o_ref[...] = acc

    return pl.pallas_call(body, name=name, out_shape=jax.ShapeDtypeStruct((rows, cols), F32))(blocks)


def _place():
    return lax.axis_index("x"), lax.axis_index("y"), lax.axis_index("c")


def _other_chips(x, y):
    return [(1 - x, y), (x, 1 - y), (1 - x, 1 - y)]


def _allgather_finish(name, shards, landed, pass_on):
    n = len(shards)

    def body(*refs):
        ins, outs, stage = refs[:n], refs[2 * n:3 * n], refs[3 * n:4 * n]
        send_sems, recv_sems, local_sems = refs[4 * n:]
        x, y, c = _place()
        chips = _other_chips(x, y)

        def copy(a, k, chip, half):
            place = outs[a].at[2 * chip[0] + chip[1], half]
            return pltpu.make_async_remote_copy(
                src_ref=place, dst_ref=place, send_sem=send_sems.at[3 * a + k], recv_sem=recv_sems.at[3 * a + k],
                device_id=(x, y, 1 - c), device_id_type=MESH)

        load = [pltpu.make_async_copy(ins[a], stage[a], local_sems.at[a]) for a in range(n)]
        local = [pltpu.make_async_copy(stage[a], outs[a].at[2 * x + y], local_sems.at[a]) for a in range(n)]
        for cp in load:
            cp.start()
        passed = [copy(a, k, chip, c) for a in range(n) if pass_on[a] for k, chip in enumerate(chips)]
        for cp in passed:
            cp.start()
        for a in range(n):
            load[a].wait()
            local[a].start()
        for a in range(n):
            if pass_on[a]:
                for k, chip in enumerate(chips):
                    copy(a, k, chip, 1 - c).wait_recv()
        for cp in passed:
            cp.wait_send()
        for cp in local:
            cp.wait()

    any_spec = pl.BlockSpec(memory_space=pl.ANY)
    return pl.pallas_call(
        body, name=name,
        out_shape=[jax.ShapeDtypeStruct((N_CHIPS,) + s.shape, s.dtype) for s in shards],
        in_specs=[any_spec] * (2 * n), out_specs=[any_spec] * n,
        input_output_aliases={n + a: a for a in range(n)},
        scratch_shapes=[pltpu.VMEM(s.shape, s.dtype) for s in shards]
        + [pltpu.SemaphoreType.DMA((3 * n,)), pltpu.SemaphoreType.DMA((3 * n,)), pltpu.SemaphoreType.DMA((n,))],
        compiler_params=pltpu.CompilerParams(vmem_limit_bytes=VMEM_LIMIT_V7X),
    )(*shards, *landed)


def _plan_first_hop(x, y, c, shards, lands):
    return [(shards[a].at[c], lands[a].at[2 * x + y, c], lands[a].at[2 * chip[0] + chip[1], c], (*chip, c))
            for a in range(len(shards)) for chip in _other_chips(x, y)]


def _plan_pass_on(x, y, c, nothing, lands):
    def place(a, chip, half):
        return lands[a].at[2 * chip[0] + chip[1], half]

    return [(place(a, chip, c), place(a, chip, c), place(a, chip, 1 - c), (x, y, 1 - c))
            for a in range(len(lands)) for chip in _other_chips(x, y)]


def _plan_own_half_to_sibling(x, y, c, nothing, lands):
    return [(lands[a].at[c], lands[a].at[c], lands[a].at[1 - c], (x, y, 1 - c)) for a in range(len(lands))]


def _plan_other_half_to_sibling(x, y, c, grads, lands):
    return [(grads[a].at[1 - c], lands[a], lands[a], (x, y, 1 - c)) for a in range(len(grads))]


def _plan_to_other_chips(x, y, c, partials, lands):
    return [(partials[a].at[2 * chip[0] + chip[1]], lands[a].at[k], lands[a].at[k], (*chip, c))
            for a in range(len(partials)) for k, chip in enumerate(_other_chips(x, y))]


def _plan_to_all(x, y, c, blocks, lands):
    flips = [(fx, fy, fc) for fx in (0, 1) for fy in (0, 1) for fc in (0, 1) if (fx, fy, fc) != (0, 0, 0)]
    peers = [(1 - x if fx else x, 1 - y if fy else y, 1 - c if fc else c) for fx, fy, fc in flips]
    return [(blocks[0], lands[0].at[4 * x + 2 * y + c], lands[0].at[4 * p[0] + 2 * p[1] + p[2]], p) for p in peers]


def _planned_copies(plan, srcs, lands, send_sems, recv_sems):
    x, y, c = _place()

    def pair(k, src, there, here, to):
        make = lambda dst: pltpu.make_async_remote_copy(
            src_ref=src, dst_ref=dst, send_sem=send_sems.at[k], recv_sem=recv_sems.at[k], device_id=to, device_id_type=MESH)
        return make(there), make(here)

    return [pair(k, *entry) for k, entry in enumerate(plan(x, y, c, srcs, lands))]


_HBM_SPEC = pl.BlockSpec(memory_space=pltpu.HBM)
_SEM_SPEC = pl.BlockSpec(memory_space=pltpu.SEMAPHORE)


def _hbm(a):
    return pltpu.with_memory_space_constraint(a, pltpu.HBM)


def _exchange_start(name, plan, n_copies, srcs, land_shapes, after, lands=None):
    if lands is None:
        lands = [lax.empty(s.shape, s.dtype) for s in land_shapes]
    land_shapes = lands
    ns, nl = len(srcs), len(land_shapes)
    n_in = ns + nl + 1

    def body(*refs):
        for send, _ in _planned_copies(plan, refs[:ns], refs[ns:ns + nl], refs[n_in], refs[n_in + 1]):
            send.start()
        refs[-1][...] = jnp.zeros_like(refs[-1])

    out = pl.pallas_call(
        body, name=name,
        out_shape=(pltpu.SemaphoreType.DMA((n_copies,)), pltpu.SemaphoreType.DMA((n_copies,)),
                   *[pltpu.HBM(s.shape, s.dtype) for s in land_shapes], jax.ShapeDtypeStruct((8, 128), F32)),
        in_specs=[_HBM_SPEC] * (ns + nl) + [pl.BlockSpec(memory_space=pl.ANY)],
        out_specs=(_SEM_SPEC, _SEM_SPEC, *[_HBM_SPEC] * nl, pl.BlockSpec(memory_space=pltpu.VMEM)),
        input_output_aliases={ns + i: 2 + i for i in range(nl)},
        compiler_params=pltpu.CompilerParams(has_side_effects=pltpu.SideEffectType.DATAFLOW_SIDE_EFFECTING),
    )(*[_hbm(s) for s in srcs], *[_hbm(l) for l in lands], after)
    return out[0], out[1], list(out[2:2 + nl]), out[-1]


def _exchange_wait(name, plan, srcs, started, after):
    send_sems, recv_sems, lands, _ = started
    ns, nl = len(srcs), len(lands)
    after = list(after) if isinstance(after, (list, tuple)) else [after]

    def body(*refs):
        for send, recv in _planned_copies(plan, refs[:ns], refs[ns:ns + nl], refs[ns + nl], refs[ns + nl + 1]):
            send.wait_send()
            recv.wait_recv()

    return pl.pallas_call(
        body, name=name, out_shape=[pltpu.HBM(l.shape, l.dtype) for l in lands],
        in_specs=[_HBM_SPEC] * (ns + nl) + [_SEM_SPEC, _SEM_SPEC] + [pl.BlockSpec(memory_space=pl.ANY)] * len(after),
        out_specs=[_HBM_SPEC] * nl, input_output_aliases={ns + i: i for i in range(nl)},
        compiler_params=pltpu.CompilerParams(has_side_effects=pltpu.SideEffectType.DATAFLOW_SIDE_EFFECTING),
    )(*[_hbm(s) for s in srcs], *lands, send_sems, recv_sems, *after)


def _like(arrays, lead, dtype=None):
    return [jax.ShapeDtypeStruct(tuple(lead) + a.shape[-2:], dtype or a.dtype) for a in arrays]


class _StepExchanges:
    def __init__(self, mats, conv_w):
        x, y, c = _place()
        self.place = jnp.stack([c, 2 * x + y]).astype(jnp.int32)
        shards = [w.astype(BF16).reshape(2, w.shape[0] // 2, w.shape[1]) for w in mats]
        self._in_shard = shards[:1]
        self._in = _exchange_start("w_in_allgather_start", _plan_first_hop, 3, self._in_shard,
                                   _like(self._in_shard, (N_CHIPS, 2)), shards[0])
        self.zero = self._in[3]
        taps = jnp.pad(conv_w, ((0, 8 - conv_w.shape[0]), (0, 128 - conv_w.shape[1])))
        self._rest_shards = shards[1:] + [jnp.stack([taps, jnp.zeros_like(taps)])]
        self._taps_shape = conv_w.shape
        self._groups = {}

    def w_in(self, after):
        landed = _exchange_wait("w_in_allgather_wait", _plan_first_hop, self._in_shard, self._in,
                                list(after) + self._rest_shards)
        (w_in,) = _allgather_finish("w_in_allgather_finish", self._in_shard, landed, [True])
        self._rest = _exchange_start("rest_allgather_start", _plan_first_hop, 3 * len(self._rest_shards),
                                     self._rest_shards, _like(self._rest_shards, (N_CHIPS, 2)), w_in)
        self.zero = self._rest[3]
        return w_in.reshape(N_CHIPS, 2 * w_in.shape[2], w_in.shape[3])

    def rest_weights(self, after):
        landed = _exchange_wait("rest_allgather_wait", _plan_first_hop, self._rest_shards, self._rest, after)
        kv, out, up, down, taps = _allgather_finish("rest_allgather_finish", self._rest_shards, landed,
                                                    [True, True, False, False, True])
        self._up_down = _exchange_start("up_down_pass_on_start", _plan_pass_on, 6, [], None, self.zero, lands=[up, down])
        self.zero = self._up_down[3]
        k, w = self._taps_shape
        taps = taps[:, 0, :k, :w].transpose(1, 0, 2).reshape(k, N_CHIPS * w)
        return [g.reshape(N_CHIPS, 2 * g.shape[2], g.shape[3]) for g in (kv, out)], taps

    def up_down(self, after):
        full = _exchange_wait("up_down_pass_on_wait", _plan_pass_on, [], self._up_down, after)
        return [g.reshape(N_CHIPS, 2 * g.shape[2], g.shape[3]) for g in full]

    def send_grads(self, key, grads):
        grads = list(grads)
        started = _exchange_start(f"{key}_grads_to_sibling_start", _plan_other_half_to_sibling, len(grads), grads,
                                  _like(grads, (N_CHIPS,)), self.zero)
        self._groups[key] = dict(grads=grads, to_sibling=started)
        self.zero = started[3]

    def grads_at_sibling(self, key, after):
        group = self._groups[key]
        grads = group["grads"]
        group["from_sibling"] = _exchange_wait(f"{key}_grads_to_sibling_wait", _plan_other_half_to_sibling, grads,
                                               group["to_sibling"], after)
        group["partials"] = _chip_sums_bf16(f"{key}_chip_sums", grads, group["from_sibling"], self.place)
        group["to_chips"] = _exchange_start(f"{key}_grads_to_chips_start", _plan_to_other_chips, 3 * len(grads),
                                            group["partials"], _like(group["partials"], (3,)), self.zero)
        self.zero = group["to_chips"][3]

    def grads_summed(self, key, after):
        group = self._groups[key]
        from_chips = _exchange_wait(f"{key}_grads_to_chips_wait", _plan_to_other_chips, group["partials"],
                                    group["to_chips"], after)
        return _final_sums(f"{key}_final_sums", group["grads"], group["from_sibling"], from_chips, self.place)

    def send_sums(self, key, sums):
        self._groups[key + "_sums"] = _exchange_start(f"{key}_sums_to_sibling_start", _plan_own_half_to_sibling,
                                                      len(sums), [], None, self.zero, lands=list(sums))
        self.zero = self._groups[key + "_sums"][3]

    def whole_sums(self, key, after):
        full = _exchange_wait(f"{key}_sums_to_sibling_wait", _plan_own_half_to_sibling, [], self._groups[key + "_sums"], after)
        return [t.reshape(2 * t.shape[1], t.shape[2]) for t in full]

    def send_small(self, block):
        self._small = block
        self._small_started = _exchange_start("small_grads_start", _plan_to_all, 7, [block],
                                              [jax.ShapeDtypeStruct((8,) + block.shape, block.dtype)], self.zero)
        self.zero = self._small_started[3]

    def small_summed(self, after):
        x, y, c = _place()
        (landed,) = _exchange_wait("small_grads_wait", _plan_to_all, [self._small], self._small_started, after)
        blocks = lax.dynamic_update_index_in_dim(landed, self._small, 4 * x + 2 * y + c, 0)
        return _sum_blocks("small_sum", blocks)


def _rope_tables(positions):
    half = HEAD // 2
    inv_freq = jnp.float32(ROPE_THETA) ** (-(jnp.arange(half, dtype=F32) * 2.0 / HEAD))
    ang = positions.astype(F32)[:, None] * inv_freq
    cos, sin = jnp.cos(ang), jnp.sin(ang)
    return jnp.tile(cos, (1, 4)), jnp.tile(jnp.concatenate([-sin, sin], axis=1), (1, 2))


def _local_step(x, mem, positions, target, gains, ex):
    g_pre_mix, g_mem, g_a, g_c, g_x, g_post_mix, g_pre_mlp, g_post_mlp = gains
    tm = ROW_TILE
    cos, sin = _rope_tables(positions)
    h = _pre_norm(x, g_pre_mix, ex.zero, tm)
    w_in = ex.w_in([h, cos, sin])

    q, k, v, bcu, qx = _in_proj_fwd(h, w_in, cos, sin, ex.zero, tm)
    ya, lse = _attn_fwd(q, k, v)
    (w_kv, w_out), conv_w = ex.rest_weights(lse)
    w_kv, w_out = (w.reshape(N_CHIPS * w.shape[1], w.shape[2]) for w in (w_kv, w_out))
    memn, mkv = _memkv_fwd(mem, g_mem, w_kv, ex.zero)
    yx, ycat, y2, x1 = _mix_fwd(ya, bcu, qx, mkv, conv_w, g_a, g_c, g_x, w_out, g_post_mix, x, tm)
    w_up, w_down = ex.up_down(x1)
    w_down = w_down.reshape(N_CHIPS * w_down.shape[1], w_down.shape[2])
    h2, f, du, df2, dx1, dg_pre_mlp, dg_post_mlp, loss = _mlp_fwd_bwd(x1, target, g_pre_mlp, g_post_mlp, w_up, w_down,
                                                                      MLP_ROW_TILE)
    gw_down = _weight_grad("grad_w_down", f, df2, True)
    gw_up = _weight_grad("grad_w_up", h2, du, False)
    ex.send_grads("early", [gw_up, gw_down])

    dy2, dya, delta, dycx, dg_post_mix, dg_a, dg_c, dg_x = _mix_bwd(dx1, y2, ya, yx, bcu, conv_w, g_a, g_c, g_x,
                                                                  w_out, g_post_mix, ex.zero, tm)
    ex.grads_at_sibling("early", dy2)
    gw_out = _weight_grad("grad_w_out", ycat, dy2, True)
    tail, dmkv, g_conv = _conv_xattn_bwd(dycx, bcu, qx, mkv, conv_w, ex.zero, tm)
    gw_kv, dg_mem = _memkv_bwd(mem, g_mem, w_kv, dmkv)
    ex.send_grads("mid", [gw_out, gw_kv])
    dqkv = _attn_bwd(q, k, v, dya, lse, delta, ex.zero)
    ex.grads_at_sibling("mid", dqkv[0])
    dproj, grad_x, dg_pre_mix = _in_proj_bwd(dqkv, tail, cos, sin, w_in, x, g_pre_mix, dx1, ex.zero, tm)
    gain_grads = [dg_pre_mix, dg_mem, dg_a, dg_c, dg_x, dg_post_mix, dg_pre_mlp, dg_post_mlp]
    ex.send_small(_pack_small(gain_grads, g_conv, loss))
    gw_in = _weight_grad("grad_w_in", h, dproj, False)
    ex.send_grads("late", [gw_in])
    return grad_x


def _pack_small(gains, conv, scalar=None):
    rows = [jnp.pad(g, ((0, 0), (0, D_MODEL - g.shape[1]))) for g in gains]
    rows.append(jnp.pad(conv, ((0, 0), (0, D_MODEL - conv.shape[1]))))
    last = jnp.zeros((SMALL_ROWS - 8 - conv.shape[0], D_MODEL), F32)
    rows.append(last if scalar is None else last.at[0:1, 0:1].set(scalar))
    return jnp.concatenate(rows, axis=0)


def _unpack_small(block, gain_widths, conv_width):
    gains = [block[i:i + 1, :w] for i, w in enumerate(gain_widths)]
    return gains, block[8:11, :conv_width], block[11, 0]


def kernel(x, mem, positions, g_pre_mix, g_mem, w_in, w_mem_kv, conv_w, g_attn_out, g_conv_out, g_xattn_out, w_out, g_post_mix, g_pre_mlp, w_up, w_down, g_post_mlp, loss_target, m_g_pre_mix, m_g_mem, m_w_in, m_w_mem_kv, m_conv_w, m_g_attn_out, m_g_conv_out, m_g_xattn_out, m_w_out, m_g_post_mix, m_g_pre_mlp, m_w_up, m_w_down, m_g_post_mlp, v_g_pre_mix, v_g_mem, v_w_in, v_w_mem_kv, v_conv_w, v_g_attn_out, v_g_conv_out, v_g_xattn_out, v_w_out, v_g_post_mix, v_g_pre_mlp, v_w_up, v_w_down, v_g_post_mlp):
    cx, cy, cc = _place()
    chip = 2 * cx + cy
    gains = [g_pre_mix, g_mem, g_attn_out, g_conv_out, g_xattn_out, g_post_mix, g_pre_mlp, g_post_mlp]
    gains_m = [m_g_pre_mix, m_g_mem, m_g_attn_out, m_g_conv_out, m_g_xattn_out, m_g_post_mix, m_g_pre_mlp, m_g_post_mlp]
    gains_v = [v_g_pre_mix, v_g_mem, v_g_attn_out, v_g_conv_out, v_g_xattn_out, v_g_post_mix, v_g_pre_mlp, v_g_post_mlp]
    gain_widths = [g.shape[1] for g in gains]
    mats = [w_in[0], w_mem_kv[0], w_out[0], w_up[0], w_down[0]]
    mats_m = [m_w_in[0], m_w_mem_kv[0], m_w_out[0], m_w_up[0], m_w_down[0]]
    mats_v = [v_w_in[0], v_w_mem_kv[0], v_w_out[0], v_w_up[0], v_w_down[0]]

    ex = _StepExchanges(mats, conv_w[0])
    grad_x = _local_step(x[0], mem[0], positions[0], loss_target[0], gains, ex)

    ex.send_sums("four", ex.grads_summed("early", ex.zero) + ex.grads_summed("mid", ex.zero))
    ex.grads_at_sibling("late", ex.zero)
    up_sum, down_sum, out_sum, kv_sum = ex.whole_sums("four", ex.zero)
    params = lambda a, g: (mats[a], g, mats_m[a], mats_v[a])
    new_up, new_down = _adamw("adamw_up_down", [params(3, up_sum), params(4, down_sum)], ex.zero)
    new_out, new_kv = _adamw("adamw_out_kv", [params(2, out_sum), params(1, kv_sum)], ex.zero)

    small, total = _small_update(ex.small_summed(new_kv[1]), chip.reshape(1).astype(jnp.int32), gains, gains_m,
                                 gains_v, conv_w[0], m_conv_w[0], v_conv_w[0])

    ex.send_sums("last", ex.grads_summed("late", small[0][1]))
    (in_sum,) = ex.whole_sums("last", ex.zero)
    (new_in,) = _adamw("adamw_in", [params(0, in_sum)], in_sum)
    mat_new = [new_in, new_kv, new_out, new_up, new_down]

    order = ["g_pre_mix", "g_mem", "w_in", "w_mem_kv", "conv_w", "g_attn_out", "g_conv_out", "g_xattn_out", "w_out",
             "g_post_mix", "g_pre_mlp", "w_up", "w_down", "g_post_mlp"]
    gain_names = ["g_pre_mix", "g_mem", "g_attn_out", "g_conv_out", "g_xattn_out", "g_post_mix", "g_pre_mlp", "g_post_mlp"]
    mat_names = ["w_in", "w_mem_kv", "w_out", "w_up", "w_down"]

    def leaf(kind, name):
        if name in gain_names:
            return small[gain_names.index(name)][kind]
        if name == "conv_w":
            return small[len(gain_names)][kind][None]
        return mat_new[mat_names.index(name)][kind][None]

    return (total[0, 0], grad_x[None], *[leaf(kind, name) for kind in range(4) for name in order])
```

```python
import jax
import jax.numpy as jnp
from jax import lax
from jax.experimental import pallas as pl
from jax.experimental.pallas import tpu as pltpu

F32, BF16 = jnp.float32, jnp.bfloat16

D_MODEL = 1024
ATTN_W = 512
CONV_W = 256
XATTN_W = 256
PROJ_W = 3 * ATTN_W + 3 * CONV_W + XATTN_W
D_FF = 4096
HEAD = 64
N_BACK = 128
DILATIONS = (1, 4, 16)
ROPE_THETA = 10000.0
EPS = 1e-6
NEG_INF = -1e30
SCALE = HEAD ** -0.5
N_CHIPS = 4
SHARD_IN = PROJ_W // N_CHIPS
SHARD_FF = D_FF // N_CHIPS

ADAM_LR, ADAM_B1, ADAM_B2, ADAM_EPS, ADAM_WD, ADAM_STEP = 0.001, 0.9, 0.999, 1e-08, 0.01, 10

VMEM_LIMIT_V7X = 56 * 1024 * 1024
ROW_TILE = 512
MLP_ROW_TILE = 256
ADAMW_ROW_TILE = 256
SMALL_ROWS = 16

NT = (((1,), (1,)), ((), ()))
TN = (((0,), (0,)), ((), ()))
MESH = pl.DeviceIdType.MESH


def _params(*sem):
    return pltpu.CompilerParams(dimension_semantics=sem, vmem_limit_bytes=VMEM_LIMIT_V7X)


def _resident(shape):
    return pl.BlockSpec(shape, lambda *_: (0,) * len(shape), pipeline_mode=pl.Buffered(1))


def _rows(tm, width):
    return pl.BlockSpec((tm, width), lambda i: (i, 0))


def _rms_hat(x):
    r = lax.rsqrt(jnp.mean(x * x, axis=-1, keepdims=True) + EPS)
    return x * r, r


def _rms_bwd(xhat, r, g, dy):
    gdy = dy * g
    return r * (gdy - xhat * jnp.mean(xhat * gdy, axis=-1, keepdims=True))


def _rope128(t, cos, sin_signed, inverse):
    lane = lax.broadcasted_iota(jnp.int32, t.shape, 1)
    first_half = (lane % HEAD) < (HEAD // 2)
    rot = jnp.where(first_half, pltpu.roll(t, 128 - HEAD // 2, 1), pltpu.roll(t, HEAD // 2, 1))
    return t * cos - rot * sin_signed if inverse else t * cos + rot * sin_signed


def _pre_norm(x, g, after, tm):
    S = x.shape[0]

    def body(x_ref, g_ref, after_ref, h_ref):
        h_ref[...] = (_rms_hat(x_ref[...])[0] * g_ref[...]).astype(BF16)

    return pl.pallas_call(
        body, name="pre_norm", grid=(S // tm,),
        in_specs=[_rows(tm, D_MODEL), _resident((1, D_MODEL)), pl.BlockSpec(memory_space=pl.ANY)],
        out_specs=_rows(tm, D_MODEL), out_shape=jax.ShapeDtypeStruct((S, D_MODEL), BF16),
        compiler_params=_params("parallel"),
    )(x, g, after)


def _side_by_side(w_hbm, w_full, sems):
    @pl.when(pl.program_id(0) == 0)
    def _():
        copies = [pltpu.make_async_copy(w_hbm.at[j], w_full.at[:, pl.ds(SHARD_IN * j, SHARD_IN)], sems.at[j])
                  for j in range(N_CHIPS)]
        for cp in copies:
            cp.start()
        for cp in copies:
            cp.wait()


def _in_proj_fwd(h, w_in, cos, sin, after, tm):
    S = h.shape[0]

    def body(h_ref, w_hbm, cos_ref, sin_ref, after_ref, q_ref, k_ref, v_ref, bcu_ref, qx_ref, proj, w_full, sems):
        _side_by_side(w_hbm, w_full, sems)
        proj[...] = jnp.dot(h_ref[...], w_full[...], preferred_element_type=F32)
        c, s = cos_ref[...], sin_ref[...]
        for j in range(ATTN_W // 128):
            lo = 128 * j
            q_ref[:, lo:lo + 128] = _rope128(proj[:, lo:lo + 128], c, s, False) * SCALE
            k_ref[:, lo:lo + 128] = _rope128(proj[:, ATTN_W + lo:ATTN_W + lo + 128], c, s, False)
        v_ref[...] = proj[:, 2 * ATTN_W:3 * ATTN_W]
        bcu_ref[...] = proj[:, 3 * ATTN_W:3 * ATTN_W + 3 * CONV_W]
        qx_ref[...] = proj[:, 3 * ATTN_W + 3 * CONV_W:PROJ_W].astype(BF16)

    return pl.pallas_call(
        body, name="in_proj_fwd", grid=(S // tm,),
        in_specs=[_rows(tm, D_MODEL), pl.BlockSpec(memory_space=pl.ANY), _rows(tm, 128), _rows(tm, 128),
                  pl.BlockSpec(memory_space=pl.ANY)],
        out_specs=[_rows(tm, ATTN_W), _rows(tm, ATTN_W), _rows(tm, ATTN_W), _rows(tm, 3 * CONV_W), _rows(tm, XATTN_W)],
        out_shape=[jax.ShapeDtypeStruct((S, ATTN_W), F32), jax.ShapeDtypeStruct((S, ATTN_W), F32),
                   jax.ShapeDtypeStruct((S, ATTN_W), F32), jax.ShapeDtypeStruct((S, 3 * CONV_W), F32),
                   jax.ShapeDtypeStruct((S, XATTN_W), BF16)],
        scratch_shapes=[pltpu.VMEM((tm, PROJ_W), F32), pltpu.VMEM((D_MODEL, PROJ_W), BF16),
                        pltpu.SemaphoreType.DMA((N_CHIPS,))],
        compiler_params=_params("arbitrary"),
    )(h, w_in, cos, sin, after)


def _memkv_fwd(mem, g_mem, w_kv, after):
    n_mem = mem.shape[0]

    def body(mem_ref, g_ref, w_ref, after_ref, mn_ref, kv_ref):
        mhat, _ = _rms_hat(mem_ref[...])
        mn = (mhat * g_ref[...]).astype(BF16)
        mn_ref[...] = mn
        kv_ref[...] = jnp.dot(mn, w_ref[...], preferred_element_type=F32).astype(BF16)

    vmem = pl.BlockSpec(memory_space=pltpu.VMEM)
    return pl.pallas_call(
        body, name="memkv_fwd", in_specs=[vmem, vmem, vmem, pl.BlockSpec(memory_space=pl.ANY)], out_specs=[vmem, vmem],
        out_shape=[jax.ShapeDtypeStruct((n_mem, D_MODEL), BF16), jax.ShapeDtypeStruct((n_mem, 2 * XATTN_W), BF16)],
        compiler_params=pltpu.CompilerParams(vmem_limit_bytes=VMEM_LIMIT_V7X),
    )(mem, g_mem, w_kv, after)


def _fill_band_bias(bias):
    row = lax.broadcasted_iota(jnp.int32, (N_BACK, 2 * N_BACK), 0)
    col = lax.broadcasted_iota(jnp.int32, (N_BACK, 2 * N_BACK), 1)
    band = (col >= row) & (col <= row + N_BACK)
    bias[1] = jnp.where(band, 0.0, NEG_INF)
    bias[0] = jnp.where(band & (col >= N_BACK), 0.0, NEG_INF)


def _strided(start, size, d):
    return pl.ds(start, size) if d == 1 else pl.ds(start, size, stride=d)


def _group_starts(g, G, nb, d):
    t0 = g * G
    r, n0 = lax.shift_right_logical(t0, nb.bit_length() - 1), lax.bitwise_and(t0, nb - 1)
    first = r + n0 * (N_BACK * d)
    before = r + jnp.maximum(n0 - 1, 0) * (N_BACK * d)
    starts = [before] + [first + u * (N_BACK * d) for u in range(G)]
    if d == 1:
        starts = [pl.multiple_of(st, N_BACK) for st in starts]
    return starts, n0


def _step_blocks(i, U, nb, d):
    G = min(U, nb)
    whole = G == nb
    row_blocks, blocks = [], []
    for grp in range(U // G):
        starts, n0 = _group_starts(i * (U // G) + grp, G, nb, d)
        base = len(row_blocks)
        if whole:
            row_blocks += [_strided(st, N_BACK, d) for st in starts[1:]]
            blocks += [(base + max(u - 1, 0), base + u, min(u, 1)) for u in range(G)]
        else:
            row_blocks += [_strided(st, N_BACK, d) for st in starts]
            blocks += [(base + u, base + u + 1, jnp.minimum(n0, 1) if u == 0 else 1) for u in range(G)]
    return row_blocks, blocks


def _by_head(a, b):
    lane = lax.broadcasted_iota(jnp.int32, (a.shape[0], 2 * HEAD), 1)
    return jnp.where(lane < HEAD, a, b)


def _head_only(t, hh):
    lane = lax.broadcasted_iota(jnp.int32, t.shape, 1)
    return jnp.where((lane < HEAD) == (hh == 0), t, jnp.zeros_like(t))


def _stack_heads(t):
    return jnp.concatenate([_head_only(t, 0), _head_only(t, 1)], axis=0)


def _head_columns(t):
    return jnp.concatenate([t[:, 0:1], t[:, HEAD:HEAD + 1]], axis=0)


def _unstack(t):
    return _by_head(t[:N_BACK], t[N_BACK:])


def _unstack_columns(t):
    return _by_head(jnp.broadcast_to(t[:N_BACK], (N_BACK, 2 * HEAD)), jnp.broadcast_to(t[N_BACK:], (N_BACK, 2 * HEAD)))


FWD_BLOCKS_PER_STEP = 4
BWD_BLOCKS_PER_STEP = 4


def _attn_fwd(q, k, v):
    S = q.shape[0]
    U = FWD_BLOCKS_PER_STEP

    def body(q_ref, k_ref, v_ref, y_ref, m_ref, l_scr, bias):
        _fill_band_bias(bias)
        for g, d in enumerate(DILATIONS):
            nb = S // d // N_BACK
            first_pattern, last_pattern = g == 0, g == len(DILATIONS) - 1

            def step(i, carry, d=d, nb=nb, first_pattern=first_pattern, last_pattern=last_pattern):
                row_blocks, blocks = _step_blocks(i, U, nb, d)
                kb = [k_ref[r, :].astype(BF16) for r in row_blocks]
                ss = []
                for before, own, which in blocks:
                    kw = jnp.concatenate([kb[before], kb[own]], 0)
                    qs = _stack_heads(q_ref[row_blocks[own], :].astype(BF16))
                    b = bias[which]
                    ss.append(lax.dot_general(qs, kw, NT, preferred_element_type=F32) + jnp.concatenate([b, b], axis=0))
                ms = [jnp.max(s, axis=1, keepdims=True) for s in ss]
                ps = [jnp.exp(s - m) for s, m in zip(ss, ms)]
                ls = [jnp.sum(p, axis=1, keepdims=True) for p in ps]
                vb = [v_ref[r, :].astype(BF16) for r in row_blocks]
                os_ = [jnp.dot(ps[u].astype(BF16), jnp.concatenate([vb[before], vb[own]], 0), preferred_element_type=F32)
                       for u, (before, own, _) in enumerate(blocks)]
                for u, (_, own, _) in enumerate(blocks):
                    o_g, m_g, l_g = _unstack(os_[u]), _unstack_columns(ms[u]), _unstack_columns(ls[u])
                    r = row_blocks[own]
                    if first_pattern:
                        m_new, l_new, acc = m_g, l_g, o_g
                    else:
                        m_old = m_ref[r, :]
                        m_new = jnp.maximum(m_old, m_g)
                        alpha, beta = jnp.exp(m_old - m_new), jnp.exp(m_g - m_new)
                        l_new = l_scr[r, :] * alpha + l_g * beta
                        acc = y_ref[r, :] * alpha + o_g * beta
                    if last_pattern:
                        y_ref[r, :] = acc / l_new
                        m_ref[r, :] = m_new + jnp.log(l_new)
                    else:
                        y_ref[r, :] = acc
                        m_ref[r, :] = m_new
                        l_scr[r, :] = l_new
                return carry

            lax.fori_loop(0, d * nb // U, step, 0)

    col = pl.BlockSpec((S, 2 * HEAD), lambda j: (0, j))
    return pl.pallas_call(
        body, name="attn_fwd", grid=(q.shape[1] // (2 * HEAD),),
        in_specs=[col, col, col], out_specs=[col, col],
        out_shape=[jax.ShapeDtypeStruct(q.shape, F32)] * 2,
        scratch_shapes=[pltpu.VMEM((S, 2 * HEAD), F32), pltpu.VMEM((2, N_BACK, 2 * N_BACK), F32)],
        compiler_params=_params("parallel"),
    )(q, k, v)


def _attn_bwd(q, k, v, dy, lse, delta, after):
    S = q.shape[0]
    U = BWD_BLOCKS_PER_STEP

    def body(q_ref, k_ref, v_ref, dy_ref, lse_ref, delta_ref, after_ref, dq_ref, dk_ref, dv_ref, bias):
        _fill_band_bias(bias)
        dk_ref[...] = jnp.zeros_like(dk_ref)
        dv_ref[...] = jnp.zeros_like(dv_ref)
        for g, d in enumerate(DILATIONS):
            nb = S // d // N_BACK

            def step(i, carry, d=d, nb=nb, g=g):
                row_blocks, blocks = _step_blocks(i, U, nb, d)
                kb = [k_ref[r, :].astype(BF16) for r in row_blocks]
                vb = [v_ref[r, :].astype(BF16) for r in row_blocks]
                kws = [jnp.concatenate([kb[before], kb[own]], 0) for before, own, _ in blocks]
                vws = [jnp.concatenate([vb[before], vb[own]], 0) for before, own, _ in blocks]
                qss = [_stack_heads(q_ref[row_blocks[own], :].astype(BF16)) for _, own, _ in blocks]
                doss = [_stack_heads(dy_ref[row_blocks[own], :].astype(BF16)) for _, own, _ in blocks]
                ss, dps = [], []
                for u, (_, _, which) in enumerate(blocks):
                    b = bias[which]
                    ss.append(lax.dot_general(qss[u], kws[u], NT, preferred_element_type=F32) + jnp.concatenate([b, b], axis=0))
                    dps.append(lax.dot_general(doss[u], vws[u], NT, preferred_element_type=F32))
                ps = [jnp.exp(ss[u] - _head_columns(lse_ref[row_blocks[own], :])) for u, (_, own, _) in enumerate(blocks)]
                dss = [(ps[u] * (dps[u] - _head_columns(delta_ref[row_blocks[own], :]))).astype(BF16)
                       for u, (_, own, _) in enumerate(blocks)]
                pbs = [p.astype(BF16) for p in ps]
                dqs = [jnp.dot(dss[u], kws[u], preferred_element_type=F32) for u in range(U)]
                dkws = [lax.dot_general(dss[u], qss[u], TN, preferred_element_type=F32) for u in range(U)]
                dvws = [lax.dot_general(pbs[u], doss[u], TN, preferred_element_type=F32) for u in range(U)]
                dk_parts, dv_parts = [None] * len(row_blocks), [None] * len(row_blocks)
                for u, (before, own, _) in enumerate(blocks):
                    dq = _unstack(dqs[u])
                    if g == 0:
                        dq_ref[row_blocks[own], :] = dq
                    else:
                        dq_ref[row_blocks[own], :] += dq
                    for idx, dkp, dvp in ((before, dkws[u][:N_BACK], dvws[u][:N_BACK]),
                                          (own, dkws[u][N_BACK:], dvws[u][N_BACK:])):
                        dk_parts[idx] = dkp if dk_parts[idx] is None else dk_parts[idx] + dkp
                        dv_parts[idx] = dvp if dv_parts[idx] is None else dv_parts[idx] + dvp
                for idx, r in enumerate(row_blocks):
                    dk_ref[r, :] += dk_parts[idx]
                    dv_ref[r, :] += dv_parts[idx]
                return carry

            lax.fori_loop(0, d * nb // U, step, 0)

    col = pl.BlockSpec((S, 2 * HEAD), lambda j: (0, j))
    return pl.pallas_call(
        body, name="attn_bwd", grid=(q.shape[1] // (2 * HEAD),),
        in_specs=[col] * 6 + [pl.BlockSpec(memory_space=pl.ANY)], out_specs=[col] * 3,
        out_shape=[jax.ShapeDtypeStruct(q.shape, F32)] * 3,
        scratch_shapes=[pltpu.VMEM((2, N_BACK, 2 * N_BACK), F32)],
        compiler_params=_params("parallel"),
    )(q, k, v, dy, lse, delta, after)


def _shift_down(z, before, k):
    row = lax.broadcasted_iota(jnp.int32, z.shape, 0)
    out = pltpu.roll(z, k, 0)
    for i in range(k):
        out = jnp.where(row == i, before[8 - k + i:8 - k + i + 1, :], out)
    return out


def _shift_up(z, after, k):
    rows = z.shape[0]
    row = lax.broadcasted_iota(jnp.int32, z.shape, 0)
    out = pltpu.roll(z, rows - k, 0)
    for i in range(k):
        out = jnp.where(row == rows - k + i, after[i:i + 1, :], out)
    return out


def _conv_fwd(bcu, before, is_first, w):
    b, c, u = bcu[:, 0:CONV_W], bcu[:, CONV_W:2 * CONV_W], bcu[:, 2 * CONV_W:3 * CONV_W]
    z = c * u
    zb = jnp.where(is_first, 0.0, before[:, CONV_W:2 * CONV_W] * before[:, 2 * CONV_W:3 * CONV_W])
    z1, z2 = _shift_down(z, zb, 1), _shift_down(z, zb, 2)
    cv = w[0:1, :] * z2 + w[1:2, :] * z1 + w[2:3, :] * z
    return b, c, u, z, z1, z2, cv


def _halo_before(tm, width):
    return pl.BlockSpec((8, width), lambda i: (jnp.maximum(i * (tm // 8) - 1, 0), 0))


def _halo_after(tm, width, S):
    return pl.BlockSpec((8, width), lambda i: (jnp.minimum((i + 1) * (tm // 8), S // 8 - 1), 0))


def _mix_fwd(ya, bcu, qx, mkv, conv_w, g_a, g_c, g_x, w_out, g_post, x, tm):
    S = x.shape[0]

    def body(ya_ref, bcu_ref, before_ref, qx_ref, mkv_ref, cw_ref, ga_ref, gc_ref, gx_ref,
             wo_ref, gp_ref, x_ref, yx_ref, ycat_ref, y2_ref, x1_ref):
        ya = ya_ref[...]
        b, _, _, _, _, _, cv = _conv_fwd(bcu_ref[...], before_ref[...], pl.program_id(0) == 0, cw_ref[...])
        yc = b * cv

        qxb, mkvb = qx_ref[...], mkv_ref[...]
        for hd in range(XATTN_W // HEAD):
            sl = slice(HEAD * hd, HEAD * (hd + 1))
            s = lax.dot_general(qxb[:, sl], mkvb[:, sl], NT, preferred_element_type=F32) * SCALE
            mx = jnp.max(s, axis=1, keepdims=True)
            p = jnp.exp(s - mx)
            l = jnp.sum(p, axis=1, keepdims=True)
            vm = mkvb[:, XATTN_W + HEAD * hd:XATTN_W + HEAD * (hd + 1)]
            yx_ref[:, sl] = jnp.dot(p.astype(BF16), vm, preferred_element_type=F32) / l
        yx = yx_ref[...]

        ycat_ref[:, 0:ATTN_W] = (_rms_hat(ya)[0] * ga_ref[...]).astype(BF16)
        ycat_ref[:, ATTN_W:ATTN_W + CONV_W] = (_rms_hat(yc)[0] * gc_ref[...]).astype(BF16)
        ycat_ref[:, ATTN_W + CONV_W:D_MODEL] = (_rms_hat(yx)[0] * gx_ref[...]).astype(BF16)
        y2 = jnp.dot(ycat_ref[...], wo_ref[...], preferred_element_type=F32)
        y2_ref[...] = y2
        x1_ref[...] = x_ref[...] + _rms_hat(y2)[0] * gp_ref[...]

    n_mem = mkv.shape[0]
    return pl.pallas_call(
        body, name="mix_fwd", grid=(S // tm,),
        in_specs=[_rows(tm, ATTN_W), _rows(tm, 3 * CONV_W), _halo_before(tm, 3 * CONV_W), _rows(tm, XATTN_W),
                  _resident((n_mem, 2 * XATTN_W)), _resident((3, CONV_W)), _resident((1, ATTN_W)),
                  _resident((1, CONV_W)), _resident((1, XATTN_W)), _resident((D_MODEL, D_MODEL)),
                  _resident((1, D_MODEL)), _rows(tm, D_MODEL)],
        out_specs=[_rows(tm, XATTN_W), _rows(tm, D_MODEL), _rows(tm, D_MODEL), _rows(tm, D_MODEL)],
        out_shape=[jax.ShapeDtypeStruct((S, XATTN_W), F32), jax.ShapeDtypeStruct((S, D_MODEL), BF16),
                   jax.ShapeDtypeStruct((S, D_MODEL), F32), jax.ShapeDtypeStruct((S, D_MODEL), F32)],
        compiler_params=_params("parallel"),
    )(ya, bcu, bcu, qx, mkv, conv_w, g_a, g_c, g_x, w_out, g_post, x)


def _mlp_fwd_bwd(x1, target, g_pre, g_post, w_up, w_down, tm):
    S = x1.shape[0]
    n_ff = D_FF // SHARD_FF

    def body(x1_ref, t_ref, gpre_ref, gpost_ref, wup_ref, wdn_ref,
             h2_ref, f_ref, du_ref, df2_ref, dx1_ref, dgpre_ref, dgpost_ref, loss_ref, u_scr):
        @pl.when(pl.program_id(0) == 0)
        def _():
            dgpre_ref[...] = jnp.zeros_like(dgpre_ref)
            dgpost_ref[...] = jnp.zeros_like(dgpost_ref)
            loss_ref[...] = jnp.zeros_like(loss_ref)

        x1 = x1_ref[...]
        x1hat, r1 = _rms_hat(x1)
        h2 = (x1hat * gpre_ref[...]).astype(BF16)
        h2_ref[...] = h2
        f2 = jnp.zeros((tm, D_MODEL), F32)
        for j in range(n_ff):
            cols = slice(SHARD_FF * j, SHARD_FF * (j + 1))
            u = jnp.maximum(jnp.dot(h2, wup_ref[j], preferred_element_type=F32), 0.0)
            u_scr[:, cols] = u
            f = (u * u).astype(BF16)
            f_ref[:, cols] = f
            f2 = f2 + jnp.dot(f, wdn_ref[cols, :], preferred_element_type=F32)
        f2hat, r2 = _rms_hat(f2)
        err = x1 + f2hat * gpost_ref[...] - t_ref[...]
        loss_ref[...] += 0.5 * jnp.sum(jnp.mean(err * err, axis=-1, keepdims=True), axis=0, keepdims=True)
        dx2 = err * (1.0 / D_MODEL)
        dgpost_ref[...] += jnp.sum(dx2 * f2hat, axis=0, keepdims=True)
        df2 = _rms_bwd(f2hat, r2, gpost_ref[...], dx2).astype(BF16)
        df2_ref[...] = df2
        dh2 = jnp.zeros((tm, D_MODEL), F32)
        for j in range(n_ff):
            cols = slice(SHARD_FF * j, SHARD_FF * (j + 1))
            df = lax.dot_general(df2, wdn_ref[cols, :], NT, preferred_element_type=F32)
            du = (2.0 * u_scr[:, cols] * df).astype(BF16)
            du_ref[:, cols] = du
            dh2 = dh2 + lax.dot_general(du, wup_ref[j], NT, preferred_element_type=F32)
        dgpre_ref[...] += jnp.sum(dh2 * x1hat, axis=0, keepdims=True)
        dx1_ref[...] = dx2 + _rms_bwd(x1hat, r1, gpre_ref[...], dh2)

    acc = pl.BlockSpec((1, D_MODEL), lambda i: (0, 0))
    return pl.pallas_call(
        body, name="mlp_fwd_bwd", grid=(S // tm,),
        in_specs=[_rows(tm, D_MODEL), _rows(tm, D_MODEL), _resident((1, D_MODEL)), _resident((1, D_MODEL)),
                  _resident((n_ff, D_MODEL, SHARD_FF)), _resident((D_FF, D_MODEL))],
        out_specs=[_rows(tm, D_MODEL), _rows(tm, D_FF), _rows(tm, D_FF), _rows(tm, D_MODEL), _rows(tm, D_MODEL),
                   acc, acc, pl.BlockSpec((1, 1), lambda i: (0, 0))],
        out_shape=[jax.ShapeDtypeStruct((S, D_MODEL), BF16), jax.ShapeDtypeStruct((S, D_FF), BF16),
                   jax.ShapeDtypeStruct((S, D_FF), BF16), jax.ShapeDtypeStruct((S, D_MODEL), BF16),
                   jax.ShapeDtypeStruct((S, D_MODEL), F32), jax.ShapeDtypeStruct((1, D_MODEL), F32),
                   jax.ShapeDtypeStruct((1, D_MODEL), F32), jax.ShapeDtypeStruct((1, 1), F32)],
        scratch_shapes=[pltpu.VMEM((tm, D_FF), F32)],
        compiler_params=_params("arbitrary"),
    )(x1, target, g_pre, g_post, w_up, w_down)


def _weight_grad(name, a, b, rows_sharded):
    S, K = a.shape
    N = b.shape[1]
    if rows_sharded:
        tk, tn = K // N_CHIPS, N
        a_spec = pl.BlockSpec((S, tk), lambda j: (0, j))
        b_spec = pl.BlockSpec((S, tn), lambda j: (0, 0), pipeline_mode=pl.Buffered(1))
    else:
        tk, tn = K, N // N_CHIPS
        a_spec = pl.BlockSpec((S, tk), lambda j: (0, 0), pipeline_mode=pl.Buffered(1))
        b_spec = pl.BlockSpec((S, tn), lambda j: (0, j))
    half = tk // 2

    def body(a_ref, b_ref, o_ref):
        res = lax.dot_general(a_ref[...], b_ref[...], TN, preferred_element_type=F32)
        o_ref[0, 0] = res[:half]
        o_ref[1, 0] = res[half:]

    return pl.pallas_call(
        body, name=name, grid=(N_CHIPS,), in_specs=[a_spec, b_spec],
        out_specs=pl.BlockSpec((2, 1, half, tn), lambda j: (0, j, 0, 0)),
        out_shape=jax.ShapeDtypeStruct((2, N_CHIPS, half, tn), F32),
        compiler_params=_params("parallel"),
    )(a, b)


def _weight_grad_w_in(h, dproj):
    S, K = h.shape
    step_w = 2 * 256
    n_steps = PROJ_W // step_w
    half = K // 2

    def body(a_ref, b_ref, o_ref):
        res = lax.dot_general(a_ref[...], b_ref[...], TN, preferred_element_type=F32)
        for step in range(n_steps):
            @pl.when(pl.program_id(0) == step)
            def _(step=step):
                lo = step * step_w
                while lo < (step + 1) * step_w:
                    chip = lo // SHARD_IN
                    hi = min((step + 1) * step_w, (chip + 1) * SHARD_IN)
                    for hh in range(2):
                        o_ref[hh, chip, :, lo - chip * SHARD_IN:hi - chip * SHARD_IN] = (
                            res[half * hh:half * (hh + 1), lo - step * step_w:hi - step * step_w])
                    lo = hi

    return pl.pallas_call(
        body, name="grad_w_in", grid=(n_steps,),
        in_specs=[pl.BlockSpec((S, K), lambda j: (0, 0), pipeline_mode=pl.Buffered(1)),
                  pl.BlockSpec((S, step_w), lambda j: (0, j))],
        out_specs=pl.BlockSpec((2, N_CHIPS, half, SHARD_IN), lambda j: (0, 0, 0, 0)),
        out_shape=jax.ShapeDtypeStruct((2, N_CHIPS, half, SHARD_IN), F32),
        compiler_params=_params("arbitrary"),
    )(h, dproj)


def _mix_bwd(dx1, y2, ya, yx, bcu, conv_w, g_a, g_c, g_x, w_out, g_post, after, tm):
    S = dx1.shape[0]

    def body(dx1_ref, y2_ref, ya_ref, yx_ref, bcu_ref, before_ref, cw_ref, ga_ref, gc_ref, gx_ref, wo_ref, gp_ref,
             after_ref, dy2_ref, dya_ref, delta_ref, dycx_ref, dgp_ref, dga_ref, dgc_ref, dgx_ref):
        @pl.when(pl.program_id(0) == 0)
        def _():
            for ref in (dgp_ref, dga_ref, dgc_ref, dgx_ref):
                ref[...] = jnp.zeros_like(ref)

        dx1 = dx1_ref[...]
        y2hat, r2 = _rms_hat(y2_ref[...])
        dgp_ref[...] += jnp.sum(dx1 * y2hat, axis=0, keepdims=True)
        dy2 = _rms_bwd(y2hat, r2, gp_ref[...], dx1).astype(BF16)
        dy2_ref[...] = dy2
        dycat = lax.dot_general(dy2, wo_ref[...], NT, preferred_element_type=F32)

        d_na = dycat[:, 0:ATTN_W]
        ya = ya_ref[...]
        yahat, ra = _rms_hat(ya)
        dga_ref[...] += jnp.sum(d_na * yahat, axis=0, keepdims=True)
        dya = _rms_bwd(yahat, ra, ga_ref[...], d_na)
        dya_ref[...] = dya
        prod = dya * ya
        hi = prod.astype(BF16)
        lo = (prod - hi.astype(F32)).astype(BF16)
        head_of = lambda axis: lax.shift_right_logical(lax.broadcasted_iota(jnp.int32, (ATTN_W, ATTN_W), axis),
                                                       HEAD.bit_length() - 1)
        same_head = head_of(0) == head_of(1)
        ones = jnp.where(same_head, 1.0, 0.0).astype(BF16)
        delta_ref[...] = (jnp.dot(hi, ones, preferred_element_type=F32) + jnp.dot(lo, ones, preferred_element_type=F32))

        b, _, _, _, _, _, cv = _conv_fwd(bcu_ref[...], before_ref[...], pl.program_id(0) == 0, cw_ref[...])
        d_nc = dycat[:, ATTN_W:ATTN_W + CONV_W]
        ychat, rc = _rms_hat(b * cv)
        dgc_ref[...] += jnp.sum(d_nc * ychat, axis=0, keepdims=True)
        dycx_ref[:, 0:CONV_W] = _rms_bwd(ychat, rc, gc_ref[...], d_nc)

        d_nx = dycat[:, ATTN_W + CONV_W:D_MODEL]
        yxhat, rx = _rms_hat(yx_ref[...])
        dgx_ref[...] += jnp.sum(d_nx * yxhat, axis=0, keepdims=True)
        dycx_ref[:, CONV_W:CONV_W + XATTN_W] = _rms_bwd(yxhat, rx, gx_ref[...], d_nx)

    acc = lambda w: pl.BlockSpec((1, w), lambda i: (0, 0))
    return pl.pallas_call(
        body, name="mix_bwd", grid=(S // tm,),
        in_specs=[_rows(tm, D_MODEL), _rows(tm, D_MODEL), _rows(tm, ATTN_W), _rows(tm, XATTN_W),
                  _rows(tm, 3 * CONV_W), _halo_before(tm, 3 * CONV_W), _resident((3, CONV_W)),
                  _resident((1, ATTN_W)), _resident((1, CONV_W)), _resident((1, XATTN_W)),
                  _resident((D_MODEL, D_MODEL)), _resident((1, D_MODEL)), pl.BlockSpec(memory_space=pl.ANY)],
        out_specs=[_rows(tm, D_MODEL), _rows(tm, ATTN_W), _rows(tm, ATTN_W), _rows(tm, CONV_W + XATTN_W),
                   acc(D_MODEL), acc(ATTN_W), acc(CONV_W), acc(XATTN_W)],
        out_shape=[jax.ShapeDtypeStruct((S, D_MODEL), BF16), jax.ShapeDtypeStruct((S, ATTN_W), F32),
                   jax.ShapeDtypeStruct((S, ATTN_W), F32),
                   jax.ShapeDtypeStruct((S, CONV_W + XATTN_W), F32), jax.ShapeDtypeStruct((1, D_MODEL), F32),
                   jax.ShapeDtypeStruct((1, ATTN_W), F32), jax.ShapeDtypeStruct((1, CONV_W), F32),
                   jax.ShapeDtypeStruct((1, XATTN_W), F32)],
        compiler_params=_params("arbitrary"),
    )(dx1, y2, ya, yx, bcu, bcu, conv_w, g_a, g_c, g_x, w_out, g_post, after)


def _conv_xattn_bwd(dycx, bcu, qx, mkv, conv_w, behind, tm):
    S = dycx.shape[0]
    n_mem = mkv.shape[0]
    n_tiles = S // tm

    def body(d_ref, dafter_ref, bcu_ref, before_ref, after_ref, qx_ref, mkv_ref, cw_ref, behind_ref,
             tail_ref, dmkv_ref, dcw_ref):
        i = pl.program_id(0)

        @pl.when(i == 0)
        def _():
            dmkv_ref[...] = jnp.zeros_like(dmkv_ref)
            dcw_ref[...] = jnp.zeros_like(dcw_ref)

        w = cw_ref[...]
        b, c, u, z, z1, z2, cv = _conv_fwd(bcu_ref[...], before_ref[...], i == 0, w)
        dyc = d_ref[:, 0:CONV_W]
        dcv = dyc * b
        dcv_after = jnp.where(i == n_tiles - 1, 0.0, dafter_ref[:, 0:CONV_W] * after_ref[:, 0:CONV_W])
        dz = w[2:3, :] * dcv + w[1:2, :] * _shift_up(dcv, dcv_after, 1) + w[0:1, :] * _shift_up(dcv, dcv_after, 2)
        dcw_ref[0:1, :] += jnp.sum(dcv * z2, axis=0, keepdims=True)
        dcw_ref[1:2, :] += jnp.sum(dcv * z1, axis=0, keepdims=True)
        dcw_ref[2:3, :] += jnp.sum(dcv * z, axis=0, keepdims=True)
        tail_ref[:, 0:CONV_W] = (dyc * cv).astype(BF16)
        tail_ref[:, CONV_W:2 * CONV_W] = (dz * u).astype(BF16)
        tail_ref[:, 2 * CONV_W:3 * CONV_W] = (dz * c).astype(BF16)

        qxb, mkvb = qx_ref[...], mkv_ref[...]
        for hd in range(XATTN_W // HEAD):
            sl = slice(HEAD * hd, HEAD * (hd + 1))
            vsl = slice(XATTN_W + HEAD * hd, XATTN_W + HEAD * (hd + 1))
            s = lax.dot_general(qxb[:, sl], mkvb[:, sl], NT, preferred_element_type=F32) * SCALE
            e = jnp.exp(s - jnp.max(s, axis=1, keepdims=True))
            p = e / jnp.sum(e, axis=1, keepdims=True)
            dob = d_ref[:, CONV_W + HEAD * hd:CONV_W + HEAD * (hd + 1)].astype(BF16)
            dp = lax.dot_general(dob, mkvb[:, vsl], NT, preferred_element_type=F32)
            ds = (p * (dp - jnp.sum(p * dp, axis=1, keepdims=True)) * SCALE).astype(BF16)
            tail_ref[:, 3 * CONV_W + HEAD * hd:3 * CONV_W + HEAD * (hd + 1)] = jnp.dot(
                ds, mkvb[:, sl], preferred_element_type=F32).astype(BF16)
            dmkv_ref[:, sl] += lax.dot_general(ds, qxb[:, sl], TN, preferred_element_type=F32)
            dmkv_ref[:, vsl] += lax.dot_general(p.astype(BF16), dob, TN, preferred_element_type=F32)

    width = CONV_W + XATTN_W
    return pl.pallas_call(
        body, name="conv_xattn_bwd", grid=(n_tiles,),
        in_specs=[_rows(tm, width), _halo_after(tm, width, S), _rows(tm, 3 * CONV_W), _halo_before(tm, 3 * CONV_W),
                  _halo_after(tm, 3 * CONV_W, S), _rows(tm, XATTN_W), _resident((n_mem, 2 * XATTN_W)),
                  _resident((3, CONV_W)), pl.BlockSpec(memory_space=pl.ANY)],
        out_specs=[_rows(tm, 3 * CONV_W + XATTN_W), pl.BlockSpec((n_mem, 2 * XATTN_W), lambda i: (0, 0)),
                   pl.BlockSpec((3, CONV_W), lambda i: (0, 0))],
        out_shape=[jax.ShapeDtypeStruct((S, 3 * CONV_W + XATTN_W), BF16),
                   jax.ShapeDtypeStruct((n_mem, 2 * XATTN_W), F32), jax.ShapeDtypeStruct((3, CONV_W), F32)],
        compiler_params=_params("arbitrary"),
    )(dycx, dycx, bcu, bcu, bcu, qx, mkv, conv_w, behind)


def _memkv_bwd(mem, g_mem, w_kv, dmkv):
    n_mem = mem.shape[0]
    half = D_MODEL // N_CHIPS // 2

    def body(mem_ref, g_ref, w_ref, d_ref, dw_ref, dg_ref):
        mhat, _ = _rms_hat(mem_ref[...])
        mn = (mhat * g_ref[...]).astype(BF16)
        d = d_ref[...].astype(BF16)
        for k in range(2 * N_CHIPS):
            dw_ref[k % 2, k // 2] = lax.dot_general(mn[:, half * k:half * (k + 1)], d, TN, preferred_element_type=F32)
        dmn = lax.dot_general(d, w_ref[...], NT, preferred_element_type=F32)
        dg_ref[...] = jnp.sum(dmn * mhat, axis=0, keepdims=True)

    return pl.pallas_call(
        body, name="memkv_bwd",
        out_shape=[jax.ShapeDtypeStruct((2, N_CHIPS, half, 2 * XATTN_W), F32), jax.ShapeDtypeStruct((1, D_MODEL), F32)],
        compiler_params=pltpu.CompilerParams(vmem_limit_bytes=VMEM_LIMIT_V7X),
    )(mem, g_mem, w_kv, dmkv)


def _in_proj_bwd(dqkv, tail, cos, sin, w_in, x, g, dx1, after, tm):
    S = x.shape[0]

    def body(dq_ref, dk_ref, dv_ref, tail_ref, cos_ref, sin_ref, w_hbm, x_ref, g_ref, dx1_ref, after_ref,
             dproj_ref, dx_ref, dg_ref, w_full, sems):
        _side_by_side(w_hbm, w_full, sems)

        @pl.when(pl.program_id(0) == 0)
        def _():
            dg_ref[...] = jnp.zeros_like(dg_ref)

        c, s = cos_ref[...], sin_ref[...]
        for j in range(ATTN_W // 128):
            cols = slice(128 * j, 128 * (j + 1))
            dproj_ref[:, cols] = _rope128(dq_ref[:, cols] * SCALE, c, s, True).astype(BF16)
            dproj_ref[:, ATTN_W + 128 * j:ATTN_W + 128 * (j + 1)] = _rope128(dk_ref[:, cols], c, s, True).astype(BF16)
        dproj_ref[:, 2 * ATTN_W:3 * ATTN_W] = dv_ref[...].astype(BF16)
        dproj_ref[:, 3 * ATTN_W:PROJ_W] = tail_ref[...]
        dh = lax.dot_general(dproj_ref[...], w_full[...], NT, preferred_element_type=F32)
        xhat, r = _rms_hat(x_ref[...])
        dg_ref[...] += jnp.sum(dh * xhat, axis=0, keepdims=True)
        dx_ref[...] = dx1_ref[...] + _rms_bwd(xhat, r, g_ref[...], dh)

    return pl.pallas_call(
        body, name="in_proj_bwd", grid=(S // tm,),
        in_specs=[_rows(tm, ATTN_W)] * 3 + [_rows(tm, PROJ_W - 3 * ATTN_W), _rows(tm, 128), _rows(tm, 128),
                  pl.BlockSpec(memory_space=pl.ANY), _rows(tm, D_MODEL), _resident((1, D_MODEL)),
                  _rows(tm, D_MODEL), pl.BlockSpec(memory_space=pl.ANY)],
        out_specs=[_rows(tm, PROJ_W), _rows(tm, D_MODEL), pl.BlockSpec((1, D_MODEL), lambda i: (0, 0))],
        out_shape=[jax.ShapeDtypeStruct((S, PROJ_W), BF16), jax.ShapeDtypeStruct((S, D_MODEL), F32),
                   jax.ShapeDtypeStruct((1, D_MODEL), F32)],
        scratch_shapes=[pltpu.VMEM((D_MODEL, PROJ_W), BF16), pltpu.SemaphoreType.DMA((N_CHIPS,))],
        compiler_params=_params("arbitrary"),
    )(*dqkv, tail, cos, sin, w_in, x, g, dx1, after)


def _row_tile(rows):
    return ROW_TILE if rows % ROW_TILE == 0 else rows


def _chip_sums_bf16(name, grads, from_sibling, place):
    k = len(grads)
    _, n, rows, _ = grads[0].shape
    tr = _row_tile(rows)

    def body(place_ref, *refs):
        for g_ref, b_ref, o_ref in zip(refs[:k], refs[k:2 * k], refs[2 * k:]):
            o_ref[...] = (g_ref[0] + b_ref[...]).astype(BF16)

    mine = lambda g: pl.BlockSpec((1, 1, tr, g.shape[3]), lambda s, i, p: (p[0], s, i, 0))
    slab = lambda g: pl.BlockSpec((1, tr, g.shape[3]), lambda s, i, p: (s, i, 0))
    return pl.pallas_call(
        body, name=name, out_shape=[jax.ShapeDtypeStruct(g.shape[1:], BF16) for g in grads],
        grid_spec=pltpu.PrefetchScalarGridSpec(
            num_scalar_prefetch=1, grid=(n, rows // tr),
            in_specs=[mine(g) for g in grads] + [slab(g) for g in grads], out_specs=[slab(g) for g in grads]),
        compiler_params=_params("parallel", "parallel"),
    )(place, *grads, *from_sibling)


def _final_sums(name, grads, from_sibling, others, place):
    k = len(grads)
    rows = grads[0].shape[2]
    tr = _row_tile(rows)

    def body(place_ref, *refs):
        for a in range(k):
            own_ref, sib_ref = refs[a], refs[k + a]
            acc = own_ref[0, 0] + sib_ref[0]
            for o in refs[2 * k + 3 * a:2 * k + 3 * a + 3]:
                acc = acc + o[0].astype(F32)
            refs[5 * k + a][0] = acc

    own = lambda g: pl.BlockSpec((1, 1, tr, g.shape[3]), lambda i, p: (p[0], p[1], i, 0))
    sib = lambda g: pl.BlockSpec((1, tr, g.shape[3]), lambda i, p: (p[1], i, 0))
    other = lambda g, j: pl.BlockSpec((1, tr, g.shape[3]), lambda i, p: (j, i, 0))
    return pl.pallas_call(
        body, name=name, out_shape=[jax.ShapeDtypeStruct((2,) + g.shape[2:], F32) for g in grads],
        grid_spec=pltpu.PrefetchScalarGridSpec(
            num_scalar_prefetch=1, grid=(rows // tr,),
            in_specs=[own(g) for g in grads] + [sib(g) for g in grads] + [other(g, j) for g in grads for j in range(3)],
            out_specs=[pl.BlockSpec((1, tr, g.shape[3]), lambda i, p: (p[0], i, 0)) for g in grads]),
        compiler_params=_params("parallel"),
    )(place, *grads, *from_sibling, *[o for o in others for _ in range(3)])


def _adamw_update(w, g, m, v):
    m = ADAM_B1 * m + (1.0 - ADAM_B1) * g
    v = ADAM_B2 * v + (1.0 - ADAM_B2) * (g * g)
    m_hat = m * (1.0 / (1.0 - ADAM_B1 ** ADAM_STEP))
    v_hat = v * (1.0 / (1.0 - ADAM_B2 ** ADAM_STEP))
    return -ADAM_LR * (m_hat / (jnp.sqrt(v_hat) + ADAM_EPS) + ADAM_WD * w), m, v


def _adamw(name, params, after):
    k = len(params)
    rows = params[0][0].shape[0]
    tr = ADAMW_ROW_TILE if rows % ADAMW_ROW_TILE == 0 else rows

    def body(*refs):
        ins, outs = refs[:4 * k], refs[4 * k + 1:]
        for a in range(k):
            w_ref, g_ref, m_ref, v_ref = ins[4 * a:4 * a + 4]
            g = g_ref[...]
            outs[4 * a][...] = g
            outs[4 * a + 1][...], outs[4 * a + 2][...], outs[4 * a + 3][...] = _adamw_update(w_ref[...], g, m_ref[...], v_ref[...])

    spec = lambda w: pl.BlockSpec((tr, w.shape[1]), lambda i: (i, 0))
    out = pl.pallas_call(
        body, name=name, grid=(rows // tr,),
        in_specs=[spec(p[0]) for p in params for _ in range(4)] + [pl.BlockSpec(memory_space=pl.ANY)],
        out_specs=[spec(p[0]) for p in params for _ in range(4)],
        out_shape=[jax.ShapeDtypeStruct(p[0].shape, F32) for p in params for _ in range(4)],
        compiler_params=_params("parallel"),
    )(*[t for p in params for t in p], after)
    return [out[4 * a:4 * a + 4] for a in range(k)]


def _small_update(summed, chip, gains, gains_m, gains_v, taps, taps_m, taps_v):
    n = len(gains)
    widths = [g.shape[1] for g in gains]
    k, w = taps.shape

    def body(*refs):
        chip_ref, sum_ref = refs[0], refs[1]
        params = [refs[2 + 3 * i:5 + 3 * i] for i in range(n + 1)]
        outs = [refs[2 + 3 * (n + 1) + 4 * i:2 + 3 * (n + 1) + 4 * (i + 1)] for i in range(n + 1)]
        loss_ref = refs[-1]
        for i in range(n):
            g = sum_ref[i:i + 1, 0:widths[i]]
            wr, mr, vr = params[i]
            outs[i][0][...] = g
            outs[i][1][...], outs[i][2][...], outs[i][3][...] = _adamw_update(wr[...], g, mr[...], vr[...])
        g = sum_ref[n:n + k, 0:w]
        for j in range(1, N_CHIPS):
            g = jnp.where(chip_ref[0] == j, sum_ref[n:n + k, w * j:w * (j + 1)], g)
        wr, mr, vr = params[n]
        outs[n][0][...] = g
        outs[n][1][...], outs[n][2][...], outs[n][3][...] = _adamw_update(wr[...], g, mr[...], vr[...])
        loss_ref[...] = sum_ref[n + k:n + k + 1, 0:1]

    vmem = pl.BlockSpec(memory_space=pltpu.VMEM)
    operands = [chip, summed]
    for p in zip(list(gains) + [taps], list(gains_m) + [taps_m], list(gains_v) + [taps_v]):
        operands += list(p)
    shapes = [jax.ShapeDtypeStruct(p.shape, F32) for p in list(gains) + [taps] for _ in range(4)]
    out = pl.pallas_call(
        body, name="small_update", out_shape=shapes + [jax.ShapeDtypeStruct((1, 1), F32)],
        in_specs=[pl.BlockSpec(memory_space=pltpu.SMEM)] + [vmem] * (len(operands) - 1),
        out_specs=[vmem] * (len(shapes) + 1),
    )(*operands)
    return [out[4 * i:4 * (i + 1)] for i in range(n + 1)], out[-1]


def _sum_blocks(name, blocks):
    n, rows, cols = blocks.shape

    def body(b_ref, o_ref):
        acc = b_ref[0]
        for k in range(1, n):
            acc = acc + b_ref[k]
        o_ref[...] = acc

    return pl.pallas_call(body, name=name, out_shape=jax.ShapeDtypeStruct((rows, cols), F32))(blocks)


def _place():
    return lax.axis_index("x"), lax.axis_index("y"), lax.axis_index("c")


def _other_chips(x, y):
    return [(1 - x, y), (x, 1 - y), (1 - x, 1 - y)]


def _allgather_finish(name, shards, landed, pass_on):
    n = len(shards)

    def body(*refs):
        ins, outs, stage = refs[:n], refs[2 * n:3 * n], refs[3 * n:4 * n]
        send_sems, recv_sems, local_sems = refs[4 * n:]
        x, y, c = _place()
        chips = _other_chips(x, y)

        def copy(a, k, chip, half):
            place = outs[a].at[2 * chip[0] + chip[1], half]
            return pltpu.make_async_remote_copy(
                src_ref=place, dst_ref=place, send_sem=send_sems.at[3 * a + k], recv_sem=recv_sems.at[3 * a + k],
                device_id=(x, y, 1 - c), device_id_type=MESH)

        load = [pltpu.make_async_copy(ins[a], stage[a], local_sems.at[a]) for a in range(n)]
        local = [pltpu.make_async_copy(stage[a], outs[a].at[2 * x + y], local_sems.at[a]) for a in range(n)]
        for cp in load:
            cp.start()
        passed = [copy(a, k, chip, c) for a in range(n) if pass_on[a] for k, chip in enumerate(chips)]
        for cp in passed:
            cp.start()
        for a in range(n):
            load[a].wait()
            local[a].start()
        for a in range(n):
            if pass_on[a]:
                for k, chip in enumerate(chips):
                    copy(a, k, chip, 1 - c).wait_recv()
        for cp in passed:
            cp.wait_send()
        for cp in local:
            cp.wait()

    any_spec = pl.BlockSpec(memory_space=pl.ANY)
    return pl.pallas_call(
        body, name=name,
        out_shape=[jax.ShapeDtypeStruct((N_CHIPS,) + s.shape, s.dtype) for s in shards],
        in_specs=[any_spec] * (2 * n), out_specs=[any_spec] * n,
        input_output_aliases={n + a: a for a in range(n)},
        scratch_shapes=[pltpu.VMEM(s.shape, s.dtype) for s in shards]
        + [pltpu.SemaphoreType.DMA((3 * n,)), pltpu.SemaphoreType.DMA((3 * n,)), pltpu.SemaphoreType.DMA((n,))],
        compiler_params=pltpu.CompilerParams(vmem_limit_bytes=VMEM_LIMIT_V7X),
    )(*shards, *landed)


def _plan_first_hop(x, y, c, shards, lands):
    return [(shards[a].at[c], lands[a].at[2 * x + y, c], lands[a].at[2 * chip[0] + chip[1], c], (*chip, c))
            for a in range(len(shards)) for chip in _other_chips(x, y)]


def _plan_pass_on(x, y, c, nothing, lands):
    def place(a, chip, half):
        return lands[a].at[2 * chip[0] + chip[1], half]

    return [(place(a, chip, c), place(a, chip, c), place(a, chip, 1 - c), (x, y, 1 - c))
            for a in range(len(lands)) for chip in _other_chips(x, y)]


def _plan_own_half_to_sibling(x, y, c, nothing, lands):
    return [(lands[a].at[c], lands[a].at[c], lands[a].at[1 - c], (x, y, 1 - c)) for a in range(len(lands))]


def _plan_other_half_to_sibling(x, y, c, grads, lands):
    return [(grads[a].at[1 - c], lands[a], lands[a], (x, y, 1 - c)) for a in range(len(grads))]


def _plan_to_other_chips(x, y, c, partials, lands):
    return [(partials[a].at[2 * chip[0] + chip[1]], lands[a].at[k], lands[a].at[k], (*chip, c))
            for a in range(len(partials)) for k, chip in enumerate(_other_chips(x, y))]


def _plan_to_all(x, y, c, blocks, lands):
    flips = [(fx, fy, fc) for fx in (0, 1) for fy in (0, 1) for fc in (0, 1) if (fx, fy, fc) != (0, 0, 0)]
    peers = [(1 - x if fx else x, 1 - y if fy else y, 1 - c if fc else c) for fx, fy, fc in flips]
    return [(blocks[0], lands[0].at[4 * x + 2 * y + c], lands[0].at[4 * p[0] + 2 * p[1] + p[2]], p) for p in peers]


def _planned_copies(plan, srcs, lands, send_sems, recv_sems):
    x, y, c = _place()

    def pair(k, src, there, here, to):
        make = lambda dst: pltpu.make_async_remote_copy(
            src_ref=src, dst_ref=dst, send_sem=send_sems.at[k], recv_sem=recv_sems.at[k], device_id=to, device_id_type=MESH)
        return make(there), make(here)

    return [pair(k, *entry) for k, entry in enumerate(plan(x, y, c, srcs, lands))]


_HBM_SPEC = pl.BlockSpec(memory_space=pltpu.HBM)
_SEM_SPEC = pl.BlockSpec(memory_space=pltpu.SEMAPHORE)


def _hbm(a):
    return pltpu.with_memory_space_constraint(a, pltpu.HBM)


def _exchange_start(name, plan, n_copies, srcs, land_shapes, after, lands=None):
    if lands is None:
        lands = [lax.empty(s.shape, s.dtype) for s in land_shapes]
    land_shapes = lands
    ns, nl = len(srcs), len(land_shapes)
    n_in = ns + nl + 1

    def body(*refs):
        for send, _ in _planned_copies(plan, refs[:ns], refs[ns:ns + nl], refs[n_in], refs[n_in + 1]):
            send.start()
        refs[-1][...] = jnp.zeros_like(refs[-1])

    out = pl.pallas_call(
        body, name=name,
        out_shape=(pltpu.SemaphoreType.DMA((n_copies,)), pltpu.SemaphoreType.DMA((n_copies,)),
                   *[pltpu.HBM(s.shape, s.dtype) for s in land_shapes], jax.ShapeDtypeStruct((8, 128), F32)),
        in_specs=[_HBM_SPEC] * (ns + nl) + [pl.BlockSpec(memory_space=pl.ANY)],
        out_specs=(_SEM_SPEC, _SEM_SPEC, *[_HBM_SPEC] * nl, pl.BlockSpec(memory_space=pltpu.VMEM)),
        input_output_aliases={ns + i: 2 + i for i in range(nl)},
        compiler_params=pltpu.CompilerParams(has_side_effects=pltpu.SideEffectType.DATAFLOW_SIDE_EFFECTING),
    )(*[_hbm(s) for s in srcs], *[_hbm(l) for l in lands], after)
    return out[0], out[1], list(out[2:2 + nl]), out[-1]


def _exchange_wait(name, plan, srcs, started, after):
    send_sems, recv_sems, lands, _ = started
    ns, nl = len(srcs), len(lands)
    after = list(after) if isinstance(after, (list, tuple)) else [after]

    def body(*refs):
        for send, recv in _planned_copies(plan, refs[:ns], refs[ns:ns + nl], refs[ns + nl], refs[ns + nl + 1]):
            send.wait_send()
            recv.wait_recv()

    return pl.pallas_call(
        body, name=name, out_shape=[pltpu.HBM(l.shape, l.dtype) for l in lands],
        in_specs=[_HBM_SPEC] * (ns + nl) + [_SEM_SPEC, _SEM_SPEC] + [pl.BlockSpec(memory_space=pl.ANY)] * len(after),
        out_specs=[_HBM_SPEC] * nl, input_output_aliases={ns + i: i for i in range(nl)},
        compiler_params=pltpu.CompilerParams(has_side_effects=pltpu.SideEffectType.DATAFLOW_SIDE_EFFECTING),
    )(*[_hbm(s) for s in srcs], *lands, send_sems, recv_sems, *after)


def _like(arrays, lead, dtype=None):
    return [jax.ShapeDtypeStruct(tuple(lead) + a.shape[-2:], dtype or a.dtype) for a in arrays]


class _StepExchanges:
    def __init__(self, mats, conv_w):
        x, y, c = _place()
        self.place = jnp.stack([c, 2 * x + y]).astype(jnp.int32)
        shards = [w.astype(BF16).reshape(2, w.shape[0] // 2, w.shape[1]) for w in mats]
        self._in_shard = shards[:1]
        self._in = _exchange_start("w_in_allgather_start", _plan_first_hop, 3, self._in_shard,
                                   _like(self._in_shard, (N_CHIPS, 2)), shards[0])
        self.zero = self._in[3]
        taps = jnp.pad(conv_w, ((0, 8 - conv_w.shape[0]), (0, 128 - conv_w.shape[1])))
        self._rest_shards = shards[1:] + [jnp.stack([taps, jnp.zeros_like(taps)])]
        self._taps_shape = conv_w.shape
        self._groups = {}

    def w_in(self, after):
        landed = _exchange_wait("w_in_allgather_wait", _plan_first_hop, self._in_shard, self._in,
                                list(after) + self._rest_shards)
        (w_in,) = _allgather_finish("w_in_allgather_finish", self._in_shard, landed, [True])
        self._rest = _exchange_start("rest_allgather_start", _plan_first_hop, 3 * len(self._rest_shards),
                                     self._rest_shards, _like(self._rest_shards, (N_CHIPS, 2)), w_in)
        self.zero = self._rest[3]
        return w_in.reshape(N_CHIPS, 2 * w_in.shape[2], w_in.shape[3])

    def rest_weights(self, after):
        landed = _exchange_wait("rest_allgather_wait", _plan_first_hop, self._rest_shards, self._rest, after)
        kv, out, up, down, taps = _allgather_finish("rest_allgather_finish", self._rest_shards, landed,
                                                    [True, True, False, False, True])
        self._up_down = _exchange_start("up_down_pass_on_start", _plan_pass_on, 6, [], None, self.zero, lands=[up, down])
        self.zero = self._up_down[3]
        k, w = self._taps_shape
        taps = taps[:, 0, :k, :w].transpose(1, 0, 2).reshape(k, N_CHIPS * w)
        return [g.reshape(N_CHIPS, 2 * g.shape[2], g.shape[3]) for g in (kv, out)], taps

    def up_down(self, after):
        full = _exchange_wait("up_down_pass_on_wait", _plan_pass_on, [], self._up_down, after)
        return [g.reshape(N_CHIPS, 2 * g.shape[2], g.shape[3]) for g in full]

    def send_grads(self, key, grads):
        grads = list(grads)
        started = _exchange_start(f"{key}_grads_to_sibling_start", _plan_other_half_to_sibling, len(grads), grads,
                                  _like(grads, (N_CHIPS,)), self.zero)
        self._groups[key] = dict(grads=grads, to_sibling=started)
        self.zero = started[3]

    def grads_at_sibling(self, key, after):
        group = self._groups[key]
        grads = group["grads"]
        group["from_sibling"] = _exchange_wait(f"{key}_grads_to_sibling_wait", _plan_other_half_to_sibling, grads,
                                               group["to_sibling"], after)
        group["partials"] = _chip_sums_bf16(f"{key}_chip_sums", grads, group["from_sibling"], self.place)
        group["to_chips"] = _exchange_start(f"{key}_grads_to_chips_start", _plan_to_other_chips, 3 * len(grads),
                                            group["partials"], _like(group["partials"], (3,)), self.zero)
        self.zero = group["to_chips"][3]

    def grads_summed(self, key, after):
        group = self._groups[key]
        from_chips = _exchange_wait(f"{key}_grads_to_chips_wait", _plan_to_other_chips, group["partials"],
                                    group["to_chips"], after)
        return _final_sums(f"{key}_final_sums", group["grads"], group["from_sibling"], from_chips, self.place)

    def send_sums(self, key, sums):
        self._groups[key + "_sums"] = _exchange_start(f"{key}_sums_to_sibling_start", _plan_own_half_to_sibling,
                                                      len(sums), [], None, self.zero, lands=list(sums))
        self.zero = self._groups[key + "_sums"][3]

    def whole_sums(self, key, after):
        full = _exchange_wait(f"{key}_sums_to_sibling_wait", _plan_own_half_to_sibling, [], self._groups[key + "_sums"], after)
        return [t.reshape(2 * t.shape[1], t.shape[2]) for t in full]

    def send_small(self, block):
        self._small = block
        self._small_started = _exchange_start("small_grads_start", _plan_to_all, 7, [block],
                                              [jax.ShapeDtypeStruct((8,) + block.shape, block.dtype)], self.zero)
        self.zero = self._small_started[3]

    def small_summed(self, after):
        x, y, c = _place()
        (landed,) = _exchange_wait("small_grads_wait", _plan_to_all, [self._small], self._small_started, after)
        blocks = lax.dynamic_update_index_in_dim(landed, self._small, 4 * x + 2 * y + c, 0)
        return _sum_blocks("small_sum", blocks)


def _rope_tables(positions):
    half = HEAD // 2
    inv_freq = jnp.float32(ROPE_THETA) ** (-(jnp.arange(half, dtype=F32) * 2.0 / HEAD))
    ang = positions.astype(F32)[:, None] * inv_freq
    cos, sin = jnp.cos(ang), jnp.sin(ang)
    return jnp.tile(cos, (1, 4)), jnp.tile(jnp.concatenate([-sin, sin], axis=1), (1, 2))


def _local_step(x, mem, positions, target, gains, ex):
    g_pre_mix, g_mem, g_a, g_c, g_x, g_post_mix, g_pre_mlp, g_post_mlp = gains
    tm = ROW_TILE
    cos, sin = _rope_tables(positions)
    h = _pre_norm(x, g_pre_mix, ex.zero, tm)
    w_in = ex.w_in([h, cos, sin])

    q, k, v, bcu, qx = _in_proj_fwd(h, w_in, cos, sin, ex.zero, tm)
    ya, lse = _attn_fwd(q, k, v)
    (w_kv, w_out), conv_w = ex.rest_weights(lse)
    w_kv, w_out = (w.reshape(N_CHIPS * w.shape[1], w.shape[2]) for w in (w_kv, w_out))
    memn, mkv = _memkv_fwd(mem, g_mem, w_kv, ex.zero)
    yx, ycat, y2, x1 = _mix_fwd(ya, bcu, qx, mkv, conv_w, g_a, g_c, g_x, w_out, g_post_mix, x, tm)
    w_up, w_down = ex.up_down(x1)
    w_down = w_down.reshape(N_CHIPS * w_down.shape[1], w_down.shape[2])
    h2, f, du, df2, dx1, dg_pre_mlp, dg_post_mlp, loss = _mlp_fwd_bwd(x1, target, g_pre_mlp, g_post_mlp, w_up, w_down,
                                                                      MLP_ROW_TILE)
    gw_down = _weight_grad("grad_w_down", f, df2, True)
    gw_up = _weight_grad("grad_w_up", h2, du, False)
    ex.send_grads("early", [gw_up, gw_down])

    dy2, dya, delta, dycx, dg_post_mix, dg_a, dg_c, dg_x = _mix_bwd(dx1, y2, ya, yx, bcu, conv_w, g_a, g_c, g_x,
                                                                  w_out, g_post_mix, ex.zero, tm)
    ex.grads_at_sibling("early", dy2)
    gw_out = _weight_grad("grad_w_out", ycat, dy2, True)
    tail, dmkv, g_conv = _conv_xattn_bwd(dycx, bcu, qx, mkv, conv_w, ex.zero, tm)
    gw_kv, dg_mem = _memkv_bwd(mem, g_mem, w_kv, dmkv)
    ex.send_grads("mid", [gw_out, gw_kv])
    dqkv = _attn_bwd(q, k, v, dya, lse, delta, ex.zero)
    ex.grads_at_sibling("mid", dqkv[0])
    dproj, grad_x, dg_pre_mix = _in_proj_bwd(dqkv, tail, cos, sin, w_in, x, g_pre_mix, dx1, ex.zero, tm)
    gain_grads = [dg_pre_mix, dg_mem, dg_a, dg_c, dg_x, dg_post_mix, dg_pre_mlp, dg_post_mlp]
    ex.send_small(_pack_small(gain_grads, g_conv, loss))
    gw_in = _weight_grad_w_in(h, dproj)
    ex.send_grads("late", [gw_in])
    return grad_x


def _pack_small(gains, conv, scalar=None):
    rows = [jnp.pad(g, ((0, 0), (0, D_MODEL - g.shape[1]))) for g in gains]
    rows.append(jnp.pad(conv, ((0, 0), (0, D_MODEL - conv.shape[1]))))
    last = jnp.zeros((SMALL_ROWS - 8 - conv.shape[0], D_MODEL), F32)
    rows.append(last if scalar is None else last.at[0:1, 0:1].set(scalar))
    return jnp.concatenate(rows, axis=0)


def _unpack_small(block, gain_widths, conv_width):
    gains = [block[i:i + 1, :w] for i, w in enumerate(gain_widths)]
    return gains, block[8:11, :conv_width], block[11, 0]


def kernel(x, mem, positions, g_pre_mix, g_mem, w_in, w_mem_kv, conv_w, g_attn_out, g_conv_out, g_xattn_out, w_out, g_post_mix, g_pre_mlp, w_up, w_down, g_post_mlp, loss_target, m_g_pre_mix, m_g_mem, m_w_in, m_w_mem_kv, m_conv_w, m_g_attn_out, m_g_conv_out, m_g_xattn_out, m_w_out, m_g_post_mix, m_g_pre_mlp, m_w_up, m_w_down, m_g_post_mlp, v_g_pre_mix, v_g_mem, v_w_in, v_w_mem_kv, v_conv_w, v_g_attn_out, v_g_conv_out, v_g_xattn_out, v_w_out, v_g_post_mix, v_g_pre_mlp, v_w_up, v_w_down, v_g_post_mlp):
    cx, cy, cc = _place()
    chip = 2 * cx + cy
    gains = [g_pre_mix, g_mem, g_attn_out, g_conv_out, g_xattn_out, g_post_mix, g_pre_mlp, g_post_mlp]
    gains_m = [m_g_pre_mix, m_g_mem, m_g_attn_out, m_g_conv_out, m_g_xattn_out, m_g_post_mix, m_g_pre_mlp, m_g_post_mlp]
    gains_v = [v_g_pre_mix, v_g_mem, v_g_attn_out, v_g_conv_out, v_g_xattn_out, v_g_post_mix, v_g_pre_mlp, v_g_post_mlp]
    gain_widths = [g.shape[1] for g in gains]
    mats = [w_in[0], w_mem_kv[0], w_out[0], w_up[0], w_down[0]]
    mats_m = [m_w_in[0], m_w_mem_kv[0], m_w_out[0], m_w_up[0], m_w_down[0]]
    mats_v = [v_w_in[0], v_w_mem_kv[0], v_w_out[0], v_w_up[0], v_w_down[0]]

    ex = _StepExchanges(mats, conv_w[0])
    grad_x = _local_step(x[0], mem[0], positions[0], loss_target[0], gains, ex)

    ex.send_sums("four", ex.grads_summed("early", ex.zero) + ex.grads_summed("mid", ex.zero))
    ex.grads_at_sibling("late", ex.zero)
    up_sum, down_sum, out_sum, kv_sum = ex.whole_sums("four", ex.zero)
    params = lambda a, g: (mats[a], g, mats_m[a], mats_v[a])
    new_up, new_down = _adamw("adamw_up_down", [params(3, up_sum), params(4, down_sum)], ex.zero)
    new_out, new_kv = _adamw("adamw_out_kv", [params(2, out_sum), params(1, kv_sum)], ex.zero)

    small, total = _small_update(ex.small_summed(new_kv[1]), chip.reshape(1).astype(jnp.int32), gains, gains_m,
                                 gains_v, conv_w[0], m_conv_w[0], v_conv_w[0])

    ex.send_sums("last", ex.grads_summed("late", small[0][1]))
    (in_sum,) = ex.whole_sums("last", ex.zero)
    (new_in,) = _adamw("adamw_in", [params(0, in_sum)], in_sum)
    mat_new = [new_in, new_kv, new_out, new_up, new_down]

    order = ["g_pre_mix", "g_mem", "w_in", "w_mem_kv", "conv_w", "g_attn_out", "g_conv_out", "g_xattn_out", "w_out",
             "g_post_mix", "g_pre_mlp", "w_up", "w_down", "g_post_mlp"]
    gain_names = ["g_pre_mix", "g_mem", "g_attn_out", "g_conv_out", "g_xattn_out", "g_post_mix", "g_pre_mlp", "g_post_mlp"]
    mat_names = ["w_in", "w_mem_kv", "w_out", "w_up", "w_down"]

    def leaf(kind, name):
        if name in gain_names:
            return small[gain_names.index(name)][kind]
        if name == "conv_w":
            return small[len(gain_names)][kind][None]
        return mat_new[mat_names.index(name)][kind][None]

    return (total[0, 0], grad_x[None], *[leaf(kind, name) for kind in range(4) for name in order])
```

```python
import jax
import jax.numpy as jnp
from jax import lax
from jax.experimental import pallas as pl
from jax.experimental.pallas import tpu as pltpu

F32, BF16 = jnp.float32, jnp.bfloat16

D_MODEL = 1024
ATTN_W = 512
CONV_W = 256
XATTN_W = 256
PROJ_W = 3 * ATTN_W + 3 * CONV_W + XATTN_W
D_FF = 4096
HEAD = 64
N_BACK = 128
DILATIONS = (1, 4, 16)
ROPE_THETA = 10000.0
EPS = 1e-6
NEG_INF = -1e30
SCALE = HEAD ** -0.5
N_CHIPS = 4
SHARD_IN = PROJ_W // N_CHIPS
SHARD_FF = D_FF // N_CHIPS

ADAM_LR, ADAM_B1, ADAM_B2, ADAM_EPS, ADAM_WD, ADAM_STEP = 0.001, 0.9, 0.999, 1e-08, 0.01, 10

VMEM_LIMIT_V7X = 56 * 1024 * 1024
ROW_TILE = 512
MLP_ROW_TILE = 256
ADAMW_ROW_TILE = 256
SMALL_ROWS = 16

NT = (((1,), (1,)), ((), ()))
TN = (((0,), (0,)), ((), ()))
MESH = pl.DeviceIdType.MESH


def _params(*sem):
    return pltpu.CompilerParams(dimension_semantics=sem, vmem_limit_bytes=VMEM_LIMIT_V7X)


def _resident(shape):
    return pl.BlockSpec(shape, lambda *_: (0,) * len(shape), pipeline_mode=pl.Buffered(1))


def _rows(tm, width):
    return pl.BlockSpec((tm, width), lambda i: (i, 0))


def _rms_hat(x):
    r = lax.rsqrt(jnp.mean(x * x, axis=-1, keepdims=True) + EPS)
    return x * r, r


def _rms_bwd(xhat, r, g, dy):
    gdy = dy * g
    return r * (gdy - xhat * jnp.mean(xhat * gdy, axis=-1, keepdims=True))


def _rope128(t, cos, sin_signed, inverse):
    lane = lax.broadcasted_iota(jnp.int32, t.shape, 1)
    first_half = (lane % HEAD) < (HEAD // 2)
    rot = jnp.where(first_half, pltpu.roll(t, 128 - HEAD // 2, 1), pltpu.roll(t, HEAD // 2, 1))
    return t * cos - rot * sin_signed if inverse else t * cos + rot * sin_signed


def _pre_norm(x, g, after, tm):
    S = x.shape[0]

    def body(x_ref, g_ref, after_ref, h_ref):
        h_ref[...] = (_rms_hat(x_ref[...])[0] * g_ref[...]).astype(BF16)

    return pl.pallas_call(
        body, name="pre_norm", grid=(S // tm,),
        in_specs=[_rows(tm, D_MODEL), _resident((1, D_MODEL)), pl.BlockSpec(memory_space=pl.ANY)],
        out_specs=_rows(tm, D_MODEL), out_shape=jax.ShapeDtypeStruct((S, D_MODEL), BF16),
        compiler_params=_params("parallel"),
    )(x, g, after)


def _side_by_side(w_hbm, w_full, sems):
    @pl.when(pl.program_id(0) == 0)
    def _():
        copies = [pltpu.make_async_copy(w_hbm.at[j], w_full.at[:, pl.ds(SHARD_IN * j, SHARD_IN)], sems.at[j])
                  for j in range(N_CHIPS)]
        for cp in copies:
            cp.start()
        for cp in copies:
            cp.wait()


def _in_proj_fwd(h, w_in, cos, sin, after, tm):
    S = h.shape[0]

    def body(h_ref, w_hbm, cos_ref, sin_ref, after_ref, q_ref, k_ref, v_ref, bcu_ref, qx_ref, proj, w_full, sems):
        _side_by_side(w_hbm, w_full, sems)
        proj[...] = jnp.dot(h_ref[...], w_full[...], preferred_element_type=F32)
        c, s = cos_ref[...], sin_ref[...]
        for j in range(ATTN_W // 128):
            lo = 128 * j
            q_ref[:, lo:lo + 128] = _rope128(proj[:, lo:lo + 128], c, s, False) * SCALE
            k_ref[:, lo:lo + 128] = _rope128(proj[:, ATTN_W + lo:ATTN_W + lo + 128], c, s, False)
        v_ref[...] = proj[:, 2 * ATTN_W:3 * ATTN_W]
        bcu_ref[...] = proj[:, 3 * ATTN_W:3 * ATTN_W + 3 * CONV_W]
        qx_ref[...] = proj[:, 3 * ATTN_W + 3 * CONV_W:PROJ_W].astype(BF16)

    return pl.pallas_call(
        body, name="in_proj_fwd", grid=(S // tm,),
        in_specs=[_rows(tm, D_MODEL), pl.BlockSpec(memory_space=pl.ANY), _rows(tm, 128), _rows(tm, 128),
                  pl.BlockSpec(memory_space=pl.ANY)],
        out_specs=[_rows(tm, ATTN_W), _rows(tm, ATTN_W), _rows(tm, ATTN_W), _rows(tm, 3 * CONV_W), _rows(tm, XATTN_W)],
        out_shape=[jax.ShapeDtypeStruct((S, ATTN_W), F32), jax.ShapeDtypeStruct((S, ATTN_W), F32),
                   jax.ShapeDtypeStruct((S, ATTN_W), F32), jax.ShapeDtypeStruct((S, 3 * CONV_W), F32),
                   jax.ShapeDtypeStruct((S, XATTN_W), BF16)],
        scratch_shapes=[pltpu.VMEM((tm, PROJ_W), F32), pltpu.VMEM((D_MODEL, PROJ_W), BF16),
                        pltpu.SemaphoreType.DMA((N_CHIPS,))],
        compiler_params=_params("arbitrary"),
    )(h, w_in, cos, sin, after)


def _memkv_fwd(mem, g_mem, w_kv, after):
    n_mem = mem.shape[0]

    def body(mem_ref, g_ref, w_ref, after_ref, mn_ref, kv_ref):
        mhat, _ = _rms_hat(mem_ref[...])
        mn = (mhat * g_ref[...]).astype(BF16)
        mn_ref[...] = mn
        kv_ref[...] = jnp.dot(mn, w_ref[...], preferred_element_type=F32).astype(BF16)

    vmem = pl.BlockSpec(memory_space=pltpu.VMEM)
    return pl.pallas_call(
        body, name="memkv_fwd", in_specs=[vmem, vmem, vmem, pl.BlockSpec(memory_space=pl.ANY)], out_specs=[vmem, vmem],
        out_shape=[jax.ShapeDtypeStruct((n_mem, D_MODEL), BF16), jax.ShapeDtypeStruct((n_mem, 2 * XATTN_W), BF16)],
        compiler_params=pltpu.CompilerParams(vmem_limit_bytes=VMEM_LIMIT_V7X),
    )(mem, g_mem, w_kv, after)


def _fill_band_bias(bias):
    row = lax.broadcasted_iota(jnp.int32, (N_BACK, 2 * N_BACK), 0)
    col = lax.broadcasted_iota(jnp.int32, (N_BACK, 2 * N_BACK), 1)
    band = (col >= row) & (col <= row + N_BACK)
    bias[1] = jnp.where(band, 0.0, NEG_INF)
    bias[0] = jnp.where(band & (col >= N_BACK), 0.0, NEG_INF)


def _strided(start, size, d):
    return pl.ds(start, size) if d == 1 else pl.ds(start, size, stride=d)


def _group_starts(g, G, nb, d):
    t0 = g * G
    r, n0 = lax.shift_right_logical(t0, nb.bit_length() - 1), lax.bitwise_and(t0, nb - 1)
    first = r + n0 * (N_BACK * d)
    before = r + jnp.maximum(n0 - 1, 0) * (N_BACK * d)
    starts = [before] + [first + u * (N_BACK * d) for u in range(G)]
    if d == 1:
        starts = [pl.multiple_of(st, N_BACK) for st in starts]
    return starts, n0


def _step_blocks(i, U, nb, d):
    G = min(U, nb)
    whole = G == nb
    row_blocks, blocks = [], []
    for grp in range(U // G):
        starts, n0 = _group_starts(i * (U // G) + grp, G, nb, d)
        base = len(row_blocks)
        if whole:
            row_blocks += [_strided(st, N_BACK, d) for st in starts[1:]]
            blocks += [(base + max(u - 1, 0), base + u, min(u, 1)) for u in range(G)]
        else:
            row_blocks += [_strided(st, N_BACK, d) for st in starts]
            blocks += [(base + u, base + u + 1, jnp.minimum(n0, 1) if u == 0 else 1) for u in range(G)]
    return row_blocks, blocks


def _by_head(a, b):
    lane = lax.broadcasted_iota(jnp.int32, (a.shape[0], 2 * HEAD), 1)
    return jnp.where(lane < HEAD, a, b)


def _head_only(t, hh):
    lane = lax.broadcasted_iota(jnp.int32, t.shape, 1)
    return jnp.where((lane < HEAD) == (hh == 0), t, jnp.zeros_like(t))


def _stack_heads(t):
    return jnp.concatenate([_head_only(t, 0), _head_only(t, 1)], axis=0)


def _head_columns(t):
    return jnp.concatenate([t[:, 0:1], t[:, HEAD:HEAD + 1]], axis=0)


def _unstack(t):
    return _by_head(t[:N_BACK], t[N_BACK:])


def _unstack_columns(t):
    return _by_head(jnp.broadcast_to(t[:N_BACK], (N_BACK, 2 * HEAD)), jnp.broadcast_to(t[N_BACK:], (N_BACK, 2 * HEAD)))


FWD_BLOCKS_PER_STEP = 4
BWD_BLOCKS_PER_STEP = 4


def _attn_fwd(q, k, v):
    S = q.shape[0]
    U = FWD_BLOCKS_PER_STEP

    def body(q_ref, k_ref, v_ref, y_ref, m_ref, l_scr, bias):
        _fill_band_bias(bias)
        for g, d in enumerate(DILATIONS):
            nb = S // d // N_BACK
            first_pattern, last_pattern = g == 0, g == len(DILATIONS) - 1

            def step(i, carry, d=d, nb=nb, first_pattern=first_pattern, last_pattern=last_pattern):
                row_blocks, blocks = _step_blocks(i, U, nb, d)
                kb = [k_ref[r, :].astype(BF16) for r in row_blocks]
                ss = []
                for before, own, which in blocks:
                    kw = jnp.concatenate([kb[before], kb[own]], 0)
                    qs = _stack_heads(q_ref[row_blocks[own], :].astype(BF16))
                    b = bias[which]
                    ss.append(lax.dot_general(qs, kw, NT, preferred_element_type=F32) + jnp.concatenate([b, b], axis=0))
                ms = [jnp.max(s, axis=1, keepdims=True) for s in ss]
                ps = [jnp.exp(s - m) for s, m in zip(ss, ms)]
                ls = [jnp.sum(p, axis=1, keepdims=True) for p in ps]
                vb = [v_ref[r, :].astype(BF16) for r in row_blocks]
                os_ = [jnp.dot(ps[u].astype(BF16), jnp.concatenate([vb[before], vb[own]], 0), preferred_element_type=F32)
                       for u, (before, own, _) in enumerate(blocks)]
                for u, (_, own, _) in enumerate(blocks):
                    o_g, m_g, l_g = _unstack(os_[u]), _unstack_columns(ms[u]), _unstack_columns(ls[u])
                    r = row_blocks[own]
                    if first_pattern:
                        m_new, l_new, acc = m_g, l_g, o_g
                    else:
                        m_old = m_ref[r, :]
                        m_new = jnp.maximum(m_old, m_g)
                        alpha, beta = jnp.exp(m_old - m_new), jnp.exp(m_g - m_new)
                        l_new = l_scr[r, :] * alpha + l_g * beta
                        acc = y_ref[r, :] * alpha + o_g * beta
                    if last_pattern:
                        y_ref[r, :] = acc / l_new
                        m_ref[r, :] = m_new + jnp.log(l_new)
                    else:
                        y_ref[r, :] = acc
                        m_ref[r, :] = m_new
                        l_scr[r, :] = l_new
                return carry

            lax.fori_loop(0, d * nb // U, step, 0)

    col = pl.BlockSpec((S, 2 * HEAD), lambda j: (0, j))
    return pl.pallas_call(
        body, name="attn_fwd", grid=(q.shape[1] // (2 * HEAD),),
        in_specs=[col, col, col], out_specs=[col, col],
        out_shape=[jax.ShapeDtypeStruct(q.shape, F32)] * 2,
        scratch_shapes=[pltpu.VMEM((S, 2 * HEAD), F32), pltpu.VMEM((2, N_BACK, 2 * N_BACK), F32)],
        compiler_params=_params("parallel"),
    )(q, k, v)


def _attn_bwd(q, k, v, dy, lse, delta, after):
    S = q.shape[0]
    U = BWD_BLOCKS_PER_STEP

    def body(q_ref, k_ref, v_ref, dy_ref, lse_ref, delta_ref, after_ref, dq_ref, dk_ref, dv_ref, bias):
        _fill_band_bias(bias)
        dk_ref[...] = jnp.zeros_like(dk_ref)
        dv_ref[...] = jnp.zeros_like(dv_ref)
        for g, d in enumerate(DILATIONS):
            nb = S // d // N_BACK

            def step(i, carry, d=d, nb=nb, g=g):
                row_blocks, blocks = _step_blocks(i, U, nb, d)
                kb = [k_ref[r, :].astype(BF16) for r in row_blocks]
                vb = [v_ref[r, :].astype(BF16) for r in row_blocks]
                kws = [jnp.concatenate([kb[before], kb[own]], 0) for before, own, _ in blocks]
                vws = [jnp.concatenate([vb[before], vb[own]], 0) for before, own, _ in blocks]
                qss = [_stack_heads(q_ref[row_blocks[own], :].astype(BF16)) for _, own, _ in blocks]
                doss = [_stack_heads(dy_ref[row_blocks[own], :].astype(BF16)) for _, own, _ in blocks]
                ss, dps = [], []
                for u, (_, _, which) in enumerate(blocks):
                    b = bias[which]
                    ss.append(lax.dot_general(qss[u], kws[u], NT, preferred_element_type=F32) + jnp.concatenate([b, b], axis=0))
                    dps.append(lax.dot_general(doss[u], vws[u], NT, preferred_element_type=F32))
                ps = [jnp.exp(ss[u] - _head_columns(lse_ref[row_blocks[own], :])) for u, (_, own, _) in enumerate(blocks)]
                dss = [(ps[u] * (dps[u] - _head_columns(delta_ref[row_blocks[own], :]))).astype(BF16)
                       for u, (_, own, _) in enumerate(blocks)]
                pbs = [p.astype(BF16) for p in ps]
                dqs = [jnp.dot(dss[u], kws[u], preferred_element_type=F32) for u in range(U)]
                dkws = [lax.dot_general(dss[u], qss[u], TN, preferred_element_type=F32) for u in range(U)]
                dvws = [lax.dot_general(pbs[u], doss[u], TN, preferred_element_type=F32) for u in range(U)]
                dk_parts, dv_parts = [None] * len(row_blocks), [None] * len(row_blocks)
                for u, (before, own, _) in enumerate(blocks):
                    dq = _unstack(dqs[u])
                    if g == 0:
                        dq_ref[row_blocks[own], :] = dq
                    else:
                        dq_ref[row_blocks[own], :] += dq
                    for idx, dkp, dvp in ((before, dkws[u][:N_BACK], dvws[u][:N_BACK]),
                                          (own, dkws[u][N_BACK:], dvws[u][N_BACK:])):
                        dk_parts[idx] = dkp if dk_parts[idx] is None else dk_parts[idx] + dkp
                        dv_parts[idx] = dvp if dv_parts[idx] is None else dv_parts[idx] + dvp
                for idx, r in enumerate(row_blocks):
                    dk_ref[r, :] += dk_parts[idx]
                    dv_ref[r, :] += dv_parts[idx]
                return carry

            lax.fori_loop(0, d * nb // U, step, 0)

    col = pl.BlockSpec((S, 2 * HEAD), lambda j: (0, j))
    return pl.pallas_call(
        body, name="attn_bwd", grid=(q.shape[1] // (2 * HEAD),),
        in_specs=[col] * 6 + [pl.BlockSpec(memory_space=pl.ANY)], out_specs=[col] * 3,
        out_shape=[jax.ShapeDtypeStruct(q.shape, F32)] * 3,
        scratch_shapes=[pltpu.VMEM((2, N_BACK, 2 * N_BACK), F32)],
        compiler_params=_params("parallel"),
    )(q, k, v, dy, lse, delta, after)


def _shift_down(z, before, k):
    row = lax.broadcasted_iota(jnp.int32, z.shape, 0)
    out = pltpu.roll(z, k, 0)
    for i in range(k):
        out = jnp.where(row == i, before[8 - k + i:8 - k + i + 1, :], out)
    return out


def _shift_up(z, after, k):
    rows = z.shape[0]
    row = lax.broadcasted_iota(jnp.int32, z.shape, 0)
    out = pltpu.roll(z, rows - k, 0)
    for i in range(k):
        out = jnp.where(row == rows - k + i, after[i:i + 1, :], out)
    return out


def _conv_fwd(bcu, before, is_first, w):
    b, c, u = bcu[:, 0:CONV_W], bcu[:, CONV_W:2 * CONV_W], bcu[:, 2 * CONV_W:3 * CONV_W]
    z = c * u
    zb = jnp.where(is_first, 0.0, before[:, CONV_W:2 * CONV_W] * before[:, 2 * CONV_W:3 * CONV_W])
    z1, z2 = _shift_down(z, zb, 1), _shift_down(z, zb, 2)
    cv = w[0:1, :] * z2 + w[1:2, :] * z1 + w[2:3, :] * z
    return b, c, u, z, z1, z2, cv


def _halo_before(tm, width):
    return pl.BlockSpec((8, width), lambda i: (jnp.maximum(i * (tm // 8) - 1, 0), 0))


def _mix_fwd(ya, bcu, qx, mkv, conv_w, g_a, g_c, g_x, w_out, g_post, x, tm):
    S = x.shape[0]

    def body(ya_ref, bcu_ref, before_ref, qx_ref, mkv_ref, cw_ref, ga_ref, gc_ref, gx_ref,
             wo_ref, gp_ref, x_ref, yx_ref, ycat_ref, y2_ref, x1_ref):
        ya = ya_ref[...]
        b, _, _, _, _, _, cv = _conv_fwd(bcu_ref[...], before_ref[...], pl.program_id(0) == 0, cw_ref[...])
        yc = b * cv

        qxb, mkvb = qx_ref[...], mkv_ref[...]
        for hd in range(XATTN_W // HEAD):
            sl = slice(HEAD * hd, HEAD * (hd + 1))
            s = lax.dot_general(qxb[:, sl], mkvb[:, sl], NT, preferred_element_type=F32) * SCALE
            mx = jnp.max(s, axis=1, keepdims=True)
            p = jnp.exp(s - mx)
            l = jnp.sum(p, axis=1, keepdims=True)
            vm = mkvb[:, XATTN_W + HEAD * hd:XATTN_W + HEAD * (hd + 1)]
            yx_ref[:, sl] = jnp.dot(p.astype(BF16), vm, preferred_element_type=F32) / l
        yx = yx_ref[...]

        ycat_ref[:, 0:ATTN_W] = (_rms_hat(ya)[0] * ga_ref[...]).astype(BF16)
        ycat_ref[:, ATTN_W:ATTN_W + CONV_W] = (_rms_hat(yc)[0] * gc_ref[...]).astype(BF16)
        ycat_ref[:, ATTN_W + CONV_W:D_MODEL] = (_rms_hat(yx)[0] * gx_ref[...]).astype(BF16)
        y2 = jnp.dot(ycat_ref[...], wo_ref[...], preferred_element_type=F32)
        y2_ref[...] = y2
        x1_ref[...] = x_ref[...] + _rms_hat(y2)[0] * gp_ref[...]

    n_mem = mkv.shape[0]
    return pl.pallas_call(
        body, name="mix_fwd", grid=(S // tm,),
        in_specs=[_rows(tm, ATTN_W), _rows(tm, 3 * CONV_W), _halo_before(tm, 3 * CONV_W), _rows(tm, XATTN_W),
                  _resident((n_mem, 2 * XATTN_W)), _resident((3, CONV_W)), _resident((1, ATTN_W)),
                  _resident((1, CONV_W)), _resident((1, XATTN_W)), _resident((D_MODEL, D_MODEL)),
                  _resident((1, D_MODEL)), _rows(tm, D_MODEL)],
        out_specs=[_rows(tm, XATTN_W), _rows(tm, D_MODEL), _rows(tm, D_MODEL), _rows(tm, D_MODEL)],
        out_shape=[jax.ShapeDtypeStruct((S, XATTN_W), F32), jax.ShapeDtypeStruct((S, D_MODEL), BF16),
                   jax.ShapeDtypeStruct((S, D_MODEL), F32), jax.ShapeDtypeStruct((S, D_MODEL), F32)],
        compiler_params=_params("parallel"),
    )(ya, bcu, bcu, qx, mkv, conv_w, g_a, g_c, g_x, w_out, g_post, x)


def _mlp_fwd_bwd(x1, target, g_pre, g_post, w_up, w_down, tm):
    S = x1.shape[0]
    n_ff = D_FF // SHARD_FF

    def body(x1_ref, t_ref, gpre_ref, gpost_ref, wup_ref, wdn_ref,
             h2_ref, f_ref, du_ref, df2_ref, dx1_ref, dgpre_ref, dgpost_ref, loss_ref, u_scr):
        @pl.when(pl.program_id(0) == 0)
        def _():
            dgpre_ref[...] = jnp.zeros_like(dgpre_ref)
            dgpost_ref[...] = jnp.zeros_like(dgpost_ref)
            loss_ref[...] = jnp.zeros_like(loss_ref)

        x1 = x1_ref[...]
        x1hat, r1 = _rms_hat(x1)
        h2 = (x1hat * gpre_ref[...]).astype(BF16)
        h2_ref[...] = h2
        f2 = jnp.zeros((tm, D_MODEL), F32)
        for j in range(n_ff):
            cols = slice(SHARD_FF * j, SHARD_FF * (j + 1))
            u = jnp.maximum(jnp.dot(h2, wup_ref[j], preferred_element_type=F32), 0.0)
            u_scr[:, cols] = u
            f = (u * u).astype(BF16)
            f_ref[:, cols] = f
            f2 = f2 + jnp.dot(f, wdn_ref[cols, :], preferred_element_type=F32)
        f2hat, r2 = _rms_hat(f2)
        err = x1 + f2hat * gpost_ref[...] - t_ref[...]
        loss_ref[...] += 0.5 * jnp.sum(jnp.mean(err * err, axis=-1, keepdims=True), axis=0, keepdims=True)
        dx2 = err * (1.0 / D_MODEL)
        dgpost_ref[...] += jnp.sum(dx2 * f2hat, axis=0, keepdims=True)
        df2 = _rms_bwd(f2hat, r2, gpost_ref[...], dx2).astype(BF16)
        df2_ref[...] = df2
        dh2 = jnp.zeros((tm, D_MODEL), F32)
        for j in range(n_ff):
            cols = slice(SHARD_FF * j, SHARD_FF * (j + 1))
            df = lax.dot_general(df2, wdn_ref[cols, :], NT, preferred_element_type=F32)
            du = (2.0 * u_scr[:, cols] * df).astype(BF16)
            du_ref[:, cols] = du
            dh2 = dh2 + lax.dot_general(du, wup_ref[j], NT, preferred_element_type=F32)
        dgpre_ref[...] += jnp.sum(dh2 * x1hat, axis=0, keepdims=True)
        dx1_ref[...] = dx2 + _rms_bwd(x1hat, r1, gpre_ref[...], dh2)

    acc = pl.BlockSpec((1, D_MODEL), lambda i: (0, 0))
    return pl.pallas_call(
        body, name="mlp_fwd_bwd", grid=(S // tm,),
        in_specs=[_rows(tm, D_MODEL), _rows(tm, D_MODEL), _resident((1, D_MODEL)), _resident((1, D_MODEL)),
                  _resident((n_ff, D_MODEL, SHARD_FF)), _resident((D_FF, D_MODEL))],
        out_specs=[_rows(tm, D_MODEL), _rows(tm, D_FF), _rows(tm, D_FF), _rows(tm, D_MODEL), _rows(tm, D_MODEL),
                   acc, acc, pl.BlockSpec((1, 1), lambda i: (0, 0))],
        out_shape=[jax.ShapeDtypeStruct((S, D_MODEL), BF16), jax.ShapeDtypeStruct((S, D_FF), BF16),
                   jax.ShapeDtypeStruct((S, D_FF), BF16), jax.ShapeDtypeStruct((S, D_MODEL), BF16),
                   jax.ShapeDtypeStruct((S, D_MODEL), F32), jax.ShapeDtypeStruct((1, D_MODEL), F32),
                   jax.ShapeDtypeStruct((1, D_MODEL), F32), jax.ShapeDtypeStruct((1, 1), F32)],
        scratch_shapes=[pltpu.VMEM((tm, D_FF), F32)],
        compiler_params=_params("arbitrary"),
    )(x1, target, g_pre, g_post, w_up, w_down)


def _weight_grad(name, a, b, rows_sharded, after):
    S, K = a.shape
    N = b.shape[1]
    if rows_sharded:
        tk, tn = K // N_CHIPS, N
        a_spec = pl.BlockSpec((S, tk), lambda j: (0, j))
        b_spec = pl.BlockSpec((S, tn), lambda j: (0, 0), pipeline_mode=pl.Buffered(1))
    else:
        tk, tn = K, N // N_CHIPS
        a_spec = pl.BlockSpec((S, tk), lambda j: (0, 0), pipeline_mode=pl.Buffered(1))
        b_spec = pl.BlockSpec((S, tn), lambda j: (0, j))
    half = tk // 2

    def body(a_ref, b_ref, after_ref, o_ref):
        res = lax.dot_general(a_ref[...], b_ref[...], TN, preferred_element_type=F32)
        o_ref[0, 0] = res[:half]
        o_ref[1, 0] = res[half:]

    return pl.pallas_call(
        body, name=name, grid=(N_CHIPS,), in_specs=[a_spec, b_spec, pl.BlockSpec(memory_space=pl.ANY)],
        out_specs=pl.BlockSpec((2, 1, half, tn), lambda j: (0, j, 0, 0)),
        out_shape=jax.ShapeDtypeStruct((2, N_CHIPS, half, tn), F32),
        compiler_params=_params("parallel"),
    )(a, b, after)


def _weight_grad_w_in(h, dproj):
    S, K = h.shape
    step_w = 2 * 256
    n_steps = PROJ_W // step_w
    half = K // 2

    def body(a_ref, b_ref, o_ref):
        res = lax.dot_general(a_ref[...], b_ref[...], TN, preferred_element_type=F32)
        for step in range(n_steps):
            @pl.when(pl.program_id(0) == step)
            def _(step=step):
                lo = step * step_w
                while lo < (step + 1) * step_w:
                    chip = lo // SHARD_IN
                    hi = min((step + 1) * step_w, (chip + 1) * SHARD_IN)
                    for hh in range(2):
                        o_ref[hh, chip, :, lo - chip * SHARD_IN:hi - chip * SHARD_IN] = (
                            res[half * hh:half * (hh + 1), lo - step * step_w:hi - step * step_w])
                    lo = hi

    return pl.pallas_call(
        body, name="grad_w_in", grid=(n_steps,),
        in_specs=[pl.BlockSpec((S, K), lambda j: (0, 0), pipeline_mode=pl.Buffered(1)),
                  pl.BlockSpec((S, step_w), lambda j: (0, j))],
        out_specs=pl.BlockSpec((2, N_CHIPS, half, SHARD_IN), lambda j: (0, 0, 0, 0)),
        out_shape=jax.ShapeDtypeStruct((2, N_CHIPS, half, SHARD_IN), F32),
        compiler_params=_params("arbitrary"),
    )(h, dproj)


def _mixer_bwd(dx1, y2, ya, yx, bcu, qx, mkv, conv_w, g_a, g_c, g_x, w_out, g_post, after, tm):
    S = dx1.shape[0]
    n_mem = mkv.shape[0]
    n_tiles = S // tm

    def body(dx1_ref, y2_ref, ya_ref, yx_ref, bcu_ref, before_ref, qx_ref, mkv_ref, cw_ref, ga_ref, gc_ref, gx_ref,
             wo_ref, gp_ref, after_ref, dy2_ref, dya_ref, delta_ref, tail_ref, dmkv_ref, dcw_ref, dgp_ref, dga_ref,
             dgc_ref, dgx_ref, carry):
        step = pl.program_id(0)
        first_tile = step == n_tiles - 1

        @pl.when(step == 0)
        def _():
            for ref in (dmkv_ref, dcw_ref, dgp_ref, dga_ref, dgc_ref, dgx_ref, carry):
                ref[...] = jnp.zeros_like(ref)

        dx1 = dx1_ref[...]
        y2hat, r2 = _rms_hat(y2_ref[...])
        dgp_ref[...] += jnp.sum(dx1 * y2hat, axis=0, keepdims=True)
        dy2 = _rms_bwd(y2hat, r2, gp_ref[...], dx1).astype(BF16)
        dy2_ref[...] = dy2
        dycat = lax.dot_general(dy2, wo_ref[...], NT, preferred_element_type=F32)

        d_na = dycat[:, 0:ATTN_W]
        ya = ya_ref[...]
        yahat, ra = _rms_hat(ya)
        dga_ref[...] += jnp.sum(d_na * yahat, axis=0, keepdims=True)
        dya = _rms_bwd(yahat, ra, ga_ref[...], d_na)
        dya_ref[...] = dya
        prod = dya * ya
        hi = prod.astype(BF16)
        lo = (prod - hi.astype(F32)).astype(BF16)
        head_of = lambda axis: lax.shift_right_logical(lax.broadcasted_iota(jnp.int32, (ATTN_W, ATTN_W), axis),
                                                       HEAD.bit_length() - 1)
        ones = jnp.where(head_of(0) == head_of(1), 1.0, 0.0).astype(BF16)
        delta_ref[...] = jnp.dot(hi, ones, preferred_element_type=F32) + jnp.dot(lo, ones, preferred_element_type=F32)

        w = cw_ref[...]
        b, c, u, z, z1, z2, cv = _conv_fwd(bcu_ref[...], before_ref[...], first_tile, w)
        d_nc = dycat[:, ATTN_W:ATTN_W + CONV_W]
        ychat, rc = _rms_hat(b * cv)
        dgc_ref[...] += jnp.sum(d_nc * ychat, axis=0, keepdims=True)
        dyc = _rms_bwd(ychat, rc, gc_ref[...], d_nc)
        dcv = dyc * b
        behind = carry[...]
        dz = w[2:3, :] * dcv + w[1:2, :] * _shift_up(dcv, behind, 1) + w[0:1, :] * _shift_up(dcv, behind, 2)
        carry[...] = dcv[0:8, :]
        dcw_ref[0:1, :] += jnp.sum(dcv * z2, axis=0, keepdims=True)
        dcw_ref[1:2, :] += jnp.sum(dcv * z1, axis=0, keepdims=True)
        dcw_ref[2:3, :] += jnp.sum(dcv * z, axis=0, keepdims=True)
        tail_ref[:, 0:CONV_W] = (dyc * cv).astype(BF16)
        tail_ref[:, CONV_W:2 * CONV_W] = (dz * u).astype(BF16)
        tail_ref[:, 2 * CONV_W:3 * CONV_W] = (dz * c).astype(BF16)

        d_nx = dycat[:, ATTN_W + CONV_W:D_MODEL]
        yxhat, rx = _rms_hat(yx_ref[...])
        dgx_ref[...] += jnp.sum(d_nx * yxhat, axis=0, keepdims=True)
        dyx = _rms_bwd(yxhat, rx, gx_ref[...], d_nx)
        qxb, mkvb = qx_ref[...], mkv_ref[...]
        for hd in range(XATTN_W // HEAD):
            sl = slice(HEAD * hd, HEAD * (hd + 1))
            vsl = slice(XATTN_W + HEAD * hd, XATTN_W + HEAD * (hd + 1))
            s = lax.dot_general(qxb[:, sl], mkvb[:, sl], NT, preferred_element_type=F32) * SCALE
            e = jnp.exp(s - jnp.max(s, axis=1, keepdims=True))
            p = e / jnp.sum(e, axis=1, keepdims=True)
            dob = dyx[:, sl].astype(BF16)
            dp = lax.dot_general(dob, mkvb[:, vsl], NT, preferred_element_type=F32)
            ds = (p * (dp - jnp.sum(p * dp, axis=1, keepdims=True)) * SCALE).astype(BF16)
            tail_ref[:, 3 * CONV_W + HEAD * hd:3 * CONV_W + HEAD * (hd + 1)] = jnp.dot(
                ds, mkvb[:, sl], preferred_element_type=F32).astype(BF16)
            dmkv_ref[:, sl] += lax.dot_general(ds, qxb[:, sl], TN, preferred_element_type=F32)
            dmkv_ref[:, vsl] += lax.dot_general(p.astype(BF16), dob, TN, preferred_element_type=F32)

    rows = lambda width: pl.BlockSpec((tm, width), lambda i: (n_tiles - 1 - i, 0))
    before = pl.BlockSpec((8, 3 * CONV_W), lambda i: (jnp.maximum((n_tiles - 1 - i) * (tm // 8) - 1, 0), 0))
    acc = lambda r, w: pl.BlockSpec((r, w), lambda i: (0, 0))
    return pl.pallas_call(
        body, name="mixer_bwd", grid=(n_tiles,),
        in_specs=[rows(D_MODEL), rows(D_MODEL), rows(ATTN_W), rows(XATTN_W), rows(3 * CONV_W), before, rows(XATTN_W),
                  _resident((n_mem, 2 * XATTN_W)), _resident((3, CONV_W)), _resident((1, ATTN_W)),
                  _resident((1, CONV_W)), _resident((1, XATTN_W)), _resident((D_MODEL, D_MODEL)),
                  _resident((1, D_MODEL)), pl.BlockSpec(memory_space=pl.ANY)],
        out_specs=[rows(D_MODEL), rows(ATTN_W), rows(ATTN_W), rows(3 * CONV_W + XATTN_W), acc(n_mem, 2 * XATTN_W),
                   acc(3, CONV_W), acc(1, D_MODEL), acc(1, ATTN_W), acc(1, CONV_W), acc(1, XATTN_W)],
        out_shape=[jax.ShapeDtypeStruct((S, D_MODEL), BF16), jax.ShapeDtypeStruct((S, ATTN_W), F32),
                   jax.ShapeDtypeStruct((S, ATTN_W), F32), jax.ShapeDtypeStruct((S, 3 * CONV_W + XATTN_W), BF16),
                   jax.ShapeDtypeStruct((n_mem, 2 * XATTN_W), F32), jax.ShapeDtypeStruct((3, CONV_W), F32),
                   jax.ShapeDtypeStruct((1, D_MODEL), F32), jax.ShapeDtypeStruct((1, ATTN_W), F32),
                   jax.ShapeDtypeStruct((1, CONV_W), F32), jax.ShapeDtypeStruct((1, XATTN_W), F32)],
        scratch_shapes=[pltpu.VMEM((8, CONV_W), F32)],
        compiler_params=_params("arbitrary"),
    )(dx1, y2, ya, yx, bcu, bcu, qx, mkv, conv_w, g_a, g_c, g_x, w_out, g_post, after)


def _memkv_bwd(mem, g_mem, w_kv, dmkv):
    n_mem = mem.shape[0]
    half = D_MODEL // N_CHIPS // 2

    def body(mem_ref, g_ref, w_ref, d_ref, dw_ref, dg_ref):
        mhat, _ = _rms_hat(mem_ref[...])
        mn = (mhat * g_ref[...]).astype(BF16)
        d = d_ref[...].astype(BF16)
        for k in range(2 * N_CHIPS):
            dw_ref[k % 2, k // 2] = lax.dot_general(mn[:, half * k:half * (k + 1)], d, TN, preferred_element_type=F32)
        dmn = lax.dot_general(d, w_ref[...], NT, preferred_element_type=F32)
        dg_ref[...] = jnp.sum(dmn * mhat, axis=0, keepdims=True)

    return pl.pallas_call(
        body, name="memkv_bwd",
        out_shape=[jax.ShapeDtypeStruct((2, N_CHIPS, half, 2 * XATTN_W), F32), jax.ShapeDtypeStruct((1, D_MODEL), F32)],
        compiler_params=pltpu.CompilerParams(vmem_limit_bytes=VMEM_LIMIT_V7X),
    )(mem, g_mem, w_kv, dmkv)


def _in_proj_bwd(dqkv, tail, cos, sin, w_in, x, g, dx1, after, tm):
    S = x.shape[0]

    def body(dq_ref, dk_ref, dv_ref, tail_ref, cos_ref, sin_ref, w_hbm, x_ref, g_ref, dx1_ref, after_ref,
             dproj_ref, dx_ref, dg_ref, w_full, sems):
        _side_by_side(w_hbm, w_full, sems)

        @pl.when(pl.program_id(0) == 0)
        def _():
            dg_ref[...] = jnp.zeros_like(dg_ref)

        halves = [slice(0, tm // 2), slice(tm // 2, tm)]
        for rows in halves:
            c, s = cos_ref[rows, :], sin_ref[rows, :]
            for j in range(ATTN_W // 128):
                cols = slice(128 * j, 128 * (j + 1))
                dproj_ref[rows, cols] = _rope128(dq_ref[rows, cols] * SCALE, c, s, True).astype(BF16)
                dproj_ref[rows, ATTN_W + 128 * j:ATTN_W + 128 * (j + 1)] = _rope128(dk_ref[rows, cols], c, s, True).astype(BF16)
            dproj_ref[rows, 2 * ATTN_W:3 * ATTN_W] = dv_ref[rows, :].astype(BF16)
            dproj_ref[rows, 3 * ATTN_W:PROJ_W] = tail_ref[rows, :]
        dhs = [lax.dot_general(dproj_ref[rows, :], w_full[...], NT, preferred_element_type=F32) for rows in halves]
        for rows, dh in zip(halves, dhs):
            xhat, r = _rms_hat(x_ref[rows, :])
            dg_ref[...] += jnp.sum(dh * xhat, axis=0, keepdims=True)
            dx_ref[rows, :] = dx1_ref[rows, :] + _rms_bwd(xhat, r, g_ref[...], dh)

    return pl.pallas_call(
        body, name="in_proj_bwd", grid=(S // tm,),
        in_specs=[_rows(tm, ATTN_W)] * 3 + [_rows(tm, PROJ_W - 3 * ATTN_W), _rows(tm, 128), _rows(tm, 128),
                  pl.BlockSpec(memory_space=pl.ANY), _rows(tm, D_MODEL), _resident((1, D_MODEL)),
                  _rows(tm, D_MODEL), pl.BlockSpec(memory_space=pl.ANY)],
        out_specs=[_rows(tm, PROJ_W), _rows(tm, D_MODEL), pl.BlockSpec((1, D_MODEL), lambda i: (0, 0))],
        out_shape=[jax.ShapeDtypeStruct((S, PROJ_W), BF16), jax.ShapeDtypeStruct((S, D_MODEL), F32),
                   jax.ShapeDtypeStruct((1, D_MODEL), F32)],
        scratch_shapes=[pltpu.VMEM((D_MODEL, PROJ_W), BF16), pltpu.SemaphoreType.DMA((N_CHIPS,))],
        compiler_params=_params("arbitrary"),
    )(*dqkv, tail, cos, sin, w_in, x, g, dx1, after)


def _row_tile(rows):
    return ROW_TILE if rows % ROW_TILE == 0 else rows


def _chip_sums_bf16(name, grads, from_sibling, place):
    k = len(grads)
    _, n, rows, _ = grads[0].shape
    tr = _row_tile(rows)

    def body(place_ref, *refs):
        for g_ref, b_ref, o_ref in zip(refs[:k], refs[k:2 * k], refs[2 * k:]):
            o_ref[...] = (g_ref[0] + b_ref[...]).astype(BF16)

    mine = lambda g: pl.BlockSpec((1, 1, tr, g.shape[3]), lambda s, i, p: (p[0], s, i, 0))
    slab = lambda g: pl.BlockSpec((1, tr, g.shape[3]), lambda s, i, p: (s, i, 0))
    return pl.pallas_call(
        body, name=name, out_shape=[jax.ShapeDtypeStruct(g.shape[1:], BF16) for g in grads],
        grid_spec=pltpu.PrefetchScalarGridSpec(
            num_scalar_prefetch=1, grid=(n, rows // tr),
            in_specs=[mine(g) for g in grads] + [slab(g) for g in grads], out_specs=[slab(g) for g in grads]),
        compiler_params=_params("parallel", "parallel"),
    )(place, *grads, *from_sibling)


def _final_sums(name, grads, from_sibling, others, place):
    k = len(grads)
    rows = grads[0].shape[2]
    tr = _row_tile(rows)

    def body(place_ref, *refs):
        for a in range(k):
            own_ref, sib_ref = refs[a], refs[k + a]
            acc = own_ref[0, 0] + sib_ref[0]
            for o in refs[2 * k + 3 * a:2 * k + 3 * a + 3]:
                acc = acc + o[0].astype(F32)
            refs[5 * k + a][0] = acc

    own = lambda g: pl.BlockSpec((1, 1, tr, g.shape[3]), lambda i, p: (p[0], p[1], i, 0))
    sib = lambda g: pl.BlockSpec((1, tr, g.shape[3]), lambda i, p: (p[1], i, 0))
    other = lambda g, j: pl.BlockSpec((1, tr, g.shape[3]), lambda i, p: (j, i, 0))
    return pl.pallas_call(
        body, name=name, out_shape=[jax.ShapeDtypeStruct((2,) + g.shape[2:], F32) for g in grads],
        grid_spec=pltpu.PrefetchScalarGridSpec(
            num_scalar_prefetch=1, grid=(rows // tr,),
            in_specs=[own(g) for g in grads] + [sib(g) for g in grads] + [other(g, j) for g in grads for j in range(3)],
            out_specs=[pl.BlockSpec((1, tr, g.shape[3]), lambda i, p: (p[0], i, 0)) for g in grads]),
        compiler_params=_params("parallel"),
    )(place, *grads, *from_sibling, *[o for o in others for _ in range(3)])


def _adamw_update(w, g, m, v):
    m = ADAM_B1 * m + (1.0 - ADAM_B1) * g
    v = ADAM_B2 * v + (1.0 - ADAM_B2) * (g * g)
    m_hat = m * (1.0 / (1.0 - ADAM_B1 ** ADAM_STEP))
    v_hat = v * (1.0 / (1.0 - ADAM_B2 ** ADAM_STEP))
    return -ADAM_LR * (m_hat / (jnp.sqrt(v_hat) + ADAM_EPS) + ADAM_WD * w), m, v


def _adamw(name, params, after):
    k = len(params)
    rows = params[0][0].shape[0]
    tr = ADAMW_ROW_TILE if rows % ADAMW_ROW_TILE == 0 else rows

    def body(*refs):
        ins, outs = refs[:4 * k], refs[4 * k + 1:]
        for a in range(k):
            w_ref, g_ref, m_ref, v_ref = ins[4 * a:4 * a + 4]
            g = g_ref[...]
            outs[4 * a][...] = g
            outs[4 * a + 1][...], outs[4 * a + 2][...], outs[4 * a + 3][...] = _adamw_update(w_ref[...], g, m_ref[...], v_ref[...])

    spec = lambda w: pl.BlockSpec((tr, w.shape[1]), lambda i: (i, 0))
    out = pl.pallas_call(
        body, name=name, grid=(rows // tr,),
        in_specs=[spec(p[0]) for p in params for _ in range(4)] + [pl.BlockSpec(memory_space=pl.ANY)],
        out_specs=[spec(p[0]) for p in params for _ in range(4)],
        out_shape=[jax.ShapeDtypeStruct(p[0].shape, F32) for p in params for _ in range(4)],
        compiler_params=_params("parallel"),
    )(*[t for p in params for t in p], after)
    return [out[4 * a:4 * a + 4] for a in range(k)]


def _small_update(summed, chip, gains, gains_m, gains_v, taps, taps_m, taps_v):
    n = len(gains)
    widths = [g.shape[1] for g in gains]
    k, w = taps.shape

    def body(*refs):
        chip_ref, sum_ref = refs[0], refs[1]
        params = [refs[2 + 3 * i:5 + 3 * i] for i in range(n + 1)]
        outs = [refs[2 + 3 * (n + 1) + 4 * i:2 + 3 * (n + 1) + 4 * (i + 1)] for i in range(n + 1)]
        loss_ref = refs[-1]
        for i in range(n):
            g = sum_ref[i:i + 1, 0:widths[i]]
            wr, mr, vr = params[i]
            outs[i][0][...] = g
            outs[i][1][...], outs[i][2][...], outs[i][3][...] = _adamw_update(wr[...], g, mr[...], vr[...])
        g = sum_ref[n:n + k, 0:w]
        for j in range(1, N_CHIPS):
            g = jnp.where(chip_ref[0] == j, sum_ref[n:n + k, w * j:w * (j + 1)], g)
        wr, mr, vr = params[n]
        outs[n][0][...] = g
        outs[n][1][...], outs[n][2][...], outs[n][3][...] = _adamw_update(wr[...], g, mr[...], vr[...])
        loss_ref[...] = sum_ref[n + k:n + k + 1, 0:1]

    vmem = pl.BlockSpec(memory_space=pltpu.VMEM)
    operands = [chip, summed]
    for p in zip(list(gains) + [taps], list(gains_m) + [taps_m], list(gains_v) + [taps_v]):
        operands += list(p)
    shapes = [jax.ShapeDtypeStruct(p.shape, F32) for p in list(gains) + [taps] for _ in range(4)]
    out = pl.pallas_call(
        body, name="small_update", out_shape=shapes + [jax.ShapeDtypeStruct((1, 1), F32)],
        in_specs=[pl.BlockSpec(memory_space=pltpu.SMEM)] + [vmem] * (len(operands) - 1),
        out_specs=[vmem] * (len(shapes) + 1),
    )(*operands)
    return [out[4 * i:4 * (i + 1)] for i in range(n + 1)], out[-1]


def _sum_blocks(name, blocks):
    n, rows, cols = blocks.shape

    def body(b_ref, o_ref):
        acc = b_ref[0]
        for k in range(1, n):
            acc = acc + b_ref[k]
        o_ref[...] = acc

    return pl.pallas_call(body, name=name, out_shape=jax.ShapeDtypeStruct((rows, cols), F32))(blocks)


def _place():
    return lax.axis_index("x"), lax.axis_index("y"), lax.axis_index("c")


def _other_chips(x, y):
    return [(1 - x, y), (x, 1 - y), (1 - x, 1 - y)]


def _allgather_finish(name, shards, landed, pass_on):
    n = len(shards)

    def body(*refs):
        ins, outs, stage = refs[:n], refs[2 * n:3 * n], refs[3 * n:4 * n]
        send_sems, recv_sems, local_sems = refs[4 * n:]
        x, y, c = _place()
        chips = _other_chips(x, y)

        def copy(a, k, chip, half):
            place = outs[a].at[2 * chip[0] + chip[1], half]
            return pltpu.make_async_remote_copy(
                src_ref=place, dst_ref=place, send_sem=send_sems.at[3 * a + k], recv_sem=recv_sems.at[3 * a + k],
                device_id=(x, y, 1 - c), device_id_type=MESH)

        load = [pltpu.make_async_copy(ins[a], stage[a], local_sems.at[a]) for a in range(n)]
        local = [pltpu.make_async_copy(stage[a], outs[a].at[2 * x + y], local_sems.at[a]) for a in range(n)]
        for cp in load:
            cp.start()
        passed = [copy(a, k, chip, c) for a in range(n) if pass_on[a] for k, chip in enumerate(chips)]
        for cp in passed:
            cp.start()
        for a in range(n):
            load[a].wait()
            local[a].start()
        for a in range(n):
            if pass_on[a]:
                for k, chip in enumerate(chips):
                    copy(a, k, chip, 1 - c).wait_recv()
        for cp in passed:
            cp.wait_send()
        for cp in local:
            cp.wait()

    any_spec = pl.BlockSpec(memory_space=pl.ANY)
    return pl.pallas_call(
        body, name=name,
        out_shape=[jax.ShapeDtypeStruct((N_CHIPS,) + s.shape, s.dtype) for s in shards],
        in_specs=[any_spec] * (2 * n), out_specs=[any_spec] * n,
        input_output_aliases={n + a: a for a in range(n)},
        scratch_shapes=[pltpu.VMEM(s.shape, s.dtype) for s in shards]
        + [pltpu.SemaphoreType.DMA((3 * n,)), pltpu.SemaphoreType.DMA((3 * n,)), pltpu.SemaphoreType.DMA((n,))],
        compiler_params=pltpu.CompilerParams(vmem_limit_bytes=VMEM_LIMIT_V7X),
    )(*shards, *landed)


def _plan_first_hop(x, y, c, shards, lands):
    return [(shards[a].at[c], lands[a].at[2 * x + y, c], lands[a].at[2 * chip[0] + chip[1], c], (*chip, c))
            for a in range(len(shards)) for chip in _other_chips(x, y)]


def _plan_pass_on(x, y, c, nothing, lands):
    def place(a, chip, half):
        return lands[a].at[2 * chip[0] + chip[1], half]

    return [(place(a, chip, c), place(a, chip, c), place(a, chip, 1 - c), (x, y, 1 - c))
            for a in range(len(lands)) for chip in _other_chips(x, y)]


def _plan_own_half_to_sibling(x, y, c, nothing, lands):
    return [(lands[a].at[c], lands[a].at[c], lands[a].at[1 - c], (x, y, 1 - c)) for a in range(len(lands))]


def _plan_other_half_to_sibling(x, y, c, grads, lands):
    return [(grads[a].at[1 - c], lands[a], lands[a], (x, y, 1 - c)) for a in range(len(grads))]


def _plan_to_other_chips(x, y, c, partials, lands):
    return [(partials[a].at[2 * chip[0] + chip[1]], lands[a].at[k], lands[a].at[k], (*chip, c))
            for a in range(len(partials)) for k, chip in enumerate(_other_chips(x, y))]


def _plan_to_all(x, y, c, blocks, lands):
    flips = [(fx, fy, fc) for fx in (0, 1) for fy in (0, 1) for fc in (0, 1) if (fx, fy, fc) != (0, 0, 0)]
    peers = [(1 - x if fx else x, 1 - y if fy else y, 1 - c if fc else c) for fx, fy, fc in flips]
    return [(blocks[0], lands[0].at[4 * x + 2 * y + c], lands[0].at[4 * p[0] + 2 * p[1] + p[2]], p) for p in peers]


def _planned_copies(plan, srcs, lands, send_sems, recv_sems):
    x, y, c = _place()

    def pair(k, src, there, here, to):
        make = lambda dst: pltpu.make_async_remote_copy(
            src_ref=src, dst_ref=dst, send_sem=send_sems.at[k], recv_sem=recv_sems.at[k], device_id=to, device_id_type=MESH)
        return make(there), make(here)

    return [pair(k, *entry) for k, entry in enumerate(plan(x, y, c, srcs, lands))]


_HBM_SPEC = pl.BlockSpec(memory_space=pltpu.HBM)
_SEM_SPEC = pl.BlockSpec(memory_space=pltpu.SEMAPHORE)


def _hbm(a):
    return pltpu.with_memory_space_constraint(a, pltpu.HBM)


def _exchange_start(name, plan, n_copies, srcs, land_shapes, after, lands=None):
    if lands is None:
        lands = [lax.empty(s.shape, s.dtype) for s in land_shapes]
    land_shapes = lands
    ns, nl = len(srcs), len(land_shapes)
    n_in = ns + nl + 1

    def body(*refs):
        for send, _ in _planned_copies(plan, refs[:ns], refs[ns:ns + nl], refs[n_in], refs[n_in + 1]):
            send.start()
        refs[-1][...] = jnp.zeros_like(refs[-1])

    out = pl.pallas_call(
        body, name=name,
        out_shape=(pltpu.SemaphoreType.DMA((n_copies,)), pltpu.SemaphoreType.DMA((n_copies,)),
                   *[pltpu.HBM(s.shape, s.dtype) for s in land_shapes], jax.ShapeDtypeStruct((8, 128), F32)),
        in_specs=[_HBM_SPEC] * (ns + nl) + [pl.BlockSpec(memory_space=pl.ANY)],
        out_specs=(_SEM_SPEC, _SEM_SPEC, *[_HBM_SPEC] * nl, pl.BlockSpec(memory_space=pltpu.VMEM)),
        input_output_aliases={ns + i: 2 + i for i in range(nl)},
        compiler_params=pltpu.CompilerParams(has_side_effects=pltpu.SideEffectType.DATAFLOW_SIDE_EFFECTING),
    )(*[_hbm(s) for s in srcs], *[_hbm(l) for l in lands], after)
    return out[0], out[1], list(out[2:2 + nl]), out[-1]


def _exchange_wait(name, plan, srcs, started, after):
    send_sems, recv_sems, lands, _ = started
    ns, nl = len(srcs), len(lands)
    after = list(after) if isinstance(after, (list, tuple)) else [after]

    def body(*refs):
        for send, recv in _planned_copies(plan, refs[:ns], refs[ns:ns + nl], refs[ns + nl], refs[ns + nl + 1]):
            send.wait_send()
            recv.wait_recv()

    return pl.pallas_call(
        body, name=name, out_shape=[pltpu.HBM(l.shape, l.dtype) for l in lands],
        in_specs=[_HBM_SPEC] * (ns + nl) + [_SEM_SPEC, _SEM_SPEC] + [pl.BlockSpec(memory_space=pl.ANY)] * len(after),
        out_specs=[_HBM_SPEC] * nl, input_output_aliases={ns + i: i for i in range(nl)},
        compiler_params=pltpu.CompilerParams(has_side_effects=pltpu.SideEffectType.DATAFLOW_SIDE_EFFECTING),
    )(*[_hbm(s) for s in srcs], *lands, send_sems, recv_sems, *after)


def _like(arrays, lead, dtype=None):
    return [jax.ShapeDtypeStruct(tuple(lead) + a.shape[-2:], dtype or a.dtype) for a in arrays]


class _StepExchanges:
    def __init__(self, mats, conv_w):
        x, y, c = _place()
        self.place = jnp.stack([c, 2 * x + y]).astype(jnp.int32)
        shards = [w.astype(BF16).reshape(2, w.shape[0] // 2, w.shape[1]) for w in mats]
        self._in_shard = shards[:1]
        self._in = _exchange_start("w_in_allgather_start", _plan_first_hop, 3, self._in_shard,
                                   _like(self._in_shard, (N_CHIPS, 2)), shards[0])
        self.zero = self._in[3]
        taps = jnp.pad(conv_w, ((0, 8 - conv_w.shape[0]), (0, 128 - conv_w.shape[1])))
        self._rest_shards = shards[1:] + [jnp.stack([taps, jnp.zeros_like(taps)])]
        self._taps_shape = conv_w.shape
        self._groups = {}

    def w_in(self, after):
        landed = _exchange_wait("w_in_allgather_wait", _plan_first_hop, self._in_shard, self._in,
                                list(after) + self._rest_shards)
        (w_in,) = _allgather_finish("w_in_allgather_finish", self._in_shard, landed, [True])
        self._rest = _exchange_start("rest_allgather_start", _plan_first_hop, 3 * len(self._rest_shards),
                                     self._rest_shards, _like(self._rest_shards, (N_CHIPS, 2)), w_in)
        self.zero = self._rest[3]
        return w_in.reshape(N_CHIPS, 2 * w_in.shape[2], w_in.shape[3])

    def rest_weights(self, after):
        landed = _exchange_wait("rest_allgather_wait", _plan_first_hop, self._rest_shards, self._rest, after)
        kv, out, up, down, taps = _allgather_finish("rest_allgather_finish", self._rest_shards, landed,
                                                    [True, True, False, False, True])
        self._up_down = _exchange_start("up_down_pass_on_start", _plan_pass_on, 6, [], None, self.zero, lands=[up, down])
        self.zero = self._up_down[3]
        k, w = self._taps_shape
        taps = taps[:, 0, :k, :w].transpose(1, 0, 2).reshape(k, N_CHIPS * w)
        return [g.reshape(N_CHIPS, 2 * g.shape[2], g.shape[3]) for g in (kv, out)], taps

    def up_down(self, after):
        full = _exchange_wait("up_down_pass_on_wait", _plan_pass_on, [], self._up_down, after)
        return [g.reshape(N_CHIPS, 2 * g.shape[2], g.shape[3]) for g in full]

    def send_grads(self, key, grads):
        grads = list(grads)
        started = _exchange_start(f"{key}_grads_to_sibling_start", _plan_other_half_to_sibling, len(grads), grads,
                                  _like(grads, (N_CHIPS,)), self.zero)
        self._groups[key] = dict(grads=grads, to_sibling=started)
        self.zero = started[3]

    def grads_at_sibling(self, key, after):
        group = self._groups[key]
        grads = group["grads"]
        group["from_sibling"] = _exchange_wait(f"{key}_grads_to_sibling_wait", _plan_other_half_to_sibling, grads,
                                               group["to_sibling"], after)
        group["partials"] = _chip_sums_bf16(f"{key}_chip_sums", grads, group["from_sibling"], self.place)
        group["to_chips"] = _exchange_start(f"{key}_grads_to_chips_start", _plan_to_other_chips, 3 * len(grads),
                                            group["partials"], _like(group["partials"], (3,)), self.zero)
        self.zero = group["to_chips"][3]

    def grads_summed(self, key, after):
        group = self._groups[key]
        from_chips = _exchange_wait(f"{key}_grads_to_chips_wait", _plan_to_other_chips, group["partials"],
                                    group["to_chips"], after)
        return _final_sums(f"{key}_final_sums", group["grads"], group["from_sibling"], from_chips, self.place)

    def send_sums(self, key, sums):
        self._groups[key + "_sums"] = _exchange_start(f"{key}_sums_to_sibling_start", _plan_own_half_to_sibling,
                                                      len(sums), [], None, self.zero, lands=list(sums))
        self.zero = self._groups[key + "_sums"][3]

    def whole_sums(self, key, after):
        full = _exchange_wait(f"{key}_sums_to_sibling_wait", _plan_own_half_to_sibling, [], self._groups[key + "_sums"], after)
        return [t.reshape(2 * t.shape[1], t.shape[2]) for t in full]

    def send_small(self, block):
        self._small = block
        self._small_started = _exchange_start("small_grads_start", _plan_to_all, 7, [block],
                                              [jax.ShapeDtypeStruct((8,) + block.shape, block.dtype)], self.zero)
        self.zero = self._small_started[3]

    def small_summed(self, after):
        x, y, c = _place()
        (landed,) = _exchange_wait("small_grads_wait", _plan_to_all, [self._small], self._small_started, after)
        blocks = lax.dynamic_update_index_in_dim(landed, self._small, 4 * x + 2 * y + c, 0)
        return _sum_blocks("small_sum", blocks)


def _rope_tables(positions):
    half = HEAD // 2
    inv_freq = jnp.float32(ROPE_THETA) ** (-(jnp.arange(half, dtype=F32) * 2.0 / HEAD))
    ang = positions.astype(F32)[:, None] * inv_freq
    cos, sin = jnp.cos(ang), jnp.sin(ang)
    return jnp.tile(cos, (1, 4)), jnp.tile(jnp.concatenate([-sin, sin], axis=1), (1, 2))


def _local_step(x, mem, positions, target, gains, ex):
    g_pre_mix, g_mem, g_a, g_c, g_x, g_post_mix, g_pre_mlp, g_post_mlp = gains
    tm = ROW_TILE
    cos, sin = _rope_tables(positions)
    h = _pre_norm(x, g_pre_mix, ex.zero, tm)
    w_in = ex.w_in([h, cos, sin])

    q, k, v, bcu, qx = _in_proj_fwd(h, w_in, cos, sin, ex.zero, tm)
    ya, lse = _attn_fwd(q, k, v)
    (w_kv, w_out), conv_w = ex.rest_weights(lse)
    w_kv, w_out = (w.reshape(N_CHIPS * w.shape[1], w.shape[2]) for w in (w_kv, w_out))
    memn, mkv = _memkv_fwd(mem, g_mem, w_kv, ex.zero)
    yx, ycat, y2, x1 = _mix_fwd(ya, bcu, qx, mkv, conv_w, g_a, g_c, g_x, w_out, g_post_mix, x, tm)
    w_up, w_down = ex.up_down(x1)
    w_down = w_down.reshape(N_CHIPS * w_down.shape[1], w_down.shape[2])
    h2, f, du, df2, dx1, dg_pre_mlp, dg_post_mlp, loss = _mlp_fwd_bwd(x1, target, g_pre_mlp, g_post_mlp, w_up, w_down,
                                                                      MLP_ROW_TILE)
    gw_down = _weight_grad("grad_w_down", f, df2, True, ex.zero)
    gw_up = _weight_grad("grad_w_up", h2, du, False, ex.zero)
    ex.send_grads("early", [gw_up, gw_down])

    dy2, dya, delta, tail, dmkv, g_conv, dg_post_mix, dg_a, dg_c, dg_x = _mixer_bwd(
        dx1, y2, ya, yx, bcu, qx, mkv, conv_w, g_a, g_c, g_x, w_out, g_post_mix, ex.zero, tm)
    ex.grads_at_sibling("early", dy2)
    gw_out = _weight_grad("grad_w_out", ycat, dy2, True, ex.zero)
    gw_kv, dg_mem = _memkv_bwd(mem, g_mem, w_kv, dmkv)
    ex.send_grads("mid", [gw_out, gw_kv])
    dqkv = _attn_bwd(q, k, v, dya, lse, delta, ex.zero)
    ex.grads_at_sibling("mid", dqkv[0])
    dproj, grad_x, dg_pre_mix = _in_proj_bwd(dqkv, tail, cos, sin, w_in, x, g_pre_mix, dx1, ex.zero, tm)
    gain_grads = [dg_pre_mix, dg_mem, dg_a, dg_c, dg_x, dg_post_mix, dg_pre_mlp, dg_post_mlp]
    ex.send_small(_pack_small(gain_grads, g_conv, loss))
    gw_in = _weight_grad_w_in(h, dproj)
    ex.send_grads("late", [gw_in])
    return grad_x


def _pack_small(gains, conv, scalar=None):
    rows = [jnp.pad(g, ((0, 0), (0, D_MODEL - g.shape[1]))) for g in gains]
    rows.append(jnp.pad(conv, ((0, 0), (0, D_MODEL - conv.shape[1]))))
    last = jnp.zeros((SMALL_ROWS - 8 - conv.shape[0], D_MODEL), F32)
    rows.append(last if scalar is None else last.at[0:1, 0:1].set(scalar))
    return jnp.concatenate(rows, axis=0)


def _unpack_small(block, gain_widths, conv_width):
    gains = [block[i:i + 1, :w] for i, w in enumerate(gain_widths)]
    return gains, block[8:11, :conv_width], block[11, 0]


def kernel(x, mem, positions, g_pre_mix, g_mem, w_in, w_mem_kv, conv_w, g_attn_out, g_conv_out, g_xattn_out, w_out, g_post_mix, g_pre_mlp, w_up, w_down, g_post_mlp, loss_target, m_g_pre_mix, m_g_mem, m_w_in, m_w_mem_kv, m_conv_w, m_g_attn_out, m_g_conv_out, m_g_xattn_out, m_w_out, m_g_post_mix, m_g_pre_mlp, m_w_up, m_w_down, m_g_post_mlp, v_g_pre_mix, v_g_mem, v_w_in, v_w_mem_kv, v_conv_w, v_g_attn_out, v_g_conv_out, v_g_xattn_out, v_w_out, v_g_post_mix, v_g_pre_mlp, v_w_up, v_w_down, v_g_post_mlp):
    cx, cy, cc = _place()
    chip = 2 * cx + cy
    gains = [g_pre_mix, g_mem, g_attn_out, g_conv_out, g_xattn_out, g_post_mix, g_pre_mlp, g_post_mlp]
    gains_m = [m_g_pre_mix, m_g_mem, m_g_attn_out, m_g_conv_out, m_g_xattn_out, m_g_post_mix, m_g_pre_mlp, m_g_post_mlp]
    gains_v = [v_g_pre_mix, v_g_mem, v_g_attn_out, v_g_conv_out, v_g_xattn_out, v_g_post_mix, v_g_pre_mlp, v_g_post_mlp]
    gain_widths = [g.shape[1] for g in gains]
    mats = [w_in[0], w_mem_kv[0], w_out[0], w_up[0], w_down[0]]
    mats_m = [m_w_in[0], m_w_mem_kv[0], m_w_out[0], m_w_up[0], m_w_down[0]]
    mats_v = [v_w_in[0], v_w_mem_kv[0], v_w_out[0], v_w_up[0], v_w_down[0]]

    ex = _StepExchanges(mats, conv_w[0])
    grad_x = _local_step(x[0], mem[0], positions[0], loss_target[0], gains, ex)

    ex.send_sums("four", ex.grads_summed("early", ex.zero) + ex.grads_summed("mid", ex.zero))
    ex.grads_at_sibling("late", ex.zero)
    up_sum, down_sum, out_sum, kv_sum = ex.whole_sums("four", ex.zero)
    params = lambda a, g: (mats[a], g, mats_m[a], mats_v[a])
    new_up, new_down = _adamw("adamw_up_down", [params(3, up_sum), params(4, down_sum)], ex.zero)
    new_out, new_kv = _adamw("adamw_out_kv", [params(2, out_sum), params(1, kv_sum)], ex.zero)

    small, total = _small_update(ex.small_summed(new_kv[1]), chip.reshape(1).astype(jnp.int32), gains, gains_m,
                                 gains_v, conv_w[0], m_conv_w[0], v_conv_w[0])

    ex.send_sums("last", ex.grads_summed("late", small[0][1]))
    (in_sum,) = ex.whole_sums("last", ex.zero)
    (new_in,) = _adamw("adamw_in", [params(0, in_sum)], in_sum)
    mat_new = [new_in, new_kv, new_out, new_up, new_down]

    order = ["g_pre_mix", "g_mem", "w_in", "w_mem_kv", "conv_w", "g_attn_out", "g_conv_out", "g_xattn_out", "w_out",
             "g_post_mix", "g_pre_mlp", "w_up", "w_down", "g_post_mlp"]
    gain_names = ["g_pre_mix", "g_mem", "g_attn_out", "g_conv_out", "g_xattn_out", "g_post_mix", "g_pre_mlp", "g_post_mlp"]
    mat_names = ["w_in", "w_mem_kv", "w_out", "w_up", "w_down"]

    def leaf(kind, name):
        if name in gain_names:
            return small[gain_names.index(name)][kind]
        if name == "conv_w":
            return small[len(gain_names)][kind][None]
        return mat_new[mat_names.index(name)][kind][None]

    return (total[0, 0], grad_x[None], *[leaf(kind, name) for kind in range(4) for name in order])
```

```python
import jax
import jax.numpy as jnp
from jax import lax
from jax.experimental import pallas as pl
from jax.experimental.pallas import tpu as pltpu

F32, BF16 = jnp.float32, jnp.bfloat16

D_MODEL = 1024
ATTN_W = 512
CONV_W = 256
XATTN_W = 256
PROJ_W = 3 * ATTN_W + 3 * CONV_W + XATTN_W
D_FF = 4096
HEAD = 64
N_BACK = 128
DILATIONS = (1, 4, 16)
ROPE_THETA = 10000.0
EPS = 1e-6
NEG_INF = -1e30
SCALE = HEAD ** -0.5
N_CHIPS = 4
SHARD_IN = PROJ_W // N_CHIPS
SHARD_FF = D_FF // N_CHIPS

ADAM_LR, ADAM_B1, ADAM_B2, ADAM_EPS, ADAM_WD, ADAM_STEP = 0.001, 0.9, 0.999, 1e-08, 0.01, 10

VMEM_LIMIT_V7X = 56 * 1024 * 1024
ROW_TILE = 512
MLP_ROW_TILE = 256
ADAMW_ROW_TILE = 256
SMALL_ROWS = 16

NT = (((1,), (1,)), ((), ()))
TN = (((0,), (0,)), ((), ()))
MESH = pl.DeviceIdType.MESH


def _params(*sem):
    return pltpu.CompilerParams(dimension_semantics=sem, vmem_limit_bytes=VMEM_LIMIT_V7X)


def _resident(shape):
    return pl.BlockSpec(shape, lambda *_: (0,) * len(shape), pipeline_mode=pl.Buffered(1))


def _rows(tm, width):
    return pl.BlockSpec((tm, width), lambda i: (i, 0))


def _rms_hat(x):
    r = lax.rsqrt(jnp.mean(x * x, axis=-1, keepdims=True) + EPS)
    return x * r, r


def _rms_bwd(xhat, r, g, dy):
    gdy = dy * g
    return r * (gdy - xhat * jnp.mean(xhat * gdy, axis=-1, keepdims=True))


def _rope128(t, cos, sin_signed, inverse):
    lane = lax.broadcasted_iota(jnp.int32, t.shape, 1)
    first_half = (lane % HEAD) < (HEAD // 2)
    rot = jnp.where(first_half, pltpu.roll(t, 128 - HEAD // 2, 1), pltpu.roll(t, HEAD // 2, 1))
    return t * cos - rot * sin_signed if inverse else t * cos + rot * sin_signed


def _pre_norm(x, g, after, tm):
    S = x.shape[0]

    def body(x_ref, g_ref, after_ref, h_ref):
        h_ref[...] = (_rms_hat(x_ref[...])[0] * g_ref[...]).astype(BF16)

    return pl.pallas_call(
        body, name="pre_norm", grid=(S // tm,),
        in_specs=[_rows(tm, D_MODEL), _resident((1, D_MODEL)), pl.BlockSpec(memory_space=pl.ANY)],
        out_specs=_rows(tm, D_MODEL), out_shape=jax.ShapeDtypeStruct((S, D_MODEL), BF16),
        compiler_params=_params("parallel"),
    )(x, g, after)


def _side_by_side(w_hbm, w_full, sems):
    @pl.when(pl.program_id(0) == 0)
    def _():
        copies = [pltpu.make_async_copy(w_hbm.at[j], w_full.at[:, pl.ds(SHARD_IN * j, SHARD_IN)], sems.at[j])
                  for j in range(N_CHIPS)]
        for cp in copies:
            cp.start()
        for cp in copies:
            cp.wait()


def _in_proj_fwd(h, w_in, cos, sin, after, tm):
    S = h.shape[0]

    def body(h_ref, w_hbm, cos_ref, sin_ref, after_ref, q_ref, k_ref, v_ref, bcu_ref, qx_ref, proj, w_full, sems):
        _side_by_side(w_hbm, w_full, sems)
        proj[...] = jnp.dot(h_ref[...], w_full[...], preferred_element_type=F32)
        c, s = cos_ref[...], sin_ref[...]
        for j in range(ATTN_W // 128):
            lo = 128 * j
            q_ref[:, lo:lo + 128] = _rope128(proj[:, lo:lo + 128], c, s, False) * SCALE
            k_ref[:, lo:lo + 128] = _rope128(proj[:, ATTN_W + lo:ATTN_W + lo + 128], c, s, False)
        v_ref[...] = proj[:, 2 * ATTN_W:3 * ATTN_W]
        bcu_ref[...] = proj[:, 3 * ATTN_W:3 * ATTN_W + 3 * CONV_W]
        qx_ref[...] = proj[:, 3 * ATTN_W + 3 * CONV_W:PROJ_W].astype(BF16)

    return pl.pallas_call(
        body, name="in_proj_fwd", grid=(S // tm,),
        in_specs=[_rows(tm, D_MODEL), pl.BlockSpec(memory_space=pl.ANY), _rows(tm, 128), _rows(tm, 128),
                  pl.BlockSpec(memory_space=pl.ANY)],
        out_specs=[_rows(tm, ATTN_W), _rows(tm, ATTN_W), _rows(tm, ATTN_W), _rows(tm, 3 * CONV_W), _rows(tm, XATTN_W)],
        out_shape=[jax.ShapeDtypeStruct((S, ATTN_W), F32), jax.ShapeDtypeStruct((S, ATTN_W), F32),
                   jax.ShapeDtypeStruct((S, ATTN_W), F32), jax.ShapeDtypeStruct((S, 3 * CONV_W), F32),
                   jax.ShapeDtypeStruct((S, XATTN_W), BF16)],
        scratch_shapes=[pltpu.VMEM((tm, PROJ_W), F32), pltpu.VMEM((D_MODEL, PROJ_W), BF16),
                        pltpu.SemaphoreType.DMA((N_CHIPS,))],
        compiler_params=_params("arbitrary"),
    )(h, w_in, cos, sin, after)


def _memkv_fwd(mem, g_mem, w_kv, after):
    n_mem = mem.shape[0]

    def body(mem_ref, g_ref, w_ref, after_ref, mn_ref, kv_ref):
        mhat, _ = _rms_hat(mem_ref[...])
        mn = (mhat * g_ref[...]).astype(BF16)
        mn_ref[...] = mn
        kv_ref[...] = jnp.dot(mn, w_ref[...], preferred_element_type=F32).astype(BF16)

    vmem = pl.BlockSpec(memory_space=pltpu.VMEM)
    return pl.pallas_call(
        body, name="memkv_fwd", in_specs=[vmem, vmem, vmem, pl.BlockSpec(memory_space=pl.ANY)], out_specs=[vmem, vmem],
        out_shape=[jax.ShapeDtypeStruct((n_mem, D_MODEL), BF16), jax.ShapeDtypeStruct((n_mem, 2 * XATTN_W), BF16)],
        compiler_params=pltpu.CompilerParams(vmem_limit_bytes=VMEM_LIMIT_V7X),
    )(mem, g_mem, w_kv, after)


def _fill_band_bias(bias):
    row = lax.broadcasted_iota(jnp.int32, (N_BACK, 2 * N_BACK), 0)
    col = lax.broadcasted_iota(jnp.int32, (N_BACK, 2 * N_BACK), 1)
    band = (col >= row) & (col <= row + N_BACK)
    bias[1] = jnp.where(band, 0.0, NEG_INF)
    bias[0] = jnp.where(band & (col >= N_BACK), 0.0, NEG_INF)


def _strided(start, size, d):
    return pl.ds(start, size) if d == 1 else pl.ds(start, size, stride=d)


def _group_starts(g, G, nb, d):
    t0 = g * G
    r, n0 = lax.shift_right_logical(t0, nb.bit_length() - 1), lax.bitwise_and(t0, nb - 1)
    first = r + n0 * (N_BACK * d)
    before = r + jnp.maximum(n0 - 1, 0) * (N_BACK * d)
    starts = [before] + [first + u * (N_BACK * d) for u in range(G)]
    if d == 1:
        starts = [pl.multiple_of(st, N_BACK) for st in starts]
    return starts, n0


def _step_blocks(i, U, nb, d):
    G = min(U, nb)
    whole = G == nb
    row_blocks, blocks = [], []
    for grp in range(U // G):
        starts, n0 = _group_starts(i * (U // G) + grp, G, nb, d)
        base = len(row_blocks)
        if whole:
            row_blocks += [_strided(st, N_BACK, d) for st in starts[1:]]
            blocks += [(base + max(u - 1, 0), base + u, min(u, 1)) for u in range(G)]
        else:
            row_blocks += [_strided(st, N_BACK, d) for st in starts]
            blocks += [(base + u, base + u + 1, jnp.minimum(n0, 1) if u == 0 else 1) for u in range(G)]
    return row_blocks, blocks


def _by_head(a, b):
    lane = lax.broadcasted_iota(jnp.int32, (a.shape[0], 2 * HEAD), 1)
    return jnp.where(lane < HEAD, a, b)


def _head_only(t, hh):
    lane = lax.broadcasted_iota(jnp.int32, t.shape, 1)
    return jnp.where((lane < HEAD) == (hh == 0), t, jnp.zeros_like(t))


def _stack_heads(t):
    return jnp.concatenate([_head_only(t, 0), _head_only(t, 1)], axis=0)


def _head_columns(t):
    return jnp.concatenate([t[:, 0:1], t[:, HEAD:HEAD + 1]], axis=0)


def _unstack(t):
    return _by_head(t[:N_BACK], t[N_BACK:])


def _unstack_columns(t):
    return _by_head(jnp.broadcast_to(t[:N_BACK], (N_BACK, 2 * HEAD)), jnp.broadcast_to(t[N_BACK:], (N_BACK, 2 * HEAD)))


FWD_BLOCKS_PER_STEP = 4
BWD_BLOCKS_PER_STEP = 4
BWD_CHUNK = 64


def _attn_fwd(q, k, v):
    S = q.shape[0]
    U = FWD_BLOCKS_PER_STEP

    def body(q_ref, k_ref, v_ref, y_ref, m_ref, l_scr, bias):
        _fill_band_bias(bias)
        for g, d in enumerate(DILATIONS):
            nb = S // d // N_BACK
            first_pattern, last_pattern = g == 0, g == len(DILATIONS) - 1

            def step(i, carry, d=d, nb=nb, first_pattern=first_pattern, last_pattern=last_pattern):
                row_blocks, blocks = _step_blocks(i, U, nb, d)
                kb = [k_ref[r, :].astype(BF16) for r in row_blocks]
                ss = []
                for before, own, which in blocks:
                    kw = jnp.concatenate([kb[before], kb[own]], 0)
                    qs = _stack_heads(q_ref[row_blocks[own], :].astype(BF16))
                    b = bias[which]
                    ss.append(lax.dot_general(qs, kw, NT, preferred_element_type=F32) + jnp.concatenate([b, b], axis=0))
                ms = [jnp.max(s, axis=1, keepdims=True) for s in ss]
                ps = [jnp.exp(s - m) for s, m in zip(ss, ms)]
                ls = [jnp.sum(p, axis=1, keepdims=True) for p in ps]
                vb = [v_ref[r, :].astype(BF16) for r in row_blocks]
                os_ = [jnp.dot(ps[u].astype(BF16), jnp.concatenate([vb[before], vb[own]], 0), preferred_element_type=F32)
                       for u, (before, own, _) in enumerate(blocks)]
                for u, (_, own, _) in enumerate(blocks):
                    o_g, m_g, l_g = _unstack(os_[u]), _unstack_columns(ms[u]), _unstack_columns(ls[u])
                    r = row_blocks[own]
                    if first_pattern:
                        m_new, l_new, acc = m_g, l_g, o_g
                    else:
                        m_old = m_ref[r, :]
                        m_new = jnp.maximum(m_old, m_g)
                        alpha, beta = jnp.exp(m_old - m_new), jnp.exp(m_g - m_new)
                        l_new = l_scr[r, :] * alpha + l_g * beta
                        acc = y_ref[r, :] * alpha + o_g * beta
                    if last_pattern:
                        y_ref[r, :] = acc / l_new
                        m_ref[r, :] = m_new + jnp.log(l_new)
                    else:
                        y_ref[r, :] = acc
                        m_ref[r, :] = m_new
                        l_scr[r, :] = l_new
                return carry

            lax.fori_loop(0, d * nb // U, step, 0)

    col = pl.BlockSpec((S, 2 * HEAD), lambda j: (0, j))
    return pl.pallas_call(
        body, name="attn_fwd", grid=(q.shape[1] // (2 * HEAD),),
        in_specs=[col, col, col], out_specs=[col, col],
        out_shape=[jax.ShapeDtypeStruct(q.shape, F32)] * 2,
        scratch_shapes=[pltpu.VMEM((S, 2 * HEAD), F32), pltpu.VMEM((2, N_BACK, 2 * N_BACK), F32)],
        compiler_params=_params("parallel"),
    )(q, k, v)


def _attn_bwd(q, k, v, dy, lse, delta, after):
    S = q.shape[0]
    U = BWD_BLOCKS_PER_STEP

    def body(q_ref, k_ref, v_ref, dy_ref, lse_ref, delta_ref, after_ref, dq_ref, dk_ref, dv_ref, bias):
        _fill_band_bias(bias)
        dk_ref[...] = jnp.zeros_like(dk_ref)
        dv_ref[...] = jnp.zeros_like(dv_ref)
        for g, d in enumerate(DILATIONS):
            nb = S // d // N_BACK

            def step(i, carry, d=d, nb=nb, g=g):
                row_blocks, blocks = _step_blocks(i, U, nb, d)
                kb = [k_ref[r, :].astype(BF16) for r in row_blocks]
                vb = [v_ref[r, :].astype(BF16) for r in row_blocks]
                kws = [jnp.concatenate([kb[before], kb[own]], 0) for before, own, _ in blocks]
                vws = [jnp.concatenate([vb[before], vb[own]], 0) for before, own, _ in blocks]
                qss = [_stack_heads(q_ref[row_blocks[own], :].astype(BF16)) for _, own, _ in blocks]
                doss = [_stack_heads(dy_ref[row_blocks[own], :].astype(BF16)) for _, own, _ in blocks]
                ss = [lax.dot_general(qss[u], kws[u], NT, preferred_element_type=F32) for u in range(U)]
                dps = [lax.dot_general(doss[u], vws[u], NT, preferred_element_type=F32) for u in range(U)]
                pbs, dss = [], []
                for u, (_, own, which) in enumerate(blocks):
                    lse_c = _head_columns(lse_ref[row_blocks[own], :])
                    delta_c = _head_columns(delta_ref[row_blocks[own], :])
                    p_parts, ds_parts = [], []
                    for r0 in range(0, 2 * N_BACK, BWD_CHUNK):
                        r = slice(r0, r0 + BWD_CHUNK)
                        mask = bias[which, r0 % N_BACK:r0 % N_BACK + BWD_CHUNK, :]
                        p_r = jnp.exp(ss[u][r] + mask - lse_c[r])
                        p_parts.append(p_r.astype(BF16))
                        ds_parts.append((p_r * (dps[u][r] - delta_c[r])).astype(BF16))
                    pbs.append(jnp.concatenate(p_parts, axis=0))
                    dss.append(jnp.concatenate(ds_parts, axis=0))
                dqs = [jnp.dot(dss[u], kws[u], preferred_element_type=F32) for u in range(U)]
                dkws = [lax.dot_general(dss[u], qss[u], TN, preferred_element_type=F32) for u in range(U)]
                dvws = [lax.dot_general(pbs[u], doss[u], TN, preferred_element_type=F32) for u in range(U)]
                dk_parts, dv_parts = [None] * len(row_blocks), [None] * len(row_blocks)
                for u, (before, own, _) in enumerate(blocks):
                    dq = _unstack(dqs[u])
                    if g == 0:
                        dq_ref[row_blocks[own], :] = dq
                    else:
                        dq_ref[row_blocks[own], :] += dq
                    for idx, dkp, dvp in ((before, dkws[u][:N_BACK], dvws[u][:N_BACK]),
                                          (own, dkws[u][N_BACK:], dvws[u][N_BACK:])):
                        dk_parts[idx] = dkp if dk_parts[idx] is None else dk_parts[idx] + dkp
                        dv_parts[idx] = dvp if dv_parts[idx] is None else dv_parts[idx] + dvp
                for idx, r in enumerate(row_blocks):
                    dk_ref[r, :] += dk_parts[idx]
                    dv_ref[r, :] += dv_parts[idx]
                return carry

            lax.fori_loop(0, d * nb // U, step, 0)

    col = pl.BlockSpec((S, 2 * HEAD), lambda j: (0, j))
    return pl.pallas_call(
        body, name="attn_bwd", grid=(q.shape[1] // (2 * HEAD),),
        in_specs=[col] * 6 + [pl.BlockSpec(memory_space=pl.ANY)], out_specs=[col] * 3,
        out_shape=[jax.ShapeDtypeStruct(q.shape, F32)] * 3,
        scratch_shapes=[pltpu.VMEM((2, N_BACK, 2 * N_BACK), F32)],
        compiler_params=_params("parallel"),
    )(q, k, v, dy, lse, delta, after)


def _shift_down(z, before, k):
    row = lax.broadcasted_iota(jnp.int32, z.shape, 0)
    out = pltpu.roll(z, k, 0)
    for i in range(k):
        out = jnp.where(row == i, before[8 - k + i:8 - k + i + 1, :], out)
    return out


def _shift_up(z, after, k):
    rows = z.shape[0]
    row = lax.broadcasted_iota(jnp.int32, z.shape, 0)
    out = pltpu.roll(z, rows - k, 0)
    for i in range(k):
        out = jnp.where(row == rows - k + i, after[i:i + 1, :], out)
    return out


def _conv_fwd(bcu, before, is_first, w):
    b, c, u = bcu[:, 0:CONV_W], bcu[:, CONV_W:2 * CONV_W], bcu[:, 2 * CONV_W:3 * CONV_W]
    z = c * u
    zb = jnp.where(is_first, 0.0, before[:, CONV_W:2 * CONV_W] * before[:, 2 * CONV_W:3 * CONV_W])
    z1, z2 = _shift_down(z, zb, 1), _shift_down(z, zb, 2)
    cv = w[0:1, :] * z2 + w[1:2, :] * z1 + w[2:3, :] * z
    return b, c, u, z, z1, z2, cv


def _halo_before(tm, width):
    return pl.BlockSpec((8, width), lambda i: (jnp.maximum(i * (tm // 8) - 1, 0), 0))


def _mix_fwd(ya, bcu, qx, mkv, conv_w, g_a, g_c, g_x, w_out, g_post, x, tm):
    S = x.shape[0]

    def body(ya_ref, bcu_ref, before_ref, qx_ref, mkv_ref, cw_ref, ga_ref, gc_ref, gx_ref,
             wo_ref, gp_ref, x_ref, yx_ref, ycat_ref, y2_ref, x1_ref):
        ya = ya_ref[...]
        b, _, _, _, _, _, cv = _conv_fwd(bcu_ref[...], before_ref[...], pl.program_id(0) == 0, cw_ref[...])
        yc = b * cv

        qxb, mkvb = qx_ref[...], mkv_ref[...]
        for hd in range(XATTN_W // HEAD):
            sl = slice(HEAD * hd, HEAD * (hd + 1))
            s = lax.dot_general(qxb[:, sl], mkvb[:, sl], NT, preferred_element_type=F32) * SCALE
            mx = jnp.max(s, axis=1, keepdims=True)
            p = jnp.exp(s - mx)
            l = jnp.sum(p, axis=1, keepdims=True)
            vm = mkvb[:, XATTN_W + HEAD * hd:XATTN_W + HEAD * (hd + 1)]
            yx_ref[:, sl] = jnp.dot(p.astype(BF16), vm, preferred_element_type=F32) / l
        yx = yx_ref[...]

        ycat_ref[:, 0:ATTN_W] = (_rms_hat(ya)[0] * ga_ref[...]).astype(BF16)
        ycat_ref[:, ATTN_W:ATTN_W + CONV_W] = (_rms_hat(yc)[0] * gc_ref[...]).astype(BF16)
        ycat_ref[:, ATTN_W + CONV_W:D_MODEL] = (_rms_hat(yx)[0] * gx_ref[...]).astype(BF16)
        y2 = jnp.dot(ycat_ref[...], wo_ref[...], preferred_element_type=F32)
        y2_ref[...] = y2
        x1_ref[...] = x_ref[...] + _rms_hat(y2)[0] * gp_ref[...]

    n_mem = mkv.shape[0]
    return pl.pallas_call(
        body, name="mix_fwd", grid=(S // tm,),
        in_specs=[_rows(tm, ATTN_W), _rows(tm, 3 * CONV_W), _halo_before(tm, 3 * CONV_W), _rows(tm, XATTN_W),
                  _resident((n_mem, 2 * XATTN_W)), _resident((3, CONV_W)), _resident((1, ATTN_W)),
                  _resident((1, CONV_W)), _resident((1, XATTN_W)), _resident((D_MODEL, D_MODEL)),
                  _resident((1, D_MODEL)), _rows(tm, D_MODEL)],
        out_specs=[_rows(tm, XATTN_W), _rows(tm, D_MODEL), _rows(tm, D_MODEL), _rows(tm, D_MODEL)],
        out_shape=[jax.ShapeDtypeStruct((S, XATTN_W), F32), jax.ShapeDtypeStruct((S, D_MODEL), BF16),
                   jax.ShapeDtypeStruct((S, D_MODEL), F32), jax.ShapeDtypeStruct((S, D_MODEL), F32)],
        compiler_params=_params("parallel"),
    )(ya, bcu, bcu, qx, mkv, conv_w, g_a, g_c, g_x, w_out, g_post, x)


def _mlp_fwd_bwd(x1, target, g_pre, g_post, w_up, w_down, tm):
    S = x1.shape[0]
    n_ff = D_FF // SHARD_FF

    def body(x1_ref, t_ref, gpre_ref, gpost_ref, wup_ref, wdn_ref,
             h2_ref, f_ref, du_ref, df2_ref, dx1_ref, dgpre_ref, dgpost_ref, loss_ref, u_scr):
        @pl.when(pl.program_id(0) == 0)
        def _():
            dgpre_ref[...] = jnp.zeros_like(dgpre_ref)
            dgpost_ref[...] = jnp.zeros_like(dgpost_ref)
            loss_ref[...] = jnp.zeros_like(loss_ref)

        x1 = x1_ref[...]
        x1hat, r1 = _rms_hat(x1)
        h2 = (x1hat * gpre_ref[...]).astype(BF16)
        h2_ref[...] = h2
        f2 = jnp.zeros((tm, D_MODEL), F32)
        for j in range(n_ff):
            cols = slice(SHARD_FF * j, SHARD_FF * (j + 1))
            u = jnp.maximum(jnp.dot(h2, wup_ref[j], preferred_element_type=F32), 0.0)
            u_scr[:, cols] = u
            f = (u * u).astype(BF16)
            f_ref[:, cols] = f
            f2 = f2 + jnp.dot(f, wdn_ref[cols, :], preferred_element_type=F32)
        f2hat, r2 = _rms_hat(f2)
        err = x1 + f2hat * gpost_ref[...] - t_ref[...]
        loss_ref[...] += 0.5 * jnp.sum(jnp.mean(err * err, axis=-1, keepdims=True), axis=0, keepdims=True)
        dx2 = err * (1.0 / D_MODEL)
        dgpost_ref[...] += jnp.sum(dx2 * f2hat, axis=0, keepdims=True)
        df2 = _rms_bwd(f2hat, r2, gpost_ref[...], dx2).astype(BF16)
        df2_ref[...] = df2
        dh2 = jnp.zeros((tm, D_MODEL), F32)
        for j in range(n_ff):
            cols = slice(SHARD_FF * j, SHARD_FF * (j + 1))
            df = lax.dot_general(df2, wdn_ref[cols, :], NT, preferred_element_type=F32)
            du = (2.0 * u_scr[:, cols] * df).astype(BF16)
            du_ref[:, cols] = du
            dh2 = dh2 + lax.dot_general(du, wup_ref[j], NT, preferred_element_type=F32)
        dgpre_ref[...] += jnp.sum(dh2 * x1hat, axis=0, keepdims=True)
        dx1_ref[...] = dx2 + _rms_bwd(x1hat, r1, gpre_ref[...], dh2)

    acc = pl.BlockSpec((1, D_MODEL), lambda i: (0, 0))
    return pl.pallas_call(
        body, name="mlp_fwd_bwd", grid=(S // tm,),
        in_specs=[_rows(tm, D_MODEL), _rows(tm, D_MODEL), _resident((1, D_MODEL)), _resident((1, D_MODEL)),
                  _resident((n_ff, D_MODEL, SHARD_FF)), _resident((D_FF, D_MODEL))],
        out_specs=[_rows(tm, D_MODEL), _rows(tm, D_FF), _rows(tm, D_FF), _rows(tm, D_MODEL), _rows(tm, D_MODEL),
                   acc, acc, pl.BlockSpec((1, 1), lambda i: (0, 0))],
        out_shape=[jax.ShapeDtypeStruct((S, D_MODEL), BF16), jax.ShapeDtypeStruct((S, D_FF), BF16),
                   jax.ShapeDtypeStruct((S, D_FF), BF16), jax.ShapeDtypeStruct((S, D_MODEL), BF16),
                   jax.ShapeDtypeStruct((S, D_MODEL), F32), jax.ShapeDtypeStruct((1, D_MODEL), F32),
                   jax.ShapeDtypeStruct((1, D_MODEL), F32), jax.ShapeDtypeStruct((1, 1), F32)],
        scratch_shapes=[pltpu.VMEM((tm, D_FF), F32)],
        compiler_params=_params("arbitrary"),
    )(x1, target, g_pre, g_post, w_up, w_down)


def _weight_grad(name, a, b, rows_sharded, after):
    S, K = a.shape
    N = b.shape[1]
    if rows_sharded:
        tk, tn = K // N_CHIPS, N
        a_spec = pl.BlockSpec((S, tk), lambda j: (0, j))
        b_spec = pl.BlockSpec((S, tn), lambda j: (0, 0), pipeline_mode=pl.Buffered(1))
    else:
        tk, tn = K, N // N_CHIPS
        a_spec = pl.BlockSpec((S, tk), lambda j: (0, 0), pipeline_mode=pl.Buffered(1))
        b_spec = pl.BlockSpec((S, tn), lambda j: (0, j))
    half = tk // 2

    def body(a_ref, b_ref, after_ref, o_ref):
        res = lax.dot_general(a_ref[...], b_ref[...], TN, preferred_element_type=F32)
        o_ref[0, 0] = res[:half]
        o_ref[1, 0] = res[half:]

    return pl.pallas_call(
        body, name=name, grid=(N_CHIPS,), in_specs=[a_spec, b_spec, pl.BlockSpec(memory_space=pl.ANY)],
        out_specs=pl.BlockSpec((2, 1, half, tn), lambda j: (0, j, 0, 0)),
        out_shape=jax.ShapeDtypeStruct((2, N_CHIPS, half, tn), F32),
        compiler_params=_params("parallel"),
    )(a, b, after)


def _weight_grad_w_in(h, dproj):
    S, K = h.shape
    step_w = 2 * 256
    n_steps = PROJ_W // step_w
    half = K // 2

    def body(a_ref, b_ref, o_ref):
        res = lax.dot_general(a_ref[...], b_ref[...], TN, preferred_element_type=F32)
        for step in range(n_steps):
            @pl.when(pl.program_id(0) == step)
            def _(step=step):
                lo = step * step_w
                while lo < (step + 1) * step_w:
                    chip = lo // SHARD_IN
                    hi = min((step + 1) * step_w, (chip + 1) * SHARD_IN)
                    for hh in range(2):
                        o_ref[hh, chip, :, lo - chip * SHARD_IN:hi - chip * SHARD_IN] = (
                            res[half * hh:half * (hh + 1), lo - step * step_w:hi - step * step_w])
                    lo = hi

    return pl.pallas_call(
        body, name="grad_w_in", grid=(n_steps,),
        in_specs=[pl.BlockSpec((S, K), lambda j: (0, 0), pipeline_mode=pl.Buffered(1)),
                  pl.BlockSpec((S, step_w), lambda j: (0, j))],
        out_specs=pl.BlockSpec((2, N_CHIPS, half, SHARD_IN), lambda j: (0, 0, 0, 0)),
        out_shape=jax.ShapeDtypeStruct((2, N_CHIPS, half, SHARD_IN), F32),
        compiler_params=_params("arbitrary"),
    )(h, dproj)


def _mixer_bwd(dx1, y2, ya, yx, bcu, qx, mkv, conv_w, g_a, g_c, g_x, w_out, g_post, after, tm):
    S = dx1.shape[0]
    n_mem = mkv.shape[0]
    n_tiles = S // tm

    def body(dx1_ref, y2_ref, ya_ref, yx_ref, bcu_ref, before_ref, qx_ref, mkv_ref, cw_ref, ga_ref, gc_ref, gx_ref,
             wo_ref, gp_ref, after_ref, dy2_ref, dya_ref, delta_ref, tail_ref, dmkv_ref, dcw_ref, dgp_ref, dga_ref,
             dgc_ref, dgx_ref, carry):
        step = pl.program_id(0)
        first_tile = step == n_tiles - 1

        @pl.when(step == 0)
        def _():
            for ref in (dmkv_ref, dcw_ref, dgp_ref, dga_ref, dgc_ref, dgx_ref, carry):
                ref[...] = jnp.zeros_like(ref)

        dx1 = dx1_ref[...]
        y2hat, r2 = _rms_hat(y2_ref[...])
        dgp_ref[...] += jnp.sum(dx1 * y2hat, axis=0, keepdims=True)
        dy2 = _rms_bwd(y2hat, r2, gp_ref[...], dx1).astype(BF16)
        dy2_ref[...] = dy2
        dycat = lax.dot_general(dy2, wo_ref[...], NT, preferred_element_type=F32)

        d_na = dycat[:, 0:ATTN_W]
        ya = ya_ref[...]
        yahat, ra = _rms_hat(ya)
        dga_ref[...] += jnp.sum(d_na * yahat, axis=0, keepdims=True)
        dya = _rms_bwd(yahat, ra, ga_ref[...], d_na)
        dya_ref[...] = dya
        prod = dya * ya
        hi = prod.astype(BF16)
        lo = (prod - hi.astype(F32)).astype(BF16)
        head_of = lambda axis: lax.shift_right_logical(lax.broadcasted_iota(jnp.int32, (ATTN_W, ATTN_W), axis),
                                                       HEAD.bit_length() - 1)
        ones = jnp.where(head_of(0) == head_of(1), 1.0, 0.0).astype(BF16)
        delta_ref[...] = jnp.dot(hi, ones, preferred_element_type=F32) + jnp.dot(lo, ones, preferred_element_type=F32)

        w = cw_ref[...]
        b, c, u, z, z1, z2, cv = _conv_fwd(bcu_ref[...], before_ref[...], first_tile, w)
        d_nc = dycat[:, ATTN_W:ATTN_W + CONV_W]
        ychat, rc = _rms_hat(b * cv)
        dgc_ref[...] += jnp.sum(d_nc * ychat, axis=0, keepdims=True)
        dyc = _rms_bwd(ychat, rc, gc_ref[...], d_nc)
        dcv = dyc * b
        behind = carry[...]
        dz = w[2:3, :] * dcv + w[1:2, :] * _shift_up(dcv, behind, 1) + w[0:1, :] * _shift_up(dcv, behind, 2)
        carry[...] = dcv[0:8, :]
        dcw_ref[0:1, :] += jnp.sum(dcv * z2, axis=0, keepdims=True)
        dcw_ref[1:2, :] += jnp.sum(dcv * z1, axis=0, keepdims=True)
        dcw_ref[2:3, :] += jnp.sum(dcv * z, axis=0, keepdims=True)
        tail_ref[:, 0:CONV_W] = (dyc * cv).astype(BF16)
        tail_ref[:, CONV_W:2 * CONV_W] = (dz * u).astype(BF16)
        tail_ref[:, 2 * CONV_W:3 * CONV_W] = (dz * c).astype(BF16)

        d_nx = dycat[:, ATTN_W + CONV_W:D_MODEL]
        yxhat, rx = _rms_hat(yx_ref[...])
        dgx_ref[...] += jnp.sum(d_nx * yxhat, axis=0, keepdims=True)
        dyx = _rms_bwd(yxhat, rx, gx_ref[...], d_nx)
        qxb, mkvb = qx_ref[...], mkv_ref[...]
        for hd in range(XATTN_W // HEAD):
            sl = slice(HEAD * hd, HEAD * (hd + 1))
            vsl = slice(XATTN_W + HEAD * hd, XATTN_W + HEAD * (hd + 1))
            s = lax.dot_general(qxb[:, sl], mkvb[:, sl], NT, preferred_element_type=F32) * SCALE
            e = jnp.exp(s - jnp.max(s, axis=1, keepdims=True))
            p = e / jnp.sum(e, axis=1, keepdims=True)
            dob = dyx[:, sl].astype(BF16)
            dp = lax.dot_general(dob, mkvb[:, vsl], NT, preferred_element_type=F32)
            ds = (p * (dp - jnp.sum(p * dp, axis=1, keepdims=True)) * SCALE).astype(BF16)
            tail_ref[:, 3 * CONV_W + HEAD * hd:3 * CONV_W + HEAD * (hd + 1)] = jnp.dot(
                ds, mkvb[:, sl], preferred_element_type=F32).astype(BF16)
            dmkv_ref[:, sl] += lax.dot_general(ds, qxb[:, sl], TN, preferred_element_type=F32)
            dmkv_ref[:, vsl] += lax.dot_general(p.astype(BF16), dob, TN, preferred_element_type=F32)

    rows = lambda width: pl.BlockSpec((tm, width), lambda i: (n_tiles - 1 - i, 0))
    before = pl.BlockSpec((8, 3 * CONV_W), lambda i: (jnp.maximum((n_tiles - 1 - i) * (tm // 8) - 1, 0), 0))
    acc = lambda r, w: pl.BlockSpec((r, w), lambda i: (0, 0))
    return pl.pallas_call(
        body, name="mixer_bwd", grid=(n_tiles,),
        in_specs=[rows(D_MODEL), rows(D_MODEL), rows(ATTN_W), rows(XATTN_W), rows(3 * CONV_W), before, rows(XATTN_W),
                  _resident((n_mem, 2 * XATTN_W)), _resident((3, CONV_W)), _resident((1, ATTN_W)),
                  _resident((1, CONV_W)), _resident((1, XATTN_W)), _resident((D_MODEL, D_MODEL)),
                  _resident((1, D_MODEL)), pl.BlockSpec(memory_space=pl.ANY)],
        out_specs=[rows(D_MODEL), rows(ATTN_W), rows(ATTN_W), rows(3 * CONV_W + XATTN_W), acc(n_mem, 2 * XATTN_W),
                   acc(3, CONV_W), acc(1, D_MODEL), acc(1, ATTN_W), acc(1, CONV_W), acc(1, XATTN_W)],
        out_shape=[jax.ShapeDtypeStruct((S, D_MODEL), BF16), jax.ShapeDtypeStruct((S, ATTN_W), F32),
                   jax.ShapeDtypeStruct((S, ATTN_W), F32), jax.ShapeDtypeStruct((S, 3 * CONV_W + XATTN_W), BF16),
                   jax.ShapeDtypeStruct((n_mem, 2 * XATTN_W), F32), jax.ShapeDtypeStruct((3, CONV_W), F32),
                   jax.ShapeDtypeStruct((1, D_MODEL), F32), jax.ShapeDtypeStruct((1, ATTN_W), F32),
                   jax.ShapeDtypeStruct((1, CONV_W), F32), jax.ShapeDtypeStruct((1, XATTN_W), F32)],
        scratch_shapes=[pltpu.VMEM((8, CONV_W), F32)],
        compiler_params=_params("arbitrary"),
    )(dx1, y2, ya, yx, bcu, bcu, qx, mkv, conv_w, g_a, g_c, g_x, w_out, g_post, after)


def _memkv_bwd(mem, g_mem, w_kv, dmkv):
    n_mem = mem.shape[0]
    half = D_MODEL // N_CHIPS // 2

    def body(mem_ref, g_ref, w_ref, d_ref, dw_ref, dg_ref):
        mhat, _ = _rms_hat(mem_ref[...])
        mn = (mhat * g_ref[...]).astype(BF16)
        d = d_ref[...].astype(BF16)
        for k in range(2 * N_CHIPS):
            dw_ref[k % 2, k // 2] = lax.dot_general(mn[:, half * k:half * (k + 1)], d, TN, preferred_element_type=F32)
        dmn = lax.dot_general(d, w_ref[...], NT, preferred_element_type=F32)
        dg_ref[...] = jnp.sum(dmn * mhat, axis=0, keepdims=True)

    return pl.pallas_call(
        body, name="memkv_bwd",
        out_shape=[jax.ShapeDtypeStruct((2, N_CHIPS, half, 2 * XATTN_W), F32), jax.ShapeDtypeStruct((1, D_MODEL), F32)],
        compiler_params=pltpu.CompilerParams(vmem_limit_bytes=VMEM_LIMIT_V7X),
    )(mem, g_mem, w_kv, dmkv)


def _in_proj_bwd(dqkv, tail, cos, sin, w_in, x, g, dx1, after, tm):
    S = x.shape[0]

    def body(dq_ref, dk_ref, dv_ref, tail_ref, cos_ref, sin_ref, w_hbm, x_ref, g_ref, dx1_ref, after_ref,
             dproj_ref, dx_ref, dg_ref, w_full, sems):
        _side_by_side(w_hbm, w_full, sems)

        @pl.when(pl.program_id(0) == 0)
        def _():
            dg_ref[...] = jnp.zeros_like(dg_ref)

        halves = [slice(0, tm // 2), slice(tm // 2, tm)]
        for rows in halves:
            c, s = cos_ref[rows, :], sin_ref[rows, :]
            for j in range(ATTN_W // 128):
                cols = slice(128 * j, 128 * (j + 1))
                dproj_ref[rows, cols] = _rope128(dq_ref[rows, cols] * SCALE, c, s, True).astype(BF16)
                dproj_ref[rows, ATTN_W + 128 * j:ATTN_W + 128 * (j + 1)] = _rope128(dk_ref[rows, cols], c, s, True).astype(BF16)
            dproj_ref[rows, 2 * ATTN_W:3 * ATTN_W] = dv_ref[rows, :].astype(BF16)
            dproj_ref[rows, 3 * ATTN_W:PROJ_W] = tail_ref[rows, :]
        dhs = [lax.dot_general(dproj_ref[rows, :], w_full[...], NT, preferred_element_type=F32) for rows in halves]
        for rows, dh in zip(halves, dhs):
            xhat, r = _rms_hat(x_ref[rows, :])
            dg_ref[...] += jnp.sum(dh * xhat, axis=0, keepdims=True)
            dx_ref[rows, :] = dx1_ref[rows, :] + _rms_bwd(xhat, r, g_ref[...], dh)

    return pl.pallas_call(
        body, name="in_proj_bwd", grid=(S // tm,),
        in_specs=[_rows(tm, ATTN_W)] * 3 + [_rows(tm, PROJ_W - 3 * ATTN_W), _rows(tm, 128), _rows(tm, 128),
                  pl.BlockSpec(memory_space=pl.ANY), _rows(tm, D_MODEL), _resident((1, D_MODEL)),
                  _rows(tm, D_MODEL), pl.BlockSpec(memory_space=pl.ANY)],
        out_specs=[_rows(tm, PROJ_W), _rows(tm, D_MODEL), pl.BlockSpec((1, D_MODEL), lambda i: (0, 0))],
        out_shape=[jax.ShapeDtypeStruct((S, PROJ_W), BF16), jax.ShapeDtypeStruct((S, D_MODEL), F32),
                   jax.ShapeDtypeStruct((1, D_MODEL), F32)],
        scratch_shapes=[pltpu.VMEM((D_MODEL, PROJ_W), BF16), pltpu.SemaphoreType.DMA((N_CHIPS,))],
        compiler_params=_params("arbitrary"),
    )(*dqkv, tail, cos, sin, w_in, x, g, dx1, after)


def _row_tile(rows):
    return ROW_TILE if rows % ROW_TILE == 0 else rows


def _chip_sums_bf16(name, grads, from_sibling, place):
    k = len(grads)
    _, n, rows, _ = grads[0].shape
    tr = _row_tile(rows)

    def body(place_ref, *refs):
        for g_ref, b_ref, o_ref in zip(refs[:k], refs[k:2 * k], refs[2 * k:]):
            o_ref[...] = (g_ref[0] + b_ref[...]).astype(BF16)

    mine = lambda g: pl.BlockSpec((1, 1, tr, g.shape[3]), lambda s, i, p: (p[0], s, i, 0))
    slab = lambda g: pl.BlockSpec((1, tr, g.shape[3]), lambda s, i, p: (s, i, 0))
    return pl.pallas_call(
        body, name=name, out_shape=[jax.ShapeDtypeStruct(g.shape[1:], BF16) for g in grads],
        grid_spec=pltpu.PrefetchScalarGridSpec(
            num_scalar_prefetch=1, grid=(n, rows // tr),
            in_specs=[mine(g) for g in grads] + [slab(g) for g in grads], out_specs=[slab(g) for g in grads]),
        compiler_params=_params("parallel", "parallel"),
    )(place, *grads, *from_sibling)


def _final_sums(name, grads, from_sibling, others, place):
    k = len(grads)
    rows = grads[0].shape[2]
    tr = _row_tile(rows)

    def body(place_ref, *refs):
        for a in range(k):
            own_ref, sib_ref = refs[a], refs[k + a]
            acc = own_ref[0, 0] + sib_ref[0]
            for o in refs[2 * k + 3 * a:2 * k + 3 * a + 3]:
                acc = acc + o[0].astype(F32)
            refs[5 * k + a][0] = acc

    own = lambda g: pl.BlockSpec((1, 1, tr, g.shape[3]), lambda i, p: (p[0], p[1], i, 0))
    sib = lambda g: pl.BlockSpec((1, tr, g.shape[3]), lambda i, p: (p[1], i, 0))
    other = lambda g, j: pl.BlockSpec((1, tr, g.shape[3]), lambda i, p: (j, i, 0))
    return pl.pallas_call(
        body, name=name, out_shape=[jax.ShapeDtypeStruct((2,) + g.shape[2:], F32) for g in grads],
        grid_spec=pltpu.PrefetchScalarGridSpec(
            num_scalar_prefetch=1, grid=(rows // tr,),
            in_specs=[own(g) for g in grads] + [sib(g) for g in grads] + [other(g, j) for g in grads for j in range(3)],
            out_specs=[pl.BlockSpec((1, tr, g.shape[3]), lambda i, p: (p[0], i, 0)) for g in grads]),
        compiler_params=_params("parallel"),
    )(place, *grads, *from_sibling, *[o for o in others for _ in range(3)])


def _adamw_update(w, g, m, v):
    m = ADAM_B1 * m + (1.0 - ADAM_B1) * g
    v = ADAM_B2 * v + (1.0 - ADAM_B2) * (g * g)
    m_hat = m * (1.0 / (1.0 - ADAM_B1 ** ADAM_STEP))
    v_hat = v * (1.0 / (1.0 - ADAM_B2 ** ADAM_STEP))
    return -ADAM_LR * (m_hat / (jnp.sqrt(v_hat) + ADAM_EPS) + ADAM_WD * w), m, v


def _adamw(name, params, after):
    k = len(params)
    rows = params[0][0].shape[0]
    tr = ADAMW_ROW_TILE if rows % ADAMW_ROW_TILE == 0 else rows

    def body(*refs):
        ins, outs = refs[:4 * k], refs[4 * k + 1:]
        for a in range(k):
            w_ref, g_ref, m_ref, v_ref = ins[4 * a:4 * a + 4]
            g = g_ref[...]
            outs[4 * a][...] = g
            outs[4 * a + 1][...], outs[4 * a + 2][...], outs[4 * a + 3][...] = _adamw_update(w_ref[...], g, m_ref[...], v_ref[...])

    spec = lambda w: pl.BlockSpec((tr, w.shape[1]), lambda i: (i, 0))
    out = pl.pallas_call(
        body, name=name, grid=(rows // tr,),
        in_specs=[spec(p[0]) for p in params for _ in range(4)] + [pl.BlockSpec(memory_space=pl.ANY)],
        out_specs=[spec(p[0]) for p in params for _ in range(4)],
        out_shape=[jax.ShapeDtypeStruct(p[0].shape, F32) for p in params for _ in range(4)],
        compiler_params=_params("parallel"),
    )(*[t for p in params for t in p], after)
    return [out[4 * a:4 * a + 4] for a in range(k)]


def _small_update(summed, chip, gains, gains_m, gains_v, taps, taps_m, taps_v):
    n = len(gains)
    widths = [g.shape[1] for g in gains]
    k, w = taps.shape

    def body(*refs):
        chip_ref, sum_ref = refs[0], refs[1]
        params = [refs[2 + 3 * i:5 + 3 * i] for i in range(n + 1)]
        outs = [refs[2 + 3 * (n + 1) + 4 * i:2 + 3 * (n + 1) + 4 * (i + 1)] for i in range(n + 1)]
        loss_ref = refs[-1]
        for i in range(n):
            g = sum_ref[i:i + 1, 0:widths[i]]
            wr, mr, vr = params[i]
            outs[i][0][...] = g
            outs[i][1][...], outs[i][2][...], outs[i][3][...] = _adamw_update(wr[...], g, mr[...], vr[...])
        g = sum_ref[n:n + k, 0:w]
        for j in range(1, N_CHIPS):
            g = jnp.where(chip_ref[0] == j, sum_ref[n:n + k, w * j:w * (j + 1)], g)
        wr, mr, vr = params[n]
        outs[n][0][...] = g
        outs[n][1][...], outs[n][2][...], outs[n][3][...] = _adamw_update(wr[...], g, mr[...], vr[...])
        loss_ref[...] = sum_ref[n + k:n + k + 1, 0:1]

    vmem = pl.BlockSpec(memory_space=pltpu.VMEM)
    operands = [chip, summed]
    for p in zip(list(gains) + [taps], list(gains_m) + [taps_m], list(gains_v) + [taps_v]):
        operands += list(p)
    shapes = [jax.ShapeDtypeStruct(p.shape, F32) for p in list(gains) + [taps] for _ in range(4)]
    out = pl.pallas_call(
        body, name="small_update", out_shape=shapes + [jax.ShapeDtypeStruct((1, 1), F32)],
        in_specs=[pl.BlockSpec(memory_space=pltpu.SMEM)] + [vmem] * (len(operands) - 1),
        out_specs=[vmem] * (len(shapes) + 1),
    )(*operands)
    return [out[4 * i:4 * (i + 1)] for i in range(n + 1)], out[-1]


def _sum_blocks(name, blocks):
    n, rows, cols = blocks.shape

    def body(b_ref, o_ref):
        acc = b_ref[0]
        for k in range(1, n):
            acc = acc + b_ref[k]
        o_ref[...] = acc

    return pl.pallas_call(body, name=name, out_shape=jax.ShapeDtypeStruct((rows, cols), F32))(blocks)


def _place():
    return lax.axis_index("x"), lax.axis_index("y"), lax.axis_index("c")


def _other_chips(x, y):
    return [(1 - x, y), (x, 1 - y), (1 - x, 1 - y)]


def _allgather_finish(name, shards, landed, pass_on):
    n = len(shards)

    def body(*refs):
        ins, outs, stage = refs[:n], refs[2 * n:3 * n], refs[3 * n:4 * n]
        send_sems, recv_sems, local_sems = refs[4 * n:]
        x, y, c = _place()
        chips = _other_chips(x, y)

        def copy(a, k, chip, half):
            place = outs[a].at[2 * chip[0] + chip[1], half]
            return pltpu.make_async_remote_copy(
                src_ref=place, dst_ref=place, send_sem=send_sems.at[3 * a + k], recv_sem=recv_sems.at[3 * a + k],
                device_id=(x, y, 1 - c), device_id_type=MESH)

        load = [pltpu.make_async_copy(ins[a], stage[a], local_sems.at[a]) for a in range(n)]
        local = [pltpu.make_async_copy(stage[a], outs[a].at[2 * x + y], local_sems.at[a]) for a in range(n)]
        for cp in load:
            cp.start()
        passed = [copy(a, k, chip, c) for a in range(n) if pass_on[a] for k, chip in enumerate(chips)]
        for cp in passed:
            cp.start()
        for a in range(n):
            load[a].wait()
            local[a].start()
        for a in range(n):
            if pass_on[a]:
                for k, chip in enumerate(chips):
                    copy(a, k, chip, 1 - c).wait_recv()
        for cp in passed:
            cp.wait_send()
        for cp in local:
            cp.wait()

    any_spec = pl.BlockSpec(memory_space=pl.ANY)
    return pl.pallas_call(
        body, name=name,
        out_shape=[jax.ShapeDtypeStruct((N_CHIPS,) + s.shape, s.dtype) for s in shards],
        in_specs=[any_spec] * (2 * n), out_specs=[any_spec] * n,
        input_output_aliases={n + a: a for a in range(n)},
        scratch_shapes=[pltpu.VMEM(s.shape, s.dtype) for s in shards]
        + [pltpu.SemaphoreType.DMA((3 * n,)), pltpu.SemaphoreType.DMA((3 * n,)), pltpu.SemaphoreType.DMA((n,))],
        compiler_params=pltpu.CompilerParams(vmem_limit_bytes=VMEM_LIMIT_V7X),
    )(*shards, *landed)


def _plan_first_hop(x, y, c, shards, lands):
    return [(shards[a].at[c], lands[a].at[2 * x + y, c], lands[a].at[2 * chip[0] + chip[1], c], (*chip, c))
            for a in range(len(shards)) for chip in _other_chips(x, y)]


def _plan_pass_on(x, y, c, nothing, lands):
    def place(a, chip, half):
        return lands[a].at[2 * chip[0] + chip[1], half]

    return [(place(a, chip, c), place(a, chip, c), place(a, chip, 1 - c), (x, y, 1 - c))
            for a in range(len(lands)) for chip in _other_chips(x, y)]


def _plan_own_half_to_sibling(x, y, c, nothing, lands):
    return [(lands[a].at[c], lands[a].at[c], lands[a].at[1 - c], (x, y, 1 - c)) for a in range(len(lands))]


def _plan_other_half_to_sibling(x, y, c, grads, lands):
    return [(grads[a].at[1 - c], lands[a], lands[a], (x, y, 1 - c)) for a in range(len(grads))]


def _plan_to_other_chips(x, y, c, partials, lands):
    return [(partials[a].at[2 * chip[0] + chip[1]], lands[a].at[k], lands[a].at[k], (*chip, c))
            for a in range(len(partials)) for k, chip in enumerate(_other_chips(x, y))]


def _plan_to_all(x, y, c, blocks, lands):
    flips = [(fx, fy, fc) for fx in (0, 1) for fy in (0, 1) for fc in (0, 1) if (fx, fy, fc) != (0, 0, 0)]
    peers = [(1 - x if fx else x, 1 - y if fy else y, 1 - c if fc else c) for fx, fy, fc in flips]
    return [(blocks[0], lands[0].at[4 * x + 2 * y + c], lands[0].at[4 * p[0] + 2 * p[1] + p[2]], p) for p in peers]


def _planned_copies(plan, srcs, lands, send_sems, recv_sems):
    x, y, c = _place()

    def pair(k, src, there, here, to):
        make = lambda dst: pltpu.make_async_remote_copy(
            src_ref=src, dst_ref=dst, send_sem=send_sems.at[k], recv_sem=recv_sems.at[k], device_id=to, device_id_type=MESH)
        return make(there), make(here)

    return [pair(k, *entry) for k, entry in enumerate(plan(x, y, c, srcs, lands))]


_HBM_SPEC = pl.BlockSpec(memory_space=pltpu.HBM)
_SEM_SPEC = pl.BlockSpec(memory_space=pltpu.SEMAPHORE)


def _hbm(a):
    return pltpu.with_memory_space_constraint(a, pltpu.HBM)


def _exchange_start(name, plan, n_copies, srcs, land_shapes, after, lands=None):
    if lands is None:
        lands = [lax.empty(s.shape, s.dtype) for s in land_shapes]
    land_shapes = lands
    ns, nl = len(srcs), len(land_shapes)
    n_in = ns + nl + 1

    def body(*refs):
        for send, _ in _planned_copies(plan, refs[:ns], refs[ns:ns + nl], refs[n_in], refs[n_in + 1]):
            send.start()
        refs[-1][...] = jnp.zeros_like(refs[-1])

    out = pl.pallas_call(
        body, name=name,
        out_shape=(pltpu.SemaphoreType.DMA((n_copies,)), pltpu.SemaphoreType.DMA((n_copies,)),
                   *[pltpu.HBM(s.shape, s.dtype) for s in land_shapes], jax.ShapeDtypeStruct((8, 128), F32)),
        in_specs=[_HBM_SPEC] * (ns + nl) + [pl.BlockSpec(memory_space=pl.ANY)],
        out_specs=(_SEM_SPEC, _SEM_SPEC, *[_HBM_SPEC] * nl, pl.BlockSpec(memory_space=pltpu.VMEM)),
        input_output_aliases={ns + i: 2 + i for i in range(nl)},
        compiler_params=pltpu.CompilerParams(has_side_effects=pltpu.SideEffectType.DATAFLOW_SIDE_EFFECTING),
    )(*[_hbm(s) for s in srcs], *[_hbm(l) for l in lands], after)
    return out[0], out[1], list(out[2:2 + nl]), out[-1]


def _exchange_wait(name, plan, srcs, started, after):
    send_sems, recv_sems, lands, _ = started
    ns, nl = len(srcs), len(lands)
    after = list(after) if isinstance(after, (list, tuple)) else [after]

    def body(*refs):
        for send, recv in _planned_copies(plan, refs[:ns], refs[ns:ns + nl], refs[ns + nl], refs[ns + nl + 1]):
            send.wait_send()
            recv.wait_recv()

    return pl.pallas_call(
        body, name=name, out_shape=[pltpu.HBM(l.shape, l.dtype) for l in lands],
        in_specs=[_HBM_SPEC] * (ns + nl) + [_SEM_SPEC, _SEM_SPEC] + [pl.BlockSpec(memory_space=pl.ANY)] * len(after),
        out_specs=[_HBM_SPEC] * nl, input_output_aliases={ns + i: i for i in range(nl)},
        compiler_params=pltpu.CompilerParams(has_side_effects=pltpu.SideEffectType.DATAFLOW_SIDE_EFFECTING),
    )(*[_hbm(s) for s in srcs], *lands, send_sems, recv_sems, *after)


def _like(arrays, lead, dtype=None):
    return [jax.ShapeDtypeStruct(tuple(lead) + a.shape[-2:], dtype or a.dtype) for a in arrays]


class _StepExchanges:
    def __init__(self, mats, conv_w):
        x, y, c = _place()
        self.place = jnp.stack([c, 2 * x + y]).astype(jnp.int32)
        shards = [w.astype(BF16).reshape(2, w.shape[0] // 2, w.shape[1]) for w in mats]
        self._in_shard = shards[:1]
        self._in = _exchange_start("w_in_allgather_start", _plan_first_hop, 3, self._in_shard,
                                   _like(self._in_shard, (N_CHIPS, 2)), shards[0])
        self.zero = self._in[3]
        taps = jnp.pad(conv_w, ((0, 8 - conv_w.shape[0]), (0, 128 - conv_w.shape[1])))
        self._rest_shards = shards[1:] + [jnp.stack([taps, jnp.zeros_like(taps)])]
        self._taps_shape = conv_w.shape
        self._groups = {}

    def w_in(self, after):
        landed = _exchange_wait("w_in_allgather_wait", _plan_first_hop, self._in_shard, self._in,
                                list(after) + self._rest_shards)
        (w_in,) = _allgather_finish("w_in_allgather_finish", self._in_shard, landed, [True])
        self._rest = _exchange_start("rest_allgather_start", _plan_first_hop, 3 * len(self._rest_shards),
                                     self._rest_shards, _like(self._rest_shards, (N_CHIPS, 2)), w_in)
        self.zero = self._rest[3]
        return w_in.reshape(N_CHIPS, 2 * w_in.shape[2], w_in.shape[3])

    def rest_weights(self, after):
        landed = _exchange_wait("rest_allgather_wait", _plan_first_hop, self._rest_shards, self._rest, after)
        kv, out, up, down, taps = _allgather_finish("rest_allgather_finish", self._rest_shards, landed,
                                                    [True, True, False, False, True])
        self._up_down = _exchange_start("up_down_pass_on_start", _plan_pass_on, 6, [], None, self.zero, lands=[up, down])
        self.zero = self._up_down[3]
        k, w = self._taps_shape
        taps = taps[:, 0, :k, :w].transpose(1, 0, 2).reshape(k, N_CHIPS * w)
        return [g.reshape(N_CHIPS, 2 * g.shape[2], g.shape[3]) for g in (kv, out)], taps

    def up_down(self, after):
        full = _exchange_wait("up_down_pass_on_wait", _plan_pass_on, [], self._up_down, after)
        return [g.reshape(N_CHIPS, 2 * g.shape[2], g.shape[3]) for g in full]

    def send_grads(self, key, grads):
        grads = list(grads)
        started = _exchange_start(f"{key}_grads_to_sibling_start", _plan_other_half_to_sibling, len(grads), grads,
                                  _like(grads, (N_CHIPS,)), self.zero)
        self._groups[key] = dict(grads=grads, to_sibling=started)
        self.zero = started[3]

    def grads_at_sibling(self, key, after):
        group = self._groups[key]
        grads = group["grads"]
        group["from_sibling"] = _exchange_wait(f"{key}_grads_to_sibling_wait", _plan_other_half_to_sibling, grads,
                                               group["to_sibling"], after)
        group["partials"] = _chip_sums_bf16(f"{key}_chip_sums", grads, group["from_sibling"], self.place)
        group["to_chips"] = _exchange_start(f"{key}_grads_to_chips_start", _plan_to_other_chips, 3 * len(grads),
                                            group["partials"], _like(group["partials"], (3,)), self.zero)
        self.zero = group["to_chips"][3]

    def grads_summed(self, key, after):
        group = self._groups[key]
        from_chips = _exchange_wait(f"{key}_grads_to_chips_wait", _plan_to_other_chips, group["partials"],
                                    group["to_chips"], after)
        return _final_sums(f"{key}_final_sums", group["grads"], group["from_sibling"], from_chips, self.place)

    def send_sums(self, key, sums):
        self._groups[key + "_sums"] = _exchange_start(f"{key}_sums_to_sibling_start", _plan_own_half_to_sibling,
                                                      len(sums), [], None, self.zero, lands=list(sums))
        self.zero = self._groups[key + "_sums"][3]

    def whole_sums(self, key, after):
        full = _exchange_wait(f"{key}_sums_to_sibling_wait", _plan_own_half_to_sibling, [], self._groups[key + "_sums"], after)
        return [t.reshape(2 * t.shape[1], t.shape[2]) for t in full]

    def send_small(self, block):
        self._small = block
        self._small_started = _exchange_start("small_grads_start", _plan_to_all, 7, [block],
                                              [jax.ShapeDtypeStruct((8,) + block.shape, block.dtype)], self.zero)
        self.zero = self._small_started[3]

    def small_summed(self, after):
        x, y, c = _place()
        (landed,) = _exchange_wait("small_grads_wait", _plan_to_all, [self._small], self._small_started, after)
        blocks = lax.dynamic_update_index_in_dim(landed, self._small, 4 * x + 2 * y + c, 0)
        return _sum_blocks("small_sum", blocks)


def _rope_tables(positions):
    half = HEAD // 2
    inv_freq = jnp.float32(ROPE_THETA) ** (-(jnp.arange(half, dtype=F32) * 2.0 / HEAD))
    ang = positions.astype(F32)[:, None] * inv_freq
    cos, sin = jnp.cos(ang), jnp.sin(ang)
    return jnp.tile(cos, (1, 4)), jnp.tile(jnp.concatenate([-sin, sin], axis=1), (1, 2))


def _local_step(x, mem, positions, target, gains, ex):
    g_pre_mix, g_mem, g_a, g_c, g_x, g_post_mix, g_pre_mlp, g_post_mlp = gains
    tm = ROW_TILE
    cos, sin = _rope_tables(positions)
    h = _pre_norm(x, g_pre_mix, ex.zero, tm)
    w_in = ex.w_in([h, cos, sin])

    q, k, v, bcu, qx = _in_proj_fwd(h, w_in, cos, sin, ex.zero, tm)
    ya, lse = _attn_fwd(q, k, v)
    (w_kv, w_out), conv_w = ex.rest_weights(lse)
    w_kv, w_out = (w.reshape(N_CHIPS * w.shape[1], w.shape[2]) for w in (w_kv, w_out))
    memn, mkv = _memkv_fwd(mem, g_mem, w_kv, ex.zero)
    yx, ycat, y2, x1 = _mix_fwd(ya, bcu, qx, mkv, conv_w, g_a, g_c, g_x, w_out, g_post_mix, x, tm)
    w_up, w_down = ex.up_down(x1)
    w_down = w_down.reshape(N_CHIPS * w_down.shape[1], w_down.shape[2])
    h2, f, du, df2, dx1, dg_pre_mlp, dg_post_mlp, loss = _mlp_fwd_bwd(x1, target, g_pre_mlp, g_post_mlp, w_up, w_down,
                                                                      MLP_ROW_TILE)
    gw_down = _weight_grad("grad_w_down", f, df2, True, ex.zero)
    gw_up = _weight_grad("grad_w_up", h2, du, False, ex.zero)
    ex.send_grads("early", [gw_up, gw_down])

    dy2, dya, delta, tail, dmkv, g_conv, dg_post_mix, dg_a, dg_c, dg_x = _mixer_bwd(
        dx1, y2, ya, yx, bcu, qx, mkv, conv_w, g_a, g_c, g_x, w_out, g_post_mix, ex.zero, tm)
    ex.grads_at_sibling("early", dy2)
    gw_out = _weight_grad("grad_w_out", ycat, dy2, True, ex.zero)
    gw_kv, dg_mem = _memkv_bwd(mem, g_mem, w_kv, dmkv)
    ex.send_grads("mid", [gw_out, gw_kv])
    dqkv = _attn_bwd(q, k, v, dya, lse, delta, ex.zero)
    ex.grads_at_sibling("mid", dqkv[0])
    dproj, grad_x, dg_pre_mix = _in_proj_bwd(dqkv, tail, cos, sin, w_in, x, g_pre_mix, dx1, ex.zero, tm)
    gain_grads = [dg_pre_mix, dg_mem, dg_a, dg_c, dg_x, dg_post_mix, dg_pre_mlp, dg_post_mlp]
    ex.send_small(_pack_small(gain_grads, g_conv, loss))
    gw_in = _weight_grad_w_in(h, dproj)
    ex.send_grads("late", [gw_in])
    return grad_x


def _pack_small(gains, conv, scalar=None):
    rows = [jnp.pad(g, ((0, 0), (0, D_MODEL - g.shape[1]))) for g in gains]
    rows.append(jnp.pad(conv, ((0, 0), (0, D_MODEL - conv.shape[1]))))
    last = jnp.zeros((SMALL_ROWS - 8 - conv.shape[0], D_MODEL), F32)
    rows.append(last if scalar is None else last.at[0:1, 0:1].set(scalar))
    return jnp.concatenate(rows, axis=0)


def _unpack_small(block, gain_widths, conv_width):
    gains = [block[i:i + 1, :w] for i, w in enumerate(gain_widths)]
    return gains, block[8:11, :conv_width], block[11, 0]


def kernel(x, mem, positions, g_pre_mix, g_mem, w_in, w_mem_kv, conv_w, g_attn_out, g_conv_out, g_xattn_out, w_out, g_post_mix, g_pre_mlp, w_up, w_down, g_post_mlp, loss_target, m_g_pre_mix, m_g_mem, m_w_in, m_w_mem_kv, m_conv_w, m_g_attn_out, m_g_conv_out, m_g_xattn_out, m_w_out, m_g_post_mix, m_g_pre_mlp, m_w_up, m_w_down, m_g_post_mlp, v_g_pre_mix, v_g_mem, v_w_in, v_w_mem_kv, v_conv_w, v_g_attn_out, v_g_conv_out, v_g_xattn_out, v_w_out, v_g_post_mix, v_g_pre_mlp, v_w_up, v_w_down, v_g_post_mlp):
    cx, cy, cc = _place()
    chip = 2 * cx + cy
    gains = [g_pre_mix, g_mem, g_attn_out, g_conv_out, g_xattn_out, g_post_mix, g_pre_mlp, g_post_mlp]
    gains_m = [m_g_pre_mix, m_g_mem, m_g_attn_out, m_g_conv_out, m_g_xattn_out, m_g_post_mix, m_g_pre_mlp, m_g_post_mlp]
    gains_v = [v_g_pre_mix, v_g_mem, v_g_attn_out, v_g_conv_out, v_g_xattn_out, v_g_post_mix, v_g_pre_mlp, v_g_post_mlp]
    gain_widths = [g.shape[1] for g in gains]
    mats = [w_in[0], w_mem_kv[0], w_out[0], w_up[0], w_down[0]]
    mats_m = [m_w_in[0], m_w_mem_kv[0], m_w_out[0], m_w_up[0], m_w_down[0]]
    mats_v = [v_w_in[0], v_w_mem_kv[0], v_w_out[0], v_w_up[0], v_w_down[0]]

    ex = _StepExchanges(mats, conv_w[0])
    grad_x = _local_step(x[0], mem[0], positions[0], loss_target[0], gains, ex)

    ex.send_sums("four", ex.grads_summed("early", ex.zero) + ex.grads_summed("mid", ex.zero))
    ex.grads_at_sibling("late", ex.zero)
    up_sum, down_sum, out_sum, kv_sum = ex.whole_sums("four", ex.zero)
    params = lambda a, g: (mats[a], g, mats_m[a], mats_v[a])
    new_up, new_down = _adamw("adamw_up_down", [params(3, up_sum), params(4, down_sum)], ex.zero)
    new_out, new_kv = _adamw("adamw_out_kv", [params(2, out_sum), params(1, kv_sum)], ex.zero)

    small, total = _small_update(ex.small_summed(new_kv[1]), chip.reshape(1).astype(jnp.int32), gains, gains_m,
                                 gains_v, conv_w[0], m_conv_w[0], v_conv_w[0])

    ex.send_sums("last", ex.grads_summed("late", small[0][1]))
    (in_sum,) = ex.whole_sums("last", ex.zero)
    (new_in,) = _adamw("adamw_in", [params(0, in_sum)], in_sum)
    mat_new = [new_in, new_kv, new_out, new_up, new_down]

    order = ["g_pre_mix", "g_mem", "w_in", "w_mem_kv", "conv_w", "g_attn_out", "g_conv_out", "g_xattn_out", "w_out",
             "g_post_mix", "g_pre_mlp", "w_up", "w_down", "g_post_mlp"]
    gain_names = ["g_pre_mix", "g_mem", "g_attn_out", "g_conv_out", "g_xattn_out", "g_post_mix", "g_pre_mlp", "g_post_mlp"]
    mat_names = ["w_in", "w_mem_kv", "w_out", "w_up", "w_down"]

    def leaf(kind, name):
        if name in gain_names:
            return small[gain_names.index(name)][kind]
        if name == "conv_w":
            return small[len(gain_names)][kind][None]
        return mat_new[mat_names.index(name)][kind][None]

    return (total[0, 0], grad_x[None], *[leaf(kind, name) for kind in range(4) for name in order])
```

```python
import jax
import jax.numpy as jnp
from jax import lax
from jax.experimental import pallas as pl
from jax.experimental.pallas import tpu as pltpu

F32, BF16 = jnp.float32, jnp.bfloat16

D_MODEL = 1024
ATTN_W = 512
CONV_W = 256
XATTN_W = 256
PROJ_W = 3 * ATTN_W + 3 * CONV_W + XATTN_W
D_FF = 4096
HEAD = 64
N_BACK = 128
DILATIONS = (1, 4, 16)
ROPE_THETA = 10000.0
EPS = 1e-6
NEG_INF = -1e30
SCALE = HEAD ** -0.5
N_CHIPS = 4
SHARD_IN = PROJ_W // N_CHIPS
SHARD_FF = D_FF // N_CHIPS

ADAM_LR, ADAM_B1, ADAM_B2, ADAM_EPS, ADAM_WD, ADAM_STEP = 0.001, 0.9, 0.999, 1e-08, 0.01, 10

VMEM_LIMIT_V7X = 56 * 1024 * 1024
ROW_TILE = 512
MLP_ROW_TILE = 256
ADAMW_ROW_TILE = 256
SMALL_ROWS = 16

NT = (((1,), (1,)), ((), ()))
TN = (((0,), (0,)), ((), ()))
MESH = pl.DeviceIdType.MESH


def _params(*sem):
    return pltpu.CompilerParams(dimension_semantics=sem, vmem_limit_bytes=VMEM_LIMIT_V7X)


def _resident(shape):
    return pl.BlockSpec(shape, lambda *_: (0,) * len(shape), pipeline_mode=pl.Buffered(1))


def _rows(tm, width):
    return pl.BlockSpec((tm, width), lambda i: (i, 0))


def _rms_hat(x):
    r = lax.rsqrt(jnp.mean(x * x, axis=-1, keepdims=True) + EPS)
    return x * r, r


def _rms_bwd(xhat, r, g, dy):
    gdy = dy * g
    return r * (gdy - xhat * jnp.mean(xhat * gdy, axis=-1, keepdims=True))


def _rope128(t, cos, sin_signed, inverse):
    lane = lax.broadcasted_iota(jnp.int32, t.shape, 1)
    first_half = (lane % HEAD) < (HEAD // 2)
    rot = jnp.where(first_half, pltpu.roll(t, 128 - HEAD // 2, 1), pltpu.roll(t, HEAD // 2, 1))
    return t * cos - rot * sin_signed if inverse else t * cos + rot * sin_signed


def _pre_norm(x, g, after, tm):
    S = x.shape[0]

    def body(x_ref, g_ref, after_ref, h_ref):
        h_ref[...] = (_rms_hat(x_ref[...])[0] * g_ref[...]).astype(BF16)

    return pl.pallas_call(
        body, name="pre_norm", grid=(S // tm,),
        in_specs=[_rows(tm, D_MODEL), _resident((1, D_MODEL)), pl.BlockSpec(memory_space=pl.ANY)],
        out_specs=_rows(tm, D_MODEL), out_shape=jax.ShapeDtypeStruct((S, D_MODEL), BF16),
        compiler_params=_params("parallel"),
    )(x, g, after)


def _side_by_side(w_hbm, w_full, sems):
    @pl.when(pl.program_id(0) == 0)
    def _():
        copies = [pltpu.make_async_copy(w_hbm.at[j], w_full.at[:, pl.ds(SHARD_IN * j, SHARD_IN)], sems.at[j])
                  for j in range(N_CHIPS)]
        for cp in copies:
            cp.start()
        for cp in copies:
            cp.wait()


def _in_proj_fwd(h, w_in, cos, sin, after, tm):
    S = h.shape[0]

    def body(h_ref, w_hbm, cos_ref, sin_ref, after_ref, q_ref, k_ref, v_ref, bcu_ref, qx_ref, proj, w_full, sems):
        _side_by_side(w_hbm, w_full, sems)
        proj[...] = jnp.dot(h_ref[...], w_full[...], preferred_element_type=F32)
        c, s = cos_ref[...], sin_ref[...]
        for j in range(ATTN_W // 128):
            lo = 128 * j
            q_ref[:, lo:lo + 128] = _rope128(proj[:, lo:lo + 128], c, s, False) * SCALE
            k_ref[:, lo:lo + 128] = _rope128(proj[:, ATTN_W + lo:ATTN_W + lo + 128], c, s, False)
        v_ref[...] = proj[:, 2 * ATTN_W:3 * ATTN_W]
        bcu_ref[...] = proj[:, 3 * ATTN_W:3 * ATTN_W + 3 * CONV_W]
        qx_ref[...] = proj[:, 3 * ATTN_W + 3 * CONV_W:PROJ_W].astype(BF16)

    return pl.pallas_call(
        body, name="in_proj_fwd", grid=(S // tm,),
        in_specs=[_rows(tm, D_MODEL), pl.BlockSpec(memory_space=pl.ANY), _rows(tm, 128), _rows(tm, 128),
                  pl.BlockSpec(memory_space=pl.ANY)],
        out_specs=[_rows(tm, ATTN_W), _rows(tm, ATTN_W), _rows(tm, ATTN_W), _rows(tm, 3 * CONV_W), _rows(tm, XATTN_W)],
        out_shape=[jax.ShapeDtypeStruct((S, ATTN_W), F32), jax.ShapeDtypeStruct((S, ATTN_W), F32),
                   jax.ShapeDtypeStruct((S, ATTN_W), F32), jax.ShapeDtypeStruct((S, 3 * CONV_W), F32),
                   jax.ShapeDtypeStruct((S, XATTN_W), BF16)],
        scratch_shapes=[pltpu.VMEM((tm, PROJ_W), F32), pltpu.VMEM((D_MODEL, PROJ_W), BF16),
                        pltpu.SemaphoreType.DMA((N_CHIPS,))],
        compiler_params=_params("arbitrary"),
    )(h, w_in, cos, sin, after)


def _memkv_fwd(mem, g_mem, w_kv, after):
    n_mem = mem.shape[0]

    def body(mem_ref, g_ref, w_ref, after_ref, mn_ref, kv_ref):
        mhat, _ = _rms_hat(mem_ref[...])
        mn = (mhat * g_ref[...]).astype(BF16)
        mn_ref[...] = mn
        kv_ref[...] = jnp.dot(mn, w_ref[...], preferred_element_type=F32).astype(BF16)

    vmem = pl.BlockSpec(memory_space=pltpu.VMEM)
    return pl.pallas_call(
        body, name="memkv_fwd", in_specs=[vmem, vmem, vmem, pl.BlockSpec(memory_space=pl.ANY)], out_specs=[vmem, vmem],
        out_shape=[jax.ShapeDtypeStruct((n_mem, D_MODEL), BF16), jax.ShapeDtypeStruct((n_mem, 2 * XATTN_W), BF16)],
        compiler_params=pltpu.CompilerParams(vmem_limit_bytes=VMEM_LIMIT_V7X),
    )(mem, g_mem, w_kv, after)


def _fill_band_bias(bias):
    row = lax.broadcasted_iota(jnp.int32, (N_BACK, 2 * N_BACK), 0)
    col = lax.broadcasted_iota(jnp.int32, (N_BACK, 2 * N_BACK), 1)
    band = (col >= row) & (col <= row + N_BACK)
    bias[1] = jnp.where(band, 0.0, NEG_INF)
    bias[0] = jnp.where(band & (col >= N_BACK), 0.0, NEG_INF)


def _strided(start, size, d):
    return pl.ds(start, size) if d == 1 else pl.ds(start, size, stride=d)


def _group_starts(g, G, nb, d):
    t0 = g * G
    r, n0 = lax.shift_right_logical(t0, nb.bit_length() - 1), lax.bitwise_and(t0, nb - 1)
    first = r + n0 * (N_BACK * d)
    before = r + jnp.maximum(n0 - 1, 0) * (N_BACK * d)
    starts = [before] + [first + u * (N_BACK * d) for u in range(G)]
    if d == 1:
        starts = [pl.multiple_of(st, N_BACK) for st in starts]
    return starts, n0


def _step_blocks(i, U, nb, d):
    G = min(U, nb)
    whole = G == nb
    row_blocks, blocks = [], []
    for grp in range(U // G):
        starts, n0 = _group_starts(i * (U // G) + grp, G, nb, d)
        base = len(row_blocks)
        if whole:
            row_blocks += [_strided(st, N_BACK, d) for st in starts[1:]]
            blocks += [(base + max(u - 1, 0), base + u, min(u, 1)) for u in range(G)]
        else:
            row_blocks += [_strided(st, N_BACK, d) for st in starts]
            blocks += [(base + u, base + u + 1, jnp.minimum(n0, 1) if u == 0 else 1) for u in range(G)]
    return row_blocks, blocks


def _by_head(a, b):
    lane = lax.broadcasted_iota(jnp.int32, (a.shape[0], 2 * HEAD), 1)
    return jnp.where(lane < HEAD, a, b)


def _head_only(t, hh):
    lane = lax.broadcasted_iota(jnp.int32, t.shape, 1)
    return jnp.where((lane < HEAD) == (hh == 0), t, jnp.zeros_like(t))


def _stack_heads(t):
    return jnp.concatenate([_head_only(t, 0), _head_only(t, 1)], axis=0)


def _head_columns(t):
    return jnp.concatenate([t[:, 0:1], t[:, HEAD:HEAD + 1]], axis=0)


def _unstack(t):
    return _by_head(t[:N_BACK], t[N_BACK:])


def _unstack_columns(t):
    return _by_head(jnp.broadcast_to(t[:N_BACK], (N_BACK, 2 * HEAD)), jnp.broadcast_to(t[N_BACK:], (N_BACK, 2 * HEAD)))


FWD_BLOCKS_PER_STEP = 4
BWD_BLOCKS_PER_STEP = 4
BWD_CHUNK = 64


def _attn_fwd(q, k, v):
    S = q.shape[0]
    U = FWD_BLOCKS_PER_STEP

    def body(q_ref, k_ref, v_ref, y_ref, m_ref, l_scr, bias):
        _fill_band_bias(bias)
        for g, d in enumerate(DILATIONS):
            nb = S // d // N_BACK
            first_pattern, last_pattern = g == 0, g == len(DILATIONS) - 1

            def step(i, carry, d=d, nb=nb, first_pattern=first_pattern, last_pattern=last_pattern):
                row_blocks, blocks = _step_blocks(i, U, nb, d)
                kb = [k_ref[r, :].astype(BF16) for r in row_blocks]
                ss = []
                for before, own, which in blocks:
                    kw = jnp.concatenate([kb[before], kb[own]], 0)
                    qs = _stack_heads(q_ref[row_blocks[own], :].astype(BF16))
                    b = bias[which]
                    ss.append(lax.dot_general(qs, kw, NT, preferred_element_type=F32) + jnp.concatenate([b, b], axis=0))
                ms = [jnp.max(s, axis=1, keepdims=True) for s in ss]
                ps = [jnp.exp(s - m) for s, m in zip(ss, ms)]
                ls = [jnp.sum(p, axis=1, keepdims=True) for p in ps]
                vb = [v_ref[r, :].astype(BF16) for r in row_blocks]
                os_ = [jnp.dot(ps[u].astype(BF16), jnp.concatenate([vb[before], vb[own]], 0), preferred_element_type=F32)
                       for u, (before, own, _) in enumerate(blocks)]
                for u, (_, own, _) in enumerate(blocks):
                    o_g, m_g, l_g = _unstack(os_[u]), _unstack_columns(ms[u]), _unstack_columns(ls[u])
                    r = row_blocks[own]
                    if first_pattern:
                        m_new, l_new, acc = m_g, l_g, o_g
                    else:
                        m_old = m_ref[r, :]
                        m_new = jnp.maximum(m_old, m_g)
                        alpha, beta = jnp.exp(m_old - m_new), jnp.exp(m_g - m_new)
                        l_new = l_scr[r, :] * alpha + l_g * beta
                        acc = y_ref[r, :] * alpha + o_g * beta
                    if last_pattern:
                        y_ref[r, :] = acc / l_new
                        m_ref[r, :] = m_new + jnp.log(l_new)
                    else:
                        y_ref[r, :] = acc
                        m_ref[r, :] = m_new
                        l_scr[r, :] = l_new
                return carry

            lax.fori_loop(0, d * nb // U, step, 0)

    col = pl.BlockSpec((S, 2 * HEAD), lambda j: (0, j))
    return pl.pallas_call(
        body, name="attn_fwd", grid=(q.shape[1] // (2 * HEAD),),
        in_specs=[col, col, col], out_specs=[col, col],
        out_shape=[jax.ShapeDtypeStruct(q.shape, F32)] * 2,
        scratch_shapes=[pltpu.VMEM((S, 2 * HEAD), F32), pltpu.VMEM((2, N_BACK, 2 * N_BACK), F32)],
        compiler_params=_params("parallel"),
    )(q, k, v)


def _attn_bwd(q, k, v, dy, lse, delta, after):
    S = q.shape[0]
    U = BWD_BLOCKS_PER_STEP

    def body(q_ref, k_ref, v_ref, dy_ref, lse_ref, delta_ref, after_ref, dq_ref, dk_ref, dv_ref, bias):
        _fill_band_bias(bias)
        dk_ref[...] = jnp.zeros_like(dk_ref)
        dv_ref[...] = jnp.zeros_like(dv_ref)
        for g, d in enumerate(DILATIONS):
            nb = S // d // N_BACK

            def step(i, carry, d=d, nb=nb, g=g):
                row_blocks, blocks = _step_blocks(i, U, nb, d)
                kb = [k_ref[r, :].astype(BF16) for r in row_blocks]
                vb = [v_ref[r, :].astype(BF16) for r in row_blocks]
                kws = [jnp.concatenate([kb[before], kb[own]], 0) for before, own, _ in blocks]
                vws = [jnp.concatenate([vb[before], vb[own]], 0) for before, own, _ in blocks]
                qss = [_stack_heads(q_ref[row_blocks[own], :].astype(BF16)) for _, own, _ in blocks]
                doss = [_stack_heads(dy_ref[row_blocks[own], :].astype(BF16)) for _, own, _ in blocks]
                ss = [lax.dot_general(qss[u], kws[u], NT, preferred_element_type=F32) for u in range(U)]
                dps = [lax.dot_general(doss[u], vws[u], NT, preferred_element_type=F32) for u in range(U)]
                pbs, dss = [], []
                for u, (_, own, which) in enumerate(blocks):
                    lse_c = _head_columns(lse_ref[row_blocks[own], :])
                    delta_c = _head_columns(delta_ref[row_blocks[own], :])
                    p_parts, ds_parts = [], []
                    for r0 in range(0, 2 * N_BACK, BWD_CHUNK):
                        r = slice(r0, r0 + BWD_CHUNK)
                        mask = bias[which, r0 % N_BACK:r0 % N_BACK + BWD_CHUNK, :]
                        p_r = jnp.exp(ss[u][r] + mask - lse_c[r])
                        p_parts.append(p_r.astype(BF16))
                        ds_parts.append((p_r * (dps[u][r] - delta_c[r])).astype(BF16))
                    pbs.append(jnp.concatenate(p_parts, axis=0))
                    dss.append(jnp.concatenate(ds_parts, axis=0))
                dqs = [jnp.dot(dss[u], kws[u], preferred_element_type=F32) for u in range(U)]
                dkws = [lax.dot_general(dss[u], qss[u], TN, preferred_element_type=F32) for u in range(U)]
                dvws = [lax.dot_general(pbs[u], doss[u], TN, preferred_element_type=F32) for u in range(U)]
                dk_parts, dv_parts = [None] * len(row_blocks), [None] * len(row_blocks)
                for u, (before, own, _) in enumerate(blocks):
                    dq = _unstack(dqs[u])
                    if g == 0:
                        dq_ref[row_blocks[own], :] = dq
                    else:
                        dq_ref[row_blocks[own], :] += dq
                    for idx, dkp, dvp in ((before, dkws[u][:N_BACK], dvws[u][:N_BACK]),
                                          (own, dkws[u][N_BACK:], dvws[u][N_BACK:])):
                        dk_parts[idx] = dkp if dk_parts[idx] is None else dk_parts[idx] + dkp
                        dv_parts[idx] = dvp if dv_parts[idx] is None else dv_parts[idx] + dvp
                for idx, r in enumerate(row_blocks):
                    dk_ref[r, :] += dk_parts[idx]
                    dv_ref[r, :] += dv_parts[idx]
                return carry

            lax.fori_loop(0, d * nb // U, step, 0)

    col = pl.BlockSpec((S, 2 * HEAD), lambda j: (0, j))
    return pl.pallas_call(
        body, name="attn_bwd", grid=(q.shape[1] // (2 * HEAD),),
        in_specs=[col] * 6 + [pl.BlockSpec(memory_space=pl.ANY)], out_specs=[col] * 3,
        out_shape=[jax.ShapeDtypeStruct(q.shape, F32)] * 3,
        scratch_shapes=[pltpu.VMEM((2, N_BACK, 2 * N_BACK), F32)],
        compiler_params=_params("parallel"),
    )(q, k, v, dy, lse, delta, after)


def _shift_down(z, before, k):
    row = lax.broadcasted_iota(jnp.int32, z.shape, 0)
    out = pltpu.roll(z, k, 0)
    for i in range(k):
        out = jnp.where(row == i, before[8 - k + i:8 - k + i + 1, :], out)
    return out


def _shift_up(z, after, k):
    rows = z.shape[0]
    row = lax.broadcasted_iota(jnp.int32, z.shape, 0)
    out = pltpu.roll(z, rows - k, 0)
    for i in range(k):
        out = jnp.where(row == rows - k + i, after[i:i + 1, :], out)
    return out


def _conv_fwd(bcu, before, is_first, w):
    b, c, u = bcu[:, 0:CONV_W], bcu[:, CONV_W:2 * CONV_W], bcu[:, 2 * CONV_W:3 * CONV_W]
    z = c * u
    zb = jnp.where(is_first, 0.0, before[:, CONV_W:2 * CONV_W] * before[:, 2 * CONV_W:3 * CONV_W])
    z1, z2 = _shift_down(z, zb, 1), _shift_down(z, zb, 2)
    cv = w[0:1, :] * z2 + w[1:2, :] * z1 + w[2:3, :] * z
    return b, c, u, z, z1, z2, cv


def _halo_before(tm, width):
    return pl.BlockSpec((8, width), lambda i: (jnp.maximum(i * (tm // 8) - 1, 0), 0))


def _mix_fwd(ya, bcu, qx, mkv, conv_w, g_a, g_c, g_x, w_out, g_post, x, tm):
    S = x.shape[0]

    def body(ya_ref, bcu_ref, before_ref, qx_ref, mkv_ref, cw_ref, ga_ref, gc_ref, gx_ref,
             wo_ref, gp_ref, x_ref, yx_ref, ycat_ref, y2_ref, x1_ref):
        ya = ya_ref[...]
        b, _, _, _, _, _, cv = _conv_fwd(bcu_ref[...], before_ref[...], pl.program_id(0) == 0, cw_ref[...])
        yc = b * cv

        qxb, mkvb = qx_ref[...], mkv_ref[...]
        for hd in range(XATTN_W // HEAD):
            sl = slice(HEAD * hd, HEAD * (hd + 1))
            s = lax.dot_general(qxb[:, sl], mkvb[:, sl], NT, preferred_element_type=F32) * SCALE
            mx = jnp.max(s, axis=1, keepdims=True)
            p = jnp.exp(s - mx)
            l = jnp.sum(p, axis=1, keepdims=True)
            vm = mkvb[:, XATTN_W + HEAD * hd:XATTN_W + HEAD * (hd + 1)]
            yx_ref[:, sl] = jnp.dot(p.astype(BF16), vm, preferred_element_type=F32) / l
        yx = yx_ref[...]

        ycat_ref[:, 0:ATTN_W] = (_rms_hat(ya)[0] * ga_ref[...]).astype(BF16)
        ycat_ref[:, ATTN_W:ATTN_W + CONV_W] = (_rms_hat(yc)[0] * gc_ref[...]).astype(BF16)
        ycat_ref[:, ATTN_W + CONV_W:D_MODEL] = (_rms_hat(yx)[0] * gx_ref[...]).astype(BF16)
        y2 = jnp.dot(ycat_ref[...], wo_ref[...], preferred_element_type=F32)
        y2_ref[...] = y2
        x1_ref[...] = x_ref[...] + _rms_hat(y2)[0] * gp_ref[...]

    n_mem = mkv.shape[0]
    return pl.pallas_call(
        body, name="mix_fwd", grid=(S // tm,),
        in_specs=[_rows(tm, ATTN_W), _rows(tm, 3 * CONV_W), _halo_before(tm, 3 * CONV_W), _rows(tm, XATTN_W),
                  _resident((n_mem, 2 * XATTN_W)), _resident((3, CONV_W)), _resident((1, ATTN_W)),
                  _resident((1, CONV_W)), _resident((1, XATTN_W)), _resident((D_MODEL, D_MODEL)),
                  _resident((1, D_MODEL)), _rows(tm, D_MODEL)],
        out_specs=[_rows(tm, XATTN_W), _rows(tm, D_MODEL), _rows(tm, D_MODEL), _rows(tm, D_MODEL)],
        out_shape=[jax.ShapeDtypeStruct((S, XATTN_W), F32), jax.ShapeDtypeStruct((S, D_MODEL), BF16),
                   jax.ShapeDtypeStruct((S, D_MODEL), F32), jax.ShapeDtypeStruct((S, D_MODEL), F32)],
        compiler_params=_params("parallel"),
    )(ya, bcu, bcu, qx, mkv, conv_w, g_a, g_c, g_x, w_out, g_post, x)


def _mlp_fwd_bwd(x1, target, g_pre, g_post, w_up, w_down, tm):
    S = x1.shape[0]
    n_ff = D_FF // SHARD_FF

    def body(x1_ref, t_ref, gpre_ref, gpost_ref, wup_ref, wdn_ref,
             h2_ref, f_ref, du_ref, df2_ref, dx1_ref, dgpre_ref, dgpost_ref, loss_ref, u_scr):
        @pl.when(pl.program_id(0) == 0)
        def _():
            dgpre_ref[...] = jnp.zeros_like(dgpre_ref)
            dgpost_ref[...] = jnp.zeros_like(dgpost_ref)
            loss_ref[...] = jnp.zeros_like(loss_ref)

        x1 = x1_ref[...]
        x1hat, r1 = _rms_hat(x1)
        h2 = (x1hat * gpre_ref[...]).astype(BF16)
        h2_ref[...] = h2
        f2 = jnp.zeros((tm, D_MODEL), F32)
        for j in range(n_ff):
            cols = slice(SHARD_FF * j, SHARD_FF * (j + 1))
            u = jnp.maximum(jnp.dot(h2, wup_ref[j], preferred_element_type=F32), 0.0)
            u_scr[:, cols] = u
            f = (u * u).astype(BF16)
            f_ref[:, cols] = f
            f2 = f2 + jnp.dot(f, wdn_ref[cols, :], preferred_element_type=F32)
        f2hat, r2 = _rms_hat(f2)
        err = x1 + f2hat * gpost_ref[...] - t_ref[...]
        loss_ref[...] += 0.5 * jnp.sum(jnp.mean(err * err, axis=-1, keepdims=True), axis=0, keepdims=True)
        dx2 = err * (1.0 / D_MODEL)
        dgpost_ref[...] += jnp.sum(dx2 * f2hat, axis=0, keepdims=True)
        df2 = _rms_bwd(f2hat, r2, gpost_ref[...], dx2).astype(BF16)
        df2_ref[...] = df2
        dh2 = jnp.zeros((tm, D_MODEL), F32)
        for j in range(n_ff):
            cols = slice(SHARD_FF * j, SHARD_FF * (j + 1))
            df = lax.dot_general(df2, wdn_ref[cols, :], NT, preferred_element_type=F32)
            du = (2.0 * u_scr[:, cols] * df).astype(BF16)
            du_ref[:, cols] = du
            dh2 = dh2 + lax.dot_general(du, wup_ref[j], NT, preferred_element_type=F32)
        dgpre_ref[...] += jnp.sum(dh2 * x1hat, axis=0, keepdims=True)
        dx1_ref[...] = dx2 + _rms_bwd(x1hat, r1, gpre_ref[...], dh2)

    acc = pl.BlockSpec((1, D_MODEL), lambda i: (0, 0))
    return pl.pallas_call(
        body, name="mlp_fwd_bwd", grid=(S // tm,),
        in_specs=[_rows(tm, D_MODEL), _rows(tm, D_MODEL), _resident((1, D_MODEL)), _resident((1, D_MODEL)),
                  _resident((n_ff, D_MODEL, SHARD_FF)), _resident((D_FF, D_MODEL))],
        out_specs=[_rows(tm, D_MODEL), _rows(tm, D_FF), _rows(tm, D_FF), _rows(tm, D_MODEL), _rows(tm, D_MODEL),
                   acc, acc, pl.BlockSpec((1, 1), lambda i: (0, 0))],
        out_shape=[jax.ShapeDtypeStruct((S, D_MODEL), BF16), jax.ShapeDtypeStruct((S, D_FF), BF16),
                   jax.ShapeDtypeStruct((S, D_FF), BF16), jax.ShapeDtypeStruct((S, D_MODEL), BF16),
                   jax.ShapeDtypeStruct((S, D_MODEL), F32), jax.ShapeDtypeStruct((1, D_MODEL), F32),
                   jax.ShapeDtypeStruct((1, D_MODEL), F32), jax.ShapeDtypeStruct((1, 1), F32)],
        scratch_shapes=[pltpu.VMEM((tm, D_FF), F32)],
        compiler_params=_params("arbitrary"),
    )(x1, target, g_pre, g_post, w_up, w_down)


def _weight_grad(name, a, b, rows_sharded, after):
    S, K = a.shape
    N = b.shape[1]
    if rows_sharded:
        tk, tn = K // N_CHIPS, N
        a_spec = pl.BlockSpec((S, tk), lambda j: (0, j))
        b_spec = pl.BlockSpec((S, tn), lambda j: (0, 0), pipeline_mode=pl.Buffered(1))
    else:
        tk, tn = K, N // N_CHIPS
        a_spec = pl.BlockSpec((S, tk), lambda j: (0, 0), pipeline_mode=pl.Buffered(1))
        b_spec = pl.BlockSpec((S, tn), lambda j: (0, j))
    half = tk // 2

    def body(a_ref, b_ref, after_ref, o_ref):
        res = lax.dot_general(a_ref[...], b_ref[...], TN, preferred_element_type=F32)
        o_ref[0, 0] = res[:half]
        o_ref[1, 0] = res[half:]

    return pl.pallas_call(
        body, name=name, grid=(N_CHIPS,), in_specs=[a_spec, b_spec, pl.BlockSpec(memory_space=pl.ANY)],
        out_specs=pl.BlockSpec((2, 1, half, tn), lambda j: (0, j, 0, 0)),
        out_shape=jax.ShapeDtypeStruct((2, N_CHIPS, half, tn), F32),
        compiler_params=_params("parallel"),
    )(a, b, after)


def _weight_grad_w_in(h, dproj):
    S, K = h.shape
    step_w = 2 * 256
    n_steps = PROJ_W // step_w
    half = K // 2

    def body(a_ref, b_ref, o_ref):
        res = lax.dot_general(a_ref[...], b_ref[...], TN, preferred_element_type=F32)
        for step in range(n_steps):
            @pl.when(pl.program_id(0) == step)
            def _(step=step):
                lo = step * step_w
                while lo < (step + 1) * step_w:
                    chip = lo // SHARD_IN
                    hi = min((step + 1) * step_w, (chip + 1) * SHARD_IN)
                    for hh in range(2):
                        o_ref[hh, chip, :, lo - chip * SHARD_IN:hi - chip * SHARD_IN] = (
                            res[half * hh:half * (hh + 1), lo - step * step_w:hi - step * step_w])
                    lo = hi

    return pl.pallas_call(
        body, name="grad_w_in", grid=(n_steps,),
        in_specs=[pl.BlockSpec((S, K), lambda j: (0, 0), pipeline_mode=pl.Buffered(1)),
                  pl.BlockSpec((S, step_w), lambda j: (0, j))],
        out_specs=pl.BlockSpec((2, N_CHIPS, half, SHARD_IN), lambda j: (0, 0, 0, 0)),
        out_shape=jax.ShapeDtypeStruct((2, N_CHIPS, half, SHARD_IN), F32),
        compiler_params=_params("arbitrary"),
    )(h, dproj)


def _mixer_bwd(dx1, y2, ya, yx, bcu, qx, mkv, conv_w, g_a, g_c, g_x, w_out, g_post, after, tm):
    S = dx1.shape[0]
    n_mem = mkv.shape[0]
    n_tiles = S // tm

    def body(dx1_ref, y2_ref, ya_ref, yx_ref, bcu_ref, before_ref, qx_ref, mkv_ref, cw_ref, ga_ref, gc_ref, gx_ref,
             wo_ref, gp_ref, after_ref, dy2_ref, dya_ref, delta_ref, tail_ref, dmkv_ref, dcw_ref, dgp_ref, dga_ref,
             dgc_ref, dgx_ref, carry):
        step = pl.program_id(0)
        first_tile = step == n_tiles - 1

        @pl.when(step == 0)
        def _():
            for ref in (dmkv_ref, dcw_ref, dgp_ref, dga_ref, dgc_ref, dgx_ref, carry):
                ref[...] = jnp.zeros_like(ref)

        dx1 = dx1_ref[...]
        y2hat, r2 = _rms_hat(y2_ref[...])
        dgp_ref[...] += jnp.sum(dx1 * y2hat, axis=0, keepdims=True)
        dy2 = _rms_bwd(y2hat, r2, gp_ref[...], dx1).astype(BF16)
        dy2_ref[...] = dy2
        dycat = lax.dot_general(dy2, wo_ref[...], NT, preferred_element_type=F32)

        d_na = dycat[:, 0:ATTN_W]
        ya = ya_ref[...]
        yahat, ra = _rms_hat(ya)
        dga_ref[...] += jnp.sum(d_na * yahat, axis=0, keepdims=True)
        dya = _rms_bwd(yahat, ra, ga_ref[...], d_na)
        dya_ref[...] = dya
        prod = dya * ya
        hi = prod.astype(BF16)
        lo = (prod - hi.astype(F32)).astype(BF16)
        head_of = lambda axis: lax.shift_right_logical(lax.broadcasted_iota(jnp.int32, (ATTN_W, ATTN_W), axis),
                                                       HEAD.bit_length() - 1)
        ones = jnp.where(head_of(0) == head_of(1), 1.0, 0.0).astype(BF16)
        delta_ref[...] = jnp.dot(hi, ones, preferred_element_type=F32) + jnp.dot(lo, ones, preferred_element_type=F32)

        w = cw_ref[...]
        b, c, u, z, z1, z2, cv = _conv_fwd(bcu_ref[...], before_ref[...], first_tile, w)
        d_nc = dycat[:, ATTN_W:ATTN_W + CONV_W]
        ychat, rc = _rms_hat(b * cv)
        dgc_ref[...] += jnp.sum(d_nc * ychat, axis=0, keepdims=True)
        dyc = _rms_bwd(ychat, rc, gc_ref[...], d_nc)
        dcv = dyc * b
        behind = carry[...]
        dz = w[2:3, :] * dcv + w[1:2, :] * _shift_up(dcv, behind, 1) + w[0:1, :] * _shift_up(dcv, behind, 2)
        carry[...] = dcv[0:8, :]
        dcw_ref[0:1, :] += jnp.sum(dcv * z2, axis=0, keepdims=True)
        dcw_ref[1:2, :] += jnp.sum(dcv * z1, axis=0, keepdims=True)
        dcw_ref[2:3, :] += jnp.sum(dcv * z, axis=0, keepdims=True)
        tail_ref[:, 0:CONV_W] = (dyc * cv).astype(BF16)
        tail_ref[:, CONV_W:2 * CONV_W] = (dz * u).astype(BF16)
        tail_ref[:, 2 * CONV_W:3 * CONV_W] = (dz * c).astype(BF16)

        d_nx = dycat[:, ATTN_W + CONV_W:D_MODEL]
        yxhat, rx = _rms_hat(yx_ref[...])
        dgx_ref[...] += jnp.sum(d_nx * yxhat, axis=0, keepdims=True)
        dyx = _rms_bwd(yxhat, rx, gx_ref[...], d_nx)
        qxb, mkvb = qx_ref[...], mkv_ref[...]
        for hd in range(XATTN_W // HEAD):
            sl = slice(HEAD * hd, HEAD * (hd + 1))
            vsl = slice(XATTN_W + HEAD * hd, XATTN_W + HEAD * (hd + 1))
            s = lax.dot_general(qxb[:, sl], mkvb[:, sl], NT, preferred_element_type=F32) * SCALE
            e = jnp.exp(s - jnp.max(s, axis=1, keepdims=True))
            p = e / jnp.sum(e, axis=1, keepdims=True)
            dob = dyx[:, sl].astype(BF16)
            dp = lax.dot_general(dob, mkvb[:, vsl], NT, preferred_element_type=F32)
            ds = (p * (dp - jnp.sum(p * dp, axis=1, keepdims=True)) * SCALE).astype(BF16)
            tail_ref[:, 3 * CONV_W + HEAD * hd:3 * CONV_W + HEAD * (hd + 1)] = jnp.dot(
                ds, mkvb[:, sl], preferred_element_type=F32).astype(BF16)
            dmkv_ref[:, sl] += lax.dot_general(ds, qxb[:, sl], TN, preferred_element_type=F32)
            dmkv_ref[:, vsl] += lax.dot_general(p.astype(BF16), dob, TN, preferred_element_type=F32)

    rows = lambda width: pl.BlockSpec((tm, width), lambda i: (n_tiles - 1 - i, 0))
    before = pl.BlockSpec((8, 3 * CONV_W), lambda i: (jnp.maximum((n_tiles - 1 - i) * (tm // 8) - 1, 0), 0))
    acc = lambda r, w: pl.BlockSpec((r, w), lambda i: (0, 0))
    return pl.pallas_call(
        body, name="mixer_bwd", grid=(n_tiles,),
        in_specs=[rows(D_MODEL), rows(D_MODEL), rows(ATTN_W), rows(XATTN_W), rows(3 * CONV_W), before, rows(XATTN_W),
                  _resident((n_mem, 2 * XATTN_W)), _resident((3, CONV_W)), _resident((1, ATTN_W)),
                  _resident((1, CONV_W)), _resident((1, XATTN_W)), _resident((D_MODEL, D_MODEL)),
                  _resident((1, D_MODEL)), pl.BlockSpec(memory_space=pl.ANY)],
        out_specs=[rows(D_MODEL), rows(ATTN_W), rows(ATTN_W), rows(3 * CONV_W + XATTN_W), acc(n_mem, 2 * XATTN_W),
                   acc(3, CONV_W), acc(1, D_MODEL), acc(1, ATTN_W), acc(1, CONV_W), acc(1, XATTN_W)],
        out_shape=[jax.ShapeDtypeStruct((S, D_MODEL), BF16), jax.ShapeDtypeStruct((S, ATTN_W), F32),
                   jax.ShapeDtypeStruct((S, ATTN_W), F32), jax.ShapeDtypeStruct((S, 3 * CONV_W + XATTN_W), BF16),
                   jax.ShapeDtypeStruct((n_mem, 2 * XATTN_W), F32), jax.ShapeDtypeStruct((3, CONV_W), F32),
                   jax.ShapeDtypeStruct((1, D_MODEL), F32), jax.ShapeDtypeStruct((1, ATTN_W), F32),
                   jax.ShapeDtypeStruct((1, CONV_W), F32), jax.ShapeDtypeStruct((1, XATTN_W), F32)],
        scratch_shapes=[pltpu.VMEM((8, CONV_W), F32)],
        compiler_params=_params("arbitrary"),
    )(dx1, y2, ya, yx, bcu, bcu, qx, mkv, conv_w, g_a, g_c, g_x, w_out, g_post, after)


def _memkv_bwd(mem, g_mem, w_kv, dmkv):
    n_mem = mem.shape[0]
    half = D_MODEL // N_CHIPS // 2

    def body(mem_ref, g_ref, w_ref, d_ref, dw_ref, dg_ref):
        mhat, _ = _rms_hat(mem_ref[...])
        mn = (mhat * g_ref[...]).astype(BF16)
        d = d_ref[...].astype(BF16)
        for k in range(2 * N_CHIPS):
            dw_ref[k % 2, k // 2] = lax.dot_general(mn[:, half * k:half * (k + 1)], d, TN, preferred_element_type=F32)
        dmn = lax.dot_general(d, w_ref[...], NT, preferred_element_type=F32)
        dg_ref[...] = jnp.sum(dmn * mhat, axis=0, keepdims=True)

    return pl.pallas_call(
        body, name="memkv_bwd",
        out_shape=[jax.ShapeDtypeStruct((2, N_CHIPS, half, 2 * XATTN_W), F32), jax.ShapeDtypeStruct((1, D_MODEL), F32)],
        compiler_params=pltpu.CompilerParams(vmem_limit_bytes=VMEM_LIMIT_V7X),
    )(mem, g_mem, w_kv, dmkv)


def _in_proj_bwd(dqkv, tail, cos, sin, w_in, x, g, dx1, after, tm):
    S = x.shape[0]

    def body(dq_ref, dk_ref, dv_ref, tail_ref, cos_ref, sin_ref, w_hbm, x_ref, g_ref, dx1_ref, after_ref,
             dproj_ref, dx_ref, dg_ref, w_full, sems):
        _side_by_side(w_hbm, w_full, sems)

        @pl.when(pl.program_id(0) == 0)
        def _():
            dg_ref[...] = jnp.zeros_like(dg_ref)

        halves = [slice(0, tm // 2), slice(tm // 2, tm)]
        for rows in halves:
            c, s = cos_ref[rows, :], sin_ref[rows, :]
            for j in range(ATTN_W // 128):
                cols = slice(128 * j, 128 * (j + 1))
                dproj_ref[rows, cols] = _rope128(dq_ref[rows, cols] * SCALE, c, s, True).astype(BF16)
                dproj_ref[rows, ATTN_W + 128 * j:ATTN_W + 128 * (j + 1)] = _rope128(dk_ref[rows, cols], c, s, True).astype(BF16)
            dproj_ref[rows, 2 * ATTN_W:3 * ATTN_W] = dv_ref[rows, :].astype(BF16)
            dproj_ref[rows, 3 * ATTN_W:PROJ_W] = tail_ref[rows, :]
        dhs = [lax.dot_general(dproj_ref[rows, :], w_full[...], NT, preferred_element_type=F32) for rows in halves]
        for rows, dh in zip(halves, dhs):
            xhat, r = _rms_hat(x_ref[rows, :])
            dg_ref[...] += jnp.sum(dh * xhat, axis=0, keepdims=True)
            dx_ref[rows, :] = dx1_ref[rows, :] + _rms_bwd(xhat, r, g_ref[...], dh)

    return pl.pallas_call(
        body, name="in_proj_bwd", grid=(S // tm,),
        in_specs=[_rows(tm, ATTN_W)] * 3 + [_rows(tm, PROJ_W - 3 * ATTN_W), _rows(tm, 128), _rows(tm, 128),
                  pl.BlockSpec(memory_space=pl.ANY), _rows(tm, D_MODEL), _resident((1, D_MODEL)),
                  _rows(tm, D_MODEL), pl.BlockSpec(memory_space=pl.ANY)],
        out_specs=[_rows(tm, PROJ_W), _rows(tm, D_MODEL), pl.BlockSpec((1, D_MODEL), lambda i: (0, 0))],
        out_shape=[jax.ShapeDtypeStruct((S, PROJ_W), BF16), jax.ShapeDtypeStruct((S, D_MODEL), F32),
                   jax.ShapeDtypeStruct((1, D_MODEL), F32)],
        scratch_shapes=[pltpu.VMEM((D_MODEL, PROJ_W), BF16), pltpu.SemaphoreType.DMA((N_CHIPS,))],
        compiler_params=_params("arbitrary"),
    )(*dqkv, tail, cos, sin, w_in, x, g, dx1, after)


def _row_tile(rows):
    return ROW_TILE if rows % ROW_TILE == 0 else rows


def _chip_sums_bf16(name, grads, from_sibling, place):
    k = len(grads)
    _, n, rows, _ = grads[0].shape
    tr = _row_tile(rows)

    def body(place_ref, *refs):
        for g_ref, b_ref, o_ref in zip(refs[:k], refs[k:2 * k], refs[2 * k:]):
            o_ref[...] = (g_ref[0] + b_ref[...]).astype(BF16)

    mine = lambda g: pl.BlockSpec((1, 1, tr, g.shape[3]), lambda s, i, p: (p[0], s, i, 0))
    slab = lambda g: pl.BlockSpec((1, tr, g.shape[3]), lambda s, i, p: (s, i, 0))
    return pl.pallas_call(
        body, name=name, out_shape=[jax.ShapeDtypeStruct(g.shape[1:], BF16) for g in grads],
        grid_spec=pltpu.PrefetchScalarGridSpec(
            num_scalar_prefetch=1, grid=(n, rows // tr),
            in_specs=[mine(g) for g in grads] + [slab(g) for g in grads], out_specs=[slab(g) for g in grads]),
        compiler_params=_params("parallel", "parallel"),
    )(place, *grads, *from_sibling)


def _final_sums(name, grads, from_sibling, others, place):
    k = len(grads)
    rows = grads[0].shape[2]
    tr = _row_tile(rows)

    def body(place_ref, *refs):
        for a in range(k):
            own_ref, sib_ref = refs[a], refs[k + a]
            acc = own_ref[0, 0] + sib_ref[0]
            for o in refs[2 * k + 3 * a:2 * k + 3 * a + 3]:
                acc = acc + o[0].astype(F32)
            refs[5 * k + a][0] = acc

    own = lambda g: pl.BlockSpec((1, 1, tr, g.shape[3]), lambda i, p: (p[0], p[1], i, 0))
    sib = lambda g: pl.BlockSpec((1, tr, g.shape[3]), lambda i, p: (p[1], i, 0))
    other = lambda g, j: pl.BlockSpec((1, tr, g.shape[3]), lambda i, p: (j, i, 0))
    return pl.pallas_call(
        body, name=name, out_shape=[jax.ShapeDtypeStruct((2,) + g.shape[2:], F32) for g in grads],
        grid_spec=pltpu.PrefetchScalarGridSpec(
            num_scalar_prefetch=1, grid=(rows // tr,),
            in_specs=[own(g) for g in grads] + [sib(g) for g in grads] + [other(g, j) for g in grads for j in range(3)],
            out_specs=[pl.BlockSpec((1, tr, g.shape[3]), lambda i, p: (p[0], i, 0)) for g in grads]),
        compiler_params=_params("parallel"),
    )(place, *grads, *from_sibling, *[o for o in others for _ in range(3)])


def _adamw_update(w, g, m, v):
    m = ADAM_B1 * m + (1.0 - ADAM_B1) * g
    v = ADAM_B2 * v + (1.0 - ADAM_B2) * (g * g)
    m_hat = m * (1.0 / (1.0 - ADAM_B1 ** ADAM_STEP))
    v_hat = v * (1.0 / (1.0 - ADAM_B2 ** ADAM_STEP))
    return -ADAM_LR * (m_hat / (jnp.sqrt(v_hat) + ADAM_EPS) + ADAM_WD * w), m, v


def _adamw(name, params, after):
    k = len(params)
    rows = params[0][0].shape[0]
    tr = ADAMW_ROW_TILE if rows % ADAMW_ROW_TILE == 0 else rows

    def body(*refs):
        ins, outs = refs[:4 * k], refs[4 * k + 1:]
        for a in range(k):
            w_ref, g_ref, m_ref, v_ref = ins[4 * a:4 * a + 4]
            g = g_ref[...]
            outs[4 * a][...] = g
            outs[4 * a + 1][...], outs[4 * a + 2][...], outs[4 * a + 3][...] = _adamw_update(w_ref[...], g, m_ref[...], v_ref[...])

    spec = lambda w: pl.BlockSpec((tr, w.shape[1]), lambda i: (i, 0))
    out = pl.pallas_call(
        body, name=name, grid=(rows // tr,),
        in_specs=[spec(p[0]) for p in params for _ in range(4)] + [pl.BlockSpec(memory_space=pl.ANY)],
        out_specs=[spec(p[0]) for p in params for _ in range(4)],
        out_shape=[jax.ShapeDtypeStruct(p[0].shape, F32) for p in params for _ in range(4)],
        compiler_params=_params("parallel"),
    )(*[t for p in params for t in p], after)
    return [out[4 * a:4 * a + 4] for a in range(k)]


def _small_update(summed, chip, gains, gains_m, gains_v, taps, taps_m, taps_v):
    n = len(gains)
    widths = [g.shape[1] for g in gains]
    k, w = taps.shape

    def body(*refs):
        chip_ref, sum_ref = refs[0], refs[1]
        params = [refs[2 + 3 * i:5 + 3 * i] for i in range(n + 1)]
        outs = [refs[2 + 3 * (n + 1) + 4 * i:2 + 3 * (n + 1) + 4 * (i + 1)] for i in range(n + 1)]
        loss_ref = refs[-1]
        for i in range(n):
            g = sum_ref[i:i + 1, 0:widths[i]]
            wr, mr, vr = params[i]
            outs[i][0][...] = g
            outs[i][1][...], outs[i][2][...], outs[i][3][...] = _adamw_update(wr[...], g, mr[...], vr[...])
        g = sum_ref[n:n + k, 0:w]
        for j in range(1, N_CHIPS):
            g = jnp.where(chip_ref[0] == j, sum_ref[n:n + k, w * j:w * (j + 1)], g)
        wr, mr, vr = params[n]
        outs[n][0][...] = g
        outs[n][1][...], outs[n][2][...], outs[n][3][...] = _adamw_update(wr[...], g, mr[...], vr[...])
        loss_ref[...] = sum_ref[n + k:n + k + 1, 0:1]

    vmem = pl.BlockSpec(memory_space=pltpu.VMEM)
    operands = [chip, summed]
    for p in zip(list(gains) + [taps], list(gains_m) + [taps_m], list(gains_v) + [taps_v]):
        operands += list(p)
    shapes = [jax.ShapeDtypeStruct(p.shape, F32) for p in list(gains) + [taps] for _ in range(4)]
    out = pl.pallas_call(
        body, name="small_update", out_shape=shapes + [jax.ShapeDtypeStruct((1, 1), F32)],
        in_specs=[pl.BlockSpec(memory_space=pltpu.SMEM)] + [vmem] * (len(operands) - 1),
        out_specs=[vmem] * (len(shapes) + 1),
    )(*operands)
    return [out[4 * i:4 * (i + 1)] for i in range(n + 1)], out[-1]


def _sum_blocks(name, blocks):
    n, rows, cols = blocks.shape

    def body(b_ref, o_ref):
        acc = b_ref[0]
        for k in range(1, n):
            acc = acc + b_ref[k]
        o_ref[...] = acc

    return pl.pallas_call(body, name=name, out_shape=jax.ShapeDtypeStruct((rows, cols), F32))(blocks)


def _place():
    return lax.axis_index("x"), lax.axis_index("y"), lax.axis_index("c")


def _other_chips(x, y):
    return [(1 - x, y), (x, 1 - y), (1 - x, 1 - y)]


def _allgather_finish(name, shards, landed, pass_on):
    n = len(shards)

    def body(*refs):
        ins, outs, stage = refs[:n], refs[2 * n:3 * n], refs[3 * n:4 * n]
        send_sems, recv_sems, local_sems = refs[4 * n:]
        x, y, c = _place()
        chips = _other_chips(x, y)

        def copy(a, k, chip, half):
            place = outs[a].at[2 * chip[0] + chip[1], half]
            return pltpu.make_async_remote_copy(
                src_ref=place, dst_ref=place, send_sem=send_sems.at[3 * a + k], recv_sem=recv_sems.at[3 * a + k],
                device_id=(x, y, 1 - c), device_id_type=MESH)

        load = [pltpu.make_async_copy(ins[a], stage[a], local_sems.at[a]) for a in range(n)]
        local = [pltpu.make_async_copy(stage[a], outs[a].at[2 * x + y], local_sems.at[a]) for a in range(n)]
        for cp in load:
            cp.start()
        passed = [copy(a, k, chip, c) for a in range(n) if pass_on[a] for k, chip in enumerate(chips)]
        for cp in passed:
            cp.start()
        for a in range(n):
            load[a].wait()
            local[a].start()
        for a in range(n):
            if pass_on[a]:
                for k, chip in enumerate(chips):
                    copy(a, k, chip, 1 - c).wait_recv()
        for cp in passed:
            cp.wait_send()
        for cp in local:
            cp.wait()

    any_spec = pl.BlockSpec(memory_space=pl.ANY)
    return pl.pallas_call(
        body, name=name,
        out_shape=[jax.ShapeDtypeStruct((N_CHIPS,) + s.shape, s.dtype) for s in shards],
        in_specs=[any_spec] * (2 * n), out_specs=[any_spec] * n,
        input_output_aliases={n + a: a for a in range(n)},
        scratch_shapes=[pltpu.VMEM(s.shape, s.dtype) for s in shards]
        + [pltpu.SemaphoreType.DMA((3 * n,)), pltpu.SemaphoreType.DMA((3 * n,)), pltpu.SemaphoreType.DMA((n,))],
        compiler_params=pltpu.CompilerParams(vmem_limit_bytes=VMEM_LIMIT_V7X),
    )(*shards, *landed)


def _plan_first_hop(x, y, c, shards, lands):
    return [(shards[a].at[c], lands[a].at[2 * x + y, c], lands[a].at[2 * chip[0] + chip[1], c], (*chip, c))
            for a in range(len(shards)) for chip in _other_chips(x, y)]


def _plan_pass_on(x, y, c, nothing, lands):
    def place(a, chip, half):
        return lands[a].at[2 * chip[0] + chip[1], half]

    return [(place(a, chip, c), place(a, chip, c), place(a, chip, 1 - c), (x, y, 1 - c))
            for a in range(len(lands)) for chip in _other_chips(x, y)]


def _plan_own_half_to_sibling(x, y, c, nothing, lands):
    return [(lands[a].at[c], lands[a].at[c], lands[a].at[1 - c], (x, y, 1 - c)) for a in range(len(lands))]


def _plan_other_half_to_sibling(x, y, c, grads, lands):
    return [(grads[a].at[1 - c], lands[a], lands[a], (x, y, 1 - c)) for a in range(len(grads))]


def _plan_to_other_chips(x, y, c, partials, lands):
    return [(partials[a].at[2 * chip[0] + chip[1]], lands[a].at[k], lands[a].at[k], (*chip, c))
            for a in range(len(partials)) for k, chip in enumerate(_other_chips(x, y))]


def _plan_to_all(x, y, c, blocks, lands):
    flips = [(fx, fy, fc) for fx in (0, 1) for fy in (0, 1) for fc in (0, 1) if (fx, fy, fc) != (0, 0, 0)]
    peers = [(1 - x if fx else x, 1 - y if fy else y, 1 - c if fc else c) for fx, fy, fc in flips]
    return [(blocks[0], lands[0].at[4 * x + 2 * y + c], lands[0].at[4 * p[0] + 2 * p[1] + p[2]], p) for p in peers]


def _planned_copies(plan, srcs, lands, send_sems, recv_sems):
    x, y, c = _place()

    def pair(k, src, there, here, to):
        make = lambda dst: pltpu.make_async_remote_copy(
            src_ref=src, dst_ref=dst, send_sem=send_sems.at[k], recv_sem=recv_sems.at[k], device_id=to, device_id_type=MESH)
        return make(there), make(here)

    return [pair(k, *entry) for k, entry in enumerate(plan(x, y, c, srcs, lands))]


_HBM_SPEC = pl.BlockSpec(memory_space=pltpu.HBM)
_SEM_SPEC = pl.BlockSpec(memory_space=pltpu.SEMAPHORE)


def _hbm(a):
    return pltpu.with_memory_space_constraint(a, pltpu.HBM)


SIBLING_COLLECTIVE_ID = 1


def _exchange_start(name, plan, n_copies, srcs, land_shapes, after, lands=None, sibling_only=False):
    if lands is None:
        lands = [lax.empty(s.shape, s.dtype) for s in land_shapes]
    land_shapes = lands
    ns, nl = len(srcs), len(land_shapes)
    n_in = ns + nl + 1

    def body(*refs):
        if sibling_only:
            x, y, c = _place()
            barrier = pltpu.get_barrier_semaphore()
            pl.semaphore_signal(barrier, inc=1, device_id=(x, y, 1 - c), device_id_type=MESH)
            pl.semaphore_wait(barrier, 1)
        for send, _ in _planned_copies(plan, refs[:ns], refs[ns:ns + nl], refs[n_in], refs[n_in + 1]):
            send.start()
        refs[-1][...] = jnp.zeros_like(refs[-1])

    out = pl.pallas_call(
        body, name=name,
        out_shape=(pltpu.SemaphoreType.DMA((n_copies,)), pltpu.SemaphoreType.DMA((n_copies,)),
                   *[pltpu.HBM(s.shape, s.dtype) for s in land_shapes], jax.ShapeDtypeStruct((8, 128), F32)),
        in_specs=[_HBM_SPEC] * (ns + nl) + [pl.BlockSpec(memory_space=pl.ANY)],
        out_specs=(_SEM_SPEC, _SEM_SPEC, *[_HBM_SPEC] * nl, pl.BlockSpec(memory_space=pltpu.VMEM)),
        input_output_aliases={ns + i: 2 + i for i in range(nl)},
        compiler_params=pltpu.CompilerParams(has_side_effects=pltpu.SideEffectType.DATAFLOW_SIDE_EFFECTING,
                                             collective_id=SIBLING_COLLECTIVE_ID if sibling_only else None),
    )(*[_hbm(s) for s in srcs], *[_hbm(l) for l in lands], after)
    return out[0], out[1], list(out[2:2 + nl]), out[-1]


def _exchange_wait(name, plan, srcs, started, after):
    send_sems, recv_sems, lands, _ = started
    ns, nl = len(srcs), len(lands)
    after = list(after) if isinstance(after, (list, tuple)) else [after]

    def body(*refs):
        for send, recv in _planned_copies(plan, refs[:ns], refs[ns:ns + nl], refs[ns + nl], refs[ns + nl + 1]):
            send.wait_send()
            recv.wait_recv()

    return pl.pallas_call(
        body, name=name, out_shape=[pltpu.HBM(l.shape, l.dtype) for l in lands],
        in_specs=[_HBM_SPEC] * (ns + nl) + [_SEM_SPEC, _SEM_SPEC] + [pl.BlockSpec(memory_space=pl.ANY)] * len(after),
        out_specs=[_HBM_SPEC] * nl, input_output_aliases={ns + i: i for i in range(nl)},
        compiler_params=pltpu.CompilerParams(has_side_effects=pltpu.SideEffectType.DATAFLOW_SIDE_EFFECTING),
    )(*[_hbm(s) for s in srcs], *lands, send_sems, recv_sems, *after)


def _like(arrays, lead, dtype=None):
    return [jax.ShapeDtypeStruct(tuple(lead) + a.shape[-2:], dtype or a.dtype) for a in arrays]


class _StepExchanges:
    def __init__(self, mats, conv_w):
        x, y, c = _place()
        self.place = jnp.stack([c, 2 * x + y]).astype(jnp.int32)
        shards = [w.astype(BF16).reshape(2, w.shape[0] // 2, w.shape[1]) for w in mats]
        self._in_shard = shards[:1]
        self._in = _exchange_start("w_in_allgather_start", _plan_first_hop, 3, self._in_shard,
                                   _like(self._in_shard, (N_CHIPS, 2)), shards[0])
        self.zero = self._in[3]
        taps = jnp.pad(conv_w, ((0, 8 - conv_w.shape[0]), (0, 128 - conv_w.shape[1])))
        self._rest_shards = shards[1:] + [jnp.stack([taps, jnp.zeros_like(taps)])]
        self._taps_shape = conv_w.shape
        self._groups = {}

    def w_in(self, after):
        landed = _exchange_wait("w_in_allgather_wait", _plan_first_hop, self._in_shard, self._in,
                                list(after) + self._rest_shards)
        (w_in,) = _allgather_finish("w_in_allgather_finish", self._in_shard, landed, [True])
        self._rest = _exchange_start("rest_allgather_start", _plan_first_hop, 3 * len(self._rest_shards),
                                     self._rest_shards, _like(self._rest_shards, (N_CHIPS, 2)), w_in)
        self.zero = self._rest[3]
        return w_in.reshape(N_CHIPS, 2 * w_in.shape[2], w_in.shape[3])

    def rest_weights(self, after):
        landed = _exchange_wait("rest_allgather_wait", _plan_first_hop, self._rest_shards, self._rest, after)
        kv, out, up, down, taps = _allgather_finish("rest_allgather_finish", self._rest_shards, landed,
                                                    [True, True, False, False, True])
        self._up_down = _exchange_start("up_down_pass_on_start", _plan_pass_on, 6, [], None, self.zero, lands=[up, down],
                                        sibling_only=True)
        self.zero = self._up_down[3]
        k, w = self._taps_shape
        taps = taps[:, 0, :k, :w].transpose(1, 0, 2).reshape(k, N_CHIPS * w)
        return [g.reshape(N_CHIPS, 2 * g.shape[2], g.shape[3]) for g in (kv, out)], taps

    def up_down(self, after):
        full = _exchange_wait("up_down_pass_on_wait", _plan_pass_on, [], self._up_down, after)
        return [g.reshape(N_CHIPS, 2 * g.shape[2], g.shape[3]) for g in full]

    def send_grads(self, key, grads):
        grads = list(grads)
        started = _exchange_start(f"{key}_grads_to_sibling_start", _plan_other_half_to_sibling, len(grads), grads,
                                  _like(grads, (N_CHIPS,)), self.zero, sibling_only=True)
        self._groups[key] = dict(grads=grads, to_sibling=started)
        self.zero = started[3]

    def grads_at_sibling(self, key, after):
        group = self._groups[key]
        grads = group["grads"]
        group["from_sibling"] = _exchange_wait(f"{key}_grads_to_sibling_wait", _plan_other_half_to_sibling, grads,
                                               group["to_sibling"], after)
        group["partials"] = _chip_sums_bf16(f"{key}_chip_sums", grads, group["from_sibling"], self.place)
        group["to_chips"] = _exchange_start(f"{key}_grads_to_chips_start", _plan_to_other_chips, 3 * len(grads),
                                            group["partials"], _like(group["partials"], (3,)), self.zero)
        self.zero = group["to_chips"][3]

    def grads_summed(self, key, after):
        group = self._groups[key]
        from_chips = _exchange_wait(f"{key}_grads_to_chips_wait", _plan_to_other_chips, group["partials"],
                                    group["to_chips"], after)
        return _final_sums(f"{key}_final_sums", group["grads"], group["from_sibling"], from_chips, self.place)

    def send_sums(self, key, sums):
        self._groups[key + "_sums"] = _exchange_start(f"{key}_sums_to_sibling_start", _plan_own_half_to_sibling,
                                                      len(sums), [], None, self.zero, lands=list(sums),
                                                      sibling_only=True)
        self.zero = self._groups[key + "_sums"][3]

    def whole_sums(self, key, after):
        full = _exchange_wait(f"{key}_sums_to_sibling_wait", _plan_own_half_to_sibling, [], self._groups[key + "_sums"], after)
        return [t.reshape(2 * t.shape[1], t.shape[2]) for t in full]

    def send_small(self, block):
        self._small = block
        self._small_started = _exchange_start("small_grads_start", _plan_to_all, 7, [block],
                                              [jax.ShapeDtypeStruct((8,) + block.shape, block.dtype)], self.zero)
        self.zero = self._small_started[3]

    def small_summed(self, after):
        x, y, c = _place()
        (landed,) = _exchange_wait("small_grads_wait", _plan_to_all, [self._small], self._small_started, after)
        blocks = lax.dynamic_update_index_in_dim(landed, self._small, 4 * x + 2 * y + c, 0)
        return _sum_blocks("small_sum", blocks)


def _rope_tables(positions):
    half = HEAD // 2
    inv_freq = jnp.float32(ROPE_THETA) ** (-(jnp.arange(half, dtype=F32) * 2.0 / HEAD))
    ang = positions.astype(F32)[:, None] * inv_freq
    cos, sin = jnp.cos(ang), jnp.sin(ang)
    return jnp.tile(cos, (1, 4)), jnp.tile(jnp.concatenate([-sin, sin], axis=1), (1, 2))


def _local_step(x, mem, positions, target, gains, ex):
    g_pre_mix, g_mem, g_a, g_c, g_x, g_post_mix, g_pre_mlp, g_post_mlp = gains
    tm = ROW_TILE
    cos, sin = _rope_tables(positions)
    h = _pre_norm(x, g_pre_mix, ex.zero, tm)
    w_in = ex.w_in([h, cos, sin])

    q, k, v, bcu, qx = _in_proj_fwd(h, w_in, cos, sin, ex.zero, tm)
    ya, lse = _attn_fwd(q, k, v)
    (w_kv, w_out), conv_w = ex.rest_weights(lse)
    w_kv, w_out = (w.reshape(N_CHIPS * w.shape[1], w.shape[2]) for w in (w_kv, w_out))
    memn, mkv = _memkv_fwd(mem, g_mem, w_kv, ex.zero)
    yx, ycat, y2, x1 = _mix_fwd(ya, bcu, qx, mkv, conv_w, g_a, g_c, g_x, w_out, g_post_mix, x, tm)
    w_up, w_down = ex.up_down(x1)
    w_down = w_down.reshape(N_CHIPS * w_down.shape[1], w_down.shape[2])
    h2, f, du, df2, dx1, dg_pre_mlp, dg_post_mlp, loss = _mlp_fwd_bwd(x1, target, g_pre_mlp, g_post_mlp, w_up, w_down,
                                                                      MLP_ROW_TILE)
    gw_down = _weight_grad("grad_w_down", f, df2, True, ex.zero)
    gw_up = _weight_grad("grad_w_up", h2, du, False, ex.zero)
    ex.send_grads("early", [gw_up, gw_down])

    dy2, dya, delta, tail, dmkv, g_conv, dg_post_mix, dg_a, dg_c, dg_x = _mixer_bwd(
        dx1, y2, ya, yx, bcu, qx, mkv, conv_w, g_a, g_c, g_x, w_out, g_post_mix, ex.zero, tm)
    ex.grads_at_sibling("early", dy2)
    gw_out = _weight_grad("grad_w_out", ycat, dy2, True, ex.zero)
    gw_kv, dg_mem = _memkv_bwd(mem, g_mem, w_kv, dmkv)
    ex.send_grads("mid", [gw_out, gw_kv])
    dqkv = _attn_bwd(q, k, v, dya, lse, delta, ex.zero)
    ex.grads_at_sibling("mid", dqkv[0])
    dproj, grad_x, dg_pre_mix = _in_proj_bwd(dqkv, tail, cos, sin, w_in, x, g_pre_mix, dx1, ex.zero, tm)
    gain_grads = [dg_pre_mix, dg_mem, dg_a, dg_c, dg_x, dg_post_mix, dg_pre_mlp, dg_post_mlp]
    ex.send_small(_pack_small(gain_grads, g_conv, loss))
    gw_in = _weight_grad_w_in(h, dproj)
    ex.send_grads("late", [gw_in])
    return grad_x


def _pack_small(gains, conv, scalar=None):
    rows = [jnp.pad(g, ((0, 0), (0, D_MODEL - g.shape[1]))) for g in gains]
    rows.append(jnp.pad(conv, ((0, 0), (0, D_MODEL - conv.shape[1]))))
    last = jnp.zeros((SMALL_ROWS - 8 - conv.shape[0], D_MODEL), F32)
    rows.append(last if scalar is None else last.at[0:1, 0:1].set(scalar))
    return jnp.concatenate(rows, axis=0)


def _unpack_small(block, gain_widths, conv_width):
    gains = [block[i:i + 1, :w] for i, w in enumerate(gain_widths)]
    return gains, block[8:11, :conv_width], block[11, 0]


def kernel(x, mem, positions, g_pre_mix, g_mem, w_in, w_mem_kv, conv_w, g_attn_out, g_conv_out, g_xattn_out, w_out, g_post_mix, g_pre_mlp, w_up, w_down, g_post_mlp, loss_target, m_g_pre_mix, m_g_mem, m_w_in, m_w_mem_kv, m_conv_w, m_g_attn_out, m_g_conv_out, m_g_xattn_out, m_w_out, m_g_post_mix, m_g_pre_mlp, m_w_up, m_w_down, m_g_post_mlp, v_g_pre_mix, v_g_mem, v_w_in, v_w_mem_kv, v_conv_w, v_g_attn_out, v_g_conv_out, v_g_xattn_out, v_w_out, v_g_post_mix, v_g_pre_mlp, v_w_up, v_w_down, v_g_post_mlp):
    cx, cy, cc = _place()
    chip = 2 * cx + cy
    gains = [g_pre_mix, g_mem, g_attn_out, g_conv_out, g_xattn_out, g_post_mix, g_pre_mlp, g_post_mlp]
    gains_m = [m_g_pre_mix, m_g_mem, m_g_attn_out, m_g_conv_out, m_g_xattn_out, m_g_post_mix, m_g_pre_mlp, m_g_post_mlp]
    gains_v = [v_g_pre_mix, v_g_mem, v_g_attn_out, v_g_conv_out, v_g_xattn_out, v_g_post_mix, v_g_pre_mlp, v_g_post_mlp]
    gain_widths = [g.shape[1] for g in gains]
    mats = [w_in[0], w_mem_kv[0], w_out[0], w_up[0], w_down[0]]
    mats_m = [m_w_in[0], m_w_mem_kv[0], m_w_out[0], m_w_up[0], m_w_down[0]]
    mats_v = [v_w_in[0], v_w_mem_kv[0], v_w_out[0], v_w_up[0], v_w_down[0]]

    ex = _StepExchanges(mats, conv_w[0])
    grad_x = _local_step(x[0], mem[0], positions[0], loss_target[0], gains, ex)

    ex.send_sums("four", ex.grads_summed("early", ex.zero) + ex.grads_summed("mid", ex.zero))
    ex.grads_at_sibling("late", ex.zero)
    up_sum, down_sum, out_sum, kv_sum = ex.whole_sums("four", ex.zero)
    params = lambda a, g: (mats[a], g, mats_m[a], mats_v[a])
    new_up, new_down = _adamw("adamw_up_down", [params(3, up_sum), params(4, down_sum)], ex.zero)
    new_out, new_kv = _adamw("adamw_out_kv", [params(2, out_sum), params(1, kv_sum)], ex.zero)

    small, total = _small_update(ex.small_summed(new_kv[1]), chip.reshape(1).astype(jnp.int32), gains, gains_m,
                                 gains_v, conv_w[0], m_conv_w[0], v_conv_w[0])

    ex.send_sums("last", ex.grads_summed("late", small[0][1]))
    (in_sum,) = ex.whole_sums("last", ex.zero)
    (new_in,) = _adamw("adamw_in", [params(0, in_sum)], in_sum)
    mat_new = [new_in, new_kv, new_out, new_up, new_down]

    order = ["g_pre_mix", "g_mem", "w_in", "w_mem_kv", "conv_w", "g_attn_out", "g_conv_out", "g_xattn_out", "w_out",
             "g_post_mix", "g_pre_mlp", "w_up", "w_down", "g_post_mlp"]
    gain_names = ["g_pre_mix", "g_mem", "g_attn_out", "g_conv_out", "g_xattn_out", "g_post_mix", "g_pre_mlp", "g_post_mlp"]
    mat_names = ["w_in", "w_mem_kv", "w_out", "w_up", "w_down"]

    def leaf(kind, name):
        if name in gain_names:
            return small[gain_names.index(name)][kind]
        if name == "conv_w":
            return small[len(gain_names)][kind][None]
        return mat_new[mat_names.index(name)][kind][None]

    return (total[0, 0], grad_x[None], *[leaf(kind, name) for kind in range(4) for name in order])
```

```python
import jax
import jax.numpy as jnp
from jax import lax
from jax.experimental import pallas as pl
from jax.experimental.pallas import tpu as pltpu

F32, BF16 = jnp.float32, jnp.bfloat16

D_MODEL = 1024
ATTN_W = 512
CONV_W = 256
XATTN_W = 256
PROJ_W = 3 * ATTN_W + 3 * CONV_W + XATTN_W
D_FF = 4096
HEAD = 64
N_BACK = 128
DILATIONS = (1, 4, 16)
ROPE_THETA = 10000.0
EPS = 1e-6
NEG_INF = -1e30
SCALE = HEAD ** -0.5
N_CHIPS = 4
SHARD_IN = PROJ_W // N_CHIPS
SHARD_FF = D_FF // N_CHIPS

ADAM_LR, ADAM_B1, ADAM_B2, ADAM_EPS, ADAM_WD, ADAM_STEP = 0.001, 0.9, 0.999, 1e-08, 0.01, 10

VMEM_LIMIT_V7X = 56 * 1024 * 1024
ROW_TILE = 512
MLP_ROW_TILE = 256
ADAMW_ROW_TILE = 256
SMALL_ROWS = 16

NT = (((1,), (1,)), ((), ()))
TN = (((0,), (0,)), ((), ()))
MESH = pl.DeviceIdType.MESH


def _params(*sem):
    return pltpu.CompilerParams(dimension_semantics=sem, vmem_limit_bytes=VMEM_LIMIT_V7X)


def _resident(shape):
    return pl.BlockSpec(shape, lambda *_: (0,) * len(shape), pipeline_mode=pl.Buffered(1))


def _rows(tm, width):
    return pl.BlockSpec((tm, width), lambda i: (i, 0))


def _rms_hat(x):
    r = lax.rsqrt(jnp.mean(x * x, axis=-1, keepdims=True) + EPS)
    return x * r, r


def _rms_bwd(xhat, r, g, dy):
    gdy = dy * g
    return r * (gdy - xhat * jnp.mean(xhat * gdy, axis=-1, keepdims=True))


def _rope128(t, cos, sin_signed, inverse):
    lane = lax.broadcasted_iota(jnp.int32, t.shape, 1)
    first_half = (lane % HEAD) < (HEAD // 2)
    rot = jnp.where(first_half, pltpu.roll(t, 128 - HEAD // 2, 1), pltpu.roll(t, HEAD // 2, 1))
    return t * cos - rot * sin_signed if inverse else t * cos + rot * sin_signed


def _pre_norm(x, g, after, tm):
    S = x.shape[0]

    def body(x_ref, g_ref, after_ref, h_ref):
        h_ref[...] = (_rms_hat(x_ref[...])[0] * g_ref[...]).astype(BF16)

    return pl.pallas_call(
        body, name="pre_norm", grid=(S // tm,),
        in_specs=[_rows(tm, D_MODEL), _resident((1, D_MODEL)), pl.BlockSpec(memory_space=pl.ANY)],
        out_specs=_rows(tm, D_MODEL), out_shape=jax.ShapeDtypeStruct((S, D_MODEL), BF16),
        compiler_params=_params("parallel"),
    )(x, g, after)


def _side_by_side(w_hbm, w_full, sems):
    @pl.when(pl.program_id(0) == 0)
    def _():
        copies = [pltpu.make_async_copy(w_hbm.at[j], w_full.at[:, pl.ds(SHARD_IN * j, SHARD_IN)], sems.at[j])
                  for j in range(N_CHIPS)]
        for cp in copies:
            cp.start()
        for cp in copies:
            cp.wait()


def _in_proj_fwd(h, w_in, cos, sin, after, tm):
    S = h.shape[0]

    def body(h_ref, w_hbm, cos_ref, sin_ref, after_ref, q_ref, k_ref, v_ref, bcu_ref, qx_ref, proj, w_full, sems):
        _side_by_side(w_hbm, w_full, sems)
        proj[...] = jnp.dot(h_ref[...], w_full[...], preferred_element_type=F32)
        c, s = cos_ref[...], sin_ref[...]
        for j in range(ATTN_W // 128):
            lo = 128 * j
            q_ref[:, lo:lo + 128] = _rope128(proj[:, lo:lo + 128], c, s, False) * SCALE
            k_ref[:, lo:lo + 128] = _rope128(proj[:, ATTN_W + lo:ATTN_W + lo + 128], c, s, False)
        v_ref[...] = proj[:, 2 * ATTN_W:3 * ATTN_W]
        bcu_ref[...] = proj[:, 3 * ATTN_W:3 * ATTN_W + 3 * CONV_W]
        qx_ref[...] = proj[:, 3 * ATTN_W + 3 * CONV_W:PROJ_W].astype(BF16)

    return pl.pallas_call(
        body, name="in_proj_fwd", grid=(S // tm,),
        in_specs=[_rows(tm, D_MODEL), pl.BlockSpec(memory_space=pl.ANY), _rows(tm, 128), _rows(tm, 128),
                  pl.BlockSpec(memory_space=pl.ANY)],
        out_specs=[_rows(tm, ATTN_W), _rows(tm, ATTN_W), _rows(tm, ATTN_W), _rows(tm, 3 * CONV_W), _rows(tm, XATTN_W)],
        out_shape=[jax.ShapeDtypeStruct((S, ATTN_W), F32), jax.ShapeDtypeStruct((S, ATTN_W), F32),
                   jax.ShapeDtypeStruct((S, ATTN_W), F32), jax.ShapeDtypeStruct((S, 3 * CONV_W), F32),
                   jax.ShapeDtypeStruct((S, XATTN_W), BF16)],
        scratch_shapes=[pltpu.VMEM((tm, PROJ_W), F32), pltpu.VMEM((D_MODEL, PROJ_W), BF16),
                        pltpu.SemaphoreType.DMA((N_CHIPS,))],
        compiler_params=_params("arbitrary"),
    )(h, w_in, cos, sin, after)


def _memkv_fwd(mem, g_mem, w_kv, after):
    n_mem = mem.shape[0]

    def body(mem_ref, g_ref, w_ref, after_ref, mn_ref, kv_ref):
        mhat, _ = _rms_hat(mem_ref[...])
        mn = (mhat * g_ref[...]).astype(BF16)
        mn_ref[...] = mn
        kv_ref[...] = jnp.dot(mn, w_ref[...], preferred_element_type=F32).astype(BF16)

    vmem = pl.BlockSpec(memory_space=pltpu.VMEM)
    return pl.pallas_call(
        body, name="memkv_fwd", in_specs=[vmem, vmem, vmem, pl.BlockSpec(memory_space=pl.ANY)], out_specs=[vmem, vmem],
        out_shape=[jax.ShapeDtypeStruct((n_mem, D_MODEL), BF16), jax.ShapeDtypeStruct((n_mem, 2 * XATTN_W), BF16)],
        compiler_params=pltpu.CompilerParams(vmem_limit_bytes=VMEM_LIMIT_V7X),
    )(mem, g_mem, w_kv, after)


def _fill_band_bias(bias):
    row = lax.broadcasted_iota(jnp.int32, (N_BACK, 2 * N_BACK), 0)
    col = lax.broadcasted_iota(jnp.int32, (N_BACK, 2 * N_BACK), 1)
    band = (col >= row) & (col <= row + N_BACK)
    bias[1] = jnp.where(band, 0.0, NEG_INF)
    bias[0] = jnp.where(band & (col >= N_BACK), 0.0, NEG_INF)


def _strided(start, size, d):
    return pl.ds(start, size) if d == 1 else pl.ds(start, size, stride=d)


def _group_starts(g, G, nb, d):
    t0 = g * G
    r, n0 = lax.shift_right_logical(t0, nb.bit_length() - 1), lax.bitwise_and(t0, nb - 1)
    first = r + n0 * (N_BACK * d)
    before = r + jnp.maximum(n0 - 1, 0) * (N_BACK * d)
    starts = [before] + [first + u * (N_BACK * d) for u in range(G)]
    if d == 1:
        starts = [pl.multiple_of(st, N_BACK) for st in starts]
    return starts, n0


def _step_blocks(i, U, nb, d):
    G = min(U, nb)
    whole = G == nb
    row_blocks, blocks = [], []
    for grp in range(U // G):
        starts, n0 = _group_starts(i * (U // G) + grp, G, nb, d)
        base = len(row_blocks)
        if whole:
            row_blocks += [_strided(st, N_BACK, d) for st in starts[1:]]
            blocks += [(base + max(u - 1, 0), base + u, min(u, 1)) for u in range(G)]
        else:
            row_blocks += [_strided(st, N_BACK, d) for st in starts]
            blocks += [(base + u, base + u + 1, jnp.minimum(n0, 1) if u == 0 else 1) for u in range(G)]
    return row_blocks, blocks


def _by_head(a, b):
    lane = lax.broadcasted_iota(jnp.int32, (a.shape[0], 2 * HEAD), 1)
    return jnp.where(lane < HEAD, a, b)


def _head_only(t, hh):
    lane = lax.broadcasted_iota(jnp.int32, t.shape, 1)
    return jnp.where((lane < HEAD) == (hh == 0), t, jnp.zeros_like(t))


def _stack_heads(t):
    return jnp.concatenate([_head_only(t, 0), _head_only(t, 1)], axis=0)


def _head_columns(t):
    return jnp.concatenate([t[:, 0:1], t[:, HEAD:HEAD + 1]], axis=0)


def _unstack(t):
    return _by_head(t[:N_BACK], t[N_BACK:])


def _unstack_columns(t):
    return _by_head(jnp.broadcast_to(t[:N_BACK], (N_BACK, 2 * HEAD)), jnp.broadcast_to(t[N_BACK:], (N_BACK, 2 * HEAD)))


FWD_BLOCKS_PER_STEP = 4
BWD_BLOCKS_PER_STEP = 4
BWD_CHUNK = 64


def _attn_fwd(q, k, v):
    S = q.shape[0]
    U = FWD_BLOCKS_PER_STEP

    def body(q_ref, k_ref, v_ref, y_ref, m_ref, l_scr, bias):
        _fill_band_bias(bias)
        for g, d in enumerate(DILATIONS):
            nb = S // d // N_BACK
            first_pattern, last_pattern = g == 0, g == len(DILATIONS) - 1

            def step(i, carry, d=d, nb=nb, first_pattern=first_pattern, last_pattern=last_pattern):
                row_blocks, blocks = _step_blocks(i, U, nb, d)
                kb = [k_ref[r, :].astype(BF16) for r in row_blocks]
                ss = []
                for before, own, which in blocks:
                    kw = jnp.concatenate([kb[before], kb[own]], 0)
                    qs = _stack_heads(q_ref[row_blocks[own], :].astype(BF16))
                    b = bias[which]
                    ss.append(lax.dot_general(qs, kw, NT, preferred_element_type=F32) + jnp.concatenate([b, b], axis=0))
                ms = [jnp.max(s, axis=1, keepdims=True) for s in ss]
                ps = [jnp.exp(s - m) for s, m in zip(ss, ms)]
                ls = [jnp.sum(p, axis=1, keepdims=True) for p in ps]
                vb = [v_ref[r, :].astype(BF16) for r in row_blocks]
                os_ = [jnp.dot(ps[u].astype(BF16), jnp.concatenate([vb[before], vb[own]], 0), preferred_element_type=F32)
                       for u, (before, own, _) in enumerate(blocks)]
                for u, (_, own, _) in enumerate(blocks):
                    o_g, m_g, l_g = _unstack(os_[u]), _unstack_columns(ms[u]), _unstack_columns(ls[u])
                    r = row_blocks[own]
                    if first_pattern:
                        m_new, l_new, acc = m_g, l_g, o_g
                    else:
                        m_old = m_ref[r, :]
                        m_new = jnp.maximum(m_old, m_g)
                        alpha, beta = jnp.exp(m_old - m_new), jnp.exp(m_g - m_new)
                        l_new = l_scr[r, :] * alpha + l_g * beta
                        acc = y_ref[r, :] * alpha + o_g * beta
                    if last_pattern:
                        y_ref[r, :] = acc / l_new
                        m_ref[r, :] = m_new + jnp.log(l_new)
                    else:
                        y_ref[r, :] = acc
                        m_ref[r, :] = m_new
                        l_scr[r, :] = l_new
                return carry

            lax.fori_loop(0, d * nb // U, step, 0)

    col = pl.BlockSpec((S, 2 * HEAD), lambda j: (0, j))
    return pl.pallas_call(
        body, name="attn_fwd", grid=(q.shape[1] // (2 * HEAD),),
        in_specs=[col, col, col], out_specs=[col, col],
        out_shape=[jax.ShapeDtypeStruct(q.shape, F32)] * 2,
        scratch_shapes=[pltpu.VMEM((S, 2 * HEAD), F32), pltpu.VMEM((2, N_BACK, 2 * N_BACK), F32)],
        compiler_params=_params("parallel"),
    )(q, k, v)


def _attn_bwd(q, k, v, dy, lse, delta, after):
    S = q.shape[0]
    U = BWD_BLOCKS_PER_STEP

    def body(q_ref, k_ref, v_ref, dy_ref, lse_ref, delta_ref, after_ref, dq_ref, dk_ref, dv_ref, bias):
        _fill_band_bias(bias)
        dk_ref[...] = jnp.zeros_like(dk_ref)
        dv_ref[...] = jnp.zeros_like(dv_ref)
        for g, d in enumerate(DILATIONS):
            nb = S // d // N_BACK

            def step(i, carry, d=d, nb=nb, g=g):
                row_blocks, blocks = _step_blocks(i, U, nb, d)
                kb = [k_ref[r, :].astype(BF16) for r in row_blocks]
                vb = [v_ref[r, :].astype(BF16) for r in row_blocks]
                kws = [jnp.concatenate([kb[before], kb[own]], 0) for before, own, _ in blocks]
                vws = [jnp.concatenate([vb[before], vb[own]], 0) for before, own, _ in blocks]
                qss = [_stack_heads(q_ref[row_blocks[own], :].astype(BF16)) for _, own, _ in blocks]
                doss = [_stack_heads(dy_ref[row_blocks[own], :].astype(BF16)) for _, own, _ in blocks]
                ss = [lax.dot_general(qss[u], kws[u], NT, preferred_element_type=F32) for u in range(U)]
                dps = [lax.dot_general(doss[u], vws[u], NT, preferred_element_type=F32) for u in range(U)]
                pbs, dss = [], []
                for u, (_, own, which) in enumerate(blocks):
                    lse_c = _head_columns(lse_ref[row_blocks[own], :])
                    delta_c = _head_columns(delta_ref[row_blocks[own], :])
                    p_parts, ds_parts = [], []
                    for r0 in range(0, 2 * N_BACK, BWD_CHUNK):
                        r = slice(r0, r0 + BWD_CHUNK)
                        mask = bias[which, r0 % N_BACK:r0 % N_BACK + BWD_CHUNK, :]
                        p_r = jnp.exp(ss[u][r] + mask - lse_c[r])
                        p_parts.append(p_r.astype(BF16))
                        ds_parts.append((p_r * (dps[u][r] - delta_c[r])).astype(BF16))
                    pbs.append(jnp.concatenate(p_parts, axis=0))
                    dss.append(jnp.concatenate(ds_parts, axis=0))
                dqs = [jnp.dot(dss[u], kws[u], preferred_element_type=F32) for u in range(U)]
                dkws = [lax.dot_general(dss[u], qss[u], TN, preferred_element_type=F32) for u in range(U)]
                dvws = [lax.dot_general(pbs[u], doss[u], TN, preferred_element_type=F32) for u in range(U)]
                dk_parts, dv_parts = [None] * len(row_blocks), [None] * len(row_blocks)
                for u, (before, own, _) in enumerate(blocks):
                    dq = _unstack(dqs[u])
                    if g == 0:
                        dq_ref[row_blocks[own], :] = dq
                    else:
                        dq_ref[row_blocks[own], :] += dq
                    for idx, dkp, dvp in ((before, dkws[u][:N_BACK], dvws[u][:N_BACK]),
                                          (own, dkws[u][N_BACK:], dvws[u][N_BACK:])):
                        dk_parts[idx] = dkp if dk_parts[idx] is None else dk_parts[idx] + dkp
                        dv_parts[idx] = dvp if dv_parts[idx] is None else dv_parts[idx] + dvp
                for idx, r in enumerate(row_blocks):
                    dk_ref[r, :] += dk_parts[idx]
                    dv_ref[r, :] += dv_parts[idx]
                return carry

            lax.fori_loop(0, d * nb // U, step, 0)

    col = pl.BlockSpec((S, 2 * HEAD), lambda j: (0, j))
    return pl.pallas_call(
        body, name="attn_bwd", grid=(q.shape[1] // (2 * HEAD),),
        in_specs=[col] * 6 + [pl.BlockSpec(memory_space=pl.ANY)], out_specs=[col] * 3,
        out_shape=[jax.ShapeDtypeStruct(q.shape, F32)] * 3,
        scratch_shapes=[pltpu.VMEM((2, N_BACK, 2 * N_BACK), F32)],
        compiler_params=_params("parallel"),
    )(q, k, v, dy, lse, delta, after)


def _shift_down(z, before, k):
    row = lax.broadcasted_iota(jnp.int32, z.shape, 0)
    out = pltpu.roll(z, k, 0)
    for i in range(k):
        out = jnp.where(row == i, before[8 - k + i:8 - k + i + 1, :], out)
    return out


def _shift_up(z, after, k):
    rows = z.shape[0]
    row = lax.broadcasted_iota(jnp.int32, z.shape, 0)
    out = pltpu.roll(z, rows - k, 0)
    for i in range(k):
        out = jnp.where(row == rows - k + i, after[i:i + 1, :], out)
    return out


def _conv_fwd(bcu, before, is_first, w):
    b, c, u = bcu[:, 0:CONV_W], bcu[:, CONV_W:2 * CONV_W], bcu[:, 2 * CONV_W:3 * CONV_W]
    z = c * u
    zb = jnp.where(is_first, 0.0, before[:, CONV_W:2 * CONV_W] * before[:, 2 * CONV_W:3 * CONV_W])
    z1, z2 = _shift_down(z, zb, 1), _shift_down(z, zb, 2)
    cv = w[0:1, :] * z2 + w[1:2, :] * z1 + w[2:3, :] * z
    return b, c, u, z, z1, z2, cv


def _halo_before(tm, width):
    return pl.BlockSpec((8, width), lambda i: (jnp.maximum(i * (tm // 8) - 1, 0), 0))


def _mix_fwd(ya, bcu, qx, mkv, conv_w, g_a, g_c, g_x, w_out, g_post, x, tm):
    S = x.shape[0]

    def body(ya_ref, bcu_ref, before_ref, qx_ref, mkv_ref, cw_ref, ga_ref, gc_ref, gx_ref,
             wo_ref, gp_ref, x_ref, yx_ref, ycat_ref, y2_ref, x1_ref):
        ya = ya_ref[...]
        b, _, _, _, _, _, cv = _conv_fwd(bcu_ref[...], before_ref[...], pl.program_id(0) == 0, cw_ref[...])
        yc = b * cv

        qxb, mkvb = qx_ref[...], mkv_ref[...]
        for hd in range(XATTN_W // HEAD):
            sl = slice(HEAD * hd, HEAD * (hd + 1))
            s = lax.dot_general(qxb[:, sl], mkvb[:, sl], NT, preferred_element_type=F32) * SCALE
            mx = jnp.max(s, axis=1, keepdims=True)
            p = jnp.exp(s - mx)
            l = jnp.sum(p, axis=1, keepdims=True)
            vm = mkvb[:, XATTN_W + HEAD * hd:XATTN_W + HEAD * (hd + 1)]
            yx_ref[:, sl] = jnp.dot(p.astype(BF16), vm, preferred_element_type=F32) / l
        yx = yx_ref[...]

        ycat_ref[:, 0:ATTN_W] = (_rms_hat(ya)[0] * ga_ref[...]).astype(BF16)
        ycat_ref[:, ATTN_W:ATTN_W + CONV_W] = (_rms_hat(yc)[0] * gc_ref[...]).astype(BF16)
        ycat_ref[:, ATTN_W + CONV_W:D_MODEL] = (_rms_hat(yx)[0] * gx_ref[...]).astype(BF16)
        y2 = jnp.dot(ycat_ref[...], wo_ref[...], preferred_element_type=F32)
        y2_ref[...] = y2
        x1_ref[...] = x_ref[...] + _rms_hat(y2)[0] * gp_ref[...]

    n_mem = mkv.shape[0]
    return pl.pallas_call(
        body, name="mix_fwd", grid=(S // tm,),
        in_specs=[_rows(tm, ATTN_W), _rows(tm, 3 * CONV_W), _halo_before(tm, 3 * CONV_W), _rows(tm, XATTN_W),
                  _resident((n_mem, 2 * XATTN_W)), _resident((3, CONV_W)), _resident((1, ATTN_W)),
                  _resident((1, CONV_W)), _resident((1, XATTN_W)), _resident((D_MODEL, D_MODEL)),
                  _resident((1, D_MODEL)), _rows(tm, D_MODEL)],
        out_specs=[_rows(tm, XATTN_W), _rows(tm, D_MODEL), _rows(tm, D_MODEL), _rows(tm, D_MODEL)],
        out_shape=[jax.ShapeDtypeStruct((S, XATTN_W), F32), jax.ShapeDtypeStruct((S, D_MODEL), BF16),
                   jax.ShapeDtypeStruct((S, D_MODEL), F32), jax.ShapeDtypeStruct((S, D_MODEL), F32)],
        compiler_params=_params("parallel"),
    )(ya, bcu, bcu, qx, mkv, conv_w, g_a, g_c, g_x, w_out, g_post, x)


def _mlp_fwd_bwd(x1, target, g_pre, g_post, w_up, w_down, tm):
    S = x1.shape[0]
    n_ff = D_FF // SHARD_FF

    def body(x1_ref, t_ref, gpre_ref, gpost_ref, wup_ref, wdn_ref,
             h2_ref, f_ref, du_ref, df2_ref, dx1_ref, dgpre_ref, dgpost_ref, loss_ref, u_scr):
        @pl.when(pl.program_id(0) == 0)
        def _():
            dgpre_ref[...] = jnp.zeros_like(dgpre_ref)
            dgpost_ref[...] = jnp.zeros_like(dgpost_ref)
            loss_ref[...] = jnp.zeros_like(loss_ref)

        x1 = x1_ref[...]
        x1hat, r1 = _rms_hat(x1)
        h2 = (x1hat * gpre_ref[...]).astype(BF16)
        h2_ref[...] = h2
        f2 = jnp.zeros((tm, D_MODEL), F32)
        for j in range(n_ff):
            cols = slice(SHARD_FF * j, SHARD_FF * (j + 1))
            u = jnp.maximum(jnp.dot(h2, wup_ref[j], preferred_element_type=F32), 0.0)
            u_scr[:, cols] = u
            f = (u * u).astype(BF16)
            f_ref[:, cols] = f
            f2 = f2 + jnp.dot(f, wdn_ref[cols, :], preferred_element_type=F32)
        f2hat, r2 = _rms_hat(f2)
        err = x1 + f2hat * gpost_ref[...] - t_ref[...]
        loss_ref[...] += 0.5 * jnp.sum(jnp.mean(err * err, axis=-1, keepdims=True), axis=0, keepdims=True)
        dx2 = err * (1.0 / D_MODEL)
        dgpost_ref[...] += jnp.sum(dx2 * f2hat, axis=0, keepdims=True)
        df2 = _rms_bwd(f2hat, r2, gpost_ref[...], dx2).astype(BF16)
        df2_ref[...] = df2
        dh2 = jnp.zeros((tm, D_MODEL), F32)
        for j in range(n_ff):
            cols = slice(SHARD_FF * j, SHARD_FF * (j + 1))
            df = lax.dot_general(df2, wdn_ref[cols, :], NT, preferred_element_type=F32)
            du = (2.0 * u_scr[:, cols] * df).astype(BF16)
            du_ref[:, cols] = du
            dh2 = dh2 + lax.dot_general(du, wup_ref[j], NT, preferred_element_type=F32)
        dgpre_ref[...] += jnp.sum(dh2 * x1hat, axis=0, keepdims=True)
        dx1_ref[...] = dx2 + _rms_bwd(x1hat, r1, gpre_ref[...], dh2)

    acc = pl.BlockSpec((1, D_MODEL), lambda i: (0, 0))
    return pl.pallas_call(
        body, name="mlp_fwd_bwd", grid=(S // tm,),
        in_specs=[_rows(tm, D_MODEL), _rows(tm, D_MODEL), _resident((1, D_MODEL)), _resident((1, D_MODEL)),
                  _resident((n_ff, D_MODEL, SHARD_FF)), _resident((D_FF, D_MODEL))],
        out_specs=[_rows(tm, D_MODEL), _rows(tm, D_FF), _rows(tm, D_FF), _rows(tm, D_MODEL), _rows(tm, D_MODEL),
                   acc, acc, pl.BlockSpec((1, 1), lambda i: (0, 0))],
        out_shape=[jax.ShapeDtypeStruct((S, D_MODEL), BF16), jax.ShapeDtypeStruct((S, D_FF), BF16),
                   jax.ShapeDtypeStruct((S, D_FF), BF16), jax.ShapeDtypeStruct((S, D_MODEL), BF16),
                   jax.ShapeDtypeStruct((S, D_MODEL), F32), jax.ShapeDtypeStruct((1, D_MODEL), F32),
                   jax.ShapeDtypeStruct((1, D_MODEL), F32), jax.ShapeDtypeStruct((1, 1), F32)],
        scratch_shapes=[pltpu.VMEM((tm, D_FF), F32)],
        compiler_params=_params("arbitrary"),
    )(x1, target, g_pre, g_post, w_up, w_down)


def _weight_grad(name, a, b, rows_sharded, after):
    S, K = a.shape
    N = b.shape[1]
    if rows_sharded:
        tk, tn = K // N_CHIPS, N
        a_spec = pl.BlockSpec((S, tk), lambda j: (0, j))
        b_spec = pl.BlockSpec((S, tn), lambda j: (0, 0), pipeline_mode=pl.Buffered(1))
    else:
        tk, tn = K, N // N_CHIPS
        a_spec = pl.BlockSpec((S, tk), lambda j: (0, 0), pipeline_mode=pl.Buffered(1))
        b_spec = pl.BlockSpec((S, tn), lambda j: (0, j))
    half = tk // 2

    def body(a_ref, b_ref, after_ref, o_ref):
        res = lax.dot_general(a_ref[...], b_ref[...], TN, preferred_element_type=F32)
        o_ref[0, 0] = res[:half]
        o_ref[1, 0] = res[half:]

    return pl.pallas_call(
        body, name=name, grid=(N_CHIPS,), in_specs=[a_spec, b_spec, pl.BlockSpec(memory_space=pl.ANY)],
        out_specs=pl.BlockSpec((2, 1, half, tn), lambda j: (0, j, 0, 0)),
        out_shape=jax.ShapeDtypeStruct((2, N_CHIPS, half, tn), F32),
        compiler_params=_params("parallel"),
    )(a, b, after)


def _weight_grad_w_in(h, dproj):
    S, K = h.shape
    step_w = 2 * 256
    n_steps = PROJ_W // step_w
    half = K // 2

    def body(a_ref, b_ref, o_ref):
        res = lax.dot_general(a_ref[...], b_ref[...], TN, preferred_element_type=F32)
        for step in range(n_steps):
            @pl.when(pl.program_id(0) == step)
            def _(step=step):
                lo = step * step_w
                while lo < (step + 1) * step_w:
                    chip = lo // SHARD_IN
                    hi = min((step + 1) * step_w, (chip + 1) * SHARD_IN)
                    for hh in range(2):
                        o_ref[hh, chip, :, lo - chip * SHARD_IN:hi - chip * SHARD_IN] = (
                            res[half * hh:half * (hh + 1), lo - step * step_w:hi - step * step_w])
                    lo = hi

    return pl.pallas_call(
        body, name="grad_w_in", grid=(n_steps,),
        in_specs=[pl.BlockSpec((S, K), lambda j: (0, 0), pipeline_mode=pl.Buffered(1)),
                  pl.BlockSpec((S, step_w), lambda j: (0, j))],
        out_specs=pl.BlockSpec((2, N_CHIPS, half, SHARD_IN), lambda j: (0, 0, 0, 0)),
        out_shape=jax.ShapeDtypeStruct((2, N_CHIPS, half, SHARD_IN), F32),
        compiler_params=_params("arbitrary"),
    )(h, dproj)


def _mixer_bwd(dx1, y2, ya, yx, bcu, qx, mkv, conv_w, g_a, g_c, g_x, w_out, g_post, after, tm):
    S = dx1.shape[0]
    n_mem = mkv.shape[0]
    n_tiles = S // tm

    def body(dx1_ref, y2_ref, ya_ref, yx_ref, bcu_ref, before_ref, qx_ref, mkv_ref, cw_ref, ga_ref, gc_ref, gx_ref,
             wo_ref, gp_ref, after_ref, dy2_ref, dya_ref, delta_ref, tail_ref, dmkv_ref, dcw_ref, dgp_ref, dga_ref,
             dgc_ref, dgx_ref, carry):
        step = pl.program_id(0)
        first_tile = step == n_tiles - 1

        @pl.when(step == 0)
        def _():
            for ref in (dmkv_ref, dcw_ref, dgp_ref, dga_ref, dgc_ref, dgx_ref, carry):
                ref[...] = jnp.zeros_like(ref)

        dx1 = dx1_ref[...]
        y2hat, r2 = _rms_hat(y2_ref[...])
        dgp_ref[...] += jnp.sum(dx1 * y2hat, axis=0, keepdims=True)
        dy2 = _rms_bwd(y2hat, r2, gp_ref[...], dx1).astype(BF16)
        dy2_ref[...] = dy2
        dycat = lax.dot_general(dy2, wo_ref[...], NT, preferred_element_type=F32)

        d_na = dycat[:, 0:ATTN_W]
        ya = ya_ref[...]
        yahat, ra = _rms_hat(ya)
        dga_ref[...] += jnp.sum(d_na * yahat, axis=0, keepdims=True)
        dya = _rms_bwd(yahat, ra, ga_ref[...], d_na)
        dya_ref[...] = dya
        prod = dya * ya
        hi = prod.astype(BF16)
        lo = (prod - hi.astype(F32)).astype(BF16)
        head_of = lambda axis: lax.shift_right_logical(lax.broadcasted_iota(jnp.int32, (ATTN_W, ATTN_W), axis),
                                                       HEAD.bit_length() - 1)
        ones = jnp.where(head_of(0) == head_of(1), 1.0, 0.0).astype(BF16)
        delta_ref[...] = jnp.dot(hi, ones, preferred_element_type=F32) + jnp.dot(lo, ones, preferred_element_type=F32)

        w = cw_ref[...]
        b, c, u, z, z1, z2, cv = _conv_fwd(bcu_ref[...], before_ref[...], first_tile, w)
        d_nc = dycat[:, ATTN_W:ATTN_W + CONV_W]
        ychat, rc = _rms_hat(b * cv)
        dgc_ref[...] += jnp.sum(d_nc * ychat, axis=0, keepdims=True)
        dyc = _rms_bwd(ychat, rc, gc_ref[...], d_nc)
        dcv = dyc * b
        behind = carry[...]
        dz = w[2:3, :] * dcv + w[1:2, :] * _shift_up(dcv, behind, 1) + w[0:1, :] * _shift_up(dcv, behind, 2)
        carry[...] = dcv[0:8, :]
        dcw_ref[0:1, :] += jnp.sum(dcv * z2, axis=0, keepdims=True)
        dcw_ref[1:2, :] += jnp.sum(dcv * z1, axis=0, keepdims=True)
        dcw_ref[2:3, :] += jnp.sum(dcv * z, axis=0, keepdims=True)
        tail_ref[:, 0:CONV_W] = (dyc * cv).astype(BF16)
        tail_ref[:, CONV_W:2 * CONV_W] = (dz * u).astype(BF16)
        tail_ref[:, 2 * CONV_W:3 * CONV_W] = (dz * c).astype(BF16)

        d_nx = dycat[:, ATTN_W + CONV_W:D_MODEL]
        yxhat, rx = _rms_hat(yx_ref[...])
        dgx_ref[...] += jnp.sum(d_nx * yxhat, axis=0, keepdims=True)
        dyx = _rms_bwd(yxhat, rx, gx_ref[...], d_nx)
        qxb, mkvb = qx_ref[...], mkv_ref[...]
        for hd in range(XATTN_W // HEAD):
            sl = slice(HEAD * hd, HEAD * (hd + 1))
            vsl = slice(XATTN_W + HEAD * hd, XATTN_W + HEAD * (hd + 1))
            s = lax.dot_general(qxb[:, sl], mkvb[:, sl], NT, preferred_element_type=F32) * SCALE
            e = jnp.exp(s - jnp.max(s, axis=1, keepdims=True))
            p = e / jnp.sum(e, axis=1, keepdims=True)
            dob = dyx[:, sl].astype(BF16)
            dp = lax.dot_general(dob, mkvb[:, vsl], NT, preferred_element_type=F32)
            ds = (p * (dp - jnp.sum(p * dp, axis=1, keepdims=True)) * SCALE).astype(BF16)
            tail_ref[:, 3 * CONV_W + HEAD * hd:3 * CONV_W + HEAD * (hd + 1)] = jnp.dot(
                ds, mkvb[:, sl], preferred_element_type=F32).astype(BF16)
            dmkv_ref[:, sl] += lax.dot_general(ds, qxb[:, sl], TN, preferred_element_type=F32)
            dmkv_ref[:, vsl] += lax.dot_general(p.astype(BF16), dob, TN, preferred_element_type=F32)

    rows = lambda width: pl.BlockSpec((tm, width), lambda i: (n_tiles - 1 - i, 0))
    before = pl.BlockSpec((8, 3 * CONV_W), lambda i: (jnp.maximum((n_tiles - 1 - i) * (tm // 8) - 1, 0), 0))
    acc = lambda r, w: pl.BlockSpec((r, w), lambda i: (0, 0))
    return pl.pallas_call(
        body, name="mixer_bwd", grid=(n_tiles,),
        in_specs=[rows(D_MODEL), rows(D_MODEL), rows(ATTN_W), rows(XATTN_W), rows(3 * CONV_W), before, rows(XATTN_W),
                  _resident((n_mem, 2 * XATTN_W)), _resident((3, CONV_W)), _resident((1, ATTN_W)),
                  _resident((1, CONV_W)), _resident((1, XATTN_W)), _resident((D_MODEL, D_MODEL)),
                  _resident((1, D_MODEL)), pl.BlockSpec(memory_space=pl.ANY)],
        out_specs=[rows(D_MODEL), rows(ATTN_W), rows(ATTN_W), rows(3 * CONV_W + XATTN_W), acc(n_mem, 2 * XATTN_W),
                   acc(3, CONV_W), acc(1, D_MODEL), acc(1, ATTN_W), acc(1, CONV_W), acc(1, XATTN_W)],
        out_shape=[jax.ShapeDtypeStruct((S, D_MODEL), BF16), jax.ShapeDtypeStruct((S, ATTN_W), F32),
                   jax.ShapeDtypeStruct((S, ATTN_W), F32), jax.ShapeDtypeStruct((S, 3 * CONV_W + XATTN_W), BF16),
                   jax.ShapeDtypeStruct((n_mem, 2 * XATTN_W), F32), jax.ShapeDtypeStruct((3, CONV_W), F32),
                   jax.ShapeDtypeStruct((1, D_MODEL), F32), jax.ShapeDtypeStruct((1, ATTN_W), F32),
                   jax.ShapeDtypeStruct((1, CONV_W), F32), jax.ShapeDtypeStruct((1, XATTN_W), F32)],
        scratch_shapes=[pltpu.VMEM((8, CONV_W), F32)],
        compiler_params=_params("arbitrary"),
    )(dx1, y2, ya, yx, bcu, bcu, qx, mkv, conv_w, g_a, g_c, g_x, w_out, g_post, after)


def _memkv_bwd(mem, g_mem, w_kv, dmkv):
    n_mem = mem.shape[0]
    half = D_MODEL // N_CHIPS // 2

    def body(mem_ref, g_ref, w_ref, d_ref, dw_ref, dg_ref):
        mhat, _ = _rms_hat(mem_ref[...])
        mn = (mhat * g_ref[...]).astype(BF16)
        d = d_ref[...].astype(BF16)
        for k in range(2 * N_CHIPS):
            dw_ref[k % 2, k // 2] = lax.dot_general(mn[:, half * k:half * (k + 1)], d, TN, preferred_element_type=F32)
        dmn = lax.dot_general(d, w_ref[...], NT, preferred_element_type=F32)
        dg_ref[...] = jnp.sum(dmn * mhat, axis=0, keepdims=True)

    return pl.pallas_call(
        body, name="memkv_bwd",
        out_shape=[jax.ShapeDtypeStruct((2, N_CHIPS, half, 2 * XATTN_W), F32), jax.ShapeDtypeStruct((1, D_MODEL), F32)],
        compiler_params=pltpu.CompilerParams(vmem_limit_bytes=VMEM_LIMIT_V7X),
    )(mem, g_mem, w_kv, dmkv)


def _in_proj_bwd(dqkv, tail, cos, sin, w_in, x, g, dx1, after, tm):
    S = x.shape[0]

    def body(dq_ref, dk_ref, dv_ref, tail_ref, cos_ref, sin_ref, w_hbm, x_ref, g_ref, dx1_ref, after_ref,
             dproj_ref, dx_ref, dg_ref, w_full, sems):
        _side_by_side(w_hbm, w_full, sems)

        @pl.when(pl.program_id(0) == 0)
        def _():
            dg_ref[...] = jnp.zeros_like(dg_ref)

        halves = [slice(0, tm // 2), slice(tm // 2, tm)]
        for rows in halves:
            c, s = cos_ref[rows, :], sin_ref[rows, :]
            for j in range(ATTN_W // 128):
                cols = slice(128 * j, 128 * (j + 1))
                dproj_ref[rows, cols] = _rope128(dq_ref[rows, cols] * SCALE, c, s, True).astype(BF16)
                dproj_ref[rows, ATTN_W + 128 * j:ATTN_W + 128 * (j + 1)] = _rope128(dk_ref[rows, cols], c, s, True).astype(BF16)
            dproj_ref[rows, 2 * ATTN_W:3 * ATTN_W] = dv_ref[rows, :].astype(BF16)
            dproj_ref[rows, 3 * ATTN_W:PROJ_W] = tail_ref[rows, :]
        dhs = [lax.dot_general(dproj_ref[rows, :], w_full[...], NT, preferred_element_type=F32) for rows in halves]
        for rows, dh in zip(halves, dhs):
            xhat, r = _rms_hat(x_ref[rows, :])
            dg_ref[...] += jnp.sum(dh * xhat, axis=0, keepdims=True)
            dx_ref[rows, :] = dx1_ref[rows, :] + _rms_bwd(xhat, r, g_ref[...], dh)

    return pl.pallas_call(
        body, name="in_proj_bwd", grid=(S // tm,),
        in_specs=[_rows(tm, ATTN_W)] * 3 + [_rows(tm, PROJ_W - 3 * ATTN_W), _rows(tm, 128), _rows(tm, 128),
                  pl.BlockSpec(memory_space=pl.ANY), _rows(tm, D_MODEL), _resident((1, D_MODEL)),
                  _rows(tm, D_MODEL), pl.BlockSpec(memory_space=pl.ANY)],
        out_specs=[_rows(tm, PROJ_W), _rows(tm, D_MODEL), pl.BlockSpec((1, D_MODEL), lambda i: (0, 0))],
        out_shape=[jax.ShapeDtypeStruct((S, PROJ_W), BF16), jax.ShapeDtypeStruct((S, D_MODEL), F32),
                   jax.ShapeDtypeStruct((1, D_MODEL), F32)],
        scratch_shapes=[pltpu.VMEM((D_MODEL, PROJ_W), BF16), pltpu.SemaphoreType.DMA((N_CHIPS,))],
        compiler_params=_params("arbitrary"),
    )(*dqkv, tail, cos, sin, w_in, x, g, dx1, after)


def _row_tile(rows):
    return ROW_TILE if rows % ROW_TILE == 0 else rows


def _chip_sums_bf16(name, grads, from_sibling, place):
    k = len(grads)
    _, n, rows, _ = grads[0].shape
    tr = _row_tile(rows)

    def body(place_ref, *refs):
        for g_ref, b_ref, o_ref in zip(refs[:k], refs[k:2 * k], refs[2 * k:]):
            o_ref[...] = (g_ref[0] + b_ref[...]).astype(BF16)

    mine = lambda g: pl.BlockSpec((1, 1, tr, g.shape[3]), lambda s, i, p: (p[0], s, i, 0))
    slab = lambda g: pl.BlockSpec((1, tr, g.shape[3]), lambda s, i, p: (s, i, 0))
    return pl.pallas_call(
        body, name=name, out_shape=[jax.ShapeDtypeStruct(g.shape[1:], BF16) for g in grads],
        grid_spec=pltpu.PrefetchScalarGridSpec(
            num_scalar_prefetch=1, grid=(n, rows // tr),
            in_specs=[mine(g) for g in grads] + [slab(g) for g in grads], out_specs=[slab(g) for g in grads]),
        compiler_params=_params("parallel", "parallel"),
    )(place, *grads, *from_sibling)


def _final_sums(name, grads, from_sibling, others, place):
    k = len(grads)
    rows = grads[0].shape[2]
    tr = _row_tile(rows)

    def body(place_ref, *refs):
        for a in range(k):
            own_ref, sib_ref = refs[a], refs[k + a]
            acc = own_ref[0, 0] + sib_ref[0]
            for o in refs[2 * k + 3 * a:2 * k + 3 * a + 3]:
                acc = acc + o[0].astype(F32)
            refs[5 * k + a][0] = acc

    own = lambda g: pl.BlockSpec((1, 1, tr, g.shape[3]), lambda i, p: (p[0], p[1], i, 0))
    sib = lambda g: pl.BlockSpec((1, tr, g.shape[3]), lambda i, p: (p[1], i, 0))
    other = lambda g, j: pl.BlockSpec((1, tr, g.shape[3]), lambda i, p: (j, i, 0))
    return pl.pallas_call(
        body, name=name, out_shape=[jax.ShapeDtypeStruct((2,) + g.shape[2:], F32) for g in grads],
        grid_spec=pltpu.PrefetchScalarGridSpec(
            num_scalar_prefetch=1, grid=(rows // tr,),
            in_specs=[own(g) for g in grads] + [sib(g) for g in grads] + [other(g, j) for g in grads for j in range(3)],
            out_specs=[pl.BlockSpec((1, tr, g.shape[3]), lambda i, p: (p[0], i, 0)) for g in grads]),
        compiler_params=_params("parallel"),
    )(place, *grads, *from_sibling, *[o for o in others for _ in range(3)])


def _adamw_update(w, g, m, v):
    m = ADAM_B1 * m + (1.0 - ADAM_B1) * g
    v = ADAM_B2 * v + (1.0 - ADAM_B2) * (g * g)
    m_hat = m * (1.0 / (1.0 - ADAM_B1 ** ADAM_STEP))
    v_hat = v * (1.0 / (1.0 - ADAM_B2 ** ADAM_STEP))
    return -ADAM_LR * (m_hat / (jnp.sqrt(v_hat) + ADAM_EPS) + ADAM_WD * w), m, v


def _adamw(name, params, after):
    k = len(params)
    rows = params[0][0].shape[0]
    tr = ADAMW_ROW_TILE if rows % ADAMW_ROW_TILE == 0 else rows

    def body(*refs):
        ins, outs = refs[:4 * k], refs[4 * k + 1:]
        for a in range(k):
            w_ref, g_ref, m_ref, v_ref = ins[4 * a:4 * a + 4]
            g = g_ref[...]
            outs[4 * a][...] = g
            outs[4 * a + 1][...], outs[4 * a + 2][...], outs[4 * a + 3][...] = _adamw_update(w_ref[...], g, m_ref[...], v_ref[...])

    spec = lambda w: pl.BlockSpec((tr, w.shape[1]), lambda i: (i, 0))
    out = pl.pallas_call(
        body, name=name, grid=(rows // tr,),
        in_specs=[spec(p[0]) for p in params for _ in range(4)] + [pl.BlockSpec(memory_space=pl.ANY)],
        out_specs=[spec(p[0]) for p in params for _ in range(4)],
        out_shape=[jax.ShapeDtypeStruct(p[0].shape, F32) for p in params for _ in range(4)],
        compiler_params=_params("parallel"),
    )(*[t for p in params for t in p], after)
    return [out[4 * a:4 * a + 4] for a in range(k)]


def _small_update(summed, chip, gains, gains_m, gains_v, taps, taps_m, taps_v):
    n = len(gains)
    widths = [g.shape[1] for g in gains]
    k, w = taps.shape

    def body(*refs):
        chip_ref, sum_ref = refs[0], refs[1]
        params = [refs[2 + 3 * i:5 + 3 * i] for i in range(n + 1)]
        outs = [refs[2 + 3 * (n + 1) + 4 * i:2 + 3 * (n + 1) + 4 * (i + 1)] for i in range(n + 1)]
        loss_ref = refs[-1]
        for i in range(n):
            g = sum_ref[i:i + 1, 0:widths[i]]
            wr, mr, vr = params[i]
            outs[i][0][...] = g
            outs[i][1][...], outs[i][2][...], outs[i][3][...] = _adamw_update(wr[...], g, mr[...], vr[...])
        g = sum_ref[n:n + k, 0:w]
        for j in range(1, N_CHIPS):
            g = jnp.where(chip_ref[0] == j, sum_ref[n:n + k, w * j:w * (j + 1)], g)
        wr, mr, vr = params[n]
        outs[n][0][...] = g
        outs[n][1][...], outs[n][2][...], outs[n][3][...] = _adamw_update(wr[...], g, mr[...], vr[...])
        loss_ref[...] = sum_ref[n + k:n + k + 1, 0:1]

    vmem = pl.BlockSpec(memory_space=pltpu.VMEM)
    operands = [chip, summed]
    for p in zip(list(gains) + [taps], list(gains_m) + [taps_m], list(gains_v) + [taps_v]):
        operands += list(p)
    shapes = [jax.ShapeDtypeStruct(p.shape, F32) for p in list(gains) + [taps] for _ in range(4)]
    out = pl.pallas_call(
        body, name="small_update", out_shape=shapes + [jax.ShapeDtypeStruct((1, 1), F32)],
        in_specs=[pl.BlockSpec(memory_space=pltpu.SMEM)] + [vmem] * (len(operands) - 1),
        out_specs=[vmem] * (len(shapes) + 1),
    )(*operands)
    return [out[4 * i:4 * (i + 1)] for i in range(n + 1)], out[-1]


def _sum_blocks(name, blocks):
    n, rows, cols = blocks.shape

    def body(b_ref, o_ref):
        acc = b_ref[0]
        for k in range(1, n):
            acc = acc + b_ref[k]
        o_ref[...] = acc

    return pl.pallas_call(body, name=name, out_shape=jax.ShapeDtypeStruct((rows, cols), F32))(blocks)


def _place():
    return lax.axis_index("x"), lax.axis_index("y"), lax.axis_index("c")


def _other_chips(x, y):
    return [(1 - x, y), (x, 1 - y), (1 - x, 1 - y)]


def _allgather_finish(name, shards, landed, pass_on):
    n = len(shards)

    def body(*refs):
        ins, outs, stage = refs[:n], refs[2 * n:3 * n], refs[3 * n:4 * n]
        send_sems, recv_sems, local_sems = refs[4 * n:]
        x, y, c = _place()
        chips = _other_chips(x, y)
        barrier = pltpu.get_barrier_semaphore()
        pl.semaphore_signal(barrier, inc=1, device_id=(x, y, 1 - c), device_id_type=MESH)
        pl.semaphore_wait(barrier, 1)

        def copy(a, k, chip, half):
            place = outs[a].at[2 * chip[0] + chip[1], half]
            return pltpu.make_async_remote_copy(
                src_ref=place, dst_ref=place, send_sem=send_sems.at[3 * a + k], recv_sem=recv_sems.at[3 * a + k],
                device_id=(x, y, 1 - c), device_id_type=MESH)

        load = [pltpu.make_async_copy(ins[a], stage[a], local_sems.at[a]) for a in range(n)]
        local = [pltpu.make_async_copy(stage[a], outs[a].at[2 * x + y], local_sems.at[a]) for a in range(n)]
        for cp in load:
            cp.start()
        passed = [copy(a, k, chip, c) for a in range(n) if pass_on[a] for k, chip in enumerate(chips)]
        for cp in passed:
            cp.start()
        for a in range(n):
            load[a].wait()
            local[a].start()
        for a in range(n):
            if pass_on[a]:
                for k, chip in enumerate(chips):
                    copy(a, k, chip, 1 - c).wait_recv()
        for cp in passed:
            cp.wait_send()
        for cp in local:
            cp.wait()

    any_spec = pl.BlockSpec(memory_space=pl.ANY)
    return pl.pallas_call(
        body, name=name,
        out_shape=[jax.ShapeDtypeStruct((N_CHIPS,) + s.shape, s.dtype) for s in shards],
        in_specs=[any_spec] * (2 * n), out_specs=[any_spec] * n,
        input_output_aliases={n + a: a for a in range(n)},
        scratch_shapes=[pltpu.VMEM(s.shape, s.dtype) for s in shards]
        + [pltpu.SemaphoreType.DMA((3 * n,)), pltpu.SemaphoreType.DMA((3 * n,)), pltpu.SemaphoreType.DMA((n,))],
        compiler_params=pltpu.CompilerParams(vmem_limit_bytes=VMEM_LIMIT_V7X, collective_id=HANDSHAKES["sibling"][0]),
    )(*shards, *landed)


def _plan_first_hop(x, y, c, shards, lands):
    return [(shards[a].at[c], lands[a].at[2 * x + y, c], lands[a].at[2 * chip[0] + chip[1], c], (*chip, c))
            for a in range(len(shards)) for chip in _other_chips(x, y)]


def _plan_pass_on(x, y, c, nothing, lands):
    def place(a, chip, half):
        return lands[a].at[2 * chip[0] + chip[1], half]

    return [(place(a, chip, c), place(a, chip, c), place(a, chip, 1 - c), (x, y, 1 - c))
            for a in range(len(lands)) for chip in _other_chips(x, y)]


def _plan_own_half_to_sibling(x, y, c, nothing, lands):
    return [(lands[a].at[c], lands[a].at[c], lands[a].at[1 - c], (x, y, 1 - c)) for a in range(len(lands))]


def _plan_other_half_to_sibling(x, y, c, grads, lands):
    return [(grads[a].at[1 - c], lands[a], lands[a], (x, y, 1 - c)) for a in range(len(grads))]


def _plan_to_other_chips(x, y, c, partials, lands):
    return [(partials[a].at[2 * chip[0] + chip[1]], lands[a].at[k], lands[a].at[k], (*chip, c))
            for a in range(len(partials)) for k, chip in enumerate(_other_chips(x, y))]


def _plan_to_all(x, y, c, blocks, lands):
    flips = [(fx, fy, fc) for fx in (0, 1) for fy in (0, 1) for fc in (0, 1) if (fx, fy, fc) != (0, 0, 0)]
    peers = [(1 - x if fx else x, 1 - y if fy else y, 1 - c if fc else c) for fx, fy, fc in flips]
    return [(blocks[0], lands[0].at[4 * x + 2 * y + c], lands[0].at[4 * p[0] + 2 * p[1] + p[2]], p) for p in peers]


def _planned_copies(plan, srcs, lands, send_sems, recv_sems):
    x, y, c = _place()

    def pair(k, src, there, here, to):
        make = lambda dst: pltpu.make_async_remote_copy(
            src_ref=src, dst_ref=dst, send_sem=send_sems.at[k], recv_sem=recv_sems.at[k], device_id=to, device_id_type=MESH)
        return make(there), make(here)

    return [pair(k, *entry) for k, entry in enumerate(plan(x, y, c, srcs, lands))]


_HBM_SPEC = pl.BlockSpec(memory_space=pltpu.HBM)
_SEM_SPEC = pl.BlockSpec(memory_space=pltpu.SEMAPHORE)


def _hbm(a):
    return pltpu.with_memory_space_constraint(a, pltpu.HBM)


HANDSHAKES = {
    "sibling": (1, lambda x, y, c: [(x, y, 1 - c)]),
}


def _exchange_start(name, plan, n_copies, srcs, land_shapes, after, lands=None, peers=None):
    if lands is None:
        lands = [lax.empty(s.shape, s.dtype) for s in land_shapes]
    land_shapes = lands
    ns, nl = len(srcs), len(land_shapes)
    n_in = ns + nl + 1
    collective_id, peers_of = HANDSHAKES[peers] if peers else (None, None)

    def body(*refs):
        if peers:
            who = peers_of(*_place())
            barrier = pltpu.get_barrier_semaphore()
            for peer in who:
                pl.semaphore_signal(barrier, inc=1, device_id=peer, device_id_type=MESH)
            pl.semaphore_wait(barrier, len(who))
        for send, _ in _planned_copies(plan, refs[:ns], refs[ns:ns + nl], refs[n_in], refs[n_in + 1]):
            send.start()
        refs[-1][...] = jnp.zeros_like(refs[-1])

    out = pl.pallas_call(
        body, name=name,
        out_shape=(pltpu.SemaphoreType.DMA((n_copies,)), pltpu.SemaphoreType.DMA((n_copies,)),
                   *[pltpu.HBM(s.shape, s.dtype) for s in land_shapes], jax.ShapeDtypeStruct((8, 128), F32)),
        in_specs=[_HBM_SPEC] * (ns + nl) + [pl.BlockSpec(memory_space=pl.ANY)],
        out_specs=(_SEM_SPEC, _SEM_SPEC, *[_HBM_SPEC] * nl, pl.BlockSpec(memory_space=pltpu.VMEM)),
        input_output_aliases={ns + i: 2 + i for i in range(nl)},
        compiler_params=pltpu.CompilerParams(has_side_effects=pltpu.SideEffectType.DATAFLOW_SIDE_EFFECTING,
                                             collective_id=collective_id),
    )(*[_hbm(s) for s in srcs], *[_hbm(l) for l in lands], after)
    return out[0], out[1], list(out[2:2 + nl]), out[-1]


def _exchange_wait(name, plan, srcs, started, after):
    send_sems, recv_sems, lands, _ = started
    ns, nl = len(srcs), len(lands)
    after = list(after) if isinstance(after, (list, tuple)) else [after]

    def body(*refs):
        for send, recv in _planned_copies(plan, refs[:ns], refs[ns:ns + nl], refs[ns + nl], refs[ns + nl + 1]):
            send.wait_send()
            recv.wait_recv()

    return pl.pallas_call(
        body, name=name, out_shape=[pltpu.HBM(l.shape, l.dtype) for l in lands],
        in_specs=[_HBM_SPEC] * (ns + nl) + [_SEM_SPEC, _SEM_SPEC] + [pl.BlockSpec(memory_space=pl.ANY)] * len(after),
        out_specs=[_HBM_SPEC] * nl, input_output_aliases={ns + i: i for i in range(nl)},
        compiler_params=pltpu.CompilerParams(has_side_effects=pltpu.SideEffectType.DATAFLOW_SIDE_EFFECTING),
    )(*[_hbm(s) for s in srcs], *lands, send_sems, recv_sems, *after)


def _like(arrays, lead, dtype=None):
    return [jax.ShapeDtypeStruct(tuple(lead) + a.shape[-2:], dtype or a.dtype) for a in arrays]


class _StepExchanges:
    def __init__(self, mats, conv_w):
        x, y, c = _place()
        self.place = jnp.stack([c, 2 * x + y]).astype(jnp.int32)
        shards = [w.astype(BF16).reshape(2, w.shape[0] // 2, w.shape[1]) for w in mats]
        self._in_shard = shards[:1]
        self._in = _exchange_start("w_in_allgather_start", _plan_first_hop, 3, self._in_shard,
                                   _like(self._in_shard, (N_CHIPS, 2)), shards[0])
        self.zero = self._in[3]
        taps = jnp.pad(conv_w, ((0, 8 - conv_w.shape[0]), (0, 128 - conv_w.shape[1])))
        self._rest_shards = shards[1:] + [jnp.stack([taps, jnp.zeros_like(taps)])]
        self._taps_shape = conv_w.shape
        self._groups = {}

    def w_in(self, after):
        landed = _exchange_wait("w_in_allgather_wait", _plan_first_hop, self._in_shard, self._in,
                                list(after) + self._rest_shards)
        (w_in,) = _allgather_finish("w_in_allgather_finish", self._in_shard, landed, [True])
        self._rest = _exchange_start("rest_allgather_start", _plan_first_hop, 3 * len(self._rest_shards),
                                     self._rest_shards, _like(self._rest_shards, (N_CHIPS, 2)), w_in)
        self.zero = self._rest[3]
        return w_in.reshape(N_CHIPS, 2 * w_in.shape[2], w_in.shape[3])

    def rest_weights(self, after):
        landed = _exchange_wait("rest_allgather_wait", _plan_first_hop, self._rest_shards, self._rest, after)
        kv, out, up, down, taps = _allgather_finish("rest_allgather_finish", self._rest_shards, landed,
                                                    [True, True, False, False, True])
        self._up_down = _exchange_start("up_down_pass_on_start", _plan_pass_on, 6, [], None, self.zero, lands=[up, down],
                                        peers="sibling")
        self.zero = self._up_down[3]
        k, w = self._taps_shape
        taps = taps[:, 0, :k, :w].transpose(1, 0, 2).reshape(k, N_CHIPS * w)
        return [g.reshape(N_CHIPS, 2 * g.shape[2], g.shape[3]) for g in (kv, out)], taps

    def up_down(self, after):
        full = _exchange_wait("up_down_pass_on_wait", _plan_pass_on, [], self._up_down, after)
        return [g.reshape(N_CHIPS, 2 * g.shape[2], g.shape[3]) for g in full]

    def send_grads(self, key, grads):
        grads = list(grads)
        started = _exchange_start(f"{key}_grads_to_sibling_start", _plan_other_half_to_sibling, len(grads), grads,
                                  _like(grads, (N_CHIPS,)), self.zero, peers="sibling")
        self._groups[key] = dict(grads=grads, to_sibling=started)
        self.zero = started[3]

    def grads_at_sibling(self, key, after):
        group = self._groups[key]
        grads = group["grads"]
        group["from_sibling"] = _exchange_wait(f"{key}_grads_to_sibling_wait", _plan_other_half_to_sibling, grads,
                                               group["to_sibling"], after)
        group["partials"] = _chip_sums_bf16(f"{key}_chip_sums", grads, group["from_sibling"], self.place)
        group["to_chips"] = _exchange_start(f"{key}_grads_to_chips_start", _plan_to_other_chips, 3 * len(grads),
                                            group["partials"], _like(group["partials"], (3,)), self.zero)
        self.zero = group["to_chips"][3]

    def grads_summed(self, key, after):
        group = self._groups[key]
        from_chips = _exchange_wait(f"{key}_grads_to_chips_wait", _plan_to_other_chips, group["partials"],
                                    group["to_chips"], after)
        return _final_sums(f"{key}_final_sums", group["grads"], group["from_sibling"], from_chips, self.place)

    def send_sums(self, key, sums):
        self._groups[key + "_sums"] = _exchange_start(f"{key}_sums_to_sibling_start", _plan_own_half_to_sibling,
                                                      len(sums), [], None, self.zero, lands=list(sums),
                                                      peers="sibling")
        self.zero = self._groups[key + "_sums"][3]

    def whole_sums(self, key, after):
        full = _exchange_wait(f"{key}_sums_to_sibling_wait", _plan_own_half_to_sibling, [], self._groups[key + "_sums"], after)
        return [t.reshape(2 * t.shape[1], t.shape[2]) for t in full]

    def send_small(self, block):
        self._small = block
        self._small_started = _exchange_start("small_grads_start", _plan_to_all, 7, [block],
                                              [jax.ShapeDtypeStruct((8,) + block.shape, block.dtype)], self.zero)
        self.zero = self._small_started[3]

    def small_summed(self, after):
        x, y, c = _place()
        (landed,) = _exchange_wait("small_grads_wait", _plan_to_all, [self._small], self._small_started, after)
        blocks = lax.dynamic_update_index_in_dim(landed, self._small, 4 * x + 2 * y + c, 0)
        return _sum_blocks("small_sum", blocks)


def _rope_tables(positions):
    half = HEAD // 2
    inv_freq = jnp.float32(ROPE_THETA) ** (-(jnp.arange(half, dtype=F32) * 2.0 / HEAD))
    ang = positions.astype(F32)[:, None] * inv_freq
    cos, sin = jnp.cos(ang), jnp.sin(ang)
    return jnp.tile(cos, (1, 4)), jnp.tile(jnp.concatenate([-sin, sin], axis=1), (1, 2))


def _local_step(x, mem, positions, target, gains, ex):
    g_pre_mix, g_mem, g_a, g_c, g_x, g_post_mix, g_pre_mlp, g_post_mlp = gains
    tm = ROW_TILE
    cos, sin = _rope_tables(positions)
    h = _pre_norm(x, g_pre_mix, ex.zero, tm)
    w_in = ex.w_in([h, cos, sin])

    q, k, v, bcu, qx = _in_proj_fwd(h, w_in, cos, sin, ex.zero, tm)
    ya, lse = _attn_fwd(q, k, v)
    (w_kv, w_out), conv_w = ex.rest_weights(lse)
    w_kv, w_out = (w.reshape(N_CHIPS * w.shape[1], w.shape[2]) for w in (w_kv, w_out))
    memn, mkv = _memkv_fwd(mem, g_mem, w_kv, ex.zero)
    yx, ycat, y2, x1 = _mix_fwd(ya, bcu, qx, mkv, conv_w, g_a, g_c, g_x, w_out, g_post_mix, x, tm)
    w_up, w_down = ex.up_down(x1)
    w_down = w_down.reshape(N_CHIPS * w_down.shape[1], w_down.shape[2])
    h2, f, du, df2, dx1, dg_pre_mlp, dg_post_mlp, loss = _mlp_fwd_bwd(x1, target, g_pre_mlp, g_post_mlp, w_up, w_down,
                                                                      MLP_ROW_TILE)
    gw_down = _weight_grad("grad_w_down", f, df2, True, ex.zero)
    gw_up = _weight_grad("grad_w_up", h2, du, False, ex.zero)
    ex.send_grads("early", [gw_up, gw_down])

    dy2, dya, delta, tail, dmkv, g_conv, dg_post_mix, dg_a, dg_c, dg_x = _mixer_bwd(
        dx1, y2, ya, yx, bcu, qx, mkv, conv_w, g_a, g_c, g_x, w_out, g_post_mix, ex.zero, tm)
    ex.grads_at_sibling("early", dy2)
    gw_out = _weight_grad("grad_w_out", ycat, dy2, True, ex.zero)
    gw_kv, dg_mem = _memkv_bwd(mem, g_mem, w_kv, dmkv)
    ex.send_grads("mid", [gw_out, gw_kv])
    dqkv = _attn_bwd(q, k, v, dya, lse, delta, ex.zero)
    ex.grads_at_sibling("mid", dqkv[0])
    dproj, grad_x, dg_pre_mix = _in_proj_bwd(dqkv, tail, cos, sin, w_in, x, g_pre_mix, dx1, ex.zero, tm)
    gain_grads = [dg_pre_mix, dg_mem, dg_a, dg_c, dg_x, dg_post_mix, dg_pre_mlp, dg_post_mlp]
    ex.send_small(_pack_small(gain_grads, g_conv, loss))
    gw_in = _weight_grad_w_in(h, dproj)
    ex.send_grads("late", [gw_in])
    return grad_x


def _pack_small(gains, conv, scalar=None):
    rows = [jnp.pad(g, ((0, 0), (0, D_MODEL - g.shape[1]))) for g in gains]
    rows.append(jnp.pad(conv, ((0, 0), (0, D_MODEL - conv.shape[1]))))
    last = jnp.zeros((SMALL_ROWS - 8 - conv.shape[0], D_MODEL), F32)
    rows.append(last if scalar is None else last.at[0:1, 0:1].set(scalar))
    return jnp.concatenate(rows, axis=0)


def _unpack_small(block, gain_widths, conv_width):
    gains = [block[i:i + 1, :w] for i, w in enumerate(gain_widths)]
    return gains, block[8:11, :conv_width], block[11, 0]


def kernel(x, mem, positions, g_pre_mix, g_mem, w_in, w_mem_kv, conv_w, g_attn_out, g_conv_out, g_xattn_out, w_out, g_post_mix, g_pre_mlp, w_up, w_down, g_post_mlp, loss_target, m_g_pre_mix, m_g_mem, m_w_in, m_w_mem_kv, m_conv_w, m_g_attn_out, m_g_conv_out, m_g_xattn_out, m_w_out, m_g_post_mix, m_g_pre_mlp, m_w_up, m_w_down, m_g_post_mlp, v_g_pre_mix, v_g_mem, v_w_in, v_w_mem_kv, v_conv_w, v_g_attn_out, v_g_conv_out, v_g_xattn_out, v_w_out, v_g_post_mix, v_g_pre_mlp, v_w_up, v_w_down, v_g_post_mlp):
    cx, cy, cc = _place()
    chip = 2 * cx + cy
    gains = [g_pre_mix, g_mem, g_attn_out, g_conv_out, g_xattn_out, g_post_mix, g_pre_mlp, g_post_mlp]
    gains_m = [m_g_pre_mix, m_g_mem, m_g_attn_out, m_g_conv_out, m_g_xattn_out, m_g_post_mix, m_g_pre_mlp, m_g_post_mlp]
    gains_v = [v_g_pre_mix, v_g_mem, v_g_attn_out, v_g_conv_out, v_g_xattn_out, v_g_post_mix, v_g_pre_mlp, v_g_post_mlp]
    gain_widths = [g.shape[1] for g in gains]
    mats = [w_in[0], w_mem_kv[0], w_out[0], w_up[0], w_down[0]]
    mats_m = [m_w_in[0], m_w_mem_kv[0], m_w_out[0], m_w_up[0], m_w_down[0]]
    mats_v = [v_w_in[0], v_w_mem_kv[0], v_w_out[0], v_w_up[0], v_w_down[0]]

    ex = _StepExchanges(mats, conv_w[0])
    grad_x = _local_step(x[0], mem[0], positions[0], loss_target[0], gains, ex)

    ex.send_sums("four", ex.grads_summed("early", ex.zero) + ex.grads_summed("mid", ex.zero))
    ex.grads_at_sibling("late", ex.zero)
    up_sum, down_sum, out_sum, kv_sum = ex.whole_sums("four", ex.zero)
    params = lambda a, g: (mats[a], g, mats_m[a], mats_v[a])
    new_up, new_down = _adamw("adamw_up_down", [params(3, up_sum), params(4, down_sum)], ex.zero)
    new_out, new_kv = _adamw("adamw_out_kv", [params(2, out_sum), params(1, kv_sum)], ex.zero)

    small, total = _small_update(ex.small_summed(new_kv[1]), chip.reshape(1).astype(jnp.int32), gains, gains_m,
                                 gains_v, conv_w[0], m_conv_w[0], v_conv_w[0])

    ex.send_sums("last", ex.grads_summed("late", small[0][1]))
    (in_sum,) = ex.whole_sums("last", ex.zero)
    (new_in,) = _adamw("adamw_in", [params(0, in_sum)], in_sum)
    mat_new = [new_in, new_kv, new_out, new_up, new_down]

    order = ["g_pre_mix", "g_mem", "w_in", "w_mem_kv", "conv_w", "g_attn_out", "g_conv_out", "g_xattn_out", "w_out",
             "g_post_mix", "g_pre_mlp", "w_up", "w_down", "g_post_mlp"]
    gain_names = ["g_pre_mix", "g_mem", "g_attn_out", "g_conv_out", "g_xattn_out", "g_post_mix", "g_pre_mlp", "g_post_mlp"]
    mat_names = ["w_in", "w_mem_kv", "w_out", "w_up", "w_down"]

    def leaf(kind, name):
        if name in gain_names:
            return small[gain_names.index(name)][kind]
        if name == "conv_w":
            return small[len(gain_names)][kind][None]
        return mat_new[mat_names.index(name)][kind][None]

    return (total[0, 0], grad_x[None], *[leaf(kind, name) for kind in range(4) for name in order])
```

```python
import jax
import jax.numpy as jnp
from jax import lax
from jax.experimental import pallas as pl
from jax.experimental.pallas import tpu as pltpu

F32, BF16 = jnp.float32, jnp.bfloat16

D_MODEL = 1024
ATTN_W = 512
CONV_W = 256
XATTN_W = 256
PROJ_W = 3 * ATTN_W + 3 * CONV_W + XATTN_W
D_FF = 4096
HEAD = 64
N_BACK = 128
DILATIONS = (1, 4, 16)
PATTERN_ORDER = DILATIONS[::-1]
ROPE_THETA = 10000.0
EPS = 1e-6
NEG_INF = -1e30
SCALE = HEAD ** -0.5
N_CHIPS = 4
SHARD_IN = PROJ_W // N_CHIPS
SHARD_FF = D_FF // N_CHIPS

ADAM_LR, ADAM_B1, ADAM_B2, ADAM_EPS, ADAM_WD, ADAM_STEP = 0.001, 0.9, 0.999, 1e-08, 0.01, 10

VMEM_LIMIT_V7X = 56 * 1024 * 1024
ROW_TILE = 512
MLP_ROW_TILE = 256
ADAMW_ROW_TILE = 256
SMALL_ROWS = 16

NT = (((1,), (1,)), ((), ()))
TN = (((0,), (0,)), ((), ()))
MESH = pl.DeviceIdType.MESH


def _params(*sem):
    return pltpu.CompilerParams(dimension_semantics=sem, vmem_limit_bytes=VMEM_LIMIT_V7X)


def _resident(shape):
    return pl.BlockSpec(shape, lambda *_: (0,) * len(shape), pipeline_mode=pl.Buffered(1))


def _rows(tm, width):
    return pl.BlockSpec((tm, width), lambda i: (i, 0))


def _rms_hat(x):
    r = lax.rsqrt(jnp.mean(x * x, axis=-1, keepdims=True) + EPS)
    return x * r, r


def _rms_bwd(xhat, r, g, dy):
    gdy = dy * g
    return r * (gdy - xhat * jnp.mean(xhat * gdy, axis=-1, keepdims=True))


def _rope128(t, cos, sin_signed, inverse):
    lane = lax.broadcasted_iota(jnp.int32, t.shape, 1)
    first_half = (lane % HEAD) < (HEAD // 2)
    rot = jnp.where(first_half, pltpu.roll(t, 128 - HEAD // 2, 1), pltpu.roll(t, HEAD // 2, 1))
    return t * cos - rot * sin_signed if inverse else t * cos + rot * sin_signed


def _pre_norm(x, g, after, tm):
    S = x.shape[0]

    def body(x_ref, g_ref, after_ref, h_ref):
        h_ref[...] = (_rms_hat(x_ref[...])[0] * g_ref[...]).astype(BF16)

    return pl.pallas_call(
        body, name="pre_norm", grid=(S // tm,),
        in_specs=[_rows(tm, D_MODEL), _resident((1, D_MODEL)), pl.BlockSpec(memory_space=pl.ANY)],
        out_specs=_rows(tm, D_MODEL), out_shape=jax.ShapeDtypeStruct((S, D_MODEL), BF16),
        compiler_params=_params("parallel"),
    )(x, g, after)


def _side_by_side(w_hbm, w_full, sems):
    @pl.when(pl.program_id(0) == 0)
    def _():
        copies = [pltpu.make_async_copy(w_hbm.at[j], w_full.at[:, pl.ds(SHARD_IN * j, SHARD_IN)], sems.at[j])
                  for j in range(N_CHIPS)]
        for cp in copies:
            cp.start()
        for cp in copies:
            cp.wait()


def _in_proj_fwd(h, w_in, cos, sin, after, tm):
    S = h.shape[0]

    def body(h_ref, w_hbm, cos_ref, sin_ref, after_ref, q_ref, k_ref, v_ref, bcu_ref, qx_ref, proj, w_full, sems):
        _side_by_side(w_hbm, w_full, sems)
        proj[...] = jnp.dot(h_ref[...], w_full[...], preferred_element_type=F32)
        c, s = cos_ref[...], sin_ref[...]
        for j in range(ATTN_W // 128):
            lo = 128 * j
            q_ref[:, lo:lo + 128] = _rope128(proj[:, lo:lo + 128], c, s, False) * SCALE
            k_ref[:, lo:lo + 128] = _rope128(proj[:, ATTN_W + lo:ATTN_W + lo + 128], c, s, False)
        v_ref[...] = proj[:, 2 * ATTN_W:3 * ATTN_W]
        bcu_ref[...] = proj[:, 3 * ATTN_W:3 * ATTN_W + 3 * CONV_W]
        qx_ref[...] = proj[:, 3 * ATTN_W + 3 * CONV_W:PROJ_W].astype(BF16)

    return pl.pallas_call(
        body, name="in_proj_fwd", grid=(S // tm,),
        in_specs=[_rows(tm, D_MODEL), pl.BlockSpec(memory_space=pl.ANY), _rows(tm, 128), _rows(tm, 128),
                  pl.BlockSpec(memory_space=pl.ANY)],
        out_specs=[_rows(tm, ATTN_W), _rows(tm, ATTN_W), _rows(tm, ATTN_W), _rows(tm, 3 * CONV_W), _rows(tm, XATTN_W)],
        out_shape=[jax.ShapeDtypeStruct((S, ATTN_W), F32), jax.ShapeDtypeStruct((S, ATTN_W), F32),
                   jax.ShapeDtypeStruct((S, ATTN_W), F32), jax.ShapeDtypeStruct((S, 3 * CONV_W), F32),
                   jax.ShapeDtypeStruct((S, XATTN_W), BF16)],
        scratch_shapes=[pltpu.VMEM((tm, PROJ_W), F32), pltpu.VMEM((D_MODEL, PROJ_W), BF16),
                        pltpu.SemaphoreType.DMA((N_CHIPS,))],
        compiler_params=_params("arbitrary"),
    )(h, w_in, cos, sin, after)


def _memkv_fwd(mem, g_mem, w_kv, after):
    n_mem = mem.shape[0]

    def body(mem_ref, g_ref, w_ref, after_ref, mn_ref, kv_ref):
        mhat, _ = _rms_hat(mem_ref[...])
        mn = (mhat * g_ref[...]).astype(BF16)
        mn_ref[...] = mn
        kv_ref[...] = jnp.dot(mn, w_ref[...], preferred_element_type=F32).astype(BF16)

    vmem = pl.BlockSpec(memory_space=pltpu.VMEM)
    return pl.pallas_call(
        body, name="memkv_fwd", in_specs=[vmem, vmem, vmem, pl.BlockSpec(memory_space=pl.ANY)], out_specs=[vmem, vmem],
        out_shape=[jax.ShapeDtypeStruct((n_mem, D_MODEL), BF16), jax.ShapeDtypeStruct((n_mem, 2 * XATTN_W), BF16)],
        compiler_params=pltpu.CompilerParams(vmem_limit_bytes=VMEM_LIMIT_V7X),
    )(mem, g_mem, w_kv, after)


def _fill_band_bias(bias):
    row = lax.broadcasted_iota(jnp.int32, (N_BACK, 2 * N_BACK), 0)
    col = lax.broadcasted_iota(jnp.int32, (N_BACK, 2 * N_BACK), 1)
    band = (col >= row) & (col <= row + N_BACK)
    bias[1] = jnp.where(band, 0.0, NEG_INF)
    bias[0] = jnp.where(band & (col >= N_BACK), 0.0, NEG_INF)


def _strided(start, size, d):
    return pl.ds(start, size) if d == 1 else pl.ds(start, size, stride=d)


def _group_starts(g, G, nb, d):
    t0 = g * G
    r, n0 = lax.shift_right_logical(t0, nb.bit_length() - 1), lax.bitwise_and(t0, nb - 1)
    first = r + n0 * (N_BACK * d)
    before = r + jnp.maximum(n0 - 1, 0) * (N_BACK * d)
    starts = [before] + [first + u * (N_BACK * d) for u in range(G)]
    if d == 1:
        starts = [pl.multiple_of(st, N_BACK) for st in starts]
    return starts, n0


def _step_blocks(i, U, nb, d):
    G = min(U, nb)
    whole = G == nb
    row_blocks, blocks = [], []
    for grp in range(U // G):
        starts, n0 = _group_starts(i * (U // G) + grp, G, nb, d)
        base = len(row_blocks)
        if whole:
            row_blocks += [_strided(st, N_BACK, d) for st in starts[1:]]
            blocks += [(base + max(u - 1, 0), base + u, min(u, 1)) for u in range(G)]
        else:
            row_blocks += [_strided(st, N_BACK, d) for st in starts]
            blocks += [(base + u, base + u + 1, jnp.minimum(n0, 1) if u == 0 else 1) for u in range(G)]
    return row_blocks, blocks


def _by_head(a, b):
    lane = lax.broadcasted_iota(jnp.int32, (a.shape[0], 2 * HEAD), 1)
    return jnp.where(lane < HEAD, a, b)


def _head_only(t, hh):
    lane = lax.broadcasted_iota(jnp.int32, t.shape, 1)
    return jnp.where((lane < HEAD) == (hh == 0), t, jnp.zeros_like(t))


def _stack_heads(t):
    return jnp.concatenate([_head_only(t, 0), _head_only(t, 1)], axis=0)


def _head_columns(t):
    return jnp.concatenate([t[:, 0:1], t[:, HEAD:HEAD + 1]], axis=0)


def _unstack(t):
    return _by_head(t[:N_BACK], t[N_BACK:])


def _unstack_columns(t):
    return _by_head(jnp.broadcast_to(t[:N_BACK], (N_BACK, 2 * HEAD)), jnp.broadcast_to(t[N_BACK:], (N_BACK, 2 * HEAD)))


FWD_BLOCKS_PER_STEP = 4
BWD_BLOCKS_PER_STEP = 4
BWD_CHUNK = 64


def _attn_fwd(q, k, v):
    S = q.shape[0]
    U = FWD_BLOCKS_PER_STEP

    def body(q_ref, k_ref, v_ref, y_ref, m_ref, l_scr, bias):
        _fill_band_bias(bias)
        for g, d in enumerate(PATTERN_ORDER):
            nb = S // d // N_BACK
            first_pattern, last_pattern = g == 0, g == len(PATTERN_ORDER) - 1

            def step(i, carry, d=d, nb=nb, first_pattern=first_pattern, last_pattern=last_pattern):
                row_blocks, blocks = _step_blocks(i, U, nb, d)
                kb = [k_ref[r, :].astype(BF16) for r in row_blocks]
                ss = []
                for before, own, which in blocks:
                    kw = jnp.concatenate([kb[before], kb[own]], 0)
                    qs = _stack_heads(q_ref[row_blocks[own], :].astype(BF16))
                    b = bias[which]
                    ss.append(lax.dot_general(qs, kw, NT, preferred_element_type=F32) + jnp.concatenate([b, b], axis=0))
                ms = [jnp.max(s, axis=1, keepdims=True) for s in ss]
                ps = [jnp.exp(s - m) for s, m in zip(ss, ms)]
                ls = [jnp.sum(p, axis=1, keepdims=True) for p in ps]
                vb = [v_ref[r, :].astype(BF16) for r in row_blocks]
                os_ = [jnp.dot(ps[u].astype(BF16), jnp.concatenate([vb[before], vb[own]], 0), preferred_element_type=F32)
                       for u, (before, own, _) in enumerate(blocks)]
                for u, (_, own, _) in enumerate(blocks):
                    o_g, m_g, l_g = _unstack(os_[u]), _unstack_columns(ms[u]), _unstack_columns(ls[u])
                    r = row_blocks[own]
                    if first_pattern:
                        m_new, l_new, acc = m_g, l_g, o_g
                    else:
                        m_old = m_ref[r, :]
                        m_new = jnp.maximum(m_old, m_g)
                        alpha, beta = jnp.exp(m_old - m_new), jnp.exp(m_g - m_new)
                        l_new = l_scr[r, :] * alpha + l_g * beta
                        acc = y_ref[r, :] * alpha + o_g * beta
                    if last_pattern:
                        y_ref[r, :] = acc / l_new
                        m_ref[r, :] = m_new + jnp.log(l_new)
                    else:
                        y_ref[r, :] = acc
                        m_ref[r, :] = m_new
                        l_scr[r, :] = l_new
                return carry

            lax.fori_loop(0, d * nb // U, step, 0)

    col = pl.BlockSpec((S, 2 * HEAD), lambda j: (0, j))
    return pl.pallas_call(
        body, name="attn_fwd", grid=(q.shape[1] // (2 * HEAD),),
        in_specs=[col, col, col], out_specs=[col, col],
        out_shape=[jax.ShapeDtypeStruct(q.shape, F32)] * 2,
        scratch_shapes=[pltpu.VMEM((S, 2 * HEAD), F32), pltpu.VMEM((2, N_BACK, 2 * N_BACK), F32)],
        compiler_params=_params("parallel"),
    )(q, k, v)


def _attn_bwd(q, k, v, dy, lse, delta, after):
    S = q.shape[0]
    U = BWD_BLOCKS_PER_STEP

    def body(q_ref, k_ref, v_ref, dy_ref, lse_ref, delta_ref, after_ref, dq_ref, dk_ref, dv_ref, bias):
        _fill_band_bias(bias)
        nb_first = S // PATTERN_ORDER[0] // N_BACK
        first_writes_all = min(U, nb_first) == nb_first
        if not first_writes_all:
            dk_ref[...] = jnp.zeros_like(dk_ref)
            dv_ref[...] = jnp.zeros_like(dv_ref)
        for g, d in enumerate(PATTERN_ORDER):
            nb = S // d // N_BACK

            def step(i, carry, d=d, nb=nb, g=g):
                row_blocks, blocks = _step_blocks(i, U, nb, d)
                kb = [k_ref[r, :].astype(BF16) for r in row_blocks]
                vb = [v_ref[r, :].astype(BF16) for r in row_blocks]
                kws = [jnp.concatenate([kb[before], kb[own]], 0) for before, own, _ in blocks]
                vws = [jnp.concatenate([vb[before], vb[own]], 0) for before, own, _ in blocks]
                qss = [_stack_heads(q_ref[row_blocks[own], :].astype(BF16)) for _, own, _ in blocks]
                doss = [_stack_heads(dy_ref[row_blocks[own], :].astype(BF16)) for _, own, _ in blocks]
                ss = [lax.dot_general(qss[u], kws[u], NT, preferred_element_type=F32) for u in range(U)]
                dps = [lax.dot_general(doss[u], vws[u], NT, preferred_element_type=F32) for u in range(U)]
                pbs, dss = [], []
                for u, (_, own, which) in enumerate(blocks):
                    lse_c = _head_columns(lse_ref[row_blocks[own], :])
                    delta_c = _head_columns(delta_ref[row_blocks[own], :])
                    p_parts, ds_parts = [], []
                    for r0 in range(0, 2 * N_BACK, BWD_CHUNK):
                        r = slice(r0, r0 + BWD_CHUNK)
                        mask = bias[which, r0 % N_BACK:r0 % N_BACK + BWD_CHUNK, :]
                        p_r = jnp.exp(ss[u][r] + mask - lse_c[r])
                        p_parts.append(p_r.astype(BF16))
                        ds_parts.append((p_r * (dps[u][r] - delta_c[r])).astype(BF16))
                    pbs.append(jnp.concatenate(p_parts, axis=0))
                    dss.append(jnp.concatenate(ds_parts, axis=0))
                dqs = [jnp.dot(dss[u], kws[u], preferred_element_type=F32) for u in range(U)]
                dkws = [lax.dot_general(dss[u], qss[u], TN, preferred_element_type=F32) for u in range(U)]
                dvws = [lax.dot_general(pbs[u], doss[u], TN, preferred_element_type=F32) for u in range(U)]
                dk_parts, dv_parts = [None] * len(row_blocks), [None] * len(row_blocks)
                for u, (before, own, _) in enumerate(blocks):
                    dq = _unstack(dqs[u])
                    if g == 0:
                        dq_ref[row_blocks[own], :] = dq
                    else:
                        dq_ref[row_blocks[own], :] += dq
                    for idx, dkp, dvp in ((before, dkws[u][:N_BACK], dvws[u][:N_BACK]),
                                          (own, dkws[u][N_BACK:], dvws[u][N_BACK:])):
                        dk_parts[idx] = dkp if dk_parts[idx] is None else dk_parts[idx] + dkp
                        dv_parts[idx] = dvp if dv_parts[idx] is None else dv_parts[idx] + dvp
                for idx, r in enumerate(row_blocks):
                    if g == 0 and first_writes_all:
                        dk_ref[r, :] = dk_parts[idx]
                        dv_ref[r, :] = dv_parts[idx]
                    else:
                        dk_ref[r, :] += dk_parts[idx]
                        dv_ref[r, :] += dv_parts[idx]
                return carry

            lax.fori_loop(0, d * nb // U, step, 0)

    col = pl.BlockSpec((S, 2 * HEAD), lambda j: (0, j))
    return pl.pallas_call(
        body, name="attn_bwd", grid=(q.shape[1] // (2 * HEAD),),
        in_specs=[col] * 6 + [pl.BlockSpec(memory_space=pl.ANY)], out_specs=[col] * 3,
        out_shape=[jax.ShapeDtypeStruct(q.shape, F32)] * 3,
        scratch_shapes=[pltpu.VMEM((2, N_BACK, 2 * N_BACK), F32)],
        compiler_params=_params("parallel"),
    )(q, k, v, dy, lse, delta, after)


def _shift_down(z, before, k):
    row = lax.broadcasted_iota(jnp.int32, z.shape, 0)
    out = pltpu.roll(z, k, 0)
    for i in range(k):
        out = jnp.where(row == i, before[8 - k + i:8 - k + i + 1, :], out)
    return out


def _shift_up(z, after, k):
    rows = z.shape[0]
    row = lax.broadcasted_iota(jnp.int32, z.shape, 0)
    out = pltpu.roll(z, rows - k, 0)
    for i in range(k):
        out = jnp.where(row == rows - k + i, after[i:i + 1, :], out)
    return out


def _conv_fwd(bcu, before, is_first, w):
    b, c, u = bcu[:, 0:CONV_W], bcu[:, CONV_W:2 * CONV_W], bcu[:, 2 * CONV_W:3 * CONV_W]
    z = c * u
    zb = jnp.where(is_first, 0.0, before[:, CONV_W:2 * CONV_W] * before[:, 2 * CONV_W:3 * CONV_W])
    z1, z2 = _shift_down(z, zb, 1), _shift_down(z, zb, 2)
    cv = w[0:1, :] * z2 + w[1:2, :] * z1 + w[2:3, :] * z
    return b, c, u, z, z1, z2, cv


def _halo_before(tm, width):
    return pl.BlockSpec((8, width), lambda i: (jnp.maximum(i * (tm // 8) - 1, 0), 0))


def _mix_fwd(ya, bcu, qx, mkv, conv_w, g_a, g_c, g_x, w_out, g_post, x, tm):
    S = x.shape[0]

    def body(ya_ref, bcu_ref, before_ref, qx_ref, mkv_ref, cw_ref, ga_ref, gc_ref, gx_ref,
             wo_ref, gp_ref, x_ref, yx_ref, ycat_ref, y2_ref, x1_ref):
        ya = ya_ref[...]
        b, _, _, _, _, _, cv = _conv_fwd(bcu_ref[...], before_ref[...], pl.program_id(0) == 0, cw_ref[...])
        yc = b * cv

        qxb, mkvb = qx_ref[...], mkv_ref[...]
        for hd in range(XATTN_W // HEAD):
            sl = slice(HEAD * hd, HEAD * (hd + 1))
            s = lax.dot_general(qxb[:, sl], mkvb[:, sl], NT, preferred_element_type=F32) * SCALE
            mx = jnp.max(s, axis=1, keepdims=True)
            p = jnp.exp(s - mx)
            l = jnp.sum(p, axis=1, keepdims=True)
            vm = mkvb[:, XATTN_W + HEAD * hd:XATTN_W + HEAD * (hd + 1)]
            yx_ref[:, sl] = jnp.dot(p.astype(BF16), vm, preferred_element_type=F32) / l
        yx = yx_ref[...]

        ycat_ref[:, 0:ATTN_W] = (_rms_hat(ya)[0] * ga_ref[...]).astype(BF16)
        ycat_ref[:, ATTN_W:ATTN_W + CONV_W] = (_rms_hat(yc)[0] * gc_ref[...]).astype(BF16)
        ycat_ref[:, ATTN_W + CONV_W:D_MODEL] = (_rms_hat(yx)[0] * gx_ref[...]).astype(BF16)
        y2 = jnp.dot(ycat_ref[...], wo_ref[...], preferred_element_type=F32)
        y2_ref[...] = y2
        x1_ref[...] = x_ref[...] + _rms_hat(y2)[0] * gp_ref[...]

    n_mem = mkv.shape[0]
    return pl.pallas_call(
        body, name="mix_fwd", grid=(S // tm,),
        in_specs=[_rows(tm, ATTN_W), _rows(tm, 3 * CONV_W), _halo_before(tm, 3 * CONV_W), _rows(tm, XATTN_W),
                  _resident((n_mem, 2 * XATTN_W)), _resident((3, CONV_W)), _resident((1, ATTN_W)),
                  _resident((1, CONV_W)), _resident((1, XATTN_W)), _resident((D_MODEL, D_MODEL)),
                  _resident((1, D_MODEL)), _rows(tm, D_MODEL)],
        out_specs=[_rows(tm, XATTN_W), _rows(tm, D_MODEL), _rows(tm, D_MODEL), _rows(tm, D_MODEL)],
        out_shape=[jax.ShapeDtypeStruct((S, XATTN_W), F32), jax.ShapeDtypeStruct((S, D_MODEL), BF16),
                   jax.ShapeDtypeStruct((S, D_MODEL), F32), jax.ShapeDtypeStruct((S, D_MODEL), F32)],
        compiler_params=_params("parallel"),
    )(ya, bcu, bcu, qx, mkv, conv_w, g_a, g_c, g_x, w_out, g_post, x)


def _mlp_fwd_bwd(x1, target, g_pre, g_post, w_up, w_down, tm):
    S = x1.shape[0]
    n_ff = D_FF // SHARD_FF

    def body(x1_ref, t_ref, gpre_ref, gpost_ref, wup_ref, wdn_ref,
             h2_ref, f_ref, du_ref, df2_ref, dx1_ref, dgpre_ref, dgpost_ref, loss_ref, u_scr):
        @pl.when(pl.program_id(0) == 0)
        def _():
            dgpre_ref[...] = jnp.zeros_like(dgpre_ref)
            dgpost_ref[...] = jnp.zeros_like(dgpost_ref)
            loss_ref[...] = jnp.zeros_like(loss_ref)

        x1 = x1_ref[...]
        x1hat, r1 = _rms_hat(x1)
        h2 = (x1hat * gpre_ref[...]).astype(BF16)
        h2_ref[...] = h2
        f2 = jnp.zeros((tm, D_MODEL), F32)
        for j in range(n_ff):
            cols = slice(SHARD_FF * j, SHARD_FF * (j + 1))
            u = jnp.maximum(jnp.dot(h2, wup_ref[j], preferred_element_type=F32), 0.0)
            u_scr[:, cols] = u
            f = (u * u).astype(BF16)
            f_ref[:, cols] = f
            f2 = f2 + jnp.dot(f, wdn_ref[cols, :], preferred_element_type=F32)
        f2hat, r2 = _rms_hat(f2)
        err = x1 + f2hat * gpost_ref[...] - t_ref[...]
        loss_ref[...] += 0.5 * jnp.sum(jnp.mean(err * err, axis=-1, keepdims=True), axis=0, keepdims=True)
        dx2 = err * (1.0 / D_MODEL)
        dgpost_ref[...] += jnp.sum(dx2 * f2hat, axis=0, keepdims=True)
        df2 = _rms_bwd(f2hat, r2, gpost_ref[...], dx2).astype(BF16)
        df2_ref[...] = df2
        dh2 = jnp.zeros((tm, D_MODEL), F32)
        for j in range(n_ff):
            cols = slice(SHARD_FF * j, SHARD_FF * (j + 1))
            df = lax.dot_general(df2, wdn_ref[cols, :], NT, preferred_element_type=F32)
            du = (2.0 * u_scr[:, cols] * df).astype(BF16)
            du_ref[:, cols] = du
            dh2 = dh2 + lax.dot_general(du, wup_ref[j], NT, preferred_element_type=F32)
        dgpre_ref[...] += jnp.sum(dh2 * x1hat, axis=0, keepdims=True)
        dx1_ref[...] = dx2 + _rms_bwd(x1hat, r1, gpre_ref[...], dh2)

    acc = pl.BlockSpec((1, D_MODEL), lambda i: (0, 0))
    return pl.pallas_call(
        body, name="mlp_fwd_bwd", grid=(S // tm,),
        in_specs=[_rows(tm, D_MODEL), _rows(tm, D_MODEL), _resident((1, D_MODEL)), _resident((1, D_MODEL)),
                  _resident((n_ff, D_MODEL, SHARD_FF)), _resident((D_FF, D_MODEL))],
        out_specs=[_rows(tm, D_MODEL), _rows(tm, D_FF), _rows(tm, D_FF), _rows(tm, D_MODEL), _rows(tm, D_MODEL),
                   acc, acc, pl.BlockSpec((1, 1), lambda i: (0, 0))],
        out_shape=[jax.ShapeDtypeStruct((S, D_MODEL), BF16), jax.ShapeDtypeStruct((S, D_FF), BF16),
                   jax.ShapeDtypeStruct((S, D_FF), BF16), jax.ShapeDtypeStruct((S, D_MODEL), BF16),
                   jax.ShapeDtypeStruct((S, D_MODEL), F32), jax.ShapeDtypeStruct((1, D_MODEL), F32),
                   jax.ShapeDtypeStruct((1, D_MODEL), F32), jax.ShapeDtypeStruct((1, 1), F32)],
        scratch_shapes=[pltpu.VMEM((tm, D_FF), F32)],
        compiler_params=_params("arbitrary"),
    )(x1, target, g_pre, g_post, w_up, w_down)


def _weight_grad(name, a, b, rows_sharded, after):
    S, K = a.shape
    N = b.shape[1]
    if rows_sharded:
        tk, tn = K // N_CHIPS, N
        a_spec = pl.BlockSpec((S, tk), lambda j: (0, j))
        b_spec = pl.BlockSpec((S, tn), lambda j: (0, 0), pipeline_mode=pl.Buffered(1))
    else:
        tk, tn = K, N // N_CHIPS
        a_spec = pl.BlockSpec((S, tk), lambda j: (0, 0), pipeline_mode=pl.Buffered(1))
        b_spec = pl.BlockSpec((S, tn), lambda j: (0, j))
    half = tk // 2

    def body(a_ref, b_ref, after_ref, o_ref):
        res = lax.dot_general(a_ref[...], b_ref[...], TN, preferred_element_type=F32)
        o_ref[0, 0] = res[:half]
        o_ref[1, 0] = res[half:]

    return pl.pallas_call(
        body, name=name, grid=(N_CHIPS,), in_specs=[a_spec, b_spec, pl.BlockSpec(memory_space=pl.ANY)],
        out_specs=pl.BlockSpec((2, 1, half, tn), lambda j: (0, j, 0, 0)),
        out_shape=jax.ShapeDtypeStruct((2, N_CHIPS, half, tn), F32),
        compiler_params=_params("parallel"),
    )(a, b, after)


def _weight_grad_w_in(h, dproj):
    S, K = h.shape
    step_w = 2 * 256
    n_steps = PROJ_W // step_w
    half = K // 2

    def body(a_ref, b_ref, o_ref):
        res = lax.dot_general(a_ref[...], b_ref[...], TN, preferred_element_type=F32)
        for step in range(n_steps):
            @pl.when(pl.program_id(0) == step)
            def _(step=step):
                lo = step * step_w
                while lo < (step + 1) * step_w:
                    chip = lo // SHARD_IN
                    hi = min((step + 1) * step_w, (chip + 1) * SHARD_IN)
                    for hh in range(2):
                        o_ref[hh, chip, :, lo - chip * SHARD_IN:hi - chip * SHARD_IN] = (
                            res[half * hh:half * (hh + 1), lo - step * step_w:hi - step * step_w])
                    lo = hi

    return pl.pallas_call(
        body, name="grad_w_in", grid=(n_steps,),
        in_specs=[pl.BlockSpec((S, K), lambda j: (0, 0), pipeline_mode=pl.Buffered(1)),
                  pl.BlockSpec((S, step_w), lambda j: (0, j))],
        out_specs=pl.BlockSpec((2, N_CHIPS, half, SHARD_IN), lambda j: (0, 0, 0, 0)),
        out_shape=jax.ShapeDtypeStruct((2, N_CHIPS, half, SHARD_IN), F32),
        compiler_params=_params("arbitrary"),
    )(h, dproj)


def _mixer_bwd(dx1, y2, ya, yx, bcu, qx, mkv, conv_w, g_a, g_c, g_x, w_out, g_post, after, tm):
    S = dx1.shape[0]
    n_mem = mkv.shape[0]
    n_tiles = S // tm

    def body(dx1_ref, y2_ref, ya_ref, yx_ref, bcu_ref, before_ref, qx_ref, mkv_ref, cw_ref, ga_ref, gc_ref, gx_ref,
             wo_ref, gp_ref, after_ref, dy2_ref, dya_ref, delta_ref, tail_ref, dmkv_ref, dcw_ref, dgp_ref, dga_ref,
             dgc_ref, dgx_ref, carry):
        step = pl.program_id(0)
        first_tile = step == n_tiles - 1

        @pl.when(step == 0)
        def _():
            for ref in (dmkv_ref, dcw_ref, dgp_ref, dga_ref, dgc_ref, dgx_ref, carry):
                ref[...] = jnp.zeros_like(ref)

        dx1 = dx1_ref[...]
        y2hat, r2 = _rms_hat(y2_ref[...])
        dgp_ref[...] += jnp.sum(dx1 * y2hat, axis=0, keepdims=True)
        dy2 = _rms_bwd(y2hat, r2, gp_ref[...], dx1).astype(BF16)
        dy2_ref[...] = dy2
        dycat = lax.dot_general(dy2, wo_ref[...], NT, preferred_element_type=F32)

        d_na = dycat[:, 0:ATTN_W]
        ya = ya_ref[...]
        yahat, ra = _rms_hat(ya)
        dga_ref[...] += jnp.sum(d_na * yahat, axis=0, keepdims=True)
        dya = _rms_bwd(yahat, ra, ga_ref[...], d_na)
        dya_ref[...] = dya
        prod = dya * ya
        hi = prod.astype(BF16)
        lo = (prod - hi.astype(F32)).astype(BF16)
        head_of = lambda axis: lax.shift_right_logical(lax.broadcasted_iota(jnp.int32, (ATTN_W, ATTN_W), axis),
                                                       HEAD.bit_length() - 1)
        ones = jnp.where(head_of(0) == head_of(1), 1.0, 0.0).astype(BF16)
        delta_ref[...] = jnp.dot(hi, ones, preferred_element_type=F32) + jnp.dot(lo, ones, preferred_element_type=F32)

        w = cw_ref[...]
        b, c, u, z, z1, z2, cv = _conv_fwd(bcu_ref[...], before_ref[...], first_tile, w)
        d_nc = dycat[:, ATTN_W:ATTN_W + CONV_W]
        ychat, rc = _rms_hat(b * cv)
        dgc_ref[...] += jnp.sum(d_nc * ychat, axis=0, keepdims=True)
        dyc = _rms_bwd(ychat, rc, gc_ref[...], d_nc)
        dcv = dyc * b
        behind = carry[...]
        dz = w[2:3, :] * dcv + w[1:2, :] * _shift_up(dcv, behind, 1) + w[0:1, :] * _shift_up(dcv, behind, 2)
        carry[...] = dcv[0:8, :]
        dcw_ref[0:1, :] += jnp.sum(dcv * z2, axis=0, keepdims=True)
        dcw_ref[1:2, :] += jnp.sum(dcv * z1, axis=0, keepdims=True)
        dcw_ref[2:3, :] += jnp.sum(dcv * z, axis=0, keepdims=True)
        tail_ref[:, 0:CONV_W] = (dyc * cv).astype(BF16)
        tail_ref[:, CONV_W:2 * CONV_W] = (dz * u).astype(BF16)
        tail_ref[:, 2 * CONV_W:3 * CONV_W] = (dz * c).astype(BF16)

        d_nx = dycat[:, ATTN_W + CONV_W:D_MODEL]
        yxhat, rx = _rms_hat(yx_ref[...])
        dgx_ref[...] += jnp.sum(d_nx * yxhat, axis=0, keepdims=True)
        dyx = _rms_bwd(yxhat, rx, gx_ref[...], d_nx)
        qxb, mkvb = qx_ref[...], mkv_ref[...]
        for hd in range(XATTN_W // HEAD):
            sl = slice(HEAD * hd, HEAD * (hd + 1))
            vsl = slice(XATTN_W + HEAD * hd, XATTN_W + HEAD * (hd + 1))
            s = lax.dot_general(qxb[:, sl], mkvb[:, sl], NT, preferred_element_type=F32) * SCALE
            e = jnp.exp(s - jnp.max(s, axis=1, keepdims=True))
            p = e / jnp.sum(e, axis=1, keepdims=True)
            dob = dyx[:, sl].astype(BF16)
            dp = lax.dot_general(dob, mkvb[:, vsl], NT, preferred_element_type=F32)
            ds = (p * (dp - jnp.sum(p * dp, axis=1, keepdims=True)) * SCALE).astype(BF16)
            tail_ref[:, 3 * CONV_W + HEAD * hd:3 * CONV_W + HEAD * (hd + 1)] = jnp.dot(
                ds, mkvb[:, sl], preferred_element_type=F32).astype(BF16)
            dmkv_ref[:, sl] += lax.dot_general(ds, qxb[:, sl], TN, preferred_element_type=F32)
            dmkv_ref[:, vsl] += lax.dot_general(p.astype(BF16), dob, TN, preferred_element_type=F32)

    rows = lambda width: pl.BlockSpec((tm, width), lambda i: (n_tiles - 1 - i, 0))
    before = pl.BlockSpec((8, 3 * CONV_W), lambda i: (jnp.maximum((n_tiles - 1 - i) * (tm // 8) - 1, 0), 0))
    acc = lambda r, w: pl.BlockSpec((r, w), lambda i: (0, 0))
    return pl.pallas_call(
        body, name="mixer_bwd", grid=(n_tiles,),
        in_specs=[rows(D_MODEL), rows(D_MODEL), rows(ATTN_W), rows(XATTN_W), rows(3 * CONV_W), before, rows(XATTN_W),
                  _resident((n_mem, 2 * XATTN_W)), _resident((3, CONV_W)), _resident((1, ATTN_W)),
                  _resident((1, CONV_W)), _resident((1, XATTN_W)), _resident((D_MODEL, D_MODEL)),
                  _resident((1, D_MODEL)), pl.BlockSpec(memory_space=pl.ANY)],
        out_specs=[rows(D_MODEL), rows(ATTN_W), rows(ATTN_W), rows(3 * CONV_W + XATTN_W), acc(n_mem, 2 * XATTN_W),
                   acc(3, CONV_W), acc(1, D_MODEL), acc(1, ATTN_W), acc(1, CONV_W), acc(1, XATTN_W)],
        out_shape=[jax.ShapeDtypeStruct((S, D_MODEL), BF16), jax.ShapeDtypeStruct((S, ATTN_W), F32),
                   jax.ShapeDtypeStruct((S, ATTN_W), F32), jax.ShapeDtypeStruct((S, 3 * CONV_W + XATTN_W), BF16),
                   jax.ShapeDtypeStruct((n_mem, 2 * XATTN_W), F32), jax.ShapeDtypeStruct((3, CONV_W), F32),
                   jax.ShapeDtypeStruct((1, D_MODEL), F32), jax.ShapeDtypeStruct((1, ATTN_W), F32),
                   jax.ShapeDtypeStruct((1, CONV_W), F32), jax.ShapeDtypeStruct((1, XATTN_W), F32)],
        scratch_shapes=[pltpu.VMEM((8, CONV_W), F32)],
        compiler_params=_params("arbitrary"),
    )(dx1, y2, ya, yx, bcu, bcu, qx, mkv, conv_w, g_a, g_c, g_x, w_out, g_post, after)


def _memkv_bwd(mem, g_mem, w_kv, dmkv):
    n_mem = mem.shape[0]
    half = D_MODEL // N_CHIPS // 2

    def body(mem_ref, g_ref, w_ref, d_ref, dw_ref, dg_ref):
        mhat, _ = _rms_hat(mem_ref[...])
        mn = (mhat * g_ref[...]).astype(BF16)
        d = d_ref[...].astype(BF16)
        for k in range(2 * N_CHIPS):
            dw_ref[k % 2, k // 2] = lax.dot_general(mn[:, half * k:half * (k + 1)], d, TN, preferred_element_type=F32)
        dmn = lax.dot_general(d, w_ref[...], NT, preferred_element_type=F32)
        dg_ref[...] = jnp.sum(dmn * mhat, axis=0, keepdims=True)

    return pl.pallas_call(
        body, name="memkv_bwd",
        out_shape=[jax.ShapeDtypeStruct((2, N_CHIPS, half, 2 * XATTN_W), F32), jax.ShapeDtypeStruct((1, D_MODEL), F32)],
        compiler_params=pltpu.CompilerParams(vmem_limit_bytes=VMEM_LIMIT_V7X),
    )(mem, g_mem, w_kv, dmkv)


def _in_proj_bwd(dqkv, tail, cos, sin, w_in, x, g, dx1, after, tm):
    S = x.shape[0]

    def body(dq_ref, dk_ref, dv_ref, tail_ref, cos_ref, sin_ref, w_hbm, x_ref, g_ref, dx1_ref, after_ref,
             dproj_ref, dx_ref, dg_ref, w_full, sems):
        _side_by_side(w_hbm, w_full, sems)

        @pl.when(pl.program_id(0) == 0)
        def _():
            dg_ref[...] = jnp.zeros_like(dg_ref)

        halves = [slice(0, tm // 2), slice(tm // 2, tm)]
        for rows in halves:
            c, s = cos_ref[rows, :], sin_ref[rows, :]
            for j in range(ATTN_W // 128):
                cols = slice(128 * j, 128 * (j + 1))
                dproj_ref[rows, cols] = _rope128(dq_ref[rows, cols] * SCALE, c, s, True).astype(BF16)
                dproj_ref[rows, ATTN_W + 128 * j:ATTN_W + 128 * (j + 1)] = _rope128(dk_ref[rows, cols], c, s, True).astype(BF16)
            dproj_ref[rows, 2 * ATTN_W:3 * ATTN_W] = dv_ref[rows, :].astype(BF16)
            dproj_ref[rows, 3 * ATTN_W:PROJ_W] = tail_ref[rows, :]
        dhs = [lax.dot_general(dproj_ref[rows, :], w_full[...], NT, preferred_element_type=F32) for rows in halves]
        for rows, dh in zip(halves, dhs):
            xhat, r = _rms_hat(x_ref[rows, :])
            dg_ref[...] += jnp.sum(dh * xhat, axis=0, keepdims=True)
            dx_ref[rows, :] = dx1_ref[rows, :] + _rms_bwd(xhat, r, g_ref[...], dh)

    return pl.pallas_call(
        body, name="in_proj_bwd", grid=(S // tm,),
        in_specs=[_rows(tm, ATTN_W)] * 3 + [_rows(tm, PROJ_W - 3 * ATTN_W), _rows(tm, 128), _rows(tm, 128),
                  pl.BlockSpec(memory_space=pl.ANY), _rows(tm, D_MODEL), _resident((1, D_MODEL)),
                  _rows(tm, D_MODEL), pl.BlockSpec(memory_space=pl.ANY)],
        out_specs=[_rows(tm, PROJ_W), _rows(tm, D_MODEL), pl.BlockSpec((1, D_MODEL), lambda i: (0, 0))],
        out_shape=[jax.ShapeDtypeStruct((S, PROJ_W), BF16), jax.ShapeDtypeStruct((S, D_MODEL), F32),
                   jax.ShapeDtypeStruct((1, D_MODEL), F32)],
        scratch_shapes=[pltpu.VMEM((D_MODEL, PROJ_W), BF16), pltpu.SemaphoreType.DMA((N_CHIPS,))],
        compiler_params=_params("arbitrary"),
    )(*dqkv, tail, cos, sin, w_in, x, g, dx1, after)


def _row_tile(rows):
    return ROW_TILE if rows % ROW_TILE == 0 else rows


def _chip_sums_bf16(name, grads, from_sibling, place):
    k = len(grads)
    _, n, rows, _ = grads[0].shape
    tr = _row_tile(rows)

    def body(place_ref, *refs):
        for g_ref, b_ref, o_ref in zip(refs[:k], refs[k:2 * k], refs[2 * k:]):
            o_ref[...] = (g_ref[0] + b_ref[...]).astype(BF16)

    mine = lambda g: pl.BlockSpec((1, 1, tr, g.shape[3]), lambda s, i, p: (p[0], s, i, 0))
    slab = lambda g: pl.BlockSpec((1, tr, g.shape[3]), lambda s, i, p: (s, i, 0))
    return pl.pallas_call(
        body, name=name, out_shape=[jax.ShapeDtypeStruct(g.shape[1:], BF16) for g in grads],
        grid_spec=pltpu.PrefetchScalarGridSpec(
            num_scalar_prefetch=1, grid=(n, rows // tr),
            in_specs=[mine(g) for g in grads] + [slab(g) for g in grads], out_specs=[slab(g) for g in grads]),
        compiler_params=_params("parallel", "parallel"),
    )(place, *grads, *from_sibling)


def _final_sums(name, grads, from_sibling, others, place):
    k = len(grads)
    rows = grads[0].shape[2]
    tr = _row_tile(rows)

    def body(place_ref, *refs):
        for a in range(k):
            own_ref, sib_ref = refs[a], refs[k + a]
            acc = own_ref[0, 0] + sib_ref[0]
            for o in refs[2 * k + 3 * a:2 * k + 3 * a + 3]:
                acc = acc + o[0].astype(F32)
            refs[5 * k + a][0] = acc

    own = lambda g: pl.BlockSpec((1, 1, tr, g.shape[3]), lambda i, p: (p[0], p[1], i, 0))
    sib = lambda g: pl.BlockSpec((1, tr, g.shape[3]), lambda i, p: (p[1], i, 0))
    other = lambda g, j: pl.BlockSpec((1, tr, g.shape[3]), lambda i, p: (j, i, 0))
    return pl.pallas_call(
        body, name=name, out_shape=[jax.ShapeDtypeStruct((2,) + g.shape[2:], F32) for g in grads],
        grid_spec=pltpu.PrefetchScalarGridSpec(
            num_scalar_prefetch=1, grid=(rows // tr,),
            in_specs=[own(g) for g in grads] + [sib(g) for g in grads] + [other(g, j) for g in grads for j in range(3)],
            out_specs=[pl.BlockSpec((1, tr, g.shape[3]), lambda i, p: (p[0], i, 0)) for g in grads]),
        compiler_params=_params("parallel"),
    )(place, *grads, *from_sibling, *[o for o in others for _ in range(3)])


def _adamw_update(w, g, m, v):
    m = ADAM_B1 * m + (1.0 - ADAM_B1) * g
    v = ADAM_B2 * v + (1.0 - ADAM_B2) * (g * g)
    m_hat = m * (1.0 / (1.0 - ADAM_B1 ** ADAM_STEP))
    v_hat = v * (1.0 / (1.0 - ADAM_B2 ** ADAM_STEP))
    return -ADAM_LR * (m_hat / (jnp.sqrt(v_hat) + ADAM_EPS) + ADAM_WD * w), m, v


def _adamw(name, params, after):
    k = len(params)
    rows = params[0][0].shape[0]
    tr = ADAMW_ROW_TILE if rows % ADAMW_ROW_TILE == 0 else rows

    def body(*refs):
        ins, outs = refs[:4 * k], refs[4 * k + 1:]
        for a in range(k):
            w_ref, g_ref, m_ref, v_ref = ins[4 * a:4 * a + 4]
            g = g_ref[...]
            outs[4 * a][...] = g
            outs[4 * a + 1][...], outs[4 * a + 2][...], outs[4 * a + 3][...] = _adamw_update(w_ref[...], g, m_ref[...], v_ref[...])

    spec = lambda w: pl.BlockSpec((tr, w.shape[1]), lambda i: (i, 0))
    out = pl.pallas_call(
        body, name=name, grid=(rows // tr,),
        in_specs=[spec(p[0]) for p in params for _ in range(4)] + [pl.BlockSpec(memory_space=pl.ANY)],
        out_specs=[spec(p[0]) for p in params for _ in range(4)],
        out_shape=[jax.ShapeDtypeStruct(p[0].shape, F32) for p in params for _ in range(4)],
        compiler_params=_params("parallel"),
    )(*[t for p in params for t in p], after)
    return [out[4 * a:4 * a + 4] for a in range(k)]


def _small_update(summed, chip, gains, gains_m, gains_v, taps, taps_m, taps_v):
    n = len(gains)
    widths = [g.shape[1] for g in gains]
    k, w = taps.shape

    def body(*refs):
        chip_ref, sum_ref = refs[0], refs[1]
        params = [refs[2 + 3 * i:5 + 3 * i] for i in range(n + 1)]
        outs = [refs[2 + 3 * (n + 1) + 4 * i:2 + 3 * (n + 1) + 4 * (i + 1)] for i in range(n + 1)]
        loss_ref = refs[-1]
        for i in range(n):
            g = sum_ref[i:i + 1, 0:widths[i]]
            wr, mr, vr = params[i]
            outs[i][0][...] = g
            outs[i][1][...], outs[i][2][...], outs[i][3][...] = _adamw_update(wr[...], g, mr[...], vr[...])
        g = sum_ref[n:n + k, 0:w]
        for j in range(1, N_CHIPS):
            g = jnp.where(chip_ref[0] == j, sum_ref[n:n + k, w * j:w * (j + 1)], g)
        wr, mr, vr = params[n]
        outs[n][0][...] = g
        outs[n][1][...], outs[n][2][...], outs[n][3][...] = _adamw_update(wr[...], g, mr[...], vr[...])
        loss_ref[...] = sum_ref[n + k:n + k + 1, 0:1]

    vmem = pl.BlockSpec(memory_space=pltpu.VMEM)
    operands = [chip, summed]
    for p in zip(list(gains) + [taps], list(gains_m) + [taps_m], list(gains_v) + [taps_v]):
        operands += list(p)
    shapes = [jax.ShapeDtypeStruct(p.shape, F32) for p in list(gains) + [taps] for _ in range(4)]
    out = pl.pallas_call(
        body, name="small_update", out_shape=shapes + [jax.ShapeDtypeStruct((1, 1), F32)],
        in_specs=[pl.BlockSpec(memory_space=pltpu.SMEM)] + [vmem] * (len(operands) - 1),
        out_specs=[vmem] * (len(shapes) + 1),
    )(*operands)
    return [out[4 * i:4 * (i + 1)] for i in range(n + 1)], out[-1]


def _sum_blocks(name, blocks):
    n, rows, cols = blocks.shape

    def body(b_ref, o_ref):
        acc = b_ref[0]
        for k in range(1, n):
            acc = acc + b_ref[k]
        o_ref[...] = acc

    return pl.pallas_call(body, name=name, out_shape=jax.ShapeDtypeStruct((rows, cols), F32))(blocks)


def _place():
    return lax.axis_index("x"), lax.axis_index("y"), lax.axis_index("c")


def _other_chips(x, y):
    return [(1 - x, y), (x, 1 - y), (1 - x, 1 - y)]


def _allgather_finish(name, shards, landed, pass_on):
    n = len(shards)

    def body(*refs):
        ins, outs, stage = refs[:n], refs[2 * n:3 * n], refs[3 * n:4 * n]
        send_sems, recv_sems, local_sems = refs[4 * n:]
        x, y, c = _place()
        chips = _other_chips(x, y)
        barrier = pltpu.get_barrier_semaphore()
        pl.semaphore_signal(barrier, inc=1, device_id=(x, y, 1 - c), device_id_type=MESH)
        pl.semaphore_wait(barrier, 1)

        def copy(a, k, chip, half):
            place = outs[a].at[2 * chip[0] + chip[1], half]
            return pltpu.make_async_remote_copy(
                src_ref=place, dst_ref=place, send_sem=send_sems.at[3 * a + k], recv_sem=recv_sems.at[3 * a + k],
                device_id=(x, y, 1 - c), device_id_type=MESH)

        load = [pltpu.make_async_copy(ins[a], stage[a], local_sems.at[a]) for a in range(n)]
        local = [pltpu.make_async_copy(stage[a], outs[a].at[2 * x + y], local_sems.at[a]) for a in range(n)]
        for cp in load:
            cp.start()
        passed = [copy(a, k, chip, c) for a in range(n) if pass_on[a] for k, chip in enumerate(chips)]
        for cp in passed:
            cp.start()
        for a in range(n):
            load[a].wait()
            local[a].start()
        for a in range(n):
            if pass_on[a]:
                for k, chip in enumerate(chips):
                    copy(a, k, chip, 1 - c).wait_recv()
        for cp in passed:
            cp.wait_send()
        for cp in local:
            cp.wait()

    any_spec = pl.BlockSpec(memory_space=pl.ANY)
    return pl.pallas_call(
        body, name=name,
        out_shape=[jax.ShapeDtypeStruct((N_CHIPS,) + s.shape, s.dtype) for s in shards],
        in_specs=[any_spec] * (2 * n), out_specs=[any_spec] * n,
        input_output_aliases={n + a: a for a in range(n)},
        scratch_shapes=[pltpu.VMEM(s.shape, s.dtype) for s in shards]
        + [pltpu.SemaphoreType.DMA((3 * n,)), pltpu.SemaphoreType.DMA((3 * n,)), pltpu.SemaphoreType.DMA((n,))],
        compiler_params=pltpu.CompilerParams(vmem_limit_bytes=VMEM_LIMIT_V7X, collective_id=HANDSHAKES["sibling"][0]),
    )(*shards, *landed)


def _plan_first_hop(x, y, c, shards, lands):
    return [(shards[a].at[c], lands[a].at[2 * x + y, c], lands[a].at[2 * chip[0] + chip[1], c], (*chip, c))
            for a in range(len(shards)) for chip in _other_chips(x, y)]


def _plan_pass_on(x, y, c, nothing, lands):
    def place(a, chip, half):
        return lands[a].at[2 * chip[0] + chip[1], half]

    return [(place(a, chip, c), place(a, chip, c), place(a, chip, 1 - c), (x, y, 1 - c))
            for a in range(len(lands)) for chip in _other_chips(x, y)]


def _plan_own_half_to_sibling(x, y, c, nothing, lands):
    return [(lands[a].at[c], lands[a].at[c], lands[a].at[1 - c], (x, y, 1 - c)) for a in range(len(lands))]


def _plan_other_half_to_sibling(x, y, c, grads, lands):
    return [(grads[a].at[1 - c], lands[a], lands[a], (x, y, 1 - c)) for a in range(len(grads))]


def _plan_to_other_chips(x, y, c, partials, lands):
    return [(partials[a].at[2 * chip[0] + chip[1]], lands[a].at[k], lands[a].at[k], (*chip, c))
            for a in range(len(partials)) for k, chip in enumerate(_other_chips(x, y))]


def _plan_to_all(x, y, c, blocks, lands):
    flips = [(fx, fy, fc) for fx in (0, 1) for fy in (0, 1) for fc in (0, 1) if (fx, fy, fc) != (0, 0, 0)]
    peers = [(1 - x if fx else x, 1 - y if fy else y, 1 - c if fc else c) for fx, fy, fc in flips]
    return [(blocks[0], lands[0].at[4 * x + 2 * y + c], lands[0].at[4 * p[0] + 2 * p[1] + p[2]], p) for p in peers]


def _planned_copies(plan, srcs, lands, send_sems, recv_sems):
    x, y, c = _place()

    def pair(k, src, there, here, to):
        make = lambda dst: pltpu.make_async_remote_copy(
            src_ref=src, dst_ref=dst, send_sem=send_sems.at[k], recv_sem=recv_sems.at[k], device_id=to, device_id_type=MESH)
        return make(there), make(here)

    return [pair(k, *entry) for k, entry in enumerate(plan(x, y, c, srcs, lands))]


_HBM_SPEC = pl.BlockSpec(memory_space=pltpu.HBM)
_SEM_SPEC = pl.BlockSpec(memory_space=pltpu.SEMAPHORE)


def _hbm(a):
    return pltpu.with_memory_space_constraint(a, pltpu.HBM)


HANDSHAKES = {
    "sibling": (1, lambda x, y, c: [(x, y, 1 - c)]),
}


def _exchange_start(name, plan, n_copies, srcs, land_shapes, after, lands=None, peers=None):
    if lands is None:
        lands = [lax.empty(s.shape, s.dtype) for s in land_shapes]
    land_shapes = lands
    ns, nl = len(srcs), len(land_shapes)
    n_in = ns + nl + 1
    collective_id, peers_of = HANDSHAKES[peers] if peers else (None, None)

    def body(*refs):
        if peers:
            who = peers_of(*_place())
            barrier = pltpu.get_barrier_semaphore()
            for peer in who:
                pl.semaphore_signal(barrier, inc=1, device_id=peer, device_id_type=MESH)
            pl.semaphore_wait(barrier, len(who))
        for send, _ in _planned_copies(plan, refs[:ns], refs[ns:ns + nl], refs[n_in], refs[n_in + 1]):
            send.start()
        refs[-1][...] = jnp.zeros_like(refs[-1])

    out = pl.pallas_call(
        body, name=name,
        out_shape=(pltpu.SemaphoreType.DMA((n_copies,)), pltpu.SemaphoreType.DMA((n_copies,)),
                   *[pltpu.HBM(s.shape, s.dtype) for s in land_shapes], jax.ShapeDtypeStruct((8, 128), F32)),
        in_specs=[_HBM_SPEC] * (ns + nl) + [pl.BlockSpec(memory_space=pl.ANY)],
        out_specs=(_SEM_SPEC, _SEM_SPEC, *[_HBM_SPEC] * nl, pl.BlockSpec(memory_space=pltpu.VMEM)),
        input_output_aliases={ns + i: 2 + i for i in range(nl)},
        compiler_params=pltpu.CompilerParams(has_side_effects=pltpu.SideEffectType.DATAFLOW_SIDE_EFFECTING,
                                             collective_id=collective_id),
    )(*[_hbm(s) for s in srcs], *[_hbm(l) for l in lands], after)
    return out[0], out[1], list(out[2:2 + nl]), out[-1]


def _exchange_wait(name, plan, srcs, started, after):
    send_sems, recv_sems, lands, _ = started
    ns, nl = len(srcs), len(lands)
    after = list(after) if isinstance(after, (list, tuple)) else [after]

    def body(*refs):
        for send, recv in _planned_copies(plan, refs[:ns], refs[ns:ns + nl], refs[ns + nl], refs[ns + nl + 1]):
            send.wait_send()
            recv.wait_recv()

    return pl.pallas_call(
        body, name=name, out_shape=[pltpu.HBM(l.shape, l.dtype) for l in lands],
        in_specs=[_HBM_SPEC] * (ns + nl) + [_SEM_SPEC, _SEM_SPEC] + [pl.BlockSpec(memory_space=pl.ANY)] * len(after),
        out_specs=[_HBM_SPEC] * nl, input_output_aliases={ns + i: i for i in range(nl)},
        compiler_params=pltpu.CompilerParams(has_side_effects=pltpu.SideEffectType.DATAFLOW_SIDE_EFFECTING),
    )(*[_hbm(s) for s in srcs], *lands, send_sems, recv_sems, *after)


def _like(arrays, lead, dtype=None):
    return [jax.ShapeDtypeStruct(tuple(lead) + a.shape[-2:], dtype or a.dtype) for a in arrays]


class _StepExchanges:
    def __init__(self, mats, conv_w):
        x, y, c = _place()
        self.place = jnp.stack([c, 2 * x + y]).astype(jnp.int32)
        shards = [w.astype(BF16).reshape(2, w.shape[0] // 2, w.shape[1]) for w in mats]
        self._in_shard = shards[:1]
        self._in = _exchange_start("w_in_allgather_start", _plan_first_hop, 3, self._in_shard,
                                   _like(self._in_shard, (N_CHIPS, 2)), shards[0])
        self.zero = self._in[3]
        taps = jnp.pad(conv_w, ((0, 8 - conv_w.shape[0]), (0, 128 - conv_w.shape[1])))
        self._rest_shards = shards[1:] + [jnp.stack([taps, jnp.zeros_like(taps)])]
        self._taps_shape = conv_w.shape
        self._groups = {}

    def w_in(self, after):
        landed = _exchange_wait("w_in_allgather_wait", _plan_first_hop, self._in_shard, self._in,
                                list(after) + self._rest_shards)
        (w_in,) = _allgather_finish("w_in_allgather_finish", self._in_shard, landed, [True])
        self._rest = _exchange_start("rest_allgather_start", _plan_first_hop, 3 * len(self._rest_shards),
                                     self._rest_shards, _like(self._rest_shards, (N_CHIPS, 2)), w_in)
        self.zero = self._rest[3]
        return w_in.reshape(N_CHIPS, 2 * w_in.shape[2], w_in.shape[3])

    def rest_weights(self, after):
        landed = _exchange_wait("rest_allgather_wait", _plan_first_hop, self._rest_shards, self._rest, after)
        kv, out, up, down, taps = _allgather_finish("rest_allgather_finish", self._rest_shards, landed,
                                                    [True, True, False, False, True])
        self._up_down = _exchange_start("up_down_pass_on_start", _plan_pass_on, 6, [], None, self.zero, lands=[up, down],
                                        peers="sibling")
        self.zero = self._up_down[3]
        k, w = self._taps_shape
        taps = taps[:, 0, :k, :w].transpose(1, 0, 2).reshape(k, N_CHIPS * w)
        return [g.reshape(N_CHIPS, 2 * g.shape[2], g.shape[3]) for g in (kv, out)], taps

    def up_down(self, after):
        full = _exchange_wait("up_down_pass_on_wait", _plan_pass_on, [], self._up_down, after)
        return [g.reshape(N_CHIPS, 2 * g.shape[2], g.shape[3]) for g in full]

    def send_grads(self, key, grads):
        grads = list(grads)
        started = _exchange_start(f"{key}_grads_to_sibling_start", _plan_other_half_to_sibling, len(grads), grads,
                                  _like(grads, (N_CHIPS,)), self.zero, peers="sibling")
        self._groups[key] = dict(grads=grads, to_sibling=started)
        self.zero = started[3]

    def grads_at_sibling(self, key, after):
        group = self._groups[key]
        grads = group["grads"]
        group["from_sibling"] = _exchange_wait(f"{key}_grads_to_sibling_wait", _plan_other_half_to_sibling, grads,
                                               group["to_sibling"], after)
        group["partials"] = _chip_sums_bf16(f"{key}_chip_sums", grads, group["from_sibling"], self.place)
        group["to_chips"] = _exchange_start(f"{key}_grads_to_chips_start", _plan_to_other_chips, 3 * len(grads),
                                            group["partials"], _like(group["partials"], (3,)), self.zero)
        self.zero = group["to_chips"][3]

    def grads_summed(self, key, after):
        group = self._groups[key]
        from_chips = _exchange_wait(f"{key}_grads_to_chips_wait", _plan_to_other_chips, group["partials"],
                                    group["to_chips"], after)
        return _final_sums(f"{key}_final_sums", group["grads"], group["from_sibling"], from_chips, self.place)

    def send_sums(self, key, sums):
        self._groups[key + "_sums"] = _exchange_start(f"{key}_sums_to_sibling_start", _plan_own_half_to_sibling,
                                                      len(sums), [], None, self.zero, lands=list(sums),
                                                      peers="sibling")
        self.zero = self._groups[key + "_sums"][3]

    def whole_sums(self, key, after):
        full = _exchange_wait(f"{key}_sums_to_sibling_wait", _plan_own_half_to_sibling, [], self._groups[key + "_sums"], after)
        return [t.reshape(2 * t.shape[1], t.shape[2]) for t in full]

    def send_small(self, block):
        self._small = block
        self._small_started = _exchange_start("small_grads_start", _plan_to_all, 7, [block],
                                              [jax.ShapeDtypeStruct((8,) + block.shape, block.dtype)], self.zero)
        self.zero = self._small_started[3]

    def small_summed(self, after):
        x, y, c = _place()
        (landed,) = _exchange_wait("small_grads_wait", _plan_to_all, [self._small], self._small_started, after)
        blocks = lax.dynamic_update_index_in_dim(landed, self._small, 4 * x + 2 * y + c, 0)
        return _sum_blocks("small_sum", blocks)


def _rope_tables(positions):
    half = HEAD // 2
    inv_freq = jnp.float32(ROPE_THETA) ** (-(jnp.arange(half, dtype=F32) * 2.0 / HEAD))
    ang = positions.astype(F32)[:, None] * inv_freq
    cos, sin = jnp.cos(ang), jnp.sin(ang)
    return jnp.tile(cos, (1, 4)), jnp.tile(jnp.concatenate([-sin, sin], axis=1), (1, 2))


def _local_step(x, mem, positions, target, gains, ex):
    g_pre_mix, g_mem, g_a, g_c, g_x, g_post_mix, g_pre_mlp, g_post_mlp = gains
    tm = ROW_TILE
    cos, sin = _rope_tables(positions)
    h = _pre_norm(x, g_pre_mix, ex.zero, tm)
    w_in = ex.w_in([h, cos, sin])

    q, k, v, bcu, qx = _in_proj_fwd(h, w_in, cos, sin, ex.zero, tm)
    ya, lse = _attn_fwd(q, k, v)
    (w_kv, w_out), conv_w = ex.rest_weights(lse)
    w_kv, w_out = (w.reshape(N_CHIPS * w.shape[1], w.shape[2]) for w in (w_kv, w_out))
    memn, mkv = _memkv_fwd(mem, g_mem, w_kv, ex.zero)
    yx, ycat, y2, x1 = _mix_fwd(ya, bcu, qx, mkv, conv_w, g_a, g_c, g_x, w_out, g_post_mix, x, tm)
    w_up, w_down = ex.up_down(x1)
    w_down = w_down.reshape(N_CHIPS * w_down.shape[1], w_down.shape[2])
    h2, f, du, df2, dx1, dg_pre_mlp, dg_post_mlp, loss = _mlp_fwd_bwd(x1, target, g_pre_mlp, g_post_mlp, w_up, w_down,
                                                                      MLP_ROW_TILE)
    gw_down = _weight_grad("grad_w_down", f, df2, True, ex.zero)
    gw_up = _weight_grad("grad_w_up", h2, du, False, ex.zero)
    ex.send_grads("early", [gw_up, gw_down])

    dy2, dya, delta, tail, dmkv, g_conv, dg_post_mix, dg_a, dg_c, dg_x = _mixer_bwd(
        dx1, y2, ya, yx, bcu, qx, mkv, conv_w, g_a, g_c, g_x, w_out, g_post_mix, ex.zero, tm)
    ex.grads_at_sibling("early", dy2)
    gw_out = _weight_grad("grad_w_out", ycat, dy2, True, ex.zero)
    gw_kv, dg_mem = _memkv_bwd(mem, g_mem, w_kv, dmkv)
    ex.send_grads("mid", [gw_out, gw_kv])
    dqkv = _attn_bwd(q, k, v, dya, lse, delta, ex.zero)
    ex.grads_at_sibling("mid", dqkv[0])
    dproj, grad_x, dg_pre_mix = _in_proj_bwd(dqkv, tail, cos, sin, w_in, x, g_pre_mix, dx1, ex.zero, tm)
    gain_grads = [dg_pre_mix, dg_mem, dg_a, dg_c, dg_x, dg_post_mix, dg_pre_mlp, dg_post_mlp]
    ex.send_small(_pack_small(gain_grads, g_conv, loss))
    gw_in = _weight_grad_w_in(h, dproj)
    ex.send_grads("late", [gw_in])
    return grad_x


def _pack_small(gains, conv, scalar=None):
    rows = [jnp.pad(g, ((0, 0), (0, D_MODEL - g.shape[1]))) for g in gains]
    rows.append(jnp.pad(conv, ((0, 0), (0, D_MODEL - conv.shape[1]))))
    last = jnp.zeros((SMALL_ROWS - 8 - conv.shape[0], D_MODEL), F32)
    rows.append(last if scalar is None else last.at[0:1, 0:1].set(scalar))
    return jnp.concatenate(rows, axis=0)


def _unpack_small(block, gain_widths, conv_width):
    gains = [block[i:i + 1, :w] for i, w in enumerate(gain_widths)]
    return gains, block[8:11, :conv_width], block[11, 0]


def kernel(x, mem, positions, g_pre_mix, g_mem, w_in, w_mem_kv, conv_w, g_attn_out, g_conv_out, g_xattn_out, w_out, g_post_mix, g_pre_mlp, w_up, w_down, g_post_mlp, loss_target, m_g_pre_mix, m_g_mem, m_w_in, m_w_mem_kv, m_conv_w, m_g_attn_out, m_g_conv_out, m_g_xattn_out, m_w_out, m_g_post_mix, m_g_pre_mlp, m_w_up, m_w_down, m_g_post_mlp, v_g_pre_mix, v_g_mem, v_w_in, v_w_mem_kv, v_conv_w, v_g_attn_out, v_g_conv_out, v_g_xattn_out, v_w_out, v_g_post_mix, v_g_pre_mlp, v_w_up, v_w_down, v_g_post_mlp):
    cx, cy, cc = _place()
    chip = 2 * cx + cy
    gains = [g_pre_mix, g_mem, g_attn_out, g_conv_out, g_xattn_out, g_post_mix, g_pre_mlp, g_post_mlp]
    gains_m = [m_g_pre_mix, m_g_mem, m_g_attn_out, m_g_conv_out, m_g_xattn_out, m_g_post_mix, m_g_pre_mlp, m_g_post_mlp]
    gains_v = [v_g_pre_mix, v_g_mem, v_g_attn_out, v_g_conv_out, v_g_xattn_out, v_g_post_mix, v_g_pre_mlp, v_g_post_mlp]
    gain_widths = [g.shape[1] for g in gains]
    mats = [w_in[0], w_mem_kv[0], w_out[0], w_up[0], w_down[0]]
    mats_m = [m_w_in[0], m_w_mem_kv[0], m_w_out[0], m_w_up[0], m_w_down[0]]
    mats_v = [v_w_in[0], v_w_mem_kv[0], v_w_out[0], v_w_up[0], v_w_down[0]]

    ex = _StepExchanges(mats, conv_w[0])
    grad_x = _local_step(x[0], mem[0], positions[0], loss_target[0], gains, ex)

    ex.send_sums("four", ex.grads_summed("early", ex.zero) + ex.grads_summed("mid", ex.zero))
    ex.grads_at_sibling("late", ex.zero)
    up_sum, down_sum, out_sum, kv_sum = ex.whole_sums("four", ex.zero)
    params = lambda a, g: (mats[a], g, mats_m[a], mats_v[a])
    new_up, new_down = _adamw("adamw_up_down", [params(3, up_sum), params(4, down_sum)], ex.zero)
    new_out, new_kv = _adamw("adamw_out_kv", [params(2, out_sum), params(1, kv_sum)], ex.zero)

    small, total = _small_update(ex.small_summed(new_kv[1]), chip.reshape(1).astype(jnp.int32), gains, gains_m,
                                 gains_v, conv_w[0], m_conv_w[0], v_conv_w[0])

    ex.send_sums("last", ex.grads_summed("late", small[0][1]))
    (in_sum,) = ex.whole_sums("last", ex.zero)
    (new_in,) = _adamw("adamw_in", [params(0, in_sum)], in_sum)
    mat_new = [new_in, new_kv, new_out, new_up, new_down]

    order = ["g_pre_mix", "g_mem", "w_in", "w_mem_kv", "conv_w", "g_attn_out", "g_conv_out", "g_xattn_out", "w_out",
             "g_post_mix", "g_pre_mlp", "w_up", "w_down", "g_post_mlp"]
    gain_names = ["g_pre_mix", "g_mem", "g_attn_out", "g_conv_out", "g_xattn_out", "g_post_mix", "g_pre_mlp", "g_post_mlp"]
    mat_names = ["w_in", "w_mem_kv", "w_out", "w_up", "w_down"]

    def leaf(kind, name):
        if name in gain_names:
            return small[gain_names.index(name)][kind]
        if name == "conv_w":
            return small[len(gain_names)][kind][None]
        return mat_new[mat_names.index(name)][kind][None]

    return (total[0, 0], grad_x[None], *[leaf(kind, name) for kind in range(4) for name in order])
```

```python
import jax
import jax.numpy as jnp
from jax import lax
from jax.experimental import pallas as pl
from jax.experimental.pallas import tpu as pltpu

F32, BF16 = jnp.float32, jnp.bfloat16

D_MODEL = 1024
ATTN_W = 512
CONV_W = 256
XATTN_W = 256
PROJ_W = 3 * ATTN_W + 3 * CONV_W + XATTN_W
D_FF = 4096
HEAD = 64
N_BACK = 128
DILATIONS = (1, 4, 16)
PATTERN_ORDER = DILATIONS[::-1]
ROPE_THETA = 10000.0
EPS = 1e-6
NEG_INF = -1e30
SCALE = HEAD ** -0.5
N_CHIPS = 4
SHARD_IN = PROJ_W // N_CHIPS
SHARD_FF = D_FF // N_CHIPS

ADAM_LR, ADAM_B1, ADAM_B2, ADAM_EPS, ADAM_WD, ADAM_STEP = 0.001, 0.9, 0.999, 1e-08, 0.01, 10

VMEM_LIMIT_V7X = 56 * 1024 * 1024
ROW_TILE = 512
MLP_ROW_TILE = 256
ADAMW_ROW_TILE = 256
SMALL_ROWS = 16

NT = (((1,), (1,)), ((), ()))
TN = (((0,), (0,)), ((), ()))
MESH = pl.DeviceIdType.MESH


def _params(*sem):
    return pltpu.CompilerParams(dimension_semantics=sem, vmem_limit_bytes=VMEM_LIMIT_V7X)


def _resident(shape):
    return pl.BlockSpec(shape, lambda *_: (0,) * len(shape), pipeline_mode=pl.Buffered(1))


def _rows(tm, width):
    return pl.BlockSpec((tm, width), lambda i: (i, 0))


def _rms_hat(x):
    r = lax.rsqrt(jnp.mean(x * x, axis=-1, keepdims=True) + EPS)
    return x * r, r


def _rms_bwd(xhat, r, g, dy):
    gdy = dy * g
    return r * (gdy - xhat * jnp.mean(xhat * gdy, axis=-1, keepdims=True))


def _rope128(t, cos, sin_signed, inverse):
    lane = lax.broadcasted_iota(jnp.int32, t.shape, 1)
    first_half = (lane % HEAD) < (HEAD // 2)
    rot = jnp.where(first_half, pltpu.roll(t, 128 - HEAD // 2, 1), pltpu.roll(t, HEAD // 2, 1))
    return t * cos - rot * sin_signed if inverse else t * cos + rot * sin_signed


def _pre_norm(x, g, after, tm):
    S = x.shape[0]

    def body(x_ref, g_ref, after_ref, h_ref):
        h_ref[...] = (_rms_hat(x_ref[...])[0] * g_ref[...]).astype(BF16)

    return pl.pallas_call(
        body, name="pre_norm", grid=(S // tm,),
        in_specs=[_rows(tm, D_MODEL), _resident((1, D_MODEL)), pl.BlockSpec(memory_space=pl.ANY)],
        out_specs=_rows(tm, D_MODEL), out_shape=jax.ShapeDtypeStruct((S, D_MODEL), BF16),
        compiler_params=_params("parallel"),
    )(x, g, after)


def _side_by_side(w_hbm, w_full, sems):
    @pl.when(pl.program_id(0) == 0)
    def _():
        copies = [pltpu.make_async_copy(w_hbm.at[j], w_full.at[:, pl.ds(SHARD_IN * j, SHARD_IN)], sems.at[j])
                  for j in range(N_CHIPS)]
        for cp in copies:
            cp.start()
        for cp in copies:
            cp.wait()


def _in_proj_fwd(h, w_in, cos, sin, after, tm):
    S = h.shape[0]

    def body(h_ref, w_hbm, cos_ref, sin_ref, after_ref, q_ref, k_ref, v_ref, bcu_ref, qx_ref, proj, w_full, sems):
        _side_by_side(w_hbm, w_full, sems)
        proj[...] = jnp.dot(h_ref[...], w_full[...], preferred_element_type=F32)
        c, s = cos_ref[...], sin_ref[...]
        for j in range(ATTN_W // 128):
            lo = 128 * j
            q_ref[:, lo:lo + 128] = _rope128(proj[:, lo:lo + 128], c, s, False) * SCALE
            k_ref[:, lo:lo + 128] = _rope128(proj[:, ATTN_W + lo:ATTN_W + lo + 128], c, s, False)
        v_ref[...] = proj[:, 2 * ATTN_W:3 * ATTN_W]
        bcu_ref[...] = proj[:, 3 * ATTN_W:3 * ATTN_W + 3 * CONV_W]
        qx_ref[...] = proj[:, 3 * ATTN_W + 3 * CONV_W:PROJ_W].astype(BF16)

    return pl.pallas_call(
        body, name="in_proj_fwd", grid=(S // tm,),
        in_specs=[_rows(tm, D_MODEL), pl.BlockSpec(memory_space=pl.ANY), _rows(tm, 128), _rows(tm, 128),
                  pl.BlockSpec(memory_space=pl.ANY)],
        out_specs=[_rows(tm, ATTN_W), _rows(tm, ATTN_W), _rows(tm, ATTN_W), _rows(tm, 3 * CONV_W), _rows(tm, XATTN_W)],
        out_shape=[jax.ShapeDtypeStruct((S, ATTN_W), F32), jax.ShapeDtypeStruct((S, ATTN_W), F32),
                   jax.ShapeDtypeStruct((S, ATTN_W), F32), jax.ShapeDtypeStruct((S, 3 * CONV_W), F32),
                   jax.ShapeDtypeStruct((S, XATTN_W), BF16)],
        scratch_shapes=[pltpu.VMEM((tm, PROJ_W), F32), pltpu.VMEM((D_MODEL, PROJ_W), BF16),
                        pltpu.SemaphoreType.DMA((N_CHIPS,))],
        compiler_params=_params("arbitrary"),
    )(h, w_in, cos, sin, after)


def _memkv_fwd(mem, g_mem, w_kv, after):
    n_mem = mem.shape[0]

    def body(mem_ref, g_ref, w_ref, after_ref, mn_ref, kv_ref):
        mhat, _ = _rms_hat(mem_ref[...])
        mn = (mhat * g_ref[...]).astype(BF16)
        mn_ref[...] = mn
        kv_ref[...] = jnp.dot(mn, w_ref[...], preferred_element_type=F32).astype(BF16)

    vmem = pl.BlockSpec(memory_space=pltpu.VMEM)
    return pl.pallas_call(
        body, name="memkv_fwd", in_specs=[vmem, vmem, vmem, pl.BlockSpec(memory_space=pl.ANY)], out_specs=[vmem, vmem],
        out_shape=[jax.ShapeDtypeStruct((n_mem, D_MODEL), BF16), jax.ShapeDtypeStruct((n_mem, 2 * XATTN_W), BF16)],
        compiler_params=pltpu.CompilerParams(vmem_limit_bytes=VMEM_LIMIT_V7X),
    )(mem, g_mem, w_kv, after)


def _fill_band_bias(bias):
    row = lax.broadcasted_iota(jnp.int32, (N_BACK, 2 * N_BACK), 0)
    col = lax.broadcasted_iota(jnp.int32, (N_BACK, 2 * N_BACK), 1)
    band = (col >= row) & (col <= row + N_BACK)
    bias[1] = jnp.where(band, 0.0, NEG_INF)
    bias[0] = jnp.where(band & (col >= N_BACK), 0.0, NEG_INF)


def _strided(start, size, d):
    return pl.ds(start, size) if d == 1 else pl.ds(start, size, stride=d)


def _group_starts(g, G, nb, d):
    t0 = g * G
    r, n0 = lax.shift_right_logical(t0, nb.bit_length() - 1), lax.bitwise_and(t0, nb - 1)
    first = r + n0 * (N_BACK * d)
    before = r + jnp.maximum(n0 - 1, 0) * (N_BACK * d)
    starts = [before] + [first + u * (N_BACK * d) for u in range(G)]
    if d == 1:
        starts = [pl.multiple_of(st, N_BACK) for st in starts]
    return starts, n0


def _step_blocks(i, U, nb, d):
    G = min(U, nb)
    whole = G == nb
    row_blocks, blocks = [], []
    for grp in range(U // G):
        starts, n0 = _group_starts(i * (U // G) + grp, G, nb, d)
        base = len(row_blocks)
        if whole:
            row_blocks += [_strided(st, N_BACK, d) for st in starts[1:]]
            blocks += [(base + max(u - 1, 0), base + u, min(u, 1)) for u in range(G)]
        else:
            row_blocks += [_strided(st, N_BACK, d) for st in starts]
            blocks += [(base + u, base + u + 1, jnp.minimum(n0, 1) if u == 0 else 1) for u in range(G)]
    return row_blocks, blocks


def _by_head(a, b):
    lane = lax.broadcasted_iota(jnp.int32, (a.shape[0], 2 * HEAD), 1)
    return jnp.where(lane < HEAD, a, b)


def _head_only(t, hh):
    lane = lax.broadcasted_iota(jnp.int32, t.shape, 1)
    return jnp.where((lane < HEAD) == (hh == 0), t, jnp.zeros_like(t))


def _stack_heads(t):
    return jnp.concatenate([_head_only(t, 0), _head_only(t, 1)], axis=0)


def _head_columns(t):
    return jnp.concatenate([t[:, 0:1], t[:, HEAD:HEAD + 1]], axis=0)


def _unstack(t):
    return _by_head(t[:N_BACK], t[N_BACK:])


def _unstack_columns(t):
    return _by_head(jnp.broadcast_to(t[:N_BACK], (N_BACK, 2 * HEAD)), jnp.broadcast_to(t[N_BACK:], (N_BACK, 2 * HEAD)))


FWD_BLOCKS_PER_STEP = 4
BWD_BLOCKS_PER_STEP = 4
BWD_CHUNK = 64


def _attn_fwd(q, k, v):
    S = q.shape[0]
    U = FWD_BLOCKS_PER_STEP

    def body(q_ref, k_ref, v_ref, y_ref, m_ref, l_scr, bias):
        _fill_band_bias(bias)
        for g, d in enumerate(PATTERN_ORDER):
            nb = S // d // N_BACK
            first_pattern, last_pattern = g == 0, g == len(PATTERN_ORDER) - 1

            def step(i, carry, d=d, nb=nb, first_pattern=first_pattern, last_pattern=last_pattern):
                row_blocks, blocks = _step_blocks(i, U, nb, d)
                kb = [k_ref[r, :].astype(BF16) for r in row_blocks]
                ss = []
                for before, own, which in blocks:
                    kw = jnp.concatenate([kb[before], kb[own]], 0)
                    qs = _stack_heads(q_ref[row_blocks[own], :].astype(BF16))
                    b = bias[which]
                    ss.append(lax.dot_general(qs, kw, NT, preferred_element_type=F32) + jnp.concatenate([b, b], axis=0))
                ms = [jnp.max(s, axis=1, keepdims=True) for s in ss]
                ps = [jnp.exp(s - m) for s, m in zip(ss, ms)]
                ls = [jnp.sum(p, axis=1, keepdims=True) for p in ps]
                vb = [v_ref[r, :].astype(BF16) for r in row_blocks]
                os_ = [jnp.dot(ps[u].astype(BF16), jnp.concatenate([vb[before], vb[own]], 0), preferred_element_type=F32)
                       for u, (before, own, _) in enumerate(blocks)]
                for u, (_, own, _) in enumerate(blocks):
                    o_g, m_g, l_g = _unstack(os_[u]), _unstack_columns(ms[u]), _unstack_columns(ls[u])
                    r = row_blocks[own]
                    if first_pattern:
                        m_new, l_new, acc = m_g, l_g, o_g
                    else:
                        m_old = m_ref[r, :]
                        m_new = jnp.maximum(m_old, m_g)
                        alpha, beta = jnp.exp(m_old - m_new), jnp.exp(m_g - m_new)
                        l_new = l_scr[r, :] * alpha + l_g * beta
                        acc = y_ref[r, :] * alpha + o_g * beta
                    if last_pattern:
                        y_ref[r, :] = acc / l_new
                        m_ref[r, :] = m_new + jnp.log(l_new)
                    else:
                        y_ref[r, :] = acc
                        m_ref[r, :] = m_new
                        l_scr[r, :] = l_new
                return carry

            lax.fori_loop(0, d * nb // U, step, 0)

    col = pl.BlockSpec((S, 2 * HEAD), lambda j: (0, j))
    return pl.pallas_call(
        body, name="attn_fwd", grid=(q.shape[1] // (2 * HEAD),),
        in_specs=[col, col, col], out_specs=[col, col],
        out_shape=[jax.ShapeDtypeStruct(q.shape, F32)] * 2,
        scratch_shapes=[pltpu.VMEM((S, 2 * HEAD), F32), pltpu.VMEM((2, N_BACK, 2 * N_BACK), F32)],
        compiler_params=_params("parallel"),
    )(q, k, v)


def _attn_bwd(q, k, v, dy, lse, delta, after):
    S = q.shape[0]
    U = BWD_BLOCKS_PER_STEP

    def body(q_ref, k_ref, v_ref, dy_ref, lse_ref, delta_ref, after_ref, dq_ref, dk_ref, dv_ref, bias):
        _fill_band_bias(bias)
        nb_first = S // PATTERN_ORDER[0] // N_BACK
        first_writes_all = min(U, nb_first) == nb_first
        if not first_writes_all:
            dk_ref[...] = jnp.zeros_like(dk_ref)
            dv_ref[...] = jnp.zeros_like(dv_ref)
        for g, d in enumerate(PATTERN_ORDER):
            nb = S // d // N_BACK

            def step(i, carry, d=d, nb=nb, g=g):
                row_blocks, blocks = _step_blocks(i, U, nb, d)
                kb = [k_ref[r, :].astype(BF16) for r in row_blocks]
                vb = [v_ref[r, :].astype(BF16) for r in row_blocks]
                kws = [jnp.concatenate([kb[before], kb[own]], 0) for before, own, _ in blocks]
                vws = [jnp.concatenate([vb[before], vb[own]], 0) for before, own, _ in blocks]
                qss = [_stack_heads(q_ref[row_blocks[own], :].astype(BF16)) for _, own, _ in blocks]
                doss = [_stack_heads(dy_ref[row_blocks[own], :].astype(BF16)) for _, own, _ in blocks]
                ss = [lax.dot_general(qss[u], kws[u], NT, preferred_element_type=F32) for u in range(U)]
                dps = [lax.dot_general(doss[u], vws[u], NT, preferred_element_type=F32) for u in range(U)]
                pbs, dss = [], []
                for u, (_, own, which) in enumerate(blocks):
                    lse_c = _head_columns(lse_ref[row_blocks[own], :])
                    delta_c = _head_columns(delta_ref[row_blocks[own], :])
                    p_parts, ds_parts = [], []
                    for r0 in range(0, 2 * N_BACK, BWD_CHUNK):
                        r = slice(r0, r0 + BWD_CHUNK)
                        mask = bias[which, r0 % N_BACK:r0 % N_BACK + BWD_CHUNK, :]
                        p_r = jnp.exp(ss[u][r] + mask - lse_c[r])
                        p_parts.append(p_r.astype(BF16))
                        ds_parts.append((p_r * (dps[u][r] - delta_c[r])).astype(BF16))
                    pbs.append(jnp.concatenate(p_parts, axis=0))
                    dss.append(jnp.concatenate(ds_parts, axis=0))
                dqs = [jnp.dot(dss[u], kws[u], preferred_element_type=F32) for u in range(U)]
                dkws = [lax.dot_general(dss[u], qss[u], TN, preferred_element_type=F32) for u in range(U)]
                dvws = [lax.dot_general(pbs[u], doss[u], TN, preferred_element_type=F32) for u in range(U)]
                dk_parts, dv_parts = [None] * len(row_blocks), [None] * len(row_blocks)
                for u, (before, own, _) in enumerate(blocks):
                    dq = _unstack(dqs[u])
                    if g == 0:
                        dq_ref[row_blocks[own], :] = dq
                    else:
                        dq_ref[row_blocks[own], :] += dq
                    for idx, dkp, dvp in ((before, dkws[u][:N_BACK], dvws[u][:N_BACK]),
                                          (own, dkws[u][N_BACK:], dvws[u][N_BACK:])):
                        dk_parts[idx] = dkp if dk_parts[idx] is None else dk_parts[idx] + dkp
                        dv_parts[idx] = dvp if dv_parts[idx] is None else dv_parts[idx] + dvp
                for idx, r in enumerate(row_blocks):
                    if g == 0 and first_writes_all:
                        dk_ref[r, :] = dk_parts[idx]
                        dv_ref[r, :] = dv_parts[idx]
                    else:
                        dk_ref[r, :] += dk_parts[idx]
                        dv_ref[r, :] += dv_parts[idx]
                return carry

            lax.fori_loop(0, d * nb // U, step, 0)

    col = pl.BlockSpec((S, 2 * HEAD), lambda j: (0, j))
    return pl.pallas_call(
        body, name="attn_bwd", grid=(q.shape[1] // (2 * HEAD),),
        in_specs=[col] * 6 + [pl.BlockSpec(memory_space=pl.ANY)], out_specs=[col] * 3,
        out_shape=[jax.ShapeDtypeStruct(q.shape, F32)] * 3,
        scratch_shapes=[pltpu.VMEM((2, N_BACK, 2 * N_BACK), F32)],
        compiler_params=_params("parallel"),
    )(q, k, v, dy, lse, delta, after)


def _shift_down(z, before, k):
    row = lax.broadcasted_iota(jnp.int32, z.shape, 0)
    out = pltpu.roll(z, k, 0)
    for i in range(k):
        out = jnp.where(row == i, before[8 - k + i:8 - k + i + 1, :], out)
    return out


def _shift_up(z, after, k):
    rows = z.shape[0]
    row = lax.broadcasted_iota(jnp.int32, z.shape, 0)
    out = pltpu.roll(z, rows - k, 0)
    for i in range(k):
        out = jnp.where(row == rows - k + i, after[i:i + 1, :], out)
    return out


def _conv_fwd(bcu, before, is_first, w):
    b, c, u = bcu[:, 0:CONV_W], bcu[:, CONV_W:2 * CONV_W], bcu[:, 2 * CONV_W:3 * CONV_W]
    z = c * u
    zb = jnp.where(is_first, 0.0, before[:, CONV_W:2 * CONV_W] * before[:, 2 * CONV_W:3 * CONV_W])
    z1, z2 = _shift_down(z, zb, 1), _shift_down(z, zb, 2)
    cv = w[0:1, :] * z2 + w[1:2, :] * z1 + w[2:3, :] * z
    return b, c, u, z, z1, z2, cv


def _halo_before(tm, width):
    return pl.BlockSpec((8, width), lambda i: (jnp.maximum(i * (tm // 8) - 1, 0), 0))


def _mix_fwd(ya, bcu, qx, mkv, conv_w, g_a, g_c, g_x, w_out, g_post, x, tm):
    S = x.shape[0]

    def body(ya_ref, bcu_ref, before_ref, qx_ref, mkv_ref, cw_ref, ga_ref, gc_ref, gx_ref,
             wo_ref, gp_ref, x_ref, yx_ref, ycat_ref, y2_ref, x1_ref):
        ya = ya_ref[...]
        b, _, _, _, _, _, cv = _conv_fwd(bcu_ref[...], before_ref[...], pl.program_id(0) == 0, cw_ref[...])
        yc = b * cv

        qxb, mkvb = qx_ref[...], mkv_ref[...]
        heads = [slice(HEAD * hd, HEAD * (hd + 1)) for hd in range(XATTN_W // HEAD)]
        ss = [lax.dot_general(qxb[:, sl], mkvb[:, sl], NT, preferred_element_type=F32) * SCALE for sl in heads]
        ms = [jnp.max(s, axis=1, keepdims=True) for s in ss]
        ps = [jnp.exp(s - m) for s, m in zip(ss, ms)]
        ls = [jnp.sum(p, axis=1, keepdims=True) for p in ps]
        os_ = [jnp.dot(p.astype(BF16), mkvb[:, XATTN_W + sl.start:XATTN_W + sl.stop], preferred_element_type=F32)
               for p, sl in zip(ps, heads)]
        for sl, o, l in zip(heads, os_, ls):
            yx_ref[:, sl] = o / l
        yx = yx_ref[...]

        ycat_ref[:, 0:ATTN_W] = (_rms_hat(ya)[0] * ga_ref[...]).astype(BF16)
        ycat_ref[:, ATTN_W:ATTN_W + CONV_W] = (_rms_hat(yc)[0] * gc_ref[...]).astype(BF16)
        ycat_ref[:, ATTN_W + CONV_W:D_MODEL] = (_rms_hat(yx)[0] * gx_ref[...]).astype(BF16)
        y2 = jnp.dot(ycat_ref[...], wo_ref[...], preferred_element_type=F32)
        y2_ref[...] = y2
        x1_ref[...] = x_ref[...] + _rms_hat(y2)[0] * gp_ref[...]

    n_mem = mkv.shape[0]
    return pl.pallas_call(
        body, name="mix_fwd", grid=(S // tm,),
        in_specs=[_rows(tm, ATTN_W), _rows(tm, 3 * CONV_W), _halo_before(tm, 3 * CONV_W), _rows(tm, XATTN_W),
                  _resident((n_mem, 2 * XATTN_W)), _resident((3, CONV_W)), _resident((1, ATTN_W)),
                  _resident((1, CONV_W)), _resident((1, XATTN_W)), _resident((D_MODEL, D_MODEL)),
                  _resident((1, D_MODEL)), _rows(tm, D_MODEL)],
        out_specs=[_rows(tm, XATTN_W), _rows(tm, D_MODEL), _rows(tm, D_MODEL), _rows(tm, D_MODEL)],
        out_shape=[jax.ShapeDtypeStruct((S, XATTN_W), F32), jax.ShapeDtypeStruct((S, D_MODEL), BF16),
                   jax.ShapeDtypeStruct((S, D_MODEL), F32), jax.ShapeDtypeStruct((S, D_MODEL), F32)],
        compiler_params=_params("parallel"),
    )(ya, bcu, bcu, qx, mkv, conv_w, g_a, g_c, g_x, w_out, g_post, x)


def _mlp_fwd_bwd(x1, target, g_pre, g_post, w_up, w_down, tm):
    S = x1.shape[0]
    n_ff = D_FF // SHARD_FF

    def body(x1_ref, t_ref, gpre_ref, gpost_ref, wup_ref, wdn_ref,
             h2_ref, f_ref, du_ref, df2_ref, dx1_ref, dgpre_ref, dgpost_ref, loss_ref, u_scr):
        @pl.when(pl.program_id(0) == 0)
        def _():
            dgpre_ref[...] = jnp.zeros_like(dgpre_ref)
            dgpost_ref[...] = jnp.zeros_like(dgpost_ref)
            loss_ref[...] = jnp.zeros_like(loss_ref)

        x1 = x1_ref[...]
        x1hat, r1 = _rms_hat(x1)
        h2 = (x1hat * gpre_ref[...]).astype(BF16)
        h2_ref[...] = h2
        f2 = jnp.zeros((tm, D_MODEL), F32)
        for j in range(n_ff):
            cols = slice(SHARD_FF * j, SHARD_FF * (j + 1))
            u = jnp.maximum(jnp.dot(h2, wup_ref[j], preferred_element_type=F32), 0.0)
            u_scr[:, cols] = u
            f = (u * u).astype(BF16)
            f_ref[:, cols] = f
            f2 = f2 + jnp.dot(f, wdn_ref[cols, :], preferred_element_type=F32)
        f2hat, r2 = _rms_hat(f2)
        err = x1 + f2hat * gpost_ref[...] - t_ref[...]
        loss_ref[...] += 0.5 * jnp.sum(jnp.mean(err * err, axis=-1, keepdims=True), axis=0, keepdims=True)
        dx2 = err * (1.0 / D_MODEL)
        dgpost_ref[...] += jnp.sum(dx2 * f2hat, axis=0, keepdims=True)
        df2 = _rms_bwd(f2hat, r2, gpost_ref[...], dx2).astype(BF16)
        df2_ref[...] = df2
        dh2 = jnp.zeros((tm, D_MODEL), F32)
        for j in range(n_ff):
            cols = slice(SHARD_FF * j, SHARD_FF * (j + 1))
            df = lax.dot_general(df2, wdn_ref[cols, :], NT, preferred_element_type=F32)
            du = (2.0 * u_scr[:, cols] * df).astype(BF16)
            du_ref[:, cols] = du
            dh2 = dh2 + lax.dot_general(du, wup_ref[j], NT, preferred_element_type=F32)
        dgpre_ref[...] += jnp.sum(dh2 * x1hat, axis=0, keepdims=True)
        dx1_ref[...] = dx2 + _rms_bwd(x1hat, r1, gpre_ref[...], dh2)

    acc = pl.BlockSpec((1, D_MODEL), lambda i: (0, 0))
    return pl.pallas_call(
        body, name="mlp_fwd_bwd", grid=(S // tm,),
        in_specs=[_rows(tm, D_MODEL), _rows(tm, D_MODEL), _resident((1, D_MODEL)), _resident((1, D_MODEL)),
                  _resident((n_ff, D_MODEL, SHARD_FF)), _resident((D_FF, D_MODEL))],
        out_specs=[_rows(tm, D_MODEL), _rows(tm, D_FF), _rows(tm, D_FF), _rows(tm, D_MODEL), _rows(tm, D_MODEL),
                   acc, acc, pl.BlockSpec((1, 1), lambda i: (0, 0))],
        out_shape=[jax.ShapeDtypeStruct((S, D_MODEL), BF16), jax.ShapeDtypeStruct((S, D_FF), BF16),
                   jax.ShapeDtypeStruct((S, D_FF), BF16), jax.ShapeDtypeStruct((S, D_MODEL), BF16),
                   jax.ShapeDtypeStruct((S, D_MODEL), F32), jax.ShapeDtypeStruct((1, D_MODEL), F32),
                   jax.ShapeDtypeStruct((1, D_MODEL), F32), jax.ShapeDtypeStruct((1, 1), F32)],
        scratch_shapes=[pltpu.VMEM((tm, D_FF), F32)],
        compiler_params=_params("arbitrary"),
    )(x1, target, g_pre, g_post, w_up, w_down)


def _weight_grad(name, a, b, rows_sharded, after):
    S, K = a.shape
    N = b.shape[1]
    if rows_sharded:
        tk, tn = K // N_CHIPS, N
        a_spec = pl.BlockSpec((S, tk), lambda j: (0, j))
        b_spec = pl.BlockSpec((S, tn), lambda j: (0, 0), pipeline_mode=pl.Buffered(1))
    else:
        tk, tn = K, N // N_CHIPS
        a_spec = pl.BlockSpec((S, tk), lambda j: (0, 0), pipeline_mode=pl.Buffered(1))
        b_spec = pl.BlockSpec((S, tn), lambda j: (0, j))
    half = tk // 2

    def body(a_ref, b_ref, after_ref, o_ref):
        res = lax.dot_general(a_ref[...], b_ref[...], TN, preferred_element_type=F32)
        o_ref[0, 0] = res[:half]
        o_ref[1, 0] = res[half:]

    return pl.pallas_call(
        body, name=name, grid=(N_CHIPS,), in_specs=[a_spec, b_spec, pl.BlockSpec(memory_space=pl.ANY)],
        out_specs=pl.BlockSpec((2, 1, half, tn), lambda j: (0, j, 0, 0)),
        out_shape=jax.ShapeDtypeStruct((2, N_CHIPS, half, tn), F32),
        compiler_params=_params("parallel"),
    )(a, b, after)


def _weight_grad_w_in(h, dproj):
    S, K = h.shape
    step_w = 2 * 256
    n_steps = PROJ_W // step_w
    half = K // 2

    def body(a_ref, b_ref, o_ref):
        res = lax.dot_general(a_ref[...], b_ref[...], TN, preferred_element_type=F32)
        for step in range(n_steps):
            @pl.when(pl.program_id(0) == step)
            def _(step=step):
                lo = step * step_w
                while lo < (step + 1) * step_w:
                    chip = lo // SHARD_IN
                    hi = min((step + 1) * step_w, (chip + 1) * SHARD_IN)
                    for hh in range(2):
                        o_ref[hh, chip, :, lo - chip * SHARD_IN:hi - chip * SHARD_IN] = (
                            res[half * hh:half * (hh + 1), lo - step * step_w:hi - step * step_w])
                    lo = hi

    return pl.pallas_call(
        body, name="grad_w_in", grid=(n_steps,),
        in_specs=[pl.BlockSpec((S, K), lambda j: (0, 0), pipeline_mode=pl.Buffered(1)),
                  pl.BlockSpec((S, step_w), lambda j: (0, j))],
        out_specs=pl.BlockSpec((2, N_CHIPS, half, SHARD_IN), lambda j: (0, 0, 0, 0)),
        out_shape=jax.ShapeDtypeStruct((2, N_CHIPS, half, SHARD_IN), F32),
        compiler_params=_params("arbitrary"),
    )(h, dproj)


def _mixer_bwd(dx1, y2, ya, yx, bcu, qx, mkv, conv_w, g_a, g_c, g_x, w_out, g_post, after, tm):
    S = dx1.shape[0]
    n_mem = mkv.shape[0]
    n_tiles = S // tm

    def body(dx1_ref, y2_ref, ya_ref, yx_ref, bcu_ref, before_ref, qx_ref, mkv_ref, cw_ref, ga_ref, gc_ref, gx_ref,
             wo_ref, gp_ref, after_ref, dy2_ref, dya_ref, delta_ref, tail_ref, dmkv_ref, dcw_ref, dgp_ref, dga_ref,
             dgc_ref, dgx_ref, carry):
        step = pl.program_id(0)
        first_tile = step == n_tiles - 1

        @pl.when(step == 0)
        def _():
            for ref in (dmkv_ref, dcw_ref, dgp_ref, dga_ref, dgc_ref, dgx_ref, carry):
                ref[...] = jnp.zeros_like(ref)

        dx1 = dx1_ref[...]
        y2hat, r2 = _rms_hat(y2_ref[...])
        dgp_ref[...] += jnp.sum(dx1 * y2hat, axis=0, keepdims=True)
        dy2 = _rms_bwd(y2hat, r2, gp_ref[...], dx1).astype(BF16)
        dy2_ref[...] = dy2
        dycat = lax.dot_general(dy2, wo_ref[...], NT, preferred_element_type=F32)

        d_na = dycat[:, 0:ATTN_W]
        ya = ya_ref[...]
        yahat, ra = _rms_hat(ya)
        dga_ref[...] += jnp.sum(d_na * yahat, axis=0, keepdims=True)
        dya = _rms_bwd(yahat, ra, ga_ref[...], d_na)
        dya_ref[...] = dya
        prod = dya * ya
        hi = prod.astype(BF16)
        lo = (prod - hi.astype(F32)).astype(BF16)
        head_of = lambda axis: lax.shift_right_logical(lax.broadcasted_iota(jnp.int32, (ATTN_W, ATTN_W), axis),
                                                       HEAD.bit_length() - 1)
        ones = jnp.where(head_of(0) == head_of(1), 1.0, 0.0).astype(BF16)
        delta_ref[...] = jnp.dot(hi, ones, preferred_element_type=F32) + jnp.dot(lo, ones, preferred_element_type=F32)

        w = cw_ref[...]
        b, c, u, z, z1, z2, cv = _conv_fwd(bcu_ref[...], before_ref[...], first_tile, w)
        d_nc = dycat[:, ATTN_W:ATTN_W + CONV_W]
        ychat, rc = _rms_hat(b * cv)
        dgc_ref[...] += jnp.sum(d_nc * ychat, axis=0, keepdims=True)
        dyc = _rms_bwd(ychat, rc, gc_ref[...], d_nc)
        dcv = dyc * b
        behind = carry[...]
        dz = w[2:3, :] * dcv + w[1:2, :] * _shift_up(dcv, behind, 1) + w[0:1, :] * _shift_up(dcv, behind, 2)
        carry[...] = dcv[0:8, :]
        dcw_ref[0:1, :] += jnp.sum(dcv * z2, axis=0, keepdims=True)
        dcw_ref[1:2, :] += jnp.sum(dcv * z1, axis=0, keepdims=True)
        dcw_ref[2:3, :] += jnp.sum(dcv * z, axis=0, keepdims=True)
        tail_ref[:, 0:CONV_W] = (dyc * cv).astype(BF16)
        tail_ref[:, CONV_W:2 * CONV_W] = (dz * u).astype(BF16)
        tail_ref[:, 2 * CONV_W:3 * CONV_W] = (dz * c).astype(BF16)

        d_nx = dycat[:, ATTN_W + CONV_W:D_MODEL]
        yxhat, rx = _rms_hat(yx_ref[...])
        dgx_ref[...] += jnp.sum(d_nx * yxhat, axis=0, keepdims=True)
        dyx = _rms_bwd(yxhat, rx, gx_ref[...], d_nx)
        qxb, mkvb = qx_ref[...], mkv_ref[...]
        heads = [slice(HEAD * hd, HEAD * (hd + 1)) for hd in range(XATTN_W // HEAD)]
        values = [slice(XATTN_W + sl.start, XATTN_W + sl.stop) for sl in heads]
        ss = [lax.dot_general(qxb[:, sl], mkvb[:, sl], NT, preferred_element_type=F32) * SCALE for sl in heads]
        es = [jnp.exp(s - jnp.max(s, axis=1, keepdims=True)) for s in ss]
        ps = [e / jnp.sum(e, axis=1, keepdims=True) for e in es]
        dobs = [dyx[:, sl].astype(BF16) for sl in heads]
        dps = [lax.dot_general(dob, mkvb[:, vsl], NT, preferred_element_type=F32) for dob, vsl in zip(dobs, values)]
        dss = [(p * (dp - jnp.sum(p * dp, axis=1, keepdims=True)) * SCALE).astype(BF16) for p, dp in zip(ps, dps)]
        for sl, vsl, p, dob, ds in zip(heads, values, ps, dobs, dss):
            tail_ref[:, 3 * CONV_W + sl.start:3 * CONV_W + sl.stop] = jnp.dot(
                ds, mkvb[:, sl], preferred_element_type=F32).astype(BF16)
            dmkv_ref[:, sl] += lax.dot_general(ds, qxb[:, sl], TN, preferred_element_type=F32)
            dmkv_ref[:, vsl] += lax.dot_general(p.astype(BF16), dob, TN, preferred_element_type=F32)

    rows = lambda width: pl.BlockSpec((tm, width), lambda i: (n_tiles - 1 - i, 0))
    before = pl.BlockSpec((8, 3 * CONV_W), lambda i: (jnp.maximum((n_tiles - 1 - i) * (tm // 8) - 1, 0), 0))
    acc = lambda r, w: pl.BlockSpec((r, w), lambda i: (0, 0))
    return pl.pallas_call(
        body, name="mixer_bwd", grid=(n_tiles,),
        in_specs=[rows(D_MODEL), rows(D_MODEL), rows(ATTN_W), rows(XATTN_W), rows(3 * CONV_W), before, rows(XATTN_W),
                  _resident((n_mem, 2 * XATTN_W)), _resident((3, CONV_W)), _resident((1, ATTN_W)),
                  _resident((1, CONV_W)), _resident((1, XATTN_W)), _resident((D_MODEL, D_MODEL)),
                  _resident((1, D_MODEL)), pl.BlockSpec(memory_space=pl.ANY)],
        out_specs=[rows(D_MODEL), rows(ATTN_W), rows(ATTN_W), rows(3 * CONV_W + XATTN_W), acc(n_mem, 2 * XATTN_W),
                   acc(3, CONV_W), acc(1, D_MODEL), acc(1, ATTN_W), acc(1, CONV_W), acc(1, XATTN_W)],
        out_shape=[jax.ShapeDtypeStruct((S, D_MODEL), BF16), jax.ShapeDtypeStruct((S, ATTN_W), F32),
                   jax.ShapeDtypeStruct((S, ATTN_W), F32), jax.ShapeDtypeStruct((S, 3 * CONV_W + XATTN_W), BF16),
                   jax.ShapeDtypeStruct((n_mem, 2 * XATTN_W), F32), jax.ShapeDtypeStruct((3, CONV_W), F32),
                   jax.ShapeDtypeStruct((1, D_MODEL), F32), jax.ShapeDtypeStruct((1, ATTN_W), F32),
                   jax.ShapeDtypeStruct((1, CONV_W), F32), jax.ShapeDtypeStruct((1, XATTN_W), F32)],
        scratch_shapes=[pltpu.VMEM((8, CONV_W), F32)],
        compiler_params=_params("arbitrary"),
    )(dx1, y2, ya, yx, bcu, bcu, qx, mkv, conv_w, g_a, g_c, g_x, w_out, g_post, after)


def _memkv_bwd(mem, g_mem, w_kv, dmkv):
    n_mem = mem.shape[0]
    half = D_MODEL // N_CHIPS // 2

    def body(mem_ref, g_ref, w_ref, d_ref, dw_ref, dg_ref):
        mhat, _ = _rms_hat(mem_ref[...])
        mn = (mhat * g_ref[...]).astype(BF16)
        d = d_ref[...].astype(BF16)
        for k in range(2 * N_CHIPS):
            dw_ref[k % 2, k // 2] = lax.dot_general(mn[:, half * k:half * (k + 1)], d, TN, preferred_element_type=F32)
        dmn = lax.dot_general(d, w_ref[...], NT, preferred_element_type=F32)
        dg_ref[...] = jnp.sum(dmn * mhat, axis=0, keepdims=True)

    return pl.pallas_call(
        body, name="memkv_bwd",
        out_shape=[jax.ShapeDtypeStruct((2, N_CHIPS, half, 2 * XATTN_W), F32), jax.ShapeDtypeStruct((1, D_MODEL), F32)],
        compiler_params=pltpu.CompilerParams(vmem_limit_bytes=VMEM_LIMIT_V7X),
    )(mem, g_mem, w_kv, dmkv)


def _in_proj_bwd(dqkv, tail, cos, sin, w_in, x, g, dx1, after, tm):
    S = x.shape[0]

    def body(dq_ref, dk_ref, dv_ref, tail_ref, cos_ref, sin_ref, w_hbm, x_ref, g_ref, dx1_ref, after_ref,
             dproj_ref, dx_ref, dg_ref, w_full, sems):
        _side_by_side(w_hbm, w_full, sems)

        @pl.when(pl.program_id(0) == 0)
        def _():
            dg_ref[...] = jnp.zeros_like(dg_ref)

        halves = [slice(0, tm // 2), slice(tm // 2, tm)]
        for rows in halves:
            c, s = cos_ref[rows, :], sin_ref[rows, :]
            for j in range(ATTN_W // 128):
                cols = slice(128 * j, 128 * (j + 1))
                dproj_ref[rows, cols] = _rope128(dq_ref[rows, cols] * SCALE, c, s, True).astype(BF16)
                dproj_ref[rows, ATTN_W + 128 * j:ATTN_W + 128 * (j + 1)] = _rope128(dk_ref[rows, cols], c, s, True).astype(BF16)
            dproj_ref[rows, 2 * ATTN_W:3 * ATTN_W] = dv_ref[rows, :].astype(BF16)
            dproj_ref[rows, 3 * ATTN_W:PROJ_W] = tail_ref[rows, :]
        dhs = [lax.dot_general(dproj_ref[rows, :], w_full[...], NT, preferred_element_type=F32) for rows in halves]
        for rows, dh in zip(halves, dhs):
            xhat, r = _rms_hat(x_ref[rows, :])
            dg_ref[...] += jnp.sum(dh * xhat, axis=0, keepdims=True)
            dx_ref[rows, :] = dx1_ref[rows, :] + _rms_bwd(xhat, r, g_ref[...], dh)

    return pl.pallas_call(
        body, name="in_proj_bwd", grid=(S // tm,),
        in_specs=[_rows(tm, ATTN_W)] * 3 + [_rows(tm, PROJ_W - 3 * ATTN_W), _rows(tm, 128), _rows(tm, 128),
                  pl.BlockSpec(memory_space=pl.ANY), _rows(tm, D_MODEL), _resident((1, D_MODEL)),
                  _rows(tm, D_MODEL), pl.BlockSpec(memory_space=pl.ANY)],
        out_specs=[_rows(tm, PROJ_W), _rows(tm, D_MODEL), pl.BlockSpec((1, D_MODEL), lambda i: (0, 0))],
        out_shape=[jax.ShapeDtypeStruct((S, PROJ_W), BF16), jax.ShapeDtypeStruct((S, D_MODEL), F32),
                   jax.ShapeDtypeStruct((1, D_MODEL), F32)],
        scratch_shapes=[pltpu.VMEM((D_MODEL, PROJ_W), BF16), pltpu.SemaphoreType.DMA((N_CHIPS,))],
        compiler_params=_params("arbitrary"),
    )(*dqkv, tail, cos, sin, w_in, x, g, dx1, after)


def _row_tile(rows):
    return ROW_TILE if rows % ROW_TILE == 0 else rows


def _chip_sums_bf16(name, grads, from_sibling, place):
    k = len(grads)
    _, n, rows, _ = grads[0].shape
    tr = _row_tile(rows)

    def body(place_ref, *refs):
        for g_ref, b_ref, o_ref in zip(refs[:k], refs[k:2 * k], refs[2 * k:]):
            o_ref[...] = (g_ref[0] + b_ref[...]).astype(BF16)

    mine = lambda g: pl.BlockSpec((1, 1, tr, g.shape[3]), lambda s, i, p: (p[0], s, i, 0))
    slab = lambda g: pl.BlockSpec((1, tr, g.shape[3]), lambda s, i, p: (s, i, 0))
    return pl.pallas_call(
        body, name=name, out_shape=[jax.ShapeDtypeStruct(g.shape[1:], BF16) for g in grads],
        grid_spec=pltpu.PrefetchScalarGridSpec(
            num_scalar_prefetch=1, grid=(n, rows // tr),
            in_specs=[mine(g) for g in grads] + [slab(g) for g in grads], out_specs=[slab(g) for g in grads]),
        compiler_params=_params("parallel", "parallel"),
    )(place, *grads, *from_sibling)


def _final_sums(name, grads, from_sibling, others, place):
    k = len(grads)
    rows = grads[0].shape[2]
    tr = _row_tile(rows)

    def body(place_ref, *refs):
        for a in range(k):
            own_ref, sib_ref = refs[a], refs[k + a]
            acc = own_ref[0, 0] + sib_ref[0]
            for o in refs[2 * k + 3 * a:2 * k + 3 * a + 3]:
                acc = acc + o[0].astype(F32)
            refs[5 * k + a][0] = acc

    own = lambda g: pl.BlockSpec((1, 1, tr, g.shape[3]), lambda i, p: (p[0], p[1], i, 0))
    sib = lambda g: pl.BlockSpec((1, tr, g.shape[3]), lambda i, p: (p[1], i, 0))
    other = lambda g, j: pl.BlockSpec((1, tr, g.shape[3]), lambda i, p: (j, i, 0))
    return pl.pallas_call(
        body, name=name, out_shape=[jax.ShapeDtypeStruct((2,) + g.shape[2:], F32) for g in grads],
        grid_spec=pltpu.PrefetchScalarGridSpec(
            num_scalar_prefetch=1, grid=(rows // tr,),
            in_specs=[own(g) for g in grads] + [sib(g) for g in grads] + [other(g, j) for g in grads for j in range(3)],
            out_specs=[pl.BlockSpec((1, tr, g.shape[3]), lambda i, p: (p[0], i, 0)) for g in grads]),
        compiler_params=_params("parallel"),
    )(place, *grads, *from_sibling, *[o for o in others for _ in range(3)])


def _adamw_update(w, g, m, v):
    m = ADAM_B1 * m + (1.0 - ADAM_B1) * g
    v = ADAM_B2 * v + (1.0 - ADAM_B2) * (g * g)
    m_hat = m * (1.0 / (1.0 - ADAM_B1 ** ADAM_STEP))
    v_hat = v * (1.0 / (1.0 - ADAM_B2 ** ADAM_STEP))
    return -ADAM_LR * (m_hat / (jnp.sqrt(v_hat) + ADAM_EPS) + ADAM_WD * w), m, v


def _adamw(name, params, after):
    k = len(params)
    rows = params[0][0].shape[0]
    tr = ADAMW_ROW_TILE if rows % ADAMW_ROW_TILE == 0 else rows

    def body(*refs):
        ins, outs = refs[:4 * k], refs[4 * k + 1:]
        for a in range(k):
            w_ref, g_ref, m_ref, v_ref = ins[4 * a:4 * a + 4]
            g = g_ref[...]
            outs[4 * a][...] = g
            outs[4 * a + 1][...], outs[4 * a + 2][...], outs[4 * a + 3][...] = _adamw_update(w_ref[...], g, m_ref[...], v_ref[...])

    spec = lambda w: pl.BlockSpec((tr, w.shape[1]), lambda i: (i, 0))
    out = pl.pallas_call(
        body, name=name, grid=(rows // tr,),
        in_specs=[spec(p[0]) for p in params for _ in range(4)] + [pl.BlockSpec(memory_space=pl.ANY)],
        out_specs=[spec(p[0]) for p in params for _ in range(4)],
        out_shape=[jax.ShapeDtypeStruct(p[0].shape, F32) for p in params for _ in range(4)],
        compiler_params=_params("parallel"),
    )(*[t for p in params for t in p], after)
    return [out[4 * a:4 * a + 4] for a in range(k)]


def _small_update(summed, chip, gains, gains_m, gains_v, taps, taps_m, taps_v):
    n = len(gains)
    widths = [g.shape[1] for g in gains]
    k, w = taps.shape

    def body(*refs):
        chip_ref, sum_ref = refs[0], refs[1]
        params = [refs[2 + 3 * i:5 + 3 * i] for i in range(n + 1)]
        outs = [refs[2 + 3 * (n + 1) + 4 * i:2 + 3 * (n + 1) + 4 * (i + 1)] for i in range(n + 1)]
        loss_ref = refs[-1]
        for i in range(n):
            g = sum_ref[i:i + 1, 0:widths[i]]
            wr, mr, vr = params[i]
            outs[i][0][...] = g
            outs[i][1][...], outs[i][2][...], outs[i][3][...] = _adamw_update(wr[...], g, mr[...], vr[...])
        g = sum_ref[n:n + k, 0:w]
        for j in range(1, N_CHIPS):
            g = jnp.where(chip_ref[0] == j, sum_ref[n:n + k, w * j:w * (j + 1)], g)
        wr, mr, vr = params[n]
        outs[n][0][...] = g
        outs[n][1][...], outs[n][2][...], outs[n][3][...] = _adamw_update(wr[...], g, mr[...], vr[...])
        loss_ref[...] = sum_ref[n + k:n + k + 1, 0:1]

    vmem = pl.BlockSpec(memory_space=pltpu.VMEM)
    operands = [chip, summed]
    for p in zip(list(gains) + [taps], list(gains_m) + [taps_m], list(gains_v) + [taps_v]):
        operands += list(p)
    shapes = [jax.ShapeDtypeStruct(p.shape, F32) for p in list(gains) + [taps] for _ in range(4)]
    out = pl.pallas_call(
        body, name="small_update", out_shape=shapes + [jax.ShapeDtypeStruct((1, 1), F32)],
        in_specs=[pl.BlockSpec(memory_space=pltpu.SMEM)] + [vmem] * (len(operands) - 1),
        out_specs=[vmem] * (len(shapes) + 1),
    )(*operands)
    return [out[4 * i:4 * (i + 1)] for i in range(n + 1)], out[-1]


def _sum_blocks(name, blocks):
    n, rows, cols = blocks.shape

    def body(b_ref, o_ref):
        acc = b_ref[0]
        for k in range(1, n):
            acc = acc + b_ref[k]
        o_ref[...] = acc

    return pl.pallas_call(body, name=name, out_shape=jax.ShapeDtypeStruct((rows, cols), F32))(blocks)


def _place():
    return lax.axis_index("x"), lax.axis_index("y"), lax.axis_index("c")


def _other_chips(x, y):
    return [(1 - x, y), (x, 1 - y), (1 - x, 1 - y)]


def _allgather_finish(name, shards, landed, pass_on):
    n = len(shards)

    def body(*refs):
        ins, outs, stage = refs[:n], refs[2 * n:3 * n], refs[3 * n:4 * n]
        send_sems, recv_sems, local_sems = refs[4 * n:]
        x, y, c = _place()
        chips = _other_chips(x, y)
        barrier = pltpu.get_barrier_semaphore()
        pl.semaphore_signal(barrier, inc=1, device_id=(x, y, 1 - c), device_id_type=MESH)
        pl.semaphore_wait(barrier, 1)

        def copy(a, k, chip, half):
            place = outs[a].at[2 * chip[0] + chip[1], half]
            return pltpu.make_async_remote_copy(
                src_ref=place, dst_ref=place, send_sem=send_sems.at[3 * a + k], recv_sem=recv_sems.at[3 * a + k],
                device_id=(x, y, 1 - c), device_id_type=MESH)

        load = [pltpu.make_async_copy(ins[a], stage[a], local_sems.at[a]) for a in range(n)]
        local = [pltpu.make_async_copy(stage[a], outs[a].at[2 * x + y], local_sems.at[a]) for a in range(n)]
        for cp in load:
            cp.start()
        passed = [copy(a, k, chip, c) for a in range(n) if pass_on[a] for k, chip in enumerate(chips)]
        for cp in passed:
            cp.start()
        for a in range(n):
            load[a].wait()
            local[a].start()
        for a in range(n):
            if pass_on[a]:
                for k, chip in enumerate(chips):
                    copy(a, k, chip, 1 - c).wait_recv()
        for cp in passed:
            cp.wait_send()
        for cp in local:
            cp.wait()

    any_spec = pl.BlockSpec(memory_space=pl.ANY)
    return pl.pallas_call(
        body, name=name,
        out_shape=[jax.ShapeDtypeStruct((N_CHIPS,) + s.shape, s.dtype) for s in shards],
        in_specs=[any_spec] * (2 * n), out_specs=[any_spec] * n,
        input_output_aliases={n + a: a for a in range(n)},
        scratch_shapes=[pltpu.VMEM(s.shape, s.dtype) for s in shards]
        + [pltpu.SemaphoreType.DMA((3 * n,)), pltpu.SemaphoreType.DMA((3 * n,)), pltpu.SemaphoreType.DMA((n,))],
        compiler_params=pltpu.CompilerParams(vmem_limit_bytes=VMEM_LIMIT_V7X, collective_id=HANDSHAKES["sibling"][0]),
    )(*shards, *landed)


def _plan_first_hop(x, y, c, shards, lands):
    return [(shards[a].at[c], lands[a].at[2 * x + y, c], lands[a].at[2 * chip[0] + chip[1], c], (*chip, c))
            for a in range(len(shards)) for chip in _other_chips(x, y)]


def _plan_pass_on(x, y, c, nothing, lands):
    def place(a, chip, half):
        return lands[a].at[2 * chip[0] + chip[1], half]

    return [(place(a, chip, c), place(a, chip, c), place(a, chip, 1 - c), (x, y, 1 - c))
            for a in range(len(lands)) for chip in _other_chips(x, y)]


def _plan_own_half_to_sibling(x, y, c, nothing, lands):
    return [(lands[a].at[c], lands[a].at[c], lands[a].at[1 - c], (x, y, 1 - c)) for a in range(len(lands))]


def _plan_other_half_to_sibling(x, y, c, grads, lands):
    return [(grads[a].at[1 - c], lands[a], lands[a], (x, y, 1 - c)) for a in range(len(grads))]


def _plan_to_other_chips(x, y, c, partials, lands):
    return [(partials[a].at[2 * chip[0] + chip[1]], lands[a].at[k], lands[a].at[k], (*chip, c))
            for a in range(len(partials)) for k, chip in enumerate(_other_chips(x, y))]


def _plan_to_all(x, y, c, blocks, lands):
    flips = [(fx, fy, fc) for fx in (0, 1) for fy in (0, 1) for fc in (0, 1) if (fx, fy, fc) != (0, 0, 0)]
    peers = [(1 - x if fx else x, 1 - y if fy else y, 1 - c if fc else c) for fx, fy, fc in flips]
    return [(blocks[0], lands[0].at[4 * x + 2 * y + c], lands[0].at[4 * p[0] + 2 * p[1] + p[2]], p) for p in peers]


def _planned_copies(plan, srcs, lands, send_sems, recv_sems):
    x, y, c = _place()

    def pair(k, src, there, here, to):
        make = lambda dst: pltpu.make_async_remote_copy(
            src_ref=src, dst_ref=dst, send_sem=send_sems.at[k], recv_sem=recv_sems.at[k], device_id=to, device_id_type=MESH)
        return make(there), make(here)

    return [pair(k, *entry) for k, entry in enumerate(plan(x, y, c, srcs, lands))]


_HBM_SPEC = pl.BlockSpec(memory_space=pltpu.HBM)
_SEM_SPEC = pl.BlockSpec(memory_space=pltpu.SEMAPHORE)


def _hbm(a):
    return pltpu.with_memory_space_constraint(a, pltpu.HBM)


HANDSHAKES = {
    "sibling": (1, lambda x, y, c: [(x, y, 1 - c)]),
}


def _exchange_start(name, plan, n_copies, srcs, land_shapes, after, lands=None, peers=None):
    if lands is None:
        lands = [lax.empty(s.shape, s.dtype) for s in land_shapes]
    land_shapes = lands
    ns, nl = len(srcs), len(land_shapes)
    n_in = ns + nl + 1
    collective_id, peers_of = HANDSHAKES[peers] if peers else (None, None)

    def body(*refs):
        if peers:
            who = peers_of(*_place())
            barrier = pltpu.get_barrier_semaphore()
            for peer in who:
                pl.semaphore_signal(barrier, inc=1, device_id=peer, device_id_type=MESH)
            pl.semaphore_wait(barrier, len(who))
        for send, _ in _planned_copies(plan, refs[:ns], refs[ns:ns + nl], refs[n_in], refs[n_in + 1]):
            send.start()
        refs[-1][...] = jnp.zeros_like(refs[-1])

    out = pl.pallas_call(
        body, name=name,
        out_shape=(pltpu.SemaphoreType.DMA((n_copies,)), pltpu.SemaphoreType.DMA((n_copies,)),
                   *[pltpu.HBM(s.shape, s.dtype) for s in land_shapes], jax.ShapeDtypeStruct((8, 128), F32)),
        in_specs=[_HBM_SPEC] * (ns + nl) + [pl.BlockSpec(memory_space=pl.ANY)],
        out_specs=(_SEM_SPEC, _SEM_SPEC, *[_HBM_SPEC] * nl, pl.BlockSpec(memory_space=pltpu.VMEM)),
        input_output_aliases={ns + i: 2 + i for i in range(nl)},
        compiler_params=pltpu.CompilerParams(has_side_effects=pltpu.SideEffectType.DATAFLOW_SIDE_EFFECTING,
                                             collective_id=collective_id),
    )(*[_hbm(s) for s in srcs], *[_hbm(l) for l in lands], after)
    return out[0], out[1], list(out[2:2 + nl]), out[-1]


def _exchange_wait(name, plan, srcs, started, after):
    send_sems, recv_sems, lands, _ = started
    ns, nl = len(srcs), len(lands)
    after = list(after) if isinstance(after, (list, tuple)) else [after]

    def body(*refs):
        for send, recv in _planned_copies(plan, refs[:ns], refs[ns:ns + nl], refs[ns + nl], refs[ns + nl + 1]):
            send.wait_send()
            recv.wait_recv()

    return pl.pallas_call(
        body, name=name, out_shape=[pltpu.HBM(l.shape, l.dtype) for l in lands],
        in_specs=[_HBM_SPEC] * (ns + nl) + [_SEM_SPEC, _SEM_SPEC] + [pl.BlockSpec(memory_space=pl.ANY)] * len(after),
        out_specs=[_HBM_SPEC] * nl, input_output_aliases={ns + i: i for i in range(nl)},
        compiler_params=pltpu.CompilerParams(has_side_effects=pltpu.SideEffectType.DATAFLOW_SIDE_EFFECTING),
    )(*[_hbm(s) for s in srcs], *lands, send_sems, recv_sems, *after)


def _like(arrays, lead, dtype=None):
    return [jax.ShapeDtypeStruct(tuple(lead) + a.shape[-2:], dtype or a.dtype) for a in arrays]


class _StepExchanges:
    def __init__(self, mats, conv_w):
        x, y, c = _place()
        self.place = jnp.stack([c, 2 * x + y]).astype(jnp.int32)
        shards = [w.astype(BF16).reshape(2, w.shape[0] // 2, w.shape[1]) for w in mats]
        self._in_shard = shards[:1]
        self._in = _exchange_start("w_in_allgather_start", _plan_first_hop, 3, self._in_shard,
                                   _like(self._in_shard, (N_CHIPS, 2)), shards[0])
        self.zero = self._in[3]
        taps = jnp.pad(conv_w, ((0, 8 - conv_w.shape[0]), (0, 128 - conv_w.shape[1])))
        self._rest_shards = shards[1:] + [jnp.stack([taps, jnp.zeros_like(taps)])]
        self._taps_shape = conv_w.shape
        self._groups = {}

    def w_in(self, after):
        landed = _exchange_wait("w_in_allgather_wait", _plan_first_hop, self._in_shard, self._in,
                                list(after) + self._rest_shards)
        (w_in,) = _allgather_finish("w_in_allgather_finish", self._in_shard, landed, [True])
        self._rest = _exchange_start("rest_allgather_start", _plan_first_hop, 3 * len(self._rest_shards),
                                     self._rest_shards, _like(self._rest_shards, (N_CHIPS, 2)), w_in)
        self.zero = self._rest[3]
        return w_in.reshape(N_CHIPS, 2 * w_in.shape[2], w_in.shape[3])

    def rest_weights(self, after):
        landed = _exchange_wait("rest_allgather_wait", _plan_first_hop, self._rest_shards, self._rest, after)
        kv, out, up, down, taps = _allgather_finish("rest_allgather_finish", self._rest_shards, landed,
                                                    [True, True, False, False, True])
        self._up_down = _exchange_start("up_down_pass_on_start", _plan_pass_on, 6, [], None, self.zero, lands=[up, down],
                                        peers="sibling")
        self.zero = self._up_down[3]
        k, w = self._taps_shape
        taps = taps[:, 0, :k, :w].transpose(1, 0, 2).reshape(k, N_CHIPS * w)
        return [g.reshape(N_CHIPS, 2 * g.shape[2], g.shape[3]) for g in (kv, out)], taps

    def up_down(self, after):
        full = _exchange_wait("up_down_pass_on_wait", _plan_pass_on, [], self._up_down, after)
        return [g.reshape(N_CHIPS, 2 * g.shape[2], g.shape[3]) for g in full]

    def send_grads(self, key, grads):
        grads = list(grads)
        started = _exchange_start(f"{key}_grads_to_sibling_start", _plan_other_half_to_sibling, len(grads), grads,
                                  _like(grads, (N_CHIPS,)), self.zero, peers="sibling")
        self._groups[key] = dict(grads=grads, to_sibling=started)
        self.zero = started[3]

    def grads_at_sibling(self, key, after):
        group = self._groups[key]
        grads = group["grads"]
        group["from_sibling"] = _exchange_wait(f"{key}_grads_to_sibling_wait", _plan_other_half_to_sibling, grads,
                                               group["to_sibling"], after)
        group["partials"] = _chip_sums_bf16(f"{key}_chip_sums", grads, group["from_sibling"], self.place)
        group["to_chips"] = _exchange_start(f"{key}_grads_to_chips_start", _plan_to_other_chips, 3 * len(grads),
                                            group["partials"], _like(group["partials"], (3,)), self.zero)
        self.zero = group["to_chips"][3]

    def grads_summed(self, key, after):
        group = self._groups[key]
        from_chips = _exchange_wait(f"{key}_grads_to_chips_wait", _plan_to_other_chips, group["partials"],
                                    group["to_chips"], after)
        return _final_sums(f"{key}_final_sums", group["grads"], group["from_sibling"], from_chips, self.place)

    def send_sums(self, key, sums):
        self._groups[key + "_sums"] = _exchange_start(f"{key}_sums_to_sibling_start", _plan_own_half_to_sibling,
                                                      len(sums), [], None, self.zero, lands=list(sums),
                                                      peers="sibling")
        self.zero = self._groups[key + "_sums"][3]

    def whole_sums(self, key, after):
        full = _exchange_wait(f"{key}_sums_to_sibling_wait", _plan_own_half_to_sibling, [], self._groups[key + "_sums"], after)
        return [t.reshape(2 * t.shape[1], t.shape[2]) for t in full]

    def send_small(self, block):
        self._small = block
        self._small_started = _exchange_start("small_grads_start", _plan_to_all, 7, [block],
                                              [jax.ShapeDtypeStruct((8,) + block.shape, block.dtype)], self.zero)
        self.zero = self._small_started[3]

    def small_summed(self, after):
        x, y, c = _place()
        (landed,) = _exchange_wait("small_grads_wait", _plan_to_all, [self._small], self._small_started, after)
        blocks = lax.dynamic_update_index_in_dim(landed, self._small, 4 * x + 2 * y + c, 0)
        return _sum_blocks("small_sum", blocks)


def _rope_tables(positions):
    half = HEAD // 2
    inv_freq = jnp.float32(ROPE_THETA) ** (-(jnp.arange(half, dtype=F32) * 2.0 / HEAD))
    ang = positions.astype(F32)[:, None] * inv_freq
    cos, sin = jnp.cos(ang), jnp.sin(ang)
    return jnp.tile(cos, (1, 4)), jnp.tile(jnp.concatenate([-sin, sin], axis=1), (1, 2))


def _local_step(x, mem, positions, target, gains, ex):
    g_pre_mix, g_mem, g_a, g_c, g_x, g_post_mix, g_pre_mlp, g_post_mlp = gains
    tm = ROW_TILE
    cos, sin = _rope_tables(positions)
    h = _pre_norm(x, g_pre_mix, ex.zero, tm)
    w_in = ex.w_in([h, cos, sin])

    q, k, v, bcu, qx = _in_proj_fwd(h, w_in, cos, sin, ex.zero, tm)
    ya, lse = _attn_fwd(q, k, v)
    (w_kv, w_out), conv_w = ex.rest_weights(lse)
    w_kv, w_out = (w.reshape(N_CHIPS * w.shape[1], w.shape[2]) for w in (w_kv, w_out))
    memn, mkv = _memkv_fwd(mem, g_mem, w_kv, ex.zero)
    yx, ycat, y2, x1 = _mix_fwd(ya, bcu, qx, mkv, conv_w, g_a, g_c, g_x, w_out, g_post_mix, x, tm)
    w_up, w_down = ex.up_down(x1)
    w_down = w_down.reshape(N_CHIPS * w_down.shape[1], w_down.shape[2])
    h2, f, du, df2, dx1, dg_pre_mlp, dg_post_mlp, loss = _mlp_fwd_bwd(x1, target, g_pre_mlp, g_post_mlp, w_up, w_down,
                                                                      MLP_ROW_TILE)
    gw_down = _weight_grad("grad_w_down", f, df2, True, ex.zero)
    gw_up = _weight_grad("grad_w_up", h2, du, False, ex.zero)
    ex.send_grads("early", [gw_up, gw_down])

    dy2, dya, delta, tail, dmkv, g_conv, dg_post_mix, dg_a, dg_c, dg_x = _mixer_bwd(
        dx1, y2, ya, yx, bcu, qx, mkv, conv_w, g_a, g_c, g_x, w_out, g_post_mix, ex.zero, tm)
    ex.grads_at_sibling("early", dy2)
    gw_out = _weight_grad("grad_w_out", ycat, dy2, True, ex.zero)
    gw_kv, dg_mem = _memkv_bwd(mem, g_mem, w_kv, dmkv)
    ex.send_grads("mid", [gw_out, gw_kv])
    dqkv = _attn_bwd(q, k, v, dya, lse, delta, ex.zero)
    ex.grads_at_sibling("mid", dqkv[0])
    dproj, grad_x, dg_pre_mix = _in_proj_bwd(dqkv, tail, cos, sin, w_in, x, g_pre_mix, dx1, ex.zero, tm)
    gain_grads = [dg_pre_mix, dg_mem, dg_a, dg_c, dg_x, dg_post_mix, dg_pre_mlp, dg_post_mlp]
    ex.send_small(_pack_small(gain_grads, g_conv, loss))
    gw_in = _weight_grad_w_in(h, dproj)
    ex.send_grads("late", [gw_in])
    return grad_x


def _pack_small(gains, conv, scalar=None):
    rows = [jnp.pad(g, ((0, 0), (0, D_MODEL - g.shape[1]))) for g in gains]
    rows.append(jnp.pad(conv, ((0, 0), (0, D_MODEL - conv.shape[1]))))
    last = jnp.zeros((SMALL_ROWS - 8 - conv.shape[0], D_MODEL), F32)
    rows.append(last if scalar is None else last.at[0:1, 0:1].set(scalar))
    return jnp.concatenate(rows, axis=0)


def _unpack_small(block, gain_widths, conv_width):
    gains = [block[i:i + 1, :w] for i, w in enumerate(gain_widths)]
    return gains, block[8:11, :conv_width], block[11, 0]


def kernel(x, mem, positions, g_pre_mix, g_mem, w_in, w_mem_kv, conv_w, g_attn_out, g_conv_out, g_xattn_out, w_out, g_post_mix, g_pre_mlp, w_up, w_down, g_post_mlp, loss_target, m_g_pre_mix, m_g_mem, m_w_in, m_w_mem_kv, m_conv_w, m_g_attn_out, m_g_conv_out, m_g_xattn_out, m_w_out, m_g_post_mix, m_g_pre_mlp, m_w_up, m_w_down, m_g_post_mlp, v_g_pre_mix, v_g_mem, v_w_in, v_w_mem_kv, v_conv_w, v_g_attn_out, v_g_conv_out, v_g_xattn_out, v_w_out, v_g_post_mix, v_g_pre_mlp, v_w_up, v_w_down, v_g_post_mlp):
    cx, cy, cc = _place()
    chip = 2 * cx + cy
    gains = [g_pre_mix, g_mem, g_attn_out, g_conv_out, g_xattn_out, g_post_mix, g_pre_mlp, g_post_mlp]
    gains_m = [m_g_pre_mix, m_g_mem, m_g_attn_out, m_g_conv_out, m_g_xattn_out, m_g_post_mix, m_g_pre_mlp, m_g_post_mlp]
    gains_v = [v_g_pre_mix, v_g_mem, v_g_attn_out, v_g_conv_out, v_g_xattn_out, v_g_post_mix, v_g_pre_mlp, v_g_post_mlp]
    gain_widths = [g.shape[1] for g in gains]
    mats = [w_in[0], w_mem_kv[0], w_out[0], w_up[0], w_down[0]]
    mats_m = [m_w_in[0], m_w_mem_kv[0], m_w_out[0], m_w_up[0], m_w_down[0]]
    mats_v = [v_w_in[0], v_w_mem_kv[0], v_w_out[0], v_w_up[0], v_w_down[0]]

    ex = _StepExchanges(mats, conv_w[0])
    grad_x = _local_step(x[0], mem[0], positions[0], loss_target[0], gains, ex)

    ex.send_sums("four", ex.grads_summed("early", ex.zero) + ex.grads_summed("mid", ex.zero))
    ex.grads_at_sibling("late", ex.zero)
    up_sum, down_sum, out_sum, kv_sum = ex.whole_sums("four", ex.zero)
    params = lambda a, g: (mats[a], g, mats_m[a], mats_v[a])
    new_up, new_down = _adamw("adamw_up_down", [params(3, up_sum), params(4, down_sum)], ex.zero)
    new_out, new_kv = _adamw("adamw_out_kv", [params(2, out_sum), params(1, kv_sum)], ex.zero)

    small, total = _small_update(ex.small_summed(new_kv[1]), chip.reshape(1).astype(jnp.int32), gains, gains_m,
                                 gains_v, conv_w[0], m_conv_w[0], v_conv_w[0])

    ex.send_sums("last", ex.grads_summed("late", small[0][1]))
    (in_sum,) = ex.whole_sums("last", ex.zero)
    (new_in,) = _adamw("adamw_in", [params(0, in_sum)], in_sum)
    mat_new = [new_in, new_kv, new_out, new_up, new_down]

    order = ["g_pre_mix", "g_mem", "w_in", "w_mem_kv", "conv_w", "g_attn_out", "g_conv_out", "g_xattn_out", "w_out",
             "g_post_mix", "g_pre_mlp", "w_up", "w_down", "g_post_mlp"]
    gain_names = ["g_pre_mix", "g_mem", "g_attn_out", "g_conv_out", "g_xattn_out", "g_post_mix", "g_pre_mlp", "g_post_mlp"]
    mat_names = ["w_in", "w_mem_kv", "w_out", "w_up", "w_down"]

    def leaf(kind, name):
        if name in gain_names:
            return small[gain_names.index(name)][kind]
        if name == "conv_w":
            return small[len(gain_names)][kind][None]
        return mat_new[mat_names.index(name)][kind][None]

    return (total[0, 0], grad_x[None], *[leaf(kind, name) for kind in range(4) for name in order])
```

```python
import jax
import jax.numpy as jnp
from jax import lax
from jax.experimental import pallas as pl
from jax.experimental.pallas import tpu as pltpu

F32, BF16 = jnp.float32, jnp.bfloat16

D_MODEL = 1024
ATTN_W = 512
CONV_W = 256
XATTN_W = 256
PROJ_W = 3 * ATTN_W + 3 * CONV_W + XATTN_W
D_FF = 4096
HEAD = 64
N_BACK = 128
DILATIONS = (1, 4, 16)
PATTERN_ORDER = DILATIONS[::-1]
ROPE_THETA = 10000.0
EPS = 1e-6
NEG_INF = -1e30
SCALE = HEAD ** -0.5
N_CHIPS = 4
SHARD_IN = PROJ_W // N_CHIPS
SHARD_FF = D_FF // N_CHIPS

ADAM_LR, ADAM_B1, ADAM_B2, ADAM_EPS, ADAM_WD, ADAM_STEP = 0.001, 0.9, 0.999, 1e-08, 0.01, 10

VMEM_LIMIT_V7X = 56 * 1024 * 1024
ROW_TILE = 512
MLP_ROW_TILE = 256
ADAMW_ROW_TILE = 256
SMALL_ROWS = 16

NT = (((1,), (1,)), ((), ()))
TN = (((0,), (0,)), ((), ()))
MESH = pl.DeviceIdType.MESH


def _params(*sem):
    return pltpu.CompilerParams(dimension_semantics=sem, vmem_limit_bytes=VMEM_LIMIT_V7X)


def _resident(shape):
    return pl.BlockSpec(shape, lambda *_: (0,) * len(shape), pipeline_mode=pl.Buffered(1))


def _rows(tm, width):
    return pl.BlockSpec((tm, width), lambda i: (i, 0))


def _rms_hat(x):
    r = lax.rsqrt(jnp.mean(x * x, axis=-1, keepdims=True) + EPS)
    return x * r, r


def _rms_bwd(xhat, r, g, dy):
    gdy = dy * g
    return r * (gdy - xhat * jnp.mean(xhat * gdy, axis=-1, keepdims=True))


def _rope128(t, cos, sin_signed, inverse):
    lane = lax.broadcasted_iota(jnp.int32, t.shape, 1)
    first_half = (lane % HEAD) < (HEAD // 2)
    rot = jnp.where(first_half, pltpu.roll(t, 128 - HEAD // 2, 1), pltpu.roll(t, HEAD // 2, 1))
    return t * cos - rot * sin_signed if inverse else t * cos + rot * sin_signed


def _pre_norm(x, g, after, tm):
    S = x.shape[0]

    def body(x_ref, g_ref, after_ref, h_ref):
        h_ref[...] = (_rms_hat(x_ref[...])[0] * g_ref[...]).astype(BF16)

    return pl.pallas_call(
        body, name="pre_norm", grid=(S // tm,),
        in_specs=[_rows(tm, D_MODEL), _resident((1, D_MODEL)), pl.BlockSpec(memory_space=pl.ANY)],
        out_specs=_rows(tm, D_MODEL), out_shape=jax.ShapeDtypeStruct((S, D_MODEL), BF16),
        compiler_params=_params("parallel"),
    )(x, g, after)


def _side_by_side(w_hbm, w_full, sems):
    width = w_hbm.shape[2]

    @pl.when(pl.program_id(0) == 0)
    def _():
        copies = [pltpu.make_async_copy(w_hbm.at[j], w_full.at[:, pl.ds(width * j, width)], sems.at[j])
                  for j in range(N_CHIPS)]
        for cp in copies:
            cp.start()
        for cp in copies:
            cp.wait()


def _in_proj_fwd(h, w_in, cos, sin, after, tm):
    S = h.shape[0]

    def body(h_ref, w_hbm, cos_ref, sin_ref, after_ref, q_ref, k_ref, v_ref, bcu_ref, qx_ref, proj, w_full, sems):
        _side_by_side(w_hbm, w_full, sems)
        halves = [slice(0, tm // 2), slice(tm // 2, tm)]
        for rows in halves:
            proj[rows, :] = jnp.dot(h_ref[rows, :], w_full[...], preferred_element_type=F32)
        for rows in halves:
            c, s = cos_ref[rows, :], sin_ref[rows, :]
            for j in range(ATTN_W // 128):
                lo = 128 * j
                q_ref[rows, lo:lo + 128] = _rope128(proj[rows, lo:lo + 128], c, s, False) * SCALE
                k_ref[rows, lo:lo + 128] = _rope128(proj[rows, ATTN_W + lo:ATTN_W + lo + 128], c, s, False)
            v_ref[rows, :] = proj[rows, 2 * ATTN_W:3 * ATTN_W]
            bcu_ref[rows, :] = proj[rows, 3 * ATTN_W:3 * ATTN_W + 3 * CONV_W]
            qx_ref[rows, :] = proj[rows, 3 * ATTN_W + 3 * CONV_W:PROJ_W].astype(BF16)

    return pl.pallas_call(
        body, name="in_proj_fwd", grid=(S // tm,),
        in_specs=[_rows(tm, D_MODEL), pl.BlockSpec(memory_space=pl.ANY), _rows(tm, 128), _rows(tm, 128),
                  pl.BlockSpec(memory_space=pl.ANY)],
        out_specs=[_rows(tm, ATTN_W), _rows(tm, ATTN_W), _rows(tm, ATTN_W), _rows(tm, 3 * CONV_W), _rows(tm, XATTN_W)],
        out_shape=[jax.ShapeDtypeStruct((S, ATTN_W), F32), jax.ShapeDtypeStruct((S, ATTN_W), F32),
                   jax.ShapeDtypeStruct((S, ATTN_W), F32), jax.ShapeDtypeStruct((S, 3 * CONV_W), F32),
                   jax.ShapeDtypeStruct((S, XATTN_W), BF16)],
        scratch_shapes=[pltpu.VMEM((tm, PROJ_W), F32), pltpu.VMEM((D_MODEL, PROJ_W), BF16),
                        pltpu.SemaphoreType.DMA((N_CHIPS,))],
        compiler_params=_params("arbitrary"),
    )(h, w_in, cos, sin, after)


def _memkv_fwd(mem, g_mem, w_kv, after):
    n_mem = mem.shape[0]

    def body(mem_ref, g_ref, w_ref, after_ref, mn_ref, kv_ref):
        mhat, _ = _rms_hat(mem_ref[...])
        mn = (mhat * g_ref[...]).astype(BF16)
        mn_ref[...] = mn
        kv_ref[...] = jnp.dot(mn, w_ref[...], preferred_element_type=F32).astype(BF16)

    vmem = pl.BlockSpec(memory_space=pltpu.VMEM)
    return pl.pallas_call(
        body, name="memkv_fwd", in_specs=[vmem, vmem, vmem, pl.BlockSpec(memory_space=pl.ANY)], out_specs=[vmem, vmem],
        out_shape=[jax.ShapeDtypeStruct((n_mem, D_MODEL), BF16), jax.ShapeDtypeStruct((n_mem, 2 * XATTN_W), BF16)],
        compiler_params=pltpu.CompilerParams(vmem_limit_bytes=VMEM_LIMIT_V7X),
    )(mem, g_mem, w_kv, after)


def _fill_band_bias(bias):
    row = lax.broadcasted_iota(jnp.int32, (N_BACK, 2 * N_BACK), 0)
    col = lax.broadcasted_iota(jnp.int32, (N_BACK, 2 * N_BACK), 1)
    band = (col >= row) & (col <= row + N_BACK)
    bias[1] = jnp.where(band, 0.0, NEG_INF)
    bias[0] = jnp.where(band & (col >= N_BACK), 0.0, NEG_INF)


def _strided(start, size, d):
    return pl.ds(start, size) if d == 1 else pl.ds(start, size, stride=d)


def _group_starts(g, G, nb, d):
    t0 = g * G
    r, n0 = lax.shift_right_logical(t0, nb.bit_length() - 1), lax.bitwise_and(t0, nb - 1)
    first = r + n0 * (N_BACK * d)
    before = r + jnp.maximum(n0 - 1, 0) * (N_BACK * d)
    starts = [before] + [first + u * (N_BACK * d) for u in range(G)]
    if d == 1:
        starts = [pl.multiple_of(st, N_BACK) for st in starts]
    return starts, n0


def _step_blocks(i, U, nb, d):
    G = min(U, nb)
    whole = G == nb
    row_blocks, blocks = [], []
    for grp in range(U // G):
        starts, n0 = _group_starts(i * (U // G) + grp, G, nb, d)
        base = len(row_blocks)
        if whole:
            row_blocks += [_strided(st, N_BACK, d) for st in starts[1:]]
            blocks += [(base + max(u - 1, 0), base + u, min(u, 1)) for u in range(G)]
        else:
            row_blocks += [_strided(st, N_BACK, d) for st in starts]
            blocks += [(base + u, base + u + 1, jnp.minimum(n0, 1) if u == 0 else 1) for u in range(G)]
    return row_blocks, blocks


def _by_head(a, b):
    lane = lax.broadcasted_iota(jnp.int32, (a.shape[0], 2 * HEAD), 1)
    return jnp.where(lane < HEAD, a, b)


def _head_only(t, hh):
    lane = lax.broadcasted_iota(jnp.int32, t.shape, 1)
    return jnp.where((lane < HEAD) == (hh == 0), t, jnp.zeros_like(t))


def _stack_heads(t):
    return jnp.concatenate([_head_only(t, 0), _head_only(t, 1)], axis=0)


def _head_columns(t):
    return jnp.concatenate([t[:, 0:1], t[:, HEAD:HEAD + 1]], axis=0)


def _unstack(t):
    return _by_head(t[:N_BACK], t[N_BACK:])


def _unstack_columns(t):
    return _by_head(jnp.broadcast_to(t[:N_BACK], (N_BACK, 2 * HEAD)), jnp.broadcast_to(t[N_BACK:], (N_BACK, 2 * HEAD)))


FWD_BLOCKS_PER_STEP = 4
BWD_BLOCKS_PER_STEP = 4
BWD_CHUNK = 64


def _attn_fwd(q, k, v):
    S = q.shape[0]
    U = FWD_BLOCKS_PER_STEP

    def body(q_ref, k_ref, v_ref, y_ref, m_ref, l_scr, bias):
        _fill_band_bias(bias)
        for g, d in enumerate(PATTERN_ORDER):
            nb = S // d // N_BACK
            first_pattern, last_pattern = g == 0, g == len(PATTERN_ORDER) - 1

            def step(i, carry, d=d, nb=nb, first_pattern=first_pattern, last_pattern=last_pattern):
                row_blocks, blocks = _step_blocks(i, U, nb, d)
                kb = [k_ref[r, :].astype(BF16) for r in row_blocks]
                ss = []
                for before, own, which in blocks:
                    kw = jnp.concatenate([kb[before], kb[own]], 0)
                    qs = _stack_heads(q_ref[row_blocks[own], :].astype(BF16))
                    b = bias[which]
                    ss.append(lax.dot_general(qs, kw, NT, preferred_element_type=F32) + jnp.concatenate([b, b], axis=0))
                ms = [jnp.max(s, axis=1, keepdims=True) for s in ss]
                ps = [jnp.exp(s - m) for s, m in zip(ss, ms)]
                ls = [jnp.sum(p, axis=1, keepdims=True) for p in ps]
                vb = [v_ref[r, :].astype(BF16) for r in row_blocks]
                os_ = [jnp.dot(ps[u].astype(BF16), jnp.concatenate([vb[before], vb[own]], 0), preferred_element_type=F32)
                       for u, (before, own, _) in enumerate(blocks)]
                for u, (_, own, _) in enumerate(blocks):
                    o_g, m_g, l_g = _unstack(os_[u]), _unstack_columns(ms[u]), _unstack_columns(ls[u])
                    r = row_blocks[own]
                    if first_pattern:
                        m_new, l_new, acc = m_g, l_g, o_g
                    else:
                        m_old = m_ref[r, :]
                        m_new = jnp.maximum(m_old, m_g)
                        alpha, beta = jnp.exp(m_old - m_new), jnp.exp(m_g - m_new)
                        l_new = l_scr[r, :] * alpha + l_g * beta
                        acc = y_ref[r, :] * alpha + o_g * beta
                    if last_pattern:
                        y_ref[r, :] = acc / l_new
                        m_ref[r, :] = m_new + jnp.log(l_new)
                    else:
                        y_ref[r, :] = acc
                        m_ref[r, :] = m_new
                        l_scr[r, :] = l_new
                return carry

            lax.fori_loop(0, d * nb // U, step, 0)

    col = pl.BlockSpec((S, 2 * HEAD), lambda j: (0, j))
    return pl.pallas_call(
        body, name="attn_fwd", grid=(q.shape[1] // (2 * HEAD),),
        in_specs=[col, col, col], out_specs=[col, col],
        out_shape=[jax.ShapeDtypeStruct(q.shape, F32)] * 2,
        scratch_shapes=[pltpu.VMEM((S, 2 * HEAD), F32), pltpu.VMEM((2, N_BACK, 2 * N_BACK), F32)],
        compiler_params=_params("parallel"),
    )(q, k, v)


def _attn_bwd(q, k, v, dy, lse, delta, after):
    S = q.shape[0]
    U = BWD_BLOCKS_PER_STEP

    def body(q_ref, k_ref, v_ref, dy_ref, lse_ref, delta_ref, after_ref, dq_ref, dk_ref, dv_ref, bias):
        _fill_band_bias(bias)
        nb_first = S // PATTERN_ORDER[0] // N_BACK
        first_writes_all = min(U, nb_first) == nb_first
        if not first_writes_all:
            dk_ref[...] = jnp.zeros_like(dk_ref)
            dv_ref[...] = jnp.zeros_like(dv_ref)
        for g, d in enumerate(PATTERN_ORDER):
            nb = S // d // N_BACK

            def step(i, carry, d=d, nb=nb, g=g):
                row_blocks, blocks = _step_blocks(i, U, nb, d)
                kb = [k_ref[r, :].astype(BF16) for r in row_blocks]
                vb = [v_ref[r, :].astype(BF16) for r in row_blocks]
                kws = [jnp.concatenate([kb[before], kb[own]], 0) for before, own, _ in blocks]
                vws = [jnp.concatenate([vb[before], vb[own]], 0) for before, own, _ in blocks]
                qss = [_stack_heads(q_ref[row_blocks[own], :].astype(BF16)) for _, own, _ in blocks]
                doss = [_stack_heads(dy_ref[row_blocks[own], :].astype(BF16)) for _, own, _ in blocks]
                ss = [lax.dot_general(qss[u], kws[u], NT, preferred_element_type=F32) for u in range(U)]
                dps = [lax.dot_general(doss[u], vws[u], NT, preferred_element_type=F32) for u in range(U)]
                pbs, dss = [], []
                for u, (_, own, which) in enumerate(blocks):
                    lse_c = _head_columns(lse_ref[row_blocks[own], :])
                    delta_c = _head_columns(delta_ref[row_blocks[own], :])
                    p_parts, ds_parts = [], []
                    for r0 in range(0, 2 * N_BACK, BWD_CHUNK):
                        r = slice(r0, r0 + BWD_CHUNK)
                        mask = bias[which, r0 % N_BACK:r0 % N_BACK + BWD_CHUNK, :]
                        p_r = jnp.exp(ss[u][r] + mask - lse_c[r])
                        p_parts.append(p_r.astype(BF16))
                        ds_parts.append((p_r * (dps[u][r] - delta_c[r])).astype(BF16))
                    pbs.append(jnp.concatenate(p_parts, axis=0))
                    dss.append(jnp.concatenate(ds_parts, axis=0))
                dqs = [jnp.dot(dss[u], kws[u], preferred_element_type=F32) for u in range(U)]
                dkws = [lax.dot_general(dss[u], qss[u], TN, preferred_element_type=F32) for u in range(U)]
                dvws = [lax.dot_general(pbs[u], doss[u], TN, preferred_element_type=F32) for u in range(U)]
                dk_parts, dv_parts = [None] * len(row_blocks), [None] * len(row_blocks)
                for u, (before, own, _) in enumerate(blocks):
                    dq = _unstack(dqs[u])
                    if g == 0:
                        dq_ref[row_blocks[own], :] = dq
                    else:
                        dq_ref[row_blocks[own], :] += dq
                    for idx, dkp, dvp in ((before, dkws[u][:N_BACK], dvws[u][:N_BACK]),
                                          (own, dkws[u][N_BACK:], dvws[u][N_BACK:])):
                        dk_parts[idx] = dkp if dk_parts[idx] is None else dk_parts[idx] + dkp
                        dv_parts[idx] = dvp if dv_parts[idx] is None else dv_parts[idx] + dvp
                for idx, r in enumerate(row_blocks):
                    if g == 0 and first_writes_all:
                        dk_ref[r, :] = dk_parts[idx]
                        dv_ref[r, :] = dv_parts[idx]
                    else:
                        dk_ref[r, :] += dk_parts[idx]
                        dv_ref[r, :] += dv_parts[idx]
                return carry

            lax.fori_loop(0, d * nb // U, step, 0)

    col = pl.BlockSpec((S, 2 * HEAD), lambda j: (0, j))
    return pl.pallas_call(
        body, name="attn_bwd", grid=(q.shape[1] // (2 * HEAD),),
        in_specs=[col] * 6 + [pl.BlockSpec(memory_space=pl.ANY)], out_specs=[col] * 3,
        out_shape=[jax.ShapeDtypeStruct(q.shape, F32)] * 3,
        scratch_shapes=[pltpu.VMEM((2, N_BACK, 2 * N_BACK), F32)],
        compiler_params=_params("parallel"),
    )(q, k, v, dy, lse, delta, after)


def _shift_down(z, before, k):
    row = lax.broadcasted_iota(jnp.int32, z.shape, 0)
    out = pltpu.roll(z, k, 0)
    for i in range(k):
        out = jnp.where(row == i, before[8 - k + i:8 - k + i + 1, :], out)
    return out


def _shift_up(z, after, k):
    rows = z.shape[0]
    row = lax.broadcasted_iota(jnp.int32, z.shape, 0)
    out = pltpu.roll(z, rows - k, 0)
    for i in range(k):
        out = jnp.where(row == rows - k + i, after[i:i + 1, :], out)
    return out


def _conv_fwd(bcu, before, is_first, w):
    b, c, u = bcu[:, 0:CONV_W], bcu[:, CONV_W:2 * CONV_W], bcu[:, 2 * CONV_W:3 * CONV_W]
    z = c * u
    zb = jnp.where(is_first, 0.0, before[:, CONV_W:2 * CONV_W] * before[:, 2 * CONV_W:3 * CONV_W])
    z1, z2 = _shift_down(z, zb, 1), _shift_down(z, zb, 2)
    cv = w[0:1, :] * z2 + w[1:2, :] * z1 + w[2:3, :] * z
    return b, c, u, z, z1, z2, cv


def _halo_before(tm, width):
    return pl.BlockSpec((8, width), lambda i: (jnp.maximum(i * (tm // 8) - 1, 0), 0))


def _mix_fwd(ya, bcu, qx, mkv, conv_w, g_a, g_c, g_x, w_out, g_post, x, tm):
    S = x.shape[0]

    def body(ya_ref, bcu_ref, before_ref, qx_ref, mkv_ref, cw_ref, ga_ref, gc_ref, gx_ref,
             wo_ref, gp_ref, x_ref, yx_ref, ycat_ref, y2_ref, x1_ref):
        ya = ya_ref[...]
        b, _, _, _, _, _, cv = _conv_fwd(bcu_ref[...], before_ref[...], pl.program_id(0) == 0, cw_ref[...])
        yc = b * cv

        qxb, mkvb = qx_ref[...], mkv_ref[...]
        heads = [slice(HEAD * hd, HEAD * (hd + 1)) for hd in range(XATTN_W // HEAD)]
        ss = [lax.dot_general(qxb[:, sl], mkvb[:, sl], NT, preferred_element_type=F32) * SCALE for sl in heads]
        ms = [jnp.max(s, axis=1, keepdims=True) for s in ss]
        ps = [jnp.exp(s - m) for s, m in zip(ss, ms)]
        ls = [jnp.sum(p, axis=1, keepdims=True) for p in ps]
        os_ = [jnp.dot(p.astype(BF16), mkvb[:, XATTN_W + sl.start:XATTN_W + sl.stop], preferred_element_type=F32)
               for p, sl in zip(ps, heads)]
        for sl, o, l in zip(heads, os_, ls):
            yx_ref[:, sl] = o / l
        yx = yx_ref[...]

        ycat_ref[:, 0:ATTN_W] = (_rms_hat(ya)[0] * ga_ref[...]).astype(BF16)
        ycat_ref[:, ATTN_W:ATTN_W + CONV_W] = (_rms_hat(yc)[0] * gc_ref[...]).astype(BF16)
        ycat_ref[:, ATTN_W + CONV_W:D_MODEL] = (_rms_hat(yx)[0] * gx_ref[...]).astype(BF16)
        y2 = jnp.dot(ycat_ref[...], wo_ref[...], preferred_element_type=F32)
        y2_ref[...] = y2
        x1_ref[...] = x_ref[...] + _rms_hat(y2)[0] * gp_ref[...]

    n_mem = mkv.shape[0]
    return pl.pallas_call(
        body, name="mix_fwd", grid=(S // tm,),
        in_specs=[_rows(tm, ATTN_W), _rows(tm, 3 * CONV_W), _halo_before(tm, 3 * CONV_W), _rows(tm, XATTN_W),
                  _resident((n_mem, 2 * XATTN_W)), _resident((3, CONV_W)), _resident((1, ATTN_W)),
                  _resident((1, CONV_W)), _resident((1, XATTN_W)), _resident((D_MODEL, D_MODEL)),
                  _resident((1, D_MODEL)), _rows(tm, D_MODEL)],
        out_specs=[_rows(tm, XATTN_W), _rows(tm, D_MODEL), _rows(tm, D_MODEL), _rows(tm, D_MODEL)],
        out_shape=[jax.ShapeDtypeStruct((S, XATTN_W), F32), jax.ShapeDtypeStruct((S, D_MODEL), BF16),
                   jax.ShapeDtypeStruct((S, D_MODEL), F32), jax.ShapeDtypeStruct((S, D_MODEL), F32)],
        compiler_params=_params("parallel"),
    )(ya, bcu, bcu, qx, mkv, conv_w, g_a, g_c, g_x, w_out, g_post, x)


def _mlp_fwd_bwd(x1, target, g_pre, g_post, w_up, w_down, tm):
    S = x1.shape[0]
    n_ff = D_FF // SHARD_FF

    def body(x1_ref, t_ref, gpre_ref, gpost_ref, wup_ref, wdn_ref,
             h2_ref, f_ref, du_ref, df2_ref, dx1_ref, dgpre_ref, dgpost_ref, loss_ref, u_scr):
        @pl.when(pl.program_id(0) == 0)
        def _():
            dgpre_ref[...] = jnp.zeros_like(dgpre_ref)
            dgpost_ref[...] = jnp.zeros_like(dgpost_ref)
            loss_ref[...] = jnp.zeros_like(loss_ref)

        x1 = x1_ref[...]
        x1hat, r1 = _rms_hat(x1)
        h2 = (x1hat * gpre_ref[...]).astype(BF16)
        h2_ref[...] = h2
        f2 = jnp.zeros((tm, D_MODEL), F32)
        for j in range(n_ff):
            cols = slice(SHARD_FF * j, SHARD_FF * (j + 1))
            u = jnp.maximum(jnp.dot(h2, wup_ref[j], preferred_element_type=F32), 0.0)
            u_scr[:, cols] = u
            f = (u * u).astype(BF16)
            f_ref[:, cols] = f
            f2 = f2 + jnp.dot(f, wdn_ref[cols, :], preferred_element_type=F32)
        f2hat, r2 = _rms_hat(f2)
        err = x1 + f2hat * gpost_ref[...] - t_ref[...]
        loss_ref[...] += 0.5 * jnp.sum(jnp.mean(err * err, axis=-1, keepdims=True), axis=0, keepdims=True)
        dx2 = err * (1.0 / D_MODEL)
        dgpost_ref[...] += jnp.sum(dx2 * f2hat, axis=0, keepdims=True)
        df2 = _rms_bwd(f2hat, r2, gpost_ref[...], dx2).astype(BF16)
        df2_ref[...] = df2
        dh2 = jnp.zeros((tm, D_MODEL), F32)
        for j in range(n_ff):
            cols = slice(SHARD_FF * j, SHARD_FF * (j + 1))
            df = lax.dot_general(df2, wdn_ref[cols, :], NT, preferred_element_type=F32)
            du = (2.0 * u_scr[:, cols] * df).astype(BF16)
            du_ref[:, cols] = du
            dh2 = dh2 + lax.dot_general(du, wup_ref[j], NT, preferred_element_type=F32)
        dgpre_ref[...] += jnp.sum(dh2 * x1hat, axis=0, keepdims=True)
        dx1_ref[...] = dx2 + _rms_bwd(x1hat, r1, gpre_ref[...], dh2)

    acc = pl.BlockSpec((1, D_MODEL), lambda i: (0, 0))
    return pl.pallas_call(
        body, name="mlp_fwd_bwd", grid=(S // tm,),
        in_specs=[_rows(tm, D_MODEL), _rows(tm, D_MODEL), _resident((1, D_MODEL)), _resident((1, D_MODEL)),
                  _resident((n_ff, D_MODEL, SHARD_FF)), _resident((D_FF, D_MODEL))],
        out_specs=[_rows(tm, D_MODEL), _rows(tm, D_FF), _rows(tm, D_FF), _rows(tm, D_MODEL), _rows(tm, D_MODEL),
                   acc, acc, pl.BlockSpec((1, 1), lambda i: (0, 0))],
        out_shape=[jax.ShapeDtypeStruct((S, D_MODEL), BF16), jax.ShapeDtypeStruct((S, D_FF), BF16),
                   jax.ShapeDtypeStruct((S, D_FF), BF16), jax.ShapeDtypeStruct((S, D_MODEL), BF16),
                   jax.ShapeDtypeStruct((S, D_MODEL), F32), jax.ShapeDtypeStruct((1, D_MODEL), F32),
                   jax.ShapeDtypeStruct((1, D_MODEL), F32), jax.ShapeDtypeStruct((1, 1), F32)],
        scratch_shapes=[pltpu.VMEM((tm, D_FF), F32)],
        compiler_params=_params("arbitrary"),
    )(x1, target, g_pre, g_post, w_up, w_down)


def _weight_grad(name, a, b, rows_sharded, after):
    S, K = a.shape
    N = b.shape[1]
    if rows_sharded:
        tk, tn = K // N_CHIPS, N
        a_spec = pl.BlockSpec((S, tk), lambda j: (0, j))
        b_spec = pl.BlockSpec((S, tn), lambda j: (0, 0), pipeline_mode=pl.Buffered(1))
    else:
        tk, tn = K, N // N_CHIPS
        a_spec = pl.BlockSpec((S, tk), lambda j: (0, 0), pipeline_mode=pl.Buffered(1))
        b_spec = pl.BlockSpec((S, tn), lambda j: (0, j))
    half = tk // 2

    def body(a_ref, b_ref, after_ref, o_ref):
        res = lax.dot_general(a_ref[...], b_ref[...], TN, preferred_element_type=F32)
        o_ref[0, 0] = res[:half]
        o_ref[1, 0] = res[half:]

    return pl.pallas_call(
        body, name=name, grid=(N_CHIPS,), in_specs=[a_spec, b_spec, pl.BlockSpec(memory_space=pl.ANY)],
        out_specs=pl.BlockSpec((2, 1, half, tn), lambda j: (0, j, 0, 0)),
        out_shape=jax.ShapeDtypeStruct((2, N_CHIPS, half, tn), F32),
        compiler_params=_params("parallel"),
    )(a, b, after)


def _weight_grad_w_in(h, dproj):
    S, K = h.shape
    step_w = 2 * 256
    n_steps = PROJ_W // step_w
    half = K // 2

    def body(a_ref, b_ref, o_ref):
        res = lax.dot_general(a_ref[...], b_ref[...], TN, preferred_element_type=F32)
        for step in range(n_steps):
            @pl.when(pl.program_id(0) == step)
            def _(step=step):
                lo = step * step_w
                while lo < (step + 1) * step_w:
                    chip = lo // SHARD_IN
                    hi = min((step + 1) * step_w, (chip + 1) * SHARD_IN)
                    for hh in range(2):
                        o_ref[hh, chip, :, lo - chip * SHARD_IN:hi - chip * SHARD_IN] = (
                            res[half * hh:half * (hh + 1), lo - step * step_w:hi - step * step_w])
                    lo = hi

    return pl.pallas_call(
        body, name="grad_w_in", grid=(n_steps,),
        in_specs=[pl.BlockSpec((S, K), lambda j: (0, 0), pipeline_mode=pl.Buffered(1)),
                  pl.BlockSpec((S, step_w), lambda j: (0, j))],
        out_specs=pl.BlockSpec((2, N_CHIPS, half, SHARD_IN), lambda j: (0, 0, 0, 0)),
        out_shape=jax.ShapeDtypeStruct((2, N_CHIPS, half, SHARD_IN), F32),
        compiler_params=_params("arbitrary"),
    )(h, dproj)


def _mixer_bwd(dx1, y2, ya, yx, bcu, qx, mkv, conv_w, g_a, g_c, g_x, w_out, g_post, after, tm):
    S = dx1.shape[0]
    n_mem = mkv.shape[0]
    n_tiles = S // tm

    def body(dx1_ref, y2_ref, ya_ref, yx_ref, bcu_ref, before_ref, qx_ref, mkv_ref, cw_ref, ga_ref, gc_ref, gx_ref,
             wo_ref, gp_ref, after_ref, dy2_ref, dya_ref, delta_ref, tail_ref, dmkv_ref, dcw_ref, dgp_ref, dga_ref,
             dgc_ref, dgx_ref, carry):
        step = pl.program_id(0)
        first_tile = step == n_tiles - 1

        @pl.when(step == 0)
        def _():
            for ref in (dmkv_ref, dcw_ref, dgp_ref, dga_ref, dgc_ref, dgx_ref, carry):
                ref[...] = jnp.zeros_like(ref)

        dx1 = dx1_ref[...]
        y2hat, r2 = _rms_hat(y2_ref[...])
        dgp_ref[...] += jnp.sum(dx1 * y2hat, axis=0, keepdims=True)
        dy2 = _rms_bwd(y2hat, r2, gp_ref[...], dx1).astype(BF16)
        dy2_ref[...] = dy2
        dycat = lax.dot_general(dy2, wo_ref[...], NT, preferred_element_type=F32)

        d_na = dycat[:, 0:ATTN_W]
        ya = ya_ref[...]
        yahat, ra = _rms_hat(ya)
        dga_ref[...] += jnp.sum(d_na * yahat, axis=0, keepdims=True)
        dya = _rms_bwd(yahat, ra, ga_ref[...], d_na)
        dya_ref[...] = dya
        prod = dya * ya
        hi = prod.astype(BF16)
        lo = (prod - hi.astype(F32)).astype(BF16)
        head_of = lambda axis: lax.shift_right_logical(lax.broadcasted_iota(jnp.int32, (ATTN_W, ATTN_W), axis),
                                                       HEAD.bit_length() - 1)
        ones = jnp.where(head_of(0) == head_of(1), 1.0, 0.0).astype(BF16)
        delta_ref[...] = jnp.dot(hi, ones, preferred_element_type=F32) + jnp.dot(lo, ones, preferred_element_type=F32)

        w = cw_ref[...]
        b, c, u, z, z1, z2, cv = _conv_fwd(bcu_ref[...], before_ref[...], first_tile, w)
        d_nc = dycat[:, ATTN_W:ATTN_W + CONV_W]
        ychat, rc = _rms_hat(b * cv)
        dgc_ref[...] += jnp.sum(d_nc * ychat, axis=0, keepdims=True)
        dyc = _rms_bwd(ychat, rc, gc_ref[...], d_nc)
        dcv = dyc * b
        behind = carry[...]
        dz = w[2:3, :] * dcv + w[1:2, :] * _shift_up(dcv, behind, 1) + w[0:1, :] * _shift_up(dcv, behind, 2)
        carry[...] = dcv[0:8, :]
        dcw_ref[0:1, :] += jnp.sum(dcv * z2, axis=0, keepdims=True)
        dcw_ref[1:2, :] += jnp.sum(dcv * z1, axis=0, keepdims=True)
        dcw_ref[2:3, :] += jnp.sum(dcv * z, axis=0, keepdims=True)
        tail_ref[:, 0:CONV_W] = (dyc * cv).astype(BF16)
        tail_ref[:, CONV_W:2 * CONV_W] = (dz * u).astype(BF16)
        tail_ref[:, 2 * CONV_W:3 * CONV_W] = (dz * c).astype(BF16)

        d_nx = dycat[:, ATTN_W + CONV_W:D_MODEL]
        yxhat, rx = _rms_hat(yx_ref[...])
        dgx_ref[...] += jnp.sum(d_nx * yxhat, axis=0, keepdims=True)
        dyx = _rms_bwd(yxhat, rx, gx_ref[...], d_nx)
        qxb, mkvb = qx_ref[...], mkv_ref[...]
        heads = [slice(HEAD * hd, HEAD * (hd + 1)) for hd in range(XATTN_W // HEAD)]
        values = [slice(XATTN_W + sl.start, XATTN_W + sl.stop) for sl in heads]
        ss = [lax.dot_general(qxb[:, sl], mkvb[:, sl], NT, preferred_element_type=F32) * SCALE for sl in heads]
        es = [jnp.exp(s - jnp.max(s, axis=1, keepdims=True)) for s in ss]
        ps = [e / jnp.sum(e, axis=1, keepdims=True) for e in es]
        dobs = [dyx[:, sl].astype(BF16) for sl in heads]
        dps = [lax.dot_general(dob, mkvb[:, vsl], NT, preferred_element_type=F32) for dob, vsl in zip(dobs, values)]
        dss = [(p * (dp - jnp.sum(p * dp, axis=1, keepdims=True)) * SCALE).astype(BF16) for p, dp in zip(ps, dps)]
        for sl, vsl, p, dob, ds in zip(heads, values, ps, dobs, dss):
            tail_ref[:, 3 * CONV_W + sl.start:3 * CONV_W + sl.stop] = jnp.dot(
                ds, mkvb[:, sl], preferred_element_type=F32).astype(BF16)
            dmkv_ref[:, sl] += lax.dot_general(ds, qxb[:, sl], TN, preferred_element_type=F32)
            dmkv_ref[:, vsl] += lax.dot_general(p.astype(BF16), dob, TN, preferred_element_type=F32)

    rows = lambda width: pl.BlockSpec((tm, width), lambda i: (n_tiles - 1 - i, 0))
    before = pl.BlockSpec((8, 3 * CONV_W), lambda i: (jnp.maximum((n_tiles - 1 - i) * (tm // 8) - 1, 0), 0))
    acc = lambda r, w: pl.BlockSpec((r, w), lambda i: (0, 0))
    return pl.pallas_call(
        body, name="mixer_bwd", grid=(n_tiles,),
        in_specs=[rows(D_MODEL), rows(D_MODEL), rows(ATTN_W), rows(XATTN_W), rows(3 * CONV_W), before, rows(XATTN_W),
                  _resident((n_mem, 2 * XATTN_W)), _resident((3, CONV_W)), _resident((1, ATTN_W)),
                  _resident((1, CONV_W)), _resident((1, XATTN_W)), _resident((D_MODEL, D_MODEL)),
                  _resident((1, D_MODEL)), pl.BlockSpec(memory_space=pl.ANY)],
        out_specs=[rows(D_MODEL), rows(ATTN_W), rows(ATTN_W), rows(3 * CONV_W + XATTN_W), acc(n_mem, 2 * XATTN_W),
                   acc(3, CONV_W), acc(1, D_MODEL), acc(1, ATTN_W), acc(1, CONV_W), acc(1, XATTN_W)],
        out_shape=[jax.ShapeDtypeStruct((S, D_MODEL), BF16), jax.ShapeDtypeStruct((S, ATTN_W), F32),
                   jax.ShapeDtypeStruct((S, ATTN_W), F32), jax.ShapeDtypeStruct((S, 3 * CONV_W + XATTN_W), BF16),
                   jax.ShapeDtypeStruct((n_mem, 2 * XATTN_W), F32), jax.ShapeDtypeStruct((3, CONV_W), F32),
                   jax.ShapeDtypeStruct((1, D_MODEL), F32), jax.ShapeDtypeStruct((1, ATTN_W), F32),
                   jax.ShapeDtypeStruct((1, CONV_W), F32), jax.ShapeDtypeStruct((1, XATTN_W), F32)],
        scratch_shapes=[pltpu.VMEM((8, CONV_W), F32)],
        compiler_params=_params("arbitrary"),
    )(dx1, y2, ya, yx, bcu, bcu, qx, mkv, conv_w, g_a, g_c, g_x, w_out, g_post, after)


def _memkv_bwd(mem, g_mem, w_kv, dmkv):
    n_mem = mem.shape[0]
    half = D_MODEL // N_CHIPS // 2

    def body(mem_ref, g_ref, w_ref, d_ref, dw_ref, dg_ref):
        mhat, _ = _rms_hat(mem_ref[...])
        mn = (mhat * g_ref[...]).astype(BF16)
        d = d_ref[...].astype(BF16)
        for k in range(2 * N_CHIPS):
            dw_ref[k % 2, k // 2] = lax.dot_general(mn[:, half * k:half * (k + 1)], d, TN, preferred_element_type=F32)
        dmn = lax.dot_general(d, w_ref[...], NT, preferred_element_type=F32)
        dg_ref[...] = jnp.sum(dmn * mhat, axis=0, keepdims=True)

    return pl.pallas_call(
        body, name="memkv_bwd",
        out_shape=[jax.ShapeDtypeStruct((2, N_CHIPS, half, 2 * XATTN_W), F32), jax.ShapeDtypeStruct((1, D_MODEL), F32)],
        compiler_params=pltpu.CompilerParams(vmem_limit_bytes=VMEM_LIMIT_V7X),
    )(mem, g_mem, w_kv, dmkv)


def _in_proj_bwd(dqkv, tail, cos, sin, w_in, x, g, dx1, after, tm):
    S = x.shape[0]

    def body(dq_ref, dk_ref, dv_ref, tail_ref, cos_ref, sin_ref, w_hbm, x_ref, g_ref, dx1_ref, after_ref,
             dproj_ref, dx_ref, dg_ref, w_full, sems):
        _side_by_side(w_hbm, w_full, sems)

        @pl.when(pl.program_id(0) == 0)
        def _():
            dg_ref[...] = jnp.zeros_like(dg_ref)

        halves = [slice(0, tm // 2), slice(tm // 2, tm)]
        for rows in halves:
            c, s = cos_ref[rows, :], sin_ref[rows, :]
            for j in range(ATTN_W // 128):
                cols = slice(128 * j, 128 * (j + 1))
                dproj_ref[rows, cols] = _rope128(dq_ref[rows, cols] * SCALE, c, s, True).astype(BF16)
                dproj_ref[rows, ATTN_W + 128 * j:ATTN_W + 128 * (j + 1)] = _rope128(dk_ref[rows, cols], c, s, True).astype(BF16)
            dproj_ref[rows, 2 * ATTN_W:3 * ATTN_W] = dv_ref[rows, :].astype(BF16)
            dproj_ref[rows, 3 * ATTN_W:PROJ_W] = tail_ref[rows, :]
        dhs = [lax.dot_general(dproj_ref[rows, :], w_full[...], NT, preferred_element_type=F32) for rows in halves]
        for rows, dh in zip(halves, dhs):
            xhat, r = _rms_hat(x_ref[rows, :])
            dg_ref[...] += jnp.sum(dh * xhat, axis=0, keepdims=True)
            dx_ref[rows, :] = dx1_ref[rows, :] + _rms_bwd(xhat, r, g_ref[...], dh)

    return pl.pallas_call(
        body, name="in_proj_bwd", grid=(S // tm,),
        in_specs=[_rows(tm, ATTN_W)] * 3 + [_rows(tm, PROJ_W - 3 * ATTN_W), _rows(tm, 128), _rows(tm, 128),
                  pl.BlockSpec(memory_space=pl.ANY), _rows(tm, D_MODEL), _resident((1, D_MODEL)),
                  _rows(tm, D_MODEL), pl.BlockSpec(memory_space=pl.ANY)],
        out_specs=[_rows(tm, PROJ_W), _rows(tm, D_MODEL), pl.BlockSpec((1, D_MODEL), lambda i: (0, 0))],
        out_shape=[jax.ShapeDtypeStruct((S, PROJ_W), BF16), jax.ShapeDtypeStruct((S, D_MODEL), F32),
                   jax.ShapeDtypeStruct((1, D_MODEL), F32)],
        scratch_shapes=[pltpu.VMEM((D_MODEL, PROJ_W), BF16), pltpu.SemaphoreType.DMA((N_CHIPS,))],
        compiler_params=_params("arbitrary"),
    )(*dqkv, tail, cos, sin, w_in, x, g, dx1, after)


def _row_tile(rows):
    return ROW_TILE if rows % ROW_TILE == 0 else rows


def _chip_sums_bf16(name, grads, from_sibling, place):
    k = len(grads)
    _, n, rows, _ = grads[0].shape
    tr = _row_tile(rows)

    def body(place_ref, *refs):
        for g_ref, b_ref, o_ref in zip(refs[:k], refs[k:2 * k], refs[2 * k:]):
            o_ref[...] = (g_ref[0] + b_ref[...]).astype(BF16)

    mine = lambda g: pl.BlockSpec((1, 1, tr, g.shape[3]), lambda s, i, p: (p[0], s, i, 0))
    slab = lambda g: pl.BlockSpec((1, tr, g.shape[3]), lambda s, i, p: (s, i, 0))
    return pl.pallas_call(
        body, name=name, out_shape=[jax.ShapeDtypeStruct(g.shape[1:], BF16) for g in grads],
        grid_spec=pltpu.PrefetchScalarGridSpec(
            num_scalar_prefetch=1, grid=(n, rows // tr),
            in_specs=[mine(g) for g in grads] + [slab(g) for g in grads], out_specs=[slab(g) for g in grads]),
        compiler_params=_params("parallel", "parallel"),
    )(place, *grads, *from_sibling)


def _final_sums(name, grads, from_sibling, others, place):
    k = len(grads)
    rows = grads[0].shape[2]
    tr = _row_tile(rows)

    def body(place_ref, *refs):
        for a in range(k):
            own_ref, sib_ref = refs[a], refs[k + a]
            acc = own_ref[0, 0] + sib_ref[0]
            for o in refs[2 * k + 3 * a:2 * k + 3 * a + 3]:
                acc = acc + o[0].astype(F32)
            refs[5 * k + a][0] = acc

    own = lambda g: pl.BlockSpec((1, 1, tr, g.shape[3]), lambda i, p: (p[0], p[1], i, 0))
    sib = lambda g: pl.BlockSpec((1, tr, g.shape[3]), lambda i, p: (p[1], i, 0))
    other = lambda g, j: pl.BlockSpec((1, tr, g.shape[3]), lambda i, p: (j, i, 0))
    return pl.pallas_call(
        body, name=name, out_shape=[jax.ShapeDtypeStruct((2,) + g.shape[2:], F32) for g in grads],
        grid_spec=pltpu.PrefetchScalarGridSpec(
            num_scalar_prefetch=1, grid=(rows // tr,),
            in_specs=[own(g) for g in grads] + [sib(g) for g in grads] + [other(g, j) for g in grads for j in range(3)],
            out_specs=[pl.BlockSpec((1, tr, g.shape[3]), lambda i, p: (p[0], i, 0)) for g in grads]),
        compiler_params=_params("parallel"),
    )(place, *grads, *from_sibling, *[o for o in others for _ in range(3)])


def _adamw_update(w, g, m, v):
    m = ADAM_B1 * m + (1.0 - ADAM_B1) * g
    v = ADAM_B2 * v + (1.0 - ADAM_B2) * (g * g)
    m_hat = m * (1.0 / (1.0 - ADAM_B1 ** ADAM_STEP))
    v_hat = v * (1.0 / (1.0 - ADAM_B2 ** ADAM_STEP))
    return -ADAM_LR * (m_hat / (jnp.sqrt(v_hat) + ADAM_EPS) + ADAM_WD * w), m, v


def _adamw(name, params, after):
    k = len(params)
    rows = params[0][0].shape[0]
    tr = ADAMW_ROW_TILE if rows % ADAMW_ROW_TILE == 0 else rows

    def body(*refs):
        ins, outs = refs[:4 * k], refs[4 * k + 1:]
        for a in range(k):
            w_ref, g_ref, m_ref, v_ref = ins[4 * a:4 * a + 4]
            g = g_ref[...]
            outs[4 * a][...] = g
            outs[4 * a + 1][...], outs[4 * a + 2][...], outs[4 * a + 3][...] = _adamw_update(w_ref[...], g, m_ref[...], v_ref[...])

    spec = lambda w: pl.BlockSpec((tr, w.shape[1]), lambda i: (i, 0))
    out = pl.pallas_call(
        body, name=name, grid=(rows // tr,),
        in_specs=[spec(p[0]) for p in params for _ in range(4)] + [pl.BlockSpec(memory_space=pl.ANY)],
        out_specs=[spec(p[0]) for p in params for _ in range(4)],
        out_shape=[jax.ShapeDtypeStruct(p[0].shape, F32) for p in params for _ in range(4)],
        compiler_params=_params("parallel"),
    )(*[t for p in params for t in p], after)
    return [out[4 * a:4 * a + 4] for a in range(k)]


def _small_update(summed, chip, gains, gains_m, gains_v, taps, taps_m, taps_v):
    n = len(gains)
    widths = [g.shape[1] for g in gains]
    k, w = taps.shape

    def body(*refs):
        chip_ref, sum_ref = refs[0], refs[1]
        params = [refs[2 + 3 * i:5 + 3 * i] for i in range(n + 1)]
        outs = [refs[2 + 3 * (n + 1) + 4 * i:2 + 3 * (n + 1) + 4 * (i + 1)] for i in range(n + 1)]
        loss_ref = refs[-1]
        for i in range(n):
            g = sum_ref[i:i + 1, 0:widths[i]]
            wr, mr, vr = params[i]
            outs[i][0][...] = g
            outs[i][1][...], outs[i][2][...], outs[i][3][...] = _adamw_update(wr[...], g, mr[...], vr[...])
        g = sum_ref[n:n + k, 0:w]
        for j in range(1, N_CHIPS):
            g = jnp.where(chip_ref[0] == j, sum_ref[n:n + k, w * j:w * (j + 1)], g)
        wr, mr, vr = params[n]
        outs[n][0][...] = g
        outs[n][1][...], outs[n][2][...], outs[n][3][...] = _adamw_update(wr[...], g, mr[...], vr[...])
        loss_ref[...] = sum_ref[n + k:n + k + 1, 0:1]

    vmem = pl.BlockSpec(memory_space=pltpu.VMEM)
    operands = [chip, summed]
    for p in zip(list(gains) + [taps], list(gains_m) + [taps_m], list(gains_v) + [taps_v]):
        operands += list(p)
    shapes = [jax.ShapeDtypeStruct(p.shape, F32) for p in list(gains) + [taps] for _ in range(4)]
    out = pl.pallas_call(
        body, name="small_update", out_shape=shapes + [jax.ShapeDtypeStruct((1, 1), F32)],
        in_specs=[pl.BlockSpec(memory_space=pltpu.SMEM)] + [vmem] * (len(operands) - 1),
        out_specs=[vmem] * (len(shapes) + 1),
    )(*operands)
    return [out[4 * i:4 * (i + 1)] for i in range(n + 1)], out[-1]


def _sum_blocks(name, blocks):
    n, rows, cols = blocks.shape

    def body(b_ref, o_ref):
        acc = b_ref[0]
        for k in range(1, n):
            acc = acc + b_ref[k]
        o_ref[...] = acc

    return pl.pallas_call(body, name=name, out_shape=jax.ShapeDtypeStruct((rows, cols), F32))(blocks)


def _place():
    return lax.axis_index("x"), lax.axis_index("y"), lax.axis_index("c")


def _other_chips(x, y):
    return [(1 - x, y), (x, 1 - y), (1 - x, 1 - y)]


def _allgather_finish(name, shards, landed, pass_on):
    n = len(shards)

    def body(*refs):
        ins, outs, stage = refs[:n], refs[2 * n:3 * n], refs[3 * n:4 * n]
        send_sems, recv_sems, local_sems = refs[4 * n:]
        x, y, c = _place()
        chips = _other_chips(x, y)
        barrier = pltpu.get_barrier_semaphore()
        pl.semaphore_signal(barrier, inc=1, device_id=(x, y, 1 - c), device_id_type=MESH)
        pl.semaphore_wait(barrier, 1)

        def copy(a, k, chip, half):
            place = outs[a].at[2 * chip[0] + chip[1], half]
            return pltpu.make_async_remote_copy(
                src_ref=place, dst_ref=place, send_sem=send_sems.at[3 * a + k], recv_sem=recv_sems.at[3 * a + k],
                device_id=(x, y, 1 - c), device_id_type=MESH)

        load = [pltpu.make_async_copy(ins[a], stage[a], local_sems.at[a]) for a in range(n)]
        local = [pltpu.make_async_copy(stage[a], outs[a].at[2 * x + y], local_sems.at[a]) for a in range(n)]
        for cp in load:
            cp.start()
        passed = [copy(a, k, chip, c) for a in range(n) if pass_on[a] for k, chip in enumerate(chips)]
        for cp in passed:
            cp.start()
        for a in range(n):
            load[a].wait()
            local[a].start()
        for a in range(n):
            if pass_on[a]:
                for k, chip in enumerate(chips):
                    copy(a, k, chip, 1 - c).wait_recv()
        for cp in passed:
            cp.wait_send()
        for cp in local:
            cp.wait()

    any_spec = pl.BlockSpec(memory_space=pl.ANY)
    return pl.pallas_call(
        body, name=name,
        out_shape=[jax.ShapeDtypeStruct((N_CHIPS,) + s.shape, s.dtype) for s in shards],
        in_specs=[any_spec] * (2 * n), out_specs=[any_spec] * n,
        input_output_aliases={n + a: a for a in range(n)},
        scratch_shapes=[pltpu.VMEM(s.shape, s.dtype) for s in shards]
        + [pltpu.SemaphoreType.DMA((3 * n,)), pltpu.SemaphoreType.DMA((3 * n,)), pltpu.SemaphoreType.DMA((n,))],
        compiler_params=pltpu.CompilerParams(vmem_limit_bytes=VMEM_LIMIT_V7X, collective_id=HANDSHAKES["sibling"][0]),
    )(*shards, *landed)


def _plan_first_hop(x, y, c, shards, lands):
    return [(shards[a].at[c], lands[a].at[2 * x + y, c], lands[a].at[2 * chip[0] + chip[1], c], (*chip, c))
            for a in range(len(shards)) for chip in _other_chips(x, y)]


def _plan_pass_on(x, y, c, nothing, lands):
    def place(a, chip, half):
        return lands[a].at[2 * chip[0] + chip[1], half]

    return [(place(a, chip, c), place(a, chip, c), place(a, chip, 1 - c), (x, y, 1 - c))
            for a in range(len(lands)) for chip in _other_chips(x, y)]


def _plan_own_half_to_sibling(x, y, c, nothing, lands):
    return [(lands[a].at[c], lands[a].at[c], lands[a].at[1 - c], (x, y, 1 - c)) for a in range(len(lands))]


def _plan_other_half_to_sibling(x, y, c, grads, lands):
    return [(grads[a].at[1 - c], lands[a], lands[a], (x, y, 1 - c)) for a in range(len(grads))]


def _plan_to_other_chips(x, y, c, partials, lands):
    return [(partials[a].at[2 * chip[0] + chip[1]], lands[a].at[k], lands[a].at[k], (*chip, c))
            for a in range(len(partials)) for k, chip in enumerate(_other_chips(x, y))]


def _plan_to_all(x, y, c, blocks, lands):
    flips = [(fx, fy, fc) for fx in (0, 1) for fy in (0, 1) for fc in (0, 1) if (fx, fy, fc) != (0, 0, 0)]
    peers = [(1 - x if fx else x, 1 - y if fy else y, 1 - c if fc else c) for fx, fy, fc in flips]
    return [(blocks[0], lands[0].at[4 * x + 2 * y + c], lands[0].at[4 * p[0] + 2 * p[1] + p[2]], p) for p in peers]


def _planned_copies(plan, srcs, lands, send_sems, recv_sems):
    x, y, c = _place()

    def pair(k, src, there, here, to):
        make = lambda dst: pltpu.make_async_remote_copy(
            src_ref=src, dst_ref=dst, send_sem=send_sems.at[k], recv_sem=recv_sems.at[k], device_id=to, device_id_type=MESH)
        return make(there), make(here)

    return [pair(k, *entry) for k, entry in enumerate(plan(x, y, c, srcs, lands))]


_HBM_SPEC = pl.BlockSpec(memory_space=pltpu.HBM)
_SEM_SPEC = pl.BlockSpec(memory_space=pltpu.SEMAPHORE)


def _hbm(a):
    return pltpu.with_memory_space_constraint(a, pltpu.HBM)


HANDSHAKES = {
    "sibling": (1, lambda x, y, c: [(x, y, 1 - c)]),
}


def _exchange_start(name, plan, n_copies, srcs, land_shapes, after, lands=None, peers=None):
    if lands is None:
        lands = [lax.empty(s.shape, s.dtype) for s in land_shapes]
    land_shapes = lands
    ns, nl = len(srcs), len(land_shapes)
    n_in = ns + nl + 1
    collective_id, peers_of = HANDSHAKES[peers] if peers else (None, None)

    def body(*refs):
        if peers:
            who = peers_of(*_place())
            barrier = pltpu.get_barrier_semaphore()
            for peer in who:
                pl.semaphore_signal(barrier, inc=1, device_id=peer, device_id_type=MESH)
            pl.semaphore_wait(barrier, len(who))
        for send, _ in _planned_copies(plan, refs[:ns], refs[ns:ns + nl], refs[n_in], refs[n_in + 1]):
            send.start()
        refs[-1][...] = jnp.zeros_like(refs[-1])

    out = pl.pallas_call(
        body, name=name,
        out_shape=(pltpu.SemaphoreType.DMA((n_copies,)), pltpu.SemaphoreType.DMA((n_copies,)),
                   *[pltpu.HBM(s.shape, s.dtype) for s in land_shapes], jax.ShapeDtypeStruct((8, 128), F32)),
        in_specs=[_HBM_SPEC] * (ns + nl) + [pl.BlockSpec(memory_space=pl.ANY)],
        out_specs=(_SEM_SPEC, _SEM_SPEC, *[_HBM_SPEC] * nl, pl.BlockSpec(memory_space=pltpu.VMEM)),
        input_output_aliases={ns + i: 2 + i for i in range(nl)},
        compiler_params=pltpu.CompilerParams(has_side_effects=pltpu.SideEffectType.DATAFLOW_SIDE_EFFECTING,
                                             collective_id=collective_id),
    )(*[_hbm(s) for s in srcs], *[_hbm(l) for l in lands], after)
    return out[0], out[1], list(out[2:2 + nl]), out[-1]


def _exchange_wait(name, plan, srcs, started, after):
    send_sems, recv_sems, lands, _ = started
    ns, nl = len(srcs), len(lands)
    after = list(after) if isinstance(after, (list, tuple)) else [after]

    def body(*refs):
        for send, recv in _planned_copies(plan, refs[:ns], refs[ns:ns + nl], refs[ns + nl], refs[ns + nl + 1]):
            send.wait_send()
            recv.wait_recv()

    return pl.pallas_call(
        body, name=name, out_shape=[pltpu.HBM(l.shape, l.dtype) for l in lands],
        in_specs=[_HBM_SPEC] * (ns + nl) + [_SEM_SPEC, _SEM_SPEC] + [pl.BlockSpec(memory_space=pl.ANY)] * len(after),
        out_specs=[_HBM_SPEC] * nl, input_output_aliases={ns + i: i for i in range(nl)},
        compiler_params=pltpu.CompilerParams(has_side_effects=pltpu.SideEffectType.DATAFLOW_SIDE_EFFECTING),
    )(*[_hbm(s) for s in srcs], *lands, send_sems, recv_sems, *after)


def _like(arrays, lead, dtype=None):
    return [jax.ShapeDtypeStruct(tuple(lead) + a.shape[-2:], dtype or a.dtype) for a in arrays]


class _StepExchanges:
    def __init__(self, mats, conv_w):
        x, y, c = _place()
        self.place = jnp.stack([c, 2 * x + y]).astype(jnp.int32)
        shards = [w.astype(BF16).reshape(2, w.shape[0] // 2, w.shape[1]) for w in mats]
        self._in_shard = shards[:1]
        self._in = _exchange_start("w_in_allgather_start", _plan_first_hop, 3, self._in_shard,
                                   _like(self._in_shard, (N_CHIPS, 2)), shards[0])
        self.zero = self._in[3]
        taps = jnp.pad(conv_w, ((0, 8 - conv_w.shape[0]), (0, 128 - conv_w.shape[1])))
        self._rest_shards = shards[1:] + [jnp.stack([taps, jnp.zeros_like(taps)])]
        self._taps_shape = conv_w.shape
        self._groups = {}

    def w_in(self, after):
        landed = _exchange_wait("w_in_allgather_wait", _plan_first_hop, self._in_shard, self._in,
                                list(after) + self._rest_shards)
        (w_in,) = _allgather_finish("w_in_allgather_finish", self._in_shard, landed, [True])
        self._rest = _exchange_start("rest_allgather_start", _plan_first_hop, 3 * len(self._rest_shards),
                                     self._rest_shards, _like(self._rest_shards, (N_CHIPS, 2)), w_in)
        self.zero = self._rest[3]
        return w_in.reshape(N_CHIPS, 2 * w_in.shape[2], w_in.shape[3])

    def rest_weights(self, after):
        landed = _exchange_wait("rest_allgather_wait", _plan_first_hop, self._rest_shards, self._rest, after)
        kv, out, up, down, taps = _allgather_finish("rest_allgather_finish", self._rest_shards, landed,
                                                    [True, True, False, False, True])
        self._up_down = _exchange_start("up_down_pass_on_start", _plan_pass_on, 6, [], None, self.zero, lands=[up, down],
                                        peers="sibling")
        self.zero = self._up_down[3]
        k, w = self._taps_shape
        taps = taps[:, 0, :k, :w].transpose(1, 0, 2).reshape(k, N_CHIPS * w)
        return [g.reshape(N_CHIPS, 2 * g.shape[2], g.shape[3]) for g in (kv, out)], taps

    def up_down(self, after):
        full = _exchange_wait("up_down_pass_on_wait", _plan_pass_on, [], self._up_down, after)
        return [g.reshape(N_CHIPS, 2 * g.shape[2], g.shape[3]) for g in full]

    def send_grads(self, key, grads):
        grads = list(grads)
        started = _exchange_start(f"{key}_grads_to_sibling_start", _plan_other_half_to_sibling, len(grads), grads,
                                  _like(grads, (N_CHIPS,)), self.zero, peers="sibling")
        self._groups[key] = dict(grads=grads, to_sibling=started)
        self.zero = started[3]

    def grads_at_sibling(self, key, after):
        group = self._groups[key]
        grads = group["grads"]
        group["from_sibling"] = _exchange_wait(f"{key}_grads_to_sibling_wait", _plan_other_half_to_sibling, grads,
                                               group["to_sibling"], after)
        group["partials"] = _chip_sums_bf16(f"{key}_chip_sums", grads, group["from_sibling"], self.place)
        group["to_chips"] = _exchange_start(f"{key}_grads_to_chips_start", _plan_to_other_chips, 3 * len(grads),
                                            group["partials"], _like(group["partials"], (3,)), self.zero)
        self.zero = group["to_chips"][3]

    def grads_summed(self, key, after):
        group = self._groups[key]
        from_chips = _exchange_wait(f"{key}_grads_to_chips_wait", _plan_to_other_chips, group["partials"],
                                    group["to_chips"], after)
        return _final_sums(f"{key}_final_sums", group["grads"], group["from_sibling"], from_chips, self.place)

    def send_sums(self, key, sums):
        self._groups[key + "_sums"] = _exchange_start(f"{key}_sums_to_sibling_start", _plan_own_half_to_sibling,
                                                      len(sums), [], None, self.zero, lands=list(sums),
                                                      peers="sibling")
        self.zero = self._groups[key + "_sums"][3]

    def whole_sums(self, key, after):
        full = _exchange_wait(f"{key}_sums_to_sibling_wait", _plan_own_half_to_sibling, [], self._groups[key + "_sums"], after)
        return [t.reshape(2 * t.shape[1], t.shape[2]) for t in full]

    def send_small(self, block):
        self._small = block
        self._small_started = _exchange_start("small_grads_start", _plan_to_all, 7, [block],
                                              [jax.ShapeDtypeStruct((8,) + block.shape, block.dtype)], self.zero)
        self.zero = self._small_started[3]

    def small_summed(self, after):
        x, y, c = _place()
        (landed,) = _exchange_wait("small_grads_wait", _plan_to_all, [self._small], self._small_started, after)
        blocks = lax.dynamic_update_index_in_dim(landed, self._small, 4 * x + 2 * y + c, 0)
        return _sum_blocks("small_sum", blocks)


def _rope_tables(positions):
    half = HEAD // 2
    inv_freq = jnp.float32(ROPE_THETA) ** (-(jnp.arange(half, dtype=F32) * 2.0 / HEAD))
    ang = positions.astype(F32)[:, None] * inv_freq
    cos, sin = jnp.cos(ang), jnp.sin(ang)
    return jnp.tile(cos, (1, 4)), jnp.tile(jnp.concatenate([-sin, sin], axis=1), (1, 2))


def _local_step(x, mem, positions, target, gains, ex):
    g_pre_mix, g_mem, g_a, g_c, g_x, g_post_mix, g_pre_mlp, g_post_mlp = gains
    tm = ROW_TILE
    cos, sin = _rope_tables(positions)
    h = _pre_norm(x, g_pre_mix, ex.zero, tm)
    w_in = ex.w_in([h, cos, sin])

    q, k, v, bcu, qx = _in_proj_fwd(h, w_in, cos, sin, ex.zero, tm)
    ya, lse = _attn_fwd(q, k, v)
    (w_kv, w_out), conv_w = ex.rest_weights(lse)
    w_kv, w_out = (w.reshape(N_CHIPS * w.shape[1], w.shape[2]) for w in (w_kv, w_out))
    memn, mkv = _memkv_fwd(mem, g_mem, w_kv, ex.zero)
    yx, ycat, y2, x1 = _mix_fwd(ya, bcu, qx, mkv, conv_w, g_a, g_c, g_x, w_out, g_post_mix, x, tm)
    w_up, w_down = ex.up_down(x1)
    w_down = w_down.reshape(N_CHIPS * w_down.shape[1], w_down.shape[2])
    h2, f, du, df2, dx1, dg_pre_mlp, dg_post_mlp, loss = _mlp_fwd_bwd(x1, target, g_pre_mlp, g_post_mlp, w_up, w_down,
                                                                      MLP_ROW_TILE)
    gw_down = _weight_grad("grad_w_down", f, df2, True, ex.zero)
    gw_up = _weight_grad("grad_w_up", h2, du, False, ex.zero)
    ex.send_grads("early", [gw_up, gw_down])

    dy2, dya, delta, tail, dmkv, g_conv, dg_post_mix, dg_a, dg_c, dg_x = _mixer_bwd(
        dx1, y2, ya, yx, bcu, qx, mkv, conv_w, g_a, g_c, g_x, w_out, g_post_mix, ex.zero, tm)
    ex.grads_at_sibling("early", dy2)
    gw_out = _weight_grad("grad_w_out", ycat, dy2, True, ex.zero)
    gw_kv, dg_mem = _memkv_bwd(mem, g_mem, w_kv, dmkv)
    ex.send_grads("mid", [gw_out, gw_kv])
    dqkv = _attn_bwd(q, k, v, dya, lse, delta, ex.zero)
    ex.grads_at_sibling("mid", dqkv[0])
    dproj, grad_x, dg_pre_mix = _in_proj_bwd(dqkv, tail, cos, sin, w_in, x, g_pre_mix, dx1, ex.zero, tm)
    gain_grads = [dg_pre_mix, dg_mem, dg_a, dg_c, dg_x, dg_post_mix, dg_pre_mlp, dg_post_mlp]
    ex.send_small(_pack_small(gain_grads, g_conv, loss))
    gw_in = _weight_grad_w_in(h, dproj)
    ex.send_grads("late", [gw_in])
    return grad_x


def _pack_small(gains, conv, scalar=None):
    rows = [jnp.pad(g, ((0, 0), (0, D_MODEL - g.shape[1]))) for g in gains]
    rows.append(jnp.pad(conv, ((0, 0), (0, D_MODEL - conv.shape[1]))))
    last = jnp.zeros((SMALL_ROWS - 8 - conv.shape[0], D_MODEL), F32)
    rows.append(last if scalar is None else last.at[0:1, 0:1].set(scalar))
    return jnp.concatenate(rows, axis=0)


def kernel(x, mem, positions, g_pre_mix, g_mem, w_in, w_mem_kv, conv_w, g_attn_out, g_conv_out, g_xattn_out, w_out, g_post_mix, g_pre_mlp, w_up, w_down, g_post_mlp, loss_target, m_g_pre_mix, m_g_mem, m_w_in, m_w_mem_kv, m_conv_w, m_g_attn_out, m_g_conv_out, m_g_xattn_out, m_w_out, m_g_post_mix, m_g_pre_mlp, m_w_up, m_w_down, m_g_post_mlp, v_g_pre_mix, v_g_mem, v_w_in, v_w_mem_kv, v_conv_w, v_g_attn_out, v_g_conv_out, v_g_xattn_out, v_w_out, v_g_post_mix, v_g_pre_mlp, v_w_up, v_w_down, v_g_post_mlp):
    chip = 2 * lax.axis_index("x") + lax.axis_index("y")
    gains = [g_pre_mix, g_mem, g_attn_out, g_conv_out, g_xattn_out, g_post_mix, g_pre_mlp, g_post_mlp]
    gains_m = [m_g_pre_mix, m_g_mem, m_g_attn_out, m_g_conv_out, m_g_xattn_out, m_g_post_mix, m_g_pre_mlp, m_g_post_mlp]
    gains_v = [v_g_pre_mix, v_g_mem, v_g_attn_out, v_g_conv_out, v_g_xattn_out, v_g_post_mix, v_g_pre_mlp, v_g_post_mlp]
    mats =[w_in[0], w_mem_kv[0], w_out[0], w_up[0], w_down[0]]
    mats_m = [m_w_in[0], m_w_mem_kv[0], m_w_out[0], m_w_up[0], m_w_down[0]]
    mats_v = [v_w_in[0], v_w_mem_kv[0], v_w_out[0], v_w_up[0], v_w_down[0]]

    ex = _StepExchanges(mats, conv_w[0])
    grad_x = _local_step(x[0], mem[0], positions[0], loss_target[0], gains, ex)

    ex.send_sums("four", ex.grads_summed("early", ex.zero) + ex.grads_summed("mid", ex.zero))
    ex.grads_at_sibling("late", ex.zero)
    up_sum, down_sum, out_sum, kv_sum = ex.whole_sums("four", ex.zero)
    params = lambda a, g: (mats[a], g, mats_m[a], mats_v[a])
    new_up, new_down = _adamw("adamw_up_down", [params(3, up_sum), params(4, down_sum)], ex.zero)
    new_out, new_kv = _adamw("adamw_out_kv", [params(2, out_sum), params(1, kv_sum)], ex.zero)

    small, total = _small_update(ex.small_summed(new_kv[1]), chip.reshape(1).astype(jnp.int32), gains, gains_m,
                                 gains_v, conv_w[0], m_conv_w[0], v_conv_w[0])

    ex.send_sums("last", ex.grads_summed("late", small[0][1]))
    (in_sum,) = ex.whole_sums("last", ex.zero)
    (new_in,) = _adamw("adamw_in", [params(0, in_sum)], in_sum)
    mat_new = [new_in, new_kv, new_out, new_up, new_down]

    order = ["g_pre_mix", "g_mem", "w_in", "w_mem_kv", "conv_w", "g_attn_out", "g_conv_out", "g_xattn_out", "w_out",
             "g_post_mix", "g_pre_mlp", "w_up", "w_down", "g_post_mlp"]
    gain_names = ["g_pre_mix", "g_mem", "g_attn_out", "g_conv_out", "g_xattn_out", "g_post_mix", "g_pre_mlp", "g_post_mlp"]
    mat_names = ["w_in", "w_mem_kv", "w_out", "w_up", "w_down"]

    def leaf(kind, name):
        if name in gain_names:
            return small[gain_names.index(name)][kind]
        if name == "conv_w":
            return small[len(gain_names)][kind][None]
        return mat_new[mat_names.index(name)][kind][None]

    return (total[0, 0], grad_x[None], *[leaf(kind, name) for kind in range(4) for name in order])
```

```python
import jax
import jax.numpy as jnp
from jax import lax
from jax.experimental import pallas as pl
from jax.experimental.pallas import tpu as pltpu

F32, BF16 = jnp.float32, jnp.bfloat16

D_MODEL = 1024
ATTN_W = 512
CONV_W = 256
XATTN_W = 256
PROJ_W = 3 * ATTN_W + 3 * CONV_W + XATTN_W
D_FF = 4096
HEAD = 64
N_BACK = 128
DILATIONS = (1, 4, 16)
PATTERN_ORDER = DILATIONS[::-1]
ROPE_THETA = 10000.0
EPS = 1e-6
NEG_INF = -1e30
SCALE = HEAD ** -0.5
N_CHIPS = 4
SHARD_IN = PROJ_W // N_CHIPS
SHARD_FF = D_FF // N_CHIPS

ADAM_LR, ADAM_B1, ADAM_B2, ADAM_EPS, ADAM_WD, ADAM_STEP = 0.001, 0.9, 0.999, 1e-08, 0.01, 10

VMEM_LIMIT_V7X = 56 * 1024 * 1024
ROW_TILE = 512
MLP_ROW_TILE = 256
ADAMW_ROW_TILE = 256
SMALL_ROWS = 16

NT = (((1,), (1,)), ((), ()))
TN = (((0,), (0,)), ((), ()))
MESH = pl.DeviceIdType.MESH


def _params(*sem):
    return pltpu.CompilerParams(dimension_semantics=sem, vmem_limit_bytes=VMEM_LIMIT_V7X)


def _resident(shape):
    return pl.BlockSpec(shape, lambda *_: (0,) * len(shape), pipeline_mode=pl.Buffered(1))


def _rows(tm, width):
    return pl.BlockSpec((tm, width), lambda i: (i, 0))


def _rms_hat(x):
    r = lax.rsqrt(jnp.mean(x * x, axis=-1, keepdims=True) + EPS)
    return x * r, r


def _rms_bwd(xhat, r, g, dy):
    gdy = dy * g
    return r * (gdy - xhat * jnp.mean(xhat * gdy, axis=-1, keepdims=True))


def _rope128(t, cos, sin_signed, inverse):
    lane = lax.broadcasted_iota(jnp.int32, t.shape, 1)
    first_half = (lane % HEAD) < (HEAD // 2)
    rot = jnp.where(first_half, pltpu.roll(t, 128 - HEAD // 2, 1), pltpu.roll(t, HEAD // 2, 1))
    return t * cos - rot * sin_signed if inverse else t * cos + rot * sin_signed


def _pre_norm(x, g, after, tm):
    S = x.shape[0]

    def body(x_ref, g_ref, after_ref, h_ref):
        h_ref[...] = (_rms_hat(x_ref[...])[0] * g_ref[...]).astype(BF16)

    return pl.pallas_call(
        body, name="pre_norm", grid=(S // tm,),
        in_specs=[_rows(tm, D_MODEL), _resident((1, D_MODEL)), pl.BlockSpec(memory_space=pl.ANY)],
        out_specs=_rows(tm, D_MODEL), out_shape=jax.ShapeDtypeStruct((S, D_MODEL), BF16),
        compiler_params=_params("parallel"),
    )(x, g, after)


def _side_by_side(w_hbm, w_full, sems):
    width = w_hbm.shape[2]

    @pl.when(pl.program_id(0) == 0)
    def _():
        copies = [pltpu.make_async_copy(w_hbm.at[j], w_full.at[:, pl.ds(width * j, width)], sems.at[j])
                  for j in range(N_CHIPS)]
        for cp in copies:
            cp.start()
        for cp in copies:
            cp.wait()


def _in_proj_fwd(h, w_in, cos, sin, after, tm):
    S = h.shape[0]

    def body(h_ref, w_hbm, cos_ref, sin_ref, after_ref, q_ref, k_ref, v_ref, bcu_ref, qx_ref, proj, w_full, sems):
        _side_by_side(w_hbm, w_full, sems)
        proj[...] = jnp.dot(h_ref[...], w_full[...], preferred_element_type=F32)
        c, s = cos_ref[...], sin_ref[...]
        for j in range(ATTN_W // 128):
            lo = 128 * j
            q_ref[:, lo:lo + 128] = _rope128(proj[:, lo:lo + 128], c, s, False) * SCALE
            k_ref[:, lo:lo + 128] = _rope128(proj[:, ATTN_W + lo:ATTN_W + lo + 128], c, s, False)
        v_ref[...] = proj[:, 2 * ATTN_W:3 * ATTN_W]
        bcu_ref[...] = proj[:, 3 * ATTN_W:3 * ATTN_W + 3 * CONV_W]
        qx_ref[...] = proj[:, 3 * ATTN_W + 3 * CONV_W:PROJ_W].astype(BF16)

    return pl.pallas_call(
        body, name="in_proj_fwd", grid=(S // tm,),
        in_specs=[_rows(tm, D_MODEL), pl.BlockSpec(memory_space=pl.ANY), _rows(tm, 128), _rows(tm, 128),
                  pl.BlockSpec(memory_space=pl.ANY)],
        out_specs=[_rows(tm, ATTN_W), _rows(tm, ATTN_W), _rows(tm, ATTN_W), _rows(tm, 3 * CONV_W), _rows(tm, XATTN_W)],
        out_shape=[jax.ShapeDtypeStruct((S, ATTN_W), F32), jax.ShapeDtypeStruct((S, ATTN_W), F32),
                   jax.ShapeDtypeStruct((S, ATTN_W), F32), jax.ShapeDtypeStruct((S, 3 * CONV_W), F32),
                   jax.ShapeDtypeStruct((S, XATTN_W), BF16)],
        scratch_shapes=[pltpu.VMEM((tm, PROJ_W), F32), pltpu.VMEM((D_MODEL, PROJ_W), BF16),
                        pltpu.SemaphoreType.DMA((N_CHIPS,))],
        compiler_params=_params("arbitrary"),
    )(h, w_in, cos, sin, after)


def _memkv_fwd(mem, g_mem, w_kv, after):
    n_mem = mem.shape[0]

    def body(mem_ref, g_ref, w_ref, after_ref, mn_ref, kv_ref):
        mhat, _ = _rms_hat(mem_ref[...])
        mn = (mhat * g_ref[...]).astype(BF16)
        mn_ref[...] = mn
        kv_ref[...] = jnp.dot(mn, w_ref[...], preferred_element_type=F32).astype(BF16)

    vmem = pl.BlockSpec(memory_space=pltpu.VMEM)
    return pl.pallas_call(
        body, name="memkv_fwd", in_specs=[vmem, vmem, vmem, pl.BlockSpec(memory_space=pl.ANY)], out_specs=[vmem, vmem],
        out_shape=[jax.ShapeDtypeStruct((n_mem, D_MODEL), BF16), jax.ShapeDtypeStruct((n_mem, 2 * XATTN_W), BF16)],
        compiler_params=pltpu.CompilerParams(vmem_limit_bytes=VMEM_LIMIT_V7X),
    )(mem, g_mem, w_kv, after)


def _fill_band_bias(bias):
    row = lax.broadcasted_iota(jnp.int32, (N_BACK, 2 * N_BACK), 0)
    col = lax.broadcasted_iota(jnp.int32, (N_BACK, 2 * N_BACK), 1)
    band = (col >= row) & (col <= row + N_BACK)
    bias[1] = jnp.where(band, 0.0, NEG_INF)
    bias[0] = jnp.where(band & (col >= N_BACK), 0.0, NEG_INF)


def _strided(start, size, d):
    return pl.ds(start, size) if d == 1 else pl.ds(start, size, stride=d)


def _group_starts(g, G, nb, d):
    t0 = g * G
    r, n0 = lax.shift_right_logical(t0, nb.bit_length() - 1), lax.bitwise_and(t0, nb - 1)
    first = r + n0 * (N_BACK * d)
    before = r + jnp.maximum(n0 - 1, 0) * (N_BACK * d)
    starts = [before] + [first + u * (N_BACK * d) for u in range(G)]
    if d == 1:
        starts = [pl.multiple_of(st, N_BACK) for st in starts]
    return starts, n0


def _step_blocks(i, U, nb, d):
    G = min(U, nb)
    whole = G == nb
    row_blocks, blocks = [], []
    for grp in range(U // G):
        starts, n0 = _group_starts(i * (U // G) + grp, G, nb, d)
        base = len(row_blocks)
        if whole:
            row_blocks += [_strided(st, N_BACK, d) for st in starts[1:]]
            blocks += [(base + max(u - 1, 0), base + u, min(u, 1)) for u in range(G)]
        else:
            row_blocks += [_strided(st, N_BACK, d) for st in starts]
            blocks += [(base + u, base + u + 1, jnp.minimum(n0, 1) if u == 0 else 1) for u in range(G)]
    return row_blocks, blocks


def _by_head(a, b):
    lane = lax.broadcasted_iota(jnp.int32, (a.shape[0], 2 * HEAD), 1)
    return jnp.where(lane < HEAD, a, b)


def _head_only(t, hh):
    lane = lax.broadcasted_iota(jnp.int32, t.shape, 1)
    return jnp.where((lane < HEAD) == (hh == 0), t, jnp.zeros_like(t))


def _stack_heads(t):
    return jnp.concatenate([_head_only(t, 0), _head_only(t, 1)], axis=0)


def _head_columns(t):
    return jnp.concatenate([t[:, 0:1], t[:, HEAD:HEAD + 1]], axis=0)


def _unstack(t):
    return _by_head(t[:N_BACK], t[N_BACK:])


def _unstack_columns(t):
    return _by_head(jnp.broadcast_to(t[:N_BACK], (N_BACK, 2 * HEAD)), jnp.broadcast_to(t[N_BACK:], (N_BACK, 2 * HEAD)))


FWD_BLOCKS_PER_STEP = 4
BWD_BLOCKS_PER_STEP = 4
BWD_CHUNK = 64


def _attn_fwd(q, k, v):
    S = q.shape[0]
    U = FWD_BLOCKS_PER_STEP

    def body(q_ref, k_ref, v_ref, y_ref, m_ref, l_scr, bias):
        _fill_band_bias(bias)
        for g, d in enumerate(PATTERN_ORDER):
            nb = S // d // N_BACK
            first_pattern, last_pattern = g == 0, g == len(PATTERN_ORDER) - 1

            def step(i, carry, d=d, nb=nb, first_pattern=first_pattern, last_pattern=last_pattern):
                row_blocks, blocks = _step_blocks(i, U, nb, d)
                kb = [k_ref[r, :].astype(BF16) for r in row_blocks]
                ss = []
                for before, own, which in blocks:
                    kw = jnp.concatenate([kb[before], kb[own]], 0)
                    qs = _stack_heads(q_ref[row_blocks[own], :].astype(BF16))
                    b = bias[which]
                    ss.append(lax.dot_general(qs, kw, NT, preferred_element_type=F32) + jnp.concatenate([b, b], axis=0))
                ms = [jnp.max(s, axis=1, keepdims=True) for s in ss]
                ps = [jnp.exp(s - m) for s, m in zip(ss, ms)]
                ls = [jnp.sum(p, axis=1, keepdims=True) for p in ps]
                vb = [v_ref[r, :].astype(BF16) for r in row_blocks]
                os_ = [jnp.dot(ps[u].astype(BF16), jnp.concatenate([vb[before], vb[own]], 0), preferred_element_type=F32)
                       for u, (before, own, _) in enumerate(blocks)]
                for u, (_, own, _) in enumerate(blocks):
                    o_g, m_g, l_g = _unstack(os_[u]), _unstack_columns(ms[u]), _unstack_columns(ls[u])
                    r = row_blocks[own]
                    if first_pattern:
                        m_new, l_new, acc = m_g, l_g, o_g
                    else:
                        m_old = m_ref[r, :]
                        m_new = jnp.maximum(m_old, m_g)
                        alpha, beta = jnp.exp(m_old - m_new), jnp.exp(m_g - m_new)
                        l_new = l_scr[r, :] * alpha + l_g * beta
                        acc = y_ref[r, :] * alpha + o_g * beta
                    if last_pattern:
                        y_ref[r, :] = acc / l_new
                        m_ref[r, :] = m_new + jnp.log(l_new)
                    else:
                        y_ref[r, :] = acc
                        m_ref[r, :] = m_new
                        l_scr[r, :] = l_new
                return carry

            lax.fori_loop(0, d * nb // U, step, 0)

    col = pl.BlockSpec((S, 2 * HEAD), lambda j: (0, j))
    return pl.pallas_call(
        body, name="attn_fwd", grid=(q.shape[1] // (2 * HEAD),),
        in_specs=[col, col, col], out_specs=[col, col],
        out_shape=[jax.ShapeDtypeStruct(q.shape, F32)] * 2,
        scratch_shapes=[pltpu.VMEM((S, 2 * HEAD), F32), pltpu.VMEM((2, N_BACK, 2 * N_BACK), F32)],
        compiler_params=_params("parallel"),
    )(q, k, v)


def _attn_bwd(q, k, v, dy, lse, delta, after):
    S = q.shape[0]
    U = BWD_BLOCKS_PER_STEP

    def body(q_ref, k_ref, v_ref, dy_ref, lse_ref, delta_ref, after_ref, dq_ref, dk_ref, dv_ref, bias):
        _fill_band_bias(bias)
        nb_first = S // PATTERN_ORDER[0] // N_BACK
        first_writes_all = min(U, nb_first) == nb_first
        if not first_writes_all:
            dk_ref[...] = jnp.zeros_like(dk_ref)
            dv_ref[...] = jnp.zeros_like(dv_ref)
        for g, d in enumerate(PATTERN_ORDER):
            nb = S // d // N_BACK

            def step(i, carry, d=d, nb=nb, g=g):
                row_blocks, blocks = _step_blocks(i, U, nb, d)
                kb = [k_ref[r, :].astype(BF16) for r in row_blocks]
                vb = [v_ref[r, :].astype(BF16) for r in row_blocks]
                kws = [jnp.concatenate([kb[before], kb[own]], 0) for before, own, _ in blocks]
                vws = [jnp.concatenate([vb[before], vb[own]], 0) for before, own, _ in blocks]
                qss = [_stack_heads(q_ref[row_blocks[own], :].astype(BF16)) for _, own, _ in blocks]
                doss = [_stack_heads(dy_ref[row_blocks[own], :].astype(BF16)) for _, own, _ in blocks]
                ss = [lax.dot_general(qss[u], kws[u], NT, preferred_element_type=F32) for u in range(U)]
                dps = [lax.dot_general(doss[u], vws[u], NT, preferred_element_type=F32) for u in range(U)]
                pbs, dss = [], []
                for u, (_, own, which) in enumerate(blocks):
                    lse_c = _head_columns(lse_ref[row_blocks[own], :])
                    delta_c = _head_columns(delta_ref[row_blocks[own], :])
                    p_parts, ds_parts = [], []
                    for r0 in range(0, 2 * N_BACK, BWD_CHUNK):
                        r = slice(r0, r0 + BWD_CHUNK)
                        mask = bias[which, r0 % N_BACK:r0 % N_BACK + BWD_CHUNK, :]
                        p_r = jnp.exp(ss[u][r] + mask - lse_c[r])
                        p_parts.append(p_r.astype(BF16))
                        ds_parts.append((p_r * (dps[u][r] - delta_c[r])).astype(BF16))
                    pbs.append(jnp.concatenate(p_parts, axis=0))
                    dss.append(jnp.concatenate(ds_parts, axis=0))
                dqs = [jnp.dot(dss[u], kws[u], preferred_element_type=F32) for u in range(U)]
                dkws = [lax.dot_general(dss[u], qss[u], TN, preferred_element_type=F32) for u in range(U)]
                dvws = [lax.dot_general(pbs[u], doss[u], TN, preferred_element_type=F32) for u in range(U)]
                dk_parts, dv_parts = [None] * len(row_blocks), [None] * len(row_blocks)
                for u, (before, own, _) in enumerate(blocks):
                    dq = _unstack(dqs[u])
                    if g == 0:
                        dq_ref[row_blocks[own], :] = dq
                    else:
                        dq_ref[row_blocks[own], :] += dq
                    for idx, dkp, dvp in ((before, dkws[u][:N_BACK], dvws[u][:N_BACK]),
                                          (own, dkws[u][N_BACK:], dvws[u][N_BACK:])):
                        dk_parts[idx] = dkp if dk_parts[idx] is None else dk_parts[idx] + dkp
                        dv_parts[idx] = dvp if dv_parts[idx] is None else dv_parts[idx] + dvp
                for idx, r in enumerate(row_blocks):
                    if g == 0 and first_writes_all:
                        dk_ref[r, :] = dk_parts[idx]
                        dv_ref[r, :] = dv_parts[idx]
                    else:
                        dk_ref[r, :] += dk_parts[idx]
                        dv_ref[r, :] += dv_parts[idx]
                return carry

            lax.fori_loop(0, d * nb // U, step, 0)

    col = pl.BlockSpec((S, 2 * HEAD), lambda j: (0, j))
    return pl.pallas_call(
        body, name="attn_bwd", grid=(q.shape[1] // (2 * HEAD),),
        in_specs=[col] * 6 + [pl.BlockSpec(memory_space=pl.ANY)], out_specs=[col] * 3,
        out_shape=[jax.ShapeDtypeStruct(q.shape, F32)] * 3,
        scratch_shapes=[pltpu.VMEM((2, N_BACK, 2 * N_BACK), F32)],
        compiler_params=_params("parallel"),
    )(q, k, v, dy, lse, delta, after)


def _shift_down(z, before, k):
    row = lax.broadcasted_iota(jnp.int32, z.shape, 0)
    out = pltpu.roll(z, k, 0)
    for i in range(k):
        out = jnp.where(row == i, before[8 - k + i:8 - k + i + 1, :], out)
    return out


def _shift_up(z, after, k):
    rows = z.shape[0]
    row = lax.broadcasted_iota(jnp.int32, z.shape, 0)
    out = pltpu.roll(z, rows - k, 0)
    for i in range(k):
        out = jnp.where(row == rows - k + i, after[i:i + 1, :], out)
    return out


def _conv_fwd(bcu, before, is_first, w):
    b, c, u = bcu[:, 0:CONV_W], bcu[:, CONV_W:2 * CONV_W], bcu[:, 2 * CONV_W:3 * CONV_W]
    z = c * u
    zb = jnp.where(is_first, 0.0, before[:, CONV_W:2 * CONV_W] * before[:, 2 * CONV_W:3 * CONV_W])
    z1, z2 = _shift_down(z, zb, 1), _shift_down(z, zb, 2)
    cv = w[0:1, :] * z2 + w[1:2, :] * z1 + w[2:3, :] * z
    return b, c, u, z, z1, z2, cv


def _halo_before(tm, width):
    return pl.BlockSpec((8, width), lambda i: (jnp.maximum(i * (tm // 8) - 1, 0), 0))


def _mix_fwd(ya, bcu, qx, mkv, conv_w, g_a, g_c, g_x, w_out, g_post, x, tm):
    S = x.shape[0]

    def body(ya_ref, bcu_ref, before_ref, qx_ref, mkv_ref, cw_ref, ga_ref, gc_ref, gx_ref,
             wo_ref, gp_ref, x_ref, yx_ref, ycat_ref, y2_ref, x1_ref):
        ya = ya_ref[...]
        b, _, _, _, _, _, cv = _conv_fwd(bcu_ref[...], before_ref[...], pl.program_id(0) == 0, cw_ref[...])
        yc = b * cv

        qxb, mkvb = qx_ref[...], mkv_ref[...]
        heads = [slice(HEAD * hd, HEAD * (hd + 1)) for hd in range(XATTN_W // HEAD)]
        ss = [lax.dot_general(qxb[:, sl], mkvb[:, sl], NT, preferred_element_type=F32) * SCALE for sl in heads]
        ms = [jnp.max(s, axis=1, keepdims=True) for s in ss]
        ps = [jnp.exp(s - m) for s, m in zip(ss, ms)]
        ls = [jnp.sum(p, axis=1, keepdims=True) for p in ps]
        os_ = [jnp.dot(p.astype(BF16), mkvb[:, XATTN_W + sl.start:XATTN_W + sl.stop], preferred_element_type=F32)
               for p, sl in zip(ps, heads)]
        for sl, o, l in zip(heads, os_, ls):
            yx_ref[:, sl] = o / l
        yx = yx_ref[...]

        ycat_ref[:, 0:ATTN_W] = (_rms_hat(ya)[0] * ga_ref[...]).astype(BF16)
        ycat_ref[:, ATTN_W:ATTN_W + CONV_W] = (_rms_hat(yc)[0] * gc_ref[...]).astype(BF16)
        ycat_ref[:, ATTN_W + CONV_W:D_MODEL] = (_rms_hat(yx)[0] * gx_ref[...]).astype(BF16)
        y2 = jnp.dot(ycat_ref[...], wo_ref[...], preferred_element_type=F32)
        y2_ref[...] = y2
        x1_ref[...] = x_ref[...] + _rms_hat(y2)[0] * gp_ref[...]

    n_mem = mkv.shape[0]
    return pl.pallas_call(
        body, name="mix_fwd", grid=(S // tm,),
        in_specs=[_rows(tm, ATTN_W), _rows(tm, 3 * CONV_W), _halo_before(tm, 3 * CONV_W), _rows(tm, XATTN_W),
                  _resident((n_mem, 2 * XATTN_W)), _resident((3, CONV_W)), _resident((1, ATTN_W)),
                  _resident((1, CONV_W)), _resident((1, XATTN_W)), _resident((D_MODEL, D_MODEL)),
                  _resident((1, D_MODEL)), _rows(tm, D_MODEL)],
        out_specs=[_rows(tm, XATTN_W), _rows(tm, D_MODEL), _rows(tm, D_MODEL), _rows(tm, D_MODEL)],
        out_shape=[jax.ShapeDtypeStruct((S, XATTN_W), F32), jax.ShapeDtypeStruct((S, D_MODEL), BF16),
                   jax.ShapeDtypeStruct((S, D_MODEL), F32), jax.ShapeDtypeStruct((S, D_MODEL), F32)],
        compiler_params=_params("parallel"),
    )(ya, bcu, bcu, qx, mkv, conv_w, g_a, g_c, g_x, w_out, g_post, x)


def _mlp_fwd_bwd(x1, target, g_pre, g_post, w_up, w_down, tm):
    S = x1.shape[0]
    n_ff = D_FF // SHARD_FF

    def body(x1_ref, t_ref, gpre_ref, gpost_ref, wup_ref, wdn_ref,
             h2_ref, f_ref, du_ref, df2_ref, dx1_ref, dgpre_ref, dgpost_ref, loss_ref, u_scr):
        @pl.when(pl.program_id(0) == 0)
        def _():
            dgpre_ref[...] = jnp.zeros_like(dgpre_ref)
            dgpost_ref[...] = jnp.zeros_like(dgpost_ref)
            loss_ref[...] = jnp.zeros_like(loss_ref)

        x1 = x1_ref[...]
        x1hat, r1 = _rms_hat(x1)
        h2 = (x1hat * gpre_ref[...]).astype(BF16)
        h2_ref[...] = h2
        f2 = jnp.zeros((tm, D_MODEL), F32)
        for j in range(n_ff):
            cols = slice(SHARD_FF * j, SHARD_FF * (j + 1))
            u = jnp.maximum(jnp.dot(h2, wup_ref[j], preferred_element_type=F32), 0.0)
            u_scr[:, cols] = u
            f = (u * u).astype(BF16)
            f_ref[:, cols] = f
            f2 = f2 + jnp.dot(f, wdn_ref[cols, :], preferred_element_type=F32)
        f2hat, r2 = _rms_hat(f2)
        err = x1 + f2hat * gpost_ref[...] - t_ref[...]
        loss_ref[...] += 0.5 * jnp.sum(jnp.mean(err * err, axis=-1, keepdims=True), axis=0, keepdims=True)
        dx2 = err * (1.0 / D_MODEL)
        dgpost_ref[...] += jnp.sum(dx2 * f2hat, axis=0, keepdims=True)
        df2 = _rms_bwd(f2hat, r2, gpost_ref[...], dx2).astype(BF16)
        df2_ref[...] = df2
        dh2 = jnp.zeros((tm, D_MODEL), F32)
        for j in range(n_ff):
            cols = slice(SHARD_FF * j, SHARD_FF * (j + 1))
            df = lax.dot_general(df2, wdn_ref[cols, :], NT, preferred_element_type=F32)
            du = (2.0 * u_scr[:, cols] * df).astype(BF16)
            du_ref[:, cols] = du
            dh2 = dh2 + lax.dot_general(du, wup_ref[j], NT, preferred_element_type=F32)
        dgpre_ref[...] += jnp.sum(dh2 * x1hat, axis=0, keepdims=True)
        dx1_ref[...] = dx2 + _rms_bwd(x1hat, r1, gpre_ref[...], dh2)

    acc = pl.BlockSpec((1, D_MODEL), lambda i: (0, 0))
    return pl.pallas_call(
        body, name="mlp_fwd_bwd", grid=(S // tm,),
        in_specs=[_rows(tm, D_MODEL), _rows(tm, D_MODEL), _resident((1, D_MODEL)), _resident((1, D_MODEL)),
                  _resident((n_ff, D_MODEL, SHARD_FF)), _resident((D_FF, D_MODEL))],
        out_specs=[_rows(tm, D_MODEL), _rows(tm, D_FF), _rows(tm, D_FF), _rows(tm, D_MODEL), _rows(tm, D_MODEL),
                   acc, acc, pl.BlockSpec((1, 1), lambda i: (0, 0))],
        out_shape=[jax.ShapeDtypeStruct((S, D_MODEL), BF16), jax.ShapeDtypeStruct((S, D_FF), BF16),
                   jax.ShapeDtypeStruct((S, D_FF), BF16), jax.ShapeDtypeStruct((S, D_MODEL), BF16),
                   jax.ShapeDtypeStruct((S, D_MODEL), F32), jax.ShapeDtypeStruct((1, D_MODEL), F32),
                   jax.ShapeDtypeStruct((1, D_MODEL), F32), jax.ShapeDtypeStruct((1, 1), F32)],
        scratch_shapes=[pltpu.VMEM((tm, D_FF), F32)],
        compiler_params=_params("arbitrary"),
    )(x1, target, g_pre, g_post, w_up, w_down)


def _weight_grad(name, a, b, rows_sharded, after):
    S, K = a.shape
    N = b.shape[1]
    if rows_sharded:
        tk, tn = K // N_CHIPS, N
        a_spec = pl.BlockSpec((S, tk), lambda j: (0, j))
        b_spec = pl.BlockSpec((S, tn), lambda j: (0, 0), pipeline_mode=pl.Buffered(1))
    else:
        tk, tn = K, N // N_CHIPS
        a_spec = pl.BlockSpec((S, tk), lambda j: (0, 0), pipeline_mode=pl.Buffered(1))
        b_spec = pl.BlockSpec((S, tn), lambda j: (0, j))
    half = tk // 2

    def body(a_ref, b_ref, after_ref, o_ref):
        res = lax.dot_general(a_ref[...], b_ref[...], TN, preferred_element_type=F32)
        o_ref[0, 0] = res[:half]
        o_ref[1, 0] = res[half:]

    return pl.pallas_call(
        body, name=name, grid=(N_CHIPS,), in_specs=[a_spec, b_spec, pl.BlockSpec(memory_space=pl.ANY)],
        out_specs=pl.BlockSpec((2, 1, half, tn), lambda j: (0, j, 0, 0)),
        out_shape=jax.ShapeDtypeStruct((2, N_CHIPS, half, tn), F32),
        compiler_params=_params("parallel"),
    )(a, b, after)


def _weight_grad_w_in(h, dproj):
    S, K = h.shape
    step_w = 2 * 256
    n_steps = PROJ_W // step_w
    half = K // 2

    def body(a_ref, b_ref, o_ref):
        res = lax.dot_general(a_ref[...], b_ref[...], TN, preferred_element_type=F32)
        for step in range(n_steps):
            @pl.when(pl.program_id(0) == step)
            def _(step=step):
                lo = step * step_w
                while lo < (step + 1) * step_w:
                    chip = lo // SHARD_IN
                    hi = min((step + 1) * step_w, (chip + 1) * SHARD_IN)
                    for hh in range(2):
                        o_ref[hh, chip, :, lo - chip * SHARD_IN:hi - chip * SHARD_IN] = (
                            res[half * hh:half * (hh + 1), lo - step * step_w:hi - step * step_w])
                    lo = hi

    return pl.pallas_call(
        body, name="grad_w_in", grid=(n_steps,),
        in_specs=[pl.BlockSpec((S, K), lambda j: (0, 0), pipeline_mode=pl.Buffered(1)),
                  pl.BlockSpec((S, step_w), lambda j: (0, j))],
        out_specs=pl.BlockSpec((2, N_CHIPS, half, SHARD_IN), lambda j: (0, 0, 0, 0)),
        out_shape=jax.ShapeDtypeStruct((2, N_CHIPS, half, SHARD_IN), F32),
        compiler_params=_params("arbitrary"),
    )(h, dproj)


def _mixer_bwd(dx1, y2, ya, yx, bcu, qx, mkv, conv_w, g_a, g_c, g_x, w_out, g_post, after, tm):
    S = dx1.shape[0]
    n_mem = mkv.shape[0]
    n_tiles = S // tm

    def body(dx1_ref, y2_ref, ya_ref, yx_ref, bcu_ref, before_ref, qx_ref, mkv_ref, cw_ref, ga_ref, gc_ref, gx_ref,
             wo_ref, gp_ref, after_ref, dy2_ref, dya_ref, delta_ref, tail_ref, dmkv_ref, dcw_ref, dgp_ref, dga_ref,
             dgc_ref, dgx_ref, carry):
        step = pl.program_id(0)
        first_tile = step == n_tiles - 1

        @pl.when(step == 0)
        def _():
            for ref in (dmkv_ref, dcw_ref, dgp_ref, dga_ref, dgc_ref, dgx_ref, carry):
                ref[...] = jnp.zeros_like(ref)

        dx1 = dx1_ref[...]
        y2hat, r2 = _rms_hat(y2_ref[...])
        dgp_ref[...] += jnp.sum(dx1 * y2hat, axis=0, keepdims=True)
        dy2 = _rms_bwd(y2hat, r2, gp_ref[...], dx1).astype(BF16)
        dy2_ref[...] = dy2
        dycat = lax.dot_general(dy2, wo_ref[...], NT, preferred_element_type=F32)

        d_na = dycat[:, 0:ATTN_W]
        ya = ya_ref[...]
        yahat, ra = _rms_hat(ya)
        dga_ref[...] += jnp.sum(d_na * yahat, axis=0, keepdims=True)
        dya = _rms_bwd(yahat, ra, ga_ref[...], d_na)
        dya_ref[...] = dya
        prod = dya * ya
        hi = prod.astype(BF16)
        lo = (prod - hi.astype(F32)).astype(BF16)
        head_of = lambda axis: lax.shift_right_logical(lax.broadcasted_iota(jnp.int32, (ATTN_W, ATTN_W), axis),
                                                       HEAD.bit_length() - 1)
        ones = jnp.where(head_of(0) == head_of(1), 1.0, 0.0).astype(BF16)
        delta_ref[...] = jnp.dot(hi, ones, preferred_element_type=F32) + jnp.dot(lo, ones, preferred_element_type=F32)

        w = cw_ref[...]
        b, c, u, z, z1, z2, cv = _conv_fwd(bcu_ref[...], before_ref[...], first_tile, w)
        d_nc = dycat[:, ATTN_W:ATTN_W + CONV_W]
        ychat, rc = _rms_hat(b * cv)
        dgc_ref[...] += jnp.sum(d_nc * ychat, axis=0, keepdims=True)
        dyc = _rms_bwd(ychat, rc, gc_ref[...], d_nc)
        dcv = dyc * b
        behind = carry[...]
        dz = w[2:3, :] * dcv + w[1:2, :] * _shift_up(dcv, behind, 1) + w[0:1, :] * _shift_up(dcv, behind, 2)
        carry[...] = dcv[0:8, :]
        dcw_ref[0:1, :] += jnp.sum(dcv * z2, axis=0, keepdims=True)
        dcw_ref[1:2, :] += jnp.sum(dcv * z1, axis=0, keepdims=True)
        dcw_ref[2:3, :] += jnp.sum(dcv * z, axis=0, keepdims=True)
        tail_ref[:, 0:CONV_W] = (dyc * cv).astype(BF16)
        tail_ref[:, CONV_W:2 * CONV_W] = (dz * u).astype(BF16)
        tail_ref[:, 2 * CONV_W:3 * CONV_W] = (dz * c).astype(BF16)

        d_nx = dycat[:, ATTN_W + CONV_W:D_MODEL]
        yxhat, rx = _rms_hat(yx_ref[...])
        dgx_ref[...] += jnp.sum(d_nx * yxhat, axis=0, keepdims=True)
        dyx = _rms_bwd(yxhat, rx, gx_ref[...], d_nx)
        qxb, mkvb = qx_ref[...], mkv_ref[...]
        heads = [slice(HEAD * hd, HEAD * (hd + 1)) for hd in range(XATTN_W // HEAD)]
        values = [slice(XATTN_W + sl.start, XATTN_W + sl.stop) for sl in heads]
        ss = [lax.dot_general(qxb[:, sl], mkvb[:, sl], NT, preferred_element_type=F32) * SCALE for sl in heads]
        es = [jnp.exp(s - jnp.max(s, axis=1, keepdims=True)) for s in ss]
        ps = [e / jnp.sum(e, axis=1, keepdims=True) for e in es]
        dobs = [dyx[:, sl].astype(BF16) for sl in heads]
        dps = [lax.dot_general(dob, mkvb[:, vsl], NT, preferred_element_type=F32) for dob, vsl in zip(dobs, values)]
        dss = [(p * (dp - jnp.sum(p * dp, axis=1, keepdims=True)) * SCALE).astype(BF16) for p, dp in zip(ps, dps)]
        for sl, vsl, p, dob, ds in zip(heads, values, ps, dobs, dss):
            tail_ref[:, 3 * CONV_W + sl.start:3 * CONV_W + sl.stop] = jnp.dot(
                ds, mkvb[:, sl], preferred_element_type=F32).astype(BF16)
            dmkv_ref[:, sl] += lax.dot_general(ds, qxb[:, sl], TN, preferred_element_type=F32)
            dmkv_ref[:, vsl] += lax.dot_general(p.astype(BF16), dob, TN, preferred_element_type=F32)

    rows = lambda width: pl.BlockSpec((tm, width), lambda i: (n_tiles - 1 - i, 0))
    before = pl.BlockSpec((8, 3 * CONV_W), lambda i: (jnp.maximum((n_tiles - 1 - i) * (tm // 8) - 1, 0), 0))
    acc = lambda r, w: pl.BlockSpec((r, w), lambda i: (0, 0))
    return pl.pallas_call(
        body, name="mixer_bwd", grid=(n_tiles,),
        in_specs=[rows(D_MODEL), rows(D_MODEL), rows(ATTN_W), rows(XATTN_W), rows(3 * CONV_W), before, rows(XATTN_W),
                  _resident((n_mem, 2 * XATTN_W)), _resident((3, CONV_W)), _resident((1, ATTN_W)),
                  _resident((1, CONV_W)), _resident((1, XATTN_W)), _resident((D_MODEL, D_MODEL)),
                  _resident((1, D_MODEL)), pl.BlockSpec(memory_space=pl.ANY)],
        out_specs=[rows(D_MODEL), rows(ATTN_W), rows(ATTN_W), rows(3 * CONV_W + XATTN_W), acc(n_mem, 2 * XATTN_W),
                   acc(3, CONV_W), acc(1, D_MODEL), acc(1, ATTN_W), acc(1, CONV_W), acc(1, XATTN_W)],
        out_shape=[jax.ShapeDtypeStruct((S, D_MODEL), BF16), jax.ShapeDtypeStruct((S, ATTN_W), F32),
                   jax.ShapeDtypeStruct((S, ATTN_W), F32), jax.ShapeDtypeStruct((S, 3 * CONV_W + XATTN_W), BF16),
                   jax.ShapeDtypeStruct((n_mem, 2 * XATTN_W), F32), jax.ShapeDtypeStruct((3, CONV_W), F32),
                   jax.ShapeDtypeStruct((1, D_MODEL), F32), jax.ShapeDtypeStruct((1, ATTN_W), F32),
                   jax.ShapeDtypeStruct((1, CONV_W), F32), jax.ShapeDtypeStruct((1, XATTN_W), F32)],
        scratch_shapes=[pltpu.VMEM((8, CONV_W), F32)],
        compiler_params=_params("arbitrary"),
    )(dx1, y2, ya, yx, bcu, bcu, qx, mkv, conv_w, g_a, g_c, g_x, w_out, g_post, after)


def _memkv_bwd(mem, g_mem, w_kv, dmkv):
    n_mem = mem.shape[0]
    half = D_MODEL // N_CHIPS // 2

    def body(mem_ref, g_ref, w_ref, d_ref, dw_ref, dg_ref):
        mhat, _ = _rms_hat(mem_ref[...])
        mn = (mhat * g_ref[...]).astype(BF16)
        d = d_ref[...].astype(BF16)
        for k in range(2 * N_CHIPS):
            dw_ref[k % 2, k // 2] = lax.dot_general(mn[:, half * k:half * (k + 1)], d, TN, preferred_element_type=F32)
        dmn = lax.dot_general(d, w_ref[...], NT, preferred_element_type=F32)
        dg_ref[...] = jnp.sum(dmn * mhat, axis=0, keepdims=True)

    return pl.pallas_call(
        body, name="memkv_bwd",
        out_shape=[jax.ShapeDtypeStruct((2, N_CHIPS, half, 2 * XATTN_W), F32), jax.ShapeDtypeStruct((1, D_MODEL), F32)],
        compiler_params=pltpu.CompilerParams(vmem_limit_bytes=VMEM_LIMIT_V7X),
    )(mem, g_mem, w_kv, dmkv)


def _in_proj_bwd(dqkv, tail, cos, sin, w_in, x, h, g, dx1, after, tm):
    S = x.shape[0]
    step_w = 2 * 256
    half = D_MODEL // 2

    def body(dq_ref, dk_ref, dv_ref, tail_ref, cos_ref, sin_ref, w_hbm, x_ref, h_ref, g_ref, dx1_ref, after_ref,
             dx_ref, gw_ref, dg_ref, dproj_ref, w_full, sems):
        _side_by_side(w_hbm, w_full, sems)

        @pl.when(pl.program_id(0) == 0)
        def _():
            dg_ref[...] = jnp.zeros_like(dg_ref)
            gw_ref[...] = jnp.zeros_like(gw_ref)

        halves = [slice(0, tm // 2), slice(tm // 2, tm)]
        for rows in halves:
            c, s = cos_ref[rows, :], sin_ref[rows, :]
            for j in range(ATTN_W // 128):
                cols = slice(128 * j, 128 * (j + 1))
                dproj_ref[rows, cols] = _rope128(dq_ref[rows, cols] * SCALE, c, s, True).astype(BF16)
                dproj_ref[rows, ATTN_W + 128 * j:ATTN_W + 128 * (j + 1)] = _rope128(dk_ref[rows, cols], c, s, True).astype(BF16)
            dproj_ref[rows, 2 * ATTN_W:3 * ATTN_W] = dv_ref[rows, :].astype(BF16)
            dproj_ref[rows, 3 * ATTN_W:PROJ_W] = tail_ref[rows, :]
        dhs = [lax.dot_general(dproj_ref[rows, :], w_full[...], NT, preferred_element_type=F32) for rows in halves]
        for rows, dh in zip(halves, dhs):
            xhat, r = _rms_hat(x_ref[rows, :])
            dg_ref[...] += jnp.sum(dh * xhat, axis=0, keepdims=True)
            dx_ref[rows, :] = dx1_ref[rows, :] + _rms_bwd(xhat, r, g_ref[...], dh)
        hb = h_ref[...]
        for step in range(PROJ_W // step_w):
            res = lax.dot_general(hb, dproj_ref[:, step * step_w:(step + 1) * step_w], TN, preferred_element_type=F32)
            lo = step * step_w
            while lo < (step + 1) * step_w:
                chip = lo // SHARD_IN
                hi = min((step + 1) * step_w, (chip + 1) * SHARD_IN)
                for hh in range(2):
                    gw_ref[hh, chip, :, lo - chip * SHARD_IN:hi - chip * SHARD_IN] += (
                        res[half * hh:half * (hh + 1), lo - step * step_w:hi - step * step_w])
                lo = hi

    whole = lambda shape: pl.BlockSpec(shape, lambda i: (0,) * len(shape))
    return pl.pallas_call(
        body, name="in_proj_bwd", grid=(S // tm,),
        in_specs=[_rows(tm, ATTN_W)] * 3 + [_rows(tm, PROJ_W - 3 * ATTN_W), _rows(tm, 128), _rows(tm, 128),
                  pl.BlockSpec(memory_space=pl.ANY), _rows(tm, D_MODEL), _rows(tm, D_MODEL), _resident((1, D_MODEL)),
                  _rows(tm, D_MODEL), pl.BlockSpec(memory_space=pl.ANY)],
        out_specs=[_rows(tm, D_MODEL), whole((2, N_CHIPS, half, SHARD_IN)), whole((1, D_MODEL))],
        out_shape=[jax.ShapeDtypeStruct((S, D_MODEL), F32), jax.ShapeDtypeStruct((2, N_CHIPS, half, SHARD_IN), F32),
                   jax.ShapeDtypeStruct((1, D_MODEL), F32)],
        scratch_shapes=[pltpu.VMEM((tm, PROJ_W), BF16), pltpu.VMEM((D_MODEL, PROJ_W), BF16),
                        pltpu.SemaphoreType.DMA((N_CHIPS,))],
        compiler_params=_params("arbitrary"),
    )(*dqkv, tail, cos, sin, w_in, x, h, g, dx1, after)


def _row_tile(rows):
    return ROW_TILE if rows % ROW_TILE == 0 else rows


def _chip_sums_bf16(name, grads, from_sibling, place):
    k = len(grads)
    _, n, rows, _ = grads[0].shape
    tr = _row_tile(rows)

    def body(place_ref, *refs):
        for g_ref, b_ref, o_ref in zip(refs[:k], refs[k:2 * k], refs[2 * k:]):
            o_ref[...] = (g_ref[0] + b_ref[...]).astype(BF16)

    mine = lambda g: pl.BlockSpec((1, 1, tr, g.shape[3]), lambda s, i, p: (p[0], s, i, 0))
    slab = lambda g: pl.BlockSpec((1, tr, g.shape[3]), lambda s, i, p: (s, i, 0))
    return pl.pallas_call(
        body, name=name, out_shape=[jax.ShapeDtypeStruct(g.shape[1:], BF16) for g in grads],
        grid_spec=pltpu.PrefetchScalarGridSpec(
            num_scalar_prefetch=1, grid=(n, rows // tr),
            in_specs=[mine(g) for g in grads] + [slab(g) for g in grads], out_specs=[slab(g) for g in grads]),
        compiler_params=_params("parallel", "parallel"),
    )(place, *grads, *from_sibling)


def _final_sums(name, grads, from_sibling, others, place):
    k = len(grads)
    rows = grads[0].shape[2]
    tr = _row_tile(rows)

    def body(place_ref, *refs):
        for a in range(k):
            own_ref, sib_ref = refs[a], refs[k + a]
            acc = own_ref[0, 0] + sib_ref[0]
            for o in refs[2 * k + 3 * a:2 * k + 3 * a + 3]:
                acc = acc + o[0].astype(F32)
            refs[5 * k + a][0] = acc

    own = lambda g: pl.BlockSpec((1, 1, tr, g.shape[3]), lambda i, p: (p[0], p[1], i, 0))
    sib = lambda g: pl.BlockSpec((1, tr, g.shape[3]), lambda i, p: (p[1], i, 0))
    other = lambda g, j: pl.BlockSpec((1, tr, g.shape[3]), lambda i, p: (j, i, 0))
    return pl.pallas_call(
        body, name=name, out_shape=[jax.ShapeDtypeStruct((2,) + g.shape[2:], F32) for g in grads],
        grid_spec=pltpu.PrefetchScalarGridSpec(
            num_scalar_prefetch=1, grid=(rows // tr,),
            in_specs=[own(g) for g in grads] + [sib(g) for g in grads] + [other(g, j) for g in grads for j in range(3)],
            out_specs=[pl.BlockSpec((1, tr, g.shape[3]), lambda i, p: (p[0], i, 0)) for g in grads]),
        compiler_params=_params("parallel"),
    )(place, *grads, *from_sibling, *[o for o in others for _ in range(3)])


def _adamw_update(w, g, m, v):
    m = ADAM_B1 * m + (1.0 - ADAM_B1) * g
    v = ADAM_B2 * v + (1.0 - ADAM_B2) * (g * g)
    m_hat = m * (1.0 / (1.0 - ADAM_B1 ** ADAM_STEP))
    v_hat = v * (1.0 / (1.0 - ADAM_B2 ** ADAM_STEP))
    return -ADAM_LR * (m_hat / (jnp.sqrt(v_hat) + ADAM_EPS) + ADAM_WD * w), m, v


def _adamw(name, params, after):
    k = len(params)
    rows = params[0][0].shape[0]
    tr = ADAMW_ROW_TILE if rows % ADAMW_ROW_TILE == 0 else rows

    def body(*refs):
        ins, outs = refs[:4 * k], refs[4 * k + 1:]
        for a in range(k):
            w_ref, g_ref, m_ref, v_ref = ins[4 * a:4 * a + 4]
            g = g_ref[...]
            outs[4 * a][...] = g
            outs[4 * a + 1][...], outs[4 * a + 2][...], outs[4 * a + 3][...] = _adamw_update(w_ref[...], g, m_ref[...], v_ref[...])

    spec = lambda w: pl.BlockSpec((tr, w.shape[1]), lambda i: (i, 0))
    out = pl.pallas_call(
        body, name=name, grid=(rows // tr,),
        in_specs=[spec(p[0]) for p in params for _ in range(4)] + [pl.BlockSpec(memory_space=pl.ANY)],
        out_specs=[spec(p[0]) for p in params for _ in range(4)],
        out_shape=[jax.ShapeDtypeStruct(p[0].shape, F32) for p in params for _ in range(4)],
        compiler_params=_params("parallel"),
    )(*[t for p in params for t in p], after)
    return [out[4 * a:4 * a + 4] for a in range(k)]


def _small_update(summed, chip, gains, gains_m, gains_v, taps, taps_m, taps_v):
    n = len(gains)
    widths = [g.shape[1] for g in gains]
    k, w = taps.shape

    def body(*refs):
        chip_ref, sum_ref = refs[0], refs[1]
        params = [refs[2 + 3 * i:5 + 3 * i] for i in range(n + 1)]
        outs = [refs[2 + 3 * (n + 1) + 4 * i:2 + 3 * (n + 1) + 4 * (i + 1)] for i in range(n + 1)]
        loss_ref = refs[-1]
        for i in range(n):
            g = sum_ref[i:i + 1, 0:widths[i]]
            wr, mr, vr = params[i]
            outs[i][0][...] = g
            outs[i][1][...], outs[i][2][...], outs[i][3][...] = _adamw_update(wr[...], g, mr[...], vr[...])
        g = sum_ref[n:n + k, 0:w]
        for j in range(1, N_CHIPS):
            g = jnp.where(chip_ref[0] == j, sum_ref[n:n + k, w * j:w * (j + 1)], g)
        wr, mr, vr = params[n]
        outs[n][0][...] = g
        outs[n][1][...], outs[n][2][...], outs[n][3][...] = _adamw_update(wr[...], g, mr[...], vr[...])
        loss_ref[...] = sum_ref[n + k:n + k + 1, 0:1]

    vmem = pl.BlockSpec(memory_space=pltpu.VMEM)
    operands = [chip, summed]
    for p in zip(list(gains) + [taps], list(gains_m) + [taps_m], list(gains_v) + [taps_v]):
        operands += list(p)
    shapes = [jax.ShapeDtypeStruct(p.shape, F32) for p in list(gains) + [taps] for _ in range(4)]
    out = pl.pallas_call(
        body, name="small_update", out_shape=shapes + [jax.ShapeDtypeStruct((1, 1), F32)],
        in_specs=[pl.BlockSpec(memory_space=pltpu.SMEM)] + [vmem] * (len(operands) - 1),
        out_specs=[vmem] * (len(shapes) + 1),
    )(*operands)
    return [out[4 * i:4 * (i + 1)] for i in range(n + 1)], out[-1]


def _sum_blocks(name, blocks):
    n, rows, cols = blocks.shape

    def body(b_ref, o_ref):
        acc = b_ref[0]
        for k in range(1, n):
            acc = acc + b_ref[k]
        o_ref[...] = acc

    return pl.pallas_call(body, name=name, out_shape=jax.ShapeDtypeStruct((rows, cols), F32))(blocks)


def _place():
    return lax.axis_index("x"), lax.axis_index("y"), lax.axis_index("c")


def _other_chips(x, y):
    return [(1 - x, y), (x, 1 - y), (1 - x, 1 - y)]


def _allgather_finish(name, shards, landed, pass_on):
    n = len(shards)

    def body(*refs):
        ins, outs, stage = refs[:n], refs[2 * n:3 * n], refs[3 * n:4 * n]
        send_sems, recv_sems, local_sems = refs[4 * n:]
        x, y, c = _place()
        chips = _other_chips(x, y)
        barrier = pltpu.get_barrier_semaphore()
        pl.semaphore_signal(barrier, inc=1, device_id=(x, y, 1 - c), device_id_type=MESH)
        pl.semaphore_wait(barrier, 1)

        def copy(a, k, chip, half):
            place = outs[a].at[2 * chip[0] + chip[1], half]
            return pltpu.make_async_remote_copy(
                src_ref=place, dst_ref=place, send_sem=send_sems.at[3 * a + k], recv_sem=recv_sems.at[3 * a + k],
                device_id=(x, y, 1 - c), device_id_type=MESH)

        load = [pltpu.make_async_copy(ins[a], stage[a], local_sems.at[a]) for a in range(n)]
        local = [pltpu.make_async_copy(stage[a], outs[a].at[2 * x + y], local_sems.at[a]) for a in range(n)]
        for cp in load:
            cp.start()
        passed = [copy(a, k, chip, c) for a in range(n) if pass_on[a] for k, chip in enumerate(chips)]
        for cp in passed:
            cp.start()
        for a in range(n):
            load[a].wait()
            local[a].start()
        for a in range(n):
            if pass_on[a]:
                for k, chip in enumerate(chips):
                    copy(a, k, chip, 1 - c).wait_recv()
        for cp in passed:
            cp.wait_send()
        for cp in local:
            cp.wait()

    any_spec = pl.BlockSpec(memory_space=pl.ANY)
    return pl.pallas_call(
        body, name=name,
        out_shape=[jax.ShapeDtypeStruct((N_CHIPS,) + s.shape, s.dtype) for s in shards],
        in_specs=[any_spec] * (2 * n), out_specs=[any_spec] * n,
        input_output_aliases={n + a: a for a in range(n)},
        scratch_shapes=[pltpu.VMEM(s.shape, s.dtype) for s in shards]
        + [pltpu.SemaphoreType.DMA((3 * n,)), pltpu.SemaphoreType.DMA((3 * n,)), pltpu.SemaphoreType.DMA((n,))],
        compiler_params=pltpu.CompilerParams(vmem_limit_bytes=VMEM_LIMIT_V7X, collective_id=HANDSHAKES["sibling"][0]),
    )(*shards, *landed)


def _plan_first_hop(x, y, c, shards, lands):
    return [(shards[a].at[c], lands[a].at[2 * x + y, c], lands[a].at[2 * chip[0] + chip[1], c], (*chip, c))
            for a in range(len(shards)) for chip in _other_chips(x, y)]


def _plan_pass_on(x, y, c, nothing, lands):
    def place(a, chip, half):
        return lands[a].at[2 * chip[0] + chip[1], half]

    return [(place(a, chip, c), place(a, chip, c), place(a, chip, 1 - c), (x, y, 1 - c))
            for a in range(len(lands)) for chip in _other_chips(x, y)]


def _plan_own_half_to_sibling(x, y, c, nothing, lands):
    return [(lands[a].at[c], lands[a].at[c], lands[a].at[1 - c], (x, y, 1 - c)) for a in range(len(lands))]


def _plan_other_half_to_sibling(x, y, c, grads, lands):
    return [(grads[a].at[1 - c], lands[a], lands[a], (x, y, 1 - c)) for a in range(len(grads))]


def _plan_to_other_chips(x, y, c, partials, lands):
    return [(partials[a].at[2 * chip[0] + chip[1]], lands[a].at[k], lands[a].at[k], (*chip, c))
            for a in range(len(partials)) for k, chip in enumerate(_other_chips(x, y))]


def _plan_to_all(x, y, c, blocks, lands):
    flips = [(fx, fy, fc) for fx in (0, 1) for fy in (0, 1) for fc in (0, 1) if (fx, fy, fc) != (0, 0, 0)]
    peers = [(1 - x if fx else x, 1 - y if fy else y, 1 - c if fc else c) for fx, fy, fc in flips]
    return [(blocks[0], lands[0].at[4 * x + 2 * y + c], lands[0].at[4 * p[0] + 2 * p[1] + p[2]], p) for p in peers]


def _planned_copies(plan, srcs, lands, send_sems, recv_sems):
    x, y, c = _place()

    def pair(k, src, there, here, to):
        make = lambda dst: pltpu.make_async_remote_copy(
            src_ref=src, dst_ref=dst, send_sem=send_sems.at[k], recv_sem=recv_sems.at[k], device_id=to, device_id_type=MESH)
        return make(there), make(here)

    return [pair(k, *entry) for k, entry in enumerate(plan(x, y, c, srcs, lands))]


_HBM_SPEC = pl.BlockSpec(memory_space=pltpu.HBM)
_SEM_SPEC = pl.BlockSpec(memory_space=pltpu.SEMAPHORE)


def _hbm(a):
    return pltpu.with_memory_space_constraint(a, pltpu.HBM)


HANDSHAKES = {
    "sibling": (1, lambda x, y, c: [(x, y, 1 - c)]),
}


def _exchange_start(name, plan, n_copies, srcs, land_shapes, after, lands=None, peers=None):
    if lands is None:
        lands = [lax.empty(s.shape, s.dtype) for s in land_shapes]
    land_shapes = lands
    ns, nl = len(srcs), len(land_shapes)
    n_in = ns + nl + 1
    collective_id, peers_of = HANDSHAKES[peers] if peers else (None, None)

    def body(*refs):
        if peers:
            who = peers_of(*_place())
            barrier = pltpu.get_barrier_semaphore()
            for peer in who:
                pl.semaphore_signal(barrier, inc=1, device_id=peer, device_id_type=MESH)
            pl.semaphore_wait(barrier, len(who))
        for send, _ in _planned_copies(plan, refs[:ns], refs[ns:ns + nl], refs[n_in], refs[n_in + 1]):
            send.start()
        refs[-1][...] = jnp.zeros_like(refs[-1])

    out = pl.pallas_call(
        body, name=name,
        out_shape=(pltpu.SemaphoreType.DMA((n_copies,)), pltpu.SemaphoreType.DMA((n_copies,)),
                   *[pltpu.HBM(s.shape, s.dtype) for s in land_shapes], jax.ShapeDtypeStruct((8, 128), F32)),
        in_specs=[_HBM_SPEC] * (ns + nl) + [pl.BlockSpec(memory_space=pl.ANY)],
        out_specs=(_SEM_SPEC, _SEM_SPEC, *[_HBM_SPEC] * nl, pl.BlockSpec(memory_space=pltpu.VMEM)),
        input_output_aliases={ns + i: 2 + i for i in range(nl)},
        compiler_params=pltpu.CompilerParams(has_side_effects=pltpu.SideEffectType.DATAFLOW_SIDE_EFFECTING,
                                             collective_id=collective_id),
    )(*[_hbm(s) for s in srcs], *[_hbm(l) for l in lands], after)
    return out[0], out[1], list(out[2:2 + nl]), out[-1]


def _exchange_wait(name, plan, srcs, started, after):
    send_sems, recv_sems, lands, _ = started
    ns, nl = len(srcs), len(lands)
    after = list(after) if isinstance(after, (list, tuple)) else [after]

    def body(*refs):
        for send, recv in _planned_copies(plan, refs[:ns], refs[ns:ns + nl], refs[ns + nl], refs[ns + nl + 1]):
            send.wait_send()
            recv.wait_recv()

    return pl.pallas_call(
        body, name=name, out_shape=[pltpu.HBM(l.shape, l.dtype) for l in lands],
        in_specs=[_HBM_SPEC] * (ns + nl) + [_SEM_SPEC, _SEM_SPEC] + [pl.BlockSpec(memory_space=pl.ANY)] * len(after),
        out_specs=[_HBM_SPEC] * nl, input_output_aliases={ns + i: i for i in range(nl)},
        compiler_params=pltpu.CompilerParams(has_side_effects=pltpu.SideEffectType.DATAFLOW_SIDE_EFFECTING),
    )(*[_hbm(s) for s in srcs], *lands, send_sems, recv_sems, *after)


def _like(arrays, lead, dtype=None):
    return [jax.ShapeDtypeStruct(tuple(lead) + a.shape[-2:], dtype or a.dtype) for a in arrays]


class _StepExchanges:
    def __init__(self, mats, conv_w):
        x, y, c = _place()
        self.place = jnp.stack([c, 2 * x + y]).astype(jnp.int32)
        shards = [w.astype(BF16).reshape(2, w.shape[0] // 2, w.shape[1]) for w in mats]
        self._in_shard = shards[:1]
        self._in = _exchange_start("w_in_allgather_start", _plan_first_hop, 3, self._in_shard,
                                   _like(self._in_shard, (N_CHIPS, 2)), shards[0])
        self.zero = self._in[3]
        taps = jnp.pad(conv_w, ((0, 8 - conv_w.shape[0]), (0, 128 - conv_w.shape[1])))
        self._rest_shards = shards[1:] + [jnp.stack([taps, jnp.zeros_like(taps)])]
        self._taps_shape = conv_w.shape
        self._groups = {}

    def w_in(self, after):
        landed = _exchange_wait("w_in_allgather_wait", _plan_first_hop, self._in_shard, self._in,
                                list(after) + self._rest_shards)
        (w_in,) = _allgather_finish("w_in_allgather_finish", self._in_shard, landed, [True])
        self._rest = _exchange_start("rest_allgather_start", _plan_first_hop, 3 * len(self._rest_shards),
                                     self._rest_shards, _like(self._rest_shards, (N_CHIPS, 2)), w_in)
        self.zero = self._rest[3]
        return w_in.reshape(N_CHIPS, 2 * w_in.shape[2], w_in.shape[3])

    def rest_weights(self, after):
        landed = _exchange_wait("rest_allgather_wait", _plan_first_hop, self._rest_shards, self._rest, after)
        kv, out, up, down, taps = _allgather_finish("rest_allgather_finish", self._rest_shards, landed,
                                                    [True, True, False, False, True])
        self._up_down = _exchange_start("up_down_pass_on_start", _plan_pass_on, 6, [], None, self.zero, lands=[up, down],
                                        peers="sibling")
        self.zero = self._up_down[3]
        k, w = self._taps_shape
        taps = taps[:, 0, :k, :w].transpose(1, 0, 2).reshape(k, N_CHIPS * w)
        return [g.reshape(N_CHIPS, 2 * g.shape[2], g.shape[3]) for g in (kv, out)], taps

    def up_down(self, after):
        full = _exchange_wait("up_down_pass_on_wait", _plan_pass_on, [], self._up_down, after)
        return [g.reshape(N_CHIPS, 2 * g.shape[2], g.shape[3]) for g in full]

    def send_grads(self, key, grads):
        grads = list(grads)
        started = _exchange_start(f"{key}_grads_to_sibling_start", _plan_other_half_to_sibling, len(grads), grads,
                                  _like(grads, (N_CHIPS,)), self.zero, peers="sibling")
        self._groups[key] = dict(grads=grads, to_sibling=started)
        self.zero = started[3]

    def grads_at_sibling(self, key, after):
        group = self._groups[key]
        grads = group["grads"]
        group["from_sibling"] = _exchange_wait(f"{key}_grads_to_sibling_wait", _plan_other_half_to_sibling, grads,
                                               group["to_sibling"], after)
        group["partials"] = _chip_sums_bf16(f"{key}_chip_sums", grads, group["from_sibling"], self.place)
        group["to_chips"] = _exchange_start(f"{key}_grads_to_chips_start", _plan_to_other_chips, 3 * len(grads),
                                            group["partials"], _like(group["partials"], (3,)), self.zero)
        self.zero = group["to_chips"][3]

    def grads_summed(self, key, after):
        group = self._groups[key]
        from_chips = _exchange_wait(f"{key}_grads_to_chips_wait", _plan_to_other_chips, group["partials"],
                                    group["to_chips"], after)
        return _final_sums(f"{key}_final_sums", group["grads"], group["from_sibling"], from_chips, self.place)

    def send_sums(self, key, sums):
        self._groups[key + "_sums"] = _exchange_start(f"{key}_sums_to_sibling_start", _plan_own_half_to_sibling,
                                                      len(sums), [], None, self.zero, lands=list(sums),
                                                      peers="sibling")
        self.zero = self._groups[key + "_sums"][3]

    def whole_sums(self, key, after):
        full = _exchange_wait(f"{key}_sums_to_sibling_wait", _plan_own_half_to_sibling, [], self._groups[key + "_sums"], after)
        return [t.reshape(2 * t.shape[1], t.shape[2]) for t in full]

    def send_small(self, block):
        self._small = block
        self._small_started = _exchange_start("small_grads_start", _plan_to_all, 7, [block],
                                              [jax.ShapeDtypeStruct((8,) + block.shape, block.dtype)], self.zero)
        self.zero = self._small_started[3]

    def small_summed(self, after):
        x, y, c = _place()
        (landed,) = _exchange_wait("small_grads_wait", _plan_to_all, [self._small], self._small_started, after)
        blocks = lax.dynamic_update_index_in_dim(landed, self._small, 4 * x + 2 * y + c, 0)
        return _sum_blocks("small_sum", blocks)


def _rope_tables(positions):
    half = HEAD // 2
    inv_freq = jnp.float32(ROPE_THETA) ** (-(jnp.arange(half, dtype=F32) * 2.0 / HEAD))
    ang = positions.astype(F32)[:, None] * inv_freq
    cos, sin = jnp.cos(ang), jnp.sin(ang)
    return jnp.tile(cos, (1, 4)), jnp.tile(jnp.concatenate([-sin, sin], axis=1), (1, 2))


def _local_step(x, mem, positions, target, gains, ex):
    g_pre_mix, g_mem, g_a, g_c, g_x, g_post_mix, g_pre_mlp, g_post_mlp = gains
    tm = ROW_TILE
    cos, sin = _rope_tables(positions)
    h = _pre_norm(x, g_pre_mix, ex.zero, tm)
    w_in = ex.w_in([h, cos, sin])

    q, k, v, bcu, qx = _in_proj_fwd(h, w_in, cos, sin, ex.zero, tm)
    ya, lse = _attn_fwd(q, k, v)
    (w_kv, w_out), conv_w = ex.rest_weights(lse)
    w_kv, w_out = (w.reshape(N_CHIPS * w.shape[1], w.shape[2]) for w in (w_kv, w_out))
    memn, mkv = _memkv_fwd(mem, g_mem, w_kv, ex.zero)
    yx, ycat, y2, x1 = _mix_fwd(ya, bcu, qx, mkv, conv_w, g_a, g_c, g_x, w_out, g_post_mix, x, tm)
    w_up, w_down = ex.up_down(x1)
    w_down = w_down.reshape(N_CHIPS * w_down.shape[1], w_down.shape[2])
    h2, f, du, df2, dx1, dg_pre_mlp, dg_post_mlp, loss = _mlp_fwd_bwd(x1, target, g_pre_mlp, g_post_mlp, w_up, w_down,
                                                                      MLP_ROW_TILE)
    gw_down = _weight_grad("grad_w_down", f, df2, True, ex.zero)
    gw_up = _weight_grad("grad_w_up", h2, du, False, ex.zero)
    ex.send_grads("early", [gw_up, gw_down])

    dy2, dya, delta, tail, dmkv, g_conv, dg_post_mix, dg_a, dg_c, dg_x = _mixer_bwd(
        dx1, y2, ya, yx, bcu, qx, mkv, conv_w, g_a, g_c, g_x, w_out, g_post_mix, ex.zero, tm)
    ex.grads_at_sibling("early", dy2)
    gw_out = _weight_grad("grad_w_out", ycat, dy2, True, ex.zero)
    gw_kv, dg_mem = _memkv_bwd(mem, g_mem, w_kv, dmkv)
    ex.send_grads("mid", [gw_out, gw_kv])
    dqkv = _attn_bwd(q, k, v, dya, lse, delta, ex.zero)
    ex.grads_at_sibling("mid", dqkv[0])
    grad_x, gw_in, dg_pre_mix = _in_proj_bwd(dqkv, tail, cos, sin, w_in, x, h, g_pre_mix, dx1, ex.zero, tm)
    gain_grads = [dg_pre_mix, dg_mem, dg_a, dg_c, dg_x, dg_post_mix, dg_pre_mlp, dg_post_mlp]
    ex.send_small(_pack_small(gain_grads, g_conv, loss))
    ex.send_grads("late", [gw_in])
    return grad_x


def _pack_small(gains, conv, scalar=None):
    rows = [jnp.pad(g, ((0, 0), (0, D_MODEL - g.shape[1]))) for g in gains]
    rows.append(jnp.pad(conv, ((0, 0), (0, D_MODEL - conv.shape[1]))))
    last = jnp.zeros((SMALL_ROWS - 8 - conv.shape[0], D_MODEL), F32)
    rows.append(last if scalar is None else last.at[0:1, 0:1].set(scalar))
    return jnp.concatenate(rows, axis=0)


def kernel(x, mem, positions, g_pre_mix, g_mem, w_in, w_mem_kv, conv_w, g_attn_out, g_conv_out, g_xattn_out, w_out, g_post_mix, g_pre_mlp, w_up, w_down, g_post_mlp, loss_target, m_g_pre_mix, m_g_mem, m_w_in, m_w_mem_kv, m_conv_w, m_g_attn_out, m_g_conv_out, m_g_xattn_out, m_w_out, m_g_post_mix, m_g_pre_mlp, m_w_up, m_w_down, m_g_post_mlp, v_g_pre_mix, v_g_mem, v_w_in, v_w_mem_kv, v_conv_w, v_g_attn_out, v_g_conv_out, v_g_xattn_out, v_w_out, v_g_post_mix, v_g_pre_mlp, v_w_up, v_w_down, v_g_post_mlp):
    chip = 2 * lax.axis_index("x") + lax.axis_index("y")
    gains = [g_pre_mix, g_mem, g_attn_out, g_conv_out, g_xattn_out, g_post_mix, g_pre_mlp, g_post_mlp]
    gains_m = [m_g_pre_mix, m_g_mem, m_g_attn_out, m_g_conv_out, m_g_xattn_out, m_g_post_mix, m_g_pre_mlp, m_g_post_mlp]
    gains_v = [v_g_pre_mix, v_g_mem, v_g_attn_out, v_g_conv_out, v_g_xattn_out, v_g_post_mix, v_g_pre_mlp, v_g_post_mlp]
    mats =[w_in[0], w_mem_kv[0], w_out[0], w_up[0], w_down[0]]
    mats_m = [m_w_in[0], m_w_mem_kv[0], m_w_out[0], m_w_up[0], m_w_down[0]]
    mats_v = [v_w_in[0], v_w_mem_kv[0], v_w_out[0], v_w_up[0], v_w_down[0]]

    ex = _StepExchanges(mats, conv_w[0])
    grad_x = _local_step(x[0], mem[0], positions[0], loss_target[0], gains, ex)

    ex.send_sums("four", ex.grads_summed("early", ex.zero) + ex.grads_summed("mid", ex.zero))
    ex.grads_at_sibling("late", ex.zero)
    up_sum, down_sum, out_sum, kv_sum = ex.whole_sums("four", ex.zero)
    params = lambda a, g: (mats[a], g, mats_m[a], mats_v[a])
    new_up, new_down = _adamw("adamw_up_down", [params(3, up_sum), params(4, down_sum)], ex.zero)
    new_out, new_kv = _adamw("adamw_out_kv", [params(2, out_sum), params(1, kv_sum)], ex.zero)

    small, total = _small_update(ex.small_summed(new_kv[1]), chip.reshape(1).astype(jnp.int32), gains, gains_m,
                                 gains_v, conv_w[0], m_conv_w[0], v_conv_w[0])

    ex.send_sums("last", ex.grads_summed("late", small[0][1]))
    (in_sum,) = ex.whole_sums("last", ex.zero)
    (new_in,) = _adamw("adamw_in", [params(0, in_sum)], in_sum)
    mat_new = [new_in, new_kv, new_out, new_up, new_down]

    order = ["g_pre_mix", "g_mem", "w_in", "w_mem_kv", "conv_w", "g_attn_out", "g_conv_out", "g_xattn_out", "w_out",
             "g_post_mix", "g_pre_mlp", "w_up", "w_down", "g_post_mlp"]
    gain_names = ["g_pre_mix", "g_mem", "g_attn_out", "g_conv_out", "g_xattn_out", "g_post_mix", "g_pre_mlp", "g_post_mlp"]
    mat_names = ["w_in", "w_mem_kv", "w_out", "w_up", "w_down"]

    def leaf(kind, name):
        if name in gain_names:
            return small[gain_names.index(name)][kind]
        if name == "conv_w":
            return small[len(gain_names)][kind][None]
        return mat_new[mat_names.index(name)][kind][None]

    return (total[0, 0], grad_x[None], *[leaf(kind, name) for kind in range(4) for name in order])
```

```python
import jax
import jax.numpy as jnp
from jax import lax
from jax.experimental import pallas as pl
from jax.experimental.pallas import tpu as pltpu

F32, BF16 = jnp.float32, jnp.bfloat16

D_MODEL = 1024
ATTN_W = 512
CONV_W = 256
XATTN_W = 256
PROJ_W = 3 * ATTN_W + 3 * CONV_W + XATTN_W
D_FF = 4096
HEAD = 64
N_BACK = 128
DILATIONS = (1, 4, 16)
PATTERN_ORDER = DILATIONS[::-1]
ROPE_THETA = 10000.0
EPS = 1e-6
NEG_INF = -1e30
SCALE = HEAD ** -0.5
N_CHIPS = 4
SHARD_IN = PROJ_W // N_CHIPS
SHARD_FF = D_FF // N_CHIPS

ADAM_LR, ADAM_B1, ADAM_B2, ADAM_EPS, ADAM_WD, ADAM_STEP = 0.001, 0.9, 0.999, 1e-08, 0.01, 10

VMEM_LIMIT_V7X = 56 * 1024 * 1024
ROW_TILE = 512
MLP_ROW_TILE = 256
ADAMW_ROW_TILE = 256
SMALL_ROWS = 16

NT = (((1,), (1,)), ((), ()))
TN = (((0,), (0,)), ((), ()))
MESH = pl.DeviceIdType.MESH


def _params(*sem):
    return pltpu.CompilerParams(dimension_semantics=sem, vmem_limit_bytes=VMEM_LIMIT_V7X)


def _resident(shape):
    return pl.BlockSpec(shape, lambda *_: (0,) * len(shape), pipeline_mode=pl.Buffered(1))


def _rows(tm, width):
    return pl.BlockSpec((tm, width), lambda i: (i, 0))


def _rms_hat(x):
    r = lax.rsqrt(jnp.mean(x * x, axis=-1, keepdims=True) + EPS)
    return x * r, r


def _rms_bwd(xhat, r, g, dy):
    gdy = dy * g
    return r * (gdy - xhat * jnp.mean(xhat * gdy, axis=-1, keepdims=True))


def _rope128(t, cos, sin_signed, inverse):
    lane = lax.broadcasted_iota(jnp.int32, t.shape, 1)
    first_half = (lane % HEAD) < (HEAD // 2)
    rot = jnp.where(first_half, pltpu.roll(t, 128 - HEAD // 2, 1), pltpu.roll(t, HEAD // 2, 1))
    return t * cos - rot * sin_signed if inverse else t * cos + rot * sin_signed


def _pre_norm(x, g, after, tm):
    S = x.shape[0]

    def body(x_ref, g_ref, after_ref, h_ref):
        h_ref[...] = (_rms_hat(x_ref[...])[0] * g_ref[...]).astype(BF16)

    return pl.pallas_call(
        body, name="pre_norm", grid=(S // tm,),
        in_specs=[_rows(tm, D_MODEL), _resident((1, D_MODEL)), pl.BlockSpec(memory_space=pl.ANY)],
        out_specs=_rows(tm, D_MODEL), out_shape=jax.ShapeDtypeStruct((S, D_MODEL), BF16),
        compiler_params=_params("parallel"),
    )(x, g, after)


def _side_by_side(w_hbm, w_full, sems):
    width = w_hbm.shape[2]

    @pl.when(pl.program_id(0) == 0)
    def _():
        copies = [pltpu.make_async_copy(w_hbm.at[j], w_full.at[:, pl.ds(width * j, width)], sems.at[j])
                  for j in range(N_CHIPS)]
        for cp in copies:
            cp.start()
        for cp in copies:
            cp.wait()


def _in_proj_fwd(h, w_in, cos, sin, after, tm):
    S = h.shape[0]

    def body(h_ref, w_hbm, cos_ref, sin_ref, after_ref, q_ref, k_ref, v_ref, bcu_ref, qx_ref, proj, w_full, sems):
        _side_by_side(w_hbm, w_full, sems)
        proj[...] = jnp.dot(h_ref[...], w_full[...], preferred_element_type=F32)
        c, s = cos_ref[...], sin_ref[...]
        for j in range(ATTN_W // 128):
            lo = 128 * j
            q_ref[:, lo:lo + 128] = _rope128(proj[:, lo:lo + 128], c, s, False) * SCALE
            k_ref[:, lo:lo + 128] = _rope128(proj[:, ATTN_W + lo:ATTN_W + lo + 128], c, s, False)
        v_ref[...] = proj[:, 2 * ATTN_W:3 * ATTN_W]
        bcu_ref[...] = proj[:, 3 * ATTN_W:3 * ATTN_W + 3 * CONV_W]
        qx_ref[...] = proj[:, 3 * ATTN_W + 3 * CONV_W:PROJ_W].astype(BF16)

    return pl.pallas_call(
        body, name="in_proj_fwd", grid=(S // tm,),
        in_specs=[_rows(tm, D_MODEL), pl.BlockSpec(memory_space=pl.ANY), _rows(tm, 128), _rows(tm, 128),
                  pl.BlockSpec(memory_space=pl.ANY)],
        out_specs=[_rows(tm, ATTN_W), _rows(tm, ATTN_W), _rows(tm, ATTN_W), _rows(tm, 3 * CONV_W), _rows(tm, XATTN_W)],
        out_shape=[jax.ShapeDtypeStruct((S, ATTN_W), F32), jax.ShapeDtypeStruct((S, ATTN_W), F32),
                   jax.ShapeDtypeStruct((S, ATTN_W), F32), jax.ShapeDtypeStruct((S, 3 * CONV_W), F32),
                   jax.ShapeDtypeStruct((S, XATTN_W), BF16)],
        scratch_shapes=[pltpu.VMEM((tm, PROJ_W), F32), pltpu.VMEM((D_MODEL, PROJ_W), BF16),
                        pltpu.SemaphoreType.DMA((N_CHIPS,))],
        compiler_params=_params("arbitrary"),
    )(h, w_in, cos, sin, after)


def _memkv_fwd(mem, g_mem, w_kv, after):
    n_mem = mem.shape[0]

    def body(mem_ref, g_ref, w_ref, after_ref, mn_ref, kv_ref):
        mhat, _ = _rms_hat(mem_ref[...])
        mn = (mhat * g_ref[...]).astype(BF16)
        mn_ref[...] = mn
        kv_ref[...] = jnp.dot(mn, w_ref[...], preferred_element_type=F32).astype(BF16)

    vmem = pl.BlockSpec(memory_space=pltpu.VMEM)
    return pl.pallas_call(
        body, name="memkv_fwd", in_specs=[vmem, vmem, vmem, pl.BlockSpec(memory_space=pl.ANY)], out_specs=[vmem, vmem],
        out_shape=[jax.ShapeDtypeStruct((n_mem, D_MODEL), BF16), jax.ShapeDtypeStruct((n_mem, 2 * XATTN_W), BF16)],
        compiler_params=pltpu.CompilerParams(vmem_limit_bytes=VMEM_LIMIT_V7X),
    )(mem, g_mem, w_kv, after)


def _fill_band_bias(bias):
    row = lax.broadcasted_iota(jnp.int32, (N_BACK, 2 * N_BACK), 0)
    col = lax.broadcasted_iota(jnp.int32, (N_BACK, 2 * N_BACK), 1)
    band = (col >= row) & (col <= row + N_BACK)
    bias[1] = jnp.where(band, 0.0, NEG_INF)
    bias[0] = jnp.where(band & (col >= N_BACK), 0.0, NEG_INF)


def _strided(start, size, d):
    return pl.ds(start, size) if d == 1 else pl.ds(start, size, stride=d)


def _group_starts(g, G, nb, d):
    t0 = g * G
    r, n0 = lax.shift_right_logical(t0, nb.bit_length() - 1), lax.bitwise_and(t0, nb - 1)
    first = r + n0 * (N_BACK * d)
    before = r + jnp.maximum(n0 - 1, 0) * (N_BACK * d)
    starts = [before] + [first + u * (N_BACK * d) for u in range(G)]
    if d == 1:
        starts = [pl.multiple_of(st, N_BACK) for st in starts]
    return starts, n0


def _step_blocks(i, U, nb, d):
    G = min(U, nb)
    whole = G == nb
    row_blocks, blocks = [], []
    for grp in range(U // G):
        starts, n0 = _group_starts(i * (U // G) + grp, G, nb, d)
        base = len(row_blocks)
        if whole:
            row_blocks += [_strided(st, N_BACK, d) for st in starts[1:]]
            blocks += [(base + max(u - 1, 0), base + u, min(u, 1)) for u in range(G)]
        else:
            row_blocks += [_strided(st, N_BACK, d) for st in starts]
            blocks += [(base + u, base + u + 1, jnp.minimum(n0, 1) if u == 0 else 1) for u in range(G)]
    return row_blocks, blocks


def _by_head(a, b):
    lane = lax.broadcasted_iota(jnp.int32, (a.shape[0], 2 * HEAD), 1)
    return jnp.where(lane < HEAD, a, b)


def _head_only(t, hh):
    lane = lax.broadcasted_iota(jnp.int32, t.shape, 1)
    return jnp.where((lane < HEAD) == (hh == 0), t, jnp.zeros_like(t))


def _stack_heads(t):
    return jnp.concatenate([_head_only(t, 0), _head_only(t, 1)], axis=0)


def _head_columns(t):
    return jnp.concatenate([t[:, 0:1], t[:, HEAD:HEAD + 1]], axis=0)


def _unstack(t):
    return _by_head(t[:N_BACK], t[N_BACK:])


def _unstack_columns(t):
    return _by_head(jnp.broadcast_to(t[:N_BACK], (N_BACK, 2 * HEAD)), jnp.broadcast_to(t[N_BACK:], (N_BACK, 2 * HEAD)))


FWD_BLOCKS_PER_STEP = 4
BWD_BLOCKS_PER_STEP = 4
BWD_CHUNK = 64


def _attn_fwd(q, k, v):
    S = q.shape[0]
    U = FWD_BLOCKS_PER_STEP

    def body(q_ref, k_ref, v_ref, y_ref, m_ref, l_scr, bias):
        _fill_band_bias(bias)
        for g, d in enumerate(PATTERN_ORDER):
            nb = S // d // N_BACK
            first_pattern, last_pattern = g == 0, g == len(PATTERN_ORDER) - 1

            def step(i, carry, d=d, nb=nb, first_pattern=first_pattern, last_pattern=last_pattern):
                row_blocks, blocks = _step_blocks(i, U, nb, d)
                kb = [k_ref[r, :].astype(BF16) for r in row_blocks]
                ss = []
                for before, own, which in blocks:
                    kw = jnp.concatenate([kb[before], kb[own]], 0)
                    qs = _stack_heads(q_ref[row_blocks[own], :].astype(BF16))
                    b = bias[which]
                    ss.append(lax.dot_general(qs, kw, NT, preferred_element_type=F32) + jnp.concatenate([b, b], axis=0))
                ms = [jnp.max(s, axis=1, keepdims=True) for s in ss]
                ps = [jnp.exp(s - m) for s, m in zip(ss, ms)]
                ls = [jnp.sum(p, axis=1, keepdims=True) for p in ps]
                vb = [v_ref[r, :].astype(BF16) for r in row_blocks]
                os_ = [jnp.dot(ps[u].astype(BF16), jnp.concatenate([vb[before], vb[own]], 0), preferred_element_type=F32)
                       for u, (before, own, _) in enumerate(blocks)]
                for u, (_, own, _) in enumerate(blocks):
                    o_g, m_g, l_g = _unstack(os_[u]), _unstack_columns(ms[u]), _unstack_columns(ls[u])
                    r = row_blocks[own]
                    if first_pattern:
                        m_new, l_new, acc = m_g, l_g, o_g
                    else:
                        m_old = m_ref[r, :]
                        m_new = jnp.maximum(m_old, m_g)
                        alpha, beta = jnp.exp(m_old - m_new), jnp.exp(m_g - m_new)
                        l_new = l_scr[r, :] * alpha + l_g * beta
                        acc = y_ref[r, :] * alpha + o_g * beta
                    if last_pattern:
                        y_ref[r, :] = acc / l_new
                        m_ref[r, :] = m_new + jnp.log(l_new)
                    else:
                        y_ref[r, :] = acc
                        m_ref[r, :] = m_new
                        l_scr[r, :] = l_new
                return carry

            lax.fori_loop(0, d * nb // U, step, 0)

    col = pl.BlockSpec((S, 2 * HEAD), lambda j: (0, j))
    return pl.pallas_call(
        body, name="attn_fwd", grid=(q.shape[1] // (2 * HEAD),),
        in_specs=[col, col, col], out_specs=[col, col],
        out_shape=[jax.ShapeDtypeStruct(q.shape, F32)] * 2,
        scratch_shapes=[pltpu.VMEM((S, 2 * HEAD), F32), pltpu.VMEM((2, N_BACK, 2 * N_BACK), F32)],
        compiler_params=_params("parallel"),
    )(q, k, v)


def _attn_bwd(q, k, v, dy, lse, delta, after):
    S = q.shape[0]
    U = BWD_BLOCKS_PER_STEP

    def body(q_ref, k_ref, v_ref, dy_ref, lse_ref, delta_ref, after_ref, dq_ref, dk_ref, dv_ref, bias):
        _fill_band_bias(bias)
        nb_first = S // PATTERN_ORDER[0] // N_BACK
        first_writes_all = min(U, nb_first) == nb_first
        if not first_writes_all:
            dk_ref[...] = jnp.zeros_like(dk_ref)
            dv_ref[...] = jnp.zeros_like(dv_ref)
        for g, d in enumerate(PATTERN_ORDER):
            nb = S // d // N_BACK

            def step(i, carry, d=d, nb=nb, g=g):
                row_blocks, blocks = _step_blocks(i, U, nb, d)
                kb = [k_ref[r, :].astype(BF16) for r in row_blocks]
                vb = [v_ref[r, :].astype(BF16) for r in row_blocks]
                kws = [jnp.concatenate([kb[before], kb[own]], 0) for before, own, _ in blocks]
                vws = [jnp.concatenate([vb[before], vb[own]], 0) for before, own, _ in blocks]
                qss = [_stack_heads(q_ref[row_blocks[own], :].astype(BF16)) for _, own, _ in blocks]
                doss = [_stack_heads(dy_ref[row_blocks[own], :].astype(BF16)) for _, own, _ in blocks]
                ss = [lax.dot_general(qss[u], kws[u], NT, preferred_element_type=F32) for u in range(U)]
                dps = [lax.dot_general(doss[u], vws[u], NT, preferred_element_type=F32) for u in range(U)]
                pbs, dss = [], []
                for u, (_, own, which) in enumerate(blocks):
                    lse_c = _head_columns(lse_ref[row_blocks[own], :])
                    delta_c = _head_columns(delta_ref[row_blocks[own], :])
                    p_parts, ds_parts = [], []
                    for r0 in range(0, 2 * N_BACK, BWD_CHUNK):
                        r = slice(r0, r0 + BWD_CHUNK)
                        mask = bias[which, r0 % N_BACK:r0 % N_BACK + BWD_CHUNK, :]
                        p_r = jnp.exp(ss[u][r] + mask - lse_c[r])
                        p_parts.append(p_r.astype(BF16))
                        ds_parts.append((p_r * (dps[u][r] - delta_c[r])).astype(BF16))
                    pbs.append(jnp.concatenate(p_parts, axis=0))
                    dss.append(jnp.concatenate(ds_parts, axis=0))
                dqs = [jnp.dot(dss[u], kws[u], preferred_element_type=F32) for u in range(U)]
                dkws = [lax.dot_general(dss[u], qss[u], TN, preferred_element_type=F32) for u in range(U)]
                dvws = [lax.dot_general(pbs[u], doss[u], TN, preferred_element_type=F32) for u in range(U)]
                dk_parts, dv_parts = [None] * len(row_blocks), [None] * len(row_blocks)
                for u, (before, own, _) in enumerate(blocks):
                    dq = _unstack(dqs[u])
                    if g == 0:
                        dq_ref[row_blocks[own], :] = dq
                    else:
                        dq_ref[row_blocks[own], :] += dq
                    for idx, dkp, dvp in ((before, dkws[u][:N_BACK], dvws[u][:N_BACK]),
                                          (own, dkws[u][N_BACK:], dvws[u][N_BACK:])):
                        dk_parts[idx] = dkp if dk_parts[idx] is None else dk_parts[idx] + dkp
                        dv_parts[idx] = dvp if dv_parts[idx] is None else dv_parts[idx] + dvp
                for idx, r in enumerate(row_blocks):
                    if g == 0 and first_writes_all:
                        dk_ref[r, :] = dk_parts[idx]
                        dv_ref[r, :] = dv_parts[idx]
                    else:
                        dk_ref[r, :] += dk_parts[idx]
                        dv_ref[r, :] += dv_parts[idx]
                return carry

            lax.fori_loop(0, d * nb // U, step, 0)

    col = pl.BlockSpec((S, 2 * HEAD), lambda j: (0, j))
    return pl.pallas_call(
        body, name="attn_bwd", grid=(q.shape[1] // (2 * HEAD),),
        in_specs=[col] * 6 + [pl.BlockSpec(memory_space=pl.ANY)], out_specs=[col] * 3,
        out_shape=[jax.ShapeDtypeStruct(q.shape, F32)] * 3,
        scratch_shapes=[pltpu.VMEM((2, N_BACK, 2 * N_BACK), F32)],
        compiler_params=_params("parallel"),
    )(q, k, v, dy, lse, delta, after)


def _shift_down(z, before, k):
    row = lax.broadcasted_iota(jnp.int32, z.shape, 0)
    out = pltpu.roll(z, k, 0)
    for i in range(k):
        out = jnp.where(row == i, before[8 - k + i:8 - k + i + 1, :], out)
    return out


def _shift_up(z, after, k):
    rows = z.shape[0]
    row = lax.broadcasted_iota(jnp.int32, z.shape, 0)
    out = pltpu.roll(z, rows - k, 0)
    for i in range(k):
        out = jnp.where(row == rows - k + i, after[i:i + 1, :], out)
    return out


def _conv_fwd(bcu, before, is_first, w):
    b, c, u = bcu[:, 0:CONV_W], bcu[:, CONV_W:2 * CONV_W], bcu[:, 2 * CONV_W:3 * CONV_W]
    z = c * u
    zb = jnp.where(is_first, 0.0, before[:, CONV_W:2 * CONV_W] * before[:, 2 * CONV_W:3 * CONV_W])
    z1, z2 = _shift_down(z, zb, 1), _shift_down(z, zb, 2)
    cv = w[0:1, :] * z2 + w[1:2, :] * z1 + w[2:3, :] * z
    return b, c, u, z, z1, z2, cv


def _halo_before(tm, width):
    return pl.BlockSpec((8, width), lambda i: (jnp.maximum(i * (tm // 8) - 1, 0), 0))


def _mix_fwd(ya, bcu, qx, mkv, conv_w, g_a, g_c, g_x, w_out, g_post, x, tm):
    S = x.shape[0]

    def body(ya_ref, bcu_ref, before_ref, qx_ref, mkv_ref, cw_ref, ga_ref, gc_ref, gx_ref,
             wo_ref, gp_ref, x_ref, yx_ref, ycat_ref, y2_ref, x1_ref):
        ya = ya_ref[...]
        b, _, _, _, _, _, cv = _conv_fwd(bcu_ref[...], before_ref[...], pl.program_id(0) == 0, cw_ref[...])
        yc = b * cv

        qxb, mkvb = qx_ref[...], mkv_ref[...]
        heads = [slice(HEAD * hd, HEAD * (hd + 1)) for hd in range(XATTN_W // HEAD)]
        ss = [lax.dot_general(qxb[:, sl], mkvb[:, sl], NT, preferred_element_type=F32) * SCALE for sl in heads]
        ms = [jnp.max(s, axis=1, keepdims=True) for s in ss]
        ps = [jnp.exp(s - m) for s, m in zip(ss, ms)]
        ls = [jnp.sum(p, axis=1, keepdims=True) for p in ps]
        os_ = [jnp.dot(p.astype(BF16), mkvb[:, XATTN_W + sl.start:XATTN_W + sl.stop], preferred_element_type=F32)
               for p, sl in zip(ps, heads)]
        for sl, o, l in zip(heads, os_, ls):
            yx_ref[:, sl] = o / l
        yx = yx_ref[...]

        ycat_ref[:, 0:ATTN_W] = (_rms_hat(ya)[0] * ga_ref[...]).astype(BF16)
        ycat_ref[:, ATTN_W:ATTN_W + CONV_W] = (_rms_hat(yc)[0] * gc_ref[...]).astype(BF16)
        ycat_ref[:, ATTN_W + CONV_W:D_MODEL] = (_rms_hat(yx)[0] * gx_ref[...]).astype(BF16)
        y2 = jnp.dot(ycat_ref[...], wo_ref[...], preferred_element_type=F32)
        y2_ref[...] = y2
        x1_ref[...] = x_ref[...] + _rms_hat(y2)[0] * gp_ref[...]

    n_mem = mkv.shape[0]
    return pl.pallas_call(
        body, name="mix_fwd", grid=(S // tm,),
        in_specs=[_rows(tm, ATTN_W), _rows(tm, 3 * CONV_W), _halo_before(tm, 3 * CONV_W), _rows(tm, XATTN_W),
                  _resident((n_mem, 2 * XATTN_W)), _resident((3, CONV_W)), _resident((1, ATTN_W)),
                  _resident((1, CONV_W)), _resident((1, XATTN_W)), _resident((D_MODEL, D_MODEL)),
                  _resident((1, D_MODEL)), _rows(tm, D_MODEL)],
        out_specs=[_rows(tm, XATTN_W), _rows(tm, D_MODEL), _rows(tm, D_MODEL), _rows(tm, D_MODEL)],
        out_shape=[jax.ShapeDtypeStruct((S, XATTN_W), F32), jax.ShapeDtypeStruct((S, D_MODEL), BF16),
                   jax.ShapeDtypeStruct((S, D_MODEL), F32), jax.ShapeDtypeStruct((S, D_MODEL), F32)],
        compiler_params=_params("parallel"),
    )(ya, bcu, bcu, qx, mkv, conv_w, g_a, g_c, g_x, w_out, g_post, x)


def _mlp_fwd_bwd(x1, target, g_pre, g_post, w_up, w_down, tm):
    S = x1.shape[0]
    n_ff = D_FF // SHARD_FF

    def body(x1_ref, t_ref, gpre_ref, gpost_ref, wup_ref, wdn_ref,
             h2_ref, f_ref, du_ref, df2_ref, dx1_ref, dgpre_ref, dgpost_ref, loss_ref, u_scr):
        @pl.when(pl.program_id(0) == 0)
        def _():
            dgpre_ref[...] = jnp.zeros_like(dgpre_ref)
            dgpost_ref[...] = jnp.zeros_like(dgpost_ref)
            loss_ref[...] = jnp.zeros_like(loss_ref)

        x1 = x1_ref[...]
        x1hat, r1 = _rms_hat(x1)
        h2 = (x1hat * gpre_ref[...]).astype(BF16)
        h2_ref[...] = h2
        f2 = jnp.zeros((tm, D_MODEL), F32)
        for j in range(n_ff):
            cols = slice(SHARD_FF * j, SHARD_FF * (j + 1))
            u = jnp.maximum(jnp.dot(h2, wup_ref[j], preferred_element_type=F32), 0.0)
            u_scr[:, cols] = u
            f = (u * u).astype(BF16)
            f_ref[:, cols] = f
            f2 = f2 + jnp.dot(f, wdn_ref[cols, :], preferred_element_type=F32)
        f2hat, r2 = _rms_hat(f2)
        err = x1 + f2hat * gpost_ref[...] - t_ref[...]
        loss_ref[...] += 0.5 * jnp.sum(jnp.mean(err * err, axis=-1, keepdims=True), axis=0, keepdims=True)
        dx2 = err * (1.0 / D_MODEL)
        dgpost_ref[...] += jnp.sum(dx2 * f2hat, axis=0, keepdims=True)
        df2 = _rms_bwd(f2hat, r2, gpost_ref[...], dx2).astype(BF16)
        df2_ref[...] = df2
        dh2 = jnp.zeros((tm, D_MODEL), F32)
        for j in range(n_ff):
            cols = slice(SHARD_FF * j, SHARD_FF * (j + 1))
            df = lax.dot_general(df2, wdn_ref[cols, :], NT, preferred_element_type=F32)
            du = (2.0 * u_scr[:, cols] * df).astype(BF16)
            du_ref[:, cols] = du
            dh2 = dh2 + lax.dot_general(du, wup_ref[j], NT, preferred_element_type=F32)
        dgpre_ref[...] += jnp.sum(dh2 * x1hat, axis=0, keepdims=True)
        dx1_ref[...] = dx2 + _rms_bwd(x1hat, r1, gpre_ref[...], dh2)

    acc = pl.BlockSpec((1, D_MODEL), lambda i: (0, 0))
    return pl.pallas_call(
        body, name="mlp_fwd_bwd", grid=(S // tm,),
        in_specs=[_rows(tm, D_MODEL), _rows(tm, D_MODEL), _resident((1, D_MODEL)), _resident((1, D_MODEL)),
                  _resident((n_ff, D_MODEL, SHARD_FF)), _resident((D_FF, D_MODEL))],
        out_specs=[_rows(tm, D_MODEL), _rows(tm, D_FF), _rows(tm, D_FF), _rows(tm, D_MODEL), _rows(tm, D_MODEL),
                   acc, acc, pl.BlockSpec((1, 1), lambda i: (0, 0))],
        out_shape=[jax.ShapeDtypeStruct((S, D_MODEL), BF16), jax.ShapeDtypeStruct((S, D_FF), BF16),
                   jax.ShapeDtypeStruct((S, D_FF), BF16), jax.ShapeDtypeStruct((S, D_MODEL), BF16),
                   jax.ShapeDtypeStruct((S, D_MODEL), F32), jax.ShapeDtypeStruct((1, D_MODEL), F32),
                   jax.ShapeDtypeStruct((1, D_MODEL), F32), jax.ShapeDtypeStruct((1, 1), F32)],
        scratch_shapes=[pltpu.VMEM((tm, D_FF), F32)],
        compiler_params=_params("arbitrary"),
    )(x1, target, g_pre, g_post, w_up, w_down)


def _weight_grad(name, a, b, rows_sharded, after):
    S, K = a.shape
    N = b.shape[1]
    if rows_sharded:
        tk, tn = K // N_CHIPS, N
        a_spec = pl.BlockSpec((S, tk), lambda j: (0, j))
        b_spec = pl.BlockSpec((S, tn), lambda j: (0, 0), pipeline_mode=pl.Buffered(1))
    else:
        tk, tn = K, N // N_CHIPS
        a_spec = pl.BlockSpec((S, tk), lambda j: (0, 0), pipeline_mode=pl.Buffered(1))
        b_spec = pl.BlockSpec((S, tn), lambda j: (0, j))
    half = tk // 2

    def body(a_ref, b_ref, after_ref, o_ref):
        res = lax.dot_general(a_ref[...], b_ref[...], TN, preferred_element_type=F32)
        o_ref[0, 0] = res[:half]
        o_ref[1, 0] = res[half:]

    return pl.pallas_call(
        body, name=name, grid=(N_CHIPS,), in_specs=[a_spec, b_spec, pl.BlockSpec(memory_space=pl.ANY)],
        out_specs=pl.BlockSpec((2, 1, half, tn), lambda j: (0, j, 0, 0)),
        out_shape=jax.ShapeDtypeStruct((2, N_CHIPS, half, tn), F32),
        compiler_params=_params("parallel"),
    )(a, b, after)


def _mixer_bwd(dx1, y2, ycat, ya, yx, bcu, qx, mkv, conv_w, g_a, g_c, g_x, w_out, g_post, after, tm):
    S = dx1.shape[0]
    n_mem = mkv.shape[0]
    n_tiles = S // tm
    half = D_MODEL // N_CHIPS // 2

    def body(dx1_ref, y2_ref, ycat_ref, ya_ref, yx_ref, bcu_ref, before_ref, qx_ref, mkv_ref, cw_ref, ga_ref, gc_ref,
             gx_ref, wo_ref, gp_ref, after_ref, gwo_ref, dya_ref, delta_ref, tail_ref, dmkv_ref, dcw_ref, dgp_ref,
             dga_ref, dgc_ref, dgx_ref, carry):
        step = pl.program_id(0)
        first_tile = step == n_tiles - 1

        @pl.when(step == 0)
        def _():
            for ref in (gwo_ref, dmkv_ref, dcw_ref, dgp_ref, dga_ref, dgc_ref, dgx_ref, carry):
                ref[...] = jnp.zeros_like(ref)

        dx1 = dx1_ref[...]
        y2hat, r2 = _rms_hat(y2_ref[...])
        dgp_ref[...] += jnp.sum(dx1 * y2hat, axis=0, keepdims=True)
        dy2 = _rms_bwd(y2hat, r2, gp_ref[...], dx1).astype(BF16)
        gwo = lax.dot_general(ycat_ref[...], dy2, TN, preferred_element_type=F32)
        for k in range(2 * N_CHIPS):
            gwo_ref[k % 2, k // 2] += gwo[half * k:half * (k + 1)]
        dycat = lax.dot_general(dy2, wo_ref[...], NT, preferred_element_type=F32)

        d_na = dycat[:, 0:ATTN_W]
        ya = ya_ref[...]
        yahat, ra = _rms_hat(ya)
        dga_ref[...] += jnp.sum(d_na * yahat, axis=0, keepdims=True)
        dya = _rms_bwd(yahat, ra, ga_ref[...], d_na)
        dya_ref[...] = dya
        prod = dya * ya
        hi = prod.astype(BF16)
        lo = (prod - hi.astype(F32)).astype(BF16)
        head_of = lambda axis: lax.shift_right_logical(lax.broadcasted_iota(jnp.int32, (ATTN_W, ATTN_W), axis),
                                                       HEAD.bit_length() - 1)
        ones = jnp.where(head_of(0) == head_of(1), 1.0, 0.0).astype(BF16)
        delta_ref[...] = jnp.dot(hi, ones, preferred_element_type=F32) + jnp.dot(lo, ones, preferred_element_type=F32)

        w = cw_ref[...]
        b, c, u, z, z1, z2, cv = _conv_fwd(bcu_ref[...], before_ref[...], first_tile, w)
        d_nc = dycat[:, ATTN_W:ATTN_W + CONV_W]
        ychat, rc = _rms_hat(b * cv)
        dgc_ref[...] += jnp.sum(d_nc * ychat, axis=0, keepdims=True)
        dyc = _rms_bwd(ychat, rc, gc_ref[...], d_nc)
        dcv = dyc * b
        behind = carry[...]
        dz = w[2:3, :] * dcv + w[1:2, :] * _shift_up(dcv, behind, 1) + w[0:1, :] * _shift_up(dcv, behind, 2)
        carry[...] = dcv[0:8, :]
        dcw_ref[0:1, :] += jnp.sum(dcv * z2, axis=0, keepdims=True)
        dcw_ref[1:2, :] += jnp.sum(dcv * z1, axis=0, keepdims=True)
        dcw_ref[2:3, :] += jnp.sum(dcv * z, axis=0, keepdims=True)
        tail_ref[:, 0:CONV_W] = (dyc * cv).astype(BF16)
        tail_ref[:, CONV_W:2 * CONV_W] = (dz * u).astype(BF16)
        tail_ref[:, 2 * CONV_W:3 * CONV_W] = (dz * c).astype(BF16)

        d_nx = dycat[:, ATTN_W + CONV_W:D_MODEL]
        yxhat, rx = _rms_hat(yx_ref[...])
        dgx_ref[...] += jnp.sum(d_nx * yxhat, axis=0, keepdims=True)
        dyx = _rms_bwd(yxhat, rx, gx_ref[...], d_nx)
        qxb, mkvb = qx_ref[...], mkv_ref[...]
        heads = [slice(HEAD * hd, HEAD * (hd + 1)) for hd in range(XATTN_W // HEAD)]
        values = [slice(XATTN_W + sl.start, XATTN_W + sl.stop) for sl in heads]
        ss = [lax.dot_general(qxb[:, sl], mkvb[:, sl], NT, preferred_element_type=F32) * SCALE for sl in heads]
        es = [jnp.exp(s - jnp.max(s, axis=1, keepdims=True)) for s in ss]
        ps = [e / jnp.sum(e, axis=1, keepdims=True) for e in es]
        dobs = [dyx[:, sl].astype(BF16) for sl in heads]
        dps = [lax.dot_general(dob, mkvb[:, vsl], NT, preferred_element_type=F32) for dob, vsl in zip(dobs, values)]
        dss = [(p * (dp - jnp.sum(p * dp, axis=1, keepdims=True)) * SCALE).astype(BF16) for p, dp in zip(ps, dps)]
        for sl, vsl, p, dob, ds in zip(heads, values, ps, dobs, dss):
            tail_ref[:, 3 * CONV_W + sl.start:3 * CONV_W + sl.stop] = jnp.dot(
                ds, mkvb[:, sl], preferred_element_type=F32).astype(BF16)
            dmkv_ref[:, sl] += lax.dot_general(ds, qxb[:, sl], TN, preferred_element_type=F32)
            dmkv_ref[:, vsl] += lax.dot_general(p.astype(BF16), dob, TN, preferred_element_type=F32)

    rows = lambda width: pl.BlockSpec((tm, width), lambda i: (n_tiles - 1 - i, 0))
    before = pl.BlockSpec((8, 3 * CONV_W), lambda i: (jnp.maximum((n_tiles - 1 - i) * (tm // 8) - 1, 0), 0))
    acc = lambda r, w: pl.BlockSpec((r, w), lambda i: (0, 0))
    return pl.pallas_call(
        body, name="mixer_bwd", grid=(n_tiles,),
        in_specs=[rows(D_MODEL), rows(D_MODEL), rows(D_MODEL), rows(ATTN_W), rows(XATTN_W), rows(3 * CONV_W), before,
                  rows(XATTN_W), _resident((n_mem, 2 * XATTN_W)), _resident((3, CONV_W)), _resident((1, ATTN_W)),
                  _resident((1, CONV_W)), _resident((1, XATTN_W)), _resident((D_MODEL, D_MODEL)),
                  _resident((1, D_MODEL)), pl.BlockSpec(memory_space=pl.ANY)],
        out_specs=[pl.BlockSpec((2, N_CHIPS, half, D_MODEL), lambda i: (0, 0, 0, 0)), rows(ATTN_W), rows(ATTN_W),
                   rows(3 * CONV_W + XATTN_W), acc(n_mem, 2 * XATTN_W),
                   acc(3, CONV_W), acc(1, D_MODEL), acc(1, ATTN_W), acc(1, CONV_W), acc(1, XATTN_W)],
        out_shape=[jax.ShapeDtypeStruct((2, N_CHIPS, half, D_MODEL), F32), jax.ShapeDtypeStruct((S, ATTN_W), F32),
                   jax.ShapeDtypeStruct((S, ATTN_W), F32), jax.ShapeDtypeStruct((S, 3 * CONV_W + XATTN_W), BF16),
                   jax.ShapeDtypeStruct((n_mem, 2 * XATTN_W), F32), jax.ShapeDtypeStruct((3, CONV_W), F32),
                   jax.ShapeDtypeStruct((1, D_MODEL), F32), jax.ShapeDtypeStruct((1, ATTN_W), F32),
                   jax.ShapeDtypeStruct((1, CONV_W), F32), jax.ShapeDtypeStruct((1, XATTN_W), F32)],
        scratch_shapes=[pltpu.VMEM((8, CONV_W), F32)],
        compiler_params=_params("arbitrary"),
    )(dx1, y2, ycat, ya, yx, bcu, bcu, qx, mkv, conv_w, g_a, g_c, g_x, w_out, g_post, after)


def _memkv_bwd(mem, g_mem, w_kv, dmkv):
    n_mem = mem.shape[0]
    half = D_MODEL // N_CHIPS // 2

    def body(mem_ref, g_ref, w_ref, d_ref, dw_ref, dg_ref):
        mhat, _ = _rms_hat(mem_ref[...])
        mn = (mhat * g_ref[...]).astype(BF16)
        d = d_ref[...].astype(BF16)
        for k in range(2 * N_CHIPS):
            dw_ref[k % 2, k // 2] = lax.dot_general(mn[:, half * k:half * (k + 1)], d, TN, preferred_element_type=F32)
        dmn = lax.dot_general(d, w_ref[...], NT, preferred_element_type=F32)
        dg_ref[...] = jnp.sum(dmn * mhat, axis=0, keepdims=True)

    return pl.pallas_call(
        body, name="memkv_bwd",
        out_shape=[jax.ShapeDtypeStruct((2, N_CHIPS, half, 2 * XATTN_W), F32), jax.ShapeDtypeStruct((1, D_MODEL), F32)],
        compiler_params=pltpu.CompilerParams(vmem_limit_bytes=VMEM_LIMIT_V7X),
    )(mem, g_mem, w_kv, dmkv)


def _in_proj_bwd(dqkv, tail, cos, sin, w_in, x, h, g, dx1, after, tm):
    S = x.shape[0]
    step_w = 2 * 256
    half = D_MODEL // 2

    def body(dq_ref, dk_ref, dv_ref, tail_ref, cos_ref, sin_ref, w_hbm, x_ref, h_ref, g_ref, dx1_ref, after_ref,
             dx_ref, gw_ref, dg_ref, dproj_ref, w_full, sems):
        _side_by_side(w_hbm, w_full, sems)

        @pl.when(pl.program_id(0) == 0)
        def _():
            dg_ref[...] = jnp.zeros_like(dg_ref)
            gw_ref[...] = jnp.zeros_like(gw_ref)

        halves = [slice(0, tm // 2), slice(tm // 2, tm)]
        for rows in halves:
            c, s = cos_ref[rows, :], sin_ref[rows, :]
            for j in range(ATTN_W // 128):
                cols = slice(128 * j, 128 * (j + 1))
                dproj_ref[rows, cols] = _rope128(dq_ref[rows, cols] * SCALE, c, s, True).astype(BF16)
                dproj_ref[rows, ATTN_W + 128 * j:ATTN_W + 128 * (j + 1)] = _rope128(dk_ref[rows, cols], c, s, True).astype(BF16)
            dproj_ref[rows, 2 * ATTN_W:3 * ATTN_W] = dv_ref[rows, :].astype(BF16)
            dproj_ref[rows, 3 * ATTN_W:PROJ_W] = tail_ref[rows, :]
        dhs = [lax.dot_general(dproj_ref[rows, :], w_full[...], NT, preferred_element_type=F32) for rows in halves]
        for rows, dh in zip(halves, dhs):
            xhat, r = _rms_hat(x_ref[rows, :])
            dg_ref[...] += jnp.sum(dh * xhat, axis=0, keepdims=True)
            dx_ref[rows, :] = dx1_ref[rows, :] + _rms_bwd(xhat, r, g_ref[...], dh)
        hb = h_ref[...]
        for step in range(PROJ_W // step_w):
            res = lax.dot_general(hb, dproj_ref[:, step * step_w:(step + 1) * step_w], TN, preferred_element_type=F32)
            lo = step * step_w
            while lo < (step + 1) * step_w:
                chip = lo // SHARD_IN
                hi = min((step + 1) * step_w, (chip + 1) * SHARD_IN)
                for hh in range(2):
                    gw_ref[hh, chip, :, lo - chip * SHARD_IN:hi - chip * SHARD_IN] += (
                        res[half * hh:half * (hh + 1), lo - step * step_w:hi - step * step_w])
                lo = hi

    whole = lambda shape: pl.BlockSpec(shape, lambda i: (0,) * len(shape))
    return pl.pallas_call(
        body, name="in_proj_bwd", grid=(S // tm,),
        in_specs=[_rows(tm, ATTN_W)] * 3 + [_rows(tm, PROJ_W - 3 * ATTN_W), _rows(tm, 128), _rows(tm, 128),
                  pl.BlockSpec(memory_space=pl.ANY), _rows(tm, D_MODEL), _rows(tm, D_MODEL), _resident((1, D_MODEL)),
                  _rows(tm, D_MODEL), pl.BlockSpec(memory_space=pl.ANY)],
        out_specs=[_rows(tm, D_MODEL), whole((2, N_CHIPS, half, SHARD_IN)), whole((1, D_MODEL))],
        out_shape=[jax.ShapeDtypeStruct((S, D_MODEL), F32), jax.ShapeDtypeStruct((2, N_CHIPS, half, SHARD_IN), F32),
                   jax.ShapeDtypeStruct((1, D_MODEL), F32)],
        scratch_shapes=[pltpu.VMEM((tm, PROJ_W), BF16), pltpu.VMEM((D_MODEL, PROJ_W), BF16),
                        pltpu.SemaphoreType.DMA((N_CHIPS,))],
        compiler_params=_params("arbitrary"),
    )(*dqkv, tail, cos, sin, w_in, x, h, g, dx1, after)


def _row_tile(rows):
    return ROW_TILE if rows % ROW_TILE == 0 else rows


def _chip_sums_bf16(name, grads, from_sibling, place):
    k = len(grads)
    _, n, rows, _ = grads[0].shape
    tr = _row_tile(rows)

    def body(place_ref, *refs):
        for g_ref, b_ref, o_ref in zip(refs[:k], refs[k:2 * k], refs[2 * k:]):
            o_ref[...] = (g_ref[0] + b_ref[...]).astype(BF16)

    mine = lambda g: pl.BlockSpec((1, 1, tr, g.shape[3]), lambda s, i, p: (p[0], s, i, 0))
    slab = lambda g: pl.BlockSpec((1, tr, g.shape[3]), lambda s, i, p: (s, i, 0))
    return pl.pallas_call(
        body, name=name, out_shape=[jax.ShapeDtypeStruct(g.shape[1:], BF16) for g in grads],
        grid_spec=pltpu.PrefetchScalarGridSpec(
            num_scalar_prefetch=1, grid=(n, rows // tr),
            in_specs=[mine(g) for g in grads] + [slab(g) for g in grads], out_specs=[slab(g) for g in grads]),
        compiler_params=_params("parallel", "parallel"),
    )(place, *grads, *from_sibling)


def _final_sums(name, grads, from_sibling, others, place):
    k = len(grads)
    rows = grads[0].shape[2]
    tr = _row_tile(rows)

    def body(place_ref, *refs):
        for a in range(k):
            own_ref, sib_ref = refs[a], refs[k + a]
            acc = own_ref[0, 0] + sib_ref[0]
            for o in refs[2 * k + 3 * a:2 * k + 3 * a + 3]:
                acc = acc + o[0].astype(F32)
            refs[5 * k + a][0] = acc

    own = lambda g: pl.BlockSpec((1, 1, tr, g.shape[3]), lambda i, p: (p[0], p[1], i, 0))
    sib = lambda g: pl.BlockSpec((1, tr, g.shape[3]), lambda i, p: (p[1], i, 0))
    other = lambda g, j: pl.BlockSpec((1, tr, g.shape[3]), lambda i, p: (j, i, 0))
    return pl.pallas_call(
        body, name=name, out_shape=[jax.ShapeDtypeStruct((2,) + g.shape[2:], F32) for g in grads],
        grid_spec=pltpu.PrefetchScalarGridSpec(
            num_scalar_prefetch=1, grid=(rows // tr,),
            in_specs=[own(g) for g in grads] + [sib(g) for g in grads] + [other(g, j) for g in grads for j in range(3)],
            out_specs=[pl.BlockSpec((1, tr, g.shape[3]), lambda i, p: (p[0], i, 0)) for g in grads]),
        compiler_params=_params("parallel"),
    )(place, *grads, *from_sibling, *[o for o in others for _ in range(3)])


def _adamw_update(w, g, m, v):
    m = ADAM_B1 * m + (1.0 - ADAM_B1) * g
    v = ADAM_B2 * v + (1.0 - ADAM_B2) * (g * g)
    m_hat = m * (1.0 / (1.0 - ADAM_B1 ** ADAM_STEP))
    v_hat = v * (1.0 / (1.0 - ADAM_B2 ** ADAM_STEP))
    return -ADAM_LR * (m_hat / (jnp.sqrt(v_hat) + ADAM_EPS) + ADAM_WD * w), m, v


def _adamw(name, params, after):
    k = len(params)
    rows = params[0][0].shape[0]
    tr = ADAMW_ROW_TILE if rows % ADAMW_ROW_TILE == 0 else rows

    def body(*refs):
        ins, outs = refs[:4 * k], refs[4 * k + 1:]
        for a in range(k):
            w_ref, g_ref, m_ref, v_ref = ins[4 * a:4 * a + 4]
            g = g_ref[...]
            outs[4 * a][...] = g
            outs[4 * a + 1][...], outs[4 * a + 2][...], outs[4 * a + 3][...] = _adamw_update(w_ref[...], g, m_ref[...], v_ref[...])

    spec = lambda w: pl.BlockSpec((tr, w.shape[1]), lambda i: (i, 0))
    out = pl.pallas_call(
        body, name=name, grid=(rows // tr,),
        in_specs=[spec(p[0]) for p in params for _ in range(4)] + [pl.BlockSpec(memory_space=pl.ANY)],
        out_specs=[spec(p[0]) for p in params for _ in range(4)],
        out_shape=[jax.ShapeDtypeStruct(p[0].shape, F32) for p in params for _ in range(4)],
        compiler_params=_params("parallel"),
    )(*[t for p in params for t in p], after)
    return [out[4 * a:4 * a + 4] for a in range(k)]


def _small_update(summed, chip, gains, gains_m, gains_v, taps, taps_m, taps_v):
    n = len(gains)
    widths = [g.shape[1] for g in gains]
    k, w = taps.shape

    def body(*refs):
        chip_ref, sum_ref = refs[0], refs[1]
        params = [refs[2 + 3 * i:5 + 3 * i] for i in range(n + 1)]
        outs = [refs[2 + 3 * (n + 1) + 4 * i:2 + 3 * (n + 1) + 4 * (i + 1)] for i in range(n + 1)]
        loss_ref = refs[-1]
        for i in range(n):
            g = sum_ref[i:i + 1, 0:widths[i]]
            wr, mr, vr = params[i]
            outs[i][0][...] = g
            outs[i][1][...], outs[i][2][...], outs[i][3][...] = _adamw_update(wr[...], g, mr[...], vr[...])
        g = sum_ref[n:n + k, 0:w]
        for j in range(1, N_CHIPS):
            g = jnp.where(chip_ref[0] == j, sum_ref[n:n + k, w * j:w * (j + 1)], g)
        wr, mr, vr = params[n]
        outs[n][0][...] = g
        outs[n][1][...], outs[n][2][...], outs[n][3][...] = _adamw_update(wr[...], g, mr[...], vr[...])
        loss_ref[...] = sum_ref[n + k:n + k + 1, 0:1]

    vmem = pl.BlockSpec(memory_space=pltpu.VMEM)
    operands = [chip, summed]
    for p in zip(list(gains) + [taps], list(gains_m) + [taps_m], list(gains_v) + [taps_v]):
        operands += list(p)
    shapes = [jax.ShapeDtypeStruct(p.shape, F32) for p in list(gains) + [taps] for _ in range(4)]
    out = pl.pallas_call(
        body, name="small_update", out_shape=shapes + [jax.ShapeDtypeStruct((1, 1), F32)],
        in_specs=[pl.BlockSpec(memory_space=pltpu.SMEM)] + [vmem] * (len(operands) - 1),
        out_specs=[vmem] * (len(shapes) + 1),
    )(*operands)
    return [out[4 * i:4 * (i + 1)] for i in range(n + 1)], out[-1]


def _sum_blocks(name, blocks):
    n, rows, cols = blocks.shape

    def body(b_ref, o_ref):
        acc = b_ref[0]
        for k in range(1, n):
            acc = acc + b_ref[k]
        o_ref[...] = acc

    return pl.pallas_call(body, name=name, out_shape=jax.ShapeDtypeStruct((rows, cols), F32))(blocks)


def _place():
    return lax.axis_index("x"), lax.axis_index("y"), lax.axis_index("c")


def _other_chips(x, y):
    return [(1 - x, y), (x, 1 - y), (1 - x, 1 - y)]


def _allgather_finish(name, shards, landed, pass_on):
    n = len(shards)

    def body(*refs):
        ins, outs, stage = refs[:n], refs[2 * n:3 * n], refs[3 * n:4 * n]
        send_sems, recv_sems, local_sems = refs[4 * n:]
        x, y, c = _place()
        chips = _other_chips(x, y)
        barrier = pltpu.get_barrier_semaphore()
        pl.semaphore_signal(barrier, inc=1, device_id=(x, y, 1 - c), device_id_type=MESH)
        pl.semaphore_wait(barrier, 1)

        def copy(a, k, chip, half):
            place = outs[a].at[2 * chip[0] + chip[1], half]
            return pltpu.make_async_remote_copy(
                src_ref=place, dst_ref=place, send_sem=send_sems.at[3 * a + k], recv_sem=recv_sems.at[3 * a + k],
                device_id=(x, y, 1 - c), device_id_type=MESH)

        load = [pltpu.make_async_copy(ins[a], stage[a], local_sems.at[a]) for a in range(n)]
        local = [pltpu.make_async_copy(stage[a], outs[a].at[2 * x + y], local_sems.at[a]) for a in range(n)]
        for cp in load:
            cp.start()
        passed = [copy(a, k, chip, c) for a in range(n) if pass_on[a] for k, chip in enumerate(chips)]
        for cp in passed:
            cp.start()
        for a in range(n):
            load[a].wait()
            local[a].start()
        for a in range(n):
            if pass_on[a]:
                for k, chip in enumerate(chips):
                    copy(a, k, chip, 1 - c).wait_recv()
        for cp in passed:
            cp.wait_send()
        for cp in local:
            cp.wait()

    any_spec = pl.BlockSpec(memory_space=pl.ANY)
    return pl.pallas_call(
        body, name=name,
        out_shape=[jax.ShapeDtypeStruct((N_CHIPS,) + s.shape, s.dtype) for s in shards],
        in_specs=[any_spec] * (2 * n), out_specs=[any_spec] * n,
        input_output_aliases={n + a: a for a in range(n)},
        scratch_shapes=[pltpu.VMEM(s.shape, s.dtype) for s in shards]
        + [pltpu.SemaphoreType.DMA((3 * n,)), pltpu.SemaphoreType.DMA((3 * n,)), pltpu.SemaphoreType.DMA((n,))],
        compiler_params=pltpu.CompilerParams(vmem_limit_bytes=VMEM_LIMIT_V7X, collective_id=HANDSHAKES["sibling"][0]),
    )(*shards, *landed)


def _plan_first_hop(x, y, c, shards, lands):
    return [(shards[a].at[c], lands[a].at[2 * x + y, c], lands[a].at[2 * chip[0] + chip[1], c], (*chip, c))
            for a in range(len(shards)) for chip in _other_chips(x, y)]


def _plan_pass_on(x, y, c, nothing, lands):
    def place(a, chip, half):
        return lands[a].at[2 * chip[0] + chip[1], half]

    return [(place(a, chip, c), place(a, chip, c), place(a, chip, 1 - c), (x, y, 1 - c))
            for a in range(len(lands)) for chip in _other_chips(x, y)]


def _plan_own_half_to_sibling(x, y, c, nothing, lands):
    return [(lands[a].at[c], lands[a].at[c], lands[a].at[1 - c], (x, y, 1 - c)) for a in range(len(lands))]


def _plan_other_half_to_sibling(x, y, c, grads, lands):
    return [(grads[a].at[1 - c], lands[a], lands[a], (x, y, 1 - c)) for a in range(len(grads))]


def _plan_to_other_chips(x, y, c, partials, lands):
    return [(partials[a].at[2 * chip[0] + chip[1]], lands[a].at[k], lands[a].at[k], (*chip, c))
            for a in range(len(partials)) for k, chip in enumerate(_other_chips(x, y))]


def _plan_to_all(x, y, c, blocks, lands):
    flips = [(fx, fy, fc) for fx in (0, 1) for fy in (0, 1) for fc in (0, 1) if (fx, fy, fc) != (0, 0, 0)]
    peers = [(1 - x if fx else x, 1 - y if fy else y, 1 - c if fc else c) for fx, fy, fc in flips]
    return [(blocks[0], lands[0].at[4 * x + 2 * y + c], lands[0].at[4 * p[0] + 2 * p[1] + p[2]], p) for p in peers]


def _planned_copies(plan, srcs, lands, send_sems, recv_sems):
    x, y, c = _place()

    def pair(k, src, there, here, to):
        make = lambda dst: pltpu.make_async_remote_copy(
            src_ref=src, dst_ref=dst, send_sem=send_sems.at[k], recv_sem=recv_sems.at[k], device_id=to, device_id_type=MESH)
        return make(there), make(here)

    return [pair(k, *entry) for k, entry in enumerate(plan(x, y, c, srcs, lands))]


_HBM_SPEC = pl.BlockSpec(memory_space=pltpu.HBM)
_SEM_SPEC = pl.BlockSpec(memory_space=pltpu.SEMAPHORE)


def _hbm(a):
    return pltpu.with_memory_space_constraint(a, pltpu.HBM)


HANDSHAKES = {
    "sibling": (1, lambda x, y, c: [(x, y, 1 - c)]),
}


def _exchange_start(name, plan, n_copies, srcs, land_shapes, after, lands=None, peers=None):
    if lands is None:
        lands = [lax.empty(s.shape, s.dtype) for s in land_shapes]
    land_shapes = lands
    ns, nl = len(srcs), len(land_shapes)
    n_in = ns + nl + 1
    collective_id, peers_of = HANDSHAKES[peers] if peers else (None, None)

    def body(*refs):
        if peers:
            who = peers_of(*_place())
            barrier = pltpu.get_barrier_semaphore()
            for peer in who:
                pl.semaphore_signal(barrier, inc=1, device_id=peer, device_id_type=MESH)
            pl.semaphore_wait(barrier, len(who))
        for send, _ in _planned_copies(plan, refs[:ns], refs[ns:ns + nl], refs[n_in], refs[n_in + 1]):
            send.start()
        refs[-1][...] = jnp.zeros_like(refs[-1])

    out = pl.pallas_call(
        body, name=name,
        out_shape=(pltpu.SemaphoreType.DMA((n_copies,)), pltpu.SemaphoreType.DMA((n_copies,)),
                   *[pltpu.HBM(s.shape, s.dtype) for s in land_shapes], jax.ShapeDtypeStruct((8, 128), F32)),
        in_specs=[_HBM_SPEC] * (ns + nl) + [pl.BlockSpec(memory_space=pl.ANY)],
        out_specs=(_SEM_SPEC, _SEM_SPEC, *[_HBM_SPEC] * nl, pl.BlockSpec(memory_space=pltpu.VMEM)),
        input_output_aliases={ns + i: 2 + i for i in range(nl)},
        compiler_params=pltpu.CompilerParams(has_side_effects=pltpu.SideEffectType.DATAFLOW_SIDE_EFFECTING,
                                             collective_id=collective_id),
    )(*[_hbm(s) for s in srcs], *[_hbm(l) for l in lands], after)
    return out[0], out[1], list(out[2:2 + nl]), out[-1]


def _exchange_wait(name, plan, srcs, started, after):
    send_sems, recv_sems, lands, _ = started
    ns, nl = len(srcs), len(lands)
    after = list(after) if isinstance(after, (list, tuple)) else [after]

    def body(*refs):
        for send, recv in _planned_copies(plan, refs[:ns], refs[ns:ns + nl], refs[ns + nl], refs[ns + nl + 1]):
            send.wait_send()
            recv.wait_recv()

    return pl.pallas_call(
        body, name=name, out_shape=[pltpu.HBM(l.shape, l.dtype) for l in lands],
        in_specs=[_HBM_SPEC] * (ns + nl) + [_SEM_SPEC, _SEM_SPEC] + [pl.BlockSpec(memory_space=pl.ANY)] * len(after),
        out_specs=[_HBM_SPEC] * nl, input_output_aliases={ns + i: i for i in range(nl)},
        compiler_params=pltpu.CompilerParams(has_side_effects=pltpu.SideEffectType.DATAFLOW_SIDE_EFFECTING),
    )(*[_hbm(s) for s in srcs], *lands, send_sems, recv_sems, *after)


def _like(arrays, lead, dtype=None):
    return [jax.ShapeDtypeStruct(tuple(lead) + a.shape[-2:], dtype or a.dtype) for a in arrays]


class _StepExchanges:
    def __init__(self, mats, conv_w):
        x, y, c = _place()
        self.place = jnp.stack([c, 2 * x + y]).astype(jnp.int32)
        shards = [w.astype(BF16).reshape(2, w.shape[0] // 2, w.shape[1]) for w in mats]
        self._in_shard = shards[:1]
        self._in = _exchange_start("w_in_allgather_start", _plan_first_hop, 3, self._in_shard,
                                   _like(self._in_shard, (N_CHIPS, 2)), shards[0])
        self.zero = self._in[3]
        taps = jnp.pad(conv_w, ((0, 8 - conv_w.shape[0]), (0, 128 - conv_w.shape[1])))
        self._rest_shards = shards[1:] + [jnp.stack([taps, jnp.zeros_like(taps)])]
        self._taps_shape = conv_w.shape
        self._groups = {}

    def w_in(self, after):
        landed = _exchange_wait("w_in_allgather_wait", _plan_first_hop, self._in_shard, self._in,
                                list(after) + self._rest_shards)
        (w_in,) = _allgather_finish("w_in_allgather_finish", self._in_shard, landed, [True])
        self._rest = _exchange_start("rest_allgather_start", _plan_first_hop, 3 * len(self._rest_shards),
                                     self._rest_shards, _like(self._rest_shards, (N_CHIPS, 2)), w_in)
        self.zero = self._rest[3]
        return w_in.reshape(N_CHIPS, 2 * w_in.shape[2], w_in.shape[3])

    def rest_weights(self, after):
        landed = _exchange_wait("rest_allgather_wait", _plan_first_hop, self._rest_shards, self._rest, after)
        kv, out, up, down, taps = _allgather_finish("rest_allgather_finish", self._rest_shards, landed,
                                                    [True, True, False, False, True])
        self._up_down = _exchange_start("up_down_pass_on_start", _plan_pass_on, 6, [], None, self.zero, lands=[up, down],
                                        peers="sibling")
        self.zero = self._up_down[3]
        k, w = self._taps_shape
        taps = taps[:, 0, :k, :w].transpose(1, 0, 2).reshape(k, N_CHIPS * w)
        return [g.reshape(N_CHIPS, 2 * g.shape[2], g.shape[3]) for g in (kv, out)], taps

    def up_down(self, after):
        full = _exchange_wait("up_down_pass_on_wait", _plan_pass_on, [], self._up_down, after)
        return [g.reshape(N_CHIPS, 2 * g.shape[2], g.shape[3]) for g in full]

    def send_grads(self, key, grads):
        grads = list(grads)
        started = _exchange_start(f"{key}_grads_to_sibling_start", _plan_other_half_to_sibling, len(grads), grads,
                                  _like(grads, (N_CHIPS,)), self.zero, peers="sibling")
        self._groups[key] = dict(grads=grads, to_sibling=started)
        self.zero = started[3]

    def grads_at_sibling(self, key, after):
        group = self._groups[key]
        grads = group["grads"]
        group["from_sibling"] = _exchange_wait(f"{key}_grads_to_sibling_wait", _plan_other_half_to_sibling, grads,
                                               group["to_sibling"], after)
        group["partials"] = _chip_sums_bf16(f"{key}_chip_sums", grads, group["from_sibling"], self.place)
        group["to_chips"] = _exchange_start(f"{key}_grads_to_chips_start", _plan_to_other_chips, 3 * len(grads),
                                            group["partials"], _like(group["partials"], (3,)), self.zero)
        self.zero = group["to_chips"][3]

    def grads_summed(self, key, after):
        group = self._groups[key]
        from_chips = _exchange_wait(f"{key}_grads_to_chips_wait", _plan_to_other_chips, group["partials"],
                                    group["to_chips"], after)
        return _final_sums(f"{key}_final_sums", group["grads"], group["from_sibling"], from_chips, self.place)

    def send_sums(self, key, sums):
        self._groups[key + "_sums"] = _exchange_start(f"{key}_sums_to_sibling_start", _plan_own_half_to_sibling,
                                                      len(sums), [], None, self.zero, lands=list(sums),
                                                      peers="sibling")
        self.zero = self._groups[key + "_sums"][3]

    def whole_sums(self, key, after):
        full = _exchange_wait(f"{key}_sums_to_sibling_wait", _plan_own_half_to_sibling, [], self._groups[key + "_sums"], after)
        return [t.reshape(2 * t.shape[1], t.shape[2]) for t in full]

    def send_small(self, block):
        self._small = block
        self._small_started = _exchange_start("small_grads_start", _plan_to_all, 7, [block],
                                              [jax.ShapeDtypeStruct((8,) + block.shape, block.dtype)], self.zero)
        self.zero = self._small_started[3]

    def small_summed(self, after):
        x, y, c = _place()
        (landed,) = _exchange_wait("small_grads_wait", _plan_to_all, [self._small], self._small_started, after)
        blocks = lax.dynamic_update_index_in_dim(landed, self._small, 4 * x + 2 * y + c, 0)
        return _sum_blocks("small_sum", blocks)


def _rope_tables(positions):
    half = HEAD // 2
    inv_freq = jnp.float32(ROPE_THETA) ** (-(jnp.arange(half, dtype=F32) * 2.0 / HEAD))
    ang = positions.astype(F32)[:, None] * inv_freq
    cos, sin = jnp.cos(ang), jnp.sin(ang)
    return jnp.tile(cos, (1, 4)), jnp.tile(jnp.concatenate([-sin, sin], axis=1), (1, 2))


def _local_step(x, mem, positions, target, gains, ex):
    g_pre_mix, g_mem, g_a, g_c, g_x, g_post_mix, g_pre_mlp, g_post_mlp = gains
    tm = ROW_TILE
    cos, sin = _rope_tables(positions)
    h = _pre_norm(x, g_pre_mix, ex.zero, tm)
    w_in = ex.w_in([h, cos, sin])

    q, k, v, bcu, qx = _in_proj_fwd(h, w_in, cos, sin, ex.zero, tm)
    ya, lse = _attn_fwd(q, k, v)
    (w_kv, w_out), conv_w = ex.rest_weights(lse)
    w_kv, w_out = (w.reshape(N_CHIPS * w.shape[1], w.shape[2]) for w in (w_kv, w_out))
    memn, mkv = _memkv_fwd(mem, g_mem, w_kv, ex.zero)
    yx, ycat, y2, x1 = _mix_fwd(ya, bcu, qx, mkv, conv_w, g_a, g_c, g_x, w_out, g_post_mix, x, tm)
    w_up, w_down = ex.up_down(x1)
    w_down = w_down.reshape(N_CHIPS * w_down.shape[1], w_down.shape[2])
    h2, f, du, df2, dx1, dg_pre_mlp, dg_post_mlp, loss = _mlp_fwd_bwd(x1, target, g_pre_mlp, g_post_mlp, w_up, w_down,
                                                                      MLP_ROW_TILE)
    gw_down = _weight_grad("grad_w_down", f, df2, True, ex.zero)
    gw_up = _weight_grad("grad_w_up", h2, du, False, ex.zero)
    ex.send_grads("early", [gw_up, gw_down])

    gw_out, dya, delta, tail, dmkv, g_conv, dg_post_mix, dg_a, dg_c, dg_x = _mixer_bwd(
        dx1, y2, ycat, ya, yx, bcu, qx, mkv, conv_w, g_a, g_c, g_x, w_out, g_post_mix, ex.zero, tm)
    ex.grads_at_sibling("early", dya)
    gw_kv, dg_mem = _memkv_bwd(mem, g_mem, w_kv, dmkv)
    ex.send_grads("mid", [gw_out, gw_kv])
    dqkv = _attn_bwd(q, k, v, dya, lse, delta, ex.zero)
    ex.grads_at_sibling("mid", dqkv[0])
    grad_x, gw_in, dg_pre_mix = _in_proj_bwd(dqkv, tail, cos, sin, w_in, x, h, g_pre_mix, dx1, ex.zero, tm)
    gain_grads = [dg_pre_mix, dg_mem, dg_a, dg_c, dg_x, dg_post_mix, dg_pre_mlp, dg_post_mlp]
    ex.send_small(_pack_small(gain_grads, g_conv, loss))
    ex.send_grads("late", [gw_in])
    return grad_x


def _pack_small(gains, conv, scalar=None):
    rows = [jnp.pad(g, ((0, 0), (0, D_MODEL - g.shape[1]))) for g in gains]
    rows.append(jnp.pad(conv, ((0, 0), (0, D_MODEL - conv.shape[1]))))
    last = jnp.zeros((SMALL_ROWS - 8 - conv.shape[0], D_MODEL), F32)
    rows.append(last if scalar is None else last.at[0:1, 0:1].set(scalar))
    return jnp.concatenate(rows, axis=0)


def kernel(x, mem, positions, g_pre_mix, g_mem, w_in, w_mem_kv, conv_w, g_attn_out, g_conv_out, g_xattn_out, w_out, g_post_mix, g_pre_mlp, w_up, w_down, g_post_mlp, loss_target, m_g_pre_mix, m_g_mem, m_w_in, m_w_mem_kv, m_conv_w, m_g_attn_out, m_g_conv_out, m_g_xattn_out, m_w_out, m_g_post_mix, m_g_pre_mlp, m_w_up, m_w_down, m_g_post_mlp, v_g_pre_mix, v_g_mem, v_w_in, v_w_mem_kv, v_conv_w, v_g_attn_out, v_g_conv_out, v_g_xattn_out, v_w_out, v_g_post_mix, v_g_pre_mlp, v_w_up, v_w_down, v_g_post_mlp):
    chip = 2 * lax.axis_index("x") + lax.axis_index("y")
    gains = [g_pre_mix, g_mem, g_attn_out, g_conv_out, g_xattn_out, g_post_mix, g_pre_mlp, g_post_mlp]
    gains_m = [m_g_pre_mix, m_g_mem, m_g_attn_out, m_g_conv_out, m_g_xattn_out, m_g_post_mix, m_g_pre_mlp, m_g_post_mlp]
    gains_v = [v_g_pre_mix, v_g_mem, v_g_attn_out, v_g_conv_out, v_g_xattn_out, v_g_post_mix, v_g_pre_mlp, v_g_post_mlp]
    mats =[w_in[0], w_mem_kv[0], w_out[0], w_up[0], w_down[0]]
    mats_m = [m_w_in[0], m_w_mem_kv[0], m_w_out[0], m_w_up[0], m_w_down[0]]
    mats_v = [v_w_in[0], v_w_mem_kv[0], v_w_out[0], v_w_up[0], v_w_down[0]]

    ex = _StepExchanges(mats, conv_w[0])
    grad_x = _local_step(x[0], mem[0], positions[0], loss_target[0], gains, ex)

    ex.send_sums("four", ex.grads_summed("early", ex.zero) + ex.grads_summed("mid", ex.zero))
    ex.grads_at_sibling("late", ex.zero)
    up_sum, down_sum, out_sum, kv_sum = ex.whole_sums("four", ex.zero)
    params = lambda a, g: (mats[a], g, mats_m[a], mats_v[a])
    new_up, new_down = _adamw("adamw_up_down", [params(3, up_sum), params(4, down_sum)], ex.zero)
    new_out, new_kv = _adamw("adamw_out_kv", [params(2, out_sum), params(1, kv_sum)], ex.zero)

    small, total = _small_update(ex.small_summed(new_kv[1]), chip.reshape(1).astype(jnp.int32), gains, gains_m,
                                 gains_v, conv_w[0], m_conv_w[0], v_conv_w[0])

    ex.send_sums("last", ex.grads_summed("late", small[0][1]))
    (in_sum,) = ex.whole_sums("last", ex.zero)
    (new_in,) = _adamw("adamw_in", [params(0, in_sum)], in_sum)
    mat_new = [new_in, new_kv, new_out, new_up, new_down]

    order = ["g_pre_mix", "g_mem", "w_in", "w_mem_kv", "conv_w", "g_attn_out", "g_conv_out", "g_xattn_out", "w_out",
             "g_post_mix", "g_pre_mlp", "w_up", "w_down", "g_post_mlp"]
    gain_names = ["g_pre_mix", "g_mem", "g_attn_out", "g_conv_out", "g_xattn_out", "g_post_mix", "g_pre_mlp", "g_post_mlp"]
    mat_names = ["w_in", "w_mem_kv", "w_out", "w_up", "w_down"]

    def leaf(kind, name):
        if name in gain_names:
            return small[gain_names.index(name)][kind]
        if name == "conv_w":
            return small[len(gain_names)][kind][None]
        return mat_new[mat_names.index(name)][kind][None]

    return (total[0, 0], grad_x[None], *[leaf(kind, name) for kind in range(4) for name in order])
```

```python
import jax
import jax.numpy as jnp
from jax import lax
from jax.experimental import pallas as pl
from jax.experimental.pallas import tpu as pltpu

F32, BF16 = jnp.float32, jnp.bfloat16

D_MODEL = 1024
ATTN_W = 512
CONV_W = 256
XATTN_W = 256
PROJ_W = 3 * ATTN_W + 3 * CONV_W + XATTN_W
D_FF = 4096
HEAD = 64
N_BACK = 128
DILATIONS = (1, 4, 16)
PATTERN_ORDER = DILATIONS[::-1]
ROPE_THETA = 10000.0
EPS = 1e-6
NEG_INF = -1e30
SCALE = HEAD ** -0.5
N_CHIPS = 4
SHARD_IN = PROJ_W // N_CHIPS
SHARD_FF = D_FF // N_CHIPS

ADAM_LR, ADAM_B1, ADAM_B2, ADAM_EPS, ADAM_WD, ADAM_STEP = 0.001, 0.9, 0.999, 1e-08, 0.01, 10

VMEM_LIMIT_V7X = 56 * 1024 * 1024
ROW_TILE = 512
MLP_ROW_TILE = 256
ADAMW_ROW_TILE = 256
SMALL_ROWS = 16
WEIGHT_GRAD_CHUNKS = 4

NT = (((1,), (1,)), ((), ()))
TN = (((0,), (0,)), ((), ()))
MESH = pl.DeviceIdType.MESH


def _params(*sem):
    return pltpu.CompilerParams(dimension_semantics=sem, vmem_limit_bytes=VMEM_LIMIT_V7X)


def _resident(shape):
    return pl.BlockSpec(shape, lambda *_: (0,) * len(shape), pipeline_mode=pl.Buffered(1))


def _rows(tm, width):
    return pl.BlockSpec((tm, width), lambda i: (i, 0))


def _rms_hat(x):
    r = lax.rsqrt(jnp.mean(x * x, axis=-1, keepdims=True) + EPS)
    return x * r, r


def _rms_bwd(xhat, r, g, dy):
    gdy = dy * g
    return r * (gdy - xhat * jnp.mean(xhat * gdy, axis=-1, keepdims=True))


def _rope128(t, cos, sin_signed, inverse):
    lane = lax.broadcasted_iota(jnp.int32, t.shape, 1)
    first_half = (lane % HEAD) < (HEAD // 2)
    rot = jnp.where(first_half, pltpu.roll(t, 128 - HEAD // 2, 1), pltpu.roll(t, HEAD // 2, 1))
    return t * cos - rot * sin_signed if inverse else t * cos + rot * sin_signed


def _pre_norm(x, g, after, tm):
    S = x.shape[0]

    def body(x_ref, g_ref, after_ref, h_ref):
        h_ref[...] = (_rms_hat(x_ref[...])[0] * g_ref[...]).astype(BF16)

    return pl.pallas_call(
        body, name="pre_norm", grid=(S // tm,),
        in_specs=[_rows(tm, D_MODEL), _resident((1, D_MODEL)), pl.BlockSpec(memory_space=pl.ANY)],
        out_specs=_rows(tm, D_MODEL), out_shape=jax.ShapeDtypeStruct((S, D_MODEL), BF16),
        compiler_params=_params("parallel"),
    )(x, g, after)


def _side_by_side(w_hbm, w_full, sems):
    width = w_hbm.shape[2]

    @pl.when(pl.program_id(0) == 0)
    def _():
        copies = [pltpu.make_async_copy(w_hbm.at[j], w_full.at[:, pl.ds(width * j, width)], sems.at[j])
                  for j in range(N_CHIPS)]
        for cp in copies:
            cp.start()
        for cp in copies:
            cp.wait()


def _in_proj_fwd(h, w_in, cos, sin, after, tm):
    S = h.shape[0]

    def body(h_ref, w_hbm, cos_ref, sin_ref, after_ref, q_ref, k_ref, v_ref, bcu_ref, qx_ref, proj, w_full, sems):
        _side_by_side(w_hbm, w_full, sems)
        proj[...] = jnp.dot(h_ref[...], w_full[...], preferred_element_type=F32)
        c, s = cos_ref[...], sin_ref[...]
        for j in range(ATTN_W // 128):
            lo = 128 * j
            q_ref[:, lo:lo + 128] = _rope128(proj[:, lo:lo + 128], c, s, False) * SCALE
            k_ref[:, lo:lo + 128] = _rope128(proj[:, ATTN_W + lo:ATTN_W + lo + 128], c, s, False)
        v_ref[...] = proj[:, 2 * ATTN_W:3 * ATTN_W]
        bcu_ref[...] = proj[:, 3 * ATTN_W:3 * ATTN_W + 3 * CONV_W]
        qx_ref[...] = proj[:, 3 * ATTN_W + 3 * CONV_W:PROJ_W].astype(BF16)

    return pl.pallas_call(
        body, name="in_proj_fwd", grid=(S // tm,),
        in_specs=[_rows(tm, D_MODEL), pl.BlockSpec(memory_space=pl.ANY), _rows(tm, 128), _rows(tm, 128),
                  pl.BlockSpec(memory_space=pl.ANY)],
        out_specs=[_rows(tm, ATTN_W), _rows(tm, ATTN_W), _rows(tm, ATTN_W), _rows(tm, 3 * CONV_W), _rows(tm, XATTN_W)],
        out_shape=[jax.ShapeDtypeStruct((S, ATTN_W), F32), jax.ShapeDtypeStruct((S, ATTN_W), F32),
                   jax.ShapeDtypeStruct((S, ATTN_W), F32), jax.ShapeDtypeStruct((S, 3 * CONV_W), F32),
                   jax.ShapeDtypeStruct((S, XATTN_W), BF16)],
        scratch_shapes=[pltpu.VMEM((tm, PROJ_W), F32), pltpu.VMEM((D_MODEL, PROJ_W), BF16),
                        pltpu.SemaphoreType.DMA((N_CHIPS,))],
        compiler_params=_params("arbitrary"),
    )(h, w_in, cos, sin, after)


def _memkv_fwd(mem, g_mem, w_kv, after):
    n_mem = mem.shape[0]

    def body(mem_ref, g_ref, w_ref, after_ref, mn_ref, kv_ref):
        mhat, _ = _rms_hat(mem_ref[...])
        mn = (mhat * g_ref[...]).astype(BF16)
        mn_ref[...] = mn
        kv_ref[...] = jnp.dot(mn, w_ref[...], preferred_element_type=F32).astype(BF16)

    vmem = pl.BlockSpec(memory_space=pltpu.VMEM)
    return pl.pallas_call(
        body, name="memkv_fwd", in_specs=[vmem, vmem, vmem, pl.BlockSpec(memory_space=pl.ANY)], out_specs=[vmem, vmem],
        out_shape=[jax.ShapeDtypeStruct((n_mem, D_MODEL), BF16), jax.ShapeDtypeStruct((n_mem, 2 * XATTN_W), BF16)],
        compiler_params=pltpu.CompilerParams(vmem_limit_bytes=VMEM_LIMIT_V7X),
    )(mem, g_mem, w_kv, after)


def _fill_band_bias(bias):
    row = lax.broadcasted_iota(jnp.int32, (N_BACK, 2 * N_BACK), 0)
    col = lax.broadcasted_iota(jnp.int32, (N_BACK, 2 * N_BACK), 1)
    band = (col >= row) & (col <= row + N_BACK)
    bias[1] = jnp.where(band, 0.0, NEG_INF)
    bias[0] = jnp.where(band & (col >= N_BACK), 0.0, NEG_INF)


def _strided(start, size, d):
    return pl.ds(start, size) if d == 1 else pl.ds(start, size, stride=d)


def _group_starts(g, G, nb, d):
    t0 = g * G
    r, n0 = lax.shift_right_logical(t0, nb.bit_length() - 1), lax.bitwise_and(t0, nb - 1)
    first = r + n0 * (N_BACK * d)
    before = r + jnp.maximum(n0 - 1, 0) * (N_BACK * d)
    starts = [before] + [first + u * (N_BACK * d) for u in range(G)]
    if d == 1:
        starts = [pl.multiple_of(st, N_BACK) for st in starts]
    return starts, n0


def _step_blocks(i, U, nb, d):
    G = min(U, nb)
    whole = G == nb
    row_blocks, blocks = [], []
    for grp in range(U // G):
        starts, n0 = _group_starts(i * (U // G) + grp, G, nb, d)
        base = len(row_blocks)
        if whole:
            row_blocks += [_strided(st, N_BACK, d) for st in starts[1:]]
            blocks += [(base + max(u - 1, 0), base + u, min(u, 1)) for u in range(G)]
        else:
            row_blocks += [_strided(st, N_BACK, d) for st in starts]
            blocks += [(base + u, base + u + 1, jnp.minimum(n0, 1) if u == 0 else 1) for u in range(G)]
    return row_blocks, blocks


def _by_head(a, b):
    lane = lax.broadcasted_iota(jnp.int32, (a.shape[0], 2 * HEAD), 1)
    return jnp.where(lane < HEAD, a, b)


def _head_only(t, hh):
    lane = lax.broadcasted_iota(jnp.int32, t.shape, 1)
    return jnp.where((lane < HEAD) == (hh == 0), t, jnp.zeros_like(t))


def _stack_heads(t):
    return jnp.concatenate([_head_only(t, 0), _head_only(t, 1)], axis=0)


def _head_columns(t):
    return jnp.concatenate([t[:, 0:1], t[:, HEAD:HEAD + 1]], axis=0)


def _unstack(t):
    return _by_head(t[:N_BACK], t[N_BACK:])


def _unstack_columns(t):
    return _by_head(jnp.broadcast_to(t[:N_BACK], (N_BACK, 2 * HEAD)), jnp.broadcast_to(t[N_BACK:], (N_BACK, 2 * HEAD)))


FWD_BLOCKS_PER_STEP = 4
BWD_BLOCKS_PER_STEP = 4
BWD_CHUNK = 64


def _attn_fwd(q, k, v):
    S = q.shape[0]
    U = FWD_BLOCKS_PER_STEP

    def body(q_ref, k_ref, v_ref, y_ref, m_ref, l_scr, bias):
        _fill_band_bias(bias)
        for g, d in enumerate(PATTERN_ORDER):
            nb = S // d // N_BACK
            first_pattern, last_pattern = g == 0, g == len(PATTERN_ORDER) - 1

            def step(i, carry, d=d, nb=nb, first_pattern=first_pattern, last_pattern=last_pattern):
                row_blocks, blocks = _step_blocks(i, U, nb, d)
                kb = [k_ref[r, :].astype(BF16) for r in row_blocks]
                ss = []
                for before, own, which in blocks:
                    kw = jnp.concatenate([kb[before], kb[own]], 0)
                    qs = _stack_heads(q_ref[row_blocks[own], :].astype(BF16))
                    b = bias[which]
                    ss.append(lax.dot_general(qs, kw, NT, preferred_element_type=F32) + jnp.concatenate([b, b], axis=0))
                ms = [jnp.max(s, axis=1, keepdims=True) for s in ss]
                ps = [jnp.exp(s - m) for s, m in zip(ss, ms)]
                ls = [jnp.sum(p, axis=1, keepdims=True) for p in ps]
                vb = [v_ref[r, :].astype(BF16) for r in row_blocks]
                os_ = [jnp.dot(ps[u].astype(BF16), jnp.concatenate([vb[before], vb[own]], 0), preferred_element_type=F32)
                       for u, (before, own, _) in enumerate(blocks)]
                for u, (_, own, _) in enumerate(blocks):
                    o_g, m_g, l_g = _unstack(os_[u]), _unstack_columns(ms[u]), _unstack_columns(ls[u])
                    r = row_blocks[own]
                    if first_pattern:
                        m_new, l_new, acc = m_g, l_g, o_g
                    else:
                        m_old = m_ref[r, :]
                        m_new = jnp.maximum(m_old, m_g)
                        alpha, beta = jnp.exp(m_old - m_new), jnp.exp(m_g - m_new)
                        l_new = l_scr[r, :] * alpha + l_g * beta
                        acc = y_ref[r, :] * alpha + o_g * beta
                    if last_pattern:
                        y_ref[r, :] = acc / l_new
                        m_ref[r, :] = m_new + jnp.log(l_new)
                    else:
                        y_ref[r, :] = acc
                        m_ref[r, :] = m_new
                        l_scr[r, :] = l_new
                return carry

            lax.fori_loop(0, d * nb // U, step, 0)

    col = pl.BlockSpec((S, 2 * HEAD), lambda j: (0, j))
    return pl.pallas_call(
        body, name="attn_fwd", grid=(q.shape[1] // (2 * HEAD),),
        in_specs=[col, col, col], out_specs=[col, col],
        out_shape=[jax.ShapeDtypeStruct(q.shape, F32)] * 2,
        scratch_shapes=[pltpu.VMEM((S, 2 * HEAD), F32), pltpu.VMEM((2, N_BACK, 2 * N_BACK), F32)],
        compiler_params=_params("parallel"),
    )(q, k, v)


def _attn_bwd(q, k, v, dy, lse, delta, after):
    S = q.shape[0]
    U = BWD_BLOCKS_PER_STEP

    def body(q_ref, k_ref, v_ref, dy_ref, lse_ref, delta_ref, after_ref, dq_ref, dk_ref, dv_ref, bias):
        _fill_band_bias(bias)
        nb_first = S // PATTERN_ORDER[0] // N_BACK
        first_writes_all = min(U, nb_first) == nb_first
        if not first_writes_all:
            dk_ref[...] = jnp.zeros_like(dk_ref)
            dv_ref[...] = jnp.zeros_like(dv_ref)
        for g, d in enumerate(PATTERN_ORDER):
            nb = S // d // N_BACK

            def step(i, carry, d=d, nb=nb, g=g):
                row_blocks, blocks = _step_blocks(i, U, nb, d)
                kb = [k_ref[r, :].astype(BF16) for r in row_blocks]
                vb = [v_ref[r, :].astype(BF16) for r in row_blocks]
                kws = [jnp.concatenate([kb[before], kb[own]], 0) for before, own, _ in blocks]
                vws = [jnp.concatenate([vb[before], vb[own]], 0) for before, own, _ in blocks]
                qss = [_stack_heads(q_ref[row_blocks[own], :].astype(BF16)) for _, own, _ in blocks]
                doss = [_stack_heads(dy_ref[row_blocks[own], :].astype(BF16)) for _, own, _ in blocks]
                ss = [lax.dot_general(qss[u], kws[u], NT, preferred_element_type=F32) for u in range(U)]
                dps = [lax.dot_general(doss[u], vws[u], NT, preferred_element_type=F32) for u in range(U)]
                pbs, dss = [], []
                for u, (_, own, which) in enumerate(blocks):
                    lse_c = _head_columns(lse_ref[row_blocks[own], :])
                    delta_c = _head_columns(delta_ref[row_blocks[own], :])
                    p_parts, ds_parts = [], []
                    for r0 in range(0, 2 * N_BACK, BWD_CHUNK):
                        r = slice(r0, r0 + BWD_CHUNK)
                        mask = bias[which, r0 % N_BACK:r0 % N_BACK + BWD_CHUNK, :]
                        p_r = jnp.exp(ss[u][r] + mask - lse_c[r])
                        p_parts.append(p_r.astype(BF16))
                        ds_parts.append((p_r * (dps[u][r] - delta_c[r])).astype(BF16))
                    pbs.append(jnp.concatenate(p_parts, axis=0))
                    dss.append(jnp.concatenate(ds_parts, axis=0))
                dqs = [jnp.dot(dss[u], kws[u], preferred_element_type=F32) for u in range(U)]
                dkws = [lax.dot_general(dss[u], qss[u], TN, preferred_element_type=F32) for u in range(U)]
                dvws = [lax.dot_general(pbs[u], doss[u], TN, preferred_element_type=F32) for u in range(U)]
                dk_parts, dv_parts = [None] * len(row_blocks), [None] * len(row_blocks)
                for u, (before, own, _) in enumerate(blocks):
                    dq = _unstack(dqs[u])
                    if g == 0:
                        dq_ref[row_blocks[own], :] = dq
                    else:
                        dq_ref[row_blocks[own], :] += dq
                    for idx, dkp, dvp in ((before, dkws[u][:N_BACK], dvws[u][:N_BACK]),
                                          (own, dkws[u][N_BACK:], dvws[u][N_BACK:])):
                        dk_parts[idx] = dkp if dk_parts[idx] is None else dk_parts[idx] + dkp
                        dv_parts[idx] = dvp if dv_parts[idx] is None else dv_parts[idx] + dvp
                for idx, r in enumerate(row_blocks):
                    if g == 0 and first_writes_all:
                        dk_ref[r, :] = dk_parts[idx]
                        dv_ref[r, :] = dv_parts[idx]
                    else:
                        dk_ref[r, :] += dk_parts[idx]
                        dv_ref[r, :] += dv_parts[idx]
                return carry

            lax.fori_loop(0, d * nb // U, step, 0)

    col = pl.BlockSpec((S, 2 * HEAD), lambda j: (0, j))
    return pl.pallas_call(
        body, name="attn_bwd", grid=(q.shape[1] // (2 * HEAD),),
        in_specs=[col] * 6 + [pl.BlockSpec(memory_space=pl.ANY)], out_specs=[col] * 3,
        out_shape=[jax.ShapeDtypeStruct(q.shape, F32)] * 3,
        scratch_shapes=[pltpu.VMEM((2, N_BACK, 2 * N_BACK), F32)],
        compiler_params=_params("parallel"),
    )(q, k, v, dy, lse, delta, after)


def _shift_down(z, before, k):
    row = lax.broadcasted_iota(jnp.int32, z.shape, 0)
    out = pltpu.roll(z, k, 0)
    for i in range(k):
        out = jnp.where(row == i, before[8 - k + i:8 - k + i + 1, :], out)
    return out


def _shift_up(z, after, k):
    rows = z.shape[0]
    row = lax.broadcasted_iota(jnp.int32, z.shape, 0)
    out = pltpu.roll(z, rows - k, 0)
    for i in range(k):
        out = jnp.where(row == rows - k + i, after[i:i + 1, :], out)
    return out


def _conv_fwd(bcu, before, is_first, w):
    b, c, u = bcu[:, 0:CONV_W], bcu[:, CONV_W:2 * CONV_W], bcu[:, 2 * CONV_W:3 * CONV_W]
    z = c * u
    zb = jnp.where(is_first, 0.0, before[:, CONV_W:2 * CONV_W] * before[:, 2 * CONV_W:3 * CONV_W])
    z1, z2 = _shift_down(z, zb, 1), _shift_down(z, zb, 2)
    cv = w[0:1, :] * z2 + w[1:2, :] * z1 + w[2:3, :] * z
    return b, c, u, z, z1, z2, cv


def _halo_before(tm, width):
    return pl.BlockSpec((8, width), lambda i: (jnp.maximum(i * (tm // 8) - 1, 0), 0))


def _mix_fwd(ya, bcu, qx, mkv, conv_w, g_a, g_c, g_x, w_out, g_post, x, tm):
    S = x.shape[0]

    def body(ya_ref, bcu_ref, before_ref, qx_ref, mkv_ref, cw_ref, ga_ref, gc_ref, gx_ref,
             wo_ref, gp_ref, x_ref, yx_ref, ycat_ref, y2_ref, x1_ref):
        ya = ya_ref[...]
        b, _, _, _, _, _, cv = _conv_fwd(bcu_ref[...], before_ref[...], pl.program_id(0) == 0, cw_ref[...])
        yc = b * cv

        qxb, mkvb = qx_ref[...], mkv_ref[...]
        heads = [slice(HEAD * hd, HEAD * (hd + 1)) for hd in range(XATTN_W // HEAD)]
        ss = [lax.dot_general(qxb[:, sl], mkvb[:, sl], NT, preferred_element_type=F32) * SCALE for sl in heads]
        ms = [jnp.max(s, axis=1, keepdims=True) for s in ss]
        ps = [jnp.exp(s - m) for s, m in zip(ss, ms)]
        ls = [jnp.sum(p, axis=1, keepdims=True) for p in ps]
        os_ = [jnp.dot(p.astype(BF16), mkvb[:, XATTN_W + sl.start:XATTN_W + sl.stop], preferred_element_type=F32)
               for p, sl in zip(ps, heads)]
        for sl, o, l in zip(heads, os_, ls):
            yx_ref[:, sl] = o / l
        yx = yx_ref[...]

        ycat_ref[:, 0:ATTN_W] = (_rms_hat(ya)[0] * ga_ref[...]).astype(BF16)
        ycat_ref[:, ATTN_W:ATTN_W + CONV_W] = (_rms_hat(yc)[0] * gc_ref[...]).astype(BF16)
        ycat_ref[:, ATTN_W + CONV_W:D_MODEL] = (_rms_hat(yx)[0] * gx_ref[...]).astype(BF16)
        y2 = jnp.dot(ycat_ref[...], wo_ref[...], preferred_element_type=F32)
        y2_ref[...] = y2
        x1_ref[...] = x_ref[...] + _rms_hat(y2)[0] * gp_ref[...]

    n_mem = mkv.shape[0]
    return pl.pallas_call(
        body, name="mix_fwd", grid=(S // tm,),
        in_specs=[_rows(tm, ATTN_W), _rows(tm, 3 * CONV_W), _halo_before(tm, 3 * CONV_W), _rows(tm, XATTN_W),
                  _resident((n_mem, 2 * XATTN_W)), _resident((3, CONV_W)), _resident((1, ATTN_W)),
                  _resident((1, CONV_W)), _resident((1, XATTN_W)), _resident((D_MODEL, D_MODEL)),
                  _resident((1, D_MODEL)), _rows(tm, D_MODEL)],
        out_specs=[_rows(tm, XATTN_W), _rows(tm, D_MODEL), _rows(tm, D_MODEL), _rows(tm, D_MODEL)],
        out_shape=[jax.ShapeDtypeStruct((S, XATTN_W), F32), jax.ShapeDtypeStruct((S, D_MODEL), BF16),
                   jax.ShapeDtypeStruct((S, D_MODEL), F32), jax.ShapeDtypeStruct((S, D_MODEL), F32)],
        compiler_params=_params("parallel"),
    )(ya, bcu, bcu, qx, mkv, conv_w, g_a, g_c, g_x, w_out, g_post, x)


def _mlp_fwd_bwd(x1, target, g_pre, g_post, w_up, w_down, tm):
    S = x1.shape[0]
    n_ff = D_FF // SHARD_FF

    def body(x1_ref, t_ref, gpre_ref, gpost_ref, wup_ref, wdn_ref,
             h2_ref, f_ref, du_ref, df2_ref, dx1_ref, dgpre_ref, dgpost_ref, loss_ref, u_scr):
        @pl.when(pl.program_id(0) == 0)
        def _():
            dgpre_ref[...] = jnp.zeros_like(dgpre_ref)
            dgpost_ref[...] = jnp.zeros_like(dgpost_ref)
            loss_ref[...] = jnp.zeros_like(loss_ref)

        x1 = x1_ref[...]
        x1hat, r1 = _rms_hat(x1)
        h2 = (x1hat * gpre_ref[...]).astype(BF16)
        h2_ref[...] = h2
        f2 = jnp.zeros((tm, D_MODEL), F32)
        for j in range(n_ff):
            cols = slice(SHARD_FF * j, SHARD_FF * (j + 1))
            u = jnp.maximum(jnp.dot(h2, wup_ref[j], preferred_element_type=F32), 0.0)
            u_scr[:, cols] = u
            f = (u * u).astype(BF16)
            f_ref[:, cols] = f
            f2 = f2 + jnp.dot(f, wdn_ref[cols, :], preferred_element_type=F32)
        f2hat, r2 = _rms_hat(f2)
        err = x1 + f2hat * gpost_ref[...] - t_ref[...]
        loss_ref[...] += 0.5 * jnp.sum(jnp.mean(err * err, axis=-1, keepdims=True), axis=0, keepdims=True)
        dx2 = err * (1.0 / D_MODEL)
        dgpost_ref[...] += jnp.sum(dx2 * f2hat, axis=0, keepdims=True)
        df2 = _rms_bwd(f2hat, r2, gpost_ref[...], dx2).astype(BF16)
        df2_ref[...] = df2
        dh2 = jnp.zeros((tm, D_MODEL), F32)
        for j in range(n_ff):
            cols = slice(SHARD_FF * j, SHARD_FF * (j + 1))
            df = lax.dot_general(df2, wdn_ref[cols, :], NT, preferred_element_type=F32)
            du = (2.0 * u_scr[:, cols] * df).astype(BF16)
            du_ref[:, cols] = du
            dh2 = dh2 + lax.dot_general(du, wup_ref[j], NT, preferred_element_type=F32)
        dgpre_ref[...] += jnp.sum(dh2 * x1hat, axis=0, keepdims=True)
        dx1_ref[...] = dx2 + _rms_bwd(x1hat, r1, gpre_ref[...], dh2)

    acc = pl.BlockSpec((1, D_MODEL), lambda i: (0, 0))
    return pl.pallas_call(
        body, name="mlp_fwd_bwd", grid=(S // tm,),
        in_specs=[_rows(tm, D_MODEL), _rows(tm, D_MODEL), _resident((1, D_MODEL)), _resident((1, D_MODEL)),
                  _resident((n_ff, D_MODEL, SHARD_FF)), _resident((D_FF, D_MODEL))],
        out_specs=[_rows(tm, D_MODEL), _rows(tm, D_FF), _rows(tm, D_FF), _rows(tm, D_MODEL), _rows(tm, D_MODEL),
                   acc, acc, pl.BlockSpec((1, 1), lambda i: (0, 0))],
        out_shape=[jax.ShapeDtypeStruct((S, D_MODEL), BF16), jax.ShapeDtypeStruct((S, D_FF), BF16),
                   jax.ShapeDtypeStruct((S, D_FF), BF16), jax.ShapeDtypeStruct((S, D_MODEL), BF16),
                   jax.ShapeDtypeStruct((S, D_MODEL), F32), jax.ShapeDtypeStruct((1, D_MODEL), F32),
                   jax.ShapeDtypeStruct((1, D_MODEL), F32), jax.ShapeDtypeStruct((1, 1), F32)],
        scratch_shapes=[pltpu.VMEM((tm, D_FF), F32)],
        compiler_params=_params("arbitrary"),
    )(x1, target, g_pre, g_post, w_up, w_down)


def _weight_grad(name, a, b, rows_sharded, after):
    S, K = a.shape
    N = b.shape[1]
    tk, tn = (K // N_CHIPS, N) if rows_sharded else (K, N // N_CHIPS)
    slab_w, whole_w = (tk, tn) if rows_sharded else (tn, tk)
    half = tk // 2
    tc = S // WEIGHT_GRAD_CHUNKS

    def body(a_hbm, b_hbm, after_ref, o_ref, whole, slabs, whole_sems, slab_sems):
        j = pl.program_id(0)
        sharded_hbm, whole_hbm = (a_hbm, b_hbm) if rows_sharded else (b_hbm, a_hbm)

        def slab_copy(shard, chunk):
            rows = pl.ds(chunk * tc, tc)
            return pltpu.make_async_copy(sharded_hbm.at[rows, pl.ds(shard * slab_w, slab_w)],
                                         slabs.at[shard % 2, rows], slab_sems.at[shard % 2, chunk])

        def whole_copy(chunk):
            rows = pl.ds(chunk * tc, tc)
            return pltpu.make_async_copy(whole_hbm.at[rows], whole.at[rows], whole_sems.at[chunk])

        def contract(rows):
            slab, other = slabs[j % 2, rows], whole[rows]
            lhs, rhs = (slab, other) if rows_sharded else (other, slab)
            return lax.dot_general(lhs, rhs, TN, preferred_element_type=F32)

        def fetch_next_slab():
            for chunk in range(WEIGHT_GRAD_CHUNKS):
                slab_copy(j + 1, chunk).start()

        def write(res):
            o_ref[0, 0] = res[:half]
            o_ref[1, 0] = res[half:]

        @pl.when(j == 0)
        def _():
            for chunk in range(WEIGHT_GRAD_CHUNKS):
                slab_copy(0, chunk).start()
                whole_copy(chunk).start()
            res = None
            for chunk in range(WEIGHT_GRAD_CHUNKS):
                slab_copy(0, chunk).wait()
                whole_copy(chunk).wait()
                if chunk == WEIGHT_GRAD_CHUNKS - 1:
                    fetch_next_slab()
                part = contract(pl.ds(chunk * tc, tc))
                res = part if res is None else res + part
            write(res)

        @pl.when(j > 0)
        def _():
            for chunk in range(WEIGHT_GRAD_CHUNKS):
                slab_copy(j, chunk).wait()
            pl.when(j + 1 < N_CHIPS)(fetch_next_slab)
            write(contract(slice(None)))

    any_spec = pl.BlockSpec(memory_space=pl.ANY)
    return pl.pallas_call(
        body, name=name, grid=(N_CHIPS,), in_specs=[any_spec] * 3,
        out_specs=pl.BlockSpec((2, 1, half, tn), lambda j: (0, j, 0, 0)),
        out_shape=jax.ShapeDtypeStruct((2, N_CHIPS, half, tn), F32),
        scratch_shapes=[pltpu.VMEM((S, whole_w), a.dtype), pltpu.VMEM((2, S, slab_w), a.dtype),
                        pltpu.SemaphoreType.DMA((WEIGHT_GRAD_CHUNKS,)), pltpu.SemaphoreType.DMA((2, WEIGHT_GRAD_CHUNKS))],
        compiler_params=_params("arbitrary"),
    )(a, b, after)


def _mixer_bwd(dx1, y2, ycat, ya, yx, bcu, qx, mkv, conv_w, g_a, g_c, g_x, w_out, g_post, after, tm):
    S = dx1.shape[0]
    n_mem = mkv.shape[0]
    n_tiles = S // tm
    half = D_MODEL // N_CHIPS // 2

    def body(dx1_ref, y2_ref, ycat_ref, ya_ref, yx_ref, bcu_ref, before_ref, qx_ref, mkv_ref, cw_ref, ga_ref, gc_ref,
             gx_ref, wo_ref, gp_ref, after_ref, gwo_ref, dya_ref, delta_ref, tail_ref, dmkv_ref, dcw_ref, dgp_ref,
             dga_ref, dgc_ref, dgx_ref, carry):
        step = pl.program_id(0)
        first_tile = step == n_tiles - 1

        @pl.when(step == 0)
        def _():
            for ref in (gwo_ref, dmkv_ref, dcw_ref, dgp_ref, dga_ref, dgc_ref, dgx_ref, carry):
                ref[...] = jnp.zeros_like(ref)

        dx1 = dx1_ref[...]
        y2hat, r2 = _rms_hat(y2_ref[...])
        dgp_ref[...] += jnp.sum(dx1 * y2hat, axis=0, keepdims=True)
        dy2 = _rms_bwd(y2hat, r2, gp_ref[...], dx1).astype(BF16)
        gwo = lax.dot_general(ycat_ref[...], dy2, TN, preferred_element_type=F32)
        for k in range(2 * N_CHIPS):
            gwo_ref[k % 2, k // 2] += gwo[half * k:half * (k + 1)]
        dycat = lax.dot_general(dy2, wo_ref[...], NT, preferred_element_type=F32)

        d_na = dycat[:, 0:ATTN_W]
        ya = ya_ref[...]
        yahat, ra = _rms_hat(ya)
        dga_ref[...] += jnp.sum(d_na * yahat, axis=0, keepdims=True)
        dya = _rms_bwd(yahat, ra, ga_ref[...], d_na)
        dya_ref[...] = dya
        prod = dya * ya
        hi = prod.astype(BF16)
        lo = (prod - hi.astype(F32)).astype(BF16)
        head_of = lambda axis: lax.shift_right_logical(lax.broadcasted_iota(jnp.int32, (ATTN_W, ATTN_W), axis),
                                                       HEAD.bit_length() - 1)
        ones = jnp.where(head_of(0) == head_of(1), 1.0, 0.0).astype(BF16)
        delta_ref[...] = jnp.dot(hi, ones, preferred_element_type=F32) + jnp.dot(lo, ones, preferred_element_type=F32)

        w = cw_ref[...]
        b, c, u, z, z1, z2, cv = _conv_fwd(bcu_ref[...], before_ref[...], first_tile, w)
        d_nc = dycat[:, ATTN_W:ATTN_W + CONV_W]
        ychat, rc = _rms_hat(b * cv)
        dgc_ref[...] += jnp.sum(d_nc * ychat, axis=0, keepdims=True)
        dyc = _rms_bwd(ychat, rc, gc_ref[...], d_nc)
        dcv = dyc * b
        behind = carry[...]
        dz = w[2:3, :] * dcv + w[1:2, :] * _shift_up(dcv, behind, 1) + w[0:1, :] * _shift_up(dcv, behind, 2)
        carry[...] = dcv[0:8, :]
        dcw_ref[0:1, :] += jnp.sum(dcv * z2, axis=0, keepdims=True)
        dcw_ref[1:2, :] += jnp.sum(dcv * z1, axis=0, keepdims=True)
        dcw_ref[2:3, :] += jnp.sum(dcv * z, axis=0, keepdims=True)
        tail_ref[:, 0:CONV_W] = (dyc * cv).astype(BF16)
        tail_ref[:, CONV_W:2 * CONV_W] = (dz * u).astype(BF16)
        tail_ref[:, 2 * CONV_W:3 * CONV_W] = (dz * c).astype(BF16)

        d_nx = dycat[:, ATTN_W + CONV_W:D_MODEL]
        yxhat, rx = _rms_hat(yx_ref[...])
        dgx_ref[...] += jnp.sum(d_nx * yxhat, axis=0, keepdims=True)
        dyx = _rms_bwd(yxhat, rx, gx_ref[...], d_nx)
        qxb, mkvb = qx_ref[...], mkv_ref[...]
        heads = [slice(HEAD * hd, HEAD * (hd + 1)) for hd in range(XATTN_W // HEAD)]
        values = [slice(XATTN_W + sl.start, XATTN_W + sl.stop) for sl in heads]
        ss = [lax.dot_general(qxb[:, sl], mkvb[:, sl], NT, preferred_element_type=F32) * SCALE for sl in heads]
        es = [jnp.exp(s - jnp.max(s, axis=1, keepdims=True)) for s in ss]
        ps = [e / jnp.sum(e, axis=1, keepdims=True) for e in es]
        dobs = [dyx[:, sl].astype(BF16) for sl in heads]
        dps = [lax.dot_general(dob, mkvb[:, vsl], NT, preferred_element_type=F32) for dob, vsl in zip(dobs, values)]
        dss = [(p * (dp - jnp.sum(p * dp, axis=1, keepdims=True)) * SCALE).astype(BF16) for p, dp in zip(ps, dps)]
        for sl, vsl, p, dob, ds in zip(heads, values, ps, dobs, dss):
            tail_ref[:, 3 * CONV_W + sl.start:3 * CONV_W + sl.stop] = jnp.dot(
                ds, mkvb[:, sl], preferred_element_type=F32).astype(BF16)
            dmkv_ref[:, sl] += lax.dot_general(ds, qxb[:, sl], TN, preferred_element_type=F32)
            dmkv_ref[:, vsl] += lax.dot_general(p.astype(BF16), dob, TN, preferred_element_type=F32)

    rows = lambda width: pl.BlockSpec((tm, width), lambda i: (n_tiles - 1 - i, 0))
    before = pl.BlockSpec((8, 3 * CONV_W), lambda i: (jnp.maximum((n_tiles - 1 - i) * (tm // 8) - 1, 0), 0))
    acc = lambda r, w: pl.BlockSpec((r, w), lambda i: (0, 0))
    return pl.pallas_call(
        body, name="mixer_bwd", grid=(n_tiles,),
        in_specs=[rows(D_MODEL), rows(D_MODEL), rows(D_MODEL), rows(ATTN_W), rows(XATTN_W), rows(3 * CONV_W), before,
                  rows(XATTN_W), _resident((n_mem, 2 * XATTN_W)), _resident((3, CONV_W)), _resident((1, ATTN_W)),
                  _resident((1, CONV_W)), _resident((1, XATTN_W)), _resident((D_MODEL, D_MODEL)),
                  _resident((1, D_MODEL)), pl.BlockSpec(memory_space=pl.ANY)],
        out_specs=[pl.BlockSpec((2, N_CHIPS, half, D_MODEL), lambda i: (0, 0, 0, 0)), rows(ATTN_W), rows(ATTN_W),
                   rows(3 * CONV_W + XATTN_W), acc(n_mem, 2 * XATTN_W),
                   acc(3, CONV_W), acc(1, D_MODEL), acc(1, ATTN_W), acc(1, CONV_W), acc(1, XATTN_W)],
        out_shape=[jax.ShapeDtypeStruct((2, N_CHIPS, half, D_MODEL), F32), jax.ShapeDtypeStruct((S, ATTN_W), F32),
                   jax.ShapeDtypeStruct((S, ATTN_W), F32), jax.ShapeDtypeStruct((S, 3 * CONV_W + XATTN_W), BF16),
                   jax.ShapeDtypeStruct((n_mem, 2 * XATTN_W), F32), jax.ShapeDtypeStruct((3, CONV_W), F32),
                   jax.ShapeDtypeStruct((1, D_MODEL), F32), jax.ShapeDtypeStruct((1, ATTN_W), F32),
                   jax.ShapeDtypeStruct((1, CONV_W), F32), jax.ShapeDtypeStruct((1, XATTN_W), F32)],
        scratch_shapes=[pltpu.VMEM((8, CONV_W), F32)],
        compiler_params=_params("arbitrary"),
    )(dx1, y2, ycat, ya, yx, bcu, bcu, qx, mkv, conv_w, g_a, g_c, g_x, w_out, g_post, after)


def _memkv_bwd(mem, g_mem, w_kv, dmkv):
    n_mem = mem.shape[0]
    half = D_MODEL // N_CHIPS // 2

    def body(mem_ref, g_ref, w_ref, d_ref, dw_ref, dg_ref):
        mhat, _ = _rms_hat(mem_ref[...])
        mn = (mhat * g_ref[...]).astype(BF16)
        d = d_ref[...].astype(BF16)
        for k in range(2 * N_CHIPS):
            dw_ref[k % 2, k // 2] = lax.dot_general(mn[:, half * k:half * (k + 1)], d, TN, preferred_element_type=F32)
        dmn = lax.dot_general(d, w_ref[...], NT, preferred_element_type=F32)
        dg_ref[...] = jnp.sum(dmn * mhat, axis=0, keepdims=True)

    return pl.pallas_call(
        body, name="memkv_bwd",
        out_shape=[jax.ShapeDtypeStruct((2, N_CHIPS, half, 2 * XATTN_W), F32), jax.ShapeDtypeStruct((1, D_MODEL), F32)],
        compiler_params=pltpu.CompilerParams(vmem_limit_bytes=VMEM_LIMIT_V7X),
    )(mem, g_mem, w_kv, dmkv)


def _in_proj_bwd(dqkv, tail, cos, sin, w_in, x, h, g, dx1, after, tm):
    S = x.shape[0]
    step_w = 2 * 256
    half = D_MODEL // 2

    def body(dq_ref, dk_ref, dv_ref, tail_ref, cos_ref, sin_ref, w_hbm, x_ref, h_ref, g_ref, dx1_ref, after_ref,
             dx_ref, gw_ref, dg_ref, dproj_ref, w_full, sems):
        _side_by_side(w_hbm, w_full, sems)

        @pl.when(pl.program_id(0) == 0)
        def _():
            dg_ref[...] = jnp.zeros_like(dg_ref)
            gw_ref[...] = jnp.zeros_like(gw_ref)

        halves = [slice(0, tm // 2), slice(tm // 2, tm)]
        for rows in halves:
            c, s = cos_ref[rows, :], sin_ref[rows, :]
            for j in range(ATTN_W // 128):
                cols = slice(128 * j, 128 * (j + 1))
                dproj_ref[rows, cols] = _rope128(dq_ref[rows, cols] * SCALE, c, s, True).astype(BF16)
                dproj_ref[rows, ATTN_W + 128 * j:ATTN_W + 128 * (j + 1)] = _rope128(dk_ref[rows, cols], c, s, True).astype(BF16)
            dproj_ref[rows, 2 * ATTN_W:3 * ATTN_W] = dv_ref[rows, :].astype(BF16)
            dproj_ref[rows, 3 * ATTN_W:PROJ_W] = tail_ref[rows, :]
        dhs = [lax.dot_general(dproj_ref[rows, :], w_full[...], NT, preferred_element_type=F32) for rows in halves]
        for rows, dh in zip(halves, dhs):
            xhat, r = _rms_hat(x_ref[rows, :])
            dg_ref[...] += jnp.sum(dh * xhat, axis=0, keepdims=True)
            dx_ref[rows, :] = dx1_ref[rows, :] + _rms_bwd(xhat, r, g_ref[...], dh)
        hb = h_ref[...]
        for step in range(PROJ_W // step_w):
            res = lax.dot_general(hb, dproj_ref[:, step * step_w:(step + 1) * step_w], TN, preferred_element_type=F32)
            lo = step * step_w
            while lo < (step + 1) * step_w:
                chip = lo // SHARD_IN
                hi = min((step + 1) * step_w, (chip + 1) * SHARD_IN)
                for hh in range(2):
                    gw_ref[hh, chip, :, lo - chip * SHARD_IN:hi - chip * SHARD_IN] += (
                        res[half * hh:half * (hh + 1), lo - step * step_w:hi - step * step_w])
                lo = hi

    whole = lambda shape: pl.BlockSpec(shape, lambda i: (0,) * len(shape))
    return pl.pallas_call(
        body, name="in_proj_bwd", grid=(S // tm,),
        in_specs=[_rows(tm, ATTN_W)] * 3 + [_rows(tm, PROJ_W - 3 * ATTN_W), _rows(tm, 128), _rows(tm, 128),
                  pl.BlockSpec(memory_space=pl.ANY), _rows(tm, D_MODEL), _rows(tm, D_MODEL), _resident((1, D_MODEL)),
                  _rows(tm, D_MODEL), pl.BlockSpec(memory_space=pl.ANY)],
        out_specs=[_rows(tm, D_MODEL), whole((2, N_CHIPS, half, SHARD_IN)), whole((1, D_MODEL))],
        out_shape=[jax.ShapeDtypeStruct((S, D_MODEL), F32), jax.ShapeDtypeStruct((2, N_CHIPS, half, SHARD_IN), F32),
                   jax.ShapeDtypeStruct((1, D_MODEL), F32)],
        scratch_shapes=[pltpu.VMEM((tm, PROJ_W), BF16), pltpu.VMEM((D_MODEL, PROJ_W), BF16),
                        pltpu.SemaphoreType.DMA((N_CHIPS,))],
        compiler_params=_params("arbitrary"),
    )(*dqkv, tail, cos, sin, w_in, x, h, g, dx1, after)


def _row_tile(rows):
    return ROW_TILE if rows % ROW_TILE == 0 else rows


def _chip_sums_bf16(name, grads, from_sibling, place):
    k = len(grads)
    _, n, rows, _ = grads[0].shape
    tr = _row_tile(rows)

    def body(place_ref, *refs):
        for g_ref, b_ref, o_ref in zip(refs[:k], refs[k:2 * k], refs[2 * k:]):
            o_ref[...] = (g_ref[0] + b_ref[...]).astype(BF16)

    mine = lambda g: pl.BlockSpec((1, 1, tr, g.shape[3]), lambda s, i, p: (p[0], s, i, 0))
    slab = lambda g: pl.BlockSpec((1, tr, g.shape[3]), lambda s, i, p: (s, i, 0))
    return pl.pallas_call(
        body, name=name, out_shape=[jax.ShapeDtypeStruct(g.shape[1:], BF16) for g in grads],
        grid_spec=pltpu.PrefetchScalarGridSpec(
            num_scalar_prefetch=1, grid=(n, rows // tr),
            in_specs=[mine(g) for g in grads] + [slab(g) for g in grads], out_specs=[slab(g) for g in grads]),
        compiler_params=_params("parallel", "parallel"),
    )(place, *grads, *from_sibling)


def _final_sums(name, grads, from_sibling, others, place):
    k = len(grads)
    rows = grads[0].shape[2]
    tr = _row_tile(rows)

    def body(place_ref, *refs):
        for a in range(k):
            own_ref, sib_ref = refs[a], refs[k + a]
            acc = own_ref[0, 0] + sib_ref[0]
            for o in refs[2 * k + 3 * a:2 * k + 3 * a + 3]:
                acc = acc + o[0].astype(F32)
            refs[5 * k + a][0] = acc

    own = lambda g: pl.BlockSpec((1, 1, tr, g.shape[3]), lambda i, p: (p[0], p[1], i, 0))
    sib = lambda g: pl.BlockSpec((1, tr, g.shape[3]), lambda i, p: (p[1], i, 0))
    other = lambda g, j: pl.BlockSpec((1, tr, g.shape[3]), lambda i, p: (j, i, 0))
    return pl.pallas_call(
        body, name=name, out_shape=[jax.ShapeDtypeStruct((2,) + g.shape[2:], F32) for g in grads],
        grid_spec=pltpu.PrefetchScalarGridSpec(
            num_scalar_prefetch=1, grid=(rows // tr,),
            in_specs=[own(g) for g in grads] + [sib(g) for g in grads] + [other(g, j) for g in grads for j in range(3)],
            out_specs=[pl.BlockSpec((1, tr, g.shape[3]), lambda i, p: (p[0], i, 0)) for g in grads]),
        compiler_params=_params("parallel"),
    )(place, *grads, *from_sibling, *[o for o in others for _ in range(3)])


def _adamw_update(w, g, m, v):
    m = ADAM_B1 * m + (1.0 - ADAM_B1) * g
    v = ADAM_B2 * v + (1.0 - ADAM_B2) * (g * g)
    m_hat = m * (1.0 / (1.0 - ADAM_B1 ** ADAM_STEP))
    v_hat = v * (1.0 / (1.0 - ADAM_B2 ** ADAM_STEP))
    return -ADAM_LR * (m_hat / (jnp.sqrt(v_hat) + ADAM_EPS) + ADAM_WD * w), m, v


def _adamw(name, params, after):
    k = len(params)
    rows = params[0][0].shape[0]
    tr = ADAMW_ROW_TILE if rows % ADAMW_ROW_TILE == 0 else rows

    def body(*refs):
        ins, outs = refs[:4 * k], refs[4 * k + 1:]
        for a in range(k):
            w_ref, g_ref, m_ref, v_ref = ins[4 * a:4 * a + 4]
            g = g_ref[...]
            outs[4 * a][...] = g
            outs[4 * a + 1][...], outs[4 * a + 2][...], outs[4 * a + 3][...] = _adamw_update(w_ref[...], g, m_ref[...], v_ref[...])

    spec = lambda w: pl.BlockSpec((tr, w.shape[1]), lambda i: (i, 0))
    out = pl.pallas_call(
        body, name=name, grid=(rows // tr,),
        in_specs=[spec(p[0]) for p in params for _ in range(4)] + [pl.BlockSpec(memory_space=pl.ANY)],
        out_specs=[spec(p[0]) for p in params for _ in range(4)],
        out_shape=[jax.ShapeDtypeStruct(p[0].shape, F32) for p in params for _ in range(4)],
        compiler_params=_params("parallel"),
    )(*[t for p in params for t in p], after)
    return [out[4 * a:4 * a + 4] for a in range(k)]


def _small_update(summed, chip, gains, gains_m, gains_v, taps, taps_m, taps_v):
    n = len(gains)
    widths = [g.shape[1] for g in gains]
    k, w = taps.shape

    def body(*refs):
        chip_ref, sum_ref = refs[0], refs[1]
        params = [refs[2 + 3 * i:5 + 3 * i] for i in range(n + 1)]
        outs = [refs[2 + 3 * (n + 1) + 4 * i:2 + 3 * (n + 1) + 4 * (i + 1)] for i in range(n + 1)]
        loss_ref = refs[-1]
        for i in range(n):
            g = sum_ref[i:i + 1, 0:widths[i]]
            wr, mr, vr = params[i]
            outs[i][0][...] = g
            outs[i][1][...], outs[i][2][...], outs[i][3][...] = _adamw_update(wr[...], g, mr[...], vr[...])
        g = sum_ref[n:n + k, 0:w]
        for j in range(1, N_CHIPS):
            g = jnp.where(chip_ref[0] == j, sum_ref[n:n + k, w * j:w * (j + 1)], g)
        wr, mr, vr = params[n]
        outs[n][0][...] = g
        outs[n][1][...], outs[n][2][...], outs[n][3][...] = _adamw_update(wr[...], g, mr[...], vr[...])
        loss_ref[...] = sum_ref[n + k:n + k + 1, 0:1]

    vmem = pl.BlockSpec(memory_space=pltpu.VMEM)
    operands = [chip, summed]
    for p in zip(list(gains) + [taps], list(gains_m) + [taps_m], list(gains_v) + [taps_v]):
        operands += list(p)
    shapes = [jax.ShapeDtypeStruct(p.shape, F32) for p in list(gains) + [taps] for _ in range(4)]
    out = pl.pallas_call(
        body, name="small_update", out_shape=shapes + [jax.ShapeDtypeStruct((1, 1), F32)],
        in_specs=[pl.BlockSpec(memory_space=pltpu.SMEM)] + [vmem] * (len(operands) - 1),
        out_specs=[vmem] * (len(shapes) + 1),
    )(*operands)
    return [out[4 * i:4 * (i + 1)] for i in range(n + 1)], out[-1]


def _sum_blocks(name, blocks):
    n, rows, cols = blocks.shape

    def body(b_ref, o_ref):
        acc = b_ref[0]
        for k in range(1, n):
            acc = acc + b_ref[k]
        o_ref[...] = acc

    return pl.pallas_call(body, name=name, out_shape=jax.ShapeDtypeStruct((rows, cols), F32))(blocks)


def _place():
    return lax.axis_index("x"), lax.axis_index("y"), lax.axis_index("c")


def _other_chips(x, y):
    return [(1 - x, y), (x, 1 - y), (1 - x, 1 - y)]


def _allgather_finish(name, shards, landed, pass_on):
    n = len(shards)

    def body(*refs):
        ins, outs, stage = refs[:n], refs[2 * n:3 * n], refs[3 * n:4 * n]
        send_sems, recv_sems, local_sems = refs[4 * n:]
        x, y, c = _place()
        chips = _other_chips(x, y)
        barrier = pltpu.get_barrier_semaphore()
        pl.semaphore_signal(barrier, inc=1, device_id=(x, y, 1 - c), device_id_type=MESH)
        pl.semaphore_wait(barrier, 1)

        def copy(a, k, chip, half):
            place = outs[a].at[2 * chip[0] + chip[1], half]
            return pltpu.make_async_remote_copy(
                src_ref=place, dst_ref=place, send_sem=send_sems.at[3 * a + k], recv_sem=recv_sems.at[3 * a + k],
                device_id=(x, y, 1 - c), device_id_type=MESH)

        load = [pltpu.make_async_copy(ins[a], stage[a], local_sems.at[a]) for a in range(n)]
        local = [pltpu.make_async_copy(stage[a], outs[a].at[2 * x + y], local_sems.at[a]) for a in range(n)]
        for cp in load:
            cp.start()
        passed = [copy(a, k, chip, c) for a in range(n) if pass_on[a] for k, chip in enumerate(chips)]
        for cp in passed:
            cp.start()
        for a in range(n):
            load[a].wait()
            local[a].start()
        for a in range(n):
            if pass_on[a]:
                for k, chip in enumerate(chips):
                    copy(a, k, chip, 1 - c).wait_recv()
        for cp in passed:
            cp.wait_send()
        for cp in local:
            cp.wait()

    any_spec = pl.BlockSpec(memory_space=pl.ANY)
    return pl.pallas_call(
        body, name=name,
        out_shape=[jax.ShapeDtypeStruct((N_CHIPS,) + s.shape, s.dtype) for s in shards],
        in_specs=[any_spec] * (2 * n), out_specs=[any_spec] * n,
        input_output_aliases={n + a: a for a in range(n)},
        scratch_shapes=[pltpu.VMEM(s.shape, s.dtype) for s in shards]
        + [pltpu.SemaphoreType.DMA((3 * n,)), pltpu.SemaphoreType.DMA((3 * n,)), pltpu.SemaphoreType.DMA((n,))],
        compiler_params=pltpu.CompilerParams(vmem_limit_bytes=VMEM_LIMIT_V7X, collective_id=HANDSHAKES["sibling"][0]),
    )(*shards, *landed)


def _plan_first_hop(x, y, c, shards, lands):
    return [(shards[a].at[c], lands[a].at[2 * x + y, c], lands[a].at[2 * chip[0] + chip[1], c], (*chip, c))
            for a in range(len(shards)) for chip in _other_chips(x, y)]


def _plan_pass_on(x, y, c, nothing, lands):
    def place(a, chip, half):
        return lands[a].at[2 * chip[0] + chip[1], half]

    return [(place(a, chip, c), place(a, chip, c), place(a, chip, 1 - c), (x, y, 1 - c))
            for a in range(len(lands)) for chip in _other_chips(x, y)]


def _plan_own_half_to_sibling(x, y, c, nothing, lands):
    return [(lands[a].at[c], lands[a].at[c], lands[a].at[1 - c], (x, y, 1 - c)) for a in range(len(lands))]


def _plan_other_half_to_sibling(x, y, c, grads, lands):
    return [(grads[a].at[1 - c], lands[a], lands[a], (x, y, 1 - c)) for a in range(len(grads))]


def _plan_to_other_chips(x, y, c, partials, lands):
    return [(partials[a].at[2 * chip[0] + chip[1]], lands[a].at[k], lands[a].at[k], (*chip, c))
            for a in range(len(partials)) for k, chip in enumerate(_other_chips(x, y))]


def _plan_to_all(x, y, c, blocks, lands):
    flips = [(fx, fy, fc) for fx in (0, 1) for fy in (0, 1) for fc in (0, 1) if (fx, fy, fc) != (0, 0, 0)]
    peers = [(1 - x if fx else x, 1 - y if fy else y, 1 - c if fc else c) for fx, fy, fc in flips]
    return [(blocks[0], lands[0].at[4 * x + 2 * y + c], lands[0].at[4 * p[0] + 2 * p[1] + p[2]], p) for p in peers]


def _planned_copies(plan, srcs, lands, send_sems, recv_sems):
    x, y, c = _place()

    def pair(k, src, there, here, to):
        make = lambda dst: pltpu.make_async_remote_copy(
            src_ref=src, dst_ref=dst, send_sem=send_sems.at[k], recv_sem=recv_sems.at[k], device_id=to, device_id_type=MESH)
        return make(there), make(here)

    return [pair(k, *entry) for k, entry in enumerate(plan(x, y, c, srcs, lands))]


_HBM_SPEC = pl.BlockSpec(memory_space=pltpu.HBM)
_SEM_SPEC = pl.BlockSpec(memory_space=pltpu.SEMAPHORE)


def _hbm(a):
    return pltpu.with_memory_space_constraint(a, pltpu.HBM)


HANDSHAKES = {
    "sibling": (1, lambda x, y, c: [(x, y, 1 - c)]),
}


def _exchange_start(name, plan, n_copies, srcs, land_shapes, after, lands=None, peers=None):
    if lands is None:
        lands = [lax.empty(s.shape, s.dtype) for s in land_shapes]
    land_shapes = lands
    ns, nl = len(srcs), len(land_shapes)
    n_in = ns + nl + 1
    collective_id, peers_of = HANDSHAKES[peers] if peers else (None, None)

    def body(*refs):
        if peers:
            who = peers_of(*_place())
            barrier = pltpu.get_barrier_semaphore()
            for peer in who:
                pl.semaphore_signal(barrier, inc=1, device_id=peer, device_id_type=MESH)
            pl.semaphore_wait(barrier, len(who))
        for send, _ in _planned_copies(plan, refs[:ns], refs[ns:ns + nl], refs[n_in], refs[n_in + 1]):
            send.start()
        refs[-1][...] = jnp.zeros_like(refs[-1])

    out = pl.pallas_call(
        body, name=name,
        out_shape=(pltpu.SemaphoreType.DMA((n_copies,)), pltpu.SemaphoreType.DMA((n_copies,)),
                   *[pltpu.HBM(s.shape, s.dtype) for s in land_shapes], jax.ShapeDtypeStruct((8, 128), F32)),
        in_specs=[_HBM_SPEC] * (ns + nl) + [pl.BlockSpec(memory_space=pl.ANY)],
        out_specs=(_SEM_SPEC, _SEM_SPEC, *[_HBM_SPEC] * nl, pl.BlockSpec(memory_space=pltpu.VMEM)),
        input_output_aliases={ns + i: 2 + i for i in range(nl)},
        compiler_params=pltpu.CompilerParams(has_side_effects=pltpu.SideEffectType.DATAFLOW_SIDE_EFFECTING,
                                             collective_id=collective_id),
    )(*[_hbm(s) for s in srcs], *[_hbm(l) for l in lands], after)
    return out[0], out[1], list(out[2:2 + nl]), out[-1]


def _exchange_wait(name, plan, srcs, started, after):
    send_sems, recv_sems, lands, _ = started
    ns, nl = len(srcs), len(lands)
    after = list(after) if isinstance(after, (list, tuple)) else [after]

    def body(*refs):
        for send, recv in _planned_copies(plan, refs[:ns], refs[ns:ns + nl], refs[ns + nl], refs[ns + nl + 1]):
            send.wait_send()
            recv.wait_recv()

    return pl.pallas_call(
        body, name=name, out_shape=[pltpu.HBM(l.shape, l.dtype) for l in lands],
        in_specs=[_HBM_SPEC] * (ns + nl) + [_SEM_SPEC, _SEM_SPEC] + [pl.BlockSpec(memory_space=pl.ANY)] * len(after),
        out_specs=[_HBM_SPEC] * nl, input_output_aliases={ns + i: i for i in range(nl)},
        compiler_params=pltpu.CompilerParams(has_side_effects=pltpu.SideEffectType.DATAFLOW_SIDE_EFFECTING),
    )(*[_hbm(s) for s in srcs], *lands, send_sems, recv_sems, *after)


def _like(arrays, lead, dtype=None):
    return [jax.ShapeDtypeStruct(tuple(lead) + a.shape[-2:], dtype or a.dtype) for a in arrays]


class _StepExchanges:
    def __init__(self, mats, conv_w):
        x, y, c = _place()
        self.place = jnp.stack([c, 2 * x + y]).astype(jnp.int32)
        shards = [w.astype(BF16).reshape(2, w.shape[0] // 2, w.shape[1]) for w in mats]
        self._in_shard = shards[:1]
        self._in = _exchange_start("w_in_allgather_start", _plan_first_hop, 3, self._in_shard,
                                   _like(self._in_shard, (N_CHIPS, 2)), shards[0])
        self.zero = self._in[3]
        taps = jnp.pad(conv_w, ((0, 8 - conv_w.shape[0]), (0, 128 - conv_w.shape[1])))
        self._rest_shards = shards[1:] + [jnp.stack([taps, jnp.zeros_like(taps)])]
        self._taps_shape = conv_w.shape
        self._groups = {}

    def w_in(self, after):
        landed = _exchange_wait("w_in_allgather_wait", _plan_first_hop, self._in_shard, self._in,
                                list(after) + self._rest_shards)
        (w_in,) = _allgather_finish("w_in_allgather_finish", self._in_shard, landed, [True])
        self._rest = _exchange_start("rest_allgather_start", _plan_first_hop, 3 * len(self._rest_shards),
                                     self._rest_shards, _like(self._rest_shards, (N_CHIPS, 2)), w_in)
        self.zero = self._rest[3]
        return w_in.reshape(N_CHIPS, 2 * w_in.shape[2], w_in.shape[3])

    def rest_weights(self, after):
        landed = _exchange_wait("rest_allgather_wait", _plan_first_hop, self._rest_shards, self._rest, after)
        kv, out, up, down, taps = _allgather_finish("rest_allgather_finish", self._rest_shards, landed,
                                                    [True, True, False, False, True])
        self._up_down = _exchange_start("up_down_pass_on_start", _plan_pass_on, 6, [], None, self.zero, lands=[up, down],
                                        peers="sibling")
        self.zero = self._up_down[3]
        k, w = self._taps_shape
        taps = taps[:, 0, :k, :w].transpose(1, 0, 2).reshape(k, N_CHIPS * w)
        return [g.reshape(N_CHIPS, 2 * g.shape[2], g.shape[3]) for g in (kv, out)], taps

    def up_down(self, after):
        full = _exchange_wait("up_down_pass_on_wait", _plan_pass_on, [], self._up_down, after)
        return [g.reshape(N_CHIPS, 2 * g.shape[2], g.shape[3]) for g in full]

    def send_grads(self, key, grads):
        grads = list(grads)
        started = _exchange_start(f"{key}_grads_to_sibling_start", _plan_other_half_to_sibling, len(grads), grads,
                                  _like(grads, (N_CHIPS,)), self.zero, peers="sibling")
        self._groups[key] = dict(grads=grads, to_sibling=started)
        self.zero = started[3]

    def grads_at_sibling(self, key, after):
        group = self._groups[key]
        grads = group["grads"]
        group["from_sibling"] = _exchange_wait(f"{key}_grads_to_sibling_wait", _plan_other_half_to_sibling, grads,
                                               group["to_sibling"], after)
        group["partials"] = _chip_sums_bf16(f"{key}_chip_sums", grads, group["from_sibling"], self.place)
        group["to_chips"] = _exchange_start(f"{key}_grads_to_chips_start", _plan_to_other_chips, 3 * len(grads),
                                            group["partials"], _like(group["partials"], (3,)), self.zero)
        self.zero = group["to_chips"][3]

    def grads_summed(self, key, after):
        group = self._groups[key]
        from_chips = _exchange_wait(f"{key}_grads_to_chips_wait", _plan_to_other_chips, group["partials"],
                                    group["to_chips"], after)
        return _final_sums(f"{key}_final_sums", group["grads"], group["from_sibling"], from_chips, self.place)

    def send_sums(self, key, sums):
        self._groups[key + "_sums"] = _exchange_start(f"{key}_sums_to_sibling_start", _plan_own_half_to_sibling,
                                                      len(sums), [], None, self.zero, lands=list(sums),
                                                      peers="sibling")
        self.zero = self._groups[key + "_sums"][3]

    def whole_sums(self, key, after):
        full = _exchange_wait(f"{key}_sums_to_sibling_wait", _plan_own_half_to_sibling, [], self._groups[key + "_sums"], after)
        return [t.reshape(2 * t.shape[1], t.shape[2]) for t in full]

    def send_small(self, block):
        self._small = block
        self._small_started = _exchange_start("small_grads_start", _plan_to_all, 7, [block],
                                              [jax.ShapeDtypeStruct((8,) + block.shape, block.dtype)], self.zero)
        self.zero = self._small_started[3]

    def small_summed(self, after):
        x, y, c = _place()
        (landed,) = _exchange_wait("small_grads_wait", _plan_to_all, [self._small], self._small_started, after)
        blocks = lax.dynamic_update_index_in_dim(landed, self._small, 4 * x + 2 * y + c, 0)
        return _sum_blocks("small_sum", blocks)


def _rope_tables(positions):
    half = HEAD // 2
    inv_freq = jnp.float32(ROPE_THETA) ** (-(jnp.arange(half, dtype=F32) * 2.0 / HEAD))
    ang = positions.astype(F32)[:, None] * inv_freq
    cos, sin = jnp.cos(ang), jnp.sin(ang)
    return jnp.tile(cos, (1, 4)), jnp.tile(jnp.concatenate([-sin, sin], axis=1), (1, 2))


def _local_step(x, mem, positions, target, gains, ex):
    g_pre_mix, g_mem, g_a, g_c, g_x, g_post_mix, g_pre_mlp, g_post_mlp = gains
    tm = ROW_TILE
    cos, sin = _rope_tables(positions)
    h = _pre_norm(x, g_pre_mix, ex.zero, tm)
    w_in = ex.w_in([h, cos, sin])

    q, k, v, bcu, qx = _in_proj_fwd(h, w_in, cos, sin, ex.zero, tm)
    ya, lse = _attn_fwd(q, k, v)
    (w_kv, w_out), conv_w = ex.rest_weights(lse)
    w_kv, w_out = (w.reshape(N_CHIPS * w.shape[1], w.shape[2]) for w in (w_kv, w_out))
    memn, mkv = _memkv_fwd(mem, g_mem, w_kv, ex.zero)
    yx, ycat, y2, x1 = _mix_fwd(ya, bcu, qx, mkv, conv_w, g_a, g_c, g_x, w_out, g_post_mix, x, tm)
    w_up, w_down = ex.up_down(x1)
    w_down = w_down.reshape(N_CHIPS * w_down.shape[1], w_down.shape[2])
    h2, f, du, df2, dx1, dg_pre_mlp, dg_post_mlp, loss = _mlp_fwd_bwd(x1, target, g_pre_mlp, g_post_mlp, w_up, w_down,
                                                                      MLP_ROW_TILE)
    gw_down = _weight_grad("grad_w_down", f, df2, True, ex.zero)
    gw_up = _weight_grad("grad_w_up", h2, du, False, ex.zero)
    ex.send_grads("early", [gw_up, gw_down])

    gw_out, dya, delta, tail, dmkv, g_conv, dg_post_mix, dg_a, dg_c, dg_x = _mixer_bwd(
        dx1, y2, ycat, ya, yx, bcu, qx, mkv, conv_w, g_a, g_c, g_x, w_out, g_post_mix, ex.zero, tm)
    ex.grads_at_sibling("early", dya)
    gw_kv, dg_mem = _memkv_bwd(mem, g_mem, w_kv, dmkv)
    ex.send_grads("mid", [gw_out, gw_kv])
    dqkv = _attn_bwd(q, k, v, dya, lse, delta, ex.zero)
    ex.grads_at_sibling("mid", dqkv[0])
    grad_x, gw_in, dg_pre_mix = _in_proj_bwd(dqkv, tail, cos, sin, w_in, x, h, g_pre_mix, dx1, ex.zero, tm)
    gain_grads = [dg_pre_mix, dg_mem, dg_a, dg_c, dg_x, dg_post_mix, dg_pre_mlp, dg_post_mlp]
    ex.send_small(_pack_small(gain_grads, g_conv, loss))
    ex.send_grads("late", [gw_in])
    return grad_x


def _pack_small(gains, conv, scalar=None):
    rows = [jnp.pad(g, ((0, 0), (0, D_MODEL - g.shape[1]))) for g in gains]
    rows.append(jnp.pad(conv, ((0, 0), (0, D_MODEL - conv.shape[1]))))
    last = jnp.zeros((SMALL_ROWS - 8 - conv.shape[0], D_MODEL), F32)
    rows.append(last if scalar is None else last.at[0:1, 0:1].set(scalar))
    return jnp.concatenate(rows, axis=0)


def kernel(x, mem, positions, g_pre_mix, g_mem, w_in, w_mem_kv, conv_w, g_attn_out, g_conv_out, g_xattn_out, w_out, g_post_mix, g_pre_mlp, w_up, w_down, g_post_mlp, loss_target, m_g_pre_mix, m_g_mem, m_w_in, m_w_mem_kv, m_conv_w, m_g_attn_out, m_g_conv_out, m_g_xattn_out, m_w_out, m_g_post_mix, m_g_pre_mlp, m_w_up, m_w_down, m_g_post_mlp, v_g_pre_mix, v_g_mem, v_w_in, v_w_mem_kv, v_conv_w, v_g_attn_out, v_g_conv_out, v_g_xattn_out, v_w_out, v_g_post_mix, v_g_pre_mlp, v_w_up, v_w_down, v_g_post_mlp):
    chip = 2 * lax.axis_index("x") + lax.axis_index("y")
    gains = [g_pre_mix, g_mem, g_attn_out, g_conv_out, g_xattn_out, g_post_mix, g_pre_mlp, g_post_mlp]
    gains_m = [m_g_pre_mix, m_g_mem, m_g_attn_out, m_g_conv_out, m_g_xattn_out, m_g_post_mix, m_g_pre_mlp, m_g_post_mlp]
    gains_v = [v_g_pre_mix, v_g_mem, v_g_attn_out, v_g_conv_out, v_g_xattn_out, v_g_post_mix, v_g_pre_mlp, v_g_post_mlp]
    mats =[w_in[0], w_mem_kv[0], w_out[0], w_up[0], w_down[0]]
    mats_m = [m_w_in[0], m_w_mem_kv[0], m_w_out[0], m_w_up[0], m_w_down[0]]
    mats_v = [v_w_in[0], v_w_mem_kv[0], v_w_out[0], v_w_up[0], v_w_down[0]]

    ex = _StepExchanges(mats, conv_w[0])
    grad_x = _local_step(x[0], mem[0], positions[0], loss_target[0], gains, ex)

    ex.send_sums("four", ex.grads_summed("early", ex.zero) + ex.grads_summed("mid", ex.zero))
    ex.grads_at_sibling("late", ex.zero)
    up_sum, down_sum, out_sum, kv_sum = ex.whole_sums("four", ex.zero)
    params = lambda a, g: (mats[a], g, mats_m[a], mats_v[a])
    new_up, new_down = _adamw("adamw_up_down", [params(3, up_sum), params(4, down_sum)], ex.zero)
    new_out, new_kv = _adamw("adamw_out_kv", [params(2, out_sum), params(1, kv_sum)], ex.zero)

    small, total = _small_update(ex.small_summed(new_kv[1]), chip.reshape(1).astype(jnp.int32), gains, gains_m,
                                 gains_v, conv_w[0], m_conv_w[0], v_conv_w[0])

    ex.send_sums("last", ex.grads_summed("late", small[0][1]))
    (in_sum,) = ex.whole_sums("last", ex.zero)
    (new_in,) = _adamw("adamw_in", [params(0, in_sum)], in_sum)
    mat_new = [new_in, new_kv, new_out, new_up, new_down]

    order = ["g_pre_mix", "g_mem", "w_in", "w_mem_kv", "conv_w", "g_attn_out", "g_conv_out", "g_xattn_out", "w_out",
             "g_post_mix", "g_pre_mlp", "w_up", "w_down", "g_post_mlp"]
    gain_names = ["g_pre_mix", "g_mem", "g_attn_out", "g_conv_out", "g_xattn_out", "g_post_mix", "g_pre_mlp", "g_post_mlp"]
    mat_names = ["w_in", "w_mem_kv", "w_out", "w_up", "w_down"]

    def leaf(kind, name):
        if name in gain_names:
            return small[gain_names.index(name)][kind]
        if name == "conv_w":
            return small[len(gain_names)][kind][None]
        return mat_new[mat_names.index(name)][kind][None]

    return (total[0, 0], grad_x[None], *[leaf(kind, name) for kind in range(4) for name in order])
```

```python
import jax
import jax.numpy as jnp
from jax import lax
from jax.experimental import pallas as pl
from jax.experimental.pallas import tpu as pltpu

F32, BF16 = jnp.float32, jnp.bfloat16

D_MODEL = 1024
ATTN_W = 512
CONV_W = 256
XATTN_W = 256
PROJ_W = 3 * ATTN_W + 3 * CONV_W + XATTN_W
D_FF = 4096
HEAD = 64
N_BACK = 128
DILATIONS = (1, 4, 16)
PATTERN_ORDER = DILATIONS[::-1]
ROPE_THETA = 10000.0
EPS = 1e-6
NEG_INF = -1e30
SCALE = HEAD ** -0.5
N_CHIPS = 4
SHARD_IN = PROJ_W // N_CHIPS
SHARD_FF = D_FF // N_CHIPS

ADAM_LR, ADAM_B1, ADAM_B2, ADAM_EPS, ADAM_WD, ADAM_STEP = 0.001, 0.9, 0.999, 1e-08, 0.01, 10

VMEM_LIMIT_V7X = 56 * 1024 * 1024
ROW_TILE = 512
MLP_ROW_TILE = 256
ADAMW_ROW_TILE = 256
SMALL_ROWS = 16

NT = (((1,), (1,)), ((), ()))
TN = (((0,), (0,)), ((), ()))
MESH = pl.DeviceIdType.MESH


def _params(*sem):
    return pltpu.CompilerParams(dimension_semantics=sem, vmem_limit_bytes=VMEM_LIMIT_V7X)


def _resident(shape):
    return pl.BlockSpec(shape, lambda *_: (0,) * len(shape), pipeline_mode=pl.Buffered(1))


def _rows(tm, width):
    return pl.BlockSpec((tm, width), lambda i: (i, 0))


def _rms_hat(x):
    r = lax.rsqrt(jnp.mean(x * x, axis=-1, keepdims=True) + EPS)
    return x * r, r


def _rms_bwd(xhat, r, g, dy):
    gdy = dy * g
    return r * (gdy - xhat * jnp.mean(xhat * gdy, axis=-1, keepdims=True))


def _rope128(t, cos, sin_signed, inverse):
    lane = lax.broadcasted_iota(jnp.int32, t.shape, 1)
    first_half = (lane % HEAD) < (HEAD // 2)
    rot = jnp.where(first_half, pltpu.roll(t, 128 - HEAD // 2, 1), pltpu.roll(t, HEAD // 2, 1))
    return t * cos - rot * sin_signed if inverse else t * cos + rot * sin_signed


def _pre_norm(x, g, after, tm):
    S = x.shape[0]

    def body(x_ref, g_ref, after_ref, h_ref):
        h_ref[...] = (_rms_hat(x_ref[...])[0] * g_ref[...]).astype(BF16)

    return pl.pallas_call(
        body, name="pre_norm", grid=(S // tm,),
        in_specs=[_rows(tm, D_MODEL), _resident((1, D_MODEL)), pl.BlockSpec(memory_space=pl.ANY)],
        out_specs=_rows(tm, D_MODEL), out_shape=jax.ShapeDtypeStruct((S, D_MODEL), BF16),
        compiler_params=_params("parallel"),
    )(x, g, after)


def _side_by_side(w_hbm, w_full, sems):
    width = w_hbm.shape[2]

    @pl.when(pl.program_id(0) == 0)
    def _():
        copies = [pltpu.make_async_copy(w_hbm.at[j], w_full.at[:, pl.ds(width * j, width)], sems.at[j])
                  for j in range(N_CHIPS)]
        for cp in copies:
            cp.start()
        for cp in copies:
            cp.wait()


def _in_proj_fwd(h, w_in, cos, sin, after, tm):
    S = h.shape[0]

    def body(h_ref, w_hbm, cos_ref, sin_ref, after_ref, q_ref, k_ref, v_ref, bcu_ref, qx_ref, proj, w_full, sems):
        _side_by_side(w_hbm, w_full, sems)
        proj[...] = jnp.dot(h_ref[...], w_full[...], preferred_element_type=F32)
        c, s = cos_ref[...], sin_ref[...]
        for j in range(ATTN_W // 128):
            lo = 128 * j
            q_ref[:, lo:lo + 128] = _rope128(proj[:, lo:lo + 128], c, s, False) * SCALE
            k_ref[:, lo:lo + 128] = _rope128(proj[:, ATTN_W + lo:ATTN_W + lo + 128], c, s, False)
        v_ref[...] = proj[:, 2 * ATTN_W:3 * ATTN_W]
        bcu_ref[...] = proj[:, 3 * ATTN_W:3 * ATTN_W + 3 * CONV_W]
        qx_ref[...] = proj[:, 3 * ATTN_W + 3 * CONV_W:PROJ_W].astype(BF16)

    return pl.pallas_call(
        body, name="in_proj_fwd", grid=(S // tm,),
        in_specs=[_rows(tm, D_MODEL), pl.BlockSpec(memory_space=pl.ANY), _rows(tm, 128), _rows(tm, 128),
                  pl.BlockSpec(memory_space=pl.ANY)],
        out_specs=[_rows(tm, ATTN_W), _rows(tm, ATTN_W), _rows(tm, ATTN_W), _rows(tm, 3 * CONV_W), _rows(tm, XATTN_W)],
        out_shape=[jax.ShapeDtypeStruct((S, ATTN_W), F32), jax.ShapeDtypeStruct((S, ATTN_W), F32),
                   jax.ShapeDtypeStruct((S, ATTN_W), F32), jax.ShapeDtypeStruct((S, 3 * CONV_W), F32),
                   jax.ShapeDtypeStruct((S, XATTN_W), BF16)],
        scratch_shapes=[pltpu.VMEM((tm, PROJ_W), F32), pltpu.VMEM((D_MODEL, PROJ_W), BF16),
                        pltpu.SemaphoreType.DMA((N_CHIPS,))],
        compiler_params=_params("arbitrary"),
    )(h, w_in, cos, sin, after)


def _memkv_fwd(mem, g_mem, w_kv, after):
    n_mem = mem.shape[0]

    def body(mem_ref, g_ref, w_ref, after_ref, mn_ref, kv_ref):
        mhat, _ = _rms_hat(mem_ref[...])
        mn = (mhat * g_ref[...]).astype(BF16)
        mn_ref[...] = mn
        kv_ref[...] = jnp.dot(mn, w_ref[...], preferred_element_type=F32).astype(BF16)

    vmem = pl.BlockSpec(memory_space=pltpu.VMEM)
    return pl.pallas_call(
        body, name="memkv_fwd", in_specs=[vmem, vmem, vmem, pl.BlockSpec(memory_space=pl.ANY)], out_specs=[vmem, vmem],
        out_shape=[jax.ShapeDtypeStruct((n_mem, D_MODEL), BF16), jax.ShapeDtypeStruct((n_mem, 2 * XATTN_W), BF16)],
        compiler_params=pltpu.CompilerParams(vmem_limit_bytes=VMEM_LIMIT_V7X),
    )(mem, g_mem, w_kv, after)


def _fill_band_bias(bias):
    row = lax.broadcasted_iota(jnp.int32, (N_BACK, 2 * N_BACK), 0)
    col = lax.broadcasted_iota(jnp.int32, (N_BACK, 2 * N_BACK), 1)
    band = (col >= row) & (col <= row + N_BACK)
    bias[1] = jnp.where(band, 0.0, NEG_INF)
    bias[0] = jnp.where(band & (col >= N_BACK), 0.0, NEG_INF)


def _strided(start, size, d):
    return pl.ds(start, size) if d == 1 else pl.ds(start, size, stride=d)


def _group_starts(g, G, nb, d):
    t0 = g * G
    r, n0 = lax.shift_right_logical(t0, nb.bit_length() - 1), lax.bitwise_and(t0, nb - 1)
    first = r + n0 * (N_BACK * d)
    before = r + jnp.maximum(n0 - 1, 0) * (N_BACK * d)
    starts = [before] + [first + u * (N_BACK * d) for u in range(G)]
    if d == 1:
        starts = [pl.multiple_of(st, N_BACK) for st in starts]
    return starts, n0


def _step_blocks(i, U, nb, d):
    G = min(U, nb)
    whole = G == nb
    row_blocks, blocks = [], []
    for grp in range(U // G):
        starts, n0 = _group_starts(i * (U // G) + grp, G, nb, d)
        base = len(row_blocks)
        if whole:
            row_blocks += [_strided(st, N_BACK, d) for st in starts[1:]]
            blocks += [(base + max(u - 1, 0), base + u, min(u, 1)) for u in range(G)]
        else:
            row_blocks += [_strided(st, N_BACK, d) for st in starts]
            blocks += [(base + u, base + u + 1, jnp.minimum(n0, 1) if u == 0 else 1) for u in range(G)]
    return row_blocks, blocks


def _by_head(a, b):
    lane = lax.broadcasted_iota(jnp.int32, (a.shape[0], 2 * HEAD), 1)
    return jnp.where(lane < HEAD, a, b)


def _head_only(t, hh):
    lane = lax.broadcasted_iota(jnp.int32, t.shape, 1)
    return jnp.where((lane < HEAD) == (hh == 0), t, jnp.zeros_like(t))


def _stack_heads(t):
    return jnp.concatenate([_head_only(t, 0), _head_only(t, 1)], axis=0)


def _head_columns(t):
    return jnp.concatenate([t[:, 0:1], t[:, HEAD:HEAD + 1]], axis=0)


def _unstack(t):
    return _by_head(t[:N_BACK], t[N_BACK:])


def _unstack_columns(t):
    return _by_head(jnp.broadcast_to(t[:N_BACK], (N_BACK, 2 * HEAD)), jnp.broadcast_to(t[N_BACK:], (N_BACK, 2 * HEAD)))


FWD_BLOCKS_PER_STEP = 4
BWD_BLOCKS_PER_STEP = 4
BWD_CHUNK = 64


def _attn_fwd(q, k, v):
    S = q.shape[0]
    U = FWD_BLOCKS_PER_STEP

    def body(q_ref, k_ref, v_ref, y_ref, m_ref, l_scr, bias):
        _fill_band_bias(bias)
        for g, d in enumerate(PATTERN_ORDER):
            nb = S // d // N_BACK
            first_pattern, last_pattern = g == 0, g == len(PATTERN_ORDER) - 1

            def step(i, carry, d=d, nb=nb, first_pattern=first_pattern, last_pattern=last_pattern):
                row_blocks, blocks = _step_blocks(i, U, nb, d)
                kb = [k_ref[r, :].astype(BF16) for r in row_blocks]
                ss = []
                for before, own, which in blocks:
                    kw = jnp.concatenate([kb[before], kb[own]], 0)
                    qs = _stack_heads(q_ref[row_blocks[own], :].astype(BF16))
                    b = bias[which]
                    ss.append(lax.dot_general(qs, kw, NT, preferred_element_type=F32) + jnp.concatenate([b, b], axis=0))
                ms = [jnp.max(s, axis=1, keepdims=True) for s in ss]
                ps = [jnp.exp(s - m) for s, m in zip(ss, ms)]
                ls = [jnp.sum(p, axis=1, keepdims=True) for p in ps]
                vb = [v_ref[r, :].astype(BF16) for r in row_blocks]
                os_ = [jnp.dot(ps[u].astype(BF16), jnp.concatenate([vb[before], vb[own]], 0), preferred_element_type=F32)
                       for u, (before, own, _) in enumerate(blocks)]
                for u, (_, own, _) in enumerate(blocks):
                    o_g, m_g, l_g = _unstack(os_[u]), _unstack_columns(ms[u]), _unstack_columns(ls[u])
                    r = row_blocks[own]
                    if first_pattern:
                        m_new, l_new, acc = m_g, l_g, o_g
                    else:
                        m_old = m_ref[r, :]
                        m_new = jnp.maximum(m_old, m_g)
                        alpha, beta = jnp.exp(m_old - m_new), jnp.exp(m_g - m_new)
                        l_new = l_scr[r, :] * alpha + l_g * beta
                        acc = y_ref[r, :] * alpha + o_g * beta
                    if last_pattern:
                        y_ref[r, :] = acc / l_new
                        m_ref[r, :] = m_new + jnp.log(l_new)
                    else:
                        y_ref[r, :] = acc
                        m_ref[r, :] = m_new
                        l_scr[r, :] = l_new
                return carry

            lax.fori_loop(0, d * nb // U, step, 0)

    col = pl.BlockSpec((S, 2 * HEAD), lambda j: (0, j))
    return pl.pallas_call(
        body, name="attn_fwd", grid=(q.shape[1] // (2 * HEAD),),
        in_specs=[col, col, col], out_specs=[col, col],
        out_shape=[jax.ShapeDtypeStruct(q.shape, F32)] * 2,
        scratch_shapes=[pltpu.VMEM((S, 2 * HEAD), F32), pltpu.VMEM((2, N_BACK, 2 * N_BACK), F32)],
        compiler_params=_params("parallel"),
    )(q, k, v)


def _attn_bwd(q, k, v, dy, lse, delta, after):
    S = q.shape[0]
    U = BWD_BLOCKS_PER_STEP

    def body(q_ref, k_ref, v_ref, dy_ref, lse_ref, delta_ref, after_ref, dq_ref, dk_ref, dv_ref, bias):
        _fill_band_bias(bias)
        nb_first = S // PATTERN_ORDER[0] // N_BACK
        first_writes_all = min(U, nb_first) == nb_first
        if not first_writes_all:
            dk_ref[...] = jnp.zeros_like(dk_ref)
            dv_ref[...] = jnp.zeros_like(dv_ref)
        for g, d in enumerate(PATTERN_ORDER):
            nb = S // d // N_BACK

            def step(i, carry, d=d, nb=nb, g=g):
                row_blocks, blocks = _step_blocks(i, U, nb, d)
                kb = [k_ref[r, :].astype(BF16) for r in row_blocks]
                vb = [v_ref[r, :].astype(BF16) for r in row_blocks]
                kws = [jnp.concatenate([kb[before], kb[own]], 0) for before, own, _ in blocks]
                vws = [jnp.concatenate([vb[before], vb[own]], 0) for before, own, _ in blocks]
                qss = [_stack_heads(q_ref[row_blocks[own], :].astype(BF16)) for _, own, _ in blocks]
                doss = [_stack_heads(dy_ref[row_blocks[own], :].astype(BF16)) for _, own, _ in blocks]
                ss = [lax.dot_general(qss[u], kws[u], NT, preferred_element_type=F32) for u in range(U)]
                dps = [lax.dot_general(doss[u], vws[u], NT, preferred_element_type=F32) for u in range(U)]
                pbs, dss = [], []
                for u, (_, own, which) in enumerate(blocks):
                    lse_c = _head_columns(lse_ref[row_blocks[own], :])
                    delta_c = _head_columns(delta_ref[row_blocks[own], :])
                    p_parts, ds_parts = [], []
                    for r0 in range(0, 2 * N_BACK, BWD_CHUNK):
                        r = slice(r0, r0 + BWD_CHUNK)
                        mask = bias[which, r0 % N_BACK:r0 % N_BACK + BWD_CHUNK, :]
                        p_r = jnp.exp(ss[u][r] + mask - lse_c[r])
                        p_parts.append(p_r.astype(BF16))
                        ds_parts.append((p_r * (dps[u][r] - delta_c[r])).astype(BF16))
                    pbs.append(jnp.concatenate(p_parts, axis=0))
                    dss.append(jnp.concatenate(ds_parts, axis=0))
                dqs = [jnp.dot(dss[u], kws[u], preferred_element_type=F32) for u in range(U)]
                dkws = [lax.dot_general(dss[u], qss[u], TN, preferred_element_type=F32) for u in range(U)]
                dvws = [lax.dot_general(pbs[u], doss[u], TN, preferred_element_type=F32) for u in range(U)]
                dk_parts, dv_parts = [None] * len(row_blocks), [None] * len(row_blocks)
                for u, (before, own, _) in enumerate(blocks):
                    dq = _unstack(dqs[u])
                    if g == 0:
                        dq_ref[row_blocks[own], :] = dq
                    else:
                        dq_ref[row_blocks[own], :] += dq
                    for idx, dkp, dvp in ((before, dkws[u][:N_BACK], dvws[u][:N_BACK]),
                                          (own, dkws[u][N_BACK:], dvws[u][N_BACK:])):
                        dk_parts[idx] = dkp if dk_parts[idx] is None else dk_parts[idx] + dkp
                        dv_parts[idx] = dvp if dv_parts[idx] is None else dv_parts[idx] + dvp
                for idx, r in enumerate(row_blocks):
                    if g == 0 and first_writes_all:
                        dk_ref[r, :] = dk_parts[idx]
                        dv_ref[r, :] = dv_parts[idx]
                    else:
                        dk_ref[r, :] += dk_parts[idx]
                        dv_ref[r, :] += dv_parts[idx]
                return carry

            lax.fori_loop(0, d * nb // U, step, 0)

    col = pl.BlockSpec((S, 2 * HEAD), lambda j: (0, j))
    return pl.pallas_call(
        body, name="attn_bwd", grid=(q.shape[1] // (2 * HEAD),),
        in_specs=[col] * 6 + [pl.BlockSpec(memory_space=pl.ANY)], out_specs=[col] * 3,
        out_shape=[jax.ShapeDtypeStruct(q.shape, F32)] * 3,
        scratch_shapes=[pltpu.VMEM((2, N_BACK, 2 * N_BACK), F32)],
        compiler_params=_params("parallel"),
    )(q, k, v, dy, lse, delta, after)


def _shift_down(z, before, k):
    row = lax.broadcasted_iota(jnp.int32, z.shape, 0)
    out = pltpu.roll(z, k, 0)
    for i in range(k):
        out = jnp.where(row == i, before[8 - k + i:8 - k + i + 1, :], out)
    return out


def _shift_up(z, after, k):
    rows = z.shape[0]
    row = lax.broadcasted_iota(jnp.int32, z.shape, 0)
    out = pltpu.roll(z, rows - k, 0)
    for i in range(k):
        out = jnp.where(row == rows - k + i, after[i:i + 1, :], out)
    return out


def _conv_fwd(bcu, before, is_first, w):
    b, c, u = bcu[:, 0:CONV_W], bcu[:, CONV_W:2 * CONV_W], bcu[:, 2 * CONV_W:3 * CONV_W]
    z = c * u
    zb = jnp.where(is_first, 0.0, before[:, CONV_W:2 * CONV_W] * before[:, 2 * CONV_W:3 * CONV_W])
    z1, z2 = _shift_down(z, zb, 1), _shift_down(z, zb, 2)
    cv = w[0:1, :] * z2 + w[1:2, :] * z1 + w[2:3, :] * z
    return b, c, u, z, z1, z2, cv


def _halo_before(tm, width):
    return pl.BlockSpec((8, width), lambda i: (jnp.maximum(i * (tm // 8) - 1, 0), 0))


def _mix_fwd(ya, bcu, qx, mkv, conv_w, g_a, g_c, g_x, w_out, g_post, x, tm):
    S = x.shape[0]

    def body(ya_ref, bcu_ref, before_ref, qx_ref, mkv_ref, cw_ref, ga_ref, gc_ref, gx_ref,
             wo_ref, gp_ref, x_ref, yx_ref, ycat_ref, y2_ref, x1_ref):
        ya = ya_ref[...]
        b, _, _, _, _, _, cv = _conv_fwd(bcu_ref[...], before_ref[...], pl.program_id(0) == 0, cw_ref[...])
        yc = b * cv

        qxb, mkvb = qx_ref[...], mkv_ref[...]
        heads = [slice(HEAD * hd, HEAD * (hd + 1)) for hd in range(XATTN_W // HEAD)]
        ss = [lax.dot_general(qxb[:, sl], mkvb[:, sl], NT, preferred_element_type=F32) * SCALE for sl in heads]
        ms = [jnp.max(s, axis=1, keepdims=True) for s in ss]
        ps = [jnp.exp(s - m) for s, m in zip(ss, ms)]
        ls = [jnp.sum(p, axis=1, keepdims=True) for p in ps]
        os_ = [jnp.dot(p.astype(BF16), mkvb[:, XATTN_W + sl.start:XATTN_W + sl.stop], preferred_element_type=F32)
               for p, sl in zip(ps, heads)]
        for sl, o, l in zip(heads, os_, ls):
            yx_ref[:, sl] = o / l
        yx = yx_ref[...]

        ycat_ref[:, 0:ATTN_W] = (_rms_hat(ya)[0] * ga_ref[...]).astype(BF16)
        ycat_ref[:, ATTN_W:ATTN_W + CONV_W] = (_rms_hat(yc)[0] * gc_ref[...]).astype(BF16)
        ycat_ref[:, ATTN_W + CONV_W:D_MODEL] = (_rms_hat(yx)[0] * gx_ref[...]).astype(BF16)
        y2 = jnp.dot(ycat_ref[...], wo_ref[...], preferred_element_type=F32)
        y2_ref[...] = y2
        x1_ref[...] = x_ref[...] + _rms_hat(y2)[0] * gp_ref[...]

    n_mem = mkv.shape[0]
    return pl.pallas_call(
        body, name="mix_fwd", grid=(S // tm,),
        in_specs=[_rows(tm, ATTN_W), _rows(tm, 3 * CONV_W), _halo_before(tm, 3 * CONV_W), _rows(tm, XATTN_W),
                  _resident((n_mem, 2 * XATTN_W)), _resident((3, CONV_W)), _resident((1, ATTN_W)),
                  _resident((1, CONV_W)), _resident((1, XATTN_W)), _resident((D_MODEL, D_MODEL)),
                  _resident((1, D_MODEL)), _rows(tm, D_MODEL)],
        out_specs=[_rows(tm, XATTN_W), _rows(tm, D_MODEL), _rows(tm, D_MODEL), _rows(tm, D_MODEL)],
        out_shape=[jax.ShapeDtypeStruct((S, XATTN_W), F32), jax.ShapeDtypeStruct((S, D_MODEL), BF16),
                   jax.ShapeDtypeStruct((S, D_MODEL), F32), jax.ShapeDtypeStruct((S, D_MODEL), F32)],
        compiler_params=_params("parallel"),
    )(ya, bcu, bcu, qx, mkv, conv_w, g_a, g_c, g_x, w_out, g_post, x)


def _mlp_fwd_bwd(x1, target, g_pre, g_post, w_up, w_down, tm):
    S = x1.shape[0]
    n_ff = D_FF // SHARD_FF
    n_tiles = S // tm
    LOSS_ROWS = 64

    def body(x1_ref, x1_prev_ref, t_prev_ref, gpre_ref, gpost_ref, wup_ref, wdn_ref,
             h2_ref, f_ref, du_ref, df2_ref, dx1_ref, dgpre_ref, dgpost_ref, loss_ref,
             u_scr, f2_scr, dx2_scr):
        i = pl.program_id(0)
        slot = i % 2

        def forward():
            x1hat, _ = _rms_hat(x1_ref[...])
            h2 = (x1hat * gpre_ref[...]).astype(BF16)
            h2_ref[...] = h2
            f2 = jnp.zeros((tm, D_MODEL), F32)
            for j in range(n_ff):
                cols = slice(SHARD_FF * j, SHARD_FF * (j + 1))
                u = jnp.maximum(jnp.dot(h2, wup_ref[j], preferred_element_type=F32), 0.0)
                u_scr[slot, :, cols] = u
                f = (u * u).astype(BF16)
                f_ref[:, cols] = f
                f2 = f2 + jnp.dot(f, wdn_ref[cols, :], preferred_element_type=F32)
            f2_scr[slot] = f2

        def loss():
            for rows in [slice(r, r + LOSS_ROWS) for r in range(0, tm, LOSS_ROWS)]:
                f2hat, r2 = _rms_hat(f2_scr[1 - slot, rows, :])
                err = x1_prev_ref[rows, :] + f2hat * gpost_ref[...] - t_prev_ref[rows, :]
                loss_ref[...] += 0.5 * jnp.sum(jnp.mean(err * err, axis=-1, keepdims=True), axis=0, keepdims=True)
                dx2 = err * (1.0 / D_MODEL)
                dx2_scr[rows, :] = dx2
                dgpost_ref[...] += jnp.sum(dx2 * f2hat, axis=0, keepdims=True)
                df2_ref[rows, :] = _rms_bwd(f2hat, r2, gpost_ref[...], dx2).astype(BF16)

        def backward():
            df2 = df2_ref[...]
            dh2 = jnp.zeros((tm, D_MODEL), F32)
            for j in range(n_ff):
                cols = slice(SHARD_FF * j, SHARD_FF * (j + 1))
                df = lax.dot_general(df2, wdn_ref[cols, :], NT, preferred_element_type=F32)
                du = (2.0 * u_scr[1 - slot, :, cols] * df).astype(BF16)
                du_ref[:, cols] = du
                dh2 = dh2 + lax.dot_general(du, wup_ref[j], NT, preferred_element_type=F32)
            x1hat, r1 = _rms_hat(x1_prev_ref[...])
            dgpre_ref[...] += jnp.sum(dh2 * x1hat, axis=0, keepdims=True)
            dx1_ref[...] = dx2_scr[...] + _rms_bwd(x1hat, r1, gpre_ref[...], dh2)

        @pl.when(i == 0)
        def _():
            dgpre_ref[...] = jnp.zeros_like(dgpre_ref)
            dgpost_ref[...] = jnp.zeros_like(dgpost_ref)
            loss_ref[...] = jnp.zeros_like(loss_ref)
            forward()

        @pl.when(jnp.logical_and(i > 0, i < n_tiles))
        def _():
            loss()
            forward()
            backward()

        @pl.when(i == n_tiles)
        def _():
            loss()
            backward()

    ahead = lambda width: pl.BlockSpec((tm, width), lambda i: (jnp.minimum(i, n_tiles - 1), 0))
    behind = lambda width: pl.BlockSpec((tm, width), lambda i: (jnp.maximum(i - 1, 0), 0))
    acc = pl.BlockSpec((1, D_MODEL), lambda i: (0, 0))
    return pl.pallas_call(
        body, name="mlp_fwd_bwd", grid=(n_tiles + 1,),
        in_specs=[ahead(D_MODEL), behind(D_MODEL), behind(D_MODEL), _resident((1, D_MODEL)), _resident((1, D_MODEL)),
                  _resident((n_ff, D_MODEL, SHARD_FF)), _resident((D_FF, D_MODEL))],
        out_specs=[ahead(D_MODEL), ahead(D_FF), behind(D_FF), behind(D_MODEL), behind(D_MODEL),
                   acc, acc, pl.BlockSpec((1, 1), lambda i: (0, 0))],
        out_shape=[jax.ShapeDtypeStruct((S, D_MODEL), BF16), jax.ShapeDtypeStruct((S, D_FF), BF16),
                   jax.ShapeDtypeStruct((S, D_FF), BF16), jax.ShapeDtypeStruct((S, D_MODEL), BF16),
                   jax.ShapeDtypeStruct((S, D_MODEL), F32), jax.ShapeDtypeStruct((1, D_MODEL), F32),
                   jax.ShapeDtypeStruct((1, D_MODEL), F32), jax.ShapeDtypeStruct((1, 1), F32)],
        scratch_shapes=[pltpu.VMEM((2, tm, D_FF), F32), pltpu.VMEM((2, tm, D_MODEL), F32), pltpu.VMEM((tm, D_MODEL), F32)],
        compiler_params=_params("arbitrary"),
    )(x1, x1, target, g_pre, g_post, w_up, w_down)


def _weight_grad(name, a, b, rows_sharded, after):
    S, K = a.shape
    N = b.shape[1]
    if rows_sharded:
        tk, tn = K // N_CHIPS, N
        a_spec = pl.BlockSpec((S, tk), lambda j: (0, j))
        b_spec = pl.BlockSpec((S, tn), lambda j: (0, 0), pipeline_mode=pl.Buffered(1))
    else:
        tk, tn = K, N // N_CHIPS
        a_spec = pl.BlockSpec((S, tk), lambda j: (0, 0), pipeline_mode=pl.Buffered(1))
        b_spec = pl.BlockSpec((S, tn), lambda j: (0, j))
    half = tk // 2

    def body(a_ref, b_ref, after_ref, o_ref):
        res = lax.dot_general(a_ref[...], b_ref[...], TN, preferred_element_type=F32)
        o_ref[0, 0] = res[:half]
        o_ref[1, 0] = res[half:]

    return pl.pallas_call(
        body, name=name, grid=(N_CHIPS,), in_specs=[a_spec, b_spec, pl.BlockSpec(memory_space=pl.ANY)],
        out_specs=pl.BlockSpec((2, 1, half, tn), lambda j: (0, j, 0, 0)),
        out_shape=jax.ShapeDtypeStruct((2, N_CHIPS, half, tn), F32),
        compiler_params=_params("parallel"),
    )(a, b, after)


def _mixer_bwd(dx1, y2, ycat, ya, yx, bcu, qx, mkv, conv_w, g_a, g_c, g_x, w_out, g_post, after, tm):
    S = dx1.shape[0]
    n_mem = mkv.shape[0]
    n_tiles = S // tm
    half = D_MODEL // N_CHIPS // 2

    def body(dx1_ref, y2_ref, ycat_ref, ya_ref, yx_ref, bcu_ref, before_ref, qx_ref, mkv_ref, cw_ref, ga_ref, gc_ref,
             gx_ref, wo_ref, gp_ref, after_ref, gwo_ref, dya_ref, delta_ref, tail_ref, dmkv_ref, dcw_ref, dgp_ref,
             dga_ref, dgc_ref, dgx_ref, carry):
        step = pl.program_id(0)
        first_tile = step == n_tiles - 1

        @pl.when(step == 0)
        def _():
            for ref in (gwo_ref, dmkv_ref, dcw_ref, dgp_ref, dga_ref, dgc_ref, dgx_ref, carry):
                ref[...] = jnp.zeros_like(ref)

        dx1 = dx1_ref[...]
        y2hat, r2 = _rms_hat(y2_ref[...])
        dgp_ref[...] += jnp.sum(dx1 * y2hat, axis=0, keepdims=True)
        dy2 = _rms_bwd(y2hat, r2, gp_ref[...], dx1).astype(BF16)
        gwo = lax.dot_general(ycat_ref[...], dy2, TN, preferred_element_type=F32)
        for k in range(2 * N_CHIPS):
            gwo_ref[k % 2, k // 2] += gwo[half * k:half * (k + 1)]
        dycat = lax.dot_general(dy2, wo_ref[...], NT, preferred_element_type=F32)

        d_na = dycat[:, 0:ATTN_W]
        ya = ya_ref[...]
        yahat, ra = _rms_hat(ya)
        dga_ref[...] += jnp.sum(d_na * yahat, axis=0, keepdims=True)
        dya = _rms_bwd(yahat, ra, ga_ref[...], d_na)
        dya_ref[...] = dya
        prod = dya * ya
        hi = prod.astype(BF16)
        lo = (prod - hi.astype(F32)).astype(BF16)
        head_of = lambda axis: lax.shift_right_logical(lax.broadcasted_iota(jnp.int32, (ATTN_W, ATTN_W), axis),
                                                       HEAD.bit_length() - 1)
        ones = jnp.where(head_of(0) == head_of(1), 1.0, 0.0).astype(BF16)
        delta_ref[...] = jnp.dot(hi, ones, preferred_element_type=F32) + jnp.dot(lo, ones, preferred_element_type=F32)

        w = cw_ref[...]
        b, c, u, z, z1, z2, cv = _conv_fwd(bcu_ref[...], before_ref[...], first_tile, w)
        d_nc = dycat[:, ATTN_W:ATTN_W + CONV_W]
        ychat, rc = _rms_hat(b * cv)
        dgc_ref[...] += jnp.sum(d_nc * ychat, axis=0, keepdims=True)
        dyc = _rms_bwd(ychat, rc, gc_ref[...], d_nc)
        dcv = dyc * b
        behind = carry[...]
        dz = w[2:3, :] * dcv + w[1:2, :] * _shift_up(dcv, behind, 1) + w[0:1, :] * _shift_up(dcv, behind, 2)
        carry[...] = dcv[0:8, :]
        dcw_ref[0:1, :] += jnp.sum(dcv * z2, axis=0, keepdims=True)
        dcw_ref[1:2, :] += jnp.sum(dcv * z1, axis=0, keepdims=True)
        dcw_ref[2:3, :] += jnp.sum(dcv * z, axis=0, keepdims=True)
        tail_ref[:, 0:CONV_W] = (dyc * cv).astype(BF16)
        tail_ref[:, CONV_W:2 * CONV_W] = (dz * u).astype(BF16)
        tail_ref[:, 2 * CONV_W:3 * CONV_W] = (dz * c).astype(BF16)

        d_nx = dycat[:, ATTN_W + CONV_W:D_MODEL]
        yxhat, rx = _rms_hat(yx_ref[...])
        dgx_ref[...] += jnp.sum(d_nx * yxhat, axis=0, keepdims=True)
        dyx = _rms_bwd(yxhat, rx, gx_ref[...], d_nx)
        qxb, mkvb = qx_ref[...], mkv_ref[...]
        heads = [slice(HEAD * hd, HEAD * (hd + 1)) for hd in range(XATTN_W // HEAD)]
        values = [slice(XATTN_W + sl.start, XATTN_W + sl.stop) for sl in heads]
        ss = [lax.dot_general(qxb[:, sl], mkvb[:, sl], NT, preferred_element_type=F32) * SCALE for sl in heads]
        es = [jnp.exp(s - jnp.max(s, axis=1, keepdims=True)) for s in ss]
        ps = [e / jnp.sum(e, axis=1, keepdims=True) for e in es]
        dobs = [dyx[:, sl].astype(BF16) for sl in heads]
        dps = [lax.dot_general(dob, mkvb[:, vsl], NT, preferred_element_type=F32) for dob, vsl in zip(dobs, values)]
        dss = [(p * (dp - jnp.sum(p * dp, axis=1, keepdims=True)) * SCALE).astype(BF16) for p, dp in zip(ps, dps)]
        for sl, vsl, p, dob, ds in zip(heads, values, ps, dobs, dss):
            tail_ref[:, 3 * CONV_W + sl.start:3 * CONV_W + sl.stop] = jnp.dot(
                ds, mkvb[:, sl], preferred_element_type=F32).astype(BF16)
            dmkv_ref[:, sl] += lax.dot_general(ds, qxb[:, sl], TN, preferred_element_type=F32)
            dmkv_ref[:, vsl] += lax.dot_general(p.astype(BF16), dob, TN, preferred_element_type=F32)

    rows = lambda width: pl.BlockSpec((tm, width), lambda i: (n_tiles - 1 - i, 0))
    before = pl.BlockSpec((8, 3 * CONV_W), lambda i: (jnp.maximum((n_tiles - 1 - i) * (tm // 8) - 1, 0), 0))
    acc = lambda r, w: pl.BlockSpec((r, w), lambda i: (0, 0))
    return pl.pallas_call(
        body, name="mixer_bwd", grid=(n_tiles,),
        in_specs=[rows(D_MODEL), rows(D_MODEL), rows(D_MODEL), rows(ATTN_W), rows(XATTN_W), rows(3 * CONV_W), before,
                  rows(XATTN_W), _resident((n_mem, 2 * XATTN_W)), _resident((3, CONV_W)), _resident((1, ATTN_W)),
                  _resident((1, CONV_W)), _resident((1, XATTN_W)), _resident((D_MODEL, D_MODEL)),
                  _resident((1, D_MODEL)), pl.BlockSpec(memory_space=pl.ANY)],
        out_specs=[pl.BlockSpec((2, N_CHIPS, half, D_MODEL), lambda i: (0, 0, 0, 0)), rows(ATTN_W), rows(ATTN_W),
                   rows(3 * CONV_W + XATTN_W), acc(n_mem, 2 * XATTN_W),
                   acc(3, CONV_W), acc(1, D_MODEL), acc(1, ATTN_W), acc(1, CONV_W), acc(1, XATTN_W)],
        out_shape=[jax.ShapeDtypeStruct((2, N_CHIPS, half, D_MODEL), F32), jax.ShapeDtypeStruct((S, ATTN_W), F32),
                   jax.ShapeDtypeStruct((S, ATTN_W), F32), jax.ShapeDtypeStruct((S, 3 * CONV_W + XATTN_W), BF16),
                   jax.ShapeDtypeStruct((n_mem, 2 * XATTN_W), F32), jax.ShapeDtypeStruct((3, CONV_W), F32),
                   jax.ShapeDtypeStruct((1, D_MODEL), F32), jax.ShapeDtypeStruct((1, ATTN_W), F32),
                   jax.ShapeDtypeStruct((1, CONV_W), F32), jax.ShapeDtypeStruct((1, XATTN_W), F32)],
        scratch_shapes=[pltpu.VMEM((8, CONV_W), F32)],
        compiler_params=_params("arbitrary"),
    )(dx1, y2, ycat, ya, yx, bcu, bcu, qx, mkv, conv_w, g_a, g_c, g_x, w_out, g_post, after)


def _memkv_bwd(mem, g_mem, w_kv, dmkv):
    n_mem = mem.shape[0]
    half = D_MODEL // N_CHIPS // 2

    def body(mem_ref, g_ref, w_ref, d_ref, dw_ref, dg_ref):
        mhat, _ = _rms_hat(mem_ref[...])
        mn = (mhat * g_ref[...]).astype(BF16)
        d = d_ref[...].astype(BF16)
        for k in range(2 * N_CHIPS):
            dw_ref[k % 2, k // 2] = lax.dot_general(mn[:, half * k:half * (k + 1)], d, TN, preferred_element_type=F32)
        dmn = lax.dot_general(d, w_ref[...], NT, preferred_element_type=F32)
        dg_ref[...] = jnp.sum(dmn * mhat, axis=0, keepdims=True)

    return pl.pallas_call(
        body, name="memkv_bwd",
        out_shape=[jax.ShapeDtypeStruct((2, N_CHIPS, half, 2 * XATTN_W), F32), jax.ShapeDtypeStruct((1, D_MODEL), F32)],
        compiler_params=pltpu.CompilerParams(vmem_limit_bytes=VMEM_LIMIT_V7X),
    )(mem, g_mem, w_kv, dmkv)


def _in_proj_bwd(dqkv, tail, cos, sin, w_in, x, h, g, dx1, after, tm):
    S = x.shape[0]
    step_w = 2 * 256
    half = D_MODEL // 2

    def body(dq_ref, dk_ref, dv_ref, tail_ref, cos_ref, sin_ref, w_hbm, x_ref, h_ref, g_ref, dx1_ref, after_ref,
             dx_ref, gw_ref, dg_ref, dproj_ref, w_full, sems):
        _side_by_side(w_hbm, w_full, sems)

        @pl.when(pl.program_id(0) == 0)
        def _():
            dg_ref[...] = jnp.zeros_like(dg_ref)
            gw_ref[...] = jnp.zeros_like(gw_ref)

        halves = [slice(0, tm // 2), slice(tm // 2, tm)]
        for rows in halves:
            c, s = cos_ref[rows, :], sin_ref[rows, :]
            for j in range(ATTN_W // 128):
                cols = slice(128 * j, 128 * (j + 1))
                dproj_ref[rows, cols] = _rope128(dq_ref[rows, cols] * SCALE, c, s, True).astype(BF16)
                dproj_ref[rows, ATTN_W + 128 * j:ATTN_W + 128 * (j + 1)] = _rope128(dk_ref[rows, cols], c, s, True).astype(BF16)
            dproj_ref[rows, 2 * ATTN_W:3 * ATTN_W] = dv_ref[rows, :].astype(BF16)
            dproj_ref[rows, 3 * ATTN_W:PROJ_W] = tail_ref[rows, :]
        dhs = [lax.dot_general(dproj_ref[rows, :], w_full[...], NT, preferred_element_type=F32) for rows in halves]
        for rows, dh in zip(halves, dhs):
            xhat, r = _rms_hat(x_ref[rows, :])
            dg_ref[...] += jnp.sum(dh * xhat, axis=0, keepdims=True)
            dx_ref[rows, :] = dx1_ref[rows, :] + _rms_bwd(xhat, r, g_ref[...], dh)
        hb = h_ref[...]
        for step in range(PROJ_W // step_w):
            res = lax.dot_general(hb, dproj_ref[:, step * step_w:(step + 1) * step_w], TN, preferred_element_type=F32)
            lo = step * step_w
            while lo < (step + 1) * step_w:
                chip = lo // SHARD_IN
                hi = min((step + 1) * step_w, (chip + 1) * SHARD_IN)
                for hh in range(2):
                    gw_ref[hh, chip, :, lo - chip * SHARD_IN:hi - chip * SHARD_IN] += (
                        res[half * hh:half * (hh + 1), lo - step * step_w:hi - step * step_w])
                lo = hi

    whole = lambda shape: pl.BlockSpec(shape, lambda i: (0,) * len(shape))
    return pl.pallas_call(
        body, name="in_proj_bwd", grid=(S // tm,),
        in_specs=[_rows(tm, ATTN_W)] * 3 + [_rows(tm, PROJ_W - 3 * ATTN_W), _rows(tm, 128), _rows(tm, 128),
                  pl.BlockSpec(memory_space=pl.ANY), _rows(tm, D_MODEL), _rows(tm, D_MODEL), _resident((1, D_MODEL)),
                  _rows(tm, D_MODEL), pl.BlockSpec(memory_space=pl.ANY)],
        out_specs=[_rows(tm, D_MODEL), whole((2, N_CHIPS, half, SHARD_IN)), whole((1, D_MODEL))],
        out_shape=[jax.ShapeDtypeStruct((S, D_MODEL), F32), jax.ShapeDtypeStruct((2, N_CHIPS, half, SHARD_IN), F32),
                   jax.ShapeDtypeStruct((1, D_MODEL), F32)],
        scratch_shapes=[pltpu.VMEM((tm, PROJ_W), BF16), pltpu.VMEM((D_MODEL, PROJ_W), BF16),
                        pltpu.SemaphoreType.DMA((N_CHIPS,))],
        compiler_params=_params("arbitrary"),
    )(*dqkv, tail, cos, sin, w_in, x, h, g, dx1, after)


def _row_tile(rows):
    return ROW_TILE if rows % ROW_TILE == 0 else rows


def _chip_sums_bf16(name, grads, from_sibling, place):
    k = len(grads)
    _, n, rows, _ = grads[0].shape
    tr = _row_tile(rows)

    def body(place_ref, *refs):
        for g_ref, b_ref, o_ref in zip(refs[:k], refs[k:2 * k], refs[2 * k:]):
            o_ref[...] = (g_ref[0] + b_ref[...]).astype(BF16)

    mine = lambda g: pl.BlockSpec((1, 1, tr, g.shape[3]), lambda s, i, p: (p[0], s, i, 0))
    slab = lambda g: pl.BlockSpec((1, tr, g.shape[3]), lambda s, i, p: (s, i, 0))
    return pl.pallas_call(
        body, name=name, out_shape=[jax.ShapeDtypeStruct(g.shape[1:], BF16) for g in grads],
        grid_spec=pltpu.PrefetchScalarGridSpec(
            num_scalar_prefetch=1, grid=(n, rows // tr),
            in_specs=[mine(g) for g in grads] + [slab(g) for g in grads], out_specs=[slab(g) for g in grads]),
        compiler_params=_params("parallel", "parallel"),
    )(place, *grads, *from_sibling)


def _final_sums(name, grads, from_sibling, others, place):
    k = len(grads)
    rows = grads[0].shape[2]
    tr = _row_tile(rows)

    def body(place_ref, *refs):
        for a in range(k):
            own_ref, sib_ref = refs[a], refs[k + a]
            acc = own_ref[0, 0] + sib_ref[0]
            for o in refs[2 * k + 3 * a:2 * k + 3 * a + 3]:
                acc = acc + o[0].astype(F32)
            refs[5 * k + a][0] = acc

    own = lambda g: pl.BlockSpec((1, 1, tr, g.shape[3]), lambda i, p: (p[0], p[1], i, 0))
    sib = lambda g: pl.BlockSpec((1, tr, g.shape[3]), lambda i, p: (p[1], i, 0))
    other = lambda g, j: pl.BlockSpec((1, tr, g.shape[3]), lambda i, p: (j, i, 0))
    return pl.pallas_call(
        body, name=name, out_shape=[jax.ShapeDtypeStruct((2,) + g.shape[2:], F32) for g in grads],
        grid_spec=pltpu.PrefetchScalarGridSpec(
            num_scalar_prefetch=1, grid=(rows // tr,),
            in_specs=[own(g) for g in grads] + [sib(g) for g in grads] + [other(g, j) for g in grads for j in range(3)],
            out_specs=[pl.BlockSpec((1, tr, g.shape[3]), lambda i, p: (p[0], i, 0)) for g in grads]),
        compiler_params=_params("parallel"),
    )(place, *grads, *from_sibling, *[o for o in others for _ in range(3)])


def _adamw_update(w, g, m, v):
    m = ADAM_B1 * m + (1.0 - ADAM_B1) * g
    v = ADAM_B2 * v + (1.0 - ADAM_B2) * (g * g)
    m_hat = m * (1.0 / (1.0 - ADAM_B1 ** ADAM_STEP))
    v_hat = v * (1.0 / (1.0 - ADAM_B2 ** ADAM_STEP))
    return -ADAM_LR * (m_hat / (jnp.sqrt(v_hat) + ADAM_EPS) + ADAM_WD * w), m, v


def _adamw(name, params, after):
    k = len(params)
    rows = params[0][0].shape[0]
    tr = ADAMW_ROW_TILE if rows % ADAMW_ROW_TILE == 0 else rows

    def body(*refs):
        ins, outs = refs[:4 * k], refs[4 * k + 1:]
        for a in range(k):
            w_ref, g_ref, m_ref, v_ref = ins[4 * a:4 * a + 4]
            g = g_ref[...]
            outs[4 * a][...] = g
            outs[4 * a + 1][...], outs[4 * a + 2][...], outs[4 * a + 3][...] = _adamw_update(w_ref[...], g, m_ref[...], v_ref[...])

    spec = lambda w: pl.BlockSpec((tr, w.shape[1]), lambda i: (i, 0))
    out = pl.pallas_call(
        body, name=name, grid=(rows // tr,),
        in_specs=[spec(p[0]) for p in params for _ in range(4)] + [pl.BlockSpec(memory_space=pl.ANY)],
        out_specs=[spec(p[0]) for p in params for _ in range(4)],
        out_shape=[jax.ShapeDtypeStruct(p[0].shape, F32) for p in params for _ in range(4)],
        compiler_params=_params("parallel"),
    )(*[t for p in params for t in p], after)
    return [out[4 * a:4 * a + 4] for a in range(k)]


def _small_update(summed, chip, gains, gains_m, gains_v, taps, taps_m, taps_v):
    n = len(gains)
    widths = [g.shape[1] for g in gains]
    k, w = taps.shape

    def body(*refs):
        chip_ref, sum_ref = refs[0], refs[1]
        params = [refs[2 + 3 * i:5 + 3 * i] for i in range(n + 1)]
        outs = [refs[2 + 3 * (n + 1) + 4 * i:2 + 3 * (n + 1) + 4 * (i + 1)] for i in range(n + 1)]
        loss_ref = refs[-1]
        for i in range(n):
            g = sum_ref[i:i + 1, 0:widths[i]]
            wr, mr, vr = params[i]
            outs[i][0][...] = g
            outs[i][1][...], outs[i][2][...], outs[i][3][...] = _adamw_update(wr[...], g, mr[...], vr[...])
        g = sum_ref[n:n + k, 0:w]
        for j in range(1, N_CHIPS):
            g = jnp.where(chip_ref[0] == j, sum_ref[n:n + k, w * j:w * (j + 1)], g)
        wr, mr, vr = params[n]
        outs[n][0][...] = g
        outs[n][1][...], outs[n][2][...], outs[n][3][...] = _adamw_update(wr[...], g, mr[...], vr[...])
        loss_ref[...] = sum_ref[n + k:n + k + 1, 0:1]

    vmem = pl.BlockSpec(memory_space=pltpu.VMEM)
    operands = [chip, summed]
    for p in zip(list(gains) + [taps], list(gains_m) + [taps_m], list(gains_v) + [taps_v]):
        operands += list(p)
    shapes = [jax.ShapeDtypeStruct(p.shape, F32) for p in list(gains) + [taps] for _ in range(4)]
    out = pl.pallas_call(
        body, name="small_update", out_shape=shapes + [jax.ShapeDtypeStruct((1, 1), F32)],
        in_specs=[pl.BlockSpec(memory_space=pltpu.SMEM)] + [vmem] * (len(operands) - 1),
        out_specs=[vmem] * (len(shapes) + 1),
    )(*operands)
    return [out[4 * i:4 * (i + 1)] for i in range(n + 1)], out[-1]


def _sum_blocks(name, blocks):
    n, rows, cols = blocks.shape

    def body(b_ref, o_ref):
        acc = b_ref[0]
        for k in range(1, n):
            acc = acc + b_ref[k]
        o_ref[...] = acc

    return pl.pallas_call(body, name=name, out_shape=jax.ShapeDtypeStruct((rows, cols), F32))(blocks)


def _place():
    return lax.axis_index("x"), lax.axis_index("y"), lax.axis_index("c")


def _other_chips(x, y):
    return [(1 - x, y), (x, 1 - y), (1 - x, 1 - y)]


def _allgather_finish(name, shards, landed, pass_on):
    n = len(shards)

    def body(*refs):
        ins, outs, stage = refs[:n], refs[2 * n:3 * n], refs[3 * n:4 * n]
        send_sems, recv_sems, local_sems = refs[4 * n:]
        x, y, c = _place()
        chips = _other_chips(x, y)
        barrier = pltpu.get_barrier_semaphore()
        pl.semaphore_signal(barrier, inc=1, device_id=(x, y, 1 - c), device_id_type=MESH)
        pl.semaphore_wait(barrier, 1)

        def copy(a, k, chip, half):
            place = outs[a].at[2 * chip[0] + chip[1], half]
            return pltpu.make_async_remote_copy(
                src_ref=place, dst_ref=place, send_sem=send_sems.at[3 * a + k], recv_sem=recv_sems.at[3 * a + k],
                device_id=(x, y, 1 - c), device_id_type=MESH)

        load = [pltpu.make_async_copy(ins[a], stage[a], local_sems.at[a]) for a in range(n)]
        local = [pltpu.make_async_copy(stage[a], outs[a].at[2 * x + y], local_sems.at[a]) for a in range(n)]
        for cp in load:
            cp.start()
        passed = [copy(a, k, chip, c) for a in range(n) if pass_on[a] for k, chip in enumerate(chips)]
        for cp in passed:
            cp.start()
        for a in range(n):
            load[a].wait()
            local[a].start()
        for a in range(n):
            if pass_on[a]:
                for k, chip in enumerate(chips):
                    copy(a, k, chip, 1 - c).wait_recv()
        for cp in passed:
            cp.wait_send()
        for cp in local:
            cp.wait()

    any_spec = pl.BlockSpec(memory_space=pl.ANY)
    return pl.pallas_call(
        body, name=name,
        out_shape=[jax.ShapeDtypeStruct((N_CHIPS,) + s.shape, s.dtype) for s in shards],
        in_specs=[any_spec] * (2 * n), out_specs=[any_spec] * n,
        input_output_aliases={n + a: a for a in range(n)},
        scratch_shapes=[pltpu.VMEM(s.shape, s.dtype) for s in shards]
        + [pltpu.SemaphoreType.DMA((3 * n,)), pltpu.SemaphoreType.DMA((3 * n,)), pltpu.SemaphoreType.DMA((n,))],
        compiler_params=pltpu.CompilerParams(vmem_limit_bytes=VMEM_LIMIT_V7X, collective_id=HANDSHAKES["sibling"][0]),
    )(*shards, *landed)


def _plan_first_hop(x, y, c, shards, lands):
    return [(shards[a].at[c], lands[a].at[2 * x + y, c], lands[a].at[2 * chip[0] + chip[1], c], (*chip, c))
            for a in range(len(shards)) for chip in _other_chips(x, y)]


def _plan_pass_on(x, y, c, nothing, lands):
    def place(a, chip, half):
        return lands[a].at[2 * chip[0] + chip[1], half]

    return [(place(a, chip, c), place(a, chip, c), place(a, chip, 1 - c), (x, y, 1 - c))
            for a in range(len(lands)) for chip in _other_chips(x, y)]


def _plan_own_half_to_sibling(x, y, c, nothing, lands):
    return [(lands[a].at[c], lands[a].at[c], lands[a].at[1 - c], (x, y, 1 - c)) for a in range(len(lands))]


def _plan_other_half_to_sibling(x, y, c, grads, lands):
    return [(grads[a].at[1 - c], lands[a], lands[a], (x, y, 1 - c)) for a in range(len(grads))]


def _plan_to_other_chips(x, y, c, partials, lands):
    return [(partials[a].at[2 * chip[0] + chip[1]], lands[a].at[k], lands[a].at[k], (*chip, c))
            for a in range(len(partials)) for k, chip in enumerate(_other_chips(x, y))]


def _plan_to_all(x, y, c, blocks, lands):
    flips = [(fx, fy, fc) for fx in (0, 1) for fy in (0, 1) for fc in (0, 1) if (fx, fy, fc) != (0, 0, 0)]
    peers = [(1 - x if fx else x, 1 - y if fy else y, 1 - c if fc else c) for fx, fy, fc in flips]
    return [(blocks[0], lands[0].at[4 * x + 2 * y + c], lands[0].at[4 * p[0] + 2 * p[1] + p[2]], p) for p in peers]


def _planned_copies(plan, srcs, lands, send_sems, recv_sems):
    x, y, c = _place()

    def pair(k, src, there, here, to):
        make = lambda dst: pltpu.make_async_remote_copy(
            src_ref=src, dst_ref=dst, send_sem=send_sems.at[k], recv_sem=recv_sems.at[k], device_id=to, device_id_type=MESH)
        return make(there), make(here)

    return [pair(k, *entry) for k, entry in enumerate(plan(x, y, c, srcs, lands))]


_HBM_SPEC = pl.BlockSpec(memory_space=pltpu.HBM)
_SEM_SPEC = pl.BlockSpec(memory_space=pltpu.SEMAPHORE)


def _hbm(a):
    return pltpu.with_memory_space_constraint(a, pltpu.HBM)


HANDSHAKES = {
    "sibling": (1, lambda x, y, c: [(x, y, 1 - c)]),
}


def _exchange_start(name, plan, n_copies, srcs, land_shapes, after, lands=None, peers=None):
    if lands is None:
        lands = [lax.empty(s.shape, s.dtype) for s in land_shapes]
    land_shapes = lands
    ns, nl = len(srcs), len(land_shapes)
    n_in = ns + nl + 1
    collective_id, peers_of = HANDSHAKES[peers] if peers else (None, None)

    def body(*refs):
        if peers:
            who = peers_of(*_place())
            barrier = pltpu.get_barrier_semaphore()
            for peer in who:
                pl.semaphore_signal(barrier, inc=1, device_id=peer, device_id_type=MESH)
            pl.semaphore_wait(barrier, len(who))
        for send, _ in _planned_copies(plan, refs[:ns], refs[ns:ns + nl], refs[n_in], refs[n_in + 1]):
            send.start()
        refs[-1][...] = jnp.zeros_like(refs[-1])

    out = pl.pallas_call(
        body, name=name,
        out_shape=(pltpu.SemaphoreType.DMA((n_copies,)), pltpu.SemaphoreType.DMA((n_copies,)),
                   *[pltpu.HBM(s.shape, s.dtype) for s in land_shapes], jax.ShapeDtypeStruct((8, 128), F32)),
        in_specs=[_HBM_SPEC] * (ns + nl) + [pl.BlockSpec(memory_space=pl.ANY)],
        out_specs=(_SEM_SPEC, _SEM_SPEC, *[_HBM_SPEC] * nl, pl.BlockSpec(memory_space=pltpu.VMEM)),
        input_output_aliases={ns + i: 2 + i for i in range(nl)},
        compiler_params=pltpu.CompilerParams(has_side_effects=pltpu.SideEffectType.DATAFLOW_SIDE_EFFECTING,
                                             collective_id=collective_id),
    )(*[_hbm(s) for s in srcs], *[_hbm(l) for l in lands], after)
    return out[0], out[1], list(out[2:2 + nl]), out[-1]


def _exchange_wait(name, plan, srcs, started, after):
    send_sems, recv_sems, lands, _ = started
    ns, nl = len(srcs), len(lands)
    after = list(after) if isinstance(after, (list, tuple)) else [after]

    def body(*refs):
        for send, recv in _planned_copies(plan, refs[:ns], refs[ns:ns + nl], refs[ns + nl], refs[ns + nl + 1]):
            send.wait_send()
            recv.wait_recv()

    return pl.pallas_call(
        body, name=name, out_shape=[pltpu.HBM(l.shape, l.dtype) for l in lands],
        in_specs=[_HBM_SPEC] * (ns + nl) + [_SEM_SPEC, _SEM_SPEC] + [pl.BlockSpec(memory_space=pl.ANY)] * len(after),
        out_specs=[_HBM_SPEC] * nl, input_output_aliases={ns + i: i for i in range(nl)},
        compiler_params=pltpu.CompilerParams(has_side_effects=pltpu.SideEffectType.DATAFLOW_SIDE_EFFECTING),
    )(*[_hbm(s) for s in srcs], *lands, send_sems, recv_sems, *after)


def _like(arrays, lead, dtype=None):
    return [jax.ShapeDtypeStruct(tuple(lead) + a.shape[-2:], dtype or a.dtype) for a in arrays]


class _StepExchanges:
    def __init__(self, mats, conv_w):
        x, y, c = _place()
        self.place = jnp.stack([c, 2 * x + y]).astype(jnp.int32)
        shards = [w.astype(BF16).reshape(2, w.shape[0] // 2, w.shape[1]) for w in mats]
        self._in_shard = shards[:1]
        self._in = _exchange_start("w_in_allgather_start", _plan_first_hop, 3, self._in_shard,
                                   _like(self._in_shard, (N_CHIPS, 2)), shards[0])
        self.zero = self._in[3]
        taps = jnp.pad(conv_w, ((0, 8 - conv_w.shape[0]), (0, 128 - conv_w.shape[1])))
        self._rest_shards = shards[1:] + [jnp.stack([taps, jnp.zeros_like(taps)])]
        self._taps_shape = conv_w.shape
        self._groups = {}

    def w_in(self, after):
        landed = _exchange_wait("w_in_allgather_wait", _plan_first_hop, self._in_shard, self._in,
                                list(after) + self._rest_shards)
        (w_in,) = _allgather_finish("w_in_allgather_finish", self._in_shard, landed, [True])
        self._rest = _exchange_start("rest_allgather_start", _plan_first_hop, 3 * len(self._rest_shards),
                                     self._rest_shards, _like(self._rest_shards, (N_CHIPS, 2)), w_in)
        self.zero = self._rest[3]
        return w_in.reshape(N_CHIPS, 2 * w_in.shape[2], w_in.shape[3])

    def rest_weights(self, after):
        landed = _exchange_wait("rest_allgather_wait", _plan_first_hop, self._rest_shards, self._rest, after)
        kv, out, up, down, taps = _allgather_finish("rest_allgather_finish", self._rest_shards, landed,
                                                    [True, True, False, False, True])
        self._up_down = _exchange_start("up_down_pass_on_start", _plan_pass_on, 6, [], None, self.zero, lands=[up, down],
                                        peers="sibling")
        self.zero = self._up_down[3]
        k, w = self._taps_shape
        taps = taps[:, 0, :k, :w].transpose(1, 0, 2).reshape(k, N_CHIPS * w)
        return [g.reshape(N_CHIPS, 2 * g.shape[2], g.shape[3]) for g in (kv, out)], taps

    def up_down(self, after):
        full = _exchange_wait("up_down_pass_on_wait", _plan_pass_on, [], self._up_down, after)
        return [g.reshape(N_CHIPS, 2 * g.shape[2], g.shape[3]) for g in full]

    def send_grads(self, key, grads):
        grads = list(grads)
        started = _exchange_start(f"{key}_grads_to_sibling_start", _plan_other_half_to_sibling, len(grads), grads,
                                  _like(grads, (N_CHIPS,)), self.zero, peers="sibling")
        self._groups[key] = dict(grads=grads, to_sibling=started)
        self.zero = started[3]

    def grads_at_sibling(self, key, after):
        group = self._groups[key]
        grads = group["grads"]
        group["from_sibling"] = _exchange_wait(f"{key}_grads_to_sibling_wait", _plan_other_half_to_sibling, grads,
                                               group["to_sibling"], after)
        group["partials"] = _chip_sums_bf16(f"{key}_chip_sums", grads, group["from_sibling"], self.place)
        group["to_chips"] = _exchange_start(f"{key}_grads_to_chips_start", _plan_to_other_chips, 3 * len(grads),
                                            group["partials"], _like(group["partials"], (3,)), self.zero)
        self.zero = group["to_chips"][3]

    def grads_summed(self, key, after):
        group = self._groups[key]
        from_chips = _exchange_wait(f"{key}_grads_to_chips_wait", _plan_to_other_chips, group["partials"],
                                    group["to_chips"], after)
        return _final_sums(f"{key}_final_sums", group["grads"], group["from_sibling"], from_chips, self.place)

    def send_sums(self, key, sums):
        self._groups[key + "_sums"] = _exchange_start(f"{key}_sums_to_sibling_start", _plan_own_half_to_sibling,
                                                      len(sums), [], None, self.zero, lands=list(sums),
                                                      peers="sibling")
        self.zero = self._groups[key + "_sums"][3]

    def whole_sums(self, key, after):
        full = _exchange_wait(f"{key}_sums_to_sibling_wait", _plan_own_half_to_sibling, [], self._groups[key + "_sums"], after)
        return [t.reshape(2 * t.shape[1], t.shape[2]) for t in full]

    def send_small(self, block):
        self._small = block
        self._small_started = _exchange_start("small_grads_start", _plan_to_all, 7, [block],
                                              [jax.ShapeDtypeStruct((8,) + block.shape, block.dtype)], self.zero)
        self.zero = self._small_started[3]

    def small_summed(self, after):
        x, y, c = _place()
        (landed,) = _exchange_wait("small_grads_wait", _plan_to_all, [self._small], self._small_started, after)
        blocks = lax.dynamic_update_index_in_dim(landed, self._small, 4 * x + 2 * y + c, 0)
        return _sum_blocks("small_sum", blocks)


def _rope_tables(positions):
    half = HEAD // 2
    inv_freq = jnp.float32(ROPE_THETA) ** (-(jnp.arange(half, dtype=F32) * 2.0 / HEAD))
    ang = positions.astype(F32)[:, None] * inv_freq
    cos, sin = jnp.cos(ang), jnp.sin(ang)
    return jnp.tile(cos, (1, 4)), jnp.tile(jnp.concatenate([-sin, sin], axis=1), (1, 2))


def _local_step(x, mem, positions, target, gains, ex):
    g_pre_mix, g_mem, g_a, g_c, g_x, g_post_mix, g_pre_mlp, g_post_mlp = gains
    tm = ROW_TILE
    cos, sin = _rope_tables(positions)
    h = _pre_norm(x, g_pre_mix, ex.zero, tm)
    w_in = ex.w_in([h, cos, sin])

    q, k, v, bcu, qx = _in_proj_fwd(h, w_in, cos, sin, ex.zero, tm)
    ya, lse = _attn_fwd(q, k, v)
    (w_kv, w_out), conv_w = ex.rest_weights(lse)
    w_kv, w_out = (w.reshape(N_CHIPS * w.shape[1], w.shape[2]) for w in (w_kv, w_out))
    memn, mkv = _memkv_fwd(mem, g_mem, w_kv, ex.zero)
    yx, ycat, y2, x1 = _mix_fwd(ya, bcu, qx, mkv, conv_w, g_a, g_c, g_x, w_out, g_post_mix, x, tm)
    w_up, w_down = ex.up_down(x1)
    w_down = w_down.reshape(N_CHIPS * w_down.shape[1], w_down.shape[2])
    h2, f, du, df2, dx1, dg_pre_mlp, dg_post_mlp, loss = _mlp_fwd_bwd(x1, target, g_pre_mlp, g_post_mlp, w_up, w_down,
                                                                      MLP_ROW_TILE)
    gw_down = _weight_grad("grad_w_down", f, df2, True, ex.zero)
    gw_up = _weight_grad("grad_w_up", h2, du, False, ex.zero)
    ex.send_grads("early", [gw_up, gw_down])

    gw_out, dya, delta, tail, dmkv, g_conv, dg_post_mix, dg_a, dg_c, dg_x = _mixer_bwd(
        dx1, y2, ycat, ya, yx, bcu, qx, mkv, conv_w, g_a, g_c, g_x, w_out, g_post_mix, ex.zero, tm)
    ex.grads_at_sibling("early", dya)
    gw_kv, dg_mem = _memkv_bwd(mem, g_mem, w_kv, dmkv)
    ex.send_grads("mid", [gw_out, gw_kv])
    dqkv = _attn_bwd(q, k, v, dya, lse, delta, ex.zero)
    ex.grads_at_sibling("mid", dqkv[0])
    grad_x, gw_in, dg_pre_mix = _in_proj_bwd(dqkv, tail, cos, sin, w_in, x, h, g_pre_mix, dx1, ex.zero, tm)
    gain_grads = [dg_pre_mix, dg_mem, dg_a, dg_c, dg_x, dg_post_mix, dg_pre_mlp, dg_post_mlp]
    ex.send_small(_pack_small(gain_grads, g_conv, loss))
    ex.send_grads("late", [gw_in])
    return grad_x


def _pack_small(gains, conv, scalar=None):
    rows = [jnp.pad(g, ((0, 0), (0, D_MODEL - g.shape[1]))) for g in gains]
    rows.append(jnp.pad(conv, ((0, 0), (0, D_MODEL - conv.shape[1]))))
    last = jnp.zeros((SMALL_ROWS - 8 - conv.shape[0], D_MODEL), F32)
    rows.append(last if scalar is None else last.at[0:1, 0:1].set(scalar))
    return jnp.concatenate(rows, axis=0)


def kernel(x, mem, positions, g_pre_mix, g_mem, w_in, w_mem_kv, conv_w, g_attn_out, g_conv_out, g_xattn_out, w_out, g_post_mix, g_pre_mlp, w_up, w_down, g_post_mlp, loss_target, m_g_pre_mix, m_g_mem, m_w_in, m_w_mem_kv, m_conv_w, m_g_attn_out, m_g_conv_out, m_g_xattn_out, m_w_out, m_g_post_mix, m_g_pre_mlp, m_w_up, m_w_down, m_g_post_mlp, v_g_pre_mix, v_g_mem, v_w_in, v_w_mem_kv, v_conv_w, v_g_attn_out, v_g_conv_out, v_g_xattn_out, v_w_out, v_g_post_mix, v_g_pre_mlp, v_w_up, v_w_down, v_g_post_mlp):
    chip = 2 * lax.axis_index("x") + lax.axis_index("y")
    gains = [g_pre_mix, g_mem, g_attn_out, g_conv_out, g_xattn_out, g_post_mix, g_pre_mlp, g_post_mlp]
    gains_m = [m_g_pre_mix, m_g_mem, m_g_attn_out, m_g_conv_out, m_g_xattn_out, m_g_post_mix, m_g_pre_mlp, m_g_post_mlp]
    gains_v = [v_g_pre_mix, v_g_mem, v_g_attn_out, v_g_conv_out, v_g_xattn_out, v_g_post_mix, v_g_pre_mlp, v_g_post_mlp]
    mats =[w_in[0], w_mem_kv[0], w_out[0], w_up[0], w_down[0]]
    mats_m = [m_w_in[0], m_w_mem_kv[0], m_w_out[0], m_w_up[0], m_w_down[0]]
    mats_v = [v_w_in[0], v_w_mem_kv[0], v_w_out[0], v_w_up[0], v_w_down[0]]

    ex = _StepExchanges(mats, conv_w[0])
    grad_x = _local_step(x[0], mem[0], positions[0], loss_target[0], gains, ex)

    ex.send_sums("four", ex.grads_summed("early", ex.zero) + ex.grads_summed("mid", ex.zero))
    ex.grads_at_sibling("late", ex.zero)
    up_sum, down_sum, out_sum, kv_sum = ex.whole_sums("four", ex.zero)
    params = lambda a, g: (mats[a], g, mats_m[a], mats_v[a])
    new_up, new_down = _adamw("adamw_up_down", [params(3, up_sum), params(4, down_sum)], ex.zero)
    new_out, new_kv = _adamw("adamw_out_kv", [params(2, out_sum), params(1, kv_sum)], ex.zero)

    small, total = _small_update(ex.small_summed(new_kv[1]), chip.reshape(1).astype(jnp.int32), gains, gains_m,
                                 gains_v, conv_w[0], m_conv_w[0], v_conv_w[0])

    ex.send_sums("last", ex.grads_summed("late", small[0][1]))
    (in_sum,) = ex.whole_sums("last", ex.zero)
    (new_in,) = _adamw("adamw_in", [params(0, in_sum)], in_sum)
    mat_new = [new_in, new_kv, new_out, new_up, new_down]

    order = ["g_pre_mix", "g_mem", "w_in", "w_mem_kv", "conv_w", "g_attn_out", "g_conv_out", "g_xattn_out", "w_out",
             "g_post_mix", "g_pre_mlp", "w_up", "w_down", "g_post_mlp"]
    gain_names = ["g_pre_mix", "g_mem", "g_attn_out", "g_conv_out", "g_xattn_out", "g_post_mix", "g_pre_mlp", "g_post_mlp"]
    mat_names = ["w_in", "w_mem_kv", "w_out", "w_up", "w_down"]

    def leaf(kind, name):
        if name in gain_names:
            return small[gain_names.index(name)][kind]
        if name == "conv_w":
            return small[len(gain_names)][kind][None]
        return mat_new[mat_names.index(name)][kind][None]

    return (total[0, 0], grad_x[None], *[leaf(kind, name) for kind in range(4) for name in order])
```

```python
import jax
import jax.numpy as jnp
from jax import lax
from jax.experimental import pallas as pl
from jax.experimental.pallas import tpu as pltpu

F32, BF16 = jnp.float32, jnp.bfloat16

D_MODEL = 1024
ATTN_W = 512
CONV_W = 256
XATTN_W = 256
PROJ_W = 3 * ATTN_W + 3 * CONV_W + XATTN_W
D_FF = 4096
HEAD = 64
N_BACK = 128
DILATIONS = (1, 4, 16)
PATTERN_ORDER = DILATIONS[::-1]
ROPE_THETA = 10000.0
EPS = 1e-6
NEG_INF = -1e30
SCALE = HEAD ** -0.5
N_CHIPS = 4
SHARD_IN = PROJ_W // N_CHIPS
SHARD_FF = D_FF // N_CHIPS

ADAM_LR, ADAM_B1, ADAM_B2, ADAM_EPS, ADAM_WD, ADAM_STEP = 0.001, 0.9, 0.999, 1e-08, 0.01, 10

VMEM_LIMIT_V7X = 56 * 1024 * 1024
ROW_TILE = 512
MLP_ROW_TILE = 256
ADAMW_ROW_TILE = 256
SMALL_ROWS = 16

NT = (((1,), (1,)), ((), ()))
TN = (((0,), (0,)), ((), ()))
MESH = pl.DeviceIdType.MESH


def _params(*sem):
    return pltpu.CompilerParams(dimension_semantics=sem, vmem_limit_bytes=VMEM_LIMIT_V7X)


def _resident(shape):
    return pl.BlockSpec(shape, lambda *_: (0,) * len(shape), pipeline_mode=pl.Buffered(1))


def _rows(tm, width):
    return pl.BlockSpec((tm, width), lambda i: (i, 0))


def _rms_hat(x):
    r = lax.rsqrt(jnp.mean(x * x, axis=-1, keepdims=True) + EPS)
    return x * r, r


def _rms_bwd(xhat, r, g, dy):
    gdy = dy * g
    return r * (gdy - xhat * jnp.mean(xhat * gdy, axis=-1, keepdims=True))


def _rope128(t, cos, sin_signed, inverse):
    lane = lax.broadcasted_iota(jnp.int32, t.shape, 1)
    first_half = (lane % HEAD) < (HEAD // 2)
    rot = jnp.where(first_half, pltpu.roll(t, 128 - HEAD // 2, 1), pltpu.roll(t, HEAD // 2, 1))
    return t * cos - rot * sin_signed if inverse else t * cos + rot * sin_signed


def _pre_norm(x, g, after, tm):
    S = x.shape[0]

    def body(x_ref, g_ref, after_ref, h_ref):
        h_ref[...] = (_rms_hat(x_ref[...])[0] * g_ref[...]).astype(BF16)

    return pl.pallas_call(
        body, name="pre_norm", grid=(S // tm,),
        in_specs=[_rows(tm, D_MODEL), _resident((1, D_MODEL)), pl.BlockSpec(memory_space=pl.ANY)],
        out_specs=_rows(tm, D_MODEL), out_shape=jax.ShapeDtypeStruct((S, D_MODEL), BF16),
        compiler_params=_params("parallel"),
    )(x, g, after)


def _side_by_side(w_hbm, w_full, sems):
    width = w_hbm.shape[2]

    @pl.when(pl.program_id(0) == 0)
    def _():
        copies = [pltpu.make_async_copy(w_hbm.at[j], w_full.at[:, pl.ds(width * j, width)], sems.at[j])
                  for j in range(N_CHIPS)]
        for cp in copies:
            cp.start()
        for cp in copies:
            cp.wait()


def _in_proj_fwd(h, w_in, cos, sin, after, tm):
    S = h.shape[0]

    def body(h_ref, w_hbm, cos_ref, sin_ref, after_ref, q_ref, k_ref, v_ref, bcu_ref, qx_ref, proj, w_full, sems):
        _side_by_side(w_hbm, w_full, sems)
        proj[...] = jnp.dot(h_ref[...], w_full[...], preferred_element_type=F32)
        c, s = cos_ref[...], sin_ref[...]
        for j in range(ATTN_W // 128):
            lo = 128 * j
            q_ref[:, lo:lo + 128] = _rope128(proj[:, lo:lo + 128], c, s, False) * SCALE
            k_ref[:, lo:lo + 128] = _rope128(proj[:, ATTN_W + lo:ATTN_W + lo + 128], c, s, False)
        v_ref[...] = proj[:, 2 * ATTN_W:3 * ATTN_W]
        bcu_ref[...] = proj[:, 3 * ATTN_W:3 * ATTN_W + 3 * CONV_W]
        qx_ref[...] = proj[:, 3 * ATTN_W + 3 * CONV_W:PROJ_W].astype(BF16)

    return pl.pallas_call(
        body, name="in_proj_fwd", grid=(S // tm,),
        in_specs=[_rows(tm, D_MODEL), pl.BlockSpec(memory_space=pl.ANY), _rows(tm, 128), _rows(tm, 128),
                  pl.BlockSpec(memory_space=pl.ANY)],
        out_specs=[_rows(tm, ATTN_W), _rows(tm, ATTN_W), _rows(tm, ATTN_W), _rows(tm, 3 * CONV_W), _rows(tm, XATTN_W)],
        out_shape=[jax.ShapeDtypeStruct((S, ATTN_W), F32), jax.ShapeDtypeStruct((S, ATTN_W), F32),
                   jax.ShapeDtypeStruct((S, ATTN_W), F32), jax.ShapeDtypeStruct((S, 3 * CONV_W), F32),
                   jax.ShapeDtypeStruct((S, XATTN_W), BF16)],
        scratch_shapes=[pltpu.VMEM((tm, PROJ_W), F32), pltpu.VMEM((D_MODEL, PROJ_W), BF16),
                        pltpu.SemaphoreType.DMA((N_CHIPS,))],
        compiler_params=_params("arbitrary"),
    )(h, w_in, cos, sin, after)


def _memkv_fwd(mem, g_mem, w_kv, after):
    n_mem = mem.shape[0]

    def body(mem_ref, g_ref, w_ref, after_ref, mn_ref, kv_ref):
        mhat, _ = _rms_hat(mem_ref[...])
        mn = (mhat * g_ref[...]).astype(BF16)
        mn_ref[...] = mn
        kv_ref[...] = jnp.dot(mn, w_ref[...], preferred_element_type=F32).astype(BF16)

    vmem = pl.BlockSpec(memory_space=pltpu.VMEM)
    return pl.pallas_call(
        body, name="memkv_fwd", in_specs=[vmem, vmem, vmem, pl.BlockSpec(memory_space=pl.ANY)], out_specs=[vmem, vmem],
        out_shape=[jax.ShapeDtypeStruct((n_mem, D_MODEL), BF16), jax.ShapeDtypeStruct((n_mem, 2 * XATTN_W), BF16)],
        compiler_params=pltpu.CompilerParams(vmem_limit_bytes=VMEM_LIMIT_V7X),
    )(mem, g_mem, w_kv, after)


def _fill_band_bias(bias):
    row = lax.broadcasted_iota(jnp.int32, (N_BACK, 2 * N_BACK), 0)
    col = lax.broadcasted_iota(jnp.int32, (N_BACK, 2 * N_BACK), 1)
    band = (col >= row) & (col <= row + N_BACK)
    bias[1] = jnp.where(band, 0.0, NEG_INF)
    bias[0] = jnp.where(band & (col >= N_BACK), 0.0, NEG_INF)


def _strided(start, size, d):
    return pl.ds(start, size) if d == 1 else pl.ds(start, size, stride=d)


def _group_starts(g, G, nb, d):
    t0 = g * G
    r, n0 = lax.shift_right_logical(t0, nb.bit_length() - 1), lax.bitwise_and(t0, nb - 1)
    first = r + n0 * (N_BACK * d)
    before = r + jnp.maximum(n0 - 1, 0) * (N_BACK * d)
    starts = [before] + [first + u * (N_BACK * d) for u in range(G)]
    if d == 1:
        starts = [pl.multiple_of(st, N_BACK) for st in starts]
    return starts, n0


def _step_blocks(i, U, nb, d):
    G = min(U, nb)
    whole = G == nb
    row_blocks, blocks = [], []
    for grp in range(U // G):
        starts, n0 = _group_starts(i * (U // G) + grp, G, nb, d)
        base = len(row_blocks)
        if whole:
            row_blocks += [_strided(st, N_BACK, d) for st in starts[1:]]
            blocks += [(base + max(u - 1, 0), base + u, min(u, 1)) for u in range(G)]
        else:
            row_blocks += [_strided(st, N_BACK, d) for st in starts]
            blocks += [(base + u, base + u + 1, jnp.minimum(n0, 1) if u == 0 else 1) for u in range(G)]
    return row_blocks, blocks


def _by_head(a, b):
    lane = lax.broadcasted_iota(jnp.int32, (a.shape[0], 2 * HEAD), 1)
    return jnp.where(lane < HEAD, a, b)


def _head_only(t, hh):
    lane = lax.broadcasted_iota(jnp.int32, t.shape, 1)
    return jnp.where((lane < HEAD) == (hh == 0), t, jnp.zeros_like(t))


def _stack_heads(t):
    return jnp.concatenate([_head_only(t, 0), _head_only(t, 1)], axis=0)


def _head_columns(t):
    return jnp.concatenate([t[:, 0:1], t[:, HEAD:HEAD + 1]], axis=0)


def _unstack(t):
    return _by_head(t[:N_BACK], t[N_BACK:])


def _unstack_columns(t):
    return _by_head(jnp.broadcast_to(t[:N_BACK], (N_BACK, 2 * HEAD)), jnp.broadcast_to(t[N_BACK:], (N_BACK, 2 * HEAD)))


FWD_BLOCKS_PER_STEP = 4
BWD_BLOCKS_PER_STEP = 4
BWD_CHUNK = 64


def _attn_fwd(q, k, v):
    S = q.shape[0]
    U = FWD_BLOCKS_PER_STEP

    def body(q_ref, k_ref, v_ref, y_ref, m_ref, l_scr, bias):
        _fill_band_bias(bias)
        for g, d in enumerate(PATTERN_ORDER):
            nb = S // d // N_BACK
            first_pattern, last_pattern = g == 0, g == len(PATTERN_ORDER) - 1

            def step(i, carry, d=d, nb=nb, first_pattern=first_pattern, last_pattern=last_pattern):
                row_blocks, blocks = _step_blocks(i, U, nb, d)
                kb = [k_ref[r, :].astype(BF16) for r in row_blocks]
                ss = []
                for before, own, which in blocks:
                    kw = jnp.concatenate([kb[before], kb[own]], 0)
                    qs = _stack_heads(q_ref[row_blocks[own], :].astype(BF16))
                    b = bias[which]
                    ss.append(lax.dot_general(qs, kw, NT, preferred_element_type=F32) + jnp.concatenate([b, b], axis=0))
                ms = [jnp.max(s, axis=1, keepdims=True) for s in ss]
                ps = [jnp.exp(s - m) for s, m in zip(ss, ms)]
                ls = [jnp.sum(p, axis=1, keepdims=True) for p in ps]
                vb = [v_ref[r, :].astype(BF16) for r in row_blocks]
                os_ = [jnp.dot(ps[u].astype(BF16), jnp.concatenate([vb[before], vb[own]], 0), preferred_element_type=F32)
                       for u, (before, own, _) in enumerate(blocks)]
                for u, (_, own, _) in enumerate(blocks):
                    o_g, m_g, l_g = _unstack(os_[u]), _unstack_columns(ms[u]), _unstack_columns(ls[u])
                    r = row_blocks[own]
                    if first_pattern:
                        m_new, l_new, acc = m_g, l_g, o_g
                    else:
                        m_old = m_ref[r, :]
                        m_new = jnp.maximum(m_old, m_g)
                        alpha, beta = jnp.exp(m_old - m_new), jnp.exp(m_g - m_new)
                        l_new = l_scr[r, :] * alpha + l_g * beta
                        acc = y_ref[r, :] * alpha + o_g * beta
                    if last_pattern:
                        y_ref[r, :] = acc / l_new
                        m_ref[r, :] = m_new + jnp.log(l_new)
                    else:
                        y_ref[r, :] = acc
                        m_ref[r, :] = m_new
                        l_scr[r, :] = l_new
                return carry

            lax.fori_loop(0, d * nb // U, step, 0)

    col = pl.BlockSpec((S, 2 * HEAD), lambda j: (0, j))
    return pl.pallas_call(
        body, name="attn_fwd", grid=(q.shape[1] // (2 * HEAD),),
        in_specs=[col, col, col], out_specs=[col, col],
        out_shape=[jax.ShapeDtypeStruct(q.shape, F32)] * 2,
        scratch_shapes=[pltpu.VMEM((S, 2 * HEAD), F32), pltpu.VMEM((2, N_BACK, 2 * N_BACK), F32)],
        compiler_params=_params("parallel"),
    )(q, k, v)


def _attn_bwd(q, k, v, dy, lse, delta, after):
    S = q.shape[0]
    U = BWD_BLOCKS_PER_STEP

    def body(q_ref, k_ref, v_ref, dy_ref, lse_ref, delta_ref, after_ref, dq_ref, dk_ref, dv_ref, bias):
        _fill_band_bias(bias)
        nb_first = S // PATTERN_ORDER[0] // N_BACK
        first_writes_all = min(U, nb_first) == nb_first
        if not first_writes_all:
            dk_ref[...] = jnp.zeros_like(dk_ref)
            dv_ref[...] = jnp.zeros_like(dv_ref)
        for g, d in enumerate(PATTERN_ORDER):
            nb = S // d // N_BACK

            def step(i, carry, d=d, nb=nb, g=g):
                row_blocks, blocks = _step_blocks(i, U, nb, d)
                kb = [k_ref[r, :].astype(BF16) for r in row_blocks]
                vb = [v_ref[r, :].astype(BF16) for r in row_blocks]
                kws = [jnp.concatenate([kb[before], kb[own]], 0) for before, own, _ in blocks]
                vws = [jnp.concatenate([vb[before], vb[own]], 0) for before, own, _ in blocks]
                qss = [_stack_heads(q_ref[row_blocks[own], :].astype(BF16)) for _, own, _ in blocks]
                doss = [_stack_heads(dy_ref[row_blocks[own], :].astype(BF16)) for _, own, _ in blocks]
                ss = [lax.dot_general(qss[u], kws[u], NT, preferred_element_type=F32) for u in range(U)]
                dps = [lax.dot_general(doss[u], vws[u], NT, preferred_element_type=F32) for u in range(U)]
                pbs, dss = [], []
                for u, (_, own, which) in enumerate(blocks):
                    lse_c = _head_columns(lse_ref[row_blocks[own], :])
                    delta_c = _head_columns(delta_ref[row_blocks[own], :])
                    p_parts, ds_parts = [], []
                    for r0 in range(0, 2 * N_BACK, BWD_CHUNK):
                        r = slice(r0, r0 + BWD_CHUNK)
                        mask = bias[which, r0 % N_BACK:r0 % N_BACK + BWD_CHUNK, :]
                        p_r = jnp.exp(ss[u][r] + mask - lse_c[r])
                        p_parts.append(p_r.astype(BF16))
                        ds_parts.append((p_r * (dps[u][r] - delta_c[r])).astype(BF16))
                    pbs.append(jnp.concatenate(p_parts, axis=0))
                    dss.append(jnp.concatenate(ds_parts, axis=0))
                dqs = [jnp.dot(dss[u], kws[u], preferred_element_type=F32) for u in range(U)]
                dkws = [lax.dot_general(dss[u], qss[u], TN, preferred_element_type=F32) for u in range(U)]
                dvws = [lax.dot_general(pbs[u], doss[u], TN, preferred_element_type=F32) for u in range(U)]
                dk_parts, dv_parts = [None] * len(row_blocks), [None] * len(row_blocks)
                for u, (before, own, _) in enumerate(blocks):
                    dq = _unstack(dqs[u])
                    if g == 0:
                        dq_ref[row_blocks[own], :] = dq
                    else:
                        dq_ref[row_blocks[own], :] += dq
                    for idx, dkp, dvp in ((before, dkws[u][:N_BACK], dvws[u][:N_BACK]),
                                          (own, dkws[u][N_BACK:], dvws[u][N_BACK:])):
                        dk_parts[idx] = dkp if dk_parts[idx] is None else dk_parts[idx] + dkp
                        dv_parts[idx] = dvp if dv_parts[idx] is None else dv_parts[idx] + dvp
                for idx, r in enumerate(row_blocks):
                    if g == 0 and first_writes_all:
                        dk_ref[r, :] = dk_parts[idx]
                        dv_ref[r, :] = dv_parts[idx]
                    else:
                        dk_ref[r, :] += dk_parts[idx]
                        dv_ref[r, :] += dv_parts[idx]
                return carry

            lax.fori_loop(0, d * nb // U, step, 0)

    col = pl.BlockSpec((S, 2 * HEAD), lambda j: (0, j))
    return pl.pallas_call(
        body, name="attn_bwd", grid=(q.shape[1] // (2 * HEAD),),
        in_specs=[col] * 6 + [pl.BlockSpec(memory_space=pl.ANY)], out_specs=[col] * 3,
        out_shape=[jax.ShapeDtypeStruct(q.shape, F32)] * 3,
        scratch_shapes=[pltpu.VMEM((2, N_BACK, 2 * N_BACK), F32)],
        compiler_params=_params("parallel"),
    )(q, k, v, dy, lse, delta, after)


def _shift_down(z, before, k):
    row = lax.broadcasted_iota(jnp.int32, z.shape, 0)
    out = pltpu.roll(z, k, 0)
    for i in range(k):
        out = jnp.where(row == i, before[8 - k + i:8 - k + i + 1, :], out)
    return out


def _shift_up(z, after, k):
    rows = z.shape[0]
    row = lax.broadcasted_iota(jnp.int32, z.shape, 0)
    out = pltpu.roll(z, rows - k, 0)
    for i in range(k):
        out = jnp.where(row == rows - k + i, after[i:i + 1, :], out)
    return out


def _conv_fwd(bcu, before, is_first, w):
    b, c, u = bcu[:, 0:CONV_W], bcu[:, CONV_W:2 * CONV_W], bcu[:, 2 * CONV_W:3 * CONV_W]
    z = c * u
    zb = jnp.where(is_first, 0.0, before[:, CONV_W:2 * CONV_W] * before[:, 2 * CONV_W:3 * CONV_W])
    z1, z2 = _shift_down(z, zb, 1), _shift_down(z, zb, 2)
    cv = w[0:1, :] * z2 + w[1:2, :] * z1 + w[2:3, :] * z
    return b, c, u, z, z1, z2, cv


def _halo_before(tm, width):
    return pl.BlockSpec((8, width), lambda i: (jnp.maximum(i * (tm // 8) - 1, 0), 0))


def _mix_fwd(ya, bcu, qx, mkv, conv_w, g_a, g_c, g_x, w_out, g_post, x, tm):
    S = x.shape[0]

    def body(ya_ref, bcu_ref, before_ref, qx_ref, mkv_ref, cw_ref, ga_ref, gc_ref, gx_ref,
             wo_ref, gp_ref, x_ref, yx_ref, ycat_ref, y2_ref, x1_ref):
        ya = ya_ref[...]
        b, _, _, _, _, _, cv = _conv_fwd(bcu_ref[...], before_ref[...], pl.program_id(0) == 0, cw_ref[...])
        yc = b * cv

        qxb, mkvb = qx_ref[...], mkv_ref[...]
        heads = [slice(HEAD * hd, HEAD * (hd + 1)) for hd in range(XATTN_W // HEAD)]
        ss = [lax.dot_general(qxb[:, sl], mkvb[:, sl], NT, preferred_element_type=F32) * SCALE for sl in heads]
        ms = [jnp.max(s, axis=1, keepdims=True) for s in ss]
        ps = [jnp.exp(s - m) for s, m in zip(ss, ms)]
        ls = [jnp.sum(p, axis=1, keepdims=True) for p in ps]
        os_ = [jnp.dot(p.astype(BF16), mkvb[:, XATTN_W + sl.start:XATTN_W + sl.stop], preferred_element_type=F32)
               for p, sl in zip(ps, heads)]
        for sl, o, l in zip(heads, os_, ls):
            yx_ref[:, sl] = o / l
        yx = yx_ref[...]

        ycat_ref[:, 0:ATTN_W] = (_rms_hat(ya)[0] * ga_ref[...]).astype(BF16)
        ycat_ref[:, ATTN_W:ATTN_W + CONV_W] = (_rms_hat(yc)[0] * gc_ref[...]).astype(BF16)
        ycat_ref[:, ATTN_W + CONV_W:D_MODEL] = (_rms_hat(yx)[0] * gx_ref[...]).astype(BF16)
        y2 = jnp.dot(ycat_ref[...], wo_ref[...], preferred_element_type=F32)
        y2_ref[...] = y2
        x1_ref[...] = x_ref[...] + _rms_hat(y2)[0] * gp_ref[...]

    n_mem = mkv.shape[0]
    return pl.pallas_call(
        body, name="mix_fwd", grid=(S // tm,),
        in_specs=[_rows(tm, ATTN_W), _rows(tm, 3 * CONV_W), _halo_before(tm, 3 * CONV_W), _rows(tm, XATTN_W),
                  _resident((n_mem, 2 * XATTN_W)), _resident((3, CONV_W)), _resident((1, ATTN_W)),
                  _resident((1, CONV_W)), _resident((1, XATTN_W)), _resident((D_MODEL, D_MODEL)),
                  _resident((1, D_MODEL)), _rows(tm, D_MODEL)],
        out_specs=[_rows(tm, XATTN_W), _rows(tm, D_MODEL), _rows(tm, D_MODEL), _rows(tm, D_MODEL)],
        out_shape=[jax.ShapeDtypeStruct((S, XATTN_W), F32), jax.ShapeDtypeStruct((S, D_MODEL), BF16),
                   jax.ShapeDtypeStruct((S, D_MODEL), F32), jax.ShapeDtypeStruct((S, D_MODEL), F32)],
        compiler_params=_params("parallel"),
    )(ya, bcu, bcu, qx, mkv, conv_w, g_a, g_c, g_x, w_out, g_post, x)


def _mlp_fwd_bwd(x1, target, g_pre, g_post, w_up, w_down, tm):
    S = x1.shape[0]
    n_ff = D_FF // SHARD_FF

    def body(x1_ref, t_ref, gpre_ref, gpost_ref, wup_ref, wdn_ref,
             h2_ref, f_ref, du_ref, df2_ref, dx1_ref, dgpre_ref, dgpost_ref, loss_ref, u_scr):
        @pl.when(pl.program_id(0) == 0)
        def _():
            dgpre_ref[...] = jnp.zeros_like(dgpre_ref)
            dgpost_ref[...] = jnp.zeros_like(dgpost_ref)
            loss_ref[...] = jnp.zeros_like(loss_ref)

        x1 = x1_ref[...]
        x1hat, r1 = _rms_hat(x1)
        h2 = (x1hat * gpre_ref[...]).astype(BF16)
        h2_ref[...] = h2
        f2 = jnp.zeros((tm, D_MODEL), F32)
        for j in range(n_ff):
            cols = slice(SHARD_FF * j, SHARD_FF * (j + 1))
            u = jnp.maximum(jnp.dot(h2, wup_ref[j], preferred_element_type=F32), 0.0)
            u_scr[:, cols] = u
            f = (u * u).astype(BF16)
            f_ref[:, cols] = f
            f2 = f2 + jnp.dot(f, wdn_ref[cols, :], preferred_element_type=F32)
        f2hat, r2 = _rms_hat(f2)
        err = x1 + f2hat * gpost_ref[...] - t_ref[...]
        loss_ref[...] += 0.5 * jnp.sum(jnp.mean(err * err, axis=-1, keepdims=True), axis=0, keepdims=True)
        dx2 = err * (1.0 / D_MODEL)
        dgpost_ref[...] += jnp.sum(dx2 * f2hat, axis=0, keepdims=True)
        df2 = _rms_bwd(f2hat, r2, gpost_ref[...], dx2).astype(BF16)
        df2_ref[...] = df2
        dh2 = jnp.zeros((tm, D_MODEL), F32)
        for j in range(n_ff):
            cols = slice(SHARD_FF * j, SHARD_FF * (j + 1))
            df = lax.dot_general(df2, wdn_ref[cols, :], NT, preferred_element_type=F32)
            du = (2.0 * u_scr[:, cols] * df).astype(BF16)
            du_ref[:, cols] = du
            dh2 = dh2 + lax.dot_general(du, wup_ref[j], NT, preferred_element_type=F32)
        dgpre_ref[...] += jnp.sum(dh2 * x1hat, axis=0, keepdims=True)
        dx1_ref[...] = dx2 + _rms_bwd(x1hat, r1, gpre_ref[...], dh2)

    acc = pl.BlockSpec((1, D_MODEL), lambda i: (0, 0))
    return pl.pallas_call(
        body, name="mlp_fwd_bwd", grid=(S // tm,),
        in_specs=[_rows(tm, D_MODEL), _rows(tm, D_MODEL), _resident((1, D_MODEL)), _resident((1, D_MODEL)),
                  _resident((n_ff, D_MODEL, SHARD_FF)), _resident((D_FF, D_MODEL))],
        out_specs=[_rows(tm, D_MODEL), _rows(tm, D_FF), _rows(tm, D_FF), _rows(tm, D_MODEL), _rows(tm, D_MODEL),
                   acc, acc, pl.BlockSpec((1, 1), lambda i: (0, 0))],
        out_shape=[jax.ShapeDtypeStruct((S, D_MODEL), BF16), jax.ShapeDtypeStruct((S, D_FF), BF16),
                   jax.ShapeDtypeStruct((S, D_FF), BF16), jax.ShapeDtypeStruct((S, D_MODEL), BF16),
                   jax.ShapeDtypeStruct((S, D_MODEL), F32), jax.ShapeDtypeStruct((1, D_MODEL), F32),
                   jax.ShapeDtypeStruct((1, D_MODEL), F32), jax.ShapeDtypeStruct((1, 1), F32)],
        scratch_shapes=[pltpu.VMEM((tm, D_FF), F32)],
        compiler_params=_params("arbitrary"),
    )(x1, target, g_pre, g_post, w_up, w_down)


def _weight_grad(name, a, b, rows_sharded, after):
    S, K = a.shape
    N = b.shape[1]
    if rows_sharded:
        tk, tn = K // N_CHIPS, N
        a_spec = pl.BlockSpec((S, tk), lambda j: (0, j))
        b_spec = pl.BlockSpec((S, tn), lambda j: (0, 0), pipeline_mode=pl.Buffered(1))
    else:
        tk, tn = K, N // N_CHIPS
        a_spec = pl.BlockSpec((S, tk), lambda j: (0, 0), pipeline_mode=pl.Buffered(1))
        b_spec = pl.BlockSpec((S, tn), lambda j: (0, j))
    half = tk // 2

    def body(a_ref, b_ref, after_ref, o_ref):
        res = lax.dot_general(a_ref[...], b_ref[...], TN, preferred_element_type=F32)
        o_ref[0, 0] = res[:half]
        o_ref[1, 0] = res[half:]

    return pl.pallas_call(
        body, name=name, grid=(N_CHIPS,), in_specs=[a_spec, b_spec, pl.BlockSpec(memory_space=pl.ANY)],
        out_specs=pl.BlockSpec((2, 1, half, tn), lambda j: (0, j, 0, 0)),
        out_shape=jax.ShapeDtypeStruct((2, N_CHIPS, half, tn), F32),
        compiler_params=_params("parallel"),
    )(a, b, after)


def _mixer_bwd(dx1, y2, ycat, ya, yx, bcu, qx, mkv, conv_w, g_a, g_c, g_x, w_out, g_post, after, tm):
    S = dx1.shape[0]
    n_mem = mkv.shape[0]
    n_tiles = S // tm
    half = D_MODEL // N_CHIPS // 2

    def body(dx1_ref, y2_ref, ycat_ref, ya_ref, yx_ref, bcu_ref, before_ref, qx_ref, mkv_ref, cw_ref, ga_ref, gc_ref,
             gx_ref, wo_ref, gp_ref, after_ref, gwo_ref, dya_ref, delta_ref, tail_ref, dmkv_ref, dcw_ref, dgp_ref,
             dga_ref, dgc_ref, dgx_ref, carry):
        step = pl.program_id(0)
        first_tile = step == n_tiles - 1

        @pl.when(step == 0)
        def _():
            for ref in (gwo_ref, dmkv_ref, dcw_ref, dgp_ref, dga_ref, dgc_ref, dgx_ref, carry):
                ref[...] = jnp.zeros_like(ref)

        dx1 = dx1_ref[...]
        y2hat, r2 = _rms_hat(y2_ref[...])
        dgp_ref[...] += jnp.sum(dx1 * y2hat, axis=0, keepdims=True)
        dy2 = _rms_bwd(y2hat, r2, gp_ref[...], dx1).astype(BF16)
        gwo = lax.dot_general(ycat_ref[...], dy2, TN, preferred_element_type=F32)
        for k in range(2 * N_CHIPS):
            gwo_ref[k % 2, k // 2] += gwo[half * k:half * (k + 1)]
        dycat = lax.dot_general(dy2, wo_ref[...], NT, preferred_element_type=F32)

        d_na = dycat[:, 0:ATTN_W]
        ya = ya_ref[...]
        yahat, ra = _rms_hat(ya)
        dga_ref[...] += jnp.sum(d_na * yahat, axis=0, keepdims=True)
        dya = _rms_bwd(yahat, ra, ga_ref[...], d_na)
        dya_ref[...] = dya
        prod = dya * ya
        hi = prod.astype(BF16)
        lo = (prod - hi.astype(F32)).astype(BF16)
        head_of = lambda axis: lax.shift_right_logical(lax.broadcasted_iota(jnp.int32, (ATTN_W, ATTN_W), axis),
                                                       HEAD.bit_length() - 1)
        ones = jnp.where(head_of(0) == head_of(1), 1.0, 0.0).astype(BF16)
        delta_ref[...] = jnp.dot(hi, ones, preferred_element_type=F32) + jnp.dot(lo, ones, preferred_element_type=F32)

        w = cw_ref[...]
        b, c, u, z, z1, z2, cv = _conv_fwd(bcu_ref[...], before_ref[...], first_tile, w)
        d_nc = dycat[:, ATTN_W:ATTN_W + CONV_W]
        ychat, rc = _rms_hat(b * cv)
        dgc_ref[...] += jnp.sum(d_nc * ychat, axis=0, keepdims=True)
        dyc = _rms_bwd(ychat, rc, gc_ref[...], d_nc)
        dcv = dyc * b
        behind = carry[...]
        dz = w[2:3, :] * dcv + w[1:2, :] * _shift_up(dcv, behind, 1) + w[0:1, :] * _shift_up(dcv, behind, 2)
        carry[...] = dcv[0:8, :]
        dcw_ref[0:1, :] += jnp.sum(dcv * z2, axis=0, keepdims=True)
        dcw_ref[1:2, :] += jnp.sum(dcv * z1, axis=0, keepdims=True)
        dcw_ref[2:3, :] += jnp.sum(dcv * z, axis=0, keepdims=True)
        tail_ref[:, 0:CONV_W] = (dyc * cv).astype(BF16)
        tail_ref[:, CONV_W:2 * CONV_W] = (dz * u).astype(BF16)
        tail_ref[:, 2 * CONV_W:3 * CONV_W] = (dz * c).astype(BF16)

        d_nx = dycat[:, ATTN_W + CONV_W:D_MODEL]
        yxhat, rx = _rms_hat(yx_ref[...])
        dgx_ref[...] += jnp.sum(d_nx * yxhat, axis=0, keepdims=True)
        dyx = _rms_bwd(yxhat, rx, gx_ref[...], d_nx)
        qxb, mkvb = qx_ref[...], mkv_ref[...]
        heads = [slice(HEAD * hd, HEAD * (hd + 1)) for hd in range(XATTN_W // HEAD)]
        values = [slice(XATTN_W + sl.start, XATTN_W + sl.stop) for sl in heads]
        ss = [lax.dot_general(qxb[:, sl], mkvb[:, sl], NT, preferred_element_type=F32) * SCALE for sl in heads]
        es = [jnp.exp(s - jnp.max(s, axis=1, keepdims=True)) for s in ss]
        ps = [e / jnp.sum(e, axis=1, keepdims=True) for e in es]
        dobs = [dyx[:, sl].astype(BF16) for sl in heads]
        dps = [lax.dot_general(dob, mkvb[:, vsl], NT, preferred_element_type=F32) for dob, vsl in zip(dobs, values)]
        dss = [(p * (dp - jnp.sum(p * dp, axis=1, keepdims=True)) * SCALE).astype(BF16) for p, dp in zip(ps, dps)]
        for sl, vsl, p, dob, ds in zip(heads, values, ps, dobs, dss):
            tail_ref[:, 3 * CONV_W + sl.start:3 * CONV_W + sl.stop] = jnp.dot(
                ds, mkvb[:, sl], preferred_element_type=F32).astype(BF16)
            dmkv_ref[:, sl] += lax.dot_general(ds, qxb[:, sl], TN, preferred_element_type=F32)
            dmkv_ref[:, vsl] += lax.dot_general(p.astype(BF16), dob, TN, preferred_element_type=F32)

    rows = lambda width: pl.BlockSpec((tm, width), lambda i: (n_tiles - 1 - i, 0))
    before = pl.BlockSpec((8, 3 * CONV_W), lambda i: (jnp.maximum((n_tiles - 1 - i) * (tm // 8) - 1, 0), 0))
    acc = lambda r, w: pl.BlockSpec((r, w), lambda i: (0, 0))
    return pl.pallas_call(
        body, name="mixer_bwd", grid=(n_tiles,),
        in_specs=[rows(D_MODEL), rows(D_MODEL), rows(D_MODEL), rows(ATTN_W), rows(XATTN_W), rows(3 * CONV_W), before,
                  rows(XATTN_W), _resident((n_mem, 2 * XATTN_W)), _resident((3, CONV_W)), _resident((1, ATTN_W)),
                  _resident((1, CONV_W)), _resident((1, XATTN_W)), _resident((D_MODEL, D_MODEL)),
                  _resident((1, D_MODEL)), pl.BlockSpec(memory_space=pl.ANY)],
        out_specs=[pl.BlockSpec((2, N_CHIPS, half, D_MODEL), lambda i: (0, 0, 0, 0)), rows(ATTN_W), rows(ATTN_W),
                   rows(3 * CONV_W + XATTN_W), acc(n_mem, 2 * XATTN_W),
                   acc(3, CONV_W), acc(1, D_MODEL), acc(1, ATTN_W), acc(1, CONV_W), acc(1, XATTN_W)],
        out_shape=[jax.ShapeDtypeStruct((2, N_CHIPS, half, D_MODEL), F32), jax.ShapeDtypeStruct((S, ATTN_W), F32),
                   jax.ShapeDtypeStruct((S, ATTN_W), F32), jax.ShapeDtypeStruct((S, 3 * CONV_W + XATTN_W), BF16),
                   jax.ShapeDtypeStruct((n_mem, 2 * XATTN_W), F32), jax.ShapeDtypeStruct((3, CONV_W), F32),
                   jax.ShapeDtypeStruct((1, D_MODEL), F32), jax.ShapeDtypeStruct((1, ATTN_W), F32),
                   jax.ShapeDtypeStruct((1, CONV_W), F32), jax.ShapeDtypeStruct((1, XATTN_W), F32)],
        scratch_shapes=[pltpu.VMEM((8, CONV_W), F32)],
        compiler_params=_params("arbitrary"),
    )(dx1, y2, ycat, ya, yx, bcu, bcu, qx, mkv, conv_w, g_a, g_c, g_x, w_out, g_post, after)


def _memkv_bwd(mem, g_mem, w_kv, dmkv):
    n_mem = mem.shape[0]
    half = D_MODEL // N_CHIPS // 2

    def body(mem_ref, g_ref, w_ref, d_ref, dw_ref, dg_ref):
        mhat, _ = _rms_hat(mem_ref[...])
        mn = (mhat * g_ref[...]).astype(BF16)
        d = d_ref[...].astype(BF16)
        for k in range(2 * N_CHIPS):
            dw_ref[k % 2, k // 2] = lax.dot_general(mn[:, half * k:half * (k + 1)], d, TN, preferred_element_type=F32)
        dmn = lax.dot_general(d, w_ref[...], NT, preferred_element_type=F32)
        dg_ref[...] = jnp.sum(dmn * mhat, axis=0, keepdims=True)

    return pl.pallas_call(
        body, name="memkv_bwd",
        out_shape=[jax.ShapeDtypeStruct((2, N_CHIPS, half, 2 * XATTN_W), F32), jax.ShapeDtypeStruct((1, D_MODEL), F32)],
        compiler_params=pltpu.CompilerParams(vmem_limit_bytes=VMEM_LIMIT_V7X),
    )(mem, g_mem, w_kv, dmkv)


def _in_proj_bwd(dqkv, tail, cos, sin, w_in, x, h, g, dx1, after, tm):
    S = x.shape[0]
    step_w = 2 * 256
    half = D_MODEL // 2

    def body(dq_ref, dk_ref, dv_ref, tail_ref, cos_ref, sin_ref, w_hbm, x_ref, h_ref, g_ref, dx1_ref, after_ref,
             dx_ref, gw_ref, dg_ref, dproj_ref, w_full, sems):
        _side_by_side(w_hbm, w_full, sems)

        @pl.when(pl.program_id(0) == 0)
        def _():
            dg_ref[...] = jnp.zeros_like(dg_ref)
            gw_ref[...] = jnp.zeros_like(gw_ref)

        halves = [slice(0, tm // 2), slice(tm // 2, tm)]
        for rows in halves:
            c, s = cos_ref[rows, :], sin_ref[rows, :]
            for j in range(ATTN_W // 128):
                cols = slice(128 * j, 128 * (j + 1))
                dproj_ref[rows, cols] = _rope128(dq_ref[rows, cols] * SCALE, c, s, True).astype(BF16)
                dproj_ref[rows, ATTN_W + 128 * j:ATTN_W + 128 * (j + 1)] = _rope128(dk_ref[rows, cols], c, s, True).astype(BF16)
            dproj_ref[rows, 2 * ATTN_W:3 * ATTN_W] = dv_ref[rows, :].astype(BF16)
            dproj_ref[rows, 3 * ATTN_W:PROJ_W] = tail_ref[rows, :]
        dhs = [lax.dot_general(dproj_ref[rows, :], w_full[...], NT, preferred_element_type=F32) for rows in halves]
        for rows, dh in zip(halves, dhs):
            xhat, r = _rms_hat(x_ref[rows, :])
            dg_ref[...] += jnp.sum(dh * xhat, axis=0, keepdims=True)
            dx_ref[rows, :] = dx1_ref[rows, :] + _rms_bwd(xhat, r, g_ref[...], dh)
        hb = h_ref[...]
        for step in range(PROJ_W // step_w):
            res = lax.dot_general(hb, dproj_ref[:, step * step_w:(step + 1) * step_w], TN, preferred_element_type=F32)
            lo = step * step_w
            while lo < (step + 1) * step_w:
                chip = lo // SHARD_IN
                hi = min((step + 1) * step_w, (chip + 1) * SHARD_IN)
                for hh in range(2):
                    gw_ref[hh, chip, :, lo - chip * SHARD_IN:hi - chip * SHARD_IN] += (
                        res[half * hh:half * (hh + 1), lo - step * step_w:hi - step * step_w])
                lo = hi

    whole = lambda shape: pl.BlockSpec(shape, lambda i: (0,) * len(shape))
    return pl.pallas_call(
        body, name="in_proj_bwd", grid=(S // tm,),
        in_specs=[_rows(tm, ATTN_W)] * 3 + [_rows(tm, PROJ_W - 3 * ATTN_W), _rows(tm, 128), _rows(tm, 128),
                  pl.BlockSpec(memory_space=pl.ANY), _rows(tm, D_MODEL), _rows(tm, D_MODEL), _resident((1, D_MODEL)),
                  _rows(tm, D_MODEL), pl.BlockSpec(memory_space=pl.ANY)],
        out_specs=[_rows(tm, D_MODEL), whole((2, N_CHIPS, half, SHARD_IN)), whole((1, D_MODEL))],
        out_shape=[jax.ShapeDtypeStruct((S, D_MODEL), F32), jax.ShapeDtypeStruct((2, N_CHIPS, half, SHARD_IN), F32),
                   jax.ShapeDtypeStruct((1, D_MODEL), F32)],
        scratch_shapes=[pltpu.VMEM((tm, PROJ_W), BF16), pltpu.VMEM((D_MODEL, PROJ_W), BF16),
                        pltpu.SemaphoreType.DMA((N_CHIPS,))],
        compiler_params=_params("arbitrary"),
    )(*dqkv, tail, cos, sin, w_in, x, h, g, dx1, after)


def _row_tile(rows):
    return ROW_TILE if rows % ROW_TILE == 0 else rows


def _chip_sums_bf16(name, grads, from_sibling, place):
    k = len(grads)
    _, n, rows, _ = grads[0].shape
    tr = _row_tile(rows)

    def body(place_ref, *refs):
        for g_ref, b_ref, o_ref in zip(refs[:k], refs[k:2 * k], refs[2 * k:]):
            o_ref[...] = (g_ref[0] + b_ref[...]).astype(BF16)

    mine = lambda g: pl.BlockSpec((1, 1, tr, g.shape[3]), lambda s, i, p: (p[0], s, i, 0))
    slab = lambda g: pl.BlockSpec((1, tr, g.shape[3]), lambda s, i, p: (s, i, 0))
    return pl.pallas_call(
        body, name=name, out_shape=[jax.ShapeDtypeStruct(g.shape[1:], BF16) for g in grads],
        grid_spec=pltpu.PrefetchScalarGridSpec(
            num_scalar_prefetch=1, grid=(n, rows // tr),
            in_specs=[mine(g) for g in grads] + [slab(g) for g in grads], out_specs=[slab(g) for g in grads]),
        compiler_params=_params("parallel", "parallel"),
    )(place, *grads, *from_sibling)


def _final_sums(name, grads, from_sibling, others, place):
    k = len(grads)
    rows = grads[0].shape[2]
    tr = _row_tile(rows)

    def body(place_ref, *refs):
        for a in range(k):
            own_ref, sib_ref = refs[a], refs[k + a]
            acc = own_ref[0, 0] + sib_ref[0]
            for o in refs[2 * k + 3 * a:2 * k + 3 * a + 3]:
                acc = acc + o[0].astype(F32)
            refs[5 * k + a][0] = acc

    own = lambda g: pl.BlockSpec((1, 1, tr, g.shape[3]), lambda i, p: (p[0], p[1], i, 0))
    sib = lambda g: pl.BlockSpec((1, tr, g.shape[3]), lambda i, p: (p[1], i, 0))
    other = lambda g, j: pl.BlockSpec((1, tr, g.shape[3]), lambda i, p: (j, i, 0))
    return pl.pallas_call(
        body, name=name, out_shape=[jax.ShapeDtypeStruct((2,) + g.shape[2:], F32) for g in grads],
        grid_spec=pltpu.PrefetchScalarGridSpec(
            num_scalar_prefetch=1, grid=(rows // tr,),
            in_specs=[own(g) for g in grads] + [sib(g) for g in grads] + [other(g, j) for g in grads for j in range(3)],
            out_specs=[pl.BlockSpec((1, tr, g.shape[3]), lambda i, p: (p[0], i, 0)) for g in grads]),
        compiler_params=_params("parallel"),
    )(place, *grads, *from_sibling, *[o for o in others for _ in range(3)])


def _adamw_update(w, g, m, v):
    m = ADAM_B1 * m + (1.0 - ADAM_B1) * g
    v = ADAM_B2 * v + (1.0 - ADAM_B2) * (g * g)
    m_hat = m * (1.0 / (1.0 - ADAM_B1 ** ADAM_STEP))
    v_hat = v * (1.0 / (1.0 - ADAM_B2 ** ADAM_STEP))
    return -ADAM_LR * (m_hat / (jnp.sqrt(v_hat) + ADAM_EPS) + ADAM_WD * w), m, v


def _adamw(name, params, after):
    k = len(params)
    rows = params[0][0].shape[0]
    tr = ADAMW_ROW_TILE if rows % ADAMW_ROW_TILE == 0 else rows

    def body(*refs):
        ins, outs = refs[:4 * k], refs[4 * k + 1:]
        for a in range(k):
            w_ref, g_ref, m_ref, v_ref = ins[4 * a:4 * a + 4]
            g = g_ref[...]
            outs[4 * a][...] = g
            outs[4 * a + 1][...], outs[4 * a + 2][...], outs[4 * a + 3][...] = _adamw_update(w_ref[...], g, m_ref[...], v_ref[...])

    spec = lambda w: pl.BlockSpec((tr, w.shape[1]), lambda i: (i, 0))
    out = pl.pallas_call(
        body, name=name, grid=(rows // tr,),
        in_specs=[spec(p[0]) for p in params for _ in range(4)] + [pl.BlockSpec(memory_space=pl.ANY)],
        out_specs=[spec(p[0]) for p in params for _ in range(4)],
        out_shape=[jax.ShapeDtypeStruct(p[0].shape, F32) for p in params for _ in range(4)],
        compiler_params=_params("parallel"),
    )(*[t for p in params for t in p], after)
    return [out[4 * a:4 * a + 4] for a in range(k)]


def _small_update(summed, chip, gains, gains_m, gains_v, taps, taps_m, taps_v):
    n = len(gains)
    widths = [g.shape[1] for g in gains]
    k, w = taps.shape

    def body(*refs):
        chip_ref, sum_ref = refs[0], refs[1]
        params = [refs[2 + 3 * i:5 + 3 * i] for i in range(n + 1)]
        outs = [refs[2 + 3 * (n + 1) + 4 * i:2 + 3 * (n + 1) + 4 * (i + 1)] for i in range(n + 1)]
        loss_ref = refs[-1]
        for i in range(n):
            g = sum_ref[i:i + 1, 0:widths[i]]
            wr, mr, vr = params[i]
            outs[i][0][...] = g
            outs[i][1][...], outs[i][2][...], outs[i][3][...] = _adamw_update(wr[...], g, mr[...], vr[...])
        g = sum_ref[n:n + k, 0:w]
        for j in range(1, N_CHIPS):
            g = jnp.where(chip_ref[0] == j, sum_ref[n:n + k, w * j:w * (j + 1)], g)
        wr, mr, vr = params[n]
        for out_ref, val in zip(outs[n], (g, *_adamw_update(wr[...], g, mr[...], vr[...]))):
            for j in range(k):
                out_ref[j] = val[j:j + 1, :]
        loss_ref[...] = sum_ref[n + k:n + k + 1, 0:1]

    vmem = pl.BlockSpec(memory_space=pltpu.VMEM)
    operands = [chip, summed]
    for p in zip(list(gains) + [taps], list(gains_m) + [taps_m], list(gains_v) + [taps_v]):
        operands += list(p)
    shapes = [jax.ShapeDtypeStruct(shape, F32) for shape in [g.shape for g in gains] + [(k, 1, w)] for _ in range(4)]
    out = pl.pallas_call(
        body, name="small_update", out_shape=shapes + [jax.ShapeDtypeStruct((1, 1), F32)],
        in_specs=[pl.BlockSpec(memory_space=pltpu.SMEM)] + [vmem] * (len(operands) - 1),
        out_specs=[vmem] * (len(shapes) + 1),
    )(*operands)
    return [out[4 * i:4 * (i + 1)] for i in range(n + 1)], out[-1]


def _sum_blocks(name, blocks):
    n, rows, cols = blocks.shape

    def body(b_ref, o_ref):
        acc = b_ref[0]
        for k in range(1, n):
            acc = acc + b_ref[k]
        o_ref[...] = acc

    return pl.pallas_call(body, name=name, out_shape=jax.ShapeDtypeStruct((rows, cols), F32))(blocks)


def _place():
    return lax.axis_index("x"), lax.axis_index("y"), lax.axis_index("c")


def _other_chips(x, y):
    return [(1 - x, y), (x, 1 - y), (1 - x, 1 - y)]


def _allgather_finish(name, shards, landed, pass_on):
    n = len(shards)

    def body(*refs):
        ins, outs, stage = refs[:n], refs[2 * n:3 * n], refs[3 * n:4 * n]
        send_sems, recv_sems, local_sems = refs[4 * n:]
        x, y, c = _place()
        chips = _other_chips(x, y)
        barrier = pltpu.get_barrier_semaphore()
        pl.semaphore_signal(barrier, inc=1, device_id=(x, y, 1 - c), device_id_type=MESH)
        pl.semaphore_wait(barrier, 1)

        def copy(a, k, chip, half):
            place = outs[a].at[2 * chip[0] + chip[1], half]
            return pltpu.make_async_remote_copy(
                src_ref=place, dst_ref=place, send_sem=send_sems.at[3 * a + k], recv_sem=recv_sems.at[3 * a + k],
                device_id=(x, y, 1 - c), device_id_type=MESH)

        load = [pltpu.make_async_copy(ins[a], stage[a], local_sems.at[a]) for a in range(n)]
        local = [pltpu.make_async_copy(stage[a], outs[a].at[2 * x + y], local_sems.at[a]) for a in range(n)]
        for cp in load:
            cp.start()
        passed = [copy(a, k, chip, c) for a in range(n) if pass_on[a] for k, chip in enumerate(chips)]
        for cp in passed:
            cp.start()
        for a in range(n):
            load[a].wait()
            local[a].start()
        for a in range(n):
            if pass_on[a]:
                for k, chip in enumerate(chips):
                    copy(a, k, chip, 1 - c).wait_recv()
        for cp in passed:
            cp.wait_send()
        for cp in local:
            cp.wait()

    any_spec = pl.BlockSpec(memory_space=pl.ANY)
    return pl.pallas_call(
        body, name=name,
        out_shape=[jax.ShapeDtypeStruct((N_CHIPS,) + s.shape, s.dtype) for s in shards],
        in_specs=[any_spec] * (2 * n), out_specs=[any_spec] * n,
        input_output_aliases={n + a: a for a in range(n)},
        scratch_shapes=[pltpu.VMEM(s.shape, s.dtype) for s in shards]
        + [pltpu.SemaphoreType.DMA((3 * n,)), pltpu.SemaphoreType.DMA((3 * n,)), pltpu.SemaphoreType.DMA((n,))],
        compiler_params=pltpu.CompilerParams(vmem_limit_bytes=VMEM_LIMIT_V7X, collective_id=HANDSHAKES["sibling"][0]),
    )(*shards, *landed)


def _plan_first_hop(x, y, c, shards, lands):
    return [(shards[a].at[c], lands[a].at[2 * x + y, c], lands[a].at[2 * chip[0] + chip[1], c], (*chip, c))
            for a in range(len(shards)) for chip in _other_chips(x, y)]


def _plan_pass_on(x, y, c, nothing, lands):
    def place(a, chip, half):
        return lands[a].at[2 * chip[0] + chip[1], half]

    return [(place(a, chip, c), place(a, chip, c), place(a, chip, 1 - c), (x, y, 1 - c))
            for a in range(len(lands)) for chip in _other_chips(x, y)]


def _plan_own_half_to_sibling(x, y, c, nothing, lands):
    return [(lands[a].at[c], lands[a].at[c], lands[a].at[1 - c], (x, y, 1 - c)) for a in range(len(lands))]


def _plan_other_half_to_sibling(x, y, c, grads, lands):
    return [(grads[a].at[1 - c], lands[a], lands[a], (x, y, 1 - c)) for a in range(len(grads))]


def _plan_to_other_chips(x, y, c, partials, lands):
    return [(partials[a].at[2 * chip[0] + chip[1]], lands[a].at[k], lands[a].at[k], (*chip, c))
            for a in range(len(partials)) for k, chip in enumerate(_other_chips(x, y))]


def _plan_to_all(x, y, c, blocks, lands):
    flips = [(fx, fy, fc) for fx in (0, 1) for fy in (0, 1) for fc in (0, 1) if (fx, fy, fc) != (0, 0, 0)]
    peers = [(1 - x if fx else x, 1 - y if fy else y, 1 - c if fc else c) for fx, fy, fc in flips]
    return [(blocks[0], lands[0].at[4 * x + 2 * y + c], lands[0].at[4 * p[0] + 2 * p[1] + p[2]], p) for p in peers]


def _planned_copies(plan, srcs, lands, send_sems, recv_sems):
    x, y, c = _place()

    def pair(k, src, there, here, to):
        make = lambda dst: pltpu.make_async_remote_copy(
            src_ref=src, dst_ref=dst, send_sem=send_sems.at[k], recv_sem=recv_sems.at[k], device_id=to, device_id_type=MESH)
        return make(there), make(here)

    return [pair(k, *entry) for k, entry in enumerate(plan(x, y, c, srcs, lands))]


_HBM_SPEC = pl.BlockSpec(memory_space=pltpu.HBM)
_SEM_SPEC = pl.BlockSpec(memory_space=pltpu.SEMAPHORE)


def _hbm(a):
    return pltpu.with_memory_space_constraint(a, pltpu.HBM)


HANDSHAKES = {
    "sibling": (1, lambda x, y, c: [(x, y, 1 - c)]),
}


def _exchange_start(name, plan, n_copies, srcs, land_shapes, after, lands=None, peers=None):
    if lands is None:
        lands = [lax.empty(s.shape, s.dtype) for s in land_shapes]
    land_shapes = lands
    ns, nl = len(srcs), len(land_shapes)
    n_in = ns + nl + 1
    collective_id, peers_of = HANDSHAKES[peers] if peers else (None, None)

    def body(*refs):
        if peers:
            who = peers_of(*_place())
            barrier = pltpu.get_barrier_semaphore()
            for peer in who:
                pl.semaphore_signal(barrier, inc=1, device_id=peer, device_id_type=MESH)
            pl.semaphore_wait(barrier, len(who))
        for send, _ in _planned_copies(plan, refs[:ns], refs[ns:ns + nl], refs[n_in], refs[n_in + 1]):
            send.start()
        refs[-1][...] = jnp.zeros_like(refs[-1])

    out = pl.pallas_call(
        body, name=name,
        out_shape=(pltpu.SemaphoreType.DMA((n_copies,)), pltpu.SemaphoreType.DMA((n_copies,)),
                   *[pltpu.HBM(s.shape, s.dtype) for s in land_shapes], jax.ShapeDtypeStruct((8, 128), F32)),
        in_specs=[_HBM_SPEC] * (ns + nl) + [pl.BlockSpec(memory_space=pl.ANY)],
        out_specs=(_SEM_SPEC, _SEM_SPEC, *[_HBM_SPEC] * nl, pl.BlockSpec(memory_space=pltpu.VMEM)),
        input_output_aliases={ns + i: 2 + i for i in range(nl)},
        compiler_params=pltpu.CompilerParams(has_side_effects=pltpu.SideEffectType.DATAFLOW_SIDE_EFFECTING,
                                             collective_id=collective_id),
    )(*[_hbm(s) for s in srcs], *[_hbm(l) for l in lands], after)
    return out[0], out[1], list(out[2:2 + nl]), out[-1]


def _exchange_wait(name, plan, srcs, started, after):
    send_sems, recv_sems, lands, _ = started
    ns, nl = len(srcs), len(lands)
    after = list(after) if isinstance(after, (list, tuple)) else [after]

    def body(*refs):
        for send, recv in _planned_copies(plan, refs[:ns], refs[ns:ns + nl], refs[ns + nl], refs[ns + nl + 1]):
            send.wait_send()
            recv.wait_recv()

    return pl.pallas_call(
        body, name=name, out_shape=[pltpu.HBM(l.shape, l.dtype) for l in lands],
        in_specs=[_HBM_SPEC] * (ns + nl) + [_SEM_SPEC, _SEM_SPEC] + [pl.BlockSpec(memory_space=pl.ANY)] * len(after),
        out_specs=[_HBM_SPEC] * nl, input_output_aliases={ns + i: i for i in range(nl)},
        compiler_params=pltpu.CompilerParams(has_side_effects=pltpu.SideEffectType.DATAFLOW_SIDE_EFFECTING),
    )(*[_hbm(s) for s in srcs], *lands, send_sems, recv_sems, *after)


def _like(arrays, lead, dtype=None):
    return [jax.ShapeDtypeStruct(tuple(lead) + a.shape[-2:], dtype or a.dtype) for a in arrays]


class _StepExchanges:
    def __init__(self, mats, conv_w):
        x, y, c = _place()
        self.place = jnp.stack([c, 2 * x + y]).astype(jnp.int32)
        shards = [w.astype(BF16).reshape(2, w.shape[0] // 2, w.shape[1]) for w in mats]
        self._in_shard = shards[:1]
        self._in = _exchange_start("w_in_allgather_start", _plan_first_hop, 3, self._in_shard,
                                   _like(self._in_shard, (N_CHIPS, 2)), shards[0])
        self.zero = self._in[3]
        taps = jnp.pad(conv_w, ((0, 8 - conv_w.shape[0]), (0, 128 - conv_w.shape[1])))
        self._rest_shards = shards[1:] + [jnp.stack([taps, jnp.zeros_like(taps)])]
        self._taps_shape = conv_w.shape
        self._groups = {}

    def w_in(self, after):
        landed = _exchange_wait("w_in_allgather_wait", _plan_first_hop, self._in_shard, self._in,
                                list(after) + self._rest_shards)
        (w_in,) = _allgather_finish("w_in_allgather_finish", self._in_shard, landed, [True])
        self._rest = _exchange_start("rest_allgather_start", _plan_first_hop, 3 * len(self._rest_shards),
                                     self._rest_shards, _like(self._rest_shards, (N_CHIPS, 2)), w_in)
        self.zero = self._rest[3]
        return w_in.reshape(N_CHIPS, 2 * w_in.shape[2], w_in.shape[3])

    def rest_weights(self, after):
        landed = _exchange_wait("rest_allgather_wait", _plan_first_hop, self._rest_shards, self._rest, after)
        kv, out, up, down, taps = _allgather_finish("rest_allgather_finish", self._rest_shards, landed,
                                                    [True, True, False, False, True])
        self._up_down = _exchange_start("up_down_pass_on_start", _plan_pass_on, 6, [], None, self.zero, lands=[up, down],
                                        peers="sibling")
        self.zero = self._up_down[3]
        k, w = self._taps_shape
        taps = taps[:, 0, :k, :w].transpose(1, 0, 2).reshape(k, N_CHIPS * w)
        return [g.reshape(N_CHIPS, 2 * g.shape[2], g.shape[3]) for g in (kv, out)], taps

    def up_down(self, after):
        full = _exchange_wait("up_down_pass_on_wait", _plan_pass_on, [], self._up_down, after)
        return [g.reshape(N_CHIPS, 2 * g.shape[2], g.shape[3]) for g in full]

    def send_grads(self, key, grads):
        grads = list(grads)
        started = _exchange_start(f"{key}_grads_to_sibling_start", _plan_other_half_to_sibling, len(grads), grads,
                                  _like(grads, (N_CHIPS,)), self.zero, peers="sibling")
        self._groups[key] = dict(grads=grads, to_sibling=started)
        self.zero = started[3]

    def grads_at_sibling(self, key, after):
        group = self._groups[key]
        grads = group["grads"]
        group["from_sibling"] = _exchange_wait(f"{key}_grads_to_sibling_wait", _plan_other_half_to_sibling, grads,
                                               group["to_sibling"], after)
        group["partials"] = _chip_sums_bf16(f"{key}_chip_sums", grads, group["from_sibling"], self.place)
        group["to_chips"] = _exchange_start(f"{key}_grads_to_chips_start", _plan_to_other_chips, 3 * len(grads),
                                            group["partials"], _like(group["partials"], (3,)), self.zero)
        self.zero = group["to_chips"][3]

    def grads_summed(self, key, after):
        group = self._groups[key]
        from_chips = _exchange_wait(f"{key}_grads_to_chips_wait", _plan_to_other_chips, group["partials"],
                                    group["to_chips"], after)
        return _final_sums(f"{key}_final_sums", group["grads"], group["from_sibling"], from_chips, self.place)

    def send_sums(self, key, sums):
        self._groups[key + "_sums"] = _exchange_start(f"{key}_sums_to_sibling_start", _plan_own_half_to_sibling,
                                                      len(sums), [], None, self.zero, lands=list(sums),
                                                      peers="sibling")
        self.zero = self._groups[key + "_sums"][3]

    def whole_sums(self, key, after):
        full = _exchange_wait(f"{key}_sums_to_sibling_wait", _plan_own_half_to_sibling, [], self._groups[key + "_sums"], after)
        return [t.reshape(2 * t.shape[1], t.shape[2]) for t in full]

    def send_small(self, block):
        self._small = block
        self._small_started = _exchange_start("small_grads_start", _plan_to_all, 7, [block],
                                              [jax.ShapeDtypeStruct((8,) + block.shape, block.dtype)], self.zero)
        self.zero = self._small_started[3]

    def small_summed(self, after):
        x, y, c = _place()
        (landed,) = _exchange_wait("small_grads_wait", _plan_to_all, [self._small], self._small_started, after)
        blocks = lax.dynamic_update_index_in_dim(landed, self._small, 4 * x + 2 * y + c, 0)
        return _sum_blocks("small_sum", blocks)


def _rope_tables(positions):
    half = HEAD // 2
    inv_freq = jnp.float32(ROPE_THETA) ** (-(jnp.arange(half, dtype=F32) * 2.0 / HEAD))
    ang = positions.astype(F32)[:, None] * inv_freq
    cos, sin = jnp.cos(ang), jnp.sin(ang)
    return jnp.tile(cos, (1, 4)), jnp.tile(jnp.concatenate([-sin, sin], axis=1), (1, 2))


def _local_step(x, mem, positions, target, gains, ex):
    g_pre_mix, g_mem, g_a, g_c, g_x, g_post_mix, g_pre_mlp, g_post_mlp = gains
    tm = ROW_TILE
    cos, sin = _rope_tables(positions)
    h = _pre_norm(x, g_pre_mix, ex.zero, tm)
    w_in = ex.w_in([h, cos, sin])

    q, k, v, bcu, qx = _in_proj_fwd(h, w_in, cos, sin, ex.zero, tm)
    ya, lse = _attn_fwd(q, k, v)
    (w_kv, w_out), conv_w = ex.rest_weights(lse)
    w_kv, w_out = (w.reshape(N_CHIPS * w.shape[1], w.shape[2]) for w in (w_kv, w_out))
    memn, mkv = _memkv_fwd(mem, g_mem, w_kv, ex.zero)
    yx, ycat, y2, x1 = _mix_fwd(ya, bcu, qx, mkv, conv_w, g_a, g_c, g_x, w_out, g_post_mix, x, tm)
    w_up, w_down = ex.up_down(x1)
    w_down = w_down.reshape(N_CHIPS * w_down.shape[1], w_down.shape[2])
    h2, f, du, df2, dx1, dg_pre_mlp, dg_post_mlp, loss = _mlp_fwd_bwd(x1, target, g_pre_mlp, g_post_mlp, w_up, w_down,
                                                                      MLP_ROW_TILE)
    gw_down = _weight_grad("grad_w_down", f, df2, True, ex.zero)
    gw_up = _weight_grad("grad_w_up", h2, du, False, ex.zero)
    ex.send_grads("early", [gw_up, gw_down])

    gw_out, dya, delta, tail, dmkv, g_conv, dg_post_mix, dg_a, dg_c, dg_x = _mixer_bwd(
        dx1, y2, ycat, ya, yx, bcu, qx, mkv, conv_w, g_a, g_c, g_x, w_out, g_post_mix, ex.zero, tm)
    ex.grads_at_sibling("early", dya)
    gw_kv, dg_mem = _memkv_bwd(mem, g_mem, w_kv, dmkv)
    ex.send_grads("mid", [gw_out, gw_kv])
    dqkv = _attn_bwd(q, k, v, dya, lse, delta, ex.zero)
    ex.grads_at_sibling("mid", dqkv[0])
    grad_x, gw_in, dg_pre_mix = _in_proj_bwd(dqkv, tail, cos, sin, w_in, x, h, g_pre_mix, dx1, ex.zero, tm)
    gain_grads = [dg_pre_mix, dg_mem, dg_a, dg_c, dg_x, dg_post_mix, dg_pre_mlp, dg_post_mlp]
    ex.send_small(_pack_small(gain_grads, g_conv, loss))
    ex.send_grads("late", [gw_in])
    return grad_x


def _pack_small(gains, conv, scalar):
    n, k = len(gains), conv.shape[0]

    def body(*refs):
        out_ref = refs[-1]
        out_ref[...] = jnp.zeros_like(out_ref)
        for i, g_ref in enumerate(refs[:n]):
            out_ref[i:i + 1, 0:g_ref.shape[1]] = g_ref[...]
        out_ref[n:n + k, 0:conv.shape[1]] = refs[n][...]
        out_ref[n + k:n + k + 1, 0:1] = refs[n + 1][...]

    return pl.pallas_call(body, name="pack_small", out_shape=jax.ShapeDtypeStruct((SMALL_ROWS, D_MODEL), F32))(
        *gains, conv, scalar)


def kernel(x, mem, positions, g_pre_mix, g_mem, w_in, w_mem_kv, conv_w, g_attn_out, g_conv_out, g_xattn_out, w_out, g_post_mix, g_pre_mlp, w_up, w_down, g_post_mlp, loss_target, m_g_pre_mix, m_g_mem, m_w_in, m_w_mem_kv, m_conv_w, m_g_attn_out, m_g_conv_out, m_g_xattn_out, m_w_out, m_g_post_mix, m_g_pre_mlp, m_w_up, m_w_down, m_g_post_mlp, v_g_pre_mix, v_g_mem, v_w_in, v_w_mem_kv, v_conv_w, v_g_attn_out, v_g_conv_out, v_g_xattn_out, v_w_out, v_g_post_mix, v_g_pre_mlp, v_w_up, v_w_down, v_g_post_mlp):
    chip = 2 * lax.axis_index("x") + lax.axis_index("y")
    gains = [g_pre_mix, g_mem, g_attn_out, g_conv_out, g_xattn_out, g_post_mix, g_pre_mlp, g_post_mlp]
    gains_m = [m_g_pre_mix, m_g_mem, m_g_attn_out, m_g_conv_out, m_g_xattn_out, m_g_post_mix, m_g_pre_mlp, m_g_post_mlp]
    gains_v = [v_g_pre_mix, v_g_mem, v_g_attn_out, v_g_conv_out, v_g_xattn_out, v_g_post_mix, v_g_pre_mlp, v_g_post_mlp]
    mats =[w_in[0], w_mem_kv[0], w_out[0], w_up[0], w_down[0]]
    mats_m = [m_w_in[0], m_w_mem_kv[0], m_w_out[0], m_w_up[0], m_w_down[0]]
    mats_v = [v_w_in[0], v_w_mem_kv[0], v_w_out[0], v_w_up[0], v_w_down[0]]

    ex = _StepExchanges(mats, conv_w[0])
    grad_x = _local_step(x[0], mem[0], positions[0], loss_target[0], gains, ex)

    ex.send_sums("four", ex.grads_summed("early", ex.zero) + ex.grads_summed("mid", ex.zero))
    ex.grads_at_sibling("late", ex.zero)
    up_sum, down_sum, out_sum, kv_sum = ex.whole_sums("four", ex.zero)
    params = lambda a, g: (mats[a], g, mats_m[a], mats_v[a])
    new_up, new_down = _adamw("adamw_up_down", [params(3, up_sum), params(4, down_sum)], ex.zero)
    new_out, new_kv = _adamw("adamw_out_kv", [params(2, out_sum), params(1, kv_sum)], ex.zero)

    small, total = _small_update(ex.small_summed(new_kv[1]), chip.reshape(1).astype(jnp.int32), gains, gains_m,
                                 gains_v, conv_w[0], m_conv_w[0], v_conv_w[0])

    ex.send_sums("last", ex.grads_summed("late", small[0][1]))
    (in_sum,) = ex.whole_sums("last", ex.zero)
    (new_in,) = _adamw("adamw_in", [params(0, in_sum)], in_sum)
    mat_new = [new_in, new_kv, new_out, new_up, new_down]

    order = ["g_pre_mix", "g_mem", "w_in", "w_mem_kv", "conv_w", "g_attn_out", "g_conv_out", "g_xattn_out", "w_out",
             "g_post_mix", "g_pre_mlp", "w_up", "w_down", "g_post_mlp"]
    gain_names = ["g_pre_mix", "g_mem", "g_attn_out", "g_conv_out", "g_xattn_out", "g_post_mix", "g_pre_mlp", "g_post_mlp"]
    mat_names = ["w_in", "w_mem_kv", "w_out", "w_up", "w_down"]

    def leaf(kind, name):
        if name in gain_names:
            return small[gain_names.index(name)][kind]
        if name == "conv_w":
            return jnp.swapaxes(small[len(gain_names)][kind], 0, 1)
        return mat_new[mat_names.index(name)][kind][None]

    return (total[0, 0], grad_x[None], *[leaf(kind, name) for kind in range(4) for name in order])
```

```python
import jax
import jax.numpy as jnp
from jax import lax
from jax.experimental import pallas as pl
from jax.experimental.pallas import tpu as pltpu

F32, BF16 = jnp.float32, jnp.bfloat16

D_MODEL = 1024
ATTN_W = 512
CONV_W = 256
XATTN_W = 256
PROJ_W = 3 * ATTN_W + 3 * CONV_W + XATTN_W
D_FF = 4096
HEAD = 64
N_BACK = 128
DILATIONS = (1, 4, 16)
PATTERN_ORDER = DILATIONS[::-1]
ROPE_THETA = 10000.0
EPS = 1e-6
NEG_INF = -1e30
SCALE = HEAD ** -0.5
N_CHIPS = 4
SHARD_IN = PROJ_W // N_CHIPS
SHARD_FF = D_FF // N_CHIPS

ADAM_LR, ADAM_B1, ADAM_B2, ADAM_EPS, ADAM_WD, ADAM_STEP = 0.001, 0.9, 0.999, 1e-08, 0.01, 10

VMEM_LIMIT_V7X = 56 * 1024 * 1024
ROW_TILE = 512
MLP_ROW_TILE = 256
ADAMW_ROW_TILE = 256
SMALL_ROWS = 16

NT = (((1,), (1,)), ((), ()))
TN = (((0,), (0,)), ((), ()))
MESH = pl.DeviceIdType.MESH


def _params(*sem):
    return pltpu.CompilerParams(dimension_semantics=sem, vmem_limit_bytes=VMEM_LIMIT_V7X)


def _resident(shape):
    return pl.BlockSpec(shape, lambda *_: (0,) * len(shape), pipeline_mode=pl.Buffered(1))


def _rows(tm, width):
    return pl.BlockSpec((tm, width), lambda i: (i, 0))


def _rms_hat(x):
    r = lax.rsqrt(jnp.mean(x * x, axis=-1, keepdims=True) + EPS)
    return x * r, r


def _rms_bwd(xhat, r, g, dy):
    gdy = dy * g
    return r * (gdy - xhat * jnp.mean(xhat * gdy, axis=-1, keepdims=True))


def _rope128(t, cos, sin_signed, inverse):
    lane = lax.broadcasted_iota(jnp.int32, t.shape, 1)
    first_half = (lane % HEAD) < (HEAD // 2)
    rot = jnp.where(first_half, pltpu.roll(t, 128 - HEAD // 2, 1), pltpu.roll(t, HEAD // 2, 1))
    return t * cos - rot * sin_signed if inverse else t * cos + rot * sin_signed


def _pre_norm(x, g, after, tm):
    S = x.shape[0]

    def body(x_ref, g_ref, after_ref, h_ref):
        h_ref[...] = (_rms_hat(x_ref[...])[0] * g_ref[...]).astype(BF16)

    return pl.pallas_call(
        body, name="pre_norm", grid=(S // tm,),
        in_specs=[_rows(tm, D_MODEL), _resident((1, D_MODEL)), pl.BlockSpec(memory_space=pl.ANY)],
        out_specs=_rows(tm, D_MODEL), out_shape=jax.ShapeDtypeStruct((S, D_MODEL), BF16),
        compiler_params=_params("parallel"),
    )(x, g, after)


def _side_by_side(w_hbm, w_full, sems):
    width = w_hbm.shape[2]

    @pl.when(pl.program_id(0) == 0)
    def _():
        copies = [pltpu.make_async_copy(w_hbm.at[j], w_full.at[:, pl.ds(width * j, width)], sems.at[j])
                  for j in range(N_CHIPS)]
        for cp in copies:
            cp.start()
        for cp in copies:
            cp.wait()


def _in_proj_fwd(h, w_in, cos, sin, after, tm):
    S = h.shape[0]

    def body(h_ref, w_hbm, cos_ref, sin_ref, after_ref, q_ref, k_ref, v_ref, bcu_ref, qx_ref, proj, w_full, sems):
        _side_by_side(w_hbm, w_full, sems)
        proj[...] = jnp.dot(h_ref[...], w_full[...], preferred_element_type=F32)
        c, s = cos_ref[...], sin_ref[...]
        for j in range(ATTN_W // 128):
            lo = 128 * j
            q_ref[:, lo:lo + 128] = _rope128(proj[:, lo:lo + 128], c, s, False) * SCALE
            k_ref[:, lo:lo + 128] = _rope128(proj[:, ATTN_W + lo:ATTN_W + lo + 128], c, s, False)
        v_ref[...] = proj[:, 2 * ATTN_W:3 * ATTN_W]
        bcu_ref[...] = proj[:, 3 * ATTN_W:3 * ATTN_W + 3 * CONV_W]
        qx_ref[...] = proj[:, 3 * ATTN_W + 3 * CONV_W:PROJ_W].astype(BF16)

    return pl.pallas_call(
        body, name="in_proj_fwd", grid=(S // tm,),
        in_specs=[_rows(tm, D_MODEL), pl.BlockSpec(memory_space=pl.ANY), _rows(tm, 128), _rows(tm, 128),
                  pl.BlockSpec(memory_space=pl.ANY)],
        out_specs=[_rows(tm, ATTN_W), _rows(tm, ATTN_W), _rows(tm, ATTN_W), _rows(tm, 3 * CONV_W), _rows(tm, XATTN_W)],
        out_shape=[jax.ShapeDtypeStruct((S, ATTN_W), F32), jax.ShapeDtypeStruct((S, ATTN_W), F32),
                   jax.ShapeDtypeStruct((S, ATTN_W), F32), jax.ShapeDtypeStruct((S, 3 * CONV_W), F32),
                   jax.ShapeDtypeStruct((S, XATTN_W), BF16)],
        scratch_shapes=[pltpu.VMEM((tm, PROJ_W), F32), pltpu.VMEM((D_MODEL, PROJ_W), BF16),
                        pltpu.SemaphoreType.DMA((N_CHIPS,))],
        compiler_params=_params("arbitrary"),
    )(h, w_in, cos, sin, after)


def _memkv_fwd(mem, g_mem, w_kv, after):
    n_mem = mem.shape[0]

    def body(mem_ref, g_ref, w_ref, after_ref, mn_ref, kv_ref):
        mhat, _ = _rms_hat(mem_ref[...])
        mn = (mhat * g_ref[...]).astype(BF16)
        mn_ref[...] = mn
        kv_ref[...] = jnp.dot(mn, w_ref[...], preferred_element_type=F32).astype(BF16)

    vmem = pl.BlockSpec(memory_space=pltpu.VMEM)
    return pl.pallas_call(
        body, name="memkv_fwd", in_specs=[vmem, vmem, vmem, pl.BlockSpec(memory_space=pl.ANY)], out_specs=[vmem, vmem],
        out_shape=[jax.ShapeDtypeStruct((n_mem, D_MODEL), BF16), jax.ShapeDtypeStruct((n_mem, 2 * XATTN_W), BF16)],
        compiler_params=pltpu.CompilerParams(vmem_limit_bytes=VMEM_LIMIT_V7X),
    )(mem, g_mem, w_kv, after)


def _fill_band_bias(bias):
    row = lax.broadcasted_iota(jnp.int32, (N_BACK, 2 * N_BACK), 0)
    col = lax.broadcasted_iota(jnp.int32, (N_BACK, 2 * N_BACK), 1)
    band = (col >= row) & (col <= row + N_BACK)
    bias[1] = jnp.where(band, 0.0, NEG_INF)
    bias[0] = jnp.where(band & (col >= N_BACK), 0.0, NEG_INF)


def _strided(start, size, d):
    return pl.ds(start, size) if d == 1 else pl.ds(start, size, stride=d)


def _group_starts(g, G, nb, d):
    t0 = g * G
    r, n0 = lax.shift_right_logical(t0, nb.bit_length() - 1), lax.bitwise_and(t0, nb - 1)
    first = r + n0 * (N_BACK * d)
    before = r + jnp.maximum(n0 - 1, 0) * (N_BACK * d)
    starts = [before] + [first + u * (N_BACK * d) for u in range(G)]
    if d == 1:
        starts = [pl.multiple_of(st, N_BACK) for st in starts]
    return starts, n0


def _step_blocks(i, U, nb, d):
    G = min(U, nb)
    whole = G == nb
    row_blocks, blocks = [], []
    for grp in range(U // G):
        starts, n0 = _group_starts(i * (U // G) + grp, G, nb, d)
        base = len(row_blocks)
        if whole:
            row_blocks += [_strided(st, N_BACK, d) for st in starts[1:]]
            blocks += [(base + max(u - 1, 0), base + u, min(u, 1)) for u in range(G)]
        else:
            row_blocks += [_strided(st, N_BACK, d) for st in starts]
            blocks += [(base + u, base + u + 1, jnp.minimum(n0, 1) if u == 0 else 1) for u in range(G)]
    return row_blocks, blocks


def _by_head(a, b):
    lane = lax.broadcasted_iota(jnp.int32, (a.shape[0], 2 * HEAD), 1)
    return jnp.where(lane < HEAD, a, b)


def _head_only(t, hh):
    lane = lax.broadcasted_iota(jnp.int32, t.shape, 1)
    return jnp.where((lane < HEAD) == (hh == 0), t, jnp.zeros_like(t))


def _stack_heads(t):
    return jnp.concatenate([_head_only(t, 0), _head_only(t, 1)], axis=0)


def _head_columns(t):
    return jnp.concatenate([t[:, 0:1], t[:, HEAD:HEAD + 1]], axis=0)


def _unstack(t):
    return _by_head(t[:N_BACK], t[N_BACK:])


def _unstack_columns(t):
    return _by_head(jnp.broadcast_to(t[:N_BACK], (N_BACK, 2 * HEAD)), jnp.broadcast_to(t[N_BACK:], (N_BACK, 2 * HEAD)))


FWD_BLOCKS_PER_STEP = 4
BWD_BLOCKS_PER_STEP = 4
BWD_CHUNK = 64


def _attn_fwd(q, k, v):
    S = q.shape[0]
    U = FWD_BLOCKS_PER_STEP

    def body(q_ref, k_ref, v_ref, y_ref, m_ref, l_scr, bias):
        _fill_band_bias(bias)
        for g, d in enumerate(PATTERN_ORDER):
            nb = S // d // N_BACK
            first_pattern, last_pattern = g == 0, g == len(PATTERN_ORDER) - 1

            def step(i, carry, d=d, nb=nb, first_pattern=first_pattern, last_pattern=last_pattern):
                row_blocks, blocks = _step_blocks(i, U, nb, d)
                kb = [k_ref[r, :].astype(BF16) for r in row_blocks]
                ss = []
                for before, own, which in blocks:
                    kw = jnp.concatenate([kb[before], kb[own]], 0)
                    qs = _stack_heads(q_ref[row_blocks[own], :].astype(BF16))
                    b = bias[which]
                    ss.append(lax.dot_general(qs, kw, NT, preferred_element_type=F32) + jnp.concatenate([b, b], axis=0))
                ms = [jnp.max(s, axis=1, keepdims=True) for s in ss]
                ps = [jnp.exp(s - m) for s, m in zip(ss, ms)]
                ls = [jnp.sum(p, axis=1, keepdims=True) for p in ps]
                vb = [v_ref[r, :].astype(BF16) for r in row_blocks]
                os_ = [jnp.dot(ps[u].astype(BF16), jnp.concatenate([vb[before], vb[own]], 0), preferred_element_type=F32)
                       for u, (before, own, _) in enumerate(blocks)]
                for u, (_, own, _) in enumerate(blocks):
                    o_g, m_g, l_g = _unstack(os_[u]), _unstack_columns(ms[u]), _unstack_columns(ls[u])
                    r = row_blocks[own]
                    if first_pattern:
                        m_new, l_new, acc = m_g, l_g, o_g
                    else:
                        m_old = m_ref[r, :]
                        m_new = jnp.maximum(m_old, m_g)
                        alpha, beta = jnp.exp(m_old - m_new), jnp.exp(m_g - m_new)
                        l_new = l_scr[r, :] * alpha + l_g * beta
                        acc = y_ref[r, :] * alpha + o_g * beta
                    if last_pattern:
                        y_ref[r, :] = acc / l_new
                        m_ref[r, :] = m_new + jnp.log(l_new)
                    else:
                        y_ref[r, :] = acc
                        m_ref[r, :] = m_new
                        l_scr[r, :] = l_new
                return carry

            lax.fori_loop(0, d * nb // U, step, 0)

    col = pl.BlockSpec((S, 2 * HEAD), lambda j: (0, j))
    return pl.pallas_call(
        body, name="attn_fwd", grid=(q.shape[1] // (2 * HEAD),),
        in_specs=[col, col, col], out_specs=[col, col],
        out_shape=[jax.ShapeDtypeStruct(q.shape, F32)] * 2,
        scratch_shapes=[pltpu.VMEM((S, 2 * HEAD), F32), pltpu.VMEM((2, N_BACK, 2 * N_BACK), F32)],
        compiler_params=_params("parallel"),
    )(q, k, v)


def _attn_bwd(q, k, v, dy, lse, delta, after):
    S = q.shape[0]
    U = BWD_BLOCKS_PER_STEP

    def body(q_ref, k_ref, v_ref, dy_ref, lse_ref, delta_ref, after_ref, dq_ref, dk_ref, dv_ref, bias):
        _fill_band_bias(bias)
        nb_first = S // PATTERN_ORDER[0] // N_BACK
        first_writes_all = min(U, nb_first) == nb_first
        if not first_writes_all:
            dk_ref[...] = jnp.zeros_like(dk_ref)
            dv_ref[...] = jnp.zeros_like(dv_ref)
        for g, d in enumerate(PATTERN_ORDER):
            nb = S // d // N_BACK

            def step(i, carry, d=d, nb=nb, g=g):
                row_blocks, blocks = _step_blocks(i, U, nb, d)
                kb = [k_ref[r, :].astype(BF16) for r in row_blocks]
                vb = [v_ref[r, :].astype(BF16) for r in row_blocks]
                kws = [jnp.concatenate([kb[before], kb[own]], 0) for before, own, _ in blocks]
                vws = [jnp.concatenate([vb[before], vb[own]], 0) for before, own, _ in blocks]
                qss = [_stack_heads(q_ref[row_blocks[own], :].astype(BF16)) for _, own, _ in blocks]
                doss = [_stack_heads(dy_ref[row_blocks[own], :].astype(BF16)) for _, own, _ in blocks]
                ss = [lax.dot_general(qss[u], kws[u], NT, preferred_element_type=F32) for u in range(U)]
                dps = [lax.dot_general(doss[u], vws[u], NT, preferred_element_type=F32) for u in range(U)]
                pbs, dss = [], []
                for u, (_, own, which) in enumerate(blocks):
                    lse_c = _head_columns(lse_ref[row_blocks[own], :])
                    delta_c = _head_columns(delta_ref[row_blocks[own], :])
                    p_parts, ds_parts = [], []
                    for r0 in range(0, 2 * N_BACK, BWD_CHUNK):
                        r = slice(r0, r0 + BWD_CHUNK)
                        mask = bias[which, r0 % N_BACK:r0 % N_BACK + BWD_CHUNK, :]
                        p_r = jnp.exp(ss[u][r] + mask - lse_c[r])
                        p_parts.append(p_r.astype(BF16))
                        ds_parts.append((p_r * (dps[u][r] - delta_c[r])).astype(BF16))
                    pbs.append(jnp.concatenate(p_parts, axis=0))
                    dss.append(jnp.concatenate(ds_parts, axis=0))
                dqs = [jnp.dot(dss[u], kws[u], preferred_element_type=F32) for u in range(U)]
                dkws = [lax.dot_general(dss[u], qss[u], TN, preferred_element_type=F32) for u in range(U)]
                dvws = [lax.dot_general(pbs[u], doss[u], TN, preferred_element_type=F32) for u in range(U)]
                dk_parts, dv_parts = [None] * len(row_blocks), [None] * len(row_blocks)
                for u, (before, own, _) in enumerate(blocks):
                    dq = _unstack(dqs[u])
                    if g == 0:
                        dq_ref[row_blocks[own], :] = dq
                    else:
                        dq_ref[row_blocks[own], :] += dq
                    for idx, dkp, dvp in ((before, dkws[u][:N_BACK], dvws[u][:N_BACK]),
                                          (own, dkws[u][N_BACK:], dvws[u][N_BACK:])):
                        dk_parts[idx] = dkp if dk_parts[idx] is None else dk_parts[idx] + dkp
                        dv_parts[idx] = dvp if dv_parts[idx] is None else dv_parts[idx] + dvp
                for idx, r in enumerate(row_blocks):
                    if g == 0 and first_writes_all:
                        dk_ref[r, :] = dk_parts[idx]
                        dv_ref[r, :] = dv_parts[idx]
                    else:
                        dk_ref[r, :] += dk_parts[idx]
                        dv_ref[r, :] += dv_parts[idx]
                return carry

            lax.fori_loop(0, d * nb // U, step, 0)

    col = pl.BlockSpec((S, 2 * HEAD), lambda j: (0, j))
    return pl.pallas_call(
        body, name="attn_bwd", grid=(q.shape[1] // (2 * HEAD),),
        in_specs=[col] * 6 + [pl.BlockSpec(memory_space=pl.ANY)], out_specs=[col] * 3,
        out_shape=[jax.ShapeDtypeStruct(q.shape, F32)] * 3,
        scratch_shapes=[pltpu.VMEM((2, N_BACK, 2 * N_BACK), F32)],
        compiler_params=_params("parallel"),
    )(q, k, v, dy, lse, delta, after)


def _shift_down(z, before, k):
    row = lax.broadcasted_iota(jnp.int32, z.shape, 0)
    out = pltpu.roll(z, k, 0)
    for i in range(k):
        out = jnp.where(row == i, before[8 - k + i:8 - k + i + 1, :], out)
    return out


def _shift_up(z, after, k):
    rows = z.shape[0]
    row = lax.broadcasted_iota(jnp.int32, z.shape, 0)
    out = pltpu.roll(z, rows - k, 0)
    for i in range(k):
        out = jnp.where(row == rows - k + i, after[i:i + 1, :], out)
    return out


def _conv_fwd(bcu, before, is_first, w):
    b, c, u = bcu[:, 0:CONV_W], bcu[:, CONV_W:2 * CONV_W], bcu[:, 2 * CONV_W:3 * CONV_W]
    z = c * u
    zb = jnp.where(is_first, 0.0, before[:, CONV_W:2 * CONV_W] * before[:, 2 * CONV_W:3 * CONV_W])
    z1, z2 = _shift_down(z, zb, 1), _shift_down(z, zb, 2)
    cv = w[0:1, :] * z2 + w[1:2, :] * z1 + w[2:3, :] * z
    return b, c, u, z, z1, z2, cv


def _halo_before(tm, width):
    return pl.BlockSpec((8, width), lambda i: (jnp.maximum(i * (tm // 8) - 1, 0), 0))


def _mix_fwd(ya, bcu, qx, mkv, conv_w, g_a, g_c, g_x, w_out, g_post, x, tm):
    S = x.shape[0]

    def body(ya_ref, bcu_ref, before_ref, qx_ref, mkv_ref, cw_ref, ga_ref, gc_ref, gx_ref,
             wo_ref, gp_ref, x_ref, yx_ref, ycat_ref, y2_ref, x1_ref):
        ya = ya_ref[...]
        b, _, _, _, _, _, cv = _conv_fwd(bcu_ref[...], before_ref[...], pl.program_id(0) == 0, cw_ref[...])
        yc = b * cv

        qxb, mkvb = qx_ref[...], mkv_ref[...]
        heads = [slice(HEAD * hd, HEAD * (hd + 1)) for hd in range(XATTN_W // HEAD)]
        ss = [lax.dot_general(qxb[:, sl], mkvb[:, sl], NT, preferred_element_type=F32) * SCALE for sl in heads]
        ms = [jnp.max(s, axis=1, keepdims=True) for s in ss]
        ps = [jnp.exp(s - m) for s, m in zip(ss, ms)]
        ls = [jnp.sum(p, axis=1, keepdims=True) for p in ps]
        os_ = [jnp.dot(p.astype(BF16), mkvb[:, XATTN_W + sl.start:XATTN_W + sl.stop], preferred_element_type=F32)
               for p, sl in zip(ps, heads)]
        for sl, o, l in zip(heads, os_, ls):
            yx_ref[:, sl] = o / l
        yx = yx_ref[...]

        ycat_ref[:, 0:ATTN_W] = (_rms_hat(ya)[0] * ga_ref[...]).astype(BF16)
        ycat_ref[:, ATTN_W:ATTN_W + CONV_W] = (_rms_hat(yc)[0] * gc_ref[...]).astype(BF16)
        ycat_ref[:, ATTN_W + CONV_W:D_MODEL] = (_rms_hat(yx)[0] * gx_ref[...]).astype(BF16)
        y2 = jnp.dot(ycat_ref[...], wo_ref[...], preferred_element_type=F32)
        y2_ref[...] = y2
        x1_ref[...] = x_ref[...] + _rms_hat(y2)[0] * gp_ref[...]

    n_mem = mkv.shape[0]
    return pl.pallas_call(
        body, name="mix_fwd", grid=(S // tm,),
        in_specs=[_rows(tm, ATTN_W), _rows(tm, 3 * CONV_W), _halo_before(tm, 3 * CONV_W), _rows(tm, XATTN_W),
                  _resident((n_mem, 2 * XATTN_W)), _resident((3, CONV_W)), _resident((1, ATTN_W)),
                  _resident((1, CONV_W)), _resident((1, XATTN_W)), _resident((D_MODEL, D_MODEL)),
                  _resident((1, D_MODEL)), _rows(tm, D_MODEL)],
        out_specs=[_rows(tm, XATTN_W), _rows(tm, D_MODEL), _rows(tm, D_MODEL), _rows(tm, D_MODEL)],
        out_shape=[jax.ShapeDtypeStruct((S, XATTN_W), F32), jax.ShapeDtypeStruct((S, D_MODEL), BF16),
                   jax.ShapeDtypeStruct((S, D_MODEL), F32), jax.ShapeDtypeStruct((S, D_MODEL), F32)],
        compiler_params=_params("parallel"),
    )(ya, bcu, bcu, qx, mkv, conv_w, g_a, g_c, g_x, w_out, g_post, x)


def _mlp_fwd_bwd(x1, target, g_pre, g_post, w_up, w_down, tm):
    S = x1.shape[0]
    n_ff = D_FF // SHARD_FF

    def body(x1_ref, t_ref, gpre_ref, gpost_ref, wup_ref, wdn_ref,
             h2_ref, f_ref, du_ref, df2_ref, dx1_ref, dgpre_ref, dgpost_ref, loss_ref, u_scr):
        @pl.when(pl.program_id(0) == 0)
        def _():
            dgpre_ref[...] = jnp.zeros_like(dgpre_ref)
            dgpost_ref[...] = jnp.zeros_like(dgpost_ref)
            loss_ref[...] = jnp.zeros_like(loss_ref)

        x1 = x1_ref[...]
        x1hat, r1 = _rms_hat(x1)
        h2 = (x1hat * gpre_ref[...]).astype(BF16)
        h2_ref[...] = h2
        f2 = jnp.zeros((tm, D_MODEL), F32)
        for j in range(n_ff):
            cols = slice(SHARD_FF * j, SHARD_FF * (j + 1))
            u = jnp.maximum(jnp.dot(h2, wup_ref[j], preferred_element_type=F32), 0.0)
            u_scr[:, cols] = u
            f = (u * u).astype(BF16)
            f_ref[:, cols] = f
            f2 = f2 + jnp.dot(f, wdn_ref[cols, :], preferred_element_type=F32)
        f2hat, r2 = _rms_hat(f2)
        err = x1 + f2hat * gpost_ref[...] - t_ref[...]
        loss_ref[...] += 0.5 * jnp.sum(jnp.mean(err * err, axis=-1, keepdims=True), axis=0, keepdims=True)
        dx2 = err * (1.0 / D_MODEL)
        dgpost_ref[...] += jnp.sum(dx2 * f2hat, axis=0, keepdims=True)
        df2 = _rms_bwd(f2hat, r2, gpost_ref[...], dx2).astype(BF16)
        df2_ref[...] = df2
        dh2 = jnp.zeros((tm, D_MODEL), F32)
        for j in range(n_ff):
            cols = slice(SHARD_FF * j, SHARD_FF * (j + 1))
            df = lax.dot_general(df2, wdn_ref[cols, :], NT, preferred_element_type=F32)
            du = (2.0 * u_scr[:, cols] * df).astype(BF16)
            du_ref[:, cols] = du
            dh2 = dh2 + lax.dot_general(du, wup_ref[j], NT, preferred_element_type=F32)
        dgpre_ref[...] += jnp.sum(dh2 * x1hat, axis=0, keepdims=True)
        dx1_ref[...] = dx2 + _rms_bwd(x1hat, r1, gpre_ref[...], dh2)

    acc = pl.BlockSpec((1, D_MODEL), lambda i: (0, 0))
    return pl.pallas_call(
        body, name="mlp_fwd_bwd", grid=(S // tm,),
        in_specs=[_rows(tm, D_MODEL), _rows(tm, D_MODEL), _resident((1, D_MODEL)), _resident((1, D_MODEL)),
                  _resident((n_ff, D_MODEL, SHARD_FF)), _resident((D_FF, D_MODEL))],
        out_specs=[_rows(tm, D_MODEL), _rows(tm, D_FF), _rows(tm, D_FF), _rows(tm, D_MODEL), _rows(tm, D_MODEL),
                   acc, acc, pl.BlockSpec((1, 1), lambda i: (0, 0))],
        out_shape=[jax.ShapeDtypeStruct((S, D_MODEL), BF16), jax.ShapeDtypeStruct((S, D_FF), BF16),
                   jax.ShapeDtypeStruct((S, D_FF), BF16), jax.ShapeDtypeStruct((S, D_MODEL), BF16),
                   jax.ShapeDtypeStruct((S, D_MODEL), F32), jax.ShapeDtypeStruct((1, D_MODEL), F32),
                   jax.ShapeDtypeStruct((1, D_MODEL), F32), jax.ShapeDtypeStruct((1, 1), F32)],
        scratch_shapes=[pltpu.VMEM((tm, D_FF), F32)],
        compiler_params=_params("arbitrary"),
    )(x1, target, g_pre, g_post, w_up, w_down)


def _weight_grad(name, a, b, rows_sharded, after):
    S, K = a.shape
    N = b.shape[1]
    if rows_sharded:
        tk, tn = K // N_CHIPS, N
        a_spec = pl.BlockSpec((S, tk), lambda j: (0, j))
        b_spec = pl.BlockSpec((S, tn), lambda j: (0, 0), pipeline_mode=pl.Buffered(1))
    else:
        tk, tn = K, N // N_CHIPS
        a_spec = pl.BlockSpec((S, tk), lambda j: (0, 0), pipeline_mode=pl.Buffered(1))
        b_spec = pl.BlockSpec((S, tn), lambda j: (0, j))
    half = tk // 2

    def body(a_ref, b_ref, after_ref, o_ref):
        res = lax.dot_general(a_ref[...], b_ref[...], TN, preferred_element_type=F32)
        o_ref[0, 0] = res[:half]
        o_ref[1, 0] = res[half:]

    return pl.pallas_call(
        body, name=name, grid=(N_CHIPS,), in_specs=[a_spec, b_spec, pl.BlockSpec(memory_space=pl.ANY)],
        out_specs=pl.BlockSpec((2, 1, half, tn), lambda j: (0, j, 0, 0)),
        out_shape=jax.ShapeDtypeStruct((2, N_CHIPS, half, tn), F32),
        compiler_params=_params("parallel"),
    )(a, b, after)


def _mixer_bwd(dx1, y2, ycat, ya, yx, bcu, qx, mkv, conv_w, g_a, g_c, g_x, w_out, g_post, after, tm):
    S = dx1.shape[0]
    n_mem = mkv.shape[0]
    n_tiles = S // tm
    half = D_MODEL // N_CHIPS // 2

    def body(dx1_ref, y2_ref, ycat_ref, ya_ref, yx_ref, bcu_ref, before_ref, qx_ref, mkv_ref, cw_ref, ga_ref, gc_ref,
             gx_ref, wo_ref, gp_ref, after_ref, gwo_ref, dya_ref, delta_ref, tail_ref, dmkv_ref, dcw_ref, dgp_ref,
             dga_ref, dgc_ref, dgx_ref, carry):
        step = pl.program_id(0)
        first_tile = step == n_tiles - 1

        @pl.when(step == 0)
        def _():
            for ref in (gwo_ref, dmkv_ref, dcw_ref, dgp_ref, dga_ref, dgc_ref, dgx_ref, carry):
                ref[...] = jnp.zeros_like(ref)

        dx1 = dx1_ref[...]
        y2hat, r2 = _rms_hat(y2_ref[...])
        dgp_ref[...] += jnp.sum(dx1 * y2hat, axis=0, keepdims=True)
        dy2 = _rms_bwd(y2hat, r2, gp_ref[...], dx1).astype(BF16)
        gwo = lax.dot_general(ycat_ref[...], dy2, TN, preferred_element_type=F32)
        for k in range(2 * N_CHIPS):
            gwo_ref[k % 2, k // 2] += gwo[half * k:half * (k + 1)]
        dycat = lax.dot_general(dy2, wo_ref[...], NT, preferred_element_type=F32)

        d_na = dycat[:, 0:ATTN_W]
        ya = ya_ref[...]
        yahat, ra = _rms_hat(ya)
        dga_ref[...] += jnp.sum(d_na * yahat, axis=0, keepdims=True)
        dya = _rms_bwd(yahat, ra, ga_ref[...], d_na)
        dya_ref[...] = dya
        prod = dya * ya
        hi = prod.astype(BF16)
        lo = (prod - hi.astype(F32)).astype(BF16)
        head_of = lambda axis: lax.shift_right_logical(lax.broadcasted_iota(jnp.int32, (ATTN_W, ATTN_W), axis),
                                                       HEAD.bit_length() - 1)
        ones = jnp.where(head_of(0) == head_of(1), 1.0, 0.0).astype(BF16)
        delta_ref[...] = jnp.dot(hi, ones, preferred_element_type=F32) + jnp.dot(lo, ones, preferred_element_type=F32)

        w = cw_ref[...]
        b, c, u, z, z1, z2, cv = _conv_fwd(bcu_ref[...], before_ref[...], first_tile, w)
        d_nc = dycat[:, ATTN_W:ATTN_W + CONV_W]
        ychat, rc = _rms_hat(b * cv)
        dgc_ref[...] += jnp.sum(d_nc * ychat, axis=0, keepdims=True)
        dyc = _rms_bwd(ychat, rc, gc_ref[...], d_nc)
        dcv = dyc * b
        behind = carry[...]
        dz = w[2:3, :] * dcv + w[1:2, :] * _shift_up(dcv, behind, 1) + w[0:1, :] * _shift_up(dcv, behind, 2)
        carry[...] = dcv[0:8, :]
        dcw_ref[0:1, :] += jnp.sum(dcv * z2, axis=0, keepdims=True)
        dcw_ref[1:2, :] += jnp.sum(dcv * z1, axis=0, keepdims=True)
        dcw_ref[2:3, :] += jnp.sum(dcv * z, axis=0, keepdims=True)
        tail_ref[:, 0:CONV_W] = (dyc * cv).astype(BF16)
        tail_ref[:, CONV_W:2 * CONV_W] = (dz * u).astype(BF16)
        tail_ref[:, 2 * CONV_W:3 * CONV_W] = (dz * c).astype(BF16)

        d_nx = dycat[:, ATTN_W + CONV_W:D_MODEL]
        yxhat, rx = _rms_hat(yx_ref[...])
        dgx_ref[...] += jnp.sum(d_nx * yxhat, axis=0, keepdims=True)
        dyx = _rms_bwd(yxhat, rx, gx_ref[...], d_nx)
        qxb, mkvb = qx_ref[...], mkv_ref[...]
        heads = [slice(HEAD * hd, HEAD * (hd + 1)) for hd in range(XATTN_W // HEAD)]
        values = [slice(XATTN_W + sl.start, XATTN_W + sl.stop) for sl in heads]
        ss = [lax.dot_general(qxb[:, sl], mkvb[:, sl], NT, preferred_element_type=F32) * SCALE for sl in heads]
        es = [jnp.exp(s - jnp.max(s, axis=1, keepdims=True)) for s in ss]
        ps = [e / jnp.sum(e, axis=1, keepdims=True) for e in es]
        dobs = [dyx[:, sl].astype(BF16) for sl in heads]
        dps = [lax.dot_general(dob, mkvb[:, vsl], NT, preferred_element_type=F32) for dob, vsl in zip(dobs, values)]
        dss = [(p * (dp - jnp.sum(p * dp, axis=1, keepdims=True)) * SCALE).astype(BF16) for p, dp in zip(ps, dps)]
        for sl, vsl, p, dob, ds in zip(heads, values, ps, dobs, dss):
            tail_ref[:, 3 * CONV_W + sl.start:3 * CONV_W + sl.stop] = jnp.dot(
                ds, mkvb[:, sl], preferred_element_type=F32).astype(BF16)
            dmkv_ref[:, sl] += lax.dot_general(ds, qxb[:, sl], TN, preferred_element_type=F32)
            dmkv_ref[:, vsl] += lax.dot_general(p.astype(BF16), dob, TN, preferred_element_type=F32)

    rows = lambda width: pl.BlockSpec((tm, width), lambda i: (n_tiles - 1 - i, 0))
    before = pl.BlockSpec((8, 3 * CONV_W), lambda i: (jnp.maximum((n_tiles - 1 - i) * (tm // 8) - 1, 0), 0))
    acc = lambda r, w: pl.BlockSpec((r, w), lambda i: (0, 0))
    return pl.pallas_call(
        body, name="mixer_bwd", grid=(n_tiles,),
        in_specs=[rows(D_MODEL), rows(D_MODEL), rows(D_MODEL), rows(ATTN_W), rows(XATTN_W), rows(3 * CONV_W), before,
                  rows(XATTN_W), _resident((n_mem, 2 * XATTN_W)), _resident((3, CONV_W)), _resident((1, ATTN_W)),
                  _resident((1, CONV_W)), _resident((1, XATTN_W)), _resident((D_MODEL, D_MODEL)),
                  _resident((1, D_MODEL)), pl.BlockSpec(memory_space=pl.ANY)],
        out_specs=[pl.BlockSpec((2, N_CHIPS, half, D_MODEL), lambda i: (0, 0, 0, 0)), rows(ATTN_W), rows(ATTN_W),
                   rows(3 * CONV_W + XATTN_W), acc(n_mem, 2 * XATTN_W),
                   acc(3, CONV_W), acc(1, D_MODEL), acc(1, ATTN_W), acc(1, CONV_W), acc(1, XATTN_W)],
        out_shape=[jax.ShapeDtypeStruct((2, N_CHIPS, half, D_MODEL), F32), jax.ShapeDtypeStruct((S, ATTN_W), F32),
                   jax.ShapeDtypeStruct((S, ATTN_W), F32), jax.ShapeDtypeStruct((S, 3 * CONV_W + XATTN_W), BF16),
                   jax.ShapeDtypeStruct((n_mem, 2 * XATTN_W), F32), jax.ShapeDtypeStruct((3, CONV_W), F32),
                   jax.ShapeDtypeStruct((1, D_MODEL), F32), jax.ShapeDtypeStruct((1, ATTN_W), F32),
                   jax.ShapeDtypeStruct((1, CONV_W), F32), jax.ShapeDtypeStruct((1, XATTN_W), F32)],
        scratch_shapes=[pltpu.VMEM((8, CONV_W), F32)],
        compiler_params=_params("arbitrary"),
    )(dx1, y2, ycat, ya, yx, bcu, bcu, qx, mkv, conv_w, g_a, g_c, g_x, w_out, g_post, after)


def _memkv_bwd(mem, g_mem, w_kv, dmkv):
    n_mem = mem.shape[0]
    half = D_MODEL // N_CHIPS // 2

    def body(mem_ref, g_ref, w_ref, d_ref, dw_ref, dg_ref):
        mhat, _ = _rms_hat(mem_ref[...])
        mn = (mhat * g_ref[...]).astype(BF16)
        d = d_ref[...].astype(BF16)
        for k in range(2 * N_CHIPS):
            dw_ref[k % 2, k // 2] = lax.dot_general(mn[:, half * k:half * (k + 1)], d, TN, preferred_element_type=F32)
        dmn = lax.dot_general(d, w_ref[...], NT, preferred_element_type=F32)
        dg_ref[...] = jnp.sum(dmn * mhat, axis=0, keepdims=True)

    return pl.pallas_call(
        body, name="memkv_bwd",
        out_shape=[jax.ShapeDtypeStruct((2, N_CHIPS, half, 2 * XATTN_W), F32), jax.ShapeDtypeStruct((1, D_MODEL), F32)],
        compiler_params=pltpu.CompilerParams(vmem_limit_bytes=VMEM_LIMIT_V7X),
    )(mem, g_mem, w_kv, dmkv)


def _in_proj_bwd(dqkv, tail, cos, sin, w_in, x, h, g, dx1, after, tm):
    S = x.shape[0]
    step_w = 2 * 256
    half = D_MODEL // 2

    def body(dq_ref, dk_ref, dv_ref, tail_ref, cos_ref, sin_ref, w_hbm, x_ref, h_ref, g_ref, dx1_ref, after_ref,
             dx_ref, gw_ref, dg_ref, dproj_ref, w_full, sems):
        _side_by_side(w_hbm, w_full, sems)

        @pl.when(pl.program_id(0) == 0)
        def _():
            dg_ref[...] = jnp.zeros_like(dg_ref)
            gw_ref[...] = jnp.zeros_like(gw_ref)

        halves = [slice(0, tm // 2), slice(tm // 2, tm)]
        for rows in halves:
            c, s = cos_ref[rows, :], sin_ref[rows, :]
            for j in range(ATTN_W // 128):
                cols = slice(128 * j, 128 * (j + 1))
                dproj_ref[rows, cols] = _rope128(dq_ref[rows, cols] * SCALE, c, s, True).astype(BF16)
                dproj_ref[rows, ATTN_W + 128 * j:ATTN_W + 128 * (j + 1)] = _rope128(dk_ref[rows, cols], c, s, True).astype(BF16)
            dproj_ref[rows, 2 * ATTN_W:3 * ATTN_W] = dv_ref[rows, :].astype(BF16)
            dproj_ref[rows, 3 * ATTN_W:PROJ_W] = tail_ref[rows, :]
        dhs = [lax.dot_general(dproj_ref[rows, :], w_full[...], NT, preferred_element_type=F32) for rows in halves]
        for rows, dh in zip(halves, dhs):
            xhat, r = _rms_hat(x_ref[rows, :])
            dg_ref[...] += jnp.sum(dh * xhat, axis=0, keepdims=True)
            dx_ref[rows, :] = dx1_ref[rows, :] + _rms_bwd(xhat, r, g_ref[...], dh)
        hb = h_ref[...]
        for step in range(PROJ_W // step_w):
            res = lax.dot_general(hb, dproj_ref[:, step * step_w:(step + 1) * step_w], TN, preferred_element_type=F32)
            lo = step * step_w
            while lo < (step + 1) * step_w:
                chip = lo // SHARD_IN
                hi = min((step + 1) * step_w, (chip + 1) * SHARD_IN)
                for hh in range(2):
                    gw_ref[hh, chip, :, lo - chip * SHARD_IN:hi - chip * SHARD_IN] += (
                        res[half * hh:half * (hh + 1), lo - step * step_w:hi - step * step_w])
                lo = hi

    whole = lambda shape: pl.BlockSpec(shape, lambda i: (0,) * len(shape))
    return pl.pallas_call(
        body, name="in_proj_bwd", grid=(S // tm,),
        in_specs=[_rows(tm, ATTN_W)] * 3 + [_rows(tm, PROJ_W - 3 * ATTN_W), _rows(tm, 128), _rows(tm, 128),
                  pl.BlockSpec(memory_space=pl.ANY), _rows(tm, D_MODEL), _rows(tm, D_MODEL), _resident((1, D_MODEL)),
                  _rows(tm, D_MODEL), pl.BlockSpec(memory_space=pl.ANY)],
        out_specs=[_rows(tm, D_MODEL), whole((2, N_CHIPS, half, SHARD_IN)), whole((1, D_MODEL))],
        out_shape=[jax.ShapeDtypeStruct((S, D_MODEL), F32), jax.ShapeDtypeStruct((2, N_CHIPS, half, SHARD_IN), F32),
                   jax.ShapeDtypeStruct((1, D_MODEL), F32)],
        scratch_shapes=[pltpu.VMEM((tm, PROJ_W), BF16), pltpu.VMEM((D_MODEL, PROJ_W), BF16),
                        pltpu.SemaphoreType.DMA((N_CHIPS,))],
        compiler_params=_params("arbitrary"),
    )(*dqkv, tail, cos, sin, w_in, x, h, g, dx1, after)


def _row_tile(rows):
    return ROW_TILE if rows % ROW_TILE == 0 else rows


def _chip_sums_bf16(name, grads, from_sibling, place):
    k = len(grads)
    _, n, rows, _ = grads[0].shape
    tr = _row_tile(rows)

    def body(place_ref, *refs):
        for g_ref, b_ref, o_ref in zip(refs[:k], refs[k:2 * k], refs[2 * k:]):
            o_ref[...] = (g_ref[0] + b_ref[...]).astype(BF16)

    mine = lambda g: pl.BlockSpec((1, 1, tr, g.shape[3]), lambda s, i, p: (p[0], s, i, 0))
    slab = lambda g: pl.BlockSpec((1, tr, g.shape[3]), lambda s, i, p: (s, i, 0))
    return pl.pallas_call(
        body, name=name, out_shape=[jax.ShapeDtypeStruct(g.shape[1:], BF16) for g in grads],
        grid_spec=pltpu.PrefetchScalarGridSpec(
            num_scalar_prefetch=1, grid=(n, rows // tr),
            in_specs=[mine(g) for g in grads] + [slab(g) for g in grads], out_specs=[slab(g) for g in grads]),
        compiler_params=_params("parallel", "parallel"),
    )(place, *grads, *from_sibling)


def _final_sums(name, grads, from_sibling, others, place):
    k = len(grads)
    rows = grads[0].shape[2]
    tr = _row_tile(rows)

    def body(place_ref, *refs):
        for a in range(k):
            own_ref, sib_ref = refs[a], refs[k + a]
            acc = own_ref[0, 0] + sib_ref[0]
            for o in refs[2 * k + 3 * a:2 * k + 3 * a + 3]:
                acc = acc + o[0].astype(F32)
            refs[5 * k + a][0] = acc

    own = lambda g: pl.BlockSpec((1, 1, tr, g.shape[3]), lambda i, p: (p[0], p[1], i, 0))
    sib = lambda g: pl.BlockSpec((1, tr, g.shape[3]), lambda i, p: (p[1], i, 0))
    other = lambda g, j: pl.BlockSpec((1, tr, g.shape[3]), lambda i, p: (j, i, 0))
    return pl.pallas_call(
        body, name=name, out_shape=[jax.ShapeDtypeStruct((2,) + g.shape[2:], F32) for g in grads],
        grid_spec=pltpu.PrefetchScalarGridSpec(
            num_scalar_prefetch=1, grid=(rows // tr,),
            in_specs=[own(g) for g in grads] + [sib(g) for g in grads] + [other(g, j) for g in grads for j in range(3)],
            out_specs=[pl.BlockSpec((1, tr, g.shape[3]), lambda i, p: (p[0], i, 0)) for g in grads]),
        compiler_params=_params("parallel"),
    )(place, *grads, *from_sibling, *[o for o in others for _ in range(3)])


def _adamw_update(w, g, m, v):
    m = ADAM_B1 * m + (1.0 - ADAM_B1) * g
    v = ADAM_B2 * v + (1.0 - ADAM_B2) * (g * g)
    m_hat = m * (1.0 / (1.0 - ADAM_B1 ** ADAM_STEP))
    v_hat = v * (1.0 / (1.0 - ADAM_B2 ** ADAM_STEP))
    return -ADAM_LR * (m_hat / (jnp.sqrt(v_hat) + ADAM_EPS) + ADAM_WD * w), m, v


def _adamw(name, params, after):
    k = len(params)
    rows = params[0][0].shape[0]
    tr = ADAMW_ROW_TILE if rows % ADAMW_ROW_TILE == 0 else rows

    def body(*refs):
        ins, outs = refs[:4 * k], refs[4 * k + 1:]
        for a in range(k):
            w_ref, g_ref, m_ref, v_ref = ins[4 * a:4 * a + 4]
            g = g_ref[...]
            outs[4 * a][...] = g
            outs[4 * a + 1][...], outs[4 * a + 2][...], outs[4 * a + 3][...] = _adamw_update(w_ref[...], g, m_ref[...], v_ref[...])

    spec = lambda w: pl.BlockSpec((tr, w.shape[1]), lambda i: (i, 0))
    out = pl.pallas_call(
        body, name=name, grid=(rows // tr,),
        in_specs=[spec(p[0]) for p in params for _ in range(4)] + [pl.BlockSpec(memory_space=pl.ANY)],
        out_specs=[spec(p[0]) for p in params for _ in range(4)],
        out_shape=[jax.ShapeDtypeStruct(p[0].shape, F32) for p in params for _ in range(4)],
        compiler_params=_params("parallel"),
    )(*[t for p in params for t in p], after)
    return [out[4 * a:4 * a + 4] for a in range(k)]


def _small_update(blocks, chip, gains, gains_m, gains_v, taps, taps_m, taps_v):
    n = len(gains)
    widths = [g.shape[1] for g in gains]
    k, w = taps.shape

    def body(*refs):
        chip_ref, blocks_ref = refs[0], refs[1]
        params = [refs[2 + 3 * i:5 + 3 * i] for i in range(n + 1)]
        outs = [refs[2 + 3 * (n + 1) + 4 * i:2 + 3 * (n + 1) + 4 * (i + 1)] for i in range(n + 1)]
        loss_ref = refs[-1]
        summed = blocks_ref[0]
        for device in range(1, blocks.shape[0]):
            summed = summed + blocks_ref[device]
        for i in range(n):
            g = summed[i:i + 1, 0:widths[i]]
            wr, mr, vr = params[i]
            outs[i][0][...] = g
            outs[i][1][...], outs[i][2][...], outs[i][3][...] = _adamw_update(wr[...], g, mr[...], vr[...])
        g = summed[n:n + k, 0:w]
        for j in range(1, N_CHIPS):
            g = jnp.where(chip_ref[0] == j, summed[n:n + k, w * j:w * (j + 1)], g)
        wr, mr, vr = params[n]
        for out_ref, val in zip(outs[n], (g, *_adamw_update(wr[...], g, mr[...], vr[...]))):
            for j in range(k):
                out_ref[j] = val[j:j + 1, :]
        loss_ref[...] = summed[n + k:n + k + 1, 0:1]

    vmem = pl.BlockSpec(memory_space=pltpu.VMEM)
    operands = [chip, blocks]
    for p in zip(list(gains) + [taps], list(gains_m) + [taps_m], list(gains_v) + [taps_v]):
        operands += list(p)
    shapes = [jax.ShapeDtypeStruct(shape, F32) for shape in [g.shape for g in gains] + [(k, 1, w)] for _ in range(4)]
    out = pl.pallas_call(
        body, name="small_update", out_shape=shapes + [jax.ShapeDtypeStruct((1, 1), F32)],
        in_specs=[pl.BlockSpec(memory_space=pltpu.SMEM)] + [vmem] * (len(operands) - 1),
        out_specs=[vmem] * (len(shapes) + 1),
    )(*operands)
    return [out[4 * i:4 * (i + 1)] for i in range(n + 1)], out[-1]


def _place():
    return lax.axis_index("x"), lax.axis_index("y"), lax.axis_index("c")


def _other_chips(x, y):
    return [(1 - x, y), (x, 1 - y), (1 - x, 1 - y)]


def _allgather_finish(name, shards, landed, pass_on):
    n = len(shards)

    def body(*refs):
        ins, outs, stage = refs[:n], refs[2 * n:3 * n], refs[3 * n:4 * n]
        send_sems, recv_sems, local_sems = refs[4 * n:]
        x, y, c = _place()
        chips = _other_chips(x, y)
        barrier = pltpu.get_barrier_semaphore()
        pl.semaphore_signal(barrier, inc=1, device_id=(x, y, 1 - c), device_id_type=MESH)
        pl.semaphore_wait(barrier, 1)

        def copy(a, k, chip, half):
            place = outs[a].at[2 * chip[0] + chip[1], half]
            return pltpu.make_async_remote_copy(
                src_ref=place, dst_ref=place, send_sem=send_sems.at[3 * a + k], recv_sem=recv_sems.at[3 * a + k],
                device_id=(x, y, 1 - c), device_id_type=MESH)

        load = [pltpu.make_async_copy(ins[a], stage[a], local_sems.at[a]) for a in range(n)]
        local = [pltpu.make_async_copy(stage[a], outs[a].at[2 * x + y], local_sems.at[a]) for a in range(n)]
        for cp in load:
            cp.start()
        passed = [copy(a, k, chip, c) for a in range(n) if pass_on[a] for k, chip in enumerate(chips)]
        for cp in passed:
            cp.start()
        for a in range(n):
            load[a].wait()
            local[a].start()
        for a in range(n):
            if pass_on[a]:
                for k, chip in enumerate(chips):
                    copy(a, k, chip, 1 - c).wait_recv()
        for cp in passed:
            cp.wait_send()
        for cp in local:
            cp.wait()

    any_spec = pl.BlockSpec(memory_space=pl.ANY)
    return pl.pallas_call(
        body, name=name,
        out_shape=[jax.ShapeDtypeStruct((N_CHIPS,) + s.shape, s.dtype) for s in shards],
        in_specs=[any_spec] * (2 * n), out_specs=[any_spec] * n,
        input_output_aliases={n + a: a for a in range(n)},
        scratch_shapes=[pltpu.VMEM(s.shape, s.dtype) for s in shards]
        + [pltpu.SemaphoreType.DMA((3 * n,)), pltpu.SemaphoreType.DMA((3 * n,)), pltpu.SemaphoreType.DMA((n,))],
        compiler_params=pltpu.CompilerParams(vmem_limit_bytes=VMEM_LIMIT_V7X, collective_id=HANDSHAKES["sibling"][0]),
    )(*shards, *landed)


def _plan_first_hop(x, y, c, shards, lands):
    return [(shards[a].at[c], lands[a].at[2 * x + y, c], lands[a].at[2 * chip[0] + chip[1], c], (*chip, c))
            for a in range(len(shards)) for chip in _other_chips(x, y)]


def _plan_pass_on(x, y, c, nothing, lands):
    def place(a, chip, half):
        return lands[a].at[2 * chip[0] + chip[1], half]

    return [(place(a, chip, c), place(a, chip, c), place(a, chip, 1 - c), (x, y, 1 - c))
            for a in range(len(lands)) for chip in _other_chips(x, y)]


def _plan_own_half_to_sibling(x, y, c, nothing, lands):
    return [(lands[a].at[c], lands[a].at[c], lands[a].at[1 - c], (x, y, 1 - c)) for a in range(len(lands))]


def _plan_other_half_to_sibling(x, y, c, grads, lands):
    return [(grads[a].at[1 - c], lands[a], lands[a], (x, y, 1 - c)) for a in range(len(grads))]


def _plan_to_other_chips(x, y, c, partials, lands):
    return [(partials[a].at[2 * chip[0] + chip[1]], lands[a].at[k], lands[a].at[k], (*chip, c))
            for a in range(len(partials)) for k, chip in enumerate(_other_chips(x, y))]


def _plan_to_all(x, y, c, blocks, lands):
    flips = [(fx, fy, fc) for fx in (0, 1) for fy in (0, 1) for fc in (0, 1) if (fx, fy, fc) != (0, 0, 0)]
    peers = [(1 - x if fx else x, 1 - y if fy else y, 1 - c if fc else c) for fx, fy, fc in flips]
    return [(blocks[0], lands[0].at[4 * x + 2 * y + c], lands[0].at[4 * p[0] + 2 * p[1] + p[2]], p) for p in peers]


def _planned_copies(plan, srcs, lands, send_sems, recv_sems):
    x, y, c = _place()

    def pair(k, src, there, here, to):
        make = lambda dst: pltpu.make_async_remote_copy(
            src_ref=src, dst_ref=dst, send_sem=send_sems.at[k], recv_sem=recv_sems.at[k], device_id=to, device_id_type=MESH)
        return make(there), make(here)

    return [pair(k, *entry) for k, entry in enumerate(plan(x, y, c, srcs, lands))]


_HBM_SPEC = pl.BlockSpec(memory_space=pltpu.HBM)
_SEM_SPEC = pl.BlockSpec(memory_space=pltpu.SEMAPHORE)


def _hbm(a):
    return pltpu.with_memory_space_constraint(a, pltpu.HBM)


HANDSHAKES = {
    "sibling": (1, lambda x, y, c: [(x, y, 1 - c)]),
}


def _exchange_start(name, plan, n_copies, srcs, land_shapes, after, lands=None, peers=None):
    if lands is None:
        lands = [lax.empty(s.shape, s.dtype) for s in land_shapes]
    land_shapes = lands
    ns, nl = len(srcs), len(land_shapes)
    n_in = ns + nl + 1
    collective_id, peers_of = HANDSHAKES[peers] if peers else (None, None)

    def body(*refs):
        if peers:
            who = peers_of(*_place())
            barrier = pltpu.get_barrier_semaphore()
            for peer in who:
                pl.semaphore_signal(barrier, inc=1, device_id=peer, device_id_type=MESH)
            pl.semaphore_wait(barrier, len(who))
        for send, _ in _planned_copies(plan, refs[:ns], refs[ns:ns + nl], refs[n_in], refs[n_in + 1]):
            send.start()
        refs[-1][...] = jnp.zeros_like(refs[-1])

    out = pl.pallas_call(
        body, name=name,
        out_shape=(pltpu.SemaphoreType.DMA((n_copies,)), pltpu.SemaphoreType.DMA((n_copies,)),
                   *[pltpu.HBM(s.shape, s.dtype) for s in land_shapes], jax.ShapeDtypeStruct((8, 128), F32)),
        in_specs=[_HBM_SPEC] * (ns + nl) + [pl.BlockSpec(memory_space=pl.ANY)],
        out_specs=(_SEM_SPEC, _SEM_SPEC, *[_HBM_SPEC] * nl, pl.BlockSpec(memory_space=pltpu.VMEM)),
        input_output_aliases={ns + i: 2 + i for i in range(nl)},
        compiler_params=pltpu.CompilerParams(has_side_effects=pltpu.SideEffectType.DATAFLOW_SIDE_EFFECTING,
                                             collective_id=collective_id),
    )(*[_hbm(s) for s in srcs], *[_hbm(l) for l in lands], after)
    return out[0], out[1], list(out[2:2 + nl]), out[-1]


def _exchange_wait(name, plan, srcs, started, after):
    send_sems, recv_sems, lands, _ = started
    ns, nl = len(srcs), len(lands)
    after = list(after) if isinstance(after, (list, tuple)) else [after]

    def body(*refs):
        for send, recv in _planned_copies(plan, refs[:ns], refs[ns:ns + nl], refs[ns + nl], refs[ns + nl + 1]):
            send.wait_send()
            recv.wait_recv()

    return pl.pallas_call(
        body, name=name, out_shape=[pltpu.HBM(l.shape, l.dtype) for l in lands],
        in_specs=[_HBM_SPEC] * (ns + nl) + [_SEM_SPEC, _SEM_SPEC] + [pl.BlockSpec(memory_space=pl.ANY)] * len(after),
        out_specs=[_HBM_SPEC] * nl, input_output_aliases={ns + i: i for i in range(nl)},
        compiler_params=pltpu.CompilerParams(has_side_effects=pltpu.SideEffectType.DATAFLOW_SIDE_EFFECTING),
    )(*[_hbm(s) for s in srcs], *lands, send_sems, recv_sems, *after)


def _like(arrays, lead, dtype=None):
    return [jax.ShapeDtypeStruct(tuple(lead) + a.shape[-2:], dtype or a.dtype) for a in arrays]


class _StepExchanges:
    def __init__(self, mats, conv_w):
        x, y, c = _place()
        self.place = jnp.stack([c, 2 * x + y]).astype(jnp.int32)
        shards = [w.astype(BF16).reshape(2, w.shape[0] // 2, w.shape[1]) for w in mats]
        self._in_shard = shards[:1]
        self._in = _exchange_start("w_in_allgather_start", _plan_first_hop, 3, self._in_shard,
                                   _like(self._in_shard, (N_CHIPS, 2)), shards[0])
        self.zero = self._in[3]
        taps = jnp.pad(conv_w, ((0, 8 - conv_w.shape[0]), (0, 128 - conv_w.shape[1])))
        self._rest_shards = shards[1:] + [jnp.stack([taps, jnp.zeros_like(taps)])]
        self._taps_shape = conv_w.shape
        self._groups = {}

    def w_in(self, after):
        landed = _exchange_wait("w_in_allgather_wait", _plan_first_hop, self._in_shard, self._in,
                                list(after) + self._rest_shards)
        (w_in,) = _allgather_finish("w_in_allgather_finish", self._in_shard, landed, [True])
        self._rest = _exchange_start("rest_allgather_start", _plan_first_hop, 3 * len(self._rest_shards),
                                     self._rest_shards, _like(self._rest_shards, (N_CHIPS, 2)), w_in)
        self.zero = self._rest[3]
        return w_in.reshape(N_CHIPS, 2 * w_in.shape[2], w_in.shape[3])

    def rest_weights(self, after):
        landed = _exchange_wait("rest_allgather_wait", _plan_first_hop, self._rest_shards, self._rest, after)
        kv, out, up, down, taps = _allgather_finish("rest_allgather_finish", self._rest_shards, landed,
                                                    [True, True, False, False, True])
        self._up_down = _exchange_start("up_down_pass_on_start", _plan_pass_on, 6, [], None, self.zero, lands=[up, down],
                                        peers="sibling")
        self.zero = self._up_down[3]
        k, w = self._taps_shape
        taps = taps[:, 0, :k, :w].transpose(1, 0, 2).reshape(k, N_CHIPS * w)
        return [g.reshape(N_CHIPS, 2 * g.shape[2], g.shape[3]) for g in (kv, out)], taps

    def up_down(self, after):
        full = _exchange_wait("up_down_pass_on_wait", _plan_pass_on, [], self._up_down, after)
        return [g.reshape(N_CHIPS, 2 * g.shape[2], g.shape[3]) for g in full]

    def send_grads(self, key, grads):
        grads = list(grads)
        started = _exchange_start(f"{key}_grads_to_sibling_start", _plan_other_half_to_sibling, len(grads), grads,
                                  _like(grads, (N_CHIPS,)), self.zero, peers="sibling")
        self._groups[key] = dict(grads=grads, to_sibling=started)
        self.zero = started[3]

    def grads_at_sibling(self, key, after):
        group = self._groups[key]
        grads = group["grads"]
        group["from_sibling"] = _exchange_wait(f"{key}_grads_to_sibling_wait", _plan_other_half_to_sibling, grads,
                                               group["to_sibling"], after)
        group["partials"] = _chip_sums_bf16(f"{key}_chip_sums", grads, group["from_sibling"], self.place)
        group["to_chips"] = _exchange_start(f"{key}_grads_to_chips_start", _plan_to_other_chips, 3 * len(grads),
                                            group["partials"], _like(group["partials"], (3,)), self.zero)
        self.zero = group["to_chips"][3]

    def grads_summed(self, key, after):
        group = self._groups[key]
        from_chips = _exchange_wait(f"{key}_grads_to_chips_wait", _plan_to_other_chips, group["partials"],
                                    group["to_chips"], after)
        return _final_sums(f"{key}_final_sums", group["grads"], group["from_sibling"], from_chips, self.place)

    def send_sums(self, key, sums):
        self._groups[key + "_sums"] = _exchange_start(f"{key}_sums_to_sibling_start", _plan_own_half_to_sibling,
                                                      len(sums), [], None, self.zero, lands=list(sums),
                                                      peers="sibling")
        self.zero = self._groups[key + "_sums"][3]

    def whole_sums(self, key, after):
        full = _exchange_wait(f"{key}_sums_to_sibling_wait", _plan_own_half_to_sibling, [], self._groups[key + "_sums"], after)
        return [t.reshape(2 * t.shape[1], t.shape[2]) for t in full]

    def send_small(self, block):
        self._small = block
        self._small_started = _exchange_start("small_grads_start", _plan_to_all, 7, [block],
                                              [jax.ShapeDtypeStruct((8,) + block.shape, block.dtype)], self.zero)
        self.zero = self._small_started[3]

    def small_blocks(self, after):
        x, y, c = _place()
        (landed,) = _exchange_wait("small_grads_wait", _plan_to_all, [self._small], self._small_started, after)
        return lax.dynamic_update_index_in_dim(landed, self._small, 4 * x + 2 * y + c, 0)


def _rope_tables(positions):
    half = HEAD // 2
    inv_freq = jnp.float32(ROPE_THETA) ** (-(jnp.arange(half, dtype=F32) * 2.0 / HEAD))
    ang = positions.astype(F32)[:, None] * inv_freq
    cos, sin = jnp.cos(ang), jnp.sin(ang)
    return jnp.tile(cos, (1, 4)), jnp.tile(jnp.concatenate([-sin, sin], axis=1), (1, 2))


def _local_step(x, mem, positions, target, gains, ex):
    g_pre_mix, g_mem, g_a, g_c, g_x, g_post_mix, g_pre_mlp, g_post_mlp = gains
    tm = ROW_TILE
    cos, sin = _rope_tables(positions)
    h = _pre_norm(x, g_pre_mix, ex.zero, tm)
    w_in = ex.w_in([h, cos, sin])

    q, k, v, bcu, qx = _in_proj_fwd(h, w_in, cos, sin, ex.zero, tm)
    ya, lse = _attn_fwd(q, k, v)
    (w_kv, w_out), conv_w = ex.rest_weights(lse)
    w_kv, w_out = (w.reshape(N_CHIPS * w.shape[1], w.shape[2]) for w in (w_kv, w_out))
    memn, mkv = _memkv_fwd(mem, g_mem, w_kv, ex.zero)
    yx, ycat, y2, x1 = _mix_fwd(ya, bcu, qx, mkv, conv_w, g_a, g_c, g_x, w_out, g_post_mix, x, tm)
    w_up, w_down = ex.up_down(x1)
    w_down = w_down.reshape(N_CHIPS * w_down.shape[1], w_down.shape[2])
    h2, f, du, df2, dx1, dg_pre_mlp, dg_post_mlp, loss = _mlp_fwd_bwd(x1, target, g_pre_mlp, g_post_mlp, w_up, w_down,
                                                                      MLP_ROW_TILE)
    gw_down = _weight_grad("grad_w_down", f, df2, True, ex.zero)
    gw_up = _weight_grad("grad_w_up", h2, du, False, ex.zero)
    ex.send_grads("early", [gw_up, gw_down])

    gw_out, dya, delta, tail, dmkv, g_conv, dg_post_mix, dg_a, dg_c, dg_x = _mixer_bwd(
        dx1, y2, ycat, ya, yx, bcu, qx, mkv, conv_w, g_a, g_c, g_x, w_out, g_post_mix, ex.zero, tm)
    ex.grads_at_sibling("early", dya)
    gw_kv, dg_mem = _memkv_bwd(mem, g_mem, w_kv, dmkv)
    ex.send_grads("mid", [gw_out, gw_kv])
    dqkv = _attn_bwd(q, k, v, dya, lse, delta, ex.zero)
    ex.grads_at_sibling("mid", dqkv[0])
    grad_x, gw_in, dg_pre_mix = _in_proj_bwd(dqkv, tail, cos, sin, w_in, x, h, g_pre_mix, dx1, ex.zero, tm)
    gain_grads = [dg_pre_mix, dg_mem, dg_a, dg_c, dg_x, dg_post_mix, dg_pre_mlp, dg_post_mlp]
    ex.send_small(_pack_small(gain_grads, g_conv, loss))
    ex.send_grads("late", [gw_in])
    return grad_x


def _pack_small(gains, conv, scalar):
    n, k = len(gains), conv.shape[0]

    def body(*refs):
        out_ref = refs[-1]
        out_ref[...] = jnp.zeros_like(out_ref)
        for i, g_ref in enumerate(refs[:n]):
            out_ref[i:i + 1, 0:g_ref.shape[1]] = g_ref[...]
        out_ref[n:n + k, 0:conv.shape[1]] = refs[n][...]
        out_ref[n + k:n + k + 1, 0:1] = refs[n + 1][...]

    return pl.pallas_call(body, name="pack_small", out_shape=jax.ShapeDtypeStruct((SMALL_ROWS, D_MODEL), F32))(
        *gains, conv, scalar)


def kernel(x, mem, positions, g_pre_mix, g_mem, w_in, w_mem_kv, conv_w, g_attn_out, g_conv_out, g_xattn_out, w_out, g_post_mix, g_pre_mlp, w_up, w_down, g_post_mlp, loss_target, m_g_pre_mix, m_g_mem, m_w_in, m_w_mem_kv, m_conv_w, m_g_attn_out, m_g_conv_out, m_g_xattn_out, m_w_out, m_g_post_mix, m_g_pre_mlp, m_w_up, m_w_down, m_g_post_mlp, v_g_pre_mix, v_g_mem, v_w_in, v_w_mem_kv, v_conv_w, v_g_attn_out, v_g_conv_out, v_g_xattn_out, v_w_out, v_g_post_mix, v_g_pre_mlp, v_w_up, v_w_down, v_g_post_mlp):
    chip = 2 * lax.axis_index("x") + lax.axis_index("y")
    gains = [g_pre_mix, g_mem, g_attn_out, g_conv_out, g_xattn_out, g_post_mix, g_pre_mlp, g_post_mlp]
    gains_m = [m_g_pre_mix, m_g_mem, m_g_attn_out, m_g_conv_out, m_g_xattn_out, m_g_post_mix, m_g_pre_mlp, m_g_post_mlp]
    gains_v = [v_g_pre_mix, v_g_mem, v_g_attn_out, v_g_conv_out, v_g_xattn_out, v_g_post_mix, v_g_pre_mlp, v_g_post_mlp]
    mats =[w_in[0], w_mem_kv[0], w_out[0], w_up[0], w_down[0]]
    mats_m = [m_w_in[0], m_w_mem_kv[0], m_w_out[0], m_w_up[0], m_w_down[0]]
    mats_v = [v_w_in[0], v_w_mem_kv[0], v_w_out[0], v_w_up[0], v_w_down[0]]

    ex = _StepExchanges(mats, conv_w[0])
    grad_x = _local_step(x[0], mem[0], positions[0], loss_target[0], gains, ex)

    ex.send_sums("four", ex.grads_summed("early", ex.zero) + ex.grads_summed("mid", ex.zero))
    ex.grads_at_sibling("late", ex.zero)
    up_sum, down_sum, out_sum, kv_sum = ex.whole_sums("four", ex.zero)
    params = lambda a, g: (mats[a], g, mats_m[a], mats_v[a])
    new_up, new_down = _adamw("adamw_up_down", [params(3, up_sum), params(4, down_sum)], ex.zero)
    new_out, new_kv = _adamw("adamw_out_kv", [params(2, out_sum), params(1, kv_sum)], ex.zero)

    small, total = _small_update(ex.small_blocks(new_kv[1]), chip.reshape(1).astype(jnp.int32), gains, gains_m,
                                 gains_v, conv_w[0], m_conv_w[0], v_conv_w[0])

    ex.send_sums("last", ex.grads_summed("late", small[0][1]))
    (in_sum,) = ex.whole_sums("last", ex.zero)
    (new_in,) = _adamw("adamw_in", [params(0, in_sum)], in_sum)
    mat_new = [new_in, new_kv, new_out, new_up, new_down]

    order = ["g_pre_mix", "g_mem", "w_in", "w_mem_kv", "conv_w", "g_attn_out", "g_conv_out", "g_xattn_out", "w_out",
             "g_post_mix", "g_pre_mlp", "w_up", "w_down", "g_post_mlp"]
    gain_names = ["g_pre_mix", "g_mem", "g_attn_out", "g_conv_out", "g_xattn_out", "g_post_mix", "g_pre_mlp", "g_post_mlp"]
    mat_names = ["w_in", "w_mem_kv", "w_out", "w_up", "w_down"]

    def leaf(kind, name):
        if name in gain_names:
            return small[gain_names.index(name)][kind]
        if name == "conv_w":
            return jnp.swapaxes(small[len(gain_names)][kind], 0, 1)
        return mat_new[mat_names.index(name)][kind][None]

    return (total[0, 0], grad_x[None], *[leaf(kind, name) for kind in range(4) for name in order])
```

```python
import jax
import jax.numpy as jnp
from jax import lax
from jax.experimental import pallas as pl
from jax.experimental.pallas import tpu as pltpu

F32, BF16 = jnp.float32, jnp.bfloat16

D_MODEL = 1024
ATTN_W = 512
CONV_W = 256
XATTN_W = 256
PROJ_W = 3 * ATTN_W + 3 * CONV_W + XATTN_W
D_FF = 4096
HEAD = 64
N_BACK = 128
DILATIONS = (1, 4, 16)
PATTERN_ORDER = DILATIONS[::-1]
ROPE_THETA = 10000.0
EPS = 1e-6
NEG_INF = -1e30
SCALE = HEAD ** -0.5
N_CHIPS = 4
SHARD_IN = PROJ_W // N_CHIPS
SHARD_FF = D_FF // N_CHIPS

ADAM_LR, ADAM_B1, ADAM_B2, ADAM_EPS, ADAM_WD, ADAM_STEP = 0.001, 0.9, 0.999, 1e-08, 0.01, 10

VMEM_LIMIT_V7X = 56 * 1024 * 1024
ROW_TILE = 512
MLP_ROW_TILE = 256
ADAMW_ROW_TILE = 256
SMALL_ROWS = 16

NT = (((1,), (1,)), ((), ()))
TN = (((0,), (0,)), ((), ()))
MESH = pl.DeviceIdType.MESH


def _params(*sem):
    return pltpu.CompilerParams(dimension_semantics=sem, vmem_limit_bytes=VMEM_LIMIT_V7X)


def _resident(shape):
    return pl.BlockSpec(shape, lambda *_: (0,) * len(shape), pipeline_mode=pl.Buffered(1))


def _rows(tm, width):
    return pl.BlockSpec((tm, width), lambda i, *_: (i, 0))


def _rms_hat(x):
    r = lax.rsqrt(jnp.mean(x * x, axis=-1, keepdims=True) + EPS)
    return x * r, r


def _rms_bwd(xhat, r, g, dy):
    gdy = dy * g
    return r * (gdy - xhat * jnp.mean(xhat * gdy, axis=-1, keepdims=True))


def _rope128(t, cos, sin_signed, inverse):
    lane = lax.broadcasted_iota(jnp.int32, t.shape, 1)
    first_half = (lane % HEAD) < (HEAD // 2)
    rot = jnp.where(first_half, pltpu.roll(t, 128 - HEAD // 2, 1), pltpu.roll(t, HEAD // 2, 1))
    return t * cos - rot * sin_signed if inverse else t * cos + rot * sin_signed


def _pre_norm(x, g, after, tm):
    S = x.shape[0]

    def body(x_ref, g_ref, after_ref, h_ref):
        h_ref[...] = (_rms_hat(x_ref[...])[0] * g_ref[...]).astype(BF16)

    return pl.pallas_call(
        body, name="pre_norm", grid=(S // tm,),
        in_specs=[_rows(tm, D_MODEL), _resident((1, D_MODEL)), pl.BlockSpec(memory_space=pl.ANY)],
        out_specs=_rows(tm, D_MODEL), out_shape=jax.ShapeDtypeStruct((S, D_MODEL), BF16),
        compiler_params=_params("parallel"),
    )(x, g, after)


def _side_by_side(w_hbm, w_full, sems):
    width = w_hbm.shape[2]

    @pl.when(pl.program_id(0) == 0)
    def _():
        copies = [pltpu.make_async_copy(w_hbm.at[j], w_full.at[:, pl.ds(width * j, width)], sems.at[j])
                  for j in range(N_CHIPS)]
        for cp in copies:
            cp.start()
        for cp in copies:
            cp.wait()


def _in_proj_fwd(h, w_in, cos, sin, after, tm):
    S = h.shape[0]

    def body(h_ref, w_hbm, cos_ref, sin_ref, after_ref, q_ref, k_ref, v_ref, bcu_ref, qx_ref, proj, w_full, sems):
        _side_by_side(w_hbm, w_full, sems)
        proj[...] = jnp.dot(h_ref[...], w_full[...], preferred_element_type=F32)
        c, s = cos_ref[...], sin_ref[...]
        for j in range(ATTN_W // 128):
            lo = 128 * j
            q_ref[:, lo:lo + 128] = _rope128(proj[:, lo:lo + 128], c, s, False) * SCALE
            k_ref[:, lo:lo + 128] = _rope128(proj[:, ATTN_W + lo:ATTN_W + lo + 128], c, s, False)
        v_ref[...] = proj[:, 2 * ATTN_W:3 * ATTN_W]
        bcu_ref[...] = proj[:, 3 * ATTN_W:3 * ATTN_W + 3 * CONV_W]
        qx_ref[...] = proj[:, 3 * ATTN_W + 3 * CONV_W:PROJ_W].astype(BF16)

    return pl.pallas_call(
        body, name="in_proj_fwd", grid=(S // tm,),
        in_specs=[_rows(tm, D_MODEL), pl.BlockSpec(memory_space=pl.ANY), _rows(tm, 128), _rows(tm, 128),
                  pl.BlockSpec(memory_space=pl.ANY)],
        out_specs=[_rows(tm, ATTN_W), _rows(tm, ATTN_W), _rows(tm, ATTN_W), _rows(tm, 3 * CONV_W), _rows(tm, XATTN_W)],
        out_shape=[jax.ShapeDtypeStruct((S, ATTN_W), F32), jax.ShapeDtypeStruct((S, ATTN_W), F32),
                   jax.ShapeDtypeStruct((S, ATTN_W), F32), jax.ShapeDtypeStruct((S, 3 * CONV_W), F32),
                   jax.ShapeDtypeStruct((S, XATTN_W), BF16)],
        scratch_shapes=[pltpu.VMEM((tm, PROJ_W), F32), pltpu.VMEM((D_MODEL, PROJ_W), BF16),
                        pltpu.SemaphoreType.DMA((N_CHIPS,))],
        compiler_params=_params("arbitrary"),
    )(h, w_in, cos, sin, after)


def _memkv_fwd(mem, g_mem, w_kv, after):
    n_mem = mem.shape[0]

    def body(mem_ref, g_ref, w_ref, after_ref, mn_ref, kv_ref):
        mhat, _ = _rms_hat(mem_ref[...])
        mn = (mhat * g_ref[...]).astype(BF16)
        mn_ref[...] = mn
        kv_ref[...] = jnp.dot(mn, w_ref[...], preferred_element_type=F32).astype(BF16)

    vmem = pl.BlockSpec(memory_space=pltpu.VMEM)
    return pl.pallas_call(
        body, name="memkv_fwd", in_specs=[vmem, vmem, vmem, pl.BlockSpec(memory_space=pl.ANY)], out_specs=[vmem, vmem],
        out_shape=[jax.ShapeDtypeStruct((n_mem, D_MODEL), BF16), jax.ShapeDtypeStruct((n_mem, 2 * XATTN_W), BF16)],
        compiler_params=pltpu.CompilerParams(vmem_limit_bytes=VMEM_LIMIT_V7X),
    )(mem, g_mem, w_kv, after)


def _fill_band_bias(bias):
    row = lax.broadcasted_iota(jnp.int32, (N_BACK, 2 * N_BACK), 0)
    col = lax.broadcasted_iota(jnp.int32, (N_BACK, 2 * N_BACK), 1)
    band = (col >= row) & (col <= row + N_BACK)
    bias[1] = jnp.where(band, 0.0, NEG_INF)
    bias[0] = jnp.where(band & (col >= N_BACK), 0.0, NEG_INF)


def _strided(start, size, d):
    return pl.ds(start, size) if d == 1 else pl.ds(start, size, stride=d)


def _group_starts(g, G, nb, d):
    t0 = g * G
    r, n0 = lax.shift_right_logical(t0, nb.bit_length() - 1), lax.bitwise_and(t0, nb - 1)
    first = r + n0 * (N_BACK * d)
    before = r + jnp.maximum(n0 - 1, 0) * (N_BACK * d)
    starts = [before] + [first + u * (N_BACK * d) for u in range(G)]
    if d == 1:
        starts = [pl.multiple_of(st, N_BACK) for st in starts]
    return starts, n0


def _step_blocks(i, U, nb, d):
    G = min(U, nb)
    whole = G == nb
    row_blocks, blocks = [], []
    for grp in range(U // G):
        starts, n0 = _group_starts(i * (U // G) + grp, G, nb, d)
        base = len(row_blocks)
        if whole:
            row_blocks += [_strided(st, N_BACK, d) for st in starts[1:]]
            blocks += [(base + max(u - 1, 0), base + u, min(u, 1)) for u in range(G)]
        else:
            row_blocks += [_strided(st, N_BACK, d) for st in starts]
            blocks += [(base + u, base + u + 1, jnp.minimum(n0, 1) if u == 0 else 1) for u in range(G)]
    return row_blocks, blocks


def _by_head(a, b):
    lane = lax.broadcasted_iota(jnp.int32, (a.shape[0], 2 * HEAD), 1)
    return jnp.where(lane < HEAD, a, b)


def _head_only(t, hh):
    lane = lax.broadcasted_iota(jnp.int32, t.shape, 1)
    return jnp.where((lane < HEAD) == (hh == 0), t, jnp.zeros_like(t))


def _stack_heads(t):
    return jnp.concatenate([_head_only(t, 0), _head_only(t, 1)], axis=0)


def _head_columns(t):
    return jnp.concatenate([t[:, 0:1], t[:, HEAD:HEAD + 1]], axis=0)


def _unstack(t):
    return _by_head(t[:N_BACK], t[N_BACK:])


def _unstack_columns(t):
    return _by_head(jnp.broadcast_to(t[:N_BACK], (N_BACK, 2 * HEAD)), jnp.broadcast_to(t[N_BACK:], (N_BACK, 2 * HEAD)))


FWD_BLOCKS_PER_STEP = 4
BWD_BLOCKS_PER_STEP = 4
BWD_CHUNK = 64


def _attn_fwd(q, k, v):
    S = q.shape[0]
    U = FWD_BLOCKS_PER_STEP

    def body(q_ref, k_ref, v_ref, y_ref, m_ref, l_scr, bias):
        _fill_band_bias(bias)
        for g, d in enumerate(PATTERN_ORDER):
            nb = S // d // N_BACK
            first_pattern, last_pattern = g == 0, g == len(PATTERN_ORDER) - 1

            def step(i, carry, d=d, nb=nb, first_pattern=first_pattern, last_pattern=last_pattern):
                row_blocks, blocks = _step_blocks(i, U, nb, d)
                kb = [k_ref[r, :].astype(BF16) for r in row_blocks]
                ss = []
                for before, own, which in blocks:
                    kw = jnp.concatenate([kb[before], kb[own]], 0)
                    qs = _stack_heads(q_ref[row_blocks[own], :].astype(BF16))
                    b = bias[which]
                    ss.append(lax.dot_general(qs, kw, NT, preferred_element_type=F32) + jnp.concatenate([b, b], axis=0))
                ms = [jnp.max(s, axis=1, keepdims=True) for s in ss]
                ps = [jnp.exp(s - m) for s, m in zip(ss, ms)]
                ls = [jnp.sum(p, axis=1, keepdims=True) for p in ps]
                vb = [v_ref[r, :].astype(BF16) for r in row_blocks]
                os_ = [jnp.dot(ps[u].astype(BF16), jnp.concatenate([vb[before], vb[own]], 0), preferred_element_type=F32)
                       for u, (before, own, _) in enumerate(blocks)]
                for u, (_, own, _) in enumerate(blocks):
                    o_g, m_g, l_g = _unstack(os_[u]), _unstack_columns(ms[u]), _unstack_columns(ls[u])
                    r = row_blocks[own]
                    if first_pattern:
                        m_new, l_new, acc = m_g, l_g, o_g
                    else:
                        m_old = m_ref[r, :]
                        m_new = jnp.maximum(m_old, m_g)
                        alpha, beta = jnp.exp(m_old - m_new), jnp.exp(m_g - m_new)
                        l_new = l_scr[r, :] * alpha + l_g * beta
                        acc = y_ref[r, :] * alpha + o_g * beta
                    if last_pattern:
                        y_ref[r, :] = acc / l_new
                        m_ref[r, :] = m_new + jnp.log(l_new)
                    else:
                        y_ref[r, :] = acc
                        m_ref[r, :] = m_new
                        l_scr[r, :] = l_new
                return carry

            lax.fori_loop(0, d * nb // U, step, 0)

    col = pl.BlockSpec((S, 2 * HEAD), lambda j: (0, j))
    return pl.pallas_call(
        body, name="attn_fwd", grid=(q.shape[1] // (2 * HEAD),),
        in_specs=[col, col, col], out_specs=[col, col],
        out_shape=[jax.ShapeDtypeStruct(q.shape, F32)] * 2,
        scratch_shapes=[pltpu.VMEM((S, 2 * HEAD), F32), pltpu.VMEM((2, N_BACK, 2 * N_BACK), F32)],
        compiler_params=_params("parallel"),
    )(q, k, v)


def _attn_bwd(q, k, v, dy, lse, delta, after):
    S = q.shape[0]
    U = BWD_BLOCKS_PER_STEP

    def body(q_ref, k_ref, v_ref, dy_ref, lse_ref, delta_ref, after_ref, dq_ref, dk_ref, dv_ref, bias):
        _fill_band_bias(bias)
        nb_first = S // PATTERN_ORDER[0] // N_BACK
        first_writes_all = min(U, nb_first) == nb_first
        if not first_writes_all:
            dk_ref[...] = jnp.zeros_like(dk_ref)
            dv_ref[...] = jnp.zeros_like(dv_ref)
        for g, d in enumerate(PATTERN_ORDER):
            nb = S // d // N_BACK

            def step(i, carry, d=d, nb=nb, g=g):
                row_blocks, blocks = _step_blocks(i, U, nb, d)
                kb = [k_ref[r, :].astype(BF16) for r in row_blocks]
                vb = [v_ref[r, :].astype(BF16) for r in row_blocks]
                kws = [jnp.concatenate([kb[before], kb[own]], 0) for before, own, _ in blocks]
                vws = [jnp.concatenate([vb[before], vb[own]], 0) for before, own, _ in blocks]
                qss = [_stack_heads(q_ref[row_blocks[own], :].astype(BF16)) for _, own, _ in blocks]
                doss = [_stack_heads(dy_ref[row_blocks[own], :].astype(BF16)) for _, own, _ in blocks]
                ss = [lax.dot_general(qss[u], kws[u], NT, preferred_element_type=F32) for u in range(U)]
                dps = [lax.dot_general(doss[u], vws[u], NT, preferred_element_type=F32) for u in range(U)]
                pbs, dss = [], []
                for u, (_, own, which) in enumerate(blocks):
                    lse_c = _head_columns(lse_ref[row_blocks[own], :])
                    delta_c = _head_columns(delta_ref[row_blocks[own], :])
                    p_parts, ds_parts = [], []
                    for r0 in range(0, 2 * N_BACK, BWD_CHUNK):
                        r = slice(r0, r0 + BWD_CHUNK)
                        mask = bias[which, r0 % N_BACK:r0 % N_BACK + BWD_CHUNK, :]
                        p_r = jnp.exp(ss[u][r] + mask - lse_c[r])
                        p_parts.append(p_r.astype(BF16))
                        ds_parts.append((p_r * (dps[u][r] - delta_c[r])).astype(BF16))
                    pbs.append(jnp.concatenate(p_parts, axis=0))
                    dss.append(jnp.concatenate(ds_parts, axis=0))
                dqs = [jnp.dot(dss[u], kws[u], preferred_element_type=F32) for u in range(U)]
                dkws = [lax.dot_general(dss[u], qss[u], TN, preferred_element_type=F32) for u in range(U)]
                dvws = [lax.dot_general(pbs[u], doss[u], TN, preferred_element_type=F32) for u in range(U)]
                dk_parts, dv_parts = [None] * len(row_blocks), [None] * len(row_blocks)
                for u, (before, own, _) in enumerate(blocks):
                    dq = _unstack(dqs[u])
                    if g == 0:
                        dq_ref[row_blocks[own], :] = dq
                    else:
                        dq_ref[row_blocks[own], :] += dq
                    for idx, dkp, dvp in ((before, dkws[u][:N_BACK], dvws[u][:N_BACK]),
                                          (own, dkws[u][N_BACK:], dvws[u][N_BACK:])):
                        dk_parts[idx] = dkp if dk_parts[idx] is None else dk_parts[idx] + dkp
                        dv_parts[idx] = dvp if dv_parts[idx] is None else dv_parts[idx] + dvp
                for idx, r in enumerate(row_blocks):
                    if g == 0 and first_writes_all:
                        dk_ref[r, :] = dk_parts[idx]
                        dv_ref[r, :] = dv_parts[idx]
                    else:
                        dk_ref[r, :] += dk_parts[idx]
                        dv_ref[r, :] += dv_parts[idx]
                return carry

            lax.fori_loop(0, d * nb // U, step, 0)

    col = pl.BlockSpec((S, 2 * HEAD), lambda j: (0, j))
    return pl.pallas_call(
        body, name="attn_bwd", grid=(q.shape[1] // (2 * HEAD),),
        in_specs=[col] * 6 + [pl.BlockSpec(memory_space=pl.ANY)], out_specs=[col] * 3,
        out_shape=[jax.ShapeDtypeStruct(q.shape, F32)] * 3,
        scratch_shapes=[pltpu.VMEM((2, N_BACK, 2 * N_BACK), F32)],
        compiler_params=_params("parallel"),
    )(q, k, v, dy, lse, delta, after)


def _shift_down(z, before, k):
    row = lax.broadcasted_iota(jnp.int32, z.shape, 0)
    out = pltpu.roll(z, k, 0)
    for i in range(k):
        out = jnp.where(row == i, before[8 - k + i:8 - k + i + 1, :], out)
    return out


def _shift_up(z, after, k):
    rows = z.shape[0]
    row = lax.broadcasted_iota(jnp.int32, z.shape, 0)
    out = pltpu.roll(z, rows - k, 0)
    for i in range(k):
        out = jnp.where(row == rows - k + i, after[i:i + 1, :], out)
    return out


def _conv_fwd(bcu, before, is_first, w):
    b, c, u = bcu[:, 0:CONV_W], bcu[:, CONV_W:2 * CONV_W], bcu[:, 2 * CONV_W:3 * CONV_W]
    z = c * u
    zb = jnp.where(is_first, 0.0, before[:, CONV_W:2 * CONV_W] * before[:, 2 * CONV_W:3 * CONV_W])
    z1, z2 = _shift_down(z, zb, 1), _shift_down(z, zb, 2)
    cv = w[0:1, :] * z2 + w[1:2, :] * z1 + w[2:3, :] * z
    return b, c, u, z, z1, z2, cv


def _halo_before(tm, width):
    return pl.BlockSpec((8, width), lambda i: (jnp.maximum(i * (tm // 8) - 1, 0), 0))


def _mix_fwd(ya, bcu, qx, mkv, conv_w, g_a, g_c, g_x, w_out, g_post, x, tm):
    S = x.shape[0]

    def body(ya_ref, bcu_ref, before_ref, qx_ref, mkv_ref, cw_ref, ga_ref, gc_ref, gx_ref,
             wo_ref, gp_ref, x_ref, yx_ref, ycat_ref, y2_ref, x1_ref):
        ya = ya_ref[...]
        b, _, _, _, _, _, cv = _conv_fwd(bcu_ref[...], before_ref[...], pl.program_id(0) == 0, cw_ref[...])
        yc = b * cv

        qxb, mkvb = qx_ref[...], mkv_ref[...]
        heads = [slice(HEAD * hd, HEAD * (hd + 1)) for hd in range(XATTN_W // HEAD)]
        ss = [lax.dot_general(qxb[:, sl], mkvb[:, sl], NT, preferred_element_type=F32) * SCALE for sl in heads]
        ms = [jnp.max(s, axis=1, keepdims=True) for s in ss]
        ps = [jnp.exp(s - m) for s, m in zip(ss, ms)]
        ls = [jnp.sum(p, axis=1, keepdims=True) for p in ps]
        os_ = [jnp.dot(p.astype(BF16), mkvb[:, XATTN_W + sl.start:XATTN_W + sl.stop], preferred_element_type=F32)
               for p, sl in zip(ps, heads)]
        for sl, o, l in zip(heads, os_, ls):
            yx_ref[:, sl] = o / l
        yx = yx_ref[...]

        ycat_ref[:, 0:ATTN_W] = (_rms_hat(ya)[0] * ga_ref[...]).astype(BF16)
        ycat_ref[:, ATTN_W:ATTN_W + CONV_W] = (_rms_hat(yc)[0] * gc_ref[...]).astype(BF16)
        ycat_ref[:, ATTN_W + CONV_W:D_MODEL] = (_rms_hat(yx)[0] * gx_ref[...]).astype(BF16)
        y2 = jnp.dot(ycat_ref[...], wo_ref[...], preferred_element_type=F32)
        y2_ref[...] = y2
        x1_ref[...] = x_ref[...] + _rms_hat(y2)[0] * gp_ref[...]

    n_mem = mkv.shape[0]
    return pl.pallas_call(
        body, name="mix_fwd", grid=(S // tm,),
        in_specs=[_rows(tm, ATTN_W), _rows(tm, 3 * CONV_W), _halo_before(tm, 3 * CONV_W), _rows(tm, XATTN_W),
                  _resident((n_mem, 2 * XATTN_W)), _resident((3, CONV_W)), _resident((1, ATTN_W)),
                  _resident((1, CONV_W)), _resident((1, XATTN_W)), _resident((D_MODEL, D_MODEL)),
                  _resident((1, D_MODEL)), _rows(tm, D_MODEL)],
        out_specs=[_rows(tm, XATTN_W), _rows(tm, D_MODEL), _rows(tm, D_MODEL), _rows(tm, D_MODEL)],
        out_shape=[jax.ShapeDtypeStruct((S, XATTN_W), F32), jax.ShapeDtypeStruct((S, D_MODEL), BF16),
                   jax.ShapeDtypeStruct((S, D_MODEL), F32), jax.ShapeDtypeStruct((S, D_MODEL), F32)],
        compiler_params=_params("parallel"),
    )(ya, bcu, bcu, qx, mkv, conv_w, g_a, g_c, g_x, w_out, g_post, x)


def _mlp_fwd_bwd(x1, target, g_pre, g_post, w_up, w_down, tm):
    S = x1.shape[0]
    n_ff = D_FF // SHARD_FF

    def body(x1_ref, t_ref, gpre_ref, gpost_ref, wup_ref, wdn_ref,
             h2_ref, f_ref, du_ref, df2_ref, dx1_ref, dgpre_ref, dgpost_ref, loss_ref, u_scr):
        @pl.when(pl.program_id(0) == 0)
        def _():
            dgpre_ref[...] = jnp.zeros_like(dgpre_ref)
            dgpost_ref[...] = jnp.zeros_like(dgpost_ref)
            loss_ref[...] = jnp.zeros_like(loss_ref)

        x1 = x1_ref[...]
        x1hat, r1 = _rms_hat(x1)
        h2 = (x1hat * gpre_ref[...]).astype(BF16)
        h2_ref[...] = h2
        f2 = jnp.zeros((tm, D_MODEL), F32)
        for j in range(n_ff):
            cols = slice(SHARD_FF * j, SHARD_FF * (j + 1))
            u = jnp.maximum(jnp.dot(h2, wup_ref[j], preferred_element_type=F32), 0.0)
            u_scr[:, cols] = u
            f = (u * u).astype(BF16)
            f_ref[:, cols] = f
            f2 = f2 + jnp.dot(f, wdn_ref[cols, :], preferred_element_type=F32)
        f2hat, r2 = _rms_hat(f2)
        err = x1 + f2hat * gpost_ref[...] - t_ref[...]
        loss_ref[...] += 0.5 * jnp.sum(jnp.mean(err * err, axis=-1, keepdims=True), axis=0, keepdims=True)
        dx2 = err * (1.0 / D_MODEL)
        dgpost_ref[...] += jnp.sum(dx2 * f2hat, axis=0, keepdims=True)
        df2 = _rms_bwd(f2hat, r2, gpost_ref[...], dx2).astype(BF16)
        df2_ref[...] = df2
        dh2 = jnp.zeros((tm, D_MODEL), F32)
        for j in range(n_ff):
            cols = slice(SHARD_FF * j, SHARD_FF * (j + 1))
            df = lax.dot_general(df2, wdn_ref[cols, :], NT, preferred_element_type=F32)
            du = (2.0 * u_scr[:, cols] * df).astype(BF16)
            du_ref[:, cols] = du
            dh2 = dh2 + lax.dot_general(du, wup_ref[j], NT, preferred_element_type=F32)
        dgpre_ref[...] += jnp.sum(dh2 * x1hat, axis=0, keepdims=True)
        dx1_ref[...] = dx2 + _rms_bwd(x1hat, r1, gpre_ref[...], dh2)

    acc = pl.BlockSpec((1, D_MODEL), lambda i: (0, 0))
    return pl.pallas_call(
        body, name="mlp_fwd_bwd", grid=(S // tm,),
        in_specs=[_rows(tm, D_MODEL), _rows(tm, D_MODEL), _resident((1, D_MODEL)), _resident((1, D_MODEL)),
                  _resident((n_ff, D_MODEL, SHARD_FF)), _resident((D_FF, D_MODEL))],
        out_specs=[_rows(tm, D_MODEL), _rows(tm, D_FF), _rows(tm, D_FF), _rows(tm, D_MODEL), _rows(tm, D_MODEL),
                   acc, acc, pl.BlockSpec((1, 1), lambda i: (0, 0))],
        out_shape=[jax.ShapeDtypeStruct((S, D_MODEL), BF16), jax.ShapeDtypeStruct((S, D_FF), BF16),
                   jax.ShapeDtypeStruct((S, D_FF), BF16), jax.ShapeDtypeStruct((S, D_MODEL), BF16),
                   jax.ShapeDtypeStruct((S, D_MODEL), F32), jax.ShapeDtypeStruct((1, D_MODEL), F32),
                   jax.ShapeDtypeStruct((1, D_MODEL), F32), jax.ShapeDtypeStruct((1, 1), F32)],
        scratch_shapes=[pltpu.VMEM((tm, D_FF), F32)],
        compiler_params=_params("arbitrary"),
    )(x1, target, g_pre, g_post, w_up, w_down)


def _weight_grad(name, a, b, rows_sharded, after):
    S, K = a.shape
    N = b.shape[1]
    if rows_sharded:
        tk, tn = K // N_CHIPS, N
        a_spec = pl.BlockSpec((S, tk), lambda j: (0, j))
        b_spec = pl.BlockSpec((S, tn), lambda j: (0, 0), pipeline_mode=pl.Buffered(1))
    else:
        tk, tn = K, N // N_CHIPS
        a_spec = pl.BlockSpec((S, tk), lambda j: (0, 0), pipeline_mode=pl.Buffered(1))
        b_spec = pl.BlockSpec((S, tn), lambda j: (0, j))
    half = tk // 2

    def body(a_ref, b_ref, after_ref, o_ref):
        res = lax.dot_general(a_ref[...], b_ref[...], TN, preferred_element_type=F32)
        o_ref[0, 0] = res[:half]
        o_ref[1, 0] = res[half:]

    return pl.pallas_call(
        body, name=name, grid=(N_CHIPS,), in_specs=[a_spec, b_spec, pl.BlockSpec(memory_space=pl.ANY)],
        out_specs=pl.BlockSpec((2, 1, half, tn), lambda j: (0, j, 0, 0)),
        out_shape=jax.ShapeDtypeStruct((2, N_CHIPS, half, tn), F32),
        compiler_params=_params("parallel"),
    )(a, b, after)


def _mixer_bwd(dx1, y2, ycat, ya, yx, bcu, qx, mkv, conv_w, g_a, g_c, g_x, w_out, g_post, after, tm):
    S = dx1.shape[0]
    n_mem = mkv.shape[0]
    n_tiles = S // tm
    half = D_MODEL // N_CHIPS // 2

    def body(dx1_ref, y2_ref, ycat_ref, ya_ref, yx_ref, bcu_ref, before_ref, qx_ref, mkv_ref, cw_ref, ga_ref, gc_ref,
             gx_ref, wo_ref, gp_ref, after_ref, gwo_ref, dya_ref, delta_ref, tail_ref, dmkv_ref, dcw_ref, dgp_ref,
             dga_ref, dgc_ref, dgx_ref, carry):
        step = pl.program_id(0)
        first_tile = step == n_tiles - 1

        @pl.when(step == 0)
        def _():
            for ref in (gwo_ref, dmkv_ref, dcw_ref, dgp_ref, dga_ref, dgc_ref, dgx_ref, carry):
                ref[...] = jnp.zeros_like(ref)

        dx1 = dx1_ref[...]
        y2hat, r2 = _rms_hat(y2_ref[...])
        dgp_ref[...] += jnp.sum(dx1 * y2hat, axis=0, keepdims=True)
        dy2 = _rms_bwd(y2hat, r2, gp_ref[...], dx1).astype(BF16)
        gwo = lax.dot_general(ycat_ref[...], dy2, TN, preferred_element_type=F32)
        for k in range(2 * N_CHIPS):
            gwo_ref[k % 2, k // 2] += gwo[half * k:half * (k + 1)]
        dycat = lax.dot_general(dy2, wo_ref[...], NT, preferred_element_type=F32)

        d_na = dycat[:, 0:ATTN_W]
        ya = ya_ref[...]
        yahat, ra = _rms_hat(ya)
        dga_ref[...] += jnp.sum(d_na * yahat, axis=0, keepdims=True)
        dya = _rms_bwd(yahat, ra, ga_ref[...], d_na)
        dya_ref[...] = dya
        prod = dya * ya
        hi = prod.astype(BF16)
        lo = (prod - hi.astype(F32)).astype(BF16)
        head_of = lambda axis: lax.shift_right_logical(lax.broadcasted_iota(jnp.int32, (ATTN_W, ATTN_W), axis),
                                                       HEAD.bit_length() - 1)
        ones = jnp.where(head_of(0) == head_of(1), 1.0, 0.0).astype(BF16)
        delta_ref[...] = jnp.dot(hi, ones, preferred_element_type=F32) + jnp.dot(lo, ones, preferred_element_type=F32)

        w = cw_ref[...]
        b, c, u, z, z1, z2, cv = _conv_fwd(bcu_ref[...], before_ref[...], first_tile, w)
        d_nc = dycat[:, ATTN_W:ATTN_W + CONV_W]
        ychat, rc = _rms_hat(b * cv)
        dgc_ref[...] += jnp.sum(d_nc * ychat, axis=0, keepdims=True)
        dyc = _rms_bwd(ychat, rc, gc_ref[...], d_nc)
        dcv = dyc * b
        behind = carry[...]
        dz = w[2:3, :] * dcv + w[1:2, :] * _shift_up(dcv, behind, 1) + w[0:1, :] * _shift_up(dcv, behind, 2)
        carry[...] = dcv[0:8, :]
        dcw_ref[0:1, :] += jnp.sum(dcv * z2, axis=0, keepdims=True)
        dcw_ref[1:2, :] += jnp.sum(dcv * z1, axis=0, keepdims=True)
        dcw_ref[2:3, :] += jnp.sum(dcv * z, axis=0, keepdims=True)
        tail_ref[:, 0:CONV_W] = (dyc * cv).astype(BF16)
        tail_ref[:, CONV_W:2 * CONV_W] = (dz * u).astype(BF16)
        tail_ref[:, 2 * CONV_W:3 * CONV_W] = (dz * c).astype(BF16)

        d_nx = dycat[:, ATTN_W + CONV_W:D_MODEL]
        yxhat, rx = _rms_hat(yx_ref[...])
        dgx_ref[...] += jnp.sum(d_nx * yxhat, axis=0, keepdims=True)
        dyx = _rms_bwd(yxhat, rx, gx_ref[...], d_nx)
        qxb, mkvb = qx_ref[...], mkv_ref[...]
        heads = [slice(HEAD * hd, HEAD * (hd + 1)) for hd in range(XATTN_W // HEAD)]
        values = [slice(XATTN_W + sl.start, XATTN_W + sl.stop) for sl in heads]
        ss = [lax.dot_general(qxb[:, sl], mkvb[:, sl], NT, preferred_element_type=F32) * SCALE for sl in heads]
        es = [jnp.exp(s - jnp.max(s, axis=1, keepdims=True)) for s in ss]
        ps = [e / jnp.sum(e, axis=1, keepdims=True) for e in es]
        dobs = [dyx[:, sl].astype(BF16) for sl in heads]
        dps = [lax.dot_general(dob, mkvb[:, vsl], NT, preferred_element_type=F32) for dob, vsl in zip(dobs, values)]
        dss = [(p * (dp - jnp.sum(p * dp, axis=1, keepdims=True)) * SCALE).astype(BF16) for p, dp in zip(ps, dps)]
        for sl, vsl, p, dob, ds in zip(heads, values, ps, dobs, dss):
            tail_ref[:, 3 * CONV_W + sl.start:3 * CONV_W + sl.stop] = jnp.dot(
                ds, mkvb[:, sl], preferred_element_type=F32).astype(BF16)
            dmkv_ref[:, sl] += lax.dot_general(ds, qxb[:, sl], TN, preferred_element_type=F32)
            dmkv_ref[:, vsl] += lax.dot_general(p.astype(BF16), dob, TN, preferred_element_type=F32)

    rows = lambda width: pl.BlockSpec((tm, width), lambda i: (n_tiles - 1 - i, 0))
    before = pl.BlockSpec((8, 3 * CONV_W), lambda i: (jnp.maximum((n_tiles - 1 - i) * (tm // 8) - 1, 0), 0))
    acc = lambda r, w: pl.BlockSpec((r, w), lambda i: (0, 0))
    return pl.pallas_call(
        body, name="mixer_bwd", grid=(n_tiles,),
        in_specs=[rows(D_MODEL), rows(D_MODEL), rows(D_MODEL), rows(ATTN_W), rows(XATTN_W), rows(3 * CONV_W), before,
                  rows(XATTN_W), _resident((n_mem, 2 * XATTN_W)), _resident((3, CONV_W)), _resident((1, ATTN_W)),
                  _resident((1, CONV_W)), _resident((1, XATTN_W)), _resident((D_MODEL, D_MODEL)),
                  _resident((1, D_MODEL)), pl.BlockSpec(memory_space=pl.ANY)],
        out_specs=[pl.BlockSpec((2, N_CHIPS, half, D_MODEL), lambda i: (0, 0, 0, 0)), rows(ATTN_W), rows(ATTN_W),
                   rows(3 * CONV_W + XATTN_W), acc(n_mem, 2 * XATTN_W),
                   acc(3, CONV_W), acc(1, D_MODEL), acc(1, ATTN_W), acc(1, CONV_W), acc(1, XATTN_W)],
        out_shape=[jax.ShapeDtypeStruct((2, N_CHIPS, half, D_MODEL), F32), jax.ShapeDtypeStruct((S, ATTN_W), F32),
                   jax.ShapeDtypeStruct((S, ATTN_W), F32), jax.ShapeDtypeStruct((S, 3 * CONV_W + XATTN_W), BF16),
                   jax.ShapeDtypeStruct((n_mem, 2 * XATTN_W), F32), jax.ShapeDtypeStruct((3, CONV_W), F32),
                   jax.ShapeDtypeStruct((1, D_MODEL), F32), jax.ShapeDtypeStruct((1, ATTN_W), F32),
                   jax.ShapeDtypeStruct((1, CONV_W), F32), jax.ShapeDtypeStruct((1, XATTN_W), F32)],
        scratch_shapes=[pltpu.VMEM((8, CONV_W), F32)],
        compiler_params=_params("arbitrary"),
    )(dx1, y2, ycat, ya, yx, bcu, bcu, qx, mkv, conv_w, g_a, g_c, g_x, w_out, g_post, after)


def _memkv_bwd(mem, g_mem, w_kv, dmkv):
    n_mem = mem.shape[0]
    half = D_MODEL // N_CHIPS // 2

    def body(mem_ref, g_ref, w_ref, d_ref, dw_ref, dg_ref):
        mhat, _ = _rms_hat(mem_ref[...])
        mn = (mhat * g_ref[...]).astype(BF16)
        d = d_ref[...].astype(BF16)
        for k in range(2 * N_CHIPS):
            dw_ref[k % 2, k // 2] = lax.dot_general(mn[:, half * k:half * (k + 1)], d, TN, preferred_element_type=F32)
        dmn = lax.dot_general(d, w_ref[...], NT, preferred_element_type=F32)
        dg_ref[...] = jnp.sum(dmn * mhat, axis=0, keepdims=True)

    return pl.pallas_call(
        body, name="memkv_bwd",
        out_shape=[jax.ShapeDtypeStruct((2, N_CHIPS, half, 2 * XATTN_W), F32), jax.ShapeDtypeStruct((1, D_MODEL), F32)],
        compiler_params=pltpu.CompilerParams(vmem_limit_bytes=VMEM_LIMIT_V7X),
    )(mem, g_mem, w_kv, dmkv)


def _in_proj_bwd(dqkv, tail, cos, sin, w_in, x, h, g, dx1, after, tm, sums=None):
    S = x.shape[0]
    step_w = 2 * 256
    half = D_MODEL // 2
    n_steps = S // tm
    sum_grads, sum_sibling, sum_others, place = sums if sums is not None else ([], [], [], jnp.zeros((2,), jnp.int32))
    k = len(sum_grads)

    def body(place_ref, dq_ref, dk_ref, dv_ref, tail_ref, cos_ref, sin_ref, w_hbm, x_ref, h_ref, g_ref, dx1_ref,
             after_ref, *refs):
        sum_refs, (dx_ref, gw_ref, dg_ref), sum_out_refs = refs[:5 * k], refs[5 * k:5 * k + 3], refs[5 * k + 3:6 * k + 3]
        dproj_ref, w_full, sems = refs[6 * k + 3:]
        for a in range(k):
            acc = sum_refs[a][0, 0] + sum_refs[k + a][0]
            for other_ref in sum_refs[2 * k + 3 * a:2 * k + 3 * a + 3]:
                acc = acc + other_ref[0].astype(F32)
            sum_out_refs[a][0] = acc
        _side_by_side(w_hbm, w_full, sems)

        @pl.when(pl.program_id(0) == 0)
        def _():
            dg_ref[...] = jnp.zeros_like(dg_ref)
            gw_ref[...] = jnp.zeros_like(gw_ref)

        halves = [slice(0, tm // 2), slice(tm // 2, tm)]
        for rows in halves:
            c, s = cos_ref[rows, :], sin_ref[rows, :]
            for j in range(ATTN_W // 128):
                cols = slice(128 * j, 128 * (j + 1))
                dproj_ref[rows, cols] = _rope128(dq_ref[rows, cols] * SCALE, c, s, True).astype(BF16)
                dproj_ref[rows, ATTN_W + 128 * j:ATTN_W + 128 * (j + 1)] = _rope128(dk_ref[rows, cols], c, s, True).astype(BF16)
            dproj_ref[rows, 2 * ATTN_W:3 * ATTN_W] = dv_ref[rows, :].astype(BF16)
            dproj_ref[rows, 3 * ATTN_W:PROJ_W] = tail_ref[rows, :]
        dhs = [lax.dot_general(dproj_ref[rows, :], w_full[...], NT, preferred_element_type=F32) for rows in halves]
        for rows, dh in zip(halves, dhs):
            xhat, r = _rms_hat(x_ref[rows, :])
            dg_ref[...] += jnp.sum(dh * xhat, axis=0, keepdims=True)
            dx_ref[rows, :] = dx1_ref[rows, :] + _rms_bwd(xhat, r, g_ref[...], dh)
        hb = h_ref[...]
        for step in range(PROJ_W // step_w):
            res = lax.dot_general(hb, dproj_ref[:, step * step_w:(step + 1) * step_w], TN, preferred_element_type=F32)
            lo = step * step_w
            while lo < (step + 1) * step_w:
                chip = lo // SHARD_IN
                hi = min((step + 1) * step_w, (chip + 1) * SHARD_IN)
                for hh in range(2):
                    gw_ref[hh, chip, :, lo - chip * SHARD_IN:hi - chip * SHARD_IN] += (
                        res[half * hh:half * (hh + 1), lo - step * step_w:hi - step * step_w])
                lo = hi

    whole = lambda shape: pl.BlockSpec(shape, lambda i, p: (0,) * len(shape))
    slab = lambda t: (1, t.shape[-2] // n_steps, t.shape[-1])
    sum_specs = ([pl.BlockSpec((1,) + slab(t), lambda i, p: (p[0], p[1], i, 0)) for t in sum_grads]
                 + [pl.BlockSpec(slab(t), lambda i, p: (p[1], i, 0)) for t in sum_grads]
                 + [pl.BlockSpec(slab(t), lambda i, p, j=j: (j, i, 0)) for t in sum_grads for j in range(3)])
    results = pl.pallas_call(
        body, name="in_proj_bwd",
        out_shape=[jax.ShapeDtypeStruct((S, D_MODEL), F32), jax.ShapeDtypeStruct((2, N_CHIPS, half, SHARD_IN), F32),
                   jax.ShapeDtypeStruct((1, D_MODEL), F32)]
        + [jax.ShapeDtypeStruct((2,) + t.shape[2:], F32) for t in sum_grads],
        grid_spec=pltpu.PrefetchScalarGridSpec(
            num_scalar_prefetch=1, grid=(n_steps,),
            in_specs=[_rows(tm, ATTN_W)] * 3 + [_rows(tm, PROJ_W - 3 * ATTN_W), _rows(tm, 128), _rows(tm, 128),
                      pl.BlockSpec(memory_space=pl.ANY), _rows(tm, D_MODEL), _rows(tm, D_MODEL),
                      _resident((1, D_MODEL)), _rows(tm, D_MODEL), pl.BlockSpec(memory_space=pl.ANY)] + sum_specs,
            out_specs=[_rows(tm, D_MODEL), whole((2, N_CHIPS, half, SHARD_IN)), whole((1, D_MODEL))]
            + [pl.BlockSpec(slab(t), lambda i, p: (p[0], i, 0)) for t in sum_grads],
            scratch_shapes=[pltpu.VMEM((tm, PROJ_W), BF16), pltpu.VMEM((D_MODEL, PROJ_W), BF16),
                            pltpu.SemaphoreType.DMA((N_CHIPS,))]),
        compiler_params=_params("arbitrary"),
    )(place, *dqkv, tail, cos, sin, w_in, x, h, g, dx1, after, *sum_grads, *sum_sibling,
      *[o for o in sum_others for _ in range(3)])
    return [*results[:3], list(results[3:])]


def _row_tile(rows):
    return ROW_TILE if rows % ROW_TILE == 0 else rows


def _chip_sums_bf16(name, grads, from_sibling, place):
    k = len(grads)
    _, n, rows, _ = grads[0].shape
    tr = _row_tile(rows)

    def body(place_ref, *refs):
        for g_ref, b_ref, o_ref in zip(refs[:k], refs[k:2 * k], refs[2 * k:]):
            o_ref[...] = (g_ref[0] + b_ref[...]).astype(BF16)

    mine = lambda g: pl.BlockSpec((1, 1, tr, g.shape[3]), lambda s, i, p: (p[0], s, i, 0))
    slab = lambda g: pl.BlockSpec((1, tr, g.shape[3]), lambda s, i, p: (s, i, 0))
    return pl.pallas_call(
        body, name=name, out_shape=[jax.ShapeDtypeStruct(g.shape[1:], BF16) for g in grads],
        grid_spec=pltpu.PrefetchScalarGridSpec(
            num_scalar_prefetch=1, grid=(n, rows // tr),
            in_specs=[mine(g) for g in grads] + [slab(g) for g in grads], out_specs=[slab(g) for g in grads]),
        compiler_params=_params("parallel", "parallel"),
    )(place, *grads, *from_sibling)


def _final_sums(name, grads, from_sibling, others, place):
    k = len(grads)
    rows = grads[0].shape[2]
    tr = _row_tile(rows)

    def body(place_ref, *refs):
        for a in range(k):
            own_ref, sib_ref = refs[a], refs[k + a]
            acc = own_ref[0, 0] + sib_ref[0]
            for o in refs[2 * k + 3 * a:2 * k + 3 * a + 3]:
                acc = acc + o[0].astype(F32)
            refs[5 * k + a][0] = acc

    own = lambda g: pl.BlockSpec((1, 1, tr, g.shape[3]), lambda i, p: (p[0], p[1], i, 0))
    sib = lambda g: pl.BlockSpec((1, tr, g.shape[3]), lambda i, p: (p[1], i, 0))
    other = lambda g, j: pl.BlockSpec((1, tr, g.shape[3]), lambda i, p: (j, i, 0))
    return pl.pallas_call(
        body, name=name, out_shape=[jax.ShapeDtypeStruct((2,) + g.shape[2:], F32) for g in grads],
        grid_spec=pltpu.PrefetchScalarGridSpec(
            num_scalar_prefetch=1, grid=(rows // tr,),
            in_specs=[own(g) for g in grads] + [sib(g) for g in grads] + [other(g, j) for g in grads for j in range(3)],
            out_specs=[pl.BlockSpec((1, tr, g.shape[3]), lambda i, p: (p[0], i, 0)) for g in grads]),
        compiler_params=_params("parallel"),
    )(place, *grads, *from_sibling, *[o for o in others for _ in range(3)])


def _adamw_update(w, g, m, v):
    m = ADAM_B1 * m + (1.0 - ADAM_B1) * g
    v = ADAM_B2 * v + (1.0 - ADAM_B2) * (g * g)
    m_hat = m * (1.0 / (1.0 - ADAM_B1 ** ADAM_STEP))
    v_hat = v * (1.0 / (1.0 - ADAM_B2 ** ADAM_STEP))
    return -ADAM_LR * (m_hat / (jnp.sqrt(v_hat) + ADAM_EPS) + ADAM_WD * w), m, v


def _adamw(name, params, after):
    k = len(params)
    rows = params[0][0].shape[0]
    tr = ADAMW_ROW_TILE if rows % ADAMW_ROW_TILE == 0 else rows

    def body(*refs):
        ins, outs = refs[:4 * k], refs[4 * k + 1:]
        for a in range(k):
            w_ref, g_ref, m_ref, v_ref = ins[4 * a:4 * a + 4]
            g = g_ref[...]
            outs[4 * a][...] = g
            outs[4 * a + 1][...], outs[4 * a + 2][...], outs[4 * a + 3][...] = _adamw_update(w_ref[...], g, m_ref[...], v_ref[...])

    spec = lambda w: pl.BlockSpec((tr, w.shape[1]), lambda i: (i, 0))
    out = pl.pallas_call(
        body, name=name, grid=(rows // tr,),
        in_specs=[spec(p[0]) for p in params for _ in range(4)] + [pl.BlockSpec(memory_space=pl.ANY)],
        out_specs=[spec(p[0]) for p in params for _ in range(4)],
        out_shape=[jax.ShapeDtypeStruct(p[0].shape, F32) for p in params for _ in range(4)],
        compiler_params=_params("parallel"),
    )(*[t for p in params for t in p], after)
    return [out[4 * a:4 * a + 4] for a in range(k)]


def _small_update(blocks, chip, gains, gains_m, gains_v, taps, taps_m, taps_v):
    n = len(gains)
    widths = [g.shape[1] for g in gains]
    k, w = taps.shape

    def body(*refs):
        chip_ref, blocks_ref = refs[0], refs[1]
        params = [refs[2 + 3 * i:5 + 3 * i] for i in range(n + 1)]
        outs = [refs[2 + 3 * (n + 1) + 4 * i:2 + 3 * (n + 1) + 4 * (i + 1)] for i in range(n + 1)]
        loss_ref = refs[-1]
        summed = blocks_ref[0]
        for device in range(1, blocks.shape[0]):
            summed = summed + blocks_ref[device]
        for i in range(n):
            g = summed[i:i + 1, 0:widths[i]]
            wr, mr, vr = params[i]
            outs[i][0][...] = g
            outs[i][1][...], outs[i][2][...], outs[i][3][...] = _adamw_update(wr[...], g, mr[...], vr[...])
        g = summed[n:n + k, 0:w]
        for j in range(1, N_CHIPS):
            g = jnp.where(chip_ref[0] == j, summed[n:n + k, w * j:w * (j + 1)], g)
        wr, mr, vr = params[n]
        for out_ref, val in zip(outs[n], (g, *_adamw_update(wr[...], g, mr[...], vr[...]))):
            for j in range(k):
                out_ref[j] = val[j:j + 1, :]
        loss_ref[...] = summed[n + k:n + k + 1, 0:1]

    vmem = pl.BlockSpec(memory_space=pltpu.VMEM)
    operands = [chip, blocks]
    for p in zip(list(gains) + [taps], list(gains_m) + [taps_m], list(gains_v) + [taps_v]):
        operands += list(p)
    shapes = [jax.ShapeDtypeStruct(shape, F32) for shape in [g.shape for g in gains] + [(k, 1, w)] for _ in range(4)]
    out = pl.pallas_call(
        body, name="small_update", out_shape=shapes + [jax.ShapeDtypeStruct((1, 1), F32)],
        in_specs=[pl.BlockSpec(memory_space=pltpu.SMEM)] + [vmem] * (len(operands) - 1),
        out_specs=[vmem] * (len(shapes) + 1),
    )(*operands)
    return [out[4 * i:4 * (i + 1)] for i in range(n + 1)], out[-1]


def _place():
    return lax.axis_index("x"), lax.axis_index("y"), lax.axis_index("c")


def _other_chips(x, y):
    return [(1 - x, y), (x, 1 - y), (1 - x, 1 - y)]


def _allgather_finish(name, shards, landed, pass_on):
    n = len(shards)

    def body(*refs):
        ins, outs, stage = refs[:n], refs[2 * n:3 * n], refs[3 * n:4 * n]
        send_sems, recv_sems, local_sems = refs[4 * n:]
        x, y, c = _place()
        chips = _other_chips(x, y)
        barrier = pltpu.get_barrier_semaphore()
        pl.semaphore_signal(barrier, inc=1, device_id=(x, y, 1 - c), device_id_type=MESH)
        pl.semaphore_wait(barrier, 1)

        def copy(a, k, chip, half):
            place = outs[a].at[2 * chip[0] + chip[1], half]
            return pltpu.make_async_remote_copy(
                src_ref=place, dst_ref=place, send_sem=send_sems.at[3 * a + k], recv_sem=recv_sems.at[3 * a + k],
                device_id=(x, y, 1 - c), device_id_type=MESH)

        load = [pltpu.make_async_copy(ins[a], stage[a], local_sems.at[a]) for a in range(n)]
        local = [pltpu.make_async_copy(stage[a], outs[a].at[2 * x + y], local_sems.at[a]) for a in range(n)]
        for cp in load:
            cp.start()
        passed = [copy(a, k, chip, c) for a in range(n) if pass_on[a] for k, chip in enumerate(chips)]
        for cp in passed:
            cp.start()
        for a in range(n):
            load[a].wait()
            local[a].start()
        for a in range(n):
            if pass_on[a]:
                for k, chip in enumerate(chips):
                    copy(a, k, chip, 1 - c).wait_recv()
        for cp in passed:
            cp.wait_send()
        for cp in local:
            cp.wait()

    any_spec = pl.BlockSpec(memory_space=pl.ANY)
    return pl.pallas_call(
        body, name=name,
        out_shape=[jax.ShapeDtypeStruct((N_CHIPS,) + s.shape, s.dtype) for s in shards],
        in_specs=[any_spec] * (2 * n), out_specs=[any_spec] * n,
        input_output_aliases={n + a: a for a in range(n)},
        scratch_shapes=[pltpu.VMEM(s.shape, s.dtype) for s in shards]
        + [pltpu.SemaphoreType.DMA((3 * n,)), pltpu.SemaphoreType.DMA((3 * n,)), pltpu.SemaphoreType.DMA((n,))],
        compiler_params=pltpu.CompilerParams(vmem_limit_bytes=VMEM_LIMIT_V7X, collective_id=HANDSHAKES["sibling"][0]),
    )(*shards, *landed)


def _plan_first_hop(x, y, c, shards, lands):
    return [(shards[a].at[c], lands[a].at[2 * x + y, c], lands[a].at[2 * chip[0] + chip[1], c], (*chip, c))
            for a in range(len(shards)) for chip in _other_chips(x, y)]


def _plan_pass_on(x, y, c, nothing, lands):
    def place(a, chip, half):
        return lands[a].at[2 * chip[0] + chip[1], half]

    return [(place(a, chip, c), place(a, chip, c), place(a, chip, 1 - c), (x, y, 1 - c))
            for a in range(len(lands)) for chip in _other_chips(x, y)]


def _plan_own_half_to_sibling(x, y, c, nothing, lands):
    return [(lands[a].at[c], lands[a].at[c], lands[a].at[1 - c], (x, y, 1 - c)) for a in range(len(lands))]


def _plan_other_half_to_sibling(x, y, c, grads, lands):
    return [(grads[a].at[1 - c], lands[a], lands[a], (x, y, 1 - c)) for a in range(len(grads))]


def _plan_to_other_chips(x, y, c, partials, lands):
    return [(partials[a].at[2 * chip[0] + chip[1]], lands[a].at[k], lands[a].at[k], (*chip, c))
            for a in range(len(partials)) for k, chip in enumerate(_other_chips(x, y))]


def _plan_to_all(x, y, c, blocks, lands):
    flips = [(fx, fy, fc) for fx in (0, 1) for fy in (0, 1) for fc in (0, 1) if (fx, fy, fc) != (0, 0, 0)]
    peers = [(1 - x if fx else x, 1 - y if fy else y, 1 - c if fc else c) for fx, fy, fc in flips]
    return [(blocks[0], lands[0].at[4 * x + 2 * y + c], lands[0].at[4 * p[0] + 2 * p[1] + p[2]], p) for p in peers]


def _planned_copies(plan, srcs, lands, send_sems, recv_sems):
    x, y, c = _place()

    def pair(k, src, there, here, to):
        make = lambda dst: pltpu.make_async_remote_copy(
            src_ref=src, dst_ref=dst, send_sem=send_sems.at[k], recv_sem=recv_sems.at[k], device_id=to, device_id_type=MESH)
        return make(there), make(here)

    return [pair(k, *entry) for k, entry in enumerate(plan(x, y, c, srcs, lands))]


_HBM_SPEC = pl.BlockSpec(memory_space=pltpu.HBM)
_SEM_SPEC = pl.BlockSpec(memory_space=pltpu.SEMAPHORE)


def _hbm(a):
    return pltpu.with_memory_space_constraint(a, pltpu.HBM)


HANDSHAKES = {
    "sibling": (1, lambda x, y, c: [(x, y, 1 - c)]),
}


def _exchange_start(name, plan, n_copies, srcs, land_shapes, after, lands=None, peers=None):
    if lands is None:
        lands = [lax.empty(s.shape, s.dtype) for s in land_shapes]
    land_shapes = lands
    ns, nl = len(srcs), len(land_shapes)
    n_in = ns + nl + 1
    collective_id, peers_of = HANDSHAKES[peers] if peers else (None, None)

    def body(*refs):
        if peers:
            who = peers_of(*_place())
            barrier = pltpu.get_barrier_semaphore()
            for peer in who:
                pl.semaphore_signal(barrier, inc=1, device_id=peer, device_id_type=MESH)
            pl.semaphore_wait(barrier, len(who))
        for send, _ in _planned_copies(plan, refs[:ns], refs[ns:ns + nl], refs[n_in], refs[n_in + 1]):
            send.start()
        refs[-1][...] = jnp.zeros_like(refs[-1])

    out = pl.pallas_call(
        body, name=name,
        out_shape=(pltpu.SemaphoreType.DMA((n_copies,)), pltpu.SemaphoreType.DMA((n_copies,)),
                   *[pltpu.HBM(s.shape, s.dtype) for s in land_shapes], jax.ShapeDtypeStruct((8, 128), F32)),
        in_specs=[_HBM_SPEC] * (ns + nl) + [pl.BlockSpec(memory_space=pl.ANY)],
        out_specs=(_SEM_SPEC, _SEM_SPEC, *[_HBM_SPEC] * nl, pl.BlockSpec(memory_space=pltpu.VMEM)),
        input_output_aliases={ns + i: 2 + i for i in range(nl)},
        compiler_params=pltpu.CompilerParams(has_side_effects=pltpu.SideEffectType.DATAFLOW_SIDE_EFFECTING,
                                             collective_id=collective_id),
    )(*[_hbm(s) for s in srcs], *[_hbm(l) for l in lands], after)
    return out[0], out[1], list(out[2:2 + nl]), out[-1]


def _exchange_wait(name, plan, srcs, started, after):
    send_sems, recv_sems, lands, _ = started
    ns, nl = len(srcs), len(lands)
    after = list(after) if isinstance(after, (list, tuple)) else [after]

    def body(*refs):
        for send, recv in _planned_copies(plan, refs[:ns], refs[ns:ns + nl], refs[ns + nl], refs[ns + nl + 1]):
            send.wait_send()
            recv.wait_recv()

    return pl.pallas_call(
        body, name=name, out_shape=[pltpu.HBM(l.shape, l.dtype) for l in lands],
        in_specs=[_HBM_SPEC] * (ns + nl) + [_SEM_SPEC, _SEM_SPEC] + [pl.BlockSpec(memory_space=pl.ANY)] * len(after),
        out_specs=[_HBM_SPEC] * nl, input_output_aliases={ns + i: i for i in range(nl)},
        compiler_params=pltpu.CompilerParams(has_side_effects=pltpu.SideEffectType.DATAFLOW_SIDE_EFFECTING),
    )(*[_hbm(s) for s in srcs], *lands, send_sems, recv_sems, *after)


def _like(arrays, lead, dtype=None):
    return [jax.ShapeDtypeStruct(tuple(lead) + a.shape[-2:], dtype or a.dtype) for a in arrays]


class _StepExchanges:
    def __init__(self, mats, conv_w):
        x, y, c = _place()
        self.place = jnp.stack([c, 2 * x + y]).astype(jnp.int32)
        shards = [w.astype(BF16).reshape(2, w.shape[0] // 2, w.shape[1]) for w in mats]
        self._in_shard = shards[:1]
        self._in = _exchange_start("w_in_allgather_start", _plan_first_hop, 3, self._in_shard,
                                   _like(self._in_shard, (N_CHIPS, 2)), shards[0])
        self.zero = self._in[3]
        taps = jnp.pad(conv_w, ((0, 8 - conv_w.shape[0]), (0, 128 - conv_w.shape[1])))
        self._rest_shards = shards[1:] + [jnp.stack([taps, jnp.zeros_like(taps)])]
        self._taps_shape = conv_w.shape
        self._groups = {}

    def w_in(self, after):
        landed = _exchange_wait("w_in_allgather_wait", _plan_first_hop, self._in_shard, self._in,
                                list(after) + self._rest_shards)
        (w_in,) = _allgather_finish("w_in_allgather_finish", self._in_shard, landed, [True])
        self._rest = _exchange_start("rest_allgather_start", _plan_first_hop, 3 * len(self._rest_shards),
                                     self._rest_shards, _like(self._rest_shards, (N_CHIPS, 2)), w_in)
        self.zero = self._rest[3]
        return w_in.reshape(N_CHIPS, 2 * w_in.shape[2], w_in.shape[3])

    def rest_weights(self, after):
        landed = _exchange_wait("rest_allgather_wait", _plan_first_hop, self._rest_shards, self._rest, after)
        kv, out, up, down, taps = _allgather_finish("rest_allgather_finish", self._rest_shards, landed,
                                                    [True, True, False, False, True])
        self._up_down = _exchange_start("up_down_pass_on_start", _plan_pass_on, 6, [], None, self.zero, lands=[up, down],
                                        peers="sibling")
        self.zero = self._up_down[3]
        k, w = self._taps_shape
        taps = taps[:, 0, :k, :w].transpose(1, 0, 2).reshape(k, N_CHIPS * w)
        return [g.reshape(N_CHIPS, 2 * g.shape[2], g.shape[3]) for g in (kv, out)], taps

    def up_down(self, after):
        full = _exchange_wait("up_down_pass_on_wait", _plan_pass_on, [], self._up_down, after)
        return [g.reshape(N_CHIPS, 2 * g.shape[2], g.shape[3]) for g in full]

    def send_grads(self, key, grads):
        grads = list(grads)
        started = _exchange_start(f"{key}_grads_to_sibling_start", _plan_other_half_to_sibling, len(grads), grads,
                                  _like(grads, (N_CHIPS,)), self.zero, peers="sibling")
        self._groups[key] = dict(grads=grads, to_sibling=started)
        self.zero = started[3]

    def grads_at_sibling(self, key, after):
        group = self._groups[key]
        grads = group["grads"]
        group["from_sibling"] = _exchange_wait(f"{key}_grads_to_sibling_wait", _plan_other_half_to_sibling, grads,
                                               group["to_sibling"], after)
        group["partials"] = _chip_sums_bf16(f"{key}_chip_sums", grads, group["from_sibling"], self.place)
        group["to_chips"] = _exchange_start(f"{key}_grads_to_chips_start", _plan_to_other_chips, 3 * len(grads),
                                            group["partials"], _like(group["partials"], (3,)), self.zero)
        self.zero = group["to_chips"][3]

    def final_sum_operands(self, key, after):
        group = self._groups[key]
        from_chips = _exchange_wait(f"{key}_grads_to_chips_wait", _plan_to_other_chips, group["partials"],
                                    group["to_chips"], after)
        return group["grads"], group["from_sibling"], from_chips, self.place

    def grads_summed(self, key, after):
        return _final_sums(f"{key}_final_sums", *self.final_sum_operands(key, after))

    def send_sums(self, key, sums):
        self._groups[key + "_sums"] = _exchange_start(f"{key}_sums_to_sibling_start", _plan_own_half_to_sibling,
                                                      len(sums), [], None, self.zero, lands=list(sums),
                                                      peers="sibling")
        self.zero = self._groups[key + "_sums"][3]

    def whole_sums(self, key, after):
        full = _exchange_wait(f"{key}_sums_to_sibling_wait", _plan_own_half_to_sibling, [], self._groups[key + "_sums"], after)
        return [t.reshape(2 * t.shape[1], t.shape[2]) for t in full]

    def send_small(self, block):
        self._small = block
        self._small_started = _exchange_start("small_grads_start", _plan_to_all, 7, [block],
                                              [jax.ShapeDtypeStruct((8,) + block.shape, block.dtype)], self.zero)
        self.zero = self._small_started[3]

    def small_blocks(self, after):
        x, y, c = _place()
        (landed,) = _exchange_wait("small_grads_wait", _plan_to_all, [self._small], self._small_started, after)
        return lax.dynamic_update_index_in_dim(landed, self._small, 4 * x + 2 * y + c, 0)


def _rope_tables(positions):
    half = HEAD // 2
    inv_freq = jnp.float32(ROPE_THETA) ** (-(jnp.arange(half, dtype=F32) * 2.0 / HEAD))
    ang = positions.astype(F32)[:, None] * inv_freq
    cos, sin = jnp.cos(ang), jnp.sin(ang)
    return jnp.tile(cos, (1, 4)), jnp.tile(jnp.concatenate([-sin, sin], axis=1), (1, 2))


def _local_step(x, mem, positions, target, gains, ex):
    g_pre_mix, g_mem, g_a, g_c, g_x, g_post_mix, g_pre_mlp, g_post_mlp = gains
    tm = ROW_TILE
    cos, sin = _rope_tables(positions)
    h = _pre_norm(x, g_pre_mix, ex.zero, tm)
    w_in = ex.w_in([h, cos, sin])

    q, k, v, bcu, qx = _in_proj_fwd(h, w_in, cos, sin, ex.zero, tm)
    ya, lse = _attn_fwd(q, k, v)
    (w_kv, w_out), conv_w = ex.rest_weights(lse)
    w_kv, w_out = (w.reshape(N_CHIPS * w.shape[1], w.shape[2]) for w in (w_kv, w_out))
    memn, mkv = _memkv_fwd(mem, g_mem, w_kv, ex.zero)
    yx, ycat, y2, x1 = _mix_fwd(ya, bcu, qx, mkv, conv_w, g_a, g_c, g_x, w_out, g_post_mix, x, tm)
    w_up, w_down = ex.up_down(x1)
    w_down = w_down.reshape(N_CHIPS * w_down.shape[1], w_down.shape[2])
    h2, f, du, df2, dx1, dg_pre_mlp, dg_post_mlp, loss = _mlp_fwd_bwd(x1, target, g_pre_mlp, g_post_mlp, w_up, w_down,
                                                                      MLP_ROW_TILE)
    gw_down = _weight_grad("grad_w_down", f, df2, True, ex.zero)
    gw_up = _weight_grad("grad_w_up", h2, du, False, ex.zero)
    ex.send_grads("early", [gw_up, gw_down])

    gw_out, dya, delta, tail, dmkv, g_conv, dg_post_mix, dg_a, dg_c, dg_x = _mixer_bwd(
        dx1, y2, ycat, ya, yx, bcu, qx, mkv, conv_w, g_a, g_c, g_x, w_out, g_post_mix, ex.zero, tm)
    ex.grads_at_sibling("early", dya)
    gw_kv, dg_mem = _memkv_bwd(mem, g_mem, w_kv, dmkv)
    ex.send_grads("mid", [gw_out, gw_kv])
    dqkv = _attn_bwd(q, k, v, dya, lse, delta, ex.zero)
    ex.grads_at_sibling("mid", dqkv[0])
    grad_x, gw_in, dg_pre_mix, early_sums = _in_proj_bwd(dqkv, tail, cos, sin, w_in, x, h, g_pre_mix, dx1, ex.zero, tm,
                                                         ex.final_sum_operands("early", dqkv[0]))
    gain_grads = [dg_pre_mix, dg_mem, dg_a, dg_c, dg_x, dg_post_mix, dg_pre_mlp, dg_post_mlp]
    ex.send_small(_pack_small(gain_grads, g_conv, loss))
    ex.send_grads("late", [gw_in])
    return grad_x, early_sums


def _pack_small(gains, conv, scalar):
    n, k = len(gains), conv.shape[0]

    def body(*refs):
        out_ref = refs[-1]
        out_ref[...] = jnp.zeros_like(out_ref)
        for i, g_ref in enumerate(refs[:n]):
            out_ref[i:i + 1, 0:g_ref.shape[1]] = g_ref[...]
        out_ref[n:n + k, 0:conv.shape[1]] = refs[n][...]
        out_ref[n + k:n + k + 1, 0:1] = refs[n + 1][...]

    return pl.pallas_call(body, name="pack_small", out_shape=jax.ShapeDtypeStruct((SMALL_ROWS, D_MODEL), F32))(
        *gains, conv, scalar)


def kernel(x, mem, positions, g_pre_mix, g_mem, w_in, w_mem_kv, conv_w, g_attn_out, g_conv_out, g_xattn_out, w_out, g_post_mix, g_pre_mlp, w_up, w_down, g_post_mlp, loss_target, m_g_pre_mix, m_g_mem, m_w_in, m_w_mem_kv, m_conv_w, m_g_attn_out, m_g_conv_out, m_g_xattn_out, m_w_out, m_g_post_mix, m_g_pre_mlp, m_w_up, m_w_down, m_g_post_mlp, v_g_pre_mix, v_g_mem, v_w_in, v_w_mem_kv, v_conv_w, v_g_attn_out, v_g_conv_out, v_g_xattn_out, v_w_out, v_g_post_mix, v_g_pre_mlp, v_w_up, v_w_down, v_g_post_mlp):
    chip = 2 * lax.axis_index("x") + lax.axis_index("y")
    gains = [g_pre_mix, g_mem, g_attn_out, g_conv_out, g_xattn_out, g_post_mix, g_pre_mlp, g_post_mlp]
    gains_m = [m_g_pre_mix, m_g_mem, m_g_attn_out, m_g_conv_out, m_g_xattn_out, m_g_post_mix, m_g_pre_mlp, m_g_post_mlp]
    gains_v = [v_g_pre_mix, v_g_mem, v_g_attn_out, v_g_conv_out, v_g_xattn_out, v_g_post_mix, v_g_pre_mlp, v_g_post_mlp]
    mats =[w_in[0], w_mem_kv[0], w_out[0], w_up[0], w_down[0]]
    mats_m = [m_w_in[0], m_w_mem_kv[0], m_w_out[0], m_w_up[0], m_w_down[0]]
    mats_v = [v_w_in[0], v_w_mem_kv[0], v_w_out[0], v_w_up[0], v_w_down[0]]

    ex = _StepExchanges(mats, conv_w[0])
    grad_x, early_sums = _local_step(x[0], mem[0], positions[0], loss_target[0], gains, ex)

    ex.send_sums("four", [*early_sums, *ex.grads_summed("mid", ex.zero)])
    ex.grads_at_sibling("late", ex.zero)
    up_sum, down_sum, out_sum, kv_sum = ex.whole_sums("four", ex.zero)
    params = lambda a, g: (mats[a], g, mats_m[a], mats_v[a])
    new_up, new_down = _adamw("adamw_up_down", [params(3, up_sum), params(4, down_sum)], ex.zero)
    new_out, new_kv = _adamw("adamw_out_kv", [params(2, out_sum), params(1, kv_sum)], ex.zero)

    small, total = _small_update(ex.small_blocks(new_kv[1]), chip.reshape(1).astype(jnp.int32), gains, gains_m,
                                 gains_v, conv_w[0], m_conv_w[0], v_conv_w[0])

    ex.send_sums("last", ex.grads_summed("late", small[0][1]))
    (in_sum,) = ex.whole_sums("last", ex.zero)
    (new_in,) = _adamw("adamw_in", [params(0, in_sum)], in_sum)
    mat_new = [new_in, new_kv, new_out, new_up, new_down]

    order = ["g_pre_mix", "g_mem", "w_in", "w_mem_kv", "conv_w", "g_attn_out", "g_conv_out", "g_xattn_out", "w_out",
             "g_post_mix", "g_pre_mlp", "w_up", "w_down", "g_post_mlp"]
    gain_names = ["g_pre_mix", "g_mem", "g_attn_out", "g_conv_out", "g_xattn_out", "g_post_mix", "g_pre_mlp", "g_post_mlp"]
    mat_names = ["w_in", "w_mem_kv", "w_out", "w_up", "w_down"]

    def leaf(kind, name):
        if name in gain_names:
            return small[gain_names.index(name)][kind]
        if name == "conv_w":
            return jnp.swapaxes(small[len(gain_names)][kind], 0, 1)
        return mat_new[mat_names.index(name)][kind][None]

    return (total[0, 0], grad_x[None], *[leaf(kind, name) for kind in range(4) for name in order])
```

```python
import jax
import jax.numpy as jnp
from jax import lax
from jax.experimental import pallas as pl
from jax.experimental.pallas import tpu as pltpu

F32, BF16 = jnp.float32, jnp.bfloat16

D_MODEL = 1024
ATTN_W = 512
CONV_W = 256
XATTN_W = 256
PROJ_W = 3 * ATTN_W + 3 * CONV_W + XATTN_W
D_FF = 4096
HEAD = 64
N_BACK = 128
DILATIONS = (1, 4, 16)
PATTERN_ORDER = DILATIONS[::-1]
ROPE_THETA = 10000.0
EPS = 1e-6
NEG_INF = -1e30
SCALE = HEAD ** -0.5
N_CHIPS = 4
SHARD_IN = PROJ_W // N_CHIPS
SHARD_FF = D_FF // N_CHIPS

ADAM_LR, ADAM_B1, ADAM_B2, ADAM_EPS, ADAM_WD, ADAM_STEP = 0.001, 0.9, 0.999, 1e-08, 0.01, 10

VMEM_LIMIT_V7X = 56 * 1024 * 1024
ROW_TILE = 512
MLP_ROW_TILE = 256
ADAMW_ROW_TILE = 256
SMALL_ROWS = 16

NT = (((1,), (1,)), ((), ()))
TN = (((0,), (0,)), ((), ()))
MESH = pl.DeviceIdType.MESH


def _params(*sem):
    return pltpu.CompilerParams(dimension_semantics=sem, vmem_limit_bytes=VMEM_LIMIT_V7X)


def _resident(shape):
    return pl.BlockSpec(shape, lambda *_: (0,) * len(shape), pipeline_mode=pl.Buffered(1))


def _rows(tm, width):
    return pl.BlockSpec((tm, width), lambda i, *_: (i, 0))


def _rms_hat(x):
    r = lax.rsqrt(jnp.mean(x * x, axis=-1, keepdims=True) + EPS)
    return x * r, r


def _rms_bwd(xhat, r, g, dy):
    gdy = dy * g
    return r * (gdy - xhat * jnp.mean(xhat * gdy, axis=-1, keepdims=True))


def _rope128(t, cos, sin_signed, inverse):
    lane = lax.broadcasted_iota(jnp.int32, t.shape, 1)
    first_half = (lane % HEAD) < (HEAD // 2)
    rot = jnp.where(first_half, pltpu.roll(t, 128 - HEAD // 2, 1), pltpu.roll(t, HEAD // 2, 1))
    return t * cos - rot * sin_signed if inverse else t * cos + rot * sin_signed


def _pre_norm(x, g, after, tm):
    S = x.shape[0]

    def body(x_ref, g_ref, after_ref, h_ref):
        h_ref[...] = (_rms_hat(x_ref[...])[0] * g_ref[...]).astype(BF16)

    return pl.pallas_call(
        body, name="pre_norm", grid=(S // tm,),
        in_specs=[_rows(tm, D_MODEL), _resident((1, D_MODEL)), pl.BlockSpec(memory_space=pl.ANY)],
        out_specs=_rows(tm, D_MODEL), out_shape=jax.ShapeDtypeStruct((S, D_MODEL), BF16),
        compiler_params=_params("parallel"),
    )(x, g, after)


def _side_by_side(w_hbm, w_full, sems):
    width = w_hbm.shape[2]

    @pl.when(pl.program_id(0) == 0)
    def _():
        copies = [pltpu.make_async_copy(w_hbm.at[j], w_full.at[:, pl.ds(width * j, width)], sems.at[j])
                  for j in range(N_CHIPS)]
        for cp in copies:
            cp.start()
        for cp in copies:
            cp.wait()


def _in_proj_fwd(h, w_in, cos, sin, after, tm):
    S = h.shape[0]

    def body(h_ref, w_hbm, cos_ref, sin_ref, after_ref, q_ref, k_ref, v_ref, bcu_ref, qx_ref, proj, w_full, sems):
        _side_by_side(w_hbm, w_full, sems)
        proj[...] = jnp.dot(h_ref[...], w_full[...], preferred_element_type=F32)
        c, s = cos_ref[...], sin_ref[...]
        for j in range(ATTN_W // 128):
            lo = 128 * j
            q_ref[:, lo:lo + 128] = _rope128(proj[:, lo:lo + 128], c, s, False) * SCALE
            k_ref[:, lo:lo + 128] = _rope128(proj[:, ATTN_W + lo:ATTN_W + lo + 128], c, s, False)
        v_ref[...] = proj[:, 2 * ATTN_W:3 * ATTN_W]
        bcu_ref[...] = proj[:, 3 * ATTN_W:3 * ATTN_W + 3 * CONV_W]
        qx_ref[...] = proj[:, 3 * ATTN_W + 3 * CONV_W:PROJ_W].astype(BF16)

    return pl.pallas_call(
        body, name="in_proj_fwd", grid=(S // tm,),
        in_specs=[_rows(tm, D_MODEL), pl.BlockSpec(memory_space=pl.ANY), _rows(tm, 128), _rows(tm, 128),
                  pl.BlockSpec(memory_space=pl.ANY)],
        out_specs=[_rows(tm, ATTN_W), _rows(tm, ATTN_W), _rows(tm, ATTN_W), _rows(tm, 3 * CONV_W), _rows(tm, XATTN_W)],
        out_shape=[jax.ShapeDtypeStruct((S, ATTN_W), F32), jax.ShapeDtypeStruct((S, ATTN_W), F32),
                   jax.ShapeDtypeStruct((S, ATTN_W), F32), jax.ShapeDtypeStruct((S, 3 * CONV_W), F32),
                   jax.ShapeDtypeStruct((S, XATTN_W), BF16)],
        scratch_shapes=[pltpu.VMEM((tm, PROJ_W), F32), pltpu.VMEM((D_MODEL, PROJ_W), BF16),
                        pltpu.SemaphoreType.DMA((N_CHIPS,))],
        compiler_params=_params("arbitrary"),
    )(h, w_in, cos, sin, after)


def _memkv_fwd(mem, g_mem, w_kv, after):
    n_mem = mem.shape[0]

    def body(mem_ref, g_ref, w_ref, after_ref, mn_ref, kv_ref):
        mhat, _ = _rms_hat(mem_ref[...])
        mn = (mhat * g_ref[...]).astype(BF16)
        mn_ref[...] = mn
        kv_ref[...] = jnp.dot(mn, w_ref[...], preferred_element_type=F32).astype(BF16)

    vmem = pl.BlockSpec(memory_space=pltpu.VMEM)
    return pl.pallas_call(
        body, name="memkv_fwd", in_specs=[vmem, vmem, vmem, pl.BlockSpec(memory_space=pl.ANY)], out_specs=[vmem, vmem],
        out_shape=[jax.ShapeDtypeStruct((n_mem, D_MODEL), BF16), jax.ShapeDtypeStruct((n_mem, 2 * XATTN_W), BF16)],
        compiler_params=pltpu.CompilerParams(vmem_limit_bytes=VMEM_LIMIT_V7X),
    )(mem, g_mem, w_kv, after)


def _fill_band_bias(bias):
    row = lax.broadcasted_iota(jnp.int32, (N_BACK, 2 * N_BACK), 0)
    col = lax.broadcasted_iota(jnp.int32, (N_BACK, 2 * N_BACK), 1)
    band = (col >= row) & (col <= row + N_BACK)
    bias[1] = jnp.where(band, 0.0, NEG_INF)
    bias[0] = jnp.where(band & (col >= N_BACK), 0.0, NEG_INF)


def _strided(start, size, d):
    return pl.ds(start, size) if d == 1 else pl.ds(start, size, stride=d)


def _group_starts(g, G, nb, d):
    t0 = g * G
    r, n0 = lax.shift_right_logical(t0, nb.bit_length() - 1), lax.bitwise_and(t0, nb - 1)
    first = r + n0 * (N_BACK * d)
    before = r + jnp.maximum(n0 - 1, 0) * (N_BACK * d)
    starts = [before] + [first + u * (N_BACK * d) for u in range(G)]
    if d == 1:
        starts = [pl.multiple_of(st, N_BACK) for st in starts]
    return starts, n0


def _step_blocks(i, U, nb, d):
    G = min(U, nb)
    whole = G == nb
    row_blocks, blocks = [], []
    for grp in range(U // G):
        starts, n0 = _group_starts(i * (U // G) + grp, G, nb, d)
        base = len(row_blocks)
        if whole:
            row_blocks += [_strided(st, N_BACK, d) for st in starts[1:]]
            blocks += [(base + max(u - 1, 0), base + u, min(u, 1)) for u in range(G)]
        else:
            row_blocks += [_strided(st, N_BACK, d) for st in starts]
            blocks += [(base + u, base + u + 1, jnp.minimum(n0, 1) if u == 0 else 1) for u in range(G)]
    return row_blocks, blocks


def _by_head(a, b):
    lane = lax.broadcasted_iota(jnp.int32, (a.shape[0], 2 * HEAD), 1)
    return jnp.where(lane < HEAD, a, b)


def _head_only(t, hh):
    lane = lax.broadcasted_iota(jnp.int32, t.shape, 1)
    return jnp.where((lane < HEAD) == (hh == 0), t, jnp.zeros_like(t))


def _stack_heads(t):
    return jnp.concatenate([_head_only(t, 0), _head_only(t, 1)], axis=0)


def _head_columns(t):
    return jnp.concatenate([t[:, 0:1], t[:, HEAD:HEAD + 1]], axis=0)


def _unstack(t):
    return _by_head(t[:N_BACK], t[N_BACK:])


def _unstack_columns(t):
    return _by_head(jnp.broadcast_to(t[:N_BACK], (N_BACK, 2 * HEAD)), jnp.broadcast_to(t[N_BACK:], (N_BACK, 2 * HEAD)))


FWD_BLOCKS_PER_STEP = 4
BWD_BLOCKS_PER_STEP = 4
BWD_CHUNK = 64


def _attn_fwd(q, k, v):
    S = q.shape[0]
    U = FWD_BLOCKS_PER_STEP

    def body(q_ref, k_ref, v_ref, y_ref, m_ref, l_scr, bias):
        _fill_band_bias(bias)
        for g, d in enumerate(PATTERN_ORDER):
            nb = S // d // N_BACK
            first_pattern, last_pattern = g == 0, g == len(PATTERN_ORDER) - 1

            def step(i, carry, d=d, nb=nb, first_pattern=first_pattern, last_pattern=last_pattern):
                row_blocks, blocks = _step_blocks(i, U, nb, d)
                kb = [k_ref[r, :].astype(BF16) for r in row_blocks]
                ss = []
                for before, own, which in blocks:
                    kw = jnp.concatenate([kb[before], kb[own]], 0)
                    qs = _stack_heads(q_ref[row_blocks[own], :].astype(BF16))
                    b = bias[which]
                    ss.append(lax.dot_general(qs, kw, NT, preferred_element_type=F32) + jnp.concatenate([b, b], axis=0))
                ms = [jnp.max(s, axis=1, keepdims=True) for s in ss]
                ps = [jnp.exp(s - m) for s, m in zip(ss, ms)]
                ls = [jnp.sum(p, axis=1, keepdims=True) for p in ps]
                vb = [v_ref[r, :].astype(BF16) for r in row_blocks]
                os_ = [jnp.dot(ps[u].astype(BF16), jnp.concatenate([vb[before], vb[own]], 0), preferred_element_type=F32)
                       for u, (before, own, _) in enumerate(blocks)]
                for u, (_, own, _) in enumerate(blocks):
                    o_g, m_g, l_g = _unstack(os_[u]), _unstack_columns(ms[u]), _unstack_columns(ls[u])
                    r = row_blocks[own]
                    if first_pattern:
                        m_new, l_new, acc = m_g, l_g, o_g
                    else:
                        m_old = m_ref[r, :]
                        m_new = jnp.maximum(m_old, m_g)
                        alpha, beta = jnp.exp(m_old - m_new), jnp.exp(m_g - m_new)
                        l_new = l_scr[r, :] * alpha + l_g * beta
                        acc = y_ref[r, :] * alpha + o_g * beta
                    if last_pattern:
                        y_ref[r, :] = acc / l_new
                        m_ref[r, :] = m_new + jnp.log(l_new)
                    else:
                        y_ref[r, :] = acc
                        m_ref[r, :] = m_new
                        l_scr[r, :] = l_new
                return carry

            lax.fori_loop(0, d * nb // U, step, 0)

    col = pl.BlockSpec((S, 2 * HEAD), lambda j: (0, j))
    return pl.pallas_call(
        body, name="attn_fwd", grid=(q.shape[1] // (2 * HEAD),),
        in_specs=[col, col, col], out_specs=[col, col],
        out_shape=[jax.ShapeDtypeStruct(q.shape, F32)] * 2,
        scratch_shapes=[pltpu.VMEM((S, 2 * HEAD), F32), pltpu.VMEM((2, N_BACK, 2 * N_BACK), F32)],
        compiler_params=_params("parallel"),
    )(q, k, v)


def _attn_bwd(q, k, v, dy, lse, delta, after):
    S = q.shape[0]
    U = BWD_BLOCKS_PER_STEP

    def body(q_ref, k_ref, v_ref, dy_ref, lse_ref, delta_ref, after_ref, dq_ref, dk_ref, dv_ref, bias):
        _fill_band_bias(bias)
        nb_first = S // PATTERN_ORDER[0] // N_BACK
        first_writes_all = min(U, nb_first) == nb_first
        if not first_writes_all:
            dk_ref[...] = jnp.zeros_like(dk_ref)
            dv_ref[...] = jnp.zeros_like(dv_ref)
        for g, d in enumerate(PATTERN_ORDER):
            nb = S // d // N_BACK

            def step(i, carry, d=d, nb=nb, g=g):
                row_blocks, blocks = _step_blocks(i, U, nb, d)
                kb = [k_ref[r, :].astype(BF16) for r in row_blocks]
                vb = [v_ref[r, :].astype(BF16) for r in row_blocks]
                kws = [jnp.concatenate([kb[before], kb[own]], 0) for before, own, _ in blocks]
                vws = [jnp.concatenate([vb[before], vb[own]], 0) for before, own, _ in blocks]
                qss = [_stack_heads(q_ref[row_blocks[own], :].astype(BF16)) for _, own, _ in blocks]
                doss = [_stack_heads(dy_ref[row_blocks[own], :].astype(BF16)) for _, own, _ in blocks]
                ss = [lax.dot_general(qss[u], kws[u], NT, preferred_element_type=F32) for u in range(U)]
                dps = [lax.dot_general(doss[u], vws[u], NT, preferred_element_type=F32) for u in range(U)]
                pbs, dss = [], []
                for u, (_, own, which) in enumerate(blocks):
                    lse_c = _head_columns(lse_ref[row_blocks[own], :])
                    delta_c = _head_columns(delta_ref[row_blocks[own], :])
                    p_parts, ds_parts = [], []
                    for r0 in range(0, 2 * N_BACK, BWD_CHUNK):
                        r = slice(r0, r0 + BWD_CHUNK)
                        mask = bias[which, r0 % N_BACK:r0 % N_BACK + BWD_CHUNK, :]
                        p_r = jnp.exp(ss[u][r] + mask - lse_c[r])
                        p_parts.append(p_r.astype(BF16))
                        ds_parts.append((p_r * (dps[u][r] - delta_c[r])).astype(BF16))
                    pbs.append(jnp.concatenate(p_parts, axis=0))
                    dss.append(jnp.concatenate(ds_parts, axis=0))
                dqs = [jnp.dot(dss[u], kws[u], preferred_element_type=F32) for u in range(U)]
                dkws = [lax.dot_general(dss[u], qss[u], TN, preferred_element_type=F32) for u in range(U)]
                dvws = [lax.dot_general(pbs[u], doss[u], TN, preferred_element_type=F32) for u in range(U)]
                dk_parts, dv_parts = [None] * len(row_blocks), [None] * len(row_blocks)
                for u, (before, own, _) in enumerate(blocks):
                    dq = _unstack(dqs[u])
                    if g == 0:
                        dq_ref[row_blocks[own], :] = dq
                    else:
                        dq_ref[row_blocks[own], :] += dq
                    for idx, dkp, dvp in ((before, dkws[u][:N_BACK], dvws[u][:N_BACK]),
                                          (own, dkws[u][N_BACK:], dvws[u][N_BACK:])):
                        dk_parts[idx] = dkp if dk_parts[idx] is None else dk_parts[idx] + dkp
                        dv_parts[idx] = dvp if dv_parts[idx] is None else dv_parts[idx] + dvp
                for idx, r in enumerate(row_blocks):
                    if g == 0 and first_writes_all:
                        dk_ref[r, :] = dk_parts[idx]
                        dv_ref[r, :] = dv_parts[idx]
                    else:
                        dk_ref[r, :] += dk_parts[idx]
                        dv_ref[r, :] += dv_parts[idx]
                return carry

            lax.fori_loop(0, d * nb // U, step, 0)

    col = pl.BlockSpec((S, 2 * HEAD), lambda j: (0, j))
    return pl.pallas_call(
        body, name="attn_bwd", grid=(q.shape[1] // (2 * HEAD),),
        in_specs=[col] * 6 + [pl.BlockSpec(memory_space=pl.ANY)], out_specs=[col] * 3,
        out_shape=[jax.ShapeDtypeStruct(q.shape, F32)] * 3,
        scratch_shapes=[pltpu.VMEM((2, N_BACK, 2 * N_BACK), F32)],
        compiler_params=_params("parallel"),
    )(q, k, v, dy, lse, delta, after)


def _shift_down(z, before, k):
    row = lax.broadcasted_iota(jnp.int32, z.shape, 0)
    out = pltpu.roll(z, k, 0)
    for i in range(k):
        out = jnp.where(row == i, before[8 - k + i:8 - k + i + 1, :], out)
    return out


def _shift_up(z, after, k):
    rows = z.shape[0]
    row = lax.broadcasted_iota(jnp.int32, z.shape, 0)
    out = pltpu.roll(z, rows - k, 0)
    for i in range(k):
        out = jnp.where(row == rows - k + i, after[i:i + 1, :], out)
    return out


def _conv_fwd(bcu, before, is_first, w):
    b, c, u = bcu[:, 0:CONV_W], bcu[:, CONV_W:2 * CONV_W], bcu[:, 2 * CONV_W:3 * CONV_W]
    z = c * u
    zb = jnp.where(is_first, 0.0, before[:, CONV_W:2 * CONV_W] * before[:, 2 * CONV_W:3 * CONV_W])
    z1, z2 = _shift_down(z, zb, 1), _shift_down(z, zb, 2)
    cv = w[0:1, :] * z2 + w[1:2, :] * z1 + w[2:3, :] * z
    return b, c, u, z, z1, z2, cv


def _halo_before(tm, width):
    return pl.BlockSpec((8, width), lambda i: (jnp.maximum(i * (tm // 8) - 1, 0), 0))


def _mix_fwd(ya, bcu, qx, mkv, conv_w, g_a, g_c, g_x, w_out, g_post, x, tm):
    S = x.shape[0]

    def body(ya_ref, bcu_ref, before_ref, qx_ref, mkv_ref, cw_ref, ga_ref, gc_ref, gx_ref,
             wo_ref, gp_ref, x_ref, yx_ref, ycat_ref, y2_ref, x1_ref):
        ya = ya_ref[...]
        b, _, _, _, _, _, cv = _conv_fwd(bcu_ref[...], before_ref[...], pl.program_id(0) == 0, cw_ref[...])
        yc = b * cv

        qxb, mkvb = qx_ref[...], mkv_ref[...]
        heads = [slice(HEAD * hd, HEAD * (hd + 1)) for hd in range(XATTN_W // HEAD)]
        ss = [lax.dot_general(qxb[:, sl], mkvb[:, sl], NT, preferred_element_type=F32) * SCALE for sl in heads]
        ms = [jnp.max(s, axis=1, keepdims=True) for s in ss]
        ps = [jnp.exp(s - m) for s, m in zip(ss, ms)]
        ls = [jnp.sum(p, axis=1, keepdims=True) for p in ps]
        os_ = [jnp.dot(p.astype(BF16), mkvb[:, XATTN_W + sl.start:XATTN_W + sl.stop], preferred_element_type=F32)
               for p, sl in zip(ps, heads)]
        for sl, o, l in zip(heads, os_, ls):
            yx_ref[:, sl] = o / l
        yx = yx_ref[...]

        ycat_ref[:, 0:ATTN_W] = (_rms_hat(ya)[0] * ga_ref[...]).astype(BF16)
        ycat_ref[:, ATTN_W:ATTN_W + CONV_W] = (_rms_hat(yc)[0] * gc_ref[...]).astype(BF16)
        ycat_ref[:, ATTN_W + CONV_W:D_MODEL] = (_rms_hat(yx)[0] * gx_ref[...]).astype(BF16)
        y2 = jnp.dot(ycat_ref[...], wo_ref[...], preferred_element_type=F32)
        y2_ref[...] = y2
        x1_ref[...] = x_ref[...] + _rms_hat(y2)[0] * gp_ref[...]

    n_mem = mkv.shape[0]
    return pl.pallas_call(
        body, name="mix_fwd", grid=(S // tm,),
        in_specs=[_rows(tm, ATTN_W), _rows(tm, 3 * CONV_W), _halo_before(tm, 3 * CONV_W), _rows(tm, XATTN_W),
                  _resident((n_mem, 2 * XATTN_W)), _resident((3, CONV_W)), _resident((1, ATTN_W)),
                  _resident((1, CONV_W)), _resident((1, XATTN_W)), _resident((D_MODEL, D_MODEL)),
                  _resident((1, D_MODEL)), _rows(tm, D_MODEL)],
        out_specs=[_rows(tm, XATTN_W), _rows(tm, D_MODEL), _rows(tm, D_MODEL), _rows(tm, D_MODEL)],
        out_shape=[jax.ShapeDtypeStruct((S, XATTN_W), F32), jax.ShapeDtypeStruct((S, D_MODEL), BF16),
                   jax.ShapeDtypeStruct((S, D_MODEL), F32), jax.ShapeDtypeStruct((S, D_MODEL), F32)],
        compiler_params=_params("parallel"),
    )(ya, bcu, bcu, qx, mkv, conv_w, g_a, g_c, g_x, w_out, g_post, x)


def _mlp_fwd_bwd(x1, target, g_pre, g_post, w_up, w_down, tm):
    S = x1.shape[0]
    n_ff = D_FF // SHARD_FF

    def body(x1_ref, t_ref, gpre_ref, gpost_ref, wup_ref, wdn_ref,
             h2_ref, f_ref, du_ref, df2_ref, dx1_ref, dgpre_ref, dgpost_ref, loss_ref, u_scr):
        @pl.when(pl.program_id(0) == 0)
        def _():
            dgpre_ref[...] = jnp.zeros_like(dgpre_ref)
            dgpost_ref[...] = jnp.zeros_like(dgpost_ref)
            loss_ref[...] = jnp.zeros_like(loss_ref)

        x1 = x1_ref[...]
        x1hat, r1 = _rms_hat(x1)
        h2 = (x1hat * gpre_ref[...]).astype(BF16)
        h2_ref[...] = h2
        f2 = jnp.zeros((tm, D_MODEL), F32)
        for j in range(n_ff):
            cols = slice(SHARD_FF * j, SHARD_FF * (j + 1))
            u = jnp.maximum(jnp.dot(h2, wup_ref[j], preferred_element_type=F32), 0.0)
            u_scr[:, cols] = u
            f = (u * u).astype(BF16)
            f_ref[:, cols] = f
            f2 = f2 + jnp.dot(f, wdn_ref[cols, :], preferred_element_type=F32)
        f2hat, r2 = _rms_hat(f2)
        err = x1 + f2hat * gpost_ref[...] - t_ref[...]
        loss_ref[...] += 0.5 * jnp.sum(jnp.mean(err * err, axis=-1, keepdims=True), axis=0, keepdims=True)
        dx2 = err * (1.0 / D_MODEL)
        dgpost_ref[...] += jnp.sum(dx2 * f2hat, axis=0, keepdims=True)
        df2 = _rms_bwd(f2hat, r2, gpost_ref[...], dx2).astype(BF16)
        df2_ref[...] = df2
        dh2 = jnp.zeros((tm, D_MODEL), F32)
        for j in range(n_ff):
            cols = slice(SHARD_FF * j, SHARD_FF * (j + 1))
            df = lax.dot_general(df2, wdn_ref[cols, :], NT, preferred_element_type=F32)
            du = (2.0 * u_scr[:, cols] * df).astype(BF16)
            du_ref[:, cols] = du
            dh2 = dh2 + lax.dot_general(du, wup_ref[j], NT, preferred_element_type=F32)
        dgpre_ref[...] += jnp.sum(dh2 * x1hat, axis=0, keepdims=True)
        dx1_ref[...] = dx2 + _rms_bwd(x1hat, r1, gpre_ref[...], dh2)

    acc = pl.BlockSpec((1, D_MODEL), lambda i: (0, 0))
    return pl.pallas_call(
        body, name="mlp_fwd_bwd", grid=(S // tm,),
        in_specs=[_rows(tm, D_MODEL), _rows(tm, D_MODEL), _resident((1, D_MODEL)), _resident((1, D_MODEL)),
                  _resident((n_ff, D_MODEL, SHARD_FF)), _resident((D_FF, D_MODEL))],
        out_specs=[_rows(tm, D_MODEL), _rows(tm, D_FF), _rows(tm, D_FF), _rows(tm, D_MODEL), _rows(tm, D_MODEL),
                   acc, acc, pl.BlockSpec((1, 1), lambda i: (0, 0))],
        out_shape=[jax.ShapeDtypeStruct((S, D_MODEL), BF16), jax.ShapeDtypeStruct((S, D_FF), BF16),
                   jax.ShapeDtypeStruct((S, D_FF), BF16), jax.ShapeDtypeStruct((S, D_MODEL), BF16),
                   jax.ShapeDtypeStruct((S, D_MODEL), F32), jax.ShapeDtypeStruct((1, D_MODEL), F32),
                   jax.ShapeDtypeStruct((1, D_MODEL), F32), jax.ShapeDtypeStruct((1, 1), F32)],
        scratch_shapes=[pltpu.VMEM((tm, D_FF), F32)],
        compiler_params=_params("arbitrary"),
    )(x1, target, g_pre, g_post, w_up, w_down)


def _weight_grad(name, a, b, rows_sharded, after):
    S, K = a.shape
    N = b.shape[1]
    if rows_sharded:
        tk, tn = K // N_CHIPS, N
        a_spec = pl.BlockSpec((S, tk), lambda j: (0, j))
        b_spec = pl.BlockSpec((S, tn), lambda j: (0, 0), pipeline_mode=pl.Buffered(1))
    else:
        tk, tn = K, N // N_CHIPS
        a_spec = pl.BlockSpec((S, tk), lambda j: (0, 0), pipeline_mode=pl.Buffered(1))
        b_spec = pl.BlockSpec((S, tn), lambda j: (0, j))
    half = tk // 2

    def body(a_ref, b_ref, after_ref, o_ref):
        res = lax.dot_general(a_ref[...], b_ref[...], TN, preferred_element_type=F32)
        o_ref[0, 0] = res[:half]
        o_ref[1, 0] = res[half:]

    return pl.pallas_call(
        body, name=name, grid=(N_CHIPS,), in_specs=[a_spec, b_spec, pl.BlockSpec(memory_space=pl.ANY)],
        out_specs=pl.BlockSpec((2, 1, half, tn), lambda j: (0, j, 0, 0)),
        out_shape=jax.ShapeDtypeStruct((2, N_CHIPS, half, tn), F32),
        compiler_params=_params("parallel"),
    )(a, b, after)


def _mixer_bwd(dx1, y2, ycat, ya, yx, bcu, qx, mkv, conv_w, g_a, g_c, g_x, w_out, g_post, after, tm):
    S = dx1.shape[0]
    n_mem = mkv.shape[0]
    n_tiles = S // tm
    half = D_MODEL // N_CHIPS // 2

    def body(dx1_ref, y2_ref, ycat_ref, ya_ref, yx_ref, bcu_ref, before_ref, qx_ref, mkv_ref, cw_ref, ga_ref, gc_ref,
             gx_ref, wo_ref, gp_ref, after_ref, gwo_ref, dya_ref, delta_ref, tail_ref, dmkv_ref, dcw_ref, dgp_ref,
             dga_ref, dgc_ref, dgx_ref, carry):
        step = pl.program_id(0)
        first_tile = step == n_tiles - 1

        @pl.when(step == 0)
        def _():
            for ref in (gwo_ref, dmkv_ref, dcw_ref, dgp_ref, dga_ref, dgc_ref, dgx_ref, carry):
                ref[...] = jnp.zeros_like(ref)

        dx1 = dx1_ref[...]
        y2hat, r2 = _rms_hat(y2_ref[...])
        dgp_ref[...] += jnp.sum(dx1 * y2hat, axis=0, keepdims=True)
        dy2 = _rms_bwd(y2hat, r2, gp_ref[...], dx1).astype(BF16)
        gwo = lax.dot_general(ycat_ref[...], dy2, TN, preferred_element_type=F32)
        for k in range(2 * N_CHIPS):
            gwo_ref[k % 2, k // 2] += gwo[half * k:half * (k + 1)]
        dycat = lax.dot_general(dy2, wo_ref[...], NT, preferred_element_type=F32)

        d_na = dycat[:, 0:ATTN_W]
        ya = ya_ref[...]
        yahat, ra = _rms_hat(ya)
        dga_ref[...] += jnp.sum(d_na * yahat, axis=0, keepdims=True)
        dya = _rms_bwd(yahat, ra, ga_ref[...], d_na)
        dya_ref[...] = dya
        prod = dya * ya
        hi = prod.astype(BF16)
        lo = (prod - hi.astype(F32)).astype(BF16)
        head_of = lambda axis: lax.shift_right_logical(lax.broadcasted_iota(jnp.int32, (ATTN_W, ATTN_W), axis),
                                                       HEAD.bit_length() - 1)
        ones = jnp.where(head_of(0) == head_of(1), 1.0, 0.0).astype(BF16)
        delta_ref[...] = jnp.dot(hi, ones, preferred_element_type=F32) + jnp.dot(lo, ones, preferred_element_type=F32)

        w = cw_ref[...]
        b, c, u, z, z1, z2, cv = _conv_fwd(bcu_ref[...], before_ref[...], first_tile, w)
        d_nc = dycat[:, ATTN_W:ATTN_W + CONV_W]
        ychat, rc = _rms_hat(b * cv)
        dgc_ref[...] += jnp.sum(d_nc * ychat, axis=0, keepdims=True)
        dyc = _rms_bwd(ychat, rc, gc_ref[...], d_nc)
        dcv = dyc * b
        behind = carry[...]
        dz = w[2:3, :] * dcv + w[1:2, :] * _shift_up(dcv, behind, 1) + w[0:1, :] * _shift_up(dcv, behind, 2)
        carry[...] = dcv[0:8, :]
        dcw_ref[0:1, :] += jnp.sum(dcv * z2, axis=0, keepdims=True)
        dcw_ref[1:2, :] += jnp.sum(dcv * z1, axis=0, keepdims=True)
        dcw_ref[2:3, :] += jnp.sum(dcv * z, axis=0, keepdims=True)
        tail_ref[:, 0:CONV_W] = (dyc * cv).astype(BF16)
        tail_ref[:, CONV_W:2 * CONV_W] = (dz * u).astype(BF16)
        tail_ref[:, 2 * CONV_W:3 * CONV_W] = (dz * c).astype(BF16)

        d_nx = dycat[:, ATTN_W + CONV_W:D_MODEL]
        yxhat, rx = _rms_hat(yx_ref[...])
        dgx_ref[...] += jnp.sum(d_nx * yxhat, axis=0, keepdims=True)
        dyx = _rms_bwd(yxhat, rx, gx_ref[...], d_nx)
        qxb, mkvb = qx_ref[...], mkv_ref[...]
        heads = [slice(HEAD * hd, HEAD * (hd + 1)) for hd in range(XATTN_W // HEAD)]
        values = [slice(XATTN_W + sl.start, XATTN_W + sl.stop) for sl in heads]
        ss = [lax.dot_general(qxb[:, sl], mkvb[:, sl], NT, preferred_element_type=F32) * SCALE for sl in heads]
        es = [jnp.exp(s - jnp.max(s, axis=1, keepdims=True)) for s in ss]
        ps = [e / jnp.sum(e, axis=1, keepdims=True) for e in es]
        dobs = [dyx[:, sl].astype(BF16) for sl in heads]
        dps = [lax.dot_general(dob, mkvb[:, vsl], NT, preferred_element_type=F32) for dob, vsl in zip(dobs, values)]
        dss = [(p * (dp - jnp.sum(p * dp, axis=1, keepdims=True)) * SCALE).astype(BF16) for p, dp in zip(ps, dps)]
        for sl, vsl, p, dob, ds in zip(heads, values, ps, dobs, dss):
            tail_ref[:, 3 * CONV_W + sl.start:3 * CONV_W + sl.stop] = jnp.dot(
                ds, mkvb[:, sl], preferred_element_type=F32).astype(BF16)
            dmkv_ref[:, sl] += lax.dot_general(ds, qxb[:, sl], TN, preferred_element_type=F32)
            dmkv_ref[:, vsl] += lax.dot_general(p.astype(BF16), dob, TN, preferred_element_type=F32)

    rows = lambda width: pl.BlockSpec((tm, width), lambda i: (n_tiles - 1 - i, 0))
    before = pl.BlockSpec((8, 3 * CONV_W), lambda i: (jnp.maximum((n_tiles - 1 - i) * (tm // 8) - 1, 0), 0))
    acc = lambda r, w: pl.BlockSpec((r, w), lambda i: (0, 0))
    return pl.pallas_call(
        body, name="mixer_bwd", grid=(n_tiles,),
        in_specs=[rows(D_MODEL), rows(D_MODEL), rows(D_MODEL), rows(ATTN_W), rows(XATTN_W), rows(3 * CONV_W), before,
                  rows(XATTN_W), _resident((n_mem, 2 * XATTN_W)), _resident((3, CONV_W)), _resident((1, ATTN_W)),
                  _resident((1, CONV_W)), _resident((1, XATTN_W)), _resident((D_MODEL, D_MODEL)),
                  _resident((1, D_MODEL)), pl.BlockSpec(memory_space=pl.ANY)],
        out_specs=[pl.BlockSpec((2, N_CHIPS, half, D_MODEL), lambda i: (0, 0, 0, 0)), rows(ATTN_W), rows(ATTN_W),
                   rows(3 * CONV_W + XATTN_W), acc(n_mem, 2 * XATTN_W),
                   acc(3, CONV_W), acc(1, D_MODEL), acc(1, ATTN_W), acc(1, CONV_W), acc(1, XATTN_W)],
        out_shape=[jax.ShapeDtypeStruct((2, N_CHIPS, half, D_MODEL), F32), jax.ShapeDtypeStruct((S, ATTN_W), F32),
                   jax.ShapeDtypeStruct((S, ATTN_W), F32), jax.ShapeDtypeStruct((S, 3 * CONV_W + XATTN_W), BF16),
                   jax.ShapeDtypeStruct((n_mem, 2 * XATTN_W), F32), jax.ShapeDtypeStruct((3, CONV_W), F32),
                   jax.ShapeDtypeStruct((1, D_MODEL), F32), jax.ShapeDtypeStruct((1, ATTN_W), F32),
                   jax.ShapeDtypeStruct((1, CONV_W), F32), jax.ShapeDtypeStruct((1, XATTN_W), F32)],
        scratch_shapes=[pltpu.VMEM((8, CONV_W), F32)],
        compiler_params=_params("arbitrary"),
    )(dx1, y2, ycat, ya, yx, bcu, bcu, qx, mkv, conv_w, g_a, g_c, g_x, w_out, g_post, after)


def _memkv_bwd(mem, g_mem, w_kv, dmkv):
    n_mem = mem.shape[0]
    half = D_MODEL // N_CHIPS // 2

    def body(mem_ref, g_ref, w_ref, d_ref, dw_ref, dg_ref):
        mhat, _ = _rms_hat(mem_ref[...])
        mn = (mhat * g_ref[...]).astype(BF16)
        d = d_ref[...].astype(BF16)
        for k in range(2 * N_CHIPS):
            dw_ref[k % 2, k // 2] = lax.dot_general(mn[:, half * k:half * (k + 1)], d, TN, preferred_element_type=F32)
        dmn = lax.dot_general(d, w_ref[...], NT, preferred_element_type=F32)
        dg_ref[...] = jnp.sum(dmn * mhat, axis=0, keepdims=True)

    return pl.pallas_call(
        body, name="memkv_bwd",
        out_shape=[jax.ShapeDtypeStruct((2, N_CHIPS, half, 2 * XATTN_W), F32), jax.ShapeDtypeStruct((1, D_MODEL), F32)],
        compiler_params=pltpu.CompilerParams(vmem_limit_bytes=VMEM_LIMIT_V7X),
    )(mem, g_mem, w_kv, dmkv)


def _sum_of_partials(own_ref, sibling_ref, other_refs):
    acc = own_ref[0, 0] + sibling_ref[0]
    for ref in other_refs:
        acc = acc + ref[0].astype(F32)
    return acc


def _in_proj_bwd(dqkv, tail, cos, sin, w_in, x, h, g, dx1, after, tm, sums=None):
    S = x.shape[0]
    step_w = 2 * 256
    half = D_MODEL // 2
    n_steps = S // tm
    sum_grads, sum_sibling, sum_others, place = sums if sums is not None else ([], [], [], jnp.zeros((2,), jnp.int32))
    k = len(sum_grads)

    def body(place_ref, dq_ref, dk_ref, dv_ref, tail_ref, cos_ref, sin_ref, w_hbm, x_ref, h_ref, g_ref, dx1_ref,
             after_ref, *refs):
        sum_refs, (dx_ref, gw_ref, dg_ref), sum_out_refs = refs[:5 * k], refs[5 * k:5 * k + 3], refs[5 * k + 3:6 * k + 3]
        dproj_ref, w_full, sems = refs[6 * k + 3:]
        for a in range(k):
            sum_out_refs[a][0] = _sum_of_partials(sum_refs[a], sum_refs[k + a], sum_refs[2 * k + 3 * a:2 * k + 3 * a + 3])
        _side_by_side(w_hbm, w_full, sems)

        @pl.when(pl.program_id(0) == 0)
        def _():
            dg_ref[...] = jnp.zeros_like(dg_ref)
            gw_ref[...] = jnp.zeros_like(gw_ref)

        halves = [slice(0, tm // 2), slice(tm // 2, tm)]
        for rows in halves:
            c, s = cos_ref[rows, :], sin_ref[rows, :]
            for j in range(ATTN_W // 128):
                cols = slice(128 * j, 128 * (j + 1))
                dproj_ref[rows, cols] = _rope128(dq_ref[rows, cols] * SCALE, c, s, True).astype(BF16)
                dproj_ref[rows, ATTN_W + 128 * j:ATTN_W + 128 * (j + 1)] = _rope128(dk_ref[rows, cols], c, s, True).astype(BF16)
            dproj_ref[rows, 2 * ATTN_W:3 * ATTN_W] = dv_ref[rows, :].astype(BF16)
            dproj_ref[rows, 3 * ATTN_W:PROJ_W] = tail_ref[rows, :]
        dhs = [lax.dot_general(dproj_ref[rows, :], w_full[...], NT, preferred_element_type=F32) for rows in halves]
        for rows, dh in zip(halves, dhs):
            xhat, r = _rms_hat(x_ref[rows, :])
            dg_ref[...] += jnp.sum(dh * xhat, axis=0, keepdims=True)
            dx_ref[rows, :] = dx1_ref[rows, :] + _rms_bwd(xhat, r, g_ref[...], dh)
        hb = h_ref[...]
        for step in range(PROJ_W // step_w):
            res = lax.dot_general(hb, dproj_ref[:, step * step_w:(step + 1) * step_w], TN, preferred_element_type=F32)
            lo = step * step_w
            while lo < (step + 1) * step_w:
                chip = lo // SHARD_IN
                hi = min((step + 1) * step_w, (chip + 1) * SHARD_IN)
                for hh in range(2):
                    gw_ref[hh, chip, :, lo - chip * SHARD_IN:hi - chip * SHARD_IN] += (
                        res[half * hh:half * (hh + 1), lo - step * step_w:hi - step * step_w])
                lo = hi

    whole = lambda shape: pl.BlockSpec(shape, lambda i, p: (0,) * len(shape))
    slab = lambda t: (1, t.shape[-2] // n_steps, t.shape[-1])
    sum_specs = ([pl.BlockSpec((1,) + slab(t), lambda i, p: (p[0], p[1], i, 0)) for t in sum_grads]
                 + [pl.BlockSpec(slab(t), lambda i, p: (p[1], i, 0)) for t in sum_grads]
                 + [pl.BlockSpec(slab(t), lambda i, p, j=j: (j, i, 0)) for t in sum_grads for j in range(3)])
    results = pl.pallas_call(
        body, name="in_proj_bwd",
        out_shape=[jax.ShapeDtypeStruct((S, D_MODEL), F32), jax.ShapeDtypeStruct((2, N_CHIPS, half, SHARD_IN), F32),
                   jax.ShapeDtypeStruct((1, D_MODEL), F32)]
        + [jax.ShapeDtypeStruct((2,) + t.shape[2:], F32) for t in sum_grads],
        grid_spec=pltpu.PrefetchScalarGridSpec(
            num_scalar_prefetch=1, grid=(n_steps,),
            in_specs=[_rows(tm, ATTN_W)] * 3 + [_rows(tm, PROJ_W - 3 * ATTN_W), _rows(tm, 128), _rows(tm, 128),
                      pl.BlockSpec(memory_space=pl.ANY), _rows(tm, D_MODEL), _rows(tm, D_MODEL),
                      _resident((1, D_MODEL)), _rows(tm, D_MODEL), pl.BlockSpec(memory_space=pl.ANY)] + sum_specs,
            out_specs=[_rows(tm, D_MODEL), whole((2, N_CHIPS, half, SHARD_IN)), whole((1, D_MODEL))]
            + [pl.BlockSpec(slab(t), lambda i, p: (p[0], i, 0)) for t in sum_grads],
            scratch_shapes=[pltpu.VMEM((tm, PROJ_W), BF16), pltpu.VMEM((D_MODEL, PROJ_W), BF16),
                            pltpu.SemaphoreType.DMA((N_CHIPS,))]),
        compiler_params=_params("arbitrary"),
    )(place, *dqkv, tail, cos, sin, w_in, x, h, g, dx1, after, *sum_grads, *sum_sibling,
      *[o for o in sum_others for _ in range(3)])
    return [*results[:3], list(results[3:])]


def _row_tile(rows):
    return ROW_TILE if rows % ROW_TILE == 0 else rows


def _chip_sums_bf16(name, grads, from_sibling, place):
    k = len(grads)
    _, n, rows, _ = grads[0].shape
    tr = _row_tile(rows)

    def body(place_ref, *refs):
        for g_ref, b_ref, o_ref in zip(refs[:k], refs[k:2 * k], refs[2 * k:]):
            o_ref[...] = (g_ref[0] + b_ref[...]).astype(BF16)

    mine = lambda g: pl.BlockSpec((1, 1, tr, g.shape[3]), lambda s, i, p: (p[0], s, i, 0))
    slab = lambda g: pl.BlockSpec((1, tr, g.shape[3]), lambda s, i, p: (s, i, 0))
    return pl.pallas_call(
        body, name=name, out_shape=[jax.ShapeDtypeStruct(g.shape[1:], BF16) for g in grads],
        grid_spec=pltpu.PrefetchScalarGridSpec(
            num_scalar_prefetch=1, grid=(n, rows // tr),
            in_specs=[mine(g) for g in grads] + [slab(g) for g in grads], out_specs=[slab(g) for g in grads]),
        compiler_params=_params("parallel", "parallel"),
    )(place, *grads, *from_sibling)


def _final_sums(name, grads, from_sibling, others, place):
    k = len(grads)
    rows = grads[0].shape[2]
    tr = _row_tile(rows)

    def body(place_ref, *refs):
        for a in range(k):
            refs[5 * k + a][0] = _sum_of_partials(refs[a], refs[k + a], refs[2 * k + 3 * a:2 * k + 3 * a + 3])

    own = lambda g: pl.BlockSpec((1, 1, tr, g.shape[3]), lambda i, p: (p[0], p[1], i, 0))
    sib = lambda g: pl.BlockSpec((1, tr, g.shape[3]), lambda i, p: (p[1], i, 0))
    other = lambda g, j: pl.BlockSpec((1, tr, g.shape[3]), lambda i, p: (j, i, 0))
    return pl.pallas_call(
        body, name=name, out_shape=[jax.ShapeDtypeStruct((2,) + g.shape[2:], F32) for g in grads],
        grid_spec=pltpu.PrefetchScalarGridSpec(
            num_scalar_prefetch=1, grid=(rows // tr,),
            in_specs=[own(g) for g in grads] + [sib(g) for g in grads] + [other(g, j) for g in grads for j in range(3)],
            out_specs=[pl.BlockSpec((1, tr, g.shape[3]), lambda i, p: (p[0], i, 0)) for g in grads]),
        compiler_params=_params("parallel"),
    )(place, *grads, *from_sibling, *[o for o in others for _ in range(3)])


def _adamw_update(w, g, m, v):
    m = ADAM_B1 * m + (1.0 - ADAM_B1) * g
    v = ADAM_B2 * v + (1.0 - ADAM_B2) * (g * g)
    m_hat = m * (1.0 / (1.0 - ADAM_B1 ** ADAM_STEP))
    v_hat = v * (1.0 / (1.0 - ADAM_B2 ** ADAM_STEP))
    return -ADAM_LR * (m_hat / (jnp.sqrt(v_hat) + ADAM_EPS) + ADAM_WD * w), m, v


def _adamw(name, params, after):
    k = len(params)
    rows = params[0][0].shape[0]
    tr = ADAMW_ROW_TILE if rows % ADAMW_ROW_TILE == 0 else rows

    def body(*refs):
        ins, outs = refs[:4 * k], refs[4 * k + 1:]
        for a in range(k):
            w_ref, g_ref, m_ref, v_ref = ins[4 * a:4 * a + 4]
            g = g_ref[...]
            outs[4 * a][...] = g
            outs[4 * a + 1][...], outs[4 * a + 2][...], outs[4 * a + 3][...] = _adamw_update(w_ref[...], g, m_ref[...], v_ref[...])

    spec = lambda w: pl.BlockSpec((tr, w.shape[1]), lambda i: (i, 0))
    out = pl.pallas_call(
        body, name=name, grid=(rows // tr,),
        in_specs=[spec(p[0]) for p in params for _ in range(4)] + [pl.BlockSpec(memory_space=pl.ANY)],
        out_specs=[spec(p[0]) for p in params for _ in range(4)],
        out_shape=[jax.ShapeDtypeStruct(p[0].shape, F32) for p in params for _ in range(4)],
        compiler_params=_params("parallel"),
    )(*[t for p in params for t in p], after)
    return [out[4 * a:4 * a + 4] for a in range(k)]


def _small_update(blocks, chip, gains, gains_m, gains_v, taps, taps_m, taps_v):
    n = len(gains)
    widths = [g.shape[1] for g in gains]
    k, w = taps.shape

    def body(*refs):
        chip_ref, blocks_ref = refs[0], refs[1]
        params = [refs[2 + 3 * i:5 + 3 * i] for i in range(n + 1)]
        outs = [refs[2 + 3 * (n + 1) + 4 * i:2 + 3 * (n + 1) + 4 * (i + 1)] for i in range(n + 1)]
        loss_ref = refs[-1]
        summed = blocks_ref[0]
        for device in range(1, blocks.shape[0]):
            summed = summed + blocks_ref[device]
        for i in range(n):
            g = summed[i:i + 1, 0:widths[i]]
            wr, mr, vr = params[i]
            outs[i][0][...] = g
            outs[i][1][...], outs[i][2][...], outs[i][3][...] = _adamw_update(wr[...], g, mr[...], vr[...])
        g = summed[n:n + k, 0:w]
        for j in range(1, N_CHIPS):
            g = jnp.where(chip_ref[0] == j, summed[n:n + k, w * j:w * (j + 1)], g)
        wr, mr, vr = params[n]
        for out_ref, val in zip(outs[n], (g, *_adamw_update(wr[...], g, mr[...], vr[...]))):
            for j in range(k):
                out_ref[j] = val[j:j + 1, :]
        loss_ref[...] = summed[n + k:n + k + 1, 0:1]

    vmem = pl.BlockSpec(memory_space=pltpu.VMEM)
    operands = [chip, blocks]
    for p in zip(list(gains) + [taps], list(gains_m) + [taps_m], list(gains_v) + [taps_v]):
        operands += list(p)
    shapes = [jax.ShapeDtypeStruct(shape, F32) for shape in [g.shape for g in gains] + [(k, 1, w)] for _ in range(4)]
    out = pl.pallas_call(
        body, name="small_update", out_shape=shapes + [jax.ShapeDtypeStruct((1, 1), F32)],
        in_specs=[pl.BlockSpec(memory_space=pltpu.SMEM)] + [vmem] * (len(operands) - 1),
        out_specs=[vmem] * (len(shapes) + 1),
    )(*operands)
    return [out[4 * i:4 * (i + 1)] for i in range(n + 1)], out[-1]


def _place():
    return lax.axis_index("x"), lax.axis_index("y"), lax.axis_index("c")


def _other_chips(x, y):
    return [(1 - x, y), (x, 1 - y), (1 - x, 1 - y)]


def _allgather_finish(name, shards, landed, pass_on):
    n = len(shards)

    def body(*refs):
        ins, outs, stage = refs[:n], refs[2 * n:3 * n], refs[3 * n:4 * n]
        send_sems, recv_sems, local_sems = refs[4 * n:]
        x, y, c = _place()
        chips = _other_chips(x, y)
        barrier = pltpu.get_barrier_semaphore()
        pl.semaphore_signal(barrier, inc=1, device_id=(x, y, 1 - c), device_id_type=MESH)
        pl.semaphore_wait(barrier, 1)

        def copy(a, k, chip, half):
            place = outs[a].at[2 * chip[0] + chip[1], half]
            return pltpu.make_async_remote_copy(
                src_ref=place, dst_ref=place, send_sem=send_sems.at[3 * a + k], recv_sem=recv_sems.at[3 * a + k],
                device_id=(x, y, 1 - c), device_id_type=MESH)

        load = [pltpu.make_async_copy(ins[a], stage[a], local_sems.at[a]) for a in range(n)]
        local = [pltpu.make_async_copy(stage[a], outs[a].at[2 * x + y], local_sems.at[a]) for a in range(n)]
        for cp in load:
            cp.start()
        passed = [copy(a, k, chip, c) for a in range(n) if pass_on[a] for k, chip in enumerate(chips)]
        for cp in passed:
            cp.start()
        for a in range(n):
            load[a].wait()
            local[a].start()
        for a in range(n):
            if pass_on[a]:
                for k, chip in enumerate(chips):
                    copy(a, k, chip, 1 - c).wait_recv()
        for cp in passed:
            cp.wait_send()
        for cp in local:
            cp.wait()

    any_spec = pl.BlockSpec(memory_space=pl.ANY)
    return pl.pallas_call(
        body, name=name,
        out_shape=[jax.ShapeDtypeStruct((N_CHIPS,) + s.shape, s.dtype) for s in shards],
        in_specs=[any_spec] * (2 * n), out_specs=[any_spec] * n,
        input_output_aliases={n + a: a for a in range(n)},
        scratch_shapes=[pltpu.VMEM(s.shape, s.dtype) for s in shards]
        + [pltpu.SemaphoreType.DMA((3 * n,)), pltpu.SemaphoreType.DMA((3 * n,)), pltpu.SemaphoreType.DMA((n,))],
        compiler_params=pltpu.CompilerParams(vmem_limit_bytes=VMEM_LIMIT_V7X, collective_id=HANDSHAKES["sibling"][0]),
    )(*shards, *landed)


def _plan_first_hop(x, y, c, shards, lands):
    return [(shards[a].at[c], lands[a].at[2 * x + y, c], lands[a].at[2 * chip[0] + chip[1], c], (*chip, c))
            for a in range(len(shards)) for chip in _other_chips(x, y)]


def _plan_pass_on(x, y, c, nothing, lands):
    def place(a, chip, half):
        return lands[a].at[2 * chip[0] + chip[1], half]

    return [(place(a, chip, c), place(a, chip, c), place(a, chip, 1 - c), (x, y, 1 - c))
            for a in range(len(lands)) for chip in _other_chips(x, y)]


def _plan_own_half_to_sibling(x, y, c, nothing, lands):
    return [(lands[a].at[c], lands[a].at[c], lands[a].at[1 - c], (x, y, 1 - c)) for a in range(len(lands))]


def _plan_other_half_to_sibling(x, y, c, grads, lands):
    return [(grads[a].at[1 - c], lands[a], lands[a], (x, y, 1 - c)) for a in range(len(grads))]


def _plan_to_other_chips(x, y, c, partials, lands):
    return [(partials[a].at[2 * chip[0] + chip[1]], lands[a].at[k], lands[a].at[k], (*chip, c))
            for a in range(len(partials)) for k, chip in enumerate(_other_chips(x, y))]


def _plan_to_all(x, y, c, blocks, lands):
    flips = [(fx, fy, fc) for fx in (0, 1) for fy in (0, 1) for fc in (0, 1) if (fx, fy, fc) != (0, 0, 0)]
    peers = [(1 - x if fx else x, 1 - y if fy else y, 1 - c if fc else c) for fx, fy, fc in flips]
    return [(blocks[0], lands[0].at[4 * x + 2 * y + c], lands[0].at[4 * p[0] + 2 * p[1] + p[2]], p) for p in peers]


def _planned_copies(plan, srcs, lands, send_sems, recv_sems):
    x, y, c = _place()

    def pair(k, src, there, here, to):
        make = lambda dst: pltpu.make_async_remote_copy(
            src_ref=src, dst_ref=dst, send_sem=send_sems.at[k], recv_sem=recv_sems.at[k], device_id=to, device_id_type=MESH)
        return make(there), make(here)

    return [pair(k, *entry) for k, entry in enumerate(plan(x, y, c, srcs, lands))]


_HBM_SPEC = pl.BlockSpec(memory_space=pltpu.HBM)
_SEM_SPEC = pl.BlockSpec(memory_space=pltpu.SEMAPHORE)


def _hbm(a):
    return pltpu.with_memory_space_constraint(a, pltpu.HBM)


HANDSHAKES = {
    "sibling": (1, lambda x, y, c: [(x, y, 1 - c)]),
}


def _exchange_start(name, plan, n_copies, srcs, land_shapes, after, lands=None, peers=None):
    if lands is None:
        lands = [lax.empty(s.shape, s.dtype) for s in land_shapes]
    land_shapes = lands
    ns, nl = len(srcs), len(land_shapes)
    n_in = ns + nl + 1
    collective_id, peers_of = HANDSHAKES[peers] if peers else (None, None)

    def body(*refs):
        if peers:
            who = peers_of(*_place())
            barrier = pltpu.get_barrier_semaphore()
            for peer in who:
                pl.semaphore_signal(barrier, inc=1, device_id=peer, device_id_type=MESH)
            pl.semaphore_wait(barrier, len(who))
        for send, _ in _planned_copies(plan, refs[:ns], refs[ns:ns + nl], refs[n_in], refs[n_in + 1]):
            send.start()
        refs[-1][...] = jnp.zeros_like(refs[-1])

    out = pl.pallas_call(
        body, name=name,
        out_shape=(pltpu.SemaphoreType.DMA((n_copies,)), pltpu.SemaphoreType.DMA((n_copies,)),
                   *[pltpu.HBM(s.shape, s.dtype) for s in land_shapes], jax.ShapeDtypeStruct((8, 128), F32)),
        in_specs=[_HBM_SPEC] * (ns + nl) + [pl.BlockSpec(memory_space=pl.ANY)],
        out_specs=(_SEM_SPEC, _SEM_SPEC, *[_HBM_SPEC] * nl, pl.BlockSpec(memory_space=pltpu.VMEM)),
        input_output_aliases={ns + i: 2 + i for i in range(nl)},
        compiler_params=pltpu.CompilerParams(has_side_effects=pltpu.SideEffectType.DATAFLOW_SIDE_EFFECTING,
                                             collective_id=collective_id),
    )(*[_hbm(s) for s in srcs], *[_hbm(l) for l in lands], after)
    return out[0], out[1], list(out[2:2 + nl]), out[-1]


def _exchange_wait(name, plan, srcs, started, after):
    send_sems, recv_sems, lands, _ = started
    ns, nl = len(srcs), len(lands)
    after = list(after) if isinstance(after, (list, tuple)) else [after]

    def body(*refs):
        for send, recv in _planned_copies(plan, refs[:ns], refs[ns:ns + nl], refs[ns + nl], refs[ns + nl + 1]):
            send.wait_send()
            recv.wait_recv()

    return pl.pallas_call(
        body, name=name, out_shape=[pltpu.HBM(l.shape, l.dtype) for l in lands],
        in_specs=[_HBM_SPEC] * (ns + nl) + [_SEM_SPEC, _SEM_SPEC] + [pl.BlockSpec(memory_space=pl.ANY)] * len(after),
        out_specs=[_HBM_SPEC] * nl, input_output_aliases={ns + i: i for i in range(nl)},
        compiler_params=pltpu.CompilerParams(has_side_effects=pltpu.SideEffectType.DATAFLOW_SIDE_EFFECTING),
    )(*[_hbm(s) for s in srcs], *lands, send_sems, recv_sems, *after)


def _like(arrays, lead, dtype=None):
    return [jax.ShapeDtypeStruct(tuple(lead) + a.shape[-2:], dtype or a.dtype) for a in arrays]


class _StepExchanges:
    def __init__(self, mats, conv_w):
        x, y, c = _place()
        self.place = jnp.stack([c, 2 * x + y]).astype(jnp.int32)
        shards = [w.astype(BF16).reshape(2, w.shape[0] // 2, w.shape[1]) for w in mats]
        self._in_shard = shards[:1]
        self._in = _exchange_start("w_in_allgather_start", _plan_first_hop, 3, self._in_shard,
                                   _like(self._in_shard, (N_CHIPS, 2)), shards[0])
        self.zero = self._in[3]
        taps = jnp.pad(conv_w, ((0, 8 - conv_w.shape[0]), (0, 128 - conv_w.shape[1])))
        self._rest_shards = shards[1:] + [jnp.stack([taps, jnp.zeros_like(taps)])]
        self._taps_shape = conv_w.shape
        self._groups = {}

    def w_in(self, after):
        landed = _exchange_wait("w_in_allgather_wait", _plan_first_hop, self._in_shard, self._in,
                                list(after) + self._rest_shards)
        (w_in,) = _allgather_finish("w_in_allgather_finish", self._in_shard, landed, [True])
        self._rest = _exchange_start("rest_allgather_start", _plan_first_hop, 3 * len(self._rest_shards),
                                     self._rest_shards, _like(self._rest_shards, (N_CHIPS, 2)), w_in)
        self.zero = self._rest[3]
        return w_in.reshape(N_CHIPS, 2 * w_in.shape[2], w_in.shape[3])

    def rest_weights(self, after):
        landed = _exchange_wait("rest_allgather_wait", _plan_first_hop, self._rest_shards, self._rest, after)
        kv, out, up, down, taps = _allgather_finish("rest_allgather_finish", self._rest_shards, landed,
                                                    [True, True, False, False, True])
        self._up_down = _exchange_start("up_down_pass_on_start", _plan_pass_on, 6, [], None, self.zero, lands=[up, down],
                                        peers="sibling")
        self.zero = self._up_down[3]
        k, w = self._taps_shape
        taps = taps[:, 0, :k, :w].transpose(1, 0, 2).reshape(k, N_CHIPS * w)
        return [g.reshape(N_CHIPS, 2 * g.shape[2], g.shape[3]) for g in (kv, out)], taps

    def up_down(self, after):
        full = _exchange_wait("up_down_pass_on_wait", _plan_pass_on, [], self._up_down, after)
        return [g.reshape(N_CHIPS, 2 * g.shape[2], g.shape[3]) for g in full]

    def send_grads(self, key, grads):
        grads = list(grads)
        started = _exchange_start(f"{key}_grads_to_sibling_start", _plan_other_half_to_sibling, len(grads), grads,
                                  _like(grads, (N_CHIPS,)), self.zero, peers="sibling")
        self._groups[key] = dict(grads=grads, to_sibling=started)
        self.zero = started[3]

    def grads_at_sibling(self, key, after):
        group = self._groups[key]
        grads = group["grads"]
        group["from_sibling"] = _exchange_wait(f"{key}_grads_to_sibling_wait", _plan_other_half_to_sibling, grads,
                                               group["to_sibling"], after)
        group["partials"] = _chip_sums_bf16(f"{key}_chip_sums", grads, group["from_sibling"], self.place)
        group["to_chips"] = _exchange_start(f"{key}_grads_to_chips_start", _plan_to_other_chips, 3 * len(grads),
                                            group["partials"], _like(group["partials"], (3,)), self.zero)
        self.zero = group["to_chips"][3]

    def final_sum_operands(self, key, after):
        group = self._groups[key]
        from_chips = _exchange_wait(f"{key}_grads_to_chips_wait", _plan_to_other_chips, group["partials"],
                                    group["to_chips"], after)
        return group["grads"], group["from_sibling"], from_chips, self.place

    def grads_summed(self, key, after):
        return _final_sums(f"{key}_final_sums", *self.final_sum_operands(key, after))

    def send_sums(self, key, sums):
        self._groups[key + "_sums"] = _exchange_start(f"{key}_sums_to_sibling_start", _plan_own_half_to_sibling,
                                                      len(sums), [], None, self.zero, lands=list(sums),
                                                      peers="sibling")
        self.zero = self._groups[key + "_sums"][3]

    def whole_sums(self, key, after):
        full = _exchange_wait(f"{key}_sums_to_sibling_wait", _plan_own_half_to_sibling, [], self._groups[key + "_sums"], after)
        return [t.reshape(2 * t.shape[1], t.shape[2]) for t in full]

    def send_small(self, block):
        self._small = block
        self._small_started = _exchange_start("small_grads_start", _plan_to_all, 7, [block],
                                              [jax.ShapeDtypeStruct((8,) + block.shape, block.dtype)], self.zero)
        self.zero = self._small_started[3]

    def small_blocks(self, after):
        x, y, c = _place()
        (landed,) = _exchange_wait("small_grads_wait", _plan_to_all, [self._small], self._small_started, after)
        return lax.dynamic_update_index_in_dim(landed, self._small, 4 * x + 2 * y + c, 0)


def _rope_tables(positions):
    half = HEAD // 2
    inv_freq = jnp.float32(ROPE_THETA) ** (-(jnp.arange(half, dtype=F32) * 2.0 / HEAD))
    ang = positions.astype(F32)[:, None] * inv_freq
    cos, sin = jnp.cos(ang), jnp.sin(ang)
    return jnp.tile(cos, (1, 4)), jnp.tile(jnp.concatenate([-sin, sin], axis=1), (1, 2))


def _local_step(x, mem, positions, target, gains, ex):
    g_pre_mix, g_mem, g_a, g_c, g_x, g_post_mix, g_pre_mlp, g_post_mlp = gains
    tm = ROW_TILE
    cos, sin = _rope_tables(positions)
    h = _pre_norm(x, g_pre_mix, ex.zero, tm)
    w_in = ex.w_in([h, cos, sin])

    q, k, v, bcu, qx = _in_proj_fwd(h, w_in, cos, sin, ex.zero, tm)
    ya, lse = _attn_fwd(q, k, v)
    (w_kv, w_out), conv_w = ex.rest_weights(lse)
    w_kv, w_out = (w.reshape(N_CHIPS * w.shape[1], w.shape[2]) for w in (w_kv, w_out))
    memn, mkv = _memkv_fwd(mem, g_mem, w_kv, ex.zero)
    yx, ycat, y2, x1 = _mix_fwd(ya, bcu, qx, mkv, conv_w, g_a, g_c, g_x, w_out, g_post_mix, x, tm)
    w_up, w_down = ex.up_down(x1)
    w_down = w_down.reshape(N_CHIPS * w_down.shape[1], w_down.shape[2])
    h2, f, du, df2, dx1, dg_pre_mlp, dg_post_mlp, loss = _mlp_fwd_bwd(x1, target, g_pre_mlp, g_post_mlp, w_up, w_down,
                                                                      MLP_ROW_TILE)
    gw_down = _weight_grad("grad_w_down", f, df2, True, ex.zero)
    gw_up = _weight_grad("grad_w_up", h2, du, False, ex.zero)
    ex.send_grads("early", [gw_up, gw_down])

    gw_out, dya, delta, tail, dmkv, g_conv, dg_post_mix, dg_a, dg_c, dg_x = _mixer_bwd(
        dx1, y2, ycat, ya, yx, bcu, qx, mkv, conv_w, g_a, g_c, g_x, w_out, g_post_mix, ex.zero, tm)
    ex.grads_at_sibling("early", dya)
    gw_kv, dg_mem = _memkv_bwd(mem, g_mem, w_kv, dmkv)
    ex.send_grads("mid", [gw_out, gw_kv])
    dqkv = _attn_bwd(q, k, v, dya, lse, delta, ex.zero)
    ex.grads_at_sibling("mid", dqkv[0])
    grad_x, gw_in, dg_pre_mix, early_sums = _in_proj_bwd(dqkv, tail, cos, sin, w_in, x, h, g_pre_mix, dx1, ex.zero, tm,
                                                         ex.final_sum_operands("early", dqkv[0]))
    ex.send_grads("late", [gw_in])
    gain_grads = [dg_pre_mix, dg_mem, dg_a, dg_c, dg_x, dg_post_mix, dg_pre_mlp, dg_post_mlp]
    ex.send_small(_pack_small(gain_grads, g_conv, loss))
    return grad_x, early_sums


def _pack_small(gains, conv, scalar):
    n, k = len(gains), conv.shape[0]

    def body(*refs):
        out_ref = refs[-1]
        out_ref[...] = jnp.zeros_like(out_ref)
        for i, g_ref in enumerate(refs[:n]):
            out_ref[i:i + 1, 0:g_ref.shape[1]] = g_ref[...]
        out_ref[n:n + k, 0:conv.shape[1]] = refs[n][...]
        out_ref[n + k:n + k + 1, 0:1] = refs[n + 1][...]

    return pl.pallas_call(body, name="pack_small", out_shape=jax.ShapeDtypeStruct((SMALL_ROWS, D_MODEL), F32))(
        *gains, conv, scalar)


def kernel(x, mem, positions, g_pre_mix, g_mem, w_in, w_mem_kv, conv_w, g_attn_out, g_conv_out, g_xattn_out, w_out, g_post_mix, g_pre_mlp, w_up, w_down, g_post_mlp, loss_target, m_g_pre_mix, m_g_mem, m_w_in, m_w_mem_kv, m_conv_w, m_g_attn_out, m_g_conv_out, m_g_xattn_out, m_w_out, m_g_post_mix, m_g_pre_mlp, m_w_up, m_w_down, m_g_post_mlp, v_g_pre_mix, v_g_mem, v_w_in, v_w_mem_kv, v_conv_w, v_g_attn_out, v_g_conv_out, v_g_xattn_out, v_w_out, v_g_post_mix, v_g_pre_mlp, v_w_up, v_w_down, v_g_post_mlp):
    chip = 2 * lax.axis_index("x") + lax.axis_index("y")
    gains = [g_pre_mix, g_mem, g_attn_out, g_conv_out, g_xattn_out, g_post_mix, g_pre_mlp, g_post_mlp]
    gains_m = [m_g_pre_mix, m_g_mem, m_g_attn_out, m_g_conv_out, m_g_xattn_out, m_g_post_mix, m_g_pre_mlp, m_g_post_mlp]
    gains_v = [v_g_pre_mix, v_g_mem, v_g_attn_out, v_g_conv_out, v_g_xattn_out, v_g_post_mix, v_g_pre_mlp, v_g_post_mlp]
    mats =[w_in[0], w_mem_kv[0], w_out[0], w_up[0], w_down[0]]
    mats_m = [m_w_in[0], m_w_mem_kv[0], m_w_out[0], m_w_up[0], m_w_down[0]]
    mats_v = [v_w_in[0], v_w_mem_kv[0], v_w_out[0], v_w_up[0], v_w_down[0]]

    ex = _StepExchanges(mats, conv_w[0])
    grad_x, early_sums = _local_step(x[0], mem[0], positions[0], loss_target[0], gains, ex)

    ex.send_sums("four", [*early_sums, *ex.grads_summed("mid", ex.zero)])
    ex.grads_at_sibling("late", ex.zero)
    up_sum, down_sum, out_sum, kv_sum = ex.whole_sums("four", ex.zero)
    params = lambda a, g: (mats[a], g, mats_m[a], mats_v[a])
    new_up, new_down = _adamw("adamw_up_down", [params(3, up_sum), params(4, down_sum)], ex.zero)
    new_out, new_kv = _adamw("adamw_out_kv", [params(2, out_sum), params(1, kv_sum)], ex.zero)

    small, total = _small_update(ex.small_blocks(new_kv[1]), chip.reshape(1).astype(jnp.int32), gains, gains_m,
                                 gains_v, conv_w[0], m_conv_w[0], v_conv_w[0])

    ex.send_sums("last", ex.grads_summed("late", small[0][1]))
    (in_sum,) = ex.whole_sums("last", ex.zero)
    (new_in,) = _adamw("adamw_in", [params(0, in_sum)], in_sum)
    mat_new = [new_in, new_kv, new_out, new_up, new_down]

    order = ["g_pre_mix", "g_mem", "w_in", "w_mem_kv", "conv_w", "g_attn_out", "g_conv_out", "g_xattn_out", "w_out",
             "g_post_mix", "g_pre_mlp", "w_up", "w_down", "g_post_mlp"]
    gain_names = ["g_pre_mix", "g_mem", "g_attn_out", "g_conv_out", "g_xattn_out", "g_post_mix", "g_pre_mlp", "g_post_mlp"]
    mat_names = ["w_in", "w_mem_kv", "w_out", "w_up", "w_down"]

    def leaf(kind, name):
        if name in gain_names:
            return small[gain_names.index(name)][kind]
        if name == "conv_w":
            return jnp.swapaxes(small[len(gain_names)][kind], 0, 1)
        return mat_new[mat_names.index(name)][kind][None]

    return (total[0, 0], grad_x[None], *[leaf(kind, name) for kind in range(4) for name in order])
```

```python
import jax
import jax.numpy as jnp
from jax import lax
from jax.experimental import pallas as pl
from jax.experimental.pallas import tpu as pltpu

F32, BF16 = jnp.float32, jnp.bfloat16

D_MODEL = 1024
ATTN_W = 512
CONV_W = 256
XATTN_W = 256
PROJ_W = 3 * ATTN_W + 3 * CONV_W + XATTN_W
D_FF = 4096
HEAD = 64
N_BACK = 128
DILATIONS = (1, 4, 16)
PATTERN_ORDER = DILATIONS[::-1]
ROPE_THETA = 10000.0
EPS = 1e-6
NEG_INF = -1e30
SCALE = HEAD ** -0.5
N_CHIPS = 4
SHARD_IN = PROJ_W // N_CHIPS
SHARD_FF = D_FF // N_CHIPS

ADAM_LR, ADAM_B1, ADAM_B2, ADAM_EPS, ADAM_WD, ADAM_STEP = 0.001, 0.9, 0.999, 1e-08, 0.01, 10

VMEM_LIMIT_V7X = 56 * 1024 * 1024
ROW_TILE = 512
MLP_ROW_TILE = 256
ADAMW_ROW_TILE = 256
SMALL_ROWS = 16

NT = (((1,), (1,)), ((), ()))
TN = (((0,), (0,)), ((), ()))
MESH = pl.DeviceIdType.MESH


def _params(*sem):
    return pltpu.CompilerParams(dimension_semantics=sem, vmem_limit_bytes=VMEM_LIMIT_V7X)


def _resident(shape):
    return pl.BlockSpec(shape, lambda *_: (0,) * len(shape), pipeline_mode=pl.Buffered(1))


def _rows(tm, width):
    return pl.BlockSpec((tm, width), lambda i, *_: (i, 0))


def _rms_hat(x):
    r = lax.rsqrt(jnp.mean(x * x, axis=-1, keepdims=True) + EPS)
    return x * r, r


def _rms_bwd(xhat, r, g, dy):
    gdy = dy * g
    return r * (gdy - xhat * jnp.mean(xhat * gdy, axis=-1, keepdims=True))


def _rope128(t, cos, sin_signed, inverse):
    lane = lax.broadcasted_iota(jnp.int32, t.shape, 1)
    first_half = (lane % HEAD) < (HEAD // 2)
    rot = jnp.where(first_half, pltpu.roll(t, 128 - HEAD // 2, 1), pltpu.roll(t, HEAD // 2, 1))
    return t * cos - rot * sin_signed if inverse else t * cos + rot * sin_signed


def _pre_norm(x, g, after, tm):
    S = x.shape[0]

    def body(x_ref, g_ref, after_ref, h_ref):
        h_ref[...] = (_rms_hat(x_ref[...])[0] * g_ref[...]).astype(BF16)

    return pl.pallas_call(
        body, name="pre_norm", grid=(S // tm,),
        in_specs=[_rows(tm, D_MODEL), _resident((1, D_MODEL)), pl.BlockSpec(memory_space=pl.ANY)],
        out_specs=_rows(tm, D_MODEL), out_shape=jax.ShapeDtypeStruct((S, D_MODEL), BF16),
        compiler_params=_params("parallel"),
    )(x, g, after)


def _side_by_side(w_hbm, w_full, sems):
    width = w_hbm.shape[2]

    @pl.when(pl.program_id(0) == 0)
    def _():
        copies = [pltpu.make_async_copy(w_hbm.at[j], w_full.at[:, pl.ds(width * j, width)], sems.at[j])
                  for j in range(N_CHIPS)]
        for cp in copies:
            cp.start()
        for cp in copies:
            cp.wait()


def _in_proj_fwd(h, w_in, cos, sin, after, tm):
    S = h.shape[0]

    def body(h_ref, w_hbm, cos_ref, sin_ref, after_ref, q_ref, k_ref, v_ref, bcu_ref, qx_ref, proj, w_full, sems):
        _side_by_side(w_hbm, w_full, sems)
        proj[...] = jnp.dot(h_ref[...], w_full[...], preferred_element_type=F32)
        c, s = cos_ref[...], sin_ref[...]
        for j in range(ATTN_W // 128):
            lo = 128 * j
            q_ref[:, lo:lo + 128] = _rope128(proj[:, lo:lo + 128], c, s, False) * SCALE
            k_ref[:, lo:lo + 128] = _rope128(proj[:, ATTN_W + lo:ATTN_W + lo + 128], c, s, False)
        v_ref[...] = proj[:, 2 * ATTN_W:3 * ATTN_W]
        bcu_ref[...] = proj[:, 3 * ATTN_W:3 * ATTN_W + 3 * CONV_W]
        qx_ref[...] = proj[:, 3 * ATTN_W + 3 * CONV_W:PROJ_W].astype(BF16)

    return pl.pallas_call(
        body, name="in_proj_fwd", grid=(S // tm,),
        in_specs=[_rows(tm, D_MODEL), pl.BlockSpec(memory_space=pl.ANY), _rows(tm, 128), _rows(tm, 128),
                  pl.BlockSpec(memory_space=pl.ANY)],
        out_specs=[_rows(tm, ATTN_W), _rows(tm, ATTN_W), _rows(tm, ATTN_W), _rows(tm, 3 * CONV_W), _rows(tm, XATTN_W)],
        out_shape=[jax.ShapeDtypeStruct((S, ATTN_W), F32), jax.ShapeDtypeStruct((S, ATTN_W), F32),
                   jax.ShapeDtypeStruct((S, ATTN_W), F32), jax.ShapeDtypeStruct((S, 3 * CONV_W), F32),
                   jax.ShapeDtypeStruct((S, XATTN_W), BF16)],
        scratch_shapes=[pltpu.VMEM((tm, PROJ_W), F32), pltpu.VMEM((D_MODEL, PROJ_W), BF16),
                        pltpu.SemaphoreType.DMA((N_CHIPS,))],
        compiler_params=_params("arbitrary"),
    )(h, w_in, cos, sin, after)


def _memkv_fwd(mem, g_mem, w_kv, after):
    n_mem = mem.shape[0]

    def body(mem_ref, g_ref, w_ref, after_ref, mn_ref, kv_ref):
        mhat, _ = _rms_hat(mem_ref[...])
        mn = (mhat * g_ref[...]).astype(BF16)
        mn_ref[...] = mn
        kv_ref[...] = jnp.dot(mn, w_ref[...], preferred_element_type=F32).astype(BF16)

    vmem = pl.BlockSpec(memory_space=pltpu.VMEM)
    return pl.pallas_call(
        body, name="memkv_fwd", in_specs=[vmem, vmem, vmem, pl.BlockSpec(memory_space=pl.ANY)], out_specs=[vmem, vmem],
        out_shape=[jax.ShapeDtypeStruct((n_mem, D_MODEL), BF16), jax.ShapeDtypeStruct((n_mem, 2 * XATTN_W), BF16)],
        compiler_params=pltpu.CompilerParams(vmem_limit_bytes=VMEM_LIMIT_V7X),
    )(mem, g_mem, w_kv, after)


def _fill_band_bias(bias):
    row = lax.broadcasted_iota(jnp.int32, (N_BACK, 2 * N_BACK), 0)
    col = lax.broadcasted_iota(jnp.int32, (N_BACK, 2 * N_BACK), 1)
    band = (col >= row) & (col <= row + N_BACK)
    bias[1] = jnp.where(band, 0.0, NEG_INF)
    bias[0] = jnp.where(band & (col >= N_BACK), 0.0, NEG_INF)


def _strided(start, size, d):
    return pl.ds(start, size) if d == 1 else pl.ds(start, size, stride=d)


def _group_starts(g, G, nb, d):
    t0 = g * G
    r, n0 = lax.shift_right_logical(t0, nb.bit_length() - 1), lax.bitwise_and(t0, nb - 1)
    first = r + n0 * (N_BACK * d)
    before = r + jnp.maximum(n0 - 1, 0) * (N_BACK * d)
    starts = [before] + [first + u * (N_BACK * d) for u in range(G)]
    if d == 1:
        starts = [pl.multiple_of(st, N_BACK) for st in starts]
    return starts, n0


def _step_blocks(i, U, nb, d):
    G = min(U, nb)
    whole = G == nb
    row_blocks, blocks = [], []
    for grp in range(U // G):
        starts, n0 = _group_starts(i * (U // G) + grp, G, nb, d)
        base = len(row_blocks)
        if whole:
            row_blocks += [_strided(st, N_BACK, d) for st in starts[1:]]
            blocks += [(base + max(u - 1, 0), base + u, min(u, 1)) for u in range(G)]
        else:
            row_blocks += [_strided(st, N_BACK, d) for st in starts]
            blocks += [(base + u, base + u + 1, jnp.minimum(n0, 1) if u == 0 else 1) for u in range(G)]
    return row_blocks, blocks


def _by_head(a, b):
    lane = lax.broadcasted_iota(jnp.int32, (a.shape[0], 2 * HEAD), 1)
    return jnp.where(lane < HEAD, a, b)


def _head_only(t, hh):
    lane = lax.broadcasted_iota(jnp.int32, t.shape, 1)
    return jnp.where((lane < HEAD) == (hh == 0), t, jnp.zeros_like(t))


def _stack_heads(t):
    return jnp.concatenate([_head_only(t, 0), _head_only(t, 1)], axis=0)


def _head_columns(t):
    return jnp.concatenate([t[:, 0:1], t[:, HEAD:HEAD + 1]], axis=0)


def _unstack(t):
    return _by_head(t[:N_BACK], t[N_BACK:])


def _unstack_columns(t):
    return _by_head(jnp.broadcast_to(t[:N_BACK], (N_BACK, 2 * HEAD)), jnp.broadcast_to(t[N_BACK:], (N_BACK, 2 * HEAD)))


FWD_BLOCKS_PER_STEP = 4
BWD_BLOCKS_PER_STEP = 4
BWD_CHUNK = 64


def _attn_fwd(q, k, v):
    S = q.shape[0]
    U = FWD_BLOCKS_PER_STEP

    def body(q_ref, k_ref, v_ref, y_ref, m_ref, l_scr, bias):
        _fill_band_bias(bias)
        for g, d in enumerate(PATTERN_ORDER):
            nb = S // d // N_BACK
            first_pattern, last_pattern = g == 0, g == len(PATTERN_ORDER) - 1

            def step(i, carry, d=d, nb=nb, first_pattern=first_pattern, last_pattern=last_pattern):
                row_blocks, blocks = _step_blocks(i, U, nb, d)
                kb = [k_ref[r, :].astype(BF16) for r in row_blocks]
                ss = []
                for before, own, which in blocks:
                    kw = jnp.concatenate([kb[before], kb[own]], 0)
                    qs = _stack_heads(q_ref[row_blocks[own], :].astype(BF16))
                    b = bias[which]
                    ss.append(lax.dot_general(qs, kw, NT, preferred_element_type=F32) + jnp.concatenate([b, b], axis=0))
                ms = [jnp.max(s, axis=1, keepdims=True) for s in ss]
                ps = [jnp.exp(s - m) for s, m in zip(ss, ms)]
                ls = [jnp.sum(p, axis=1, keepdims=True) for p in ps]
                vb = [v_ref[r, :].astype(BF16) for r in row_blocks]
                os_ = [jnp.dot(ps[u].astype(BF16), jnp.concatenate([vb[before], vb[own]], 0), preferred_element_type=F32)
                       for u, (before, own, _) in enumerate(blocks)]
                for u, (_, own, _) in enumerate(blocks):
                    o_g, m_g, l_g = _unstack(os_[u]), _unstack_columns(ms[u]), _unstack_columns(ls[u])
                    r = row_blocks[own]
                    if first_pattern:
                        m_new, l_new, acc = m_g, l_g, o_g
                    else:
                        m_old = m_ref[r, :]
                        m_new = jnp.maximum(m_old, m_g)
                        alpha, beta = jnp.exp(m_old - m_new), jnp.exp(m_g - m_new)
                        l_new = l_scr[r, :] * alpha + l_g * beta
                        acc = y_ref[r, :] * alpha + o_g * beta
                    if last_pattern:
                        y_ref[r, :] = acc / l_new
                        m_ref[r, :] = m_new + jnp.log(l_new)
                    else:
                        y_ref[r, :] = acc
                        m_ref[r, :] = m_new
                        l_scr[r, :] = l_new
                return carry

            lax.fori_loop(0, d * nb // U, step, 0)

    col = pl.BlockSpec((S, 2 * HEAD), lambda j: (0, j))
    return pl.pallas_call(
        body, name="attn_fwd", grid=(q.shape[1] // (2 * HEAD),),
        in_specs=[col, col, col], out_specs=[col, col],
        out_shape=[jax.ShapeDtypeStruct(q.shape, F32)] * 2,
        scratch_shapes=[pltpu.VMEM((S, 2 * HEAD), F32), pltpu.VMEM((2, N_BACK, 2 * N_BACK), F32)],
        compiler_params=_params("parallel"),
    )(q, k, v)


def _attn_bwd(q, k, v, dy, lse, delta, after):
    S = q.shape[0]
    U = BWD_BLOCKS_PER_STEP

    def body(q_ref, k_ref, v_ref, dy_ref, lse_ref, delta_ref, after_ref, dq_ref, dk_ref, dv_ref, bias):
        _fill_band_bias(bias)
        nb_first = S // PATTERN_ORDER[0] // N_BACK
        first_writes_all = min(U, nb_first) == nb_first
        if not first_writes_all:
            dk_ref[...] = jnp.zeros_like(dk_ref)
            dv_ref[...] = jnp.zeros_like(dv_ref)
        for g, d in enumerate(PATTERN_ORDER):
            nb = S // d // N_BACK

            def step(i, carry, d=d, nb=nb, g=g):
                row_blocks, blocks = _step_blocks(i, U, nb, d)
                kb = [k_ref[r, :].astype(BF16) for r in row_blocks]
                vb = [v_ref[r, :].astype(BF16) for r in row_blocks]
                kws = [jnp.concatenate([kb[before], kb[own]], 0) for before, own, _ in blocks]
                vws = [jnp.concatenate([vb[before], vb[own]], 0) for before, own, _ in blocks]
                qss = [_stack_heads(q_ref[row_blocks[own], :].astype(BF16)) for _, own, _ in blocks]
                doss = [_stack_heads(dy_ref[row_blocks[own], :].astype(BF16)) for _, own, _ in blocks]
                ss = [lax.dot_general(qss[u], kws[u], NT, preferred_element_type=F32) for u in range(U)]
                dps = [lax.dot_general(doss[u], vws[u], NT, preferred_element_type=F32) for u in range(U)]
                pbs, dss = [], []
                for u, (_, own, which) in enumerate(blocks):
                    lse_c = _head_columns(lse_ref[row_blocks[own], :])
                    delta_c = _head_columns(delta_ref[row_blocks[own], :])
                    p_parts, ds_parts = [], []
                    for r0 in range(0, 2 * N_BACK, BWD_CHUNK):
                        r = slice(r0, r0 + BWD_CHUNK)
                        mask = bias[which, r0 % N_BACK:r0 % N_BACK + BWD_CHUNK, :]
                        p_r = jnp.exp(ss[u][r] + mask - lse_c[r])
                        p_parts.append(p_r.astype(BF16))
                        ds_parts.append((p_r * (dps[u][r] - delta_c[r])).astype(BF16))
                    pbs.append(jnp.concatenate(p_parts, axis=0))
                    dss.append(jnp.concatenate(ds_parts, axis=0))
                dqs = [jnp.dot(dss[u], kws[u], preferred_element_type=F32) for u in range(U)]
                dkws = [lax.dot_general(dss[u], qss[u], TN, preferred_element_type=F32) for u in range(U)]
                dvws = [lax.dot_general(pbs[u], doss[u], TN, preferred_element_type=F32) for u in range(U)]
                dk_parts, dv_parts = [None] * len(row_blocks), [None] * len(row_blocks)
                for u, (before, own, _) in enumerate(blocks):
                    dq = _unstack(dqs[u])
                    if g == 0:
                        dq_ref[row_blocks[own], :] = dq
                    else:
                        dq_ref[row_blocks[own], :] += dq
                    for idx, dkp, dvp in ((before, dkws[u][:N_BACK], dvws[u][:N_BACK]),
                                          (own, dkws[u][N_BACK:], dvws[u][N_BACK:])):
                        dk_parts[idx] = dkp if dk_parts[idx] is None else dk_parts[idx] + dkp
                        dv_parts[idx] = dvp if dv_parts[idx] is None else dv_parts[idx] + dvp
                for idx, r in enumerate(row_blocks):
                    if g == 0 and first_writes_all:
                        dk_ref[r, :] = dk_parts[idx]
                        dv_ref[r, :] = dv_parts[idx]
                    else:
                        dk_ref[r, :] += dk_parts[idx]
                        dv_ref[r, :] += dv_parts[idx]
                return carry

            lax.fori_loop(0, d * nb // U, step, 0)

    col = pl.BlockSpec((S, 2 * HEAD), lambda j: (0, j))
    return pl.pallas_call(
        body, name="attn_bwd", grid=(q.shape[1] // (2 * HEAD),),
        in_specs=[col] * 6 + [pl.BlockSpec(memory_space=pl.ANY)], out_specs=[col] * 3,
        out_shape=[jax.ShapeDtypeStruct(q.shape, F32)] * 3,
        scratch_shapes=[pltpu.VMEM((2, N_BACK, 2 * N_BACK), F32)],
        compiler_params=_params("parallel"),
    )(q, k, v, dy, lse, delta, after)


def _shift_down(z, before, k):
    row = lax.broadcasted_iota(jnp.int32, z.shape, 0)
    out = pltpu.roll(z, k, 0)
    for i in range(k):
        out = jnp.where(row == i, before[8 - k + i:8 - k + i + 1, :], out)
    return out


def _shift_up(z, after, k):
    rows = z.shape[0]
    row = lax.broadcasted_iota(jnp.int32, z.shape, 0)
    out = pltpu.roll(z, rows - k, 0)
    for i in range(k):
        out = jnp.where(row == rows - k + i, after[i:i + 1, :], out)
    return out


def _conv_fwd(bcu, before, is_first, w):
    b, c, u = bcu[:, 0:CONV_W], bcu[:, CONV_W:2 * CONV_W], bcu[:, 2 * CONV_W:3 * CONV_W]
    z = c * u
    zb = jnp.where(is_first, 0.0, before[:, CONV_W:2 * CONV_W] * before[:, 2 * CONV_W:3 * CONV_W])
    z1, z2 = _shift_down(z, zb, 1), _shift_down(z, zb, 2)
    cv = w[0:1, :] * z2 + w[1:2, :] * z1 + w[2:3, :] * z
    return b, c, u, z, z1, z2, cv


def _halo_before(tm, width):
    return pl.BlockSpec((8, width), lambda i: (jnp.maximum(i * (tm // 8) - 1, 0), 0))


def _mix_fwd(ya, bcu, qx, mkv, conv_w, g_a, g_c, g_x, w_out, g_post, x, tm):
    S = x.shape[0]

    def body(ya_ref, bcu_ref, before_ref, qx_ref, mkv_ref, cw_ref, ga_ref, gc_ref, gx_ref,
             wo_ref, gp_ref, x_ref, yx_ref, ycat_ref, y2_ref, x1_ref):
        ya = ya_ref[...]
        b, _, _, _, _, _, cv = _conv_fwd(bcu_ref[...], before_ref[...], pl.program_id(0) == 0, cw_ref[...])
        yc = b * cv

        qxb, mkvb = qx_ref[...], mkv_ref[...]
        heads = [slice(HEAD * hd, HEAD * (hd + 1)) for hd in range(XATTN_W // HEAD)]
        ss = [lax.dot_general(qxb[:, sl], mkvb[:, sl], NT, preferred_element_type=F32) * SCALE for sl in heads]
        ms = [jnp.max(s, axis=1, keepdims=True) for s in ss]
        ps = [jnp.exp(s - m) for s, m in zip(ss, ms)]
        ls = [jnp.sum(p, axis=1, keepdims=True) for p in ps]
        os_ = [jnp.dot(p.astype(BF16), mkvb[:, XATTN_W + sl.start:XATTN_W + sl.stop], preferred_element_type=F32)
               for p, sl in zip(ps, heads)]
        for sl, o, l in zip(heads, os_, ls):
            yx_ref[:, sl] = o / l
        yx = yx_ref[...]

        ycat_ref[:, 0:ATTN_W] = (_rms_hat(ya)[0] * ga_ref[...]).astype(BF16)
        ycat_ref[:, ATTN_W:ATTN_W + CONV_W] = (_rms_hat(yc)[0] * gc_ref[...]).astype(BF16)
        ycat_ref[:, ATTN_W + CONV_W:D_MODEL] = (_rms_hat(yx)[0] * gx_ref[...]).astype(BF16)
        y2 = jnp.dot(ycat_ref[...], wo_ref[...], preferred_element_type=F32)
        y2_ref[...] = y2
        x1_ref[...] = x_ref[...] + _rms_hat(y2)[0] * gp_ref[...]

    n_mem = mkv.shape[0]
    return pl.pallas_call(
        body, name="mix_fwd", grid=(S // tm,),
        in_specs=[_rows(tm, ATTN_W), _rows(tm, 3 * CONV_W), _halo_before(tm, 3 * CONV_W), _rows(tm, XATTN_W),
                  _resident((n_mem, 2 * XATTN_W)), _resident((3, CONV_W)), _resident((1, ATTN_W)),
                  _resident((1, CONV_W)), _resident((1, XATTN_W)), _resident((D_MODEL, D_MODEL)),
                  _resident((1, D_MODEL)), _rows(tm, D_MODEL)],
        out_specs=[_rows(tm, XATTN_W), _rows(tm, D_MODEL), _rows(tm, D_MODEL), _rows(tm, D_MODEL)],
        out_shape=[jax.ShapeDtypeStruct((S, XATTN_W), F32), jax.ShapeDtypeStruct((S, D_MODEL), BF16),
                   jax.ShapeDtypeStruct((S, D_MODEL), F32), jax.ShapeDtypeStruct((S, D_MODEL), F32)],
        compiler_params=_params("parallel"),
    )(ya, bcu, bcu, qx, mkv, conv_w, g_a, g_c, g_x, w_out, g_post, x)


def _mlp_fwd_bwd(x1, target, g_pre, g_post, w_up, w_down, tm):
    S = x1.shape[0]
    n_ff = D_FF // SHARD_FF

    def body(x1_ref, t_ref, gpre_ref, gpost_ref, wup_ref, wdn_ref,
             h2_ref, f_ref, du_ref, df2_ref, dx1_ref, dgpre_ref, dgpost_ref, loss_ref, u_scr):
        @pl.when(pl.program_id(0) == 0)
        def _():
            dgpre_ref[...] = jnp.zeros_like(dgpre_ref)
            dgpost_ref[...] = jnp.zeros_like(dgpost_ref)
            loss_ref[...] = jnp.zeros_like(loss_ref)

        x1 = x1_ref[...]
        x1hat, r1 = _rms_hat(x1)
        h2 = (x1hat * gpre_ref[...]).astype(BF16)
        h2_ref[...] = h2
        f2 = jnp.zeros((tm, D_MODEL), F32)
        for j in range(n_ff):
            cols = slice(SHARD_FF * j, SHARD_FF * (j + 1))
            u = jnp.maximum(jnp.dot(h2, wup_ref[j], preferred_element_type=F32), 0.0)
            u_scr[:, cols] = u
            f = (u * u).astype(BF16)
            f_ref[:, cols] = f
            f2 = f2 + jnp.dot(f, wdn_ref[cols, :], preferred_element_type=F32)
        f2hat, r2 = _rms_hat(f2)
        err = x1 + f2hat * gpost_ref[...] - t_ref[...]
        loss_ref[...] += 0.5 * jnp.sum(jnp.mean(err * err, axis=-1, keepdims=True), axis=0, keepdims=True)
        dx2 = err * (1.0 / D_MODEL)
        dgpost_ref[...] += jnp.sum(dx2 * f2hat, axis=0, keepdims=True)
        df2 = _rms_bwd(f2hat, r2, gpost_ref[...], dx2).astype(BF16)
        df2_ref[...] = df2
        dh2 = jnp.zeros((tm, D_MODEL), F32)
        for j in range(n_ff):
            cols = slice(SHARD_FF * j, SHARD_FF * (j + 1))
            df = lax.dot_general(df2, wdn_ref[cols, :], NT, preferred_element_type=F32)
            du = (2.0 * u_scr[:, cols] * df).astype(BF16)
            du_ref[:, cols] = du
            dh2 = dh2 + lax.dot_general(du, wup_ref[j], NT, preferred_element_type=F32)
        dgpre_ref[...] += jnp.sum(dh2 * x1hat, axis=0, keepdims=True)
        dx1_ref[...] = dx2 + _rms_bwd(x1hat, r1, gpre_ref[...], dh2)

    acc = pl.BlockSpec((1, D_MODEL), lambda i: (0, 0))
    return pl.pallas_call(
        body, name="mlp_fwd_bwd", grid=(S // tm,),
        in_specs=[_rows(tm, D_MODEL), _rows(tm, D_MODEL), _resident((1, D_MODEL)), _resident((1, D_MODEL)),
                  _resident((n_ff, D_MODEL, SHARD_FF)), _resident((D_FF, D_MODEL))],
        out_specs=[_rows(tm, D_MODEL), _rows(tm, D_FF), _rows(tm, D_FF), _rows(tm, D_MODEL), _rows(tm, D_MODEL),
                   acc, acc, pl.BlockSpec((1, 1), lambda i: (0, 0))],
        out_shape=[jax.ShapeDtypeStruct((S, D_MODEL), BF16), jax.ShapeDtypeStruct((S, D_FF), BF16),
                   jax.ShapeDtypeStruct((S, D_FF), BF16), jax.ShapeDtypeStruct((S, D_MODEL), BF16),
                   jax.ShapeDtypeStruct((S, D_MODEL), F32), jax.ShapeDtypeStruct((1, D_MODEL), F32),
                   jax.ShapeDtypeStruct((1, D_MODEL), F32), jax.ShapeDtypeStruct((1, 1), F32)],
        scratch_shapes=[pltpu.VMEM((tm, D_FF), F32)],
        compiler_params=_params("arbitrary"),
    )(x1, target, g_pre, g_post, w_up, w_down)


def _weight_grad(name, a, b, rows_sharded, after):
    S, K = a.shape
    N = b.shape[1]
    if rows_sharded:
        tk, tn = K // N_CHIPS, N
        a_spec = pl.BlockSpec((S, tk), lambda j: (0, j))
        b_spec = pl.BlockSpec((S, tn), lambda j: (0, 0), pipeline_mode=pl.Buffered(1))
    else:
        tk, tn = K, N // N_CHIPS
        a_spec = pl.BlockSpec((S, tk), lambda j: (0, 0), pipeline_mode=pl.Buffered(1))
        b_spec = pl.BlockSpec((S, tn), lambda j: (0, j))
    half = tk // 2

    def body(a_ref, b_ref, after_ref, o_ref):
        res = lax.dot_general(a_ref[...], b_ref[...], TN, preferred_element_type=F32)
        o_ref[0, 0] = res[:half]
        o_ref[1, 0] = res[half:]

    return pl.pallas_call(
        body, name=name, grid=(N_CHIPS,), in_specs=[a_spec, b_spec, pl.BlockSpec(memory_space=pl.ANY)],
        out_specs=pl.BlockSpec((2, 1, half, tn), lambda j: (0, j, 0, 0)),
        out_shape=jax.ShapeDtypeStruct((2, N_CHIPS, half, tn), F32),
        compiler_params=_params("parallel"),
    )(a, b, after)


def _mixer_bwd(dx1, y2, ycat, ya, yx, bcu, qx, mkv, conv_w, g_a, g_c, g_x, w_out, g_post, after, tm):
    S = dx1.shape[0]
    n_mem = mkv.shape[0]
    n_tiles = S // tm
    half = D_MODEL // N_CHIPS // 2

    def body(dx1_ref, y2_ref, ycat_ref, ya_ref, yx_ref, bcu_ref, before_ref, qx_ref, mkv_ref, cw_ref, ga_ref, gc_ref,
             gx_ref, wo_ref, gp_ref, after_ref, gwo_ref, dya_ref, delta_ref, tail_ref, dmkv_ref, dcw_ref, dgp_ref,
             dga_ref, dgc_ref, dgx_ref, carry):
        step = pl.program_id(0)
        first_tile = step == n_tiles - 1

        @pl.when(step == 0)
        def _():
            for ref in (gwo_ref, dmkv_ref, dcw_ref, dgp_ref, dga_ref, dgc_ref, dgx_ref, carry):
                ref[...] = jnp.zeros_like(ref)

        dx1 = dx1_ref[...]
        y2hat, r2 = _rms_hat(y2_ref[...])
        dgp_ref[...] += jnp.sum(dx1 * y2hat, axis=0, keepdims=True)
        dy2 = _rms_bwd(y2hat, r2, gp_ref[...], dx1).astype(BF16)
        gwo = lax.dot_general(ycat_ref[...], dy2, TN, preferred_element_type=F32)
        for k in range(2 * N_CHIPS):
            gwo_ref[k % 2, k // 2] += gwo[half * k:half * (k + 1)]
        dycat = lax.dot_general(dy2, wo_ref[...], NT, preferred_element_type=F32)

        d_na = dycat[:, 0:ATTN_W]
        ya = ya_ref[...]
        yahat, ra = _rms_hat(ya)
        dga_ref[...] += jnp.sum(d_na * yahat, axis=0, keepdims=True)
        dya = _rms_bwd(yahat, ra, ga_ref[...], d_na)
        dya_ref[...] = dya
        prod = dya * ya
        hi = prod.astype(BF16)
        lo = (prod - hi.astype(F32)).astype(BF16)
        head_of = lambda axis: lax.shift_right_logical(lax.broadcasted_iota(jnp.int32, (ATTN_W, ATTN_W), axis),
                                                       HEAD.bit_length() - 1)
        ones = jnp.where(head_of(0) == head_of(1), 1.0, 0.0).astype(BF16)
        delta_ref[...] = jnp.dot(hi, ones, preferred_element_type=F32) + jnp.dot(lo, ones, preferred_element_type=F32)

        w = cw_ref[...]
        b, c, u, z, z1, z2, cv = _conv_fwd(bcu_ref[...], before_ref[...], first_tile, w)
        d_nc = dycat[:, ATTN_W:ATTN_W + CONV_W]
        ychat, rc = _rms_hat(b * cv)
        dgc_ref[...] += jnp.sum(d_nc * ychat, axis=0, keepdims=True)
        dyc = _rms_bwd(ychat, rc, gc_ref[...], d_nc)
        dcv = dyc * b
        behind = carry[...]
        dz = w[2:3, :] * dcv + w[1:2, :] * _shift_up(dcv, behind, 1) + w[0:1, :] * _shift_up(dcv, behind, 2)
        carry[...] = dcv[0:8, :]
        dcw_ref[0:1, :] += jnp.sum(dcv * z2, axis=0, keepdims=True)
        dcw_ref[1:2, :] += jnp.sum(dcv * z1, axis=0, keepdims=True)
        dcw_ref[2:3, :] += jnp.sum(dcv * z, axis=0, keepdims=True)
        tail_ref[:, 0:CONV_W] = (dyc * cv).astype(BF16)
        tail_ref[:, CONV_W:2 * CONV_W] = (dz * u).astype(BF16)
        tail_ref[:, 2 * CONV_W:3 * CONV_W] = (dz * c).astype(BF16)

        d_nx = dycat[:, ATTN_W + CONV_W:D_MODEL]
        yxhat, rx = _rms_hat(yx_ref[...])
        dgx_ref[...] += jnp.sum(d_nx * yxhat, axis=0, keepdims=True)
        dyx = _rms_bwd(yxhat, rx, gx_ref[...], d_nx)
        qxb, mkvb = qx_ref[...], mkv_ref[...]
        heads = [slice(HEAD * hd, HEAD * (hd + 1)) for hd in range(XATTN_W // HEAD)]
        values = [slice(XATTN_W + sl.start, XATTN_W + sl.stop) for sl in heads]
        ss = [lax.dot_general(qxb[:, sl], mkvb[:, sl], NT, preferred_element_type=F32) * SCALE for sl in heads]
        es = [jnp.exp(s - jnp.max(s, axis=1, keepdims=True)) for s in ss]
        ps = [e / jnp.sum(e, axis=1, keepdims=True) for e in es]
        dobs = [dyx[:, sl].astype(BF16) for sl in heads]
        dps = [lax.dot_general(dob, mkvb[:, vsl], NT, preferred_element_type=F32) for dob, vsl in zip(dobs, values)]
        dss = [(p * (dp - jnp.sum(p * dp, axis=1, keepdims=True)) * SCALE).astype(BF16) for p, dp in zip(ps, dps)]
        for sl, vsl, p, dob, ds in zip(heads, values, ps, dobs, dss):
            tail_ref[:, 3 * CONV_W + sl.start:3 * CONV_W + sl.stop] = jnp.dot(
                ds, mkvb[:, sl], preferred_element_type=F32).astype(BF16)
            dmkv_ref[:, sl] += lax.dot_general(ds, qxb[:, sl], TN, preferred_element_type=F32)
            dmkv_ref[:, vsl] += lax.dot_general(p.astype(BF16), dob, TN, preferred_element_type=F32)

    rows = lambda width: pl.BlockSpec((tm, width), lambda i: (n_tiles - 1 - i, 0))
    before = pl.BlockSpec((8, 3 * CONV_W), lambda i: (jnp.maximum((n_tiles - 1 - i) * (tm // 8) - 1, 0), 0))
    acc = lambda r, w: pl.BlockSpec((r, w), lambda i: (0, 0))
    return pl.pallas_call(
        body, name="mixer_bwd", grid=(n_tiles,),
        in_specs=[rows(D_MODEL), rows(D_MODEL), rows(D_MODEL), rows(ATTN_W), rows(XATTN_W), rows(3 * CONV_W), before,
                  rows(XATTN_W), _resident((n_mem, 2 * XATTN_W)), _resident((3, CONV_W)), _resident((1, ATTN_W)),
                  _resident((1, CONV_W)), _resident((1, XATTN_W)), _resident((D_MODEL, D_MODEL)),
                  _resident((1, D_MODEL)), pl.BlockSpec(memory_space=pl.ANY)],
        out_specs=[pl.BlockSpec((2, N_CHIPS, half, D_MODEL), lambda i: (0, 0, 0, 0)), rows(ATTN_W), rows(ATTN_W),
                   rows(3 * CONV_W + XATTN_W), acc(n_mem, 2 * XATTN_W),
                   acc(3, CONV_W), acc(1, D_MODEL), acc(1, ATTN_W), acc(1, CONV_W), acc(1, XATTN_W)],
        out_shape=[jax.ShapeDtypeStruct((2, N_CHIPS, half, D_MODEL), F32), jax.ShapeDtypeStruct((S, ATTN_W), F32),
                   jax.ShapeDtypeStruct((S, ATTN_W), F32), jax.ShapeDtypeStruct((S, 3 * CONV_W + XATTN_W), BF16),
                   jax.ShapeDtypeStruct((n_mem, 2 * XATTN_W), F32), jax.ShapeDtypeStruct((3, CONV_W), F32),
                   jax.ShapeDtypeStruct((1, D_MODEL), F32), jax.ShapeDtypeStruct((1, ATTN_W), F32),
                   jax.ShapeDtypeStruct((1, CONV_W), F32), jax.ShapeDtypeStruct((1, XATTN_W), F32)],
        scratch_shapes=[pltpu.VMEM((8, CONV_W), F32)],
        compiler_params=_params("arbitrary"),
    )(dx1, y2, ycat, ya, yx, bcu, bcu, qx, mkv, conv_w, g_a, g_c, g_x, w_out, g_post, after)


def _memkv_bwd(mem, g_mem, w_kv, dmkv):
    n_mem = mem.shape[0]
    half = D_MODEL // N_CHIPS // 2

    def body(mem_ref, g_ref, w_ref, d_ref, dw_ref, dg_ref):
        mhat, _ = _rms_hat(mem_ref[...])
        mn = (mhat * g_ref[...]).astype(BF16)
        d = d_ref[...].astype(BF16)
        for k in range(2 * N_CHIPS):
            dw_ref[k % 2, k // 2] = lax.dot_general(mn[:, half * k:half * (k + 1)], d, TN, preferred_element_type=F32)
        dmn = lax.dot_general(d, w_ref[...], NT, preferred_element_type=F32)
        dg_ref[...] = jnp.sum(dmn * mhat, axis=0, keepdims=True)

    return pl.pallas_call(
        body, name="memkv_bwd",
        out_shape=[jax.ShapeDtypeStruct((2, N_CHIPS, half, 2 * XATTN_W), F32), jax.ShapeDtypeStruct((1, D_MODEL), F32)],
        compiler_params=pltpu.CompilerParams(vmem_limit_bytes=VMEM_LIMIT_V7X),
    )(mem, g_mem, w_kv, dmkv)


def _sum_of_partials(own_ref, sibling_ref, other_refs):
    acc = own_ref[0, 0] + sibling_ref[0]
    for ref in other_refs:
        acc = acc + ref[0].astype(F32)
    return acc


def _in_proj_bwd(dqkv, tail, cos, sin, w_in, x, h, g, dx1, after, tm, sums=None):
    S = x.shape[0]
    step_w = 2 * 256
    half = D_MODEL // 2
    n_steps = S // tm
    sum_grads, sum_sibling, sum_others, place = sums if sums is not None else ([], [], [], jnp.zeros((2,), jnp.int32))
    k = len(sum_grads)

    def body(place_ref, dq_ref, dk_ref, dv_ref, tail_ref, cos_ref, sin_ref, w_hbm, x_ref, h_ref, g_ref, dx1_ref,
             after_ref, *refs):
        sum_refs, (dx_ref, gw_ref, dg_ref), sum_out_refs = refs[:5 * k], refs[5 * k:5 * k + 3], refs[5 * k + 3:6 * k + 3]
        dproj_ref, w_full, sems = refs[6 * k + 3:]
        for a in range(k):
            sum_out_refs[a][0] = _sum_of_partials(sum_refs[a], sum_refs[k + a], sum_refs[2 * k + 3 * a:2 * k + 3 * a + 3])
        _side_by_side(w_hbm, w_full, sems)

        @pl.when(pl.program_id(0) == 0)
        def _():
            dg_ref[...] = jnp.zeros_like(dg_ref)
            gw_ref[...] = jnp.zeros_like(gw_ref)

        halves = [slice(0, tm // 2), slice(tm // 2, tm)]
        for rows in halves:
            c, s = cos_ref[rows, :], sin_ref[rows, :]
            for j in range(ATTN_W // 128):
                cols = slice(128 * j, 128 * (j + 1))
                dproj_ref[rows, cols] = _rope128(dq_ref[rows, cols] * SCALE, c, s, True).astype(BF16)
                dproj_ref[rows, ATTN_W + 128 * j:ATTN_W + 128 * (j + 1)] = _rope128(dk_ref[rows, cols], c, s, True).astype(BF16)
            dproj_ref[rows, 2 * ATTN_W:3 * ATTN_W] = dv_ref[rows, :].astype(BF16)
            dproj_ref[rows, 3 * ATTN_W:PROJ_W] = tail_ref[rows, :]
        dhs = [lax.dot_general(dproj_ref[rows, :], w_full[...], NT, preferred_element_type=F32) for rows in halves]
        for rows, dh in zip(halves, dhs):
            xhat, r = _rms_hat(x_ref[rows, :])
            dg_ref[...] += jnp.sum(dh * xhat, axis=0, keepdims=True)
            dx_ref[rows, :] = dx1_ref[rows, :] + _rms_bwd(xhat, r, g_ref[...], dh)
        hb = h_ref[...]
        for step in range(PROJ_W // step_w):
            res = lax.dot_general(hb, dproj_ref[:, step * step_w:(step + 1) * step_w], TN, preferred_element_type=F32)
            lo = step * step_w
            while lo < (step + 1) * step_w:
                chip = lo // SHARD_IN
                hi = min((step + 1) * step_w, (chip + 1) * SHARD_IN)
                for hh in range(2):
                    gw_ref[hh, chip, :, lo - chip * SHARD_IN:hi - chip * SHARD_IN] += (
                        res[half * hh:half * (hh + 1), lo - step * step_w:hi - step * step_w])
                lo = hi

    whole = lambda shape: pl.BlockSpec(shape, lambda i, p: (0,) * len(shape))
    slab = lambda t: (1, t.shape[-2] // n_steps, t.shape[-1])
    sum_specs = ([pl.BlockSpec((1,) + slab(t), lambda i, p: (p[0], p[1], i, 0)) for t in sum_grads]
                 + [pl.BlockSpec(slab(t), lambda i, p: (p[1], i, 0)) for t in sum_grads]
                 + [pl.BlockSpec(slab(t), lambda i, p, j=j: (j, i, 0)) for t in sum_grads for j in range(3)])
    results = pl.pallas_call(
        body, name="in_proj_bwd",
        out_shape=[jax.ShapeDtypeStruct((S, D_MODEL), F32), jax.ShapeDtypeStruct((2, N_CHIPS, half, SHARD_IN), F32),
                   jax.ShapeDtypeStruct((1, D_MODEL), F32)]
        + [jax.ShapeDtypeStruct((2,) + t.shape[2:], F32) for t in sum_grads],
        grid_spec=pltpu.PrefetchScalarGridSpec(
            num_scalar_prefetch=1, grid=(n_steps,),
            in_specs=[_rows(tm, ATTN_W)] * 3 + [_rows(tm, PROJ_W - 3 * ATTN_W), _rows(tm, 128), _rows(tm, 128),
                      pl.BlockSpec(memory_space=pl.ANY), _rows(tm, D_MODEL), _rows(tm, D_MODEL),
                      _resident((1, D_MODEL)), _rows(tm, D_MODEL), pl.BlockSpec(memory_space=pl.ANY)] + sum_specs,
            out_specs=[_rows(tm, D_MODEL), whole((2, N_CHIPS, half, SHARD_IN)), whole((1, D_MODEL))]
            + [pl.BlockSpec(slab(t), lambda i, p: (p[0], i, 0)) for t in sum_grads],
            scratch_shapes=[pltpu.VMEM((tm, PROJ_W), BF16), pltpu.VMEM((D_MODEL, PROJ_W), BF16),
                            pltpu.SemaphoreType.DMA((N_CHIPS,))]),
        compiler_params=_params("arbitrary"),
    )(place, *dqkv, tail, cos, sin, w_in, x, h, g, dx1, after, *sum_grads, *sum_sibling,
      *[o for o in sum_others for _ in range(3)])
    return [*results[:3], list(results[3:])]


def _row_tile(rows):
    return ROW_TILE if rows % ROW_TILE == 0 else rows


def _chip_sums_bf16(name, grads, from_sibling, place):
    k = len(grads)
    _, n, rows, _ = grads[0].shape
    tr = _row_tile(rows)

    def body(place_ref, *refs):
        for g_ref, b_ref, o_ref in zip(refs[:k], refs[k:2 * k], refs[2 * k:]):
            o_ref[...] = (g_ref[0] + b_ref[...]).astype(BF16)

    mine = lambda g: pl.BlockSpec((1, 1, tr, g.shape[3]), lambda s, i, p: (p[0], s, i, 0))
    slab = lambda g: pl.BlockSpec((1, tr, g.shape[3]), lambda s, i, p: (s, i, 0))
    return pl.pallas_call(
        body, name=name, out_shape=[jax.ShapeDtypeStruct(g.shape[1:], BF16) for g in grads],
        grid_spec=pltpu.PrefetchScalarGridSpec(
            num_scalar_prefetch=1, grid=(n, rows // tr),
            in_specs=[mine(g) for g in grads] + [slab(g) for g in grads], out_specs=[slab(g) for g in grads]),
        compiler_params=_params("parallel", "parallel"),
    )(place, *grads, *from_sibling)


def _final_sums(name, grads, from_sibling, others, place):
    k = len(grads)
    rows = grads[0].shape[2]
    tr = _row_tile(rows)

    def body(place_ref, *refs):
        for a in range(k):
            refs[5 * k + a][0] = _sum_of_partials(refs[a], refs[k + a], refs[2 * k + 3 * a:2 * k + 3 * a + 3])

    own = lambda g: pl.BlockSpec((1, 1, tr, g.shape[3]), lambda i, p: (p[0], p[1], i, 0))
    sib = lambda g: pl.BlockSpec((1, tr, g.shape[3]), lambda i, p: (p[1], i, 0))
    other = lambda g, j: pl.BlockSpec((1, tr, g.shape[3]), lambda i, p: (j, i, 0))
    return pl.pallas_call(
        body, name=name, out_shape=[jax.ShapeDtypeStruct((2,) + g.shape[2:], F32) for g in grads],
        grid_spec=pltpu.PrefetchScalarGridSpec(
            num_scalar_prefetch=1, grid=(rows // tr,),
            in_specs=[own(g) for g in grads] + [sib(g) for g in grads] + [other(g, j) for g in grads for j in range(3)],
            out_specs=[pl.BlockSpec((1, tr, g.shape[3]), lambda i, p: (p[0], i, 0)) for g in grads]),
        compiler_params=_params("parallel"),
    )(place, *grads, *from_sibling, *[o for o in others for _ in range(3)])


def _adamw_update(w, g, m, v):
    m = ADAM_B1 * m + (1.0 - ADAM_B1) * g
    v = ADAM_B2 * v + (1.0 - ADAM_B2) * (g * g)
    m_hat = m * (1.0 / (1.0 - ADAM_B1 ** ADAM_STEP))
    v_hat = v * (1.0 / (1.0 - ADAM_B2 ** ADAM_STEP))
    return -ADAM_LR * (m_hat / (jnp.sqrt(v_hat) + ADAM_EPS) + ADAM_WD * w), m, v


def _adamw(name, params, after):
    k = len(params)
    rows = params[0][0].shape[0]
    tr = ADAMW_ROW_TILE if rows % ADAMW_ROW_TILE == 0 else rows

    def body(*refs):
        ins, outs = refs[:4 * k], refs[4 * k + 1:]
        for a in range(k):
            w_ref, g_ref, m_ref, v_ref = ins[4 * a:4 * a + 4]
            g = g_ref[...]
            outs[4 * a][...] = g
            outs[4 * a + 1][...], outs[4 * a + 2][...], outs[4 * a + 3][...] = _adamw_update(w_ref[...], g, m_ref[...], v_ref[...])

    spec = lambda w: pl.BlockSpec((tr, w.shape[1]), lambda i: (i, 0))
    out = pl.pallas_call(
        body, name=name, grid=(rows // tr,),
        in_specs=[spec(p[0]) for p in params for _ in range(4)] + [pl.BlockSpec(memory_space=pl.ANY)],
        out_specs=[spec(p[0]) for p in params for _ in range(4)],
        out_shape=[jax.ShapeDtypeStruct(p[0].shape, F32) for p in params for _ in range(4)],
        compiler_params=_params("parallel"),
    )(*[t for p in params for t in p], after)
    return [out[4 * a:4 * a + 4] for a in range(k)]


def _small_update(blocks, chip, gains, gains_m, gains_v, taps, taps_m, taps_v):
    n = len(gains)
    widths = [g.shape[1] for g in gains]
    k, w = taps.shape

    def body(*refs):
        chip_ref, blocks_ref = refs[0], refs[1]
        params = [refs[2 + 3 * i:5 + 3 * i] for i in range(n + 1)]
        outs = [refs[2 + 3 * (n + 1) + 4 * i:2 + 3 * (n + 1) + 4 * (i + 1)] for i in range(n + 1)]
        loss_ref = refs[-1]
        summed = blocks_ref[0]
        for device in range(1, blocks.shape[0]):
            summed = summed + blocks_ref[device]
        for i in range(n):
            g = summed[i:i + 1, 0:widths[i]]
            wr, mr, vr = params[i]
            outs[i][0][...] = g
            outs[i][1][...], outs[i][2][...], outs[i][3][...] = _adamw_update(wr[...], g, mr[...], vr[...])
        g = summed[n:n + k, 0:w]
        for j in range(1, N_CHIPS):
            g = jnp.where(chip_ref[0] == j, summed[n:n + k, w * j:w * (j + 1)], g)
        wr, mr, vr = params[n]
        for out_ref, val in zip(outs[n], (g, *_adamw_update(wr[...], g, mr[...], vr[...]))):
            for j in range(k):
                out_ref[j] = val[j:j + 1, :]
        loss_ref[...] = summed[n + k:n + k + 1, 0:1]

    vmem = pl.BlockSpec(memory_space=pltpu.VMEM)
    operands = [chip, blocks]
    for p in zip(list(gains) + [taps], list(gains_m) + [taps_m], list(gains_v) + [taps_v]):
        operands += list(p)
    shapes = [jax.ShapeDtypeStruct(shape, F32) for shape in [g.shape for g in gains] + [(k, 1, w)] for _ in range(4)]
    out = pl.pallas_call(
        body, name="small_update", out_shape=shapes + [jax.ShapeDtypeStruct((1, 1), F32)],
        in_specs=[pl.BlockSpec(memory_space=pltpu.SMEM)] + [vmem] * (len(operands) - 1),
        out_specs=[vmem] * (len(shapes) + 1),
    )(*operands)
    return [out[4 * i:4 * (i + 1)] for i in range(n + 1)], out[-1]


def _place():
    return lax.axis_index("x"), lax.axis_index("y"), lax.axis_index("c")


def _other_chips(x, y):
    return [(1 - x, y), (x, 1 - y), (1 - x, 1 - y)]


def _allgather_finish(name, shards, landed, pass_on):
    n = len(shards)

    def body(*refs):
        ins, outs, stage = refs[:n], refs[2 * n:3 * n], refs[3 * n:4 * n]
        send_sems, recv_sems, local_sems = refs[4 * n:]
        x, y, c = _place()
        chips = _other_chips(x, y)
        barrier = pltpu.get_barrier_semaphore()
        pl.semaphore_signal(barrier, inc=1, device_id=(x, y, 1 - c), device_id_type=MESH)
        pl.semaphore_wait(barrier, 1)

        def copy(a, k, chip, half):
            place = outs[a].at[2 * chip[0] + chip[1], half]
            return pltpu.make_async_remote_copy(
                src_ref=place, dst_ref=place, send_sem=send_sems.at[3 * a + k], recv_sem=recv_sems.at[3 * a + k],
                device_id=(x, y, 1 - c), device_id_type=MESH)

        load = [pltpu.make_async_copy(ins[a], stage[a], local_sems.at[a]) for a in range(n)]
        local = [pltpu.make_async_copy(stage[a], outs[a].at[2 * x + y], local_sems.at[a]) for a in range(n)]
        for cp in load:
            cp.start()
        passed = [copy(a, k, chip, c) for a in range(n) if pass_on[a] for k, chip in enumerate(chips)]
        for cp in passed:
            cp.start()
        for a in range(n):
            load[a].wait()
            local[a].start()
        for a in range(n):
            if pass_on[a]:
                for k, chip in enumerate(chips):
                    copy(a, k, chip, 1 - c).wait_recv()
        for cp in passed:
            cp.wait_send()
        for cp in local:
            cp.wait()

    any_spec = pl.BlockSpec(memory_space=pl.ANY)
    return pl.pallas_call(
        body, name=name,
        out_shape=[jax.ShapeDtypeStruct((N_CHIPS,) + s.shape, s.dtype) for s in shards],
        in_specs=[any_spec] * (2 * n), out_specs=[any_spec] * n,
        input_output_aliases={n + a: a for a in range(n)},
        scratch_shapes=[pltpu.VMEM(s.shape, s.dtype) for s in shards]
        + [pltpu.SemaphoreType.DMA((3 * n,)), pltpu.SemaphoreType.DMA((3 * n,)), pltpu.SemaphoreType.DMA((n,))],
        compiler_params=pltpu.CompilerParams(vmem_limit_bytes=VMEM_LIMIT_V7X, collective_id=HANDSHAKES["sibling"][0]),
    )(*shards, *landed)


def _plan_first_hop(x, y, c, shards, lands):
    return [(shards[a].at[c], lands[a].at[2 * x + y, c], lands[a].at[2 * chip[0] + chip[1], c], (*chip, c))
            for a in range(len(shards)) for chip in _other_chips(x, y)]


def _plan_pass_on(x, y, c, nothing, lands):
    def place(a, chip, half):
        return lands[a].at[2 * chip[0] + chip[1], half]

    return [(place(a, chip, c), place(a, chip, c), place(a, chip, 1 - c), (x, y, 1 - c))
            for a in range(len(lands)) for chip in _other_chips(x, y)]


def _plan_own_half_to_sibling(x, y, c, nothing, lands):
    return [(lands[a].at[c], lands[a].at[c], lands[a].at[1 - c], (x, y, 1 - c)) for a in range(len(lands))]


def _plan_other_half_to_sibling(x, y, c, grads, lands):
    return [(grads[a].at[1 - c], lands[a], lands[a], (x, y, 1 - c)) for a in range(len(grads))]


def _plan_to_other_chips(x, y, c, partials, lands):
    return [(partials[a].at[2 * chip[0] + chip[1]], lands[a].at[k], lands[a].at[k], (*chip, c))
            for a in range(len(partials)) for k, chip in enumerate(_other_chips(x, y))]


def _plan_to_all(x, y, c, blocks, lands):
    flips = [(fx, fy, fc) for fx in (0, 1) for fy in (0, 1) for fc in (0, 1) if (fx, fy, fc) != (0, 0, 0)]
    peers = [(1 - x if fx else x, 1 - y if fy else y, 1 - c if fc else c) for fx, fy, fc in flips]
    return [(blocks[0], lands[0].at[4 * x + 2 * y + c], lands[0].at[4 * p[0] + 2 * p[1] + p[2]], p) for p in peers]


def _planned_copies(plan, srcs, lands, send_sems, recv_sems):
    x, y, c = _place()

    def pair(k, src, there, here, to):
        make = lambda dst: pltpu.make_async_remote_copy(
            src_ref=src, dst_ref=dst, send_sem=send_sems.at[k], recv_sem=recv_sems.at[k], device_id=to, device_id_type=MESH)
        return make(there), make(here)

    return [pair(k, *entry) for k, entry in enumerate(plan(x, y, c, srcs, lands))]


_HBM_SPEC = pl.BlockSpec(memory_space=pltpu.HBM)
_SEM_SPEC = pl.BlockSpec(memory_space=pltpu.SEMAPHORE)


def _hbm(a):
    return pltpu.with_memory_space_constraint(a, pltpu.HBM)


HANDSHAKES = {
    "sibling": (1, lambda x, y, c: [(x, y, 1 - c)]),
}


def _exchange_start(name, plan, n_copies, srcs, land_shapes, after, lands=None, peers=None):
    if lands is None:
        lands = [lax.empty(s.shape, s.dtype) for s in land_shapes]
    land_shapes = lands
    ns, nl = len(srcs), len(land_shapes)
    n_in = ns + nl + 1
    collective_id, peers_of = HANDSHAKES[peers] if peers else (None, None)

    def body(*refs):
        if peers:
            who = peers_of(*_place())
            barrier = pltpu.get_barrier_semaphore()
            for peer in who:
                pl.semaphore_signal(barrier, inc=1, device_id=peer, device_id_type=MESH)
            pl.semaphore_wait(barrier, len(who))
        for send, _ in _planned_copies(plan, refs[:ns], refs[ns:ns + nl], refs[n_in], refs[n_in + 1]):
            send.start()
        refs[-1][...] = jnp.zeros_like(refs[-1])

    out = pl.pallas_call(
        body, name=name,
        out_shape=(pltpu.SemaphoreType.DMA((n_copies,)), pltpu.SemaphoreType.DMA((n_copies,)),
                   *[pltpu.HBM(s.shape, s.dtype) for s in land_shapes], jax.ShapeDtypeStruct((8, 128), F32)),
        in_specs=[_HBM_SPEC] * (ns + nl) + [pl.BlockSpec(memory_space=pl.ANY)],
        out_specs=(_SEM_SPEC, _SEM_SPEC, *[_HBM_SPEC] * nl, pl.BlockSpec(memory_space=pltpu.VMEM)),
        input_output_aliases={ns + i: 2 + i for i in range(nl)},
        compiler_params=pltpu.CompilerParams(has_side_effects=pltpu.SideEffectType.DATAFLOW_SIDE_EFFECTING,
                                             collective_id=collective_id),
    )(*[_hbm(s) for s in srcs], *[_hbm(l) for l in lands], after)
    return out[0], out[1], list(out[2:2 + nl]), out[-1]


def _exchange_wait(name, plan, srcs, started, after):
    send_sems, recv_sems, lands, _ = started
    ns, nl = len(srcs), len(lands)
    after = list(after) if isinstance(after, (list, tuple)) else [after]

    def body(*refs):
        for send, recv in _planned_copies(plan, refs[:ns], refs[ns:ns + nl], refs[ns + nl], refs[ns + nl + 1]):
            send.wait_send()
            recv.wait_recv()

    return pl.pallas_call(
        body, name=name, out_shape=[pltpu.HBM(l.shape, l.dtype) for l in lands],
        in_specs=[_HBM_SPEC] * (ns + nl) + [_SEM_SPEC, _SEM_SPEC] + [pl.BlockSpec(memory_space=pl.ANY)] * len(after),
        out_specs=[_HBM_SPEC] * nl, input_output_aliases={ns + i: i for i in range(nl)},
        compiler_params=pltpu.CompilerParams(has_side_effects=pltpu.SideEffectType.DATAFLOW_SIDE_EFFECTING),
    )(*[_hbm(s) for s in srcs], *lands, send_sems, recv_sems, *after)


def _like(arrays, lead, dtype=None):
    return [jax.ShapeDtypeStruct(tuple(lead) + a.shape[-2:], dtype or a.dtype) for a in arrays]


class _StepExchanges:
    def __init__(self, mats, conv_w):
        x, y, c = _place()
        self.place = jnp.stack([c, 2 * x + y]).astype(jnp.int32)
        shards = [w.astype(BF16).reshape(2, w.shape[0] // 2, w.shape[1]) for w in mats]
        self._in_shard = shards[:1]
        self._in = _exchange_start("w_in_allgather_start", _plan_first_hop, 3, self._in_shard,
                                   _like(self._in_shard, (N_CHIPS, 2)), shards[0])
        self.zero = self._in[3]
        taps = jnp.pad(conv_w, ((0, 8 - conv_w.shape[0]), (0, 128 - conv_w.shape[1])))
        self._rest_shards = shards[1:] + [jnp.stack([taps, jnp.zeros_like(taps)])]
        self._taps_shape = conv_w.shape
        self._groups = {}

    def w_in(self, after):
        landed = _exchange_wait("w_in_allgather_wait", _plan_first_hop, self._in_shard, self._in,
                                list(after) + self._rest_shards)
        (w_in,) = _allgather_finish("w_in_allgather_finish", self._in_shard, landed, [True])
        self._rest = _exchange_start("rest_allgather_start", _plan_first_hop, 3 * len(self._rest_shards),
                                     self._rest_shards, _like(self._rest_shards, (N_CHIPS, 2)), w_in)
        self.zero = self._rest[3]
        return w_in.reshape(N_CHIPS, 2 * w_in.shape[2], w_in.shape[3])

    def rest_weights(self, after):
        landed = _exchange_wait("rest_allgather_wait", _plan_first_hop, self._rest_shards, self._rest, after)
        kv, out, up, down, taps = _allgather_finish("rest_allgather_finish", self._rest_shards, landed,
                                                    [True, True, False, False, True])
        self._up_down = _exchange_start("up_down_pass_on_start", _plan_pass_on, 6, [], None, self.zero, lands=[up, down],
                                        peers="sibling")
        self.zero = self._up_down[3]
        k, w = self._taps_shape
        taps = taps[:, 0, :k, :w].transpose(1, 0, 2).reshape(k, N_CHIPS * w)
        return [g.reshape(N_CHIPS, 2 * g.shape[2], g.shape[3]) for g in (kv, out)], taps

    def up_down(self, after):
        full = _exchange_wait("up_down_pass_on_wait", _plan_pass_on, [], self._up_down, after)
        return [g.reshape(N_CHIPS, 2 * g.shape[2], g.shape[3]) for g in full]

    def send_grads(self, key, grads):
        grads = list(grads)
        started = _exchange_start(f"{key}_grads_to_sibling_start", _plan_other_half_to_sibling, len(grads), grads,
                                  _like(grads, (N_CHIPS,)), self.zero, peers="sibling")
        self._groups[key] = dict(grads=grads, to_sibling=started)
        self.zero = started[3]

    def grads_at_sibling(self, key, after):
        group = self._groups[key]
        grads = group["grads"]
        group["from_sibling"] = _exchange_wait(f"{key}_grads_to_sibling_wait", _plan_other_half_to_sibling, grads,
                                               group["to_sibling"], after)
        group["partials"] = _chip_sums_bf16(f"{key}_chip_sums", grads, group["from_sibling"], self.place)
        group["to_chips"] = _exchange_start(f"{key}_grads_to_chips_start", _plan_to_other_chips, 3 * len(grads),
                                            group["partials"], _like(group["partials"], (3,)), self.zero)
        self.zero = group["to_chips"][3]

    def final_sum_operands(self, key, after):
        group = self._groups[key]
        from_chips = _exchange_wait(f"{key}_grads_to_chips_wait", _plan_to_other_chips, group["partials"],
                                    group["to_chips"], after)
        return group["grads"], group["from_sibling"], from_chips, self.place

    def grads_summed(self, key, after):
        return _final_sums(f"{key}_final_sums", *self.final_sum_operands(key, after))

    def send_sums(self, key, sums):
        self._groups[key + "_sums"] = _exchange_start(f"{key}_sums_to_sibling_start", _plan_own_half_to_sibling,
                                                      len(sums), [], None, self.zero, lands=list(sums),
                                                      peers="sibling")
        self.zero = self._groups[key + "_sums"][3]

    def whole_sums(self, key, after):
        full = _exchange_wait(f"{key}_sums_to_sibling_wait", _plan_own_half_to_sibling, [], self._groups[key + "_sums"], after)
        return [t.reshape(2 * t.shape[1], t.shape[2]) for t in full]

    def send_small(self, block):
        self._small = block
        self._small_started = _exchange_start("small_grads_start", _plan_to_all, 7, [block],
                                              [jax.ShapeDtypeStruct((8,) + block.shape, block.dtype)], self.zero)
        self.zero = self._small_started[3]

    def small_blocks(self, after):
        x, y, c = _place()
        (landed,) = _exchange_wait("small_grads_wait", _plan_to_all, [self._small], self._small_started, after)
        return lax.dynamic_update_index_in_dim(landed, self._small, 4 * x + 2 * y + c, 0)


def _rope_tables(positions):
    half = HEAD // 2
    inv_freq = jnp.float32(ROPE_THETA) ** (-(jnp.arange(half, dtype=F32) * 2.0 / HEAD))
    ang = positions.astype(F32)[:, None] * inv_freq
    cos, sin = jnp.cos(ang), jnp.sin(ang)
    return jnp.tile(cos, (1, 4)), jnp.tile(jnp.concatenate([-sin, sin], axis=1), (1, 2))


def _local_step(x, mem, positions, target, gains, ex):
    g_pre_mix, g_mem, g_a, g_c, g_x, g_post_mix, g_pre_mlp, g_post_mlp = gains
    tm = ROW_TILE
    cos, sin = _rope_tables(positions)
    h = _pre_norm(x, g_pre_mix, ex.zero, tm)
    w_in = ex.w_in([h, cos, sin])

    q, k, v, bcu, qx = _in_proj_fwd(h, w_in, cos, sin, ex.zero, tm)
    ya, lse = _attn_fwd(q, k, v)
    (w_kv, w_out), conv_w = ex.rest_weights(lse)
    w_kv, w_out = (w.reshape(N_CHIPS * w.shape[1], w.shape[2]) for w in (w_kv, w_out))
    memn, mkv = _memkv_fwd(mem, g_mem, w_kv, ex.zero)
    yx, ycat, y2, x1 = _mix_fwd(ya, bcu, qx, mkv, conv_w, g_a, g_c, g_x, w_out, g_post_mix, x, tm)
    w_up, w_down = ex.up_down(x1)
    w_down = w_down.reshape(N_CHIPS * w_down.shape[1], w_down.shape[2])
    h2, f, du, df2, dx1, dg_pre_mlp, dg_post_mlp, loss = _mlp_fwd_bwd(x1, target, g_pre_mlp, g_post_mlp, w_up, w_down,
                                                                      MLP_ROW_TILE)
    gw_down = _weight_grad("grad_w_down", f, df2, True, ex.zero)
    gw_up = _weight_grad("grad_w_up", h2, du, False, ex.zero)
    ex.send_grads("early", [gw_up, gw_down])

    gw_out, dya, delta, tail, dmkv, g_conv, dg_post_mix, dg_a, dg_c, dg_x = _mixer_bwd(
        dx1, y2, ycat, ya, yx, bcu, qx, mkv, conv_w, g_a, g_c, g_x, w_out, g_post_mix, ex.zero, tm)
    ex.grads_at_sibling("early", dya)
    gw_kv, dg_mem = _memkv_bwd(mem, g_mem, w_kv, dmkv)
    ex.send_grads("mid", [gw_out, gw_kv])
    dqkv = _attn_bwd(q, k, v, dya, lse, delta, ex.zero)
    ex.grads_at_sibling("mid", dqkv[0])
    grad_x, gw_in, dg_pre_mix, early_sums = _in_proj_bwd(dqkv, tail, cos, sin, w_in, x, h, g_pre_mix, dx1, ex.zero, tm,
                                                         ex.final_sum_operands("early", dqkv[0]))
    ex.send_grads("late", [gw_in])
    gain_grads = [dg_pre_mix, dg_mem, dg_a, dg_c, dg_x, dg_post_mix, dg_pre_mlp, dg_post_mlp]
    ex.send_small(_pack_small(gain_grads, g_conv, loss))
    return grad_x, early_sums


def _pack_small(gains, conv, scalar):
    n, k = len(gains), conv.shape[0]

    def body(*refs):
        out_ref = refs[-1]
        out_ref[...] = jnp.zeros_like(out_ref)
        for i, g_ref in enumerate(refs[:n]):
            out_ref[i:i + 1, 0:g_ref.shape[1]] = g_ref[...]
        out_ref[n:n + k, 0:conv.shape[1]] = refs[n][...]
        out_ref[n + k:n + k + 1, 0:1] = refs[n + 1][...]

    return pl.pallas_call(body, name="pack_small", out_shape=jax.ShapeDtypeStruct((SMALL_ROWS, D_MODEL), F32))(
        *gains, conv, scalar)


def kernel(x, mem, positions, g_pre_mix, g_mem, w_in, w_mem_kv, conv_w, g_attn_out, g_conv_out, g_xattn_out, w_out, g_post_mix, g_pre_mlp, w_up, w_down, g_post_mlp, loss_target, m_g_pre_mix, m_g_mem, m_w_in, m_w_mem_kv, m_conv_w, m_g_attn_out, m_g_conv_out, m_g_xattn_out, m_w_out, m_g_post_mix, m_g_pre_mlp, m_w_up, m_w_down, m_g_post_mlp, v_g_pre_mix, v_g_mem, v_w_in, v_w_mem_kv, v_conv_w, v_g_attn_out, v_g_conv_out, v_g_xattn_out, v_w_out, v_g_post_mix, v_g_pre_mlp, v_w_up, v_w_down, v_g_post_mlp):
    chip = 2 * lax.axis_index("x") + lax.axis_index("y")
    gains = [g_pre_mix, g_mem, g_attn_out, g_conv_out, g_xattn_out, g_post_mix, g_pre_mlp, g_post_mlp]
    gains_m = [m_g_pre_mix, m_g_mem, m_g_attn_out, m_g_conv_out, m_g_xattn_out, m_g_post_mix, m_g_pre_mlp, m_g_post_mlp]
    gains_v = [v_g_pre_mix, v_g_mem, v_g_attn_out, v_g_conv_out, v_g_xattn_out, v_g_post_mix, v_g_pre_mlp, v_g_post_mlp]
    mats =[w_in[0], w_mem_kv[0], w_out[0], w_up[0], w_down[0]]
    mats_m = [m_w_in[0], m_w_mem_kv[0], m_w_out[0], m_w_up[0], m_w_down[0]]
    mats_v = [v_w_in[0], v_w_mem_kv[0], v_w_out[0], v_w_up[0], v_w_down[0]]

    ex = _StepExchanges(mats, conv_w[0])
    grad_x, early_sums = _local_step(x[0], mem[0], positions[0], loss_target[0], gains, ex)

    ex.send_sums("four", [*early_sums, *ex.grads_summed("mid", ex.zero)])
    ex.grads_at_sibling("late", ex.zero)
    up_sum, down_sum, out_sum, kv_sum = ex.whole_sums("four", ex.zero)
    params = lambda a, g: (mats[a], g, mats_m[a], mats_v[a])
    new_up, new_down = _adamw("adamw_up_down", [params(3, up_sum), params(4, down_sum)], ex.zero)
    new_out, new_kv = _adamw("adamw_out_kv", [params(2, out_sum), params(1, kv_sum)], ex.zero)

    ex.send_sums("last", ex.grads_summed("late", new_kv[1]))
    small, total = _small_update(ex.small_blocks(ex.zero), chip.reshape(1).astype(jnp.int32), gains, gains_m,
                                 gains_v, conv_w[0], m_conv_w[0], v_conv_w[0])
    (in_sum,) = ex.whole_sums("last", small[0][1])
    (new_in,) = _adamw("adamw_in", [params(0, in_sum)], in_sum)
    mat_new = [new_in, new_kv, new_out, new_up, new_down]

    order = ["g_pre_mix", "g_mem", "w_in", "w_mem_kv", "conv_w", "g_attn_out", "g_conv_out", "g_xattn_out", "w_out",
             "g_post_mix", "g_pre_mlp", "w_up", "w_down", "g_post_mlp"]
    gain_names = ["g_pre_mix", "g_mem", "g_attn_out", "g_conv_out", "g_xattn_out", "g_post_mix", "g_pre_mlp", "g_post_mlp"]
    mat_names = ["w_in", "w_mem_kv", "w_out", "w_up", "w_down"]

    def leaf(kind, name):
        if name in gain_names:
            return small[gain_names.index(name)][kind]
        if name == "conv_w":
            return jnp.swapaxes(small[len(gain_names)][kind], 0, 1)
        return mat_new[mat_names.index(name)][kind][None]

    return (total[0, 0], grad_x[None], *[leaf(kind, name) for kind in range(4) for name in order])
```

```python
import jax
import jax.numpy as jnp
from jax import lax
from jax.experimental import pallas as pl
from jax.experimental.pallas import tpu as pltpu

F32, BF16 = jnp.float32, jnp.bfloat16

D_MODEL = 1024
ATTN_W = 512
CONV_W = 256
XATTN_W = 256
PROJ_W = 3 * ATTN_W + 3 * CONV_W + XATTN_W
D_FF = 4096
HEAD = 64
N_BACK = 128
DILATIONS = (1, 4, 16)
PATTERN_ORDER = DILATIONS[::-1]
ROPE_THETA = 10000.0
EPS = 1e-6
NEG_INF = -1e30
SCALE = HEAD ** -0.5
N_CHIPS = 4
SHARD_IN = PROJ_W // N_CHIPS
SHARD_FF = D_FF // N_CHIPS

ADAM_LR, ADAM_B1, ADAM_B2, ADAM_EPS, ADAM_WD, ADAM_STEP = 0.001, 0.9, 0.999, 1e-08, 0.01, 10

VMEM_LIMIT_V7X = 56 * 1024 * 1024
ROW_TILE = 512
MLP_ROW_TILE = 256
ADAMW_ROW_TILE = 256
SMALL_ROWS = 16

NT = (((1,), (1,)), ((), ()))
TN = (((0,), (0,)), ((), ()))
MESH = pl.DeviceIdType.MESH


def _params(*sem):
    return pltpu.CompilerParams(dimension_semantics=sem, vmem_limit_bytes=VMEM_LIMIT_V7X)


def _resident(shape):
    return pl.BlockSpec(shape, lambda *_: (0,) * len(shape), pipeline_mode=pl.Buffered(1))


def _rows(tm, width):
    return pl.BlockSpec((tm, width), lambda i, *_: (i, 0))


def _rms_hat(x):
    r = lax.rsqrt(jnp.mean(x * x, axis=-1, keepdims=True) + EPS)
    return x * r, r


def _rms_bwd(xhat, r, g, dy):
    gdy = dy * g
    return r * (gdy - xhat * jnp.mean(xhat * gdy, axis=-1, keepdims=True))


def _rope128(t, cos, sin_signed, inverse):
    lane = lax.broadcasted_iota(jnp.int32, t.shape, 1)
    first_half = (lane % HEAD) < (HEAD // 2)
    rot = jnp.where(first_half, pltpu.roll(t, 128 - HEAD // 2, 1), pltpu.roll(t, HEAD // 2, 1))
    return t * cos - rot * sin_signed if inverse else t * cos + rot * sin_signed


def _pre_norm(x, g, after, tm):
    S = x.shape[0]

    def body(x_ref, g_ref, after_ref, h_ref):
        h_ref[...] = (_rms_hat(x_ref[...])[0] * g_ref[...]).astype(BF16)

    return pl.pallas_call(
        body, name="pre_norm", grid=(S // tm,),
        in_specs=[_rows(tm, D_MODEL), _resident((1, D_MODEL)), pl.BlockSpec(memory_space=pl.ANY)],
        out_specs=_rows(tm, D_MODEL), out_shape=jax.ShapeDtypeStruct((S, D_MODEL), BF16),
        compiler_params=_params("parallel"),
    )(x, g, after)


def _side_by_side(w_hbm, w_full, sems):
    width = w_hbm.shape[2]

    @pl.when(pl.program_id(0) == 0)
    def _():
        copies = [pltpu.make_async_copy(w_hbm.at[j], w_full.at[:, pl.ds(width * j, width)], sems.at[j])
                  for j in range(N_CHIPS)]
        for cp in copies:
            cp.start()
        for cp in copies:
            cp.wait()


def _in_proj_fwd(h, w_in, cos, sin, after, tm):
    S = h.shape[0]

    def body(h_ref, w_hbm, cos_ref, sin_ref, after_ref, q_ref, k_ref, v_ref, bcu_ref, qx_ref, proj, w_full, sems):
        _side_by_side(w_hbm, w_full, sems)
        proj[...] = jnp.dot(h_ref[...], w_full[...], preferred_element_type=F32)
        c, s = cos_ref[...], sin_ref[...]
        for j in range(ATTN_W // 128):
            lo = 128 * j
            q_ref[:, lo:lo + 128] = _rope128(proj[:, lo:lo + 128], c, s, False) * SCALE
            k_ref[:, lo:lo + 128] = _rope128(proj[:, ATTN_W + lo:ATTN_W + lo + 128], c, s, False)
        v_ref[...] = proj[:, 2 * ATTN_W:3 * ATTN_W]
        bcu_ref[...] = proj[:, 3 * ATTN_W:3 * ATTN_W + 3 * CONV_W]
        qx_ref[...] = proj[:, 3 * ATTN_W + 3 * CONV_W:PROJ_W].astype(BF16)

    return pl.pallas_call(
        body, name="in_proj_fwd", grid=(S // tm,),
        in_specs=[_rows(tm, D_MODEL), pl.BlockSpec(memory_space=pl.ANY), _rows(tm, 128), _rows(tm, 128),
                  pl.BlockSpec(memory_space=pl.ANY)],
        out_specs=[_rows(tm, ATTN_W), _rows(tm, ATTN_W), _rows(tm, ATTN_W), _rows(tm, 3 * CONV_W), _rows(tm, XATTN_W)],
        out_shape=[jax.ShapeDtypeStruct((S, ATTN_W), F32), jax.ShapeDtypeStruct((S, ATTN_W), F32),
                   jax.ShapeDtypeStruct((S, ATTN_W), F32), jax.ShapeDtypeStruct((S, 3 * CONV_W), F32),
                   jax.ShapeDtypeStruct((S, XATTN_W), BF16)],
        scratch_shapes=[pltpu.VMEM((tm, PROJ_W), F32), pltpu.VMEM((D_MODEL, PROJ_W), BF16),
                        pltpu.SemaphoreType.DMA((N_CHIPS,))],
        compiler_params=_params("arbitrary"),
    )(h, w_in, cos, sin, after)


def _memkv_fwd(mem, g_mem, w_kv, after):
    n_mem = mem.shape[0]

    def body(mem_ref, g_ref, w_ref, after_ref, mn_ref, kv_ref):
        mhat, _ = _rms_hat(mem_ref[...])
        mn = (mhat * g_ref[...]).astype(BF16)
        mn_ref[...] = mn
        kv_ref[...] = jnp.dot(mn, w_ref[...], preferred_element_type=F32).astype(BF16)

    vmem = pl.BlockSpec(memory_space=pltpu.VMEM)
    return pl.pallas_call(
        body, name="memkv_fwd", in_specs=[vmem, vmem, vmem, pl.BlockSpec(memory_space=pl.ANY)], out_specs=[vmem, vmem],
        out_shape=[jax.ShapeDtypeStruct((n_mem, D_MODEL), BF16), jax.ShapeDtypeStruct((n_mem, 2 * XATTN_W), BF16)],
        compiler_params=pltpu.CompilerParams(vmem_limit_bytes=VMEM_LIMIT_V7X),
    )(mem, g_mem, w_kv, after)


def _fill_band_bias(bias):
    row = lax.broadcasted_iota(jnp.int32, (N_BACK, 2 * N_BACK), 0)
    col = lax.broadcasted_iota(jnp.int32, (N_BACK, 2 * N_BACK), 1)
    band = (col >= row) & (col <= row + N_BACK)
    bias[1] = jnp.where(band, 0.0, NEG_INF)
    bias[0] = jnp.where(band & (col >= N_BACK), 0.0, NEG_INF)


def _strided(start, size, d):
    return pl.ds(start, size) if d == 1 else pl.ds(start, size, stride=d)


def _group_starts(g, G, nb, d):
    t0 = g * G
    r, n0 = lax.shift_right_logical(t0, nb.bit_length() - 1), lax.bitwise_and(t0, nb - 1)
    first = r + n0 * (N_BACK * d)
    before = r + jnp.maximum(n0 - 1, 0) * (N_BACK * d)
    starts = [before] + [first + u * (N_BACK * d) for u in range(G)]
    if d == 1:
        starts = [pl.multiple_of(st, N_BACK) for st in starts]
    return starts, n0


def _step_blocks(i, U, nb, d):
    G = min(U, nb)
    whole = G == nb
    row_blocks, blocks = [], []
    for grp in range(U // G):
        starts, n0 = _group_starts(i * (U // G) + grp, G, nb, d)
        base = len(row_blocks)
        if whole:
            row_blocks += [_strided(st, N_BACK, d) for st in starts[1:]]
            blocks += [(base + max(u - 1, 0), base + u, min(u, 1)) for u in range(G)]
        else:
            row_blocks += [_strided(st, N_BACK, d) for st in starts]
            blocks += [(base + u, base + u + 1, jnp.minimum(n0, 1) if u == 0 else 1) for u in range(G)]
    return row_blocks, blocks


def _by_head(a, b):
    lane = lax.broadcasted_iota(jnp.int32, (a.shape[0], 2 * HEAD), 1)
    return jnp.where(lane < HEAD, a, b)


def _head_only(t, hh):
    lane = lax.broadcasted_iota(jnp.int32, t.shape, 1)
    return jnp.where((lane < HEAD) == (hh == 0), t, jnp.zeros_like(t))


def _stack_heads(t):
    return jnp.concatenate([_head_only(t, 0), _head_only(t, 1)], axis=0)


def _head_columns(t):
    return jnp.concatenate([t[:, 0:1], t[:, HEAD:HEAD + 1]], axis=0)


def _unstack(t):
    return _by_head(t[:N_BACK], t[N_BACK:])


def _unstack_columns(t):
    return _by_head(jnp.broadcast_to(t[:N_BACK], (N_BACK, 2 * HEAD)), jnp.broadcast_to(t[N_BACK:], (N_BACK, 2 * HEAD)))


FWD_BLOCKS_PER_STEP = 4
BWD_BLOCKS_PER_STEP = 4
BWD_CHUNK = 64


def _attn_fwd(q, k, v):
    S = q.shape[0]
    U = FWD_BLOCKS_PER_STEP

    def body(q_ref, k_ref, v_ref, y_ref, m_ref, l_scr, bias):
        _fill_band_bias(bias)
        for g, d in enumerate(PATTERN_ORDER):
            nb = S // d // N_BACK
            first_pattern, last_pattern = g == 0, g == len(PATTERN_ORDER) - 1

            def step(i, carry, d=d, nb=nb, first_pattern=first_pattern, last_pattern=last_pattern):
                row_blocks, blocks = _step_blocks(i, U, nb, d)
                kb = [k_ref[r, :].astype(BF16) for r in row_blocks]
                ss = []
                for before, own, which in blocks:
                    kw = jnp.concatenate([kb[before], kb[own]], 0)
                    qs = _stack_heads(q_ref[row_blocks[own], :].astype(BF16))
                    b = bias[which]
                    ss.append(lax.dot_general(qs, kw, NT, preferred_element_type=F32) + jnp.concatenate([b, b], axis=0))
                ms = [jnp.max(s, axis=1, keepdims=True) for s in ss]
                ps = [jnp.exp(s - m) for s, m in zip(ss, ms)]
                ls = [jnp.sum(p, axis=1, keepdims=True) for p in ps]
                vb = [v_ref[r, :].astype(BF16) for r in row_blocks]
                os_ = [jnp.dot(ps[u].astype(BF16), jnp.concatenate([vb[before], vb[own]], 0), preferred_element_type=F32)
                       for u, (before, own, _) in enumerate(blocks)]
                for u, (_, own, _) in enumerate(blocks):
                    o_g, m_g, l_g = _unstack(os_[u]), _unstack_columns(ms[u]), _unstack_columns(ls[u])
                    r = row_blocks[own]
                    if first_pattern:
                        m_new, l_new, acc = m_g, l_g, o_g
                    else:
                        m_old = m_ref[r, :]
                        m_new = jnp.maximum(m_old, m_g)
                        alpha, beta = jnp.exp(m_old - m_new), jnp.exp(m_g - m_new)
                        l_new = l_scr[r, :] * alpha + l_g * beta
                        acc = y_ref[r, :] * alpha + o_g * beta
                    if last_pattern:
                        y_ref[r, :] = acc / l_new
                        m_ref[r, :] = m_new + jnp.log(l_new)
                    else:
                        y_ref[r, :] = acc
                        m_ref[r, :] = m_new
                        l_scr[r, :] = l_new
                return carry

            lax.fori_loop(0, d * nb // U, step, 0)

    col = pl.BlockSpec((S, 2 * HEAD), lambda j: (0, j))
    return pl.pallas_call(
        body, name="attn_fwd", grid=(q.shape[1] // (2 * HEAD),),
        in_specs=[col, col, col], out_specs=[col, col],
        out_shape=[jax.ShapeDtypeStruct(q.shape, F32)] * 2,
        scratch_shapes=[pltpu.VMEM((S, 2 * HEAD), F32), pltpu.VMEM((2, N_BACK, 2 * N_BACK), F32)],
        compiler_params=_params("parallel"),
    )(q, k, v)


def _attn_bwd(q, k, v, dy, lse, delta, after):
    S = q.shape[0]
    U = BWD_BLOCKS_PER_STEP

    def body(q_ref, k_ref, v_ref, dy_ref, lse_ref, delta_ref, after_ref, dq_ref, dk_ref, dv_ref, bias):
        _fill_band_bias(bias)
        nb_first = S // PATTERN_ORDER[0] // N_BACK
        first_writes_all = min(U, nb_first) == nb_first
        if not first_writes_all:
            dk_ref[...] = jnp.zeros_like(dk_ref)
            dv_ref[...] = jnp.zeros_like(dv_ref)
        for g, d in enumerate(PATTERN_ORDER):
            nb = S // d // N_BACK

            def step(i, carry, d=d, nb=nb, g=g):
                row_blocks, blocks = _step_blocks(i, U, nb, d)
                kb = [k_ref[r, :].astype(BF16) for r in row_blocks]
                vb = [v_ref[r, :].astype(BF16) for r in row_blocks]
                kws = [jnp.concatenate([kb[before], kb[own]], 0) for before, own, _ in blocks]
                vws = [jnp.concatenate([vb[before], vb[own]], 0) for before, own, _ in blocks]
                qss = [_stack_heads(q_ref[row_blocks[own], :].astype(BF16)) for _, own, _ in blocks]
                doss = [_stack_heads(dy_ref[row_blocks[own], :].astype(BF16)) for _, own, _ in blocks]
                ss = [lax.dot_general(qss[u], kws[u], NT, preferred_element_type=F32) for u in range(U)]
                dps = [lax.dot_general(doss[u], vws[u], NT, preferred_element_type=F32) for u in range(U)]
                pbs, dss = [], []
                for u, (_, own, which) in enumerate(blocks):
                    lse_c = _head_columns(lse_ref[row_blocks[own], :])
                    delta_c = _head_columns(delta_ref[row_blocks[own], :])
                    p_parts, ds_parts = [], []
                    for r0 in range(0, 2 * N_BACK, BWD_CHUNK):
                        r = slice(r0, r0 + BWD_CHUNK)
                        mask = bias[which, r0 % N_BACK:r0 % N_BACK + BWD_CHUNK, :]
                        p_r = jnp.exp(ss[u][r] + mask - lse_c[r])
                        p_parts.append(p_r.astype(BF16))
                        ds_parts.append((p_r * (dps[u][r] - delta_c[r])).astype(BF16))
                    pbs.append(jnp.concatenate(p_parts, axis=0))
                    dss.append(jnp.concatenate(ds_parts, axis=0))
                dqs = [jnp.dot(dss[u], kws[u], preferred_element_type=F32) for u in range(U)]
                dkws = [lax.dot_general(dss[u], qss[u], TN, preferred_element_type=F32) for u in range(U)]
                dvws = [lax.dot_general(pbs[u], doss[u], TN, preferred_element_type=F32) for u in range(U)]
                dk_parts, dv_parts = [None] * len(row_blocks), [None] * len(row_blocks)
                for u, (before, own, _) in enumerate(blocks):
                    dq = _unstack(dqs[u])
                    if g == 0:
                        dq_ref[row_blocks[own], :] = dq
                    else:
                        dq_ref[row_blocks[own], :] += dq
                    for idx, dkp, dvp in ((before, dkws[u][:N_BACK], dvws[u][:N_BACK]),
                                          (own, dkws[u][N_BACK:], dvws[u][N_BACK:])):
                        dk_parts[idx] = dkp if dk_parts[idx] is None else dk_parts[idx] + dkp
                        dv_parts[idx] = dvp if dv_parts[idx] is None else dv_parts[idx] + dvp
                for idx, r in enumerate(row_blocks):
                    if g == 0 and first_writes_all:
                        dk_ref[r, :] = dk_parts[idx]
                        dv_ref[r, :] = dv_parts[idx]
                    else:
                        dk_ref[r, :] += dk_parts[idx]
                        dv_ref[r, :] += dv_parts[idx]
                return carry

            lax.fori_loop(0, d * nb // U, step, 0)

    col = pl.BlockSpec((S, 2 * HEAD), lambda j: (0, j))
    return pl.pallas_call(
        body, name="attn_bwd", grid=(q.shape[1] // (2 * HEAD),),
        in_specs=[col] * 6 + [pl.BlockSpec(memory_space=pl.ANY)], out_specs=[col] * 3,
        out_shape=[jax.ShapeDtypeStruct(q.shape, F32)] * 3,
        scratch_shapes=[pltpu.VMEM((2, N_BACK, 2 * N_BACK), F32)],
        compiler_params=_params("parallel"),
    )(q, k, v, dy, lse, delta, after)


def _shift_down(z, before, k):
    row = lax.broadcasted_iota(jnp.int32, z.shape, 0)
    out = pltpu.roll(z, k, 0)
    for i in range(k):
        out = jnp.where(row == i, before[8 - k + i:8 - k + i + 1, :], out)
    return out


def _shift_up(z, after, k):
    rows = z.shape[0]
    row = lax.broadcasted_iota(jnp.int32, z.shape, 0)
    out = pltpu.roll(z, rows - k, 0)
    for i in range(k):
        out = jnp.where(row == rows - k + i, after[i:i + 1, :], out)
    return out


def _conv_fwd(bcu, before, is_first, w):
    b, c, u = bcu[:, 0:CONV_W], bcu[:, CONV_W:2 * CONV_W], bcu[:, 2 * CONV_W:3 * CONV_W]
    z = c * u
    zb = jnp.where(is_first, 0.0, before[:, CONV_W:2 * CONV_W] * before[:, 2 * CONV_W:3 * CONV_W])
    z1, z2 = _shift_down(z, zb, 1), _shift_down(z, zb, 2)
    cv = w[0:1, :] * z2 + w[1:2, :] * z1 + w[2:3, :] * z
    return b, c, u, z, z1, z2, cv


def _halo_before(tm, width):
    return pl.BlockSpec((8, width), lambda i: (jnp.maximum(i * (tm // 8) - 1, 0), 0))


def _mix_fwd(ya, bcu, qx, mkv, conv_w, g_a, g_c, g_x, w_out, g_post, x, tm):
    S = x.shape[0]

    def body(ya_ref, bcu_ref, before_ref, qx_ref, mkv_ref, cw_ref, ga_ref, gc_ref, gx_ref,
             wo_ref, gp_ref, x_ref, yx_ref, ycat_ref, y2_ref, x1_ref):
        ya = ya_ref[...]
        b, _, _, _, _, _, cv = _conv_fwd(bcu_ref[...], before_ref[...], pl.program_id(0) == 0, cw_ref[...])
        yc = b * cv

        qxb, mkvb = qx_ref[...], mkv_ref[...]
        heads = [slice(HEAD * hd, HEAD * (hd + 1)) for hd in range(XATTN_W // HEAD)]
        ss = [lax.dot_general(qxb[:, sl], mkvb[:, sl], NT, preferred_element_type=F32) * SCALE for sl in heads]
        ms = [jnp.max(s, axis=1, keepdims=True) for s in ss]
        ps = [jnp.exp(s - m) for s, m in zip(ss, ms)]
        ls = [jnp.sum(p, axis=1, keepdims=True) for p in ps]
        os_ = [jnp.dot(p.astype(BF16), mkvb[:, XATTN_W + sl.start:XATTN_W + sl.stop], preferred_element_type=F32)
               for p, sl in zip(ps, heads)]
        for sl, o, l in zip(heads, os_, ls):
            yx_ref[:, sl] = o / l
        yx = yx_ref[...]

        ycat_ref[:, 0:ATTN_W] = (_rms_hat(ya)[0] * ga_ref[...]).astype(BF16)
        ycat_ref[:, ATTN_W:ATTN_W + CONV_W] = (_rms_hat(yc)[0] * gc_ref[...]).astype(BF16)
        ycat_ref[:, ATTN_W + CONV_W:D_MODEL] = (_rms_hat(yx)[0] * gx_ref[...]).astype(BF16)
        y2 = jnp.dot(ycat_ref[...], wo_ref[...], preferred_element_type=F32)
        y2_ref[...] = y2
        x1_ref[...] = x_ref[...] + _rms_hat(y2)[0] * gp_ref[...]

    n_mem = mkv.shape[0]
    return pl.pallas_call(
        body, name="mix_fwd", grid=(S // tm,),
        in_specs=[_rows(tm, ATTN_W), _rows(tm, 3 * CONV_W), _halo_before(tm, 3 * CONV_W), _rows(tm, XATTN_W),
                  _resident((n_mem, 2 * XATTN_W)), _resident((3, CONV_W)), _resident((1, ATTN_W)),
                  _resident((1, CONV_W)), _resident((1, XATTN_W)), _resident((D_MODEL, D_MODEL)),
                  _resident((1, D_MODEL)), _rows(tm, D_MODEL)],
        out_specs=[_rows(tm, XATTN_W), _rows(tm, D_MODEL), _rows(tm, D_MODEL), _rows(tm, D_MODEL)],
        out_shape=[jax.ShapeDtypeStruct((S, XATTN_W), F32), jax.ShapeDtypeStruct((S, D_MODEL), BF16),
                   jax.ShapeDtypeStruct((S, D_MODEL), F32), jax.ShapeDtypeStruct((S, D_MODEL), F32)],
        compiler_params=_params("parallel"),
    )(ya, bcu, bcu, qx, mkv, conv_w, g_a, g_c, g_x, w_out, g_post, x)


def _mlp_fwd_bwd(x1, target, g_pre, g_post, w_up, w_down, tm):
    S = x1.shape[0]
    n_ff = D_FF // SHARD_FF
    n_tiles = S // tm
    piece = tm // n_ff

    def zero_after(t):
        bits = lax.bitcast_convert_type(jnp.sum(t, keepdims=True), jnp.uint32)
        bits = lax.shift_right_logical(lax.shift_right_logical(bits, jnp.uint32(16)), jnp.uint32(16))
        return lax.bitcast_convert_type(bits, F32)

    def body(x1_ref, x1_prev_ref, t_prev_ref, gpre_ref, gpost_ref, wup_ref, wdn_ref,
             h2_ref, f_ref, du_ref, df2_ref, dx1_ref, dgpre_ref, dgpost_ref, loss_ref,
             u_scr, f2_scr, dx2_scr):
        i = pl.program_id(0)
        slot = i % 2

        def loss_part(j):
            rows = slice(piece * j, piece * (j + 1))
            f2hat, r2 = _rms_hat(f2_scr[1 - slot, rows, :])
            err = x1_prev_ref[rows, :] + f2hat * gpost_ref[...] - t_prev_ref[rows, :]
            loss_ref[...] += 0.5 * jnp.sum(jnp.mean(err * err, axis=-1, keepdims=True), axis=0, keepdims=True)
            dx2 = err * (1.0 / D_MODEL)
            dx2_scr[rows, :] = dx2
            dgpost_ref[...] += jnp.sum(dx2 * f2hat, axis=0, keepdims=True)
            df2 = _rms_bwd(f2hat, r2, gpost_ref[...], dx2)
            df2_ref[rows, :] = df2.astype(BF16)
            return zero_after(df2)

        def forward(between):
            x1hat, _ = _rms_hat(x1_ref[...])
            h2 = (x1hat * gpre_ref[...]).astype(BF16)
            h2_ref[...] = h2
            f2 = jnp.zeros((tm, D_MODEL), F32)
            tie = None
            for j in range(n_ff):
                cols = slice(SHARD_FF * j, SHARD_FF * (j + 1))
                u = jnp.maximum(jnp.dot(h2, wup_ref[j], preferred_element_type=F32), 0.0)
                if tie is not None:
                    u = u + tie
                u_scr[slot, :, cols] = u
                f = (u * u).astype(BF16)
                f_ref[:, cols] = f
                f2 = f2 + jnp.dot(f, wdn_ref[cols, :], preferred_element_type=F32)
                tie = between(j)
            f2_scr[slot] = f2

        def backward():
            df2 = df2_ref[...]
            dh2 = jnp.zeros((tm, D_MODEL), F32)
            for j in range(n_ff):
                cols = slice(SHARD_FF * j, SHARD_FF * (j + 1))
                df = lax.dot_general(df2, wdn_ref[cols, :], NT, preferred_element_type=F32)
                du = (2.0 * u_scr[1 - slot, :, cols] * df).astype(BF16)
                du_ref[:, cols] = du
                dh2 = dh2 + lax.dot_general(du, wup_ref[j], NT, preferred_element_type=F32)
            x1hat, r1 = _rms_hat(x1_prev_ref[...])
            dgpre_ref[...] += jnp.sum(dh2 * x1hat, axis=0, keepdims=True)
            dx1_ref[...] = dx2_scr[...] + _rms_bwd(x1hat, r1, gpre_ref[...], dh2)

        @pl.when(i == 0)
        def _():
            dgpre_ref[...] = jnp.zeros_like(dgpre_ref)
            dgpost_ref[...] = jnp.zeros_like(dgpost_ref)
            loss_ref[...] = jnp.zeros_like(loss_ref)
            forward(lambda j: None)

        @pl.when(jnp.logical_and(i > 0, i < n_tiles))
        def _():
            forward(loss_part)
            backward()

        @pl.when(i == n_tiles)
        def _():
            for j in range(n_ff):
                loss_part(j)
            backward()

    ahead = lambda width: pl.BlockSpec((tm, width), lambda i: (jnp.minimum(i, n_tiles - 1), 0))
    behind = lambda width: pl.BlockSpec((tm, width), lambda i: (jnp.maximum(i - 1, 0), 0))
    acc = pl.BlockSpec((1, D_MODEL), lambda i: (0, 0))
    return pl.pallas_call(
        body, name="mlp_fwd_bwd", grid=(n_tiles + 1,),
        in_specs=[ahead(D_MODEL), behind(D_MODEL), behind(D_MODEL), _resident((1, D_MODEL)), _resident((1, D_MODEL)),
                  _resident((n_ff, D_MODEL, SHARD_FF)), _resident((D_FF, D_MODEL))],
        out_specs=[ahead(D_MODEL), ahead(D_FF), behind(D_FF), behind(D_MODEL), behind(D_MODEL),
                   acc, acc, pl.BlockSpec((1, 1), lambda i: (0, 0))],
        out_shape=[jax.ShapeDtypeStruct((S, D_MODEL), BF16), jax.ShapeDtypeStruct((S, D_FF), BF16),
                   jax.ShapeDtypeStruct((S, D_FF), BF16), jax.ShapeDtypeStruct((S, D_MODEL), BF16),
                   jax.ShapeDtypeStruct((S, D_MODEL), F32), jax.ShapeDtypeStruct((1, D_MODEL), F32),
                   jax.ShapeDtypeStruct((1, D_MODEL), F32), jax.ShapeDtypeStruct((1, 1), F32)],
        scratch_shapes=[pltpu.VMEM((2, tm, D_FF), F32), pltpu.VMEM((2, tm, D_MODEL), F32), pltpu.VMEM((tm, D_MODEL), F32)],
        compiler_params=_params("arbitrary"),
    )(x1, x1, target, g_pre, g_post, w_up, w_down)


def _weight_grad(name, a, b, rows_sharded, after):
    S, K = a.shape
    N = b.shape[1]
    if rows_sharded:
        tk, tn = K // N_CHIPS, N
        a_spec = pl.BlockSpec((S, tk), lambda j: (0, j))
        b_spec = pl.BlockSpec((S, tn), lambda j: (0, 0), pipeline_mode=pl.Buffered(1))
    else:
        tk, tn = K, N // N_CHIPS
        a_spec = pl.BlockSpec((S, tk), lambda j: (0, 0), pipeline_mode=pl.Buffered(1))
        b_spec = pl.BlockSpec((S, tn), lambda j: (0, j))
    half = tk // 2

    def body(a_ref, b_ref, after_ref, o_ref):
        res = lax.dot_general(a_ref[...], b_ref[...], TN, preferred_element_type=F32)
        o_ref[0, 0] = res[:half]
        o_ref[1, 0] = res[half:]

    return pl.pallas_call(
        body, name=name, grid=(N_CHIPS,), in_specs=[a_spec, b_spec, pl.BlockSpec(memory_space=pl.ANY)],
        out_specs=pl.BlockSpec((2, 1, half, tn), lambda j: (0, j, 0, 0)),
        out_shape=jax.ShapeDtypeStruct((2, N_CHIPS, half, tn), F32),
        compiler_params=_params("parallel"),
    )(a, b, after)


def _mixer_bwd(dx1, y2, ycat, ya, yx, bcu, qx, mkv, conv_w, g_a, g_c, g_x, w_out, g_post, after, tm):
    S = dx1.shape[0]
    n_mem = mkv.shape[0]
    n_tiles = S // tm
    half = D_MODEL // N_CHIPS // 2

    def body(dx1_ref, y2_ref, ycat_ref, ya_ref, yx_ref, bcu_ref, before_ref, qx_ref, mkv_ref, cw_ref, ga_ref, gc_ref,
             gx_ref, wo_ref, gp_ref, after_ref, gwo_ref, dya_ref, delta_ref, tail_ref, dmkv_ref, dcw_ref, dgp_ref,
             dga_ref, dgc_ref, dgx_ref, carry):
        step = pl.program_id(0)
        first_tile = step == n_tiles - 1

        @pl.when(step == 0)
        def _():
            for ref in (gwo_ref, dmkv_ref, dcw_ref, dgp_ref, dga_ref, dgc_ref, dgx_ref, carry):
                ref[...] = jnp.zeros_like(ref)

        dx1 = dx1_ref[...]
        y2hat, r2 = _rms_hat(y2_ref[...])
        dgp_ref[...] += jnp.sum(dx1 * y2hat, axis=0, keepdims=True)
        dy2 = _rms_bwd(y2hat, r2, gp_ref[...], dx1).astype(BF16)
        gwo = lax.dot_general(ycat_ref[...], dy2, TN, preferred_element_type=F32)
        for k in range(2 * N_CHIPS):
            gwo_ref[k % 2, k // 2] += gwo[half * k:half * (k + 1)]
        dycat = lax.dot_general(dy2, wo_ref[...], NT, preferred_element_type=F32)

        d_na = dycat[:, 0:ATTN_W]
        ya = ya_ref[...]
        yahat, ra = _rms_hat(ya)
        dga_ref[...] += jnp.sum(d_na * yahat, axis=0, keepdims=True)
        dya = _rms_bwd(yahat, ra, ga_ref[...], d_na)
        dya_ref[...] = dya
        prod = dya * ya
        hi = prod.astype(BF16)
        lo = (prod - hi.astype(F32)).astype(BF16)
        head_of = lambda axis: lax.shift_right_logical(lax.broadcasted_iota(jnp.int32, (ATTN_W, ATTN_W), axis),
                                                       HEAD.bit_length() - 1)
        ones = jnp.where(head_of(0) == head_of(1), 1.0, 0.0).astype(BF16)
        delta_ref[...] = jnp.dot(hi, ones, preferred_element_type=F32) + jnp.dot(lo, ones, preferred_element_type=F32)

        w = cw_ref[...]
        b, c, u, z, z1, z2, cv = _conv_fwd(bcu_ref[...], before_ref[...], first_tile, w)
        d_nc = dycat[:, ATTN_W:ATTN_W + CONV_W]
        ychat, rc = _rms_hat(b * cv)
        dgc_ref[...] += jnp.sum(d_nc * ychat, axis=0, keepdims=True)
        dyc = _rms_bwd(ychat, rc, gc_ref[...], d_nc)
        dcv = dyc * b
        behind = carry[...]
        dz = w[2:3, :] * dcv + w[1:2, :] * _shift_up(dcv, behind, 1) + w[0:1, :] * _shift_up(dcv, behind, 2)
        carry[...] = dcv[0:8, :]
        dcw_ref[0:1, :] += jnp.sum(dcv * z2, axis=0, keepdims=True)
        dcw_ref[1:2, :] += jnp.sum(dcv * z1, axis=0, keepdims=True)
        dcw_ref[2:3, :] += jnp.sum(dcv * z, axis=0, keepdims=True)
        tail_ref[:, 0:CONV_W] = (dyc * cv).astype(BF16)
        tail_ref[:, CONV_W:2 * CONV_W] = (dz * u).astype(BF16)
        tail_ref[:, 2 * CONV_W:3 * CONV_W] = (dz * c).astype(BF16)

        d_nx = dycat[:, ATTN_W + CONV_W:D_MODEL]
        yxhat, rx = _rms_hat(yx_ref[...])
        dgx_ref[...] += jnp.sum(d_nx * yxhat, axis=0, keepdims=True)
        dyx = _rms_bwd(yxhat, rx, gx_ref[...], d_nx)
        qxb, mkvb = qx_ref[...], mkv_ref[...]
        heads = [slice(HEAD * hd, HEAD * (hd + 1)) for hd in range(XATTN_W // HEAD)]
        values = [slice(XATTN_W + sl.start, XATTN_W + sl.stop) for sl in heads]
        ss = [lax.dot_general(qxb[:, sl], mkvb[:, sl], NT, preferred_element_type=F32) * SCALE for sl in heads]
        es = [jnp.exp(s - jnp.max(s, axis=1, keepdims=True)) for s in ss]
        ps = [e / jnp.sum(e, axis=1, keepdims=True) for e in es]
        dobs = [dyx[:, sl].astype(BF16) for sl in heads]
        dps = [lax.dot_general(dob, mkvb[:, vsl], NT, preferred_element_type=F32) for dob, vsl in zip(dobs, values)]
        dss = [(p * (dp - jnp.sum(p * dp, axis=1, keepdims=True)) * SCALE).astype(BF16) for p, dp in zip(ps, dps)]
        for sl, vsl, p, dob, ds in zip(heads, values, ps, dobs, dss):
            tail_ref[:, 3 * CONV_W + sl.start:3 * CONV_W + sl.stop] = jnp.dot(
                ds, mkvb[:, sl], preferred_element_type=F32).astype(BF16)
            dmkv_ref[:, sl] += lax.dot_general(ds, qxb[:, sl], TN, preferred_element_type=F32)
            dmkv_ref[:, vsl] += lax.dot_general(p.astype(BF16), dob, TN, preferred_element_type=F32)

    rows = lambda width: pl.BlockSpec((tm, width), lambda i: (n_tiles - 1 - i, 0))
    before = pl.BlockSpec((8, 3 * CONV_W), lambda i: (jnp.maximum((n_tiles - 1 - i) * (tm // 8) - 1, 0), 0))
    acc = lambda r, w: pl.BlockSpec((r, w), lambda i: (0, 0))
    return pl.pallas_call(
        body, name="mixer_bwd", grid=(n_tiles,),
        in_specs=[rows(D_MODEL), rows(D_MODEL), rows(D_MODEL), rows(ATTN_W), rows(XATTN_W), rows(3 * CONV_W), before,
                  rows(XATTN_W), _resident((n_mem, 2 * XATTN_W)), _resident((3, CONV_W)), _resident((1, ATTN_W)),
                  _resident((1, CONV_W)), _resident((1, XATTN_W)), _resident((D_MODEL, D_MODEL)),
                  _resident((1, D_MODEL)), pl.BlockSpec(memory_space=pl.ANY)],
        out_specs=[pl.BlockSpec((2, N_CHIPS, half, D_MODEL), lambda i: (0, 0, 0, 0)), rows(ATTN_W), rows(ATTN_W),
                   rows(3 * CONV_W + XATTN_W), acc(n_mem, 2 * XATTN_W),
                   acc(3, CONV_W), acc(1, D_MODEL), acc(1, ATTN_W), acc(1, CONV_W), acc(1, XATTN_W)],
        out_shape=[jax.ShapeDtypeStruct((2, N_CHIPS, half, D_MODEL), F32), jax.ShapeDtypeStruct((S, ATTN_W), F32),
                   jax.ShapeDtypeStruct((S, ATTN_W), F32), jax.ShapeDtypeStruct((S, 3 * CONV_W + XATTN_W), BF16),
                   jax.ShapeDtypeStruct((n_mem, 2 * XATTN_W), F32), jax.ShapeDtypeStruct((3, CONV_W), F32),
                   jax.ShapeDtypeStruct((1, D_MODEL), F32), jax.ShapeDtypeStruct((1, ATTN_W), F32),
                   jax.ShapeDtypeStruct((1, CONV_W), F32), jax.ShapeDtypeStruct((1, XATTN_W), F32)],
        scratch_shapes=[pltpu.VMEM((8, CONV_W), F32)],
        compiler_params=_params("arbitrary"),
    )(dx1, y2, ycat, ya, yx, bcu, bcu, qx, mkv, conv_w, g_a, g_c, g_x, w_out, g_post, after)


def _memkv_bwd(mem, g_mem, w_kv, dmkv):
    n_mem = mem.shape[0]
    half = D_MODEL // N_CHIPS // 2

    def body(mem_ref, g_ref, w_ref, d_ref, dw_ref, dg_ref):
        mhat, _ = _rms_hat(mem_ref[...])
        mn = (mhat * g_ref[...]).astype(BF16)
        d = d_ref[...].astype(BF16)
        for k in range(2 * N_CHIPS):
            dw_ref[k % 2, k // 2] = lax.dot_general(mn[:, half * k:half * (k + 1)], d, TN, preferred_element_type=F32)
        dmn = lax.dot_general(d, w_ref[...], NT, preferred_element_type=F32)
        dg_ref[...] = jnp.sum(dmn * mhat, axis=0, keepdims=True)

    return pl.pallas_call(
        body, name="memkv_bwd",
        out_shape=[jax.ShapeDtypeStruct((2, N_CHIPS, half, 2 * XATTN_W), F32), jax.ShapeDtypeStruct((1, D_MODEL), F32)],
        compiler_params=pltpu.CompilerParams(vmem_limit_bytes=VMEM_LIMIT_V7X),
    )(mem, g_mem, w_kv, dmkv)


def _sum_of_partials(own_ref, sibling_ref, other_refs):
    acc = own_ref[0, 0] + sibling_ref[0]
    for ref in other_refs:
        acc = acc + ref[0].astype(F32)
    return acc


def _in_proj_bwd(dqkv, tail, cos, sin, w_in, x, h, g, dx1, after, tm, sums=None):
    S = x.shape[0]
    step_w = 2 * 256
    half = D_MODEL // 2
    n_steps = S // tm
    sum_grads, sum_sibling, sum_others, place = sums if sums is not None else ([], [], [], jnp.zeros((2,), jnp.int32))
    k = len(sum_grads)

    def body(place_ref, dq_ref, dk_ref, dv_ref, tail_ref, cos_ref, sin_ref, w_hbm, x_ref, h_ref, g_ref, dx1_ref,
             after_ref, *refs):
        sum_refs, (dx_ref, gw_ref, dg_ref), sum_out_refs = refs[:5 * k], refs[5 * k:5 * k + 3], refs[5 * k + 3:6 * k + 3]
        dproj_ref, w_full, sems = refs[6 * k + 3:]
        for a in range(k):
            sum_out_refs[a][0] = _sum_of_partials(sum_refs[a], sum_refs[k + a], sum_refs[2 * k + 3 * a:2 * k + 3 * a + 3])
        _side_by_side(w_hbm, w_full, sems)

        @pl.when(pl.program_id(0) == 0)
        def _():
            dg_ref[...] = jnp.zeros_like(dg_ref)
            gw_ref[...] = jnp.zeros_like(gw_ref)

        halves = [slice(0, tm // 2), slice(tm // 2, tm)]
        for rows in halves:
            c, s = cos_ref[rows, :], sin_ref[rows, :]
            for j in range(ATTN_W // 128):
                cols = slice(128 * j, 128 * (j + 1))
                dproj_ref[rows, cols] = _rope128(dq_ref[rows, cols] * SCALE, c, s, True).astype(BF16)
                dproj_ref[rows, ATTN_W + 128 * j:ATTN_W + 128 * (j + 1)] = _rope128(dk_ref[rows, cols], c, s, True).astype(BF16)
            dproj_ref[rows, 2 * ATTN_W:3 * ATTN_W] = dv_ref[rows, :].astype(BF16)
            dproj_ref[rows, 3 * ATTN_W:PROJ_W] = tail_ref[rows, :]
        dhs = [lax.dot_general(dproj_ref[rows, :], w_full[...], NT, preferred_element_type=F32) for rows in halves]
        for rows, dh in zip(halves, dhs):
            xhat, r = _rms_hat(x_ref[rows, :])
            dg_ref[...] += jnp.sum(dh * xhat, axis=0, keepdims=True)
            dx_ref[rows, :] = dx1_ref[rows, :] + _rms_bwd(xhat, r, g_ref[...], dh)
        hb = h_ref[...]
        for step in range(PROJ_W // step_w):
            res = lax.dot_general(hb, dproj_ref[:, step * step_w:(step + 1) * step_w], TN, preferred_element_type=F32)
            lo = step * step_w
            while lo < (step + 1) * step_w:
                chip = lo // SHARD_IN
                hi = min((step + 1) * step_w, (chip + 1) * SHARD_IN)
                for hh in range(2):
                    gw_ref[hh, chip, :, lo - chip * SHARD_IN:hi - chip * SHARD_IN] += (
                        res[half * hh:half * (hh + 1), lo - step * step_w:hi - step * step_w])
                lo = hi

    whole = lambda shape: pl.BlockSpec(shape, lambda i, p: (0,) * len(shape))
    slab = lambda t: (1, t.shape[-2] // n_steps, t.shape[-1])
    sum_specs = ([pl.BlockSpec((1,) + slab(t), lambda i, p: (p[0], p[1], i, 0)) for t in sum_grads]
                 + [pl.BlockSpec(slab(t), lambda i, p: (p[1], i, 0)) for t in sum_grads]
                 + [pl.BlockSpec(slab(t), lambda i, p, j=j: (j, i, 0)) for t in sum_grads for j in range(3)])
    results = pl.pallas_call(
        body, name="in_proj_bwd",
        out_shape=[jax.ShapeDtypeStruct((S, D_MODEL), F32), jax.ShapeDtypeStruct((2, N_CHIPS, half, SHARD_IN), F32),
                   jax.ShapeDtypeStruct((1, D_MODEL), F32)]
        + [jax.ShapeDtypeStruct((2,) + t.shape[2:], F32) for t in sum_grads],
        grid_spec=pltpu.PrefetchScalarGridSpec(
            num_scalar_prefetch=1, grid=(n_steps,),
            in_specs=[_rows(tm, ATTN_W)] * 3 + [_rows(tm, PROJ_W - 3 * ATTN_W), _rows(tm, 128), _rows(tm, 128),
                      pl.BlockSpec(memory_space=pl.ANY), _rows(tm, D_MODEL), _rows(tm, D_MODEL),
                      _resident((1, D_MODEL)), _rows(tm, D_MODEL), pl.BlockSpec(memory_space=pl.ANY)] + sum_specs,
            out_specs=[_rows(tm, D_MODEL), whole((2, N_CHIPS, half, SHARD_IN)), whole((1, D_MODEL))]
            + [pl.BlockSpec(slab(t), lambda i, p: (p[0], i, 0)) for t in sum_grads],
            scratch_shapes=[pltpu.VMEM((tm, PROJ_W), BF16), pltpu.VMEM((D_MODEL, PROJ_W), BF16),
                            pltpu.SemaphoreType.DMA((N_CHIPS,))]),
        compiler_params=_params("arbitrary"),
    )(place, *dqkv, tail, cos, sin, w_in, x, h, g, dx1, after, *sum_grads, *sum_sibling,
      *[o for o in sum_others for _ in range(3)])
    return [*results[:3], list(results[3:])]


def _row_tile(rows):
    return ROW_TILE if rows % ROW_TILE == 0 else rows


def _chip_sums_bf16(name, grads, from_sibling, place):
    k = len(grads)
    _, n, rows, _ = grads[0].shape
    tr = _row_tile(rows)

    def body(place_ref, *refs):
        for g_ref, b_ref, o_ref in zip(refs[:k], refs[k:2 * k], refs[2 * k:]):
            o_ref[...] = (g_ref[0] + b_ref[...]).astype(BF16)

    mine = lambda g: pl.BlockSpec((1, 1, tr, g.shape[3]), lambda s, i, p: (p[0], s, i, 0))
    slab = lambda g: pl.BlockSpec((1, tr, g.shape[3]), lambda s, i, p: (s, i, 0))
    return pl.pallas_call(
        body, name=name, out_shape=[jax.ShapeDtypeStruct(g.shape[1:], BF16) for g in grads],
        grid_spec=pltpu.PrefetchScalarGridSpec(
            num_scalar_prefetch=1, grid=(n, rows // tr),
            in_specs=[mine(g) for g in grads] + [slab(g) for g in grads], out_specs=[slab(g) for g in grads]),
        compiler_params=_params("parallel", "parallel"),
    )(place, *grads, *from_sibling)


def _final_sums(name, grads, from_sibling, others, place):
    k = len(grads)
    rows = grads[0].shape[2]
    tr = _row_tile(rows)

    def body(place_ref, *refs):
        for a in range(k):
            refs[5 * k + a][0] = _sum_of_partials(refs[a], refs[k + a], refs[2 * k + 3 * a:2 * k + 3 * a + 3])

    own = lambda g: pl.BlockSpec((1, 1, tr, g.shape[3]), lambda i, p: (p[0], p[1], i, 0))
    sib = lambda g: pl.BlockSpec((1, tr, g.shape[3]), lambda i, p: (p[1], i, 0))
    other = lambda g, j: pl.BlockSpec((1, tr, g.shape[3]), lambda i, p: (j, i, 0))
    return pl.pallas_call(
        body, name=name, out_shape=[jax.ShapeDtypeStruct((2,) + g.shape[2:], F32) for g in grads],
        grid_spec=pltpu.PrefetchScalarGridSpec(
            num_scalar_prefetch=1, grid=(rows // tr,),
            in_specs=[own(g) for g in grads] + [sib(g) for g in grads] + [other(g, j) for g in grads for j in range(3)],
            out_specs=[pl.BlockSpec((1, tr, g.shape[3]), lambda i, p: (p[0], i, 0)) for g in grads]),
        compiler_params=_params("parallel"),
    )(place, *grads, *from_sibling, *[o for o in others for _ in range(3)])


def _adamw_update(w, g, m, v):
    m = ADAM_B1 * m + (1.0 - ADAM_B1) * g
    v = ADAM_B2 * v + (1.0 - ADAM_B2) * (g * g)
    m_hat = m * (1.0 / (1.0 - ADAM_B1 ** ADAM_STEP))
    v_hat = v * (1.0 / (1.0 - ADAM_B2 ** ADAM_STEP))
    return -ADAM_LR * (m_hat / (jnp.sqrt(v_hat) + ADAM_EPS) + ADAM_WD * w), m, v


def _adamw(name, params, after):
    k = len(params)
    rows = params[0][0].shape[0]
    tr = ADAMW_ROW_TILE if rows % ADAMW_ROW_TILE == 0 else rows

    def body(*refs):
        ins, outs = refs[:4 * k], refs[4 * k + 1:]
        for a in range(k):
            w_ref, g_ref, m_ref, v_ref = ins[4 * a:4 * a + 4]
            g = g_ref[...]
            outs[4 * a][...] = g
            outs[4 * a + 1][...], outs[4 * a + 2][...], outs[4 * a + 3][...] = _adamw_update(w_ref[...], g, m_ref[...], v_ref[...])

    spec = lambda w: pl.BlockSpec((tr, w.shape[1]), lambda i: (i, 0))
    out = pl.pallas_call(
        body, name=name, grid=(rows // tr,),
        in_specs=[spec(p[0]) for p in params for _ in range(4)] + [pl.BlockSpec(memory_space=pl.ANY)],
        out_specs=[spec(p[0]) for p in params for _ in range(4)],
        out_shape=[jax.ShapeDtypeStruct(p[0].shape, F32) for p in params for _ in range(4)],
        compiler_params=_params("parallel"),
    )(*[t for p in params for t in p], after)
    return [out[4 * a:4 * a + 4] for a in range(k)]


def _small_update(blocks, chip, gains, gains_m, gains_v, taps, taps_m, taps_v):
    n = len(gains)
    widths = [g.shape[1] for g in gains]
    k, w = taps.shape

    def body(*refs):
        chip_ref, blocks_ref = refs[0], refs[1]
        params = [refs[2 + 3 * i:5 + 3 * i] for i in range(n + 1)]
        outs = [refs[2 + 3 * (n + 1) + 4 * i:2 + 3 * (n + 1) + 4 * (i + 1)] for i in range(n + 1)]
        loss_ref = refs[-1]
        summed = blocks_ref[0]
        for device in range(1, blocks.shape[0]):
            summed = summed + blocks_ref[device]
        for i in range(n):
            g = summed[i:i + 1, 0:widths[i]]
            wr, mr, vr = params[i]
            outs[i][0][...] = g
            outs[i][1][...], outs[i][2][...], outs[i][3][...] = _adamw_update(wr[...], g, mr[...], vr[...])
        g = summed[n:n + k, 0:w]
        for j in range(1, N_CHIPS):
            g = jnp.where(chip_ref[0] == j, summed[n:n + k, w * j:w * (j + 1)], g)
        wr, mr, vr = params[n]
        for out_ref, val in zip(outs[n], (g, *_adamw_update(wr[...], g, mr[...], vr[...]))):
            for j in range(k):
                out_ref[j] = val[j:j + 1, :]
        loss_ref[...] = summed[n + k:n + k + 1, 0:1]

    vmem = pl.BlockSpec(memory_space=pltpu.VMEM)
    operands = [chip, blocks]
    for p in zip(list(gains) + [taps], list(gains_m) + [taps_m], list(gains_v) + [taps_v]):
        operands += list(p)
    shapes = [jax.ShapeDtypeStruct(shape, F32) for shape in [g.shape for g in gains] + [(k, 1, w)] for _ in range(4)]
    out = pl.pallas_call(
        body, name="small_update", out_shape=shapes + [jax.ShapeDtypeStruct((1, 1), F32)],
        in_specs=[pl.BlockSpec(memory_space=pltpu.SMEM)] + [vmem] * (len(operands) - 1),
        out_specs=[vmem] * (len(shapes) + 1),
    )(*operands)
    return [out[4 * i:4 * (i + 1)] for i in range(n + 1)], out[-1]


def _place():
    return lax.axis_index("x"), lax.axis_index("y"), lax.axis_index("c")


def _other_chips(x, y):
    return [(1 - x, y), (x, 1 - y), (1 - x, 1 - y)]


def _allgather_finish(name, shards, landed, pass_on):
    n = len(shards)

    def body(*refs):
        ins, outs, stage = refs[:n], refs[2 * n:3 * n], refs[3 * n:4 * n]
        send_sems, recv_sems, local_sems = refs[4 * n:]
        x, y, c = _place()
        chips = _other_chips(x, y)
        barrier = pltpu.get_barrier_semaphore()
        pl.semaphore_signal(barrier, inc=1, device_id=(x, y, 1 - c), device_id_type=MESH)
        pl.semaphore_wait(barrier, 1)

        def copy(a, k, chip, half):
            place = outs[a].at[2 * chip[0] + chip[1], half]
            return pltpu.make_async_remote_copy(
                src_ref=place, dst_ref=place, send_sem=send_sems.at[3 * a + k], recv_sem=recv_sems.at[3 * a + k],
                device_id=(x, y, 1 - c), device_id_type=MESH)

        load = [pltpu.make_async_copy(ins[a], stage[a], local_sems.at[a]) for a in range(n)]
        local = [pltpu.make_async_copy(stage[a], outs[a].at[2 * x + y], local_sems.at[a]) for a in range(n)]
        for cp in load:
            cp.start()
        passed = [copy(a, k, chip, c) for a in range(n) if pass_on[a] for k, chip in enumerate(chips)]
        for cp in passed:
            cp.start()
        for a in range(n):
            load[a].wait()
            local[a].start()
        for a in range(n):
            if pass_on[a]:
                for k, chip in enumerate(chips):
                    copy(a, k, chip, 1 - c).wait_recv()
        for cp in passed:
            cp.wait_send()
        for cp in local:
            cp.wait()

    any_spec = pl.BlockSpec(memory_space=pl.ANY)
    return pl.pallas_call(
        body, name=name,
        out_shape=[jax.ShapeDtypeStruct((N_CHIPS,) + s.shape, s.dtype) for s in shards],
        in_specs=[any_spec] * (2 * n), out_specs=[any_spec] * n,
        input_output_aliases={n + a: a for a in range(n)},
        scratch_shapes=[pltpu.VMEM(s.shape, s.dtype) for s in shards]
        + [pltpu.SemaphoreType.DMA((3 * n,)), pltpu.SemaphoreType.DMA((3 * n,)), pltpu.SemaphoreType.DMA((n,))],
        compiler_params=pltpu.CompilerParams(vmem_limit_bytes=VMEM_LIMIT_V7X, collective_id=HANDSHAKES["sibling"][0]),
    )(*shards, *landed)


def _plan_first_hop(x, y, c, shards, lands):
    return [(shards[a].at[c], lands[a].at[2 * x + y, c], lands[a].at[2 * chip[0] + chip[1], c], (*chip, c))
            for a in range(len(shards)) for chip in _other_chips(x, y)]


def _plan_pass_on(x, y, c, nothing, lands):
    def place(a, chip, half):
        return lands[a].at[2 * chip[0] + chip[1], half]

    return [(place(a, chip, c), place(a, chip, c), place(a, chip, 1 - c), (x, y, 1 - c))
            for a in range(len(lands)) for chip in _other_chips(x, y)]


def _plan_own_half_to_sibling(x, y, c, nothing, lands):
    return [(lands[a].at[c], lands[a].at[c], lands[a].at[1 - c], (x, y, 1 - c)) for a in range(len(lands))]


def _plan_other_half_to_sibling(x, y, c, grads, lands):
    return [(grads[a].at[1 - c], lands[a], lands[a], (x, y, 1 - c)) for a in range(len(grads))]


def _plan_to_other_chips(x, y, c, partials, lands):
    return [(partials[a].at[2 * chip[0] + chip[1]], lands[a].at[k], lands[a].at[k], (*chip, c))
            for a in range(len(partials)) for k, chip in enumerate(_other_chips(x, y))]


def _plan_to_all(x, y, c, blocks, lands):
    flips = [(fx, fy, fc) for fx in (0, 1) for fy in (0, 1) for fc in (0, 1) if (fx, fy, fc) != (0, 0, 0)]
    peers = [(1 - x if fx else x, 1 - y if fy else y, 1 - c if fc else c) for fx, fy, fc in flips]
    return [(blocks[0], lands[0].at[4 * x + 2 * y + c], lands[0].at[4 * p[0] + 2 * p[1] + p[2]], p) for p in peers]


def _planned_copies(plan, srcs, lands, send_sems, recv_sems):
    x, y, c = _place()

    def pair(k, src, there, here, to):
        make = lambda dst: pltpu.make_async_remote_copy(
            src_ref=src, dst_ref=dst, send_sem=send_sems.at[k], recv_sem=recv_sems.at[k], device_id=to, device_id_type=MESH)
        return make(there), make(here)

    return [pair(k, *entry) for k, entry in enumerate(plan(x, y, c, srcs, lands))]


_HBM_SPEC = pl.BlockSpec(memory_space=pltpu.HBM)
_SEM_SPEC = pl.BlockSpec(memory_space=pltpu.SEMAPHORE)


def _hbm(a):
    return pltpu.with_memory_space_constraint(a, pltpu.HBM)


HANDSHAKES = {
    "sibling": (1, lambda x, y, c: [(x, y, 1 - c)]),
}


def _exchange_start(name, plan, n_copies, srcs, land_shapes, after, lands=None, peers=None):
    if lands is None:
        lands = [lax.empty(s.shape, s.dtype) for s in land_shapes]
    land_shapes = lands
    ns, nl = len(srcs), len(land_shapes)
    n_in = ns + nl + 1
    collective_id, peers_of = HANDSHAKES[peers] if peers else (None, None)

    def body(*refs):
        if peers:
            who = peers_of(*_place())
            barrier = pltpu.get_barrier_semaphore()
            for peer in who:
                pl.semaphore_signal(barrier, inc=1, device_id=peer, device_id_type=MESH)
            pl.semaphore_wait(barrier, len(who))
        for send, _ in _planned_copies(plan, refs[:ns], refs[ns:ns + nl], refs[n_in], refs[n_in + 1]):
            send.start()
        refs[-1][...] = jnp.zeros_like(refs[-1])

    out = pl.pallas_call(
        body, name=name,
        out_shape=(pltpu.SemaphoreType.DMA((n_copies,)), pltpu.SemaphoreType.DMA((n_copies,)),
                   *[pltpu.HBM(s.shape, s.dtype) for s in land_shapes], jax.ShapeDtypeStruct((8, 128), F32)),
        in_specs=[_HBM_SPEC] * (ns + nl) + [pl.BlockSpec(memory_space=pl.ANY)],
        out_specs=(_SEM_SPEC, _SEM_SPEC, *[_HBM_SPEC] * nl, pl.BlockSpec(memory_space=pltpu.VMEM)),
        input_output_aliases={ns + i: 2 + i for i in range(nl)},
        compiler_params=pltpu.CompilerParams(has_side_effects=pltpu.SideEffectType.DATAFLOW_SIDE_EFFECTING,
                                             collective_id=collective_id),
    )(*[_hbm(s) for s in srcs], *[_hbm(l) for l in lands], after)
    return out[0], out[1], list(out[2:2 + nl]), out[-1]


def _exchange_wait(name, plan, srcs, started, after):
    send_sems, recv_sems, lands, _ = started
    ns, nl = len(srcs), len(lands)
    after = list(after) if isinstance(after, (list, tuple)) else [after]

    def body(*refs):
        for send, recv in _planned_copies(plan, refs[:ns], refs[ns:ns + nl], refs[ns + nl], refs[ns + nl + 1]):
            send.wait_send()
            recv.wait_recv()

    return pl.pallas_call(
        body, name=name, out_shape=[pltpu.HBM(l.shape, l.dtype) for l in lands],
        in_specs=[_HBM_SPEC] * (ns + nl) + [_SEM_SPEC, _SEM_SPEC] + [pl.BlockSpec(memory_space=pl.ANY)] * len(after),
        out_specs=[_HBM_SPEC] * nl, input_output_aliases={ns + i: i for i in range(nl)},
        compiler_params=pltpu.CompilerParams(has_side_effects=pltpu.SideEffectType.DATAFLOW_SIDE_EFFECTING),
    )(*[_hbm(s) for s in srcs], *lands, send_sems, recv_sems, *after)


def _like(arrays, lead, dtype=None):
    return [jax.ShapeDtypeStruct(tuple(lead) + a.shape[-2:], dtype or a.dtype) for a in arrays]


class _StepExchanges:
    def __init__(self, mats, conv_w):
        x, y, c = _place()
        self.place = jnp.stack([c, 2 * x + y]).astype(jnp.int32)
        shards = [w.astype(BF16).reshape(2, w.shape[0] // 2, w.shape[1]) for w in mats]
        self._in_shard = shards[:1]
        self._in = _exchange_start("w_in_allgather_start", _plan_first_hop, 3, self._in_shard,
                                   _like(self._in_shard, (N_CHIPS, 2)), shards[0])
        self.zero = self._in[3]
        taps = jnp.pad(conv_w, ((0, 8 - conv_w.shape[0]), (0, 128 - conv_w.shape[1])))
        self._rest_shards = shards[1:] + [jnp.stack([taps, jnp.zeros_like(taps)])]
        self._taps_shape = conv_w.shape
        self._groups = {}

    def w_in(self, after):
        landed = _exchange_wait("w_in_allgather_wait", _plan_first_hop, self._in_shard, self._in,
                                list(after) + self._rest_shards)
        (w_in,) = _allgather_finish("w_in_allgather_finish", self._in_shard, landed, [True])
        self._rest = _exchange_start("rest_allgather_start", _plan_first_hop, 3 * len(self._rest_shards),
                                     self._rest_shards, _like(self._rest_shards, (N_CHIPS, 2)), w_in)
        self.zero = self._rest[3]
        return w_in.reshape(N_CHIPS, 2 * w_in.shape[2], w_in.shape[3])

    def rest_weights(self, after):
        landed = _exchange_wait("rest_allgather_wait", _plan_first_hop, self._rest_shards, self._rest, after)
        kv, out, up, down, taps = _allgather_finish("rest_allgather_finish", self._rest_shards, landed,
                                                    [True, True, False, False, True])
        self._up_down = _exchange_start("up_down_pass_on_start", _plan_pass_on, 6, [], None, self.zero, lands=[up, down],
                                        peers="sibling")
        self.zero = self._up_down[3]
        k, w = self._taps_shape
        taps = taps[:, 0, :k, :w].transpose(1, 0, 2).reshape(k, N_CHIPS * w)
        return [g.reshape(N_CHIPS, 2 * g.shape[2], g.shape[3]) for g in (kv, out)], taps

    def up_down(self, after):
        full = _exchange_wait("up_down_pass_on_wait", _plan_pass_on, [], self._up_down, after)
        return [g.reshape(N_CHIPS, 2 * g.shape[2], g.shape[3]) for g in full]

    def send_grads(self, key, grads):
        grads = list(grads)
        started = _exchange_start(f"{key}_grads_to_sibling_start", _plan_other_half_to_sibling, len(grads), grads,
                                  _like(grads, (N_CHIPS,)), self.zero, peers="sibling")
        self._groups[key] = dict(grads=grads, to_sibling=started)
        self.zero = started[3]

    def grads_at_sibling(self, key, after):
        group = self._groups[key]
        grads = group["grads"]
        group["from_sibling"] = _exchange_wait(f"{key}_grads_to_sibling_wait", _plan_other_half_to_sibling, grads,
                                               group["to_sibling"], after)
        group["partials"] = _chip_sums_bf16(f"{key}_chip_sums", grads, group["from_sibling"], self.place)
        group["to_chips"] = _exchange_start(f"{key}_grads_to_chips_start", _plan_to_other_chips, 3 * len(grads),
                                            group["partials"], _like(group["partials"], (3,)), self.zero)
        self.zero = group["to_chips"][3]

    def final_sum_operands(self, key, after):
        group = self._groups[key]
        from_chips = _exchange_wait(f"{key}_grads_to_chips_wait", _plan_to_other_chips, group["partials"],
                                    group["to_chips"], after)
        return group["grads"], group["from_sibling"], from_chips, self.place

    def grads_summed(self, key, after):
        return _final_sums(f"{key}_final_sums", *self.final_sum_operands(key, after))

    def send_sums(self, key, sums):
        self._groups[key + "_sums"] = _exchange_start(f"{key}_sums_to_sibling_start", _plan_own_half_to_sibling,
                                                      len(sums), [], None, self.zero, lands=list(sums),
                                                      peers="sibling")
        self.zero = self._groups[key + "_sums"][3]

    def whole_sums(self, key, after):
        full = _exchange_wait(f"{key}_sums_to_sibling_wait", _plan_own_half_to_sibling, [], self._groups[key + "_sums"], after)
        return [t.reshape(2 * t.shape[1], t.shape[2]) for t in full]

    def send_small(self, block):
        self._small = block
        self._small_started = _exchange_start("small_grads_start", _plan_to_all, 7, [block],
                                              [jax.ShapeDtypeStruct((8,) + block.shape, block.dtype)], self.zero)
        self.zero = self._small_started[3]

    def small_blocks(self, after):
        x, y, c = _place()
        (landed,) = _exchange_wait("small_grads_wait", _plan_to_all, [self._small], self._small_started, after)
        return lax.dynamic_update_index_in_dim(landed, self._small, 4 * x + 2 * y + c, 0)


def _rope_tables(positions):
    half = HEAD // 2
    inv_freq = jnp.float32(ROPE_THETA) ** (-(jnp.arange(half, dtype=F32) * 2.0 / HEAD))
    ang = positions.astype(F32)[:, None] * inv_freq
    cos, sin = jnp.cos(ang), jnp.sin(ang)
    return jnp.tile(cos, (1, 4)), jnp.tile(jnp.concatenate([-sin, sin], axis=1), (1, 2))


def _local_step(x, mem, positions, target, gains, ex):
    g_pre_mix, g_mem, g_a, g_c, g_x, g_post_mix, g_pre_mlp, g_post_mlp = gains
    tm = ROW_TILE
    cos, sin = _rope_tables(positions)
    h = _pre_norm(x, g_pre_mix, ex.zero, tm)
    w_in = ex.w_in([h, cos, sin])

    q, k, v, bcu, qx = _in_proj_fwd(h, w_in, cos, sin, ex.zero, tm)
    ya, lse = _attn_fwd(q, k, v)
    (w_kv, w_out), conv_w = ex.rest_weights(lse)
    w_kv, w_out = (w.reshape(N_CHIPS * w.shape[1], w.shape[2]) for w in (w_kv, w_out))
    memn, mkv = _memkv_fwd(mem, g_mem, w_kv, ex.zero)
    yx, ycat, y2, x1 = _mix_fwd(ya, bcu, qx, mkv, conv_w, g_a, g_c, g_x, w_out, g_post_mix, x, tm)
    w_up, w_down = ex.up_down(x1)
    w_down = w_down.reshape(N_CHIPS * w_down.shape[1], w_down.shape[2])
    h2, f, du, df2, dx1, dg_pre_mlp, dg_post_mlp, loss = _mlp_fwd_bwd(x1, target, g_pre_mlp, g_post_mlp, w_up, w_down,
                                                                      MLP_ROW_TILE)
    gw_down = _weight_grad("grad_w_down", f, df2, True, ex.zero)
    gw_up = _weight_grad("grad_w_up", h2, du, False, ex.zero)
    ex.send_grads("early", [gw_up, gw_down])

    gw_out, dya, delta, tail, dmkv, g_conv, dg_post_mix, dg_a, dg_c, dg_x = _mixer_bwd(
        dx1, y2, ycat, ya, yx, bcu, qx, mkv, conv_w, g_a, g_c, g_x, w_out, g_post_mix, ex.zero, tm)
    ex.grads_at_sibling("early", dya)
    gw_kv, dg_mem = _memkv_bwd(mem, g_mem, w_kv, dmkv)
    ex.send_grads("mid", [gw_out, gw_kv])
    dqkv = _attn_bwd(q, k, v, dya, lse, delta, ex.zero)
    ex.grads_at_sibling("mid", dqkv[0])
    grad_x, gw_in, dg_pre_mix, early_sums = _in_proj_bwd(dqkv, tail, cos, sin, w_in, x, h, g_pre_mix, dx1, ex.zero, tm,
                                                         ex.final_sum_operands("early", dqkv[0]))
    ex.send_grads("late", [gw_in])
    gain_grads = [dg_pre_mix, dg_mem, dg_a, dg_c, dg_x, dg_post_mix, dg_pre_mlp, dg_post_mlp]
    ex.send_small(_pack_small(gain_grads, g_conv, loss))
    return grad_x, early_sums


def _pack_small(gains, conv, scalar):
    n, k = len(gains), conv.shape[0]

    def body(*refs):
        out_ref = refs[-1]
        out_ref[...] = jnp.zeros_like(out_ref)
        for i, g_ref in enumerate(refs[:n]):
            out_ref[i:i + 1, 0:g_ref.shape[1]] = g_ref[...]
        out_ref[n:n + k, 0:conv.shape[1]] = refs[n][...]
        out_ref[n + k:n + k + 1, 0:1] = refs[n + 1][...]

    return pl.pallas_call(body, name="pack_small", out_shape=jax.ShapeDtypeStruct((SMALL_ROWS, D_MODEL), F32))(
        *gains, conv, scalar)


def kernel(x, mem, positions, g_pre_mix, g_mem, w_in, w_mem_kv, conv_w, g_attn_out, g_conv_out, g_xattn_out, w_out, g_post_mix, g_pre_mlp, w_up, w_down, g_post_mlp, loss_target, m_g_pre_mix, m_g_mem, m_w_in, m_w_mem_kv, m_conv_w, m_g_attn_out, m_g_conv_out, m_g_xattn_out, m_w_out, m_g_post_mix, m_g_pre_mlp, m_w_up, m_w_down, m_g_post_mlp, v_g_pre_mix, v_g_mem, v_w_in, v_w_mem_kv, v_conv_w, v_g_attn_out, v_g_conv_out, v_g_xattn_out, v_w_out, v_g_post_mix, v_g_pre_mlp, v_w_up, v_w_down, v_g_post_mlp):
    chip = 2 * lax.axis_index("x") + lax.axis_index("y")
    gains = [g_pre_mix, g_mem, g_attn_out, g_conv_out, g_xattn_out, g_post_mix, g_pre_mlp, g_post_mlp]
    gains_m = [m_g_pre_mix, m_g_mem, m_g_attn_out, m_g_conv_out, m_g_xattn_out, m_g_post_mix, m_g_pre_mlp, m_g_post_mlp]
    gains_v = [v_g_pre_mix, v_g_mem, v_g_attn_out, v_g_conv_out, v_g_xattn_out, v_g_post_mix, v_g_pre_mlp, v_g_post_mlp]
    mats =[w_in[0], w_mem_kv[0], w_out[0], w_up[0], w_down[0]]
    mats_m = [m_w_in[0], m_w_mem_kv[0], m_w_out[0], m_w_up[0], m_w_down[0]]
    mats_v = [v_w_in[0], v_w_mem_kv[0], v_w_out[0], v_w_up[0], v_w_down[0]]

    ex = _StepExchanges(mats, conv_w[0])
    grad_x, early_sums = _local_step(x[0], mem[0], positions[0], loss_target[0], gains, ex)

    ex.send_sums("four", [*early_sums, *ex.grads_summed("mid", ex.zero)])
    ex.grads_at_sibling("late", ex.zero)
    up_sum, down_sum, out_sum, kv_sum = ex.whole_sums("four", ex.zero)
    params = lambda a, g: (mats[a], g, mats_m[a], mats_v[a])
    new_up, new_down = _adamw("adamw_up_down", [params(3, up_sum), params(4, down_sum)], ex.zero)
    new_out, new_kv = _adamw("adamw_out_kv", [params(2, out_sum), params(1, kv_sum)], ex.zero)

    ex.send_sums("last", ex.grads_summed("late", new_kv[1]))
    small, total = _small_update(ex.small_blocks(ex.zero), chip.reshape(1).astype(jnp.int32), gains, gains_m,
                                 gains_v, conv_w[0], m_conv_w[0], v_conv_w[0])
    (in_sum,) = ex.whole_sums("last", small[0][1])
    (new_in,) = _adamw("adamw_in", [params(0, in_sum)], in_sum)
    mat_new = [new_in, new_kv, new_out, new_up, new_down]

    order = ["g_pre_mix", "g_mem", "w_in", "w_mem_kv", "conv_w", "g_attn_out", "g_conv_out", "g_xattn_out", "w_out",
             "g_post_mix", "g_pre_mlp", "w_up", "w_down", "g_post_mlp"]
    gain_names = ["g_pre_mix", "g_mem", "g_attn_out", "g_conv_out", "g_xattn_out", "g_post_mix", "g_pre_mlp", "g_post_mlp"]
    mat_names = ["w_in", "w_mem_kv", "w_out", "w_up", "w_down"]

    def leaf(kind, name):
        if name in gain_names:
            return small[gain_names.index(name)][kind]
        if name == "conv_w":
            return jnp.swapaxes(small[len(gain_names)][kind], 0, 1)
        return mat_new[mat_names.index(name)][kind][None]

    return (total[0, 0], grad_x[None], *[leaf(kind, name) for kind in range(4) for name in order])
```

```python
import jax
import jax.numpy as jnp
from jax import lax
from jax.experimental import pallas as pl
from jax.experimental.pallas import tpu as pltpu

F32, BF16 = jnp.float32, jnp.bfloat16

D_MODEL = 1024
ATTN_W = 512
CONV_W = 256
XATTN_W = 256
PROJ_W = 3 * ATTN_W + 3 * CONV_W + XATTN_W
D_FF = 4096
HEAD = 64
N_BACK = 128
DILATIONS = (1, 4, 16)
PATTERN_ORDER = DILATIONS[::-1]
ROPE_THETA = 10000.0
EPS = 1e-6
NEG_INF = -1e30
SCALE = HEAD ** -0.5
N_CHIPS = 4
SHARD_IN = PROJ_W // N_CHIPS
SHARD_FF = D_FF // N_CHIPS

ADAM_LR, ADAM_B1, ADAM_B2, ADAM_EPS, ADAM_WD, ADAM_STEP = 0.001, 0.9, 0.999, 1e-08, 0.01, 10

VMEM_LIMIT_V7X = 56 * 1024 * 1024
ROW_TILE = 512
MLP_ROW_TILE = 256
ADAMW_ROW_TILE = 256
SMALL_ROWS = 16

NT = (((1,), (1,)), ((), ()))
TN = (((0,), (0,)), ((), ()))
MESH = pl.DeviceIdType.MESH


def _params(*sem):
    return pltpu.CompilerParams(dimension_semantics=sem, vmem_limit_bytes=VMEM_LIMIT_V7X)


def _resident(shape):
    return pl.BlockSpec(shape, lambda *_: (0,) * len(shape), pipeline_mode=pl.Buffered(1))


def _rows(tm, width):
    return pl.BlockSpec((tm, width), lambda i, *_: (i, 0))


def _rms_hat(x):
    r = lax.rsqrt(jnp.mean(x * x, axis=-1, keepdims=True) + EPS)
    return x * r, r


def _rms_bwd(xhat, r, g, dy):
    gdy = dy * g
    return r * (gdy - xhat * jnp.mean(xhat * gdy, axis=-1, keepdims=True))


def _rope128(t, cos, sin_signed, inverse):
    lane = lax.broadcasted_iota(jnp.int32, t.shape, 1)
    first_half = (lane % HEAD) < (HEAD // 2)
    rot = jnp.where(first_half, pltpu.roll(t, 128 - HEAD // 2, 1), pltpu.roll(t, HEAD // 2, 1))
    return t * cos - rot * sin_signed if inverse else t * cos + rot * sin_signed


def _pre_norm(x, g, after, tm):
    S = x.shape[0]

    def body(x_ref, g_ref, after_ref, h_ref):
        h_ref[...] = (_rms_hat(x_ref[...])[0] * g_ref[...]).astype(BF16)

    return pl.pallas_call(
        body, name="pre_norm", grid=(S // tm,),
        in_specs=[_rows(tm, D_MODEL), _resident((1, D_MODEL)), pl.BlockSpec(memory_space=pl.ANY)],
        out_specs=_rows(tm, D_MODEL), out_shape=jax.ShapeDtypeStruct((S, D_MODEL), BF16),
        compiler_params=_params("parallel"),
    )(x, g, after)


def _side_by_side(w_hbm, w_full, sems):
    width = w_hbm.shape[2]

    @pl.when(pl.program_id(0) == 0)
    def _():
        copies = [pltpu.make_async_copy(w_hbm.at[j], w_full.at[:, pl.ds(width * j, width)], sems.at[j])
                  for j in range(N_CHIPS)]
        for cp in copies:
            cp.start()
        for cp in copies:
            cp.wait()


def _in_proj_fwd(h, w_in, cos, sin, after, tm):
    S = h.shape[0]

    def body(h_ref, w_hbm, cos_ref, sin_ref, after_ref, q_ref, k_ref, v_ref, bcu_ref, qx_ref, proj, w_full, sems):
        _side_by_side(w_hbm, w_full, sems)
        proj[...] = jnp.dot(h_ref[...], w_full[...], preferred_element_type=F32)
        c, s = cos_ref[...], sin_ref[...]
        for j in range(ATTN_W // 128):
            lo = 128 * j
            q_ref[:, lo:lo + 128] = _rope128(proj[:, lo:lo + 128], c, s, False) * SCALE
            k_ref[:, lo:lo + 128] = _rope128(proj[:, ATTN_W + lo:ATTN_W + lo + 128], c, s, False)
        v_ref[...] = proj[:, 2 * ATTN_W:3 * ATTN_W]
        bcu_ref[...] = proj[:, 3 * ATTN_W:3 * ATTN_W + 3 * CONV_W]
        qx_ref[...] = proj[:, 3 * ATTN_W + 3 * CONV_W:PROJ_W].astype(BF16)

    return pl.pallas_call(
        body, name="in_proj_fwd", grid=(S // tm,),
        in_specs=[_rows(tm, D_MODEL), pl.BlockSpec(memory_space=pl.ANY), _rows(tm, 128), _rows(tm, 128),
                  pl.BlockSpec(memory_space=pl.ANY)],
        out_specs=[_rows(tm, ATTN_W), _rows(tm, ATTN_W), _rows(tm, ATTN_W), _rows(tm, 3 * CONV_W), _rows(tm, XATTN_W)],
        out_shape=[jax.ShapeDtypeStruct((S, ATTN_W), F32), jax.ShapeDtypeStruct((S, ATTN_W), F32),
                   jax.ShapeDtypeStruct((S, ATTN_W), F32), jax.ShapeDtypeStruct((S, 3 * CONV_W), F32),
                   jax.ShapeDtypeStruct((S, XATTN_W), BF16)],
        scratch_shapes=[pltpu.VMEM((tm, PROJ_W), F32), pltpu.VMEM((D_MODEL, PROJ_W), BF16),
                        pltpu.SemaphoreType.DMA((N_CHIPS,))],
        compiler_params=_params("arbitrary"),
    )(h, w_in, cos, sin, after)


def _memkv_fwd(mem, g_mem, w_kv, after):
    n_mem = mem.shape[0]

    def body(mem_ref, g_ref, w_ref, after_ref, mn_ref, kv_ref):
        mhat, _ = _rms_hat(mem_ref[...])
        mn = (mhat * g_ref[...]).astype(BF16)
        mn_ref[...] = mn
        kv_ref[...] = jnp.dot(mn, w_ref[...], preferred_element_type=F32).astype(BF16)

    vmem = pl.BlockSpec(memory_space=pltpu.VMEM)
    return pl.pallas_call(
        body, name="memkv_fwd", in_specs=[vmem, vmem, vmem, pl.BlockSpec(memory_space=pl.ANY)], out_specs=[vmem, vmem],
        out_shape=[jax.ShapeDtypeStruct((n_mem, D_MODEL), BF16), jax.ShapeDtypeStruct((n_mem, 2 * XATTN_W), BF16)],
        compiler_params=pltpu.CompilerParams(vmem_limit_bytes=VMEM_LIMIT_V7X),
    )(mem, g_mem, w_kv, after)


def _fill_band_bias(bias):
    row = lax.broadcasted_iota(jnp.int32, (N_BACK, 2 * N_BACK), 0)
    col = lax.broadcasted_iota(jnp.int32, (N_BACK, 2 * N_BACK), 1)
    band = (col >= row) & (col <= row + N_BACK)
    bias[1] = jnp.where(band, 0.0, NEG_INF)
    bias[0] = jnp.where(band & (col >= N_BACK), 0.0, NEG_INF)


def _strided(start, size, d):
    return pl.ds(start, size) if d == 1 else pl.ds(start, size, stride=d)


def _group_starts(g, G, nb, d):
    t0 = g * G
    r, n0 = lax.shift_right_logical(t0, nb.bit_length() - 1), lax.bitwise_and(t0, nb - 1)
    first = r + n0 * (N_BACK * d)
    before = r + jnp.maximum(n0 - 1, 0) * (N_BACK * d)
    starts = [before] + [first + u * (N_BACK * d) for u in range(G)]
    if d == 1:
        starts = [pl.multiple_of(st, N_BACK) for st in starts]
    return starts, n0


def _step_blocks(i, U, nb, d):
    G = min(U, nb)
    whole = G == nb
    row_blocks, blocks = [], []
    for grp in range(U // G):
        starts, n0 = _group_starts(i * (U // G) + grp, G, nb, d)
        base = len(row_blocks)
        if whole:
            row_blocks += [_strided(st, N_BACK, d) for st in starts[1:]]
            blocks += [(base + max(u - 1, 0), base + u, min(u, 1)) for u in range(G)]
        else:
            row_blocks += [_strided(st, N_BACK, d) for st in starts]
            blocks += [(base + u, base + u + 1, jnp.minimum(n0, 1) if u == 0 else 1) for u in range(G)]
    return row_blocks, blocks


def _by_head(a, b):
    lane = lax.broadcasted_iota(jnp.int32, (a.shape[0], 2 * HEAD), 1)
    return jnp.where(lane < HEAD, a, b)


def _head_only(t, hh):
    lane = lax.broadcasted_iota(jnp.int32, t.shape, 1)
    return jnp.where((lane < HEAD) == (hh == 0), t, jnp.zeros_like(t))


def _stack_heads(t):
    return jnp.concatenate([_head_only(t, 0), _head_only(t, 1)], axis=0)


def _head_columns(t):
    return jnp.concatenate([t[:, 0:1], t[:, HEAD:HEAD + 1]], axis=0)


def _unstack(t):
    return _by_head(t[:N_BACK], t[N_BACK:])


def _unstack_columns(t):
    return _by_head(jnp.broadcast_to(t[:N_BACK], (N_BACK, 2 * HEAD)), jnp.broadcast_to(t[N_BACK:], (N_BACK, 2 * HEAD)))


FWD_BLOCKS_PER_STEP = 4
BWD_BLOCKS_PER_STEP = 4
BWD_CHUNK = 64


def _attn_fwd(q, k, v):
    S = q.shape[0]
    U = FWD_BLOCKS_PER_STEP

    def body(q_ref, k_ref, v_ref, y_ref, m_ref, l_scr, bias):
        _fill_band_bias(bias)
        for g, d in enumerate(PATTERN_ORDER):
            nb = S // d // N_BACK
            first_pattern, last_pattern = g == 0, g == len(PATTERN_ORDER) - 1

            def step(i, carry, d=d, nb=nb, first_pattern=first_pattern, last_pattern=last_pattern):
                row_blocks, blocks = _step_blocks(i, U, nb, d)
                kb = [k_ref[r, :].astype(BF16) for r in row_blocks]
                ss = []
                for before, own, which in blocks:
                    kw = jnp.concatenate([kb[before], kb[own]], 0)
                    qs = _stack_heads(q_ref[row_blocks[own], :].astype(BF16))
                    b = bias[which]
                    ss.append(lax.dot_general(qs, kw, NT, preferred_element_type=F32) + jnp.concatenate([b, b], axis=0))
                ms = [jnp.max(s, axis=1, keepdims=True) for s in ss]
                ps = [jnp.exp(s - m) for s, m in zip(ss, ms)]
                ls = [jnp.sum(p, axis=1, keepdims=True) for p in ps]
                vb = [v_ref[r, :].astype(BF16) for r in row_blocks]
                os_ = [jnp.dot(ps[u].astype(BF16), jnp.concatenate([vb[before], vb[own]], 0), preferred_element_type=F32)
                       for u, (before, own, _) in enumerate(blocks)]
                for u, (_, own, _) in enumerate(blocks):
                    o_g, m_g, l_g = _unstack(os_[u]), _unstack_columns(ms[u]), _unstack_columns(ls[u])
                    r = row_blocks[own]
                    if first_pattern:
                        m_new, l_new, acc = m_g, l_g, o_g
                    else:
                        m_old = m_ref[r, :]
                        m_new = jnp.maximum(m_old, m_g)
                        alpha, beta = jnp.exp(m_old - m_new), jnp.exp(m_g - m_new)
                        l_new = l_scr[r, :] * alpha + l_g * beta
                        acc = y_ref[r, :] * alpha + o_g * beta
                    if last_pattern:
                        y_ref[r, :] = acc / l_new
                        m_ref[r, :] = m_new + jnp.log(l_new)
                    else:
                        y_ref[r, :] = acc
                        m_ref[r, :] = m_new
                        l_scr[r, :] = l_new
                return carry

            lax.fori_loop(0, d * nb // U, step, 0)

    col = pl.BlockSpec((S, 2 * HEAD), lambda j: (0, j))
    return pl.pallas_call(
        body, name="attn_fwd", grid=(q.shape[1] // (2 * HEAD),),
        in_specs=[col, col, col], out_specs=[col, col],
        out_shape=[jax.ShapeDtypeStruct(q.shape, F32)] * 2,
        scratch_shapes=[pltpu.VMEM((S, 2 * HEAD), F32), pltpu.VMEM((2, N_BACK, 2 * N_BACK), F32)],
        compiler_params=_params("parallel"),
    )(q, k, v)


def _attn_bwd(q, k, v, dy, lse, delta, after):
    S = q.shape[0]
    U = BWD_BLOCKS_PER_STEP

    def body(q_ref, k_ref, v_ref, dy_ref, lse_ref, delta_ref, after_ref, dq_ref, dk_ref, dv_ref, bias):
        _fill_band_bias(bias)
        nb_first = S // PATTERN_ORDER[0] // N_BACK
        first_writes_all = min(U, nb_first) == nb_first
        if not first_writes_all:
            dk_ref[...] = jnp.zeros_like(dk_ref)
            dv_ref[...] = jnp.zeros_like(dv_ref)
        for g, d in enumerate(PATTERN_ORDER):
            nb = S // d // N_BACK

            def step(i, carry, d=d, nb=nb, g=g):
                row_blocks, blocks = _step_blocks(i, U, nb, d)
                kb = [k_ref[r, :].astype(BF16) for r in row_blocks]
                vb = [v_ref[r, :].astype(BF16) for r in row_blocks]
                kws = [jnp.concatenate([kb[before], kb[own]], 0) for before, own, _ in blocks]
                vws = [jnp.concatenate([vb[before], vb[own]], 0) for before, own, _ in blocks]
                qss = [_stack_heads(q_ref[row_blocks[own], :].astype(BF16)) for _, own, _ in blocks]
                doss = [_stack_heads(dy_ref[row_blocks[own], :].astype(BF16)) for _, own, _ in blocks]
                ss = [lax.dot_general(qss[u], kws[u], NT, preferred_element_type=F32) for u in range(U)]
                dps = [lax.dot_general(doss[u], vws[u], NT, preferred_element_type=F32) for u in range(U)]
                pbs, dss = [], []
                for u, (_, own, which) in enumerate(blocks):
                    lse_c = _head_columns(lse_ref[row_blocks[own], :])
                    delta_c = _head_columns(delta_ref[row_blocks[own], :])
                    p_parts, ds_parts = [], []
                    for r0 in range(0, 2 * N_BACK, BWD_CHUNK):
                        r = slice(r0, r0 + BWD_CHUNK)
                        mask = bias[which, r0 % N_BACK:r0 % N_BACK + BWD_CHUNK, :]
                        p_r = jnp.exp(ss[u][r] + mask - lse_c[r])
                        p_parts.append(p_r.astype(BF16))
                        ds_parts.append((p_r * (dps[u][r] - delta_c[r])).astype(BF16))
                    pbs.append(jnp.concatenate(p_parts, axis=0))
                    dss.append(jnp.concatenate(ds_parts, axis=0))
                dqs = [jnp.dot(dss[u], kws[u], preferred_element_type=F32) for u in range(U)]
                dkws = [lax.dot_general(dss[u], qss[u], TN, preferred_element_type=F32) for u in range(U)]
                dvws = [lax.dot_general(pbs[u], doss[u], TN, preferred_element_type=F32) for u in range(U)]
                dk_parts, dv_parts = [None] * len(row_blocks), [None] * len(row_blocks)
                for u, (before, own, _) in enumerate(blocks):
                    dq = _unstack(dqs[u])
                    if g == 0:
                        dq_ref[row_blocks[own], :] = dq
                    else:
                        dq_ref[row_blocks[own], :] += dq
                    for idx, dkp, dvp in ((before, dkws[u][:N_BACK], dvws[u][:N_BACK]),
                                          (own, dkws[u][N_BACK:], dvws[u][N_BACK:])):
                        dk_parts[idx] = dkp if dk_parts[idx] is None else dk_parts[idx] + dkp
                        dv_parts[idx] = dvp if dv_parts[idx] is None else dv_parts[idx] + dvp
                for idx, r in enumerate(row_blocks):
                    if g == 0 and first_writes_all:
                        dk_ref[r, :] = dk_parts[idx]
                        dv_ref[r, :] = dv_parts[idx]
                    else:
                        dk_ref[r, :] += dk_parts[idx]
                        dv_ref[r, :] += dv_parts[idx]
                return carry

            lax.fori_loop(0, d * nb // U, step, 0)

    col = pl.BlockSpec((S, 2 * HEAD), lambda j: (0, j))
    return pl.pallas_call(
        body, name="attn_bwd", grid=(q.shape[1] // (2 * HEAD),),
        in_specs=[col] * 6 + [pl.BlockSpec(memory_space=pl.ANY)], out_specs=[col] * 3,
        out_shape=[jax.ShapeDtypeStruct(q.shape, F32)] * 3,
        scratch_shapes=[pltpu.VMEM((2, N_BACK, 2 * N_BACK), F32)],
        compiler_params=_params("parallel"),
    )(q, k, v, dy, lse, delta, after)


def _shift_down(z, before, k):
    row = lax.broadcasted_iota(jnp.int32, z.shape, 0)
    out = pltpu.roll(z, k, 0)
    for i in range(k):
        out = jnp.where(row == i, before[8 - k + i:8 - k + i + 1, :], out)
    return out


def _shift_up(z, after, k):
    rows = z.shape[0]
    row = lax.broadcasted_iota(jnp.int32, z.shape, 0)
    out = pltpu.roll(z, rows - k, 0)
    for i in range(k):
        out = jnp.where(row == rows - k + i, after[i:i + 1, :], out)
    return out


def _conv_fwd(bcu, before, is_first, w):
    b, c, u = bcu[:, 0:CONV_W], bcu[:, CONV_W:2 * CONV_W], bcu[:, 2 * CONV_W:3 * CONV_W]
    z = c * u
    zb = jnp.where(is_first, 0.0, before[:, CONV_W:2 * CONV_W] * before[:, 2 * CONV_W:3 * CONV_W])
    z1, z2 = _shift_down(z, zb, 1), _shift_down(z, zb, 2)
    cv = w[0:1, :] * z2 + w[1:2, :] * z1 + w[2:3, :] * z
    return b, c, u, z, z1, z2, cv


def _halo_before(tm, width):
    return pl.BlockSpec((8, width), lambda i: (jnp.maximum(i * (tm // 8) - 1, 0), 0))


def _mix_fwd(ya, bcu, qx, mkv, conv_w, g_a, g_c, g_x, w_out, g_post, x, tm):
    S = x.shape[0]

    def body(ya_ref, bcu_ref, before_ref, qx_ref, mkv_ref, cw_ref, ga_ref, gc_ref, gx_ref,
             wo_ref, gp_ref, x_ref, yx_ref, ycat_ref, y2_ref, x1_ref):
        ya = ya_ref[...]
        b, _, _, _, _, _, cv = _conv_fwd(bcu_ref[...], before_ref[...], pl.program_id(0) == 0, cw_ref[...])
        yc = b * cv

        qxb, mkvb = qx_ref[...], mkv_ref[...]
        heads = [slice(HEAD * hd, HEAD * (hd + 1)) for hd in range(XATTN_W // HEAD)]
        ss = [lax.dot_general(qxb[:, sl], mkvb[:, sl], NT, preferred_element_type=F32) * SCALE for sl in heads]
        ms = [jnp.max(s, axis=1, keepdims=True) for s in ss]
        ps = [jnp.exp(s - m) for s, m in zip(ss, ms)]
        ls = [jnp.sum(p, axis=1, keepdims=True) for p in ps]
        os_ = [jnp.dot(p.astype(BF16), mkvb[:, XATTN_W + sl.start:XATTN_W + sl.stop], preferred_element_type=F32)
               for p, sl in zip(ps, heads)]
        for sl, o, l in zip(heads, os_, ls):
            yx_ref[:, sl] = o / l
        yx = yx_ref[...]

        ycat_ref[:, 0:ATTN_W] = (_rms_hat(ya)[0] * ga_ref[...]).astype(BF16)
        ycat_ref[:, ATTN_W:ATTN_W + CONV_W] = (_rms_hat(yc)[0] * gc_ref[...]).astype(BF16)
        ycat_ref[:, ATTN_W + CONV_W:D_MODEL] = (_rms_hat(yx)[0] * gx_ref[...]).astype(BF16)
        y2 = jnp.dot(ycat_ref[...], wo_ref[...], preferred_element_type=F32)
        y2_ref[...] = y2
        x1_ref[...] = x_ref[...] + _rms_hat(y2)[0] * gp_ref[...]

    n_mem = mkv.shape[0]
    return pl.pallas_call(
        body, name="mix_fwd", grid=(S // tm,),
        in_specs=[_rows(tm, ATTN_W), _rows(tm, 3 * CONV_W), _halo_before(tm, 3 * CONV_W), _rows(tm, XATTN_W),
                  _resident((n_mem, 2 * XATTN_W)), _resident((3, CONV_W)), _resident((1, ATTN_W)),
                  _resident((1, CONV_W)), _resident((1, XATTN_W)), _resident((D_MODEL, D_MODEL)),
                  _resident((1, D_MODEL)), _rows(tm, D_MODEL)],
        out_specs=[_rows(tm, XATTN_W), _rows(tm, D_MODEL), _rows(tm, D_MODEL), _rows(tm, D_MODEL)],
        out_shape=[jax.ShapeDtypeStruct((S, XATTN_W), F32), jax.ShapeDtypeStruct((S, D_MODEL), BF16),
                   jax.ShapeDtypeStruct((S, D_MODEL), F32), jax.ShapeDtypeStruct((S, D_MODEL), F32)],
        compiler_params=_params("parallel"),
    )(ya, bcu, bcu, qx, mkv, conv_w, g_a, g_c, g_x, w_out, g_post, x)


def _mlp_fwd_bwd(x1, target, g_pre, g_post, w_up, w_down, tm):
    S = x1.shape[0]
    n_ff = D_FF // SHARD_FF

    def body(x1_ref, t_ref, gpre_ref, gpost_ref, wup_ref, wdn_ref,
             h2_ref, f_ref, du_ref, df2_ref, dx1_ref, dgpre_ref, dgpost_ref, loss_ref, u_scr):
        @pl.when(pl.program_id(0) == 0)
        def _():
            dgpre_ref[...] = jnp.zeros_like(dgpre_ref)
            dgpost_ref[...] = jnp.zeros_like(dgpost_ref)
            loss_ref[...] = jnp.zeros_like(loss_ref)

        x1 = x1_ref[...]
        x1hat, r1 = _rms_hat(x1)
        h2 = (x1hat * gpre_ref[...]).astype(BF16)
        h2_ref[...] = h2
        f2 = jnp.zeros((tm, D_MODEL), F32)
        for j in range(n_ff):
            cols = slice(SHARD_FF * j, SHARD_FF * (j + 1))
            u = jnp.maximum(jnp.dot(h2, wup_ref[j], preferred_element_type=F32), 0.0)
            u_scr[:, cols] = u
            f = (u * u).astype(BF16)
            f_ref[:, cols] = f
            f2 = f2 + jnp.dot(f, wdn_ref[cols, :], preferred_element_type=F32)
        f2hat, r2 = _rms_hat(f2)
        err = x1 + f2hat * gpost_ref[...] - t_ref[...]
        loss_ref[...] += 0.5 * jnp.sum(jnp.mean(err * err, axis=-1, keepdims=True), axis=0, keepdims=True)
        dx2 = err * (1.0 / D_MODEL)
        dgpost_ref[...] += jnp.sum(dx2 * f2hat, axis=0, keepdims=True)
        df2 = _rms_bwd(f2hat, r2, gpost_ref[...], dx2).astype(BF16)
        df2_ref[...] = df2
        dh2 = jnp.zeros((tm, D_MODEL), F32)
        for j in range(n_ff):
            cols = slice(SHARD_FF * j, SHARD_FF * (j + 1))
            df = lax.dot_general(df2, wdn_ref[cols, :], NT, preferred_element_type=F32)
            du = (2.0 * u_scr[:, cols] * df).astype(BF16)
            du_ref[:, cols] = du
            dh2 = dh2 + lax.dot_general(du, wup_ref[j], NT, preferred_element_type=F32)
        dgpre_ref[...] += jnp.sum(dh2 * x1hat, axis=0, keepdims=True)
        dx1_ref[...] = dx2 + _rms_bwd(x1hat, r1, gpre_ref[...], dh2)

    acc = pl.BlockSpec((1, D_MODEL), lambda i: (0, 0))
    return pl.pallas_call(
        body, name="mlp_fwd_bwd", grid=(S // tm,),
        in_specs=[_rows(tm, D_MODEL), _rows(tm, D_MODEL), _resident((1, D_MODEL)), _resident((1, D_MODEL)),
                  _resident((n_ff, D_MODEL, SHARD_FF)), _resident((D_FF, D_MODEL))],
        out_specs=[_rows(tm, D_MODEL), _rows(tm, D_FF), _rows(tm, D_FF), _rows(tm, D_MODEL), _rows(tm, D_MODEL),
                   acc, acc, pl.BlockSpec((1, 1), lambda i: (0, 0))],
        out_shape=[jax.ShapeDtypeStruct((S, D_MODEL), BF16), jax.ShapeDtypeStruct((S, D_FF), BF16),
                   jax.ShapeDtypeStruct((S, D_FF), BF16), jax.ShapeDtypeStruct((S, D_MODEL), BF16),
                   jax.ShapeDtypeStruct((S, D_MODEL), F32), jax.ShapeDtypeStruct((1, D_MODEL), F32),
                   jax.ShapeDtypeStruct((1, D_MODEL), F32), jax.ShapeDtypeStruct((1, 1), F32)],
        scratch_shapes=[pltpu.VMEM((tm, D_FF), F32)],
        compiler_params=_params("arbitrary"),
    )(x1, target, g_pre, g_post, w_up, w_down)


def _weight_grad(name, a, b, rows_sharded, after):
    S, K = a.shape
    N = b.shape[1]
    if rows_sharded:
        tk, tn = K // N_CHIPS, N
        a_spec = pl.BlockSpec((S, tk), lambda j: (0, j))
        b_spec = pl.BlockSpec((S, tn), lambda j: (0, 0), pipeline_mode=pl.Buffered(1))
    else:
        tk, tn = K, N // N_CHIPS
        a_spec = pl.BlockSpec((S, tk), lambda j: (0, 0), pipeline_mode=pl.Buffered(1))
        b_spec = pl.BlockSpec((S, tn), lambda j: (0, j))
    half = tk // 2

    def body(a_ref, b_ref, after_ref, o_ref):
        res = lax.dot_general(a_ref[...], b_ref[...], TN, preferred_element_type=F32)
        o_ref[0, 0] = res[:half]
        o_ref[1, 0] = res[half:]

    return pl.pallas_call(
        body, name=name, grid=(N_CHIPS,), in_specs=[a_spec, b_spec, pl.BlockSpec(memory_space=pl.ANY)],
        out_specs=pl.BlockSpec((2, 1, half, tn), lambda j: (0, j, 0, 0)),
        out_shape=jax.ShapeDtypeStruct((2, N_CHIPS, half, tn), F32),
        compiler_params=_params("parallel"),
    )(a, b, after)


def _mixer_bwd(dx1, y2, ycat, ya, yx, bcu, qx, mkv, conv_w, g_a, g_c, g_x, w_out, g_post, after, tm):
    S = dx1.shape[0]
    n_mem = mkv.shape[0]
    n_tiles = S // tm
    half = D_MODEL // N_CHIPS // 2

    def body(dx1_ref, y2_ref, ycat_ref, ya_ref, yx_ref, bcu_ref, before_ref, qx_ref, mkv_ref, cw_ref, ga_ref, gc_ref,
             gx_ref, wo_ref, gp_ref, after_ref, gwo_ref, dya_ref, delta_ref, tail_ref, dmkv_ref, dcw_ref, dgp_ref,
             dga_ref, dgc_ref, dgx_ref, carry):
        step = pl.program_id(0)
        first_tile = step == n_tiles - 1

        @pl.when(step == 0)
        def _():
            for ref in (gwo_ref, dmkv_ref, dcw_ref, dgp_ref, dga_ref, dgc_ref, dgx_ref, carry):
                ref[...] = jnp.zeros_like(ref)

        dx1 = dx1_ref[...]
        y2hat, r2 = _rms_hat(y2_ref[...])
        dgp_ref[...] += jnp.sum(dx1 * y2hat, axis=0, keepdims=True)
        dy2 = _rms_bwd(y2hat, r2, gp_ref[...], dx1).astype(BF16)
        gwo = lax.dot_general(ycat_ref[...], dy2, TN, preferred_element_type=F32)
        for k in range(2 * N_CHIPS):
            gwo_ref[k % 2, k // 2] += gwo[half * k:half * (k + 1)]
        dycat = lax.dot_general(dy2, wo_ref[...], NT, preferred_element_type=F32)

        d_na = dycat[:, 0:ATTN_W]
        ya = ya_ref[...]
        yahat, ra = _rms_hat(ya)
        dga_ref[...] += jnp.sum(d_na * yahat, axis=0, keepdims=True)
        dya = _rms_bwd(yahat, ra, ga_ref[...], d_na)
        dya_ref[...] = dya
        prod = dya * ya
        hi = prod.astype(BF16)
        lo = (prod - hi.astype(F32)).astype(BF16)
        head_of = lambda axis: lax.shift_right_logical(lax.broadcasted_iota(jnp.int32, (ATTN_W, ATTN_W), axis),
                                                       HEAD.bit_length() - 1)
        ones = jnp.where(head_of(0) == head_of(1), 1.0, 0.0).astype(BF16)
        delta_ref[...] = jnp.dot(hi, ones, preferred_element_type=F32) + jnp.dot(lo, ones, preferred_element_type=F32)

        w = cw_ref[...]
        b, c, u, z, z1, z2, cv = _conv_fwd(bcu_ref[...], before_ref[...], first_tile, w)
        d_nc = dycat[:, ATTN_W:ATTN_W + CONV_W]
        ychat, rc = _rms_hat(b * cv)
        dgc_ref[...] += jnp.sum(d_nc * ychat, axis=0, keepdims=True)
        dyc = _rms_bwd(ychat, rc, gc_ref[...], d_nc)
        dcv = dyc * b
        behind = carry[...]
        dz = w[2:3, :] * dcv + w[1:2, :] * _shift_up(dcv, behind, 1) + w[0:1, :] * _shift_up(dcv, behind, 2)
        carry[...] = dcv[0:8, :]
        dcw_ref[0:1, :] += jnp.sum(dcv * z2, axis=0, keepdims=True)
        dcw_ref[1:2, :] += jnp.sum(dcv * z1, axis=0, keepdims=True)
        dcw_ref[2:3, :] += jnp.sum(dcv * z, axis=0, keepdims=True)
        tail_ref[:, 0:CONV_W] = (dyc * cv).astype(BF16)
        tail_ref[:, CONV_W:2 * CONV_W] = (dz * u).astype(BF16)
        tail_ref[:, 2 * CONV_W:3 * CONV_W] = (dz * c).astype(BF16)

        d_nx = dycat[:, ATTN_W + CONV_W:D_MODEL]
        yxhat, rx = _rms_hat(yx_ref[...])
        dgx_ref[...] += jnp.sum(d_nx * yxhat, axis=0, keepdims=True)
        dyx = _rms_bwd(yxhat, rx, gx_ref[...], d_nx)
        qxb, mkvb = qx_ref[...], mkv_ref[...]
        heads = [slice(HEAD * hd, HEAD * (hd + 1)) for hd in range(XATTN_W // HEAD)]
        values = [slice(XATTN_W + sl.start, XATTN_W + sl.stop) for sl in heads]
        ss = [lax.dot_general(qxb[:, sl], mkvb[:, sl], NT, preferred_element_type=F32) * SCALE for sl in heads]
        es = [jnp.exp(s - jnp.max(s, axis=1, keepdims=True)) for s in ss]
        ps = [e / jnp.sum(e, axis=1, keepdims=True) for e in es]
        dobs = [dyx[:, sl].astype(BF16) for sl in heads]
        dps = [lax.dot_general(dob, mkvb[:, vsl], NT, preferred_element_type=F32) for dob, vsl in zip(dobs, values)]
        dss = [(p * (dp - jnp.sum(p * dp, axis=1, keepdims=True)) * SCALE).astype(BF16) for p, dp in zip(ps, dps)]
        for sl, vsl, p, dob, ds in zip(heads, values, ps, dobs, dss):
            tail_ref[:, 3 * CONV_W + sl.start:3 * CONV_W + sl.stop] = jnp.dot(
                ds, mkvb[:, sl], preferred_element_type=F32).astype(BF16)
            dmkv_ref[:, sl] += lax.dot_general(ds, qxb[:, sl], TN, preferred_element_type=F32)
            dmkv_ref[:, vsl] += lax.dot_general(p.astype(BF16), dob, TN, preferred_element_type=F32)

    rows = lambda width: pl.BlockSpec((tm, width), lambda i: (n_tiles - 1 - i, 0))
    before = pl.BlockSpec((8, 3 * CONV_W), lambda i: (jnp.maximum((n_tiles - 1 - i) * (tm // 8) - 1, 0), 0))
    acc = lambda r, w: pl.BlockSpec((r, w), lambda i: (0, 0))
    return pl.pallas_call(
        body, name="mixer_bwd", grid=(n_tiles,),
        in_specs=[rows(D_MODEL), rows(D_MODEL), rows(D_MODEL), rows(ATTN_W), rows(XATTN_W), rows(3 * CONV_W), before,
                  rows(XATTN_W), _resident((n_mem, 2 * XATTN_W)), _resident((3, CONV_W)), _resident((1, ATTN_W)),
                  _resident((1, CONV_W)), _resident((1, XATTN_W)), _resident((D_MODEL, D_MODEL)),
                  _resident((1, D_MODEL)), pl.BlockSpec(memory_space=pl.ANY)],
        out_specs=[pl.BlockSpec((2, N_CHIPS, half, D_MODEL), lambda i: (0, 0, 0, 0)), rows(ATTN_W), rows(ATTN_W),
                   rows(3 * CONV_W + XATTN_W), acc(n_mem, 2 * XATTN_W),
                   acc(3, CONV_W), acc(1, D_MODEL), acc(1, ATTN_W), acc(1, CONV_W), acc(1, XATTN_W)],
        out_shape=[jax.ShapeDtypeStruct((2, N_CHIPS, half, D_MODEL), F32), jax.ShapeDtypeStruct((S, ATTN_W), F32),
                   jax.ShapeDtypeStruct((S, ATTN_W), F32), jax.ShapeDtypeStruct((S, 3 * CONV_W + XATTN_W), BF16),
                   jax.ShapeDtypeStruct((n_mem, 2 * XATTN_W), F32), jax.ShapeDtypeStruct((3, CONV_W), F32),
                   jax.ShapeDtypeStruct((1, D_MODEL), F32), jax.ShapeDtypeStruct((1, ATTN_W), F32),
                   jax.ShapeDtypeStruct((1, CONV_W), F32), jax.ShapeDtypeStruct((1, XATTN_W), F32)],
        scratch_shapes=[pltpu.VMEM((8, CONV_W), F32)],
        compiler_params=_params("arbitrary"),
    )(dx1, y2, ycat, ya, yx, bcu, bcu, qx, mkv, conv_w, g_a, g_c, g_x, w_out, g_post, after)


def _memkv_bwd(mem, g_mem, w_kv, dmkv):
    n_mem = mem.shape[0]
    half = D_MODEL // N_CHIPS // 2

    def body(mem_ref, g_ref, w_ref, d_ref, dw_ref, dg_ref):
        mhat, _ = _rms_hat(mem_ref[...])
        mn = (mhat * g_ref[...]).astype(BF16)
        d = d_ref[...].astype(BF16)
        for k in range(2 * N_CHIPS):
            dw_ref[k % 2, k // 2] = lax.dot_general(mn[:, half * k:half * (k + 1)], d, TN, preferred_element_type=F32)
        dmn = lax.dot_general(d, w_ref[...], NT, preferred_element_type=F32)
        dg_ref[...] = jnp.sum(dmn * mhat, axis=0, keepdims=True)

    return pl.pallas_call(
        body, name="memkv_bwd",
        out_shape=[jax.ShapeDtypeStruct((2, N_CHIPS, half, 2 * XATTN_W), F32), jax.ShapeDtypeStruct((1, D_MODEL), F32)],
        compiler_params=pltpu.CompilerParams(vmem_limit_bytes=VMEM_LIMIT_V7X),
    )(mem, g_mem, w_kv, dmkv)


def _sum_of_partials(own_ref, sibling_ref, other_refs):
    acc = own_ref[0, 0] + sibling_ref[0]
    for ref in other_refs:
        acc = acc + ref[0].astype(F32)
    return acc


def _in_proj_bwd(dqkv, tail, cos, sin, w_in, x, h, g, dx1, after, tm, small, sums=None):
    S = x.shape[0]
    step_w = 2 * 256
    half = D_MODEL // 2
    n_steps = S // tm
    sum_grads, sum_sibling, sum_others, place = sums if sums is not None else ([], [], [], jnp.zeros((2,), jnp.int32))
    k = len(sum_grads)
    n_small = len(small)

    def body(place_ref, dq_ref, dk_ref, dv_ref, tail_ref, cos_ref, sin_ref, w_hbm, x_ref, h_ref, g_ref, dx1_ref,
             after_ref, *refs):
        sum_refs, small_refs, refs = refs[:5 * k], refs[5 * k:5 * k + n_small], refs[5 * k + n_small:]
        (dx_ref, gw_ref, dg_ref), sum_out_refs, small_ref = refs[:3], refs[3:3 + k], refs[3 + k]
        dproj_ref, w_full, sems = refs[4 + k:]
        for a in range(k):
            sum_out_refs[a][0] = _sum_of_partials(sum_refs[a], sum_refs[k + a], sum_refs[2 * k + 3 * a:2 * k + 3 * a + 3])
        _side_by_side(w_hbm, w_full, sems)

        @pl.when(pl.program_id(0) == 0)
        def _():
            dg_ref[...] = jnp.zeros_like(dg_ref)
            gw_ref[...] = jnp.zeros_like(gw_ref)

        halves = [slice(0, tm // 2), slice(tm // 2, tm)]
        for rows in halves:
            c, s = cos_ref[rows, :], sin_ref[rows, :]
            for j in range(ATTN_W // 128):
                cols = slice(128 * j, 128 * (j + 1))
                dproj_ref[rows, cols] = _rope128(dq_ref[rows, cols] * SCALE, c, s, True).astype(BF16)
                dproj_ref[rows, ATTN_W + 128 * j:ATTN_W + 128 * (j + 1)] = _rope128(dk_ref[rows, cols], c, s, True).astype(BF16)
            dproj_ref[rows, 2 * ATTN_W:3 * ATTN_W] = dv_ref[rows, :].astype(BF16)
            dproj_ref[rows, 3 * ATTN_W:PROJ_W] = tail_ref[rows, :]
        dhs = [lax.dot_general(dproj_ref[rows, :], w_full[...], NT, preferred_element_type=F32) for rows in halves]
        for rows, dh in zip(halves, dhs):
            xhat, r = _rms_hat(x_ref[rows, :])
            dg_ref[...] += jnp.sum(dh * xhat, axis=0, keepdims=True)
            dx_ref[rows, :] = dx1_ref[rows, :] + _rms_bwd(xhat, r, g_ref[...], dh)
        hb = h_ref[...]
        for step in range(PROJ_W // step_w):
            res = lax.dot_general(hb, dproj_ref[:, step * step_w:(step + 1) * step_w], TN, preferred_element_type=F32)
            lo = step * step_w
            while lo < (step + 1) * step_w:
                chip = lo // SHARD_IN
                hi = min((step + 1) * step_w, (chip + 1) * SHARD_IN)
                for hh in range(2):
                    gw_ref[hh, chip, :, lo - chip * SHARD_IN:hi - chip * SHARD_IN] += (
                        res[half * hh:half * (hh + 1), lo - step * step_w:hi - step * step_w])
                lo = hi

        @pl.when(pl.program_id(0) == n_steps - 1)
        def _():
            small_ref[...] = jnp.zeros_like(small_ref)
            small_ref[0:1, :] = dg_ref[...]
            row = 1
            for ref in small_refs:
                small_ref[row:row + ref.shape[0], 0:ref.shape[1]] = ref[...]
                row += ref.shape[0]

    whole = lambda shape: pl.BlockSpec(shape, lambda i, p: (0,) * len(shape))
    slab = lambda t: (1, t.shape[-2] // n_steps, t.shape[-1])
    sum_specs = ([pl.BlockSpec((1,) + slab(t), lambda i, p: (p[0], p[1], i, 0)) for t in sum_grads]
                 + [pl.BlockSpec(slab(t), lambda i, p: (p[1], i, 0)) for t in sum_grads]
                 + [pl.BlockSpec(slab(t), lambda i, p, j=j: (j, i, 0)) for t in sum_grads for j in range(3)])
    results = pl.pallas_call(
        body, name="in_proj_bwd",
        out_shape=[jax.ShapeDtypeStruct((S, D_MODEL), F32), jax.ShapeDtypeStruct((2, N_CHIPS, half, SHARD_IN), F32),
                   jax.ShapeDtypeStruct((1, D_MODEL), F32)]
        + [jax.ShapeDtypeStruct((2,) + t.shape[2:], F32) for t in sum_grads]
        + [jax.ShapeDtypeStruct((SMALL_ROWS, D_MODEL), F32)],
        grid_spec=pltpu.PrefetchScalarGridSpec(
            num_scalar_prefetch=1, grid=(n_steps,),
            in_specs=[_rows(tm, ATTN_W)] * 3 + [_rows(tm, PROJ_W - 3 * ATTN_W), _rows(tm, 128), _rows(tm, 128),
                      pl.BlockSpec(memory_space=pl.ANY), _rows(tm, D_MODEL), _rows(tm, D_MODEL),
                      _resident((1, D_MODEL)), _rows(tm, D_MODEL), pl.BlockSpec(memory_space=pl.ANY)] + sum_specs
            + [_resident(t.shape) for t in small],
            out_specs=[_rows(tm, D_MODEL), whole((2, N_CHIPS, half, SHARD_IN)), whole((1, D_MODEL))]
            + [pl.BlockSpec(slab(t), lambda i, p: (p[0], i, 0)) for t in sum_grads] + [whole((SMALL_ROWS, D_MODEL))],
            scratch_shapes=[pltpu.VMEM((tm, PROJ_W), BF16), pltpu.VMEM((D_MODEL, PROJ_W), BF16),
                            pltpu.SemaphoreType.DMA((N_CHIPS,))]),
        compiler_params=_params("arbitrary"),
    )(place, *dqkv, tail, cos, sin, w_in, x, h, g, dx1, after, *sum_grads, *sum_sibling,
      *[o for o in sum_others for _ in range(3)], *small)
    return [results[0], results[1], results[-1], list(results[3:-1])]


def _row_tile(rows):
    return ROW_TILE if rows % ROW_TILE == 0 else rows


def _chip_sums_bf16(name, grads, from_sibling, place):
    k = len(grads)
    _, n, rows, _ = grads[0].shape
    tr = _row_tile(rows)

    def body(place_ref, *refs):
        for g_ref, b_ref, o_ref in zip(refs[:k], refs[k:2 * k], refs[2 * k:]):
            o_ref[...] = (g_ref[0] + b_ref[...]).astype(BF16)

    mine = lambda g: pl.BlockSpec((1, 1, tr, g.shape[3]), lambda s, i, p: (p[0], s, i, 0))
    slab = lambda g: pl.BlockSpec((1, tr, g.shape[3]), lambda s, i, p: (s, i, 0))
    return pl.pallas_call(
        body, name=name, out_shape=[jax.ShapeDtypeStruct(g.shape[1:], BF16) for g in grads],
        grid_spec=pltpu.PrefetchScalarGridSpec(
            num_scalar_prefetch=1, grid=(n, rows // tr),
            in_specs=[mine(g) for g in grads] + [slab(g) for g in grads], out_specs=[slab(g) for g in grads]),
        compiler_params=_params("parallel", "parallel"),
    )(place, *grads, *from_sibling)


def _final_sums(name, grads, from_sibling, others, place):
    k = len(grads)
    rows = grads[0].shape[2]
    tr = _row_tile(rows)

    def body(place_ref, *refs):
        for a in range(k):
            refs[5 * k + a][0] = _sum_of_partials(refs[a], refs[k + a], refs[2 * k + 3 * a:2 * k + 3 * a + 3])

    own = lambda g: pl.BlockSpec((1, 1, tr, g.shape[3]), lambda i, p: (p[0], p[1], i, 0))
    sib = lambda g: pl.BlockSpec((1, tr, g.shape[3]), lambda i, p: (p[1], i, 0))
    other = lambda g, j: pl.BlockSpec((1, tr, g.shape[3]), lambda i, p: (j, i, 0))
    return pl.pallas_call(
        body, name=name, out_shape=[jax.ShapeDtypeStruct((2,) + g.shape[2:], F32) for g in grads],
        grid_spec=pltpu.PrefetchScalarGridSpec(
            num_scalar_prefetch=1, grid=(rows // tr,),
            in_specs=[own(g) for g in grads] + [sib(g) for g in grads] + [other(g, j) for g in grads for j in range(3)],
            out_specs=[pl.BlockSpec((1, tr, g.shape[3]), lambda i, p: (p[0], i, 0)) for g in grads]),
        compiler_params=_params("parallel"),
    )(place, *grads, *from_sibling, *[o for o in others for _ in range(3)])


def _adamw_update(w, g, m, v):
    m = ADAM_B1 * m + (1.0 - ADAM_B1) * g
    v = ADAM_B2 * v + (1.0 - ADAM_B2) * (g * g)
    m_hat = m * (1.0 / (1.0 - ADAM_B1 ** ADAM_STEP))
    v_hat = v * (1.0 / (1.0 - ADAM_B2 ** ADAM_STEP))
    return -ADAM_LR * (m_hat / (jnp.sqrt(v_hat) + ADAM_EPS) + ADAM_WD * w), m, v


def _adamw(name, params, after):
    k = len(params)
    rows = params[0][0].shape[0]
    tr = ADAMW_ROW_TILE if rows % ADAMW_ROW_TILE == 0 else rows

    def body(*refs):
        ins, outs = refs[:4 * k], refs[4 * k + 1:]
        for a in range(k):
            w_ref, g_ref, m_ref, v_ref = ins[4 * a:4 * a + 4]
            g = g_ref[...]
            outs[4 * a][...] = g
            outs[4 * a + 1][...], outs[4 * a + 2][...], outs[4 * a + 3][...] = _adamw_update(w_ref[...], g, m_ref[...], v_ref[...])

    spec = lambda w: pl.BlockSpec((tr, w.shape[1]), lambda i: (i, 0))
    out = pl.pallas_call(
        body, name=name, grid=(rows // tr,),
        in_specs=[spec(p[0]) for p in params for _ in range(4)] + [pl.BlockSpec(memory_space=pl.ANY)],
        out_specs=[spec(p[0]) for p in params for _ in range(4)],
        out_shape=[jax.ShapeDtypeStruct(p[0].shape, F32) for p in params for _ in range(4)],
        compiler_params=_params("parallel"),
    )(*[t for p in params for t in p], after)
    return [out[4 * a:4 * a + 4] for a in range(k)]


def _small_update(blocks, chip, gains, gains_m, gains_v, taps, taps_m, taps_v):
    n = len(gains)
    widths = [g.shape[1] for g in gains]
    k, w = taps.shape

    def body(*refs):
        chip_ref, blocks_ref = refs[0], refs[1]
        params = [refs[2 + 3 * i:5 + 3 * i] for i in range(n + 1)]
        outs = [refs[2 + 3 * (n + 1) + 4 * i:2 + 3 * (n + 1) + 4 * (i + 1)] for i in range(n + 1)]
        loss_ref = refs[-1]
        summed = blocks_ref[0]
        for device in range(1, blocks.shape[0]):
            summed = summed + blocks_ref[device]
        for i in range(n):
            g = summed[i:i + 1, 0:widths[i]]
            wr, mr, vr = params[i]
            outs[i][0][...] = g
            outs[i][1][...], outs[i][2][...], outs[i][3][...] = _adamw_update(wr[...], g, mr[...], vr[...])
        g = summed[n:n + k, 0:w]
        for j in range(1, N_CHIPS):
            g = jnp.where(chip_ref[0] == j, summed[n:n + k, w * j:w * (j + 1)], g)
        wr, mr, vr = params[n]
        for out_ref, val in zip(outs[n], (g, *_adamw_update(wr[...], g, mr[...], vr[...]))):
            for j in range(k):
                out_ref[j] = val[j:j + 1, :]
        loss_ref[...] = summed[n + k:n + k + 1, 0:1]

    vmem = pl.BlockSpec(memory_space=pltpu.VMEM)
    operands = [chip, blocks]
    for p in zip(list(gains) + [taps], list(gains_m) + [taps_m], list(gains_v) + [taps_v]):
        operands += list(p)
    shapes = [jax.ShapeDtypeStruct(shape, F32) for shape in [g.shape for g in gains] + [(k, 1, w)] for _ in range(4)]
    out = pl.pallas_call(
        body, name="small_update", out_shape=shapes + [jax.ShapeDtypeStruct((1, 1), F32)],
        in_specs=[pl.BlockSpec(memory_space=pltpu.SMEM)] + [vmem] * (len(operands) - 1),
        out_specs=[vmem] * (len(shapes) + 1),
    )(*operands)
    return [out[4 * i:4 * (i + 1)] for i in range(n + 1)], out[-1]


def _place():
    return lax.axis_index("x"), lax.axis_index("y"), lax.axis_index("c")


def _other_chips(x, y):
    return [(1 - x, y), (x, 1 - y), (1 - x, 1 - y)]


def _allgather_finish(name, shards, landed, pass_on):
    n = len(shards)

    def body(*refs):
        ins, outs, stage = refs[:n], refs[2 * n:3 * n], refs[3 * n:4 * n]
        send_sems, recv_sems, local_sems = refs[4 * n:]
        x, y, c = _place()
        chips = _other_chips(x, y)
        barrier = pltpu.get_barrier_semaphore()
        pl.semaphore_signal(barrier, inc=1, device_id=(x, y, 1 - c), device_id_type=MESH)
        pl.semaphore_wait(barrier, 1)

        def copy(a, k, chip, half):
            place = outs[a].at[2 * chip[0] + chip[1], half]
            return pltpu.make_async_remote_copy(
                src_ref=place, dst_ref=place, send_sem=send_sems.at[3 * a + k], recv_sem=recv_sems.at[3 * a + k],
                device_id=(x, y, 1 - c), device_id_type=MESH)

        load = [pltpu.make_async_copy(ins[a], stage[a], local_sems.at[a]) for a in range(n)]
        local = [pltpu.make_async_copy(stage[a], outs[a].at[2 * x + y], local_sems.at[a]) for a in range(n)]
        for cp in load:
            cp.start()
        passed = [copy(a, k, chip, c) for a in range(n) if pass_on[a] for k, chip in enumerate(chips)]
        for cp in passed:
            cp.start()
        for a in range(n):
            load[a].wait()
            local[a].start()
        for a in range(n):
            if pass_on[a]:
                for k, chip in enumerate(chips):
                    copy(a, k, chip, 1 - c).wait_recv()
        for cp in passed:
            cp.wait_send()
        for cp in local:
            cp.wait()

    any_spec = pl.BlockSpec(memory_space=pl.ANY)
    return pl.pallas_call(
        body, name=name,
        out_shape=[jax.ShapeDtypeStruct((N_CHIPS,) + s.shape, s.dtype) for s in shards],
        in_specs=[any_spec] * (2 * n), out_specs=[any_spec] * n,
        input_output_aliases={n + a: a for a in range(n)},
        scratch_shapes=[pltpu.VMEM(s.shape, s.dtype) for s in shards]
        + [pltpu.SemaphoreType.DMA((3 * n,)), pltpu.SemaphoreType.DMA((3 * n,)), pltpu.SemaphoreType.DMA((n,))],
        compiler_params=pltpu.CompilerParams(vmem_limit_bytes=VMEM_LIMIT_V7X, collective_id=HANDSHAKES["sibling"][0]),
    )(*shards, *landed)


def _plan_first_hop(x, y, c, shards, lands):
    return [(shards[a].at[c], lands[a].at[2 * x + y, c], lands[a].at[2 * chip[0] + chip[1], c], (*chip, c))
            for a in range(len(shards)) for chip in _other_chips(x, y)]


def _plan_pass_on(x, y, c, nothing, lands):
    def place(a, chip, half):
        return lands[a].at[2 * chip[0] + chip[1], half]

    return [(place(a, chip, c), place(a, chip, c), place(a, chip, 1 - c), (x, y, 1 - c))
            for a in range(len(lands)) for chip in _other_chips(x, y)]


def _plan_own_half_to_sibling(x, y, c, nothing, lands):
    return [(lands[a].at[c], lands[a].at[c], lands[a].at[1 - c], (x, y, 1 - c)) for a in range(len(lands))]


def _plan_other_half_to_sibling(x, y, c, grads, lands):
    return [(grads[a].at[1 - c], lands[a], lands[a], (x, y, 1 - c)) for a in range(len(grads))]


def _plan_to_other_chips(x, y, c, partials, lands):
    return [(partials[a].at[2 * chip[0] + chip[1]], lands[a].at[k], lands[a].at[k], (*chip, c))
            for a in range(len(partials)) for k, chip in enumerate(_other_chips(x, y))]


def _plan_to_all(x, y, c, blocks, lands):
    flips = [(fx, fy, fc) for fx in (0, 1) for fy in (0, 1) for fc in (0, 1) if (fx, fy, fc) != (0, 0, 0)]
    peers = [(1 - x if fx else x, 1 - y if fy else y, 1 - c if fc else c) for fx, fy, fc in flips]
    return [(blocks[0], lands[0].at[4 * x + 2 * y + c], lands[0].at[4 * p[0] + 2 * p[1] + p[2]], p) for p in peers]


def _planned_copies(plan, srcs, lands, send_sems, recv_sems):
    x, y, c = _place()

    def pair(k, src, there, here, to):
        make = lambda dst: pltpu.make_async_remote_copy(
            src_ref=src, dst_ref=dst, send_sem=send_sems.at[k], recv_sem=recv_sems.at[k], device_id=to, device_id_type=MESH)
        return make(there), make(here)

    return [pair(k, *entry) for k, entry in enumerate(plan(x, y, c, srcs, lands))]


_HBM_SPEC = pl.BlockSpec(memory_space=pltpu.HBM)
_SEM_SPEC = pl.BlockSpec(memory_space=pltpu.SEMAPHORE)


def _hbm(a):
    return pltpu.with_memory_space_constraint(a, pltpu.HBM)


HANDSHAKES = {
    "sibling": (1, lambda x, y, c: [(x, y, 1 - c)]),
}


def _exchange_start(name, plan, n_copies, srcs, land_shapes, after, lands=None, peers=None):
    if lands is None:
        lands = [lax.empty(s.shape, s.dtype) for s in land_shapes]
    land_shapes = lands
    ns, nl = len(srcs), len(land_shapes)
    n_in = ns + nl + 1
    collective_id, peers_of = HANDSHAKES[peers] if peers else (None, None)

    def body(*refs):
        if peers:
            who = peers_of(*_place())
            barrier = pltpu.get_barrier_semaphore()
            for peer in who:
                pl.semaphore_signal(barrier, inc=1, device_id=peer, device_id_type=MESH)
            pl.semaphore_wait(barrier, len(who))
        for send, _ in _planned_copies(plan, refs[:ns], refs[ns:ns + nl], refs[n_in], refs[n_in + 1]):
            send.start()
        refs[-1][...] = jnp.zeros_like(refs[-1])

    out = pl.pallas_call(
        body, name=name,
        out_shape=(pltpu.SemaphoreType.DMA((n_copies,)), pltpu.SemaphoreType.DMA((n_copies,)),
                   *[pltpu.HBM(s.shape, s.dtype) for s in land_shapes], jax.ShapeDtypeStruct((8, 128), F32)),
        in_specs=[_HBM_SPEC] * (ns + nl) + [pl.BlockSpec(memory_space=pl.ANY)],
        out_specs=(_SEM_SPEC, _SEM_SPEC, *[_HBM_SPEC] * nl, pl.BlockSpec(memory_space=pltpu.VMEM)),
        input_output_aliases={ns + i: 2 + i for i in range(nl)},
        compiler_params=pltpu.CompilerParams(has_side_effects=pltpu.SideEffectType.DATAFLOW_SIDE_EFFECTING,
                                             collective_id=collective_id),
    )(*[_hbm(s) for s in srcs], *[_hbm(l) for l in lands], after)
    return out[0], out[1], list(out[2:2 + nl]), out[-1]


def _exchange_wait(name, plan, srcs, started, after):
    send_sems, recv_sems, lands, _ = started
    ns, nl = len(srcs), len(lands)
    after = list(after) if isinstance(after, (list, tuple)) else [after]

    def body(*refs):
        for send, recv in _planned_copies(plan, refs[:ns], refs[ns:ns + nl], refs[ns + nl], refs[ns + nl + 1]):
            send.wait_send()
            recv.wait_recv()

    return pl.pallas_call(
        body, name=name, out_shape=[pltpu.HBM(l.shape, l.dtype) for l in lands],
        in_specs=[_HBM_SPEC] * (ns + nl) + [_SEM_SPEC, _SEM_SPEC] + [pl.BlockSpec(memory_space=pl.ANY)] * len(after),
        out_specs=[_HBM_SPEC] * nl, input_output_aliases={ns + i: i for i in range(nl)},
        compiler_params=pltpu.CompilerParams(has_side_effects=pltpu.SideEffectType.DATAFLOW_SIDE_EFFECTING),
    )(*[_hbm(s) for s in srcs], *lands, send_sems, recv_sems, *after)


def _like(arrays, lead, dtype=None):
    return [jax.ShapeDtypeStruct(tuple(lead) + a.shape[-2:], dtype or a.dtype) for a in arrays]


class _StepExchanges:
    def __init__(self, mats, conv_w):
        x, y, c = _place()
        self.place = jnp.stack([c, 2 * x + y]).astype(jnp.int32)
        shards = [w.astype(BF16).reshape(2, w.shape[0] // 2, w.shape[1]) for w in mats]
        self._in_shard = shards[:1]
        self._in = _exchange_start("w_in_allgather_start", _plan_first_hop, 3, self._in_shard,
                                   _like(self._in_shard, (N_CHIPS, 2)), shards[0])
        self.zero = self._in[3]
        taps = jnp.pad(conv_w, ((0, 8 - conv_w.shape[0]), (0, 128 - conv_w.shape[1])))
        self._rest_shards = shards[1:] + [jnp.stack([taps, jnp.zeros_like(taps)])]
        self._taps_shape = conv_w.shape
        self._groups = {}

    def w_in(self, after):
        landed = _exchange_wait("w_in_allgather_wait", _plan_first_hop, self._in_shard, self._in,
                                list(after) + self._rest_shards)
        (w_in,) = _allgather_finish("w_in_allgather_finish", self._in_shard, landed, [True])
        self._rest = _exchange_start("rest_allgather_start", _plan_first_hop, 3 * len(self._rest_shards),
                                     self._rest_shards, _like(self._rest_shards, (N_CHIPS, 2)), w_in)
        self.zero = self._rest[3]
        return w_in.reshape(N_CHIPS, 2 * w_in.shape[2], w_in.shape[3])

    def rest_weights(self, after):
        landed = _exchange_wait("rest_allgather_wait", _plan_first_hop, self._rest_shards, self._rest, after)
        kv, out, up, down, taps = _allgather_finish("rest_allgather_finish", self._rest_shards, landed,
                                                    [True, True, False, False, True])
        self._up_down = _exchange_start("up_down_pass_on_start", _plan_pass_on, 6, [], None, self.zero, lands=[up, down],
                                        peers="sibling")
        self.zero = self._up_down[3]
        k, w = self._taps_shape
        taps = taps[:, 0, :k, :w].transpose(1, 0, 2).reshape(k, N_CHIPS * w)
        return [g.reshape(N_CHIPS, 2 * g.shape[2], g.shape[3]) for g in (kv, out)], taps

    def up_down(self, after):
        full = _exchange_wait("up_down_pass_on_wait", _plan_pass_on, [], self._up_down, after)
        return [g.reshape(N_CHIPS, 2 * g.shape[2], g.shape[3]) for g in full]

    def send_grads(self, key, grads):
        grads = list(grads)
        started = _exchange_start(f"{key}_grads_to_sibling_start", _plan_other_half_to_sibling, len(grads), grads,
                                  _like(grads, (N_CHIPS,)), self.zero, peers="sibling")
        self._groups[key] = dict(grads=grads, to_sibling=started)
        self.zero = started[3]

    def grads_at_sibling(self, key, after):
        group = self._groups[key]
        grads = group["grads"]
        group["from_sibling"] = _exchange_wait(f"{key}_grads_to_sibling_wait", _plan_other_half_to_sibling, grads,
                                               group["to_sibling"], after)
        group["partials"] = _chip_sums_bf16(f"{key}_chip_sums", grads, group["from_sibling"], self.place)
        group["to_chips"] = _exchange_start(f"{key}_grads_to_chips_start", _plan_to_other_chips, 3 * len(grads),
                                            group["partials"], _like(group["partials"], (3,)), self.zero)
        self.zero = group["to_chips"][3]

    def final_sum_operands(self, key, after):
        group = self._groups[key]
        from_chips = _exchange_wait(f"{key}_grads_to_chips_wait", _plan_to_other_chips, group["partials"],
                                    group["to_chips"], after)
        return group["grads"], group["from_sibling"], from_chips, self.place

    def grads_summed(self, key, after):
        return _final_sums(f"{key}_final_sums", *self.final_sum_operands(key, after))

    def send_sums(self, key, sums):
        self._groups[key + "_sums"] = _exchange_start(f"{key}_sums_to_sibling_start", _plan_own_half_to_sibling,
                                                      len(sums), [], None, self.zero, lands=list(sums),
                                                      peers="sibling")
        self.zero = self._groups[key + "_sums"][3]

    def whole_sums(self, key, after):
        full = _exchange_wait(f"{key}_sums_to_sibling_wait", _plan_own_half_to_sibling, [], self._groups[key + "_sums"], after)
        return [t.reshape(2 * t.shape[1], t.shape[2]) for t in full]

    def send_small(self, block):
        self._small = block
        self._small_started = _exchange_start("small_grads_start", _plan_to_all, 7, [block],
                                              [jax.ShapeDtypeStruct((8,) + block.shape, block.dtype)], self.zero)
        self.zero = self._small_started[3]

    def small_blocks(self, after):
        x, y, c = _place()
        (landed,) = _exchange_wait("small_grads_wait", _plan_to_all, [self._small], self._small_started, after)
        return lax.dynamic_update_index_in_dim(landed, self._small, 4 * x + 2 * y + c, 0)


def _rope_tables(positions):
    half = HEAD // 2
    inv_freq = jnp.float32(ROPE_THETA) ** (-(jnp.arange(half, dtype=F32) * 2.0 / HEAD))
    ang = positions.astype(F32)[:, None] * inv_freq
    cos, sin = jnp.cos(ang), jnp.sin(ang)
    return jnp.tile(cos, (1, 4)), jnp.tile(jnp.concatenate([-sin, sin], axis=1), (1, 2))


def _local_step(x, mem, positions, target, gains, ex):
    g_pre_mix, g_mem, g_a, g_c, g_x, g_post_mix, g_pre_mlp, g_post_mlp = gains
    tm = ROW_TILE
    cos, sin = _rope_tables(positions)
    h = _pre_norm(x, g_pre_mix, ex.zero, tm)
    w_in = ex.w_in([h, cos, sin])

    q, k, v, bcu, qx = _in_proj_fwd(h, w_in, cos, sin, ex.zero, tm)
    ya, lse = _attn_fwd(q, k, v)
    (w_kv, w_out), conv_w = ex.rest_weights(lse)
    w_kv, w_out = (w.reshape(N_CHIPS * w.shape[1], w.shape[2]) for w in (w_kv, w_out))
    memn, mkv = _memkv_fwd(mem, g_mem, w_kv, ex.zero)
    yx, ycat, y2, x1 = _mix_fwd(ya, bcu, qx, mkv, conv_w, g_a, g_c, g_x, w_out, g_post_mix, x, tm)
    w_up, w_down = ex.up_down(x1)
    w_down = w_down.reshape(N_CHIPS * w_down.shape[1], w_down.shape[2])
    h2, f, du, df2, dx1, dg_pre_mlp, dg_post_mlp, loss = _mlp_fwd_bwd(x1, target, g_pre_mlp, g_post_mlp, w_up, w_down,
                                                                      MLP_ROW_TILE)
    gw_down = _weight_grad("grad_w_down", f, df2, True, ex.zero)
    gw_up = _weight_grad("grad_w_up", h2, du, False, ex.zero)
    ex.send_grads("early", [gw_up, gw_down])

    gw_out, dya, delta, tail, dmkv, g_conv, dg_post_mix, dg_a, dg_c, dg_x = _mixer_bwd(
        dx1, y2, ycat, ya, yx, bcu, qx, mkv, conv_w, g_a, g_c, g_x, w_out, g_post_mix, ex.zero, tm)
    ex.grads_at_sibling("early", dya)
    gw_kv, dg_mem = _memkv_bwd(mem, g_mem, w_kv, dmkv)
    ex.send_grads("mid", [gw_out, gw_kv])
    dqkv = _attn_bwd(q, k, v, dya, lse, delta, ex.zero)
    ex.grads_at_sibling("mid", dqkv[0])
    small = [dg_mem, dg_a, dg_c, dg_x, dg_post_mix, dg_pre_mlp, dg_post_mlp, g_conv, loss]
    grad_x, gw_in, small_block, early_sums = _in_proj_bwd(dqkv, tail, cos, sin, w_in, x, h, g_pre_mix, dx1, ex.zero, tm,
                                                          small, ex.final_sum_operands("early", dqkv[0]))
    ex.send_grads("late", [gw_in])
    ex.send_small(small_block)
    return grad_x, early_sums


def kernel(x, mem, positions, g_pre_mix, g_mem, w_in, w_mem_kv, conv_w, g_attn_out, g_conv_out, g_xattn_out, w_out, g_post_mix, g_pre_mlp, w_up, w_down, g_post_mlp, loss_target, m_g_pre_mix, m_g_mem, m_w_in, m_w_mem_kv, m_conv_w, m_g_attn_out, m_g_conv_out, m_g_xattn_out, m_w_out, m_g_post_mix, m_g_pre_mlp, m_w_up, m_w_down, m_g_post_mlp, v_g_pre_mix, v_g_mem, v_w_in, v_w_mem_kv, v_conv_w, v_g_attn_out, v_g_conv_out, v_g_xattn_out, v_w_out, v_g_post_mix, v_g_pre_mlp, v_w_up, v_w_down, v_g_post_mlp):
    chip = 2 * lax.axis_index("x") + lax.axis_index("y")
    gains = [g_pre_mix, g_mem, g_attn_out, g_conv_out, g_xattn_out, g_post_mix, g_pre_mlp, g_post_mlp]
    gains_m = [m_g_pre_mix, m_g_mem, m_g_attn_out, m_g_conv_out, m_g_xattn_out, m_g_post_mix, m_g_pre_mlp, m_g_post_mlp]
    gains_v = [v_g_pre_mix, v_g_mem, v_g_attn_out, v_g_conv_out, v_g_xattn_out, v_g_post_mix, v_g_pre_mlp, v_g_post_mlp]
    mats =[w_in[0], w_mem_kv[0], w_out[0], w_up[0], w_down[0]]
    mats_m = [m_w_in[0], m_w_mem_kv[0], m_w_out[0], m_w_up[0], m_w_down[0]]
    mats_v = [v_w_in[0], v_w_mem_kv[0], v_w_out[0], v_w_up[0], v_w_down[0]]

    ex = _StepExchanges(mats, conv_w[0])
    grad_x, early_sums = _local_step(x[0], mem[0], positions[0], loss_target[0], gains, ex)

    ex.send_sums("four", [*early_sums, *ex.grads_summed("mid", ex.zero)])
    ex.grads_at_sibling("late", ex.zero)
    up_sum, down_sum, out_sum, kv_sum = ex.whole_sums("four", ex.zero)
    params = lambda a, g: (mats[a], g, mats_m[a], mats_v[a])
    new_up, new_down = _adamw("adamw_up_down", [params(3, up_sum), params(4, down_sum)], ex.zero)
    new_out, new_kv = _adamw("adamw_out_kv", [params(2, out_sum), params(1, kv_sum)], ex.zero)

    ex.send_sums("last", ex.grads_summed("late", new_kv[1]))
    small, total = _small_update(ex.small_blocks(ex.zero), chip.reshape(1).astype(jnp.int32), gains, gains_m,
                                 gains_v, conv_w[0], m_conv_w[0], v_conv_w[0])
    (in_sum,) = ex.whole_sums("last", small[0][1])
    (new_in,) = _adamw("adamw_in", [params(0, in_sum)], in_sum)
    mat_new = [new_in, new_kv, new_out, new_up, new_down]

    order = ["g_pre_mix", "g_mem", "w_in", "w_mem_kv", "conv_w", "g_attn_out", "g_conv_out", "g_xattn_out", "w_out",
             "g_post_mix", "g_pre_mlp", "w_up", "w_down", "g_post_mlp"]
    gain_names = ["g_pre_mix", "g_mem", "g_attn_out", "g_conv_out", "g_xattn_out", "g_post_mix", "g_pre_mlp", "g_post_mlp"]
    mat_names = ["w_in", "w_mem_kv", "w_out", "w_up", "w_down"]

    def leaf(kind, name):
        if name in gain_names:
            return small[gain_names.index(name)][kind]
        if name == "conv_w":
            return jnp.swapaxes(small[len(gain_names)][kind], 0, 1)
        return mat_new[mat_names.index(name)][kind][None]

    return (total[0, 0], grad_x[None], *[leaf(kind, name) for kind in range(4) for name in order])
```

```python
import jax
import jax.numpy as jnp
from jax import lax
from jax.experimental import pallas as pl
from jax.experimental.pallas import tpu as pltpu

F32, BF16 = jnp.float32, jnp.bfloat16

D_MODEL = 1024
ATTN_W = 512
CONV_W = 256
XATTN_W = 256
PROJ_W = 3 * ATTN_W + 3 * CONV_W + XATTN_W
D_FF = 4096
HEAD = 64
N_BACK = 128
DILATIONS = (1, 4, 16)
PATTERN_ORDER = DILATIONS[::-1]
ROPE_THETA = 10000.0
EPS = 1e-6
NEG_INF = -1e30
SCALE = HEAD ** -0.5
N_CHIPS = 4
SHARD_IN = PROJ_W // N_CHIPS
SHARD_FF = D_FF // N_CHIPS

ADAM_LR, ADAM_B1, ADAM_B2, ADAM_EPS, ADAM_WD, ADAM_STEP = 0.001, 0.9, 0.999, 1e-08, 0.01, 10

VMEM_LIMIT_V7X = 56 * 1024 * 1024
ROW_TILE = 512
MLP_ROW_TILE = 256
ADAMW_ROW_TILE = 256
SMALL_ROWS = 16

NT = (((1,), (1,)), ((), ()))
TN = (((0,), (0,)), ((), ()))
MESH = pl.DeviceIdType.MESH


def _params(*sem):
    return pltpu.CompilerParams(dimension_semantics=sem, vmem_limit_bytes=VMEM_LIMIT_V7X)


def _resident(shape):
    return pl.BlockSpec(shape, lambda *_: (0,) * len(shape), pipeline_mode=pl.Buffered(1))


def _rows(tm, width):
    return pl.BlockSpec((tm, width), lambda i, *_: (i, 0))


def _rms_hat(x):
    r = lax.rsqrt(jnp.mean(x * x, axis=-1, keepdims=True) + EPS)
    return x * r, r


def _rms_bwd(xhat, r, g, dy):
    gdy = dy * g
    return r * (gdy - xhat * jnp.mean(xhat * gdy, axis=-1, keepdims=True))


def _rope128(t, cos, sin_signed, inverse):
    lane = lax.broadcasted_iota(jnp.int32, t.shape, 1)
    first_half = (lane % HEAD) < (HEAD // 2)
    rot = jnp.where(first_half, pltpu.roll(t, 128 - HEAD // 2, 1), pltpu.roll(t, HEAD // 2, 1))
    return t * cos - rot * sin_signed if inverse else t * cos + rot * sin_signed


def _pre_norm(x, g, after, tm):
    S = x.shape[0]

    def body(x_ref, g_ref, after_ref, h_ref):
        h_ref[...] = (_rms_hat(x_ref[...])[0] * g_ref[...]).astype(BF16)

    return pl.pallas_call(
        body, name="pre_norm", grid=(S // tm,),
        in_specs=[_rows(tm, D_MODEL), _resident((1, D_MODEL)), pl.BlockSpec(memory_space=pl.ANY)],
        out_specs=_rows(tm, D_MODEL), out_shape=jax.ShapeDtypeStruct((S, D_MODEL), BF16),
        compiler_params=_params("parallel"),
    )(x, g, after)


def _side_by_side(w_hbm, w_full, sems):
    width = w_hbm.shape[2]

    @pl.when(pl.program_id(0) == 0)
    def _():
        copies = [pltpu.make_async_copy(w_hbm.at[j], w_full.at[:, pl.ds(width * j, width)], sems.at[j])
                  for j in range(N_CHIPS)]
        for j, cp in enumerate(copies):
            cp.start(priority=j % 2)
        for cp in copies:
            cp.wait()


def _in_proj_fwd(h, w_in, cos, sin, after, tm):
    S = h.shape[0]

    def body(h_ref, w_hbm, cos_ref, sin_ref, after_ref, q_ref, k_ref, v_ref, bcu_ref, qx_ref, proj, w_full, sems):
        _side_by_side(w_hbm, w_full, sems)
        proj[...] = jnp.dot(h_ref[...], w_full[...], preferred_element_type=F32)
        c, s = cos_ref[...], sin_ref[...]
        for j in range(ATTN_W // 128):
            lo = 128 * j
            q_ref[:, lo:lo + 128] = _rope128(proj[:, lo:lo + 128], c, s, False) * SCALE
            k_ref[:, lo:lo + 128] = _rope128(proj[:, ATTN_W + lo:ATTN_W + lo + 128], c, s, False)
        v_ref[...] = proj[:, 2 * ATTN_W:3 * ATTN_W]
        bcu_ref[...] = proj[:, 3 * ATTN_W:3 * ATTN_W + 3 * CONV_W]
        qx_ref[...] = proj[:, 3 * ATTN_W + 3 * CONV_W:PROJ_W].astype(BF16)

    return pl.pallas_call(
        body, name="in_proj_fwd", grid=(S // tm,),
        in_specs=[_rows(tm, D_MODEL), pl.BlockSpec(memory_space=pl.ANY), _rows(tm, 128), _rows(tm, 128),
                  pl.BlockSpec(memory_space=pl.ANY)],
        out_specs=[_rows(tm, ATTN_W), _rows(tm, ATTN_W), _rows(tm, ATTN_W), _rows(tm, 3 * CONV_W), _rows(tm, XATTN_W)],
        out_shape=[jax.ShapeDtypeStruct((S, ATTN_W), F32), jax.ShapeDtypeStruct((S, ATTN_W), F32),
                   jax.ShapeDtypeStruct((S, ATTN_W), F32), jax.ShapeDtypeStruct((S, 3 * CONV_W), F32),
                   jax.ShapeDtypeStruct((S, XATTN_W), BF16)],
        scratch_shapes=[pltpu.VMEM((tm, PROJ_W), F32), pltpu.VMEM((D_MODEL, PROJ_W), BF16),
                        pltpu.SemaphoreType.DMA((N_CHIPS,))],
        compiler_params=_params("arbitrary"),
    )(h, w_in, cos, sin, after)


def _memkv_fwd(mem, g_mem, w_kv, after):
    n_mem = mem.shape[0]

    def body(mem_ref, g_ref, w_ref, after_ref, mn_ref, kv_ref):
        mhat, _ = _rms_hat(mem_ref[...])
        mn = (mhat * g_ref[...]).astype(BF16)
        mn_ref[...] = mn
        kv_ref[...] = jnp.dot(mn, w_ref[...], preferred_element_type=F32).astype(BF16)

    vmem = pl.BlockSpec(memory_space=pltpu.VMEM)
    return pl.pallas_call(
        body, name="memkv_fwd", in_specs=[vmem, vmem, vmem, pl.BlockSpec(memory_space=pl.ANY)], out_specs=[vmem, vmem],
        out_shape=[jax.ShapeDtypeStruct((n_mem, D_MODEL), BF16), jax.ShapeDtypeStruct((n_mem, 2 * XATTN_W), BF16)],
        compiler_params=pltpu.CompilerParams(vmem_limit_bytes=VMEM_LIMIT_V7X),
    )(mem, g_mem, w_kv, after)


def _fill_band_bias(bias):
    row = lax.broadcasted_iota(jnp.int32, (N_BACK, 2 * N_BACK), 0)
    col = lax.broadcasted_iota(jnp.int32, (N_BACK, 2 * N_BACK), 1)
    band = (col >= row) & (col <= row + N_BACK)
    bias[1] = jnp.where(band, 0.0, NEG_INF)
    bias[0] = jnp.where(band & (col >= N_BACK), 0.0, NEG_INF)


def _strided(start, size, d):
    return pl.ds(start, size) if d == 1 else pl.ds(start, size, stride=d)


def _group_starts(g, G, nb, d):
    t0 = g * G
    r, n0 = lax.shift_right_logical(t0, nb.bit_length() - 1), lax.bitwise_and(t0, nb - 1)
    first = r + n0 * (N_BACK * d)
    before = r + jnp.maximum(n0 - 1, 0) * (N_BACK * d)
    starts = [before] + [first + u * (N_BACK * d) for u in range(G)]
    if d == 1:
        starts = [pl.multiple_of(st, N_BACK) for st in starts]
    return starts, n0


def _step_blocks(i, U, nb, d):
    G = min(U, nb)
    whole = G == nb
    row_blocks, blocks = [], []
    for grp in range(U // G):
        starts, n0 = _group_starts(i * (U // G) + grp, G, nb, d)
        base = len(row_blocks)
        if whole:
            row_blocks += [_strided(st, N_BACK, d) for st in starts[1:]]
            blocks += [(base + max(u - 1, 0), base + u, min(u, 1)) for u in range(G)]
        else:
            row_blocks += [_strided(st, N_BACK, d) for st in starts]
            blocks += [(base + u, base + u + 1, jnp.minimum(n0, 1) if u == 0 else 1) for u in range(G)]
    return row_blocks, blocks


def _by_head(a, b):
    lane = lax.broadcasted_iota(jnp.int32, (a.shape[0], 2 * HEAD), 1)
    return jnp.where(lane < HEAD, a, b)


def _head_only(t, hh):
    lane = lax.broadcasted_iota(jnp.int32, t.shape, 1)
    return jnp.where((lane < HEAD) == (hh == 0), t, jnp.zeros_like(t))


def _stack_heads(t):
    return jnp.concatenate([_head_only(t, 0), _head_only(t, 1)], axis=0)


def _head_columns(t):
    return jnp.concatenate([t[:, 0:1], t[:, HEAD:HEAD + 1]], axis=0)


def _unstack(t):
    return _by_head(t[:N_BACK], t[N_BACK:])


def _unstack_columns(t):
    return _by_head(jnp.broadcast_to(t[:N_BACK], (N_BACK, 2 * HEAD)), jnp.broadcast_to(t[N_BACK:], (N_BACK, 2 * HEAD)))


FWD_BLOCKS_PER_STEP = 4
BWD_BLOCKS_PER_STEP = 4
BWD_CHUNK = 64


def _attn_fwd(q, k, v):
    S = q.shape[0]
    U = FWD_BLOCKS_PER_STEP

    def body(q_ref, k_ref, v_ref, y_ref, m_ref, l_scr, bias):
        _fill_band_bias(bias)
        for g, d in enumerate(PATTERN_ORDER):
            nb = S // d // N_BACK
            first_pattern, last_pattern = g == 0, g == len(PATTERN_ORDER) - 1

            def step(i, carry, d=d, nb=nb, first_pattern=first_pattern, last_pattern=last_pattern):
                row_blocks, blocks = _step_blocks(i, U, nb, d)
                kb = [k_ref[r, :].astype(BF16) for r in row_blocks]
                ss = []
                for before, own, which in blocks:
                    kw = jnp.concatenate([kb[before], kb[own]], 0)
                    qs = _stack_heads(q_ref[row_blocks[own], :].astype(BF16))
                    b = bias[which]
                    ss.append(lax.dot_general(qs, kw, NT, preferred_element_type=F32) + jnp.concatenate([b, b], axis=0))
                ms = [jnp.max(s, axis=1, keepdims=True) for s in ss]
                ps = [jnp.exp(s - m) for s, m in zip(ss, ms)]
                ls = [jnp.sum(p, axis=1, keepdims=True) for p in ps]
                vb = [v_ref[r, :].astype(BF16) for r in row_blocks]
                os_ = [jnp.dot(ps[u].astype(BF16), jnp.concatenate([vb[before], vb[own]], 0), preferred_element_type=F32)
                       for u, (before, own, _) in enumerate(blocks)]
                for u, (_, own, _) in enumerate(blocks):
                    o_g, m_g, l_g = _unstack(os_[u]), _unstack_columns(ms[u]), _unstack_columns(ls[u])
                    r = row_blocks[own]
                    if first_pattern:
                        m_new, l_new, acc = m_g, l_g, o_g
                    else:
                        m_old = m_ref[r, :]
                        m_new = jnp.maximum(m_old, m_g)
                        alpha, beta = jnp.exp(m_old - m_new), jnp.exp(m_g - m_new)
                        l_new = l_scr[r, :] * alpha + l_g * beta
                        acc = y_ref[r, :] * alpha + o_g * beta
                    if last_pattern:
                        y_ref[r, :] = acc / l_new
                        m_ref[r, :] = m_new + jnp.log(l_new)
                    else:
                        y_ref[r, :] = acc
                        m_ref[r, :] = m_new
                        l_scr[r, :] = l_new
                return carry

            lax.fori_loop(0, d * nb // U, step, 0)

    col = pl.BlockSpec((S, 2 * HEAD), lambda j: (0, j))
    return pl.pallas_call(
        body, name="attn_fwd", grid=(q.shape[1] // (2 * HEAD),),
        in_specs=[col, col, col], out_specs=[col, col],
        out_shape=[jax.ShapeDtypeStruct(q.shape, F32)] * 2,
        scratch_shapes=[pltpu.VMEM((S, 2 * HEAD), F32), pltpu.VMEM((2, N_BACK, 2 * N_BACK), F32)],
        compiler_params=_params("parallel"),
    )(q, k, v)


def _attn_bwd(q, k, v, dy, lse, delta, after):
    S = q.shape[0]
    U = BWD_BLOCKS_PER_STEP

    def body(q_ref, k_ref, v_ref, dy_ref, lse_ref, delta_ref, after_ref, dq_ref, dk_ref, dv_ref, bias):
        _fill_band_bias(bias)
        nb_first = S // PATTERN_ORDER[0] // N_BACK
        first_writes_all = min(U, nb_first) == nb_first
        if not first_writes_all:
            dk_ref[...] = jnp.zeros_like(dk_ref)
            dv_ref[...] = jnp.zeros_like(dv_ref)
        for g, d in enumerate(PATTERN_ORDER):
            nb = S // d // N_BACK

            def step(i, carry, d=d, nb=nb, g=g):
                row_blocks, blocks = _step_blocks(i, U, nb, d)
                kb = [k_ref[r, :].astype(BF16) for r in row_blocks]
                vb = [v_ref[r, :].astype(BF16) for r in row_blocks]
                kws = [jnp.concatenate([kb[before], kb[own]], 0) for before, own, _ in blocks]
                vws = [jnp.concatenate([vb[before], vb[own]], 0) for before, own, _ in blocks]
                qss = [_stack_heads(q_ref[row_blocks[own], :].astype(BF16)) for _, own, _ in blocks]
                doss = [_stack_heads(dy_ref[row_blocks[own], :].astype(BF16)) for _, own, _ in blocks]
                ss = [lax.dot_general(qss[u], kws[u], NT, preferred_element_type=F32) for u in range(U)]
                dps = [lax.dot_general(doss[u], vws[u], NT, preferred_element_type=F32) for u in range(U)]
                pbs, dss = [], []
                for u, (_, own, which) in enumerate(blocks):
                    lse_c = _head_columns(lse_ref[row_blocks[own], :])
                    delta_c = _head_columns(delta_ref[row_blocks[own], :])
                    p_parts, ds_parts = [], []
                    for r0 in range(0, 2 * N_BACK, BWD_CHUNK):
                        r = slice(r0, r0 + BWD_CHUNK)
                        mask = bias[which, r0 % N_BACK:r0 % N_BACK + BWD_CHUNK, :]
                        p_r = jnp.exp(ss[u][r] + mask - lse_c[r])
                        p_parts.append(p_r.astype(BF16))
                        ds_parts.append((p_r * (dps[u][r] - delta_c[r])).astype(BF16))
                    pbs.append(jnp.concatenate(p_parts, axis=0))
                    dss.append(jnp.concatenate(ds_parts, axis=0))
                dqs = [jnp.dot(dss[u], kws[u], preferred_element_type=F32) for u in range(U)]
                dkws = [lax.dot_general(dss[u], qss[u], TN, preferred_element_type=F32) for u in range(U)]
                dvws = [lax.dot_general(pbs[u], doss[u], TN, preferred_element_type=F32) for u in range(U)]
                dk_parts, dv_parts = [None] * len(row_blocks), [None] * len(row_blocks)
                for u, (before, own, _) in enumerate(blocks):
                    dq = _unstack(dqs[u])
                    if g == 0:
                        dq_ref[row_blocks[own], :] = dq
                    else:
                        dq_ref[row_blocks[own], :] += dq
                    for idx, dkp, dvp in ((before, dkws[u][:N_BACK], dvws[u][:N_BACK]),
                                          (own, dkws[u][N_BACK:], dvws[u][N_BACK:])):
                        dk_parts[idx] = dkp if dk_parts[idx] is None else dk_parts[idx] + dkp
                        dv_parts[idx] = dvp if dv_parts[idx] is None else dv_parts[idx] + dvp
                for idx, r in enumerate(row_blocks):
                    if g == 0 and first_writes_all:
                        dk_ref[r, :] = dk_parts[idx]
                        dv_ref[r, :] = dv_parts[idx]
                    else:
                        dk_ref[r, :] += dk_parts[idx]
                        dv_ref[r, :] += dv_parts[idx]
                return carry

            lax.fori_loop(0, d * nb // U, step, 0)

    col = pl.BlockSpec((S, 2 * HEAD), lambda j: (0, j))
    return pl.pallas_call(
        body, name="attn_bwd", grid=(q.shape[1] // (2 * HEAD),),
        in_specs=[col] * 6 + [pl.BlockSpec(memory_space=pl.ANY)], out_specs=[col] * 3,
        out_shape=[jax.ShapeDtypeStruct(q.shape, F32)] * 3,
        scratch_shapes=[pltpu.VMEM((2, N_BACK, 2 * N_BACK), F32)],
        compiler_params=_params("parallel"),
    )(q, k, v, dy, lse, delta, after)


def _shift_down(z, before, k):
    row = lax.broadcasted_iota(jnp.int32, z.shape, 0)
    out = pltpu.roll(z, k, 0)
    for i in range(k):
        out = jnp.where(row == i, before[8 - k + i:8 - k + i + 1, :], out)
    return out


def _shift_up(z, after, k):
    rows = z.shape[0]
    row = lax.broadcasted_iota(jnp.int32, z.shape, 0)
    out = pltpu.roll(z, rows - k, 0)
    for i in range(k):
        out = jnp.where(row == rows - k + i, after[i:i + 1, :], out)
    return out


def _conv_fwd(bcu, before, is_first, w):
    b, c, u = bcu[:, 0:CONV_W], bcu[:, CONV_W:2 * CONV_W], bcu[:, 2 * CONV_W:3 * CONV_W]
    z = c * u
    zb = jnp.where(is_first, 0.0, before[:, CONV_W:2 * CONV_W] * before[:, 2 * CONV_W:3 * CONV_W])
    z1, z2 = _shift_down(z, zb, 1), _shift_down(z, zb, 2)
    cv = w[0:1, :] * z2 + w[1:2, :] * z1 + w[2:3, :] * z
    return b, c, u, z, z1, z2, cv


def _halo_before(tm, width):
    return pl.BlockSpec((8, width), lambda i: (jnp.maximum(i * (tm // 8) - 1, 0), 0))


def _mix_fwd(ya, bcu, qx, mkv, conv_w, g_a, g_c, g_x, w_out, g_post, x, tm):
    S = x.shape[0]

    def body(ya_ref, bcu_ref, before_ref, qx_ref, mkv_ref, cw_ref, ga_ref, gc_ref, gx_ref,
             wo_ref, gp_ref, x_ref, yx_ref, ycat_ref, y2_ref, x1_ref):
        ya = ya_ref[...]
        b, _, _, _, _, _, cv = _conv_fwd(bcu_ref[...], before_ref[...], pl.program_id(0) == 0, cw_ref[...])
        yc = b * cv

        qxb, mkvb = qx_ref[...], mkv_ref[...]
        heads = [slice(HEAD * hd, HEAD * (hd + 1)) for hd in range(XATTN_W // HEAD)]
        ss = [lax.dot_general(qxb[:, sl], mkvb[:, sl], NT, preferred_element_type=F32) * SCALE for sl in heads]
        ms = [jnp.max(s, axis=1, keepdims=True) for s in ss]
        ps = [jnp.exp(s - m) for s, m in zip(ss, ms)]
        ls = [jnp.sum(p, axis=1, keepdims=True) for p in ps]
        os_ = [jnp.dot(p.astype(BF16), mkvb[:, XATTN_W + sl.start:XATTN_W + sl.stop], preferred_element_type=F32)
               for p, sl in zip(ps, heads)]
        for sl, o, l in zip(heads, os_, ls):
            yx_ref[:, sl] = o / l
        yx = yx_ref[...]

        ycat_ref[:, 0:ATTN_W] = (_rms_hat(ya)[0] * ga_ref[...]).astype(BF16)
        ycat_ref[:, ATTN_W:ATTN_W + CONV_W] = (_rms_hat(yc)[0] * gc_ref[...]).astype(BF16)
        ycat_ref[:, ATTN_W + CONV_W:D_MODEL] = (_rms_hat(yx)[0] * gx_ref[...]).astype(BF16)
        y2 = jnp.dot(ycat_ref[...], wo_ref[...], preferred_element_type=F32)
        y2_ref[...] = y2
        x1_ref[...] = x_ref[...] + _rms_hat(y2)[0] * gp_ref[...]

    n_mem = mkv.shape[0]
    return pl.pallas_call(
        body, name="mix_fwd", grid=(S // tm,),
        in_specs=[_rows(tm, ATTN_W), _rows(tm, 3 * CONV_W), _halo_before(tm, 3 * CONV_W), _rows(tm, XATTN_W),
                  _resident((n_mem, 2 * XATTN_W)), _resident((3, CONV_W)), _resident((1, ATTN_W)),
                  _resident((1, CONV_W)), _resident((1, XATTN_W)), _resident((D_MODEL, D_MODEL)),
                  _resident((1, D_MODEL)), _rows(tm, D_MODEL)],
        out_specs=[_rows(tm, XATTN_W), _rows(tm, D_MODEL), _rows(tm, D_MODEL), _rows(tm, D_MODEL)],
        out_shape=[jax.ShapeDtypeStruct((S, XATTN_W), F32), jax.ShapeDtypeStruct((S, D_MODEL), BF16),
                   jax.ShapeDtypeStruct((S, D_MODEL), F32), jax.ShapeDtypeStruct((S, D_MODEL), F32)],
        compiler_params=_params("parallel"),
    )(ya, bcu, bcu, qx, mkv, conv_w, g_a, g_c, g_x, w_out, g_post, x)


def _mlp_fwd_bwd(x1, target, g_pre, g_post, w_up, w_down, tm):
    S = x1.shape[0]
    n_ff = D_FF // SHARD_FF

    def body(x1_ref, t_ref, gpre_ref, gpost_ref, wup_ref, wdn_ref,
             h2_ref, f_ref, du_ref, df2_ref, dx1_ref, dgpre_ref, dgpost_ref, loss_ref, u_scr):
        @pl.when(pl.program_id(0) == 0)
        def _():
            dgpre_ref[...] = jnp.zeros_like(dgpre_ref)
            dgpost_ref[...] = jnp.zeros_like(dgpost_ref)
            loss_ref[...] = jnp.zeros_like(loss_ref)

        x1 = x1_ref[...]
        x1hat, r1 = _rms_hat(x1)
        h2 = (x1hat * gpre_ref[...]).astype(BF16)
        h2_ref[...] = h2
        f2 = jnp.zeros((tm, D_MODEL), F32)
        for j in range(n_ff):
            cols = slice(SHARD_FF * j, SHARD_FF * (j + 1))
            u = jnp.maximum(jnp.dot(h2, wup_ref[j], preferred_element_type=F32), 0.0)
            u_scr[:, cols] = u
            f = (u * u).astype(BF16)
            f_ref[:, cols] = f
            f2 = f2 + jnp.dot(f, wdn_ref[cols, :], preferred_element_type=F32)
        f2hat, r2 = _rms_hat(f2)
        err = x1 + f2hat * gpost_ref[...] - t_ref[...]
        loss_ref[...] += 0.5 * jnp.sum(jnp.mean(err * err, axis=-1, keepdims=True), axis=0, keepdims=True)
        dx2 = err * (1.0 / D_MODEL)
        dgpost_ref[...] += jnp.sum(dx2 * f2hat, axis=0, keepdims=True)
        df2 = _rms_bwd(f2hat, r2, gpost_ref[...], dx2).astype(BF16)
        df2_ref[...] = df2
        dh2 = jnp.zeros((tm, D_MODEL), F32)
        for j in range(n_ff):
            cols = slice(SHARD_FF * j, SHARD_FF * (j + 1))
            df = lax.dot_general(df2, wdn_ref[cols, :], NT, preferred_element_type=F32)
            du = (2.0 * u_scr[:, cols] * df).astype(BF16)
            du_ref[:, cols] = du
            dh2 = dh2 + lax.dot_general(du, wup_ref[j], NT, preferred_element_type=F32)
        dgpre_ref[...] += jnp.sum(dh2 * x1hat, axis=0, keepdims=True)
        dx1_ref[...] = dx2 + _rms_bwd(x1hat, r1, gpre_ref[...], dh2)

    acc = pl.BlockSpec((1, D_MODEL), lambda i: (0, 0))
    return pl.pallas_call(
        body, name="mlp_fwd_bwd", grid=(S // tm,),
        in_specs=[_rows(tm, D_MODEL), _rows(tm, D_MODEL), _resident((1, D_MODEL)), _resident((1, D_MODEL)),
                  _resident((n_ff, D_MODEL, SHARD_FF)), _resident((D_FF, D_MODEL))],
        out_specs=[_rows(tm, D_MODEL), _rows(tm, D_FF), _rows(tm, D_FF), _rows(tm, D_MODEL), _rows(tm, D_MODEL),
                   acc, acc, pl.BlockSpec((1, 1), lambda i: (0, 0))],
        out_shape=[jax.ShapeDtypeStruct((S, D_MODEL), BF16), jax.ShapeDtypeStruct((S, D_FF), BF16),
                   jax.ShapeDtypeStruct((S, D_FF), BF16), jax.ShapeDtypeStruct((S, D_MODEL), BF16),
                   jax.ShapeDtypeStruct((S, D_MODEL), F32), jax.ShapeDtypeStruct((1, D_MODEL), F32),
                   jax.ShapeDtypeStruct((1, D_MODEL), F32), jax.ShapeDtypeStruct((1, 1), F32)],
        scratch_shapes=[pltpu.VMEM((tm, D_FF), F32)],
        compiler_params=_params("arbitrary"),
    )(x1, target, g_pre, g_post, w_up, w_down)


def _weight_grad(name, a, b, rows_sharded, after):
    S, K = a.shape
    N = b.shape[1]
    if rows_sharded:
        tk, tn = K // N_CHIPS, N
        a_spec = pl.BlockSpec((S, tk), lambda j: (0, j))
        b_spec = pl.BlockSpec((S, tn), lambda j: (0, 0), pipeline_mode=pl.Buffered(1))
    else:
        tk, tn = K, N // N_CHIPS
        a_spec = pl.BlockSpec((S, tk), lambda j: (0, 0), pipeline_mode=pl.Buffered(1))
        b_spec = pl.BlockSpec((S, tn), lambda j: (0, j))
    half = tk // 2

    def body(a_ref, b_ref, after_ref, o_ref):
        res = lax.dot_general(a_ref[...], b_ref[...], TN, preferred_element_type=F32)
        o_ref[0, 0] = res[:half]
        o_ref[1, 0] = res[half:]

    return pl.pallas_call(
        body, name=name, grid=(N_CHIPS,), in_specs=[a_spec, b_spec, pl.BlockSpec(memory_space=pl.ANY)],
        out_specs=pl.BlockSpec((2, 1, half, tn), lambda j: (0, j, 0, 0)),
        out_shape=jax.ShapeDtypeStruct((2, N_CHIPS, half, tn), F32),
        compiler_params=_params("parallel"),
    )(a, b, after)


def _mixer_bwd(dx1, y2, ycat, ya, yx, bcu, qx, mkv, conv_w, g_a, g_c, g_x, w_out, g_post, after, tm):
    S = dx1.shape[0]
    n_mem = mkv.shape[0]
    n_tiles = S // tm
    half = D_MODEL // N_CHIPS // 2

    def body(dx1_ref, y2_ref, ycat_ref, ya_ref, yx_ref, bcu_ref, before_ref, qx_ref, mkv_ref, cw_ref, ga_ref, gc_ref,
             gx_ref, wo_ref, gp_ref, after_ref, gwo_ref, dya_ref, delta_ref, tail_ref, dmkv_ref, dcw_ref, dgp_ref,
             dga_ref, dgc_ref, dgx_ref, carry):
        step = pl.program_id(0)
        first_tile = step == n_tiles - 1

        @pl.when(step == 0)
        def _():
            for ref in (gwo_ref, dmkv_ref, dcw_ref, dgp_ref, dga_ref, dgc_ref, dgx_ref, carry):
                ref[...] = jnp.zeros_like(ref)

        dx1 = dx1_ref[...]
        y2hat, r2 = _rms_hat(y2_ref[...])
        dgp_ref[...] += jnp.sum(dx1 * y2hat, axis=0, keepdims=True)
        dy2 = _rms_bwd(y2hat, r2, gp_ref[...], dx1).astype(BF16)
        gwo = lax.dot_general(ycat_ref[...], dy2, TN, preferred_element_type=F32)
        for k in range(2 * N_CHIPS):
            gwo_ref[k % 2, k // 2] += gwo[half * k:half * (k + 1)]
        dycat = lax.dot_general(dy2, wo_ref[...], NT, preferred_element_type=F32)

        d_na = dycat[:, 0:ATTN_W]
        ya = ya_ref[...]
        yahat, ra = _rms_hat(ya)
        dga_ref[...] += jnp.sum(d_na * yahat, axis=0, keepdims=True)
        dya = _rms_bwd(yahat, ra, ga_ref[...], d_na)
        dya_ref[...] = dya
        prod = dya * ya
        hi = prod.astype(BF16)
        lo = (prod - hi.astype(F32)).astype(BF16)
        head_of = lambda axis: lax.shift_right_logical(lax.broadcasted_iota(jnp.int32, (ATTN_W, ATTN_W), axis),
                                                       HEAD.bit_length() - 1)
        ones = jnp.where(head_of(0) == head_of(1), 1.0, 0.0).astype(BF16)
        delta_ref[...] = jnp.dot(hi, ones, preferred_element_type=F32) + jnp.dot(lo, ones, preferred_element_type=F32)

        w = cw_ref[...]
        b, c, u, z, z1, z2, cv = _conv_fwd(bcu_ref[...], before_ref[...], first_tile, w)
        d_nc = dycat[:, ATTN_W:ATTN_W + CONV_W]
        ychat, rc = _rms_hat(b * cv)
        dgc_ref[...] += jnp.sum(d_nc * ychat, axis=0, keepdims=True)
        dyc = _rms_bwd(ychat, rc, gc_ref[...], d_nc)
        dcv = dyc * b
        behind = carry[...]
        dz = w[2:3, :] * dcv + w[1:2, :] * _shift_up(dcv, behind, 1) + w[0:1, :] * _shift_up(dcv, behind, 2)
        carry[...] = dcv[0:8, :]
        dcw_ref[0:1, :] += jnp.sum(dcv * z2, axis=0, keepdims=True)
        dcw_ref[1:2, :] += jnp.sum(dcv * z1, axis=0, keepdims=True)
        dcw_ref[2:3, :] += jnp.sum(dcv * z, axis=0, keepdims=True)
        tail_ref[:, 0:CONV_W] = (dyc * cv).astype(BF16)
        tail_ref[:, CONV_W:2 * CONV_W] = (dz * u).astype(BF16)
        tail_ref[:, 2 * CONV_W:3 * CONV_W] = (dz * c).astype(BF16)

        d_nx = dycat[:, ATTN_W + CONV_W:D_MODEL]
        yxhat, rx = _rms_hat(yx_ref[...])
        dgx_ref[...] += jnp.sum(d_nx * yxhat, axis=0, keepdims=True)
        dyx = _rms_bwd(yxhat, rx, gx_ref[...], d_nx)
        qxb, mkvb = qx_ref[...], mkv_ref[...]
        heads = [slice(HEAD * hd, HEAD * (hd + 1)) for hd in range(XATTN_W // HEAD)]
        values = [slice(XATTN_W + sl.start, XATTN_W + sl.stop) for sl in heads]
        ss = [lax.dot_general(qxb[:, sl], mkvb[:, sl], NT, preferred_element_type=F32) * SCALE for sl in heads]
        es = [jnp.exp(s - jnp.max(s, axis=1, keepdims=True)) for s in ss]
        ps = [e / jnp.sum(e, axis=1, keepdims=True) for e in es]
        dobs = [dyx[:, sl].astype(BF16) for sl in heads]
        dps = [lax.dot_general(dob, mkvb[:, vsl], NT, preferred_element_type=F32) for dob, vsl in zip(dobs, values)]
        dss = [(p * (dp - jnp.sum(p * dp, axis=1, keepdims=True)) * SCALE).astype(BF16) for p, dp in zip(ps, dps)]
        for sl, vsl, p, dob, ds in zip(heads, values, ps, dobs, dss):
            tail_ref[:, 3 * CONV_W + sl.start:3 * CONV_W + sl.stop] = jnp.dot(
                ds, mkvb[:, sl], preferred_element_type=F32).astype(BF16)
            dmkv_ref[:, sl] += lax.dot_general(ds, qxb[:, sl], TN, preferred_element_type=F32)
            dmkv_ref[:, vsl] += lax.dot_general(p.astype(BF16), dob, TN, preferred_element_type=F32)

    rows = lambda width: pl.BlockSpec((tm, width), lambda i: (n_tiles - 1 - i, 0))
    before = pl.BlockSpec((8, 3 * CONV_W), lambda i: (jnp.maximum((n_tiles - 1 - i) * (tm // 8) - 1, 0), 0))
    acc = lambda r, w: pl.BlockSpec((r, w), lambda i: (0, 0))
    return pl.pallas_call(
        body, name="mixer_bwd", grid=(n_tiles,),
        in_specs=[rows(D_MODEL), rows(D_MODEL), rows(D_MODEL), rows(ATTN_W), rows(XATTN_W), rows(3 * CONV_W), before,
                  rows(XATTN_W), _resident((n_mem, 2 * XATTN_W)), _resident((3, CONV_W)), _resident((1, ATTN_W)),
                  _resident((1, CONV_W)), _resident((1, XATTN_W)), _resident((D_MODEL, D_MODEL)),
                  _resident((1, D_MODEL)), pl.BlockSpec(memory_space=pl.ANY)],
        out_specs=[pl.BlockSpec((2, N_CHIPS, half, D_MODEL), lambda i: (0, 0, 0, 0)), rows(ATTN_W), rows(ATTN_W),
                   rows(3 * CONV_W + XATTN_W), acc(n_mem, 2 * XATTN_W),
                   acc(3, CONV_W), acc(1, D_MODEL), acc(1, ATTN_W), acc(1, CONV_W), acc(1, XATTN_W)],
        out_shape=[jax.ShapeDtypeStruct((2, N_CHIPS, half, D_MODEL), F32), jax.ShapeDtypeStruct((S, ATTN_W), F32),
                   jax.ShapeDtypeStruct((S, ATTN_W), F32), jax.ShapeDtypeStruct((S, 3 * CONV_W + XATTN_W), BF16),
                   jax.ShapeDtypeStruct((n_mem, 2 * XATTN_W), F32), jax.ShapeDtypeStruct((3, CONV_W), F32),
                   jax.ShapeDtypeStruct((1, D_MODEL), F32), jax.ShapeDtypeStruct((1, ATTN_W), F32),
                   jax.ShapeDtypeStruct((1, CONV_W), F32), jax.ShapeDtypeStruct((1, XATTN_W), F32)],
        scratch_shapes=[pltpu.VMEM((8, CONV_W), F32)],
        compiler_params=_params("arbitrary"),
    )(dx1, y2, ycat, ya, yx, bcu, bcu, qx, mkv, conv_w, g_a, g_c, g_x, w_out, g_post, after)


def _memkv_bwd(mem, g_mem, w_kv, dmkv):
    n_mem = mem.shape[0]
    half = D_MODEL // N_CHIPS // 2

    def body(mem_ref, g_ref, w_ref, d_ref, dw_ref, dg_ref):
        mhat, _ = _rms_hat(mem_ref[...])
        mn = (mhat * g_ref[...]).astype(BF16)
        d = d_ref[...].astype(BF16)
        for k in range(2 * N_CHIPS):
            dw_ref[k % 2, k // 2] = lax.dot_general(mn[:, half * k:half * (k + 1)], d, TN, preferred_element_type=F32)
        dmn = lax.dot_general(d, w_ref[...], NT, preferred_element_type=F32)
        dg_ref[...] = jnp.sum(dmn * mhat, axis=0, keepdims=True)

    return pl.pallas_call(
        body, name="memkv_bwd",
        out_shape=[jax.ShapeDtypeStruct((2, N_CHIPS, half, 2 * XATTN_W), F32), jax.ShapeDtypeStruct((1, D_MODEL), F32)],
        compiler_params=pltpu.CompilerParams(vmem_limit_bytes=VMEM_LIMIT_V7X),
    )(mem, g_mem, w_kv, dmkv)


def _sum_of_partials(own_ref, sibling_ref, other_refs):
    acc = own_ref[0, 0] + sibling_ref[0]
    for ref in other_refs:
        acc = acc + ref[0].astype(F32)
    return acc


def _in_proj_bwd(dqkv, tail, cos, sin, w_in, x, h, g, dx1, after, tm, small, sums=None):
    S = x.shape[0]
    step_w = 2 * 256
    half = D_MODEL // 2
    n_steps = S // tm
    sum_grads, sum_sibling, sum_others, place = sums if sums is not None else ([], [], [], jnp.zeros((2,), jnp.int32))
    k = len(sum_grads)
    n_small = len(small)

    def body(place_ref, dq_ref, dk_ref, dv_ref, tail_ref, cos_ref, sin_ref, w_hbm, x_ref, h_ref, g_ref, dx1_ref,
             after_ref, *refs):
        sum_refs, small_refs, refs = refs[:5 * k], refs[5 * k:5 * k + n_small], refs[5 * k + n_small:]
        (dx_ref, gw_ref, dg_ref), sum_out_refs, small_ref = refs[:3], refs[3:3 + k], refs[3 + k]
        dproj_ref, w_full, sems = refs[4 + k:]
        for a in range(k):
            sum_out_refs[a][0] = _sum_of_partials(sum_refs[a], sum_refs[k + a], sum_refs[2 * k + 3 * a:2 * k + 3 * a + 3])
        _side_by_side(w_hbm, w_full, sems)

        @pl.when(pl.program_id(0) == 0)
        def _():
            dg_ref[...] = jnp.zeros_like(dg_ref)
            gw_ref[...] = jnp.zeros_like(gw_ref)

        halves = [slice(0, tm // 2), slice(tm // 2, tm)]
        for rows in halves:
            c, s = cos_ref[rows, :], sin_ref[rows, :]
            for j in range(ATTN_W // 128):
                cols = slice(128 * j, 128 * (j + 1))
                dproj_ref[rows, cols] = _rope128(dq_ref[rows, cols] * SCALE, c, s, True).astype(BF16)
                dproj_ref[rows, ATTN_W + 128 * j:ATTN_W + 128 * (j + 1)] = _rope128(dk_ref[rows, cols], c, s, True).astype(BF16)
            dproj_ref[rows, 2 * ATTN_W:3 * ATTN_W] = dv_ref[rows, :].astype(BF16)
            dproj_ref[rows, 3 * ATTN_W:PROJ_W] = tail_ref[rows, :]
        dhs = [lax.dot_general(dproj_ref[rows, :], w_full[...], NT, preferred_element_type=F32) for rows in halves]
        for rows, dh in zip(halves, dhs):
            xhat, r = _rms_hat(x_ref[rows, :])
            dg_ref[...] += jnp.sum(dh * xhat, axis=0, keepdims=True)
            dx_ref[rows, :] = dx1_ref[rows, :] + _rms_bwd(xhat, r, g_ref[...], dh)
        hb = h_ref[...]
        for step in range(PROJ_W // step_w):
            res = lax.dot_general(hb, dproj_ref[:, step * step_w:(step + 1) * step_w], TN, preferred_element_type=F32)
            lo = step * step_w
            while lo < (step + 1) * step_w:
                chip = lo // SHARD_IN
                hi = min((step + 1) * step_w, (chip + 1) * SHARD_IN)
                for hh in range(2):
                    gw_ref[hh, chip, :, lo - chip * SHARD_IN:hi - chip * SHARD_IN] += (
                        res[half * hh:half * (hh + 1), lo - step * step_w:hi - step * step_w])
                lo = hi

        @pl.when(pl.program_id(0) == n_steps - 1)
        def _():
            small_ref[...] = jnp.zeros_like(small_ref)
            small_ref[0:1, :] = dg_ref[...]
            row = 1
            for ref in small_refs:
                small_ref[row:row + ref.shape[0], 0:ref.shape[1]] = ref[...]
                row += ref.shape[0]

    whole = lambda shape: pl.BlockSpec(shape, lambda i, p: (0,) * len(shape))
    slab = lambda t: (1, t.shape[-2] // n_steps, t.shape[-1])
    sum_specs = ([pl.BlockSpec((1,) + slab(t), lambda i, p: (p[0], p[1], i, 0)) for t in sum_grads]
                 + [pl.BlockSpec(slab(t), lambda i, p: (p[1], i, 0)) for t in sum_grads]
                 + [pl.BlockSpec(slab(t), lambda i, p, j=j: (j, i, 0)) for t in sum_grads for j in range(3)])
    results = pl.pallas_call(
        body, name="in_proj_bwd",
        out_shape=[jax.ShapeDtypeStruct((S, D_MODEL), F32), jax.ShapeDtypeStruct((2, N_CHIPS, half, SHARD_IN), F32),
                   jax.ShapeDtypeStruct((1, D_MODEL), F32)]
        + [jax.ShapeDtypeStruct((2,) + t.shape[2:], F32) for t in sum_grads]
        + [jax.ShapeDtypeStruct((SMALL_ROWS, D_MODEL), F32)],
        grid_spec=pltpu.PrefetchScalarGridSpec(
            num_scalar_prefetch=1, grid=(n_steps,),
            in_specs=[_rows(tm, ATTN_W)] * 3 + [_rows(tm, PROJ_W - 3 * ATTN_W), _rows(tm, 128), _rows(tm, 128),
                      pl.BlockSpec(memory_space=pl.ANY), _rows(tm, D_MODEL), _rows(tm, D_MODEL),
                      _resident((1, D_MODEL)), _rows(tm, D_MODEL), pl.BlockSpec(memory_space=pl.ANY)] + sum_specs
            + [_resident(t.shape) for t in small],
            out_specs=[_rows(tm, D_MODEL), whole((2, N_CHIPS, half, SHARD_IN)), whole((1, D_MODEL))]
            + [pl.BlockSpec(slab(t), lambda i, p: (p[0], i, 0)) for t in sum_grads] + [whole((SMALL_ROWS, D_MODEL))],
            scratch_shapes=[pltpu.VMEM((tm, PROJ_W), BF16), pltpu.VMEM((D_MODEL, PROJ_W), BF16),
                            pltpu.SemaphoreType.DMA((N_CHIPS,))]),
        compiler_params=_params("arbitrary"),
    )(place, *dqkv, tail, cos, sin, w_in, x, h, g, dx1, after, *sum_grads, *sum_sibling,
      *[o for o in sum_others for _ in range(3)], *small)
    return [results[0], results[1], results[-1], list(results[3:-1])]


def _row_tile(rows):
    return ROW_TILE if rows % ROW_TILE == 0 else rows


def _chip_sums_bf16(name, grads, from_sibling, place):
    k = len(grads)
    _, n, rows, _ = grads[0].shape
    tr = _row_tile(rows)

    def body(place_ref, *refs):
        for g_ref, b_ref, o_ref in zip(refs[:k], refs[k:2 * k], refs[2 * k:]):
            o_ref[...] = (g_ref[0] + b_ref[...]).astype(BF16)

    mine = lambda g: pl.BlockSpec((1, 1, tr, g.shape[3]), lambda s, i, p: (p[0], s, i, 0))
    slab = lambda g: pl.BlockSpec((1, tr, g.shape[3]), lambda s, i, p: (s, i, 0))
    return pl.pallas_call(
        body, name=name, out_shape=[jax.ShapeDtypeStruct(g.shape[1:], BF16) for g in grads],
        grid_spec=pltpu.PrefetchScalarGridSpec(
            num_scalar_prefetch=1, grid=(n, rows // tr),
            in_specs=[mine(g) for g in grads] + [slab(g) for g in grads], out_specs=[slab(g) for g in grads]),
        compiler_params=_params("parallel", "parallel"),
    )(place, *grads, *from_sibling)


def _final_sums(name, grads, from_sibling, others, place):
    k = len(grads)
    rows = grads[0].shape[2]
    tr = _row_tile(rows)

    def body(place_ref, *refs):
        for a in range(k):
            refs[5 * k + a][0] = _sum_of_partials(refs[a], refs[k + a], refs[2 * k + 3 * a:2 * k + 3 * a + 3])

    own = lambda g: pl.BlockSpec((1, 1, tr, g.shape[3]), lambda i, p: (p[0], p[1], i, 0))
    sib = lambda g: pl.BlockSpec((1, tr, g.shape[3]), lambda i, p: (p[1], i, 0))
    other = lambda g, j: pl.BlockSpec((1, tr, g.shape[3]), lambda i, p: (j, i, 0))
    return pl.pallas_call(
        body, name=name, out_shape=[jax.ShapeDtypeStruct((2,) + g.shape[2:], F32) for g in grads],
        grid_spec=pltpu.PrefetchScalarGridSpec(
            num_scalar_prefetch=1, grid=(rows // tr,),
            in_specs=[own(g) for g in grads] + [sib(g) for g in grads] + [other(g, j) for g in grads for j in range(3)],
            out_specs=[pl.BlockSpec((1, tr, g.shape[3]), lambda i, p: (p[0], i, 0)) for g in grads]),
        compiler_params=_params("parallel"),
    )(place, *grads, *from_sibling, *[o for o in others for _ in range(3)])


def _adamw_update(w, g, m, v):
    m = ADAM_B1 * m + (1.0 - ADAM_B1) * g
    v = ADAM_B2 * v + (1.0 - ADAM_B2) * (g * g)
    m_hat = m * (1.0 / (1.0 - ADAM_B1 ** ADAM_STEP))
    v_hat = v * (1.0 / (1.0 - ADAM_B2 ** ADAM_STEP))
    return -ADAM_LR * (m_hat / (jnp.sqrt(v_hat) + ADAM_EPS) + ADAM_WD * w), m, v


def _adamw(name, params, after):
    k = len(params)
    rows = params[0][0].shape[0]
    tr = ADAMW_ROW_TILE if rows % ADAMW_ROW_TILE == 0 else rows

    def body(*refs):
        ins, outs = refs[:4 * k], refs[4 * k + 1:]
        for a in range(k):
            w_ref, g_ref, m_ref, v_ref = ins[4 * a:4 * a + 4]
            g = g_ref[...]
            outs[4 * a][...] = g
            outs[4 * a + 1][...], outs[4 * a + 2][...], outs[4 * a + 3][...] = _adamw_update(w_ref[...], g, m_ref[...], v_ref[...])

    spec = lambda w: pl.BlockSpec((tr, w.shape[1]), lambda i: (i, 0))
    out = pl.pallas_call(
        body, name=name, grid=(rows // tr,),
        in_specs=[spec(p[0]) for p in params for _ in range(4)] + [pl.BlockSpec(memory_space=pl.ANY)],
        out_specs=[spec(p[0]) for p in params for _ in range(4)],
        out_shape=[jax.ShapeDtypeStruct(p[0].shape, F32) for p in params for _ in range(4)],
        compiler_params=_params("parallel"),
    )(*[t for p in params for t in p], after)
    return [out[4 * a:4 * a + 4] for a in range(k)]


def _small_update(blocks, chip, gains, gains_m, gains_v, taps, taps_m, taps_v):
    n = len(gains)
    widths = [g.shape[1] for g in gains]
    k, w = taps.shape

    def body(*refs):
        chip_ref, blocks_ref = refs[0], refs[1]
        params = [refs[2 + 3 * i:5 + 3 * i] for i in range(n + 1)]
        outs = [refs[2 + 3 * (n + 1) + 4 * i:2 + 3 * (n + 1) + 4 * (i + 1)] for i in range(n + 1)]
        loss_ref = refs[-1]
        summed = blocks_ref[0]
        for device in range(1, blocks.shape[0]):
            summed = summed + blocks_ref[device]
        for i in range(n):
            g = summed[i:i + 1, 0:widths[i]]
            wr, mr, vr = params[i]
            outs[i][0][...] = g
            outs[i][1][...], outs[i][2][...], outs[i][3][...] = _adamw_update(wr[...], g, mr[...], vr[...])
        g = summed[n:n + k, 0:w]
        for j in range(1, N_CHIPS):
            g = jnp.where(chip_ref[0] == j, summed[n:n + k, w * j:w * (j + 1)], g)
        wr, mr, vr = params[n]
        for out_ref, val in zip(outs[n], (g, *_adamw_update(wr[...], g, mr[...], vr[...]))):
            for j in range(k):
                out_ref[j] = val[j:j + 1, :]
        loss_ref[...] = summed[n + k:n + k + 1, 0:1]

    vmem = pl.BlockSpec(memory_space=pltpu.VMEM)
    operands = [chip, blocks]
    for p in zip(list(gains) + [taps], list(gains_m) + [taps_m], list(gains_v) + [taps_v]):
        operands += list(p)
    shapes = [jax.ShapeDtypeStruct(shape, F32) for shape in [g.shape for g in gains] + [(k, 1, w)] for _ in range(4)]
    out = pl.pallas_call(
        body, name="small_update", out_shape=shapes + [jax.ShapeDtypeStruct((1, 1), F32)],
        in_specs=[pl.BlockSpec(memory_space=pltpu.SMEM)] + [vmem] * (len(operands) - 1),
        out_specs=[vmem] * (len(shapes) + 1),
    )(*operands)
    return [out[4 * i:4 * (i + 1)] for i in range(n + 1)], out[-1]


def _place():
    return lax.axis_index("x"), lax.axis_index("y"), lax.axis_index("c")


def _other_chips(x, y):
    return [(1 - x, y), (x, 1 - y), (1 - x, 1 - y)]


def _allgather_finish(name, shards, landed, pass_on):
    n = len(shards)

    def body(*refs):
        ins, outs, stage = refs[:n], refs[2 * n:3 * n], refs[3 * n:4 * n]
        send_sems, recv_sems, local_sems = refs[4 * n:]
        x, y, c = _place()
        chips = _other_chips(x, y)
        barrier = pltpu.get_barrier_semaphore()
        pl.semaphore_signal(barrier, inc=1, device_id=(x, y, 1 - c), device_id_type=MESH)
        pl.semaphore_wait(barrier, 1)

        def copy(a, k, chip, half):
            place = outs[a].at[2 * chip[0] + chip[1], half]
            return pltpu.make_async_remote_copy(
                src_ref=place, dst_ref=place, send_sem=send_sems.at[3 * a + k], recv_sem=recv_sems.at[3 * a + k],
                device_id=(x, y, 1 - c), device_id_type=MESH)

        load = [pltpu.make_async_copy(ins[a], stage[a], local_sems.at[a]) for a in range(n)]
        local = [pltpu.make_async_copy(stage[a], outs[a].at[2 * x + y], local_sems.at[a]) for a in range(n)]
        for cp in load:
            cp.start()
        passed = [copy(a, k, chip, c) for a in range(n) if pass_on[a] for k, chip in enumerate(chips)]
        for cp in passed:
            cp.start()
        for a in range(n):
            load[a].wait()
            local[a].start()
        for a in range(n):
            if pass_on[a]:
                for k, chip in enumerate(chips):
                    copy(a, k, chip, 1 - c).wait_recv()
        for cp in passed:
            cp.wait_send()
        for cp in local:
            cp.wait()

    any_spec = pl.BlockSpec(memory_space=pl.ANY)
    return pl.pallas_call(
        body, name=name,
        out_shape=[jax.ShapeDtypeStruct((N_CHIPS,) + s.shape, s.dtype) for s in shards],
        in_specs=[any_spec] * (2 * n), out_specs=[any_spec] * n,
        input_output_aliases={n + a: a for a in range(n)},
        scratch_shapes=[pltpu.VMEM(s.shape, s.dtype) for s in shards]
        + [pltpu.SemaphoreType.DMA((3 * n,)), pltpu.SemaphoreType.DMA((3 * n,)), pltpu.SemaphoreType.DMA((n,))],
        compiler_params=pltpu.CompilerParams(vmem_limit_bytes=VMEM_LIMIT_V7X, collective_id=HANDSHAKES["sibling"][0]),
    )(*shards, *landed)


def _plan_first_hop(x, y, c, shards, lands):
    return [(shards[a].at[c], lands[a].at[2 * x + y, c], lands[a].at[2 * chip[0] + chip[1], c], (*chip, c))
            for a in range(len(shards)) for chip in _other_chips(x, y)]


def _plan_pass_on(x, y, c, nothing, lands):
    def place(a, chip, half):
        return lands[a].at[2 * chip[0] + chip[1], half]

    return [(place(a, chip, c), place(a, chip, c), place(a, chip, 1 - c), (x, y, 1 - c))
            for a in range(len(lands)) for chip in _other_chips(x, y)]


def _plan_own_half_to_sibling(x, y, c, nothing, lands):
    return [(lands[a].at[c], lands[a].at[c], lands[a].at[1 - c], (x, y, 1 - c)) for a in range(len(lands))]


def _plan_other_half_to_sibling(x, y, c, grads, lands):
    return [(grads[a].at[1 - c], lands[a], lands[a], (x, y, 1 - c)) for a in range(len(grads))]


def _plan_to_other_chips(x, y, c, partials, lands):
    return [(partials[a].at[2 * chip[0] + chip[1]], lands[a].at[k], lands[a].at[k], (*chip, c))
            for a in range(len(partials)) for k, chip in enumerate(_other_chips(x, y))]


def _plan_to_all(x, y, c, blocks, lands):
    flips = [(fx, fy, fc) for fx in (0, 1) for fy in (0, 1) for fc in (0, 1) if (fx, fy, fc) != (0, 0, 0)]
    peers = [(1 - x if fx else x, 1 - y if fy else y, 1 - c if fc else c) for fx, fy, fc in flips]
    return [(blocks[0], lands[0].at[4 * x + 2 * y + c], lands[0].at[4 * p[0] + 2 * p[1] + p[2]], p) for p in peers]


def _planned_copies(plan, srcs, lands, send_sems, recv_sems):
    x, y, c = _place()

    def pair(k, src, there, here, to):
        make = lambda dst: pltpu.make_async_remote_copy(
            src_ref=src, dst_ref=dst, send_sem=send_sems.at[k], recv_sem=recv_sems.at[k], device_id=to, device_id_type=MESH)
        return make(there), make(here)

    return [pair(k, *entry) for k, entry in enumerate(plan(x, y, c, srcs, lands))]


_HBM_SPEC = pl.BlockSpec(memory_space=pltpu.HBM)
_SEM_SPEC = pl.BlockSpec(memory_space=pltpu.SEMAPHORE)


def _hbm(a):
    return pltpu.with_memory_space_constraint(a, pltpu.HBM)


HANDSHAKES = {
    "sibling": (1, lambda x, y, c: [(x, y, 1 - c)]),
}


def _exchange_start(name, plan, n_copies, srcs, land_shapes, after, lands=None, peers=None):
    if lands is None:
        lands = [lax.empty(s.shape, s.dtype) for s in land_shapes]
    land_shapes = lands
    ns, nl = len(srcs), len(land_shapes)
    n_in = ns + nl + 1
    collective_id, peers_of = HANDSHAKES[peers] if peers else (None, None)

    def body(*refs):
        if peers:
            who = peers_of(*_place())
            barrier = pltpu.get_barrier_semaphore()
            for peer in who:
                pl.semaphore_signal(barrier, inc=1, device_id=peer, device_id_type=MESH)
            pl.semaphore_wait(barrier, len(who))
        for send, _ in _planned_copies(plan, refs[:ns], refs[ns:ns + nl], refs[n_in], refs[n_in + 1]):
            send.start()
        refs[-1][...] = jnp.zeros_like(refs[-1])

    out = pl.pallas_call(
        body, name=name,
        out_shape=(pltpu.SemaphoreType.DMA((n_copies,)), pltpu.SemaphoreType.DMA((n_copies,)),
                   *[pltpu.HBM(s.shape, s.dtype) for s in land_shapes], jax.ShapeDtypeStruct((8, 128), F32)),
        in_specs=[_HBM_SPEC] * (ns + nl) + [pl.BlockSpec(memory_space=pl.ANY)],
        out_specs=(_SEM_SPEC, _SEM_SPEC, *[_HBM_SPEC] * nl, pl.BlockSpec(memory_space=pltpu.VMEM)),
        input_output_aliases={ns + i: 2 + i for i in range(nl)},
        compiler_params=pltpu.CompilerParams(has_side_effects=pltpu.SideEffectType.DATAFLOW_SIDE_EFFECTING,
                                             collective_id=collective_id),
    )(*[_hbm(s) for s in srcs], *[_hbm(l) for l in lands], after)
    return out[0], out[1], list(out[2:2 + nl]), out[-1]


def _exchange_wait(name, plan, srcs, started, after):
    send_sems, recv_sems, lands, _ = started
    ns, nl = len(srcs), len(lands)
    after = list(after) if isinstance(after, (list, tuple)) else [after]

    def body(*refs):
        for send, recv in _planned_copies(plan, refs[:ns], refs[ns:ns + nl], refs[ns + nl], refs[ns + nl + 1]):
            send.wait_send()
            recv.wait_recv()

    return pl.pallas_call(
        body, name=name, out_shape=[pltpu.HBM(l.shape, l.dtype) for l in lands],
        in_specs=[_HBM_SPEC] * (ns + nl) + [_SEM_SPEC, _SEM_SPEC] + [pl.BlockSpec(memory_space=pl.ANY)] * len(after),
        out_specs=[_HBM_SPEC] * nl, input_output_aliases={ns + i: i for i in range(nl)},
        compiler_params=pltpu.CompilerParams(has_side_effects=pltpu.SideEffectType.DATAFLOW_SIDE_EFFECTING),
    )(*[_hbm(s) for s in srcs], *lands, send_sems, recv_sems, *after)


def _like(arrays, lead, dtype=None):
    return [jax.ShapeDtypeStruct(tuple(lead) + a.shape[-2:], dtype or a.dtype) for a in arrays]


class _StepExchanges:
    def __init__(self, mats, conv_w):
        x, y, c = _place()
        self.place = jnp.stack([c, 2 * x + y]).astype(jnp.int32)
        shards = [w.astype(BF16).reshape(2, w.shape[0] // 2, w.shape[1]) for w in mats]
        self._in_shard = shards[:1]
        self._in = _exchange_start("w_in_allgather_start", _plan_first_hop, 3, self._in_shard,
                                   _like(self._in_shard, (N_CHIPS, 2)), shards[0])
        self.zero = self._in[3]
        taps = jnp.pad(conv_w, ((0, 8 - conv_w.shape[0]), (0, 128 - conv_w.shape[1])))
        self._rest_shards = shards[1:] + [jnp.stack([taps, jnp.zeros_like(taps)])]
        self._taps_shape = conv_w.shape
        self._groups = {}

    def w_in(self, after):
        landed = _exchange_wait("w_in_allgather_wait", _plan_first_hop, self._in_shard, self._in,
                                list(after) + self._rest_shards)
        (w_in,) = _allgather_finish("w_in_allgather_finish", self._in_shard, landed, [True])
        self._rest = _exchange_start("rest_allgather_start", _plan_first_hop, 3 * len(self._rest_shards),
                                     self._rest_shards, _like(self._rest_shards, (N_CHIPS, 2)), w_in)
        self.zero = self._rest[3]
        return w_in.reshape(N_CHIPS, 2 * w_in.shape[2], w_in.shape[3])

    def rest_weights(self, after):
        landed = _exchange_wait("rest_allgather_wait", _plan_first_hop, self._rest_shards, self._rest, after)
        kv, out, up, down, taps = _allgather_finish("rest_allgather_finish", self._rest_shards, landed,
                                                    [True, True, False, False, True])
        self._up_down = _exchange_start("up_down_pass_on_start", _plan_pass_on, 6, [], None, self.zero, lands=[up, down],
                                        peers="sibling")
        self.zero = self._up_down[3]
        k, w = self._taps_shape
        taps = taps[:, 0, :k, :w].transpose(1, 0, 2).reshape(k, N_CHIPS * w)
        return [g.reshape(N_CHIPS, 2 * g.shape[2], g.shape[3]) for g in (kv, out)], taps

    def up_down(self, after):
        full = _exchange_wait("up_down_pass_on_wait", _plan_pass_on, [], self._up_down, after)
        return [g.reshape(N_CHIPS, 2 * g.shape[2], g.shape[3]) for g in full]

    def send_grads(self, key, grads):
        grads = list(grads)
        started = _exchange_start(f"{key}_grads_to_sibling_start", _plan_other_half_to_sibling, len(grads), grads,
                                  _like(grads, (N_CHIPS,)), self.zero, peers="sibling")
        self._groups[key] = dict(grads=grads, to_sibling=started)
        self.zero = started[3]

    def grads_at_sibling(self, key, after):
        group = self._groups[key]
        grads = group["grads"]
        group["from_sibling"] = _exchange_wait(f"{key}_grads_to_sibling_wait", _plan_other_half_to_sibling, grads,
                                               group["to_sibling"], after)
        group["partials"] = _chip_sums_bf16(f"{key}_chip_sums", grads, group["from_sibling"], self.place)
        group["to_chips"] = _exchange_start(f"{key}_grads_to_chips_start", _plan_to_other_chips, 3 * len(grads),
                                            group["partials"], _like(group["partials"], (3,)), self.zero)
        self.zero = group["to_chips"][3]

    def final_sum_operands(self, key, after):
        group = self._groups[key]
        from_chips = _exchange_wait(f"{key}_grads_to_chips_wait", _plan_to_other_chips, group["partials"],
                                    group["to_chips"], after)
        return group["grads"], group["from_sibling"], from_chips, self.place

    def grads_summed(self, key, after):
        return _final_sums(f"{key}_final_sums", *self.final_sum_operands(key, after))

    def send_sums(self, key, sums):
        self._groups[key + "_sums"] = _exchange_start(f"{key}_sums_to_sibling_start", _plan_own_half_to_sibling,
                                                      len(sums), [], None, self.zero, lands=list(sums),
                                                      peers="sibling")
        self.zero = self._groups[key + "_sums"][3]

    def whole_sums(self, key, after):
        full = _exchange_wait(f"{key}_sums_to_sibling_wait", _plan_own_half_to_sibling, [], self._groups[key + "_sums"], after)
        return [t.reshape(2 * t.shape[1], t.shape[2]) for t in full]

    def send_small(self, block):
        self._small = block
        self._small_started = _exchange_start("small_grads_start", _plan_to_all, 7, [block],
                                              [jax.ShapeDtypeStruct((8,) + block.shape, block.dtype)], self.zero)
        self.zero = self._small_started[3]

    def small_blocks(self, after):
        x, y, c = _place()
        (landed,) = _exchange_wait("small_grads_wait", _plan_to_all, [self._small], self._small_started, after)
        return lax.dynamic_update_index_in_dim(landed, self._small, 4 * x + 2 * y + c, 0)


def _rope_tables(positions):
    half = HEAD // 2
    inv_freq = jnp.float32(ROPE_THETA) ** (-(jnp.arange(half, dtype=F32) * 2.0 / HEAD))
    ang = positions.astype(F32)[:, None] * inv_freq
    cos, sin = jnp.cos(ang), jnp.sin(ang)
    return jnp.tile(cos, (1, 4)), jnp.tile(jnp.concatenate([-sin, sin], axis=1), (1, 2))


def _local_step(x, mem, positions, target, gains, ex):
    g_pre_mix, g_mem, g_a, g_c, g_x, g_post_mix, g_pre_mlp, g_post_mlp = gains
    tm = ROW_TILE
    cos, sin = _rope_tables(positions)
    h = _pre_norm(x, g_pre_mix, ex.zero, tm)
    w_in = ex.w_in([h, cos, sin])

    q, k, v, bcu, qx = _in_proj_fwd(h, w_in, cos, sin, ex.zero, tm)
    ya, lse = _attn_fwd(q, k, v)
    (w_kv, w_out), conv_w = ex.rest_weights(lse)
    w_kv, w_out = (w.reshape(N_CHIPS * w.shape[1], w.shape[2]) for w in (w_kv, w_out))
    memn, mkv = _memkv_fwd(mem, g_mem, w_kv, ex.zero)
    yx, ycat, y2, x1 = _mix_fwd(ya, bcu, qx, mkv, conv_w, g_a, g_c, g_x, w_out, g_post_mix, x, tm)
    w_up, w_down = ex.up_down(x1)
    w_down = w_down.reshape(N_CHIPS * w_down.shape[1], w_down.shape[2])
    h2, f, du, df2, dx1, dg_pre_mlp, dg_post_mlp, loss = _mlp_fwd_bwd(x1, target, g_pre_mlp, g_post_mlp, w_up, w_down,
                                                                      MLP_ROW_TILE)
    gw_down = _weight_grad("grad_w_down", f, df2, True, ex.zero)
    gw_up = _weight_grad("grad_w_up", h2, du, False, ex.zero)
    ex.send_grads("early", [gw_up, gw_down])

    gw_out, dya, delta, tail, dmkv, g_conv, dg_post_mix, dg_a, dg_c, dg_x = _mixer_bwd(
        dx1, y2, ycat, ya, yx, bcu, qx, mkv, conv_w, g_a, g_c, g_x, w_out, g_post_mix, ex.zero, tm)
    ex.grads_at_sibling("early", dya)
    gw_kv, dg_mem = _memkv_bwd(mem, g_mem, w_kv, dmkv)
    ex.send_grads("mid", [gw_out, gw_kv])
    dqkv = _attn_bwd(q, k, v, dya, lse, delta, ex.zero)
    ex.grads_at_sibling("mid", dqkv[0])
    small = [dg_mem, dg_a, dg_c, dg_x, dg_post_mix, dg_pre_mlp, dg_post_mlp, g_conv, loss]
    grad_x, gw_in, small_block, early_sums = _in_proj_bwd(dqkv, tail, cos, sin, w_in, x, h, g_pre_mix, dx1, ex.zero, tm,
                                                          small, ex.final_sum_operands("early", dqkv[0]))
    ex.send_grads("late", [gw_in])
    ex.send_small(small_block)
    return grad_x, early_sums


def kernel(x, mem, positions, g_pre_mix, g_mem, w_in, w_mem_kv, conv_w, g_attn_out, g_conv_out, g_xattn_out, w_out, g_post_mix, g_pre_mlp, w_up, w_down, g_post_mlp, loss_target, m_g_pre_mix, m_g_mem, m_w_in, m_w_mem_kv, m_conv_w, m_g_attn_out, m_g_conv_out, m_g_xattn_out, m_w_out, m_g_post_mix, m_g_pre_mlp, m_w_up, m_w_down, m_g_post_mlp, v_g_pre_mix, v_g_mem, v_w_in, v_w_mem_kv, v_conv_w, v_g_attn_out, v_g_conv_out, v_g_xattn_out, v_w_out, v_g_post_mix, v_g_pre_mlp, v_w_up, v_w_down, v_g_post_mlp):
    chip = 2 * lax.axis_index("x") + lax.axis_index("y")
    gains = [g_pre_mix, g_mem, g_attn_out, g_conv_out, g_xattn_out, g_post_mix, g_pre_mlp, g_post_mlp]
    gains_m = [m_g_pre_mix, m_g_mem, m_g_attn_out, m_g_conv_out, m_g_xattn_out, m_g_post_mix, m_g_pre_mlp, m_g_post_mlp]
    gains_v = [v_g_pre_mix, v_g_mem, v_g_attn_out, v_g_conv_out, v_g_xattn_out, v_g_post_mix, v_g_pre_mlp, v_g_post_mlp]
    mats =[w_in[0], w_mem_kv[0], w_out[0], w_up[0], w_down[0]]
    mats_m = [m_w_in[0], m_w_mem_kv[0], m_w_out[0], m_w_up[0], m_w_down[0]]
    mats_v = [v_w_in[0], v_w_mem_kv[0], v_w_out[0], v_w_up[0], v_w_down[0]]

    ex = _StepExchanges(mats, conv_w[0])
    grad_x, early_sums = _local_step(x[0], mem[0], positions[0], loss_target[0], gains, ex)

    ex.send_sums("four", [*early_sums, *ex.grads_summed("mid", ex.zero)])
    ex.grads_at_sibling("late", ex.zero)
    up_sum, down_sum, out_sum, kv_sum = ex.whole_sums("four", ex.zero)
    params = lambda a, g: (mats[a], g, mats_m[a], mats_v[a])
    new_up, new_down = _adamw("adamw_up_down", [params(3, up_sum), params(4, down_sum)], ex.zero)
    new_out, new_kv = _adamw("adamw_out_kv", [params(2, out_sum), params(1, kv_sum)], ex.zero)

    ex.send_sums("last", ex.grads_summed("late", new_kv[1]))
    small, total = _small_update(ex.small_blocks(ex.zero), chip.reshape(1).astype(jnp.int32), gains, gains_m,
                                 gains_v, conv_w[0], m_conv_w[0], v_conv_w[0])
    (in_sum,) = ex.whole_sums("last", small[0][1])
    (new_in,) = _adamw("adamw_in", [params(0, in_sum)], in_sum)
    mat_new = [new_in, new_kv, new_out, new_up, new_down]

    order = ["g_pre_mix", "g_mem", "w_in", "w_mem_kv", "conv_w", "g_attn_out", "g_conv_out", "g_xattn_out", "w_out",
             "g_post_mix", "g_pre_mlp", "w_up", "w_down", "g_post_mlp"]
    gain_names = ["g_pre_mix", "g_mem", "g_attn_out", "g_conv_out", "g_xattn_out", "g_post_mix", "g_pre_mlp", "g_post_mlp"]
    mat_names = ["w_in", "w_mem_kv", "w_out", "w_up", "w_down"]

    def leaf(kind, name):
        if name in gain_names:
            return small[gain_names.index(name)][kind]
        if name == "conv_w":
            return jnp.swapaxes(small[len(gain_names)][kind], 0, 1)
        return mat_new[mat_names.index(name)][kind][None]

    return (total[0, 0], grad_x[None], *[leaf(kind, name) for kind in range(4) for name in order])
```
